```python
import math
import jax, jax.numpy as jnp
from jax import lax
import numpy as np


D_MODEL = 1024
BATCH = 16
SEQ = 2048
DEPTH = 1

HEAD_DIM = 64
FOX_HEADS = 12
DIL_HEADS = 12
MEM_HEADS = 4
MEM_HEAD_DIM = 128
MEM_LEN = 256
FOX_W = FOX_HEADS * HEAD_DIM
DIL_W = DIL_HEADS * HEAD_DIM
MEM_W = MEM_HEADS * MEM_HEAD_DIM
MIX_W = FOX_W + DIL_W + MEM_W
DILATIONS = ((128, 1), (512, 4), (2048, 16))
BLOCK = 128
ROPE_THETA = 500000.0
ROPE_DIM = HEAD_DIM // 4
RMS_EPS = 1e-6
NEG_INF = -1e30
IN_SIZES = [FOX_W] * 4 + [FOX_HEADS] + [DIL_W] * 4 + [MEM_W] * 2
IN_W = sum(IN_SIZES)

kernel_name = 'hymba_fox_dilated_memory_block'


def rmsnorm(x, g):
    xf = x.astype(jnp.float32)
    y = xf * lax.rsqrt(jnp.mean(xf * xf, axis=-1, keepdims=True) + RMS_EPS)
    return (y * g.astype(jnp.float32)).astype(x.dtype)


def rope_partial(t, pos):
    half = ROPE_DIM // 2
    inv_freq = 1.0 / (ROPE_THETA ** (jnp.arange(0, ROPE_DIM, 2, dtype=jnp.float32) / ROPE_DIM))
    ang = pos[:, None] * inv_freq[None, :]
    cos = jnp.cos(ang)[None, :, None, :]
    sin = jnp.sin(ang)[None, :, None, :]
    tr = t[..., :ROPE_DIM].astype(jnp.float32)
    t1, t2 = tr[..., :half], tr[..., half:]
    rot = jnp.concatenate([t1 * cos - t2 * sin, t2 * cos + t1 * sin], axis=-1)
    return jnp.concatenate([rot.astype(t.dtype), t[..., ROPE_DIM:]], axis=-1)


def forgetting_attention(q, k, v, logf):
    B, S, H, E = q.shape
    scale = 1.0 / math.sqrt(E)
    c = jnp.cumsum(logf, axis=1).transpose(0, 2, 1)
    vf = v.astype(jnp.float32)
    outs = []
    for i in range(S // BLOCK):
        q0, q1 = i * BLOCK, (i + 1) * BLOCK
        s = jnp.einsum('bqhe,bkhe->bhqk', q[:, q0:q1], k[:, :q1]).astype(jnp.float32) * scale
        s = s + c[:, :, q0:q1, None] - c[:, :, None, :q1]
        mask = (q0 + jnp.arange(BLOCK))[:, None] >= jnp.arange(q1)[None, :]
        s = jnp.where(mask[None, None], s, NEG_INF)
        p = jax.nn.softmax(s, axis=-1)
        outs.append(jnp.einsum('bhqk,bkhe->bqhe', p, vf[:, :q1]))
    return jnp.concatenate(outs, axis=1)


def dilated_pattern(q, k, v, dilation, n_steps):
    B, S, H, E = q.shape
    L = S // dilation
    nb = -(-L // BLOCK)
    Lp = nb * BLOCK
    scale = 1.0 / math.sqrt(E)

    def to_blocks(t):
        t = t.reshape(B, L, dilation, H, E)
        t = jnp.pad(t, ((0, 0), (0, Lp - L), (0, 0), (0, 0), (0, 0)))
        return t.reshape(B, nb, BLOCK, dilation, H, E)

    def with_prev(t):
        prev = jnp.pad(t, ((0, 0), (1, 0), (0, 0), (0, 0), (0, 0), (0, 0)))[:, :nb]
        return jnp.concatenate([prev, t], axis=2)

    qb = to_blocks(q)
    kc = with_prev(to_blocks(k))
    vc = with_prev(to_blocks(v)).astype(jnp.float32)
    s = jnp.einsum('bnqrhe,bnkrhe->bnrhqk', qb, kc).astype(jnp.float32) * scale
    lq = jnp.arange(nb)[:, None] * BLOCK + jnp.arange(BLOCK)[None, :]
    lk = (jnp.arange(nb)[:, None] - 1) * BLOCK + jnp.arange(2 * BLOCK)[None, :]
    delta = lq[:, :, None] - lk[:, None, :]
    mask = (delta >= 0) & (delta <= n_steps) & (lk[:, None, :] >= 0)
    s = jnp.where(mask[None, :, None, None], s, NEG_INF)
    m = jnp.max(s, axis=-1, keepdims=True)
    e = jnp.exp(s - m)
    den = jnp.sum(e, axis=-1)
    num = jnp.einsum('bnrhqk,bnkrhe->bnqrhe', e, vc)
    num = num.reshape(B, Lp, dilation, H, E)[:, :L].reshape(B, S, H, E)

    def rows(t):
        t = t.transpose(0, 1, 4, 2, 3).reshape(B, Lp, dilation, H)
        return t[:, :L].reshape(B, S, H)

    return num, rows(den), rows(m[..., 0])


def dilated_attention(q, k, v):
    parts = [dilated_pattern(q, k, v, d, w // d) for (w, d) in DILATIONS]
    m_all = parts[0][2]
    for p in parts[1:]:
        m_all = jnp.maximum(m_all, p[2])
    num_tot = 0.0
    den_tot = 0.0
    for num, den, m in parts:
        w = jnp.exp(m - m_all)
        num_tot = num_tot + num * w[..., None]
        den_tot = den_tot + den * w
    return num_tot / den_tot[..., None]


def memory_attention(q, mk, mv):
    scale = 1.0 / math.sqrt(q.shape[-1])
    s = jnp.einsum('bqhe,bkhe->bhqk', q, mk).astype(jnp.float32) * scale
    p = jax.nn.softmax(s, axis=-1)
    return jnp.einsum('bhqk,bkhe->bqhe', p, mv.astype(jnp.float32))


def _fwd_setup_inputs(seed: int = 0) -> dict:
    key = jax.random.key(seed)
    ks = jax.random.split(key, 10)
    f32 = jnp.float32
    x = jax.random.normal(ks[0], (BATCH, SEQ, D_MODEL), f32)
    mem = jax.random.normal(ks[1], (BATCH, MEM_LEN, D_MODEL), f32)
    norm_g = 1.0 + 0.02 * jax.random.normal(ks[2], (DEPTH, D_MODEL), f32)
    w_in = jax.random.normal(ks[3], (DEPTH, D_MODEL, IN_W), f32) * D_MODEL ** -0.5
    b_forget = jax.random.uniform(ks[4], (DEPTH, FOX_HEADS), f32, 1.0, 4.0)
    mem_norm_g = 1.0 + 0.02 * jax.random.normal(ks[5], (DEPTH, D_MODEL), f32)
    w_mem_kv = jax.random.normal(ks[6], (DEPTH, D_MODEL, 2 * MEM_W), f32) * D_MODEL ** -0.5
    w_out = jax.random.normal(ks[7], (DEPTH, MIX_W, D_MODEL), f32) * MIX_W ** -0.5
    final_norm_g = 1.0 + 0.02 * jax.random.normal(ks[8], (D_MODEL,), f32)
    return {'x': x, 'mem': mem, 'norm_g': norm_g, 'w_in': w_in, 'b_forget': b_forget,
            'mem_norm_g': mem_norm_g, 'w_mem_kv': w_mem_kv, 'w_out': w_out,
            'final_norm_g': final_norm_g}


def _fwd_reference(x, mem, norm_g, w_in, b_forget, mem_norm_g, w_mem_kv, w_out, final_norm_g):
    B, S, _ = x.shape
    pos = jnp.arange(S, dtype=jnp.float32)
    split_idx = np.cumsum(IN_SIZES)[:-1].tolist()
    for l in range(DEPTH):
        h = rmsnorm(x, norm_g[l])
        proj = h @ w_in[l]
        (fq, fk, fv, fg, flog, dq, dk, dv, dg, mq, mg) = jnp.split(proj, split_idx, axis=-1)

        logf = jax.nn.log_sigmoid((flog + b_forget[l]).astype(jnp.float32))
        hs = (B, S, FOX_HEADS, HEAD_DIM)
        fox = forgetting_attention(fq.reshape(hs), fk.reshape(hs), fv.reshape(hs), logf)
        fox = fox.reshape(B, S, FOX_W).astype(x.dtype)

        hs = (B, S, DIL_HEADS, HEAD_DIM)
        dqr = rope_partial(dq.reshape(hs), pos)
        dkr = rope_partial(dk.reshape(hs), pos)
        dil = dilated_attention(dqr, dkr, dv.reshape(hs)).reshape(B, S, DIL_W).astype(x.dtype)

        mh = rmsnorm(mem, mem_norm_g[l])
        mk, mv = jnp.split(mh @ w_mem_kv[l], 2, axis=-1)
        ms = (B, mem.shape[1], MEM_HEADS, MEM_HEAD_DIM)
        memo = memory_attention(mq.reshape(B, S, MEM_HEADS, MEM_HEAD_DIM), mk.reshape(ms), mv.reshape(ms))
        memo = memo.reshape(B, S, MEM_W).astype(x.dtype)

        y = jnp.concatenate([fox * jax.nn.silu(fg), dil * jax.nn.silu(dg), memo * jax.nn.silu(mg)], axis=-1)
        x = x + y @ w_out[l]
    return rmsnorm(x, final_norm_g)


import jax as _jax
import jax.numpy as _jnp

TWIN_FORMAT = 'train_step'
FWD_PARAMS = ['x', 'mem', 'norm_g', 'w_in', 'b_forget', 'mem_norm_g', 'w_mem_kv', 'w_out', 'final_norm_g']
TWIN_WEIGHTS = ['norm_g', 'w_in', 'b_forget', 'mem_norm_g', 'w_mem_kv', 'w_out', 'final_norm_g']
TWIN_DIFF_INPUT = 'x'
TWIN_INPUTS = ['x', 'mem', 'norm_g', 'w_in', 'b_forget', 'mem_norm_g', 'w_mem_kv', 'w_out', 'final_norm_g', 'loss_target', 'm_norm_g', 'm_w_in', 'm_b_forget', 'm_mem_norm_g', 'm_w_mem_kv', 'm_w_out', 'm_final_norm_g', 'v_norm_g', 'v_w_in', 'v_b_forget', 'v_mem_norm_g', 'v_w_mem_kv', 'v_w_out', 'v_final_norm_g']
TWIN_OUTPUTS = ['loss', 'grad_x', 'grad_norm_g', 'grad_w_in', 'grad_b_forget', 'grad_mem_norm_g', 'grad_w_mem_kv', 'grad_w_out', 'grad_final_norm_g', 'delta_norm_g', 'delta_w_in', 'delta_b_forget', 'delta_mem_norm_g', 'delta_w_mem_kv', 'delta_w_out', 'delta_final_norm_g', 'new_m_norm_g', 'new_m_w_in', 'new_m_b_forget', 'new_m_mem_norm_g', 'new_m_w_mem_kv', 'new_m_w_out', 'new_m_final_norm_g', 'new_v_norm_g', 'new_v_w_in', 'new_v_b_forget', 'new_v_mem_norm_g', 'new_v_w_mem_kv', 'new_v_w_out', 'new_v_final_norm_g']
TWIN_LEAF_KINDS = {'loss': 'loss', 'grad_x': 'grad_x', 'grad_norm_g': 'grad_w', 'grad_w_in': 'grad_w', 'grad_b_forget': 'grad_w', 'grad_mem_norm_g': 'grad_w', 'grad_w_mem_kv': 'grad_w', 'grad_w_out': 'grad_w', 'grad_final_norm_g': 'grad_w', 'delta_norm_g': 'delta_w', 'delta_w_in': 'delta_w', 'delta_b_forget': 'delta_w', 'delta_mem_norm_g': 'delta_w', 'delta_w_mem_kv': 'delta_w', 'delta_w_out': 'delta_w', 'delta_final_norm_g': 'delta_w', 'new_m_norm_g': 'new_m', 'new_m_w_in': 'new_m', 'new_m_b_forget': 'new_m', 'new_m_mem_norm_g': 'new_m', 'new_m_w_mem_kv': 'new_m', 'new_m_w_out': 'new_m', 'new_m_final_norm_g': 'new_m', 'new_v_norm_g': 'new_v', 'new_v_w_in': 'new_v', 'new_v_b_forget': 'new_v', 'new_v_mem_norm_g': 'new_v', 'new_v_w_mem_kv': 'new_v', 'new_v_w_out': 'new_v', 'new_v_final_norm_g': 'new_v'}


def _forward(args):
    return _fwd_reference(*[args[k] for k in FWD_PARAMS])


def _output_shape():
    out = _jax.eval_shape(lambda: _forward(_fwd_setup_inputs(0)))
    return out.shape, out.dtype

N_MICROBATCH = 1
ADAM_LR = 0.001
ADAM_B1 = 0.9
ADAM_B2 = 0.999
ADAM_EPS = 1e-08
ADAM_WD = 0.01
ADAM_STEP = 10
PER_EXAMPLE_BATCH_AXIS = {'x': 0, 'mem': 0, 'loss_target': 0}
SHARED_INPUTS = []
_WEIGHT_DTYPES = {'norm_g': _jnp.float32, 'w_in': _jnp.float32, 'b_forget': _jnp.float32, 'mem_norm_g': _jnp.float32, 'w_mem_kv': _jnp.float32, 'w_out': _jnp.float32, 'final_norm_g': _jnp.float32}
MOMENT_SCALE = {'norm_g': 5.304995e-02, 'w_in': 2.074066e-02, 'b_forget': 2.075598e-01, 'mem_norm_g': 8.806458e-03, 'w_mem_kv': 7.884067e-03, 'w_out': 3.051722e-02, 'final_norm_g': 3.197048e+01}


def _to_microbatches(a, axis):
    t = _jnp.moveaxis(a, axis, 0)
    t = t.reshape((N_MICROBATCH, t.shape[0] // N_MICROBATCH) + t.shape[1:])
    return _jnp.moveaxis(t, 1, axis + 1)


def setup_inputs(seed: int = 0) -> dict:
    inp = _fwd_setup_inputs(seed)
    key = _jax.random.fold_in(_jax.random.key(seed), 7919)
    shape, _ = _output_shape()
    out = dict(inp)
    out["loss_target"] = _jax.random.normal(_jax.random.fold_in(key, 0), shape, _jnp.float32)
    for i, name in enumerate(TWIN_WEIGHTS):
        w = inp[name].astype(_jnp.float32)
        if MOMENT_SCALE is None:
            s = _jnp.sqrt(_jnp.mean(_jnp.square(w)) + 1e-30)
        else:
            s = MOMENT_SCALE[name]
        km, kv = _jax.random.split(_jax.random.fold_in(key, i + 1))
        out[name] = w
        out["m_" + name] = s * _jax.random.normal(km, w.shape, _jnp.float32)
        out["v_" + name] = (s * s) * _jax.random.uniform(kv, w.shape, _jnp.float32, 0.5, 1.5)
    if N_MICROBATCH > 1:
        for name, axis in PER_EXAMPLE_BATCH_AXIS.items():
            out[name] = _to_microbatches(out[name], axis)
    return {'x': out['x'], 'mem': out['mem'], 'norm_g': out['norm_g'], 'w_in': out['w_in'], 'b_forget': out['b_forget'], 'mem_norm_g': out['mem_norm_g'], 'w_mem_kv': out['w_mem_kv'], 'w_out': out['w_out'], 'final_norm_g': out['final_norm_g'], 'loss_target': out['loss_target'], 'm_norm_g': out['m_norm_g'], 'm_w_in': out['m_w_in'], 'm_b_forget': out['m_b_forget'], 'm_mem_norm_g': out['m_mem_norm_g'], 'm_w_mem_kv': out['m_w_mem_kv'], 'm_w_out': out['m_w_out'], 'm_final_norm_g': out['m_final_norm_g'], 'v_norm_g': out['v_norm_g'], 'v_w_in': out['v_w_in'], 'v_b_forget': out['v_b_forget'], 'v_mem_norm_g': out['v_mem_norm_g'], 'v_w_mem_kv': out['v_w_mem_kv'], 'v_w_out': out['v_w_out'], 'v_final_norm_g': out['v_final_norm_g']}


def _loss(weights, diff, rest, loss_target):
    with _jax.named_scope("forward"):
        args = {**rest, TWIN_DIFF_INPUT: diff, **{k: w.astype(_WEIGHT_DTYPES[k]) for k, w in weights.items()}}
        y = _forward(args)
    with _jax.named_scope("loss_head"):
        err = _jnp.square(y.astype(_jnp.float32) - loss_target)
        return 0.5 * _jnp.sum(_jnp.mean(err, axis=-1)) if err.ndim else 0.5 * err


def _adamw(w, g, m, v):
    m = ADAM_B1 * m + (1.0 - ADAM_B1) * g
    v = ADAM_B2 * v + (1.0 - ADAM_B2) * _jnp.square(g)
    m_hat = m / (1.0 - ADAM_B1 ** ADAM_STEP)
    v_hat = v / (1.0 - ADAM_B2 ** ADAM_STEP)
    delta = -ADAM_LR * (m_hat / (_jnp.sqrt(v_hat) + ADAM_EPS) + ADAM_WD * w)
    return delta, m, v


def reference(x, mem, norm_g, w_in, b_forget, mem_norm_g, w_mem_kv, w_out, final_norm_g, loss_target, m_norm_g, m_w_in, m_b_forget, m_mem_norm_g, m_w_mem_kv, m_w_out, m_final_norm_g, v_norm_g, v_w_in, v_b_forget, v_mem_norm_g, v_w_mem_kv, v_w_out, v_final_norm_g):
    given = dict(x=x, mem=mem, norm_g=norm_g, w_in=w_in, b_forget=b_forget, mem_norm_g=mem_norm_g, w_mem_kv=w_mem_kv, w_out=w_out, final_norm_g=final_norm_g, loss_target=loss_target, m_norm_g=m_norm_g, m_w_in=m_w_in, m_b_forget=m_b_forget, m_mem_norm_g=m_mem_norm_g, m_w_mem_kv=m_w_mem_kv, m_w_out=m_w_out, m_final_norm_g=m_final_norm_g, v_norm_g=v_norm_g, v_w_in=v_w_in, v_b_forget=v_b_forget, v_mem_norm_g=v_mem_norm_g, v_w_mem_kv=v_w_mem_kv, v_w_out=v_w_out, v_final_norm_g=v_final_norm_g)
    weights = {n: given[n] for n in TWIN_WEIGHTS}
    shared = {n: given[n] for n in SHARED_INPUTS}
    per_example = {n: given[n] for n in ['x', 'mem']}
    grad_fn = _jax.value_and_grad(_loss, argnums=(0, 1))

    def one_microbatch(ex, loss_target):
        ex = dict(ex)
        diff = ex.pop(TWIN_DIFF_INPUT)
        return grad_fn(weights, diff, {**shared, **ex}, loss_target)

    if N_MICROBATCH == 1:
        loss, (grad_w, grad_x) = one_microbatch(per_example, given["loss_target"])
    else:
        def body(carry, xs):
            loss_sum, grad_sum = carry
            l_k, (gw_k, gx_k) = one_microbatch(xs[0], xs[1])
            with _jax.named_scope("update"):
                return (loss_sum + l_k, _jax.tree.map(_jnp.add, grad_sum, gw_k)), gx_k

        init = (_jnp.zeros((), _jnp.float32), _jax.tree.map(_jnp.zeros_like, weights))
        (loss, grad_w), grad_x = _jax.lax.scan(body, init, (per_example, given["loss_target"]))
    with _jax.named_scope("update"):
        delta_w, new_m, new_v = {}, {}, {}
        for n in TWIN_WEIGHTS:
            delta_w[n], new_m[n], new_v[n] = _adamw(weights[n], grad_w[n], given["m_" + n], given["v_" + n])
    return (loss, grad_x, *[grad_w[n] for n in TWIN_WEIGHTS], *[delta_w[n] for n in TWIN_WEIGHTS],
            *[new_m[n] for n in TWIN_WEIGHTS], *[new_v[n] for n in TWIN_WEIGHTS])
```

```python
import functools
import math

import jax
import jax.numpy as jnp
from jax import lax
from jax.experimental import pallas as pl
from jax.experimental.pallas import tpu as pltpu

F32 = jnp.float32
BF16 = jnp.bfloat16

D_MODEL = 1024
SEQ = 2048
HEAD_DIM = 64
FOX_HEADS = 12
DIL_HEADS = 12
MEM_HEADS = 4
MEM_HEAD_DIM = 128
MEM_LEN = 256
FOX_W = FOX_HEADS * HEAD_DIM
DIL_W = DIL_HEADS * HEAD_DIM
MEM_W = MEM_HEADS * MEM_HEAD_DIM
MIX_W = FOX_W + DIL_W + MEM_W
DILATIONS = ((128, 1), (512, 4), (2048, 16))
ROPE_THETA = 500000.0
ROPE_DIM = HEAD_DIM // 4
RMS_EPS = 1e-6
NEG_INF = -1e30
IN_SIZES = [FOX_W] * 4 + [FOX_HEADS] + [DIL_W] * 4 + [MEM_W] * 2
IN_W = sum(IN_SIZES)

ADAM_LR = 0.001
ADAM_B1 = 0.9
ADAM_B2 = 0.999
ADAM_EPS = 1e-08
ADAM_WD = 0.01
ADAM_STEP = 10

LANES = 128
N_CHIPS = 4
PW = 7168
PWF = PW + LANES
C_FQ, C_FK, C_FV, C_FG = 0, 768, 1536, 2304
C_DQ, C_DK, C_DV, C_DG = 3072, 3840, 4608, 5376
C_MQ, C_MG = 6144, 6656
VMEM_LIMIT = 48 * 1024 * 1024


def _cparams(**kw):
    return pltpu.CompilerParams(vmem_limit_bytes=VMEM_LIMIT, **kw)


def _matmul(a, b, *, out_dtype, tm, tn, tk, name):
    m, kdim = a.shape
    _, n = b.shape
    nk = kdim // tk
    assert m % tm == 0 and n % tn == 0 and kdim % tk == 0

    def body(a_ref, b_ref, o_ref, acc_ref):
        k = pl.program_id(2)

        @pl.when(k == 0)
        def _():
            acc_ref[...] = jnp.zeros_like(acc_ref)

        acc_ref[...] += jnp.dot(a_ref[...], b_ref[...], preferred_element_type=F32)

        @pl.when(k == nk - 1)
        def _():
            o_ref[...] = acc_ref[...].astype(o_ref.dtype)

    return pl.pallas_call(
        body,
        out_shape=jax.ShapeDtypeStruct((m, n), out_dtype),
        grid=(m // tm, n // tn, nk),
        in_specs=[pl.BlockSpec((tm, tk), lambda i, j, k: (i, k)), pl.BlockSpec((tk, tn), lambda i, j, k: (k, j))],
        out_specs=pl.BlockSpec((tm, tn), lambda i, j, k: (i, j)),
        scratch_shapes=[pltpu.VMEM((tm, tn), F32)],
        compiler_params=_cparams(dimension_semantics=("parallel", "parallel", "arbitrary")),
        name=name,
    )(a, b)


def _rms_fwd(x, g, *, tm, name):
    t, d = x.shape

    def body(x_ref, g_ref, h_ref):
        xv = x_ref[...]
        r = lax.rsqrt(jnp.mean(xv * xv, axis=-1, keepdims=True) + RMS_EPS)
        h_ref[...] = (xv * r * g_ref[...]).astype(h_ref.dtype)

    return pl.pallas_call(
        body,
        out_shape=jax.ShapeDtypeStruct((t, d), BF16),
        grid=(t // tm,),
        in_specs=[pl.BlockSpec((tm, d), lambda i: (i, 0)), pl.BlockSpec((1, d), lambda i: (0, 0))],
        out_specs=pl.BlockSpec((tm, d), lambda i: (i, 0)),
        compiler_params=_cparams(),
        name=name,
    )(x, g)


def _rope_tables():
    half = ROPE_DIM // 2
    pos = jnp.arange(SEQ, dtype=F32)
    inv_freq = 1.0 / (ROPE_THETA ** (jnp.arange(0, ROPE_DIM, 2, dtype=F32) / ROPE_DIM))
    ang = pos[:, None] * inv_freq[None, :]
    cos, sin = jnp.cos(ang), jnp.sin(ang)
    one = jnp.ones((SEQ, HEAD_DIM - ROPE_DIM), F32)
    zero = jnp.zeros((SEQ, HEAD_DIM - ROPE_DIM), F32)
    zh = jnp.zeros((SEQ, half), F32)
    c = jnp.concatenate([cos, cos, one], axis=1)
    s1 = jnp.concatenate([zh, sin, zero], axis=1)
    s2 = jnp.concatenate([-sin, zh, zero], axis=1)
    rep = LANES // HEAD_DIM
    return jnp.tile(c, (1, rep)), jnp.tile(s1, (1, rep)), jnp.tile(s2, (1, rep))


def _rope_apply(t, c, s1, s2, transpose=False):
    n = t.shape[-1]
    rep = n // LANES
    c, s1, s2 = (jnp.tile(u, (1, rep)) for u in (c, s1, s2))
    half = ROPE_DIM // 2
    if not transpose:
        return t * c + pltpu.roll(t, half, 1) * s1 + pltpu.roll(t, n - half, 1) * s2
    return t * c + pltpu.roll(t * s1, n - half, 1) + pltpu.roll(t * s2, half, 1)


def _proj(h, w, tabs, *, tm, tn, name):
    t, d = h.shape
    n = w.shape[1]
    assert C_DQ % tn == 0 and (C_DV - C_DQ) % tn == 0
    rope_lo, rope_hi = C_DQ // tn, C_DV // tn
    s_blocks = SEQ // tm

    def body(h_ref, w_ref, c_ref, s1_ref, s2_ref, o_ref):
        j = pl.program_id(1)
        acc = jnp.dot(h_ref[...], w_ref[...], preferred_element_type=F32)
        is_rope = jnp.logical_and(j >= rope_lo, j < rope_hi)

        @pl.when(is_rope)
        def _():
            o_ref[...] = _rope_apply(acc, c_ref[...], s1_ref[...], s2_ref[...]).astype(o_ref.dtype)

        @pl.when(jnp.logical_not(is_rope))
        def _():
            o_ref[...] = acc.astype(o_ref.dtype)

    tab_spec = pl.BlockSpec((tm, LANES), lambda i, j: (i % s_blocks, 0))
    return pl.pallas_call(
        body,
        out_shape=jax.ShapeDtypeStruct((t, n), BF16),
        grid=(t // tm, n // tn),
        in_specs=[pl.BlockSpec((tm, d), lambda i, j: (i, 0)), pl.BlockSpec((d, tn), lambda i, j: (0, j)),
                  tab_spec, tab_spec, tab_spec],
        out_specs=pl.BlockSpec((tm, tn), lambda i, j: (i, j)),
        compiler_params=_cparams(dimension_semantics=("parallel", "parallel")),
        name=name,
    )(h, w, *tabs)


def _split3(x):
    hi = x.astype(BF16)
    r1 = x - hi.astype(F32)
    mid = r1.astype(BF16)
    lo = (r1 - mid.astype(F32)).astype(BF16)
    return hi, mid, lo


def _dot3(sel, x, sel_is_lhs):
    out = None
    for piece in _split3(x):
        t = jnp.dot(sel, piece, preferred_element_type=F32) if sel_is_lhs else jnp.dot(piece, sel, preferred_element_type=F32)
        out = t if out is None else out + t
    return out


def _head_expand_matrix():
    r = lax.broadcasted_iota(jnp.int32, (LANES, FOX_W), 0)
    c = lax.broadcasted_iota(jnp.int32, (LANES, FOX_W), 1)
    return jnp.where(c // HEAD_DIM == r, 1.0, 0.0).astype(BF16)


def _flog_fwd(flog, bpad, *, nb, ts, name):
    ns = SEQ // ts

    def body(f_ref, b_ref, c_ref, cb_ref, carry_ref):
        s = pl.program_id(1)

        @pl.when(s == 0)
        def _():
            carry_ref[...] = jnp.zeros_like(carry_ref)

        z = f_ref[...] + b_ref[...]
        logf = jnp.minimum(z, 0.0) - jnp.log(1.0 + jnp.exp(-jnp.abs(z)))
        r = lax.broadcasted_iota(jnp.int32, (ts, ts), 0)
        c = lax.broadcasted_iota(jnp.int32, (ts, ts), 1)
        tri = jnp.where(r >= c, 1.0, 0.0).astype(BF16)
        cs = _dot3(tri, logf, True) + carry_ref[0:1, :]
        carry_ref[...] = jnp.broadcast_to(cs[ts - 1:ts, :], carry_ref.shape)
        c_ref[...] = cs
        cb_ref[...] = _dot3(_head_expand_matrix(), cs, False)

    return pl.pallas_call(
        body,
        out_shape=(jax.ShapeDtypeStruct((nb * SEQ, LANES), F32), jax.ShapeDtypeStruct((nb * SEQ, FOX_W), F32)),
        grid=(nb, ns),
        in_specs=[pl.BlockSpec((ts, LANES), lambda b, s: (b * ns + s, 0)), pl.BlockSpec((1, LANES), lambda b, s: (0, 0))],
        out_specs=(pl.BlockSpec((ts, LANES), lambda b, s: (b * ns + s, 0)), pl.BlockSpec((ts, FOX_W), lambda b, s: (b * ns + s, 0))),
        scratch_shapes=[pltpu.VMEM((8, LANES), F32)],
        compiler_params=_cparams(dimension_semantics=("parallel", "arbitrary")),
        name=name,
    )(flog, bpad)


def _flog_bwd(dcol, flog, bpad, *, nb, ts, name):
    ns = SEQ // ts

    def body(d_ref, f_ref, b_ref, o_ref, gb_ref, carry_ref):
        bi = pl.program_id(0)
        s = pl.program_id(1)

        @pl.when(s == 0)
        def _():
            carry_ref[...] = jnp.zeros_like(carry_ref)

        @pl.when(jnp.logical_and(bi == 0, s == 0))
        def _():
            gb_ref[...] = jnp.zeros_like(gb_ref)

        r = lax.broadcasted_iota(jnp.int32, (ts, ts), 0)
        c = lax.broadcasted_iota(jnp.int32, (ts, ts), 1)
        tri = jnp.where(r <= c, 1.0, 0.0).astype(BF16)
        rc = _dot3(tri, d_ref[...], True) + carry_ref[0:1, :]
        carry_ref[...] = jnp.broadcast_to(rc[0:1, :], carry_ref.shape)
        z = f_ref[...] + b_ref[...]
        dz = rc / (1.0 + jnp.exp(z))
        o_ref[...] = dz.astype(o_ref.dtype)
        gb_ref[...] += jnp.broadcast_to(jnp.sum(dz, axis=0, keepdims=True), gb_ref.shape)

    rev = lambda b, s: (b * ns + (ns - 1 - s), 0)
    return pl.pallas_call(
        body,
        out_shape=(jax.ShapeDtypeStruct((nb * SEQ, LANES), BF16), jax.ShapeDtypeStruct((8, LANES), F32)),
        grid=(nb, ns),
        in_specs=[pl.BlockSpec((ts, LANES), rev), pl.BlockSpec((ts, LANES), rev), pl.BlockSpec((1, LANES), lambda b, s: (0, 0))],
        out_specs=(pl.BlockSpec((ts, LANES), rev), pl.BlockSpec((8, LANES), lambda b, s: (0, 0))),
        scratch_shapes=[pltpu.VMEM((8, LANES), F32)],
        compiler_params=_cparams(dimension_semantics=("arbitrary", "arbitrary")),
        name=name,
    )(dcol, flog, bpad)


class _AttnCfg:
    def __init__(self, *, e, tq, tk, lq, lk, causal, window, ncol, qcol, kcol, vcol, split_p=False):
        self.e, self.tq, self.tk, self.lq, self.lk = e, tq, tk, lq, lk
        self.split_p = split_p
        self.causal, self.window = causal, window
        self.ncol, self.qcol, self.kcol, self.vcol = ncol, qcol, kcol, vcol
        self.nh = LANES // e
        self.scale = 1.0 / math.sqrt(e)
        self.nq, self.nk = lq // tq, lk // tk

    def k_range(self, i):
        if not self.causal:
            return 0, self.nk
        hi = ((i + 1) * self.tq - 1) // self.tk + 1
        if self.window is None:
            return 0, hi
        return jnp.maximum((i * self.tq - self.window) // self.tk, 0), hi


def _head_masks(nh):
    lane = lax.broadcasted_iota(jnp.int32, (1, LANES), 1)
    return [None] if nh == 1 else [lane < HEAD_DIM, lane >= HEAD_DIM]


def _sel(mask, a, b):
    return a if mask is None else jnp.where(mask, a, b)


def _scores(cfg, qh, kb, q0, k0, dlt0, bias):
    s = lax.dot_general(qh, kb, (((1,), (1,)), ((), ())), preferred_element_type=F32) * cfg.scale
    if bias is not None:
        s = s + bias
    if cfg.causal:
        d = dlt0 + (q0 - k0)
        if cfg.window is None:
            ok = d >= 0
        else:
            ok = d.astype(jnp.uint32) <= jnp.uint32(cfg.window)
        s = jnp.where(ok, s, NEG_INF)
    return s


def _attn_fwd(cfg, q, k, v, *, out_cols, bias=None, state=None, finalize=True, name):
    g = q.shape[0]
    tq, tk, e, nh = cfg.tq, cfg.tk, cfg.e, cfg.nh

    def body(*refs):
        refs = list(refs)
        q_ref, k_ref, v_ref = refs[:3]
        del refs[:3]
        if bias is not None:
            cb_ref, cr_ref = refs[:2]
            del refs[:2]
        if state is not None:
            ai_ref, mi_ref, li_ref = refs[:3]
            del refs[:3]
        out_refs = refs
        masks = _head_masks(nh)
        dlt0 = lax.broadcasted_iota(jnp.int32, (tq, tk), 0) - lax.broadcasted_iota(jnp.int32, (tq, tk), 1)

        def qbody(i, carry):
            q0 = pl.multiple_of(i * tq, tq)
            rows = pl.ds(q0, tq)
            qb = q_ref[rows, :]
            lo, hi = cfg.k_range(i)
            res = []
            for h in range(nh):
                qh = _sel(masks[h], qb, jnp.zeros_like(qb))
                if state is not None:
                    m0 = mi_ref[rows, h * e:h * e + 1]
                    l0 = li_ref[rows, h * e:h * e + 1]
                    a0 = ai_ref[rows, :]
                else:
                    m0 = jnp.full((tq, 1), NEG_INF, F32)
                    l0 = jnp.zeros((tq, 1), F32)
                    a0 = jnp.zeros((tq, LANES), F32)
                cq = cb_ref[rows, h * e:h * e + 1] if bias is not None else None

                def kbody(jk, c, qh=qh, cq=cq, h=h):
                    m, l, a = c
                    k0 = pl.multiple_of(jk * tk, tk)
                    kb = k_ref[pl.ds(k0, tk), :]
                    vb = v_ref[pl.ds(k0, tk), :]
                    b = (cq - cr_ref[jk, h:h + 1, :]) if bias is not None else None
                    s = _scores(cfg, qh, kb, q0, k0, dlt0, b)
                    m_new = jnp.maximum(m, jnp.max(s, axis=1, keepdims=True))
                    alpha = jnp.exp(m - m_new)
                    p = jnp.exp(s - m_new)
                    l = alpha * l + jnp.sum(p, axis=1, keepdims=True)
                    pb = p.astype(BF16)
                    pv = jnp.dot(pb, vb, preferred_element_type=F32)
                    if cfg.split_p:
                        pv = pv + jnp.dot((p - pb.astype(F32)).astype(BF16), vb, preferred_element_type=F32)
                    a = alpha * a + pv
                    return m_new, l, a

                res.append(lax.fori_loop(lo, hi, kbody, (m0, l0, a0)))
            if nh == 1:
                m, l, a = res[0]
                m, l = jnp.broadcast_to(m, (tq, LANES)), jnp.broadcast_to(l, (tq, LANES))
            else:
                m = jnp.where(masks[0], res[0][0], res[1][0])
                l = jnp.where(masks[0], res[0][1], res[1][1])
                a = jnp.where(masks[0], res[0][2], res[1][2])
            if finalize:
                out_refs[0][rows, :] = a / l
                out_refs[1][rows, :] = m + jnp.log(l)
            else:
                out_refs[0][rows, :] = a
                out_refs[1][rows, :] = m
                out_refs[2][rows, :] = l
            return carry

        lax.fori_loop(0, cfg.nq, qbody, 0)

    qspec = pl.BlockSpec((None, cfg.lq, LANES), lambda b, j: (b, 0, cfg.qcol(j)))
    kspec = pl.BlockSpec((None, cfg.lk, LANES), lambda b, j: (b, 0, cfg.kcol(j)))
    vspec = pl.BlockSpec((None, cfg.lk, LANES), lambda b, j: (b, 0, cfg.vcol(j)))
    ospec = pl.BlockSpec((None, cfg.lq, LANES), lambda b, j: (b, 0, j))
    args, in_specs = [q, k, v], [qspec, kspec, vspec]
    if bias is not None:
        args += list(bias)
        in_specs += [ospec, pl.BlockSpec((None, None, cfg.nk, 8, tk), lambda b, j: (b, j, 0, 0, 0))]
    aliases = {}
    if state is not None:
        aliases = {len(args) + t: t for t in range(3 if not finalize else 2)}
        args += list(state)
        in_specs += [ospec] * 3
    n_out = 2 if finalize else 3
    osd = jax.ShapeDtypeStruct((g, cfg.lq, out_cols), F32)
    return pl.pallas_call(
        body,
        out_shape=(osd,) * n_out,
        grid=(g, cfg.ncol),
        in_specs=in_specs,
        out_specs=(ospec,) * n_out,
        input_output_aliases=aliases,
        compiler_params=_cparams(dimension_semantics=("parallel", "parallel")),
        name=name,
    )(*args)


def _attn_bwd(cfg, q, k, v, do, o, lse, *, out_cols, kv_cols, bias=None, acc=None, name):
    g = q.shape[0]
    tq, tk, e, nh = cfg.tq, cfg.tk, cfg.e, cfg.nh
    t0 = (((0,), (0,)), ((), ()))

    def body(*refs):
        refs = list(refs)
        q_ref, k_ref, v_ref, do_ref, o_ref, lse_ref = refs[:6]
        del refs[:6]
        if bias is not None:
            cb_ref, cr_ref = refs[:2]
            del refs[:2]
        if acc is not None:
            dqi_ref, dki_ref, dvi_ref = refs[:3]
            del refs[:3]
        dq_ref, dk_ref, dv_ref = refs[:3]
        dcr_ref = refs[3] if bias is not None else None
        masks = _head_masks(nh)
        dlt0 = lax.broadcasted_iota(jnp.int32, (tq, tk), 0) - lax.broadcasted_iota(jnp.int32, (tq, tk), 1)
        if acc is not None:
            dq_ref[...] = dqi_ref[...]
            dk_ref[...] = dki_ref[...]
            dv_ref[...] = dvi_ref[...]
        else:
            dq_ref[...] = jnp.zeros_like(dq_ref)
            dk_ref[...] = jnp.zeros_like(dk_ref)
            dv_ref[...] = jnp.zeros_like(dv_ref)
        if dcr_ref is not None:
            dcr_ref[...] = jnp.zeros_like(dcr_ref)

        def qbody(i, carry):
            q0 = pl.multiple_of(i * tq, tq)
            rows = pl.ds(q0, tq)
            qb = q_ref[rows, :]
            dob = do_ref[rows, :].astype(BF16)
            prod = dob.astype(F32) * o_ref[rows, :]
            lo, hi = cfg.k_range(i)
            dqs = []
            for h in range(nh):
                qh = _sel(masks[h], qb, jnp.zeros_like(qb))
                doh = _sel(masks[h], dob, jnp.zeros_like(dob))
                lse_h = lse_ref[rows, h * e:h * e + 1]
                delta = jnp.sum(_sel(masks[h], prod, jnp.zeros_like(prod)), axis=1, keepdims=True)
                cq = cb_ref[rows, h * e:h * e + 1] if bias is not None else None

                def kbody(jk, dq_acc, qh=qh, doh=doh, lse_h=lse_h, delta=delta, cq=cq, h=h):
                    k0 = pl.multiple_of(jk * tk, tk)
                    krows = pl.ds(k0, tk)
                    kb = k_ref[krows, :]
                    vb = v_ref[krows, :]
                    b = (cq - cr_ref[jk, h:h + 1, :]) if bias is not None else None
                    s = _scores(cfg, qh, kb, q0, k0, dlt0, b)
                    p = jnp.exp(s - lse_h)
                    dp = lax.dot_general(doh, vb, (((1,), (1,)), ((), ())), preferred_element_type=F32)
                    ds = p * (dp - delta)
                    if dcr_ref is not None:
                        dcr_ref[jk, h:h + 1, :] += jnp.sum(ds, axis=0, keepdims=True)
                    dsb = (ds * cfg.scale).astype(BF16)
                    dv_ref[krows, :] += lax.dot_general(p.astype(BF16), doh, t0, preferred_element_type=F32)
                    dk_ref[krows, :] += lax.dot_general(dsb, qh, t0, preferred_element_type=F32)
                    return dq_acc + jnp.dot(dsb, kb, preferred_element_type=F32)

                dqs.append(lax.fori_loop(lo, hi, kbody, jnp.zeros((tq, LANES), F32)))
            dq = dqs[0] if nh == 1 else jnp.where(masks[0], dqs[0], dqs[1])
            dq_ref[rows, :] += dq
            return carry

        lax.fori_loop(0, cfg.nq, qbody, 0)

    qspec = pl.BlockSpec((None, cfg.lq, LANES), lambda b, j: (b, 0, cfg.qcol(j)))
    kspec = pl.BlockSpec((None, cfg.lk, LANES), lambda b, j: (b, 0, cfg.kcol(j)))
    vspec = pl.BlockSpec((None, cfg.lk, LANES), lambda b, j: (b, 0, cfg.vcol(j)))
    ospec = pl.BlockSpec((None, cfg.lq, LANES), lambda b, j: (b, 0, j))
    kvspec = pl.BlockSpec((None, cfg.lk, LANES), lambda b, j: (b, 0, j))
    args, in_specs = [q, k, v, do, o, lse], [qspec, kspec, vspec, ospec, ospec, ospec]
    out_shape = [jax.ShapeDtypeStruct((g, cfg.lq, out_cols), F32), jax.ShapeDtypeStruct((g, cfg.lk, kv_cols), F32),
                 jax.ShapeDtypeStruct((g, cfg.lk, kv_cols), F32)]
    out_specs = [ospec, kvspec, kvspec]
    if bias is not None:
        args += list(bias)
        crspec = pl.BlockSpec((None, None, cfg.nk, 8, tk), lambda b, j: (b, j, 0, 0, 0))
        in_specs += [ospec, crspec]
        out_shape.append(jax.ShapeDtypeStruct((g, cfg.ncol, cfg.nk, 8, tk), F32))
        out_specs.append(crspec)
    aliases = {}
    if acc is not None:
        aliases = {len(args) + t: t for t in range(3)}
        args += list(acc)
        in_specs += [ospec, kvspec, kvspec]
    return pl.pallas_call(
        body,
        out_shape=tuple(out_shape),
        grid=(g, cfg.ncol),
        in_specs=in_specs,
        out_specs=tuple(out_specs),
        input_output_aliases=aliases,
        compiler_params=_cparams(dimension_semantics=("parallel", "parallel")),
        name=name,
    )(*args)


PBLK = PW // LANES


def _fox_cfg():
    return _AttnCfg(e=HEAD_DIM, tq=256, tk=256, lq=SEQ, lk=SEQ, causal=True, window=None, ncol=FOX_W // LANES, split_p=True,
                    qcol=lambda j: C_FQ // LANES + j, kcol=lambda j: C_FK // LANES + j, vcol=lambda j: C_FV // LANES + j)


def _dil_cfg(d):
    l = SEQ // d
    per = DIL_W // LANES

    def col(base):
        return lambda j: (j // per) * PBLK + base // LANES + j % per

    return _AttnCfg(e=HEAD_DIM, tq=128, tk=128, lq=l, lk=l, causal=True, window=128, ncol=d * per,
                    qcol=col(C_DQ), kcol=col(C_DK), vcol=col(C_DV))


def _mem_cfg():
    return _AttnCfg(e=MEM_HEAD_DIM, tq=256, tk=MEM_LEN, lq=SEQ, lk=MEM_LEN, causal=False, window=None, ncol=MEM_HEADS,
                    qcol=lambda j: C_MQ // LANES + j, kcol=lambda j: j, vcol=lambda j: MEM_HEADS + j)


N_YBLK = MIX_W // LANES
_GATE_BLK = (C_FG // LANES, C_DG // LANES, C_MG // LANES)
_B1, _B2 = FOX_W // LANES, (FOX_W + DIL_W) // LANES


def _att_specs(tm):
    fspec = pl.BlockSpec((tm, LANES), lambda i, j: (i, jnp.minimum(j, _B1 - 1)))
    dspec = pl.BlockSpec((tm, LANES), lambda i, j: (i, jnp.clip(j - _B1, 0, _B2 - _B1 - 1)))
    mspec = pl.BlockSpec((tm, LANES), lambda i, j: (i, jnp.clip(j - _B2, 0, N_YBLK - _B2 - 1)))

    def gcol(j):
        return jnp.where(j < _B1, _GATE_BLK[0] + j, jnp.where(j < _B2, _GATE_BLK[1] + j - _B1, _GATE_BLK[2] + j - _B2))

    gspec = pl.BlockSpec((tm, LANES), lambda i, j: (i, gcol(j)))
    return fspec, dspec, mspec, gspec


def _pick_att(j, f_ref, d_ref, m_ref):
    return jnp.where(j < _B1, f_ref[...], jnp.where(j < _B2, d_ref[...], m_ref[...]))


def _gate_fwd(fox, dil, memo, p16, *, tm, name):
    t = fox.shape[0]

    def body(f_ref, d_ref, m_ref, g_ref, y_ref):
        j = pl.program_id(1)
        a = _pick_att(j, f_ref, d_ref, m_ref)
        gt = g_ref[...].astype(F32)
        y_ref[...] = (a * gt / (1.0 + jnp.exp(-gt))).astype(y_ref.dtype)

    return pl.pallas_call(
        body,
        out_shape=jax.ShapeDtypeStruct((t, MIX_W), BF16),
        grid=(t // tm, N_YBLK),
        in_specs=list(_att_specs(tm)),
        out_specs=pl.BlockSpec((tm, LANES), lambda i, j: (i, j)),
        compiler_params=_cparams(dimension_semantics=("parallel", "parallel")),
        name=name,
    )(fox, dil, memo, p16)


def _gate_bwd(dy, fox, dil, memo, p16, *, tm, name):
    t = fox.shape[0]

    def body(dy_ref, f_ref, d_ref, m_ref, g_ref, da_ref, dg_ref):
        j = pl.program_id(1)
        a = _pick_att(j, f_ref, d_ref, m_ref)
        gt = g_ref[...].astype(F32)
        sg = 1.0 / (1.0 + jnp.exp(-gt))
        dyv = dy_ref[...]
        da_ref[...] = dyv * gt * sg
        dg_ref[...] = (dyv * a * sg * (1.0 + gt * (1.0 - sg))).astype(dg_ref.dtype)

    yspec = pl.BlockSpec((tm, LANES), lambda i, j: (i, j))
    return pl.pallas_call(
        body,
        out_shape=(jax.ShapeDtypeStruct((t, MIX_W), F32), jax.ShapeDtypeStruct((t, MIX_W), BF16)),
        grid=(t // tm, N_YBLK),
        in_specs=[yspec] + list(_att_specs(tm)),
        out_specs=(yspec, yspec),
        compiler_params=_cparams(dimension_semantics=("parallel", "parallel")),
        name=name,
    )(dy, fox, dil, memo, p16)


def _out_loss(y, wo, x, tgt, gfin, *, tm, name):
    t, d = x.shape
    n_feat = float(d)

    def body(y_ref, w_ref, x_ref, t_ref, g_ref, dx_ref, dxb_ref, st_ref):
        i = pl.program_id(0)

        @pl.when(i == 0)
        def _():
            st_ref[...] = jnp.zeros_like(st_ref)

        x2 = x_ref[...] + jnp.dot(y_ref[...], w_ref[...], preferred_element_type=F32)
        r = lax.rsqrt(jnp.mean(x2 * x2, axis=-1, keepdims=True) + RMS_EPS)
        nrm = x2 * r
        gv = g_ref[...]
        err = nrm * gv - t_ref[...]
        dout = err * (1.0 / n_feat)
        dn = dout * gv
        dx2 = r * (dn - nrm * jnp.mean(dn * nrm, axis=-1, keepdims=True))
        dx_ref[...] = dx2
        dxb_ref[...] = dx2.astype(dxb_ref.dtype)
        st_ref[0:1, :] += jnp.sum(dout * nrm, axis=0, keepdims=True)
        st_ref[1:2, :] += (0.5 / n_feat) * jnp.sum(err * err, axis=0, keepdims=True)

    row = pl.BlockSpec((tm, d), lambda i: (i, 0))
    return pl.pallas_call(
        body,
        out_shape=(jax.ShapeDtypeStruct((t, d), F32), jax.ShapeDtypeStruct((t, d), BF16), jax.ShapeDtypeStruct((8, d), F32)),
        grid=(t // tm,),
        in_specs=[pl.BlockSpec((tm, MIX_W), lambda i: (i, 0)), pl.BlockSpec((MIX_W, d), lambda i: (0, 0)), row, row,
                  pl.BlockSpec((1, d), lambda i: (0, 0))],
        out_specs=(row, row, pl.BlockSpec((8, d), lambda i: (0, 0))),
        compiler_params=_cparams(dimension_semantics=("arbitrary",)),
        name=name,
    )(y, wo, x, tgt, gfin)


def _dh_rms_bwd(dp, wt, x, g, resid, *, tm, tk, name):
    t, d = x.shape
    kdim = dp.shape[1]
    nk = kdim // tk

    def body(*refs):
        if resid is not None:
            dp_ref, w_ref, x_ref, g_ref, r_ref, dx_ref, gg_ref, acc_ref = refs
        else:
            dp_ref, w_ref, x_ref, g_ref, dx_ref, gg_ref, acc_ref = refs
        i = pl.program_id(0)
        k = pl.program_id(1)

        @pl.when(jnp.logical_and(i == 0, k == 0))
        def _():
            gg_ref[...] = jnp.zeros_like(gg_ref)

        @pl.when(k == 0)
        def _():
            acc_ref[...] = jnp.zeros_like(acc_ref)

        acc_ref[...] += jnp.dot(dp_ref[...], w_ref[...], preferred_element_type=F32)

        @pl.when(k == nk - 1)
        def _():
            dh = acc_ref[...]
            xv = x_ref[...]
            r = lax.rsqrt(jnp.mean(xv * xv, axis=-1, keepdims=True) + RMS_EPS)
            nrm = xv * r
            dn = dh * g_ref[...]
            dx = r * (dn - nrm * jnp.mean(dn * nrm, axis=-1, keepdims=True))
            if resid is not None:
                dx = dx + r_ref[...]
            dx_ref[...] = dx
            gg_ref[0:1, :] += jnp.sum(dh * nrm, axis=0, keepdims=True)

    row = pl.BlockSpec((tm, d), lambda i, k: (i, 0))
    in_specs = [pl.BlockSpec((tm, tk), lambda i, k: (i, k)), pl.BlockSpec((tk, d), lambda i, k: (k, 0)), row,
                pl.BlockSpec((1, d), lambda i, k: (0, 0))]
    args = [dp, wt, x, g]
    if resid is not None:
        in_specs.append(row)
        args.append(resid)
    return pl.pallas_call(
        body,
        out_shape=(jax.ShapeDtypeStruct((t, d), F32), jax.ShapeDtypeStruct((8, d), F32)),
        grid=(t // tm, nk),
        in_specs=in_specs,
        out_specs=(row, pl.BlockSpec((8, d), lambda i, k: (0, 0))),
        scratch_shapes=[pltpu.VMEM((tm, d), F32)],
        compiler_params=_cparams(dimension_semantics=("arbitrary", "arbitrary")),
        name=name,
    )(*args)


def _rope_bwd(dq, tabs, *, tm, name):
    t, n = dq.shape
    s_blocks = SEQ // tm

    def body(d_ref, c_ref, s1_ref, s2_ref, o_ref):
        o_ref[...] = _rope_apply(d_ref[...], c_ref[...], s1_ref[...], s2_ref[...], transpose=True).astype(o_ref.dtype)

    tab_spec = pl.BlockSpec((tm, LANES), lambda i: (i % s_blocks, 0))
    return pl.pallas_call(
        body,
        out_shape=jax.ShapeDtypeStruct((t, n), BF16),
        grid=(t // tm,),
        in_specs=[pl.BlockSpec((tm, n), lambda i: (i, 0)), tab_spec, tab_spec, tab_spec],
        out_specs=pl.BlockSpec((tm, n), lambda i: (i, 0)),
        compiler_params=_cparams(),
        name=name,
    )(dq, *tabs)


def _rearrange_w_in(w):
    offs = [0]
    for s in IN_SIZES:
        offs.append(offs[-1] + s)
    pieces = [w[:, offs[i]:offs[i + 1]] for i in range(len(IN_SIZES))]
    flog = jnp.pad(pieces[4], ((0, 0), (0, LANES - FOX_HEADS)))
    return jnp.concatenate(pieces[:4] + pieces[5:] + [flog], axis=1)


def _restore_w_in_cols(g):
    return jnp.concatenate([g[:, :C_DQ], g[:, PW:PW + FOX_HEADS], g[:, C_DQ:PW]], axis=1)


def _local_grads(x, mem, norm_g, w_r, b_forget, mem_norm_g, w_kv, w_o, final_norm_g, tgt):
    nb = x.shape[0]
    t = nb * SEQ
    x2d = x.reshape(t, D_MODEL)
    tgt2d = tgt.reshape(t, D_MODEL)
    tabs = _rope_tables()
    bpad = jnp.pad(b_forget.reshape(1, FOX_HEADS), ((0, 0), (0, LANES - FOX_HEADS)))

    h = _rms_fwd(x2d, norm_g.reshape(1, D_MODEL), tm=512, name="rms_x")
    p16 = _proj(h, w_r[:, :PW], tabs, tm=1024, tn=512, name="proj")
    flog = _matmul(h, w_r[:, PW:], out_dtype=F32, tm=1024, tn=LANES, tk=D_MODEL, name="proj_flog")
    c12, cb = _flog_fwd(flog, bpad, nb=nb, ts=256, name="flog_fwd")

    fcfg = _fox_cfg()
    crow = c12[:, :FOX_HEADS].reshape(nb, fcfg.nk, fcfg.tk, fcfg.ncol, 2).transpose(0, 3, 1, 4, 2)
    crow = jnp.pad(crow, ((0, 0), (0, 0), (0, 0), (0, 6), (0, 0)))
    p3 = p16.reshape(nb, SEQ, PW)
    cb3 = cb.reshape(nb, SEQ, FOX_W)
    fox, fox_lse = _attn_fwd(fcfg, p3, p3, p3, out_cols=FOX_W, bias=(cb3, crow), name="fox_fwd")

    state = None
    for idx, (_, d) in enumerate(DILATIONS):
        cfg = _dil_cfg(d)
        pv = p16.reshape(nb, SEQ // d, d * PW)
        last = idx == len(DILATIONS) - 1
        if state is not None:
            state = tuple(s.reshape(nb, SEQ // d, d * DIL_W) for s in state)
        state = _attn_fwd(cfg, pv, pv, pv, out_cols=d * DIL_W, state=state, finalize=last, name=f"dil{d}_fwd")
    dil, dil_lse = (s.reshape(nb, SEQ, DIL_W) for s in state)

    mh = _rms_fwd(mem.reshape(nb * MEM_LEN, D_MODEL), mem_norm_g.reshape(1, D_MODEL), tm=nb * MEM_LEN, name="rms_mem")
    mkv = _matmul(mh, w_kv, out_dtype=BF16, tm=nb * MEM_LEN, tn=512, tk=D_MODEL, name="mem_kv")
    mkv3 = mkv.reshape(nb, MEM_LEN, 2 * MEM_W)
    mcfg = _mem_cfg()
    memo, mem_lse = _attn_fwd(mcfg, p3, mkv3, mkv3, out_cols=MEM_W, name="mem_fwd")

    fox2, dil2, memo2 = fox.reshape(t, FOX_W), dil.reshape(t, DIL_W), memo.reshape(t, MEM_W)
    y = _gate_fwd(fox2, dil2, memo2, p16, tm=1024, name="gate_fwd")
    dx2, dx2b, st = _out_loss(y, w_o, x2d, tgt2d, final_norm_g.reshape(1, D_MODEL), tm=512, name="out_loss")

    g_wo = _matmul(y.T, dx2b, out_dtype=F32, tm=1024, tn=512, tk=1024, name="grad_w_out")
    dy = _matmul(dx2b, w_o.T, out_dtype=F32, tm=1024, tn=512, tk=D_MODEL, name="d_y")
    datt, dgate = _gate_bwd(dy, fox2, dil2, memo2, p16, tm=1024, name="gate_bwd")
    dfox = datt[:, :FOX_W].reshape(nb, SEQ, FOX_W)
    ddil = datt[:, FOX_W:FOX_W + DIL_W].reshape(nb, SEQ, DIL_W)
    dmemo = datt[:, FOX_W + DIL_W:].reshape(nb, SEQ, MEM_W)

    dfq, dfk, dfv, dcr = _attn_bwd(fcfg, p3, p3, p3, dfox, fox, fox_lse, out_cols=FOX_W, kv_cols=FOX_W,
                                    bias=(cb3, crow), name="fox_bwd")
    dcol = -dcr[:, :, :, :2, :].transpose(0, 2, 4, 1, 3).reshape(t, FOX_HEADS)
    dcol = jnp.pad(dcol, ((0, 0), (0, LANES - FOX_HEADS)))
    dflog, gb = _flog_bwd(dcol, flog, bpad, nb=nb, ts=256, name="flog_bwd")

    acc = None
    for _, d in DILATIONS:
        cfg = _dil_cfg(d)
        l = SEQ // d
        pv = p16.reshape(nb, l, d * PW)
        view = lambda a: a.reshape(nb, l, d * DIL_W)
        if acc is not None:
            acc = tuple(view(a) for a in acc)
        acc = _attn_bwd(cfg, pv, pv, pv, view(ddil), view(dil), view(dil_lse), out_cols=d * DIL_W, kv_cols=d * DIL_W,
                        acc=acc, name=f"dil{d}_bwd")
    ddq, ddk, ddv = (a.reshape(t, DIL_W) for a in acc)
    ddq = _rope_bwd(ddq, tabs, tm=512, name="rope_bwd_q")
    ddk = _rope_bwd(ddk, tabs, tm=512, name="rope_bwd_k")

    dmq, dmk, dmv = _attn_bwd(mcfg, p3, mkv3, mkv3, dmemo, memo, mem_lse, out_cols=MEM_W, kv_cols=MEM_W, name="mem_bwd")
    dmkv = jnp.concatenate([dmk, dmv], axis=-1).reshape(nb * MEM_LEN, 2 * MEM_W).astype(BF16)
    g_wkv = _matmul(mh.T, dmkv, out_dtype=F32, tm=512, tn=512, tk=nb * MEM_LEN, name="grad_w_kv")
    _, gmn = _dh_rms_bwd(dmkv, w_kv.T, mem.reshape(nb * MEM_LEN, D_MODEL), mem_norm_g.reshape(1, D_MODEL), None,
                         tm=nb * MEM_LEN, tk=2 * MEM_W, name="mem_rms_bwd")

    bf = lambda a: a.reshape(t, -1).astype(BF16)
    dp = jnp.concatenate([bf(dfq), bf(dfk), bf(dfv), dgate[:, :FOX_W], ddq, ddk, bf(ddv), dgate[:, FOX_W:FOX_W + DIL_W],
                          bf(dmq), dgate[:, FOX_W + DIL_W:], dflog], axis=1)
    g_wr = _matmul(h.T, dp, out_dtype=F32, tm=512, tn=PWF // 3, tk=1024, name="grad_w_in")
    gx, gng = _dh_rms_bwd(dp, w_r.T, x2d, norm_g.reshape(1, D_MODEL), dx2, tm=512, tk=PWF // 3, name="in_rms_bwd")

    gb_row = jnp.pad(gb[0:1, :], ((0, 0), (0, D_MODEL - LANES)))
    small = jnp.concatenate([gng[0:1], gmn[0:1], st[0:1], gb_row, st[1:2], jnp.zeros((3, D_MODEL), F32)], axis=0)
    return gx.reshape(nb, SEQ, D_MODEL), g_wr, g_wkv, g_wo, small


MESH = pl.DeviceIdType.MESH
ANY = pl.BlockSpec(memory_space=pl.ANY)


def _place():
    x, y, c = lax.axis_index("x"), lax.axis_index("y"), lax.axis_index("c")
    other_chips = [(1 - x, y), (x, 1 - y), (1 - x, 1 - y)]
    return x, y, c, other_chips


def _gather_weights(shards):
    n = len(shards)

    def body(*refs):
        in_refs, out_refs = refs[:n], refs[n:2 * n]
        send_sems, recv_sems, local_sems = refs[2 * n:]
        x, y, c, chips = _place()
        me_chip = 2 * x + y
        sibling = (x, y, 1 - c)

        def half(ref, pc, rows):
            return ref.at[pl.ds(pc * (rows // 2), rows // 2), :]

        def rcopy(k, src, dst, to):
            return pltpu.make_async_remote_copy(src_ref=src, dst_ref=dst, send_sem=send_sems.at[k], recv_sem=recv_sems.at[k],
                                                device_id=to, device_id_type=MESH)

        locals_, sends = [], []
        for t in range(n):
            rows = shards[t].shape[0]
            lc = pltpu.make_async_copy(in_refs[t], out_refs[t].at[me_chip], local_sems.at[t])
            lc.start()
            locals_.append(lc)
            for j, chip in enumerate(chips):
                cp = rcopy(6 * t + j, half(in_refs[t], c, rows), half(out_refs[t].at[me_chip], c, rows), (*chip, c))
                cp.start()
                sends.append(cp)
        for t in range(n):
            rows = shards[t].shape[0]
            for j, chip in enumerate(chips):
                slot = out_refs[t].at[2 * chip[0] + chip[1]]
                rcopy(6 * t + j, half(slot, c, rows), half(slot, c, rows), sibling).wait_recv()
                fw = rcopy(6 * t + 3 + j, half(slot, c, rows), half(slot, c, rows), sibling)
                fw.start()
                sends.append(fw)
        for t in range(n):
            rows = shards[t].shape[0]
            for j, chip in enumerate(chips):
                slot = out_refs[t].at[2 * chip[0] + chip[1]]
                rcopy(6 * t + 3 + j, half(slot, 1 - c, rows), half(slot, 1 - c, rows), sibling).wait_recv()
        for cp in sends:
            cp.wait_send()
        for lc in locals_:
            lc.wait()

    return pl.pallas_call(
        body,
        out_shape=tuple(jax.ShapeDtypeStruct((N_CHIPS,) + s.shape, s.dtype) for s in shards),
        in_specs=[ANY] * n,
        out_specs=tuple([ANY] * n),
        scratch_shapes=[pltpu.SemaphoreType.DMA((6 * n,)), pltpu.SemaphoreType.DMA((6 * n,)), pltpu.SemaphoreType.DMA((n,))],
        name="gather_weights",
    )(*shards)


def _pair_exchange(gs):
    n = len(gs)

    def body(*refs):
        g_refs, r_refs = refs[:n], refs[n:2 * n]
        send_sems, recv_sems = refs[2 * n:]
        x, y, c, _ = _place()
        cps = []
        for t in range(n):
            hr = gs[t].shape[1] // 2
            cp = pltpu.make_async_remote_copy(src_ref=g_refs[t].at[:, pl.ds((1 - c) * hr, hr), :], dst_ref=r_refs[t],
                                              send_sem=send_sems.at[t], recv_sem=recv_sems.at[t],
                                              device_id=(x, y, 1 - c), device_id_type=MESH)
            cp.start()
            cps.append(cp)
        for cp in cps:
            cp.wait()

    return pl.pallas_call(
        body,
        out_shape=tuple(jax.ShapeDtypeStruct((N_CHIPS, g.shape[1] // 2, g.shape[2]), g.dtype) for g in gs),
        in_specs=[ANY] * n,
        out_specs=tuple([ANY] * n),
        scratch_shapes=[pltpu.SemaphoreType.DMA((n,)), pltpu.SemaphoreType.DMA((n,))],
        name="pair_exchange",
    )(*gs)


def _chip_exchange(ps):
    n = len(ps)

    def body(*refs):
        p_refs, o_refs = refs[:n], refs[n:2 * n]
        send_sems, recv_sems, local_sems = refs[2 * n:]
        x, y, c, chips = _place()
        me_chip = 2 * x + y
        cps, locals_ = [], []
        for t in range(n):
            lc = pltpu.make_async_copy(p_refs[t].at[me_chip], o_refs[t].at[me_chip], local_sems.at[t])
            lc.start()
            locals_.append(lc)
            for j, chip in enumerate(chips):
                cp = pltpu.make_async_remote_copy(src_ref=p_refs[t].at[2 * chip[0] + chip[1]], dst_ref=o_refs[t].at[me_chip],
                                                  send_sem=send_sems.at[3 * t + j], recv_sem=recv_sems.at[3 * t + j],
                                                  device_id=(*chip, c), device_id_type=MESH)
                cp.start()
                cps.append(cp)
        for cp in cps:
            cp.wait()
        for lc in locals_:
            lc.wait()

    return pl.pallas_call(
        body,
        out_shape=tuple(jax.ShapeDtypeStruct(p.shape, p.dtype) for p in ps),
        in_specs=[ANY] * n,
        out_specs=tuple([ANY] * n),
        scratch_shapes=[pltpu.SemaphoreType.DMA((3 * n,)), pltpu.SemaphoreType.DMA((3 * n,)), pltpu.SemaphoreType.DMA((n,))],
        name="chip_exchange",
    )(*ps)


def _pair_gather(rs):
    n = len(rs)

    def body(*refs):
        r_refs, o_refs = refs[:n], refs[n:2 * n]
        send_sems, recv_sems, local_sems = refs[2 * n:]
        x, y, c, _ = _place()
        cps, locals_ = [], []
        for t in range(n):
            lc = pltpu.make_async_copy(r_refs[t], o_refs[t].at[c], local_sems.at[t])
            lc.start()
            locals_.append(lc)
            cp = pltpu.make_async_remote_copy(src_ref=r_refs[t], dst_ref=o_refs[t].at[c], send_sem=send_sems.at[t],
                                              recv_sem=recv_sems.at[t], device_id=(x, y, 1 - c), device_id_type=MESH)
            cp.start()
            cps.append(cp)
        for cp in cps:
            cp.wait()
        for lc in locals_:
            lc.wait()

    return pl.pallas_call(
        body,
        out_shape=tuple(jax.ShapeDtypeStruct((2,) + r.shape, r.dtype) for r in rs),
        in_specs=[ANY] * n,
        out_specs=tuple([ANY] * n),
        scratch_shapes=[pltpu.SemaphoreType.DMA((n,)), pltpu.SemaphoreType.DMA((n,)), pltpu.SemaphoreType.DMA((n,))],
        name="pair_gather",
    )(*rs)


N_DEV = 8
LOSS_ROW = 4


def _small_allreduce(small):
    def body(s_ref, o_ref, all_ref, send_sems, recv_sems):
        x, y, c, _ = _place()
        me = 4 * x + 2 * y + c
        all_ref[me] = s_ref[...]
        cps = []
        for k in range(1, N_DEV):
            peer = tuple(1 - p if (k >> s) & 1 else p for p, s in ((x, 2), (y, 1), (c, 0)))
            cp = pltpu.make_async_remote_copy(src_ref=s_ref, dst_ref=all_ref.at[me], send_sem=send_sems.at[k - 1],
                                              recv_sem=recv_sems.at[k - 1], device_id=peer, device_id_type=MESH)
            cp.start()
            cps.append(cp)
        for cp in cps:
            cp.wait()
        tot = all_ref[0]
        for d in range(1, N_DEV):
            tot = tot + all_ref[d]
        o_ref[...] = tot
        o_ref[LOSS_ROW:LOSS_ROW + 1, :] = jnp.broadcast_to(jnp.sum(tot[LOSS_ROW:LOSS_ROW + 1, :], axis=1, keepdims=True),
                                                          (1, tot.shape[1]))

    vm = pl.BlockSpec(memory_space=pltpu.VMEM)
    return pl.pallas_call(
        body,
        out_shape=jax.ShapeDtypeStruct(small.shape, small.dtype),
        in_specs=[vm],
        out_specs=vm,
        scratch_shapes=[pltpu.VMEM((N_DEV,) + small.shape, small.dtype), pltpu.SemaphoreType.DMA((N_DEV - 1,)),
                        pltpu.SemaphoreType.DMA((N_DEV - 1,))],
        name="small_allreduce",
    )(small)


def _sum_pair(g, recv, cidx, *, tr, name):
    _, hr, cols = recv.shape
    nr = hr // tr

    def body(c_ref, g_ref, r_ref, o_ref):
        o_ref[...] = (g_ref[...] + r_ref[...]).astype(o_ref.dtype)

    grid_spec = pltpu.PrefetchScalarGridSpec(
        num_scalar_prefetch=1,
        grid=(N_CHIPS, nr),
        in_specs=[pl.BlockSpec((None, tr, cols), lambda k, i, c_ref: (k, c_ref[0] * nr + i, 0)),
                  pl.BlockSpec((None, tr, cols), lambda k, i, c_ref: (k, i, 0))],
        out_specs=pl.BlockSpec((None, tr, cols), lambda k, i, c_ref: (k, i, 0)),
    )
    return pl.pallas_call(body, out_shape=jax.ShapeDtypeStruct(recv.shape, recv.dtype), grid_spec=grid_spec,
                          compiler_params=_cparams(), name=name)(cidx, g, recv)


def _sum_chips(p, *, tr, name):
    _, rows, cols = p.shape

    def body(p_ref, o_ref):
        tot = p_ref[0].astype(F32)
        for k in range(1, N_CHIPS):
            tot = tot + p_ref[k].astype(F32)
        o_ref[...] = tot

    return pl.pallas_call(
        body,
        out_shape=jax.ShapeDtypeStruct((rows, cols), F32),
        grid=(rows // tr,),
        in_specs=[pl.BlockSpec((N_CHIPS, tr, cols), lambda i: (0, i, 0))],
        out_specs=pl.BlockSpec((tr, cols), lambda i: (i, 0)),
        compiler_params=_cparams(),
        name=name,
    )(p)


def _adamw(w, g, m, v, *, tr, name):
    rows, cols = w.shape
    bc1 = 1.0 / (1.0 - ADAM_B1 ** ADAM_STEP)
    bc2 = 1.0 / (1.0 - ADAM_B2 ** ADAM_STEP)

    def body(w_ref, g_ref, m_ref, v_ref, d_ref, nm_ref, nv_ref):
        gv = g_ref[...]
        nm = ADAM_B1 * m_ref[...] + (1.0 - ADAM_B1) * gv
        nv = ADAM_B2 * v_ref[...] + (1.0 - ADAM_B2) * (gv * gv)
        d_ref[...] = -ADAM_LR * ((nm * bc1) / (jnp.sqrt(nv * bc2) + ADAM_EPS) + ADAM_WD * w_ref[...])
        nm_ref[...] = nm
        nv_ref[...] = nv

    spec = pl.BlockSpec((tr, cols), lambda i: (i, 0))
    sd = jax.ShapeDtypeStruct((rows, cols), F32)
    return pl.pallas_call(body, out_shape=(sd, sd, sd), grid=(rows // tr,), in_specs=[spec] * 4, out_specs=(spec,) * 3,
                          compiler_params=_cparams(), name=name)(w, g, m, v)


def _pack_small(norm, mem_norm, final_norm, b_forget):
    rows = [norm.reshape(1, D_MODEL), mem_norm.reshape(1, D_MODEL), final_norm.reshape(1, D_MODEL),
            jnp.pad(b_forget.reshape(1, FOX_HEADS), ((0, 0), (0, D_MODEL - FOX_HEADS))), jnp.zeros((4, D_MODEL), F32)]
    return jnp.concatenate(rows, axis=0)


def _unpack_small(a):
    return a[0:1], a[3:4, :FOX_HEADS], a[1:2], a[2]


def kernel(x, mem, norm_g, w_in, b_forget, mem_norm_g, w_mem_kv, w_out, final_norm_g, loss_target, m_norm_g, m_w_in, m_b_forget, m_mem_norm_g, m_w_mem_kv, m_w_out, m_final_norm_g, v_norm_g, v_w_in, v_b_forget, v_mem_norm_g, v_w_mem_kv, v_w_out, v_final_norm_g):
    cidx = lax.axis_index("c").astype(jnp.int32).reshape(1)

    g_in, g_kv, g_out = _gather_weights([w_in[0].astype(BF16), w_mem_kv[0].astype(BF16), w_out[0].astype(BF16)])
    w_r = _rearrange_w_in(jnp.concatenate([g_in[k] for k in range(N_CHIPS)], axis=1))
    w_kv = g_kv.reshape(D_MODEL, 2 * MEM_W)
    w_o = g_out.reshape(MIX_W, D_MODEL)

    gx, g_wr, g_wkv, g_wo, small = _local_grads(x, mem, norm_g, w_r, b_forget, mem_norm_g, w_kv, w_o, final_norm_g, loss_target)

    shard_w = IN_W // N_CHIPS
    slabs = [_restore_w_in_cols(g_wr).reshape(D_MODEL, N_CHIPS, shard_w).transpose(1, 0, 2),
             g_wkv.reshape(N_CHIPS, D_MODEL // N_CHIPS, 2 * MEM_W),
             g_wo.reshape(N_CHIPS, MIX_W // N_CHIPS, D_MODEL)]
    trs = (128, 128, 256)
    names = ("w_in", "w_mem_kv", "w_out")
    recv = _pair_exchange(slabs)
    pair = [_sum_pair(g, r, cidx, tr=tr, name=f"sum_pair_{nm}") for g, r, tr, nm in zip(slabs, recv, trs, names)]
    got = _chip_exchange(pair)
    red = [_sum_chips(p, tr=tr, name=f"sum_chips_{nm}") for p, tr, nm in zip(got, trs, names)]
    full = _pair_gather(red)
    grads = [f.reshape(2 * f.shape[1], f.shape[2]) for f in full]

    outs = {}
    for nm, g, w, m, v, tr in zip(names, grads, (w_in, w_mem_kv, w_out), (m_w_in, m_w_mem_kv, m_w_out),
                                  (v_w_in, v_w_mem_kv, v_w_out), trs):
        d, nmo, nvo = _adamw(w[0], g, m[0], v[0], tr=tr, name=f"adamw_{nm}")
        outs[nm] = tuple(a[None] for a in (g, d, nmo, nvo))

    gsum = _small_allreduce(small)
    sd, sm, sv = _adamw(_pack_small(norm_g, mem_norm_g, final_norm_g, b_forget), gsum,
                        _pack_small(m_norm_g, m_mem_norm_g, m_final_norm_g, m_b_forget),
                        _pack_small(v_norm_g, v_mem_norm_g, v_final_norm_g, v_b_forget), tr=8, name="adamw_small")
    loss = gsum[LOSS_ROW, 0]

    def group(i, small_arr):
        ng, bf, mg, fg = _unpack_small(small_arr)
        return (ng, outs["w_in"][i], bf, mg, outs["w_mem_kv"][i], outs["w_out"][i], fg)

    return (loss, gx, *group(0, gsum), *group(1, sd), *group(2, sm), *group(3, sv))
```

```python
import functools
import math

import jax
import jax.numpy as jnp
from jax import lax
from jax.experimental import pallas as pl
from jax.experimental.pallas import tpu as pltpu

F32 = jnp.float32
BF16 = jnp.bfloat16

D_MODEL = 1024
SEQ = 2048
HEAD_DIM = 64
FOX_HEADS = 12
DIL_HEADS = 12
MEM_HEADS = 4
MEM_HEAD_DIM = 128
MEM_LEN = 256
FOX_W = FOX_HEADS * HEAD_DIM
DIL_W = DIL_HEADS * HEAD_DIM
MEM_W = MEM_HEADS * MEM_HEAD_DIM
MIX_W = FOX_W + DIL_W + MEM_W
DILATIONS = ((128, 1), (512, 4), (2048, 16))
ROPE_THETA = 500000.0
ROPE_DIM = HEAD_DIM // 4
RMS_EPS = 1e-6
NEG_INF = -1e30
IN_SIZES = [FOX_W] * 4 + [FOX_HEADS] + [DIL_W] * 4 + [MEM_W] * 2
IN_W = sum(IN_SIZES)

ADAM_LR = 0.001
ADAM_B1 = 0.9
ADAM_B2 = 0.999
ADAM_EPS = 1e-08
ADAM_WD = 0.01
ADAM_STEP = 10

LANES = 128
N_CHIPS = 4
PW = 7168
PWF = PW + LANES
C_FQ, C_FK, C_FV, C_FG = 0, 768, 1536, 2304
C_DQ, C_DK, C_DV, C_DG = 3072, 3840, 4608, 5376
C_MQ, C_MG = 6144, 6656
VMEM_LIMIT = 48 * 1024 * 1024


def _cparams(**kw):
    return pltpu.CompilerParams(vmem_limit_bytes=VMEM_LIMIT, **kw)


def _matmul(a, b, *, out_dtype, tm, tn, tk, name):
    m, kdim = a.shape
    _, n = b.shape
    nk = kdim // tk
    assert m % tm == 0 and n % tn == 0 and kdim % tk == 0

    def body(a_ref, b_ref, o_ref, acc_ref):
        k = pl.program_id(2)

        @pl.when(k == 0)
        def _():
            acc_ref[...] = jnp.zeros_like(acc_ref)

        acc_ref[...] += jnp.dot(a_ref[...], b_ref[...], preferred_element_type=F32)

        @pl.when(k == nk - 1)
        def _():
            o_ref[...] = acc_ref[...].astype(o_ref.dtype)

    return pl.pallas_call(
        body,
        out_shape=jax.ShapeDtypeStruct((m, n), out_dtype),
        grid=(m // tm, n // tn, nk),
        in_specs=[pl.BlockSpec((tm, tk), lambda i, j, k: (i, k)), pl.BlockSpec((tk, tn), lambda i, j, k: (k, j))],
        out_specs=pl.BlockSpec((tm, tn), lambda i, j, k: (i, j)),
        scratch_shapes=[pltpu.VMEM((tm, tn), F32)],
        compiler_params=_cparams(dimension_semantics=("parallel", "parallel", "arbitrary")),
        name=name,
    )(a, b)


def _rms_fwd(x, g, *, tm, name):
    t, d = x.shape

    def body(x_ref, g_ref, h_ref):
        xv = x_ref[...]
        r = lax.rsqrt(jnp.mean(xv * xv, axis=-1, keepdims=True) + RMS_EPS)
        h_ref[...] = (xv * r * g_ref[...]).astype(h_ref.dtype)

    return pl.pallas_call(
        body,
        out_shape=jax.ShapeDtypeStruct((t, d), BF16),
        grid=(t // tm,),
        in_specs=[pl.BlockSpec((tm, d), lambda i: (i, 0)), pl.BlockSpec((1, d), lambda i: (0, 0))],
        out_specs=pl.BlockSpec((tm, d), lambda i: (i, 0)),
        compiler_params=_cparams(),
        name=name,
    )(x, g)


def _rope_tables():
    half = ROPE_DIM // 2
    pos = jnp.arange(SEQ, dtype=F32)
    inv_freq = 1.0 / (ROPE_THETA ** (jnp.arange(0, ROPE_DIM, 2, dtype=F32) / ROPE_DIM))
    ang = pos[:, None] * inv_freq[None, :]
    cos, sin = jnp.cos(ang), jnp.sin(ang)
    one = jnp.ones((SEQ, HEAD_DIM - ROPE_DIM), F32)
    zero = jnp.zeros((SEQ, HEAD_DIM - ROPE_DIM), F32)
    zh = jnp.zeros((SEQ, half), F32)
    c = jnp.concatenate([cos, cos, one], axis=1)
    s1 = jnp.concatenate([zh, sin, zero], axis=1)
    s2 = jnp.concatenate([-sin, zh, zero], axis=1)
    rep = LANES // HEAD_DIM
    return jnp.tile(c, (1, rep)), jnp.tile(s1, (1, rep)), jnp.tile(s2, (1, rep))


def _rope_apply(t, c, s1, s2, transpose=False):
    n = t.shape[-1]
    rep = n // LANES
    c, s1, s2 = (jnp.tile(u, (1, rep)) for u in (c, s1, s2))
    half = ROPE_DIM // 2
    if not transpose:
        return t * c + pltpu.roll(t, half, 1) * s1 + pltpu.roll(t, n - half, 1) * s2
    return t * c + pltpu.roll(t * s1, n - half, 1) + pltpu.roll(t * s2, half, 1)


def _proj(h, w, tabs, *, tm, tn, name):
    t, d = h.shape
    n = w.shape[1]
    assert C_DQ % tn == 0 and (C_DV - C_DQ) % tn == 0
    rope_lo, rope_hi = C_DQ // tn, C_DV // tn
    s_blocks = SEQ // tm

    def body(h_ref, w_ref, c_ref, s1_ref, s2_ref, o_ref):
        j = pl.program_id(1)
        acc = jnp.dot(h_ref[...], w_ref[...], preferred_element_type=F32)
        is_rope = jnp.logical_and(j >= rope_lo, j < rope_hi)

        @pl.when(is_rope)
        def _():
            o_ref[...] = _rope_apply(acc, c_ref[...], s1_ref[...], s2_ref[...]).astype(o_ref.dtype)

        @pl.when(jnp.logical_not(is_rope))
        def _():
            o_ref[...] = acc.astype(o_ref.dtype)

    tab_spec = pl.BlockSpec((tm, LANES), lambda i, j: (i % s_blocks, 0))
    return pl.pallas_call(
        body,
        out_shape=jax.ShapeDtypeStruct((t, n), BF16),
        grid=(t // tm, n // tn),
        in_specs=[pl.BlockSpec((tm, d), lambda i, j: (i, 0)), pl.BlockSpec((d, tn), lambda i, j: (0, j)),
                  tab_spec, tab_spec, tab_spec],
        out_specs=pl.BlockSpec((tm, tn), lambda i, j: (i, j)),
        compiler_params=_cparams(dimension_semantics=("parallel", "parallel")),
        name=name,
    )(h, w, *tabs)


def _split3(x):
    hi = x.astype(BF16)
    r1 = x - hi.astype(F32)
    mid = r1.astype(BF16)
    lo = (r1 - mid.astype(F32)).astype(BF16)
    return hi, mid, lo


def _dot3(sel, x, sel_is_lhs):
    out = None
    for piece in _split3(x):
        t = jnp.dot(sel, piece, preferred_element_type=F32) if sel_is_lhs else jnp.dot(piece, sel, preferred_element_type=F32)
        out = t if out is None else out + t
    return out


def _head_expand_matrix():
    r = lax.broadcasted_iota(jnp.int32, (LANES, FOX_W), 0)
    c = lax.broadcasted_iota(jnp.int32, (LANES, FOX_W), 1)
    return jnp.where(c // HEAD_DIM == r, 1.0, 0.0).astype(BF16)


def _flog_fwd(flog, bpad, *, nb, ts, name):
    ns = SEQ // ts

    def body(f_ref, b_ref, c_ref, cb_ref, carry_ref):
        s = pl.program_id(1)

        @pl.when(s == 0)
        def _():
            carry_ref[...] = jnp.zeros_like(carry_ref)

        z = f_ref[...] + b_ref[...]
        logf = jnp.minimum(z, 0.0) - jnp.log(1.0 + jnp.exp(-jnp.abs(z)))
        r = lax.broadcasted_iota(jnp.int32, (ts, ts), 0)
        c = lax.broadcasted_iota(jnp.int32, (ts, ts), 1)
        tri = jnp.where(r >= c, 1.0, 0.0).astype(BF16)
        cs = _dot3(tri, logf, True) + carry_ref[0:1, :]
        carry_ref[...] = jnp.broadcast_to(cs[ts - 1:ts, :], carry_ref.shape)
        c_ref[...] = cs
        cb_ref[...] = _dot3(_head_expand_matrix(), cs, False)

    return pl.pallas_call(
        body,
        out_shape=(jax.ShapeDtypeStruct((nb * SEQ, LANES), F32), jax.ShapeDtypeStruct((nb * SEQ, FOX_W), F32)),
        grid=(nb, ns),
        in_specs=[pl.BlockSpec((ts, LANES), lambda b, s: (b * ns + s, 0)), pl.BlockSpec((1, LANES), lambda b, s: (0, 0))],
        out_specs=(pl.BlockSpec((ts, LANES), lambda b, s: (b * ns + s, 0)), pl.BlockSpec((ts, FOX_W), lambda b, s: (b * ns + s, 0))),
        scratch_shapes=[pltpu.VMEM((8, LANES), F32)],
        compiler_params=_cparams(dimension_semantics=("parallel", "arbitrary")),
        name=name,
    )(flog, bpad)


def _flog_bwd(dcol, flog, bpad, *, nb, ts, name):
    ns = SEQ // ts

    def body(d_ref, f_ref, b_ref, o_ref, gb_ref, carry_ref):
        bi = pl.program_id(0)
        s = pl.program_id(1)

        @pl.when(s == 0)
        def _():
            carry_ref[...] = jnp.zeros_like(carry_ref)

        @pl.when(jnp.logical_and(bi == 0, s == 0))
        def _():
            gb_ref[...] = jnp.zeros_like(gb_ref)

        r = lax.broadcasted_iota(jnp.int32, (ts, ts), 0)
        c = lax.broadcasted_iota(jnp.int32, (ts, ts), 1)
        tri = jnp.where(r <= c, 1.0, 0.0).astype(BF16)
        rc = _dot3(tri, d_ref[...], True) + carry_ref[0:1, :]
        carry_ref[...] = jnp.broadcast_to(rc[0:1, :], carry_ref.shape)
        z = f_ref[...] + b_ref[...]
        dz = rc / (1.0 + jnp.exp(z))
        o_ref[...] = dz.astype(o_ref.dtype)
        gb_ref[...] += jnp.broadcast_to(jnp.sum(dz, axis=0, keepdims=True), gb_ref.shape)

    rev = lambda b, s: (b * ns + (ns - 1 - s), 0)
    return pl.pallas_call(
        body,
        out_shape=(jax.ShapeDtypeStruct((nb * SEQ, LANES), BF16), jax.ShapeDtypeStruct((8, LANES), F32)),
        grid=(nb, ns),
        in_specs=[pl.BlockSpec((ts, LANES), rev), pl.BlockSpec((ts, LANES), rev), pl.BlockSpec((1, LANES), lambda b, s: (0, 0))],
        out_specs=(pl.BlockSpec((ts, LANES), rev), pl.BlockSpec((8, LANES), lambda b, s: (0, 0))),
        scratch_shapes=[pltpu.VMEM((8, LANES), F32)],
        compiler_params=_cparams(dimension_semantics=("arbitrary", "arbitrary")),
        name=name,
    )(dcol, flog, bpad)


class _AttnCfg:
    def __init__(self, *, e, tq, tk, lq, lk, causal, window, ncol, qcol, kcol, vcol, split_p=False):
        self.e, self.tq, self.tk, self.lq, self.lk = e, tq, tk, lq, lk
        self.split_p = split_p
        self.causal, self.window = causal, window
        self.ncol, self.qcol, self.kcol, self.vcol = ncol, qcol, kcol, vcol
        self.nh = LANES // e
        self.scale = 1.0 / math.sqrt(e)
        self.nq, self.nk = lq // tq, lk // tk

    def k_range(self, i):
        if not self.causal:
            return 0, self.nk
        hi = ((i + 1) * self.tq - 1) // self.tk + 1
        if self.window is None:
            return 0, hi
        return jnp.maximum((i * self.tq - self.window) // self.tk, 0), hi


def _head_masks(nh):
    lane = lax.broadcasted_iota(jnp.int32, (1, LANES), 1)
    return [None] if nh == 1 else [lane < HEAD_DIM, lane >= HEAD_DIM]


def _sel(mask, a, b):
    return a if mask is None else jnp.where(mask, a, b)


def _scores(cfg, qh, kb, q0, k0, dlt0, bias):
    s = lax.dot_general(qh, kb, (((1,), (1,)), ((), ())), preferred_element_type=F32) * cfg.scale
    if bias is not None:
        s = s + bias
    if cfg.causal:
        d = dlt0 + (q0 - k0)
        if cfg.window is None:
            ok = d >= 0
        else:
            ok = d.astype(jnp.uint32) <= jnp.uint32(cfg.window)
        s = jnp.where(ok, s, NEG_INF)
    return s


def _attn_fwd(cfg, q, k, v, *, out_cols, bias=None, state=None, finalize=True, name):
    g = q.shape[0]
    tq, tk, e, nh = cfg.tq, cfg.tk, cfg.e, cfg.nh

    def body(*refs):
        refs = list(refs)
        q_ref, k_ref, v_ref = refs[:3]
        del refs[:3]
        if bias is not None:
            cb_ref, cr_ref = refs[:2]
            del refs[:2]
        if state is not None:
            ai_ref, mi_ref, li_ref = refs[:3]
            del refs[:3]
        out_refs = refs
        masks = _head_masks(nh)
        dlt0 = lax.broadcasted_iota(jnp.int32, (tq, tk), 0) - lax.broadcasted_iota(jnp.int32, (tq, tk), 1)

        def qbody(i, carry):
            q0 = pl.multiple_of(i * tq, tq)
            rows = pl.ds(q0, tq)
            qb = q_ref[rows, :]
            lo, hi = cfg.k_range(i)
            res = []
            for h in range(nh):
                qh = _sel(masks[h], qb, jnp.zeros_like(qb))
                if state is not None:
                    m0 = mi_ref[rows, h * e:h * e + 1]
                    l0 = li_ref[rows, h * e:h * e + 1]
                    a0 = ai_ref[rows, :]
                else:
                    m0 = jnp.full((tq, 1), NEG_INF, F32)
                    l0 = jnp.zeros((tq, 1), F32)
                    a0 = jnp.zeros((tq, LANES), F32)
                cq = cb_ref[rows, h * e:h * e + 1] if bias is not None else None

                def kbody(jk, c, qh=qh, cq=cq, h=h):
                    m, l, a = c
                    k0 = pl.multiple_of(jk * tk, tk)
                    kb = k_ref[pl.ds(k0, tk), :]
                    vb = v_ref[pl.ds(k0, tk), :]
                    b = (cq - cr_ref[jk, h:h + 1, :]) if bias is not None else None
                    s = _scores(cfg, qh, kb, q0, k0, dlt0, b)
                    m_new = jnp.maximum(m, jnp.max(s, axis=1, keepdims=True))
                    alpha = jnp.exp(m - m_new)
                    p = jnp.exp(s - m_new)
                    l = alpha * l + jnp.sum(p, axis=1, keepdims=True)
                    pb = p.astype(BF16)
                    pv = jnp.dot(pb, vb, preferred_element_type=F32)
                    if cfg.split_p:
                        pv = pv + jnp.dot((p - pb.astype(F32)).astype(BF16), vb, preferred_element_type=F32)
                    a = alpha * a + pv
                    return m_new, l, a

                res.append(lax.fori_loop(lo, hi, kbody, (m0, l0, a0)))
            if nh == 1:
                m, l, a = res[0]
                m, l = jnp.broadcast_to(m, (tq, LANES)), jnp.broadcast_to(l, (tq, LANES))
            else:
                m = jnp.where(masks[0], res[0][0], res[1][0])
                l = jnp.where(masks[0], res[0][1], res[1][1])
                a = jnp.where(masks[0], res[0][2], res[1][2])
            if finalize:
                out_refs[0][rows, :] = a / l
                out_refs[1][rows, :] = m + jnp.log(l)
            else:
                out_refs[0][rows, :] = a
                out_refs[1][rows, :] = m
                out_refs[2][rows, :] = l
            return carry

        lax.fori_loop(0, cfg.nq, qbody, 0)

    qspec = pl.BlockSpec((None, cfg.lq, LANES), lambda b, j: (b, 0, cfg.qcol(j)))
    kspec = pl.BlockSpec((None, cfg.lk, LANES), lambda b, j: (b, 0, cfg.kcol(j)))
    vspec = pl.BlockSpec((None, cfg.lk, LANES), lambda b, j: (b, 0, cfg.vcol(j)))
    ospec = pl.BlockSpec((None, cfg.lq, LANES), lambda b, j: (b, 0, j))
    args, in_specs = [q, k, v], [qspec, kspec, vspec]
    if bias is not None:
        args += list(bias)
        in_specs += [ospec, pl.BlockSpec((None, None, cfg.nk, 8, tk), lambda b, j: (b, j, 0, 0, 0))]
    aliases = {}
    if state is not None:
        aliases = {len(args) + t: t for t in range(3 if not finalize else 2)}
        args += list(state)
        in_specs += [ospec] * 3
    n_out = 2 if finalize else 3
    osd = jax.ShapeDtypeStruct((g, cfg.lq, out_cols), F32)
    return pl.pallas_call(
        body,
        out_shape=(osd,) * n_out,
        grid=(g, cfg.ncol),
        in_specs=in_specs,
        out_specs=(ospec,) * n_out,
        input_output_aliases=aliases,
        compiler_params=_cparams(dimension_semantics=("parallel", "parallel")),
        name=name,
    )(*args)


def _attn_bwd(cfg, q, k, v, do, o, lse, *, out_cols, kv_cols, bias=None, acc=None, name):
    g = q.shape[0]
    tq, tk, e, nh = cfg.tq, cfg.tk, cfg.e, cfg.nh
    t0 = (((0,), (0,)), ((), ()))

    def body(*refs):
        refs = list(refs)
        q_ref, k_ref, v_ref, do_ref, o_ref, lse_ref = refs[:6]
        del refs[:6]
        if bias is not None:
            cb_ref, cr_ref = refs[:2]
            del refs[:2]
        if acc is not None:
            dqi_ref, dki_ref, dvi_ref = refs[:3]
            del refs[:3]
        dq_ref, dk_ref, dv_ref = refs[:3]
        dcr_ref = refs[3] if bias is not None else None
        masks = _head_masks(nh)
        dlt0 = lax.broadcasted_iota(jnp.int32, (tq, tk), 0) - lax.broadcasted_iota(jnp.int32, (tq, tk), 1)
        if acc is not None:
            dq_ref[...] = dqi_ref[...]
            dk_ref[...] = dki_ref[...]
            dv_ref[...] = dvi_ref[...]
        else:
            dq_ref[...] = jnp.zeros_like(dq_ref)
            dk_ref[...] = jnp.zeros_like(dk_ref)
            dv_ref[...] = jnp.zeros_like(dv_ref)
        if dcr_ref is not None:
            dcr_ref[...] = jnp.zeros_like(dcr_ref)

        def qbody(i, carry):
            q0 = pl.multiple_of(i * tq, tq)
            rows = pl.ds(q0, tq)
            qb = q_ref[rows, :]
            dob = do_ref[rows, :].astype(BF16)
            prod = dob.astype(F32) * o_ref[rows, :]
            lo, hi = cfg.k_range(i)
            dqs = []
            for h in range(nh):
                qh = _sel(masks[h], qb, jnp.zeros_like(qb))
                doh = _sel(masks[h], dob, jnp.zeros_like(dob))
                lse_h = lse_ref[rows, h * e:h * e + 1]
                delta = jnp.sum(_sel(masks[h], prod, jnp.zeros_like(prod)), axis=1, keepdims=True)
                cq = cb_ref[rows, h * e:h * e + 1] if bias is not None else None

                def kbody(jk, dq_acc, qh=qh, doh=doh, lse_h=lse_h, delta=delta, cq=cq, h=h):
                    k0 = pl.multiple_of(jk * tk, tk)
                    krows = pl.ds(k0, tk)
                    kb = k_ref[krows, :]
                    vb = v_ref[krows, :]
                    b = (cq - cr_ref[jk, h:h + 1, :]) if bias is not None else None
                    s = _scores(cfg, qh, kb, q0, k0, dlt0, b)
                    p = jnp.exp(s - lse_h)
                    dp = lax.dot_general(doh, vb, (((1,), (1,)), ((), ())), preferred_element_type=F32)
                    ds = p * (dp - delta)
                    if dcr_ref is not None:
                        dcr_ref[jk, h:h + 1, :] += jnp.sum(ds, axis=0, keepdims=True)
                    dsb = (ds * cfg.scale).astype(BF16)
                    dv_ref[krows, :] += lax.dot_general(p.astype(BF16), doh, t0, preferred_element_type=F32)
                    dk_ref[krows, :] += lax.dot_general(dsb, qh, t0, preferred_element_type=F32)
                    return dq_acc + jnp.dot(dsb, kb, preferred_element_type=F32)

                dqs.append(lax.fori_loop(lo, hi, kbody, jnp.zeros((tq, LANES), F32)))
            dq = dqs[0] if nh == 1 else jnp.where(masks[0], dqs[0], dqs[1])
            dq_ref[rows, :] += dq
            return carry

        lax.fori_loop(0, cfg.nq, qbody, 0)

    qspec = pl.BlockSpec((None, cfg.lq, LANES), lambda b, j: (b, 0, cfg.qcol(j)))
    kspec = pl.BlockSpec((None, cfg.lk, LANES), lambda b, j: (b, 0, cfg.kcol(j)))
    vspec = pl.BlockSpec((None, cfg.lk, LANES), lambda b, j: (b, 0, cfg.vcol(j)))
    ospec = pl.BlockSpec((None, cfg.lq, LANES), lambda b, j: (b, 0, j))
    kvspec = pl.BlockSpec((None, cfg.lk, LANES), lambda b, j: (b, 0, j))
    args, in_specs = [q, k, v, do, o, lse], [qspec, kspec, vspec, ospec, ospec, ospec]
    out_shape = [jax.ShapeDtypeStruct((g, cfg.lq, out_cols), F32), jax.ShapeDtypeStruct((g, cfg.lk, kv_cols), F32),
                 jax.ShapeDtypeStruct((g, cfg.lk, kv_cols), F32)]
    out_specs = [ospec, kvspec, kvspec]
    if bias is not None:
        args += list(bias)
        crspec = pl.BlockSpec((None, None, cfg.nk, 8, tk), lambda b, j: (b, j, 0, 0, 0))
        in_specs += [ospec, crspec]
        out_shape.append(jax.ShapeDtypeStruct((g, cfg.ncol, cfg.nk, 8, tk), F32))
        out_specs.append(crspec)
    aliases = {}
    if acc is not None:
        aliases = {len(args) + t: t for t in range(3)}
        args += list(acc)
        in_specs += [ospec, kvspec, kvspec]
    return pl.pallas_call(
        body,
        out_shape=tuple(out_shape),
        grid=(g, cfg.ncol),
        in_specs=in_specs,
        out_specs=tuple(out_specs),
        input_output_aliases=aliases,
        compiler_params=_cparams(dimension_semantics=("parallel", "parallel")),
        name=name,
    )(*args)


PBLK = PW // LANES


def _fox_cfg():
    return _AttnCfg(e=HEAD_DIM, tq=256, tk=256, lq=SEQ, lk=SEQ, causal=True, window=None, ncol=FOX_W // LANES, split_p=True,
                    qcol=lambda j: C_FQ // LANES + j, kcol=lambda j: C_FK // LANES + j, vcol=lambda j: C_FV // LANES + j)


def _dil_cfg(d):
    l = SEQ // d
    per = DIL_W // LANES

    def col(base):
        return lambda j: (j // per) * PBLK + base // LANES + j % per

    return _AttnCfg(e=HEAD_DIM, tq=128, tk=128, lq=l, lk=l, causal=True, window=128, ncol=d * per,
                    qcol=col(C_DQ), kcol=col(C_DK), vcol=col(C_DV))


def _mem_cfg():
    return _AttnCfg(e=MEM_HEAD_DIM, tq=256, tk=MEM_LEN, lq=SEQ, lk=MEM_LEN, causal=False, window=None, ncol=MEM_HEADS,
                    qcol=lambda j: C_MQ // LANES + j, kcol=lambda j: j, vcol=lambda j: MEM_HEADS + j)


N_YBLK = MIX_W // LANES
_GATE_BLK = (C_FG // LANES, C_DG // LANES, C_MG // LANES)
_B1, _B2 = FOX_W // LANES, (FOX_W + DIL_W) // LANES


def _att_specs(tm):
    fspec = pl.BlockSpec((tm, LANES), lambda i, j: (i, jnp.minimum(j, _B1 - 1)))
    dspec = pl.BlockSpec((tm, LANES), lambda i, j: (i, jnp.clip(j - _B1, 0, _B2 - _B1 - 1)))
    mspec = pl.BlockSpec((tm, LANES), lambda i, j: (i, jnp.clip(j - _B2, 0, N_YBLK - _B2 - 1)))

    def gcol(j):
        return jnp.where(j < _B1, _GATE_BLK[0] + j, jnp.where(j < _B2, _GATE_BLK[1] + j - _B1, _GATE_BLK[2] + j - _B2))

    gspec = pl.BlockSpec((tm, LANES), lambda i, j: (i, gcol(j)))
    return fspec, dspec, mspec, gspec


def _pick_att(j, f_ref, d_ref, m_ref):
    return jnp.where(j < _B1, f_ref[...], jnp.where(j < _B2, d_ref[...], m_ref[...]))


def _gate_fwd(fox, dil, memo, p16, *, tm, name):
    t = fox.shape[0]

    def body(f_ref, d_ref, m_ref, g_ref, y_ref):
        j = pl.program_id(1)
        a = _pick_att(j, f_ref, d_ref, m_ref)
        gt = g_ref[...].astype(F32)
        y_ref[...] = (a * gt / (1.0 + jnp.exp(-gt))).astype(y_ref.dtype)

    return pl.pallas_call(
        body,
        out_shape=jax.ShapeDtypeStruct((t, MIX_W), BF16),
        grid=(t // tm, N_YBLK),
        in_specs=list(_att_specs(tm)),
        out_specs=pl.BlockSpec((tm, LANES), lambda i, j: (i, j)),
        compiler_params=_cparams(dimension_semantics=("parallel", "parallel")),
        name=name,
    )(fox, dil, memo, p16)


def _gate_bwd(dy, fox, dil, memo, p16, *, tm, name):
    t = fox.shape[0]

    def body(dy_ref, f_ref, d_ref, m_ref, g_ref, da_ref, dg_ref):
        j = pl.program_id(1)
        a = _pick_att(j, f_ref, d_ref, m_ref)
        gt = g_ref[...].astype(F32)
        sg = 1.0 / (1.0 + jnp.exp(-gt))
        dyv = dy_ref[...]
        da_ref[...] = dyv * gt * sg
        dg_ref[...] = (dyv * a * sg * (1.0 + gt * (1.0 - sg))).astype(dg_ref.dtype)

    yspec = pl.BlockSpec((tm, LANES), lambda i, j: (i, j))
    return pl.pallas_call(
        body,
        out_shape=(jax.ShapeDtypeStruct((t, MIX_W), F32), jax.ShapeDtypeStruct((t, MIX_W), BF16)),
        grid=(t // tm, N_YBLK),
        in_specs=[yspec] + list(_att_specs(tm)),
        out_specs=(yspec, yspec),
        compiler_params=_cparams(dimension_semantics=("parallel", "parallel")),
        name=name,
    )(dy, fox, dil, memo, p16)


def _out_loss(y, wo, x, tgt, gfin, *, tm, name):
    t, d = x.shape
    n_feat = float(d)

    def body(y_ref, w_ref, x_ref, t_ref, g_ref, dx_ref, dxb_ref, st_ref):
        i = pl.program_id(0)

        @pl.when(i == 0)
        def _():
            st_ref[...] = jnp.zeros_like(st_ref)

        x2 = x_ref[...] + jnp.dot(y_ref[...], w_ref[...], preferred_element_type=F32)
        r = lax.rsqrt(jnp.mean(x2 * x2, axis=-1, keepdims=True) + RMS_EPS)
        nrm = x2 * r
        gv = g_ref[...]
        err = nrm * gv - t_ref[...]
        dout = err * (1.0 / n_feat)
        dn = dout * gv
        dx2 = r * (dn - nrm * jnp.mean(dn * nrm, axis=-1, keepdims=True))
        dx_ref[...] = dx2
        dxb_ref[...] = dx2.astype(dxb_ref.dtype)
        st_ref[0:1, :] += jnp.sum(dout * nrm, axis=0, keepdims=True)
        st_ref[1:2, :] += (0.5 / n_feat) * jnp.sum(err * err, axis=0, keepdims=True)

    row = pl.BlockSpec((tm, d), lambda i: (i, 0))
    return pl.pallas_call(
        body,
        out_shape=(jax.ShapeDtypeStruct((t, d), F32), jax.ShapeDtypeStruct((t, d), BF16), jax.ShapeDtypeStruct((8, d), F32)),
        grid=(t // tm,),
        in_specs=[pl.BlockSpec((tm, MIX_W), lambda i: (i, 0)), pl.BlockSpec((MIX_W, d), lambda i: (0, 0)), row, row,
                  pl.BlockSpec((1, d), lambda i: (0, 0))],
        out_specs=(row, row, pl.BlockSpec((8, d), lambda i: (0, 0))),
        compiler_params=_cparams(dimension_semantics=("arbitrary",)),
        name=name,
    )(y, wo, x, tgt, gfin)


def _dh_rms_bwd(dp, wt, x, g, resid, *, tm, tk, name):
    t, d = x.shape
    kdim = dp.shape[1]
    nk = kdim // tk

    def body(*refs):
        if resid is not None:
            dp_ref, w_ref, x_ref, g_ref, r_ref, dx_ref, gg_ref, acc_ref = refs
        else:
            dp_ref, w_ref, x_ref, g_ref, dx_ref, gg_ref, acc_ref = refs
        i = pl.program_id(0)
        k = pl.program_id(1)

        @pl.when(jnp.logical_and(i == 0, k == 0))
        def _():
            gg_ref[...] = jnp.zeros_like(gg_ref)

        @pl.when(k == 0)
        def _():
            acc_ref[...] = jnp.zeros_like(acc_ref)

        acc_ref[...] += jnp.dot(dp_ref[...], w_ref[...], preferred_element_type=F32)

        @pl.when(k == nk - 1)
        def _():
            dh = acc_ref[...]
            xv = x_ref[...]
            r = lax.rsqrt(jnp.mean(xv * xv, axis=-1, keepdims=True) + RMS_EPS)
            nrm = xv * r
            dn = dh * g_ref[...]
            dx = r * (dn - nrm * jnp.mean(dn * nrm, axis=-1, keepdims=True))
            if resid is not None:
                dx = dx + r_ref[...]
            dx_ref[...] = dx
            gg_ref[0:1, :] += jnp.sum(dh * nrm, axis=0, keepdims=True)

    row = pl.BlockSpec((tm, d), lambda i, k: (i, 0))
    in_specs = [pl.BlockSpec((tm, tk), lambda i, k: (i, k)), pl.BlockSpec((tk, d), lambda i, k: (k, 0)), row,
                pl.BlockSpec((1, d), lambda i, k: (0, 0))]
    args = [dp, wt, x, g]
    if resid is not None:
        in_specs.append(row)
        args.append(resid)
    return pl.pallas_call(
        body,
        out_shape=(jax.ShapeDtypeStruct((t, d), F32), jax.ShapeDtypeStruct((8, d), F32)),
        grid=(t // tm, nk),
        in_specs=in_specs,
        out_specs=(row, pl.BlockSpec((8, d), lambda i, k: (0, 0))),
        scratch_shapes=[pltpu.VMEM((tm, d), F32)],
        compiler_params=_cparams(dimension_semantics=("arbitrary", "arbitrary")),
        name=name,
    )(*args)


def _rope_bwd(dq, tabs, *, tm, name):
    t, n = dq.shape
    s_blocks = SEQ // tm

    def body(d_ref, c_ref, s1_ref, s2_ref, o_ref):
        o_ref[...] = _rope_apply(d_ref[...], c_ref[...], s1_ref[...], s2_ref[...], transpose=True).astype(o_ref.dtype)

    tab_spec = pl.BlockSpec((tm, LANES), lambda i: (i % s_blocks, 0))
    return pl.pallas_call(
        body,
        out_shape=jax.ShapeDtypeStruct((t, n), BF16),
        grid=(t // tm,),
        in_specs=[pl.BlockSpec((tm, n), lambda i: (i, 0)), tab_spec, tab_spec, tab_spec],
        out_specs=pl.BlockSpec((tm, n), lambda i: (i, 0)),
        compiler_params=_cparams(),
        name=name,
    )(dq, *tabs)


def _rearrange_w_in(w):
    offs = [0]
    for s in IN_SIZES:
        offs.append(offs[-1] + s)
    pieces = [w[:, offs[i]:offs[i + 1]] for i in range(len(IN_SIZES))]
    flog = jnp.pad(pieces[4], ((0, 0), (0, LANES - FOX_HEADS)))
    return jnp.concatenate(pieces[:4] + pieces[5:] + [flog], axis=1)


def _restore_w_in_cols(g):
    return jnp.concatenate([g[:, :C_DQ], g[:, PW:PW + FOX_HEADS], g[:, C_DQ:PW]], axis=1)


def _local_grads(x, mem, norm_g, w_r, b_forget, mem_norm_g, w_kv, w_o, final_norm_g, tgt):
    nb = x.shape[0]
    t = nb * SEQ
    x2d = x.reshape(t, D_MODEL)
    tgt2d = tgt.reshape(t, D_MODEL)
    tabs = _rope_tables()
    bpad = jnp.pad(b_forget.reshape(1, FOX_HEADS), ((0, 0), (0, LANES - FOX_HEADS)))

    h = _rms_fwd(x2d, norm_g.reshape(1, D_MODEL), tm=512, name="rms_x")
    p16 = _proj(h, w_r[:, :PW], tabs, tm=1024, tn=512, name="proj")
    flog = _matmul(h, w_r[:, PW:], out_dtype=F32, tm=1024, tn=LANES, tk=D_MODEL, name="proj_flog")
    c12, cb = _flog_fwd(flog, bpad, nb=nb, ts=256, name="flog_fwd")

    fcfg = _fox_cfg()
    crow = c12[:, :FOX_HEADS].reshape(nb, fcfg.nk, fcfg.tk, fcfg.ncol, 2).transpose(0, 3, 1, 4, 2)
    crow = jnp.pad(crow, ((0, 0), (0, 0), (0, 0), (0, 6), (0, 0)))
    p3 = p16.reshape(nb, SEQ, PW)
    cb3 = cb.reshape(nb, SEQ, FOX_W)
    fox, fox_lse = _attn_fwd(fcfg, p3, p3, p3, out_cols=FOX_W, bias=(cb3, crow), name="fox_fwd")

    state = None
    for idx, (_, d) in enumerate(DILATIONS):
        cfg = _dil_cfg(d)
        pv = p16.reshape(nb, SEQ // d, d * PW)
        last = idx == len(DILATIONS) - 1
        if state is not None:
            state = tuple(s.reshape(nb, SEQ // d, d * DIL_W) for s in state)
        state = _attn_fwd(cfg, pv, pv, pv, out_cols=d * DIL_W, state=state, finalize=last, name=f"dil{d}_fwd")
    dil, dil_lse = (s.reshape(nb, SEQ, DIL_W) for s in state)

    mh = _rms_fwd(mem.reshape(nb * MEM_LEN, D_MODEL), mem_norm_g.reshape(1, D_MODEL), tm=nb * MEM_LEN, name="rms_mem")
    mkv = _matmul(mh, w_kv, out_dtype=BF16, tm=nb * MEM_LEN, tn=512, tk=D_MODEL, name="mem_kv")
    mkv3 = mkv.reshape(nb, MEM_LEN, 2 * MEM_W)
    mcfg = _mem_cfg()
    memo, mem_lse = _attn_fwd(mcfg, p3, mkv3, mkv3, out_cols=MEM_W, name="mem_fwd")

    fox2, dil2, memo2 = fox.reshape(t, FOX_W), dil.reshape(t, DIL_W), memo.reshape(t, MEM_W)
    y = _gate_fwd(fox2, dil2, memo2, p16, tm=1024, name="gate_fwd")
    dx2, dx2b, st = _out_loss(y, w_o, x2d, tgt2d, final_norm_g.reshape(1, D_MODEL), tm=512, name="out_loss")

    g_wo = _matmul(y.T, dx2b, out_dtype=F32, tm=1024, tn=512, tk=1024, name="grad_w_out")
    dy = _matmul(dx2b, w_o.T, out_dtype=F32, tm=1024, tn=512, tk=D_MODEL, name="d_y")
    datt, dgate = _gate_bwd(dy, fox2, dil2, memo2, p16, tm=1024, name="gate_bwd")
    dfox = datt[:, :FOX_W].reshape(nb, SEQ, FOX_W)
    ddil = datt[:, FOX_W:FOX_W + DIL_W].reshape(nb, SEQ, DIL_W)
    dmemo = datt[:, FOX_W + DIL_W:].reshape(nb, SEQ, MEM_W)

    dfq, dfk, dfv, dcr = _attn_bwd(fcfg, p3, p3, p3, dfox, fox, fox_lse, out_cols=FOX_W, kv_cols=FOX_W,
                                    bias=(cb3, crow), name="fox_bwd")
    dcol = -dcr[:, :, :, :2, :].transpose(0, 2, 4, 1, 3).reshape(t, FOX_HEADS)
    dcol = jnp.pad(dcol, ((0, 0), (0, LANES - FOX_HEADS)))
    dflog, gb = _flog_bwd(dcol, flog, bpad, nb=nb, ts=256, name="flog_bwd")

    acc = None
    for _, d in DILATIONS:
        cfg = _dil_cfg(d)
        l = SEQ // d
        pv = p16.reshape(nb, l, d * PW)
        view = lambda a: a.reshape(nb, l, d * DIL_W)
        if acc is not None:
            acc = tuple(view(a) for a in acc)
        acc = _attn_bwd(cfg, pv, pv, pv, view(ddil), view(dil), view(dil_lse), out_cols=d * DIL_W, kv_cols=d * DIL_W,
                        acc=acc, name=f"dil{d}_bwd")
    ddq, ddk, ddv = (a.reshape(t, DIL_W) for a in acc)
    ddq = _rope_bwd(ddq, tabs, tm=512, name="rope_bwd_q")
    ddk = _rope_bwd(ddk, tabs, tm=512, name="rope_bwd_k")

    dmq, dmk, dmv = _attn_bwd(mcfg, p3, mkv3, mkv3, dmemo, memo, mem_lse, out_cols=MEM_W, kv_cols=MEM_W, name="mem_bwd")
    dmkv = jnp.concatenate([dmk, dmv], axis=-1).reshape(nb * MEM_LEN, 2 * MEM_W).astype(BF16)
    g_wkv = _matmul(mh.T, dmkv, out_dtype=F32, tm=512, tn=512, tk=nb * MEM_LEN, name="grad_w_kv")
    _, gmn = _dh_rms_bwd(dmkv, w_kv.T, mem.reshape(nb * MEM_LEN, D_MODEL), mem_norm_g.reshape(1, D_MODEL), None,
                         tm=nb * MEM_LEN, tk=2 * MEM_W, name="mem_rms_bwd")

    bf = lambda a: a.reshape(t, -1).astype(BF16)
    dp = jnp.concatenate([bf(dfq), bf(dfk), bf(dfv), dgate[:, :FOX_W], ddq, ddk, bf(ddv), dgate[:, FOX_W:FOX_W + DIL_W],
                          bf(dmq), dgate[:, FOX_W + DIL_W:], dflog], axis=1)
    g_wr = _matmul(h.T, dp, out_dtype=F32, tm=512, tn=PWF // 3, tk=1024, name="grad_w_in")
    gx, gng = _dh_rms_bwd(dp, w_r.T, x2d, norm_g.reshape(1, D_MODEL), dx2, tm=512, tk=PWF // 3, name="in_rms_bwd")

    gb_row = jnp.pad(gb[0:1, :], ((0, 0), (0, D_MODEL - LANES)))
    small = jnp.concatenate([gng[0:1], gmn[0:1], st[0:1], gb_row, st[1:2], jnp.zeros((3, D_MODEL), F32)], axis=0)
    return gx.reshape(nb, SEQ, D_MODEL), g_wr, g_wkv, g_wo, small


MESH = pl.DeviceIdType.MESH
ANY = pl.BlockSpec(memory_space=pl.ANY)


def _place():
    x, y, c = lax.axis_index("x"), lax.axis_index("y"), lax.axis_index("c")
    other_chips = [(1 - x, y), (x, 1 - y), (1 - x, 1 - y)]
    return x, y, c, other_chips


def _gather_weights(shards):
    n = len(shards)

    def body(*refs):
        in_refs, out_refs = refs[:n], refs[n:2 * n]
        send_sems, recv_sems = refs[2 * n:]
        x, y, c, chips = _place()
        me_chip = 2 * x + y
        sibling = (x, y, 1 - c)

        def half(ref, pc, rows):
            return ref.at[pl.ds(pc * (rows // 2), rows // 2), :]

        def rcopy(k, src, dst, to):
            return pltpu.make_async_remote_copy(src_ref=src, dst_ref=dst, send_sem=send_sems.at[k], recv_sem=recv_sems.at[k],
                                                device_id=to, device_id_type=MESH)

        sends = []
        for t in range(n):
            rows = shards[t].shape[0]
            for j, chip in enumerate(chips):
                cp = rcopy(6 * t + j, half(in_refs[t], c, rows), half(out_refs[t].at[me_chip], c, rows), (*chip, c))
                cp.start()
                sends.append(cp)
        for t in range(n):
            rows = shards[t].shape[0]
            for j, chip in enumerate(chips):
                slot = out_refs[t].at[2 * chip[0] + chip[1]]
                rcopy(6 * t + j, half(slot, c, rows), half(slot, c, rows), sibling).wait_recv()
                fw = rcopy(6 * t + 3 + j, half(slot, c, rows), half(slot, c, rows), sibling)
                fw.start()
                sends.append(fw)
        for t in range(n):
            rows = shards[t].shape[0]
            for j, chip in enumerate(chips):
                slot = out_refs[t].at[2 * chip[0] + chip[1]]
                rcopy(6 * t + 3 + j, half(slot, 1 - c, rows), half(slot, 1 - c, rows), sibling).wait_recv()
        for cp in sends:
            cp.wait_send()

    return pl.pallas_call(
        body,
        out_shape=tuple(jax.ShapeDtypeStruct((N_CHIPS,) + s.shape, s.dtype) for s in shards),
        in_specs=[ANY] * n,
        out_specs=tuple([ANY] * n),
        scratch_shapes=[pltpu.SemaphoreType.DMA((6 * n,)), pltpu.SemaphoreType.DMA((6 * n,))],
        name="gather_weights",
    )(*shards)


def _pair_exchange(gs):
    n = len(gs)

    def body(*refs):
        g_refs, r_refs = refs[:n], refs[n:2 * n]
        send_sems, recv_sems = refs[2 * n:]
        x, y, c, _ = _place()
        cps = []
        for t in range(n):
            hr = gs[t].shape[1] // 2
            cp = pltpu.make_async_remote_copy(src_ref=g_refs[t].at[:, pl.ds((1 - c) * hr, hr), :], dst_ref=r_refs[t],
                                              send_sem=send_sems.at[t], recv_sem=recv_sems.at[t],
                                              device_id=(x, y, 1 - c), device_id_type=MESH)
            cp.start()
            cps.append(cp)
        for cp in cps:
            cp.wait()

    return pl.pallas_call(
        body,
        out_shape=tuple(jax.ShapeDtypeStruct((N_CHIPS, g.shape[1] // 2, g.shape[2]), g.dtype) for g in gs),
        in_specs=[ANY] * n,
        out_specs=tuple([ANY] * n),
        scratch_shapes=[pltpu.SemaphoreType.DMA((n,)), pltpu.SemaphoreType.DMA((n,))],
        name="pair_exchange",
    )(*gs)


def _chip_exchange(ps):
    n = len(ps)

    def body(*refs):
        p_refs, o_refs = refs[:n], refs[n:2 * n]
        send_sems, recv_sems = refs[2 * n:]
        x, y, c, chips = _place()
        me_chip = 2 * x + y
        cps = []
        for t in range(n):
            for j, chip in enumerate(chips):
                cp = pltpu.make_async_remote_copy(src_ref=p_refs[t].at[2 * chip[0] + chip[1]], dst_ref=o_refs[t].at[me_chip],
                                                  send_sem=send_sems.at[3 * t + j], recv_sem=recv_sems.at[3 * t + j],
                                                  device_id=(*chip, c), device_id_type=MESH)
                cp.start()
                cps.append(cp)
        for cp in cps:
            cp.wait()

    return pl.pallas_call(
        body,
        out_shape=tuple(jax.ShapeDtypeStruct(p.shape, p.dtype) for p in ps),
        in_specs=[ANY] * n,
        out_specs=tuple([ANY] * n),
        scratch_shapes=[pltpu.SemaphoreType.DMA((3 * n,)), pltpu.SemaphoreType.DMA((3 * n,))],
        name="chip_exchange",
    )(*ps)


def _pair_swap(rs):
    n = len(rs)

    def body(*refs):
        r_refs, o_refs = refs[:n], refs[n:2 * n]
        send_sems, recv_sems = refs[2 * n:]
        x, y, c, _ = _place()
        cps = []
        for t in range(n):
            cp = pltpu.make_async_remote_copy(src_ref=r_refs[t], dst_ref=o_refs[t], send_sem=send_sems.at[t],
                                              recv_sem=recv_sems.at[t], device_id=(x, y, 1 - c), device_id_type=MESH)
            cp.start()
            cps.append(cp)
        for cp in cps:
            cp.wait()

    return pl.pallas_call(
        body,
        out_shape=tuple(jax.ShapeDtypeStruct(r.shape, r.dtype) for r in rs),
        in_specs=[ANY] * n,
        out_specs=tuple([ANY] * n),
        scratch_shapes=[pltpu.SemaphoreType.DMA((n,)), pltpu.SemaphoreType.DMA((n,))],
        name="pair_swap",
    )(*rs)


N_DEV = 8
LOSS_ROW = 4


def _small_allreduce(small):
    def body(s_ref, o_ref, all_ref, send_sems, recv_sems):
        x, y, c, _ = _place()
        me = 4 * x + 2 * y + c
        all_ref[me] = s_ref[...]
        cps = []
        for k in range(1, N_DEV):
            peer = tuple(1 - p if (k >> s) & 1 else p for p, s in ((x, 2), (y, 1), (c, 0)))
            cp = pltpu.make_async_remote_copy(src_ref=s_ref, dst_ref=all_ref.at[me], send_sem=send_sems.at[k - 1],
                                              recv_sem=recv_sems.at[k - 1], device_id=peer, device_id_type=MESH)
            cp.start()
            cps.append(cp)
        for cp in cps:
            cp.wait()
        tot = all_ref[0]
        for d in range(1, N_DEV):
            tot = tot + all_ref[d]
        o_ref[...] = tot
        o_ref[LOSS_ROW:LOSS_ROW + 1, :] = jnp.broadcast_to(jnp.sum(tot[LOSS_ROW:LOSS_ROW + 1, :], axis=1, keepdims=True),
                                                          (1, tot.shape[1]))

    vm = pl.BlockSpec(memory_space=pltpu.VMEM)
    return pl.pallas_call(
        body,
        out_shape=jax.ShapeDtypeStruct(small.shape, small.dtype),
        in_specs=[vm],
        out_specs=vm,
        scratch_shapes=[pltpu.VMEM((N_DEV,) + small.shape, small.dtype), pltpu.SemaphoreType.DMA((N_DEV - 1,)),
                        pltpu.SemaphoreType.DMA((N_DEV - 1,))],
        name="small_allreduce",
    )(small)


def _sum_pair(g, recv, cidx, *, tr, name):
    _, hr, cols = recv.shape
    nr = hr // tr

    def body(c_ref, g_ref, r_ref, o_ref):
        o_ref[...] = (g_ref[...] + r_ref[...]).astype(o_ref.dtype)

    grid_spec = pltpu.PrefetchScalarGridSpec(
        num_scalar_prefetch=1,
        grid=(N_CHIPS, nr),
        in_specs=[pl.BlockSpec((None, tr, cols), lambda k, i, c_ref: (k, c_ref[0] * nr + i, 0)),
                  pl.BlockSpec((None, tr, cols), lambda k, i, c_ref: (k, i, 0))],
        out_specs=pl.BlockSpec((None, tr, cols), lambda k, i, c_ref: (k, i, 0)),
    )
    return pl.pallas_call(body, out_shape=jax.ShapeDtypeStruct(recv.shape, BF16), grid_spec=grid_spec,
                          compiler_params=_cparams(), name=name)(cidx, g, recv)


def _sum_chips(p, *, tr, name):
    _, rows, cols = p.shape

    def body(p_ref, o_ref):
        tot = p_ref[0].astype(F32)
        for k in range(1, N_CHIPS):
            tot = tot + p_ref[k].astype(F32)
        o_ref[...] = tot

    return pl.pallas_call(
        body,
        out_shape=jax.ShapeDtypeStruct((rows, cols), F32),
        grid=(rows // tr,),
        in_specs=[pl.BlockSpec((N_CHIPS, tr, cols), lambda i: (0, i, 0))],
        out_specs=pl.BlockSpec((tr, cols), lambda i: (i, 0)),
        compiler_params=_cparams(),
        name=name,
    )(p)


def _adamw(w, g, m, v, *, tr, name):
    rows, cols = w.shape
    bc1 = 1.0 / (1.0 - ADAM_B1 ** ADAM_STEP)
    bc2 = 1.0 / (1.0 - ADAM_B2 ** ADAM_STEP)

    def body(w_ref, g_ref, m_ref, v_ref, d_ref, nm_ref, nv_ref):
        gv = g_ref[...]
        nm = ADAM_B1 * m_ref[...] + (1.0 - ADAM_B1) * gv
        nv = ADAM_B2 * v_ref[...] + (1.0 - ADAM_B2) * (gv * gv)
        d_ref[...] = -ADAM_LR * ((nm * bc1) / (jnp.sqrt(nv * bc2) + ADAM_EPS) + ADAM_WD * w_ref[...])
        nm_ref[...] = nm
        nv_ref[...] = nv

    spec = pl.BlockSpec((tr, cols), lambda i: (i, 0))
    sd = jax.ShapeDtypeStruct((rows, cols), F32)
    return pl.pallas_call(body, out_shape=(sd, sd, sd), grid=(rows // tr,), in_specs=[spec] * 4, out_specs=(spec,) * 3,
                          compiler_params=_cparams(), name=name)(w, g, m, v)


def _pack_small(norm, mem_norm, final_norm, b_forget):
    rows = [norm.reshape(1, D_MODEL), mem_norm.reshape(1, D_MODEL), final_norm.reshape(1, D_MODEL),
            jnp.pad(b_forget.reshape(1, FOX_HEADS), ((0, 0), (0, D_MODEL - FOX_HEADS))), jnp.zeros((4, D_MODEL), F32)]
    return jnp.concatenate(rows, axis=0)


def _unpack_small(a):
    return a[0:1], a[3:4, :FOX_HEADS], a[1:2], a[2]


def kernel(x, mem, norm_g, w_in, b_forget, mem_norm_g, w_mem_kv, w_out, final_norm_g, loss_target, m_norm_g, m_w_in, m_b_forget, m_mem_norm_g, m_w_mem_kv, m_w_out, m_final_norm_g, v_norm_g, v_w_in, v_b_forget, v_mem_norm_g, v_w_mem_kv, v_w_out, v_final_norm_g):
    core = lax.axis_index("c").astype(jnp.int32)
    me_chip = (2 * lax.axis_index("x") + lax.axis_index("y")).astype(jnp.int32)
    cidx = core.reshape(1)

    def own_slot(arr, own):
        return lax.dynamic_update_slice(arr, own[None].astype(arr.dtype), (me_chip,) + (0,) * own.ndim)

    mine = [w_in[0].astype(BF16), w_mem_kv[0].astype(BF16), w_out[0].astype(BF16)]
    g_in, g_kv, g_out = (own_slot(g, s) for g, s in zip(_gather_weights(mine), mine))
    w_r = _rearrange_w_in(jnp.concatenate([g_in[k] for k in range(N_CHIPS)], axis=1))
    w_kv = g_kv.reshape(D_MODEL, 2 * MEM_W)
    w_o = g_out.reshape(MIX_W, D_MODEL)

    gx, g_wr, g_wkv, g_wo, small = _local_grads(x, mem, norm_g, w_r, b_forget, mem_norm_g, w_kv, w_o, final_norm_g, loss_target)

    shard_w = IN_W // N_CHIPS
    slabs = [_restore_w_in_cols(g_wr).reshape(D_MODEL, N_CHIPS, shard_w).transpose(1, 0, 2),
             g_wkv.reshape(N_CHIPS, D_MODEL // N_CHIPS, 2 * MEM_W),
             g_wo.reshape(N_CHIPS, MIX_W // N_CHIPS, D_MODEL)]
    trs = (128, 128, 256)
    names = ("w_in", "w_mem_kv", "w_out")
    recv = _pair_exchange(slabs)
    pair = [_sum_pair(g, r, cidx, tr=tr, name=f"sum_pair_{nm}") for g, r, tr, nm in zip(slabs, recv, trs, names)]
    got = [lax.dynamic_update_slice(g, lax.dynamic_slice(p, (me_chip, 0, 0), (1,) + p.shape[1:]), (me_chip, 0, 0))
           for g, p in zip(_chip_exchange(pair), pair)]
    red = [_sum_chips(p, tr=tr, name=f"sum_chips_{nm}") for p, tr, nm in zip(got, trs, names)]
    sib = _pair_swap(red)
    grads = [jnp.where(core == 0, jnp.concatenate([r, s], axis=0), jnp.concatenate([s, r], axis=0)) for r, s in zip(red, sib)]

    outs = {}
    for nm, g, w, m, v, tr in zip(names, grads, (w_in, w_mem_kv, w_out), (m_w_in, m_w_mem_kv, m_w_out),
                                  (v_w_in, v_w_mem_kv, v_w_out), trs):
        d, nmo, nvo = _adamw(w[0], g, m[0], v[0], tr=tr, name=f"adamw_{nm}")
        outs[nm] = tuple(a[None] for a in (g, d, nmo, nvo))

    gsum = _small_allreduce(small)
    sd, sm, sv = _adamw(_pack_small(norm_g, mem_norm_g, final_norm_g, b_forget), gsum,
                        _pack_small(m_norm_g, m_mem_norm_g, m_final_norm_g, m_b_forget),
                        _pack_small(v_norm_g, v_mem_norm_g, v_final_norm_g, v_b_forget), tr=8, name="adamw_small")
    loss = gsum[LOSS_ROW, 0]

    def group(i, small_arr):
        ng, bf, mg, fg = _unpack_small(small_arr)
        return (ng, outs["w_in"][i], bf, mg, outs["w_mem_kv"][i], outs["w_out"][i], fg)

    return (loss, gx, *group(0, gsum), *group(1, sd), *group(2, sm), *group(3, sv))
```

```python
import functools
import math

import jax
import jax.numpy as jnp
from jax import lax
from jax.experimental import pallas as pl
from jax.experimental.pallas import tpu as pltpu

F32 = jnp.float32
BF16 = jnp.bfloat16

D_MODEL = 1024
SEQ = 2048
HEAD_DIM = 64
FOX_HEADS = 12
DIL_HEADS = 12
MEM_HEADS = 4
MEM_HEAD_DIM = 128
MEM_LEN = 256
FOX_W = FOX_HEADS * HEAD_DIM
DIL_W = DIL_HEADS * HEAD_DIM
MEM_W = MEM_HEADS * MEM_HEAD_DIM
MIX_W = FOX_W + DIL_W + MEM_W
DILATIONS = ((128, 1), (512, 4), (2048, 16))
ROPE_THETA = 500000.0
ROPE_DIM = HEAD_DIM // 4
RMS_EPS = 1e-6
NEG_INF = -1e30
IN_SIZES = [FOX_W] * 4 + [FOX_HEADS] + [DIL_W] * 4 + [MEM_W] * 2
IN_W = sum(IN_SIZES)

ADAM_LR = 0.001
ADAM_B1 = 0.9
ADAM_B2 = 0.999
ADAM_EPS = 1e-08
ADAM_WD = 0.01
ADAM_STEP = 10

LANES = 128
N_CHIPS = 4
PW = 7168
PWF = PW + LANES
C_FQ, C_FK, C_FV, C_FG = 0, 768, 1536, 2304
C_DQ, C_DK, C_DV, C_DG = 3072, 3840, 4608, 5376
C_MQ, C_MG = 6144, 6656
VMEM_LIMIT = 48 * 1024 * 1024


def _cparams(**kw):
    return pltpu.CompilerParams(vmem_limit_bytes=VMEM_LIMIT, **kw)


def _matmul(a, b, *, out_dtype, tm, tn, tk, name):
    m, kdim = a.shape
    _, n = b.shape
    nk = kdim // tk
    assert m % tm == 0 and n % tn == 0 and kdim % tk == 0

    def body(a_ref, b_ref, o_ref, acc_ref):
        k = pl.program_id(2)

        @pl.when(k == 0)
        def _():
            acc_ref[...] = jnp.zeros_like(acc_ref)

        acc_ref[...] += jnp.dot(a_ref[...], b_ref[...], preferred_element_type=F32)

        @pl.when(k == nk - 1)
        def _():
            o_ref[...] = acc_ref[...].astype(o_ref.dtype)

    return pl.pallas_call(
        body,
        out_shape=jax.ShapeDtypeStruct((m, n), out_dtype),
        grid=(m // tm, n // tn, nk),
        in_specs=[pl.BlockSpec((tm, tk), lambda i, j, k: (i, k)), pl.BlockSpec((tk, tn), lambda i, j, k: (k, j))],
        out_specs=pl.BlockSpec((tm, tn), lambda i, j, k: (i, j)),
        scratch_shapes=[pltpu.VMEM((tm, tn), F32)],
        compiler_params=_cparams(dimension_semantics=("parallel", "parallel", "arbitrary")),
        name=name,
    )(a, b)


def _rms_fwd(x, g, *, tm, name):
    t, d = x.shape

    def body(x_ref, g_ref, h_ref):
        xv = x_ref[...]
        r = lax.rsqrt(jnp.mean(xv * xv, axis=-1, keepdims=True) + RMS_EPS)
        h_ref[...] = (xv * r * g_ref[...]).astype(h_ref.dtype)

    return pl.pallas_call(
        body,
        out_shape=jax.ShapeDtypeStruct((t, d), BF16),
        grid=(t // tm,),
        in_specs=[pl.BlockSpec((tm, d), lambda i: (i, 0)), pl.BlockSpec((1, d), lambda i: (0, 0))],
        out_specs=pl.BlockSpec((tm, d), lambda i: (i, 0)),
        compiler_params=_cparams(),
        name=name,
    )(x, g)


def _rope_tables():
    half = ROPE_DIM // 2
    pos = jnp.arange(SEQ, dtype=F32)
    inv_freq = 1.0 / (ROPE_THETA ** (jnp.arange(0, ROPE_DIM, 2, dtype=F32) / ROPE_DIM))
    ang = pos[:, None] * inv_freq[None, :]
    cos, sin = jnp.cos(ang), jnp.sin(ang)
    one = jnp.ones((SEQ, HEAD_DIM - ROPE_DIM), F32)
    zero = jnp.zeros((SEQ, HEAD_DIM - ROPE_DIM), F32)
    zh = jnp.zeros((SEQ, half), F32)
    c = jnp.concatenate([cos, cos, one], axis=1)
    s1 = jnp.concatenate([zh, sin, zero], axis=1)
    s2 = jnp.concatenate([-sin, zh, zero], axis=1)
    rep = LANES // HEAD_DIM
    return jnp.tile(c, (1, rep)), jnp.tile(s1, (1, rep)), jnp.tile(s2, (1, rep))


def _rope_apply(t, c, s1, s2, transpose=False):
    n = t.shape[-1]
    rep = n // LANES
    c, s1, s2 = (jnp.tile(u, (1, rep)) for u in (c, s1, s2))
    half = ROPE_DIM // 2
    if not transpose:
        return t * c + pltpu.roll(t, half, 1) * s1 + pltpu.roll(t, n - half, 1) * s2
    return t * c + pltpu.roll(t * s1, n - half, 1) + pltpu.roll(t * s2, half, 1)


def _proj(h, w, tabs, *, tm, tn, name):
    t, d = h.shape
    n = w.shape[1]
    assert C_DQ % tn == 0 and (C_DV - C_DQ) % tn == 0 and (C_DG - C_DQ) % tn == 0
    rope_lo, rope_hi, dil_hi = C_DQ // tn, C_DV // tn, C_DG // tn
    s_blocks = SEQ // tm

    def body(h_ref, w_ref, c_ref, s1_ref, s2_ref, o_ref, f_ref):
        j = pl.program_id(1)
        acc = jnp.dot(h_ref[...], w_ref[...], preferred_element_type=F32)
        is_rope = jnp.logical_and(j >= rope_lo, j < rope_hi)

        @pl.when(is_rope)
        def _():
            r = _rope_apply(acc, c_ref[...], s1_ref[...], s2_ref[...])
            o_ref[...] = r.astype(o_ref.dtype)
            f_ref[...] = r

        @pl.when(jnp.logical_not(is_rope))
        def _():
            o_ref[...] = acc.astype(o_ref.dtype)

        @pl.when(jnp.logical_and(j >= rope_hi, j < dil_hi))
        def _():
            f_ref[...] = acc

    tab_spec = pl.BlockSpec((tm, LANES), lambda i, j: (i % s_blocks, 0))
    f_spec = pl.BlockSpec((tm, tn), lambda i, j: (i, jnp.clip(j - rope_lo, 0, dil_hi - rope_lo - 1)))
    return pl.pallas_call(
        body,
        out_shape=(jax.ShapeDtypeStruct((t, n), BF16), jax.ShapeDtypeStruct((t, 3 * DIL_W), F32)),
        grid=(t // tm, n // tn),
        in_specs=[pl.BlockSpec((tm, d), lambda i, j: (i, 0)), pl.BlockSpec((d, tn), lambda i, j: (0, j)),
                  tab_spec, tab_spec, tab_spec],
        out_specs=(pl.BlockSpec((tm, tn), lambda i, j: (i, j)), f_spec),
        compiler_params=_cparams(dimension_semantics=("parallel", "arbitrary")),
        name=name,
    )(h, w, *tabs)


def _split3(x):
    hi = x.astype(BF16)
    r1 = x - hi.astype(F32)
    mid = r1.astype(BF16)
    lo = (r1 - mid.astype(F32)).astype(BF16)
    return hi, mid, lo


def _dot3(sel, x, sel_is_lhs):
    out = None
    for piece in _split3(x):
        t = jnp.dot(sel, piece, preferred_element_type=F32) if sel_is_lhs else jnp.dot(piece, sel, preferred_element_type=F32)
        out = t if out is None else out + t
    return out


def _head_expand_matrix():
    r = lax.broadcasted_iota(jnp.int32, (LANES, FOX_W), 0)
    c = lax.broadcasted_iota(jnp.int32, (LANES, FOX_W), 1)
    return jnp.where(c // HEAD_DIM == r, 1.0, 0.0).astype(BF16)


def _flog_fwd(flog, bpad, *, nb, ts, name):
    ns = SEQ // ts

    def body(f_ref, b_ref, c_ref, cb_ref, carry_ref):
        s = pl.program_id(1)

        @pl.when(s == 0)
        def _():
            carry_ref[...] = jnp.zeros_like(carry_ref)

        z = f_ref[...] + b_ref[...]
        logf = jnp.minimum(z, 0.0) - jnp.log(1.0 + jnp.exp(-jnp.abs(z)))
        r = lax.broadcasted_iota(jnp.int32, (ts, ts), 0)
        c = lax.broadcasted_iota(jnp.int32, (ts, ts), 1)
        tri = jnp.where(r >= c, 1.0, 0.0).astype(BF16)
        cs = _dot3(tri, logf, True) + carry_ref[0:1, :]
        carry_ref[...] = jnp.broadcast_to(cs[ts - 1:ts, :], carry_ref.shape)
        c_ref[...] = cs
        cb_ref[...] = _dot3(_head_expand_matrix(), cs, False)

    return pl.pallas_call(
        body,
        out_shape=(jax.ShapeDtypeStruct((nb * SEQ, LANES), F32), jax.ShapeDtypeStruct((nb * SEQ, FOX_W), F32)),
        grid=(nb, ns),
        in_specs=[pl.BlockSpec((ts, LANES), lambda b, s: (b * ns + s, 0)), pl.BlockSpec((1, LANES), lambda b, s: (0, 0))],
        out_specs=(pl.BlockSpec((ts, LANES), lambda b, s: (b * ns + s, 0)), pl.BlockSpec((ts, FOX_W), lambda b, s: (b * ns + s, 0))),
        scratch_shapes=[pltpu.VMEM((8, LANES), F32)],
        compiler_params=_cparams(dimension_semantics=("parallel", "arbitrary")),
        name=name,
    )(flog, bpad)


def _flog_bwd(dcol, flog, bpad, *, nb, ts, name):
    ns = SEQ // ts

    def body(d_ref, f_ref, b_ref, o_ref, gb_ref, carry_ref):
        bi = pl.program_id(0)
        s = pl.program_id(1)

        @pl.when(s == 0)
        def _():
            carry_ref[...] = jnp.zeros_like(carry_ref)

        @pl.when(jnp.logical_and(bi == 0, s == 0))
        def _():
            gb_ref[...] = jnp.zeros_like(gb_ref)

        r = lax.broadcasted_iota(jnp.int32, (ts, ts), 0)
        c = lax.broadcasted_iota(jnp.int32, (ts, ts), 1)
        tri = jnp.where(r <= c, 1.0, 0.0).astype(BF16)
        rc = _dot3(tri, d_ref[...], True) + carry_ref[0:1, :]
        carry_ref[...] = jnp.broadcast_to(rc[0:1, :], carry_ref.shape)
        z = f_ref[...] + b_ref[...]
        dz = rc / (1.0 + jnp.exp(z))
        o_ref[...] = dz.astype(o_ref.dtype)
        gb_ref[...] += jnp.broadcast_to(jnp.sum(dz, axis=0, keepdims=True), gb_ref.shape)

    rev = lambda b, s: (b * ns + (ns - 1 - s), 0)
    return pl.pallas_call(
        body,
        out_shape=(jax.ShapeDtypeStruct((nb * SEQ, LANES), BF16), jax.ShapeDtypeStruct((8, LANES), F32)),
        grid=(nb, ns),
        in_specs=[pl.BlockSpec((ts, LANES), rev), pl.BlockSpec((ts, LANES), rev), pl.BlockSpec((1, LANES), lambda b, s: (0, 0))],
        out_specs=(pl.BlockSpec((ts, LANES), rev), pl.BlockSpec((8, LANES), lambda b, s: (0, 0))),
        scratch_shapes=[pltpu.VMEM((8, LANES), F32)],
        compiler_params=_cparams(dimension_semantics=("arbitrary", "arbitrary")),
        name=name,
    )(dcol, flog, bpad)


class _AttnCfg:
    def __init__(self, *, e, tq, tk, lq, lk, causal, window, ncol, qcol, kcol, vcol, split_p=False):
        self.e, self.tq, self.tk, self.lq, self.lk = e, tq, tk, lq, lk
        self.split_p = split_p
        self.causal, self.window = causal, window
        self.ncol, self.qcol, self.kcol, self.vcol = ncol, qcol, kcol, vcol
        self.nh = LANES // e
        self.scale = 1.0 / math.sqrt(e)
        self.nq, self.nk = lq // tq, lk // tk

    def k_range(self, i):
        if not self.causal:
            return 0, self.nk
        hi = ((i + 1) * self.tq - 1) // self.tk + 1
        if self.window is None:
            return 0, hi
        return jnp.maximum((i * self.tq - self.window) // self.tk, 0), hi


def _head_masks(nh):
    lane = lax.broadcasted_iota(jnp.int32, (1, LANES), 1)
    return [None] if nh == 1 else [lane < HEAD_DIM, lane >= HEAD_DIM]


def _sel(mask, a, b):
    return a if mask is None else jnp.where(mask, a, b)


def _scores(cfg, qh, kb, q0, k0, dlt0, bias):
    s = lax.dot_general(qh, kb, (((1,), (1,)), ((), ())), preferred_element_type=F32) * cfg.scale
    if bias is not None:
        s = s + bias
    if cfg.causal:
        d = dlt0 + (q0 - k0)
        if cfg.window is None:
            ok = d >= 0
        else:
            ok = d.astype(jnp.uint32) <= jnp.uint32(cfg.window)
        s = jnp.where(ok, s, NEG_INF)
    return s


def _attn_fwd(cfg, q, k, v, *, out_cols, bias=None, state=None, finalize=True, name):
    g = q.shape[0]
    tq, tk, e, nh = cfg.tq, cfg.tk, cfg.e, cfg.nh

    def body(*refs):
        refs = list(refs)
        q_ref, k_ref, v_ref = refs[:3]
        del refs[:3]
        if bias is not None:
            cb_ref, cr_ref = refs[:2]
            del refs[:2]
        if state is not None:
            ai_ref, mi_ref, li_ref = refs[:3]
            del refs[:3]
        out_refs = refs
        masks = _head_masks(nh)
        dlt0 = lax.broadcasted_iota(jnp.int32, (tq, tk), 0) - lax.broadcasted_iota(jnp.int32, (tq, tk), 1)

        def qbody(i, carry):
            q0 = pl.multiple_of(i * tq, tq)
            rows = pl.ds(q0, tq)
            qb = q_ref[rows, :]
            lo, hi = cfg.k_range(i)
            res = []
            for h in range(nh):
                qh = _sel(masks[h], qb, jnp.zeros_like(qb))
                if state is not None:
                    m0 = mi_ref[rows, h * e:h * e + 1]
                    l0 = li_ref[rows, h * e:h * e + 1]
                    a0 = ai_ref[rows, :]
                else:
                    m0 = jnp.full((tq, 1), NEG_INF, F32)
                    l0 = jnp.zeros((tq, 1), F32)
                    a0 = jnp.zeros((tq, LANES), F32)
                cq = cb_ref[rows, h * e:h * e + 1] if bias is not None else None

                def kbody(jk, c, qh=qh, cq=cq, h=h):
                    m, l, a = c
                    k0 = pl.multiple_of(jk * tk, tk)
                    kb = k_ref[pl.ds(k0, tk), :]
                    vb = v_ref[pl.ds(k0, tk), :]
                    b = (cq - cr_ref[jk, h:h + 1, :]) if bias is not None else None
                    s = _scores(cfg, qh, kb, q0, k0, dlt0, b)
                    m_new = jnp.maximum(m, jnp.max(s, axis=1, keepdims=True))
                    alpha = jnp.exp(m - m_new)
                    p = jnp.exp(s - m_new)
                    l = alpha * l + jnp.sum(p, axis=1, keepdims=True)
                    pb = p.astype(BF16)
                    pv = jnp.dot(pb, vb, preferred_element_type=F32)
                    if cfg.split_p:
                        pv = pv + jnp.dot((p - pb.astype(F32)).astype(BF16), vb, preferred_element_type=F32)
                    a = alpha * a + pv
                    return m_new, l, a

                res.append(lax.fori_loop(lo, hi, kbody, (m0, l0, a0)))
            if nh == 1:
                m, l, a = res[0]
                m, l = jnp.broadcast_to(m, (tq, LANES)), jnp.broadcast_to(l, (tq, LANES))
            else:
                m = jnp.where(masks[0], res[0][0], res[1][0])
                l = jnp.where(masks[0], res[0][1], res[1][1])
                a = jnp.where(masks[0], res[0][2], res[1][2])
            if finalize:
                out_refs[0][rows, :] = a / l
                out_refs[1][rows, :] = m + jnp.log(l)
            else:
                out_refs[0][rows, :] = a
                out_refs[1][rows, :] = m
                out_refs[2][rows, :] = l
            return carry

        lax.fori_loop(0, cfg.nq, qbody, 0)

    qspec = pl.BlockSpec((None, cfg.lq, LANES), lambda b, j: (b, 0, cfg.qcol(j)))
    kspec = pl.BlockSpec((None, cfg.lk, LANES), lambda b, j: (b, 0, cfg.kcol(j)))
    vspec = pl.BlockSpec((None, cfg.lk, LANES), lambda b, j: (b, 0, cfg.vcol(j)))
    ospec = pl.BlockSpec((None, cfg.lq, LANES), lambda b, j: (b, 0, j))
    args, in_specs = [q, k, v], [qspec, kspec, vspec]
    if bias is not None:
        args += list(bias)
        in_specs += [ospec, pl.BlockSpec((None, None, cfg.nk, 8, tk), lambda b, j: (b, j, 0, 0, 0))]
    aliases = {}
    if state is not None:
        aliases = {len(args) + t: t for t in range(3 if not finalize else 2)}
        args += list(state)
        in_specs += [ospec] * 3
    n_out = 2 if finalize else 3
    osd = jax.ShapeDtypeStruct((g, cfg.lq, out_cols), F32)
    return pl.pallas_call(
        body,
        out_shape=(osd,) * n_out,
        grid=(g, cfg.ncol),
        in_specs=in_specs,
        out_specs=(ospec,) * n_out,
        input_output_aliases=aliases,
        compiler_params=_cparams(dimension_semantics=("parallel", "parallel")),
        name=name,
    )(*args)


def _attn_bwd(cfg, q, k, v, do, o, lse, *, out_cols, kv_cols, bias=None, acc=None, name):
    g = q.shape[0]
    tq, tk, e, nh = cfg.tq, cfg.tk, cfg.e, cfg.nh
    t0 = (((0,), (0,)), ((), ()))

    def body(*refs):
        refs = list(refs)
        q_ref, k_ref, v_ref, do_ref, o_ref, lse_ref = refs[:6]
        del refs[:6]
        if bias is not None:
            cb_ref, cr_ref = refs[:2]
            del refs[:2]
        if acc is not None:
            dqi_ref, dki_ref, dvi_ref = refs[:3]
            del refs[:3]
        dq_ref, dk_ref, dv_ref = refs[:3]
        dcr_ref = refs[3] if bias is not None else None
        masks = _head_masks(nh)
        dlt0 = lax.broadcasted_iota(jnp.int32, (tq, tk), 0) - lax.broadcasted_iota(jnp.int32, (tq, tk), 1)
        if acc is not None:
            dq_ref[...] = dqi_ref[...]
            dk_ref[...] = dki_ref[...]
            dv_ref[...] = dvi_ref[...]
        else:
            dq_ref[...] = jnp.zeros_like(dq_ref)
            dk_ref[...] = jnp.zeros_like(dk_ref)
            dv_ref[...] = jnp.zeros_like(dv_ref)
        if dcr_ref is not None:
            dcr_ref[...] = jnp.zeros_like(dcr_ref)

        def qbody(i, carry):
            q0 = pl.multiple_of(i * tq, tq)
            rows = pl.ds(q0, tq)
            qb = q_ref[rows, :]
            dob = do_ref[rows, :].astype(BF16)
            prod = dob.astype(F32) * o_ref[rows, :]
            lo, hi = cfg.k_range(i)
            dqs = []
            for h in range(nh):
                qh = _sel(masks[h], qb, jnp.zeros_like(qb))
                doh = _sel(masks[h], dob, jnp.zeros_like(dob))
                lse_h = lse_ref[rows, h * e:h * e + 1]
                delta = jnp.sum(_sel(masks[h], prod, jnp.zeros_like(prod)), axis=1, keepdims=True)
                cq = cb_ref[rows, h * e:h * e + 1] if bias is not None else None

                def kbody(jk, dq_acc, qh=qh, doh=doh, lse_h=lse_h, delta=delta, cq=cq, h=h):
                    k0 = pl.multiple_of(jk * tk, tk)
                    krows = pl.ds(k0, tk)
                    kb = k_ref[krows, :]
                    vb = v_ref[krows, :]
                    b = (cq - cr_ref[jk, h:h + 1, :]) if bias is not None else None
                    s = _scores(cfg, qh, kb, q0, k0, dlt0, b)
                    p = jnp.exp(s - lse_h)
                    dp = lax.dot_general(doh, vb, (((1,), (1,)), ((), ())), preferred_element_type=F32)
                    ds = p * (dp - delta)
                    if dcr_ref is not None:
                        dcr_ref[jk, h:h + 1, :] += jnp.sum(ds, axis=0, keepdims=True)
                    dsb = (ds * cfg.scale).astype(BF16)
                    dv_ref[krows, :] += lax.dot_general(p.astype(BF16), doh, t0, preferred_element_type=F32)
                    dk_ref[krows, :] += lax.dot_general(dsb, qh, t0, preferred_element_type=F32)
                    return dq_acc + jnp.dot(dsb, kb, preferred_element_type=F32)

                dqs.append(lax.fori_loop(lo, hi, kbody, jnp.zeros((tq, LANES), F32)))
            dq = dqs[0] if nh == 1 else jnp.where(masks[0], dqs[0], dqs[1])
            dq_ref[rows, :] += dq
            return carry

        lax.fori_loop(0, cfg.nq, qbody, 0)

    qspec = pl.BlockSpec((None, cfg.lq, LANES), lambda b, j: (b, 0, cfg.qcol(j)))
    kspec = pl.BlockSpec((None, cfg.lk, LANES), lambda b, j: (b, 0, cfg.kcol(j)))
    vspec = pl.BlockSpec((None, cfg.lk, LANES), lambda b, j: (b, 0, cfg.vcol(j)))
    ospec = pl.BlockSpec((None, cfg.lq, LANES), lambda b, j: (b, 0, j))
    kvspec = pl.BlockSpec((None, cfg.lk, LANES), lambda b, j: (b, 0, j))
    args, in_specs = [q, k, v, do, o, lse], [qspec, kspec, vspec, ospec, ospec, ospec]
    out_shape = [jax.ShapeDtypeStruct((g, cfg.lq, out_cols), F32), jax.ShapeDtypeStruct((g, cfg.lk, kv_cols), F32),
                 jax.ShapeDtypeStruct((g, cfg.lk, kv_cols), F32)]
    out_specs = [ospec, kvspec, kvspec]
    if bias is not None:
        args += list(bias)
        crspec = pl.BlockSpec((None, None, cfg.nk, 8, tk), lambda b, j: (b, j, 0, 0, 0))
        in_specs += [ospec, crspec]
        out_shape.append(jax.ShapeDtypeStruct((g, cfg.ncol, cfg.nk, 8, tk), F32))
        out_specs.append(crspec)
    aliases = {}
    if acc is not None:
        aliases = {len(args) + t: t for t in range(3)}
        args += list(acc)
        in_specs += [ospec, kvspec, kvspec]
    return pl.pallas_call(
        body,
        out_shape=tuple(out_shape),
        grid=(g, cfg.ncol),
        in_specs=in_specs,
        out_specs=tuple(out_specs),
        input_output_aliases=aliases,
        compiler_params=_cparams(dimension_semantics=("parallel", "parallel")),
        name=name,
    )(*args)


BLK = 128
NBLK = SEQ // BLK
QK_SCALE = 1.0 / math.sqrt(HEAD_DIM)
DIL_STEPS = tuple(d for _, d in DILATIONS)
assert all(w // d == BLK for w, d in DILATIONS)
_T0 = (((0,), (0,)), ((), ()))
_NT = (((1,), (1,)), ((), ()))


def _stack_heads(a, masks):
    z = jnp.zeros_like(a)
    return jnp.concatenate([jnp.where(masks[0], a, z), jnp.where(masks[1], a, z)], axis=0)


def _tri_bias(lower):
    r = lax.broadcasted_iota(jnp.int32, (BLK, BLK), 0)
    c = lax.broadcasted_iota(jnp.int32, (BLK, BLK), 1)
    return jnp.where((c <= r) if lower else (c >= r), 0.0, NEG_INF).astype(F32)


def _dil_rows(r, i, d):
    start = r + i * (BLK * d)
    return pl.ds(start, BLK) if d == 1 else pl.ds(start, BLK, stride=d)


def _dil_blocks(d, fn):
    nbk = SEQ // d // BLK
    if d == 1:
        lax.fori_loop(0, nbk, lambda i, c: (fn(0, i, None), c)[1], 0)
    elif nbk > 1:
        def rbody(r, c):
            for i in range(nbk):
                fn(r, i, i > 0)
            return c
        lax.fori_loop(0, d, rbody, 0)
    else:
        def rbody(rr, c):
            fn(2 * rr, 0, False)
            fn(2 * rr + 1, 0, False)
            return c
        lax.fori_loop(0, d // 2, rbody, 0)


def _dil_key_tiles(r, i, d, has_prev, qrows, tri_cur, tri_prev):
    tiles = [(qrows, tri_cur)]
    if has_prev is None:
        tiles.append((_dil_rows(r, jnp.maximum(i - 1, 0), d), tri_prev + jnp.where(i > 0, 0.0, NEG_INF)))
    elif has_prev:
        tiles.append((_dil_rows(r, i - 1, d), tri_prev))
    return tiles


def _dil_fwd(qkv, *, name):
    nb = qkv.shape[0]
    ncol = DIL_W // LANES
    hd = HEAD_DIM

    def body(q_ref, k_ref, v_ref, o_ref, lse_ref, m_ref, l_ref, a_ref):
        masks = _head_masks(2)
        tri_cur, tri_prev = _tri_bias(True), _tri_bias(False)
        for pi, d in enumerate(DIL_STEPS):
            first, last = pi == 0, pi == len(DIL_STEPS) - 1

            def blk(r, i, has_prev, d=d, first=first, last=last):
                qrows = _dil_rows(r, i, d)
                qcat = _stack_heads((q_ref[qrows, :] * QK_SCALE).astype(BF16), masks)
                ss, vcats = [], []
                for krows, bias in _dil_key_tiles(r, i, d, has_prev, qrows, tri_cur, tri_prev):
                    s = lax.dot_general(qcat, k_ref[krows, :].astype(BF16), _NT, preferred_element_type=F32)
                    ss.append((s[:BLK] + bias, s[BLK:] + bias))
                    vcats.append(_stack_heads(v_ref[krows, :].astype(BF16), masks))
                e0 = ss[0][0] if len(ss) == 1 else jnp.maximum(ss[0][0], ss[1][0])
                e1 = ss[0][1] if len(ss) == 1 else jnp.maximum(ss[0][1], ss[1][1])
                n0 = jnp.max(e0, axis=1, keepdims=True)
                n1 = jnp.max(e1, axis=1, keepdims=True)
                if not first:
                    mo, lo = m_ref[qrows, :], l_ref[qrows, :]
                    m0, m1 = mo[:, 0:1], mo[:, hd:hd + 1]
                    n0, n1 = jnp.maximum(n0, m0), jnp.maximum(n1, m1)
                    a0, a1 = jnp.exp(m0 - n0), jnp.exp(m1 - n1)
                ps = [(jnp.exp(s0 - n0), jnp.exp(s1 - n1)) for s0, s1 in ss]
                t0 = ps[0][0] if len(ps) == 1 else ps[0][0] + ps[1][0]
                t1 = ps[0][1] if len(ps) == 1 else ps[0][1] + ps[1][1]
                l0 = jnp.sum(t0, axis=1, keepdims=True)
                l1 = jnp.sum(t1, axis=1, keepdims=True)
                acc = None
                for (p0, p1), vcat in zip(ps, vcats):
                    pv = jnp.dot(jnp.concatenate([p0, p1], axis=1).astype(BF16), vcat, preferred_element_type=F32)
                    acc = pv if acc is None else acc + pv
                if not first:
                    l0 = l0 + a0 * lo[:, 0:1]
                    l1 = l1 + a1 * lo[:, hd:hd + 1]
                    acc = acc + a_ref[qrows, :] * jnp.where(masks[0], a0, a1)
                if last:
                    o_ref[qrows, :] = acc / jnp.where(masks[0], l0, l1)
                    lse_ref[qrows, :] = jnp.where(masks[0], n0 + jnp.log(l0), n1 + jnp.log(l1))
                else:
                    m_ref[qrows, :] = jnp.where(masks[0], n0, n1)
                    l_ref[qrows, :] = jnp.where(masks[0], l0, l1)
                    a_ref[qrows, :] = acc

            _dil_blocks(d, blk)

    spec = lambda off: pl.BlockSpec((None, SEQ, LANES), lambda b, j: (b, 0, off + j))
    ospec = pl.BlockSpec((None, SEQ, LANES), lambda b, j: (b, 0, j))
    osd = jax.ShapeDtypeStruct((nb, SEQ, DIL_W), F32)
    return pl.pallas_call(
        body, out_shape=(osd, osd), grid=(nb, ncol),
        in_specs=[spec(0), spec(ncol), spec(2 * ncol)], out_specs=(ospec, ospec),
        scratch_shapes=[pltpu.VMEM((SEQ, LANES), F32)] * 3,
        compiler_params=_cparams(dimension_semantics=("parallel", "parallel")), name=name,
    )(qkv, qkv, qkv)


def _dil_bwd(qkv, do, o, lse, *, name):
    nb = qkv.shape[0]
    ncol = DIL_W // LANES
    hd = HEAD_DIM

    def body(q_ref, k_ref, v_ref, do_ref, o_ref, lse_ref, dq_ref, dk_ref, dv_ref, dl_ref):
        masks = _head_masks(2)
        tri_cur, tri_prev = _tri_bias(True), _tri_bias(False)
        dq_ref[...] = jnp.zeros_like(dq_ref)
        dk_ref[...] = jnp.zeros_like(dk_ref)
        dv_ref[...] = jnp.zeros_like(dv_ref)

        def delta_body(i, c):
            rows = pl.ds(pl.multiple_of(i * BLK, BLK), BLK)
            prod = do_ref[rows, :].astype(BF16).astype(F32) * o_ref[rows, :]
            z = jnp.zeros_like(prod)
            dl_ref[rows, :] = jnp.where(masks[0], jnp.sum(jnp.where(masks[0], prod, z), axis=1, keepdims=True),
                                        jnp.sum(jnp.where(masks[1], prod, z), axis=1, keepdims=True))
            return c

        lax.fori_loop(0, NBLK, delta_body, 0)

        for d in DIL_STEPS:
            def blk(r, i, has_prev, d=d):
                qrows = _dil_rows(r, i, d)
                qcat = _stack_heads((q_ref[qrows, :] * QK_SCALE).astype(BF16), masks)
                docat = _stack_heads(do_ref[qrows, :].astype(BF16), masks)
                lseb, dlb = lse_ref[qrows, :], dl_ref[qrows, :]
                lse0, lse1 = lseb[:, 0:1], lseb[:, hd:hd + 1]
                dl0, dl1 = dlb[:, 0:1], dlb[:, hd:hd + 1]
                dq = None
                for krows, bias in _dil_key_tiles(r, i, d, has_prev, qrows, tri_cur, tri_prev):
                    kf = k_ref[krows, :]
                    vb = v_ref[krows, :].astype(BF16)
                    s = lax.dot_general(qcat, kf.astype(BF16), _NT, preferred_element_type=F32)
                    p0 = jnp.exp(s[:BLK] + bias - lse0)
                    p1 = jnp.exp(s[BLK:] + bias - lse1)
                    dp = lax.dot_general(docat, vb, _NT, preferred_element_type=F32)
                    ds0 = p0 * (dp[:BLK] - dl0)
                    ds1 = p1 * (dp[BLK:] - dl1)
                    pcat = jnp.concatenate([p0, p1], axis=0).astype(BF16)
                    dscat = jnp.concatenate([ds0, ds1], axis=0).astype(BF16)
                    dv_ref[krows, :] += lax.dot_general(pcat, docat, _T0, preferred_element_type=F32)
                    dk_ref[krows, :] += lax.dot_general(dscat, qcat, _T0, preferred_element_type=F32)
                    dsrow = jnp.concatenate([ds0, ds1], axis=1).astype(BF16)
                    t = jnp.dot(dsrow, _stack_heads((kf * QK_SCALE).astype(BF16), masks), preferred_element_type=F32)
                    dq = t if dq is None else dq + t
                dq_ref[qrows, :] += dq

            _dil_blocks(d, blk)

    spec = lambda off: pl.BlockSpec((None, SEQ, LANES), lambda b, j: (b, 0, off + j))
    ospec = pl.BlockSpec((None, SEQ, LANES), lambda b, j: (b, 0, j))
    osd = jax.ShapeDtypeStruct((nb, SEQ, DIL_W), F32)
    return pl.pallas_call(
        body, out_shape=(osd, osd, osd), grid=(nb, ncol),
        in_specs=[spec(0), spec(ncol), spec(2 * ncol), ospec, ospec, ospec], out_specs=(ospec, ospec, ospec),
        scratch_shapes=[pltpu.VMEM((SEQ, LANES), F32)],
        compiler_params=_cparams(dimension_semantics=("parallel", "parallel")), name=name,
    )(qkv, qkv, qkv, do, o, lse)


PBLK = PW // LANES


def _fox_cfg():
    return _AttnCfg(e=HEAD_DIM, tq=256, tk=256, lq=SEQ, lk=SEQ, causal=True, window=None, ncol=FOX_W // LANES, split_p=True,
                    qcol=lambda j: C_FQ // LANES + j, kcol=lambda j: C_FK // LANES + j, vcol=lambda j: C_FV // LANES + j)


def _mem_cfg():
    return _AttnCfg(e=MEM_HEAD_DIM, tq=256, tk=MEM_LEN, lq=SEQ, lk=MEM_LEN, causal=False, window=None, ncol=MEM_HEADS,
                    qcol=lambda j: C_MQ // LANES + j, kcol=lambda j: j, vcol=lambda j: MEM_HEADS + j)


N_YBLK = MIX_W // LANES
_GATE_BLK = (C_FG // LANES, C_DG // LANES, C_MG // LANES)
_B1, _B2 = FOX_W // LANES, (FOX_W + DIL_W) // LANES


def _att_specs(tm):
    fspec = pl.BlockSpec((tm, LANES), lambda i, j: (i, jnp.minimum(j, _B1 - 1)))
    dspec = pl.BlockSpec((tm, LANES), lambda i, j: (i, jnp.clip(j - _B1, 0, _B2 - _B1 - 1)))
    mspec = pl.BlockSpec((tm, LANES), lambda i, j: (i, jnp.clip(j - _B2, 0, N_YBLK - _B2 - 1)))

    def gcol(j):
        return jnp.where(j < _B1, _GATE_BLK[0] + j, jnp.where(j < _B2, _GATE_BLK[1] + j - _B1, _GATE_BLK[2] + j - _B2))

    gspec = pl.BlockSpec((tm, LANES), lambda i, j: (i, gcol(j)))
    return fspec, dspec, mspec, gspec


def _pick_att(j, f_ref, d_ref, m_ref):
    return jnp.where(j < _B1, f_ref[...], jnp.where(j < _B2, d_ref[...], m_ref[...]))


def _gate_fwd(fox, dil, memo, p16, *, tm, name):
    t = fox.shape[0]

    def body(f_ref, d_ref, m_ref, g_ref, y_ref):
        j = pl.program_id(1)
        a = _pick_att(j, f_ref, d_ref, m_ref)
        gt = g_ref[...].astype(F32)
        y_ref[...] = (a * gt / (1.0 + jnp.exp(-gt))).astype(y_ref.dtype)

    return pl.pallas_call(
        body,
        out_shape=jax.ShapeDtypeStruct((t, MIX_W), BF16),
        grid=(t // tm, N_YBLK),
        in_specs=list(_att_specs(tm)),
        out_specs=pl.BlockSpec((tm, LANES), lambda i, j: (i, j)),
        compiler_params=_cparams(dimension_semantics=("parallel", "parallel")),
        name=name,
    )(fox, dil, memo, p16)


def _gate_bwd(dy, fox, dil, memo, p16, *, tm, name):
    t = fox.shape[0]

    def body(dy_ref, f_ref, d_ref, m_ref, g_ref, da_ref, dg_ref):
        j = pl.program_id(1)
        a = _pick_att(j, f_ref, d_ref, m_ref)
        gt = g_ref[...].astype(F32)
        sg = 1.0 / (1.0 + jnp.exp(-gt))
        dyv = dy_ref[...]
        da_ref[...] = dyv * gt * sg
        dg_ref[...] = (dyv * a * sg * (1.0 + gt * (1.0 - sg))).astype(dg_ref.dtype)

    yspec = pl.BlockSpec((tm, LANES), lambda i, j: (i, j))
    return pl.pallas_call(
        body,
        out_shape=(jax.ShapeDtypeStruct((t, MIX_W), F32), jax.ShapeDtypeStruct((t, MIX_W), BF16)),
        grid=(t // tm, N_YBLK),
        in_specs=[yspec] + list(_att_specs(tm)),
        out_specs=(yspec, yspec),
        compiler_params=_cparams(dimension_semantics=("parallel", "parallel")),
        name=name,
    )(dy, fox, dil, memo, p16)


def _out_loss(y, wo, x, tgt, gfin, *, tm, name):
    t, d = x.shape
    n_feat = float(d)

    def body(y_ref, w_ref, x_ref, t_ref, g_ref, dx_ref, dxb_ref, st_ref):
        i = pl.program_id(0)

        @pl.when(i == 0)
        def _():
            st_ref[...] = jnp.zeros_like(st_ref)

        x2 = x_ref[...] + jnp.dot(y_ref[...], w_ref[...], preferred_element_type=F32)
        r = lax.rsqrt(jnp.mean(x2 * x2, axis=-1, keepdims=True) + RMS_EPS)
        nrm = x2 * r
        gv = g_ref[...]
        err = nrm * gv - t_ref[...]
        dout = err * (1.0 / n_feat)
        dn = dout * gv
        dx2 = r * (dn - nrm * jnp.mean(dn * nrm, axis=-1, keepdims=True))
        dx_ref[...] = dx2
        dxb_ref[...] = dx2.astype(dxb_ref.dtype)
        st_ref[0:1, :] += jnp.sum(dout * nrm, axis=0, keepdims=True)
        st_ref[1:2, :] += (0.5 / n_feat) * jnp.sum(err * err, axis=0, keepdims=True)

    row = pl.BlockSpec((tm, d), lambda i: (i, 0))
    return pl.pallas_call(
        body,
        out_shape=(jax.ShapeDtypeStruct((t, d), F32), jax.ShapeDtypeStruct((t, d), BF16), jax.ShapeDtypeStruct((8, d), F32)),
        grid=(t // tm,),
        in_specs=[pl.BlockSpec((tm, MIX_W), lambda i: (i, 0)), pl.BlockSpec((MIX_W, d), lambda i: (0, 0)), row, row,
                  pl.BlockSpec((1, d), lambda i: (0, 0))],
        out_specs=(row, row, pl.BlockSpec((8, d), lambda i: (0, 0))),
        compiler_params=_cparams(dimension_semantics=("arbitrary",)),
        name=name,
    )(y, wo, x, tgt, gfin)


def _dh_rms_bwd(dp, wt, x, g, resid, *, tm, tk, name):
    t, d = x.shape
    kdim = dp.shape[1]
    nk = kdim // tk

    def body(*refs):
        if resid is not None:
            dp_ref, w_ref, x_ref, g_ref, r_ref, dx_ref, gg_ref, acc_ref = refs
        else:
            dp_ref, w_ref, x_ref, g_ref, dx_ref, gg_ref, acc_ref = refs
        i = pl.program_id(0)
        k = pl.program_id(1)

        @pl.when(jnp.logical_and(i == 0, k == 0))
        def _():
            gg_ref[...] = jnp.zeros_like(gg_ref)

        @pl.when(k == 0)
        def _():
            acc_ref[...] = jnp.zeros_like(acc_ref)

        acc_ref[...] += jnp.dot(dp_ref[...], w_ref[...], preferred_element_type=F32)

        @pl.when(k == nk - 1)
        def _():
            dh = acc_ref[...]
            xv = x_ref[...]
            r = lax.rsqrt(jnp.mean(xv * xv, axis=-1, keepdims=True) + RMS_EPS)
            nrm = xv * r
            dn = dh * g_ref[...]
            dx = r * (dn - nrm * jnp.mean(dn * nrm, axis=-1, keepdims=True))
            if resid is not None:
                dx = dx + r_ref[...]
            dx_ref[...] = dx
            gg_ref[0:1, :] += jnp.sum(dh * nrm, axis=0, keepdims=True)

    row = pl.BlockSpec((tm, d), lambda i, k: (i, 0))
    in_specs = [pl.BlockSpec((tm, tk), lambda i, k: (i, k)), pl.BlockSpec((tk, d), lambda i, k: (k, 0)), row,
                pl.BlockSpec((1, d), lambda i, k: (0, 0))]
    args = [dp, wt, x, g]
    if resid is not None:
        in_specs.append(row)
        args.append(resid)
    return pl.pallas_call(
        body,
        out_shape=(jax.ShapeDtypeStruct((t, d), F32), jax.ShapeDtypeStruct((8, d), F32)),
        grid=(t // tm, nk),
        in_specs=in_specs,
        out_specs=(row, pl.BlockSpec((8, d), lambda i, k: (0, 0))),
        scratch_shapes=[pltpu.VMEM((tm, d), F32)],
        compiler_params=_cparams(dimension_semantics=("arbitrary", "arbitrary")),
        name=name,
    )(*args)


def _rope_bwd(dq, tabs, *, tm, name):
    t, n = dq.shape
    s_blocks = SEQ // tm

    def body(d_ref, c_ref, s1_ref, s2_ref, o_ref):
        o_ref[...] = _rope_apply(d_ref[...], c_ref[...], s1_ref[...], s2_ref[...], transpose=True).astype(o_ref.dtype)

    tab_spec = pl.BlockSpec((tm, LANES), lambda i: (i % s_blocks, 0))
    return pl.pallas_call(
        body,
        out_shape=jax.ShapeDtypeStruct((t, n), BF16),
        grid=(t // tm,),
        in_specs=[pl.BlockSpec((tm, n), lambda i: (i, 0)), tab_spec, tab_spec, tab_spec],
        out_specs=pl.BlockSpec((tm, n), lambda i: (i, 0)),
        compiler_params=_cparams(),
        name=name,
    )(dq, *tabs)


def _rearrange_w_in(w):
    offs = [0]
    for s in IN_SIZES:
        offs.append(offs[-1] + s)
    pieces = [w[:, offs[i]:offs[i + 1]] for i in range(len(IN_SIZES))]
    flog = jnp.pad(pieces[4], ((0, 0), (0, LANES - FOX_HEADS)))
    return jnp.concatenate(pieces[:4] + pieces[5:] + [flog], axis=1)


def _restore_w_in_cols(g):
    return jnp.concatenate([g[:, :C_DQ], g[:, PW:PW + FOX_HEADS], g[:, C_DQ:PW]], axis=1)


def _local_grads(x, mem, norm_g, w_r, b_forget, mem_norm_g, w_kv, w_o, final_norm_g, tgt):
    nb = x.shape[0]
    t = nb * SEQ
    x2d = x.reshape(t, D_MODEL)
    tgt2d = tgt.reshape(t, D_MODEL)
    tabs = _rope_tables()
    bpad = jnp.pad(b_forget.reshape(1, FOX_HEADS), ((0, 0), (0, LANES - FOX_HEADS)))

    h = _rms_fwd(x2d, norm_g.reshape(1, D_MODEL), tm=512, name="rms_x")
    p16, dqkv = _proj(h, w_r[:, :PW], tabs, tm=1024, tn=256, name="proj")
    flog = _matmul(h, w_r[:, PW:], out_dtype=F32, tm=1024, tn=LANES, tk=D_MODEL, name="proj_flog")
    c12, cb = _flog_fwd(flog, bpad, nb=nb, ts=256, name="flog_fwd")

    fcfg = _fox_cfg()
    crow = c12[:, :FOX_HEADS].reshape(nb, fcfg.nk, fcfg.tk, fcfg.ncol, 2).transpose(0, 3, 1, 4, 2)
    crow = jnp.pad(crow, ((0, 0), (0, 0), (0, 0), (0, 6), (0, 0)))
    p3 = p16.reshape(nb, SEQ, PW)
    cb3 = cb.reshape(nb, SEQ, FOX_W)
    fox, fox_lse = _attn_fwd(fcfg, p3, p3, p3, out_cols=FOX_W, bias=(cb3, crow), name="fox_fwd")

    dqkv3 = dqkv.reshape(nb, SEQ, 3 * DIL_W)
    dil, dil_lse = _dil_fwd(dqkv3, name="dil_fwd")

    mh = _rms_fwd(mem.reshape(nb * MEM_LEN, D_MODEL), mem_norm_g.reshape(1, D_MODEL), tm=nb * MEM_LEN, name="rms_mem")
    mkv = _matmul(mh, w_kv, out_dtype=BF16, tm=nb * MEM_LEN, tn=512, tk=D_MODEL, name="mem_kv")
    mkv3 = mkv.reshape(nb, MEM_LEN, 2 * MEM_W)
    mcfg = _mem_cfg()
    memo, mem_lse = _attn_fwd(mcfg, p3, mkv3, mkv3, out_cols=MEM_W, name="mem_fwd")

    fox2, dil2, memo2 = fox.reshape(t, FOX_W), dil.reshape(t, DIL_W), memo.reshape(t, MEM_W)
    y = _gate_fwd(fox2, dil2, memo2, p16, tm=1024, name="gate_fwd")
    dx2, dx2b, st = _out_loss(y, w_o, x2d, tgt2d, final_norm_g.reshape(1, D_MODEL), tm=512, name="out_loss")

    g_wo = _matmul(y.T, dx2b, out_dtype=F32, tm=1024, tn=512, tk=1024, name="grad_w_out")
    dy = _matmul(dx2b, w_o.T, out_dtype=F32, tm=1024, tn=512, tk=D_MODEL, name="d_y")
    datt, dgate = _gate_bwd(dy, fox2, dil2, memo2, p16, tm=1024, name="gate_bwd")
    dfox = datt[:, :FOX_W].reshape(nb, SEQ, FOX_W)
    ddil = datt[:, FOX_W:FOX_W + DIL_W].reshape(nb, SEQ, DIL_W)
    dmemo = datt[:, FOX_W + DIL_W:].reshape(nb, SEQ, MEM_W)

    dfq, dfk, dfv, dcr = _attn_bwd(fcfg, p3, p3, p3, dfox, fox, fox_lse, out_cols=FOX_W, kv_cols=FOX_W,
                                    bias=(cb3, crow), name="fox_bwd")
    dcol = -dcr[:, :, :, :2, :].transpose(0, 2, 4, 1, 3).reshape(t, FOX_HEADS)
    dcol = jnp.pad(dcol, ((0, 0), (0, LANES - FOX_HEADS)))
    dflog, gb = _flog_bwd(dcol, flog, bpad, nb=nb, ts=256, name="flog_bwd")

    ddq, ddk, ddv = (a.reshape(t, DIL_W) for a in _dil_bwd(dqkv3, ddil, dil, dil_lse, name="dil_bwd"))
    ddq = _rope_bwd(ddq, tabs, tm=512, name="rope_bwd_q")
    ddk = _rope_bwd(ddk, tabs, tm=512, name="rope_bwd_k")

    dmq, dmk, dmv = _attn_bwd(mcfg, p3, mkv3, mkv3, dmemo, memo, mem_lse, out_cols=MEM_W, kv_cols=MEM_W, name="mem_bwd")
    dmkv = jnp.concatenate([dmk, dmv], axis=-1).reshape(nb * MEM_LEN, 2 * MEM_W).astype(BF16)
    g_wkv = _matmul(mh.T, dmkv, out_dtype=F32, tm=512, tn=512, tk=nb * MEM_LEN, name="grad_w_kv")
    _, gmn = _dh_rms_bwd(dmkv, w_kv.T, mem.reshape(nb * MEM_LEN, D_MODEL), mem_norm_g.reshape(1, D_MODEL), None,
                         tm=nb * MEM_LEN, tk=2 * MEM_W, name="mem_rms_bwd")

    bf = lambda a: a.reshape(t, -1).astype(BF16)
    dp = jnp.concatenate([bf(dfq), bf(dfk), bf(dfv), dgate[:, :FOX_W], ddq, ddk, bf(ddv), dgate[:, FOX_W:FOX_W + DIL_W],
                          bf(dmq), dgate[:, FOX_W + DIL_W:], dflog], axis=1)
    g_wr = _matmul(h.T, dp, out_dtype=F32, tm=512, tn=PWF // 3, tk=1024, name="grad_w_in")
    gx, gng = _dh_rms_bwd(dp, w_r.T, x2d, norm_g.reshape(1, D_MODEL), dx2, tm=512, tk=PWF // 3, name="in_rms_bwd")

    gb_row = jnp.pad(gb[0:1, :], ((0, 0), (0, D_MODEL - LANES)))
    small = jnp.concatenate([gng[0:1], gmn[0:1], st[0:1], gb_row, st[1:2], jnp.zeros((3, D_MODEL), F32)], axis=0)
    return gx.reshape(nb, SEQ, D_MODEL), g_wr, g_wkv, g_wo, small


MESH = pl.DeviceIdType.MESH
ANY = pl.BlockSpec(memory_space=pl.ANY)


def _place():
    x, y, c = lax.axis_index("x"), lax.axis_index("y"), lax.axis_index("c")
    other_chips = [(1 - x, y), (x, 1 - y), (1 - x, 1 - y)]
    return x, y, c, other_chips


def _gather_weights(shards):
    n = len(shards)

    def body(*refs):
        in_refs, out_refs = refs[:n], refs[n:2 * n]
        send_sems, recv_sems = refs[2 * n:]
        x, y, c, chips = _place()
        me_chip = 2 * x + y
        sibling = (x, y, 1 - c)

        def half(ref, pc, rows):
            return ref.at[pl.ds(pc * (rows // 2), rows // 2), :]

        def rcopy(k, src, dst, to):
            return pltpu.make_async_remote_copy(src_ref=src, dst_ref=dst, send_sem=send_sems.at[k], recv_sem=recv_sems.at[k],
                                                device_id=to, device_id_type=MESH)

        sends = []
        for t in range(n):
            rows = shards[t].shape[0]
            for j, chip in enumerate(chips):
                cp = rcopy(6 * t + j, half(in_refs[t], c, rows), half(out_refs[t].at[me_chip], c, rows), (*chip, c))
                cp.start()
                sends.append(cp)
        for t in range(n):
            rows = shards[t].shape[0]
            for j, chip in enumerate(chips):
                slot = out_refs[t].at[2 * chip[0] + chip[1]]
                rcopy(6 * t + j, half(slot, c, rows), half(slot, c, rows), sibling).wait_recv()
                fw = rcopy(6 * t + 3 + j, half(slot, c, rows), half(slot, c, rows), sibling)
                fw.start()
                sends.append(fw)
        for t in range(n):
            rows = shards[t].shape[0]
            for j, chip in enumerate(chips):
                slot = out_refs[t].at[2 * chip[0] + chip[1]]
                rcopy(6 * t + 3 + j, half(slot, 1 - c, rows), half(slot, 1 - c, rows), sibling).wait_recv()
        for cp in sends:
            cp.wait_send()

    return pl.pallas_call(
        body,
        out_shape=tuple(jax.ShapeDtypeStruct((N_CHIPS,) + s.shape, s.dtype) for s in shards),
        in_specs=[ANY] * n,
        out_specs=tuple([ANY] * n),
        scratch_shapes=[pltpu.SemaphoreType.DMA((6 * n,)), pltpu.SemaphoreType.DMA((6 * n,))],
        name="gather_weights",
    )(*shards)


def _pair_exchange(gs):
    n = len(gs)

    def body(*refs):
        g_refs, r_refs = refs[:n], refs[n:2 * n]
        send_sems, recv_sems = refs[2 * n:]
        x, y, c, _ = _place()
        cps = []
        for t in range(n):
            hr = gs[t].shape[1] // 2
            cp = pltpu.make_async_remote_copy(src_ref=g_refs[t].at[:, pl.ds((1 - c) * hr, hr), :], dst_ref=r_refs[t],
                                              send_sem=send_sems.at[t], recv_sem=recv_sems.at[t],
                                              device_id=(x, y, 1 - c), device_id_type=MESH)
            cp.start()
            cps.append(cp)
        for cp in cps:
            cp.wait()

    return pl.pallas_call(
        body,
        out_shape=tuple(jax.ShapeDtypeStruct((N_CHIPS, g.shape[1] // 2, g.shape[2]), g.dtype) for g in gs),
        in_specs=[ANY] * n,
        out_specs=tuple([ANY] * n),
        scratch_shapes=[pltpu.SemaphoreType.DMA((n,)), pltpu.SemaphoreType.DMA((n,))],
        name="pair_exchange",
    )(*gs)


def _chip_exchange(ps):
    n = len(ps)

    def body(*refs):
        p_refs, o_refs = refs[:n], refs[n:2 * n]
        send_sems, recv_sems = refs[2 * n:]
        x, y, c, chips = _place()
        me_chip = 2 * x + y
        cps = []
        for t in range(n):
            for j, chip in enumerate(chips):
                cp = pltpu.make_async_remote_copy(src_ref=p_refs[t].at[2 * chip[0] + chip[1]], dst_ref=o_refs[t].at[me_chip],
                                                  send_sem=send_sems.at[3 * t + j], recv_sem=recv_sems.at[3 * t + j],
                                                  device_id=(*chip, c), device_id_type=MESH)
                cp.start()
                cps.append(cp)
        for cp in cps:
            cp.wait()

    return pl.pallas_call(
        body,
        out_shape=tuple(jax.ShapeDtypeStruct(p.shape, p.dtype) for p in ps),
        in_specs=[ANY] * n,
        out_specs=tuple([ANY] * n),
        scratch_shapes=[pltpu.SemaphoreType.DMA((3 * n,)), pltpu.SemaphoreType.DMA((3 * n,))],
        name="chip_exchange",
    )(*ps)


def _pair_swap(rs):
    n = len(rs)

    def body(*refs):
        r_refs, o_refs = refs[:n], refs[n:2 * n]
        send_sems, recv_sems = refs[2 * n:]
        x, y, c, _ = _place()
        cps = []
        for t in range(n):
            cp = pltpu.make_async_remote_copy(src_ref=r_refs[t], dst_ref=o_refs[t], send_sem=send_sems.at[t],
                                              recv_sem=recv_sems.at[t], device_id=(x, y, 1 - c), device_id_type=MESH)
            cp.start()
            cps.append(cp)
        for cp in cps:
            cp.wait()

    return pl.pallas_call(
        body,
        out_shape=tuple(jax.ShapeDtypeStruct(r.shape, r.dtype) for r in rs),
        in_specs=[ANY] * n,
        out_specs=tuple([ANY] * n),
        scratch_shapes=[pltpu.SemaphoreType.DMA((n,)), pltpu.SemaphoreType.DMA((n,))],
        name="pair_swap",
    )(*rs)


N_DEV = 8
LOSS_ROW = 4


def _small_allreduce(small):
    def body(s_ref, o_ref, all_ref, send_sems, recv_sems):
        x, y, c, _ = _place()
        me = 4 * x + 2 * y + c
        all_ref[me] = s_ref[...]
        cps = []
        for k in range(1, N_DEV):
            peer = tuple(1 - p if (k >> s) & 1 else p for p, s in ((x, 2), (y, 1), (c, 0)))
            cp = pltpu.make_async_remote_copy(src_ref=s_ref, dst_ref=all_ref.at[me], send_sem=send_sems.at[k - 1],
                                              recv_sem=recv_sems.at[k - 1], device_id=peer, device_id_type=MESH)
            cp.start()
            cps.append(cp)
        for cp in cps:
            cp.wait()
        tot = all_ref[0]
        for d in range(1, N_DEV):
            tot = tot + all_ref[d]
        o_ref[...] = tot
        o_ref[LOSS_ROW:LOSS_ROW + 1, :] = jnp.broadcast_to(jnp.sum(tot[LOSS_ROW:LOSS_ROW + 1, :], axis=1, keepdims=True),
                                                          (1, tot.shape[1]))

    vm = pl.BlockSpec(memory_space=pltpu.VMEM)
    return pl.pallas_call(
        body,
        out_shape=jax.ShapeDtypeStruct(small.shape, small.dtype),
        in_specs=[vm],
        out_specs=vm,
        scratch_shapes=[pltpu.VMEM((N_DEV,) + small.shape, small.dtype), pltpu.SemaphoreType.DMA((N_DEV - 1,)),
                        pltpu.SemaphoreType.DMA((N_DEV - 1,))],
        name="small_allreduce",
    )(small)


def _sum_pair(g, recv, cidx, *, tr, name):
    _, hr, cols = recv.shape
    nr = hr // tr

    def body(c_ref, g_ref, r_ref, o_ref):
        o_ref[...] = (g_ref[...] + r_ref[...]).astype(o_ref.dtype)

    grid_spec = pltpu.PrefetchScalarGridSpec(
        num_scalar_prefetch=1,
        grid=(N_CHIPS, nr),
        in_specs=[pl.BlockSpec((None, tr, cols), lambda k, i, c_ref: (k, c_ref[0] * nr + i, 0)),
                  pl.BlockSpec((None, tr, cols), lambda k, i, c_ref: (k, i, 0))],
        out_specs=pl.BlockSpec((None, tr, cols), lambda k, i, c_ref: (k, i, 0)),
    )
    return pl.pallas_call(body, out_shape=jax.ShapeDtypeStruct(recv.shape, BF16), grid_spec=grid_spec,
                          compiler_params=_cparams(), name=name)(cidx, g, recv)


def _sum_chips(p, *, tr, name):
    _, rows, cols = p.shape

    def body(p_ref, o_ref):
        tot = p_ref[0].astype(F32)
        for k in range(1, N_CHIPS):
            tot = tot + p_ref[k].astype(F32)
        o_ref[...] = tot

    return pl.pallas_call(
        body,
        out_shape=jax.ShapeDtypeStruct((rows, cols), F32),
        grid=(rows // tr,),
        in_specs=[pl.BlockSpec((N_CHIPS, tr, cols), lambda i: (0, i, 0))],
        out_specs=pl.BlockSpec((tr, cols), lambda i: (i, 0)),
        compiler_params=_cparams(),
        name=name,
    )(p)


def _adamw(w, g, m, v, *, tr, name):
    rows, cols = w.shape
    bc1 = 1.0 / (1.0 - ADAM_B1 ** ADAM_STEP)
    bc2 = 1.0 / (1.0 - ADAM_B2 ** ADAM_STEP)

    def body(w_ref, g_ref, m_ref, v_ref, d_ref, nm_ref, nv_ref):
        gv = g_ref[...]
        nm = ADAM_B1 * m_ref[...] + (1.0 - ADAM_B1) * gv
        nv = ADAM_B2 * v_ref[...] + (1.0 - ADAM_B2) * (gv * gv)
        d_ref[...] = -ADAM_LR * ((nm * bc1) / (jnp.sqrt(nv * bc2) + ADAM_EPS) + ADAM_WD * w_ref[...])
        nm_ref[...] = nm
        nv_ref[...] = nv

    spec = pl.BlockSpec((tr, cols), lambda i: (i, 0))
    sd = jax.ShapeDtypeStruct((rows, cols), F32)
    return pl.pallas_call(body, out_shape=(sd, sd, sd), grid=(rows // tr,), in_specs=[spec] * 4, out_specs=(spec,) * 3,
                          compiler_params=_cparams(), name=name)(w, g, m, v)


def _pack_small(norm, mem_norm, final_norm, b_forget):
    rows = [norm.reshape(1, D_MODEL), mem_norm.reshape(1, D_MODEL), final_norm.reshape(1, D_MODEL),
            jnp.pad(b_forget.reshape(1, FOX_HEADS), ((0, 0), (0, D_MODEL - FOX_HEADS))), jnp.zeros((4, D_MODEL), F32)]
    return jnp.concatenate(rows, axis=0)


def _unpack_small(a):
    return a[0:1], a[3:4, :FOX_HEADS], a[1:2], a[2]


def kernel(x, mem, norm_g, w_in, b_forget, mem_norm_g, w_mem_kv, w_out, final_norm_g, loss_target, m_norm_g, m_w_in, m_b_forget, m_mem_norm_g, m_w_mem_kv, m_w_out, m_final_norm_g, v_norm_g, v_w_in, v_b_forget, v_mem_norm_g, v_w_mem_kv, v_w_out, v_final_norm_g):
    core = lax.axis_index("c").astype(jnp.int32)
    me_chip = (2 * lax.axis_index("x") + lax.axis_index("y")).astype(jnp.int32)
    cidx = core.reshape(1)

    def own_slot(arr, own):
        return lax.dynamic_update_slice(arr, own[None].astype(arr.dtype), (me_chip,) + (0,) * own.ndim)

    mine = [w_in[0].astype(BF16), w_mem_kv[0].astype(BF16), w_out[0].astype(BF16)]
    g_in, g_kv, g_out = (own_slot(g, s) for g, s in zip(_gather_weights(mine), mine))
    w_r = _rearrange_w_in(jnp.concatenate([g_in[k] for k in range(N_CHIPS)], axis=1))
    w_kv = g_kv.reshape(D_MODEL, 2 * MEM_W)
    w_o = g_out.reshape(MIX_W, D_MODEL)

    gx, g_wr, g_wkv, g_wo, small = _local_grads(x, mem, norm_g, w_r, b_forget, mem_norm_g, w_kv, w_o, final_norm_g, loss_target)

    shard_w = IN_W // N_CHIPS
    slabs = [_restore_w_in_cols(g_wr).reshape(D_MODEL, N_CHIPS, shard_w).transpose(1, 0, 2),
             g_wkv.reshape(N_CHIPS, D_MODEL // N_CHIPS, 2 * MEM_W),
             g_wo.reshape(N_CHIPS, MIX_W // N_CHIPS, D_MODEL)]
    trs = (128, 128, 256)
    names = ("w_in", "w_mem_kv", "w_out")
    recv = _pair_exchange(slabs)
    pair = [_sum_pair(g, r, cidx, tr=tr, name=f"sum_pair_{nm}") for g, r, tr, nm in zip(slabs, recv, trs, names)]
    got = [lax.dynamic_update_slice(g, lax.dynamic_slice(p, (me_chip, 0, 0), (1,) + p.shape[1:]), (me_chip, 0, 0))
           for g, p in zip(_chip_exchange(pair), pair)]
    red = [_sum_chips(p, tr=tr, name=f"sum_chips_{nm}") for p, tr, nm in zip(got, trs, names)]
    sib = _pair_swap(red)
    grads = [jnp.where(core == 0, jnp.concatenate([r, s], axis=0), jnp.concatenate([s, r], axis=0)) for r, s in zip(red, sib)]

    outs = {}
    for nm, g, w, m, v, tr in zip(names, grads, (w_in, w_mem_kv, w_out), (m_w_in, m_w_mem_kv, m_w_out),
                                  (v_w_in, v_w_mem_kv, v_w_out), trs):
        d, nmo, nvo = _adamw(w[0], g, m[0], v[0], tr=tr, name=f"adamw_{nm}")
        outs[nm] = tuple(a[None] for a in (g, d, nmo, nvo))

    gsum = _small_allreduce(small)
    sd, sm, sv = _adamw(_pack_small(norm_g, mem_norm_g, final_norm_g, b_forget), gsum,
                        _pack_small(m_norm_g, m_mem_norm_g, m_final_norm_g, m_b_forget),
                        _pack_small(v_norm_g, v_mem_norm_g, v_final_norm_g, v_b_forget), tr=8, name="adamw_small")
    loss = gsum[LOSS_ROW, 0]

    def group(i, small_arr):
        ng, bf, mg, fg = _unpack_small(small_arr)
        return (ng, outs["w_in"][i], bf, mg, outs["w_mem_kv"][i], outs["w_out"][i], fg)

    return (loss, gx, *group(0, gsum), *group(1, sd), *group(2, sm), *group(3, sv))
```

```python
import functools
import math

import jax
import jax.numpy as jnp
from jax import lax
from jax.experimental import pallas as pl
from jax.experimental.pallas import tpu as pltpu

F32 = jnp.float32
BF16 = jnp.bfloat16

D_MODEL = 1024
SEQ = 2048
HEAD_DIM = 64
FOX_HEADS = 12
DIL_HEADS = 12
MEM_HEADS = 4
MEM_HEAD_DIM = 128
MEM_LEN = 256
FOX_W = FOX_HEADS * HEAD_DIM
DIL_W = DIL_HEADS * HEAD_DIM
MEM_W = MEM_HEADS * MEM_HEAD_DIM
MIX_W = FOX_W + DIL_W + MEM_W
DILATIONS = ((128, 1), (512, 4), (2048, 16))
ROPE_THETA = 500000.0
ROPE_DIM = HEAD_DIM // 4
RMS_EPS = 1e-6
NEG_INF = -1e30
IN_SIZES = [FOX_W] * 4 + [FOX_HEADS] + [DIL_W] * 4 + [MEM_W] * 2
IN_W = sum(IN_SIZES)

ADAM_LR = 0.001
ADAM_B1 = 0.9
ADAM_B2 = 0.999
ADAM_EPS = 1e-08
ADAM_WD = 0.01
ADAM_STEP = 10

LANES = 128
N_CHIPS = 4
PW = 7168
PWF = PW + LANES
C_FQ, C_FK, C_FV, C_FG = 0, 768, 1536, 2304
C_DQ, C_DK, C_DV, C_DG = 3072, 3840, 4608, 5376
C_MQ, C_MG = 6144, 6656
VMEM_LIMIT = 48 * 1024 * 1024


def _cparams(**kw):
    return pltpu.CompilerParams(vmem_limit_bytes=VMEM_LIMIT, **kw)


def _matmul(a, b, *, out_dtype, tm, tn, tk, name):
    m, kdim = a.shape
    _, n = b.shape
    nk = kdim // tk
    assert m % tm == 0 and n % tn == 0 and kdim % tk == 0

    def body(a_ref, b_ref, o_ref, acc_ref):
        k = pl.program_id(2)

        @pl.when(k == 0)
        def _():
            acc_ref[...] = jnp.zeros_like(acc_ref)

        acc_ref[...] += jnp.dot(a_ref[...], b_ref[...], preferred_element_type=F32)

        @pl.when(k == nk - 1)
        def _():
            o_ref[...] = acc_ref[...].astype(o_ref.dtype)

    return pl.pallas_call(
        body,
        out_shape=jax.ShapeDtypeStruct((m, n), out_dtype),
        grid=(m // tm, n // tn, nk),
        in_specs=[pl.BlockSpec((tm, tk), lambda i, j, k: (i, k)), pl.BlockSpec((tk, tn), lambda i, j, k: (k, j))],
        out_specs=pl.BlockSpec((tm, tn), lambda i, j, k: (i, j)),
        scratch_shapes=[pltpu.VMEM((tm, tn), F32)],
        compiler_params=_cparams(dimension_semantics=("parallel", "parallel", "arbitrary")),
        name=name,
    )(a, b)


def _rms_fwd(x, g, *, tm, name):
    t, d = x.shape

    def body(x_ref, g_ref, h_ref):
        xv = x_ref[...]
        r = lax.rsqrt(jnp.mean(xv * xv, axis=-1, keepdims=True) + RMS_EPS)
        h_ref[...] = (xv * r * g_ref[...]).astype(h_ref.dtype)

    return pl.pallas_call(
        body,
        out_shape=jax.ShapeDtypeStruct((t, d), BF16),
        grid=(t // tm,),
        in_specs=[pl.BlockSpec((tm, d), lambda i: (i, 0)), pl.BlockSpec((1, d), lambda i: (0, 0))],
        out_specs=pl.BlockSpec((tm, d), lambda i: (i, 0)),
        compiler_params=_cparams(),
        name=name,
    )(x, g)


def _rope_tables():
    half = ROPE_DIM // 2
    pos = jnp.arange(SEQ, dtype=F32)
    inv_freq = 1.0 / (ROPE_THETA ** (jnp.arange(0, ROPE_DIM, 2, dtype=F32) / ROPE_DIM))
    ang = pos[:, None] * inv_freq[None, :]
    cos, sin = jnp.cos(ang), jnp.sin(ang)
    one = jnp.ones((SEQ, HEAD_DIM - ROPE_DIM), F32)
    zero = jnp.zeros((SEQ, HEAD_DIM - ROPE_DIM), F32)
    zh = jnp.zeros((SEQ, half), F32)
    c = jnp.concatenate([cos, cos, one], axis=1)
    s1 = jnp.concatenate([zh, sin, zero], axis=1)
    s2 = jnp.concatenate([-sin, zh, zero], axis=1)
    rep = LANES // HEAD_DIM
    return jnp.tile(c, (1, rep)), jnp.tile(s1, (1, rep)), jnp.tile(s2, (1, rep))


def _rope_apply(t, c, s1, s2, transpose=False):
    n = t.shape[-1]
    rep = n // LANES
    c, s1, s2 = (jnp.tile(u, (1, rep)) for u in (c, s1, s2))
    half = ROPE_DIM // 2
    if not transpose:
        return t * c + pltpu.roll(t, half, 1) * s1 + pltpu.roll(t, n - half, 1) * s2
    return t * c + pltpu.roll(t * s1, n - half, 1) + pltpu.roll(t * s2, half, 1)


def _proj(h, w, tabs, *, tm, tn, name):
    t, d = h.shape
    n = w.shape[1]
    assert C_DQ % tn == 0 and (C_DV - C_DQ) % tn == 0 and (C_DG - C_DQ) % tn == 0
    rope_lo, rope_hi, dil_hi = C_DQ // tn, C_DV // tn, C_DG // tn
    s_blocks = SEQ // tm

    def body(h_ref, w_ref, c_ref, s1_ref, s2_ref, o_ref, f_ref):
        j = pl.program_id(1)
        acc = jnp.dot(h_ref[...], w_ref[...], preferred_element_type=F32)
        is_rope = jnp.logical_and(j >= rope_lo, j < rope_hi)

        @pl.when(is_rope)
        def _():
            r = _rope_apply(acc, c_ref[...], s1_ref[...], s2_ref[...])
            o_ref[...] = r.astype(o_ref.dtype)
            f_ref[...] = r

        @pl.when(jnp.logical_not(is_rope))
        def _():
            o_ref[...] = acc.astype(o_ref.dtype)

        @pl.when(jnp.logical_and(j >= rope_hi, j < dil_hi))
        def _():
            f_ref[...] = acc

    tab_spec = pl.BlockSpec((tm, LANES), lambda i, j: (i % s_blocks, 0))
    f_spec = pl.BlockSpec((tm, tn), lambda i, j: (i, jnp.clip(j - rope_lo, 0, dil_hi - rope_lo - 1)))
    return pl.pallas_call(
        body,
        out_shape=(jax.ShapeDtypeStruct((t, n), BF16), jax.ShapeDtypeStruct((t, 3 * DIL_W), F32)),
        grid=(t // tm, n // tn),
        in_specs=[pl.BlockSpec((tm, d), lambda i, j: (i, 0)), pl.BlockSpec((d, tn), lambda i, j: (0, j)),
                  tab_spec, tab_spec, tab_spec],
        out_specs=(pl.BlockSpec((tm, tn), lambda i, j: (i, j)), f_spec),
        compiler_params=_cparams(dimension_semantics=("parallel", "arbitrary")),
        name=name,
    )(h, w, *tabs)


def _split3(x):
    hi = x.astype(BF16)
    r1 = x - hi.astype(F32)
    mid = r1.astype(BF16)
    lo = (r1 - mid.astype(F32)).astype(BF16)
    return hi, mid, lo


def _dot3(sel, x, sel_is_lhs):
    out = None
    for piece in _split3(x):
        t = jnp.dot(sel, piece, preferred_element_type=F32) if sel_is_lhs else jnp.dot(piece, sel, preferred_element_type=F32)
        out = t if out is None else out + t
    return out


def _flog_fwd(flog, bpad, *, nb, ts, name):
    ns = SEQ // ts

    def body(f_ref, b_ref, c_ref, carry_ref):
        s = pl.program_id(1)

        @pl.when(s == 0)
        def _():
            carry_ref[...] = jnp.zeros_like(carry_ref)

        z = f_ref[...] + b_ref[...]
        logf = jnp.minimum(z, 0.0) - jnp.log(1.0 + jnp.exp(-jnp.abs(z)))
        r = lax.broadcasted_iota(jnp.int32, (ts, ts), 0)
        c = lax.broadcasted_iota(jnp.int32, (ts, ts), 1)
        tri = jnp.where(r >= c, 1.0, 0.0).astype(BF16)
        cs = _dot3(tri, logf, True) + carry_ref[0:1, :]
        carry_ref[...] = jnp.broadcast_to(cs[ts - 1:ts, :], carry_ref.shape)
        c_ref[...] = cs

    return pl.pallas_call(
        body,
        out_shape=jax.ShapeDtypeStruct((nb * SEQ, LANES), F32),
        grid=(nb, ns),
        in_specs=[pl.BlockSpec((ts, LANES), lambda b, s: (b * ns + s, 0)), pl.BlockSpec((1, LANES), lambda b, s: (0, 0))],
        out_specs=pl.BlockSpec((ts, LANES), lambda b, s: (b * ns + s, 0)),
        scratch_shapes=[pltpu.VMEM((8, LANES), F32)],
        compiler_params=_cparams(dimension_semantics=("parallel", "arbitrary")),
        name=name,
    )(flog, bpad)


def _flog_bwd(dcol, flog, bpad, *, nb, ts, name):
    ns = SEQ // ts

    def body(d_ref, f_ref, b_ref, o_ref, gb_ref, carry_ref):
        bi = pl.program_id(0)
        s = pl.program_id(1)

        @pl.when(s == 0)
        def _():
            carry_ref[...] = jnp.zeros_like(carry_ref)

        @pl.when(jnp.logical_and(bi == 0, s == 0))
        def _():
            gb_ref[...] = jnp.zeros_like(gb_ref)

        r = lax.broadcasted_iota(jnp.int32, (ts, ts), 0)
        c = lax.broadcasted_iota(jnp.int32, (ts, ts), 1)
        tri = jnp.where(r <= c, 1.0, 0.0).astype(BF16)
        rc = _dot3(tri, d_ref[...], True) + carry_ref[0:1, :]
        carry_ref[...] = jnp.broadcast_to(rc[0:1, :], carry_ref.shape)
        z = f_ref[...] + b_ref[...]
        dz = rc / (1.0 + jnp.exp(z))
        o_ref[...] = dz.astype(o_ref.dtype)
        gb_ref[...] += jnp.broadcast_to(jnp.sum(dz, axis=0, keepdims=True), gb_ref.shape)

    rev = lambda b, s: (b * ns + (ns - 1 - s), 0)
    return pl.pallas_call(
        body,
        out_shape=(jax.ShapeDtypeStruct((nb * SEQ, LANES), BF16), jax.ShapeDtypeStruct((8, LANES), F32)),
        grid=(nb, ns),
        in_specs=[pl.BlockSpec((ts, LANES), rev), pl.BlockSpec((ts, LANES), rev), pl.BlockSpec((1, LANES), lambda b, s: (0, 0))],
        out_specs=(pl.BlockSpec((ts, LANES), rev), pl.BlockSpec((8, LANES), lambda b, s: (0, 0))),
        scratch_shapes=[pltpu.VMEM((8, LANES), F32)],
        compiler_params=_cparams(dimension_semantics=("arbitrary", "arbitrary")),
        name=name,
    )(dcol, flog, bpad)


class _AttnCfg:
    def __init__(self, *, e, tq, tk, lq, lk, causal, window, ncol, qcol, kcol, vcol, split_p=False):
        self.e, self.tq, self.tk, self.lq, self.lk = e, tq, tk, lq, lk
        self.split_p = split_p
        self.causal, self.window = causal, window
        self.ncol, self.qcol, self.kcol, self.vcol = ncol, qcol, kcol, vcol
        self.nh = LANES // e
        self.scale = 1.0 / math.sqrt(e)
        self.nq, self.nk = lq // tq, lk // tk

    def k_range(self, i):
        if not self.causal:
            return 0, self.nk
        hi = ((i + 1) * self.tq - 1) // self.tk + 1
        if self.window is None:
            return 0, hi
        return jnp.maximum((i * self.tq - self.window) // self.tk, 0), hi


def _head_masks(nh):
    lane = lax.broadcasted_iota(jnp.int32, (1, LANES), 1)
    return [None] if nh == 1 else [lane < HEAD_DIM, lane >= HEAD_DIM]


def _sel(mask, a, b):
    return a if mask is None else jnp.where(mask, a, b)


def _scores(cfg, qh, kb, q0, k0, dlt0, bias):
    s = lax.dot_general(qh, kb, (((1,), (1,)), ((), ())), preferred_element_type=F32) * cfg.scale
    if bias is not None:
        s = s + bias
    if cfg.causal:
        d = dlt0 + (q0 - k0)
        if cfg.window is None:
            ok = d >= 0
        else:
            ok = d.astype(jnp.uint32) <= jnp.uint32(cfg.window)
        s = jnp.where(ok, s, NEG_INF)
    return s


def _attn_fwd(cfg, q, k, v, *, out_cols, bias=None, state=None, finalize=True, name):
    g = q.shape[0]
    tq, tk, e, nh = cfg.tq, cfg.tk, cfg.e, cfg.nh

    def body(*refs):
        refs = list(refs)
        q_ref, k_ref, v_ref = refs[:3]
        del refs[:3]
        if bias is not None:
            cb_ref, cr_ref = refs[:2]
            del refs[:2]
        if state is not None:
            ai_ref, mi_ref, li_ref = refs[:3]
            del refs[:3]
        out_refs = refs
        masks = _head_masks(nh)
        dlt0 = lax.broadcasted_iota(jnp.int32, (tq, tk), 0) - lax.broadcasted_iota(jnp.int32, (tq, tk), 1)

        def qbody(i, carry):
            q0 = pl.multiple_of(i * tq, tq)
            rows = pl.ds(q0, tq)
            qb = q_ref[rows, :]
            lo, hi = cfg.k_range(i)
            res = []
            for h in range(nh):
                qh = _sel(masks[h], qb, jnp.zeros_like(qb))
                if state is not None:
                    m0 = mi_ref[rows, h * e:h * e + 1]
                    l0 = li_ref[rows, h * e:h * e + 1]
                    a0 = ai_ref[rows, :]
                else:
                    m0 = jnp.full((tq, 1), NEG_INF, F32)
                    l0 = jnp.zeros((tq, 1), F32)
                    a0 = jnp.zeros((tq, LANES), F32)
                cq = cb_ref[rows, h * e:h * e + 1] if bias is not None else None

                def kbody(jk, c, qh=qh, cq=cq, h=h):
                    m, l, a = c
                    k0 = pl.multiple_of(jk * tk, tk)
                    kb = k_ref[pl.ds(k0, tk), :]
                    vb = v_ref[pl.ds(k0, tk), :]
                    b = (cq - cr_ref[jk, h:h + 1, :]) if bias is not None else None
                    s = _scores(cfg, qh, kb, q0, k0, dlt0, b)
                    m_new = jnp.maximum(m, jnp.max(s, axis=1, keepdims=True))
                    alpha = jnp.exp(m - m_new)
                    p = jnp.exp(s - m_new)
                    l = alpha * l + jnp.sum(p, axis=1, keepdims=True)
                    pb = p.astype(BF16)
                    pv = jnp.dot(pb, vb, preferred_element_type=F32)
                    if cfg.split_p:
                        pv = pv + jnp.dot((p - pb.astype(F32)).astype(BF16), vb, preferred_element_type=F32)
                    a = alpha * a + pv
                    return m_new, l, a

                res.append(lax.fori_loop(lo, hi, kbody, (m0, l0, a0)))
            if nh == 1:
                m, l, a = res[0]
                m, l = jnp.broadcast_to(m, (tq, LANES)), jnp.broadcast_to(l, (tq, LANES))
            else:
                m = jnp.where(masks[0], res[0][0], res[1][0])
                l = jnp.where(masks[0], res[0][1], res[1][1])
                a = jnp.where(masks[0], res[0][2], res[1][2])
            if finalize:
                out_refs[0][rows, :] = a / l
                out_refs[1][rows, :] = m + jnp.log(l)
            else:
                out_refs[0][rows, :] = a
                out_refs[1][rows, :] = m
                out_refs[2][rows, :] = l
            return carry

        lax.fori_loop(0, cfg.nq, qbody, 0)

    qspec = pl.BlockSpec((None, cfg.lq, LANES), lambda b, j: (b, 0, cfg.qcol(j)))
    kspec = pl.BlockSpec((None, cfg.lk, LANES), lambda b, j: (b, 0, cfg.kcol(j)))
    vspec = pl.BlockSpec((None, cfg.lk, LANES), lambda b, j: (b, 0, cfg.vcol(j)))
    ospec = pl.BlockSpec((None, cfg.lq, LANES), lambda b, j: (b, 0, j))
    args, in_specs = [q, k, v], [qspec, kspec, vspec]
    if bias is not None:
        args += list(bias)
        in_specs += [ospec, pl.BlockSpec((None, None, cfg.nk, 8, tk), lambda b, j: (b, j, 0, 0, 0))]
    aliases = {}
    if state is not None:
        aliases = {len(args) + t: t for t in range(3 if not finalize else 2)}
        args += list(state)
        in_specs += [ospec] * 3
    n_out = 2 if finalize else 3
    osd = jax.ShapeDtypeStruct((g, cfg.lq, out_cols), F32)
    return pl.pallas_call(
        body,
        out_shape=(osd,) * n_out,
        grid=(g, cfg.ncol),
        in_specs=in_specs,
        out_specs=(ospec,) * n_out,
        input_output_aliases=aliases,
        compiler_params=_cparams(dimension_semantics=("parallel", "parallel")),
        name=name,
    )(*args)


def _attn_bwd(cfg, q, k, v, do, o, lse, *, out_cols, kv_cols, bias=None, acc=None, name):
    g = q.shape[0]
    tq, tk, e, nh = cfg.tq, cfg.tk, cfg.e, cfg.nh
    t0 = (((0,), (0,)), ((), ()))

    def body(*refs):
        refs = list(refs)
        q_ref, k_ref, v_ref, do_ref, o_ref, lse_ref = refs[:6]
        del refs[:6]
        if bias is not None:
            cb_ref, cr_ref = refs[:2]
            del refs[:2]
        if acc is not None:
            dqi_ref, dki_ref, dvi_ref = refs[:3]
            del refs[:3]
        dq_ref, dk_ref, dv_ref = refs[:3]
        dcr_ref = refs[3] if bias is not None else None
        masks = _head_masks(nh)
        dlt0 = lax.broadcasted_iota(jnp.int32, (tq, tk), 0) - lax.broadcasted_iota(jnp.int32, (tq, tk), 1)
        if acc is not None:
            dq_ref[...] = dqi_ref[...]
            dk_ref[...] = dki_ref[...]
            dv_ref[...] = dvi_ref[...]
        else:
            dq_ref[...] = jnp.zeros_like(dq_ref)
            dk_ref[...] = jnp.zeros_like(dk_ref)
            dv_ref[...] = jnp.zeros_like(dv_ref)
        if dcr_ref is not None:
            dcr_ref[...] = jnp.zeros_like(dcr_ref)

        def qbody(i, carry):
            q0 = pl.multiple_of(i * tq, tq)
            rows = pl.ds(q0, tq)
            qb = q_ref[rows, :]
            dob = do_ref[rows, :].astype(BF16)
            prod = dob.astype(F32) * o_ref[rows, :]
            lo, hi = cfg.k_range(i)
            dqs = []
            for h in range(nh):
                qh = _sel(masks[h], qb, jnp.zeros_like(qb))
                doh = _sel(masks[h], dob, jnp.zeros_like(dob))
                lse_h = lse_ref[rows, h * e:h * e + 1]
                delta = jnp.sum(_sel(masks[h], prod, jnp.zeros_like(prod)), axis=1, keepdims=True)
                cq = cb_ref[rows, h * e:h * e + 1] if bias is not None else None

                def kbody(jk, dq_acc, qh=qh, doh=doh, lse_h=lse_h, delta=delta, cq=cq, h=h):
                    k0 = pl.multiple_of(jk * tk, tk)
                    krows = pl.ds(k0, tk)
                    kb = k_ref[krows, :]
                    vb = v_ref[krows, :]
                    b = (cq - cr_ref[jk, h:h + 1, :]) if bias is not None else None
                    s = _scores(cfg, qh, kb, q0, k0, dlt0, b)
                    p = jnp.exp(s - lse_h)
                    dp = lax.dot_general(doh, vb, (((1,), (1,)), ((), ())), preferred_element_type=F32)
                    ds = p * (dp - delta)
                    if dcr_ref is not None:
                        dcr_ref[jk, h:h + 1, :] += jnp.sum(ds, axis=0, keepdims=True)
                    dsb = (ds * cfg.scale).astype(BF16)
                    dv_ref[krows, :] += lax.dot_general(p.astype(BF16), doh, t0, preferred_element_type=F32)
                    dk_ref[krows, :] += lax.dot_general(dsb, qh, t0, preferred_element_type=F32)
                    return dq_acc + jnp.dot(dsb, kb, preferred_element_type=F32)

                dqs.append(lax.fori_loop(lo, hi, kbody, jnp.zeros((tq, LANES), F32)))
            dq = dqs[0] if nh == 1 else jnp.where(masks[0], dqs[0], dqs[1])
            dq_ref[rows, :] += dq
            return carry

        lax.fori_loop(0, cfg.nq, qbody, 0)

    qspec = pl.BlockSpec((None, cfg.lq, LANES), lambda b, j: (b, 0, cfg.qcol(j)))
    kspec = pl.BlockSpec((None, cfg.lk, LANES), lambda b, j: (b, 0, cfg.kcol(j)))
    vspec = pl.BlockSpec((None, cfg.lk, LANES), lambda b, j: (b, 0, cfg.vcol(j)))
    ospec = pl.BlockSpec((None, cfg.lq, LANES), lambda b, j: (b, 0, j))
    kvspec = pl.BlockSpec((None, cfg.lk, LANES), lambda b, j: (b, 0, j))
    args, in_specs = [q, k, v, do, o, lse], [qspec, kspec, vspec, ospec, ospec, ospec]
    out_shape = [jax.ShapeDtypeStruct((g, cfg.lq, out_cols), F32), jax.ShapeDtypeStruct((g, cfg.lk, kv_cols), F32),
                 jax.ShapeDtypeStruct((g, cfg.lk, kv_cols), F32)]
    out_specs = [ospec, kvspec, kvspec]
    if bias is not None:
        args += list(bias)
        crspec = pl.BlockSpec((None, None, cfg.nk, 8, tk), lambda b, j: (b, j, 0, 0, 0))
        in_specs += [ospec, crspec]
        out_shape.append(jax.ShapeDtypeStruct((g, cfg.ncol, cfg.nk, 8, tk), F32))
        out_specs.append(crspec)
    aliases = {}
    if acc is not None:
        aliases = {len(args) + t: t for t in range(3)}
        args += list(acc)
        in_specs += [ospec, kvspec, kvspec]
    return pl.pallas_call(
        body,
        out_shape=tuple(out_shape),
        grid=(g, cfg.ncol),
        in_specs=in_specs,
        out_specs=tuple(out_specs),
        input_output_aliases=aliases,
        compiler_params=_cparams(dimension_semantics=("parallel", "parallel")),
        name=name,
    )(*args)


BLK = 128
NBLK = SEQ // BLK
QK_SCALE = 1.0 / math.sqrt(HEAD_DIM)
DIL_STEPS = tuple(d for _, d in DILATIONS)
assert all(w // d == BLK for w, d in DILATIONS)
_T0 = (((0,), (0,)), ((), ()))
_NT = (((1,), (1,)), ((), ()))


def _stack_heads(a, masks):
    z = jnp.zeros_like(a)
    return jnp.concatenate([jnp.where(masks[0], a, z), jnp.where(masks[1], a, z)], axis=0)


def _tri_bias(lower):
    r = lax.broadcasted_iota(jnp.int32, (BLK, BLK), 0)
    c = lax.broadcasted_iota(jnp.int32, (BLK, BLK), 1)
    return jnp.where((c <= r) if lower else (c >= r), 0.0, NEG_INF).astype(F32)


def _dil_rows(r, i, d):
    start = r + i * (BLK * d)
    return pl.ds(start, BLK) if d == 1 else pl.ds(start, BLK, stride=d)


def _dil_blocks(d, fn):
    nbk = SEQ // d // BLK
    if d == 1:
        lax.fori_loop(0, nbk, lambda i, c: (fn(0, i, None), c)[1], 0)
    elif nbk > 1:
        def rbody(r, c):
            for i in range(nbk):
                fn(r, i, i > 0)
            return c
        lax.fori_loop(0, d, rbody, 0)
    else:
        def rbody(rr, c):
            fn(2 * rr, 0, False)
            fn(2 * rr + 1, 0, False)
            return c
        lax.fori_loop(0, d // 2, rbody, 0)


def _dil_key_tiles(r, i, d, has_prev, qrows, tri_cur, tri_prev):
    tiles = [(qrows, tri_cur)]
    if has_prev is None:
        tiles.append((_dil_rows(r, jnp.maximum(i - 1, 0), d), tri_prev + jnp.where(i > 0, 0.0, NEG_INF)))
    elif has_prev:
        tiles.append((_dil_rows(r, i - 1, d), tri_prev))
    return tiles


def _dil_fwd(qkv, *, name):
    nb = qkv.shape[0]
    ncol = DIL_W // LANES
    hd = HEAD_DIM

    def body(q_ref, k_ref, v_ref, o_ref, lse_ref, m_ref, l_ref, a_ref):
        masks = _head_masks(2)
        tri_cur, tri_prev = _tri_bias(True), _tri_bias(False)
        for pi, d in enumerate(DIL_STEPS):
            first, last = pi == 0, pi == len(DIL_STEPS) - 1

            def blk(r, i, has_prev, d=d, first=first, last=last):
                qrows = _dil_rows(r, i, d)
                qcat = _stack_heads((q_ref[qrows, :] * QK_SCALE).astype(BF16), masks)
                ss, vcats = [], []
                for krows, bias in _dil_key_tiles(r, i, d, has_prev, qrows, tri_cur, tri_prev):
                    s = lax.dot_general(qcat, k_ref[krows, :].astype(BF16), _NT, preferred_element_type=F32)
                    ss.append((s[:BLK] + bias, s[BLK:] + bias))
                    vcats.append(_stack_heads(v_ref[krows, :].astype(BF16), masks))
                e0 = ss[0][0] if len(ss) == 1 else jnp.maximum(ss[0][0], ss[1][0])
                e1 = ss[0][1] if len(ss) == 1 else jnp.maximum(ss[0][1], ss[1][1])
                n0 = jnp.max(e0, axis=1, keepdims=True)
                n1 = jnp.max(e1, axis=1, keepdims=True)
                if not first:
                    mo, lo = m_ref[qrows, :], l_ref[qrows, :]
                    m0, m1 = mo[:, 0:1], mo[:, hd:hd + 1]
                    n0, n1 = jnp.maximum(n0, m0), jnp.maximum(n1, m1)
                    a0, a1 = jnp.exp(m0 - n0), jnp.exp(m1 - n1)
                ps = [(jnp.exp(s0 - n0), jnp.exp(s1 - n1)) for s0, s1 in ss]
                t0 = ps[0][0] if len(ps) == 1 else ps[0][0] + ps[1][0]
                t1 = ps[0][1] if len(ps) == 1 else ps[0][1] + ps[1][1]
                l0 = jnp.sum(t0, axis=1, keepdims=True)
                l1 = jnp.sum(t1, axis=1, keepdims=True)
                acc = None
                for (p0, p1), vcat in zip(ps, vcats):
                    pv = jnp.dot(jnp.concatenate([p0, p1], axis=1).astype(BF16), vcat, preferred_element_type=F32)
                    acc = pv if acc is None else acc + pv
                if not first:
                    l0 = l0 + a0 * lo[:, 0:1]
                    l1 = l1 + a1 * lo[:, hd:hd + 1]
                    acc = acc + a_ref[qrows, :] * jnp.where(masks[0], a0, a1)
                if last:
                    o_ref[qrows, :] = acc / jnp.where(masks[0], l0, l1)
                    lse_ref[qrows, :] = jnp.where(masks[0], n0 + jnp.log(l0), n1 + jnp.log(l1))
                else:
                    m_ref[qrows, :] = jnp.where(masks[0], n0, n1)
                    l_ref[qrows, :] = jnp.where(masks[0], l0, l1)
                    a_ref[qrows, :] = acc

            _dil_blocks(d, blk)

    spec = lambda off: pl.BlockSpec((None, SEQ, LANES), lambda b, j: (b, 0, off + j))
    ospec = pl.BlockSpec((None, SEQ, LANES), lambda b, j: (b, 0, j))
    osd = jax.ShapeDtypeStruct((nb, SEQ, DIL_W), F32)
    return pl.pallas_call(
        body, out_shape=(osd, osd), grid=(nb, ncol),
        in_specs=[spec(0), spec(ncol), spec(2 * ncol)], out_specs=(ospec, ospec),
        scratch_shapes=[pltpu.VMEM((SEQ, LANES), F32)] * 3,
        compiler_params=_cparams(dimension_semantics=("parallel", "parallel")), name=name,
    )(qkv, qkv, qkv)


def _dil_bwd(qkv, do, o, lse, *, name):
    nb = qkv.shape[0]
    ncol = DIL_W // LANES
    hd = HEAD_DIM

    def body(q_ref, k_ref, v_ref, do_ref, o_ref, lse_ref, dq_ref, dk_ref, dv_ref, dl_ref):
        masks = _head_masks(2)
        tri_cur, tri_prev = _tri_bias(True), _tri_bias(False)
        dq_ref[...] = jnp.zeros_like(dq_ref)
        dk_ref[...] = jnp.zeros_like(dk_ref)
        dv_ref[...] = jnp.zeros_like(dv_ref)

        def delta_body(i, c):
            rows = pl.ds(pl.multiple_of(i * BLK, BLK), BLK)
            prod = do_ref[rows, :].astype(BF16).astype(F32) * o_ref[rows, :]
            z = jnp.zeros_like(prod)
            dl_ref[rows, :] = jnp.where(masks[0], jnp.sum(jnp.where(masks[0], prod, z), axis=1, keepdims=True),
                                        jnp.sum(jnp.where(masks[1], prod, z), axis=1, keepdims=True))
            return c

        lax.fori_loop(0, NBLK, delta_body, 0)

        for d in DIL_STEPS:
            def blk(r, i, has_prev, d=d):
                qrows = _dil_rows(r, i, d)
                qcat = _stack_heads((q_ref[qrows, :] * QK_SCALE).astype(BF16), masks)
                docat = _stack_heads(do_ref[qrows, :].astype(BF16), masks)
                lseb, dlb = lse_ref[qrows, :], dl_ref[qrows, :]
                lse0, lse1 = lseb[:, 0:1], lseb[:, hd:hd + 1]
                dl0, dl1 = dlb[:, 0:1], dlb[:, hd:hd + 1]
                dq = None
                for krows, bias in _dil_key_tiles(r, i, d, has_prev, qrows, tri_cur, tri_prev):
                    kf = k_ref[krows, :]
                    vb = v_ref[krows, :].astype(BF16)
                    s = lax.dot_general(qcat, kf.astype(BF16), _NT, preferred_element_type=F32)
                    p0 = jnp.exp(s[:BLK] + bias - lse0)
                    p1 = jnp.exp(s[BLK:] + bias - lse1)
                    dp = lax.dot_general(docat, vb, _NT, preferred_element_type=F32)
                    ds0 = p0 * (dp[:BLK] - dl0)
                    ds1 = p1 * (dp[BLK:] - dl1)
                    pcat = jnp.concatenate([p0, p1], axis=0).astype(BF16)
                    dscat = jnp.concatenate([ds0, ds1], axis=0).astype(BF16)
                    dv_ref[krows, :] += lax.dot_general(pcat, docat, _T0, preferred_element_type=F32)
                    dk_ref[krows, :] += lax.dot_general(dscat, qcat, _T0, preferred_element_type=F32)
                    dsrow = jnp.concatenate([ds0, ds1], axis=1).astype(BF16)
                    t = jnp.dot(dsrow, _stack_heads((kf * QK_SCALE).astype(BF16), masks), preferred_element_type=F32)
                    dq = t if dq is None else dq + t
                dq_ref[qrows, :] += dq

            _dil_blocks(d, blk)

    spec = lambda off: pl.BlockSpec((None, SEQ, LANES), lambda b, j: (b, 0, off + j))
    ospec = pl.BlockSpec((None, SEQ, LANES), lambda b, j: (b, 0, j))
    osd = jax.ShapeDtypeStruct((nb, SEQ, DIL_W), F32)
    return pl.pallas_call(
        body, out_shape=(osd, osd, osd), grid=(nb, ncol),
        in_specs=[spec(0), spec(ncol), spec(2 * ncol), ospec, ospec, ospec], out_specs=(ospec, ospec, ospec),
        scratch_shapes=[pltpu.VMEM((SEQ, LANES), F32)],
        compiler_params=_cparams(dimension_semantics=("parallel", "parallel")), name=name,
    )(qkv, qkv, qkv, do, o, lse)


FOX_GROUP = 4
assert NBLK % FOX_GROUP == 0
_FOX_COLS = tuple(c // LANES for c in (C_FQ, C_FK, C_FV))


def _fox_specs():
    cols = [pl.BlockSpec((None, SEQ, LANES), (lambda b, j, off=off: (b, 0, off + j))) for off in _FOX_COLS]
    ospec = pl.BlockSpec((None, SEQ, LANES), lambda b, j: (b, 0, j))
    crspec = pl.BlockSpec((None, None, NBLK, 8, BLK), lambda b, j: (b, j, 0, 0, 0))
    return cols, ospec, crspec


def _fox_key_rows(t, e):
    return pl.ds(pl.multiple_of((FOX_GROUP * t + e) * BLK, BLK), BLK)


def _fox_fwd(p3, crow, *, name):
    nb = p3.shape[0]
    g = FOX_GROUP

    def body(q_ref, k_ref, v_ref, cr_ref, o_ref, lse_ref):
        masks = _head_masks(2)
        tri = _tri_bias(True)

        def qk(qcat, t):
            return tuple(lax.dot_general(qcat, k_ref[_fox_key_rows(t, e), :], _NT, preferred_element_type=F32) for e in range(g))

        def consume(ss, t, state, nblk, diag):
            m0, m1, l0, l1, acc = state
            us = []
            for e in range(nblk):
                cr = cr_ref[g * t + e]
                u0 = ss[e][:BLK] - cr[0:1, :]
                u1 = ss[e][BLK:] - cr[1:2, :]
                if diag and e == nblk - 1:
                    u0, u1 = u0 + tri, u1 + tri
                us.append((u0, u1))
            x0 = functools.reduce(jnp.maximum, [u[0] for u in us])
            x1 = functools.reduce(jnp.maximum, [u[1] for u in us])
            n0 = jnp.maximum(m0, jnp.max(x0, axis=1, keepdims=True))
            n1 = jnp.maximum(m1, jnp.max(x1, axis=1, keepdims=True))
            a0, a1 = jnp.exp(m0 - n0), jnp.exp(m1 - n1)
            acc = acc * jnp.where(masks[0], a0, a1)
            t0 = t1 = None
            for e in range(nblk):
                p0, p1 = jnp.exp(us[e][0] - n0), jnp.exp(us[e][1] - n1)
                t0 = p0 if t0 is None else t0 + p0
                t1 = p1 if t1 is None else t1 + p1
                pcat = jnp.concatenate([p0, p1], axis=1)
                hi = pcat.astype(BF16)
                lo = (pcat - hi.astype(F32)).astype(BF16)
                vcat = _stack_heads(v_ref[_fox_key_rows(t, e), :], masks)
                acc = acc + jnp.dot(hi, vcat, preferred_element_type=F32) + jnp.dot(lo, vcat, preferred_element_type=F32)
            l0 = a0 * l0 + jnp.sum(t0, axis=1, keepdims=True)
            l1 = a1 * l1 + jnp.sum(t1, axis=1, keepdims=True)
            return n0, n1, l0, l1, acc

        def qblock(ng, a):
            rows = pl.ds(pl.multiple_of((g * ng + a) * BLK, BLK), BLK)
            qcat = _stack_heads(q_ref[rows, :] * QK_SCALE, masks)
            neg = jnp.full((BLK, 1), NEG_INF, F32)
            z1 = jnp.zeros((BLK, 1), F32)
            state = (neg, neg, z1, z1, jnp.zeros((BLK, LANES), F32))

            def step(t, c):
                ss, st = c
                nxt = qk(qcat, t + 1)
                return nxt, consume(ss, t, st, g, False)

            ss, state = lax.fori_loop(0, ng, step, (qk(qcat, 0), state))
            m0, m1, l0, l1, acc = consume(ss, ng, state, a + 1, True)
            o_ref[rows, :] = acc / jnp.where(masks[0], l0, l1)
            lse_ref[rows, :] = jnp.where(masks[0], m0 + jnp.log(l0), m1 + jnp.log(l1))

        def gbody(ng, c):
            for a in range(g):
                qblock(ng, a)
            return c

        lax.fori_loop(0, NBLK // g, gbody, 0)

    cols, ospec, crspec = _fox_specs()
    osd = jax.ShapeDtypeStruct((nb, SEQ, FOX_W), F32)
    return pl.pallas_call(
        body, out_shape=(osd, osd), grid=(nb, FOX_W // LANES), in_specs=cols + [crspec], out_specs=(ospec, ospec),
        compiler_params=_cparams(dimension_semantics=("parallel", "parallel")), name=name,
    )(p3, p3, p3, crow)


def _fox_bwd(p3, crow, do, o, lse, *, name):
    nb = p3.shape[0]
    g = FOX_GROUP
    hd = HEAD_DIM

    def body(q_ref, k_ref, v_ref, cr_ref, do_ref, o_ref, lse_ref, dq_ref, dk_ref, dv_ref, dcr_ref):
        masks = _head_masks(2)
        tri = _tri_bias(True)
        dk_ref[...] = jnp.zeros_like(dk_ref)
        dv_ref[...] = jnp.zeros_like(dv_ref)
        dcr_ref[...] = jnp.zeros_like(dcr_ref)

        def products(qcat, docat, t):
            out = []
            for e in range(g):
                krows = _fox_key_rows(t, e)
                out.append(lax.dot_general(qcat, k_ref[krows, :], _NT, preferred_element_type=F32))
                out.append(lax.dot_general(docat, v_ref[krows, :], _NT, preferred_element_type=F32))
            return tuple(out)

        def consume(prod, t, ctx, dq, nblk, diag):
            qcat, docat, lse0, lse1, dl0, dl1 = ctx
            for e in range(nblk):
                jb = g * t + e
                krows = _fox_key_rows(t, e)
                s, dp = prod[2 * e], prod[2 * e + 1]
                cr = cr_ref[jb]
                u0 = s[:BLK] - cr[0:1, :]
                u1 = s[BLK:] - cr[1:2, :]
                if diag and e == nblk - 1:
                    u0, u1 = u0 + tri, u1 + tri
                p0 = jnp.exp(u0 - lse0)
                p1 = jnp.exp(u1 - lse1)
                ds0 = p0 * (dp[:BLK] - dl0)
                ds1 = p1 * (dp[BLK:] - dl1)
                dcr_ref[jb, 0:1, :] += jnp.sum(ds0, axis=0, keepdims=True)
                dcr_ref[jb, 1:2, :] += jnp.sum(ds1, axis=0, keepdims=True)
                pcat = jnp.concatenate([p0, p1], axis=0).astype(BF16)
                dscat = jnp.concatenate([ds0, ds1], axis=0).astype(BF16)
                dv_ref[krows, :] += lax.dot_general(pcat, docat, _T0, preferred_element_type=F32)
                dk_ref[krows, :] += lax.dot_general(dscat, qcat, _T0, preferred_element_type=F32)
                dsrow = jnp.concatenate([ds0, ds1], axis=1).astype(BF16)
                dq = dq + jnp.dot(dsrow, _stack_heads(k_ref[krows, :] * QK_SCALE, masks), preferred_element_type=F32)
            return dq

        def qblock(ng, a):
            rows = pl.ds(pl.multiple_of((g * ng + a) * BLK, BLK), BLK)
            qcat = _stack_heads(q_ref[rows, :] * QK_SCALE, masks)
            dob = do_ref[rows, :].astype(BF16)
            docat = _stack_heads(dob, masks)
            prod = dob.astype(F32) * o_ref[rows, :]
            z = jnp.zeros_like(prod)
            dl0 = jnp.sum(jnp.where(masks[0], prod, z), axis=1, keepdims=True)
            dl1 = jnp.sum(jnp.where(masks[1], prod, z), axis=1, keepdims=True)
            lseb = lse_ref[rows, :]
            ctx = (qcat, docat, lseb[:, 0:1], lseb[:, hd:hd + 1], dl0, dl1)

            def step(t, c):
                pr, dq = c
                nxt = products(qcat, docat, t + 1)
                return nxt, consume(pr, t, ctx, dq, g, False)

            pr, dq = lax.fori_loop(0, ng, step, (products(qcat, docat, 0), jnp.zeros((BLK, LANES), F32)))
            dq_ref[rows, :] = consume(pr, ng, ctx, dq, a + 1, True)

        def gbody(ng, c):
            for a in range(g):
                qblock(ng, a)
            return c

        lax.fori_loop(0, NBLK // g, gbody, 0)

    cols, ospec, crspec = _fox_specs()
    osd = jax.ShapeDtypeStruct((nb, SEQ, FOX_W), F32)
    return pl.pallas_call(
        body, out_shape=(osd, osd, osd, jax.ShapeDtypeStruct((nb, FOX_W // LANES, NBLK, 8, BLK), F32)),
        grid=(nb, FOX_W // LANES), in_specs=cols + [crspec, ospec, ospec, ospec], out_specs=(ospec, ospec, ospec, crspec),
        compiler_params=_cparams(dimension_semantics=("parallel", "parallel")), name=name,
    )(p3, p3, p3, crow, do, o, lse)


def _mem_cfg():
    return _AttnCfg(e=MEM_HEAD_DIM, tq=256, tk=MEM_LEN, lq=SEQ, lk=MEM_LEN, causal=False, window=None, ncol=MEM_HEADS,
                    qcol=lambda j: C_MQ // LANES + j, kcol=lambda j: j, vcol=lambda j: MEM_HEADS + j)


N_YBLK = MIX_W // LANES
_GATE_BLK = (C_FG // LANES, C_DG // LANES, C_MG // LANES)
_B1, _B2 = FOX_W // LANES, (FOX_W + DIL_W) // LANES


def _att_specs(tm):
    fspec = pl.BlockSpec((tm, LANES), lambda i, j: (i, jnp.minimum(j, _B1 - 1)))
    dspec = pl.BlockSpec((tm, LANES), lambda i, j: (i, jnp.clip(j - _B1, 0, _B2 - _B1 - 1)))
    mspec = pl.BlockSpec((tm, LANES), lambda i, j: (i, jnp.clip(j - _B2, 0, N_YBLK - _B2 - 1)))

    def gcol(j):
        return jnp.where(j < _B1, _GATE_BLK[0] + j, jnp.where(j < _B2, _GATE_BLK[1] + j - _B1, _GATE_BLK[2] + j - _B2))

    gspec = pl.BlockSpec((tm, LANES), lambda i, j: (i, gcol(j)))
    return fspec, dspec, mspec, gspec


def _pick_att(j, f_ref, d_ref, m_ref):
    return jnp.where(j < _B1, f_ref[...], jnp.where(j < _B2, d_ref[...], m_ref[...]))


def _gate_fwd(fox, dil, memo, p16, *, tm, name):
    t = fox.shape[0]

    def body(f_ref, d_ref, m_ref, g_ref, y_ref):
        j = pl.program_id(1)
        a = _pick_att(j, f_ref, d_ref, m_ref)
        gt = g_ref[...].astype(F32)
        y_ref[...] = (a * gt / (1.0 + jnp.exp(-gt))).astype(y_ref.dtype)

    return pl.pallas_call(
        body,
        out_shape=jax.ShapeDtypeStruct((t, MIX_W), BF16),
        grid=(t // tm, N_YBLK),
        in_specs=list(_att_specs(tm)),
        out_specs=pl.BlockSpec((tm, LANES), lambda i, j: (i, j)),
        compiler_params=_cparams(dimension_semantics=("parallel", "parallel")),
        name=name,
    )(fox, dil, memo, p16)


def _gate_bwd(dy, fox, dil, memo, p16, *, tm, name):
    t = fox.shape[0]

    def body(dy_ref, f_ref, d_ref, m_ref, g_ref, da_ref, dg_ref):
        j = pl.program_id(1)
        a = _pick_att(j, f_ref, d_ref, m_ref)
        gt = g_ref[...].astype(F32)
        sg = 1.0 / (1.0 + jnp.exp(-gt))
        dyv = dy_ref[...]
        da_ref[...] = dyv * gt * sg
        dg_ref[...] = (dyv * a * sg * (1.0 + gt * (1.0 - sg))).astype(dg_ref.dtype)

    yspec = pl.BlockSpec((tm, LANES), lambda i, j: (i, j))
    return pl.pallas_call(
        body,
        out_shape=(jax.ShapeDtypeStruct((t, MIX_W), F32), jax.ShapeDtypeStruct((t, MIX_W), BF16)),
        grid=(t // tm, N_YBLK),
        in_specs=[yspec] + list(_att_specs(tm)),
        out_specs=(yspec, yspec),
        compiler_params=_cparams(dimension_semantics=("parallel", "parallel")),
        name=name,
    )(dy, fox, dil, memo, p16)


def _out_loss(y, wo, x, tgt, gfin, *, tm, name):
    t, d = x.shape
    n_feat = float(d)

    def body(y_ref, w_ref, x_ref, t_ref, g_ref, dx_ref, dxb_ref, st_ref):
        i = pl.program_id(0)

        @pl.when(i == 0)
        def _():
            st_ref[...] = jnp.zeros_like(st_ref)

        x2 = x_ref[...] + jnp.dot(y_ref[...], w_ref[...], preferred_element_type=F32)
        r = lax.rsqrt(jnp.mean(x2 * x2, axis=-1, keepdims=True) + RMS_EPS)
        nrm = x2 * r
        gv = g_ref[...]
        err = nrm * gv - t_ref[...]
        dout = err * (1.0 / n_feat)
        dn = dout * gv
        dx2 = r * (dn - nrm * jnp.mean(dn * nrm, axis=-1, keepdims=True))
        dx_ref[...] = dx2
        dxb_ref[...] = dx2.astype(dxb_ref.dtype)
        st_ref[0:1, :] += jnp.sum(dout * nrm, axis=0, keepdims=True)
        st_ref[1:2, :] += (0.5 / n_feat) * jnp.sum(err * err, axis=0, keepdims=True)

    row = pl.BlockSpec((tm, d), lambda i: (i, 0))
    return pl.pallas_call(
        body,
        out_shape=(jax.ShapeDtypeStruct((t, d), F32), jax.ShapeDtypeStruct((t, d), BF16), jax.ShapeDtypeStruct((8, d), F32)),
        grid=(t // tm,),
        in_specs=[pl.BlockSpec((tm, MIX_W), lambda i: (i, 0)), pl.BlockSpec((MIX_W, d), lambda i: (0, 0)), row, row,
                  pl.BlockSpec((1, d), lambda i: (0, 0))],
        out_specs=(row, row, pl.BlockSpec((8, d), lambda i: (0, 0))),
        compiler_params=_cparams(dimension_semantics=("arbitrary",)),
        name=name,
    )(y, wo, x, tgt, gfin)


def _dh_rms_bwd(dp, wt, x, g, resid, *, tm, tk, name):
    t, d = x.shape
    kdim = dp.shape[1]
    nk = kdim // tk

    def body(*refs):
        if resid is not None:
            dp_ref, w_ref, x_ref, g_ref, r_ref, dx_ref, gg_ref, acc_ref = refs
        else:
            dp_ref, w_ref, x_ref, g_ref, dx_ref, gg_ref, acc_ref = refs
        i = pl.program_id(0)
        k = pl.program_id(1)

        @pl.when(jnp.logical_and(i == 0, k == 0))
        def _():
            gg_ref[...] = jnp.zeros_like(gg_ref)

        @pl.when(k == 0)
        def _():
            acc_ref[...] = jnp.zeros_like(acc_ref)

        acc_ref[...] += jnp.dot(dp_ref[...], w_ref[...], preferred_element_type=F32)

        @pl.when(k == nk - 1)
        def _():
            dh = acc_ref[...]
            xv = x_ref[...]
            r = lax.rsqrt(jnp.mean(xv * xv, axis=-1, keepdims=True) + RMS_EPS)
            nrm = xv * r
            dn = dh * g_ref[...]
            dx = r * (dn - nrm * jnp.mean(dn * nrm, axis=-1, keepdims=True))
            if resid is not None:
                dx = dx + r_ref[...]
            dx_ref[...] = dx
            gg_ref[0:1, :] += jnp.sum(dh * nrm, axis=0, keepdims=True)

    row = pl.BlockSpec((tm, d), lambda i, k: (i, 0))
    in_specs = [pl.BlockSpec((tm, tk), lambda i, k: (i, k)), pl.BlockSpec((tk, d), lambda i, k: (k, 0)), row,
                pl.BlockSpec((1, d), lambda i, k: (0, 0))]
    args = [dp, wt, x, g]
    if resid is not None:
        in_specs.append(row)
        args.append(resid)
    return pl.pallas_call(
        body,
        out_shape=(jax.ShapeDtypeStruct((t, d), F32), jax.ShapeDtypeStruct((8, d), F32)),
        grid=(t // tm, nk),
        in_specs=in_specs,
        out_specs=(row, pl.BlockSpec((8, d), lambda i, k: (0, 0))),
        scratch_shapes=[pltpu.VMEM((tm, d), F32)],
        compiler_params=_cparams(dimension_semantics=("arbitrary", "arbitrary")),
        name=name,
    )(*args)


def _rope_bwd(dq, tabs, *, tm, name):
    t, n = dq.shape
    s_blocks = SEQ // tm

    def body(d_ref, c_ref, s1_ref, s2_ref, o_ref):
        o_ref[...] = _rope_apply(d_ref[...], c_ref[...], s1_ref[...], s2_ref[...], transpose=True).astype(o_ref.dtype)

    tab_spec = pl.BlockSpec((tm, LANES), lambda i: (i % s_blocks, 0))
    return pl.pallas_call(
        body,
        out_shape=jax.ShapeDtypeStruct((t, n), BF16),
        grid=(t // tm,),
        in_specs=[pl.BlockSpec((tm, n), lambda i: (i, 0)), tab_spec, tab_spec, tab_spec],
        out_specs=pl.BlockSpec((tm, n), lambda i: (i, 0)),
        compiler_params=_cparams(),
        name=name,
    )(dq, *tabs)


def _rearrange_w_in(w):
    offs = [0]
    for s in IN_SIZES:
        offs.append(offs[-1] + s)
    pieces = [w[:, offs[i]:offs[i + 1]] for i in range(len(IN_SIZES))]
    flog = jnp.pad(pieces[4], ((0, 0), (0, LANES - FOX_HEADS)))
    return jnp.concatenate(pieces[:4] + pieces[5:] + [flog], axis=1)


def _restore_w_in_cols(g):
    return jnp.concatenate([g[:, :C_DQ], g[:, PW:PW + FOX_HEADS], g[:, C_DQ:PW]], axis=1)


def _local_grads(x, mem, norm_g, w_r, b_forget, mem_norm_g, w_kv, w_o, final_norm_g, tgt):
    nb = x.shape[0]
    t = nb * SEQ
    x2d = x.reshape(t, D_MODEL)
    tgt2d = tgt.reshape(t, D_MODEL)
    tabs = _rope_tables()
    bpad = jnp.pad(b_forget.reshape(1, FOX_HEADS), ((0, 0), (0, LANES - FOX_HEADS)))

    h = _rms_fwd(x2d, norm_g.reshape(1, D_MODEL), tm=512, name="rms_x")
    p16, dqkv = _proj(h, w_r[:, :PW], tabs, tm=2048, tn=256, name="proj")
    flog = _matmul(h, w_r[:, PW:], out_dtype=F32, tm=1024, tn=LANES, tk=D_MODEL, name="proj_flog")
    c12 = _flog_fwd(flog, bpad, nb=nb, ts=256, name="flog_fwd")

    crow = c12[:, :FOX_HEADS].reshape(nb, NBLK, BLK, FOX_HEADS // 2, 2).transpose(0, 3, 1, 4, 2)
    crow = jnp.pad(crow, ((0, 0), (0, 0), (0, 0), (0, 6), (0, 0)))
    p3 = p16.reshape(nb, SEQ, PW)
    fox, fox_lse = _fox_fwd(p3, crow, name="fox_fwd")

    dqkv3 = dqkv.reshape(nb, SEQ, 3 * DIL_W)
    dil, dil_lse = _dil_fwd(dqkv3, name="dil_fwd")

    mh = _rms_fwd(mem.reshape(nb * MEM_LEN, D_MODEL), mem_norm_g.reshape(1, D_MODEL), tm=nb * MEM_LEN, name="rms_mem")
    mkv = _matmul(mh, w_kv, out_dtype=BF16, tm=nb * MEM_LEN, tn=512, tk=D_MODEL, name="mem_kv")
    mkv3 = mkv.reshape(nb, MEM_LEN, 2 * MEM_W)
    mcfg = _mem_cfg()
    memo, mem_lse = _attn_fwd(mcfg, p3, mkv3, mkv3, out_cols=MEM_W, name="mem_fwd")

    fox2, dil2, memo2 = fox.reshape(t, FOX_W), dil.reshape(t, DIL_W), memo.reshape(t, MEM_W)
    y = _gate_fwd(fox2, dil2, memo2, p16, tm=1024, name="gate_fwd")
    dx2, dx2b, st = _out_loss(y, w_o, x2d, tgt2d, final_norm_g.reshape(1, D_MODEL), tm=512, name="out_loss")

    g_wo = _matmul(y.T, dx2b, out_dtype=F32, tm=1024, tn=512, tk=1024, name="grad_w_out")
    dy = _matmul(dx2b, w_o.T, out_dtype=F32, tm=1024, tn=512, tk=D_MODEL, name="d_y")
    datt, dgate = _gate_bwd(dy, fox2, dil2, memo2, p16, tm=1024, name="gate_bwd")
    dfox = datt[:, :FOX_W].reshape(nb, SEQ, FOX_W)
    ddil = datt[:, FOX_W:FOX_W + DIL_W].reshape(nb, SEQ, DIL_W)
    dmemo = datt[:, FOX_W + DIL_W:].reshape(nb, SEQ, MEM_W)

    dfq, dfk, dfv, dcr = _fox_bwd(p3, crow, dfox, fox, fox_lse, name="fox_bwd")
    dcol = -dcr[:, :, :, :2, :].transpose(0, 2, 4, 1, 3).reshape(t, FOX_HEADS)
    dcol = jnp.pad(dcol, ((0, 0), (0, LANES - FOX_HEADS)))
    dflog, gb = _flog_bwd(dcol, flog, bpad, nb=nb, ts=256, name="flog_bwd")

    ddq, ddk, ddv = (a.reshape(t, DIL_W) for a in _dil_bwd(dqkv3, ddil, dil, dil_lse, name="dil_bwd"))
    ddq = _rope_bwd(ddq, tabs, tm=512, name="rope_bwd_q")
    ddk = _rope_bwd(ddk, tabs, tm=512, name="rope_bwd_k")

    dmq, dmk, dmv = _attn_bwd(mcfg, p3, mkv3, mkv3, dmemo, memo, mem_lse, out_cols=MEM_W, kv_cols=MEM_W, name="mem_bwd")
    dmkv = jnp.concatenate([dmk, dmv], axis=-1).reshape(nb * MEM_LEN, 2 * MEM_W).astype(BF16)
    g_wkv = _matmul(mh.T, dmkv, out_dtype=F32, tm=512, tn=512, tk=nb * MEM_LEN, name="grad_w_kv")
    _, gmn = _dh_rms_bwd(dmkv, w_kv.T, mem.reshape(nb * MEM_LEN, D_MODEL), mem_norm_g.reshape(1, D_MODEL), None,
                         tm=nb * MEM_LEN, tk=2 * MEM_W, name="mem_rms_bwd")

    bf = lambda a: a.reshape(t, -1).astype(BF16)
    dp = jnp.concatenate([bf(dfq), bf(dfk), bf(dfv), dgate[:, :FOX_W], ddq, ddk, bf(ddv), dgate[:, FOX_W:FOX_W + DIL_W],
                          bf(dmq), dgate[:, FOX_W + DIL_W:], dflog], axis=1)
    g_wr = _matmul(h.T, dp, out_dtype=F32, tm=512, tn=PWF // 3, tk=1024, name="grad_w_in")
    gx, gng = _dh_rms_bwd(dp, w_r.T, x2d, norm_g.reshape(1, D_MODEL), dx2, tm=512, tk=PWF // 3, name="in_rms_bwd")

    gb_row = jnp.pad(gb[0:1, :], ((0, 0), (0, D_MODEL - LANES)))
    small = jnp.concatenate([gng[0:1], gmn[0:1], st[0:1], gb_row, st[1:2], jnp.zeros((3, D_MODEL), F32)], axis=0)
    return gx.reshape(nb, SEQ, D_MODEL), g_wr, g_wkv, g_wo, small


MESH = pl.DeviceIdType.MESH
ANY = pl.BlockSpec(memory_space=pl.ANY)


def _place():
    x, y, c = lax.axis_index("x"), lax.axis_index("y"), lax.axis_index("c")
    other_chips = [(1 - x, y), (x, 1 - y), (1 - x, 1 - y)]
    return x, y, c, other_chips


def _gather_weights(shards):
    n = len(shards)

    def body(*refs):
        in_refs, out_refs = refs[:n], refs[n:2 * n]
        send_sems, recv_sems = refs[2 * n:]
        x, y, c, chips = _place()
        me_chip = 2 * x + y
        sibling = (x, y, 1 - c)

        def half(ref, pc, rows):
            return ref.at[pl.ds(pc * (rows // 2), rows // 2), :]

        def rcopy(k, src, dst, to):
            return pltpu.make_async_remote_copy(src_ref=src, dst_ref=dst, send_sem=send_sems.at[k], recv_sem=recv_sems.at[k],
                                                device_id=to, device_id_type=MESH)

        sends = []
        for t in range(n):
            rows = shards[t].shape[0]
            for j, chip in enumerate(chips):
                cp = rcopy(6 * t + j, half(in_refs[t], c, rows), half(out_refs[t].at[me_chip], c, rows), (*chip, c))
                cp.start()
                sends.append(cp)
        for t in range(n):
            rows = shards[t].shape[0]
            for j, chip in enumerate(chips):
                slot = out_refs[t].at[2 * chip[0] + chip[1]]
                rcopy(6 * t + j, half(slot, c, rows), half(slot, c, rows), sibling).wait_recv()
                fw = rcopy(6 * t + 3 + j, half(slot, c, rows), half(slot, c, rows), sibling)
                fw.start()
                sends.append(fw)
        for t in range(n):
            rows = shards[t].shape[0]
            for j, chip in enumerate(chips):
                slot = out_refs[t].at[2 * chip[0] + chip[1]]
                rcopy(6 * t + 3 + j, half(slot, 1 - c, rows), half(slot, 1 - c, rows), sibling).wait_recv()
        for cp in sends:
            cp.wait_send()

    return pl.pallas_call(
        body,
        out_shape=tuple(jax.ShapeDtypeStruct((N_CHIPS,) + s.shape, s.dtype) for s in shards),
        in_specs=[ANY] * n,
        out_specs=tuple([ANY] * n),
        scratch_shapes=[pltpu.SemaphoreType.DMA((6 * n,)), pltpu.SemaphoreType.DMA((6 * n,))],
        name="gather_weights",
    )(*shards)


def _pair_exchange(gs):
    n = len(gs)

    def body(*refs):
        g_refs, r_refs = refs[:n], refs[n:2 * n]
        send_sems, recv_sems = refs[2 * n:]
        x, y, c, _ = _place()
        cps = []
        for t in range(n):
            hr = gs[t].shape[1] // 2
            cp = pltpu.make_async_remote_copy(src_ref=g_refs[t].at[:, pl.ds((1 - c) * hr, hr), :], dst_ref=r_refs[t],
                                              send_sem=send_sems.at[t], recv_sem=recv_sems.at[t],
                                              device_id=(x, y, 1 - c), device_id_type=MESH)
            cp.start()
            cps.append(cp)
        for cp in cps:
            cp.wait()

    return pl.pallas_call(
        body,
        out_shape=tuple(jax.ShapeDtypeStruct((N_CHIPS, g.shape[1] // 2, g.shape[2]), g.dtype) for g in gs),
        in_specs=[ANY] * n,
        out_specs=tuple([ANY] * n),
        scratch_shapes=[pltpu.SemaphoreType.DMA((n,)), pltpu.SemaphoreType.DMA((n,))],
        name="pair_exchange",
    )(*gs)


def _chip_exchange(ps):
    n = len(ps)

    def body(*refs):
        p_refs, o_refs = refs[:n], refs[n:2 * n]
        send_sems, recv_sems = refs[2 * n:]
        x, y, c, chips = _place()
        me_chip = 2 * x + y
        cps = []
        for t in range(n):
            for j, chip in enumerate(chips):
                cp = pltpu.make_async_remote_copy(src_ref=p_refs[t].at[2 * chip[0] + chip[1]], dst_ref=o_refs[t].at[me_chip],
                                                  send_sem=send_sems.at[3 * t + j], recv_sem=recv_sems.at[3 * t + j],
                                                  device_id=(*chip, c), device_id_type=MESH)
                cp.start()
                cps.append(cp)
        for cp in cps:
            cp.wait()

    return pl.pallas_call(
        body,
        out_shape=tuple(jax.ShapeDtypeStruct(p.shape, p.dtype) for p in ps),
        in_specs=[ANY] * n,
        out_specs=tuple([ANY] * n),
        scratch_shapes=[pltpu.SemaphoreType.DMA((3 * n,)), pltpu.SemaphoreType.DMA((3 * n,))],
        name="chip_exchange",
    )(*ps)


def _pair_swap(rs):
    n = len(rs)

    def body(*refs):
        r_refs, o_refs = refs[:n], refs[n:2 * n]
        send_sems, recv_sems = refs[2 * n:]
        x, y, c, _ = _place()
        cps = []
        for t in range(n):
            cp = pltpu.make_async_remote_copy(src_ref=r_refs[t], dst_ref=o_refs[t], send_sem=send_sems.at[t],
                                              recv_sem=recv_sems.at[t], device_id=(x, y, 1 - c), device_id_type=MESH)
            cp.start()
            cps.append(cp)
        for cp in cps:
            cp.wait()

    return pl.pallas_call(
        body,
        out_shape=tuple(jax.ShapeDtypeStruct(r.shape, r.dtype) for r in rs),
        in_specs=[ANY] * n,
        out_specs=tuple([ANY] * n),
        scratch_shapes=[pltpu.SemaphoreType.DMA((n,)), pltpu.SemaphoreType.DMA((n,))],
        name="pair_swap",
    )(*rs)


N_DEV = 8
LOSS_ROW = 4


def _small_allreduce(small):
    def body(s_ref, o_ref, all_ref, send_sems, recv_sems):
        x, y, c, _ = _place()
        me = 4 * x + 2 * y + c
        all_ref[me] = s_ref[...]
        cps = []
        for k in range(1, N_DEV):
            peer = tuple(1 - p if (k >> s) & 1 else p for p, s in ((x, 2), (y, 1), (c, 0)))
            cp = pltpu.make_async_remote_copy(src_ref=s_ref, dst_ref=all_ref.at[me], send_sem=send_sems.at[k - 1],
                                              recv_sem=recv_sems.at[k - 1], device_id=peer, device_id_type=MESH)
            cp.start()
            cps.append(cp)
        for cp in cps:
            cp.wait()
        tot = all_ref[0]
        for d in range(1, N_DEV):
            tot = tot + all_ref[d]
        o_ref[...] = tot
        o_ref[LOSS_ROW:LOSS_ROW + 1, :] = jnp.broadcast_to(jnp.sum(tot[LOSS_ROW:LOSS_ROW + 1, :], axis=1, keepdims=True),
                                                          (1, tot.shape[1]))

    vm = pl.BlockSpec(memory_space=pltpu.VMEM)
    return pl.pallas_call(
        body,
        out_shape=jax.ShapeDtypeStruct(small.shape, small.dtype),
        in_specs=[vm],
        out_specs=vm,
        scratch_shapes=[pltpu.VMEM((N_DEV,) + small.shape, small.dtype), pltpu.SemaphoreType.DMA((N_DEV - 1,)),
                        pltpu.SemaphoreType.DMA((N_DEV - 1,))],
        name="small_allreduce",
    )(small)


def _sum_pair(g, recv, cidx, *, tr, name):
    _, hr, cols = recv.shape
    nr = hr // tr

    def body(c_ref, g_ref, r_ref, o_ref):
        o_ref[...] = (g_ref[...] + r_ref[...]).astype(o_ref.dtype)

    grid_spec = pltpu.PrefetchScalarGridSpec(
        num_scalar_prefetch=1,
        grid=(N_CHIPS, nr),
        in_specs=[pl.BlockSpec((None, tr, cols), lambda k, i, c_ref: (k, c_ref[0] * nr + i, 0)),
                  pl.BlockSpec((None, tr, cols), lambda k, i, c_ref: (k, i, 0))],
        out_specs=pl.BlockSpec((None, tr, cols), lambda k, i, c_ref: (k, i, 0)),
    )
    return pl.pallas_call(body, out_shape=jax.ShapeDtypeStruct(recv.shape, BF16), grid_spec=grid_spec,
                          compiler_params=_cparams(), name=name)(cidx, g, recv)


def _sum_chips(p, *, tr, name):
    _, rows, cols = p.shape

    def body(p_ref, o_ref):
        tot = p_ref[0].astype(F32)
        for k in range(1, N_CHIPS):
            tot = tot + p_ref[k].astype(F32)
        o_ref[...] = tot

    return pl.pallas_call(
        body,
        out_shape=jax.ShapeDtypeStruct((rows, cols), F32),
        grid=(rows // tr,),
        in_specs=[pl.BlockSpec((N_CHIPS, tr, cols), lambda i: (0, i, 0))],
        out_specs=pl.BlockSpec((tr, cols), lambda i: (i, 0)),
        compiler_params=_cparams(),
        name=name,
    )(p)


def _adamw(w, g, m, v, *, tr, name):
    rows, cols = w.shape
    bc1 = 1.0 / (1.0 - ADAM_B1 ** ADAM_STEP)
    bc2 = 1.0 / (1.0 - ADAM_B2 ** ADAM_STEP)

    def body(w_ref, g_ref, m_ref, v_ref, d_ref, nm_ref, nv_ref):
        gv = g_ref[...]
        nm = ADAM_B1 * m_ref[...] + (1.0 - ADAM_B1) * gv
        nv = ADAM_B2 * v_ref[...] + (1.0 - ADAM_B2) * (gv * gv)
        d_ref[...] = -ADAM_LR * ((nm * bc1) / (jnp.sqrt(nv * bc2) + ADAM_EPS) + ADAM_WD * w_ref[...])
        nm_ref[...] = nm
        nv_ref[...] = nv

    spec = pl.BlockSpec((tr, cols), lambda i: (i, 0))
    sd = jax.ShapeDtypeStruct((rows, cols), F32)
    return pl.pallas_call(body, out_shape=(sd, sd, sd), grid=(rows // tr,), in_specs=[spec] * 4, out_specs=(spec,) * 3,
                          compiler_params=_cparams(), name=name)(w, g, m, v)


def _pack_small(norm, mem_norm, final_norm, b_forget):
    rows = [norm.reshape(1, D_MODEL), mem_norm.reshape(1, D_MODEL), final_norm.reshape(1, D_MODEL),
            jnp.pad(b_forget.reshape(1, FOX_HEADS), ((0, 0), (0, D_MODEL - FOX_HEADS))), jnp.zeros((4, D_MODEL), F32)]
    return jnp.concatenate(rows, axis=0)


def _unpack_small(a):
    return a[0:1], a[3:4, :FOX_HEADS], a[1:2], a[2]


def kernel(x, mem, norm_g, w_in, b_forget, mem_norm_g, w_mem_kv, w_out, final_norm_g, loss_target, m_norm_g, m_w_in, m_b_forget, m_mem_norm_g, m_w_mem_kv, m_w_out, m_final_norm_g, v_norm_g, v_w_in, v_b_forget, v_mem_norm_g, v_w_mem_kv, v_w_out, v_final_norm_g):
    core = lax.axis_index("c").astype(jnp.int32)
    me_chip = (2 * lax.axis_index("x") + lax.axis_index("y")).astype(jnp.int32)
    cidx = core.reshape(1)

    def own_slot(arr, own):
        return lax.dynamic_update_slice(arr, own[None].astype(arr.dtype), (me_chip,) + (0,) * own.ndim)

    mine = [w_in[0].astype(BF16), w_mem_kv[0].astype(BF16), w_out[0].astype(BF16)]
    g_in, g_kv, g_out = (own_slot(g, s) for g, s in zip(_gather_weights(mine), mine))
    w_r = _rearrange_w_in(jnp.concatenate([g_in[k] for k in range(N_CHIPS)], axis=1))
    w_kv = g_kv.reshape(D_MODEL, 2 * MEM_W)
    w_o = g_out.reshape(MIX_W, D_MODEL)

    gx, g_wr, g_wkv, g_wo, small = _local_grads(x, mem, norm_g, w_r, b_forget, mem_norm_g, w_kv, w_o, final_norm_g, loss_target)

    shard_w = IN_W // N_CHIPS
    slabs = [_restore_w_in_cols(g_wr).reshape(D_MODEL, N_CHIPS, shard_w).transpose(1, 0, 2),
             g_wkv.reshape(N_CHIPS, D_MODEL // N_CHIPS, 2 * MEM_W),
             g_wo.reshape(N_CHIPS, MIX_W // N_CHIPS, D_MODEL)]
    trs = (128, 128, 256)
    names = ("w_in", "w_mem_kv", "w_out")
    recv = _pair_exchange(slabs)
    pair = [_sum_pair(g, r, cidx, tr=tr, name=f"sum_pair_{nm}") for g, r, tr, nm in zip(slabs, recv, trs, names)]
    got = [lax.dynamic_update_slice(g, lax.dynamic_slice(p, (me_chip, 0, 0), (1,) + p.shape[1:]), (me_chip, 0, 0))
           for g, p in zip(_chip_exchange(pair), pair)]
    red = [_sum_chips(p, tr=tr, name=f"sum_chips_{nm}") for p, tr, nm in zip(got, trs, names)]
    sib = _pair_swap(red)
    grads = [jnp.where(core == 0, jnp.concatenate([r, s], axis=0), jnp.concatenate([s, r], axis=0)) for r, s in zip(red, sib)]

    outs = {}
    for nm, g, w, m, v, tr in zip(names, grads, (w_in, w_mem_kv, w_out), (m_w_in, m_w_mem_kv, m_w_out),
                                  (v_w_in, v_w_mem_kv, v_w_out), trs):
        d, nmo, nvo = _adamw(w[0], g, m[0], v[0], tr=tr, name=f"adamw_{nm}")
        outs[nm] = tuple(a[None] for a in (g, d, nmo, nvo))

    gsum = _small_allreduce(small)
    sd, sm, sv = _adamw(_pack_small(norm_g, mem_norm_g, final_norm_g, b_forget), gsum,
                        _pack_small(m_norm_g, m_mem_norm_g, m_final_norm_g, m_b_forget),
                        _pack_small(v_norm_g, v_mem_norm_g, v_final_norm_g, v_b_forget), tr=8, name="adamw_small")
    loss = gsum[LOSS_ROW, 0]

    def group(i, small_arr):
        ng, bf, mg, fg = _unpack_small(small_arr)
        return (ng, outs["w_in"][i], bf, mg, outs["w_mem_kv"][i], outs["w_out"][i], fg)

    return (loss, gx, *group(0, gsum), *group(1, sd), *group(2, sm), *group(3, sv))
```

```python
import functools
import math

import jax
import jax.numpy as jnp
from jax import lax
from jax.experimental import pallas as pl
from jax.experimental.pallas import tpu as pltpu

F32 = jnp.float32
BF16 = jnp.bfloat16

D_MODEL = 1024
SEQ = 2048
HEAD_DIM = 64
FOX_HEADS = 12
DIL_HEADS = 12
MEM_HEADS = 4
MEM_HEAD_DIM = 128
MEM_LEN = 256
FOX_W = FOX_HEADS * HEAD_DIM
DIL_W = DIL_HEADS * HEAD_DIM
MEM_W = MEM_HEADS * MEM_HEAD_DIM
MIX_W = FOX_W + DIL_W + MEM_W
DILATIONS = ((128, 1), (512, 4), (2048, 16))
ROPE_THETA = 500000.0
ROPE_DIM = HEAD_DIM // 4
RMS_EPS = 1e-6
NEG_INF = -1e30
IN_SIZES = [FOX_W] * 4 + [FOX_HEADS] + [DIL_W] * 4 + [MEM_W] * 2
IN_W = sum(IN_SIZES)

ADAM_LR = 0.001
ADAM_B1 = 0.9
ADAM_B2 = 0.999
ADAM_EPS = 1e-08
ADAM_WD = 0.01
ADAM_STEP = 10

LANES = 128
N_CHIPS = 4
PW = 7168
PWF = PW + LANES
C_FQ, C_FK, C_FV, C_FG = 0, 768, 1536, 2304
C_DQ, C_DK, C_DV, C_DG = 3072, 3840, 4608, 5376
C_MQ, C_MG = 6144, 6656
VMEM_LIMIT = 48 * 1024 * 1024


def _cparams(**kw):
    return pltpu.CompilerParams(vmem_limit_bytes=VMEM_LIMIT, **kw)


def _matmul(a, b, *, out_dtype, tm, tn, tk, name, mode="nn"):
    if mode == "tn":
        (kdim, m), n = a.shape, b.shape[1]
        a_spec = pl.BlockSpec((tk, tm), lambda i, j, k: (k, i))
        b_spec = pl.BlockSpec((tk, tn), lambda i, j, k: (k, j))
        dims = _T0
    elif mode == "nt":
        (m, kdim), n = a.shape, b.shape[0]
        a_spec = pl.BlockSpec((tm, tk), lambda i, j, k: (i, k))
        b_spec = pl.BlockSpec((tn, tk), lambda i, j, k: (j, k))
        dims = _NT
    else:
        (m, kdim), n = a.shape, b.shape[1]
        a_spec = pl.BlockSpec((tm, tk), lambda i, j, k: (i, k))
        b_spec = pl.BlockSpec((tk, tn), lambda i, j, k: (k, j))
        dims = (((1,), (0,)), ((), ()))
    nk = kdim // tk
    assert m % tm == 0 and n % tn == 0 and kdim % tk == 0

    def body(a_ref, b_ref, o_ref, acc_ref):
        k = pl.program_id(2)

        @pl.when(k == 0)
        def _():
            acc_ref[...] = jnp.zeros_like(acc_ref)

        acc_ref[...] += lax.dot_general(a_ref[...], b_ref[...], dims, preferred_element_type=F32)

        @pl.when(k == nk - 1)
        def _():
            o_ref[...] = acc_ref[...].astype(o_ref.dtype)

    return pl.pallas_call(
        body,
        out_shape=jax.ShapeDtypeStruct((m, n), out_dtype),
        grid=(m // tm, n // tn, nk),
        in_specs=[a_spec, b_spec],
        out_specs=pl.BlockSpec((tm, tn), lambda i, j, k: (i, j)),
        scratch_shapes=[pltpu.VMEM((tm, tn), F32)],
        compiler_params=_cparams(dimension_semantics=("parallel", "parallel", "arbitrary")),
        name=name,
    )(a, b)


def _rms_fwd(x, g, *, tm, name):
    t, d = x.shape

    def body(x_ref, g_ref, h_ref):
        xv = x_ref[...]
        r = lax.rsqrt(jnp.mean(xv * xv, axis=-1, keepdims=True) + RMS_EPS)
        h_ref[...] = (xv * r * g_ref[...]).astype(h_ref.dtype)

    return pl.pallas_call(
        body,
        out_shape=jax.ShapeDtypeStruct((t, d), BF16),
        grid=(t // tm,),
        in_specs=[pl.BlockSpec((tm, d), lambda i: (i, 0)), pl.BlockSpec((1, d), lambda i: (0, 0))],
        out_specs=pl.BlockSpec((tm, d), lambda i: (i, 0)),
        compiler_params=_cparams(),
        name=name,
    )(x, g)


def _rope_tables():
    half = ROPE_DIM // 2
    pos = jnp.arange(SEQ, dtype=F32)
    inv_freq = 1.0 / (ROPE_THETA ** (jnp.arange(0, ROPE_DIM, 2, dtype=F32) / ROPE_DIM))
    ang = pos[:, None] * inv_freq[None, :]
    cos, sin = jnp.cos(ang), jnp.sin(ang)
    one = jnp.ones((SEQ, HEAD_DIM - ROPE_DIM), F32)
    zero = jnp.zeros((SEQ, HEAD_DIM - ROPE_DIM), F32)
    zh = jnp.zeros((SEQ, half), F32)
    c = jnp.concatenate([cos, cos, one], axis=1)
    s1 = jnp.concatenate([zh, sin, zero], axis=1)
    s2 = jnp.concatenate([-sin, zh, zero], axis=1)
    rep = LANES // HEAD_DIM
    return jnp.tile(c, (1, rep)), jnp.tile(s1, (1, rep)), jnp.tile(s2, (1, rep))


def _rope_apply(t, c, s1, s2, transpose=False):
    n = t.shape[-1]
    rep = n // LANES
    c, s1, s2 = (jnp.tile(u, (1, rep)) for u in (c, s1, s2))
    half = ROPE_DIM // 2
    if not transpose:
        return t * c + pltpu.roll(t, half, 1) * s1 + pltpu.roll(t, n - half, 1) * s2
    return t * c + pltpu.roll(t * s1, n - half, 1) + pltpu.roll(t * s2, half, 1)


def _proj(h, w, tabs, *, n, tm, tn, name):
    t, d = h.shape
    assert C_DQ % tn == 0 and (C_DV - C_DQ) % tn == 0 and (C_DG - C_DQ) % tn == 0
    rope_lo, rope_hi, dil_hi = C_DQ // tn, C_DV // tn, C_DG // tn
    s_blocks = SEQ // tm

    def body(h_ref, w_ref, c_ref, s1_ref, s2_ref, o_ref, f_ref):
        j = pl.program_id(1)
        acc = jnp.dot(h_ref[...], w_ref[...], preferred_element_type=F32)
        is_rope = jnp.logical_and(j >= rope_lo, j < rope_hi)

        @pl.when(is_rope)
        def _():
            r = _rope_apply(acc, c_ref[...], s1_ref[...], s2_ref[...])
            o_ref[...] = r.astype(o_ref.dtype)
            f_ref[...] = r

        @pl.when(jnp.logical_not(is_rope))
        def _():
            o_ref[...] = acc.astype(o_ref.dtype)

        @pl.when(jnp.logical_and(j >= rope_hi, j < dil_hi))
        def _():
            f_ref[...] = acc

    tab_spec = pl.BlockSpec((tm, LANES), lambda i, j: (i % s_blocks, 0))
    f_spec = pl.BlockSpec((tm, tn), lambda i, j: (i, jnp.clip(j - rope_lo, 0, dil_hi - rope_lo - 1)))
    return pl.pallas_call(
        body,
        out_shape=(jax.ShapeDtypeStruct((t, n), BF16), jax.ShapeDtypeStruct((t, 3 * DIL_W), F32)),
        grid=(t // tm, n // tn),
        in_specs=[pl.BlockSpec((tm, d), lambda i, j: (i, 0)), pl.BlockSpec((d, tn), lambda i, j: (0, j)),
                  tab_spec, tab_spec, tab_spec],
        out_specs=(pl.BlockSpec((tm, tn), lambda i, j: (i, j)), f_spec),
        compiler_params=_cparams(dimension_semantics=("parallel", "arbitrary")),
        name=name,
    )(h, w, *tabs)


def _split3(x):
    hi = x.astype(BF16)
    r1 = x - hi.astype(F32)
    mid = r1.astype(BF16)
    lo = (r1 - mid.astype(F32)).astype(BF16)
    return hi, mid, lo


def _dot3(sel, x, sel_is_lhs):
    out = None
    for piece in _split3(x):
        t = jnp.dot(sel, piece, preferred_element_type=F32) if sel_is_lhs else jnp.dot(piece, sel, preferred_element_type=F32)
        out = t if out is None else out + t
    return out


def _flog_fwd(flog, bpad, *, nb, ts, name):
    ns = SEQ // ts

    def body(f_ref, b_ref, c_ref, carry_ref):
        s = pl.program_id(1)

        @pl.when(s == 0)
        def _():
            carry_ref[...] = jnp.zeros_like(carry_ref)

        z = f_ref[...] + b_ref[...]
        logf = jnp.minimum(z, 0.0) - jnp.log(1.0 + jnp.exp(-jnp.abs(z)))
        r = lax.broadcasted_iota(jnp.int32, (ts, ts), 0)
        c = lax.broadcasted_iota(jnp.int32, (ts, ts), 1)
        tri = jnp.where(r >= c, 1.0, 0.0).astype(BF16)
        cs = _dot3(tri, logf, True) + carry_ref[0:1, :]
        carry_ref[...] = jnp.broadcast_to(cs[ts - 1:ts, :], carry_ref.shape)
        c_ref[...] = cs

    return pl.pallas_call(
        body,
        out_shape=jax.ShapeDtypeStruct((nb * SEQ, LANES), F32),
        grid=(nb, ns),
        in_specs=[pl.BlockSpec((ts, LANES), lambda b, s: (b * ns + s, 0)), pl.BlockSpec((1, LANES), lambda b, s: (0, 0))],
        out_specs=pl.BlockSpec((ts, LANES), lambda b, s: (b * ns + s, 0)),
        scratch_shapes=[pltpu.VMEM((8, LANES), F32)],
        compiler_params=_cparams(dimension_semantics=("parallel", "arbitrary")),
        name=name,
    )(flog, bpad)


def _flog_bwd(dcol, flog, bpad, *, nb, ts, name):
    ns = SEQ // ts

    def body(d_ref, f_ref, b_ref, o_ref, gb_ref, carry_ref):
        bi = pl.program_id(0)
        s = pl.program_id(1)

        @pl.when(s == 0)
        def _():
            carry_ref[...] = jnp.zeros_like(carry_ref)

        @pl.when(jnp.logical_and(bi == 0, s == 0))
        def _():
            gb_ref[...] = jnp.zeros_like(gb_ref)

        r = lax.broadcasted_iota(jnp.int32, (ts, ts), 0)
        c = lax.broadcasted_iota(jnp.int32, (ts, ts), 1)
        tri = jnp.where(r <= c, 1.0, 0.0).astype(BF16)
        rc = _dot3(tri, d_ref[...], True) + carry_ref[0:1, :]
        carry_ref[...] = jnp.broadcast_to(rc[0:1, :], carry_ref.shape)
        z = f_ref[...] + b_ref[...]
        dz = rc / (1.0 + jnp.exp(z))
        o_ref[...] = dz.astype(o_ref.dtype)
        gb_ref[...] += jnp.broadcast_to(jnp.sum(dz, axis=0, keepdims=True), gb_ref.shape)

    rev = lambda b, s: (b * ns + (ns - 1 - s), 0)
    return pl.pallas_call(
        body,
        out_shape=(jax.ShapeDtypeStruct((nb * SEQ, LANES), BF16), jax.ShapeDtypeStruct((8, LANES), F32)),
        grid=(nb, ns),
        in_specs=[pl.BlockSpec((ts, LANES), rev), pl.BlockSpec((ts, LANES), rev), pl.BlockSpec((1, LANES), lambda b, s: (0, 0))],
        out_specs=(pl.BlockSpec((ts, LANES), rev), pl.BlockSpec((8, LANES), lambda b, s: (0, 0))),
        scratch_shapes=[pltpu.VMEM((8, LANES), F32)],
        compiler_params=_cparams(dimension_semantics=("arbitrary", "arbitrary")),
        name=name,
    )(dcol, flog, bpad)


class _AttnCfg:
    def __init__(self, *, e, tq, tk, lq, lk, causal, window, ncol, qcol, kcol, vcol, split_p=False):
        self.e, self.tq, self.tk, self.lq, self.lk = e, tq, tk, lq, lk
        self.split_p = split_p
        self.causal, self.window = causal, window
        self.ncol, self.qcol, self.kcol, self.vcol = ncol, qcol, kcol, vcol
        self.nh = LANES // e
        self.scale = 1.0 / math.sqrt(e)
        self.nq, self.nk = lq // tq, lk // tk

    def k_range(self, i):
        if not self.causal:
            return 0, self.nk
        hi = ((i + 1) * self.tq - 1) // self.tk + 1
        if self.window is None:
            return 0, hi
        return jnp.maximum((i * self.tq - self.window) // self.tk, 0), hi


def _head_masks(nh):
    lane = lax.broadcasted_iota(jnp.int32, (1, LANES), 1)
    return [None] if nh == 1 else [lane < HEAD_DIM, lane >= HEAD_DIM]


def _sel(mask, a, b):
    return a if mask is None else jnp.where(mask, a, b)


def _scores(cfg, qh, kb, q0, k0, dlt0, bias):
    s = lax.dot_general(qh, kb, (((1,), (1,)), ((), ())), preferred_element_type=F32) * cfg.scale
    if bias is not None:
        s = s + bias
    if cfg.causal:
        d = dlt0 + (q0 - k0)
        if cfg.window is None:
            ok = d >= 0
        else:
            ok = d.astype(jnp.uint32) <= jnp.uint32(cfg.window)
        s = jnp.where(ok, s, NEG_INF)
    return s


def _attn_fwd(cfg, q, k, v, *, out_cols, bias=None, state=None, finalize=True, name):
    g = q.shape[0]
    tq, tk, e, nh = cfg.tq, cfg.tk, cfg.e, cfg.nh

    def body(*refs):
        refs = list(refs)
        q_ref, k_ref, v_ref = refs[:3]
        del refs[:3]
        if bias is not None:
            cb_ref, cr_ref = refs[:2]
            del refs[:2]
        if state is not None:
            ai_ref, mi_ref, li_ref = refs[:3]
            del refs[:3]
        out_refs = refs
        masks = _head_masks(nh)
        dlt0 = lax.broadcasted_iota(jnp.int32, (tq, tk), 0) - lax.broadcasted_iota(jnp.int32, (tq, tk), 1)

        def qbody(i, carry):
            q0 = pl.multiple_of(i * tq, tq)
            rows = pl.ds(q0, tq)
            qb = q_ref[rows, :]
            lo, hi = cfg.k_range(i)
            res = []
            for h in range(nh):
                qh = _sel(masks[h], qb, jnp.zeros_like(qb))
                if state is not None:
                    m0 = mi_ref[rows, h * e:h * e + 1]
                    l0 = li_ref[rows, h * e:h * e + 1]
                    a0 = ai_ref[rows, :]
                else:
                    m0 = jnp.full((tq, 1), NEG_INF, F32)
                    l0 = jnp.zeros((tq, 1), F32)
                    a0 = jnp.zeros((tq, LANES), F32)
                cq = cb_ref[rows, h * e:h * e + 1] if bias is not None else None

                def kbody(jk, c, qh=qh, cq=cq, h=h):
                    m, l, a = c
                    k0 = pl.multiple_of(jk * tk, tk)
                    kb = k_ref[pl.ds(k0, tk), :]
                    vb = v_ref[pl.ds(k0, tk), :]
                    b = (cq - cr_ref[jk, h:h + 1, :]) if bias is not None else None
                    s = _scores(cfg, qh, kb, q0, k0, dlt0, b)
                    m_new = jnp.maximum(m, jnp.max(s, axis=1, keepdims=True))
                    alpha = jnp.exp(m - m_new)
                    p = jnp.exp(s - m_new)
                    l = alpha * l + jnp.sum(p, axis=1, keepdims=True)
                    pb = p.astype(BF16)
                    pv = jnp.dot(pb, vb, preferred_element_type=F32)
                    if cfg.split_p:
                        pv = pv + jnp.dot((p - pb.astype(F32)).astype(BF16), vb, preferred_element_type=F32)
                    a = alpha * a + pv
                    return m_new, l, a

                res.append(lax.fori_loop(lo, hi, kbody, (m0, l0, a0)))
            if nh == 1:
                m, l, a = res[0]
                m, l = jnp.broadcast_to(m, (tq, LANES)), jnp.broadcast_to(l, (tq, LANES))
            else:
                m = jnp.where(masks[0], res[0][0], res[1][0])
                l = jnp.where(masks[0], res[0][1], res[1][1])
                a = jnp.where(masks[0], res[0][2], res[1][2])
            if finalize:
                out_refs[0][rows, :] = a / l
                out_refs[1][rows, :] = m + jnp.log(l)
            else:
                out_refs[0][rows, :] = a
                out_refs[1][rows, :] = m
                out_refs[2][rows, :] = l
            return carry

        lax.fori_loop(0, cfg.nq, qbody, 0)

    qspec = pl.BlockSpec((None, cfg.lq, LANES), lambda b, j: (b, 0, cfg.qcol(j)))
    kspec = pl.BlockSpec((None, cfg.lk, LANES), lambda b, j: (b, 0, cfg.kcol(j)))
    vspec = pl.BlockSpec((None, cfg.lk, LANES), lambda b, j: (b, 0, cfg.vcol(j)))
    ospec = pl.BlockSpec((None, cfg.lq, LANES), lambda b, j: (b, 0, j))
    args, in_specs = [q, k, v], [qspec, kspec, vspec]
    if bias is not None:
        args += list(bias)
        in_specs += [ospec, pl.BlockSpec((None, None, cfg.nk, 8, tk), lambda b, j: (b, j, 0, 0, 0))]
    aliases = {}
    if state is not None:
        aliases = {len(args) + t: t for t in range(3 if not finalize else 2)}
        args += list(state)
        in_specs += [ospec] * 3
    n_out = 2 if finalize else 3
    osd = jax.ShapeDtypeStruct((g, cfg.lq, out_cols), F32)
    return pl.pallas_call(
        body,
        out_shape=(osd,) * n_out,
        grid=(g, cfg.ncol),
        in_specs=in_specs,
        out_specs=(ospec,) * n_out,
        input_output_aliases=aliases,
        compiler_params=_cparams(dimension_semantics=("parallel", "parallel")),
        name=name,
    )(*args)


def _attn_bwd(cfg, q, k, v, do, o, lse, *, out_cols, kv_cols, bias=None, acc=None, do_off=0, name):
    g = q.shape[0]
    tq, tk, e, nh = cfg.tq, cfg.tk, cfg.e, cfg.nh
    t0 = (((0,), (0,)), ((), ()))

    def body(*refs):
        refs = list(refs)
        q_ref, k_ref, v_ref, do_ref, o_ref, lse_ref = refs[:6]
        del refs[:6]
        if bias is not None:
            cb_ref, cr_ref = refs[:2]
            del refs[:2]
        if acc is not None:
            dqi_ref, dki_ref, dvi_ref = refs[:3]
            del refs[:3]
        dq_ref, dk_ref, dv_ref = refs[:3]
        dcr_ref = refs[3] if bias is not None else None
        masks = _head_masks(nh)
        dlt0 = lax.broadcasted_iota(jnp.int32, (tq, tk), 0) - lax.broadcasted_iota(jnp.int32, (tq, tk), 1)
        if acc is not None:
            dq_ref[...] = dqi_ref[...]
            dk_ref[...] = dki_ref[...]
            dv_ref[...] = dvi_ref[...]
        else:
            dq_ref[...] = jnp.zeros_like(dq_ref)
            dk_ref[...] = jnp.zeros_like(dk_ref)
            dv_ref[...] = jnp.zeros_like(dv_ref)
        if dcr_ref is not None:
            dcr_ref[...] = jnp.zeros_like(dcr_ref)

        def qbody(i, carry):
            q0 = pl.multiple_of(i * tq, tq)
            rows = pl.ds(q0, tq)
            qb = q_ref[rows, :]
            dob = do_ref[rows, :].astype(BF16)
            prod = dob.astype(F32) * o_ref[rows, :]
            lo, hi = cfg.k_range(i)
            dqs = []
            for h in range(nh):
                qh = _sel(masks[h], qb, jnp.zeros_like(qb))
                doh = _sel(masks[h], dob, jnp.zeros_like(dob))
                lse_h = lse_ref[rows, h * e:h * e + 1]
                delta = jnp.sum(_sel(masks[h], prod, jnp.zeros_like(prod)), axis=1, keepdims=True)
                cq = cb_ref[rows, h * e:h * e + 1] if bias is not None else None

                def kbody(jk, dq_acc, qh=qh, doh=doh, lse_h=lse_h, delta=delta, cq=cq, h=h):
                    k0 = pl.multiple_of(jk * tk, tk)
                    krows = pl.ds(k0, tk)
                    kb = k_ref[krows, :]
                    vb = v_ref[krows, :]
                    b = (cq - cr_ref[jk, h:h + 1, :]) if bias is not None else None
                    s = _scores(cfg, qh, kb, q0, k0, dlt0, b)
                    p = jnp.exp(s - lse_h)
                    dp = lax.dot_general(doh, vb, (((1,), (1,)), ((), ())), preferred_element_type=F32)
                    ds = p * (dp - delta)
                    if dcr_ref is not None:
                        dcr_ref[jk, h:h + 1, :] += jnp.sum(ds, axis=0, keepdims=True)
                    dsb = (ds * cfg.scale).astype(BF16)
                    dv_ref[krows, :] += lax.dot_general(p.astype(BF16), doh, t0, preferred_element_type=F32)
                    dk_ref[krows, :] += lax.dot_general(dsb, qh, t0, preferred_element_type=F32)
                    return dq_acc + jnp.dot(dsb, kb, preferred_element_type=F32)

                dqs.append(lax.fori_loop(lo, hi, kbody, jnp.zeros((tq, LANES), F32)))
            dq = dqs[0] if nh == 1 else jnp.where(masks[0], dqs[0], dqs[1])
            dq_ref[rows, :] += dq
            return carry

        lax.fori_loop(0, cfg.nq, qbody, 0)

    qspec = pl.BlockSpec((None, cfg.lq, LANES), lambda b, j: (b, 0, cfg.qcol(j)))
    kspec = pl.BlockSpec((None, cfg.lk, LANES), lambda b, j: (b, 0, cfg.kcol(j)))
    vspec = pl.BlockSpec((None, cfg.lk, LANES), lambda b, j: (b, 0, cfg.vcol(j)))
    ospec = pl.BlockSpec((None, cfg.lq, LANES), lambda b, j: (b, 0, j))
    kvspec = pl.BlockSpec((None, cfg.lk, LANES), lambda b, j: (b, 0, j))
    dospec = pl.BlockSpec((None, cfg.lq, LANES), lambda b, j: (b, 0, do_off + j))
    args, in_specs = [q, k, v, do, o, lse], [qspec, kspec, vspec, dospec, ospec, ospec]
    out_shape = [jax.ShapeDtypeStruct((g, cfg.lq, out_cols), F32), jax.ShapeDtypeStruct((g, cfg.lk, kv_cols), F32),
                 jax.ShapeDtypeStruct((g, cfg.lk, kv_cols), F32)]
    out_specs = [ospec, kvspec, kvspec]
    if bias is not None:
        args += list(bias)
        crspec = pl.BlockSpec((None, None, cfg.nk, 8, tk), lambda b, j: (b, j, 0, 0, 0))
        in_specs += [ospec, crspec]
        out_shape.append(jax.ShapeDtypeStruct((g, cfg.ncol, cfg.nk, 8, tk), F32))
        out_specs.append(crspec)
    aliases = {}
    if acc is not None:
        aliases = {len(args) + t: t for t in range(3)}
        args += list(acc)
        in_specs += [ospec, kvspec, kvspec]
    return pl.pallas_call(
        body,
        out_shape=tuple(out_shape),
        grid=(g, cfg.ncol),
        in_specs=in_specs,
        out_specs=tuple(out_specs),
        input_output_aliases=aliases,
        compiler_params=_cparams(dimension_semantics=("parallel", "parallel")),
        name=name,
    )(*args)


BLK = 128
NBLK = SEQ // BLK
QK_SCALE = 1.0 / math.sqrt(HEAD_DIM)
DIL_STEPS = tuple(d for _, d in DILATIONS)
assert all(w // d == BLK for w, d in DILATIONS)
_T0 = (((0,), (0,)), ((), ()))
_NT = (((1,), (1,)), ((), ()))


def _stack_heads(a, masks):
    z = jnp.zeros_like(a)
    return jnp.concatenate([jnp.where(masks[0], a, z), jnp.where(masks[1], a, z)], axis=0)


def _tri_bias(lower):
    r = lax.broadcasted_iota(jnp.int32, (BLK, BLK), 0)
    c = lax.broadcasted_iota(jnp.int32, (BLK, BLK), 1)
    return jnp.where((c <= r) if lower else (c >= r), 0.0, NEG_INF).astype(F32)


def _dil_rows(r, i, d):
    start = r + i * (BLK * d)
    return pl.ds(start, BLK) if d == 1 else pl.ds(start, BLK, stride=d)


def _dil_blocks(d, fn):
    nbk = SEQ // d // BLK
    if d == 1:
        lax.fori_loop(0, nbk, lambda i, c: (fn(0, i, None), c)[1], 0)
    elif nbk > 1:
        def rbody(r, c):
            for i in range(nbk):
                fn(r, i, i > 0)
            return c
        lax.fori_loop(0, d, rbody, 0)
    else:
        def rbody(rr, c):
            fn(2 * rr, 0, False)
            fn(2 * rr + 1, 0, False)
            return c
        lax.fori_loop(0, d // 2, rbody, 0)


def _dil_key_tiles(r, i, d, has_prev, qrows, tri_cur, tri_prev):
    tiles = [(qrows, tri_cur)]
    if has_prev is None:
        tiles.append((_dil_rows(r, jnp.maximum(i - 1, 0), d), tri_prev + jnp.where(i > 0, 0.0, NEG_INF)))
    elif has_prev:
        tiles.append((_dil_rows(r, i - 1, d), tri_prev))
    return tiles


def _dil_fwd(qkv, *, name):
    nb = qkv.shape[0]
    ncol = DIL_W // LANES
    hd = HEAD_DIM

    def body(q_ref, k_ref, v_ref, o_ref, lse_ref, m_ref, l_ref, a_ref):
        masks = _head_masks(2)
        tri_cur, tri_prev = _tri_bias(True), _tri_bias(False)
        for pi, d in enumerate(DIL_STEPS):
            first, last = pi == 0, pi == len(DIL_STEPS) - 1

            def blk(r, i, has_prev, d=d, first=first, last=last):
                qrows = _dil_rows(r, i, d)
                qcat = _stack_heads((q_ref[qrows, :] * QK_SCALE).astype(BF16), masks)
                ss, vcats = [], []
                for krows, bias in _dil_key_tiles(r, i, d, has_prev, qrows, tri_cur, tri_prev):
                    s = lax.dot_general(qcat, k_ref[krows, :].astype(BF16), _NT, preferred_element_type=F32)
                    ss.append((s[:BLK] + bias, s[BLK:] + bias))
                    vcats.append(_stack_heads(v_ref[krows, :].astype(BF16), masks))
                e0 = ss[0][0] if len(ss) == 1 else jnp.maximum(ss[0][0], ss[1][0])
                e1 = ss[0][1] if len(ss) == 1 else jnp.maximum(ss[0][1], ss[1][1])
                n0 = jnp.max(e0, axis=1, keepdims=True)
                n1 = jnp.max(e1, axis=1, keepdims=True)
                if not first:
                    mo, lo = m_ref[qrows, :], l_ref[qrows, :]
                    m0, m1 = mo[:, 0:1], mo[:, hd:hd + 1]
                    n0, n1 = jnp.maximum(n0, m0), jnp.maximum(n1, m1)
                    a0, a1 = jnp.exp(m0 - n0), jnp.exp(m1 - n1)
                ps = [(jnp.exp(s0 - n0), jnp.exp(s1 - n1)) for s0, s1 in ss]
                t0 = ps[0][0] if len(ps) == 1 else ps[0][0] + ps[1][0]
                t1 = ps[0][1] if len(ps) == 1 else ps[0][1] + ps[1][1]
                l0 = jnp.sum(t0, axis=1, keepdims=True)
                l1 = jnp.sum(t1, axis=1, keepdims=True)
                acc = None
                for (p0, p1), vcat in zip(ps, vcats):
                    pv = jnp.dot(jnp.concatenate([p0, p1], axis=1).astype(BF16), vcat, preferred_element_type=F32)
                    acc = pv if acc is None else acc + pv
                if not first:
                    l0 = l0 + a0 * lo[:, 0:1]
                    l1 = l1 + a1 * lo[:, hd:hd + 1]
                    acc = acc + a_ref[qrows, :] * jnp.where(masks[0], a0, a1)
                if last:
                    o_ref[qrows, :] = acc / jnp.where(masks[0], l0, l1)
                    lse_ref[qrows, :] = jnp.where(masks[0], n0 + jnp.log(l0), n1 + jnp.log(l1))
                else:
                    m_ref[qrows, :] = jnp.where(masks[0], n0, n1)
                    l_ref[qrows, :] = jnp.where(masks[0], l0, l1)
                    a_ref[qrows, :] = acc

            _dil_blocks(d, blk)

    spec = lambda off: pl.BlockSpec((None, SEQ, LANES), lambda b, j: (b, 0, off + j))
    ospec = pl.BlockSpec((None, SEQ, LANES), lambda b, j: (b, 0, j))
    osd = jax.ShapeDtypeStruct((nb, SEQ, DIL_W), F32)
    return pl.pallas_call(
        body, out_shape=(osd, osd), grid=(nb, ncol),
        in_specs=[spec(0), spec(ncol), spec(2 * ncol)], out_specs=(ospec, ospec),
        scratch_shapes=[pltpu.VMEM((SEQ, LANES), F32)] * 3,
        compiler_params=_cparams(dimension_semantics=("parallel", "parallel")), name=name,
    )(qkv, qkv, qkv)


def _dil_bwd(qkv, do, o, lse, tabs, *, do_off, name):
    nb = qkv.shape[0]
    ncol = DIL_W // LANES
    hd = HEAD_DIM

    def body(q_ref, k_ref, v_ref, do_ref, o_ref, lse_ref, c_ref, s1_ref, s2_ref, dqo_ref, dko_ref, dvo_ref,
             dq_ref, dk_ref, dv_ref, dl_ref, dof_ref):
        masks = _head_masks(2)
        tri_cur, tri_prev = _tri_bias(True), _tri_bias(False)
        dq_ref[...] = jnp.zeros_like(dq_ref)
        dk_ref[...] = jnp.zeros_like(dk_ref)
        dv_ref[...] = jnp.zeros_like(dv_ref)

        def delta_body(i, c):
            rows = pl.ds(pl.multiple_of(i * BLK, BLK), BLK)
            dof = do_ref[rows, :].astype(F32)
            dof_ref[rows, :] = dof
            prod = dof * o_ref[rows, :]
            z = jnp.zeros_like(prod)
            dl_ref[rows, :] = jnp.where(masks[0], jnp.sum(jnp.where(masks[0], prod, z), axis=1, keepdims=True),
                                        jnp.sum(jnp.where(masks[1], prod, z), axis=1, keepdims=True))
            return c

        lax.fori_loop(0, NBLK, delta_body, 0)

        for d in DIL_STEPS:
            def blk(r, i, has_prev, d=d):
                qrows = _dil_rows(r, i, d)
                qcat = _stack_heads((q_ref[qrows, :] * QK_SCALE).astype(BF16), masks)
                docat = _stack_heads(dof_ref[qrows, :].astype(BF16), masks)
                lseb, dlb = lse_ref[qrows, :], dl_ref[qrows, :]
                lse0, lse1 = lseb[:, 0:1], lseb[:, hd:hd + 1]
                dl0, dl1 = dlb[:, 0:1], dlb[:, hd:hd + 1]
                dq = None
                for krows, bias in _dil_key_tiles(r, i, d, has_prev, qrows, tri_cur, tri_prev):
                    kf = k_ref[krows, :]
                    vb = v_ref[krows, :].astype(BF16)
                    s = lax.dot_general(qcat, kf.astype(BF16), _NT, preferred_element_type=F32)
                    p0 = jnp.exp(s[:BLK] + bias - lse0)
                    p1 = jnp.exp(s[BLK:] + bias - lse1)
                    dp = lax.dot_general(docat, vb, _NT, preferred_element_type=F32)
                    ds0 = p0 * (dp[:BLK] - dl0)
                    ds1 = p1 * (dp[BLK:] - dl1)
                    pcat = jnp.concatenate([p0, p1], axis=0).astype(BF16)
                    dscat = jnp.concatenate([ds0, ds1], axis=0).astype(BF16)
                    dv_ref[krows, :] += lax.dot_general(pcat, docat, _T0, preferred_element_type=F32)
                    dk_ref[krows, :] += lax.dot_general(dscat, qcat, _T0, preferred_element_type=F32)
                    dsrow = jnp.concatenate([ds0, ds1], axis=1).astype(BF16)
                    t = jnp.dot(dsrow, _stack_heads((kf * QK_SCALE).astype(BF16), masks), preferred_element_type=F32)
                    dq = t if dq is None else dq + t
                dq_ref[qrows, :] += dq

            _dil_blocks(d, blk)

        def out_body(i, c):
            rows = pl.ds(pl.multiple_of(i * BLK, BLK), BLK)
            tab = (c_ref[rows, :], s1_ref[rows, :], s2_ref[rows, :])
            dqo_ref[rows, :] = _rope_apply(dq_ref[rows, :], *tab, transpose=True).astype(dqo_ref.dtype)
            dko_ref[rows, :] = _rope_apply(dk_ref[rows, :], *tab, transpose=True).astype(dko_ref.dtype)
            dvo_ref[rows, :] = dv_ref[rows, :].astype(dvo_ref.dtype)
            return c

        lax.fori_loop(0, NBLK, out_body, 0)

    spec = lambda off: pl.BlockSpec((None, SEQ, LANES), lambda b, j: (b, 0, off + j))
    ospec = pl.BlockSpec((None, SEQ, LANES), lambda b, j: (b, 0, j))
    tspec = pl.BlockSpec((SEQ, LANES), lambda b, j: (0, 0))
    osd = jax.ShapeDtypeStruct((nb, SEQ, DIL_W), BF16)
    return pl.pallas_call(
        body, out_shape=(osd, osd, osd), grid=(nb, ncol),
        in_specs=[spec(0), spec(ncol), spec(2 * ncol), spec(do_off), ospec, ospec, tspec, tspec, tspec],
        out_specs=(ospec, ospec, ospec),
        scratch_shapes=[pltpu.VMEM((SEQ, LANES), F32)] * 5,
        compiler_params=_cparams(dimension_semantics=("parallel", "parallel")), name=name,
    )(qkv, qkv, qkv, do, o, lse, *tabs)


FOX_GROUP = 4
assert NBLK % FOX_GROUP == 0
_FOX_COLS = tuple(c // LANES for c in (C_FQ, C_FK, C_FV))


def _fox_specs():
    cols = [pl.BlockSpec((None, SEQ, LANES), (lambda b, j, off=off: (b, 0, off + j))) for off in _FOX_COLS]
    ospec = pl.BlockSpec((None, SEQ, LANES), lambda b, j: (b, 0, j))
    crspec = pl.BlockSpec((None, None, NBLK, 8, BLK), lambda b, j: (b, j, 0, 0, 0))
    return cols, ospec, crspec


def _fox_key_rows(t, e):
    return pl.ds(pl.multiple_of((FOX_GROUP * t + e) * BLK, BLK), BLK)


def _fox_fwd(p3, crow, *, name):
    nb = p3.shape[0]
    g = FOX_GROUP

    def body(q_ref, k_ref, v_ref, cr_ref, o_ref, lse_ref):
        masks = _head_masks(2)
        tri = _tri_bias(True)

        def qk(qcat, t):
            return tuple(lax.dot_general(qcat, k_ref[_fox_key_rows(t, e), :], _NT, preferred_element_type=F32) for e in range(g))

        def consume(ss, t, state, nblk, diag):
            m0, m1, l0, l1, acc = state
            us = []
            for e in range(nblk):
                cr = cr_ref[g * t + e]
                u0 = ss[e][:BLK] - cr[0:1, :]
                u1 = ss[e][BLK:] - cr[1:2, :]
                if diag and e == nblk - 1:
                    u0, u1 = u0 + tri, u1 + tri
                us.append((u0, u1))
            x0 = functools.reduce(jnp.maximum, [u[0] for u in us])
            x1 = functools.reduce(jnp.maximum, [u[1] for u in us])
            n0 = jnp.maximum(m0, jnp.max(x0, axis=1, keepdims=True))
            n1 = jnp.maximum(m1, jnp.max(x1, axis=1, keepdims=True))
            a0, a1 = jnp.exp(m0 - n0), jnp.exp(m1 - n1)
            acc = acc * jnp.where(masks[0], a0, a1)
            t0 = t1 = None
            for e in range(nblk):
                p0, p1 = jnp.exp(us[e][0] - n0), jnp.exp(us[e][1] - n1)
                t0 = p0 if t0 is None else t0 + p0
                t1 = p1 if t1 is None else t1 + p1
                pcat = jnp.concatenate([p0, p1], axis=1)
                hi = pcat.astype(BF16)
                lo = (pcat - hi.astype(F32)).astype(BF16)
                vcat = _stack_heads(v_ref[_fox_key_rows(t, e), :], masks)
                acc = acc + jnp.dot(hi, vcat, preferred_element_type=F32) + jnp.dot(lo, vcat, preferred_element_type=F32)
            l0 = a0 * l0 + jnp.sum(t0, axis=1, keepdims=True)
            l1 = a1 * l1 + jnp.sum(t1, axis=1, keepdims=True)
            return n0, n1, l0, l1, acc

        def qblock(ng, a):
            rows = pl.ds(pl.multiple_of((g * ng + a) * BLK, BLK), BLK)
            qcat = _stack_heads(q_ref[rows, :] * QK_SCALE, masks)
            neg = jnp.full((BLK, 1), NEG_INF, F32)
            z1 = jnp.zeros((BLK, 1), F32)
            state = (neg, neg, z1, z1, jnp.zeros((BLK, LANES), F32))

            def step(t, c):
                ss, st = c
                nxt = qk(qcat, t + 1)
                return nxt, consume(ss, t, st, g, False)

            ss, state = lax.fori_loop(0, ng, step, (qk(qcat, 0), state))
            m0, m1, l0, l1, acc = consume(ss, ng, state, a + 1, True)
            o_ref[rows, :] = acc / jnp.where(masks[0], l0, l1)
            lse_ref[rows, :] = jnp.where(masks[0], m0 + jnp.log(l0), m1 + jnp.log(l1))

        def gbody(ng, c):
            for a in range(g):
                qblock(ng, a)
            return c

        lax.fori_loop(0, NBLK // g, gbody, 0)

    cols, ospec, crspec = _fox_specs()
    osd = jax.ShapeDtypeStruct((nb, SEQ, FOX_W), F32)
    return pl.pallas_call(
        body, out_shape=(osd, osd), grid=(nb, FOX_W // LANES), in_specs=cols + [crspec], out_specs=(ospec, ospec),
        compiler_params=_cparams(dimension_semantics=("parallel", "parallel")), name=name,
    )(p3, p3, p3, crow)


def _fox_bwd(p3, crow, do, o, lse, *, do_off, name):
    nb = p3.shape[0]
    g = FOX_GROUP
    hd = HEAD_DIM

    def body(q_ref, k_ref, v_ref, cr_ref, do_ref, o_ref, lse_ref, dq_ref, dko_ref, dvo_ref, dcr_ref, dk_ref, dv_ref):
        masks = _head_masks(2)
        tri = _tri_bias(True)
        dk_ref[...] = jnp.zeros_like(dk_ref)
        dv_ref[...] = jnp.zeros_like(dv_ref)
        dcr_ref[...] = jnp.zeros_like(dcr_ref)

        def products(qcat, docat, t):
            out = []
            for e in range(g):
                krows = _fox_key_rows(t, e)
                out.append(lax.dot_general(qcat, k_ref[krows, :], _NT, preferred_element_type=F32))
                out.append(lax.dot_general(docat, v_ref[krows, :], _NT, preferred_element_type=F32))
            return tuple(out)

        def consume(prod, t, ctx, dq, nblk, diag):
            qcat, docat, lse0, lse1, dl0, dl1 = ctx
            for e in range(nblk):
                jb = g * t + e
                krows = _fox_key_rows(t, e)
                s, dp = prod[2 * e], prod[2 * e + 1]
                cr = cr_ref[jb]
                u0 = s[:BLK] - cr[0:1, :]
                u1 = s[BLK:] - cr[1:2, :]
                if diag and e == nblk - 1:
                    u0, u1 = u0 + tri, u1 + tri
                p0 = jnp.exp(u0 - lse0)
                p1 = jnp.exp(u1 - lse1)
                ds0 = p0 * (dp[:BLK] - dl0)
                ds1 = p1 * (dp[BLK:] - dl1)
                dcr_ref[jb, 0:1, :] += jnp.sum(ds0, axis=0, keepdims=True)
                dcr_ref[jb, 1:2, :] += jnp.sum(ds1, axis=0, keepdims=True)
                pcat = jnp.concatenate([p0, p1], axis=0).astype(BF16)
                dscat = jnp.concatenate([ds0, ds1], axis=0).astype(BF16)
                dv_ref[krows, :] += lax.dot_general(pcat, docat, _T0, preferred_element_type=F32)
                dk_ref[krows, :] += lax.dot_general(dscat, qcat, _T0, preferred_element_type=F32)
                dsrow = jnp.concatenate([ds0, ds1], axis=1).astype(BF16)
                dq = dq + jnp.dot(dsrow, _stack_heads(k_ref[krows, :] * QK_SCALE, masks), preferred_element_type=F32)
            return dq

        def qblock(ng, a):
            rows = pl.ds(pl.multiple_of((g * ng + a) * BLK, BLK), BLK)
            qcat = _stack_heads(q_ref[rows, :] * QK_SCALE, masks)
            dob = do_ref[rows, :].astype(BF16)
            docat = _stack_heads(dob, masks)
            prod = dob.astype(F32) * o_ref[rows, :]
            z = jnp.zeros_like(prod)
            dl0 = jnp.sum(jnp.where(masks[0], prod, z), axis=1, keepdims=True)
            dl1 = jnp.sum(jnp.where(masks[1], prod, z), axis=1, keepdims=True)
            lseb = lse_ref[rows, :]
            ctx = (qcat, docat, lseb[:, 0:1], lseb[:, hd:hd + 1], dl0, dl1)

            def step(t, c):
                pr, dq = c
                nxt = products(qcat, docat, t + 1)
                return nxt, consume(pr, t, ctx, dq, g, False)

            pr, dq = lax.fori_loop(0, ng, step, (products(qcat, docat, 0), jnp.zeros((BLK, LANES), F32)))
            dq_ref[rows, :] = consume(pr, ng, ctx, dq, a + 1, True).astype(dq_ref.dtype)

        def gbody(ng, c):
            for a in range(g):
                qblock(ng, a)
            return c

        lax.fori_loop(0, NBLK // g, gbody, 0)
        dko_ref[...] = dk_ref[...].astype(dko_ref.dtype)
        dvo_ref[...] = dv_ref[...].astype(dvo_ref.dtype)

    cols, ospec, crspec = _fox_specs()
    dospec = pl.BlockSpec((None, SEQ, LANES), lambda b, j: (b, 0, do_off + j))
    osd = jax.ShapeDtypeStruct((nb, SEQ, FOX_W), BF16)
    return pl.pallas_call(
        body, out_shape=(osd, osd, osd, jax.ShapeDtypeStruct((nb, FOX_W // LANES, NBLK, 8, BLK), F32)),
        grid=(nb, FOX_W // LANES), in_specs=cols + [crspec, dospec, ospec, ospec], out_specs=(ospec, ospec, ospec, crspec),
        scratch_shapes=[pltpu.VMEM((SEQ, LANES), F32)] * 2,
        compiler_params=_cparams(dimension_semantics=("parallel", "parallel")), name=name,
    )(p3, p3, p3, crow, do, o, lse)


def _mem_cfg():
    return _AttnCfg(e=MEM_HEAD_DIM, tq=256, tk=MEM_LEN, lq=SEQ, lk=MEM_LEN, causal=False, window=None, ncol=MEM_HEADS,
                    qcol=lambda j: C_MQ // LANES + j, kcol=lambda j: j, vcol=lambda j: MEM_HEADS + j)


N_YBLK = MIX_W // LANES
_GATE_BLK = (C_FG // LANES, C_DG // LANES, C_MG // LANES)
_B1, _B2 = FOX_W // LANES, (FOX_W + DIL_W) // LANES


def _att_specs(tm):
    fspec = pl.BlockSpec((tm, LANES), lambda i, j: (i, jnp.minimum(j, _B1 - 1)))
    dspec = pl.BlockSpec((tm, LANES), lambda i, j: (i, jnp.clip(j - _B1, 0, _B2 - _B1 - 1)))
    mspec = pl.BlockSpec((tm, LANES), lambda i, j: (i, jnp.clip(j - _B2, 0, N_YBLK - _B2 - 1)))

    def gcol(j):
        return jnp.where(j < _B1, _GATE_BLK[0] + j, jnp.where(j < _B2, _GATE_BLK[1] + j - _B1, _GATE_BLK[2] + j - _B2))

    gspec = pl.BlockSpec((tm, LANES), lambda i, j: (i, gcol(j)))
    return fspec, dspec, mspec, gspec


def _pick_att(j, f_ref, d_ref, m_ref):
    return jnp.where(j < _B1, f_ref[...], jnp.where(j < _B2, d_ref[...], m_ref[...]))


def _gate_fwd(fox, dil, memo, p16, *, tm, name):
    t = fox.shape[0]

    def body(f_ref, d_ref, m_ref, g_ref, y_ref):
        j = pl.program_id(1)
        a = _pick_att(j, f_ref, d_ref, m_ref)
        gt = g_ref[...].astype(F32)
        y_ref[...] = (a * gt / (1.0 + jnp.exp(-gt))).astype(y_ref.dtype)

    return pl.pallas_call(
        body,
        out_shape=jax.ShapeDtypeStruct((t, MIX_W), BF16),
        grid=(t // tm, N_YBLK),
        in_specs=list(_att_specs(tm)),
        out_specs=pl.BlockSpec((tm, LANES), lambda i, j: (i, j)),
        compiler_params=_cparams(dimension_semantics=("parallel", "parallel")),
        name=name,
    )(fox, dil, memo, p16)


def _gate_bwd(dy, fox, dil, memo, p16, *, tm, name):
    t = fox.shape[0]

    def body(dy_ref, f_ref, d_ref, m_ref, g_ref, da_ref, dg_ref):
        j = pl.program_id(1)
        a = _pick_att(j, f_ref, d_ref, m_ref)
        gt = g_ref[...].astype(F32)
        sg = 1.0 / (1.0 + jnp.exp(-gt))
        dyv = dy_ref[...]
        da_ref[...] = (dyv * gt * sg).astype(da_ref.dtype)
        dg_ref[...] = (dyv * a * sg * (1.0 + gt * (1.0 - sg))).astype(dg_ref.dtype)

    yspec = pl.BlockSpec((tm, LANES), lambda i, j: (i, j))
    return pl.pallas_call(
        body,
        out_shape=(jax.ShapeDtypeStruct((t, MIX_W), BF16), jax.ShapeDtypeStruct((t, MIX_W), BF16)),
        grid=(t // tm, N_YBLK),
        in_specs=[yspec] + list(_att_specs(tm)),
        out_specs=(yspec, yspec),
        compiler_params=_cparams(dimension_semantics=("parallel", "parallel")),
        name=name,
    )(dy, fox, dil, memo, p16)


def _out_loss(y, wo, x, tgt, gfin, *, tm, name):
    t, d = x.shape
    n_feat = float(d)

    def body(y_ref, w_ref, x_ref, t_ref, g_ref, dx_ref, dxb_ref, st_ref):
        i = pl.program_id(0)

        @pl.when(i == 0)
        def _():
            st_ref[...] = jnp.zeros_like(st_ref)

        x2 = x_ref[...] + jnp.dot(y_ref[...], w_ref[...], preferred_element_type=F32)
        r = lax.rsqrt(jnp.mean(x2 * x2, axis=-1, keepdims=True) + RMS_EPS)
        nrm = x2 * r
        gv = g_ref[...]
        err = nrm * gv - t_ref[...]
        dout = err * (1.0 / n_feat)
        dn = dout * gv
        dx2 = r * (dn - nrm * jnp.mean(dn * nrm, axis=-1, keepdims=True))
        dx_ref[...] = dx2
        dxb_ref[...] = dx2.astype(dxb_ref.dtype)
        st_ref[0:1, :] += jnp.sum(dout * nrm, axis=0, keepdims=True)
        st_ref[1:2, :] += (0.5 / n_feat) * jnp.sum(err * err, axis=0, keepdims=True)

    row = pl.BlockSpec((tm, d), lambda i: (i, 0))
    return pl.pallas_call(
        body,
        out_shape=(jax.ShapeDtypeStruct((t, d), F32), jax.ShapeDtypeStruct((t, d), BF16), jax.ShapeDtypeStruct((8, d), F32)),
        grid=(t // tm,),
        in_specs=[pl.BlockSpec((tm, MIX_W), lambda i: (i, 0)), pl.BlockSpec((MIX_W, d), lambda i: (0, 0)), row, row,
                  pl.BlockSpec((1, d), lambda i: (0, 0))],
        out_specs=(row, row, pl.BlockSpec((8, d), lambda i: (0, 0))),
        compiler_params=_cparams(dimension_semantics=("arbitrary",)),
        name=name,
    )(y, wo, x, tgt, gfin)


def _dh_rms_bwd(dp, w, x, g, resid, *, tm, tk, name):
    t, d = x.shape
    kdim = dp.shape[1]
    nk = kdim // tk

    def body(*refs):
        if resid is not None:
            dp_ref, w_ref, x_ref, g_ref, r_ref, dx_ref, gg_ref, acc_ref = refs
        else:
            dp_ref, w_ref, x_ref, g_ref, dx_ref, gg_ref, acc_ref = refs
        i = pl.program_id(0)
        k = pl.program_id(1)

        @pl.when(jnp.logical_and(i == 0, k == 0))
        def _():
            gg_ref[...] = jnp.zeros_like(gg_ref)

        @pl.when(k == 0)
        def _():
            acc_ref[...] = jnp.zeros_like(acc_ref)

        acc_ref[...] += lax.dot_general(dp_ref[...], w_ref[...], _NT, preferred_element_type=F32)

        @pl.when(k == nk - 1)
        def _():
            dh = acc_ref[...]
            xv = x_ref[...]
            r = lax.rsqrt(jnp.mean(xv * xv, axis=-1, keepdims=True) + RMS_EPS)
            nrm = xv * r
            dn = dh * g_ref[...]
            dx = r * (dn - nrm * jnp.mean(dn * nrm, axis=-1, keepdims=True))
            if resid is not None:
                dx = dx + r_ref[...]
            dx_ref[...] = dx
            gg_ref[0:1, :] += jnp.sum(dh * nrm, axis=0, keepdims=True)

    row = pl.BlockSpec((tm, d), lambda i, k: (i, 0))
    in_specs = [pl.BlockSpec((tm, tk), lambda i, k: (i, k)), pl.BlockSpec((d, tk), lambda i, k: (0, k)), row,
                pl.BlockSpec((1, d), lambda i, k: (0, 0))]
    args = [dp, w, x, g]
    if resid is not None:
        in_specs.append(row)
        args.append(resid)
    return pl.pallas_call(
        body,
        out_shape=(jax.ShapeDtypeStruct((t, d), F32), jax.ShapeDtypeStruct((8, d), F32)),
        grid=(t // tm, nk),
        in_specs=in_specs,
        out_specs=(row, pl.BlockSpec((8, d), lambda i, k: (0, 0))),
        scratch_shapes=[pltpu.VMEM((tm, d), F32)],
        compiler_params=_cparams(dimension_semantics=("arbitrary", "arbitrary")),
        name=name,
    )(*args)


_FLOG0 = 4 * FOX_W
_W_IN_SEGMENTS = ((0, _FLOG0, 0), (_FLOG0, _FLOG0 + FOX_HEADS, PW), (_FLOG0 + FOX_HEADS, IN_W, C_DQ))
SHARD_W = IN_W // N_CHIPS


def _rearrange_w_in(shards):
    def cols(lo, hi):
        parts = []
        for k in range(N_CHIPS):
            a, b = max(lo, k * SHARD_W), min(hi, (k + 1) * SHARD_W)
            if a < b:
                parts.append(shards[k][:, a - k * SHARD_W:b - k * SHARD_W])
        return parts

    (a0, a1, _), (f0, f1, _), (b0, b1, _) = _W_IN_SEGMENTS
    pad = jnp.zeros((shards[0].shape[0], LANES - FOX_HEADS), shards[0].dtype)
    return jnp.concatenate(cols(a0, a1) + cols(b0, b1) + cols(f0, f1) + [pad], axis=1)


def _w_in_grad_slabs(g):
    slabs = []
    for k in range(N_CHIPS):
        parts = []
        for lo, hi, at in _W_IN_SEGMENTS:
            a, b = max(lo, k * SHARD_W), min(hi, (k + 1) * SHARD_W)
            if a < b:
                parts.append(g[:, at + a - lo:at + b - lo])
        slabs.append(jnp.concatenate(parts, axis=1))
    return jnp.stack(slabs, axis=0)


def _local_grads(x, mem, norm_g, w_r, b_forget, mem_norm_g, w_kv, w_o, final_norm_g, tgt):
    nb = x.shape[0]
    t = nb * SEQ
    x2d = x.reshape(t, D_MODEL)
    tgt2d = tgt.reshape(t, D_MODEL)
    tabs = _rope_tables()
    bpad = jnp.pad(b_forget.reshape(1, FOX_HEADS), ((0, 0), (0, LANES - FOX_HEADS)))

    h = _rms_fwd(x2d, norm_g.reshape(1, D_MODEL), tm=512, name="rms_x")
    p16, dqkv = _proj(h, w_r, tabs, n=PW, tm=2048, tn=256, name="proj")
    flog = _matmul(h, w_r[:, PW:], out_dtype=F32, tm=1024, tn=LANES, tk=D_MODEL, name="proj_flog")
    c12 = _flog_fwd(flog, bpad, nb=nb, ts=256, name="flog_fwd")

    crow = c12[:, :FOX_HEADS].reshape(nb, NBLK, BLK, FOX_HEADS // 2, 2).transpose(0, 3, 1, 4, 2)
    crow = jnp.pad(crow, ((0, 0), (0, 0), (0, 0), (0, 6), (0, 0)))
    p3 = p16.reshape(nb, SEQ, PW)
    fox, fox_lse = _fox_fwd(p3, crow, name="fox_fwd")

    dqkv3 = dqkv.reshape(nb, SEQ, 3 * DIL_W)
    dil, dil_lse = _dil_fwd(dqkv3, name="dil_fwd")

    mh = _rms_fwd(mem.reshape(nb * MEM_LEN, D_MODEL), mem_norm_g.reshape(1, D_MODEL), tm=nb * MEM_LEN, name="rms_mem")
    mkv = _matmul(mh, w_kv, out_dtype=BF16, tm=nb * MEM_LEN, tn=512, tk=D_MODEL, name="mem_kv")
    mkv3 = mkv.reshape(nb, MEM_LEN, 2 * MEM_W)
    mcfg = _mem_cfg()
    memo, mem_lse = _attn_fwd(mcfg, p3, mkv3, mkv3, out_cols=MEM_W, name="mem_fwd")

    fox2, dil2, memo2 = fox.reshape(t, FOX_W), dil.reshape(t, DIL_W), memo.reshape(t, MEM_W)
    y = _gate_fwd(fox2, dil2, memo2, p16, tm=1024, name="gate_fwd")
    dx2, dx2b, st = _out_loss(y, w_o, x2d, tgt2d, final_norm_g.reshape(1, D_MODEL), tm=512, name="out_loss")

    g_wo = _matmul(y, dx2b, mode="tn", out_dtype=F32, tm=1024, tn=512, tk=1024, name="grad_w_out")
    dy = _matmul(dx2b, w_o, mode="nt", out_dtype=F32, tm=1024, tn=512, tk=D_MODEL, name="d_y")
    datt, dgate = _gate_bwd(dy, fox2, dil2, memo2, p16, tm=1024, name="gate_bwd")
    datt3 = datt.reshape(nb, SEQ, MIX_W)

    dfq, dfk, dfv, dcr = _fox_bwd(p3, crow, datt3, fox, fox_lse, do_off=0, name="fox_bwd")
    dcol = -dcr[:, :, :, :2, :].transpose(0, 2, 4, 1, 3).reshape(t, FOX_HEADS)
    dcol = jnp.pad(dcol, ((0, 0), (0, LANES - FOX_HEADS)))
    dflog, gb = _flog_bwd(dcol, flog, bpad, nb=nb, ts=256, name="flog_bwd")

    ddq, ddk, ddv = _dil_bwd(dqkv3, datt3, dil, dil_lse, tabs, do_off=_B1, name="dil_bwd")

    dmq, dmk, dmv = _attn_bwd(mcfg, p3, mkv3, mkv3, datt3, memo, mem_lse, out_cols=MEM_W, kv_cols=MEM_W, do_off=_B2,
                              name="mem_bwd")
    dmkv = jnp.concatenate([dmk, dmv], axis=-1).reshape(nb * MEM_LEN, 2 * MEM_W).astype(BF16)
    g_wkv = _matmul(mh, dmkv, mode="tn", out_dtype=F32, tm=512, tn=512, tk=nb * MEM_LEN, name="grad_w_kv")
    _, gmn = _dh_rms_bwd(dmkv, w_kv, mem.reshape(nb * MEM_LEN, D_MODEL), mem_norm_g.reshape(1, D_MODEL), None,
                         tm=nb * MEM_LEN, tk=2 * MEM_W, name="mem_rms_bwd")

    flat = lambda a: a.reshape(t, -1)
    dp = jnp.concatenate([flat(dfq), flat(dfk), flat(dfv), dgate[:, :FOX_W], flat(ddq), flat(ddk), flat(ddv),
                          dgate[:, FOX_W:FOX_W + DIL_W], flat(dmq).astype(BF16), dgate[:, FOX_W + DIL_W:], dflog], axis=1)
    g_wr = _matmul(h, dp, mode="tn", out_dtype=F32, tm=512, tn=PWF // 3, tk=1024, name="grad_w_in")
    gx, gng = _dh_rms_bwd(dp, w_r, x2d, norm_g.reshape(1, D_MODEL), dx2, tm=512, tk=PWF // 3, name="in_rms_bwd")

    gb_row = jnp.pad(gb[0:1, :], ((0, 0), (0, D_MODEL - LANES)))
    small = jnp.concatenate([gng[0:1], gmn[0:1], st[0:1], gb_row, st[1:2], jnp.zeros((3, D_MODEL), F32)], axis=0)
    return gx.reshape(nb, SEQ, D_MODEL), g_wr, g_wkv, g_wo, small


MESH = pl.DeviceIdType.MESH
ANY = pl.BlockSpec(memory_space=pl.ANY)


def _place():
    x, y, c = lax.axis_index("x"), lax.axis_index("y"), lax.axis_index("c")
    other_chips = [(1 - x, y), (x, 1 - y), (1 - x, 1 - y)]
    return x, y, c, other_chips


def _gather_weights(shards):
    n = len(shards)

    def body(*refs):
        in_refs, out_refs = refs[:n], refs[n:2 * n]
        send_sems, recv_sems = refs[2 * n:]
        x, y, c, chips = _place()
        me_chip = 2 * x + y
        sibling = (x, y, 1 - c)

        def half(ref, pc, rows):
            return ref.at[pl.ds(pc * (rows // 2), rows // 2), :]

        def rcopy(k, src, dst, to):
            return pltpu.make_async_remote_copy(src_ref=src, dst_ref=dst, send_sem=send_sems.at[k], recv_sem=recv_sems.at[k],
                                                device_id=to, device_id_type=MESH)

        sends = []
        for t in range(n):
            rows = shards[t].shape[0]
            for j, chip in enumerate(chips):
                cp = rcopy(6 * t + j, half(in_refs[t], c, rows), half(out_refs[t].at[me_chip], c, rows), (*chip, c))
                cp.start()
                sends.append(cp)
        for t in range(n):
            rows = shards[t].shape[0]
            for j, chip in enumerate(chips):
                slot = out_refs[t].at[2 * chip[0] + chip[1]]
                rcopy(6 * t + j, half(slot, c, rows), half(slot, c, rows), sibling).wait_recv()
                fw = rcopy(6 * t + 3 + j, half(slot, c, rows), half(slot, c, rows), sibling)
                fw.start()
                sends.append(fw)
        for t in range(n):
            rows = shards[t].shape[0]
            for j, chip in enumerate(chips):
                slot = out_refs[t].at[2 * chip[0] + chip[1]]
                rcopy(6 * t + 3 + j, half(slot, 1 - c, rows), half(slot, 1 - c, rows), sibling).wait_recv()
        for cp in sends:
            cp.wait_send()

    return pl.pallas_call(
        body,
        out_shape=tuple(jax.ShapeDtypeStruct((N_CHIPS,) + s.shape, s.dtype) for s in shards),
        in_specs=[ANY] * n,
        out_specs=tuple([ANY] * n),
        scratch_shapes=[pltpu.SemaphoreType.DMA((6 * n,)), pltpu.SemaphoreType.DMA((6 * n,))],
        name="gather_weights",
    )(*shards)


def _pair_exchange(gs):
    n = len(gs)

    def body(*refs):
        g_refs, r_refs = refs[:n], refs[n:2 * n]
        send_sems, recv_sems = refs[2 * n:]
        x, y, c, _ = _place()
        cps = []
        for t in range(n):
            hr = gs[t].shape[1] // 2
            cp = pltpu.make_async_remote_copy(src_ref=g_refs[t].at[:, pl.ds((1 - c) * hr, hr), :], dst_ref=r_refs[t],
                                              send_sem=send_sems.at[t], recv_sem=recv_sems.at[t],
                                              device_id=(x, y, 1 - c), device_id_type=MESH)
            cp.start()
            cps.append(cp)
        for cp in cps:
            cp.wait()

    return pl.pallas_call(
        body,
        out_shape=tuple(jax.ShapeDtypeStruct((N_CHIPS, g.shape[1] // 2, g.shape[2]), g.dtype) for g in gs),
        in_specs=[ANY] * n,
        out_specs=tuple([ANY] * n),
        scratch_shapes=[pltpu.SemaphoreType.DMA((n,)), pltpu.SemaphoreType.DMA((n,))],
        name="pair_exchange",
    )(*gs)


def _chip_exchange(ps):
    n = len(ps)

    def body(*refs):
        p_refs, o_refs = refs[:n], refs[n:2 * n]
        send_sems, recv_sems = refs[2 * n:]
        x, y, c, chips = _place()
        me_chip = 2 * x + y
        cps = []
        for t in range(n):
            for j, chip in enumerate(chips):
                cp = pltpu.make_async_remote_copy(src_ref=p_refs[t].at[2 * chip[0] + chip[1]], dst_ref=o_refs[t].at[me_chip],
                                                  send_sem=send_sems.at[3 * t + j], recv_sem=recv_sems.at[3 * t + j],
                                                  device_id=(*chip, c), device_id_type=MESH)
                cp.start()
                cps.append(cp)
        for cp in cps:
            cp.wait()

    return pl.pallas_call(
        body,
        out_shape=tuple(jax.ShapeDtypeStruct(p.shape, p.dtype) for p in ps),
        in_specs=[ANY] * n,
        out_specs=tuple([ANY] * n),
        scratch_shapes=[pltpu.SemaphoreType.DMA((3 * n,)), pltpu.SemaphoreType.DMA((3 * n,))],
        name="chip_exchange",
    )(*ps)


def _pair_swap(rs):
    n = len(rs)

    def body(*refs):
        r_refs, o_refs = refs[:n], refs[n:2 * n]
        send_sems, recv_sems = refs[2 * n:]
        x, y, c, _ = _place()
        cps = []
        for t in range(n):
            cp = pltpu.make_async_remote_copy(src_ref=r_refs[t], dst_ref=o_refs[t], send_sem=send_sems.at[t],
                                              recv_sem=recv_sems.at[t], device_id=(x, y, 1 - c), device_id_type=MESH)
            cp.start()
            cps.append(cp)
        for cp in cps:
            cp.wait()

    return pl.pallas_call(
        body,
        out_shape=tuple(jax.ShapeDtypeStruct(r.shape, r.dtype) for r in rs),
        in_specs=[ANY] * n,
        out_specs=tuple([ANY] * n),
        scratch_shapes=[pltpu.SemaphoreType.DMA((n,)), pltpu.SemaphoreType.DMA((n,))],
        name="pair_swap",
    )(*rs)


N_DEV = 8
LOSS_ROW = 4


def _small_allreduce(small):
    def body(s_ref, o_ref, all_ref, send_sems, recv_sems):
        x, y, c, _ = _place()
        me = 4 * x + 2 * y + c
        all_ref[me] = s_ref[...]
        cps = []
        for k in range(1, N_DEV):
            peer = tuple(1 - p if (k >> s) & 1 else p for p, s in ((x, 2), (y, 1), (c, 0)))
            cp = pltpu.make_async_remote_copy(src_ref=s_ref, dst_ref=all_ref.at[me], send_sem=send_sems.at[k - 1],
                                              recv_sem=recv_sems.at[k - 1], device_id=peer, device_id_type=MESH)
            cp.start()
            cps.append(cp)
        for cp in cps:
            cp.wait()
        tot = all_ref[0]
        for d in range(1, N_DEV):
            tot = tot + all_ref[d]
        o_ref[...] = tot
        o_ref[LOSS_ROW:LOSS_ROW + 1, :] = jnp.broadcast_to(jnp.sum(tot[LOSS_ROW:LOSS_ROW + 1, :], axis=1, keepdims=True),
                                                          (1, tot.shape[1]))

    vm = pl.BlockSpec(memory_space=pltpu.VMEM)
    return pl.pallas_call(
        body,
        out_shape=jax.ShapeDtypeStruct(small.shape, small.dtype),
        in_specs=[vm],
        out_specs=vm,
        scratch_shapes=[pltpu.VMEM((N_DEV,) + small.shape, small.dtype), pltpu.SemaphoreType.DMA((N_DEV - 1,)),
                        pltpu.SemaphoreType.DMA((N_DEV - 1,))],
        name="small_allreduce",
    )(small)


def _sum_pair(g, recv, cidx, *, tr, name):
    _, hr, cols = recv.shape
    nr = hr // tr

    def body(c_ref, g_ref, r_ref, o_ref):
        o_ref[...] = (g_ref[...] + r_ref[...]).astype(o_ref.dtype)

    grid_spec = pltpu.PrefetchScalarGridSpec(
        num_scalar_prefetch=1,
        grid=(N_CHIPS, nr),
        in_specs=[pl.BlockSpec((None, tr, cols), lambda k, i, c_ref: (k, c_ref[0] * nr + i, 0)),
                  pl.BlockSpec((None, tr, cols), lambda k, i, c_ref: (k, i, 0))],
        out_specs=pl.BlockSpec((None, tr, cols), lambda k, i, c_ref: (k, i, 0)),
    )
    return pl.pallas_call(body, out_shape=jax.ShapeDtypeStruct(recv.shape, BF16), grid_spec=grid_spec,
                          compiler_params=_cparams(), name=name)(cidx, g, recv)


def _sum_chips(p, *, tr, name):
    _, rows, cols = p.shape

    def body(p_ref, o_ref):
        tot = p_ref[0].astype(F32)
        for k in range(1, N_CHIPS):
            tot = tot + p_ref[k].astype(F32)
        o_ref[...] = tot

    return pl.pallas_call(
        body,
        out_shape=jax.ShapeDtypeStruct((rows, cols), F32),
        grid=(rows // tr,),
        in_specs=[pl.BlockSpec((N_CHIPS, tr, cols), lambda i: (0, i, 0))],
        out_specs=pl.BlockSpec((tr, cols), lambda i: (i, 0)),
        compiler_params=_cparams(),
        name=name,
    )(p)


def _adamw(w, g, m, v, *, tr, name):
    rows, cols = w.shape
    bc1 = 1.0 / (1.0 - ADAM_B1 ** ADAM_STEP)
    bc2 = 1.0 / (1.0 - ADAM_B2 ** ADAM_STEP)

    def body(w_ref, g_ref, m_ref, v_ref, d_ref, nm_ref, nv_ref):
        gv = g_ref[...]
        nm = ADAM_B1 * m_ref[...] + (1.0 - ADAM_B1) * gv
        nv = ADAM_B2 * v_ref[...] + (1.0 - ADAM_B2) * (gv * gv)
        d_ref[...] = -ADAM_LR * ((nm * bc1) / (jnp.sqrt(nv * bc2) + ADAM_EPS) + ADAM_WD * w_ref[...])
        nm_ref[...] = nm
        nv_ref[...] = nv

    spec = pl.BlockSpec((tr, cols), lambda i: (i, 0))
    sd = jax.ShapeDtypeStruct((rows, cols), F32)
    return pl.pallas_call(body, out_shape=(sd, sd, sd), grid=(rows // tr,), in_specs=[spec] * 4, out_specs=(spec,) * 3,
                          compiler_params=_cparams(), name=name)(w, g, m, v)


def _pack_small(norm, mem_norm, final_norm, b_forget):
    rows = [norm.reshape(1, D_MODEL), mem_norm.reshape(1, D_MODEL), final_norm.reshape(1, D_MODEL),
            jnp.pad(b_forget.reshape(1, FOX_HEADS), ((0, 0), (0, D_MODEL - FOX_HEADS))), jnp.zeros((4, D_MODEL), F32)]
    return jnp.concatenate(rows, axis=0)


def _unpack_small(a):
    return a[0:1], a[3:4, :FOX_HEADS], a[1:2], a[2]


def kernel(x, mem, norm_g, w_in, b_forget, mem_norm_g, w_mem_kv, w_out, final_norm_g, loss_target, m_norm_g, m_w_in, m_b_forget, m_mem_norm_g, m_w_mem_kv, m_w_out, m_final_norm_g, v_norm_g, v_w_in, v_b_forget, v_mem_norm_g, v_w_mem_kv, v_w_out, v_final_norm_g):
    core = lax.axis_index("c").astype(jnp.int32)
    me_chip = (2 * lax.axis_index("x") + lax.axis_index("y")).astype(jnp.int32)
    cidx = core.reshape(1)

    def own_slot(arr, own):
        return lax.dynamic_update_slice(arr, own[None].astype(arr.dtype), (me_chip,) + (0,) * own.ndim)

    mine = [w_in[0].astype(BF16), w_mem_kv[0].astype(BF16), w_out[0].astype(BF16)]
    g_in, g_kv, g_out = (own_slot(g, s) for g, s in zip(_gather_weights(mine), mine))
    w_r = _rearrange_w_in([g_in[k] for k in range(N_CHIPS)])
    w_kv = g_kv.reshape(D_MODEL, 2 * MEM_W)
    w_o = g_out.reshape(MIX_W, D_MODEL)

    gx, g_wr, g_wkv, g_wo, small = _local_grads(x, mem, norm_g, w_r, b_forget, mem_norm_g, w_kv, w_o, final_norm_g, loss_target)

    slabs = [_w_in_grad_slabs(g_wr),
             g_wkv.reshape(N_CHIPS, D_MODEL // N_CHIPS, 2 * MEM_W),
             g_wo.reshape(N_CHIPS, MIX_W // N_CHIPS, D_MODEL)]
    trs = (128, 128, 256)
    names = ("w_in", "w_mem_kv", "w_out")
    recv = _pair_exchange(slabs)
    pair = [_sum_pair(g, r, cidx, tr=tr, name=f"sum_pair_{nm}") for g, r, tr, nm in zip(slabs, recv, trs, names)]
    got = [lax.dynamic_update_slice(g, lax.dynamic_slice(p, (me_chip, 0, 0), (1,) + p.shape[1:]), (me_chip, 0, 0))
           for g, p in zip(_chip_exchange(pair), pair)]
    red = [_sum_chips(p, tr=tr, name=f"sum_chips_{nm}") for p, tr, nm in zip(got, trs, names)]
    sib = _pair_swap(red)
    grads = [jnp.where(core == 0, jnp.concatenate([r, s], axis=0), jnp.concatenate([s, r], axis=0)) for r, s in zip(red, sib)]

    outs = {}
    for nm, g, w, m, v, tr in zip(names, grads, (w_in, w_mem_kv, w_out), (m_w_in, m_w_mem_kv, m_w_out),
                                  (v_w_in, v_w_mem_kv, v_w_out), trs):
        d, nmo, nvo = _adamw(w[0], g, m[0], v[0], tr=tr, name=f"adamw_{nm}")
        outs[nm] = tuple(a[None] for a in (g, d, nmo, nvo))

    gsum = _small_allreduce(small)
    sd, sm, sv = _adamw(_pack_small(norm_g, mem_norm_g, final_norm_g, b_forget), gsum,
                        _pack_small(m_norm_g, m_mem_norm_g, m_final_norm_g, m_b_forget),
                        _pack_small(v_norm_g, v_mem_norm_g, v_final_norm_g, v_b_forget), tr=8, name="adamw_small")
    loss = gsum[LOSS_ROW, 0]

    def group(i, small_arr):
        ng, bf, mg, fg = _unpack_small(small_arr)
        return (ng, outs["w_in"][i], bf, mg, outs["w_mem_kv"][i], outs["w_out"][i], fg)

    return (loss, gx, *group(0, gsum), *group(1, sd), *group(2, sm), *group(3, sv))
```

```python
import functools
import math

import jax
import jax.numpy as jnp
from jax import lax
from jax.experimental import pallas as pl
from jax.experimental.pallas import tpu as pltpu

F32 = jnp.float32
BF16 = jnp.bfloat16

D_MODEL = 1024
SEQ = 2048
HEAD_DIM = 64
FOX_HEADS = 12
DIL_HEADS = 12
MEM_HEADS = 4
MEM_HEAD_DIM = 128
MEM_LEN = 256
FOX_W = FOX_HEADS * HEAD_DIM
DIL_W = DIL_HEADS * HEAD_DIM
MEM_W = MEM_HEADS * MEM_HEAD_DIM
MIX_W = FOX_W + DIL_W + MEM_W
DILATIONS = ((128, 1), (512, 4), (2048, 16))
ROPE_THETA = 500000.0
ROPE_DIM = HEAD_DIM // 4
RMS_EPS = 1e-6
NEG_INF = -1e30
IN_SIZES = [FOX_W] * 4 + [FOX_HEADS] + [DIL_W] * 4 + [MEM_W] * 2
IN_W = sum(IN_SIZES)

ADAM_LR = 0.001
ADAM_B1 = 0.9
ADAM_B2 = 0.999
ADAM_EPS = 1e-08
ADAM_WD = 0.01
ADAM_STEP = 10

LANES = 128
N_CHIPS = 4
PW = 7168
PWF = PW + LANES
C_FQ, C_FK, C_FV, C_FG = 0, 768, 1536, 2304
C_DQ, C_DK, C_DV, C_DG = 3072, 3840, 4608, 5376
C_MQ, C_MG = 6144, 6656
VMEM_LIMIT = 48 * 1024 * 1024


def _cparams(**kw):
    return pltpu.CompilerParams(vmem_limit_bytes=VMEM_LIMIT, **kw)


def _matmul(a, b, *, out_dtype, tm, tn, tk, name, mode="nn"):
    if mode == "tn":
        (kdim, m), n = a.shape, b.shape[1]
        a_spec = pl.BlockSpec((tk, tm), lambda i, j, k: (k, i))
        b_spec = pl.BlockSpec((tk, tn), lambda i, j, k: (k, j))
        dims = _T0
    elif mode == "nt":
        (m, kdim), n = a.shape, b.shape[0]
        a_spec = pl.BlockSpec((tm, tk), lambda i, j, k: (i, k))
        b_spec = pl.BlockSpec((tn, tk), lambda i, j, k: (j, k))
        dims = _NT
    else:
        (m, kdim), n = a.shape, b.shape[1]
        a_spec = pl.BlockSpec((tm, tk), lambda i, j, k: (i, k))
        b_spec = pl.BlockSpec((tk, tn), lambda i, j, k: (k, j))
        dims = (((1,), (0,)), ((), ()))
    nk = kdim // tk
    assert m % tm == 0 and n % tn == 0 and kdim % tk == 0

    def body(a_ref, b_ref, o_ref, acc_ref):
        k = pl.program_id(2)

        @pl.when(k == 0)
        def _():
            acc_ref[...] = jnp.zeros_like(acc_ref)

        acc_ref[...] += lax.dot_general(a_ref[...], b_ref[...], dims, preferred_element_type=F32)

        @pl.when(k == nk - 1)
        def _():
            o_ref[...] = acc_ref[...].astype(o_ref.dtype)

    return pl.pallas_call(
        body,
        out_shape=jax.ShapeDtypeStruct((m, n), out_dtype),
        grid=(m // tm, n // tn, nk),
        in_specs=[a_spec, b_spec],
        out_specs=pl.BlockSpec((tm, tn), lambda i, j, k: (i, j)),
        scratch_shapes=[pltpu.VMEM((tm, tn), F32)],
        compiler_params=_cparams(dimension_semantics=("parallel", "parallel", "arbitrary")),
        name=name,
    )(a, b)


def _rms_fwd(x, g, *, tm, name):
    t, d = x.shape

    def body(x_ref, g_ref, h_ref):
        xv = x_ref[...]
        r = lax.rsqrt(jnp.mean(xv * xv, axis=-1, keepdims=True) + RMS_EPS)
        h_ref[...] = (xv * r * g_ref[...]).astype(h_ref.dtype)

    return pl.pallas_call(
        body,
        out_shape=jax.ShapeDtypeStruct((t, d), BF16),
        grid=(t // tm,),
        in_specs=[pl.BlockSpec((tm, d), lambda i: (i, 0)), pl.BlockSpec((1, d), lambda i: (0, 0))],
        out_specs=pl.BlockSpec((tm, d), lambda i: (i, 0)),
        compiler_params=_cparams(),
        name=name,
    )(x, g)


def _rope_tables():
    half = ROPE_DIM // 2
    pos = jnp.arange(SEQ, dtype=F32)
    inv_freq = 1.0 / (ROPE_THETA ** (jnp.arange(0, ROPE_DIM, 2, dtype=F32) / ROPE_DIM))
    ang = pos[:, None] * inv_freq[None, :]
    cos, sin = jnp.cos(ang), jnp.sin(ang)
    one = jnp.ones((SEQ, HEAD_DIM - ROPE_DIM), F32)
    zero = jnp.zeros((SEQ, HEAD_DIM - ROPE_DIM), F32)
    zh = jnp.zeros((SEQ, half), F32)
    c = jnp.concatenate([cos, cos, one], axis=1)
    s1 = jnp.concatenate([zh, sin, zero], axis=1)
    s2 = jnp.concatenate([-sin, zh, zero], axis=1)
    rep = LANES // HEAD_DIM
    return jnp.tile(c, (1, rep)), jnp.tile(s1, (1, rep)), jnp.tile(s2, (1, rep))


def _rope_apply(t, c, s1, s2, transpose=False):
    n = t.shape[-1]
    rep = n // LANES
    c, s1, s2 = (jnp.tile(u, (1, rep)) for u in (c, s1, s2))
    half = ROPE_DIM // 2
    if not transpose:
        return t * c + pltpu.roll(t, half, 1) * s1 + pltpu.roll(t, n - half, 1) * s2
    return t * c + pltpu.roll(t * s1, n - half, 1) + pltpu.roll(t * s2, half, 1)


def _proj(h, w, tabs, *, n, tm, tn, name):
    t, d = h.shape
    assert C_DQ % tn == 0 and (C_DV - C_DQ) % tn == 0 and (C_DG - C_DQ) % tn == 0
    rope_lo, rope_hi, dil_hi = C_DQ // tn, C_DV // tn, C_DG // tn
    s_blocks = SEQ // tm

    def body(h_ref, w_ref, c_ref, s1_ref, s2_ref, o_ref, f_ref):
        j = pl.program_id(1)
        acc = jnp.dot(h_ref[...], w_ref[...], preferred_element_type=F32)
        is_rope = jnp.logical_and(j >= rope_lo, j < rope_hi)

        @pl.when(is_rope)
        def _():
            r = _rope_apply(acc, c_ref[...], s1_ref[...], s2_ref[...])
            o_ref[...] = r.astype(o_ref.dtype)
            f_ref[...] = r

        @pl.when(jnp.logical_not(is_rope))
        def _():
            o_ref[...] = acc.astype(o_ref.dtype)

        @pl.when(jnp.logical_and(j >= rope_hi, j < dil_hi))
        def _():
            f_ref[...] = acc

    tab_spec = pl.BlockSpec((tm, LANES), lambda i, j: (i % s_blocks, 0))
    f_spec = pl.BlockSpec((tm, tn), lambda i, j: (i, jnp.clip(j - rope_lo, 0, dil_hi - rope_lo - 1)))
    return pl.pallas_call(
        body,
        out_shape=(jax.ShapeDtypeStruct((t, n), BF16), jax.ShapeDtypeStruct((t, 3 * DIL_W), F32)),
        grid=(t // tm, n // tn),
        in_specs=[pl.BlockSpec((tm, d), lambda i, j: (i, 0)), pl.BlockSpec((d, tn), lambda i, j: (0, j)),
                  tab_spec, tab_spec, tab_spec],
        out_specs=(pl.BlockSpec((tm, tn), lambda i, j: (i, j)), f_spec),
        compiler_params=_cparams(dimension_semantics=("parallel", "arbitrary")),
        name=name,
    )(h, w, *tabs)


def _split3(x):
    hi = x.astype(BF16)
    r1 = x - hi.astype(F32)
    mid = r1.astype(BF16)
    lo = (r1 - mid.astype(F32)).astype(BF16)
    return hi, mid, lo


def _dot3(sel, x, sel_is_lhs):
    out = None
    for piece in _split3(x):
        t = jnp.dot(sel, piece, preferred_element_type=F32) if sel_is_lhs else jnp.dot(piece, sel, preferred_element_type=F32)
        out = t if out is None else out + t
    return out


def _flog_fwd(flog, bpad, *, nb, ts, name):
    ns = SEQ // ts

    def body(f_ref, b_ref, c_ref, carry_ref):
        s = pl.program_id(1)

        @pl.when(s == 0)
        def _():
            carry_ref[...] = jnp.zeros_like(carry_ref)

        z = f_ref[...] + b_ref[...]
        logf = jnp.minimum(z, 0.0) - jnp.log(1.0 + jnp.exp(-jnp.abs(z)))
        r = lax.broadcasted_iota(jnp.int32, (ts, ts), 0)
        c = lax.broadcasted_iota(jnp.int32, (ts, ts), 1)
        tri = jnp.where(r >= c, 1.0, 0.0).astype(BF16)
        cs = _dot3(tri, logf, True) + carry_ref[0:1, :]
        carry_ref[...] = jnp.broadcast_to(cs[ts - 1:ts, :], carry_ref.shape)
        c_ref[...] = cs

    return pl.pallas_call(
        body,
        out_shape=jax.ShapeDtypeStruct((nb * SEQ, LANES), F32),
        grid=(nb, ns),
        in_specs=[pl.BlockSpec((ts, LANES), lambda b, s: (b * ns + s, 0)), pl.BlockSpec((1, LANES), lambda b, s: (0, 0))],
        out_specs=pl.BlockSpec((ts, LANES), lambda b, s: (b * ns + s, 0)),
        scratch_shapes=[pltpu.VMEM((8, LANES), F32)],
        compiler_params=_cparams(dimension_semantics=("parallel", "arbitrary")),
        name=name,
    )(flog, bpad)


def _flog_bwd(dcol, flog, bpad, *, nb, ts, name):
    ns = SEQ // ts

    def body(d_ref, f_ref, b_ref, o_ref, gb_ref, carry_ref):
        bi = pl.program_id(0)
        s = pl.program_id(1)

        @pl.when(s == 0)
        def _():
            carry_ref[...] = jnp.zeros_like(carry_ref)

        @pl.when(jnp.logical_and(bi == 0, s == 0))
        def _():
            gb_ref[...] = jnp.zeros_like(gb_ref)

        r = lax.broadcasted_iota(jnp.int32, (ts, ts), 0)
        c = lax.broadcasted_iota(jnp.int32, (ts, ts), 1)
        tri = jnp.where(r <= c, 1.0, 0.0).astype(BF16)
        rc = _dot3(tri, d_ref[...], True) + carry_ref[0:1, :]
        carry_ref[...] = jnp.broadcast_to(rc[0:1, :], carry_ref.shape)
        z = f_ref[...] + b_ref[...]
        dz = rc / (1.0 + jnp.exp(z))
        o_ref[...] = dz.astype(o_ref.dtype)
        gb_ref[...] += jnp.broadcast_to(jnp.sum(dz, axis=0, keepdims=True), gb_ref.shape)

    rev = lambda b, s: (b * ns + (ns - 1 - s), 0)
    return pl.pallas_call(
        body,
        out_shape=(jax.ShapeDtypeStruct((nb * SEQ, LANES), BF16), jax.ShapeDtypeStruct((8, LANES), F32)),
        grid=(nb, ns),
        in_specs=[pl.BlockSpec((ts, LANES), rev), pl.BlockSpec((ts, LANES), rev), pl.BlockSpec((1, LANES), lambda b, s: (0, 0))],
        out_specs=(pl.BlockSpec((ts, LANES), rev), pl.BlockSpec((8, LANES), lambda b, s: (0, 0))),
        scratch_shapes=[pltpu.VMEM((8, LANES), F32)],
        compiler_params=_cparams(dimension_semantics=("arbitrary", "arbitrary")),
        name=name,
    )(dcol, flog, bpad)


class _AttnCfg:
    def __init__(self, *, e, tq, tk, lq, lk, causal, window, ncol, qcol, kcol, vcol, split_p=False):
        self.e, self.tq, self.tk, self.lq, self.lk = e, tq, tk, lq, lk
        self.split_p = split_p
        self.causal, self.window = causal, window
        self.ncol, self.qcol, self.kcol, self.vcol = ncol, qcol, kcol, vcol
        self.nh = LANES // e
        self.scale = 1.0 / math.sqrt(e)
        self.nq, self.nk = lq // tq, lk // tk

    def k_range(self, i):
        if not self.causal:
            return 0, self.nk
        hi = ((i + 1) * self.tq - 1) // self.tk + 1
        if self.window is None:
            return 0, hi
        return jnp.maximum((i * self.tq - self.window) // self.tk, 0), hi


def _head_masks(nh):
    lane = lax.broadcasted_iota(jnp.int32, (1, LANES), 1)
    return [None] if nh == 1 else [lane < HEAD_DIM, lane >= HEAD_DIM]


def _sel(mask, a, b):
    return a if mask is None else jnp.where(mask, a, b)


def _scores(cfg, qh, kb, q0, k0, dlt0, bias):
    s = lax.dot_general(qh, kb, (((1,), (1,)), ((), ())), preferred_element_type=F32) * cfg.scale
    if bias is not None:
        s = s + bias
    if cfg.causal:
        d = dlt0 + (q0 - k0)
        if cfg.window is None:
            ok = d >= 0
        else:
            ok = d.astype(jnp.uint32) <= jnp.uint32(cfg.window)
        s = jnp.where(ok, s, NEG_INF)
    return s


def _attn_fwd(cfg, q, k, v, *, out_cols, bias=None, state=None, finalize=True, name):
    g = q.shape[0]
    tq, tk, e, nh = cfg.tq, cfg.tk, cfg.e, cfg.nh

    def body(*refs):
        refs = list(refs)
        q_ref, k_ref, v_ref = refs[:3]
        del refs[:3]
        if bias is not None:
            cb_ref, cr_ref = refs[:2]
            del refs[:2]
        if state is not None:
            ai_ref, mi_ref, li_ref = refs[:3]
            del refs[:3]
        out_refs = refs
        masks = _head_masks(nh)
        dlt0 = lax.broadcasted_iota(jnp.int32, (tq, tk), 0) - lax.broadcasted_iota(jnp.int32, (tq, tk), 1)

        def qbody(i, carry):
            q0 = pl.multiple_of(i * tq, tq)
            rows = pl.ds(q0, tq)
            qb = q_ref[rows, :]
            lo, hi = cfg.k_range(i)
            res = []
            for h in range(nh):
                qh = _sel(masks[h], qb, jnp.zeros_like(qb))
                if state is not None:
                    m0 = mi_ref[rows, h * e:h * e + 1]
                    l0 = li_ref[rows, h * e:h * e + 1]
                    a0 = ai_ref[rows, :]
                else:
                    m0 = jnp.full((tq, 1), NEG_INF, F32)
                    l0 = jnp.zeros((tq, 1), F32)
                    a0 = jnp.zeros((tq, LANES), F32)
                cq = cb_ref[rows, h * e:h * e + 1] if bias is not None else None

                def kbody(jk, c, qh=qh, cq=cq, h=h):
                    m, l, a = c
                    k0 = pl.multiple_of(jk * tk, tk)
                    kb = k_ref[pl.ds(k0, tk), :]
                    vb = v_ref[pl.ds(k0, tk), :]
                    b = (cq - cr_ref[jk, h:h + 1, :]) if bias is not None else None
                    s = _scores(cfg, qh, kb, q0, k0, dlt0, b)
                    m_new = jnp.maximum(m, jnp.max(s, axis=1, keepdims=True))
                    alpha = jnp.exp(m - m_new)
                    p = jnp.exp(s - m_new)
                    l = alpha * l + jnp.sum(p, axis=1, keepdims=True)
                    pb = p.astype(BF16)
                    pv = jnp.dot(pb, vb, preferred_element_type=F32)
                    if cfg.split_p:
                        pv = pv + jnp.dot((p - pb.astype(F32)).astype(BF16), vb, preferred_element_type=F32)
                    a = alpha * a + pv
                    return m_new, l, a

                res.append(lax.fori_loop(lo, hi, kbody, (m0, l0, a0)))
            if nh == 1:
                m, l, a = res[0]
                m, l = jnp.broadcast_to(m, (tq, LANES)), jnp.broadcast_to(l, (tq, LANES))
            else:
                m = jnp.where(masks[0], res[0][0], res[1][0])
                l = jnp.where(masks[0], res[0][1], res[1][1])
                a = jnp.where(masks[0], res[0][2], res[1][2])
            if finalize:
                out_refs[0][rows, :] = a / l
                out_refs[1][rows, :] = m + jnp.log(l)
            else:
                out_refs[0][rows, :] = a
                out_refs[1][rows, :] = m
                out_refs[2][rows, :] = l
            return carry

        lax.fori_loop(0, cfg.nq, qbody, 0)

    qspec = pl.BlockSpec((None, cfg.lq, LANES), lambda b, j: (b, 0, cfg.qcol(j)))
    kspec = pl.BlockSpec((None, cfg.lk, LANES), lambda b, j: (b, 0, cfg.kcol(j)))
    vspec = pl.BlockSpec((None, cfg.lk, LANES), lambda b, j: (b, 0, cfg.vcol(j)))
    ospec = pl.BlockSpec((None, cfg.lq, LANES), lambda b, j: (b, 0, j))
    args, in_specs = [q, k, v], [qspec, kspec, vspec]
    if bias is not None:
        args += list(bias)
        in_specs += [ospec, pl.BlockSpec((None, None, cfg.nk, 8, tk), lambda b, j: (b, j, 0, 0, 0))]
    aliases = {}
    if state is not None:
        aliases = {len(args) + t: t for t in range(3 if not finalize else 2)}
        args += list(state)
        in_specs += [ospec] * 3
    n_out = 2 if finalize else 3
    osd = jax.ShapeDtypeStruct((g, cfg.lq, out_cols), F32)
    return pl.pallas_call(
        body,
        out_shape=(osd,) * n_out,
        grid=(g, cfg.ncol),
        in_specs=in_specs,
        out_specs=(ospec,) * n_out,
        input_output_aliases=aliases,
        compiler_params=_cparams(dimension_semantics=("parallel", "parallel")),
        name=name,
    )(*args)


def _attn_bwd(cfg, q, k, v, do, o, lse, *, out_cols, kv_cols, bias=None, acc=None, do_off=0, name):
    g = q.shape[0]
    tq, tk, e, nh = cfg.tq, cfg.tk, cfg.e, cfg.nh
    t0 = (((0,), (0,)), ((), ()))

    def body(*refs):
        refs = list(refs)
        q_ref, k_ref, v_ref, do_ref, o_ref, lse_ref = refs[:6]
        del refs[:6]
        if bias is not None:
            cb_ref, cr_ref = refs[:2]
            del refs[:2]
        if acc is not None:
            dqi_ref, dki_ref, dvi_ref = refs[:3]
            del refs[:3]
        dq_ref, dk_ref, dv_ref = refs[:3]
        dcr_ref = refs[3] if bias is not None else None
        masks = _head_masks(nh)
        dlt0 = lax.broadcasted_iota(jnp.int32, (tq, tk), 0) - lax.broadcasted_iota(jnp.int32, (tq, tk), 1)
        if acc is not None:
            dq_ref[...] = dqi_ref[...]
            dk_ref[...] = dki_ref[...]
            dv_ref[...] = dvi_ref[...]
        else:
            dq_ref[...] = jnp.zeros_like(dq_ref)
            dk_ref[...] = jnp.zeros_like(dk_ref)
            dv_ref[...] = jnp.zeros_like(dv_ref)
        if dcr_ref is not None:
            dcr_ref[...] = jnp.zeros_like(dcr_ref)

        def qbody(i, carry):
            q0 = pl.multiple_of(i * tq, tq)
            rows = pl.ds(q0, tq)
            qb = q_ref[rows, :]
            dob = do_ref[rows, :].astype(BF16)
            prod = dob.astype(F32) * o_ref[rows, :]
            lo, hi = cfg.k_range(i)
            dqs = []
            for h in range(nh):
                qh = _sel(masks[h], qb, jnp.zeros_like(qb))
                doh = _sel(masks[h], dob, jnp.zeros_like(dob))
                lse_h = lse_ref[rows, h * e:h * e + 1]
                delta = jnp.sum(_sel(masks[h], prod, jnp.zeros_like(prod)), axis=1, keepdims=True)
                cq = cb_ref[rows, h * e:h * e + 1] if bias is not None else None

                def kbody(jk, dq_acc, qh=qh, doh=doh, lse_h=lse_h, delta=delta, cq=cq, h=h):
                    k0 = pl.multiple_of(jk * tk, tk)
                    krows = pl.ds(k0, tk)
                    kb = k_ref[krows, :]
                    vb = v_ref[krows, :]
                    b = (cq - cr_ref[jk, h:h + 1, :]) if bias is not None else None
                    s = _scores(cfg, qh, kb, q0, k0, dlt0, b)
                    p = jnp.exp(s - lse_h)
                    dp = lax.dot_general(doh, vb, (((1,), (1,)), ((), ())), preferred_element_type=F32)
                    ds = p * (dp - delta)
                    if dcr_ref is not None:
                        dcr_ref[jk, h:h + 1, :] += jnp.sum(ds, axis=0, keepdims=True)
                    dsb = (ds * cfg.scale).astype(BF16)
                    dv_ref[krows, :] += lax.dot_general(p.astype(BF16), doh, t0, preferred_element_type=F32)
                    dk_ref[krows, :] += lax.dot_general(dsb, qh, t0, preferred_element_type=F32)
                    return dq_acc + jnp.dot(dsb, kb, preferred_element_type=F32)

                dqs.append(lax.fori_loop(lo, hi, kbody, jnp.zeros((tq, LANES), F32)))
            dq = dqs[0] if nh == 1 else jnp.where(masks[0], dqs[0], dqs[1])
            dq_ref[rows, :] += dq
            return carry

        lax.fori_loop(0, cfg.nq, qbody, 0)

    qspec = pl.BlockSpec((None, cfg.lq, LANES), lambda b, j: (b, 0, cfg.qcol(j)))
    kspec = pl.BlockSpec((None, cfg.lk, LANES), lambda b, j: (b, 0, cfg.kcol(j)))
    vspec = pl.BlockSpec((None, cfg.lk, LANES), lambda b, j: (b, 0, cfg.vcol(j)))
    ospec = pl.BlockSpec((None, cfg.lq, LANES), lambda b, j: (b, 0, j))
    kvspec = pl.BlockSpec((None, cfg.lk, LANES), lambda b, j: (b, 0, j))
    dospec = pl.BlockSpec((None, cfg.lq, LANES), lambda b, j: (b, 0, do_off + j))
    args, in_specs = [q, k, v, do, o, lse], [qspec, kspec, vspec, dospec, ospec, ospec]
    out_shape = [jax.ShapeDtypeStruct((g, cfg.lq, out_cols), F32), jax.ShapeDtypeStruct((g, cfg.lk, kv_cols), F32),
                 jax.ShapeDtypeStruct((g, cfg.lk, kv_cols), F32)]
    out_specs = [ospec, kvspec, kvspec]
    if bias is not None:
        args += list(bias)
        crspec = pl.BlockSpec((None, None, cfg.nk, 8, tk), lambda b, j: (b, j, 0, 0, 0))
        in_specs += [ospec, crspec]
        out_shape.append(jax.ShapeDtypeStruct((g, cfg.ncol, cfg.nk, 8, tk), F32))
        out_specs.append(crspec)
    aliases = {}
    if acc is not None:
        aliases = {len(args) + t: t for t in range(3)}
        args += list(acc)
        in_specs += [ospec, kvspec, kvspec]
    return pl.pallas_call(
        body,
        out_shape=tuple(out_shape),
        grid=(g, cfg.ncol),
        in_specs=in_specs,
        out_specs=tuple(out_specs),
        input_output_aliases=aliases,
        compiler_params=_cparams(dimension_semantics=("parallel", "parallel")),
        name=name,
    )(*args)


BLK = 128
NBLK = SEQ // BLK
QK_SCALE = 1.0 / math.sqrt(HEAD_DIM)
DIL_STEPS = tuple(d for _, d in DILATIONS)
assert all(w // d == BLK for w, d in DILATIONS)
_T0 = (((0,), (0,)), ((), ()))
_NT = (((1,), (1,)), ((), ()))


def _stack_heads(a, masks):
    z = jnp.zeros_like(a)
    return jnp.concatenate([jnp.where(masks[0], a, z), jnp.where(masks[1], a, z)], axis=0)


def _tri_bias(lower):
    r = lax.broadcasted_iota(jnp.int32, (BLK, BLK), 0)
    c = lax.broadcasted_iota(jnp.int32, (BLK, BLK), 1)
    return jnp.where((c <= r) if lower else (c >= r), 0.0, NEG_INF).astype(F32)


def _dil_rows(r, i, d):
    start = r + i * (BLK * d)
    return pl.ds(start, BLK) if d == 1 else pl.ds(start, BLK, stride=d)


DIL_SET = 4


def _dil_sets(d, fn):
    nbk = SEQ // d // BLK
    if d == 1:
        def gbody(g, c):
            fn([(0, DIL_SET * g + a, None if a == 0 else True) for a in range(DIL_SET)])
            return c
        lax.fori_loop(0, nbk // DIL_SET, gbody, 0)
    elif nbk > 1:
        assert nbk == DIL_SET
        def rbody(r, c):
            fn([(r, i, i > 0) for i in range(nbk)])
            return c
        lax.fori_loop(0, d, rbody, 0)
    else:
        def rbody(rr, c):
            fn([(DIL_SET * rr + a, 0, False) for a in range(DIL_SET)])
            return c
        lax.fori_loop(0, d // DIL_SET, rbody, 0)


def _dil_key_tiles(r, i, d, has_prev, qrows, tri_cur, tri_prev):
    tiles = [(qrows, tri_cur)]
    if has_prev is None:
        tiles.append((_dil_rows(r, jnp.maximum(i - 1, 0), d), tri_prev + jnp.where(i > 0, 0.0, NEG_INF)))
    elif has_prev:
        tiles.append((_dil_rows(r, i - 1, d), tri_prev))
    return tiles


def _dil_fwd(qkv, *, name):
    nb = qkv.shape[0]
    ncol = DIL_W // LANES
    hd = HEAD_DIM

    def body(q_ref, k_ref, v_ref, o_ref, lse_ref, m_ref, l_ref, a_ref):
        masks = _head_masks(2)
        tri_cur, tri_prev = _tri_bias(True), _tri_bias(False)
        for pi, d in enumerate(DIL_STEPS):
            first, last = pi == 0, pi == len(DIL_STEPS) - 1

            def qset(blocks, d=d, first=first, last=last):
                work = []
                for r, i, has_prev in blocks:
                    qrows = _dil_rows(r, i, d)
                    qcat = _stack_heads((q_ref[qrows, :] * QK_SCALE).astype(BF16), masks)
                    ss, krs = [], []
                    for krows, bias in _dil_key_tiles(r, i, d, has_prev, qrows, tri_cur, tri_prev):
                        s = lax.dot_general(qcat, k_ref[krows, :].astype(BF16), _NT, preferred_element_type=F32)
                        ss.append((s[:BLK] + bias, s[BLK:] + bias))
                        krs.append(krows)
                    work.append((qrows, ss, krs))
                for qrows, ss, krs in work:
                    e0 = ss[0][0] if len(ss) == 1 else jnp.maximum(ss[0][0], ss[1][0])
                    e1 = ss[0][1] if len(ss) == 1 else jnp.maximum(ss[0][1], ss[1][1])
                    n0 = jnp.max(e0, axis=1, keepdims=True)
                    n1 = jnp.max(e1, axis=1, keepdims=True)
                    if not first:
                        mo, lo = m_ref[qrows, :], l_ref[qrows, :]
                        m0, m1 = mo[:, 0:1], mo[:, hd:hd + 1]
                        n0, n1 = jnp.maximum(n0, m0), jnp.maximum(n1, m1)
                        a0, a1 = jnp.exp(m0 - n0), jnp.exp(m1 - n1)
                    ps = [(jnp.exp(s0 - n0), jnp.exp(s1 - n1)) for s0, s1 in ss]
                    t0 = ps[0][0] if len(ps) == 1 else ps[0][0] + ps[1][0]
                    t1 = ps[0][1] if len(ps) == 1 else ps[0][1] + ps[1][1]
                    l0 = jnp.sum(t0, axis=1, keepdims=True)
                    l1 = jnp.sum(t1, axis=1, keepdims=True)
                    acc = None
                    for (p0, p1), krows in zip(ps, krs):
                        vcat = _stack_heads(v_ref[krows, :].astype(BF16), masks)
                        pv = jnp.dot(jnp.concatenate([p0, p1], axis=1).astype(BF16), vcat, preferred_element_type=F32)
                        acc = pv if acc is None else acc + pv
                    if not first:
                        l0 = l0 + a0 * lo[:, 0:1]
                        l1 = l1 + a1 * lo[:, hd:hd + 1]
                        acc = acc + a_ref[qrows, :] * jnp.where(masks[0], a0, a1)
                    if last:
                        o_ref[qrows, :] = acc / jnp.where(masks[0], l0, l1)
                        lse_ref[qrows, :] = jnp.where(masks[0], n0 + jnp.log(l0), n1 + jnp.log(l1))
                    else:
                        m_ref[qrows, :] = jnp.where(masks[0], n0, n1)
                        l_ref[qrows, :] = jnp.where(masks[0], l0, l1)
                        a_ref[qrows, :] = acc

            _dil_sets(d, qset)

    spec = lambda off: pl.BlockSpec((None, SEQ, LANES), lambda b, j: (b, 0, off + j))
    ospec = pl.BlockSpec((None, SEQ, LANES), lambda b, j: (b, 0, j))
    osd = jax.ShapeDtypeStruct((nb, SEQ, DIL_W), F32)
    return pl.pallas_call(
        body, out_shape=(osd, osd), grid=(nb, ncol),
        in_specs=[spec(0), spec(ncol), spec(2 * ncol)], out_specs=(ospec, ospec),
        scratch_shapes=[pltpu.VMEM((SEQ, LANES), F32)] * 3,
        compiler_params=_cparams(dimension_semantics=("parallel", "parallel")), name=name,
    )(qkv, qkv, qkv)


def _dil_bwd(qkv, do, o, lse, tabs, *, do_off, name):
    nb = qkv.shape[0]
    ncol = DIL_W // LANES
    hd = HEAD_DIM

    def body(q_ref, k_ref, v_ref, do_ref, o_ref, lse_ref, c_ref, s1_ref, s2_ref, dqo_ref, dko_ref, dvo_ref,
             dq_ref, dk_ref, dv_ref, dl_ref, dof_ref):
        masks = _head_masks(2)
        tri_cur, tri_prev = _tri_bias(True), _tri_bias(False)
        dq_ref[...] = jnp.zeros_like(dq_ref)
        dk_ref[...] = jnp.zeros_like(dk_ref)
        dv_ref[...] = jnp.zeros_like(dv_ref)

        def delta_body(i, c):
            rows = pl.ds(pl.multiple_of(i * BLK, BLK), BLK)
            dof = do_ref[rows, :].astype(F32)
            dof_ref[rows, :] = dof
            prod = dof * o_ref[rows, :]
            z = jnp.zeros_like(prod)
            dl_ref[rows, :] = jnp.where(masks[0], jnp.sum(jnp.where(masks[0], prod, z), axis=1, keepdims=True),
                                        jnp.sum(jnp.where(masks[1], prod, z), axis=1, keepdims=True))
            return c

        lax.fori_loop(0, NBLK, delta_body, 0)

        for d in DIL_STEPS:
            def qset(blocks, d=d):
                work = []
                for r, i, has_prev in blocks:
                    qrows = _dil_rows(r, i, d)
                    qcat = _stack_heads((q_ref[qrows, :] * QK_SCALE).astype(BF16), masks)
                    docat = _stack_heads(dof_ref[qrows, :].astype(BF16), masks)
                    tiles = []
                    for krows, bias in _dil_key_tiles(r, i, d, has_prev, qrows, tri_cur, tri_prev):
                        s = lax.dot_general(qcat, k_ref[krows, :].astype(BF16), _NT, preferred_element_type=F32)
                        dp = lax.dot_general(docat, v_ref[krows, :].astype(BF16), _NT, preferred_element_type=F32)
                        tiles.append((krows, s, dp, bias))
                    work.append((qrows, qcat, docat, tiles))
                for qrows, qcat, docat, tiles in work:
                    lseb, dlb = lse_ref[qrows, :], dl_ref[qrows, :]
                    lse0, lse1 = lseb[:, 0:1], lseb[:, hd:hd + 1]
                    dl0, dl1 = dlb[:, 0:1], dlb[:, hd:hd + 1]
                    dq = None
                    for krows, s, dp, bias in tiles:
                        p0 = jnp.exp(s[:BLK] + bias - lse0)
                        p1 = jnp.exp(s[BLK:] + bias - lse1)
                        ds0 = p0 * (dp[:BLK] - dl0)
                        ds1 = p1 * (dp[BLK:] - dl1)
                        pcat = jnp.concatenate([p0, p1], axis=0).astype(BF16)
                        dscat = jnp.concatenate([ds0, ds1], axis=0).astype(BF16)
                        dv_ref[krows, :] += lax.dot_general(pcat, docat, _T0, preferred_element_type=F32)
                        dk_ref[krows, :] += lax.dot_general(dscat, qcat, _T0, preferred_element_type=F32)
                        dsrow = jnp.concatenate([ds0, ds1], axis=1).astype(BF16)
                        kcat = _stack_heads((k_ref[krows, :] * QK_SCALE).astype(BF16), masks)
                        t = jnp.dot(dsrow, kcat, preferred_element_type=F32)
                        dq = t if dq is None else dq + t
                    dq_ref[qrows, :] += dq

            _dil_sets(d, qset)

        def out_body(i, c):
            rows = pl.ds(pl.multiple_of(i * BLK, BLK), BLK)
            tab = (c_ref[rows, :], s1_ref[rows, :], s2_ref[rows, :])
            dqo_ref[rows, :] = _rope_apply(dq_ref[rows, :], *tab, transpose=True).astype(dqo_ref.dtype)
            dko_ref[rows, :] = _rope_apply(dk_ref[rows, :], *tab, transpose=True).astype(dko_ref.dtype)
            dvo_ref[rows, :] = dv_ref[rows, :].astype(dvo_ref.dtype)
            return c

        lax.fori_loop(0, NBLK, out_body, 0)

    spec = lambda off: pl.BlockSpec((None, SEQ, LANES), lambda b, j: (b, 0, off + j))
    ospec = pl.BlockSpec((None, SEQ, LANES), lambda b, j: (b, 0, j))
    tspec = pl.BlockSpec((SEQ, LANES), lambda b, j: (0, 0))
    osd = jax.ShapeDtypeStruct((nb, SEQ, DIL_W), BF16)
    return pl.pallas_call(
        body, out_shape=(osd, osd, osd), grid=(nb, ncol),
        in_specs=[spec(0), spec(ncol), spec(2 * ncol), spec(do_off), ospec, ospec, tspec, tspec, tspec],
        out_specs=(ospec, ospec, ospec),
        scratch_shapes=[pltpu.VMEM((SEQ, LANES), F32)] * 5,
        compiler_params=_cparams(dimension_semantics=("parallel", "parallel")), name=name,
    )(qkv, qkv, qkv, do, o, lse, *tabs)


FOX_GROUP = 4
assert NBLK % FOX_GROUP == 0
_FOX_COLS = tuple(c // LANES for c in (C_FQ, C_FK, C_FV))


def _fox_specs():
    cols = [pl.BlockSpec((None, SEQ, LANES), (lambda b, j, off=off: (b, 0, off + j))) for off in _FOX_COLS]
    ospec = pl.BlockSpec((None, SEQ, LANES), lambda b, j: (b, 0, j))
    crspec = pl.BlockSpec((None, None, NBLK, 8, BLK), lambda b, j: (b, j, 0, 0, 0))
    return cols, ospec, crspec


def _fox_key_rows(t, e):
    return pl.ds(pl.multiple_of((FOX_GROUP * t + e) * BLK, BLK), BLK)


def _fox_fwd(p3, crow, *, name):
    nb = p3.shape[0]
    g = FOX_GROUP

    def body(q_ref, k_ref, v_ref, cr_ref, o_ref, lse_ref):
        masks = _head_masks(2)
        tri = _tri_bias(True)

        def qk(qcat, t):
            return tuple(lax.dot_general(qcat, k_ref[_fox_key_rows(t, e), :], _NT, preferred_element_type=F32) for e in range(g))

        def consume(ss, t, state, nblk, diag):
            m0, m1, l0, l1, acc = state
            us = []
            for e in range(nblk):
                cr = cr_ref[g * t + e]
                u0 = ss[e][:BLK] - cr[0:1, :]
                u1 = ss[e][BLK:] - cr[1:2, :]
                if diag and e == nblk - 1:
                    u0, u1 = u0 + tri, u1 + tri
                us.append((u0, u1))
            x0 = functools.reduce(jnp.maximum, [u[0] for u in us])
            x1 = functools.reduce(jnp.maximum, [u[1] for u in us])
            n0 = jnp.maximum(m0, jnp.max(x0, axis=1, keepdims=True))
            n1 = jnp.maximum(m1, jnp.max(x1, axis=1, keepdims=True))
            a0, a1 = jnp.exp(m0 - n0), jnp.exp(m1 - n1)
            acc = acc * jnp.where(masks[0], a0, a1)
            t0 = t1 = None
            for e in range(nblk):
                p0, p1 = jnp.exp(us[e][0] - n0), jnp.exp(us[e][1] - n1)
                t0 = p0 if t0 is None else t0 + p0
                t1 = p1 if t1 is None else t1 + p1
                pcat = jnp.concatenate([p0, p1], axis=1)
                hi = pcat.astype(BF16)
                lo = (pcat - hi.astype(F32)).astype(BF16)
                vcat = _stack_heads(v_ref[_fox_key_rows(t, e), :], masks)
                acc = acc + jnp.dot(hi, vcat, preferred_element_type=F32) + jnp.dot(lo, vcat, preferred_element_type=F32)
            l0 = a0 * l0 + jnp.sum(t0, axis=1, keepdims=True)
            l1 = a1 * l1 + jnp.sum(t1, axis=1, keepdims=True)
            return n0, n1, l0, l1, acc

        def qblock(ng, a):
            rows = pl.ds(pl.multiple_of((g * ng + a) * BLK, BLK), BLK)
            qcat = _stack_heads(q_ref[rows, :] * QK_SCALE, masks)
            neg = jnp.full((BLK, 1), NEG_INF, F32)
            z1 = jnp.zeros((BLK, 1), F32)
            state = (neg, neg, z1, z1, jnp.zeros((BLK, LANES), F32))

            def step(t, c):
                ss, st = c
                nxt = qk(qcat, t + 1)
                return nxt, consume(ss, t, st, g, False)

            ss, state = lax.fori_loop(0, ng, step, (qk(qcat, 0), state))
            m0, m1, l0, l1, acc = consume(ss, ng, state, a + 1, True)
            o_ref[rows, :] = acc / jnp.where(masks[0], l0, l1)
            lse_ref[rows, :] = jnp.where(masks[0], m0 + jnp.log(l0), m1 + jnp.log(l1))

        def gbody(ng, c):
            for a in range(g):
                qblock(ng, a)
            return c

        lax.fori_loop(0, NBLK // g, gbody, 0)

    cols, ospec, crspec = _fox_specs()
    osd = jax.ShapeDtypeStruct((nb, SEQ, FOX_W), F32)
    return pl.pallas_call(
        body, out_shape=(osd, osd), grid=(nb, FOX_W // LANES), in_specs=cols + [crspec], out_specs=(ospec, ospec),
        compiler_params=_cparams(dimension_semantics=("parallel", "parallel")), name=name,
    )(p3, p3, p3, crow)


def _fox_bwd(p3, crow, do, o, lse, *, do_off, name):
    nb = p3.shape[0]
    g = FOX_GROUP
    hd = HEAD_DIM

    def body(q_ref, k_ref, v_ref, cr_ref, do_ref, o_ref, lse_ref, dq_ref, dko_ref, dvo_ref, dcr_ref, dk_ref, dv_ref):
        masks = _head_masks(2)
        tri = _tri_bias(True)
        dk_ref[...] = jnp.zeros_like(dk_ref)
        dv_ref[...] = jnp.zeros_like(dv_ref)
        dcr_ref[...] = jnp.zeros_like(dcr_ref)

        def products(qcat, docat, t):
            out = []
            for e in range(g):
                krows = _fox_key_rows(t, e)
                out.append(lax.dot_general(qcat, k_ref[krows, :], _NT, preferred_element_type=F32))
                out.append(lax.dot_general(docat, v_ref[krows, :], _NT, preferred_element_type=F32))
            return tuple(out)

        def consume(prod, t, ctx, dq, nblk, diag):
            qcat, docat, lse0, lse1, dl0, dl1 = ctx
            for e in range(nblk):
                jb = g * t + e
                krows = _fox_key_rows(t, e)
                s, dp = prod[2 * e], prod[2 * e + 1]
                cr = cr_ref[jb]
                u0 = s[:BLK] - cr[0:1, :]
                u1 = s[BLK:] - cr[1:2, :]
                if diag and e == nblk - 1:
                    u0, u1 = u0 + tri, u1 + tri
                p0 = jnp.exp(u0 - lse0)
                p1 = jnp.exp(u1 - lse1)
                ds0 = p0 * (dp[:BLK] - dl0)
                ds1 = p1 * (dp[BLK:] - dl1)
                dcr_ref[jb, 0:1, :] += jnp.sum(ds0, axis=0, keepdims=True)
                dcr_ref[jb, 1:2, :] += jnp.sum(ds1, axis=0, keepdims=True)
                pcat = jnp.concatenate([p0, p1], axis=0).astype(BF16)
                dscat = jnp.concatenate([ds0, ds1], axis=0).astype(BF16)
                dv_ref[krows, :] += lax.dot_general(pcat, docat, _T0, preferred_element_type=F32)
                dk_ref[krows, :] += lax.dot_general(dscat, qcat, _T0, preferred_element_type=F32)
                dsrow = jnp.concatenate([ds0, ds1], axis=1).astype(BF16)
                dq = dq + jnp.dot(dsrow, _stack_heads(k_ref[krows, :] * QK_SCALE, masks), preferred_element_type=F32)
            return dq

        def qblock(ng, a):
            rows = pl.ds(pl.multiple_of((g * ng + a) * BLK, BLK), BLK)
            qcat = _stack_heads(q_ref[rows, :] * QK_SCALE, masks)
            dob = do_ref[rows, :].astype(BF16)
            docat = _stack_heads(dob, masks)
            prod = dob.astype(F32) * o_ref[rows, :]
            z = jnp.zeros_like(prod)
            dl0 = jnp.sum(jnp.where(masks[0], prod, z), axis=1, keepdims=True)
            dl1 = jnp.sum(jnp.where(masks[1], prod, z), axis=1, keepdims=True)
            lseb = lse_ref[rows, :]
            ctx = (qcat, docat, lseb[:, 0:1], lseb[:, hd:hd + 1], dl0, dl1)

            def step(t, c):
                pr, dq = c
                nxt = products(qcat, docat, t + 1)
                return nxt, consume(pr, t, ctx, dq, g, False)

            pr, dq = lax.fori_loop(0, ng, step, (products(qcat, docat, 0), jnp.zeros((BLK, LANES), F32)))
            dq_ref[rows, :] = consume(pr, ng, ctx, dq, a + 1, True).astype(dq_ref.dtype)

        def gbody(ng, c):
            for a in range(g):
                qblock(ng, a)
            return c

        lax.fori_loop(0, NBLK // g, gbody, 0)
        dko_ref[...] = dk_ref[...].astype(dko_ref.dtype)
        dvo_ref[...] = dv_ref[...].astype(dvo_ref.dtype)

    cols, ospec, crspec = _fox_specs()
    dospec = pl.BlockSpec((None, SEQ, LANES), lambda b, j: (b, 0, do_off + j))
    osd = jax.ShapeDtypeStruct((nb, SEQ, FOX_W), BF16)
    return pl.pallas_call(
        body, out_shape=(osd, osd, osd, jax.ShapeDtypeStruct((nb, FOX_W // LANES, NBLK, 8, BLK), F32)),
        grid=(nb, FOX_W // LANES), in_specs=cols + [crspec, dospec, ospec, ospec], out_specs=(ospec, ospec, ospec, crspec),
        scratch_shapes=[pltpu.VMEM((SEQ, LANES), F32)] * 2,
        compiler_params=_cparams(dimension_semantics=("parallel", "parallel")), name=name,
    )(p3, p3, p3, crow, do, o, lse)


def _mem_cfg():
    return _AttnCfg(e=MEM_HEAD_DIM, tq=256, tk=MEM_LEN, lq=SEQ, lk=MEM_LEN, causal=False, window=None, ncol=MEM_HEADS,
                    qcol=lambda j: C_MQ // LANES + j, kcol=lambda j: j, vcol=lambda j: MEM_HEADS + j)


N_YBLK = MIX_W // LANES
_GATE_BLK = (C_FG // LANES, C_DG // LANES, C_MG // LANES)
_B1, _B2 = FOX_W // LANES, (FOX_W + DIL_W) // LANES


def _att_specs(tm):
    fspec = pl.BlockSpec((tm, LANES), lambda i, j: (i, jnp.minimum(j, _B1 - 1)))
    dspec = pl.BlockSpec((tm, LANES), lambda i, j: (i, jnp.clip(j - _B1, 0, _B2 - _B1 - 1)))
    mspec = pl.BlockSpec((tm, LANES), lambda i, j: (i, jnp.clip(j - _B2, 0, N_YBLK - _B2 - 1)))

    def gcol(j):
        return jnp.where(j < _B1, _GATE_BLK[0] + j, jnp.where(j < _B2, _GATE_BLK[1] + j - _B1, _GATE_BLK[2] + j - _B2))

    gspec = pl.BlockSpec((tm, LANES), lambda i, j: (i, gcol(j)))
    return fspec, dspec, mspec, gspec


def _pick_att(j, f_ref, d_ref, m_ref):
    return jnp.where(j < _B1, f_ref[...], jnp.where(j < _B2, d_ref[...], m_ref[...]))


def _gate_fwd(fox, dil, memo, p16, *, tm, name):
    t = fox.shape[0]

    def body(f_ref, d_ref, m_ref, g_ref, y_ref):
        j = pl.program_id(1)
        a = _pick_att(j, f_ref, d_ref, m_ref)
        gt = g_ref[...].astype(F32)
        y_ref[...] = (a * gt / (1.0 + jnp.exp(-gt))).astype(y_ref.dtype)

    return pl.pallas_call(
        body,
        out_shape=jax.ShapeDtypeStruct((t, MIX_W), BF16),
        grid=(t // tm, N_YBLK),
        in_specs=list(_att_specs(tm)),
        out_specs=pl.BlockSpec((tm, LANES), lambda i, j: (i, j)),
        compiler_params=_cparams(dimension_semantics=("parallel", "parallel")),
        name=name,
    )(fox, dil, memo, p16)


def _gate_bwd(dy, fox, dil, memo, p16, *, tm, name):
    t = fox.shape[0]

    def body(dy_ref, f_ref, d_ref, m_ref, g_ref, da_ref, dg_ref):
        j = pl.program_id(1)
        a = _pick_att(j, f_ref, d_ref, m_ref)
        gt = g_ref[...].astype(F32)
        sg = 1.0 / (1.0 + jnp.exp(-gt))
        dyv = dy_ref[...]
        da_ref[...] = (dyv * gt * sg).astype(da_ref.dtype)
        dg_ref[...] = (dyv * a * sg * (1.0 + gt * (1.0 - sg))).astype(dg_ref.dtype)

    yspec = pl.BlockSpec((tm, LANES), lambda i, j: (i, j))
    return pl.pallas_call(
        body,
        out_shape=(jax.ShapeDtypeStruct((t, MIX_W), BF16), jax.ShapeDtypeStruct((t, MIX_W), BF16)),
        grid=(t // tm, N_YBLK),
        in_specs=[yspec] + list(_att_specs(tm)),
        out_specs=(yspec, yspec),
        compiler_params=_cparams(dimension_semantics=("parallel", "parallel")),
        name=name,
    )(dy, fox, dil, memo, p16)


def _out_loss(y, wo, x, tgt, gfin, *, tm, name):
    t, d = x.shape
    n_feat = float(d)

    def body(y_ref, w_ref, x_ref, t_ref, g_ref, dx_ref, dxb_ref, st_ref):
        i = pl.program_id(0)

        @pl.when(i == 0)
        def _():
            st_ref[...] = jnp.zeros_like(st_ref)

        x2 = x_ref[...] + jnp.dot(y_ref[...], w_ref[...], preferred_element_type=F32)
        r = lax.rsqrt(jnp.mean(x2 * x2, axis=-1, keepdims=True) + RMS_EPS)
        nrm = x2 * r
        gv = g_ref[...]
        err = nrm * gv - t_ref[...]
        dout = err * (1.0 / n_feat)
        dn = dout * gv
        dx2 = r * (dn - nrm * jnp.mean(dn * nrm, axis=-1, keepdims=True))
        dx_ref[...] = dx2
        dxb_ref[...] = dx2.astype(dxb_ref.dtype)
        st_ref[0:1, :] += jnp.sum(dout * nrm, axis=0, keepdims=True)
        st_ref[1:2, :] += (0.5 / n_feat) * jnp.sum(err * err, axis=0, keepdims=True)

    row = pl.BlockSpec((tm, d), lambda i: (i, 0))
    return pl.pallas_call(
        body,
        out_shape=(jax.ShapeDtypeStruct((t, d), F32), jax.ShapeDtypeStruct((t, d), BF16), jax.ShapeDtypeStruct((8, d), F32)),
        grid=(t // tm,),
        in_specs=[pl.BlockSpec((tm, MIX_W), lambda i: (i, 0)), pl.BlockSpec((MIX_W, d), lambda i: (0, 0)), row, row,
                  pl.BlockSpec((1, d), lambda i: (0, 0))],
        out_specs=(row, row, pl.BlockSpec((8, d), lambda i: (0, 0))),
        compiler_params=_cparams(dimension_semantics=("arbitrary",)),
        name=name,
    )(y, wo, x, tgt, gfin)


def _dh_rms_bwd(dp, w, x, g, resid, *, tm, tk, name):
    t, d = x.shape
    kdim = dp.shape[1]
    nk = kdim // tk

    def body(*refs):
        if resid is not None:
            dp_ref, w_ref, x_ref, g_ref, r_ref, dx_ref, gg_ref, acc_ref = refs
        else:
            dp_ref, w_ref, x_ref, g_ref, dx_ref, gg_ref, acc_ref = refs
        i = pl.program_id(0)
        k = pl.program_id(1)

        @pl.when(jnp.logical_and(i == 0, k == 0))
        def _():
            gg_ref[...] = jnp.zeros_like(gg_ref)

        @pl.when(k == 0)
        def _():
            acc_ref[...] = jnp.zeros_like(acc_ref)

        acc_ref[...] += lax.dot_general(dp_ref[...], w_ref[...], _NT, preferred_element_type=F32)

        @pl.when(k == nk - 1)
        def _():
            dh = acc_ref[...]
            xv = x_ref[...]
            r = lax.rsqrt(jnp.mean(xv * xv, axis=-1, keepdims=True) + RMS_EPS)
            nrm = xv * r
            dn = dh * g_ref[...]
            dx = r * (dn - nrm * jnp.mean(dn * nrm, axis=-1, keepdims=True))
            if resid is not None:
                dx = dx + r_ref[...]
            dx_ref[...] = dx
            gg_ref[0:1, :] += jnp.sum(dh * nrm, axis=0, keepdims=True)

    row = pl.BlockSpec((tm, d), lambda i, k: (i, 0))
    in_specs = [pl.BlockSpec((tm, tk), lambda i, k: (i, k)), pl.BlockSpec((d, tk), lambda i, k: (0, k)), row,
                pl.BlockSpec((1, d), lambda i, k: (0, 0))]
    args = [dp, w, x, g]
    if resid is not None:
        in_specs.append(row)
        args.append(resid)
    return pl.pallas_call(
        body,
        out_shape=(jax.ShapeDtypeStruct((t, d), F32), jax.ShapeDtypeStruct((8, d), F32)),
        grid=(t // tm, nk),
        in_specs=in_specs,
        out_specs=(row, pl.BlockSpec((8, d), lambda i, k: (0, 0))),
        scratch_shapes=[pltpu.VMEM((tm, d), F32)],
        compiler_params=_cparams(dimension_semantics=("arbitrary", "arbitrary")),
        name=name,
    )(*args)


_FLOG0 = 4 * FOX_W
_W_IN_SEGMENTS = ((0, _FLOG0, 0), (_FLOG0, _FLOG0 + FOX_HEADS, PW), (_FLOG0 + FOX_HEADS, IN_W, C_DQ))
SHARD_W = IN_W // N_CHIPS


def _rearrange_w_in(shards):
    def cols(lo, hi):
        parts = []
        for k in range(N_CHIPS):
            a, b = max(lo, k * SHARD_W), min(hi, (k + 1) * SHARD_W)
            if a < b:
                parts.append(shards[k][:, a - k * SHARD_W:b - k * SHARD_W])
        return parts

    (a0, a1, _), (f0, f1, _), (b0, b1, _) = _W_IN_SEGMENTS
    pad = jnp.zeros((shards[0].shape[0], LANES - FOX_HEADS), shards[0].dtype)
    return jnp.concatenate(cols(a0, a1) + cols(b0, b1) + cols(f0, f1) + [pad], axis=1)


def _w_in_grad_slabs(g):
    slabs = []
    for k in range(N_CHIPS):
        parts = []
        for lo, hi, at in _W_IN_SEGMENTS:
            a, b = max(lo, k * SHARD_W), min(hi, (k + 1) * SHARD_W)
            if a < b:
                parts.append(g[:, at + a - lo:at + b - lo])
        slabs.append(jnp.concatenate(parts, axis=1))
    return jnp.stack(slabs, axis=0)


def _local_grads(x, mem, norm_g, w_r, b_forget, mem_norm_g, w_kv, w_o, final_norm_g, tgt):
    nb = x.shape[0]
    t = nb * SEQ
    x2d = x.reshape(t, D_MODEL)
    tgt2d = tgt.reshape(t, D_MODEL)
    tabs = _rope_tables()
    bpad = jnp.pad(b_forget.reshape(1, FOX_HEADS), ((0, 0), (0, LANES - FOX_HEADS)))

    h = _rms_fwd(x2d, norm_g.reshape(1, D_MODEL), tm=512, name="rms_x")
    p16, dqkv = _proj(h, w_r, tabs, n=PW, tm=2048, tn=256, name="proj")
    flog = _matmul(h, w_r[:, PW:], out_dtype=F32, tm=1024, tn=LANES, tk=D_MODEL, name="proj_flog")
    c12 = _flog_fwd(flog, bpad, nb=nb, ts=256, name="flog_fwd")

    crow = c12[:, :FOX_HEADS].reshape(nb, NBLK, BLK, FOX_HEADS // 2, 2).transpose(0, 3, 1, 4, 2)
    crow = jnp.pad(crow, ((0, 0), (0, 0), (0, 0), (0, 6), (0, 0)))
    p3 = p16.reshape(nb, SEQ, PW)
    fox, fox_lse = _fox_fwd(p3, crow, name="fox_fwd")

    dqkv3 = dqkv.reshape(nb, SEQ, 3 * DIL_W)
    dil, dil_lse = _dil_fwd(dqkv3, name="dil_fwd")

    mh = _rms_fwd(mem.reshape(nb * MEM_LEN, D_MODEL), mem_norm_g.reshape(1, D_MODEL), tm=nb * MEM_LEN, name="rms_mem")
    mkv = _matmul(mh, w_kv, out_dtype=BF16, tm=nb * MEM_LEN, tn=512, tk=D_MODEL, name="mem_kv")
    mkv3 = mkv.reshape(nb, MEM_LEN, 2 * MEM_W)
    mcfg = _mem_cfg()
    memo, mem_lse = _attn_fwd(mcfg, p3, mkv3, mkv3, out_cols=MEM_W, name="mem_fwd")

    fox2, dil2, memo2 = fox.reshape(t, FOX_W), dil.reshape(t, DIL_W), memo.reshape(t, MEM_W)
    y = _gate_fwd(fox2, dil2, memo2, p16, tm=1024, name="gate_fwd")
    dx2, dx2b, st = _out_loss(y, w_o, x2d, tgt2d, final_norm_g.reshape(1, D_MODEL), tm=512, name="out_loss")

    g_wo = _matmul(y, dx2b, mode="tn", out_dtype=F32, tm=1024, tn=512, tk=1024, name="grad_w_out")
    dy = _matmul(dx2b, w_o, mode="nt", out_dtype=F32, tm=1024, tn=512, tk=D_MODEL, name="d_y")
    datt, dgate = _gate_bwd(dy, fox2, dil2, memo2, p16, tm=1024, name="gate_bwd")
    datt3 = datt.reshape(nb, SEQ, MIX_W)

    dfq, dfk, dfv, dcr = _fox_bwd(p3, crow, datt3, fox, fox_lse, do_off=0, name="fox_bwd")
    dcol = -dcr[:, :, :, :2, :].transpose(0, 2, 4, 1, 3).reshape(t, FOX_HEADS)
    dcol = jnp.pad(dcol, ((0, 0), (0, LANES - FOX_HEADS)))
    dflog, gb = _flog_bwd(dcol, flog, bpad, nb=nb, ts=256, name="flog_bwd")

    ddq, ddk, ddv = _dil_bwd(dqkv3, datt3, dil, dil_lse, tabs, do_off=_B1, name="dil_bwd")

    dmq, dmk, dmv = _attn_bwd(mcfg, p3, mkv3, mkv3, datt3, memo, mem_lse, out_cols=MEM_W, kv_cols=MEM_W, do_off=_B2,
                              name="mem_bwd")
    dmkv = jnp.concatenate([dmk, dmv], axis=-1).reshape(nb * MEM_LEN, 2 * MEM_W).astype(BF16)
    g_wkv = _matmul(mh, dmkv, mode="tn", out_dtype=F32, tm=512, tn=512, tk=nb * MEM_LEN, name="grad_w_kv")
    _, gmn = _dh_rms_bwd(dmkv, w_kv, mem.reshape(nb * MEM_LEN, D_MODEL), mem_norm_g.reshape(1, D_MODEL), None,
                         tm=nb * MEM_LEN, tk=2 * MEM_W, name="mem_rms_bwd")

    flat = lambda a: a.reshape(t, -1)
    dp = jnp.concatenate([flat(dfq), flat(dfk), flat(dfv), dgate[:, :FOX_W], flat(ddq), flat(ddk), flat(ddv),
                          dgate[:, FOX_W:FOX_W + DIL_W], flat(dmq).astype(BF16), dgate[:, FOX_W + DIL_W:], dflog], axis=1)
    g_wr = _matmul(h, dp, mode="tn", out_dtype=F32, tm=512, tn=PWF // 3, tk=1024, name="grad_w_in")
    gx, gng = _dh_rms_bwd(dp, w_r, x2d, norm_g.reshape(1, D_MODEL), dx2, tm=512, tk=PWF // 3, name="in_rms_bwd")

    gb_row = jnp.pad(gb[0:1, :], ((0, 0), (0, D_MODEL - LANES)))
    small = jnp.concatenate([gng[0:1], gmn[0:1], st[0:1], gb_row, st[1:2], jnp.zeros((3, D_MODEL), F32)], axis=0)
    return gx.reshape(nb, SEQ, D_MODEL), g_wr, g_wkv, g_wo, small


MESH = pl.DeviceIdType.MESH
ANY = pl.BlockSpec(memory_space=pl.ANY)


def _place():
    x, y, c = lax.axis_index("x"), lax.axis_index("y"), lax.axis_index("c")
    other_chips = [(1 - x, y), (x, 1 - y), (1 - x, 1 - y)]
    return x, y, c, other_chips


def _gather_weights(shards):
    n = len(shards)

    def body(*refs):
        in_refs, out_refs = refs[:n], refs[n:2 * n]
        send_sems, recv_sems = refs[2 * n:]
        x, y, c, chips = _place()
        me_chip = 2 * x + y
        sibling = (x, y, 1 - c)

        def half(ref, pc, rows):
            return ref.at[pl.ds(pc * (rows // 2), rows // 2), :]

        def rcopy(k, src, dst, to):
            return pltpu.make_async_remote_copy(src_ref=src, dst_ref=dst, send_sem=send_sems.at[k], recv_sem=recv_sems.at[k],
                                                device_id=to, device_id_type=MESH)

        sends = []
        for t in range(n):
            rows = shards[t].shape[0]
            for j, chip in enumerate(chips):
                cp = rcopy(6 * t + j, half(in_refs[t], c, rows), half(out_refs[t].at[me_chip], c, rows), (*chip, c))
                cp.start()
                sends.append(cp)
        for t in range(n):
            rows = shards[t].shape[0]
            for j, chip in enumerate(chips):
                slot = out_refs[t].at[2 * chip[0] + chip[1]]
                rcopy(6 * t + j, half(slot, c, rows), half(slot, c, rows), sibling).wait_recv()
                fw = rcopy(6 * t + 3 + j, half(slot, c, rows), half(slot, c, rows), sibling)
                fw.start()
                sends.append(fw)
        for t in range(n):
            rows = shards[t].shape[0]
            for j, chip in enumerate(chips):
                slot = out_refs[t].at[2 * chip[0] + chip[1]]
                rcopy(6 * t + 3 + j, half(slot, 1 - c, rows), half(slot, 1 - c, rows), sibling).wait_recv()
        for cp in sends:
            cp.wait_send()

    return pl.pallas_call(
        body,
        out_shape=tuple(jax.ShapeDtypeStruct((N_CHIPS,) + s.shape, s.dtype) for s in shards),
        in_specs=[ANY] * n,
        out_specs=tuple([ANY] * n),
        scratch_shapes=[pltpu.SemaphoreType.DMA((6 * n,)), pltpu.SemaphoreType.DMA((6 * n,))],
        name="gather_weights",
    )(*shards)


def _pair_exchange(gs):
    n = len(gs)

    def body(*refs):
        g_refs, r_refs = refs[:n], refs[n:2 * n]
        send_sems, recv_sems = refs[2 * n:]
        x, y, c, _ = _place()
        cps = []
        for t in range(n):
            hr = gs[t].shape[1] // 2
            cp = pltpu.make_async_remote_copy(src_ref=g_refs[t].at[:, pl.ds((1 - c) * hr, hr), :], dst_ref=r_refs[t],
                                              send_sem=send_sems.at[t], recv_sem=recv_sems.at[t],
                                              device_id=(x, y, 1 - c), device_id_type=MESH)
            cp.start()
            cps.append(cp)
        for cp in cps:
            cp.wait()

    return pl.pallas_call(
        body,
        out_shape=tuple(jax.ShapeDtypeStruct((N_CHIPS, g.shape[1] // 2, g.shape[2]), g.dtype) for g in gs),
        in_specs=[ANY] * n,
        out_specs=tuple([ANY] * n),
        scratch_shapes=[pltpu.SemaphoreType.DMA((n,)), pltpu.SemaphoreType.DMA((n,))],
        name="pair_exchange",
    )(*gs)


def _chip_exchange(ps):
    n = len(ps)

    def body(*refs):
        p_refs, o_refs = refs[:n], refs[n:2 * n]
        send_sems, recv_sems = refs[2 * n:]
        x, y, c, chips = _place()
        me_chip = 2 * x + y
        cps = []
        for t in range(n):
            for j, chip in enumerate(chips):
                cp = pltpu.make_async_remote_copy(src_ref=p_refs[t].at[2 * chip[0] + chip[1]], dst_ref=o_refs[t].at[me_chip],
                                                  send_sem=send_sems.at[3 * t + j], recv_sem=recv_sems.at[3 * t + j],
                                                  device_id=(*chip, c), device_id_type=MESH)
                cp.start()
                cps.append(cp)
        for cp in cps:
            cp.wait()

    return pl.pallas_call(
        body,
        out_shape=tuple(jax.ShapeDtypeStruct(p.shape, p.dtype) for p in ps),
        in_specs=[ANY] * n,
        out_specs=tuple([ANY] * n),
        scratch_shapes=[pltpu.SemaphoreType.DMA((3 * n,)), pltpu.SemaphoreType.DMA((3 * n,))],
        name="chip_exchange",
    )(*ps)


def _pair_swap(rs):
    n = len(rs)

    def body(*refs):
        r_refs, o_refs = refs[:n], refs[n:2 * n]
        send_sems, recv_sems = refs[2 * n:]
        x, y, c, _ = _place()
        cps = []
        for t in range(n):
            cp = pltpu.make_async_remote_copy(src_ref=r_refs[t], dst_ref=o_refs[t], send_sem=send_sems.at[t],
                                              recv_sem=recv_sems.at[t], device_id=(x, y, 1 - c), device_id_type=MESH)
            cp.start()
            cps.append(cp)
        for cp in cps:
            cp.wait()

    return pl.pallas_call(
        body,
        out_shape=tuple(jax.ShapeDtypeStruct(r.shape, r.dtype) for r in rs),
        in_specs=[ANY] * n,
        out_specs=tuple([ANY] * n),
        scratch_shapes=[pltpu.SemaphoreType.DMA((n,)), pltpu.SemaphoreType.DMA((n,))],
        name="pair_swap",
    )(*rs)


N_DEV = 8
LOSS_ROW = 4


def _small_allreduce(small):
    def body(s_ref, o_ref, all_ref, send_sems, recv_sems):
        x, y, c, _ = _place()
        me = 4 * x + 2 * y + c
        all_ref[me] = s_ref[...]
        cps = []
        for k in range(1, N_DEV):
            peer = tuple(1 - p if (k >> s) & 1 else p for p, s in ((x, 2), (y, 1), (c, 0)))
            cp = pltpu.make_async_remote_copy(src_ref=s_ref, dst_ref=all_ref.at[me], send_sem=send_sems.at[k - 1],
                                              recv_sem=recv_sems.at[k - 1], device_id=peer, device_id_type=MESH)
            cp.start()
            cps.append(cp)
        for cp in cps:
            cp.wait()
        tot = all_ref[0]
        for d in range(1, N_DEV):
            tot = tot + all_ref[d]
        o_ref[...] = tot
        o_ref[LOSS_ROW:LOSS_ROW + 1, :] = jnp.broadcast_to(jnp.sum(tot[LOSS_ROW:LOSS_ROW + 1, :], axis=1, keepdims=True),
                                                          (1, tot.shape[1]))

    vm = pl.BlockSpec(memory_space=pltpu.VMEM)
    return pl.pallas_call(
        body,
        out_shape=jax.ShapeDtypeStruct(small.shape, small.dtype),
        in_specs=[vm],
        out_specs=vm,
        scratch_shapes=[pltpu.VMEM((N_DEV,) + small.shape, small.dtype), pltpu.SemaphoreType.DMA((N_DEV - 1,)),
                        pltpu.SemaphoreType.DMA((N_DEV - 1,))],
        name="small_allreduce",
    )(small)


def _sum_pair(g, recv, cidx, *, tr, name):
    _, hr, cols = recv.shape
    nr = hr // tr

    def body(c_ref, g_ref, r_ref, o_ref):
        o_ref[...] = (g_ref[...] + r_ref[...]).astype(o_ref.dtype)

    grid_spec = pltpu.PrefetchScalarGridSpec(
        num_scalar_prefetch=1,
        grid=(N_CHIPS, nr),
        in_specs=[pl.BlockSpec((None, tr, cols), lambda k, i, c_ref: (k, c_ref[0] * nr + i, 0)),
                  pl.BlockSpec((None, tr, cols), lambda k, i, c_ref: (k, i, 0))],
        out_specs=pl.BlockSpec((None, tr, cols), lambda k, i, c_ref: (k, i, 0)),
    )
    return pl.pallas_call(body, out_shape=jax.ShapeDtypeStruct(recv.shape, BF16), grid_spec=grid_spec,
                          compiler_params=_cparams(), name=name)(cidx, g, recv)


def _sum_chips(p, *, tr, name):
    _, rows, cols = p.shape

    def body(p_ref, o_ref):
        tot = p_ref[0].astype(F32)
        for k in range(1, N_CHIPS):
            tot = tot + p_ref[k].astype(F32)
        o_ref[...] = tot

    return pl.pallas_call(
        body,
        out_shape=jax.ShapeDtypeStruct((rows, cols), F32),
        grid=(rows // tr,),
        in_specs=[pl.BlockSpec((N_CHIPS, tr, cols), lambda i: (0, i, 0))],
        out_specs=pl.BlockSpec((tr, cols), lambda i: (i, 0)),
        compiler_params=_cparams(),
        name=name,
    )(p)


def _adamw(w, g, m, v, *, tr, name):
    rows, cols = w.shape
    bc1 = 1.0 / (1.0 - ADAM_B1 ** ADAM_STEP)
    bc2 = 1.0 / (1.0 - ADAM_B2 ** ADAM_STEP)

    def body(w_ref, g_ref, m_ref, v_ref, d_ref, nm_ref, nv_ref):
        gv = g_ref[...]
        nm = ADAM_B1 * m_ref[...] + (1.0 - ADAM_B1) * gv
        nv = ADAM_B2 * v_ref[...] + (1.0 - ADAM_B2) * (gv * gv)
        d_ref[...] = -ADAM_LR * ((nm * bc1) / (jnp.sqrt(nv * bc2) + ADAM_EPS) + ADAM_WD * w_ref[...])
        nm_ref[...] = nm
        nv_ref[...] = nv

    spec = pl.BlockSpec((tr, cols), lambda i: (i, 0))
    sd = jax.ShapeDtypeStruct((rows, cols), F32)
    return pl.pallas_call(body, out_shape=(sd, sd, sd), grid=(rows // tr,), in_specs=[spec] * 4, out_specs=(spec,) * 3,
                          compiler_params=_cparams(), name=name)(w, g, m, v)


def _pack_small(norm, mem_norm, final_norm, b_forget):
    rows = [norm.reshape(1, D_MODEL), mem_norm.reshape(1, D_MODEL), final_norm.reshape(1, D_MODEL),
            jnp.pad(b_forget.reshape(1, FOX_HEADS), ((0, 0), (0, D_MODEL - FOX_HEADS))), jnp.zeros((4, D_MODEL), F32)]
    return jnp.concatenate(rows, axis=0)


def _unpack_small(a):
    return a[0:1], a[3:4, :FOX_HEADS], a[1:2], a[2]


def kernel(x, mem, norm_g, w_in, b_forget, mem_norm_g, w_mem_kv, w_out, final_norm_g, loss_target, m_norm_g, m_w_in, m_b_forget, m_mem_norm_g, m_w_mem_kv, m_w_out, m_final_norm_g, v_norm_g, v_w_in, v_b_forget, v_mem_norm_g, v_w_mem_kv, v_w_out, v_final_norm_g):
    core = lax.axis_index("c").astype(jnp.int32)
    me_chip = (2 * lax.axis_index("x") + lax.axis_index("y")).astype(jnp.int32)
    cidx = core.reshape(1)

    def own_slot(arr, own):
        return lax.dynamic_update_slice(arr, own[None].astype(arr.dtype), (me_chip,) + (0,) * own.ndim)

    mine = [w_in[0].astype(BF16), w_mem_kv[0].astype(BF16), w_out[0].astype(BF16)]
    g_in, g_kv, g_out = (own_slot(g, s) for g, s in zip(_gather_weights(mine), mine))
    w_r = _rearrange_w_in([g_in[k] for k in range(N_CHIPS)])
    w_kv = g_kv.reshape(D_MODEL, 2 * MEM_W)
    w_o = g_out.reshape(MIX_W, D_MODEL)

    gx, g_wr, g_wkv, g_wo, small = _local_grads(x, mem, norm_g, w_r, b_forget, mem_norm_g, w_kv, w_o, final_norm_g, loss_target)

    slabs = [_w_in_grad_slabs(g_wr),
             g_wkv.reshape(N_CHIPS, D_MODEL // N_CHIPS, 2 * MEM_W),
             g_wo.reshape(N_CHIPS, MIX_W // N_CHIPS, D_MODEL)]
    trs = (128, 128, 256)
    names = ("w_in", "w_mem_kv", "w_out")
    recv = _pair_exchange(slabs)
    pair = [_sum_pair(g, r, cidx, tr=tr, name=f"sum_pair_{nm}") for g, r, tr, nm in zip(slabs, recv, trs, names)]
    got = [lax.dynamic_update_slice(g, lax.dynamic_slice(p, (me_chip, 0, 0), (1,) + p.shape[1:]), (me_chip, 0, 0))
           for g, p in zip(_chip_exchange(pair), pair)]
    red = [_sum_chips(p, tr=tr, name=f"sum_chips_{nm}") for p, tr, nm in zip(got, trs, names)]
    sib = _pair_swap(red)
    grads = [jnp.where(core == 0, jnp.concatenate([r, s], axis=0), jnp.concatenate([s, r], axis=0)) for r, s in zip(red, sib)]

    outs = {}
    for nm, g, w, m, v, tr in zip(names, grads, (w_in, w_mem_kv, w_out), (m_w_in, m_w_mem_kv, m_w_out),
                                  (v_w_in, v_w_mem_kv, v_w_out), trs):
        d, nmo, nvo = _adamw(w[0], g, m[0], v[0], tr=tr, name=f"adamw_{nm}")
        outs[nm] = tuple(a[None] for a in (g, d, nmo, nvo))

    gsum = _small_allreduce(small)
    sd, sm, sv = _adamw(_pack_small(norm_g, mem_norm_g, final_norm_g, b_forget), gsum,
                        _pack_small(m_norm_g, m_mem_norm_g, m_final_norm_g, m_b_forget),
                        _pack_small(v_norm_g, v_mem_norm_g, v_final_norm_g, v_b_forget), tr=8, name="adamw_small")
    loss = gsum[LOSS_ROW, 0]

    def group(i, small_arr):
        ng, bf, mg, fg = _unpack_small(small_arr)
        return (ng, outs["w_in"][i], bf, mg, outs["w_mem_kv"][i], outs["w_out"][i], fg)

    return (loss, gx, *group(0, gsum), *group(1, sd), *group(2, sm), *group(3, sv))
```

```python
import functools
import math

import jax
import jax.numpy as jnp
from jax import lax
from jax.experimental import pallas as pl
from jax.experimental.pallas import tpu as pltpu

F32 = jnp.float32
BF16 = jnp.bfloat16

D_MODEL = 1024
SEQ = 2048
HEAD_DIM = 64
FOX_HEADS = 12
DIL_HEADS = 12
MEM_HEADS = 4
MEM_HEAD_DIM = 128
MEM_LEN = 256
FOX_W = FOX_HEADS * HEAD_DIM
DIL_W = DIL_HEADS * HEAD_DIM
MEM_W = MEM_HEADS * MEM_HEAD_DIM
MIX_W = FOX_W + DIL_W + MEM_W
DILATIONS = ((128, 1), (512, 4), (2048, 16))
ROPE_THETA = 500000.0
ROPE_DIM = HEAD_DIM // 4
RMS_EPS = 1e-6
NEG_INF = -1e30
IN_SIZES = [FOX_W] * 4 + [FOX_HEADS] + [DIL_W] * 4 + [MEM_W] * 2
IN_W = sum(IN_SIZES)

ADAM_LR = 0.001
ADAM_B1 = 0.9
ADAM_B2 = 0.999
ADAM_EPS = 1e-08
ADAM_WD = 0.01
ADAM_STEP = 10

LANES = 128
N_CHIPS = 4
PW = 7168
PWF = PW + LANES
C_FQ, C_FK, C_FV, C_FG = 0, 768, 1536, 2304
C_DQ, C_DK, C_DV, C_DG = 3072, 3840, 4608, 5376
C_MQ, C_MG = 6144, 6656
VMEM_LIMIT = 48 * 1024 * 1024


def _cparams(**kw):
    return pltpu.CompilerParams(vmem_limit_bytes=VMEM_LIMIT, **kw)


def _matmul(a, b, *, out_dtype, tm, tn, tk, name, mode="nn"):
    if mode == "tn":
        (kdim, m), n = a.shape, b.shape[1]
        a_spec = pl.BlockSpec((tk, tm), lambda i, j, k: (k, i))
        b_spec = pl.BlockSpec((tk, tn), lambda i, j, k: (k, j))
        dims = _T0
    elif mode == "nt":
        (m, kdim), n = a.shape, b.shape[0]
        a_spec = pl.BlockSpec((tm, tk), lambda i, j, k: (i, k))
        b_spec = pl.BlockSpec((tn, tk), lambda i, j, k: (j, k))
        dims = _NT
    else:
        (m, kdim), n = a.shape, b.shape[1]
        a_spec = pl.BlockSpec((tm, tk), lambda i, j, k: (i, k))
        b_spec = pl.BlockSpec((tk, tn), lambda i, j, k: (k, j))
        dims = (((1,), (0,)), ((), ()))
    nk = kdim // tk
    assert m % tm == 0 and n % tn == 0 and kdim % tk == 0

    def body(a_ref, b_ref, o_ref, acc_ref):
        k = pl.program_id(2)

        @pl.when(k == 0)
        def _():
            acc_ref[...] = jnp.zeros_like(acc_ref)

        acc_ref[...] += lax.dot_general(a_ref[...], b_ref[...], dims, preferred_element_type=F32)

        @pl.when(k == nk - 1)
        def _():
            o_ref[...] = acc_ref[...].astype(o_ref.dtype)

    return pl.pallas_call(
        body,
        out_shape=jax.ShapeDtypeStruct((m, n), out_dtype),
        grid=(m // tm, n // tn, nk),
        in_specs=[a_spec, b_spec],
        out_specs=pl.BlockSpec((tm, tn), lambda i, j, k: (i, j)),
        scratch_shapes=[pltpu.VMEM((tm, tn), F32)],
        compiler_params=_cparams(dimension_semantics=("parallel", "parallel", "arbitrary")),
        name=name,
    )(a, b)


def _rms_fwd(x, g, *, tm, name):
    t, d = x.shape

    def body(x_ref, g_ref, h_ref):
        xv = x_ref[...]
        r = lax.rsqrt(jnp.mean(xv * xv, axis=-1, keepdims=True) + RMS_EPS)
        h_ref[...] = (xv * r * g_ref[...]).astype(h_ref.dtype)

    return pl.pallas_call(
        body,
        out_shape=jax.ShapeDtypeStruct((t, d), BF16),
        grid=(t // tm,),
        in_specs=[pl.BlockSpec((tm, d), lambda i: (i, 0)), pl.BlockSpec((1, d), lambda i: (0, 0))],
        out_specs=pl.BlockSpec((tm, d), lambda i: (i, 0)),
        compiler_params=_cparams(),
        name=name,
    )(x, g)


def _rope_tables():
    half = ROPE_DIM // 2
    pos = jnp.arange(SEQ, dtype=F32)
    inv_freq = 1.0 / (ROPE_THETA ** (jnp.arange(0, ROPE_DIM, 2, dtype=F32) / ROPE_DIM))
    ang = pos[:, None] * inv_freq[None, :]
    cos, sin = jnp.cos(ang), jnp.sin(ang)
    one = jnp.ones((SEQ, HEAD_DIM - ROPE_DIM), F32)
    zero = jnp.zeros((SEQ, HEAD_DIM - ROPE_DIM), F32)
    zh = jnp.zeros((SEQ, half), F32)
    c = jnp.concatenate([cos, cos, one], axis=1)
    s1 = jnp.concatenate([zh, sin, zero], axis=1)
    s2 = jnp.concatenate([-sin, zh, zero], axis=1)
    rep = LANES // HEAD_DIM
    return jnp.tile(c, (1, rep)), jnp.tile(s1, (1, rep)), jnp.tile(s2, (1, rep))


def _rope_apply(t, c, s1, s2, transpose=False):
    n = t.shape[-1]
    rep = n // LANES
    c, s1, s2 = (jnp.tile(u, (1, rep)) for u in (c, s1, s2))
    half = ROPE_DIM // 2
    if not transpose:
        return t * c + pltpu.roll(t, half, 1) * s1 + pltpu.roll(t, n - half, 1) * s2
    return t * c + pltpu.roll(t * s1, n - half, 1) + pltpu.roll(t * s2, half, 1)


def _proj(h, w, tabs, *, n, tm, tn, name):
    t, d = h.shape
    assert C_DQ % tn == 0 and (C_DV - C_DQ) % tn == 0 and (C_DG - C_DQ) % tn == 0
    rope_lo, rope_hi, dil_hi = C_DQ // tn, C_DV // tn, C_DG // tn
    s_blocks = SEQ // tm

    def body(h_ref, w_ref, c_ref, s1_ref, s2_ref, o_ref, f_ref):
        j = pl.program_id(1)
        acc = jnp.dot(h_ref[...], w_ref[...], preferred_element_type=F32)
        is_rope = jnp.logical_and(j >= rope_lo, j < rope_hi)

        @pl.when(is_rope)
        def _():
            r = _rope_apply(acc, c_ref[...], s1_ref[...], s2_ref[...])
            o_ref[...] = r.astype(o_ref.dtype)
            f_ref[...] = r

        @pl.when(jnp.logical_not(is_rope))
        def _():
            o_ref[...] = acc.astype(o_ref.dtype)

        @pl.when(jnp.logical_and(j >= rope_hi, j < dil_hi))
        def _():
            f_ref[...] = acc

    tab_spec = pl.BlockSpec((tm, LANES), lambda i, j: (i % s_blocks, 0))
    f_spec = pl.BlockSpec((tm, tn), lambda i, j: (i, jnp.clip(j - rope_lo, 0, dil_hi - rope_lo - 1)))
    return pl.pallas_call(
        body,
        out_shape=(jax.ShapeDtypeStruct((t, n), BF16), jax.ShapeDtypeStruct((t, 3 * DIL_W), F32)),
        grid=(t // tm, n // tn),
        in_specs=[pl.BlockSpec((tm, d), lambda i, j: (i, 0)), pl.BlockSpec((d, tn), lambda i, j: (0, j)),
                  tab_spec, tab_spec, tab_spec],
        out_specs=(pl.BlockSpec((tm, tn), lambda i, j: (i, j)), f_spec),
        compiler_params=_cparams(dimension_semantics=("parallel", "arbitrary")),
        name=name,
    )(h, w, *tabs)


def _split3(x):
    hi = x.astype(BF16)
    r1 = x - hi.astype(F32)
    mid = r1.astype(BF16)
    lo = (r1 - mid.astype(F32)).astype(BF16)
    return hi, mid, lo


def _dot3(sel, x, sel_is_lhs):
    out = None
    for piece in _split3(x):
        t = jnp.dot(sel, piece, preferred_element_type=F32) if sel_is_lhs else jnp.dot(piece, sel, preferred_element_type=F32)
        out = t if out is None else out + t
    return out


def _flog_fwd(flog, bpad, *, nb, ts, name):
    ns = SEQ // ts

    def body(f_ref, b_ref, c_ref, carry_ref):
        s = pl.program_id(1)

        @pl.when(s == 0)
        def _():
            carry_ref[...] = jnp.zeros_like(carry_ref)

        z = f_ref[...] + b_ref[...]
        logf = jnp.minimum(z, 0.0) - jnp.log(1.0 + jnp.exp(-jnp.abs(z)))
        r = lax.broadcasted_iota(jnp.int32, (ts, ts), 0)
        c = lax.broadcasted_iota(jnp.int32, (ts, ts), 1)
        tri = jnp.where(r >= c, 1.0, 0.0).astype(BF16)
        cs = _dot3(tri, logf, True) + carry_ref[0:1, :]
        carry_ref[...] = jnp.broadcast_to(cs[ts - 1:ts, :], carry_ref.shape)
        c_ref[...] = cs

    return pl.pallas_call(
        body,
        out_shape=jax.ShapeDtypeStruct((nb * SEQ, LANES), F32),
        grid=(nb, ns),
        in_specs=[pl.BlockSpec((ts, LANES), lambda b, s: (b * ns + s, 0)), pl.BlockSpec((1, LANES), lambda b, s: (0, 0))],
        out_specs=pl.BlockSpec((ts, LANES), lambda b, s: (b * ns + s, 0)),
        scratch_shapes=[pltpu.VMEM((8, LANES), F32)],
        compiler_params=_cparams(dimension_semantics=("parallel", "arbitrary")),
        name=name,
    )(flog, bpad)


def _flog_bwd(dcol, flog, bpad, *, nb, ts, name):
    ns = SEQ // ts

    def body(d_ref, f_ref, b_ref, o_ref, gb_ref, carry_ref):
        bi = pl.program_id(0)
        s = pl.program_id(1)

        @pl.when(s == 0)
        def _():
            carry_ref[...] = jnp.zeros_like(carry_ref)

        @pl.when(jnp.logical_and(bi == 0, s == 0))
        def _():
            gb_ref[...] = jnp.zeros_like(gb_ref)

        r = lax.broadcasted_iota(jnp.int32, (ts, ts), 0)
        c = lax.broadcasted_iota(jnp.int32, (ts, ts), 1)
        tri = jnp.where(r <= c, 1.0, 0.0).astype(BF16)
        rc = _dot3(tri, d_ref[...], True) + carry_ref[0:1, :]
        carry_ref[...] = jnp.broadcast_to(rc[0:1, :], carry_ref.shape)
        z = f_ref[...] + b_ref[...]
        dz = rc / (1.0 + jnp.exp(z))
        o_ref[...] = dz.astype(o_ref.dtype)
        gb_ref[...] += jnp.broadcast_to(jnp.sum(dz, axis=0, keepdims=True), gb_ref.shape)

    rev = lambda b, s: (b * ns + (ns - 1 - s), 0)
    return pl.pallas_call(
        body,
        out_shape=(jax.ShapeDtypeStruct((nb * SEQ, LANES), BF16), jax.ShapeDtypeStruct((8, LANES), F32)),
        grid=(nb, ns),
        in_specs=[pl.BlockSpec((ts, LANES), rev), pl.BlockSpec((ts, LANES), rev), pl.BlockSpec((1, LANES), lambda b, s: (0, 0))],
        out_specs=(pl.BlockSpec((ts, LANES), rev), pl.BlockSpec((8, LANES), lambda b, s: (0, 0))),
        scratch_shapes=[pltpu.VMEM((8, LANES), F32)],
        compiler_params=_cparams(dimension_semantics=("arbitrary", "arbitrary")),
        name=name,
    )(dcol, flog, bpad)


class _AttnCfg:
    def __init__(self, *, e, tq, tk, lq, lk, causal, window, ncol, qcol, kcol, vcol, split_p=False):
        self.e, self.tq, self.tk, self.lq, self.lk = e, tq, tk, lq, lk
        self.split_p = split_p
        self.causal, self.window = causal, window
        self.ncol, self.qcol, self.kcol, self.vcol = ncol, qcol, kcol, vcol
        self.nh = LANES // e
        self.scale = 1.0 / math.sqrt(e)
        self.nq, self.nk = lq // tq, lk // tk

    def k_range(self, i):
        if not self.causal:
            return 0, self.nk
        hi = ((i + 1) * self.tq - 1) // self.tk + 1
        if self.window is None:
            return 0, hi
        return jnp.maximum((i * self.tq - self.window) // self.tk, 0), hi


def _head_masks(nh):
    lane = lax.broadcasted_iota(jnp.int32, (1, LANES), 1)
    return [None] if nh == 1 else [lane < HEAD_DIM, lane >= HEAD_DIM]


def _sel(mask, a, b):
    return a if mask is None else jnp.where(mask, a, b)


def _scores(cfg, qh, kb, q0, k0, dlt0, bias):
    s = lax.dot_general(qh, kb, (((1,), (1,)), ((), ())), preferred_element_type=F32) * cfg.scale
    if bias is not None:
        s = s + bias
    if cfg.causal:
        d = dlt0 + (q0 - k0)
        if cfg.window is None:
            ok = d >= 0
        else:
            ok = d.astype(jnp.uint32) <= jnp.uint32(cfg.window)
        s = jnp.where(ok, s, NEG_INF)
    return s


def _attn_fwd(cfg, q, k, v, *, out_cols, bias=None, state=None, finalize=True, name):
    g = q.shape[0]
    tq, tk, e, nh = cfg.tq, cfg.tk, cfg.e, cfg.nh

    def body(*refs):
        refs = list(refs)
        q_ref, k_ref, v_ref = refs[:3]
        del refs[:3]
        if bias is not None:
            cb_ref, cr_ref = refs[:2]
            del refs[:2]
        if state is not None:
            ai_ref, mi_ref, li_ref = refs[:3]
            del refs[:3]
        out_refs = refs
        masks = _head_masks(nh)
        dlt0 = lax.broadcasted_iota(jnp.int32, (tq, tk), 0) - lax.broadcasted_iota(jnp.int32, (tq, tk), 1)

        def qbody(i, carry):
            q0 = pl.multiple_of(i * tq, tq)
            rows = pl.ds(q0, tq)
            qb = q_ref[rows, :]
            lo, hi = cfg.k_range(i)
            res = []
            for h in range(nh):
                qh = _sel(masks[h], qb, jnp.zeros_like(qb))
                if state is not None:
                    m0 = mi_ref[rows, h * e:h * e + 1]
                    l0 = li_ref[rows, h * e:h * e + 1]
                    a0 = ai_ref[rows, :]
                else:
                    m0 = jnp.full((tq, 1), NEG_INF, F32)
                    l0 = jnp.zeros((tq, 1), F32)
                    a0 = jnp.zeros((tq, LANES), F32)
                cq = cb_ref[rows, h * e:h * e + 1] if bias is not None else None

                def kbody(jk, c, qh=qh, cq=cq, h=h):
                    m, l, a = c
                    k0 = pl.multiple_of(jk * tk, tk)
                    kb = k_ref[pl.ds(k0, tk), :]
                    vb = v_ref[pl.ds(k0, tk), :]
                    b = (cq - cr_ref[jk, h:h + 1, :]) if bias is not None else None
                    s = _scores(cfg, qh, kb, q0, k0, dlt0, b)
                    m_new = jnp.maximum(m, jnp.max(s, axis=1, keepdims=True))
                    alpha = jnp.exp(m - m_new)
                    p = jnp.exp(s - m_new)
                    l = alpha * l + jnp.sum(p, axis=1, keepdims=True)
                    pb = p.astype(BF16)
                    pv = jnp.dot(pb, vb, preferred_element_type=F32)
                    if cfg.split_p:
                        pv = pv + jnp.dot((p - pb.astype(F32)).astype(BF16), vb, preferred_element_type=F32)
                    a = alpha * a + pv
                    return m_new, l, a

                res.append(lax.fori_loop(lo, hi, kbody, (m0, l0, a0)))
            if nh == 1:
                m, l, a = res[0]
                m, l = jnp.broadcast_to(m, (tq, LANES)), jnp.broadcast_to(l, (tq, LANES))
            else:
                m = jnp.where(masks[0], res[0][0], res[1][0])
                l = jnp.where(masks[0], res[0][1], res[1][1])
                a = jnp.where(masks[0], res[0][2], res[1][2])
            if finalize:
                out_refs[0][rows, :] = a / l
                out_refs[1][rows, :] = m + jnp.log(l)
            else:
                out_refs[0][rows, :] = a
                out_refs[1][rows, :] = m
                out_refs[2][rows, :] = l
            return carry

        lax.fori_loop(0, cfg.nq, qbody, 0)

    qspec = pl.BlockSpec((None, cfg.lq, LANES), lambda b, j: (b, 0, cfg.qcol(j)))
    kspec = pl.BlockSpec((None, cfg.lk, LANES), lambda b, j: (b, 0, cfg.kcol(j)))
    vspec = pl.BlockSpec((None, cfg.lk, LANES), lambda b, j: (b, 0, cfg.vcol(j)))
    ospec = pl.BlockSpec((None, cfg.lq, LANES), lambda b, j: (b, 0, j))
    args, in_specs = [q, k, v], [qspec, kspec, vspec]
    if bias is not None:
        args += list(bias)
        in_specs += [ospec, pl.BlockSpec((None, None, cfg.nk, 8, tk), lambda b, j: (b, j, 0, 0, 0))]
    aliases = {}
    if state is not None:
        aliases = {len(args) + t: t for t in range(3 if not finalize else 2)}
        args += list(state)
        in_specs += [ospec] * 3
    n_out = 2 if finalize else 3
    osd = jax.ShapeDtypeStruct((g, cfg.lq, out_cols), F32)
    return pl.pallas_call(
        body,
        out_shape=(osd,) * n_out,
        grid=(g, cfg.ncol),
        in_specs=in_specs,
        out_specs=(ospec,) * n_out,
        input_output_aliases=aliases,
        compiler_params=_cparams(dimension_semantics=("parallel", "parallel")),
        name=name,
    )(*args)


def _attn_bwd(cfg, q, k, v, do, o, lse, *, out_cols, kv_cols, bias=None, acc=None, do_off=0, name):
    g = q.shape[0]
    tq, tk, e, nh = cfg.tq, cfg.tk, cfg.e, cfg.nh
    t0 = (((0,), (0,)), ((), ()))

    def body(*refs):
        refs = list(refs)
        q_ref, k_ref, v_ref, do_ref, o_ref, lse_ref = refs[:6]
        del refs[:6]
        if bias is not None:
            cb_ref, cr_ref = refs[:2]
            del refs[:2]
        if acc is not None:
            dqi_ref, dki_ref, dvi_ref = refs[:3]
            del refs[:3]
        dq_ref, dk_ref, dv_ref = refs[:3]
        dcr_ref = refs[3] if bias is not None else None
        masks = _head_masks(nh)
        dlt0 = lax.broadcasted_iota(jnp.int32, (tq, tk), 0) - lax.broadcasted_iota(jnp.int32, (tq, tk), 1)
        if acc is not None:
            dq_ref[...] = dqi_ref[...]
            dk_ref[...] = dki_ref[...]
            dv_ref[...] = dvi_ref[...]
        else:
            dq_ref[...] = jnp.zeros_like(dq_ref)
            dk_ref[...] = jnp.zeros_like(dk_ref)
            dv_ref[...] = jnp.zeros_like(dv_ref)
        if dcr_ref is not None:
            dcr_ref[...] = jnp.zeros_like(dcr_ref)

        def qbody(i, carry):
            q0 = pl.multiple_of(i * tq, tq)
            rows = pl.ds(q0, tq)
            qb = q_ref[rows, :]
            dob = do_ref[rows, :].astype(BF16)
            prod = dob.astype(F32) * o_ref[rows, :]
            lo, hi = cfg.k_range(i)
            dqs = []
            for h in range(nh):
                qh = _sel(masks[h], qb, jnp.zeros_like(qb))
                doh = _sel(masks[h], dob, jnp.zeros_like(dob))
                lse_h = lse_ref[rows, h * e:h * e + 1]
                delta = jnp.sum(_sel(masks[h], prod, jnp.zeros_like(prod)), axis=1, keepdims=True)
                cq = cb_ref[rows, h * e:h * e + 1] if bias is not None else None

                def kbody(jk, dq_acc, qh=qh, doh=doh, lse_h=lse_h, delta=delta, cq=cq, h=h):
                    k0 = pl.multiple_of(jk * tk, tk)
                    krows = pl.ds(k0, tk)
                    kb = k_ref[krows, :]
                    vb = v_ref[krows, :]
                    b = (cq - cr_ref[jk, h:h + 1, :]) if bias is not None else None
                    s = _scores(cfg, qh, kb, q0, k0, dlt0, b)
                    p = jnp.exp(s - lse_h)
                    dp = lax.dot_general(doh, vb, (((1,), (1,)), ((), ())), preferred_element_type=F32)
                    ds = p * (dp - delta)
                    if dcr_ref is not None:
                        dcr_ref[jk, h:h + 1, :] += jnp.sum(ds, axis=0, keepdims=True)
                    dsb = (ds * cfg.scale).astype(BF16)
                    dv_ref[krows, :] += lax.dot_general(p.astype(BF16), doh, t0, preferred_element_type=F32)
                    dk_ref[krows, :] += lax.dot_general(dsb, qh, t0, preferred_element_type=F32)
                    return dq_acc + jnp.dot(dsb, kb, preferred_element_type=F32)

                dqs.append(lax.fori_loop(lo, hi, kbody, jnp.zeros((tq, LANES), F32)))
            dq = dqs[0] if nh == 1 else jnp.where(masks[0], dqs[0], dqs[1])
            dq_ref[rows, :] += dq
            return carry

        lax.fori_loop(0, cfg.nq, qbody, 0)

    qspec = pl.BlockSpec((None, cfg.lq, LANES), lambda b, j: (b, 0, cfg.qcol(j)))
    kspec = pl.BlockSpec((None, cfg.lk, LANES), lambda b, j: (b, 0, cfg.kcol(j)))
    vspec = pl.BlockSpec((None, cfg.lk, LANES), lambda b, j: (b, 0, cfg.vcol(j)))
    ospec = pl.BlockSpec((None, cfg.lq, LANES), lambda b, j: (b, 0, j))
    kvspec = pl.BlockSpec((None, cfg.lk, LANES), lambda b, j: (b, 0, j))
    dospec = pl.BlockSpec((None, cfg.lq, LANES), lambda b, j: (b, 0, do_off + j))
    args, in_specs = [q, k, v, do, o, lse], [qspec, kspec, vspec, dospec, ospec, ospec]
    out_shape = [jax.ShapeDtypeStruct((g, cfg.lq, out_cols), F32), jax.ShapeDtypeStruct((g, cfg.lk, kv_cols), F32),
                 jax.ShapeDtypeStruct((g, cfg.lk, kv_cols), F32)]
    out_specs = [ospec, kvspec, kvspec]
    if bias is not None:
        args += list(bias)
        crspec = pl.BlockSpec((None, None, cfg.nk, 8, tk), lambda b, j: (b, j, 0, 0, 0))
        in_specs += [ospec, crspec]
        out_shape.append(jax.ShapeDtypeStruct((g, cfg.ncol, cfg.nk, 8, tk), F32))
        out_specs.append(crspec)
    aliases = {}
    if acc is not None:
        aliases = {len(args) + t: t for t in range(3)}
        args += list(acc)
        in_specs += [ospec, kvspec, kvspec]
    return pl.pallas_call(
        body,
        out_shape=tuple(out_shape),
        grid=(g, cfg.ncol),
        in_specs=in_specs,
        out_specs=tuple(out_specs),
        input_output_aliases=aliases,
        compiler_params=_cparams(dimension_semantics=("parallel", "parallel")),
        name=name,
    )(*args)


BLK = 128
NBLK = SEQ // BLK
QK_SCALE = 1.0 / math.sqrt(HEAD_DIM)
DIL_STEPS = tuple(d for _, d in DILATIONS)
assert all(w // d == BLK for w, d in DILATIONS)
_T0 = (((0,), (0,)), ((), ()))
_NT = (((1,), (1,)), ((), ()))


def _stack_heads(a, masks):
    z = jnp.zeros_like(a)
    return jnp.concatenate([jnp.where(masks[0], a, z), jnp.where(masks[1], a, z)], axis=0)


def _tri_bias(lower):
    r = lax.broadcasted_iota(jnp.int32, (BLK, BLK), 0)
    c = lax.broadcasted_iota(jnp.int32, (BLK, BLK), 1)
    return jnp.where((c <= r) if lower else (c >= r), 0.0, NEG_INF).astype(F32)


def _dil_rows(r, i, d):
    start = r + i * (BLK * d)
    return pl.ds(start, BLK) if d == 1 else pl.ds(start, BLK, stride=d)


DIL_SET = 4


def _dil_sets(d, fn):
    nbk = SEQ // d // BLK
    if d == 1:
        def gbody(g, c):
            fn([(0, DIL_SET * g + a, None if a == 0 else True) for a in range(DIL_SET)])
            return c
        lax.fori_loop(0, nbk // DIL_SET, gbody, 0)
    elif nbk > 1:
        assert nbk == DIL_SET
        def rbody(r, c):
            fn([(r, i, i > 0) for i in range(nbk)])
            return c
        lax.fori_loop(0, d, rbody, 0)
    else:
        def rbody(rr, c):
            fn([(DIL_SET * rr + a, 0, False) for a in range(DIL_SET)])
            return c
        lax.fori_loop(0, d // DIL_SET, rbody, 0)


def _dil_key_tiles(r, i, d, has_prev, qrows, tri_cur, tri_prev):
    tiles = [(qrows, tri_cur)]
    if has_prev is None:
        tiles.append((_dil_rows(r, jnp.maximum(i - 1, 0), d), tri_prev + jnp.where(i > 0, 0.0, NEG_INF)))
    elif has_prev:
        tiles.append((_dil_rows(r, i - 1, d), tri_prev))
    return tiles


def _dil_fwd(qkv, *, name):
    nb = qkv.shape[0]
    ncol = DIL_W // LANES
    hd = HEAD_DIM

    def body(q_ref, k_ref, v_ref, o_ref, lse_ref, m_ref, l_ref, a_ref):
        masks = _head_masks(2)
        tri_cur, tri_prev = _tri_bias(True), _tri_bias(False)
        for pi, d in enumerate(DIL_STEPS):
            first, last = pi == 0, pi == len(DIL_STEPS) - 1

            def qset(blocks, d=d, first=first, last=last):
                work = []
                for r, i, has_prev in blocks:
                    qrows = _dil_rows(r, i, d)
                    qcat = _stack_heads((q_ref[qrows, :] * QK_SCALE).astype(BF16), masks)
                    ss, krs = [], []
                    for krows, bias in _dil_key_tiles(r, i, d, has_prev, qrows, tri_cur, tri_prev):
                        s = lax.dot_general(qcat, k_ref[krows, :].astype(BF16), _NT, preferred_element_type=F32)
                        ss.append((s[:BLK] + bias, s[BLK:] + bias))
                        krs.append(krows)
                    work.append((qrows, ss, krs))
                for qrows, ss, krs in work:
                    e0 = ss[0][0] if len(ss) == 1 else jnp.maximum(ss[0][0], ss[1][0])
                    e1 = ss[0][1] if len(ss) == 1 else jnp.maximum(ss[0][1], ss[1][1])
                    n0 = jnp.max(e0, axis=1, keepdims=True)
                    n1 = jnp.max(e1, axis=1, keepdims=True)
                    if not first:
                        mo, lo = m_ref[qrows, :], l_ref[qrows, :]
                        m0, m1 = mo[:, 0:1], mo[:, hd:hd + 1]
                        n0, n1 = jnp.maximum(n0, m0), jnp.maximum(n1, m1)
                        a0, a1 = jnp.exp(m0 - n0), jnp.exp(m1 - n1)
                    ps = [(jnp.exp(s0 - n0), jnp.exp(s1 - n1)) for s0, s1 in ss]
                    t0 = ps[0][0] if len(ps) == 1 else ps[0][0] + ps[1][0]
                    t1 = ps[0][1] if len(ps) == 1 else ps[0][1] + ps[1][1]
                    l0 = jnp.sum(t0, axis=1, keepdims=True)
                    l1 = jnp.sum(t1, axis=1, keepdims=True)
                    acc = None
                    for (p0, p1), krows in zip(ps, krs):
                        vcat = _stack_heads(v_ref[krows, :].astype(BF16), masks)
                        pv = jnp.dot(jnp.concatenate([p0, p1], axis=1).astype(BF16), vcat, preferred_element_type=F32)
                        acc = pv if acc is None else acc + pv
                    if not first:
                        l0 = l0 + a0 * lo[:, 0:1]
                        l1 = l1 + a1 * lo[:, hd:hd + 1]
                        acc = acc + a_ref[qrows, :] * jnp.where(masks[0], a0, a1)
                    if last:
                        o_ref[qrows, :] = acc / jnp.where(masks[0], l0, l1)
                        lse_ref[qrows, :] = jnp.where(masks[0], n0 + jnp.log(l0), n1 + jnp.log(l1))
                    else:
                        m_ref[qrows, :] = jnp.where(masks[0], n0, n1)
                        l_ref[qrows, :] = jnp.where(masks[0], l0, l1)
                        a_ref[qrows, :] = acc

            _dil_sets(d, qset)

    spec = lambda off: pl.BlockSpec((None, SEQ, LANES), lambda b, j: (b, 0, off + j))
    ospec = pl.BlockSpec((None, SEQ, LANES), lambda b, j: (b, 0, j))
    osd = jax.ShapeDtypeStruct((nb, SEQ, DIL_W), F32)
    return pl.pallas_call(
        body, out_shape=(osd, osd), grid=(nb, ncol),
        in_specs=[spec(0), spec(ncol), spec(2 * ncol)], out_specs=(ospec, ospec),
        scratch_shapes=[pltpu.VMEM((SEQ, LANES), F32)] * 3,
        compiler_params=_cparams(dimension_semantics=("parallel", "parallel")), name=name,
    )(qkv, qkv, qkv)


def _dil_bwd(qkv, do, o, lse, tabs, *, do_off, name):
    nb = qkv.shape[0]
    ncol = DIL_W // LANES
    hd = HEAD_DIM

    def body(q_ref, k_ref, v_ref, do_ref, o_ref, lse_ref, c_ref, s1_ref, s2_ref, dqo_ref, dko_ref, dvo_ref,
             dq_ref, dk_ref, dv_ref, dl_ref, dof_ref):
        masks = _head_masks(2)
        tri_cur, tri_prev = _tri_bias(True), _tri_bias(False)
        dq_ref[...] = jnp.zeros_like(dq_ref)
        dk_ref[...] = jnp.zeros_like(dk_ref)
        dv_ref[...] = jnp.zeros_like(dv_ref)

        def delta_body(i, c):
            rows = pl.ds(pl.multiple_of(i * BLK, BLK), BLK)
            dof = do_ref[rows, :].astype(F32)
            dof_ref[rows, :] = dof
            prod = dof * o_ref[rows, :]
            z = jnp.zeros_like(prod)
            dl_ref[rows, :] = jnp.where(masks[0], jnp.sum(jnp.where(masks[0], prod, z), axis=1, keepdims=True),
                                        jnp.sum(jnp.where(masks[1], prod, z), axis=1, keepdims=True))
            return c

        lax.fori_loop(0, NBLK, delta_body, 0)

        for d in DIL_STEPS:
            def qset(blocks, d=d):
                work = []
                for r, i, has_prev in blocks:
                    qrows = _dil_rows(r, i, d)
                    qcat = _stack_heads((q_ref[qrows, :] * QK_SCALE).astype(BF16), masks)
                    docat = _stack_heads(dof_ref[qrows, :].astype(BF16), masks)
                    tiles = []
                    for krows, bias in _dil_key_tiles(r, i, d, has_prev, qrows, tri_cur, tri_prev):
                        s = lax.dot_general(qcat, k_ref[krows, :].astype(BF16), _NT, preferred_element_type=F32)
                        dp = lax.dot_general(docat, v_ref[krows, :].astype(BF16), _NT, preferred_element_type=F32)
                        tiles.append((krows, s, dp, bias))
                    work.append((qrows, qcat, docat, tiles))
                for qrows, qcat, docat, tiles in work:
                    lseb, dlb = lse_ref[qrows, :], dl_ref[qrows, :]
                    lse0, lse1 = lseb[:, 0:1], lseb[:, hd:hd + 1]
                    dl0, dl1 = dlb[:, 0:1], dlb[:, hd:hd + 1]
                    dq = None
                    for krows, s, dp, bias in tiles:
                        p0 = jnp.exp(s[:BLK] + bias - lse0)
                        p1 = jnp.exp(s[BLK:] + bias - lse1)
                        ds0 = p0 * (dp[:BLK] - dl0)
                        ds1 = p1 * (dp[BLK:] - dl1)
                        pcat = jnp.concatenate([p0, p1], axis=0).astype(BF16)
                        dscat = jnp.concatenate([ds0, ds1], axis=0).astype(BF16)
                        dv_ref[krows, :] += lax.dot_general(pcat, docat, _T0, preferred_element_type=F32)
                        dk_ref[krows, :] += lax.dot_general(dscat, qcat, _T0, preferred_element_type=F32)
                        dsrow = jnp.concatenate([ds0, ds1], axis=1).astype(BF16)
                        kcat = _stack_heads((k_ref[krows, :] * QK_SCALE).astype(BF16), masks)
                        t = jnp.dot(dsrow, kcat, preferred_element_type=F32)
                        dq = t if dq is None else dq + t
                    dq_ref[qrows, :] += dq

            _dil_sets(d, qset)

        def out_body(i, c):
            rows = pl.ds(pl.multiple_of(i * BLK, BLK), BLK)
            tab = (c_ref[rows, :], s1_ref[rows, :], s2_ref[rows, :])
            dqo_ref[rows, :] = _rope_apply(dq_ref[rows, :], *tab, transpose=True).astype(dqo_ref.dtype)
            dko_ref[rows, :] = _rope_apply(dk_ref[rows, :], *tab, transpose=True).astype(dko_ref.dtype)
            dvo_ref[rows, :] = dv_ref[rows, :].astype(dvo_ref.dtype)
            return c

        lax.fori_loop(0, NBLK, out_body, 0)

    spec = lambda off: pl.BlockSpec((None, SEQ, LANES), lambda b, j: (b, 0, off + j))
    ospec = pl.BlockSpec((None, SEQ, LANES), lambda b, j: (b, 0, j))
    tspec = pl.BlockSpec((SEQ, LANES), lambda b, j: (0, 0))
    osd = jax.ShapeDtypeStruct((nb, SEQ, DIL_W), BF16)
    return pl.pallas_call(
        body, out_shape=(osd, osd, osd), grid=(nb, ncol),
        in_specs=[spec(0), spec(ncol), spec(2 * ncol), spec(do_off), ospec, ospec, tspec, tspec, tspec],
        out_specs=(ospec, ospec, ospec),
        scratch_shapes=[pltpu.VMEM((SEQ, LANES), F32)] * 5,
        compiler_params=_cparams(dimension_semantics=("parallel", "parallel")), name=name,
    )(qkv, qkv, qkv, do, o, lse, *tabs)


FOX_GROUP = 4
assert NBLK % FOX_GROUP == 0
_FOX_COLS = tuple(c // LANES for c in (C_FQ, C_FK, C_FV))


def _fox_specs():
    cols = [pl.BlockSpec((None, SEQ, LANES), (lambda b, j, off=off: (b, 0, off + j))) for off in _FOX_COLS]
    ospec = pl.BlockSpec((None, SEQ, LANES), lambda b, j: (b, 0, j))
    crspec = pl.BlockSpec((None, None, NBLK, 8, BLK), lambda b, j: (b, j, 0, 0, 0))
    return cols, ospec, crspec


def _fox_key_rows(t, e):
    return pl.ds(pl.multiple_of((FOX_GROUP * t + e) * BLK, BLK), BLK)


def _fox_fwd(p3, crow, *, name):
    nb = p3.shape[0]
    g = FOX_GROUP

    def body(q_ref, k_ref, v_ref, cr_ref, o_ref, lse_ref):
        masks = _head_masks(2)
        tri = _tri_bias(True)

        def qk(qcat, t):
            return tuple(lax.dot_general(qcat, k_ref[_fox_key_rows(t, e), :], _NT, preferred_element_type=F32) for e in range(g))

        def consume(ss, t, state, nblk, diag):
            m0, m1, l0, l1, acc = state
            us = []
            for e in range(nblk):
                cr = cr_ref[g * t + e]
                u0 = ss[e][:BLK] - cr[0:1, :]
                u1 = ss[e][BLK:] - cr[1:2, :]
                if diag and e == nblk - 1:
                    u0, u1 = u0 + tri, u1 + tri
                us.append((u0, u1))
            x0 = functools.reduce(jnp.maximum, [u[0] for u in us])
            x1 = functools.reduce(jnp.maximum, [u[1] for u in us])
            n0 = jnp.maximum(m0, jnp.max(x0, axis=1, keepdims=True))
            n1 = jnp.maximum(m1, jnp.max(x1, axis=1, keepdims=True))
            a0, a1 = jnp.exp(m0 - n0), jnp.exp(m1 - n1)
            acc = acc * jnp.where(masks[0], a0, a1)
            t0 = t1 = None
            for e in range(nblk):
                p0, p1 = jnp.exp(us[e][0] - n0), jnp.exp(us[e][1] - n1)
                t0 = p0 if t0 is None else t0 + p0
                t1 = p1 if t1 is None else t1 + p1
                pcat = jnp.concatenate([p0, p1], axis=1)
                hi = pcat.astype(BF16)
                lo = (pcat - hi.astype(F32)).astype(BF16)
                vcat = _stack_heads(v_ref[_fox_key_rows(t, e), :], masks)
                acc = acc + jnp.dot(hi, vcat, preferred_element_type=F32) + jnp.dot(lo, vcat, preferred_element_type=F32)
            l0 = a0 * l0 + jnp.sum(t0, axis=1, keepdims=True)
            l1 = a1 * l1 + jnp.sum(t1, axis=1, keepdims=True)
            return n0, n1, l0, l1, acc

        def gbody(ng, c):
            neg = jnp.full((BLK, 1), NEG_INF, F32)
            z1 = jnp.zeros((BLK, 1), F32)
            rows = [pl.ds(pl.multiple_of((g * ng + a) * BLK, BLK), BLK) for a in range(g)]
            qcats = [_stack_heads(q_ref[rows[a], :] * QK_SCALE, masks) for a in range(g)]
            first = [qk(qcats[a], 0) for a in range(g)]
            done = []
            for a in range(g):
                def step(t, cc, qcat=qcats[a]):
                    ss, st = cc
                    nxt = qk(qcat, t + 1)
                    return nxt, consume(ss, t, st, g, False)

                done.append(lax.fori_loop(0, ng, step, (first[a], (neg, neg, z1, z1, jnp.zeros((BLK, LANES), F32)))))
            for a in range(g):
                ss, state = done[a]
                m0, m1, l0, l1, acc = consume(ss, ng, state, a + 1, True)
                o_ref[rows[a], :] = acc / jnp.where(masks[0], l0, l1)
                lse_ref[rows[a], :] = jnp.where(masks[0], m0 + jnp.log(l0), m1 + jnp.log(l1))
            return c

        lax.fori_loop(0, NBLK // g, gbody, 0)

    cols, ospec, crspec = _fox_specs()
    osd = jax.ShapeDtypeStruct((nb, SEQ, FOX_W), F32)
    return pl.pallas_call(
        body, out_shape=(osd, osd), grid=(nb, FOX_W // LANES), in_specs=cols + [crspec], out_specs=(ospec, ospec),
        compiler_params=_cparams(dimension_semantics=("parallel", "parallel")), name=name,
    )(p3, p3, p3, crow)


def _fox_bwd(p3, crow, do, o, lse, *, do_off, name):
    nb = p3.shape[0]
    g = FOX_GROUP
    hd = HEAD_DIM

    def body(q_ref, k_ref, v_ref, cr_ref, do_ref, o_ref, lse_ref, dq_ref, dko_ref, dvo_ref, dcr_ref, dk_ref, dv_ref):
        masks = _head_masks(2)
        tri = _tri_bias(True)
        dk_ref[...] = jnp.zeros_like(dk_ref)
        dv_ref[...] = jnp.zeros_like(dv_ref)
        dcr_ref[...] = jnp.zeros_like(dcr_ref)

        def products(qcat, docat, t):
            out = []
            for e in range(g):
                krows = _fox_key_rows(t, e)
                out.append(lax.dot_general(qcat, k_ref[krows, :], _NT, preferred_element_type=F32))
                out.append(lax.dot_general(docat, v_ref[krows, :], _NT, preferred_element_type=F32))
            return tuple(out)

        def consume(prod, t, ctx, dq, nblk, diag):
            qcat, docat, lse0, lse1, dl0, dl1 = ctx
            for e in range(nblk):
                jb = g * t + e
                krows = _fox_key_rows(t, e)
                s, dp = prod[2 * e], prod[2 * e + 1]
                cr = cr_ref[jb]
                u0 = s[:BLK] - cr[0:1, :]
                u1 = s[BLK:] - cr[1:2, :]
                if diag and e == nblk - 1:
                    u0, u1 = u0 + tri, u1 + tri
                p0 = jnp.exp(u0 - lse0)
                p1 = jnp.exp(u1 - lse1)
                ds0 = p0 * (dp[:BLK] - dl0)
                ds1 = p1 * (dp[BLK:] - dl1)
                dcr_ref[jb, 0:1, :] += jnp.sum(ds0, axis=0, keepdims=True)
                dcr_ref[jb, 1:2, :] += jnp.sum(ds1, axis=0, keepdims=True)
                pcat = jnp.concatenate([p0, p1], axis=0).astype(BF16)
                dscat = jnp.concatenate([ds0, ds1], axis=0).astype(BF16)
                dv_ref[krows, :] += lax.dot_general(pcat, docat, _T0, preferred_element_type=F32)
                dk_ref[krows, :] += lax.dot_general(dscat, qcat, _T0, preferred_element_type=F32)
                dsrow = jnp.concatenate([ds0, ds1], axis=1).astype(BF16)
                dq = dq + jnp.dot(dsrow, _stack_heads(k_ref[krows, :] * QK_SCALE, masks), preferred_element_type=F32)
            return dq

        def gbody(ng, c):
            ctxs, rows = [], []
            for a in range(g):
                r = pl.ds(pl.multiple_of((g * ng + a) * BLK, BLK), BLK)
                qcat = _stack_heads(q_ref[r, :] * QK_SCALE, masks)
                dob = do_ref[r, :].astype(BF16)
                prod = dob.astype(F32) * o_ref[r, :]
                z = jnp.zeros_like(prod)
                dl0 = jnp.sum(jnp.where(masks[0], prod, z), axis=1, keepdims=True)
                dl1 = jnp.sum(jnp.where(masks[1], prod, z), axis=1, keepdims=True)
                lseb = lse_ref[r, :]
                ctxs.append((qcat, _stack_heads(dob, masks), lseb[:, 0:1], lseb[:, hd:hd + 1], dl0, dl1))
                rows.append(r)
            first = [products(ctxs[a][0], ctxs[a][1], 0) for a in range(g)]
            done = []
            for a in range(g):
                def step(t, cc, ctx=ctxs[a]):
                    pr, dq = cc
                    nxt = products(ctx[0], ctx[1], t + 1)
                    return nxt, consume(pr, t, ctx, dq, g, False)

                done.append(lax.fori_loop(0, ng, step, (first[a], jnp.zeros((BLK, LANES), F32))))
            for a in range(g):
                pr, dq = done[a]
                dq_ref[rows[a], :] = consume(pr, ng, ctxs[a], dq, a + 1, True).astype(dq_ref.dtype)
            return c

        lax.fori_loop(0, NBLK // g, gbody, 0)
        dko_ref[...] = dk_ref[...].astype(dko_ref.dtype)
        dvo_ref[...] = dv_ref[...].astype(dvo_ref.dtype)

    cols, ospec, crspec = _fox_specs()
    dospec = pl.BlockSpec((None, SEQ, LANES), lambda b, j: (b, 0, do_off + j))
    osd = jax.ShapeDtypeStruct((nb, SEQ, FOX_W), BF16)
    return pl.pallas_call(
        body, out_shape=(osd, osd, osd, jax.ShapeDtypeStruct((nb, FOX_W // LANES, NBLK, 8, BLK), F32)),
        grid=(nb, FOX_W // LANES), in_specs=cols + [crspec, dospec, ospec, ospec], out_specs=(ospec, ospec, ospec, crspec),
        scratch_shapes=[pltpu.VMEM((SEQ, LANES), F32)] * 2,
        compiler_params=_cparams(dimension_semantics=("parallel", "parallel")), name=name,
    )(p3, p3, p3, crow, do, o, lse)


def _mem_cfg():
    return _AttnCfg(e=MEM_HEAD_DIM, tq=256, tk=MEM_LEN, lq=SEQ, lk=MEM_LEN, causal=False, window=None, ncol=MEM_HEADS,
                    qcol=lambda j: C_MQ // LANES + j, kcol=lambda j: j, vcol=lambda j: MEM_HEADS + j)


_B1, _B2 = FOX_W // LANES, (FOX_W + DIL_W) // LANES


def _dy_gate_bwd(dx2b, wo, fox, dil, memo, p16, *, tm, tn, name):
    t, d = dx2b.shape
    assert FOX_W % tn == 0 and DIL_W % tn == 0 and MEM_W % tn == 0 and all(c % tn == 0 for c in (C_FG, C_DG, C_MG))
    n1, n2, n3 = FOX_W // tn, (FOX_W + DIL_W) // tn, MIX_W // tn

    def body(dx_ref, w_ref, f_ref, d_ref, m_ref, g_ref, da_ref, dg_ref):
        j = pl.program_id(1)
        dyv = lax.dot_general(dx_ref[...], w_ref[...], _NT, preferred_element_type=F32)
        a = jnp.where(j < n1, f_ref[...], jnp.where(j < n2, d_ref[...], m_ref[...]))
        gt = g_ref[...].astype(F32)
        sg = 1.0 / (1.0 + jnp.exp(-gt))
        da_ref[...] = (dyv * gt * sg).astype(da_ref.dtype)
        dg_ref[...] = (dyv * a * sg * (1.0 + gt * (1.0 - sg))).astype(dg_ref.dtype)

    def gcol(j):
        return jnp.where(j < n1, C_FG // tn + j, jnp.where(j < n2, C_DG // tn + j - n1, C_MG // tn + j - n2))

    tile = pl.BlockSpec((tm, tn), lambda i, j: (i, j))
    return pl.pallas_call(
        body,
        out_shape=(jax.ShapeDtypeStruct((t, MIX_W), BF16), jax.ShapeDtypeStruct((t, MIX_W), BF16)),
        grid=(t // tm, n3),
        in_specs=[pl.BlockSpec((tm, d), lambda i, j: (i, 0)), pl.BlockSpec((tn, d), lambda i, j: (j, 0)),
                  pl.BlockSpec((tm, tn), lambda i, j: (i, jnp.minimum(j, n1 - 1))),
                  pl.BlockSpec((tm, tn), lambda i, j: (i, jnp.clip(j - n1, 0, n2 - n1 - 1))),
                  pl.BlockSpec((tm, tn), lambda i, j: (i, jnp.clip(j - n2, 0, n3 - n2 - 1))),
                  pl.BlockSpec((tm, tn), lambda i, j: (i, gcol(j)))],
        out_specs=(tile, tile),
        compiler_params=_cparams(dimension_semantics=("parallel", "parallel")),
        name=name,
    )(dx2b, wo, fox, dil, memo, p16)


def _silu(g):
    return g / (1.0 + jnp.exp(-g))


def _out_loss(fox, dil, memo, p16, wo, x, tgt, gfin, *, tm, name):
    t, d = x.shape
    n_feat = float(d)

    def body(f_ref, d_ref, m_ref, fg_ref, dg_ref, mg_ref, w_ref, x_ref, t_ref, g_ref, y_ref, dx_ref, dxb_ref, st_ref):
        i = pl.program_id(0)

        @pl.when(i == 0)
        def _():
            st_ref[...] = jnp.zeros_like(st_ref)

        y = jnp.concatenate([(a_ref[...] * _silu(gt_ref[...].astype(F32))).astype(BF16)
                             for a_ref, gt_ref in ((f_ref, fg_ref), (d_ref, dg_ref), (m_ref, mg_ref))], axis=1)
        y_ref[...] = y
        x2 = x_ref[...] + jnp.dot(y, w_ref[...], preferred_element_type=F32)
        r = lax.rsqrt(jnp.mean(x2 * x2, axis=-1, keepdims=True) + RMS_EPS)
        nrm = x2 * r
        gv = g_ref[...]
        err = nrm * gv - t_ref[...]
        dout = err * (1.0 / n_feat)
        dn = dout * gv
        dx2 = r * (dn - nrm * jnp.mean(dn * nrm, axis=-1, keepdims=True))
        dx_ref[...] = dx2
        dxb_ref[...] = dx2.astype(dxb_ref.dtype)
        st_ref[0:1, :] += jnp.sum(dout * nrm, axis=0, keepdims=True)
        st_ref[1:2, :] += (0.5 / n_feat) * jnp.sum(err * err, axis=0, keepdims=True)

    row = pl.BlockSpec((tm, d), lambda i: (i, 0))
    whole = lambda w: pl.BlockSpec((tm, w), lambda i: (i, 0))
    gate = lambda w, col: pl.BlockSpec((tm, w), lambda i: (i, col // w))
    return pl.pallas_call(
        body,
        out_shape=(jax.ShapeDtypeStruct((t, MIX_W), BF16), jax.ShapeDtypeStruct((t, d), F32), jax.ShapeDtypeStruct((t, d), BF16),
                   jax.ShapeDtypeStruct((8, d), F32)),
        grid=(t // tm,),
        in_specs=[whole(FOX_W), whole(DIL_W), whole(MEM_W), gate(FOX_W, C_FG), gate(DIL_W, C_DG), gate(MEM_W, C_MG),
                  pl.BlockSpec((MIX_W, d), lambda i: (0, 0)), row, row, pl.BlockSpec((1, d), lambda i: (0, 0))],
        out_specs=(pl.BlockSpec((tm, MIX_W), lambda i: (i, 0)), row, row, pl.BlockSpec((8, d), lambda i: (0, 0))),
        compiler_params=_cparams(dimension_semantics=("arbitrary",)),
        name=name,
    )(fox, dil, memo, p16, p16, p16, wo, x, tgt, gfin)


def _dh_rms_bwd(dp, w, x, g, resid, *, tm, tk, name):
    t, d = x.shape
    kdim = dp.shape[1]
    nk = kdim // tk

    def body(*refs):
        if resid is not None:
            dp_ref, w_ref, x_ref, g_ref, r_ref, dx_ref, gg_ref, acc_ref = refs
        else:
            dp_ref, w_ref, x_ref, g_ref, dx_ref, gg_ref, acc_ref = refs
        i = pl.program_id(0)
        k = pl.program_id(1)

        @pl.when(jnp.logical_and(i == 0, k == 0))
        def _():
            gg_ref[...] = jnp.zeros_like(gg_ref)

        @pl.when(k == 0)
        def _():
            acc_ref[...] = jnp.zeros_like(acc_ref)

        acc_ref[...] += lax.dot_general(dp_ref[...], w_ref[...], _NT, preferred_element_type=F32)

        @pl.when(k == nk - 1)
        def _():
            dh = acc_ref[...]
            xv = x_ref[...]
            r = lax.rsqrt(jnp.mean(xv * xv, axis=-1, keepdims=True) + RMS_EPS)
            nrm = xv * r
            dn = dh * g_ref[...]
            dx = r * (dn - nrm * jnp.mean(dn * nrm, axis=-1, keepdims=True))
            if resid is not None:
                dx = dx + r_ref[...]
            dx_ref[...] = dx
            gg_ref[0:1, :] += jnp.sum(dh * nrm, axis=0, keepdims=True)

    row = pl.BlockSpec((tm, d), lambda i, k: (i, 0))
    in_specs = [pl.BlockSpec((tm, tk), lambda i, k: (i, k)), pl.BlockSpec((d, tk), lambda i, k: (0, k)), row,
                pl.BlockSpec((1, d), lambda i, k: (0, 0))]
    args = [dp, w, x, g]
    if resid is not None:
        in_specs.append(row)
        args.append(resid)
    return pl.pallas_call(
        body,
        out_shape=(jax.ShapeDtypeStruct((t, d), F32), jax.ShapeDtypeStruct((8, d), F32)),
        grid=(t // tm, nk),
        in_specs=in_specs,
        out_specs=(row, pl.BlockSpec((8, d), lambda i, k: (0, 0))),
        scratch_shapes=[pltpu.VMEM((tm, d), F32)],
        compiler_params=_cparams(dimension_semantics=("arbitrary", "arbitrary")),
        name=name,
    )(*args)


_FLOG0 = 4 * FOX_W
_W_IN_SEGMENTS = ((0, _FLOG0, 0), (_FLOG0, _FLOG0 + FOX_HEADS, PW), (_FLOG0 + FOX_HEADS, IN_W, C_DQ))
SHARD_W = IN_W // N_CHIPS


def _rearrange_w_in(shards):
    def cols(lo, hi):
        parts = []
        for k in range(N_CHIPS):
            a, b = max(lo, k * SHARD_W), min(hi, (k + 1) * SHARD_W)
            if a < b:
                parts.append(shards[k][:, a - k * SHARD_W:b - k * SHARD_W])
        return parts

    (a0, a1, _), (f0, f1, _), (b0, b1, _) = _W_IN_SEGMENTS
    pad = jnp.zeros((shards[0].shape[0], LANES - FOX_HEADS), shards[0].dtype)
    return jnp.concatenate(cols(a0, a1) + cols(b0, b1) + cols(f0, f1) + [pad], axis=1)


def _w_in_grad_slabs(g):
    slabs = []
    for k in range(N_CHIPS):
        parts = []
        for lo, hi, at in _W_IN_SEGMENTS:
            a, b = max(lo, k * SHARD_W), min(hi, (k + 1) * SHARD_W)
            if a < b:
                parts.append(g[:, at + a - lo:at + b - lo])
        slabs.append(jnp.concatenate(parts, axis=1))
    return jnp.stack(slabs, axis=0)


def _local_grads(x, mem, norm_g, w_r, b_forget, mem_norm_g, w_kv, w_o, final_norm_g, tgt):
    nb = x.shape[0]
    t = nb * SEQ
    x2d = x.reshape(t, D_MODEL)
    tgt2d = tgt.reshape(t, D_MODEL)
    tabs = _rope_tables()
    bpad = jnp.pad(b_forget.reshape(1, FOX_HEADS), ((0, 0), (0, LANES - FOX_HEADS)))

    h = _rms_fwd(x2d, norm_g.reshape(1, D_MODEL), tm=512, name="rms_x")
    p16, dqkv = _proj(h, w_r, tabs, n=PW, tm=2048, tn=256, name="proj")
    flog = _matmul(h, w_r[:, PW:], out_dtype=F32, tm=1024, tn=LANES, tk=D_MODEL, name="proj_flog")
    c12 = _flog_fwd(flog, bpad, nb=nb, ts=256, name="flog_fwd")

    crow = c12[:, :FOX_HEADS].reshape(nb, NBLK, BLK, FOX_HEADS // 2, 2).transpose(0, 3, 1, 4, 2)
    crow = jnp.pad(crow, ((0, 0), (0, 0), (0, 0), (0, 6), (0, 0)))
    p3 = p16.reshape(nb, SEQ, PW)
    fox, fox_lse = _fox_fwd(p3, crow, name="fox_fwd")

    dqkv3 = dqkv.reshape(nb, SEQ, 3 * DIL_W)
    dil, dil_lse = _dil_fwd(dqkv3, name="dil_fwd")

    mh = _rms_fwd(mem.reshape(nb * MEM_LEN, D_MODEL), mem_norm_g.reshape(1, D_MODEL), tm=nb * MEM_LEN, name="rms_mem")
    mkv = _matmul(mh, w_kv, out_dtype=BF16, tm=nb * MEM_LEN, tn=512, tk=D_MODEL, name="mem_kv")
    mkv3 = mkv.reshape(nb, MEM_LEN, 2 * MEM_W)
    mcfg = _mem_cfg()
    memo, mem_lse = _attn_fwd(mcfg, p3, mkv3, mkv3, out_cols=MEM_W, name="mem_fwd")

    fox2, dil2, memo2 = fox.reshape(t, FOX_W), dil.reshape(t, DIL_W), memo.reshape(t, MEM_W)
    y, dx2, dx2b, st = _out_loss(fox2, dil2, memo2, p16, w_o, x2d, tgt2d, final_norm_g.reshape(1, D_MODEL), tm=256,
                                 name="out_loss")

    g_wo = _matmul(y, dx2b, mode="tn", out_dtype=F32, tm=1024, tn=512, tk=1024, name="grad_w_out")
    datt, dgate = _dy_gate_bwd(dx2b, w_o, fox2, dil2, memo2, p16, tm=1024, tn=256, name="dy_gate_bwd")
    datt3 = datt.reshape(nb, SEQ, MIX_W)

    dfq, dfk, dfv, dcr = _fox_bwd(p3, crow, datt3, fox, fox_lse, do_off=0, name="fox_bwd")
    dcol = -dcr[:, :, :, :2, :].transpose(0, 2, 4, 1, 3).reshape(t, FOX_HEADS)
    dcol = jnp.pad(dcol, ((0, 0), (0, LANES - FOX_HEADS)))
    dflog, gb = _flog_bwd(dcol, flog, bpad, nb=nb, ts=256, name="flog_bwd")

    ddq, ddk, ddv = _dil_bwd(dqkv3, datt3, dil, dil_lse, tabs, do_off=_B1, name="dil_bwd")

    dmq, dmk, dmv = _attn_bwd(mcfg, p3, mkv3, mkv3, datt3, memo, mem_lse, out_cols=MEM_W, kv_cols=MEM_W, do_off=_B2,
                              name="mem_bwd")
    dmkv = jnp.concatenate([dmk, dmv], axis=-1).reshape(nb * MEM_LEN, 2 * MEM_W).astype(BF16)
    g_wkv = _matmul(mh, dmkv, mode="tn", out_dtype=F32, tm=512, tn=512, tk=nb * MEM_LEN, name="grad_w_kv")
    _, gmn = _dh_rms_bwd(dmkv, w_kv, mem.reshape(nb * MEM_LEN, D_MODEL), mem_norm_g.reshape(1, D_MODEL), None,
                         tm=nb * MEM_LEN, tk=2 * MEM_W, name="mem_rms_bwd")

    flat = lambda a: a.reshape(t, -1)
    dp = jnp.concatenate([flat(dfq), flat(dfk), flat(dfv), dgate[:, :FOX_W], flat(ddq), flat(ddk), flat(ddv),
                          dgate[:, FOX_W:FOX_W + DIL_W], flat(dmq).astype(BF16), dgate[:, FOX_W + DIL_W:], dflog], axis=1)
    g_wr = _matmul(h, dp, mode="tn", out_dtype=F32, tm=512, tn=PWF // 3, tk=1024, name="grad_w_in")
    gx, gng = _dh_rms_bwd(dp, w_r, x2d, norm_g.reshape(1, D_MODEL), dx2, tm=512, tk=PWF // 3, name="in_rms_bwd")

    gb_row = jnp.pad(gb[0:1, :], ((0, 0), (0, D_MODEL - LANES)))
    small = jnp.concatenate([gng[0:1], gmn[0:1], st[0:1], gb_row, st[1:2], jnp.zeros((3, D_MODEL), F32)], axis=0)
    return gx.reshape(nb, SEQ, D_MODEL), g_wr, g_wkv, g_wo, small


MESH = pl.DeviceIdType.MESH
ANY = pl.BlockSpec(memory_space=pl.ANY)


def _place():
    x, y, c = lax.axis_index("x"), lax.axis_index("y"), lax.axis_index("c")
    other_chips = [(1 - x, y), (x, 1 - y), (1 - x, 1 - y)]
    return x, y, c, other_chips


def _gather_weights(shards):
    n = len(shards)

    def body(*refs):
        in_refs, out_refs = refs[:n], refs[n:2 * n]
        send_sems, recv_sems = refs[2 * n:]
        x, y, c, chips = _place()
        me_chip = 2 * x + y
        sibling = (x, y, 1 - c)

        def half(ref, pc, rows):
            return ref.at[pl.ds(pc * (rows // 2), rows // 2), :]

        def rcopy(k, src, dst, to):
            return pltpu.make_async_remote_copy(src_ref=src, dst_ref=dst, send_sem=send_sems.at[k], recv_sem=recv_sems.at[k],
                                                device_id=to, device_id_type=MESH)

        sends = []
        for t in range(n):
            rows = shards[t].shape[0]
            for j, chip in enumerate(chips):
                cp = rcopy(6 * t + j, half(in_refs[t], c, rows), half(out_refs[t].at[me_chip], c, rows), (*chip, c))
                cp.start()
                sends.append(cp)
        for t in range(n):
            rows = shards[t].shape[0]
            for j, chip in enumerate(chips):
                slot = out_refs[t].at[2 * chip[0] + chip[1]]
                rcopy(6 * t + j, half(slot, c, rows), half(slot, c, rows), sibling).wait_recv()
                fw = rcopy(6 * t + 3 + j, half(slot, c, rows), half(slot, c, rows), sibling)
                fw.start()
                sends.append(fw)
        for t in range(n):
            rows = shards[t].shape[0]
            for j, chip in enumerate(chips):
                slot = out_refs[t].at[2 * chip[0] + chip[1]]
                rcopy(6 * t + 3 + j, half(slot, 1 - c, rows), half(slot, 1 - c, rows), sibling).wait_recv()
        for cp in sends:
            cp.wait_send()

    return pl.pallas_call(
        body,
        out_shape=tuple(jax.ShapeDtypeStruct((N_CHIPS,) + s.shape, s.dtype) for s in shards),
        in_specs=[ANY] * n,
        out_specs=tuple([ANY] * n),
        scratch_shapes=[pltpu.SemaphoreType.DMA((6 * n,)), pltpu.SemaphoreType.DMA((6 * n,))],
        name="gather_weights",
    )(*shards)


def _pair_exchange(gs):
    n = len(gs)

    def body(*refs):
        g_refs, r_refs = refs[:n], refs[n:2 * n]
        send_sems, recv_sems = refs[2 * n:]
        x, y, c, _ = _place()
        cps = []
        for t in range(n):
            hr = gs[t].shape[1] // 2
            cp = pltpu.make_async_remote_copy(src_ref=g_refs[t].at[:, pl.ds((1 - c) * hr, hr), :], dst_ref=r_refs[t],
                                              send_sem=send_sems.at[t], recv_sem=recv_sems.at[t],
                                              device_id=(x, y, 1 - c), device_id_type=MESH)
            cp.start()
            cps.append(cp)
        for cp in cps:
            cp.wait()

    return pl.pallas_call(
        body,
        out_shape=tuple(jax.ShapeDtypeStruct((N_CHIPS, g.shape[1] // 2, g.shape[2]), g.dtype) for g in gs),
        in_specs=[ANY] * n,
        out_specs=tuple([ANY] * n),
        scratch_shapes=[pltpu.SemaphoreType.DMA((n,)), pltpu.SemaphoreType.DMA((n,))],
        name="pair_exchange",
    )(*gs)


def _chip_exchange(ps):
    n = len(ps)

    def body(*refs):
        p_refs, o_refs = refs[:n], refs[n:2 * n]
        send_sems, recv_sems = refs[2 * n:]
        x, y, c, chips = _place()
        me_chip = 2 * x + y
        cps = []
        for t in range(n):
            for j, chip in enumerate(chips):
                cp = pltpu.make_async_remote_copy(src_ref=p_refs[t].at[2 * chip[0] + chip[1]], dst_ref=o_refs[t].at[me_chip],
                                                  send_sem=send_sems.at[3 * t + j], recv_sem=recv_sems.at[3 * t + j],
                                                  device_id=(*chip, c), device_id_type=MESH)
                cp.start()
                cps.append(cp)
        for cp in cps:
            cp.wait()

    return pl.pallas_call(
        body,
        out_shape=tuple(jax.ShapeDtypeStruct(p.shape, p.dtype) for p in ps),
        in_specs=[ANY] * n,
        out_specs=tuple([ANY] * n),
        scratch_shapes=[pltpu.SemaphoreType.DMA((3 * n,)), pltpu.SemaphoreType.DMA((3 * n,))],
        name="chip_exchange",
    )(*ps)


def _pair_swap(rs):
    n = len(rs)

    def body(*refs):
        r_refs, o_refs = refs[:n], refs[n:2 * n]
        send_sems, recv_sems = refs[2 * n:]
        x, y, c, _ = _place()
        cps = []
        for t in range(n):
            cp = pltpu.make_async_remote_copy(src_ref=r_refs[t], dst_ref=o_refs[t], send_sem=send_sems.at[t],
                                              recv_sem=recv_sems.at[t], device_id=(x, y, 1 - c), device_id_type=MESH)
            cp.start()
            cps.append(cp)
        for cp in cps:
            cp.wait()

    return pl.pallas_call(
        body,
        out_shape=tuple(jax.ShapeDtypeStruct(r.shape, r.dtype) for r in rs),
        in_specs=[ANY] * n,
        out_specs=tuple([ANY] * n),
        scratch_shapes=[pltpu.SemaphoreType.DMA((n,)), pltpu.SemaphoreType.DMA((n,))],
        name="pair_swap",
    )(*rs)


N_DEV = 8
LOSS_ROW = 4


def _small_allreduce(small):
    def body(s_ref, o_ref, all_ref, send_sems, recv_sems):
        x, y, c, _ = _place()
        me = 4 * x + 2 * y + c
        all_ref[me] = s_ref[...]
        cps = []
        for k in range(1, N_DEV):
            peer = tuple(1 - p if (k >> s) & 1 else p for p, s in ((x, 2), (y, 1), (c, 0)))
            cp = pltpu.make_async_remote_copy(src_ref=s_ref, dst_ref=all_ref.at[me], send_sem=send_sems.at[k - 1],
                                              recv_sem=recv_sems.at[k - 1], device_id=peer, device_id_type=MESH)
            cp.start()
            cps.append(cp)
        for cp in cps:
            cp.wait()
        tot = all_ref[0]
        for d in range(1, N_DEV):
            tot = tot + all_ref[d]
        o_ref[...] = tot
        o_ref[LOSS_ROW:LOSS_ROW + 1, :] = jnp.broadcast_to(jnp.sum(tot[LOSS_ROW:LOSS_ROW + 1, :], axis=1, keepdims=True),
                                                          (1, tot.shape[1]))

    vm = pl.BlockSpec(memory_space=pltpu.VMEM)
    return pl.pallas_call(
        body,
        out_shape=jax.ShapeDtypeStruct(small.shape, small.dtype),
        in_specs=[vm],
        out_specs=vm,
        scratch_shapes=[pltpu.VMEM((N_DEV,) + small.shape, small.dtype), pltpu.SemaphoreType.DMA((N_DEV - 1,)),
                        pltpu.SemaphoreType.DMA((N_DEV - 1,))],
        name="small_allreduce",
    )(small)


def _sum_pair(g, recv, cidx, *, tr, name):
    _, hr, cols = recv.shape
    nr = hr // tr

    def body(c_ref, g_ref, r_ref, o_ref):
        o_ref[...] = (g_ref[...] + r_ref[...]).astype(o_ref.dtype)

    grid_spec = pltpu.PrefetchScalarGridSpec(
        num_scalar_prefetch=1,
        grid=(N_CHIPS, nr),
        in_specs=[pl.BlockSpec((None, tr, cols), lambda k, i, c_ref: (k, c_ref[0] * nr + i, 0)),
                  pl.BlockSpec((None, tr, cols), lambda k, i, c_ref: (k, i, 0))],
        out_specs=pl.BlockSpec((None, tr, cols), lambda k, i, c_ref: (k, i, 0)),
    )
    return pl.pallas_call(body, out_shape=jax.ShapeDtypeStruct(recv.shape, BF16), grid_spec=grid_spec,
                          compiler_params=_cparams(), name=name)(cidx, g, recv)


def _sum_chips(p, *, tr, name):
    _, rows, cols = p.shape

    def body(p_ref, o_ref):
        tot = p_ref[0].astype(F32)
        for k in range(1, N_CHIPS):
            tot = tot + p_ref[k].astype(F32)
        o_ref[...] = tot

    return pl.pallas_call(
        body,
        out_shape=jax.ShapeDtypeStruct((rows, cols), F32),
        grid=(rows // tr,),
        in_specs=[pl.BlockSpec((N_CHIPS, tr, cols), lambda i: (0, i, 0))],
        out_specs=pl.BlockSpec((tr, cols), lambda i: (i, 0)),
        compiler_params=_cparams(),
        name=name,
    )(p)


def _adamw(w, g, m, v, *, tr, name):
    rows, cols = w.shape
    bc1 = 1.0 / (1.0 - ADAM_B1 ** ADAM_STEP)
    bc2 = 1.0 / (1.0 - ADAM_B2 ** ADAM_STEP)

    def body(w_ref, g_ref, m_ref, v_ref, d_ref, nm_ref, nv_ref):
        gv = g_ref[...]
        nm = ADAM_B1 * m_ref[...] + (1.0 - ADAM_B1) * gv
        nv = ADAM_B2 * v_ref[...] + (1.0 - ADAM_B2) * (gv * gv)
        d_ref[...] = -ADAM_LR * ((nm * bc1) / (jnp.sqrt(nv * bc2) + ADAM_EPS) + ADAM_WD * w_ref[...])
        nm_ref[...] = nm
        nv_ref[...] = nv

    spec = pl.BlockSpec((tr, cols), lambda i: (i, 0))
    sd = jax.ShapeDtypeStruct((rows, cols), F32)
    return pl.pallas_call(body, out_shape=(sd, sd, sd), grid=(rows // tr,), in_specs=[spec] * 4, out_specs=(spec,) * 3,
                          compiler_params=_cparams(), name=name)(w, g, m, v)


def _pack_small(norm, mem_norm, final_norm, b_forget):
    rows = [norm.reshape(1, D_MODEL), mem_norm.reshape(1, D_MODEL), final_norm.reshape(1, D_MODEL),
            jnp.pad(b_forget.reshape(1, FOX_HEADS), ((0, 0), (0, D_MODEL - FOX_HEADS))), jnp.zeros((4, D_MODEL), F32)]
    return jnp.concatenate(rows, axis=0)


def _unpack_small(a):
    return a[0:1], a[3:4, :FOX_HEADS], a[1:2], a[2]


def kernel(x, mem, norm_g, w_in, b_forget, mem_norm_g, w_mem_kv, w_out, final_norm_g, loss_target, m_norm_g, m_w_in, m_b_forget, m_mem_norm_g, m_w_mem_kv, m_w_out, m_final_norm_g, v_norm_g, v_w_in, v_b_forget, v_mem_norm_g, v_w_mem_kv, v_w_out, v_final_norm_g):
    core = lax.axis_index("c").astype(jnp.int32)
    me_chip = (2 * lax.axis_index("x") + lax.axis_index("y")).astype(jnp.int32)
    cidx = core.reshape(1)

    def own_slot(arr, own):
        return lax.dynamic_update_slice(arr, own[None].astype(arr.dtype), (me_chip,) + (0,) * own.ndim)

    mine = [w_in[0].astype(BF16), w_mem_kv[0].astype(BF16), w_out[0].astype(BF16)]
    g_in, g_kv, g_out = (own_slot(g, s) for g, s in zip(_gather_weights(mine), mine))
    w_r = _rearrange_w_in([g_in[k] for k in range(N_CHIPS)])
    w_kv = g_kv.reshape(D_MODEL, 2 * MEM_W)
    w_o = g_out.reshape(MIX_W, D_MODEL)

    gx, g_wr, g_wkv, g_wo, small = _local_grads(x, mem, norm_g, w_r, b_forget, mem_norm_g, w_kv, w_o, final_norm_g, loss_target)

    slabs = [_w_in_grad_slabs(g_wr),
             g_wkv.reshape(N_CHIPS, D_MODEL // N_CHIPS, 2 * MEM_W),
             g_wo.reshape(N_CHIPS, MIX_W // N_CHIPS, D_MODEL)]
    trs = (128, 128, 256)
    names = ("w_in", "w_mem_kv", "w_out")
    recv = _pair_exchange(slabs)
    pair = [_sum_pair(g, r, cidx, tr=tr, name=f"sum_pair_{nm}") for g, r, tr, nm in zip(slabs, recv, trs, names)]
    got = [lax.dynamic_update_slice(g, lax.dynamic_slice(p, (me_chip, 0, 0), (1,) + p.shape[1:]), (me_chip, 0, 0))
           for g, p in zip(_chip_exchange(pair), pair)]
    red = [_sum_chips(p, tr=tr, name=f"sum_chips_{nm}") for p, tr, nm in zip(got, trs, names)]
    sib = _pair_swap(red)
    grads = [jnp.where(core == 0, jnp.concatenate([r, s], axis=0), jnp.concatenate([s, r], axis=0)) for r, s in zip(red, sib)]

    outs = {}
    for nm, g, w, m, v, tr in zip(names, grads, (w_in, w_mem_kv, w_out), (m_w_in, m_w_mem_kv, m_w_out),
                                  (v_w_in, v_w_mem_kv, v_w_out), trs):
        d, nmo, nvo = _adamw(w[0], g, m[0], v[0], tr=tr, name=f"adamw_{nm}")
        outs[nm] = tuple(a[None] for a in (g, d, nmo, nvo))

    gsum = _small_allreduce(small)
    sd, sm, sv = _adamw(_pack_small(norm_g, mem_norm_g, final_norm_g, b_forget), gsum,
                        _pack_small(m_norm_g, m_mem_norm_g, m_final_norm_g, m_b_forget),
                        _pack_small(v_norm_g, v_mem_norm_g, v_final_norm_g, v_b_forget), tr=8, name="adamw_small")
    loss = gsum[LOSS_ROW, 0]

    def group(i, small_arr):
        ng, bf, mg, fg = _unpack_small(small_arr)
        return (ng, outs["w_in"][i], bf, mg, outs["w_mem_kv"][i], outs["w_out"][i], fg)

    return (loss, gx, *group(0, gsum), *group(1, sd), *group(2, sm), *group(3, sv))
```

```python
import functools
import math

import jax
import jax.numpy as jnp
from jax import lax
from jax.experimental import pallas as pl
from jax.experimental.pallas import tpu as pltpu

F32 = jnp.float32
BF16 = jnp.bfloat16

D_MODEL = 1024
SEQ = 2048
HEAD_DIM = 64
FOX_HEADS = 12
DIL_HEADS = 12
MEM_HEADS = 4
MEM_HEAD_DIM = 128
MEM_LEN = 256
FOX_W = FOX_HEADS * HEAD_DIM
DIL_W = DIL_HEADS * HEAD_DIM
MEM_W = MEM_HEADS * MEM_HEAD_DIM
MIX_W = FOX_W + DIL_W + MEM_W
DILATIONS = ((128, 1), (512, 4), (2048, 16))
ROPE_THETA = 500000.0
ROPE_DIM = HEAD_DIM // 4
RMS_EPS = 1e-6
NEG_INF = -1e30
IN_SIZES = [FOX_W] * 4 + [FOX_HEADS] + [DIL_W] * 4 + [MEM_W] * 2
IN_W = sum(IN_SIZES)

ADAM_LR = 0.001
ADAM_B1 = 0.9
ADAM_B2 = 0.999
ADAM_EPS = 1e-08
ADAM_WD = 0.01
ADAM_STEP = 10

LANES = 128
N_CHIPS = 4
PW = 7168
PWF = PW + LANES
C_FQ, C_FK, C_FV, C_FG = 0, 768, 1536, 2304
C_DQ, C_DK, C_DV, C_DG = 3072, 3840, 4608, 5376
C_MQ, C_MG = 6144, 6656
VMEM_LIMIT = 48 * 1024 * 1024


def _cparams(**kw):
    return pltpu.CompilerParams(vmem_limit_bytes=VMEM_LIMIT, **kw)


def _matmul(a, b, *, out_dtype, tm, tn, tk, name, mode="nn"):
    if mode == "tn":
        (kdim, m), n = a.shape, b.shape[1]
        a_spec = pl.BlockSpec((tk, tm), lambda i, j, k: (k, i))
        b_spec = pl.BlockSpec((tk, tn), lambda i, j, k: (k, j))
        dims = _T0
    elif mode == "nt":
        (m, kdim), n = a.shape, b.shape[0]
        a_spec = pl.BlockSpec((tm, tk), lambda i, j, k: (i, k))
        b_spec = pl.BlockSpec((tn, tk), lambda i, j, k: (j, k))
        dims = _NT
    else:
        (m, kdim), n = a.shape, b.shape[1]
        a_spec = pl.BlockSpec((tm, tk), lambda i, j, k: (i, k))
        b_spec = pl.BlockSpec((tk, tn), lambda i, j, k: (k, j))
        dims = (((1,), (0,)), ((), ()))
    nk = kdim // tk
    assert m % tm == 0 and n % tn == 0 and kdim % tk == 0

    def body(a_ref, b_ref, o_ref, acc_ref):
        k = pl.program_id(2)

        @pl.when(k == 0)
        def _():
            acc_ref[...] = jnp.zeros_like(acc_ref)

        acc_ref[...] += lax.dot_general(a_ref[...], b_ref[...], dims, preferred_element_type=F32)

        @pl.when(k == nk - 1)
        def _():
            o_ref[...] = acc_ref[...].astype(o_ref.dtype)

    return pl.pallas_call(
        body,
        out_shape=jax.ShapeDtypeStruct((m, n), out_dtype),
        grid=(m // tm, n // tn, nk),
        in_specs=[a_spec, b_spec],
        out_specs=pl.BlockSpec((tm, tn), lambda i, j, k: (i, j)),
        scratch_shapes=[pltpu.VMEM((tm, tn), F32)],
        compiler_params=_cparams(dimension_semantics=("parallel", "parallel", "arbitrary")),
        name=name,
    )(a, b)


def _rms_fwd(x, g, *, tm, name):
    t, d = x.shape

    def body(x_ref, g_ref, h_ref):
        xv = x_ref[...]
        r = lax.rsqrt(jnp.mean(xv * xv, axis=-1, keepdims=True) + RMS_EPS)
        h_ref[...] = (xv * r * g_ref[...]).astype(h_ref.dtype)

    return pl.pallas_call(
        body,
        out_shape=jax.ShapeDtypeStruct((t, d), BF16),
        grid=(t // tm,),
        in_specs=[pl.BlockSpec((tm, d), lambda i: (i, 0)), pl.BlockSpec((1, d), lambda i: (0, 0))],
        out_specs=pl.BlockSpec((tm, d), lambda i: (i, 0)),
        compiler_params=_cparams(),
        name=name,
    )(x, g)


def _rope_tables():
    half = ROPE_DIM // 2
    pos = jnp.arange(SEQ, dtype=F32)
    inv_freq = 1.0 / (ROPE_THETA ** (jnp.arange(0, ROPE_DIM, 2, dtype=F32) / ROPE_DIM))
    ang = pos[:, None] * inv_freq[None, :]
    cos, sin = jnp.cos(ang), jnp.sin(ang)
    one = jnp.ones((SEQ, HEAD_DIM - ROPE_DIM), F32)
    zero = jnp.zeros((SEQ, HEAD_DIM - ROPE_DIM), F32)
    zh = jnp.zeros((SEQ, half), F32)
    c = jnp.concatenate([cos, cos, one], axis=1)
    s1 = jnp.concatenate([zh, sin, zero], axis=1)
    s2 = jnp.concatenate([-sin, zh, zero], axis=1)
    rep = LANES // HEAD_DIM
    return jnp.tile(c, (1, rep)), jnp.tile(s1, (1, rep)), jnp.tile(s2, (1, rep))


def _rope_apply(t, c, s1, s2, transpose=False):
    n = t.shape[-1]
    rep = n // LANES
    c, s1, s2 = (jnp.tile(u, (1, rep)) for u in (c, s1, s2))
    half = ROPE_DIM // 2
    if not transpose:
        return t * c + pltpu.roll(t, half, 1) * s1 + pltpu.roll(t, n - half, 1) * s2
    return t * c + pltpu.roll(t * s1, n - half, 1) + pltpu.roll(t * s2, half, 1)


def _proj(h, w, tabs, *, n, tm, tn, name):
    t, d = h.shape
    assert C_DQ % tn == 0 and (C_DV - C_DQ) % tn == 0 and (C_DG - C_DQ) % tn == 0
    rope_lo, rope_hi, dil_hi = C_DQ // tn, C_DV // tn, C_DG // tn
    s_blocks = SEQ // tm

    def body(h_ref, w_ref, c_ref, s1_ref, s2_ref, o_ref, f_ref):
        j = pl.program_id(1)
        acc = jnp.dot(h_ref[...], w_ref[...], preferred_element_type=F32)
        is_rope = jnp.logical_and(j >= rope_lo, j < rope_hi)

        @pl.when(is_rope)
        def _():
            r = _rope_apply(acc, c_ref[...], s1_ref[...], s2_ref[...])
            o_ref[...] = r.astype(o_ref.dtype)
            f_ref[...] = r

        @pl.when(jnp.logical_not(is_rope))
        def _():
            o_ref[...] = acc.astype(o_ref.dtype)

        @pl.when(jnp.logical_and(j >= rope_hi, j < dil_hi))
        def _():
            f_ref[...] = acc

    tab_spec = pl.BlockSpec((tm, LANES), lambda i, j: (i % s_blocks, 0))
    f_spec = pl.BlockSpec((tm, tn), lambda i, j: (i, jnp.clip(j - rope_lo, 0, dil_hi - rope_lo - 1)))
    return pl.pallas_call(
        body,
        out_shape=(jax.ShapeDtypeStruct((t, n), BF16), jax.ShapeDtypeStruct((t, 3 * DIL_W), F32)),
        grid=(t // tm, n // tn),
        in_specs=[pl.BlockSpec((tm, d), lambda i, j: (i, 0)), pl.BlockSpec((d, tn), lambda i, j: (0, j)),
                  tab_spec, tab_spec, tab_spec],
        out_specs=(pl.BlockSpec((tm, tn), lambda i, j: (i, j)), f_spec),
        compiler_params=_cparams(dimension_semantics=("parallel", "arbitrary")),
        name=name,
    )(h, w, *tabs)


def _split3(x):
    hi = x.astype(BF16)
    r1 = x - hi.astype(F32)
    mid = r1.astype(BF16)
    lo = (r1 - mid.astype(F32)).astype(BF16)
    return hi, mid, lo


def _dot3(sel, x, sel_is_lhs):
    out = None
    for piece in _split3(x):
        t = jnp.dot(sel, piece, preferred_element_type=F32) if sel_is_lhs else jnp.dot(piece, sel, preferred_element_type=F32)
        out = t if out is None else out + t
    return out


def _flog_fwd(flog, bpad, *, nb, ts, name):
    ns = SEQ // ts

    def body(f_ref, b_ref, c_ref, carry_ref):
        s = pl.program_id(1)

        @pl.when(s == 0)
        def _():
            carry_ref[...] = jnp.zeros_like(carry_ref)

        z = f_ref[...] + b_ref[...]
        logf = jnp.minimum(z, 0.0) - jnp.log(1.0 + jnp.exp(-jnp.abs(z)))
        r = lax.broadcasted_iota(jnp.int32, (ts, ts), 0)
        c = lax.broadcasted_iota(jnp.int32, (ts, ts), 1)
        tri = jnp.where(r >= c, 1.0, 0.0).astype(BF16)
        cs = _dot3(tri, logf, True) + carry_ref[0:1, :]
        carry_ref[...] = jnp.broadcast_to(cs[ts - 1:ts, :], carry_ref.shape)
        c_ref[...] = cs

    return pl.pallas_call(
        body,
        out_shape=jax.ShapeDtypeStruct((nb * SEQ, LANES), F32),
        grid=(nb, ns),
        in_specs=[pl.BlockSpec((ts, LANES), lambda b, s: (b * ns + s, 0)), pl.BlockSpec((1, LANES), lambda b, s: (0, 0))],
        out_specs=pl.BlockSpec((ts, LANES), lambda b, s: (b * ns + s, 0)),
        scratch_shapes=[pltpu.VMEM((8, LANES), F32)],
        compiler_params=_cparams(dimension_semantics=("parallel", "arbitrary")),
        name=name,
    )(flog, bpad)


def _flog_bwd(dcol, flog, bpad, *, nb, ts, name):
    ns = SEQ // ts

    def body(d_ref, f_ref, b_ref, o_ref, gb_ref, carry_ref):
        bi = pl.program_id(0)
        s = pl.program_id(1)

        @pl.when(s == 0)
        def _():
            carry_ref[...] = jnp.zeros_like(carry_ref)

        @pl.when(jnp.logical_and(bi == 0, s == 0))
        def _():
            gb_ref[...] = jnp.zeros_like(gb_ref)

        r = lax.broadcasted_iota(jnp.int32, (ts, ts), 0)
        c = lax.broadcasted_iota(jnp.int32, (ts, ts), 1)
        tri = jnp.where(r <= c, 1.0, 0.0).astype(BF16)
        rc = _dot3(tri, d_ref[...], True) + carry_ref[0:1, :]
        carry_ref[...] = jnp.broadcast_to(rc[0:1, :], carry_ref.shape)
        z = f_ref[...] + b_ref[...]
        dz = rc / (1.0 + jnp.exp(z))
        o_ref[...] = dz.astype(o_ref.dtype)
        gb_ref[...] += jnp.broadcast_to(jnp.sum(dz, axis=0, keepdims=True), gb_ref.shape)

    rev = lambda b, s: (b * ns + (ns - 1 - s), 0)
    return pl.pallas_call(
        body,
        out_shape=(jax.ShapeDtypeStruct((nb * SEQ, LANES), BF16), jax.ShapeDtypeStruct((8, LANES), F32)),
        grid=(nb, ns),
        in_specs=[pl.BlockSpec((ts, LANES), rev), pl.BlockSpec((ts, LANES), rev), pl.BlockSpec((1, LANES), lambda b, s: (0, 0))],
        out_specs=(pl.BlockSpec((ts, LANES), rev), pl.BlockSpec((8, LANES), lambda b, s: (0, 0))),
        scratch_shapes=[pltpu.VMEM((8, LANES), F32)],
        compiler_params=_cparams(dimension_semantics=("arbitrary", "arbitrary")),
        name=name,
    )(dcol, flog, bpad)


class _AttnCfg:
    def __init__(self, *, e, tq, tk, lq, lk, causal, window, ncol, qcol, kcol, vcol, split_p=False):
        self.e, self.tq, self.tk, self.lq, self.lk = e, tq, tk, lq, lk
        self.split_p = split_p
        self.causal, self.window = causal, window
        self.ncol, self.qcol, self.kcol, self.vcol = ncol, qcol, kcol, vcol
        self.nh = LANES // e
        self.scale = 1.0 / math.sqrt(e)
        self.nq, self.nk = lq // tq, lk // tk

    def k_range(self, i):
        if not self.causal:
            return 0, self.nk
        hi = ((i + 1) * self.tq - 1) // self.tk + 1
        if self.window is None:
            return 0, hi
        return jnp.maximum((i * self.tq - self.window) // self.tk, 0), hi


def _head_masks(nh):
    lane = lax.broadcasted_iota(jnp.int32, (1, LANES), 1)
    return [None] if nh == 1 else [lane < HEAD_DIM, lane >= HEAD_DIM]


def _sel(mask, a, b):
    return a if mask is None else jnp.where(mask, a, b)


def _scores(cfg, qh, kb, q0, k0, dlt0, bias):
    s = lax.dot_general(qh, kb, (((1,), (1,)), ((), ())), preferred_element_type=F32) * cfg.scale
    if bias is not None:
        s = s + bias
    if cfg.causal:
        d = dlt0 + (q0 - k0)
        if cfg.window is None:
            ok = d >= 0
        else:
            ok = d.astype(jnp.uint32) <= jnp.uint32(cfg.window)
        s = jnp.where(ok, s, NEG_INF)
    return s


def _attn_fwd(cfg, q, k, v, *, out_cols, bias=None, state=None, finalize=True, name):
    g = q.shape[0]
    tq, tk, e, nh = cfg.tq, cfg.tk, cfg.e, cfg.nh

    def body(*refs):
        refs = list(refs)
        q_ref, k_ref, v_ref = refs[:3]
        del refs[:3]
        if bias is not None:
            cb_ref, cr_ref = refs[:2]
            del refs[:2]
        if state is not None:
            ai_ref, mi_ref, li_ref = refs[:3]
            del refs[:3]
        out_refs = refs
        masks = _head_masks(nh)
        dlt0 = lax.broadcasted_iota(jnp.int32, (tq, tk), 0) - lax.broadcasted_iota(jnp.int32, (tq, tk), 1)

        def qbody(i, carry):
            q0 = pl.multiple_of(i * tq, tq)
            rows = pl.ds(q0, tq)
            qb = q_ref[rows, :]
            lo, hi = cfg.k_range(i)
            res = []
            for h in range(nh):
                qh = _sel(masks[h], qb, jnp.zeros_like(qb))
                if state is not None:
                    m0 = mi_ref[rows, h * e:h * e + 1]
                    l0 = li_ref[rows, h * e:h * e + 1]
                    a0 = ai_ref[rows, :]
                else:
                    m0 = jnp.full((tq, 1), NEG_INF, F32)
                    l0 = jnp.zeros((tq, 1), F32)
                    a0 = jnp.zeros((tq, LANES), F32)
                cq = cb_ref[rows, h * e:h * e + 1] if bias is not None else None

                def kbody(jk, c, qh=qh, cq=cq, h=h):
                    m, l, a = c
                    k0 = pl.multiple_of(jk * tk, tk)
                    kb = k_ref[pl.ds(k0, tk), :]
                    vb = v_ref[pl.ds(k0, tk), :]
                    b = (cq - cr_ref[jk, h:h + 1, :]) if bias is not None else None
                    s = _scores(cfg, qh, kb, q0, k0, dlt0, b)
                    m_new = jnp.maximum(m, jnp.max(s, axis=1, keepdims=True))
                    alpha = jnp.exp(m - m_new)
                    p = jnp.exp(s - m_new)
                    l = alpha * l + jnp.sum(p, axis=1, keepdims=True)
                    pb = p.astype(BF16)
                    pv = jnp.dot(pb, vb, preferred_element_type=F32)
                    if cfg.split_p:
                        pv = pv + jnp.dot((p - pb.astype(F32)).astype(BF16), vb, preferred_element_type=F32)
                    a = alpha * a + pv
                    return m_new, l, a

                res.append(lax.fori_loop(lo, hi, kbody, (m0, l0, a0)))
            if nh == 1:
                m, l, a = res[0]
                m, l = jnp.broadcast_to(m, (tq, LANES)), jnp.broadcast_to(l, (tq, LANES))
            else:
                m = jnp.where(masks[0], res[0][0], res[1][0])
                l = jnp.where(masks[0], res[0][1], res[1][1])
                a = jnp.where(masks[0], res[0][2], res[1][2])
            if finalize:
                out_refs[0][rows, :] = a / l
                out_refs[1][rows, :] = m + jnp.log(l)
            else:
                out_refs[0][rows, :] = a
                out_refs[1][rows, :] = m
                out_refs[2][rows, :] = l
            return carry

        lax.fori_loop(0, cfg.nq, qbody, 0)

    qspec = pl.BlockSpec((None, cfg.lq, LANES), lambda b, j: (b, 0, cfg.qcol(j)))
    kspec = pl.BlockSpec((None, cfg.lk, LANES), lambda b, j: (b, 0, cfg.kcol(j)))
    vspec = pl.BlockSpec((None, cfg.lk, LANES), lambda b, j: (b, 0, cfg.vcol(j)))
    ospec = pl.BlockSpec((None, cfg.lq, LANES), lambda b, j: (b, 0, j))
    args, in_specs = [q, k, v], [qspec, kspec, vspec]
    if bias is not None:
        args += list(bias)
        in_specs += [ospec, pl.BlockSpec((None, None, cfg.nk, 8, tk), lambda b, j: (b, j, 0, 0, 0))]
    aliases = {}
    if state is not None:
        aliases = {len(args) + t: t for t in range(3 if not finalize else 2)}
        args += list(state)
        in_specs += [ospec] * 3
    n_out = 2 if finalize else 3
    osd = jax.ShapeDtypeStruct((g, cfg.lq, out_cols), F32)
    return pl.pallas_call(
        body,
        out_shape=(osd,) * n_out,
        grid=(g, cfg.ncol),
        in_specs=in_specs,
        out_specs=(ospec,) * n_out,
        input_output_aliases=aliases,
        compiler_params=_cparams(dimension_semantics=("parallel", "parallel")),
        name=name,
    )(*args)


def _attn_bwd(cfg, q, k, v, do, o, lse, *, out_cols, kv_cols, bias=None, acc=None, do_off=0, name):
    g = q.shape[0]
    tq, tk, e, nh = cfg.tq, cfg.tk, cfg.e, cfg.nh
    t0 = (((0,), (0,)), ((), ()))

    def body(*refs):
        refs = list(refs)
        q_ref, k_ref, v_ref, do_ref, o_ref, lse_ref = refs[:6]
        del refs[:6]
        if bias is not None:
            cb_ref, cr_ref = refs[:2]
            del refs[:2]
        if acc is not None:
            dqi_ref, dki_ref, dvi_ref = refs[:3]
            del refs[:3]
        dq_ref, dk_ref, dv_ref = refs[:3]
        dcr_ref = refs[3] if bias is not None else None
        masks = _head_masks(nh)
        dlt0 = lax.broadcasted_iota(jnp.int32, (tq, tk), 0) - lax.broadcasted_iota(jnp.int32, (tq, tk), 1)
        if acc is not None:
            dq_ref[...] = dqi_ref[...]
            dk_ref[...] = dki_ref[...]
            dv_ref[...] = dvi_ref[...]
        else:
            dq_ref[...] = jnp.zeros_like(dq_ref)
            dk_ref[...] = jnp.zeros_like(dk_ref)
            dv_ref[...] = jnp.zeros_like(dv_ref)
        if dcr_ref is not None:
            dcr_ref[...] = jnp.zeros_like(dcr_ref)

        def qbody(i, carry):
            q0 = pl.multiple_of(i * tq, tq)
            rows = pl.ds(q0, tq)
            qb = q_ref[rows, :]
            dob = do_ref[rows, :].astype(BF16)
            prod = dob.astype(F32) * o_ref[rows, :]
            lo, hi = cfg.k_range(i)
            dqs = []
            for h in range(nh):
                qh = _sel(masks[h], qb, jnp.zeros_like(qb))
                doh = _sel(masks[h], dob, jnp.zeros_like(dob))
                lse_h = lse_ref[rows, h * e:h * e + 1]
                delta = jnp.sum(_sel(masks[h], prod, jnp.zeros_like(prod)), axis=1, keepdims=True)
                cq = cb_ref[rows, h * e:h * e + 1] if bias is not None else None

                def kbody(jk, dq_acc, qh=qh, doh=doh, lse_h=lse_h, delta=delta, cq=cq, h=h):
                    k0 = pl.multiple_of(jk * tk, tk)
                    krows = pl.ds(k0, tk)
                    kb = k_ref[krows, :]
                    vb = v_ref[krows, :]
                    b = (cq - cr_ref[jk, h:h + 1, :]) if bias is not None else None
                    s = _scores(cfg, qh, kb, q0, k0, dlt0, b)
                    p = jnp.exp(s - lse_h)
                    dp = lax.dot_general(doh, vb, (((1,), (1,)), ((), ())), preferred_element_type=F32)
                    ds = p * (dp - delta)
                    if dcr_ref is not None:
                        dcr_ref[jk, h:h + 1, :] += jnp.sum(ds, axis=0, keepdims=True)
                    dsb = (ds * cfg.scale).astype(BF16)
                    dv_ref[krows, :] += lax.dot_general(p.astype(BF16), doh, t0, preferred_element_type=F32)
                    dk_ref[krows, :] += lax.dot_general(dsb, qh, t0, preferred_element_type=F32)
                    return dq_acc + jnp.dot(dsb, kb, preferred_element_type=F32)

                dqs.append(lax.fori_loop(lo, hi, kbody, jnp.zeros((tq, LANES), F32)))
            dq = dqs[0] if nh == 1 else jnp.where(masks[0], dqs[0], dqs[1])
            dq_ref[rows, :] += dq
            return carry

        lax.fori_loop(0, cfg.nq, qbody, 0)

    qspec = pl.BlockSpec((None, cfg.lq, LANES), lambda b, j: (b, 0, cfg.qcol(j)))
    kspec = pl.BlockSpec((None, cfg.lk, LANES), lambda b, j: (b, 0, cfg.kcol(j)))
    vspec = pl.BlockSpec((None, cfg.lk, LANES), lambda b, j: (b, 0, cfg.vcol(j)))
    ospec = pl.BlockSpec((None, cfg.lq, LANES), lambda b, j: (b, 0, j))
    kvspec = pl.BlockSpec((None, cfg.lk, LANES), lambda b, j: (b, 0, j))
    dospec = pl.BlockSpec((None, cfg.lq, LANES), lambda b, j: (b, 0, do_off + j))
    args, in_specs = [q, k, v, do, o, lse], [qspec, kspec, vspec, dospec, ospec, ospec]
    out_shape = [jax.ShapeDtypeStruct((g, cfg.lq, out_cols), F32), jax.ShapeDtypeStruct((g, cfg.lk, kv_cols), F32),
                 jax.ShapeDtypeStruct((g, cfg.lk, kv_cols), F32)]
    out_specs = [ospec, kvspec, kvspec]
    if bias is not None:
        args += list(bias)
        crspec = pl.BlockSpec((None, None, cfg.nk, 8, tk), lambda b, j: (b, j, 0, 0, 0))
        in_specs += [ospec, crspec]
        out_shape.append(jax.ShapeDtypeStruct((g, cfg.ncol, cfg.nk, 8, tk), F32))
        out_specs.append(crspec)
    aliases = {}
    if acc is not None:
        aliases = {len(args) + t: t for t in range(3)}
        args += list(acc)
        in_specs += [ospec, kvspec, kvspec]
    return pl.pallas_call(
        body,
        out_shape=tuple(out_shape),
        grid=(g, cfg.ncol),
        in_specs=in_specs,
        out_specs=tuple(out_specs),
        input_output_aliases=aliases,
        compiler_params=_cparams(dimension_semantics=("parallel", "parallel")),
        name=name,
    )(*args)


BLK = 128
NBLK = SEQ // BLK
QK_SCALE = 1.0 / math.sqrt(HEAD_DIM)
DIL_STEPS = tuple(d for _, d in DILATIONS)
assert all(w // d == BLK for w, d in DILATIONS)
_T0 = (((0,), (0,)), ((), ()))
_NT = (((1,), (1,)), ((), ()))


def _stack_heads(a, masks):
    z = jnp.zeros_like(a)
    return jnp.concatenate([jnp.where(masks[0], a, z), jnp.where(masks[1], a, z)], axis=0)


def _tri_bias(lower):
    r = lax.broadcasted_iota(jnp.int32, (BLK, BLK), 0)
    c = lax.broadcasted_iota(jnp.int32, (BLK, BLK), 1)
    return jnp.where((c <= r) if lower else (c >= r), 0.0, NEG_INF).astype(F32)


def _dil_rows(r, i, d):
    start = r + i * (BLK * d)
    return pl.ds(start, BLK) if d == 1 else pl.ds(start, BLK, stride=d)


DIL_SET = 4


def _dil_sets(d, fn):
    nbk = SEQ // d // BLK
    if d == 1:
        def gbody(g, c):
            fn([(0, DIL_SET * g + a, None if a == 0 else True) for a in range(DIL_SET)])
            return c
        lax.fori_loop(0, nbk // DIL_SET, gbody, 0)
    elif nbk > 1:
        assert nbk == DIL_SET
        def rbody(r, c):
            fn([(r, i, i > 0) for i in range(nbk)])
            return c
        lax.fori_loop(0, d, rbody, 0)
    else:
        def rbody(rr, c):
            fn([(DIL_SET * rr + a, 0, False) for a in range(DIL_SET)])
            return c
        lax.fori_loop(0, d // DIL_SET, rbody, 0)


def _dil_key_tiles(r, i, d, has_prev, qrows, tri_cur, tri_prev):
    tiles = [(qrows, tri_cur)]
    if has_prev is None:
        tiles.append((_dil_rows(r, jnp.maximum(i - 1, 0), d), tri_prev + jnp.where(i > 0, 0.0, NEG_INF)))
    elif has_prev:
        tiles.append((_dil_rows(r, i - 1, d), tri_prev))
    return tiles


def _dil_fwd(qkv, *, name):
    nb = qkv.shape[0]
    ncol = DIL_W // LANES
    hd = HEAD_DIM

    def body(q_ref, k_ref, v_ref, o_ref, lse_ref, m_ref, l_ref, a_ref):
        masks = _head_masks(2)
        tri_cur, tri_prev = _tri_bias(True), _tri_bias(False)
        for pi, d in enumerate(DIL_STEPS):
            first, last = pi == 0, pi == len(DIL_STEPS) - 1

            def qset(blocks, d=d, first=first, last=last):
                work = []
                for r, i, has_prev in blocks:
                    qrows = _dil_rows(r, i, d)
                    qcat = _stack_heads((q_ref[qrows, :] * QK_SCALE).astype(BF16), masks)
                    ss, krs = [], []
                    for krows, bias in _dil_key_tiles(r, i, d, has_prev, qrows, tri_cur, tri_prev):
                        s = lax.dot_general(qcat, k_ref[krows, :].astype(BF16), _NT, preferred_element_type=F32)
                        ss.append((s[:BLK] + bias, s[BLK:] + bias))
                        krs.append(krows)
                    work.append((qrows, ss, krs))
                for qrows, ss, krs in work:
                    e0 = ss[0][0] if len(ss) == 1 else jnp.maximum(ss[0][0], ss[1][0])
                    e1 = ss[0][1] if len(ss) == 1 else jnp.maximum(ss[0][1], ss[1][1])
                    n0 = jnp.max(e0, axis=1, keepdims=True)
                    n1 = jnp.max(e1, axis=1, keepdims=True)
                    if not first:
                        mo, lo = m_ref[qrows, :], l_ref[qrows, :]
                        m0, m1 = mo[:, 0:1], mo[:, hd:hd + 1]
                        n0, n1 = jnp.maximum(n0, m0), jnp.maximum(n1, m1)
                        a0, a1 = jnp.exp(m0 - n0), jnp.exp(m1 - n1)
                    ps = [(jnp.exp(s0 - n0), jnp.exp(s1 - n1)) for s0, s1 in ss]
                    t0 = ps[0][0] if len(ps) == 1 else ps[0][0] + ps[1][0]
                    t1 = ps[0][1] if len(ps) == 1 else ps[0][1] + ps[1][1]
                    l0 = jnp.sum(t0, axis=1, keepdims=True)
                    l1 = jnp.sum(t1, axis=1, keepdims=True)
                    acc = None
                    for (p0, p1), krows in zip(ps, krs):
                        vcat = _stack_heads(v_ref[krows, :].astype(BF16), masks)
                        pv = jnp.dot(jnp.concatenate([p0, p1], axis=1).astype(BF16), vcat, preferred_element_type=F32)
                        acc = pv if acc is None else acc + pv
                    if not first:
                        l0 = l0 + a0 * lo[:, 0:1]
                        l1 = l1 + a1 * lo[:, hd:hd + 1]
                        acc = acc + a_ref[qrows, :] * jnp.where(masks[0], a0, a1)
                    if last:
                        o_ref[qrows, :] = acc / jnp.where(masks[0], l0, l1)
                        lse_ref[qrows, :] = jnp.where(masks[0], n0 + jnp.log(l0), n1 + jnp.log(l1))
                    else:
                        m_ref[qrows, :] = jnp.where(masks[0], n0, n1)
                        l_ref[qrows, :] = jnp.where(masks[0], l0, l1)
                        a_ref[qrows, :] = acc

            _dil_sets(d, qset)

    spec = lambda off: pl.BlockSpec((None, SEQ, LANES), lambda b, j: (b, 0, off + j))
    ospec = pl.BlockSpec((None, SEQ, LANES), lambda b, j: (b, 0, j))
    osd = jax.ShapeDtypeStruct((nb, SEQ, DIL_W), F32)
    return pl.pallas_call(
        body, out_shape=(osd, osd), grid=(nb, ncol),
        in_specs=[spec(0), spec(ncol), spec(2 * ncol)], out_specs=(ospec, ospec),
        scratch_shapes=[pltpu.VMEM((SEQ, LANES), F32)] * 3,
        compiler_params=_cparams(dimension_semantics=("parallel", "parallel")), name=name,
    )(qkv, qkv, qkv)


def _dil_bwd(qkv, do, o, lse, tabs, *, do_off, name):
    nb = qkv.shape[0]
    ncol = DIL_W // LANES
    hd = HEAD_DIM

    def body(q_ref, k_ref, v_ref, do_ref, o_ref, lse_ref, c_ref, s1_ref, s2_ref, dqo_ref, dko_ref, dvo_ref,
             dq_ref, dk_ref, dv_ref, dl_ref, dof_ref):
        masks = _head_masks(2)
        tri_cur, tri_prev = _tri_bias(True), _tri_bias(False)
        dq_ref[...] = jnp.zeros_like(dq_ref)
        dk_ref[...] = jnp.zeros_like(dk_ref)
        dv_ref[...] = jnp.zeros_like(dv_ref)

        def delta_body(i, c):
            rows = pl.ds(pl.multiple_of(i * BLK, BLK), BLK)
            dof = do_ref[rows, :].astype(F32)
            dof_ref[rows, :] = dof
            prod = dof * o_ref[rows, :]
            z = jnp.zeros_like(prod)
            dl_ref[rows, :] = jnp.where(masks[0], jnp.sum(jnp.where(masks[0], prod, z), axis=1, keepdims=True),
                                        jnp.sum(jnp.where(masks[1], prod, z), axis=1, keepdims=True))
            return c

        lax.fori_loop(0, NBLK, delta_body, 0)

        for d in DIL_STEPS:
            def qset(blocks, d=d):
                work = []
                for r, i, has_prev in blocks:
                    qrows = _dil_rows(r, i, d)
                    qcat = _stack_heads((q_ref[qrows, :] * QK_SCALE).astype(BF16), masks)
                    docat = _stack_heads(dof_ref[qrows, :].astype(BF16), masks)
                    tiles = []
                    for krows, bias in _dil_key_tiles(r, i, d, has_prev, qrows, tri_cur, tri_prev):
                        s = lax.dot_general(qcat, k_ref[krows, :].astype(BF16), _NT, preferred_element_type=F32)
                        dp = lax.dot_general(docat, v_ref[krows, :].astype(BF16), _NT, preferred_element_type=F32)
                        tiles.append((krows, s, dp, bias))
                    work.append((qrows, qcat, docat, tiles))
                for qrows, qcat, docat, tiles in work:
                    lseb, dlb = lse_ref[qrows, :], dl_ref[qrows, :]
                    lse0, lse1 = lseb[:, 0:1], lseb[:, hd:hd + 1]
                    dl0, dl1 = dlb[:, 0:1], dlb[:, hd:hd + 1]
                    dq = None
                    for krows, s, dp, bias in tiles:
                        p0 = jnp.exp(s[:BLK] + bias - lse0)
                        p1 = jnp.exp(s[BLK:] + bias - lse1)
                        ds0 = p0 * (dp[:BLK] - dl0)
                        ds1 = p1 * (dp[BLK:] - dl1)
                        pcat = jnp.concatenate([p0, p1], axis=0).astype(BF16)
                        dscat = jnp.concatenate([ds0, ds1], axis=0).astype(BF16)
                        dv_ref[krows, :] += lax.dot_general(pcat, docat, _T0, preferred_element_type=F32)
                        dk_ref[krows, :] += lax.dot_general(dscat, qcat, _T0, preferred_element_type=F32)
                        dsrow = jnp.concatenate([ds0, ds1], axis=1).astype(BF16)
                        kcat = _stack_heads((k_ref[krows, :] * QK_SCALE).astype(BF16), masks)
                        t = jnp.dot(dsrow, kcat, preferred_element_type=F32)
                        dq = t if dq is None else dq + t
                    dq_ref[qrows, :] += dq

            _dil_sets(d, qset)

        def out_body(i, c):
            rows = pl.ds(pl.multiple_of(i * BLK, BLK), BLK)
            tab = (c_ref[rows, :], s1_ref[rows, :], s2_ref[rows, :])
            dqo_ref[rows, :] = _rope_apply(dq_ref[rows, :], *tab, transpose=True).astype(dqo_ref.dtype)
            dko_ref[rows, :] = _rope_apply(dk_ref[rows, :], *tab, transpose=True).astype(dko_ref.dtype)
            dvo_ref[rows, :] = dv_ref[rows, :].astype(dvo_ref.dtype)
            return c

        lax.fori_loop(0, NBLK, out_body, 0)

    spec = lambda off: pl.BlockSpec((None, SEQ, LANES), lambda b, j: (b, 0, off + j))
    ospec = pl.BlockSpec((None, SEQ, LANES), lambda b, j: (b, 0, j))
    tspec = pl.BlockSpec((SEQ, LANES), lambda b, j: (0, 0))
    osd = jax.ShapeDtypeStruct((nb, SEQ, DIL_W), BF16)
    return pl.pallas_call(
        body, out_shape=(osd, osd, osd), grid=(nb, ncol),
        in_specs=[spec(0), spec(ncol), spec(2 * ncol), spec(do_off), ospec, ospec, tspec, tspec, tspec],
        out_specs=(ospec, ospec, ospec),
        scratch_shapes=[pltpu.VMEM((SEQ, LANES), F32)] * 5,
        compiler_params=_cparams(dimension_semantics=("parallel", "parallel")), name=name,
    )(qkv, qkv, qkv, do, o, lse, *tabs)


FOX_GROUP = 4
assert NBLK % FOX_GROUP == 0
_FOX_COLS = tuple(c // LANES for c in (C_FQ, C_FK, C_FV))


def _fox_specs():
    cols = [pl.BlockSpec((None, SEQ, LANES), (lambda b, j, off=off: (b, 0, off + j))) for off in _FOX_COLS]
    ospec = pl.BlockSpec((None, SEQ, LANES), lambda b, j: (b, 0, j))
    crspec = pl.BlockSpec((None, None, NBLK, 8, BLK), lambda b, j: (b, j, 0, 0, 0))
    return cols, ospec, crspec


def _fox_key_rows(t, e):
    return pl.ds(pl.multiple_of((FOX_GROUP * t + e) * BLK, BLK), BLK)


def _fox_fwd(p3, crow, *, name):
    nb = p3.shape[0]
    g = FOX_GROUP

    def body(q_ref, k_ref, v_ref, cr_ref, o_ref, lse_ref):
        masks = _head_masks(2)
        tri = _tri_bias(True)

        def qk(qcat, t):
            return tuple(lax.dot_general(qcat, k_ref[_fox_key_rows(t, e), :], _NT, preferred_element_type=F32) for e in range(g))

        def consume(ss, t, state, nblk, diag):
            m0, m1, l0, l1, acc = state
            us = []
            for e in range(nblk):
                cr = cr_ref[g * t + e]
                u0 = ss[e][:BLK] - cr[0:1, :]
                u1 = ss[e][BLK:] - cr[1:2, :]
                if diag and e == nblk - 1:
                    u0, u1 = u0 + tri, u1 + tri
                us.append((u0, u1))
            x0 = functools.reduce(jnp.maximum, [u[0] for u in us])
            x1 = functools.reduce(jnp.maximum, [u[1] for u in us])
            n0 = jnp.maximum(m0, jnp.max(x0, axis=1, keepdims=True))
            n1 = jnp.maximum(m1, jnp.max(x1, axis=1, keepdims=True))
            a0, a1 = jnp.exp(m0 - n0), jnp.exp(m1 - n1)
            acc = acc * jnp.where(masks[0], a0, a1)
            t0 = t1 = None
            for e in range(nblk):
                p0, p1 = jnp.exp(us[e][0] - n0), jnp.exp(us[e][1] - n1)
                t0 = p0 if t0 is None else t0 + p0
                t1 = p1 if t1 is None else t1 + p1
                pcat = jnp.concatenate([p0, p1], axis=1)
                hi = pcat.astype(BF16)
                lo = (pcat - hi.astype(F32)).astype(BF16)
                vcat = _stack_heads(v_ref[_fox_key_rows(t, e), :], masks)
                acc = acc + jnp.dot(hi, vcat, preferred_element_type=F32) + jnp.dot(lo, vcat, preferred_element_type=F32)
            l0 = a0 * l0 + jnp.sum(t0, axis=1, keepdims=True)
            l1 = a1 * l1 + jnp.sum(t1, axis=1, keepdims=True)
            return n0, n1, l0, l1, acc

        def gbody(ng, c):
            neg = jnp.full((BLK, 1), NEG_INF, F32)
            z1 = jnp.zeros((BLK, 1), F32)
            rows = [pl.ds(pl.multiple_of((g * ng + a) * BLK, BLK), BLK) for a in range(g)]
            qcats = [_stack_heads(q_ref[rows[a], :] * QK_SCALE, masks) for a in range(g)]
            first = [qk(qcats[a], 0) for a in range(g)]
            done = []
            for a in range(g):
                def step(t, cc, qcat=qcats[a]):
                    ss, st = cc
                    nxt = qk(qcat, t + 1)
                    return nxt, consume(ss, t, st, g, False)

                done.append(lax.fori_loop(0, ng, step, (first[a], (neg, neg, z1, z1, jnp.zeros((BLK, LANES), F32)))))
            for a in range(g):
                ss, state = done[a]
                m0, m1, l0, l1, acc = consume(ss, ng, state, a + 1, True)
                o_ref[rows[a], :] = acc / jnp.where(masks[0], l0, l1)
                lse_ref[rows[a], :] = jnp.where(masks[0], m0 + jnp.log(l0), m1 + jnp.log(l1))
            return c

        lax.fori_loop(0, NBLK // g, gbody, 0)

    cols, ospec, crspec = _fox_specs()
    osd = jax.ShapeDtypeStruct((nb, SEQ, FOX_W), F32)
    return pl.pallas_call(
        body, out_shape=(osd, osd), grid=(nb, FOX_W // LANES), in_specs=cols + [crspec], out_specs=(ospec, ospec),
        compiler_params=_cparams(dimension_semantics=("parallel", "parallel")), name=name,
    )(p3, p3, p3, crow)


def _fox_bwd(p3, crow, do, o, lse, *, do_off, name):
    nb = p3.shape[0]
    g = FOX_GROUP
    hd = HEAD_DIM

    def body(q_ref, k_ref, v_ref, cr_ref, do_ref, o_ref, lse_ref, dq_ref, dko_ref, dvo_ref, dcr_ref, dk_ref, dv_ref):
        masks = _head_masks(2)
        tri = _tri_bias(True)
        dk_ref[...] = jnp.zeros_like(dk_ref)
        dv_ref[...] = jnp.zeros_like(dv_ref)
        dcr_ref[...] = jnp.zeros_like(dcr_ref)

        def products(qcat, docat, t):
            out = []
            for e in range(g):
                krows = _fox_key_rows(t, e)
                out.append(lax.dot_general(qcat, k_ref[krows, :], _NT, preferred_element_type=F32))
                out.append(lax.dot_general(docat, v_ref[krows, :], _NT, preferred_element_type=F32))
            return tuple(out)

        def consume(prod, t, ctx, dq, nblk, diag):
            qcat, docat, lse0, lse1, dl0, dl1 = ctx
            for e in range(nblk):
                jb = g * t + e
                krows = _fox_key_rows(t, e)
                s, dp = prod[2 * e], prod[2 * e + 1]
                cr = cr_ref[jb]
                u0 = s[:BLK] - cr[0:1, :]
                u1 = s[BLK:] - cr[1:2, :]
                if diag and e == nblk - 1:
                    u0, u1 = u0 + tri, u1 + tri
                p0 = jnp.exp(u0 - lse0)
                p1 = jnp.exp(u1 - lse1)
                ds0 = p0 * (dp[:BLK] - dl0)
                ds1 = p1 * (dp[BLK:] - dl1)
                dcr_ref[jb, 0:1, :] += jnp.sum(ds0, axis=0, keepdims=True)
                dcr_ref[jb, 1:2, :] += jnp.sum(ds1, axis=0, keepdims=True)
                pcat = jnp.concatenate([p0, p1], axis=0).astype(BF16)
                dscat = jnp.concatenate([ds0, ds1], axis=0).astype(BF16)
                dv_ref[krows, :] += lax.dot_general(pcat, docat, _T0, preferred_element_type=F32)
                dk_ref[krows, :] += lax.dot_general(dscat, qcat, _T0, preferred_element_type=F32)
                dsrow = jnp.concatenate([ds0, ds1], axis=1).astype(BF16)
                dq = dq + jnp.dot(dsrow, _stack_heads(k_ref[krows, :] * QK_SCALE, masks), preferred_element_type=F32)
            return dq

        def gbody(ng, c):
            ctxs, rows = [], []
            for a in range(g):
                r = pl.ds(pl.multiple_of((g * ng + a) * BLK, BLK), BLK)
                qcat = _stack_heads(q_ref[r, :] * QK_SCALE, masks)
                dob = do_ref[r, :].astype(BF16)
                prod = dob.astype(F32) * o_ref[r, :]
                z = jnp.zeros_like(prod)
                dl0 = jnp.sum(jnp.where(masks[0], prod, z), axis=1, keepdims=True)
                dl1 = jnp.sum(jnp.where(masks[1], prod, z), axis=1, keepdims=True)
                lseb = lse_ref[r, :]
                ctxs.append((qcat, _stack_heads(dob, masks), lseb[:, 0:1], lseb[:, hd:hd + 1], dl0, dl1))
                rows.append(r)
            first = [products(ctxs[a][0], ctxs[a][1], 0) for a in range(g)]
            done = []
            for a in range(g):
                def step(t, cc, ctx=ctxs[a]):
                    pr, dq = cc
                    nxt = products(ctx[0], ctx[1], t + 1)
                    return nxt, consume(pr, t, ctx, dq, g, False)

                done.append(lax.fori_loop(0, ng, step, (first[a], jnp.zeros((BLK, LANES), F32))))
            for a in range(g):
                pr, dq = done[a]
                dq_ref[rows[a], :] = consume(pr, ng, ctxs[a], dq, a + 1, True).astype(dq_ref.dtype)
            return c

        lax.fori_loop(0, NBLK // g, gbody, 0)
        dko_ref[...] = dk_ref[...].astype(dko_ref.dtype)
        dvo_ref[...] = dv_ref[...].astype(dvo_ref.dtype)

    cols, ospec, crspec = _fox_specs()
    dospec = pl.BlockSpec((None, SEQ, LANES), lambda b, j: (b, 0, do_off + j))
    osd = jax.ShapeDtypeStruct((nb, SEQ, FOX_W), BF16)
    return pl.pallas_call(
        body, out_shape=(osd, osd, osd, jax.ShapeDtypeStruct((nb, FOX_W // LANES, NBLK, 8, BLK), F32)),
        grid=(nb, FOX_W // LANES), in_specs=cols + [crspec, dospec, ospec, ospec], out_specs=(ospec, ospec, ospec, crspec),
        scratch_shapes=[pltpu.VMEM((SEQ, LANES), F32)] * 2,
        compiler_params=_cparams(dimension_semantics=("parallel", "parallel")), name=name,
    )(p3, p3, p3, crow, do, o, lse)


def _mem_cfg():
    return _AttnCfg(e=MEM_HEAD_DIM, tq=256, tk=MEM_LEN, lq=SEQ, lk=MEM_LEN, causal=False, window=None, ncol=MEM_HEADS,
                    qcol=lambda j: C_MQ // LANES + j, kcol=lambda j: j, vcol=lambda j: MEM_HEADS + j)


_B1, _B2 = FOX_W // LANES, (FOX_W + DIL_W) // LANES


def _dy_gate_bwd(dx2b, wo, fox, dil, memo, p16, *, tm, tn, name):
    t, d = dx2b.shape
    assert FOX_W % tn == 0 and DIL_W % tn == 0 and MEM_W % tn == 0 and all(c % tn == 0 for c in (C_FG, C_DG, C_MG))
    n1, n2, n3 = FOX_W // tn, (FOX_W + DIL_W) // tn, MIX_W // tn

    def body(dx_ref, w_ref, f_ref, d_ref, m_ref, g_ref, da_ref, dg_ref):
        j = pl.program_id(1)
        dyv = lax.dot_general(dx_ref[...], w_ref[...], _NT, preferred_element_type=F32)
        a = jnp.where(j < n1, f_ref[...], jnp.where(j < n2, d_ref[...], m_ref[...]))
        gt = g_ref[...].astype(F32)
        sg = 1.0 / (1.0 + jnp.exp(-gt))
        da_ref[...] = (dyv * gt * sg).astype(da_ref.dtype)
        dg_ref[...] = (dyv * a * sg * (1.0 + gt * (1.0 - sg))).astype(dg_ref.dtype)

    def gcol(j):
        return jnp.where(j < n1, C_FG // tn + j, jnp.where(j < n2, C_DG // tn + j - n1, C_MG // tn + j - n2))

    tile = pl.BlockSpec((tm, tn), lambda i, j: (i, j))
    return pl.pallas_call(
        body,
        out_shape=(jax.ShapeDtypeStruct((t, MIX_W), BF16), jax.ShapeDtypeStruct((t, MIX_W), BF16)),
        grid=(t // tm, n3),
        in_specs=[pl.BlockSpec((tm, d), lambda i, j: (i, 0)), pl.BlockSpec((tn, d), lambda i, j: (j, 0)),
                  pl.BlockSpec((tm, tn), lambda i, j: (i, jnp.minimum(j, n1 - 1))),
                  pl.BlockSpec((tm, tn), lambda i, j: (i, jnp.clip(j - n1, 0, n2 - n1 - 1))),
                  pl.BlockSpec((tm, tn), lambda i, j: (i, jnp.clip(j - n2, 0, n3 - n2 - 1))),
                  pl.BlockSpec((tm, tn), lambda i, j: (i, gcol(j)))],
        out_specs=(tile, tile),
        compiler_params=_cparams(dimension_semantics=("parallel", "parallel")),
        name=name,
    )(dx2b, wo, fox, dil, memo, p16)


def _silu(g):
    return g / (1.0 + jnp.exp(-g))


def _out_loss(fox, dil, memo, p16, wo, x, tgt, gfin, *, tm, name):
    t, d = x.shape
    n_feat = float(d)

    def body(f_ref, d_ref, m_ref, fg_ref, dg_ref, mg_ref, w_ref, x_ref, t_ref, g_ref, y_ref, dx_ref, dxb_ref, st_ref):
        i = pl.program_id(0)

        @pl.when(i == 0)
        def _():
            st_ref[...] = jnp.zeros_like(st_ref)

        y = jnp.concatenate([(a_ref[...] * _silu(gt_ref[...].astype(F32))).astype(BF16)
                             for a_ref, gt_ref in ((f_ref, fg_ref), (d_ref, dg_ref), (m_ref, mg_ref))], axis=1)
        y_ref[...] = y
        x2 = x_ref[...] + jnp.dot(y, w_ref[...], preferred_element_type=F32)
        r = lax.rsqrt(jnp.mean(x2 * x2, axis=-1, keepdims=True) + RMS_EPS)
        nrm = x2 * r
        gv = g_ref[...]
        err = nrm * gv - t_ref[...]
        dout = err * (1.0 / n_feat)
        dn = dout * gv
        dx2 = r * (dn - nrm * jnp.mean(dn * nrm, axis=-1, keepdims=True))
        dx_ref[...] = dx2
        dxb_ref[...] = dx2.astype(dxb_ref.dtype)
        st_ref[0:1, :] += jnp.sum(dout * nrm, axis=0, keepdims=True)
        st_ref[1:2, :] += (0.5 / n_feat) * jnp.sum(err * err, axis=0, keepdims=True)

    row = pl.BlockSpec((tm, d), lambda i: (i, 0))
    whole = lambda w: pl.BlockSpec((tm, w), lambda i: (i, 0))
    gate = lambda w, col: pl.BlockSpec((tm, w), lambda i: (i, col // w))
    return pl.pallas_call(
        body,
        out_shape=(jax.ShapeDtypeStruct((t, MIX_W), BF16), jax.ShapeDtypeStruct((t, d), F32), jax.ShapeDtypeStruct((t, d), BF16),
                   jax.ShapeDtypeStruct((8, d), F32)),
        grid=(t // tm,),
        in_specs=[whole(FOX_W), whole(DIL_W), whole(MEM_W), gate(FOX_W, C_FG), gate(DIL_W, C_DG), gate(MEM_W, C_MG),
                  pl.BlockSpec((MIX_W, d), lambda i: (0, 0)), row, row, pl.BlockSpec((1, d), lambda i: (0, 0))],
        out_specs=(pl.BlockSpec((tm, MIX_W), lambda i: (i, 0)), row, row, pl.BlockSpec((8, d), lambda i: (0, 0))),
        compiler_params=_cparams(dimension_semantics=("arbitrary",)),
        name=name,
    )(fox, dil, memo, p16, p16, p16, wo, x, tgt, gfin)


def _dh_rms_bwd(dp, w, x, g, resid, *, tm, tk, name):
    t, d = x.shape
    kdim = dp.shape[1]
    nk = kdim // tk

    def body(*refs):
        if resid is not None:
            dp_ref, w_ref, x_ref, g_ref, r_ref, dx_ref, gg_ref, acc_ref = refs
        else:
            dp_ref, w_ref, x_ref, g_ref, dx_ref, gg_ref, acc_ref = refs
        i = pl.program_id(0)
        k = pl.program_id(1)

        @pl.when(jnp.logical_and(i == 0, k == 0))
        def _():
            gg_ref[...] = jnp.zeros_like(gg_ref)

        @pl.when(k == 0)
        def _():
            acc_ref[...] = jnp.zeros_like(acc_ref)

        acc_ref[...] += lax.dot_general(dp_ref[...], w_ref[...], _NT, preferred_element_type=F32)

        @pl.when(k == nk - 1)
        def _():
            dh = acc_ref[...]
            xv = x_ref[...]
            r = lax.rsqrt(jnp.mean(xv * xv, axis=-1, keepdims=True) + RMS_EPS)
            nrm = xv * r
            dn = dh * g_ref[...]
            dx = r * (dn - nrm * jnp.mean(dn * nrm, axis=-1, keepdims=True))
            if resid is not None:
                dx = dx + r_ref[...]
            dx_ref[...] = dx
            gg_ref[0:1, :] += jnp.sum(dh * nrm, axis=0, keepdims=True)

    row = pl.BlockSpec((tm, d), lambda i, k: (i, 0))
    in_specs = [pl.BlockSpec((tm, tk), lambda i, k: (i, k)), pl.BlockSpec((d, tk), lambda i, k: (0, k)), row,
                pl.BlockSpec((1, d), lambda i, k: (0, 0))]
    args = [dp, w, x, g]
    if resid is not None:
        in_specs.append(row)
        args.append(resid)
    return pl.pallas_call(
        body,
        out_shape=(jax.ShapeDtypeStruct((t, d), F32), jax.ShapeDtypeStruct((8, d), F32)),
        grid=(t // tm, nk),
        in_specs=in_specs,
        out_specs=(row, pl.BlockSpec((8, d), lambda i, k: (0, 0))),
        scratch_shapes=[pltpu.VMEM((tm, d), F32)],
        compiler_params=_cparams(dimension_semantics=("arbitrary", "arbitrary")),
        name=name,
    )(*args)


_FLOG0 = 4 * FOX_W
_W_IN_SEGMENTS = ((0, _FLOG0, 0), (_FLOG0, _FLOG0 + FOX_HEADS, PW), (_FLOG0 + FOX_HEADS, IN_W, C_DQ))
SHARD_W = IN_W // N_CHIPS


def _rearrange_w_in(shards):
    def cols(lo, hi):
        parts = []
        for k in range(N_CHIPS):
            a, b = max(lo, k * SHARD_W), min(hi, (k + 1) * SHARD_W)
            if a < b:
                parts.append(shards[k][:, a - k * SHARD_W:b - k * SHARD_W])
        return parts

    (a0, a1, _), (f0, f1, _), (b0, b1, _) = _W_IN_SEGMENTS
    pad = jnp.zeros((shards[0].shape[0], LANES - FOX_HEADS), shards[0].dtype)
    return jnp.concatenate(cols(a0, a1) + cols(b0, b1) + cols(f0, f1) + [pad], axis=1)


def _w_in_grad_slabs(g):
    slabs = []
    for k in range(N_CHIPS):
        parts = []
        for lo, hi, at in _W_IN_SEGMENTS:
            a, b = max(lo, k * SHARD_W), min(hi, (k + 1) * SHARD_W)
            if a < b:
                parts.append(g[:, at + a - lo:at + b - lo])
        slabs.append(jnp.concatenate(parts, axis=1))
    return jnp.stack(slabs, axis=0)


def _local_grads(x, mem, norm_g, w_r, b_forget, mem_norm_g, w_kv, w_o, final_norm_g, tgt, start_reduce=None):
    nb = x.shape[0]
    t = nb * SEQ
    x2d = x.reshape(t, D_MODEL)
    tgt2d = tgt.reshape(t, D_MODEL)
    tabs = _rope_tables()
    bpad = jnp.pad(b_forget.reshape(1, FOX_HEADS), ((0, 0), (0, LANES - FOX_HEADS)))

    h = _rms_fwd(x2d, norm_g.reshape(1, D_MODEL), tm=512, name="rms_x")
    p16, dqkv = _proj(h, w_r, tabs, n=PW, tm=2048, tn=256, name="proj")
    flog = _matmul(h, w_r[:, PW:], out_dtype=F32, tm=1024, tn=LANES, tk=D_MODEL, name="proj_flog")
    c12 = _flog_fwd(flog, bpad, nb=nb, ts=256, name="flog_fwd")

    crow = c12[:, :FOX_HEADS].reshape(nb, NBLK, BLK, FOX_HEADS // 2, 2).transpose(0, 3, 1, 4, 2)
    crow = jnp.pad(crow, ((0, 0), (0, 0), (0, 0), (0, 6), (0, 0)))
    p3 = p16.reshape(nb, SEQ, PW)
    fox, fox_lse = _fox_fwd(p3, crow, name="fox_fwd")

    dqkv3 = dqkv.reshape(nb, SEQ, 3 * DIL_W)
    dil, dil_lse = _dil_fwd(dqkv3, name="dil_fwd")

    mh = _rms_fwd(mem.reshape(nb * MEM_LEN, D_MODEL), mem_norm_g.reshape(1, D_MODEL), tm=nb * MEM_LEN, name="rms_mem")
    mkv = _matmul(mh, w_kv, out_dtype=BF16, tm=nb * MEM_LEN, tn=512, tk=D_MODEL, name="mem_kv")
    mkv3 = mkv.reshape(nb, MEM_LEN, 2 * MEM_W)
    mcfg = _mem_cfg()
    memo, mem_lse = _attn_fwd(mcfg, p3, mkv3, mkv3, out_cols=MEM_W, name="mem_fwd")

    fox2, dil2, memo2 = fox.reshape(t, FOX_W), dil.reshape(t, DIL_W), memo.reshape(t, MEM_W)
    y, dx2, dx2b, st = _out_loss(fox2, dil2, memo2, p16, w_o, x2d, tgt2d, final_norm_g.reshape(1, D_MODEL), tm=256,
                                 name="out_loss")

    g_wo = _matmul(y, dx2b, mode="tn", out_dtype=F32, tm=1024, tn=512, tk=1024, name="grad_w_out")
    datt, dgate = _dy_gate_bwd(dx2b, w_o, fox2, dil2, memo2, p16, tm=1024, tn=256, name="dy_gate_bwd")
    datt3 = datt.reshape(nb, SEQ, MIX_W)

    dfq, dfk, dfv, dcr = _fox_bwd(p3, crow, datt3, fox, fox_lse, do_off=0, name="fox_bwd")
    dcol = -dcr[:, :, :, :2, :].transpose(0, 2, 4, 1, 3).reshape(t, FOX_HEADS)
    dcol = jnp.pad(dcol, ((0, 0), (0, LANES - FOX_HEADS)))
    dflog, gb = _flog_bwd(dcol, flog, bpad, nb=nb, ts=256, name="flog_bwd")

    ddq, ddk, ddv = _dil_bwd(dqkv3, datt3, dil, dil_lse, tabs, do_off=_B1, name="dil_bwd")

    dmq, dmk, dmv = _attn_bwd(mcfg, p3, mkv3, mkv3, datt3, memo, mem_lse, out_cols=MEM_W, kv_cols=MEM_W, do_off=_B2,
                              name="mem_bwd")
    dmkv = jnp.concatenate([dmk, dmv], axis=-1).reshape(nb * MEM_LEN, 2 * MEM_W).astype(BF16)
    g_wkv = _matmul(mh, dmkv, mode="tn", out_dtype=F32, tm=512, tn=512, tk=nb * MEM_LEN, name="grad_w_kv")
    _, gmn = _dh_rms_bwd(dmkv, w_kv, mem.reshape(nb * MEM_LEN, D_MODEL), mem_norm_g.reshape(1, D_MODEL), None,
                         tm=nb * MEM_LEN, tk=2 * MEM_W, name="mem_rms_bwd")

    flat = lambda a: a.reshape(t, -1)
    dp = jnp.concatenate([flat(dfq), flat(dfk), flat(dfv), dgate[:, :FOX_W], flat(ddq), flat(ddk), flat(ddv),
                          dgate[:, FOX_W:FOX_W + DIL_W], flat(dmq).astype(BF16), dgate[:, FOX_W + DIL_W:], dflog], axis=1)
    g_wr = _matmul(h, dp, mode="tn", out_dtype=F32, tm=512, tn=PWF // 3, tk=1024, name="grad_w_in")
    gain = norm_g.reshape(1, D_MODEL)
    if start_reduce is not None:
        gain = gain + start_reduce(g_wr, g_wkv, g_wo)[0:1, 0:1]
    gx, gng = _dh_rms_bwd(dp, w_r, x2d, gain, dx2, tm=512, tk=PWF // 3, name="in_rms_bwd")

    gb_row = jnp.pad(gb[0:1, :], ((0, 0), (0, D_MODEL - LANES)))
    small = jnp.concatenate([gng[0:1], gmn[0:1], st[0:1], gb_row, st[1:2], jnp.zeros((3, D_MODEL), F32)], axis=0)
    return gx.reshape(nb, SEQ, D_MODEL), g_wr, g_wkv, g_wo, small


MESH = pl.DeviceIdType.MESH
ANY = pl.BlockSpec(memory_space=pl.ANY)


def _place():
    x, y, c = lax.axis_index("x"), lax.axis_index("y"), lax.axis_index("c")
    other_chips = [(1 - x, y), (x, 1 - y), (1 - x, 1 - y)]
    return x, y, c, other_chips


def _gather_weights(shards):
    n = len(shards)

    def body(*refs):
        in_refs, out_refs = refs[:n], refs[n:2 * n]
        send_sems, recv_sems = refs[2 * n:]
        x, y, c, chips = _place()
        me_chip = 2 * x + y
        sibling = (x, y, 1 - c)

        def half(ref, pc, rows):
            return ref.at[pl.ds(pc * (rows // 2), rows // 2), :]

        def rcopy(k, src, dst, to):
            return pltpu.make_async_remote_copy(src_ref=src, dst_ref=dst, send_sem=send_sems.at[k], recv_sem=recv_sems.at[k],
                                                device_id=to, device_id_type=MESH)

        sends = []
        for t in range(n):
            rows = shards[t].shape[0]
            for j, chip in enumerate(chips):
                cp = rcopy(6 * t + j, half(in_refs[t], c, rows), half(out_refs[t].at[me_chip], c, rows), (*chip, c))
                cp.start()
                sends.append(cp)
        for t in range(n):
            rows = shards[t].shape[0]
            for j, chip in enumerate(chips):
                slot = out_refs[t].at[2 * chip[0] + chip[1]]
                rcopy(6 * t + j, half(slot, c, rows), half(slot, c, rows), sibling).wait_recv()
                fw = rcopy(6 * t + 3 + j, half(slot, c, rows), half(slot, c, rows), sibling)
                fw.start()
                sends.append(fw)
        for t in range(n):
            rows = shards[t].shape[0]
            for j, chip in enumerate(chips):
                slot = out_refs[t].at[2 * chip[0] + chip[1]]
                rcopy(6 * t + 3 + j, half(slot, 1 - c, rows), half(slot, 1 - c, rows), sibling).wait_recv()
        for cp in sends:
            cp.wait_send()

    return pl.pallas_call(
        body,
        out_shape=tuple(jax.ShapeDtypeStruct((N_CHIPS,) + s.shape, s.dtype) for s in shards),
        in_specs=[ANY] * n,
        out_specs=tuple([ANY] * n),
        scratch_shapes=[pltpu.SemaphoreType.DMA((6 * n,)), pltpu.SemaphoreType.DMA((6 * n,))],
        name="gather_weights",
    )(*shards)


def _pair_exchange(gs):
    n = len(gs)

    def body(*refs):
        g_refs, r_refs = refs[:n], refs[n:2 * n]
        send_sems, recv_sems = refs[2 * n:]
        x, y, c, _ = _place()
        cps = []
        for t in range(n):
            hr = gs[t].shape[1] // 2
            cp = pltpu.make_async_remote_copy(src_ref=g_refs[t].at[:, pl.ds((1 - c) * hr, hr), :], dst_ref=r_refs[t],
                                              send_sem=send_sems.at[t], recv_sem=recv_sems.at[t],
                                              device_id=(x, y, 1 - c), device_id_type=MESH)
            cp.start()
            cps.append(cp)
        for cp in cps:
            cp.wait()

    return pl.pallas_call(
        body,
        out_shape=tuple(jax.ShapeDtypeStruct((N_CHIPS, g.shape[1] // 2, g.shape[2]), g.dtype) for g in gs),
        in_specs=[ANY] * n,
        out_specs=tuple([ANY] * n),
        scratch_shapes=[pltpu.SemaphoreType.DMA((n,)), pltpu.SemaphoreType.DMA((n,))],
        name="pair_exchange",
    )(*gs)


def _chip_exchange(ps):
    n = len(ps)

    def body(*refs):
        p_refs, o_refs = refs[:n], refs[n:2 * n]
        send_sems, recv_sems = refs[2 * n:]
        x, y, c, chips = _place()
        me_chip = 2 * x + y
        cps = []
        for t in range(n):
            for j, chip in enumerate(chips):
                cp = pltpu.make_async_remote_copy(src_ref=p_refs[t].at[2 * chip[0] + chip[1]], dst_ref=o_refs[t].at[me_chip],
                                                  send_sem=send_sems.at[3 * t + j], recv_sem=recv_sems.at[3 * t + j],
                                                  device_id=(*chip, c), device_id_type=MESH)
                cp.start()
                cps.append(cp)
        for cp in cps:
            cp.wait()

    return pl.pallas_call(
        body,
        out_shape=tuple(jax.ShapeDtypeStruct(p.shape, p.dtype) for p in ps),
        in_specs=[ANY] * n,
        out_specs=tuple([ANY] * n),
        scratch_shapes=[pltpu.SemaphoreType.DMA((3 * n,)), pltpu.SemaphoreType.DMA((3 * n,))],
        name="chip_exchange",
    )(*ps)


_HBM = pl.BlockSpec(memory_space=pltpu.HBM)
_SEM = pl.BlockSpec(memory_space=pltpu.SEMAPHORE)
_DATAFLOW = pltpu.SideEffectType.DATAFLOW_SIDE_EFFECTING


def _chip_copies(p_refs, land_refs, send_sems, recv_sems):
    x, y, c, chips = _place()
    me_chip = 2 * x + y
    return [pltpu.make_async_remote_copy(src_ref=p_refs[t].at[2 * chip[0] + chip[1]], dst_ref=land_refs[t].at[me_chip],
                                         send_sem=send_sems.at[3 * t + j], recv_sem=recv_sems.at[3 * t + j],
                                         device_id=(*chip, c), device_id_type=MESH)
            for t in range(len(p_refs)) for j, chip in enumerate(chips)]


def _chip_exchange_start(ps):
    n = len(ps)

    def body(*refs):
        p_refs, land_refs = refs[:n], refs[n:2 * n]
        send_sems, recv_sems = refs[2 * n:2 * n + 2]
        token = refs[-1]
        for cp in _chip_copies(p_refs, land_refs, send_sems, recv_sems):
            cp.start()
        token[...] = jnp.zeros_like(token)

    hbm = [pltpu.HBM(p.shape, p.dtype) for p in ps]
    args = [pltpu.with_memory_space_constraint(p, pltpu.HBM) for p in ps]
    args += [pltpu.with_memory_space_constraint(lax.empty(p.shape, p.dtype), pltpu.HBM) for p in ps]
    out = pl.pallas_call(
        body,
        name="chip_exchange_start",
        out_shape=(pltpu.SemaphoreType.DMA((3 * n,)), pltpu.SemaphoreType.DMA((3 * n,)), *hbm, *hbm,
                   jax.ShapeDtypeStruct((8, LANES), F32)),
        in_specs=[_HBM] * (2 * n),
        out_specs=(_SEM, _SEM, *([_HBM] * (2 * n)), pl.BlockSpec(memory_space=pltpu.VMEM)),
        input_output_aliases={i: 2 + i for i in range(2 * n)},
        compiler_params=pltpu.CompilerParams(has_side_effects=_DATAFLOW),
    )(*args)
    return out[0], out[1], out[2:2 + n], out[2 + n:2 + 2 * n], out[-1]


def _chip_exchange_wait(send_sems, recv_sems, p_thru, land_thru, after):
    n = len(p_thru)

    def body(*refs):
        p_refs, land_refs = refs[:n], refs[n:2 * n]
        ssem, rsem = refs[2 * n:2 * n + 2]
        for cp in _chip_copies(p_refs, land_refs, ssem, rsem):
            cp.wait_send()
            cp.wait_recv()

    hbm = [pltpu.HBM(p.shape, p.dtype) for p in p_thru]
    out = pl.pallas_call(
        body,
        name="chip_exchange_wait",
        out_shape=(*hbm, *hbm),
        in_specs=[_HBM] * (2 * n) + [_SEM, _SEM, ANY],
        out_specs=tuple([_HBM] * (2 * n)),
        input_output_aliases={i: i for i in range(2 * n)},
        compiler_params=pltpu.CompilerParams(has_side_effects=_DATAFLOW),
    )(*p_thru, *land_thru, send_sems, recv_sems, after)
    return out[:n], out[n:]


def _pair_swap(rs):
    n = len(rs)

    def body(*refs):
        r_refs, o_refs = refs[:n], refs[n:2 * n]
        send_sems, recv_sems = refs[2 * n:]
        x, y, c, _ = _place()
        cps = []
        for t in range(n):
            cp = pltpu.make_async_remote_copy(src_ref=r_refs[t], dst_ref=o_refs[t], send_sem=send_sems.at[t],
                                              recv_sem=recv_sems.at[t], device_id=(x, y, 1 - c), device_id_type=MESH)
            cp.start()
            cps.append(cp)
        for cp in cps:
            cp.wait()

    return pl.pallas_call(
        body,
        out_shape=tuple(jax.ShapeDtypeStruct(r.shape, r.dtype) for r in rs),
        in_specs=[ANY] * n,
        out_specs=tuple([ANY] * n),
        scratch_shapes=[pltpu.SemaphoreType.DMA((n,)), pltpu.SemaphoreType.DMA((n,))],
        name="pair_swap",
    )(*rs)


N_DEV = 8
LOSS_ROW = 4


def _small_allreduce(small):
    def body(s_ref, o_ref, all_ref, send_sems, recv_sems):
        x, y, c, _ = _place()
        me = 4 * x + 2 * y + c
        all_ref[me] = s_ref[...]
        cps = []
        for k in range(1, N_DEV):
            peer = tuple(1 - p if (k >> s) & 1 else p for p, s in ((x, 2), (y, 1), (c, 0)))
            cp = pltpu.make_async_remote_copy(src_ref=s_ref, dst_ref=all_ref.at[me], send_sem=send_sems.at[k - 1],
                                              recv_sem=recv_sems.at[k - 1], device_id=peer, device_id_type=MESH)
            cp.start()
            cps.append(cp)
        for cp in cps:
            cp.wait()
        tot = all_ref[0]
        for d in range(1, N_DEV):
            tot = tot + all_ref[d]
        o_ref[...] = tot
        o_ref[LOSS_ROW:LOSS_ROW + 1, :] = jnp.broadcast_to(jnp.sum(tot[LOSS_ROW:LOSS_ROW + 1, :], axis=1, keepdims=True),
                                                          (1, tot.shape[1]))

    vm = pl.BlockSpec(memory_space=pltpu.VMEM)
    return pl.pallas_call(
        body,
        out_shape=jax.ShapeDtypeStruct(small.shape, small.dtype),
        in_specs=[vm],
        out_specs=vm,
        scratch_shapes=[pltpu.VMEM((N_DEV,) + small.shape, small.dtype), pltpu.SemaphoreType.DMA((N_DEV - 1,)),
                        pltpu.SemaphoreType.DMA((N_DEV - 1,))],
        name="small_allreduce",
    )(small)


def _sum_pair(g, recv, cidx, *, tr, name):
    _, hr, cols = recv.shape
    nr = hr // tr

    def body(c_ref, g_ref, r_ref, o_ref):
        o_ref[...] = (g_ref[...] + r_ref[...]).astype(o_ref.dtype)

    grid_spec = pltpu.PrefetchScalarGridSpec(
        num_scalar_prefetch=1,
        grid=(N_CHIPS, nr),
        in_specs=[pl.BlockSpec((None, tr, cols), lambda k, i, c_ref: (k, c_ref[0] * nr + i, 0)),
                  pl.BlockSpec((None, tr, cols), lambda k, i, c_ref: (k, i, 0))],
        out_specs=pl.BlockSpec((None, tr, cols), lambda k, i, c_ref: (k, i, 0)),
    )
    return pl.pallas_call(body, out_shape=jax.ShapeDtypeStruct(recv.shape, BF16), grid_spec=grid_spec,
                          compiler_params=_cparams(), name=name)(cidx, g, recv)


def _sum_chips(p, *, tr, name):
    _, rows, cols = p.shape

    def body(p_ref, o_ref):
        tot = p_ref[0].astype(F32)
        for k in range(1, N_CHIPS):
            tot = tot + p_ref[k].astype(F32)
        o_ref[...] = tot

    return pl.pallas_call(
        body,
        out_shape=jax.ShapeDtypeStruct((rows, cols), F32),
        grid=(rows // tr,),
        in_specs=[pl.BlockSpec((N_CHIPS, tr, cols), lambda i: (0, i, 0))],
        out_specs=pl.BlockSpec((tr, cols), lambda i: (i, 0)),
        compiler_params=_cparams(),
        name=name,
    )(p)


def _adamw(w, g, m, v, *, tr, name):
    rows, cols = w.shape
    bc1 = 1.0 / (1.0 - ADAM_B1 ** ADAM_STEP)
    bc2 = 1.0 / (1.0 - ADAM_B2 ** ADAM_STEP)

    def body(w_ref, g_ref, m_ref, v_ref, d_ref, nm_ref, nv_ref):
        gv = g_ref[...]
        nm = ADAM_B1 * m_ref[...] + (1.0 - ADAM_B1) * gv
        nv = ADAM_B2 * v_ref[...] + (1.0 - ADAM_B2) * (gv * gv)
        d_ref[...] = -ADAM_LR * ((nm * bc1) / (jnp.sqrt(nv * bc2) + ADAM_EPS) + ADAM_WD * w_ref[...])
        nm_ref[...] = nm
        nv_ref[...] = nv

    spec = pl.BlockSpec((tr, cols), lambda i: (i, 0))
    sd = jax.ShapeDtypeStruct((rows, cols), F32)
    return pl.pallas_call(body, out_shape=(sd, sd, sd), grid=(rows // tr,), in_specs=[spec] * 4, out_specs=(spec,) * 3,
                          compiler_params=_cparams(), name=name)(w, g, m, v)


def _pack_small(norm, mem_norm, final_norm, b_forget):
    rows = [norm.reshape(1, D_MODEL), mem_norm.reshape(1, D_MODEL), final_norm.reshape(1, D_MODEL),
            jnp.pad(b_forget.reshape(1, FOX_HEADS), ((0, 0), (0, D_MODEL - FOX_HEADS))), jnp.zeros((4, D_MODEL), F32)]
    return jnp.concatenate(rows, axis=0)


def _unpack_small(a):
    return a[0:1], a[3:4, :FOX_HEADS], a[1:2], a[2]


def kernel(x, mem, norm_g, w_in, b_forget, mem_norm_g, w_mem_kv, w_out, final_norm_g, loss_target, m_norm_g, m_w_in, m_b_forget, m_mem_norm_g, m_w_mem_kv, m_w_out, m_final_norm_g, v_norm_g, v_w_in, v_b_forget, v_mem_norm_g, v_w_mem_kv, v_w_out, v_final_norm_g):
    core = lax.axis_index("c").astype(jnp.int32)
    me_chip = (2 * lax.axis_index("x") + lax.axis_index("y")).astype(jnp.int32)
    cidx = core.reshape(1)

    def own_slot(arr, own):
        return lax.dynamic_update_slice(arr, own[None].astype(arr.dtype), (me_chip,) + (0,) * own.ndim)

    mine = [w_in[0].astype(BF16), w_mem_kv[0].astype(BF16), w_out[0].astype(BF16)]
    g_in, g_kv, g_out = (own_slot(g, s) for g, s in zip(_gather_weights(mine), mine))
    w_r = _rearrange_w_in([g_in[k] for k in range(N_CHIPS)])
    w_kv = g_kv.reshape(D_MODEL, 2 * MEM_W)
    w_o = g_out.reshape(MIX_W, D_MODEL)

    trs = (128, 128, 256)
    names = ("w_in", "w_mem_kv", "w_out")
    flight = []

    def start_reduce(g_wr, g_wkv, g_wo):
        slabs = [_w_in_grad_slabs(g_wr),
                 g_wkv.reshape(N_CHIPS, D_MODEL // N_CHIPS, 2 * MEM_W),
                 g_wo.reshape(N_CHIPS, MIX_W // N_CHIPS, D_MODEL)]
        recv = _pair_exchange(slabs)
        pair = [_sum_pair(g, r, cidx, tr=tr, name=f"sum_pair_{nm}") for g, r, tr, nm in zip(slabs, recv, trs, names)]
        *handles, token = _chip_exchange_start(pair)
        flight.extend(handles)
        return token

    gx, g_wr, g_wkv, g_wo, small = _local_grads(x, mem, norm_g, w_r, b_forget, mem_norm_g, w_kv, w_o, final_norm_g, loss_target,
                                                start_reduce=start_reduce)

    send_sems, recv_sems, pair, land = flight
    pair, landed = _chip_exchange_wait(send_sems, recv_sems, pair, land, small)
    got = [lax.dynamic_update_slice(g, lax.dynamic_slice(p, (me_chip, 0, 0), (1,) + p.shape[1:]), (me_chip, 0, 0))
           for g, p in zip(landed, pair)]
    red = [_sum_chips(p, tr=tr, name=f"sum_chips_{nm}") for p, tr, nm in zip(got, trs, names)]
    sib = _pair_swap(red)
    grads = [jnp.where(core == 0, jnp.concatenate([r, s], axis=0), jnp.concatenate([s, r], axis=0)) for r, s in zip(red, sib)]

    outs = {}
    for nm, g, w, m, v, tr in zip(names, grads, (w_in, w_mem_kv, w_out), (m_w_in, m_w_mem_kv, m_w_out),
                                  (v_w_in, v_w_mem_kv, v_w_out), trs):
        d, nmo, nvo = _adamw(w[0], g, m[0], v[0], tr=tr, name=f"adamw_{nm}")
        outs[nm] = tuple(a[None] for a in (g, d, nmo, nvo))

    gsum = _small_allreduce(small)
    sd, sm, sv = _adamw(_pack_small(norm_g, mem_norm_g, final_norm_g, b_forget), gsum,
                        _pack_small(m_norm_g, m_mem_norm_g, m_final_norm_g, m_b_forget),
                        _pack_small(v_norm_g, v_mem_norm_g, v_final_norm_g, v_b_forget), tr=8, name="adamw_small")
    loss = gsum[LOSS_ROW, 0]

    def group(i, small_arr):
        ng, bf, mg, fg = _unpack_small(small_arr)
        return (ng, outs["w_in"][i], bf, mg, outs["w_mem_kv"][i], outs["w_out"][i], fg)

    return (loss, gx, *group(0, gsum), *group(1, sd), *group(2, sm), *group(3, sv))
```

```python
import functools
import math

import jax
import jax.numpy as jnp
from jax import lax
from jax.experimental import pallas as pl
from jax.experimental.pallas import tpu as pltpu

F32 = jnp.float32
BF16 = jnp.bfloat16

D_MODEL = 1024
SEQ = 2048
HEAD_DIM = 64
FOX_HEADS = 12
DIL_HEADS = 12
MEM_HEADS = 4
MEM_HEAD_DIM = 128
MEM_LEN = 256
FOX_W = FOX_HEADS * HEAD_DIM
DIL_W = DIL_HEADS * HEAD_DIM
MEM_W = MEM_HEADS * MEM_HEAD_DIM
MIX_W = FOX_W + DIL_W + MEM_W
DILATIONS = ((128, 1), (512, 4), (2048, 16))
ROPE_THETA = 500000.0
ROPE_DIM = HEAD_DIM // 4
RMS_EPS = 1e-6
NEG_INF = -1e30
IN_SIZES = [FOX_W] * 4 + [FOX_HEADS] + [DIL_W] * 4 + [MEM_W] * 2
IN_W = sum(IN_SIZES)

ADAM_LR = 0.001
ADAM_B1 = 0.9
ADAM_B2 = 0.999
ADAM_EPS = 1e-08
ADAM_WD = 0.01
ADAM_STEP = 10

LANES = 128
N_CHIPS = 4
PW = 7168
PWF = PW + 4 * LANES
C_FQ, C_FK, C_FV, C_FG = 0, 768, 1536, 2304
C_DQ, C_DK, C_DV, C_DG = 3072, 3840, 4608, 5376
C_MQ, C_MG = 6144, 6656
VMEM_LIMIT = 48 * 1024 * 1024


def _cparams(**kw):
    return pltpu.CompilerParams(vmem_limit_bytes=VMEM_LIMIT, **kw)


def _matmul(a, b, *, out_dtype, tm, tn, tk, name, mode="nn"):
    if mode == "tn":
        (kdim, m), n = a.shape, b.shape[1]
        a_spec = pl.BlockSpec((tk, tm), lambda i, j, k: (k, i))
        b_spec = pl.BlockSpec((tk, tn), lambda i, j, k: (k, j))
        dims = _T0
    elif mode == "nt":
        (m, kdim), n = a.shape, b.shape[0]
        a_spec = pl.BlockSpec((tm, tk), lambda i, j, k: (i, k))
        b_spec = pl.BlockSpec((tn, tk), lambda i, j, k: (j, k))
        dims = _NT
    else:
        (m, kdim), n = a.shape, b.shape[1]
        a_spec = pl.BlockSpec((tm, tk), lambda i, j, k: (i, k))
        b_spec = pl.BlockSpec((tk, tn), lambda i, j, k: (k, j))
        dims = (((1,), (0,)), ((), ()))
    nk = kdim // tk
    assert m % tm == 0 and n % tn == 0 and kdim % tk == 0

    def body(a_ref, b_ref, o_ref, *scratch):
        prod = lax.dot_general(a_ref[...], b_ref[...], dims, preferred_element_type=F32)
        if nk == 1:
            o_ref[...] = prod.astype(o_ref.dtype)
            return
        acc_ref, = scratch
        k = pl.program_id(2)

        @pl.when(k == 0)
        def _():
            acc_ref[...] = prod

        @pl.when(k > 0)
        def _():
            acc_ref[...] += prod

        @pl.when(k == nk - 1)
        def _():
            o_ref[...] = acc_ref[...].astype(o_ref.dtype)

    return pl.pallas_call(
        body,
        out_shape=jax.ShapeDtypeStruct((m, n), out_dtype),
        grid=(m // tm, n // tn, nk),
        in_specs=[a_spec, b_spec],
        out_specs=pl.BlockSpec((tm, tn), lambda i, j, k: (i, j)),
        scratch_shapes=[pltpu.VMEM((tm, tn), F32)] if nk > 1 else [],
        compiler_params=_cparams(dimension_semantics=("parallel", "parallel", "arbitrary")),
        name=name,
    )(a, b)


def _rms_fwd(x, g, *, tm, name):
    t, d = x.shape

    def body(x_ref, g_ref, h_ref):
        xv = x_ref[...]
        r = lax.rsqrt(jnp.mean(xv * xv, axis=-1, keepdims=True) + RMS_EPS)
        h_ref[...] = (xv * r * g_ref[...]).astype(h_ref.dtype)

    return pl.pallas_call(
        body,
        out_shape=jax.ShapeDtypeStruct((t, d), BF16),
        grid=(t // tm,),
        in_specs=[pl.BlockSpec((tm, d), lambda i: (i, 0)), pl.BlockSpec((1, d), lambda i: (0, 0))],
        out_specs=pl.BlockSpec((tm, d), lambda i: (i, 0)),
        compiler_params=_cparams(),
        name=name,
    )(x, g)


def _rope_tables():
    half = ROPE_DIM // 2
    pos = jnp.arange(SEQ, dtype=F32)
    inv_freq = 1.0 / (ROPE_THETA ** (jnp.arange(0, ROPE_DIM, 2, dtype=F32) / ROPE_DIM))
    ang = pos[:, None] * inv_freq[None, :]
    cos, sin = jnp.cos(ang), jnp.sin(ang)
    one = jnp.ones((SEQ, HEAD_DIM - ROPE_DIM), F32)
    zero = jnp.zeros((SEQ, HEAD_DIM - ROPE_DIM), F32)
    zh = jnp.zeros((SEQ, half), F32)
    c = jnp.concatenate([cos, cos, one], axis=1)
    s1 = jnp.concatenate([zh, sin, zero], axis=1)
    s2 = jnp.concatenate([-sin, zh, zero], axis=1)
    rep = LANES // HEAD_DIM
    return jnp.tile(c, (1, rep)), jnp.tile(s1, (1, rep)), jnp.tile(s2, (1, rep))


def _rope_apply(t, c, s1, s2, transpose=False):
    n = t.shape[-1]
    rep = n // LANES
    c, s1, s2 = (jnp.tile(u, (1, rep)) for u in (c, s1, s2))
    half = ROPE_DIM // 2
    if not transpose:
        return t * c + pltpu.roll(t, half, 1) * s1 + pltpu.roll(t, n - half, 1) * s2
    return t * c + pltpu.roll(t * s1, n - half, 1) + pltpu.roll(t * s2, half, 1)


def _proj(h, w, tabs, *, n, tm, tn, name):
    t, d = h.shape
    assert C_DQ % tn == 0 and (C_DV - C_DQ) % tn == 0 and (C_DG - C_DQ) % tn == 0
    rope_lo, rope_hi, dil_hi = C_DQ // tn, C_DV // tn, C_DG // tn
    s_blocks = SEQ // tm

    def body(h_ref, w_ref, c_ref, s1_ref, s2_ref, o_ref, f_ref):
        j = pl.program_id(1)
        acc = jnp.dot(h_ref[...], w_ref[...], preferred_element_type=F32)
        is_rope = jnp.logical_and(j >= rope_lo, j < rope_hi)

        @pl.when(is_rope)
        def _():
            r = _rope_apply(acc, c_ref[...], s1_ref[...], s2_ref[...])
            o_ref[...] = r.astype(o_ref.dtype)
            f_ref[...] = r

        @pl.when(jnp.logical_not(is_rope))
        def _():
            o_ref[...] = acc.astype(o_ref.dtype)

        @pl.when(jnp.logical_and(j >= rope_hi, j < dil_hi))
        def _():
            f_ref[...] = acc

    tab_spec = pl.BlockSpec((tm, LANES), lambda i, j: (i % s_blocks, 0))
    f_spec = pl.BlockSpec((tm, tn), lambda i, j: (i, jnp.clip(j - rope_lo, 0, dil_hi - rope_lo - 1)))
    return pl.pallas_call(
        body,
        out_shape=(jax.ShapeDtypeStruct((t, n), BF16), jax.ShapeDtypeStruct((t, 3 * DIL_W), F32)),
        grid=(t // tm, n // tn),
        in_specs=[pl.BlockSpec((tm, d), lambda i, j: (i, 0)), pl.BlockSpec((d, tn), lambda i, j: (0, j)),
                  tab_spec, tab_spec, tab_spec],
        out_specs=(pl.BlockSpec((tm, tn), lambda i, j: (i, j)), f_spec),
        compiler_params=_cparams(dimension_semantics=("parallel", "arbitrary")),
        name=name,
    )(h, w, *tabs)


def _split3(x):
    hi = x.astype(BF16)
    r1 = x - hi.astype(F32)
    mid = r1.astype(BF16)
    lo = (r1 - mid.astype(F32)).astype(BF16)
    return hi, mid, lo


def _dot3(sel, x, sel_is_lhs):
    out = None
    for piece in _split3(x):
        t = jnp.dot(sel, piece, preferred_element_type=F32) if sel_is_lhs else jnp.dot(piece, sel, preferred_element_type=F32)
        out = t if out is None else out + t
    return out


def _flog_fwd(flog, bpad, *, nb, ts, name):
    ns = SEQ // ts

    def body(f_ref, b_ref, c_ref, carry_ref):
        s = pl.program_id(1)

        @pl.when(s == 0)
        def _():
            carry_ref[...] = jnp.zeros_like(carry_ref)

        z = f_ref[...] + b_ref[...]
        logf = jnp.minimum(z, 0.0) - jnp.log(1.0 + jnp.exp(-jnp.abs(z)))
        r = lax.broadcasted_iota(jnp.int32, (ts, ts), 0)
        c = lax.broadcasted_iota(jnp.int32, (ts, ts), 1)
        tri = jnp.where(r >= c, 1.0, 0.0).astype(BF16)
        cs = _dot3(tri, logf, True) + carry_ref[0:1, :]
        carry_ref[...] = jnp.broadcast_to(cs[ts - 1:ts, :], carry_ref.shape)
        c_ref[...] = cs

    return pl.pallas_call(
        body,
        out_shape=jax.ShapeDtypeStruct((nb * SEQ, LANES), F32),
        grid=(nb, ns),
        in_specs=[pl.BlockSpec((ts, LANES), lambda b, s: (b * ns + s, 0)), pl.BlockSpec((1, LANES), lambda b, s: (0, 0))],
        out_specs=pl.BlockSpec((ts, LANES), lambda b, s: (b * ns + s, 0)),
        scratch_shapes=[pltpu.VMEM((8, LANES), F32)],
        compiler_params=_cparams(dimension_semantics=("parallel", "arbitrary")),
        name=name,
    )(flog, bpad)


def _flog_bwd(dcol, flog, bpad, *, nb, ts, name):
    ns = SEQ // ts

    def body(d_ref, f_ref, b_ref, o_ref, gb_ref, carry_ref):
        bi = pl.program_id(0)
        s = pl.program_id(1)

        @pl.when(s == 0)
        def _():
            carry_ref[...] = jnp.zeros_like(carry_ref)

        @pl.when(jnp.logical_and(bi == 0, s == 0))
        def _():
            gb_ref[...] = jnp.zeros_like(gb_ref)

        r = lax.broadcasted_iota(jnp.int32, (ts, ts), 0)
        c = lax.broadcasted_iota(jnp.int32, (ts, ts), 1)
        tri = jnp.where(r <= c, 1.0, 0.0).astype(BF16)
        rc = _dot3(tri, d_ref[...], True) + carry_ref[0:1, :]
        carry_ref[...] = jnp.broadcast_to(rc[0:1, :], carry_ref.shape)
        z = f_ref[...] + b_ref[...]
        dz = rc / (1.0 + jnp.exp(z))
        o_ref[...] = dz.astype(o_ref.dtype)
        gb_ref[...] += jnp.broadcast_to(jnp.sum(dz, axis=0, keepdims=True), gb_ref.shape)

    rev = lambda b, s: (b * ns + (ns - 1 - s), 0)
    return pl.pallas_call(
        body,
        out_shape=(jax.ShapeDtypeStruct((nb * SEQ, LANES), BF16), jax.ShapeDtypeStruct((8, LANES), F32)),
        grid=(nb, ns),
        in_specs=[pl.BlockSpec((ts, LANES), rev), pl.BlockSpec((ts, LANES), rev), pl.BlockSpec((1, LANES), lambda b, s: (0, 0))],
        out_specs=(pl.BlockSpec((ts, LANES), rev), pl.BlockSpec((8, LANES), lambda b, s: (0, 0))),
        scratch_shapes=[pltpu.VMEM((8, LANES), F32)],
        compiler_params=_cparams(dimension_semantics=("arbitrary", "arbitrary")),
        name=name,
    )(dcol, flog, bpad)


class _AttnCfg:
    def __init__(self, *, e, tq, tk, lq, lk, causal, window, ncol, qcol, kcol, vcol, split_p=False):
        self.e, self.tq, self.tk, self.lq, self.lk = e, tq, tk, lq, lk
        self.split_p = split_p
        self.causal, self.window = causal, window
        self.ncol, self.qcol, self.kcol, self.vcol = ncol, qcol, kcol, vcol
        self.nh = LANES // e
        self.scale = 1.0 / math.sqrt(e)
        self.nq, self.nk = lq // tq, lk // tk

    def k_range(self, i):
        if not self.causal:
            return 0, self.nk
        hi = ((i + 1) * self.tq - 1) // self.tk + 1
        if self.window is None:
            return 0, hi
        return jnp.maximum((i * self.tq - self.window) // self.tk, 0), hi


def _head_masks(nh):
    lane = lax.broadcasted_iota(jnp.int32, (1, LANES), 1)
    return [None] if nh == 1 else [lane < HEAD_DIM, lane >= HEAD_DIM]


def _sel(mask, a, b):
    return a if mask is None else jnp.where(mask, a, b)


def _scores(cfg, qh, kb, q0, k0, dlt0, bias):
    s = lax.dot_general(qh, kb, (((1,), (1,)), ((), ())), preferred_element_type=F32) * cfg.scale
    if bias is not None:
        s = s + bias
    if cfg.causal:
        d = dlt0 + (q0 - k0)
        if cfg.window is None:
            ok = d >= 0
        else:
            ok = d.astype(jnp.uint32) <= jnp.uint32(cfg.window)
        s = jnp.where(ok, s, NEG_INF)
    return s


def _attn_fwd(cfg, q, k, v, *, out_cols, bias=None, state=None, finalize=True, name):
    g = q.shape[0]
    tq, tk, e, nh = cfg.tq, cfg.tk, cfg.e, cfg.nh

    def body(*refs):
        refs = list(refs)
        q_ref, k_ref, v_ref = refs[:3]
        del refs[:3]
        if bias is not None:
            cb_ref, cr_ref = refs[:2]
            del refs[:2]
        if state is not None:
            ai_ref, mi_ref, li_ref = refs[:3]
            del refs[:3]
        out_refs = refs
        masks = _head_masks(nh)
        dlt0 = lax.broadcasted_iota(jnp.int32, (tq, tk), 0) - lax.broadcasted_iota(jnp.int32, (tq, tk), 1)

        def qbody(i, carry):
            q0 = pl.multiple_of(i * tq, tq)
            rows = pl.ds(q0, tq)
            qb = q_ref[rows, :]
            lo, hi = cfg.k_range(i)
            res = []
            for h in range(nh):
                qh = _sel(masks[h], qb, jnp.zeros_like(qb))
                if state is not None:
                    m0 = mi_ref[rows, h * e:h * e + 1]
                    l0 = li_ref[rows, h * e:h * e + 1]
                    a0 = ai_ref[rows, :]
                else:
                    m0 = jnp.full((tq, 1), NEG_INF, F32)
                    l0 = jnp.zeros((tq, 1), F32)
                    a0 = jnp.zeros((tq, LANES), F32)
                cq = cb_ref[rows, h * e:h * e + 1] if bias is not None else None

                def kbody(jk, c, qh=qh, cq=cq, h=h):
                    m, l, a = c
                    k0 = pl.multiple_of(jk * tk, tk)
                    kb = k_ref[pl.ds(k0, tk), :]
                    vb = v_ref[pl.ds(k0, tk), :]
                    b = (cq - cr_ref[jk, h:h + 1, :]) if bias is not None else None
                    s = _scores(cfg, qh, kb, q0, k0, dlt0, b)
                    m_new = jnp.maximum(m, jnp.max(s, axis=1, keepdims=True))
                    alpha = jnp.exp(m - m_new)
                    p = jnp.exp(s - m_new)
                    l = alpha * l + jnp.sum(p, axis=1, keepdims=True)
                    pb = p.astype(BF16)
                    pv = jnp.dot(pb, vb, preferred_element_type=F32)
                    if cfg.split_p:
                        pv = pv + jnp.dot((p - pb.astype(F32)).astype(BF16), vb, preferred_element_type=F32)
                    a = alpha * a + pv
                    return m_new, l, a

                res.append(lax.fori_loop(lo, hi, kbody, (m0, l0, a0)))
            if nh == 1:
                m, l, a = res[0]
                m, l = jnp.broadcast_to(m, (tq, LANES)), jnp.broadcast_to(l, (tq, LANES))
            else:
                m = jnp.where(masks[0], res[0][0], res[1][0])
                l = jnp.where(masks[0], res[0][1], res[1][1])
                a = jnp.where(masks[0], res[0][2], res[1][2])
            if finalize:
                out_refs[0][rows, :] = a / l
                out_refs[1][rows, :] = m + jnp.log(l)
            else:
                out_refs[0][rows, :] = a
                out_refs[1][rows, :] = m
                out_refs[2][rows, :] = l
            return carry

        lax.fori_loop(0, cfg.nq, qbody, 0)

    qspec = pl.BlockSpec((None, cfg.lq, LANES), lambda b, j: (b, 0, cfg.qcol(j)))
    kspec = pl.BlockSpec((None, cfg.lk, LANES), lambda b, j: (b, 0, cfg.kcol(j)))
    vspec = pl.BlockSpec((None, cfg.lk, LANES), lambda b, j: (b, 0, cfg.vcol(j)))
    ospec = pl.BlockSpec((None, cfg.lq, LANES), lambda b, j: (b, 0, j))
    args, in_specs = [q, k, v], [qspec, kspec, vspec]
    if bias is not None:
        args += list(bias)
        in_specs += [ospec, pl.BlockSpec((None, None, cfg.nk, 8, tk), lambda b, j: (b, j, 0, 0, 0))]
    aliases = {}
    if state is not None:
        aliases = {len(args) + t: t for t in range(3 if not finalize else 2)}
        args += list(state)
        in_specs += [ospec] * 3
    n_out = 2 if finalize else 3
    osd = jax.ShapeDtypeStruct((g, cfg.lq, out_cols), F32)
    return pl.pallas_call(
        body,
        out_shape=(osd,) * n_out,
        grid=(g, cfg.ncol),
        in_specs=in_specs,
        out_specs=(ospec,) * n_out,
        input_output_aliases=aliases,
        compiler_params=_cparams(dimension_semantics=("parallel", "parallel")),
        name=name,
    )(*args)


def _attn_bwd(cfg, q, k, v, do, o, lse, *, out_cols, kv_cols, bias=None, acc=None, do_off=0, name):
    g = q.shape[0]
    tq, tk, e, nh = cfg.tq, cfg.tk, cfg.e, cfg.nh
    t0 = (((0,), (0,)), ((), ()))

    def body(*refs):
        refs = list(refs)
        q_ref, k_ref, v_ref, do_ref, o_ref, lse_ref = refs[:6]
        del refs[:6]
        if bias is not None:
            cb_ref, cr_ref = refs[:2]
            del refs[:2]
        if acc is not None:
            dqi_ref, dki_ref, dvi_ref = refs[:3]
            del refs[:3]
        dq_ref, dk_ref, dv_ref = refs[:3]
        dcr_ref = refs[3] if bias is not None else None
        masks = _head_masks(nh)
        dlt0 = lax.broadcasted_iota(jnp.int32, (tq, tk), 0) - lax.broadcasted_iota(jnp.int32, (tq, tk), 1)
        if acc is not None:
            dq_ref[...] = dqi_ref[...]
            dk_ref[...] = dki_ref[...]
            dv_ref[...] = dvi_ref[...]
        else:
            dq_ref[...] = jnp.zeros_like(dq_ref)
            dk_ref[...] = jnp.zeros_like(dk_ref)
            dv_ref[...] = jnp.zeros_like(dv_ref)
        if dcr_ref is not None:
            dcr_ref[...] = jnp.zeros_like(dcr_ref)

        def qbody(i, carry):
            q0 = pl.multiple_of(i * tq, tq)
            rows = pl.ds(q0, tq)
            qb = q_ref[rows, :]
            dob = do_ref[rows, :].astype(BF16)
            prod = dob.astype(F32) * o_ref[rows, :]
            lo, hi = cfg.k_range(i)
            dqs = []
            for h in range(nh):
                qh = _sel(masks[h], qb, jnp.zeros_like(qb))
                doh = _sel(masks[h], dob, jnp.zeros_like(dob))
                lse_h = lse_ref[rows, h * e:h * e + 1]
                delta = jnp.sum(_sel(masks[h], prod, jnp.zeros_like(prod)), axis=1, keepdims=True)
                cq = cb_ref[rows, h * e:h * e + 1] if bias is not None else None

                def kbody(jk, dq_acc, qh=qh, doh=doh, lse_h=lse_h, delta=delta, cq=cq, h=h):
                    k0 = pl.multiple_of(jk * tk, tk)
                    krows = pl.ds(k0, tk)
                    kb = k_ref[krows, :]
                    vb = v_ref[krows, :]
                    b = (cq - cr_ref[jk, h:h + 1, :]) if bias is not None else None
                    s = _scores(cfg, qh, kb, q0, k0, dlt0, b)
                    p = jnp.exp(s - lse_h)
                    dp = lax.dot_general(doh, vb, (((1,), (1,)), ((), ())), preferred_element_type=F32)
                    ds = p * (dp - delta)
                    if dcr_ref is not None:
                        dcr_ref[jk, h:h + 1, :] += jnp.sum(ds, axis=0, keepdims=True)
                    dsb = (ds * cfg.scale).astype(BF16)
                    dv_ref[krows, :] += lax.dot_general(p.astype(BF16), doh, t0, preferred_element_type=F32)
                    dk_ref[krows, :] += lax.dot_general(dsb, qh, t0, preferred_element_type=F32)
                    return dq_acc + jnp.dot(dsb, kb, preferred_element_type=F32)

                dqs.append(lax.fori_loop(lo, hi, kbody, jnp.zeros((tq, LANES), F32)))
            dq = dqs[0] if nh == 1 else jnp.where(masks[0], dqs[0], dqs[1])
            dq_ref[rows, :] += dq
            return carry

        lax.fori_loop(0, cfg.nq, qbody, 0)

    qspec = pl.BlockSpec((None, cfg.lq, LANES), lambda b, j: (b, 0, cfg.qcol(j)))
    kspec = pl.BlockSpec((None, cfg.lk, LANES), lambda b, j: (b, 0, cfg.kcol(j)))
    vspec = pl.BlockSpec((None, cfg.lk, LANES), lambda b, j: (b, 0, cfg.vcol(j)))
    ospec = pl.BlockSpec((None, cfg.lq, LANES), lambda b, j: (b, 0, j))
    kvspec = pl.BlockSpec((None, cfg.lk, LANES), lambda b, j: (b, 0, j))
    dospec = pl.BlockSpec((None, cfg.lq, LANES), lambda b, j: (b, 0, do_off + j))
    args, in_specs = [q, k, v, do, o, lse], [qspec, kspec, vspec, dospec, ospec, ospec]
    out_shape = [jax.ShapeDtypeStruct((g, cfg.lq, out_cols), F32), jax.ShapeDtypeStruct((g, cfg.lk, kv_cols), F32),
                 jax.ShapeDtypeStruct((g, cfg.lk, kv_cols), F32)]
    out_specs = [ospec, kvspec, kvspec]
    if bias is not None:
        args += list(bias)
        crspec = pl.BlockSpec((None, None, cfg.nk, 8, tk), lambda b, j: (b, j, 0, 0, 0))
        in_specs += [ospec, crspec]
        out_shape.append(jax.ShapeDtypeStruct((g, cfg.ncol, cfg.nk, 8, tk), F32))
        out_specs.append(crspec)
    aliases = {}
    if acc is not None:
        aliases = {len(args) + t: t for t in range(3)}
        args += list(acc)
        in_specs += [ospec, kvspec, kvspec]
    return pl.pallas_call(
        body,
        out_shape=tuple(out_shape),
        grid=(g, cfg.ncol),
        in_specs=in_specs,
        out_specs=tuple(out_specs),
        input_output_aliases=aliases,
        compiler_params=_cparams(dimension_semantics=("parallel", "parallel")),
        name=name,
    )(*args)


BLK = 128
NBLK = SEQ // BLK
QK_SCALE = 1.0 / math.sqrt(HEAD_DIM)
DIL_STEPS = tuple(d for _, d in DILATIONS)
assert all(w // d == BLK for w, d in DILATIONS)
_T0 = (((0,), (0,)), ((), ()))
_NT = (((1,), (1,)), ((), ()))


def _stack_heads(a, masks):
    z = jnp.zeros_like(a)
    return jnp.concatenate([jnp.where(masks[0], a, z), jnp.where(masks[1], a, z)], axis=0)


def _tri_bias(lower):
    r = lax.broadcasted_iota(jnp.int32, (BLK, BLK), 0)
    c = lax.broadcasted_iota(jnp.int32, (BLK, BLK), 1)
    return jnp.where((c <= r) if lower else (c >= r), 0.0, NEG_INF).astype(F32)


def _dil_rows(r, i, d):
    start = r + i * (BLK * d)
    return pl.ds(start, BLK) if d == 1 else pl.ds(start, BLK, stride=d)


DIL_SET = 4


def _dil_sets(d, fn):
    nbk = SEQ // d // BLK
    if d == 1:
        def gbody(g, c):
            fn([(0, DIL_SET * g + a, None if a == 0 else True) for a in range(DIL_SET)])
            return c
        lax.fori_loop(0, nbk // DIL_SET, gbody, 0)
    elif nbk > 1:
        assert nbk == DIL_SET
        def rbody(r, c):
            fn([(r, i, i > 0) for i in range(nbk)])
            return c
        lax.fori_loop(0, d, rbody, 0)
    else:
        def rbody(rr, c):
            fn([(DIL_SET * rr + a, 0, False) for a in range(DIL_SET)])
            return c
        lax.fori_loop(0, d // DIL_SET, rbody, 0)


def _dil_key_tiles(r, i, d, has_prev, qrows, tri_cur, tri_prev):
    tiles = [(qrows, tri_cur)]
    if has_prev is None:
        tiles.append((_dil_rows(r, jnp.maximum(i - 1, 0), d), tri_prev + jnp.where(i > 0, 0.0, NEG_INF)))
    elif has_prev:
        tiles.append((_dil_rows(r, i - 1, d), tri_prev))
    return tiles


def _dil_fwd(qkv, *, name):
    nb = qkv.shape[0]
    ncol = DIL_W // LANES
    hd = HEAD_DIM

    def body(q_ref, k_ref, v_ref, o_ref, lse_ref, m_ref, l_ref, a_ref):
        masks = _head_masks(2)
        tri_cur, tri_prev = _tri_bias(True), _tri_bias(False)
        for pi, d in enumerate(DIL_STEPS):
            first, last = pi == 0, pi == len(DIL_STEPS) - 1

            def qset(blocks, d=d, first=first, last=last):
                work = []
                for r, i, has_prev in blocks:
                    qrows = _dil_rows(r, i, d)
                    qcat = _stack_heads((q_ref[qrows, :] * QK_SCALE).astype(BF16), masks)
                    ss, krs = [], []
                    for krows, bias in _dil_key_tiles(r, i, d, has_prev, qrows, tri_cur, tri_prev):
                        s = lax.dot_general(qcat, k_ref[krows, :].astype(BF16), _NT, preferred_element_type=F32)
                        ss.append((s[:BLK] + bias, s[BLK:] + bias))
                        krs.append(krows)
                    work.append((qrows, ss, krs))
                for qrows, ss, krs in work:
                    e0 = ss[0][0] if len(ss) == 1 else jnp.maximum(ss[0][0], ss[1][0])
                    e1 = ss[0][1] if len(ss) == 1 else jnp.maximum(ss[0][1], ss[1][1])
                    n0 = jnp.max(e0, axis=1, keepdims=True)
                    n1 = jnp.max(e1, axis=1, keepdims=True)
                    if not first:
                        mo, lo = m_ref[qrows, :], l_ref[qrows, :]
                        m0, m1 = mo[:, 0:1], mo[:, hd:hd + 1]
                        n0, n1 = jnp.maximum(n0, m0), jnp.maximum(n1, m1)
                        a0, a1 = jnp.exp(m0 - n0), jnp.exp(m1 - n1)
                    ps = [(jnp.exp(s0 - n0), jnp.exp(s1 - n1)) for s0, s1 in ss]
                    t0 = ps[0][0] if len(ps) == 1 else ps[0][0] + ps[1][0]
                    t1 = ps[0][1] if len(ps) == 1 else ps[0][1] + ps[1][1]
                    l0 = jnp.sum(t0, axis=1, keepdims=True)
                    l1 = jnp.sum(t1, axis=1, keepdims=True)
                    acc = None
                    for (p0, p1), krows in zip(ps, krs):
                        vcat = _stack_heads(v_ref[krows, :].astype(BF16), masks)
                        pv = jnp.dot(jnp.concatenate([p0, p1], axis=1).astype(BF16), vcat, preferred_element_type=F32)
                        acc = pv if acc is None else acc + pv
                    if not first:
                        l0 = l0 + a0 * lo[:, 0:1]
                        l1 = l1 + a1 * lo[:, hd:hd + 1]
                        acc = acc + a_ref[qrows, :] * jnp.where(masks[0], a0, a1)
                    if last:
                        o_ref[qrows, :] = acc / jnp.where(masks[0], l0, l1)
                        lse_ref[qrows, :] = jnp.where(masks[0], n0 + jnp.log(l0), n1 + jnp.log(l1))
                    else:
                        m_ref[qrows, :] = jnp.where(masks[0], n0, n1)
                        l_ref[qrows, :] = jnp.where(masks[0], l0, l1)
                        a_ref[qrows, :] = acc

            _dil_sets(d, qset)

    spec = lambda off: pl.BlockSpec((None, SEQ, LANES), lambda b, j: (b, 0, off + j))
    ospec = pl.BlockSpec((None, SEQ, LANES), lambda b, j: (b, 0, j))
    osd = jax.ShapeDtypeStruct((nb, SEQ, DIL_W), F32)
    return pl.pallas_call(
        body, out_shape=(osd, osd), grid=(nb, ncol),
        in_specs=[spec(0), spec(ncol), spec(2 * ncol)], out_specs=(ospec, ospec),
        scratch_shapes=[pltpu.VMEM((SEQ, LANES), F32)] * 3,
        compiler_params=_cparams(dimension_semantics=("parallel", "parallel")), name=name,
    )(qkv, qkv, qkv)


def _dil_bwd(qkv, do, o, lse, tabs, *, do_off, name):
    nb = qkv.shape[0]
    ncol = DIL_W // LANES
    hd = HEAD_DIM

    def body(q_ref, k_ref, v_ref, do_ref, o_ref, lse_ref, c_ref, s1_ref, s2_ref, dqo_ref, dko_ref, dvo_ref,
             dq_ref, dk_ref, dv_ref, dl_ref, dof_ref):
        masks = _head_masks(2)
        tri_cur, tri_prev = _tri_bias(True), _tri_bias(False)
        dq_ref[...] = jnp.zeros_like(dq_ref)
        dk_ref[...] = jnp.zeros_like(dk_ref)
        dv_ref[...] = jnp.zeros_like(dv_ref)

        def delta_body(i, c):
            rows = pl.ds(pl.multiple_of(i * BLK, BLK), BLK)
            dof = do_ref[rows, :].astype(F32)
            dof_ref[rows, :] = dof
            prod = dof * o_ref[rows, :]
            z = jnp.zeros_like(prod)
            dl_ref[rows, :] = jnp.where(masks[0], jnp.sum(jnp.where(masks[0], prod, z), axis=1, keepdims=True),
                                        jnp.sum(jnp.where(masks[1], prod, z), axis=1, keepdims=True))
            return c

        lax.fori_loop(0, NBLK, delta_body, 0)

        for d in DIL_STEPS:
            def qset(blocks, d=d):
                work = []
                for r, i, has_prev in blocks:
                    qrows = _dil_rows(r, i, d)
                    qcat = _stack_heads((q_ref[qrows, :] * QK_SCALE).astype(BF16), masks)
                    docat = _stack_heads(dof_ref[qrows, :].astype(BF16), masks)
                    tiles = []
                    for krows, bias in _dil_key_tiles(r, i, d, has_prev, qrows, tri_cur, tri_prev):
                        s = lax.dot_general(qcat, k_ref[krows, :].astype(BF16), _NT, preferred_element_type=F32)
                        dp = lax.dot_general(docat, v_ref[krows, :].astype(BF16), _NT, preferred_element_type=F32)
                        tiles.append((krows, s, dp, bias))
                    work.append((qrows, qcat, docat, tiles))
                for qrows, qcat, docat, tiles in work:
                    lseb, dlb = lse_ref[qrows, :], dl_ref[qrows, :]
                    lse0, lse1 = lseb[:, 0:1], lseb[:, hd:hd + 1]
                    dl0, dl1 = dlb[:, 0:1], dlb[:, hd:hd + 1]
                    dq = None
                    for krows, s, dp, bias in tiles:
                        p0 = jnp.exp(s[:BLK] + bias - lse0)
                        p1 = jnp.exp(s[BLK:] + bias - lse1)
                        ds0 = p0 * (dp[:BLK] - dl0)
                        ds1 = p1 * (dp[BLK:] - dl1)
                        pcat = jnp.concatenate([p0, p1], axis=0).astype(BF16)
                        dscat = jnp.concatenate([ds0, ds1], axis=0).astype(BF16)
                        dv_ref[krows, :] += lax.dot_general(pcat, docat, _T0, preferred_element_type=F32)
                        dk_ref[krows, :] += lax.dot_general(dscat, qcat, _T0, preferred_element_type=F32)
                        dsrow = jnp.concatenate([ds0, ds1], axis=1).astype(BF16)
                        kcat = _stack_heads((k_ref[krows, :] * QK_SCALE).astype(BF16), masks)
                        t = jnp.dot(dsrow, kcat, preferred_element_type=F32)
                        dq = t if dq is None else dq + t
                    dq_ref[qrows, :] += dq

            _dil_sets(d, qset)

        def out_body(i, c):
            rows = pl.ds(pl.multiple_of(i * BLK, BLK), BLK)
            tab = (c_ref[rows, :], s1_ref[rows, :], s2_ref[rows, :])
            dqo_ref[rows, :] = _rope_apply(dq_ref[rows, :], *tab, transpose=True).astype(dqo_ref.dtype)
            dko_ref[rows, :] = _rope_apply(dk_ref[rows, :], *tab, transpose=True).astype(dko_ref.dtype)
            dvo_ref[rows, :] = dv_ref[rows, :].astype(dvo_ref.dtype)
            return c

        lax.fori_loop(0, NBLK, out_body, 0)

    spec = lambda off: pl.BlockSpec((None, SEQ, LANES), lambda b, j: (b, 0, off + j))
    ospec = pl.BlockSpec((None, SEQ, LANES), lambda b, j: (b, 0, j))
    tspec = pl.BlockSpec((SEQ, LANES), lambda b, j: (0, 0))
    osd = jax.ShapeDtypeStruct((nb, SEQ, DIL_W), BF16)
    return pl.pallas_call(
        body, out_shape=(osd, osd, osd), grid=(nb, ncol),
        in_specs=[spec(0), spec(ncol), spec(2 * ncol), spec(do_off), ospec, ospec, tspec, tspec, tspec],
        out_specs=(ospec, ospec, ospec),
        scratch_shapes=[pltpu.VMEM((SEQ, LANES), F32)] * 5,
        compiler_params=_cparams(dimension_semantics=("parallel", "parallel")), name=name,
    )(qkv, qkv, qkv, do, o, lse, *tabs)


FOX_GROUP = 4
assert NBLK % FOX_GROUP == 0
_FOX_COLS = tuple(c // LANES for c in (C_FQ, C_FK, C_FV))


def _fox_specs():
    cols = [pl.BlockSpec((None, SEQ, LANES), (lambda b, j, off=off: (b, 0, off + j))) for off in _FOX_COLS]
    ospec = pl.BlockSpec((None, SEQ, LANES), lambda b, j: (b, 0, j))
    crspec = pl.BlockSpec((None, None, NBLK, 8, BLK), lambda b, j: (b, j, 0, 0, 0))
    return cols, ospec, crspec


def _fox_key_rows(t, e):
    return pl.ds(pl.multiple_of((FOX_GROUP * t + e) * BLK, BLK), BLK)


def _fox_fwd(p3, crow, *, name):
    nb = p3.shape[0]
    g = FOX_GROUP

    def body(q_ref, k_ref, v_ref, cr_ref, o_ref, lse_ref):
        masks = _head_masks(2)
        tri = _tri_bias(True)

        def qk(qcat, t):
            return tuple(lax.dot_general(qcat, k_ref[_fox_key_rows(t, e), :], _NT, preferred_element_type=F32) for e in range(g))

        def consume(ss, t, state, nblk, diag):
            m0, m1, l0, l1, acc = state
            us = []
            for e in range(nblk):
                cr = cr_ref[g * t + e]
                u0 = ss[e][:BLK] - cr[0:1, :]
                u1 = ss[e][BLK:] - cr[1:2, :]
                if diag and e == nblk - 1:
                    u0, u1 = u0 + tri, u1 + tri
                us.append((u0, u1))
            x0 = functools.reduce(jnp.maximum, [u[0] for u in us])
            x1 = functools.reduce(jnp.maximum, [u[1] for u in us])
            n0 = jnp.maximum(m0, jnp.max(x0, axis=1, keepdims=True))
            n1 = jnp.maximum(m1, jnp.max(x1, axis=1, keepdims=True))
            a0, a1 = jnp.exp(m0 - n0), jnp.exp(m1 - n1)
            acc = acc * jnp.where(masks[0], a0, a1)
            t0 = t1 = None
            for e in range(nblk):
                p0, p1 = jnp.exp(us[e][0] - n0), jnp.exp(us[e][1] - n1)
                t0 = p0 if t0 is None else t0 + p0
                t1 = p1 if t1 is None else t1 + p1
                pcat = jnp.concatenate([p0, p1], axis=1)
                hi = pcat.astype(BF16)
                lo = (pcat - hi.astype(F32)).astype(BF16)
                vcat = _stack_heads(v_ref[_fox_key_rows(t, e), :], masks)
                acc = acc + jnp.dot(hi, vcat, preferred_element_type=F32) + jnp.dot(lo, vcat, preferred_element_type=F32)
            l0 = a0 * l0 + jnp.sum(t0, axis=1, keepdims=True)
            l1 = a1 * l1 + jnp.sum(t1, axis=1, keepdims=True)
            return n0, n1, l0, l1, acc

        def gbody(ng, c):
            neg = jnp.full((BLK, 1), NEG_INF, F32)
            z1 = jnp.zeros((BLK, 1), F32)
            rows = [pl.ds(pl.multiple_of((g * ng + a) * BLK, BLK), BLK) for a in range(g)]
            qcats = [_stack_heads(q_ref[rows[a], :] * QK_SCALE, masks) for a in range(g)]
            first = [qk(qcats[a], 0) for a in range(g)]
            done = []
            for a in range(g):
                def step(t, cc, qcat=qcats[a]):
                    ss, st = cc
                    nxt = qk(qcat, t + 1)
                    return nxt, consume(ss, t, st, g, False)

                done.append(lax.fori_loop(0, ng, step, (first[a], (neg, neg, z1, z1, jnp.zeros((BLK, LANES), F32)))))
            for a in range(g):
                ss, state = done[a]
                m0, m1, l0, l1, acc = consume(ss, ng, state, a + 1, True)
                o_ref[rows[a], :] = acc / jnp.where(masks[0], l0, l1)
                lse_ref[rows[a], :] = jnp.where(masks[0], m0 + jnp.log(l0), m1 + jnp.log(l1))
            return c

        lax.fori_loop(0, NBLK // g, gbody, 0)

    cols, ospec, crspec = _fox_specs()
    osd = jax.ShapeDtypeStruct((nb, SEQ, FOX_W), F32)
    return pl.pallas_call(
        body, out_shape=(osd, osd), grid=(nb, FOX_W // LANES), in_specs=cols + [crspec], out_specs=(ospec, ospec),
        compiler_params=_cparams(dimension_semantics=("parallel", "parallel")), name=name,
    )(p3, p3, p3, crow)


def _fox_bwd(p3, crow, do, o, lse, *, do_off, name):
    nb = p3.shape[0]
    g = FOX_GROUP
    hd = HEAD_DIM

    def body(q_ref, k_ref, v_ref, cr_ref, do_ref, o_ref, lse_ref, dq_ref, dko_ref, dvo_ref, dcr_ref, dk_ref, dv_ref):
        masks = _head_masks(2)
        tri = _tri_bias(True)
        dk_ref[...] = jnp.zeros_like(dk_ref)
        dv_ref[...] = jnp.zeros_like(dv_ref)
        dcr_ref[...] = jnp.zeros_like(dcr_ref)

        def products(qcat, docat, t):
            out = []
            for e in range(g):
                krows = _fox_key_rows(t, e)
                out.append(lax.dot_general(qcat, k_ref[krows, :], _NT, preferred_element_type=F32))
                out.append(lax.dot_general(docat, v_ref[krows, :], _NT, preferred_element_type=F32))
            return tuple(out)

        def consume(prod, t, ctx, dq, nblk, diag):
            qcat, docat, lse0, lse1, dl0, dl1 = ctx
            for e in range(nblk):
                jb = g * t + e
                krows = _fox_key_rows(t, e)
                s, dp = prod[2 * e], prod[2 * e + 1]
                cr = cr_ref[jb]
                u0 = s[:BLK] - cr[0:1, :]
                u1 = s[BLK:] - cr[1:2, :]
                if diag and e == nblk - 1:
                    u0, u1 = u0 + tri, u1 + tri
                p0 = jnp.exp(u0 - lse0)
                p1 = jnp.exp(u1 - lse1)
                ds0 = p0 * (dp[:BLK] - dl0)
                ds1 = p1 * (dp[BLK:] - dl1)
                dcr_ref[jb, 0:1, :] += jnp.sum(ds0, axis=0, keepdims=True)
                dcr_ref[jb, 1:2, :] += jnp.sum(ds1, axis=0, keepdims=True)
                pcat = jnp.concatenate([p0, p1], axis=0).astype(BF16)
                dscat = jnp.concatenate([ds0, ds1], axis=0).astype(BF16)
                dv_ref[krows, :] += lax.dot_general(pcat, docat, _T0, preferred_element_type=F32)
                dk_ref[krows, :] += lax.dot_general(dscat, qcat, _T0, preferred_element_type=F32)
                dsrow = jnp.concatenate([ds0, ds1], axis=1).astype(BF16)
                dq = dq + jnp.dot(dsrow, _stack_heads(k_ref[krows, :] * QK_SCALE, masks), preferred_element_type=F32)
            return dq

        def gbody(ng, c):
            ctxs, rows = [], []
            for a in range(g):
                r = pl.ds(pl.multiple_of((g * ng + a) * BLK, BLK), BLK)
                qcat = _stack_heads(q_ref[r, :] * QK_SCALE, masks)
                dob = do_ref[r, :].astype(BF16)
                prod = dob.astype(F32) * o_ref[r, :]
                z = jnp.zeros_like(prod)
                dl0 = jnp.sum(jnp.where(masks[0], prod, z), axis=1, keepdims=True)
                dl1 = jnp.sum(jnp.where(masks[1], prod, z), axis=1, keepdims=True)
                lseb = lse_ref[r, :]
                ctxs.append((qcat, _stack_heads(dob, masks), lseb[:, 0:1], lseb[:, hd:hd + 1], dl0, dl1))
                rows.append(r)
            first = [products(ctxs[a][0], ctxs[a][1], 0) for a in range(g)]
            done = []
            for a in range(g):
                def step(t, cc, ctx=ctxs[a]):
                    pr, dq = cc
                    nxt = products(ctx[0], ctx[1], t + 1)
                    return nxt, consume(pr, t, ctx, dq, g, False)

                done.append(lax.fori_loop(0, ng, step, (first[a], jnp.zeros((BLK, LANES), F32))))
            for a in range(g):
                pr, dq = done[a]
                dq_ref[rows[a], :] = consume(pr, ng, ctxs[a], dq, a + 1, True).astype(dq_ref.dtype)
            return c

        lax.fori_loop(0, NBLK // g, gbody, 0)
        dko_ref[...] = dk_ref[...].astype(dko_ref.dtype)
        dvo_ref[...] = dv_ref[...].astype(dvo_ref.dtype)

    cols, ospec, crspec = _fox_specs()
    dospec = pl.BlockSpec((None, SEQ, LANES), lambda b, j: (b, 0, do_off + j))
    osd = jax.ShapeDtypeStruct((nb, SEQ, FOX_W), BF16)
    return pl.pallas_call(
        body, out_shape=(osd, osd, osd, jax.ShapeDtypeStruct((nb, FOX_W // LANES, NBLK, 8, BLK), F32)),
        grid=(nb, FOX_W // LANES), in_specs=cols + [crspec, dospec, ospec, ospec], out_specs=(ospec, ospec, ospec, crspec),
        scratch_shapes=[pltpu.VMEM((SEQ, LANES), F32)] * 2,
        compiler_params=_cparams(dimension_semantics=("parallel", "parallel")), name=name,
    )(p3, p3, p3, crow, do, o, lse)


def _mem_cfg():
    return _AttnCfg(e=MEM_HEAD_DIM, tq=256, tk=MEM_LEN, lq=SEQ, lk=MEM_LEN, causal=False, window=None, ncol=MEM_HEADS,
                    qcol=lambda j: C_MQ // LANES + j, kcol=lambda j: j, vcol=lambda j: MEM_HEADS + j)


_B1, _B2 = FOX_W // LANES, (FOX_W + DIL_W) // LANES


def _dy_gate_bwd(dx2b, wo, fox, dil, memo, p16, *, tm, tn, name):
    t, d = dx2b.shape
    assert FOX_W % tn == 0 and DIL_W % tn == 0 and MEM_W % tn == 0 and all(c % tn == 0 for c in (C_FG, C_DG, C_MG))
    n1, n2, n3 = FOX_W // tn, (FOX_W + DIL_W) // tn, MIX_W // tn

    def body(dx_ref, w_ref, f_ref, d_ref, m_ref, g_ref, da_ref, dg_ref):
        j = pl.program_id(1)
        dyv = lax.dot_general(dx_ref[...], w_ref[...], _NT, preferred_element_type=F32)
        a = jnp.where(j < n1, f_ref[...], jnp.where(j < n2, d_ref[...], m_ref[...]))
        gt = g_ref[...].astype(F32)
        sg = 1.0 / (1.0 + jnp.exp(-gt))
        da_ref[...] = (dyv * gt * sg).astype(da_ref.dtype)
        dg_ref[...] = (dyv * a * sg * (1.0 + gt * (1.0 - sg))).astype(dg_ref.dtype)

    def gcol(j):
        return jnp.where(j < n1, C_FG // tn + j, jnp.where(j < n2, C_DG // tn + j - n1, C_MG // tn + j - n2))

    tile = pl.BlockSpec((tm, tn), lambda i, j: (i, j))
    return pl.pallas_call(
        body,
        out_shape=(jax.ShapeDtypeStruct((t, MIX_W), BF16), jax.ShapeDtypeStruct((t, MIX_W), BF16)),
        grid=(t // tm, n3),
        in_specs=[pl.BlockSpec((tm, d), lambda i, j: (i, 0)), pl.BlockSpec((tn, d), lambda i, j: (j, 0)),
                  pl.BlockSpec((tm, tn), lambda i, j: (i, jnp.minimum(j, n1 - 1))),
                  pl.BlockSpec((tm, tn), lambda i, j: (i, jnp.clip(j - n1, 0, n2 - n1 - 1))),
                  pl.BlockSpec((tm, tn), lambda i, j: (i, jnp.clip(j - n2, 0, n3 - n2 - 1))),
                  pl.BlockSpec((tm, tn), lambda i, j: (i, gcol(j)))],
        out_specs=(tile, tile),
        compiler_params=_cparams(dimension_semantics=("parallel", "parallel")),
        name=name,
    )(dx2b, wo, fox, dil, memo, p16)


def _silu(g):
    return g / (1.0 + jnp.exp(-g))


def _out_loss(fox, dil, memo, p16, wo, x, tgt, gfin, *, tm, name):
    t, d = x.shape
    n_feat = float(d)

    def body(f_ref, d_ref, m_ref, fg_ref, dg_ref, mg_ref, w_ref, x_ref, t_ref, g_ref, y_ref, dx_ref, dxb_ref, st_ref):
        i = pl.program_id(0)

        @pl.when(i == 0)
        def _():
            st_ref[...] = jnp.zeros_like(st_ref)

        y = jnp.concatenate([(a_ref[...] * _silu(gt_ref[...].astype(F32))).astype(BF16)
                             for a_ref, gt_ref in ((f_ref, fg_ref), (d_ref, dg_ref), (m_ref, mg_ref))], axis=1)
        y_ref[...] = y
        x2 = x_ref[...] + jnp.dot(y, w_ref[...], preferred_element_type=F32)
        r = lax.rsqrt(jnp.mean(x2 * x2, axis=-1, keepdims=True) + RMS_EPS)
        nrm = x2 * r
        gv = g_ref[...]
        err = nrm * gv - t_ref[...]
        dout = err * (1.0 / n_feat)
        dn = dout * gv
        dx2 = r * (dn - nrm * jnp.mean(dn * nrm, axis=-1, keepdims=True))
        dx_ref[...] = dx2
        dxb_ref[...] = dx2.astype(dxb_ref.dtype)
        st_ref[0:1, :] += jnp.sum(dout * nrm, axis=0, keepdims=True)
        st_ref[1:2, :] += (0.5 / n_feat) * jnp.sum(err * err, axis=0, keepdims=True)

    row = pl.BlockSpec((tm, d), lambda i: (i, 0))
    whole = lambda w: pl.BlockSpec((tm, w), lambda i: (i, 0))
    gate = lambda w, col: pl.BlockSpec((tm, w), lambda i: (i, col // w))
    return pl.pallas_call(
        body,
        out_shape=(jax.ShapeDtypeStruct((t, MIX_W), BF16), jax.ShapeDtypeStruct((t, d), F32), jax.ShapeDtypeStruct((t, d), BF16),
                   jax.ShapeDtypeStruct((8, d), F32)),
        grid=(t // tm,),
        in_specs=[whole(FOX_W), whole(DIL_W), whole(MEM_W), gate(FOX_W, C_FG), gate(DIL_W, C_DG), gate(MEM_W, C_MG),
                  pl.BlockSpec((MIX_W, d), lambda i: (0, 0)), row, row, pl.BlockSpec((1, d), lambda i: (0, 0))],
        out_specs=(pl.BlockSpec((tm, MIX_W), lambda i: (i, 0)), row, row, pl.BlockSpec((8, d), lambda i: (0, 0))),
        compiler_params=_cparams(dimension_semantics=("arbitrary",)),
        name=name,
    )(fox, dil, memo, p16, p16, p16, wo, x, tgt, gfin)


def _dh_rms_bwd(dp, w, x, g, resid, *, tm, tk, name):
    t, d = x.shape
    kdim = dp.shape[1]
    nk = kdim // tk

    def body(*refs):
        if resid is not None:
            dp_ref, w_ref, x_ref, g_ref, r_ref, dx_ref, gg_ref, acc_ref = refs
        else:
            dp_ref, w_ref, x_ref, g_ref, dx_ref, gg_ref, acc_ref = refs
        i = pl.program_id(0)
        k = pl.program_id(1)

        @pl.when(jnp.logical_and(i == 0, k == 0))
        def _():
            gg_ref[...] = jnp.zeros_like(gg_ref)

        prod = lax.dot_general(dp_ref[...], w_ref[...], _NT, preferred_element_type=F32)

        @pl.when(k == 0)
        def _():
            acc_ref[...] = prod

        @pl.when(k > 0)
        def _():
            acc_ref[...] += prod

        @pl.when(k == nk - 1)
        def _():
            dh = acc_ref[...]
            xv = x_ref[...]
            r = lax.rsqrt(jnp.mean(xv * xv, axis=-1, keepdims=True) + RMS_EPS)
            nrm = xv * r
            dn = dh * g_ref[...]
            dx = r * (dn - nrm * jnp.mean(dn * nrm, axis=-1, keepdims=True))
            if resid is not None:
                dx = dx + r_ref[...]
            dx_ref[...] = dx
            gg_ref[0:1, :] += jnp.sum(dh * nrm, axis=0, keepdims=True)

    row = pl.BlockSpec((tm, d), lambda i, k: (i, 0))
    in_specs = [pl.BlockSpec((tm, tk), lambda i, k: (i, k)), pl.BlockSpec((d, tk), lambda i, k: (0, k)), row,
                pl.BlockSpec((1, d), lambda i, k: (0, 0))]
    args = [dp, w, x, g]
    if resid is not None:
        in_specs.append(row)
        args.append(resid)
    return pl.pallas_call(
        body,
        out_shape=(jax.ShapeDtypeStruct((t, d), F32), jax.ShapeDtypeStruct((8, d), F32)),
        grid=(t // tm, nk),
        in_specs=in_specs,
        out_specs=(row, pl.BlockSpec((8, d), lambda i, k: (0, 0))),
        scratch_shapes=[pltpu.VMEM((tm, d), F32)],
        compiler_params=_cparams(dimension_semantics=("arbitrary", "arbitrary")),
        name=name,
    )(*args)


_FLOG0 = 4 * FOX_W
_W_IN_SEGMENTS = ((0, _FLOG0, 0), (_FLOG0, _FLOG0 + FOX_HEADS, PW), (_FLOG0 + FOX_HEADS, IN_W, C_DQ))
SHARD_W = IN_W // N_CHIPS


def _rearrange_w_in(shards):
    def cols(lo, hi):
        parts = []
        for k in range(N_CHIPS):
            a, b = max(lo, k * SHARD_W), min(hi, (k + 1) * SHARD_W)
            if a < b:
                parts.append(shards[k][:, a - k * SHARD_W:b - k * SHARD_W])
        return parts

    (a0, a1, _), (f0, f1, _), (b0, b1, _) = _W_IN_SEGMENTS
    pad = jnp.zeros((shards[0].shape[0], PWF - PW - FOX_HEADS), shards[0].dtype)
    return jnp.concatenate(cols(a0, a1) + cols(b0, b1) + cols(f0, f1) + [pad], axis=1)


def _w_in_grad_slabs(g):
    slabs = []
    for k in range(N_CHIPS):
        parts = []
        for lo, hi, at in _W_IN_SEGMENTS:
            a, b = max(lo, k * SHARD_W), min(hi, (k + 1) * SHARD_W)
            if a < b:
                parts.append(g[:, at + a - lo:at + b - lo])
        slabs.append(jnp.concatenate(parts, axis=1))
    return jnp.stack(slabs, axis=0)


def _local_grads(x, mem, norm_g, w_r, b_forget, mem_norm_g, w_kv, w_o, final_norm_g, tgt, start_reduce=None):
    nb = x.shape[0]
    t = nb * SEQ
    x2d = x.reshape(t, D_MODEL)
    tgt2d = tgt.reshape(t, D_MODEL)
    tabs = _rope_tables()
    bpad = jnp.pad(b_forget.reshape(1, FOX_HEADS), ((0, 0), (0, LANES - FOX_HEADS)))

    h = _rms_fwd(x2d, norm_g.reshape(1, D_MODEL), tm=512, name="rms_x")
    p16, dqkv = _proj(h, w_r, tabs, n=PW, tm=2048, tn=256, name="proj")
    flog = _matmul(h, w_r[:, PW:PW + LANES], out_dtype=F32, tm=1024, tn=LANES, tk=D_MODEL, name="proj_flog")
    c12 = _flog_fwd(flog, bpad, nb=nb, ts=256, name="flog_fwd")

    crow = c12[:, :FOX_HEADS].reshape(nb, NBLK, BLK, FOX_HEADS // 2, 2).transpose(0, 3, 1, 4, 2)
    crow = jnp.pad(crow, ((0, 0), (0, 0), (0, 0), (0, 6), (0, 0)))
    p3 = p16.reshape(nb, SEQ, PW)
    fox, fox_lse = _fox_fwd(p3, crow, name="fox_fwd")

    dqkv3 = dqkv.reshape(nb, SEQ, 3 * DIL_W)
    dil, dil_lse = _dil_fwd(dqkv3, name="dil_fwd")

    mh = _rms_fwd(mem.reshape(nb * MEM_LEN, D_MODEL), mem_norm_g.reshape(1, D_MODEL), tm=nb * MEM_LEN, name="rms_mem")
    mkv = _matmul(mh, w_kv, out_dtype=BF16, tm=nb * MEM_LEN, tn=512, tk=D_MODEL, name="mem_kv")
    mkv3 = mkv.reshape(nb, MEM_LEN, 2 * MEM_W)
    mcfg = _mem_cfg()
    memo, mem_lse = _attn_fwd(mcfg, p3, mkv3, mkv3, out_cols=MEM_W, name="mem_fwd")

    fox2, dil2, memo2 = fox.reshape(t, FOX_W), dil.reshape(t, DIL_W), memo.reshape(t, MEM_W)
    y, dx2, dx2b, st = _out_loss(fox2, dil2, memo2, p16, w_o, x2d, tgt2d, final_norm_g.reshape(1, D_MODEL), tm=256,
                                 name="out_loss")

    g_wo = _matmul(y, dx2b, mode="tn", out_dtype=F32, tm=1024, tn=512, tk=1024, name="grad_w_out")
    datt, dgate = _dy_gate_bwd(dx2b, w_o, fox2, dil2, memo2, p16, tm=1024, tn=256, name="dy_gate_bwd")
    datt3 = datt.reshape(nb, SEQ, MIX_W)

    dfq, dfk, dfv, dcr = _fox_bwd(p3, crow, datt3, fox, fox_lse, do_off=0, name="fox_bwd")
    dcol = -dcr[:, :, :, :2, :].transpose(0, 2, 4, 1, 3).reshape(t, FOX_HEADS)
    dcol = jnp.pad(dcol, ((0, 0), (0, LANES - FOX_HEADS)))
    dflog, gb = _flog_bwd(dcol, flog, bpad, nb=nb, ts=256, name="flog_bwd")

    ddq, ddk, ddv = _dil_bwd(dqkv3, datt3, dil, dil_lse, tabs, do_off=_B1, name="dil_bwd")

    dmq, dmk, dmv = _attn_bwd(mcfg, p3, mkv3, mkv3, datt3, memo, mem_lse, out_cols=MEM_W, kv_cols=MEM_W, do_off=_B2,
                              name="mem_bwd")
    dmkv = jnp.concatenate([dmk, dmv], axis=-1).reshape(nb * MEM_LEN, 2 * MEM_W).astype(BF16)
    g_wkv = _matmul(mh, dmkv, mode="tn", out_dtype=F32, tm=512, tn=512, tk=nb * MEM_LEN, name="grad_w_kv")
    _, gmn = _dh_rms_bwd(dmkv, w_kv, mem.reshape(nb * MEM_LEN, D_MODEL), mem_norm_g.reshape(1, D_MODEL), None,
                         tm=nb * MEM_LEN, tk=2 * MEM_W, name="mem_rms_bwd")

    flat = lambda a: a.reshape(t, -1)
    dp = jnp.concatenate([flat(dfq), flat(dfk), flat(dfv), dgate[:, :FOX_W], flat(ddq), flat(ddk), flat(ddv),
                          dgate[:, FOX_W:FOX_W + DIL_W], flat(dmq).astype(BF16), dgate[:, FOX_W + DIL_W:], dflog,
                          jnp.zeros((t, PWF - PW - LANES), BF16)], axis=1)
    g_wr = _matmul(h, dp, mode="tn", out_dtype=F32, tm=D_MODEL, tn=512, tk=t, name="grad_w_in")
    gain = norm_g.reshape(1, D_MODEL)
    if start_reduce is not None:
        gain = gain + start_reduce(g_wr, g_wkv, g_wo)[0:1, 0:1]
    gx, gng = _dh_rms_bwd(dp, w_r, x2d, gain, dx2, tm=512, tk=PWF // 3, name="in_rms_bwd")

    gb_row = jnp.pad(gb[0:1, :], ((0, 0), (0, D_MODEL - LANES)))
    small = jnp.concatenate([gng[0:1], gmn[0:1], st[0:1], gb_row, st[1:2], jnp.zeros((3, D_MODEL), F32)], axis=0)
    return gx.reshape(nb, SEQ, D_MODEL), g_wr, g_wkv, g_wo, small


MESH = pl.DeviceIdType.MESH
ANY = pl.BlockSpec(memory_space=pl.ANY)


def _place():
    x, y, c = lax.axis_index("x"), lax.axis_index("y"), lax.axis_index("c")
    other_chips = [(1 - x, y), (x, 1 - y), (1 - x, 1 - y)]
    return x, y, c, other_chips


def _gather_weights(shards):
    n = len(shards)

    def body(*refs):
        in_refs, out_refs = refs[:n], refs[n:2 * n]
        send_sems, recv_sems = refs[2 * n:]
        x, y, c, chips = _place()
        me_chip = 2 * x + y
        sibling = (x, y, 1 - c)

        def half(ref, pc, rows):
            return ref.at[pl.ds(pc * (rows // 2), rows // 2), :]

        def rcopy(k, src, dst, to):
            return pltpu.make_async_remote_copy(src_ref=src, dst_ref=dst, send_sem=send_sems.at[k], recv_sem=recv_sems.at[k],
                                                device_id=to, device_id_type=MESH)

        sends = []
        for t in range(n):
            rows = shards[t].shape[0]
            for j, chip in enumerate(chips):
                cp = rcopy(6 * t + j, half(in_refs[t], c, rows), half(out_refs[t].at[me_chip], c, rows), (*chip, c))
                cp.start()
                sends.append(cp)
        for t in range(n):
            rows = shards[t].shape[0]
            for j, chip in enumerate(chips):
                slot = out_refs[t].at[2 * chip[0] + chip[1]]
                rcopy(6 * t + j, half(slot, c, rows), half(slot, c, rows), sibling).wait_recv()
                fw = rcopy(6 * t + 3 + j, half(slot, c, rows), half(slot, c, rows), sibling)
                fw.start()
                sends.append(fw)
        for t in range(n):
            rows = shards[t].shape[0]
            for j, chip in enumerate(chips):
                slot = out_refs[t].at[2 * chip[0] + chip[1]]
                rcopy(6 * t + 3 + j, half(slot, 1 - c, rows), half(slot, 1 - c, rows), sibling).wait_recv()
        for cp in sends:
            cp.wait_send()

    return pl.pallas_call(
        body,
        out_shape=tuple(jax.ShapeDtypeStruct((N_CHIPS,) + s.shape, s.dtype) for s in shards),
        in_specs=[ANY] * n,
        out_specs=tuple([ANY] * n),
        scratch_shapes=[pltpu.SemaphoreType.DMA((6 * n,)), pltpu.SemaphoreType.DMA((6 * n,))],
        name="gather_weights",
    )(*shards)


def _pair_exchange(gs):
    n = len(gs)

    def body(*refs):
        g_refs, r_refs = refs[:n], refs[n:2 * n]
        send_sems, recv_sems = refs[2 * n:]
        x, y, c, _ = _place()
        cps = []
        for t in range(n):
            hr = gs[t].shape[1] // 2
            cp = pltpu.make_async_remote_copy(src_ref=g_refs[t].at[:, pl.ds((1 - c) * hr, hr), :], dst_ref=r_refs[t],
                                              send_sem=send_sems.at[t], recv_sem=recv_sems.at[t],
                                              device_id=(x, y, 1 - c), device_id_type=MESH)
            cp.start()
            cps.append(cp)
        for cp in cps:
            cp.wait()

    return pl.pallas_call(
        body,
        out_shape=tuple(jax.ShapeDtypeStruct((N_CHIPS, g.shape[1] // 2, g.shape[2]), g.dtype) for g in gs),
        in_specs=[ANY] * n,
        out_specs=tuple([ANY] * n),
        scratch_shapes=[pltpu.SemaphoreType.DMA((n,)), pltpu.SemaphoreType.DMA((n,))],
        name="pair_exchange",
    )(*gs)


def _chip_exchange(ps):
    n = len(ps)

    def body(*refs):
        p_refs, o_refs = refs[:n], refs[n:2 * n]
        send_sems, recv_sems = refs[2 * n:]
        x, y, c, chips = _place()
        me_chip = 2 * x + y
        cps = []
        for t in range(n):
            for j, chip in enumerate(chips):
                cp = pltpu.make_async_remote_copy(src_ref=p_refs[t].at[2 * chip[0] + chip[1]], dst_ref=o_refs[t].at[me_chip],
                                                  send_sem=send_sems.at[3 * t + j], recv_sem=recv_sems.at[3 * t + j],
                                                  device_id=(*chip, c), device_id_type=MESH)
                cp.start()
                cps.append(cp)
        for cp in cps:
            cp.wait()

    return pl.pallas_call(
        body,
        out_shape=tuple(jax.ShapeDtypeStruct(p.shape, p.dtype) for p in ps),
        in_specs=[ANY] * n,
        out_specs=tuple([ANY] * n),
        scratch_shapes=[pltpu.SemaphoreType.DMA((3 * n,)), pltpu.SemaphoreType.DMA((3 * n,))],
        name="chip_exchange",
    )(*ps)


_HBM = pl.BlockSpec(memory_space=pltpu.HBM)
_SEM = pl.BlockSpec(memory_space=pltpu.SEMAPHORE)
_DATAFLOW = pltpu.SideEffectType.DATAFLOW_SIDE_EFFECTING


def _chip_copies(p_refs, land_refs, send_sems, recv_sems):
    x, y, c, chips = _place()
    me_chip = 2 * x + y
    return [pltpu.make_async_remote_copy(src_ref=p_refs[t].at[2 * chip[0] + chip[1]], dst_ref=land_refs[t].at[me_chip],
                                         send_sem=send_sems.at[3 * t + j], recv_sem=recv_sems.at[3 * t + j],
                                         device_id=(*chip, c), device_id_type=MESH)
            for t in range(len(p_refs)) for j, chip in enumerate(chips)]


def _chip_exchange_start(ps):
    n = len(ps)

    def body(*refs):
        p_refs, land_refs = refs[:n], refs[n:2 * n]
        send_sems, recv_sems = refs[2 * n:2 * n + 2]
        token = refs[-1]
        for cp in _chip_copies(p_refs, land_refs, send_sems, recv_sems):
            cp.start()
        token[...] = jnp.zeros_like(token)

    hbm = [pltpu.HBM(p.shape, p.dtype) for p in ps]
    args = [pltpu.with_memory_space_constraint(p, pltpu.HBM) for p in ps]
    args += [pltpu.with_memory_space_constraint(lax.empty(p.shape, p.dtype), pltpu.HBM) for p in ps]
    out = pl.pallas_call(
        body,
        name="chip_exchange_start",
        out_shape=(pltpu.SemaphoreType.DMA((3 * n,)), pltpu.SemaphoreType.DMA((3 * n,)), *hbm, *hbm,
                   jax.ShapeDtypeStruct((8, LANES), F32)),
        in_specs=[_HBM] * (2 * n),
        out_specs=(_SEM, _SEM, *([_HBM] * (2 * n)), pl.BlockSpec(memory_space=pltpu.VMEM)),
        input_output_aliases={i: 2 + i for i in range(2 * n)},
        compiler_params=pltpu.CompilerParams(has_side_effects=_DATAFLOW),
    )(*args)
    return out[0], out[1], out[2:2 + n], out[2 + n:2 + 2 * n], out[-1]


def _chip_exchange_wait(send_sems, recv_sems, p_thru, land_thru, after):
    n = len(p_thru)

    def body(*refs):
        p_refs, land_refs = refs[:n], refs[n:2 * n]
        ssem, rsem = refs[2 * n:2 * n + 2]
        for cp in _chip_copies(p_refs, land_refs, ssem, rsem):
            cp.wait_send()
            cp.wait_recv()

    hbm = [pltpu.HBM(p.shape, p.dtype) for p in p_thru]
    out = pl.pallas_call(
        body,
        name="chip_exchange_wait",
        out_shape=(*hbm, *hbm),
        in_specs=[_HBM] * (2 * n) + [_SEM, _SEM, ANY],
        out_specs=tuple([_HBM] * (2 * n)),
        input_output_aliases={i: i for i in range(2 * n)},
        compiler_params=pltpu.CompilerParams(has_side_effects=_DATAFLOW),
    )(*p_thru, *land_thru, send_sems, recv_sems, after)
    return out[:n], out[n:]


def _pair_swap(rs):
    n = len(rs)

    def body(*refs):
        r_refs, o_refs = refs[:n], refs[n:2 * n]
        send_sems, recv_sems = refs[2 * n:]
        x, y, c, _ = _place()
        cps = []
        for t in range(n):
            cp = pltpu.make_async_remote_copy(src_ref=r_refs[t], dst_ref=o_refs[t], send_sem=send_sems.at[t],
                                              recv_sem=recv_sems.at[t], device_id=(x, y, 1 - c), device_id_type=MESH)
            cp.start()
            cps.append(cp)
        for cp in cps:
            cp.wait()

    return pl.pallas_call(
        body,
        out_shape=tuple(jax.ShapeDtypeStruct(r.shape, r.dtype) for r in rs),
        in_specs=[ANY] * n,
        out_specs=tuple([ANY] * n),
        scratch_shapes=[pltpu.SemaphoreType.DMA((n,)), pltpu.SemaphoreType.DMA((n,))],
        name="pair_swap",
    )(*rs)


N_DEV = 8
LOSS_ROW = 4


def _small_allreduce(small):
    def body(s_ref, o_ref, all_ref, send_sems, recv_sems):
        x, y, c, _ = _place()
        me = 4 * x + 2 * y + c
        all_ref[me] = s_ref[...]
        cps = []
        for k in range(1, N_DEV):
            peer = tuple(1 - p if (k >> s) & 1 else p for p, s in ((x, 2), (y, 1), (c, 0)))
            cp = pltpu.make_async_remote_copy(src_ref=s_ref, dst_ref=all_ref.at[me], send_sem=send_sems.at[k - 1],
                                              recv_sem=recv_sems.at[k - 1], device_id=peer, device_id_type=MESH)
            cp.start()
            cps.append(cp)
        for cp in cps:
            cp.wait()
        tot = all_ref[0]
        for d in range(1, N_DEV):
            tot = tot + all_ref[d]
        o_ref[...] = tot
        o_ref[LOSS_ROW:LOSS_ROW + 1, :] = jnp.broadcast_to(jnp.sum(tot[LOSS_ROW:LOSS_ROW + 1, :], axis=1, keepdims=True),
                                                          (1, tot.shape[1]))

    vm = pl.BlockSpec(memory_space=pltpu.VMEM)
    return pl.pallas_call(
        body,
        out_shape=jax.ShapeDtypeStruct(small.shape, small.dtype),
        in_specs=[vm],
        out_specs=vm,
        scratch_shapes=[pltpu.VMEM((N_DEV,) + small.shape, small.dtype), pltpu.SemaphoreType.DMA((N_DEV - 1,)),
                        pltpu.SemaphoreType.DMA((N_DEV - 1,))],
        name="small_allreduce",
    )(small)


def _sum_pair(g, recv, cidx, *, tr, name):
    _, hr, cols = recv.shape
    nr = hr // tr

    def body(c_ref, g_ref, r_ref, o_ref):
        o_ref[...] = (g_ref[...] + r_ref[...]).astype(o_ref.dtype)

    grid_spec = pltpu.PrefetchScalarGridSpec(
        num_scalar_prefetch=1,
        grid=(N_CHIPS, nr),
        in_specs=[pl.BlockSpec((None, tr, cols), lambda k, i, c_ref: (k, c_ref[0] * nr + i, 0)),
                  pl.BlockSpec((None, tr, cols), lambda k, i, c_ref: (k, i, 0))],
        out_specs=pl.BlockSpec((None, tr, cols), lambda k, i, c_ref: (k, i, 0)),
    )
    return pl.pallas_call(body, out_shape=jax.ShapeDtypeStruct(recv.shape, BF16), grid_spec=grid_spec,
                          compiler_params=_cparams(), name=name)(cidx, g, recv)


def _sum_chips(p, *, tr, name):
    _, rows, cols = p.shape

    def body(p_ref, o_ref):
        tot = p_ref[0].astype(F32)
        for k in range(1, N_CHIPS):
            tot = tot + p_ref[k].astype(F32)
        o_ref[...] = tot

    return pl.pallas_call(
        body,
        out_shape=jax.ShapeDtypeStruct((rows, cols), F32),
        grid=(rows // tr,),
        in_specs=[pl.BlockSpec((N_CHIPS, tr, cols), lambda i: (0, i, 0))],
        out_specs=pl.BlockSpec((tr, cols), lambda i: (i, 0)),
        compiler_params=_cparams(),
        name=name,
    )(p)


def _adamw(w, g, m, v, *, tr, name):
    rows, cols = w.shape
    bc1 = 1.0 / (1.0 - ADAM_B1 ** ADAM_STEP)
    bc2 = 1.0 / (1.0 - ADAM_B2 ** ADAM_STEP)

    def body(w_ref, g_ref, m_ref, v_ref, d_ref, nm_ref, nv_ref):
        gv = g_ref[...]
        nm = ADAM_B1 * m_ref[...] + (1.0 - ADAM_B1) * gv
        nv = ADAM_B2 * v_ref[...] + (1.0 - ADAM_B2) * (gv * gv)
        d_ref[...] = -ADAM_LR * ((nm * bc1) / (jnp.sqrt(nv * bc2) + ADAM_EPS) + ADAM_WD * w_ref[...])
        nm_ref[...] = nm
        nv_ref[...] = nv

    spec = pl.BlockSpec((tr, cols), lambda i: (i, 0))
    sd = jax.ShapeDtypeStruct((rows, cols), F32)
    return pl.pallas_call(body, out_shape=(sd, sd, sd), grid=(rows // tr,), in_specs=[spec] * 4, out_specs=(spec,) * 3,
                          compiler_params=_cparams(), name=name)(w, g, m, v)


def _adamw_halves(w, own, sib, cidx, m, v, *, tr, name):
    rows, cols = w.shape
    hr = own.shape[0]
    nr = hr // tr
    assert rows == 2 * hr and hr % tr == 0
    bc1 = 1.0 / (1.0 - ADAM_B1 ** ADAM_STEP)
    bc2 = 1.0 / (1.0 - ADAM_B2 ** ADAM_STEP)

    def body(c_ref, w_ref, o_ref, s_ref, m_ref, v_ref, g_ref, d_ref, nm_ref, nv_ref):
        mine = (pl.program_id(0) // nr) == c_ref[0]
        gv = jnp.where(mine, o_ref[...], s_ref[...])
        nm = ADAM_B1 * m_ref[...] + (1.0 - ADAM_B1) * gv
        nv = ADAM_B2 * v_ref[...] + (1.0 - ADAM_B2) * (gv * gv)
        g_ref[...] = gv
        d_ref[...] = -ADAM_LR * ((nm * bc1) / (jnp.sqrt(nv * bc2) + ADAM_EPS) + ADAM_WD * w_ref[...])
        nm_ref[...] = nm
        nv_ref[...] = nv

    full = pl.BlockSpec((tr, cols), lambda i, c_ref: (i, 0))
    half = pl.BlockSpec((tr, cols), lambda i, c_ref: (i % nr, 0))
    sd = jax.ShapeDtypeStruct((rows, cols), F32)
    grid_spec = pltpu.PrefetchScalarGridSpec(num_scalar_prefetch=1, grid=(rows // tr,), in_specs=[full, half, half, full, full],
                                             out_specs=(full,) * 4)
    return pl.pallas_call(body, out_shape=(sd,) * 4, grid_spec=grid_spec, compiler_params=_cparams(), name=name)(
        cidx, w, own, sib, m, v)


def _pack_small(norm, mem_norm, final_norm, b_forget):
    rows = [norm.reshape(1, D_MODEL), mem_norm.reshape(1, D_MODEL), final_norm.reshape(1, D_MODEL),
            jnp.pad(b_forget.reshape(1, FOX_HEADS), ((0, 0), (0, D_MODEL - FOX_HEADS))), jnp.zeros((4, D_MODEL), F32)]
    return jnp.concatenate(rows, axis=0)


def _unpack_small(a):
    return a[0:1], a[3:4, :FOX_HEADS], a[1:2], a[2]


def kernel(x, mem, norm_g, w_in, b_forget, mem_norm_g, w_mem_kv, w_out, final_norm_g, loss_target, m_norm_g, m_w_in, m_b_forget, m_mem_norm_g, m_w_mem_kv, m_w_out, m_final_norm_g, v_norm_g, v_w_in, v_b_forget, v_mem_norm_g, v_w_mem_kv, v_w_out, v_final_norm_g):
    core = lax.axis_index("c").astype(jnp.int32)
    me_chip = (2 * lax.axis_index("x") + lax.axis_index("y")).astype(jnp.int32)
    cidx = core.reshape(1)

    def own_slot(arr, own):
        return lax.dynamic_update_slice(arr, own[None].astype(arr.dtype), (me_chip,) + (0,) * own.ndim)

    mine = [w_in[0].astype(BF16), w_mem_kv[0].astype(BF16), w_out[0].astype(BF16)]
    g_in, g_kv, g_out = (own_slot(g, s) for g, s in zip(_gather_weights(mine), mine))
    w_r = _rearrange_w_in([g_in[k] for k in range(N_CHIPS)])
    w_kv = g_kv.reshape(D_MODEL, 2 * MEM_W)
    w_o = g_out.reshape(MIX_W, D_MODEL)

    trs = (128, 128, 256)
    names = ("w_in", "w_mem_kv", "w_out")
    flight = []

    def start_reduce(g_wr, g_wkv, g_wo):
        slabs = [_w_in_grad_slabs(g_wr),
                 g_wkv.reshape(N_CHIPS, D_MODEL // N_CHIPS, 2 * MEM_W),
                 g_wo.reshape(N_CHIPS, MIX_W // N_CHIPS, D_MODEL)]
        recv = _pair_exchange(slabs)
        pair = [_sum_pair(g, r, cidx, tr=tr, name=f"sum_pair_{nm}") for g, r, tr, nm in zip(slabs, recv, trs, names)]
        *handles, token = _chip_exchange_start(pair)
        flight.extend(handles)
        return token

    gx, g_wr, g_wkv, g_wo, small = _local_grads(x, mem, norm_g, w_r, b_forget, mem_norm_g, w_kv, w_o, final_norm_g, loss_target,
                                                start_reduce=start_reduce)

    send_sems, recv_sems, pair, land = flight
    pair, landed = _chip_exchange_wait(send_sems, recv_sems, pair, land, small)
    got = [lax.dynamic_update_slice(g, lax.dynamic_slice(p, (me_chip, 0, 0), (1,) + p.shape[1:]), (me_chip, 0, 0))
           for g, p in zip(landed, pair)]
    red = [_sum_chips(p, tr=tr, name=f"sum_chips_{nm}") for p, tr, nm in zip(got, trs, names)]
    sib = _pair_swap(red)

    outs = {}
    for nm, r, s, w, m, v, tr in zip(names, red, sib, (w_in, w_mem_kv, w_out), (m_w_in, m_w_mem_kv, m_w_out),
                                     (v_w_in, v_w_mem_kv, v_w_out), trs):
        outs[nm] = tuple(a[None] for a in _adamw_halves(w[0], r, s, cidx, m[0], v[0], tr=tr, name=f"adamw_{nm}"))

    gsum = _small_allreduce(small)
    sd, sm, sv = _adamw(_pack_small(norm_g, mem_norm_g, final_norm_g, b_forget), gsum,
                        _pack_small(m_norm_g, m_mem_norm_g, m_final_norm_g, m_b_forget),
                        _pack_small(v_norm_g, v_mem_norm_g, v_final_norm_g, v_b_forget), tr=8, name="adamw_small")
    loss = gsum[LOSS_ROW, 0]

    def group(i, small_arr):
        ng, bf, mg, fg = _unpack_small(small_arr)
        return (ng, outs["w_in"][i], bf, mg, outs["w_mem_kv"][i], outs["w_out"][i], fg)

    return (loss, gx, *group(0, gsum), *group(1, sd), *group(2, sm), *group(3, sv))
```

```python
import functools
import math

import jax
import jax.numpy as jnp
from jax import lax
from jax.experimental import pallas as pl
from jax.experimental.pallas import tpu as pltpu

F32 = jnp.float32
BF16 = jnp.bfloat16

D_MODEL = 1024
SEQ = 2048
HEAD_DIM = 64
FOX_HEADS = 12
DIL_HEADS = 12
MEM_HEADS = 4
MEM_HEAD_DIM = 128
MEM_LEN = 256
FOX_W = FOX_HEADS * HEAD_DIM
DIL_W = DIL_HEADS * HEAD_DIM
MEM_W = MEM_HEADS * MEM_HEAD_DIM
MIX_W = FOX_W + DIL_W + MEM_W
DILATIONS = ((128, 1), (512, 4), (2048, 16))
ROPE_THETA = 500000.0
ROPE_DIM = HEAD_DIM // 4
RMS_EPS = 1e-6
NEG_INF = -1e30
IN_SIZES = [FOX_W] * 4 + [FOX_HEADS] + [DIL_W] * 4 + [MEM_W] * 2
IN_W = sum(IN_SIZES)

ADAM_LR = 0.001
ADAM_B1 = 0.9
ADAM_B2 = 0.999
ADAM_EPS = 1e-08
ADAM_WD = 0.01
ADAM_STEP = 10

LANES = 128
N_CHIPS = 4
PW = 7168
PWF = PW + 4 * LANES
C_FQ, C_FK, C_FV, C_FG = 0, 768, 1536, 2304
C_DQ, C_DK, C_DV, C_DG = 3072, 3840, 4608, 5376
C_MQ, C_MG = 6144, 6656
VMEM_LIMIT = 48 * 1024 * 1024


def _cparams(**kw):
    return pltpu.CompilerParams(vmem_limit_bytes=VMEM_LIMIT, **kw)


def _matmul(a, b, *, out_dtype, tm, tn, tk, name, mode="nn"):
    if mode == "tn":
        (kdim, m), n = a.shape, b.shape[1]
        a_spec = pl.BlockSpec((tk, tm), lambda i, j, k: (k, i))
        b_spec = pl.BlockSpec((tk, tn), lambda i, j, k: (k, j))
        dims = _T0
    elif mode == "nt":
        (m, kdim), n = a.shape, b.shape[0]
        a_spec = pl.BlockSpec((tm, tk), lambda i, j, k: (i, k))
        b_spec = pl.BlockSpec((tn, tk), lambda i, j, k: (j, k))
        dims = _NT
    else:
        (m, kdim), n = a.shape, b.shape[1]
        a_spec = pl.BlockSpec((tm, tk), lambda i, j, k: (i, k))
        b_spec = pl.BlockSpec((tk, tn), lambda i, j, k: (k, j))
        dims = (((1,), (0,)), ((), ()))
    nk = kdim // tk
    assert m % tm == 0 and n % tn == 0 and kdim % tk == 0

    def body(a_ref, b_ref, o_ref, *scratch):
        prod = lax.dot_general(a_ref[...], b_ref[...], dims, preferred_element_type=F32)
        if nk == 1:
            o_ref[...] = prod.astype(o_ref.dtype)
            return
        acc_ref, = scratch
        k = pl.program_id(2)

        @pl.when(k == 0)
        def _():
            acc_ref[...] = prod

        @pl.when(k > 0)
        def _():
            acc_ref[...] += prod

        @pl.when(k == nk - 1)
        def _():
            o_ref[...] = acc_ref[...].astype(o_ref.dtype)

    return pl.pallas_call(
        body,
        out_shape=jax.ShapeDtypeStruct((m, n), out_dtype),
        grid=(m // tm, n // tn, nk),
        in_specs=[a_spec, b_spec],
        out_specs=pl.BlockSpec((tm, tn), lambda i, j, k: (i, j)),
        scratch_shapes=[pltpu.VMEM((tm, tn), F32)] if nk > 1 else [],
        compiler_params=_cparams(dimension_semantics=("parallel", "parallel", "arbitrary")),
        name=name,
    )(a, b)


def _rms_fwd(x, g, *, tm, name):
    t, d = x.shape

    def body(x_ref, g_ref, h_ref):
        xv = x_ref[...]
        r = lax.rsqrt(jnp.mean(xv * xv, axis=-1, keepdims=True) + RMS_EPS)
        h_ref[...] = (xv * r * g_ref[...]).astype(h_ref.dtype)

    return pl.pallas_call(
        body,
        out_shape=jax.ShapeDtypeStruct((t, d), BF16),
        grid=(t // tm,),
        in_specs=[pl.BlockSpec((tm, d), lambda i: (i, 0)), pl.BlockSpec((1, d), lambda i: (0, 0))],
        out_specs=pl.BlockSpec((tm, d), lambda i: (i, 0)),
        compiler_params=_cparams(),
        name=name,
    )(x, g)


def _rope_tables():
    half = ROPE_DIM // 2
    pos = jnp.arange(SEQ, dtype=F32)
    inv_freq = 1.0 / (ROPE_THETA ** (jnp.arange(0, ROPE_DIM, 2, dtype=F32) / ROPE_DIM))
    ang = pos[:, None] * inv_freq[None, :]
    cos, sin = jnp.cos(ang), jnp.sin(ang)
    one = jnp.ones((SEQ, HEAD_DIM - ROPE_DIM), F32)
    zero = jnp.zeros((SEQ, HEAD_DIM - ROPE_DIM), F32)
    zh = jnp.zeros((SEQ, half), F32)
    c = jnp.concatenate([cos, cos, one], axis=1)
    s1 = jnp.concatenate([zh, sin, zero], axis=1)
    s2 = jnp.concatenate([-sin, zh, zero], axis=1)
    rep = LANES // HEAD_DIM
    return jnp.tile(c, (1, rep)), jnp.tile(s1, (1, rep)), jnp.tile(s2, (1, rep))


def _rope_apply(t, c, s1, s2, transpose=False):
    n = t.shape[-1]
    rep = n // LANES
    c, s1, s2 = (jnp.tile(u, (1, rep)) for u in (c, s1, s2))
    half = ROPE_DIM // 2
    if not transpose:
        return t * c + pltpu.roll(t, half, 1) * s1 + pltpu.roll(t, n - half, 1) * s2
    return t * c + pltpu.roll(t * s1, n - half, 1) + pltpu.roll(t * s2, half, 1)


def _proj(h, w, tabs, *, n, tm, tn, name):
    t, d = h.shape
    assert C_DQ % tn == 0 and (C_DV - C_DQ) % tn == 0 and (C_DG - C_DQ) % tn == 0
    rope_lo, rope_hi, dil_hi = C_DQ // tn, C_DV // tn, C_DG // tn
    s_blocks = SEQ // tm

    def body(h_ref, w_ref, c_ref, s1_ref, s2_ref, o_ref, f_ref):
        j = pl.program_id(1)
        acc = jnp.dot(h_ref[...], w_ref[...], preferred_element_type=F32)
        is_rope = jnp.logical_and(j >= rope_lo, j < rope_hi)

        @pl.when(is_rope)
        def _():
            r = _rope_apply(acc, c_ref[...], s1_ref[...], s2_ref[...])
            o_ref[...] = r.astype(o_ref.dtype)
            f_ref[...] = r

        @pl.when(jnp.logical_not(is_rope))
        def _():
            o_ref[...] = acc.astype(o_ref.dtype)

        @pl.when(jnp.logical_and(j >= rope_hi, j < dil_hi))
        def _():
            f_ref[...] = acc

    tab_spec = pl.BlockSpec((tm, LANES), lambda i, j: (i % s_blocks, 0))
    f_spec = pl.BlockSpec((tm, tn), lambda i, j: (i, jnp.clip(j - rope_lo, 0, dil_hi - rope_lo - 1)))
    return pl.pallas_call(
        body,
        out_shape=(jax.ShapeDtypeStruct((t, n), BF16), jax.ShapeDtypeStruct((t, 3 * DIL_W), F32)),
        grid=(t // tm, n // tn),
        in_specs=[pl.BlockSpec((tm, d), lambda i, j: (i, 0)), pl.BlockSpec((d, tn), lambda i, j: (0, j)),
                  tab_spec, tab_spec, tab_spec],
        out_specs=(pl.BlockSpec((tm, tn), lambda i, j: (i, j)), f_spec),
        compiler_params=_cparams(dimension_semantics=("parallel", "arbitrary")),
        name=name,
    )(h, w, *tabs)


def _split3(x):
    hi = x.astype(BF16)
    r1 = x - hi.astype(F32)
    mid = r1.astype(BF16)
    lo = (r1 - mid.astype(F32)).astype(BF16)
    return hi, mid, lo


def _dot3(sel, x, sel_is_lhs):
    out = None
    for piece in _split3(x):
        t = jnp.dot(sel, piece, preferred_element_type=F32) if sel_is_lhs else jnp.dot(piece, sel, preferred_element_type=F32)
        out = t if out is None else out + t
    return out


def _flog_fwd(flog, bpad, *, nb, ts, name):
    ns = SEQ // ts

    def body(f_ref, b_ref, c_ref, carry_ref):
        s = pl.program_id(1)

        @pl.when(s == 0)
        def _():
            carry_ref[...] = jnp.zeros_like(carry_ref)

        z = f_ref[...] + b_ref[...]
        logf = jnp.minimum(z, 0.0) - jnp.log(1.0 + jnp.exp(-jnp.abs(z)))
        r = lax.broadcasted_iota(jnp.int32, (ts, ts), 0)
        c = lax.broadcasted_iota(jnp.int32, (ts, ts), 1)
        tri = jnp.where(r >= c, 1.0, 0.0).astype(BF16)
        cs = _dot3(tri, logf, True) + carry_ref[0:1, :]
        carry_ref[...] = jnp.broadcast_to(cs[ts - 1:ts, :], carry_ref.shape)
        c_ref[...] = cs

    return pl.pallas_call(
        body,
        out_shape=jax.ShapeDtypeStruct((nb * SEQ, LANES), F32),
        grid=(nb, ns),
        in_specs=[pl.BlockSpec((ts, LANES), lambda b, s: (b * ns + s, 0)), pl.BlockSpec((1, LANES), lambda b, s: (0, 0))],
        out_specs=pl.BlockSpec((ts, LANES), lambda b, s: (b * ns + s, 0)),
        scratch_shapes=[pltpu.VMEM((8, LANES), F32)],
        compiler_params=_cparams(dimension_semantics=("parallel", "arbitrary")),
        name=name,
    )(flog, bpad)


def _flog_bwd(dcol, flog, bpad, *, nb, ts, name):
    ns = SEQ // ts

    def body(d_ref, f_ref, b_ref, o_ref, gb_ref, carry_ref):
        bi = pl.program_id(0)
        s = pl.program_id(1)

        @pl.when(s == 0)
        def _():
            carry_ref[...] = jnp.zeros_like(carry_ref)

        @pl.when(jnp.logical_and(bi == 0, s == 0))
        def _():
            gb_ref[...] = jnp.zeros_like(gb_ref)

        r = lax.broadcasted_iota(jnp.int32, (ts, ts), 0)
        c = lax.broadcasted_iota(jnp.int32, (ts, ts), 1)
        tri = jnp.where(r <= c, 1.0, 0.0).astype(BF16)
        rc = _dot3(tri, d_ref[...], True) + carry_ref[0:1, :]
        carry_ref[...] = jnp.broadcast_to(rc[0:1, :], carry_ref.shape)
        z = f_ref[...] + b_ref[...]
        dz = rc / (1.0 + jnp.exp(z))
        o_ref[...] = dz.astype(o_ref.dtype)
        gb_ref[...] += jnp.broadcast_to(jnp.sum(dz, axis=0, keepdims=True), gb_ref.shape)

    rev = lambda b, s: (b * ns + (ns - 1 - s), 0)
    return pl.pallas_call(
        body,
        out_shape=(jax.ShapeDtypeStruct((nb * SEQ, LANES), BF16), jax.ShapeDtypeStruct((8, LANES), F32)),
        grid=(nb, ns),
        in_specs=[pl.BlockSpec((ts, LANES), rev), pl.BlockSpec((ts, LANES), rev), pl.BlockSpec((1, LANES), lambda b, s: (0, 0))],
        out_specs=(pl.BlockSpec((ts, LANES), rev), pl.BlockSpec((8, LANES), lambda b, s: (0, 0))),
        scratch_shapes=[pltpu.VMEM((8, LANES), F32)],
        compiler_params=_cparams(dimension_semantics=("arbitrary", "arbitrary")),
        name=name,
    )(dcol, flog, bpad)


class _AttnCfg:
    def __init__(self, *, e, tq, tk, lq, lk, causal, window, ncol, qcol, kcol, vcol, split_p=False):
        self.e, self.tq, self.tk, self.lq, self.lk = e, tq, tk, lq, lk
        self.split_p = split_p
        self.causal, self.window = causal, window
        self.ncol, self.qcol, self.kcol, self.vcol = ncol, qcol, kcol, vcol
        self.nh = LANES // e
        self.scale = 1.0 / math.sqrt(e)
        self.nq, self.nk = lq // tq, lk // tk

    def k_range(self, i):
        if not self.causal:
            return 0, self.nk
        hi = ((i + 1) * self.tq - 1) // self.tk + 1
        if self.window is None:
            return 0, hi
        return jnp.maximum((i * self.tq - self.window) // self.tk, 0), hi


def _head_masks(nh):
    lane = lax.broadcasted_iota(jnp.int32, (1, LANES), 1)
    return [None] if nh == 1 else [lane < HEAD_DIM, lane >= HEAD_DIM]


def _sel(mask, a, b):
    return a if mask is None else jnp.where(mask, a, b)


def _scores(cfg, qh, kb, q0, k0, dlt0, bias):
    s = lax.dot_general(qh, kb, (((1,), (1,)), ((), ())), preferred_element_type=F32) * cfg.scale
    if bias is not None:
        s = s + bias
    if cfg.causal:
        d = dlt0 + (q0 - k0)
        if cfg.window is None:
            ok = d >= 0
        else:
            ok = d.astype(jnp.uint32) <= jnp.uint32(cfg.window)
        s = jnp.where(ok, s, NEG_INF)
    return s


def _attn_fwd(cfg, q, k, v, *, out_cols, bias=None, state=None, finalize=True, name):
    g = q.shape[0]
    tq, tk, e, nh = cfg.tq, cfg.tk, cfg.e, cfg.nh

    def body(*refs):
        refs = list(refs)
        q_ref, k_ref, v_ref = refs[:3]
        del refs[:3]
        if bias is not None:
            cb_ref, cr_ref = refs[:2]
            del refs[:2]
        if state is not None:
            ai_ref, mi_ref, li_ref = refs[:3]
            del refs[:3]
        out_refs = refs
        masks = _head_masks(nh)
        dlt0 = lax.broadcasted_iota(jnp.int32, (tq, tk), 0) - lax.broadcasted_iota(jnp.int32, (tq, tk), 1)

        def qbody(i, carry):
            q0 = pl.multiple_of(i * tq, tq)
            rows = pl.ds(q0, tq)
            qb = q_ref[rows, :]
            lo, hi = cfg.k_range(i)
            res = []
            for h in range(nh):
                qh = _sel(masks[h], qb, jnp.zeros_like(qb))
                if state is not None:
                    m0 = mi_ref[rows, h * e:h * e + 1]
                    l0 = li_ref[rows, h * e:h * e + 1]
                    a0 = ai_ref[rows, :]
                else:
                    m0 = jnp.full((tq, 1), NEG_INF, F32)
                    l0 = jnp.zeros((tq, 1), F32)
                    a0 = jnp.zeros((tq, LANES), F32)
                cq = cb_ref[rows, h * e:h * e + 1] if bias is not None else None

                def kbody(jk, c, qh=qh, cq=cq, h=h):
                    m, l, a = c
                    k0 = pl.multiple_of(jk * tk, tk)
                    kb = k_ref[pl.ds(k0, tk), :]
                    vb = v_ref[pl.ds(k0, tk), :]
                    b = (cq - cr_ref[jk, h:h + 1, :]) if bias is not None else None
                    s = _scores(cfg, qh, kb, q0, k0, dlt0, b)
                    m_new = jnp.maximum(m, jnp.max(s, axis=1, keepdims=True))
                    alpha = jnp.exp(m - m_new)
                    p = jnp.exp(s - m_new)
                    l = alpha * l + jnp.sum(p, axis=1, keepdims=True)
                    pb = p.astype(BF16)
                    pv = jnp.dot(pb, vb, preferred_element_type=F32)
                    if cfg.split_p:
                        pv = pv + jnp.dot((p - pb.astype(F32)).astype(BF16), vb, preferred_element_type=F32)
                    a = alpha * a + pv
                    return m_new, l, a

                res.append(lax.fori_loop(lo, hi, kbody, (m0, l0, a0)))
            if nh == 1:
                m, l, a = res[0]
                m, l = jnp.broadcast_to(m, (tq, LANES)), jnp.broadcast_to(l, (tq, LANES))
            else:
                m = jnp.where(masks[0], res[0][0], res[1][0])
                l = jnp.where(masks[0], res[0][1], res[1][1])
                a = jnp.where(masks[0], res[0][2], res[1][2])
            if finalize:
                out_refs[0][rows, :] = a / l
                out_refs[1][rows, :] = m + jnp.log(l)
            else:
                out_refs[0][rows, :] = a
                out_refs[1][rows, :] = m
                out_refs[2][rows, :] = l
            return carry

        lax.fori_loop(0, cfg.nq, qbody, 0)

    qspec = pl.BlockSpec((None, cfg.lq, LANES), lambda b, j: (b, 0, cfg.qcol(j)))
    kspec = pl.BlockSpec((None, cfg.lk, LANES), lambda b, j: (b, 0, cfg.kcol(j)))
    vspec = pl.BlockSpec((None, cfg.lk, LANES), lambda b, j: (b, 0, cfg.vcol(j)))
    ospec = pl.BlockSpec((None, cfg.lq, LANES), lambda b, j: (b, 0, j))
    args, in_specs = [q, k, v], [qspec, kspec, vspec]
    if bias is not None:
        args += list(bias)
        in_specs += [ospec, pl.BlockSpec((None, None, cfg.nk, 8, tk), lambda b, j: (b, j, 0, 0, 0))]
    aliases = {}
    if state is not None:
        aliases = {len(args) + t: t for t in range(3 if not finalize else 2)}
        args += list(state)
        in_specs += [ospec] * 3
    n_out = 2 if finalize else 3
    osd = jax.ShapeDtypeStruct((g, cfg.lq, out_cols), F32)
    return pl.pallas_call(
        body,
        out_shape=(osd,) * n_out,
        grid=(g, cfg.ncol),
        in_specs=in_specs,
        out_specs=(ospec,) * n_out,
        input_output_aliases=aliases,
        compiler_params=_cparams(dimension_semantics=("parallel", "parallel")),
        name=name,
    )(*args)


def _attn_bwd(cfg, q, k, v, do, o, lse, *, out_cols, kv_cols, bias=None, acc=None, do_off=0, name):
    g = q.shape[0]
    tq, tk, e, nh = cfg.tq, cfg.tk, cfg.e, cfg.nh
    t0 = (((0,), (0,)), ((), ()))

    def body(*refs):
        refs = list(refs)
        q_ref, k_ref, v_ref, do_ref, o_ref, lse_ref = refs[:6]
        del refs[:6]
        if bias is not None:
            cb_ref, cr_ref = refs[:2]
            del refs[:2]
        if acc is not None:
            dqi_ref, dki_ref, dvi_ref = refs[:3]
            del refs[:3]
        dq_ref, dk_ref, dv_ref = refs[:3]
        dcr_ref = refs[3] if bias is not None else None
        masks = _head_masks(nh)
        dlt0 = lax.broadcasted_iota(jnp.int32, (tq, tk), 0) - lax.broadcasted_iota(jnp.int32, (tq, tk), 1)
        if acc is not None:
            dq_ref[...] = dqi_ref[...]
            dk_ref[...] = dki_ref[...]
            dv_ref[...] = dvi_ref[...]
        else:
            dq_ref[...] = jnp.zeros_like(dq_ref)
            dk_ref[...] = jnp.zeros_like(dk_ref)
            dv_ref[...] = jnp.zeros_like(dv_ref)
        if dcr_ref is not None:
            dcr_ref[...] = jnp.zeros_like(dcr_ref)

        def qbody(i, carry):
            q0 = pl.multiple_of(i * tq, tq)
            rows = pl.ds(q0, tq)
            qb = q_ref[rows, :]
            dob = do_ref[rows, :].astype(BF16)
            prod = dob.astype(F32) * o_ref[rows, :]
            lo, hi = cfg.k_range(i)
            dqs = []
            for h in range(nh):
                qh = _sel(masks[h], qb, jnp.zeros_like(qb))
                doh = _sel(masks[h], dob, jnp.zeros_like(dob))
                lse_h = lse_ref[rows, h * e:h * e + 1]
                delta = jnp.sum(_sel(masks[h], prod, jnp.zeros_like(prod)), axis=1, keepdims=True)
                cq = cb_ref[rows, h * e:h * e + 1] if bias is not None else None

                def kbody(jk, dq_acc, qh=qh, doh=doh, lse_h=lse_h, delta=delta, cq=cq, h=h):
                    k0 = pl.multiple_of(jk * tk, tk)
                    krows = pl.ds(k0, tk)
                    kb = k_ref[krows, :]
                    vb = v_ref[krows, :]
                    b = (cq - cr_ref[jk, h:h + 1, :]) if bias is not None else None
                    s = _scores(cfg, qh, kb, q0, k0, dlt0, b)
                    p = jnp.exp(s - lse_h)
                    dp = lax.dot_general(doh, vb, (((1,), (1,)), ((), ())), preferred_element_type=F32)
                    ds = p * (dp - delta)
                    if dcr_ref is not None:
                        dcr_ref[jk, h:h + 1, :] += jnp.sum(ds, axis=0, keepdims=True)
                    dsb = (ds * cfg.scale).astype(BF16)
                    dv_ref[krows, :] += lax.dot_general(p.astype(BF16), doh, t0, preferred_element_type=F32)
                    dk_ref[krows, :] += lax.dot_general(dsb, qh, t0, preferred_element_type=F32)
                    return dq_acc + jnp.dot(dsb, kb, preferred_element_type=F32)

                dqs.append(lax.fori_loop(lo, hi, kbody, jnp.zeros((tq, LANES), F32)))
            dq = dqs[0] if nh == 1 else jnp.where(masks[0], dqs[0], dqs[1])
            dq_ref[rows, :] += dq
            return carry

        lax.fori_loop(0, cfg.nq, qbody, 0)

    qspec = pl.BlockSpec((None, cfg.lq, LANES), lambda b, j: (b, 0, cfg.qcol(j)))
    kspec = pl.BlockSpec((None, cfg.lk, LANES), lambda b, j: (b, 0, cfg.kcol(j)))
    vspec = pl.BlockSpec((None, cfg.lk, LANES), lambda b, j: (b, 0, cfg.vcol(j)))
    ospec = pl.BlockSpec((None, cfg.lq, LANES), lambda b, j: (b, 0, j))
    kvspec = pl.BlockSpec((None, cfg.lk, LANES), lambda b, j: (b, 0, j))
    dospec = pl.BlockSpec((None, cfg.lq, LANES), lambda b, j: (b, 0, do_off + j))
    args, in_specs = [q, k, v, do, o, lse], [qspec, kspec, vspec, dospec, ospec, ospec]
    out_shape = [jax.ShapeDtypeStruct((g, cfg.lq, out_cols), F32), jax.ShapeDtypeStruct((g, cfg.lk, kv_cols), F32),
                 jax.ShapeDtypeStruct((g, cfg.lk, kv_cols), F32)]
    out_specs = [ospec, kvspec, kvspec]
    if bias is not None:
        args += list(bias)
        crspec = pl.BlockSpec((None, None, cfg.nk, 8, tk), lambda b, j: (b, j, 0, 0, 0))
        in_specs += [ospec, crspec]
        out_shape.append(jax.ShapeDtypeStruct((g, cfg.ncol, cfg.nk, 8, tk), F32))
        out_specs.append(crspec)
    aliases = {}
    if acc is not None:
        aliases = {len(args) + t: t for t in range(3)}
        args += list(acc)
        in_specs += [ospec, kvspec, kvspec]
    return pl.pallas_call(
        body,
        out_shape=tuple(out_shape),
        grid=(g, cfg.ncol),
        in_specs=in_specs,
        out_specs=tuple(out_specs),
        input_output_aliases=aliases,
        compiler_params=_cparams(dimension_semantics=("parallel", "parallel")),
        name=name,
    )(*args)


BLK = 128
NBLK = SEQ // BLK
QK_SCALE = 1.0 / math.sqrt(HEAD_DIM)
DIL_STEPS = tuple(d for _, d in DILATIONS)
assert all(w // d == BLK for w, d in DILATIONS)
_T0 = (((0,), (0,)), ((), ()))
_NT = (((1,), (1,)), ((), ()))


def _stack_heads(a, masks):
    z = jnp.zeros_like(a)
    return jnp.concatenate([jnp.where(masks[0], a, z), jnp.where(masks[1], a, z)], axis=0)


def _tri_bias(lower):
    r = lax.broadcasted_iota(jnp.int32, (BLK, BLK), 0)
    c = lax.broadcasted_iota(jnp.int32, (BLK, BLK), 1)
    return jnp.where((c <= r) if lower else (c >= r), 0.0, NEG_INF).astype(F32)


def _dil_rows(r, i, d):
    start = r + i * (BLK * d)
    return pl.ds(start, BLK) if d == 1 else pl.ds(start, BLK, stride=d)


DIL_SET = 4


def _dil_sets(d, fn):
    nbk = SEQ // d // BLK
    if d == 1:
        def gbody(g, c):
            fn([(0, DIL_SET * g + a, None if a == 0 else True) for a in range(DIL_SET)])
            return c
        lax.fori_loop(0, nbk // DIL_SET, gbody, 0)
    elif nbk > 1:
        assert nbk == DIL_SET
        def rbody(r, c):
            fn([(r, i, i > 0) for i in range(nbk)])
            return c
        lax.fori_loop(0, d, rbody, 0)
    else:
        def rbody(rr, c):
            fn([(DIL_SET * rr + a, 0, False) for a in range(DIL_SET)])
            return c
        lax.fori_loop(0, d // DIL_SET, rbody, 0)


def _dil_key_tiles(r, i, d, has_prev, qrows, tri_cur, tri_prev):
    tiles = [(qrows, tri_cur)]
    if has_prev is None:
        tiles.append((_dil_rows(r, jnp.maximum(i - 1, 0), d), tri_prev + jnp.where(i > 0, 0.0, NEG_INF)))
    elif has_prev:
        tiles.append((_dil_rows(r, i - 1, d), tri_prev))
    return tiles


def _dil_fwd(qkv, *, name):
    nb = qkv.shape[0]
    ncol = DIL_W // LANES
    hd = HEAD_DIM

    def body(q_ref, k_ref, v_ref, o_ref, lse_ref, m_ref, l_ref, a_ref):
        masks = _head_masks(2)
        tri_cur, tri_prev = _tri_bias(True), _tri_bias(False)
        for pi, d in enumerate(DIL_STEPS):
            first, last = pi == 0, pi == len(DIL_STEPS) - 1

            def qset(blocks, d=d, first=first, last=last):
                work = []
                for r, i, has_prev in blocks:
                    qrows = _dil_rows(r, i, d)
                    qcat = _stack_heads((q_ref[qrows, :] * QK_SCALE).astype(BF16), masks)
                    ss, krs = [], []
                    for krows, bias in _dil_key_tiles(r, i, d, has_prev, qrows, tri_cur, tri_prev):
                        s = lax.dot_general(qcat, k_ref[krows, :].astype(BF16), _NT, preferred_element_type=F32)
                        ss.append((s[:BLK] + bias, s[BLK:] + bias))
                        krs.append(krows)
                    work.append((qrows, ss, krs))
                for qrows, ss, krs in work:
                    e0 = ss[0][0] if len(ss) == 1 else jnp.maximum(ss[0][0], ss[1][0])
                    e1 = ss[0][1] if len(ss) == 1 else jnp.maximum(ss[0][1], ss[1][1])
                    n0 = jnp.max(e0, axis=1, keepdims=True)
                    n1 = jnp.max(e1, axis=1, keepdims=True)
                    if not first:
                        mo, lo = m_ref[qrows, :], l_ref[qrows, :]
                        m0, m1 = mo[:, 0:1], mo[:, hd:hd + 1]
                        n0, n1 = jnp.maximum(n0, m0), jnp.maximum(n1, m1)
                        a0, a1 = jnp.exp(m0 - n0), jnp.exp(m1 - n1)
                    ps = [(jnp.exp(s0 - n0), jnp.exp(s1 - n1)) for s0, s1 in ss]
                    t0 = ps[0][0] if len(ps) == 1 else ps[0][0] + ps[1][0]
                    t1 = ps[0][1] if len(ps) == 1 else ps[0][1] + ps[1][1]
                    l0 = jnp.sum(t0, axis=1, keepdims=True)
                    l1 = jnp.sum(t1, axis=1, keepdims=True)
                    acc = None
                    for (p0, p1), krows in zip(ps, krs):
                        vcat = _stack_heads(v_ref[krows, :].astype(BF16), masks)
                        pv = jnp.dot(jnp.concatenate([p0, p1], axis=1).astype(BF16), vcat, preferred_element_type=F32)
                        acc = pv if acc is None else acc + pv
                    if not first:
                        l0 = l0 + a0 * lo[:, 0:1]
                        l1 = l1 + a1 * lo[:, hd:hd + 1]
                        acc = acc + a_ref[qrows, :] * jnp.where(masks[0], a0, a1)
                    if last:
                        o_ref[qrows, :] = acc / jnp.where(masks[0], l0, l1)
                        lse_ref[qrows, :] = jnp.where(masks[0], n0 + jnp.log(l0), n1 + jnp.log(l1))
                    else:
                        m_ref[qrows, :] = jnp.where(masks[0], n0, n1)
                        l_ref[qrows, :] = jnp.where(masks[0], l0, l1)
                        a_ref[qrows, :] = acc

            _dil_sets(d, qset)

    spec = lambda off: pl.BlockSpec((None, SEQ, LANES), lambda b, j: (b, 0, off + j))
    ospec = pl.BlockSpec((None, SEQ, LANES), lambda b, j: (b, 0, j))
    osd = jax.ShapeDtypeStruct((nb, SEQ, DIL_W), F32)
    return pl.pallas_call(
        body, out_shape=(osd, osd), grid=(nb, ncol),
        in_specs=[spec(0), spec(ncol), spec(2 * ncol)], out_specs=(ospec, ospec),
        scratch_shapes=[pltpu.VMEM((SEQ, LANES), F32)] * 3,
        compiler_params=_cparams(dimension_semantics=("parallel", "parallel")), name=name,
    )(qkv, qkv, qkv)


def _dil_bwd(qkv, do, o, lse, tabs, *, do_off, name):
    nb = qkv.shape[0]
    ncol = DIL_W // LANES
    hd = HEAD_DIM

    def body(q_ref, k_ref, v_ref, do_ref, o_ref, lse_ref, c_ref, s1_ref, s2_ref, dqo_ref, dko_ref, dvo_ref,
             dq_ref, dk_ref, dv_ref, dl_ref, dof_ref):
        masks = _head_masks(2)
        tri_cur, tri_prev = _tri_bias(True), _tri_bias(False)
        dq_ref[...] = jnp.zeros_like(dq_ref)
        dk_ref[...] = jnp.zeros_like(dk_ref)
        dv_ref[...] = jnp.zeros_like(dv_ref)

        def delta_body(i, c):
            rows = pl.ds(pl.multiple_of(i * BLK, BLK), BLK)
            dof = do_ref[rows, :].astype(F32)
            dof_ref[rows, :] = dof
            prod = dof * o_ref[rows, :]
            z = jnp.zeros_like(prod)
            dl_ref[rows, :] = jnp.where(masks[0], jnp.sum(jnp.where(masks[0], prod, z), axis=1, keepdims=True),
                                        jnp.sum(jnp.where(masks[1], prod, z), axis=1, keepdims=True))
            return c

        lax.fori_loop(0, NBLK, delta_body, 0)

        for d in DIL_STEPS:
            def qset(blocks, d=d):
                work = []
                for r, i, has_prev in blocks:
                    qrows = _dil_rows(r, i, d)
                    qcat = _stack_heads((q_ref[qrows, :] * QK_SCALE).astype(BF16), masks)
                    docat = _stack_heads(dof_ref[qrows, :].astype(BF16), masks)
                    tiles = []
                    for krows, bias in _dil_key_tiles(r, i, d, has_prev, qrows, tri_cur, tri_prev):
                        s = lax.dot_general(qcat, k_ref[krows, :].astype(BF16), _NT, preferred_element_type=F32)
                        dp = lax.dot_general(docat, v_ref[krows, :].astype(BF16), _NT, preferred_element_type=F32)
                        tiles.append((krows, s, dp, bias))
                    work.append((qrows, qcat, docat, tiles))
                for qrows, qcat, docat, tiles in work:
                    lseb, dlb = lse_ref[qrows, :], dl_ref[qrows, :]
                    lse0, lse1 = lseb[:, 0:1], lseb[:, hd:hd + 1]
                    dl0, dl1 = dlb[:, 0:1], dlb[:, hd:hd + 1]
                    dq = None
                    for krows, s, dp, bias in tiles:
                        p0 = jnp.exp(s[:BLK] + bias - lse0)
                        p1 = jnp.exp(s[BLK:] + bias - lse1)
                        ds0 = p0 * (dp[:BLK] - dl0)
                        ds1 = p1 * (dp[BLK:] - dl1)
                        ds0b, ds1b = ds0.astype(BF16), ds1.astype(BF16)
                        pcat = jnp.concatenate([p0.astype(BF16), p1.astype(BF16)], axis=0)
                        dscat = jnp.concatenate([ds0b, ds1b], axis=0)
                        dv_ref[krows, :] += lax.dot_general(pcat, docat, _T0, preferred_element_type=F32)
                        dk_ref[krows, :] += lax.dot_general(dscat, qcat, _T0, preferred_element_type=F32)
                        dsrow = jnp.concatenate([ds0b, ds1b], axis=1)
                        kcat = _stack_heads((k_ref[krows, :] * QK_SCALE).astype(BF16), masks)
                        t = jnp.dot(dsrow, kcat, preferred_element_type=F32)
                        dq = t if dq is None else dq + t
                    dq_ref[qrows, :] += dq

            _dil_sets(d, qset)

        def out_body(i, c):
            rows = pl.ds(pl.multiple_of(i * BLK, BLK), BLK)
            tab = (c_ref[rows, :], s1_ref[rows, :], s2_ref[rows, :])
            dqo_ref[rows, :] = _rope_apply(dq_ref[rows, :], *tab, transpose=True).astype(dqo_ref.dtype)
            dko_ref[rows, :] = _rope_apply(dk_ref[rows, :], *tab, transpose=True).astype(dko_ref.dtype)
            dvo_ref[rows, :] = dv_ref[rows, :].astype(dvo_ref.dtype)
            return c

        lax.fori_loop(0, NBLK, out_body, 0)

    spec = lambda off: pl.BlockSpec((None, SEQ, LANES), lambda b, j: (b, 0, off + j))
    ospec = pl.BlockSpec((None, SEQ, LANES), lambda b, j: (b, 0, j))
    tspec = pl.BlockSpec((SEQ, LANES), lambda b, j: (0, 0))
    osd = jax.ShapeDtypeStruct((nb, SEQ, DIL_W), BF16)
    return pl.pallas_call(
        body, out_shape=(osd, osd, osd), grid=(nb, ncol),
        in_specs=[spec(0), spec(ncol), spec(2 * ncol), spec(do_off), ospec, ospec, tspec, tspec, tspec],
        out_specs=(ospec, ospec, ospec),
        scratch_shapes=[pltpu.VMEM((SEQ, LANES), F32)] * 5,
        compiler_params=_cparams(dimension_semantics=("parallel", "parallel")), name=name,
    )(qkv, qkv, qkv, do, o, lse, *tabs)


FOX_GROUP = 4
assert NBLK % FOX_GROUP == 0
_FOX_COLS = tuple(c // LANES for c in (C_FQ, C_FK, C_FV))


def _fox_specs():
    cols = [pl.BlockSpec((None, SEQ, LANES), (lambda b, j, off=off: (b, 0, off + j))) for off in _FOX_COLS]
    ospec = pl.BlockSpec((None, SEQ, LANES), lambda b, j: (b, 0, j))
    crspec = pl.BlockSpec((None, None, NBLK, 8, BLK), lambda b, j: (b, j, 0, 0, 0))
    return cols, ospec, crspec


def _fox_key_rows(t, e):
    return pl.ds(pl.multiple_of((FOX_GROUP * t + e) * BLK, BLK), BLK)


def _fox_fwd(p3, crow, *, name):
    nb = p3.shape[0]
    g = FOX_GROUP

    def body(q_ref, k_ref, v_ref, cr_ref, o_ref, lse_ref):
        masks = _head_masks(2)
        tri = _tri_bias(True)

        def qk(qcat, t):
            return tuple(lax.dot_general(qcat, k_ref[_fox_key_rows(t, e), :], _NT, preferred_element_type=F32) for e in range(g))

        def consume(ss, t, state, nblk, diag):
            m0, m1, l0, l1, acc = state
            us = []
            for e in range(nblk):
                cr = cr_ref[g * t + e]
                u0 = ss[e][:BLK] - cr[0:1, :]
                u1 = ss[e][BLK:] - cr[1:2, :]
                if diag and e == nblk - 1:
                    u0, u1 = u0 + tri, u1 + tri
                us.append((u0, u1))
            x0 = functools.reduce(jnp.maximum, [u[0] for u in us])
            x1 = functools.reduce(jnp.maximum, [u[1] for u in us])
            n0 = jnp.maximum(m0, jnp.max(x0, axis=1, keepdims=True))
            n1 = jnp.maximum(m1, jnp.max(x1, axis=1, keepdims=True))
            a0, a1 = jnp.exp(m0 - n0), jnp.exp(m1 - n1)
            acc = acc * jnp.where(masks[0], a0, a1)
            t0 = t1 = None
            for e in range(nblk):
                p0, p1 = jnp.exp(us[e][0] - n0), jnp.exp(us[e][1] - n1)
                t0 = p0 if t0 is None else t0 + p0
                t1 = p1 if t1 is None else t1 + p1
                pcat = jnp.concatenate([p0, p1], axis=1)
                hi = pcat.astype(BF16)
                lo = (pcat - hi.astype(F32)).astype(BF16)
                vcat = _stack_heads(v_ref[_fox_key_rows(t, e), :], masks)
                acc = acc + jnp.dot(hi, vcat, preferred_element_type=F32) + jnp.dot(lo, vcat, preferred_element_type=F32)
            l0 = a0 * l0 + jnp.sum(t0, axis=1, keepdims=True)
            l1 = a1 * l1 + jnp.sum(t1, axis=1, keepdims=True)
            return n0, n1, l0, l1, acc

        def gbody(ng, c):
            neg = jnp.full((BLK, 1), NEG_INF, F32)
            z1 = jnp.zeros((BLK, 1), F32)
            rows = [pl.ds(pl.multiple_of((g * ng + a) * BLK, BLK), BLK) for a in range(g)]
            qcats = [_stack_heads(q_ref[rows[a], :] * QK_SCALE, masks) for a in range(g)]
            first = [qk(qcats[a], 0) for a in range(g)]
            done = []
            for a in range(g):
                def step(t, cc, qcat=qcats[a]):
                    ss, st = cc
                    nxt = qk(qcat, t + 1)
                    return nxt, consume(ss, t, st, g, False)

                done.append(lax.fori_loop(0, ng, step, (first[a], (neg, neg, z1, z1, jnp.zeros((BLK, LANES), F32)))))
            for a in range(g):
                ss, state = done[a]
                m0, m1, l0, l1, acc = consume(ss, ng, state, a + 1, True)
                o_ref[rows[a], :] = acc / jnp.where(masks[0], l0, l1)
                lse_ref[rows[a], :] = jnp.where(masks[0], m0 + jnp.log(l0), m1 + jnp.log(l1))
            return c

        lax.fori_loop(0, NBLK // g, gbody, 0)

    cols, ospec, crspec = _fox_specs()
    osd = jax.ShapeDtypeStruct((nb, SEQ, FOX_W), F32)
    return pl.pallas_call(
        body, out_shape=(osd, osd), grid=(nb, FOX_W // LANES), in_specs=cols + [crspec], out_specs=(ospec, ospec),
        compiler_params=_cparams(dimension_semantics=("parallel", "parallel")), name=name,
    )(p3, p3, p3, crow)


def _fox_bwd(p3, crow, do, o, lse, *, do_off, name):
    nb = p3.shape[0]
    g = FOX_GROUP
    hd = HEAD_DIM

    def body(q_ref, k_ref, v_ref, cr_ref, do_ref, o_ref, lse_ref, dq_ref, dko_ref, dvo_ref, dcr_ref, dk_ref, dv_ref):
        masks = _head_masks(2)
        tri = _tri_bias(True)
        dk_ref[...] = jnp.zeros_like(dk_ref)
        dv_ref[...] = jnp.zeros_like(dv_ref)
        dcr_ref[...] = jnp.zeros_like(dcr_ref)

        def products(qcat, docat, t):
            out = []
            for e in range(g):
                krows = _fox_key_rows(t, e)
                out.append(lax.dot_general(qcat, k_ref[krows, :], _NT, preferred_element_type=F32))
                out.append(lax.dot_general(docat, v_ref[krows, :], _NT, preferred_element_type=F32))
            return tuple(out)

        def consume(prod, t, ctx, dq, nblk, diag):
            qcat, docat, lse0, lse1, dl0, dl1 = ctx
            for e in range(nblk):
                jb = g * t + e
                krows = _fox_key_rows(t, e)
                s, dp = prod[2 * e], prod[2 * e + 1]
                cr = cr_ref[jb]
                u0 = s[:BLK] - cr[0:1, :]
                u1 = s[BLK:] - cr[1:2, :]
                if diag and e == nblk - 1:
                    u0, u1 = u0 + tri, u1 + tri
                p0 = jnp.exp(u0 - lse0)
                p1 = jnp.exp(u1 - lse1)
                ds0 = p0 * (dp[:BLK] - dl0)
                ds1 = p1 * (dp[BLK:] - dl1)
                dcr_ref[jb, 0:1, :] += jnp.sum(ds0, axis=0, keepdims=True)
                dcr_ref[jb, 1:2, :] += jnp.sum(ds1, axis=0, keepdims=True)
                ds0b, ds1b = ds0.astype(BF16), ds1.astype(BF16)
                pcat = jnp.concatenate([p0.astype(BF16), p1.astype(BF16)], axis=0)
                dscat = jnp.concatenate([ds0b, ds1b], axis=0)
                dv_ref[krows, :] += lax.dot_general(pcat, docat, _T0, preferred_element_type=F32)
                dk_ref[krows, :] += lax.dot_general(dscat, qcat, _T0, preferred_element_type=F32)
                dsrow = jnp.concatenate([ds0b, ds1b], axis=1)
                dq = dq + jnp.dot(dsrow, _stack_heads(k_ref[krows, :] * QK_SCALE, masks), preferred_element_type=F32)
            return dq

        def gbody(ng, c):
            ctxs, rows = [], []
            for a in range(g):
                r = pl.ds(pl.multiple_of((g * ng + a) * BLK, BLK), BLK)
                qcat = _stack_heads(q_ref[r, :] * QK_SCALE, masks)
                dob = do_ref[r, :].astype(BF16)
                prod = dob.astype(F32) * o_ref[r, :]
                z = jnp.zeros_like(prod)
                dl0 = jnp.sum(jnp.where(masks[0], prod, z), axis=1, keepdims=True)
                dl1 = jnp.sum(jnp.where(masks[1], prod, z), axis=1, keepdims=True)
                lseb = lse_ref[r, :]
                ctxs.append((qcat, _stack_heads(dob, masks), lseb[:, 0:1], lseb[:, hd:hd + 1], dl0, dl1))
                rows.append(r)
            first = [products(ctxs[a][0], ctxs[a][1], 0) for a in range(g)]
            done = []
            for a in range(g):
                def step(t, cc, ctx=ctxs[a]):
                    pr, dq = cc
                    nxt = products(ctx[0], ctx[1], t + 1)
                    return nxt, consume(pr, t, ctx, dq, g, False)

                done.append(lax.fori_loop(0, ng, step, (first[a], jnp.zeros((BLK, LANES), F32))))
            for a in range(g):
                pr, dq = done[a]
                dq_ref[rows[a], :] = consume(pr, ng, ctxs[a], dq, a + 1, True).astype(dq_ref.dtype)
            return c

        lax.fori_loop(0, NBLK // g, gbody, 0)
        dko_ref[...] = dk_ref[...].astype(dko_ref.dtype)
        dvo_ref[...] = dv_ref[...].astype(dvo_ref.dtype)

    cols, ospec, crspec = _fox_specs()
    dospec = pl.BlockSpec((None, SEQ, LANES), lambda b, j: (b, 0, do_off + j))
    osd = jax.ShapeDtypeStruct((nb, SEQ, FOX_W), BF16)
    return pl.pallas_call(
        body, out_shape=(osd, osd, osd, jax.ShapeDtypeStruct((nb, FOX_W // LANES, NBLK, 8, BLK), F32)),
        grid=(nb, FOX_W // LANES), in_specs=cols + [crspec, dospec, ospec, ospec], out_specs=(ospec, ospec, ospec, crspec),
        scratch_shapes=[pltpu.VMEM((SEQ, LANES), F32)] * 2,
        compiler_params=_cparams(dimension_semantics=("parallel", "parallel")), name=name,
    )(p3, p3, p3, crow, do, o, lse)


def _mem_cfg():
    return _AttnCfg(e=MEM_HEAD_DIM, tq=256, tk=MEM_LEN, lq=SEQ, lk=MEM_LEN, causal=False, window=None, ncol=MEM_HEADS,
                    qcol=lambda j: C_MQ // LANES + j, kcol=lambda j: j, vcol=lambda j: MEM_HEADS + j)


_B1, _B2 = FOX_W // LANES, (FOX_W + DIL_W) // LANES


def _dy_gate_bwd(dx2b, wo, fox, dil, memo, p16, *, tm, tn, name):
    t, d = dx2b.shape
    assert FOX_W % tn == 0 and DIL_W % tn == 0 and MEM_W % tn == 0 and all(c % tn == 0 for c in (C_FG, C_DG, C_MG))
    n1, n2, n3 = FOX_W // tn, (FOX_W + DIL_W) // tn, MIX_W // tn

    def body(dx_ref, w_ref, f_ref, d_ref, m_ref, g_ref, da_ref, dg_ref):
        j = pl.program_id(1)
        dyv = lax.dot_general(dx_ref[...], w_ref[...], _NT, preferred_element_type=F32)
        a = jnp.where(j < n1, f_ref[...], jnp.where(j < n2, d_ref[...], m_ref[...]))
        gt = g_ref[...].astype(F32)
        sg = 1.0 / (1.0 + jnp.exp(-gt))
        da_ref[...] = (dyv * gt * sg).astype(da_ref.dtype)
        dg_ref[...] = (dyv * a * sg * (1.0 + gt * (1.0 - sg))).astype(dg_ref.dtype)

    def gcol(j):
        return jnp.where(j < n1, C_FG // tn + j, jnp.where(j < n2, C_DG // tn + j - n1, C_MG // tn + j - n2))

    tile = pl.BlockSpec((tm, tn), lambda i, j: (i, j))
    return pl.pallas_call(
        body,
        out_shape=(jax.ShapeDtypeStruct((t, MIX_W), BF16), jax.ShapeDtypeStruct((t, MIX_W), BF16)),
        grid=(t // tm, n3),
        in_specs=[pl.BlockSpec((tm, d), lambda i, j: (i, 0)), pl.BlockSpec((tn, d), lambda i, j: (j, 0)),
                  pl.BlockSpec((tm, tn), lambda i, j: (i, jnp.minimum(j, n1 - 1))),
                  pl.BlockSpec((tm, tn), lambda i, j: (i, jnp.clip(j - n1, 0, n2 - n1 - 1))),
                  pl.BlockSpec((tm, tn), lambda i, j: (i, jnp.clip(j - n2, 0, n3 - n2 - 1))),
                  pl.BlockSpec((tm, tn), lambda i, j: (i, gcol(j)))],
        out_specs=(tile, tile),
        compiler_params=_cparams(dimension_semantics=("parallel", "parallel")),
        name=name,
    )(dx2b, wo, fox, dil, memo, p16)


def _silu(g):
    return g / (1.0 + jnp.exp(-g))


def _out_loss(fox, dil, memo, p16, wo, x, tgt, gfin, *, tm, name):
    t, d = x.shape
    n_feat = float(d)

    def body(f_ref, d_ref, m_ref, fg_ref, dg_ref, mg_ref, w_ref, x_ref, t_ref, g_ref, y_ref, dx_ref, dxb_ref, st_ref):
        i = pl.program_id(0)

        @pl.when(i == 0)
        def _():
            st_ref[...] = jnp.zeros_like(st_ref)

        y = jnp.concatenate([(a_ref[...] * _silu(gt_ref[...].astype(F32))).astype(BF16)
                             for a_ref, gt_ref in ((f_ref, fg_ref), (d_ref, dg_ref), (m_ref, mg_ref))], axis=1)
        y_ref[...] = y
        x2 = x_ref[...] + jnp.dot(y, w_ref[...], preferred_element_type=F32)
        r = lax.rsqrt(jnp.mean(x2 * x2, axis=-1, keepdims=True) + RMS_EPS)
        nrm = x2 * r
        gv = g_ref[...]
        err = nrm * gv - t_ref[...]
        dout = err * (1.0 / n_feat)
        dn = dout * gv
        dx2 = r * (dn - nrm * jnp.mean(dn * nrm, axis=-1, keepdims=True))
        dx_ref[...] = dx2
        dxb_ref[...] = dx2.astype(dxb_ref.dtype)
        st_ref[0:1, :] += jnp.sum(dout * nrm, axis=0, keepdims=True)
        st_ref[1:2, :] += (0.5 / n_feat) * jnp.sum(err * err, axis=0, keepdims=True)

    row = pl.BlockSpec((tm, d), lambda i: (i, 0))
    whole = lambda w: pl.BlockSpec((tm, w), lambda i: (i, 0))
    gate = lambda w, col: pl.BlockSpec((tm, w), lambda i: (i, col // w))
    return pl.pallas_call(
        body,
        out_shape=(jax.ShapeDtypeStruct((t, MIX_W), BF16), jax.ShapeDtypeStruct((t, d), F32), jax.ShapeDtypeStruct((t, d), BF16),
                   jax.ShapeDtypeStruct((8, d), F32)),
        grid=(t // tm,),
        in_specs=[whole(FOX_W), whole(DIL_W), whole(MEM_W), gate(FOX_W, C_FG), gate(DIL_W, C_DG), gate(MEM_W, C_MG),
                  pl.BlockSpec((MIX_W, d), lambda i: (0, 0)), row, row, pl.BlockSpec((1, d), lambda i: (0, 0))],
        out_specs=(pl.BlockSpec((tm, MIX_W), lambda i: (i, 0)), row, row, pl.BlockSpec((8, d), lambda i: (0, 0))),
        compiler_params=_cparams(dimension_semantics=("arbitrary",)),
        name=name,
    )(fox, dil, memo, p16, p16, p16, wo, x, tgt, gfin)


def _dh_rms_bwd(dp, w, x, g, resid, *, tm, tk, name):
    t, d = x.shape
    kdim = dp.shape[1]
    nk = kdim // tk

    def body(*refs):
        if resid is not None:
            dp_ref, w_ref, x_ref, g_ref, r_ref, dx_ref, gg_ref, acc_ref = refs
        else:
            dp_ref, w_ref, x_ref, g_ref, dx_ref, gg_ref, acc_ref = refs
        i = pl.program_id(0)
        k = pl.program_id(1)

        @pl.when(jnp.logical_and(i == 0, k == 0))
        def _():
            gg_ref[...] = jnp.zeros_like(gg_ref)

        prod = lax.dot_general(dp_ref[...], w_ref[...], _NT, preferred_element_type=F32)

        @pl.when(k == 0)
        def _():
            acc_ref[...] = prod

        @pl.when(k > 0)
        def _():
            acc_ref[...] += prod

        @pl.when(k == nk - 1)
        def _():
            dh = acc_ref[...]
            xv = x_ref[...]
            r = lax.rsqrt(jnp.mean(xv * xv, axis=-1, keepdims=True) + RMS_EPS)
            nrm = xv * r
            dn = dh * g_ref[...]
            dx = r * (dn - nrm * jnp.mean(dn * nrm, axis=-1, keepdims=True))
            if resid is not None:
                dx = dx + r_ref[...]
            dx_ref[...] = dx
            gg_ref[0:1, :] += jnp.sum(dh * nrm, axis=0, keepdims=True)

    row = pl.BlockSpec((tm, d), lambda i, k: (i, 0))
    in_specs = [pl.BlockSpec((tm, tk), lambda i, k: (i, k)), pl.BlockSpec((d, tk), lambda i, k: (0, k)), row,
                pl.BlockSpec((1, d), lambda i, k: (0, 0))]
    args = [dp, w, x, g]
    if resid is not None:
        in_specs.append(row)
        args.append(resid)
    return pl.pallas_call(
        body,
        out_shape=(jax.ShapeDtypeStruct((t, d), F32), jax.ShapeDtypeStruct((8, d), F32)),
        grid=(t // tm, nk),
        in_specs=in_specs,
        out_specs=(row, pl.BlockSpec((8, d), lambda i, k: (0, 0))),
        scratch_shapes=[pltpu.VMEM((tm, d), F32)],
        compiler_params=_cparams(dimension_semantics=("arbitrary", "arbitrary")),
        name=name,
    )(*args)


_FLOG0 = 4 * FOX_W
_W_IN_SEGMENTS = ((0, _FLOG0, 0), (_FLOG0, _FLOG0 + FOX_HEADS, PW), (_FLOG0 + FOX_HEADS, IN_W, C_DQ))
SHARD_W = IN_W // N_CHIPS


def _rearrange_w_in(shards):
    def cols(lo, hi):
        parts = []
        for k in range(N_CHIPS):
            a, b = max(lo, k * SHARD_W), min(hi, (k + 1) * SHARD_W)
            if a < b:
                parts.append(shards[k][:, a - k * SHARD_W:b - k * SHARD_W])
        return parts

    (a0, a1, _), (f0, f1, _), (b0, b1, _) = _W_IN_SEGMENTS
    pad = jnp.zeros((shards[0].shape[0], PWF - PW - FOX_HEADS), shards[0].dtype)
    return jnp.concatenate(cols(a0, a1) + cols(b0, b1) + cols(f0, f1) + [pad], axis=1)


def _w_in_grad_slabs(g):
    slabs = []
    for k in range(N_CHIPS):
        parts = []
        for lo, hi, at in _W_IN_SEGMENTS:
            a, b = max(lo, k * SHARD_W), min(hi, (k + 1) * SHARD_W)
            if a < b:
                parts.append(g[:, at + a - lo:at + b - lo])
        slabs.append(jnp.concatenate(parts, axis=1))
    return jnp.stack(slabs, axis=0)


def _local_grads(x, mem, norm_g, w_r, b_forget, mem_norm_g, w_kv, w_o, final_norm_g, tgt, start_reduce=None,
                 early_token=None, late_weights=None):
    nb = x.shape[0]
    t = nb * SEQ
    x2d = x.reshape(t, D_MODEL)
    tgt2d = tgt.reshape(t, D_MODEL)
    tabs = _rope_tables()
    bpad = jnp.pad(b_forget.reshape(1, FOX_HEADS), ((0, 0), (0, LANES - FOX_HEADS)))

    gain0 = norm_g.reshape(1, D_MODEL)
    if early_token is not None:
        gain0 = gain0 + early_token[0:1, 0:1]
    h = _rms_fwd(x2d, gain0, tm=512, name="rms_x")
    p16, dqkv = _proj(h, w_r, tabs, n=PW, tm=2048, tn=256, name="proj")
    flog = _matmul(h, w_r[:, PW:PW + LANES], out_dtype=F32, tm=1024, tn=LANES, tk=D_MODEL, name="proj_flog")
    c12 = _flog_fwd(flog, bpad, nb=nb, ts=256, name="flog_fwd")

    crow = c12[:, :FOX_HEADS].reshape(nb, NBLK, BLK, FOX_HEADS // 2, 2).transpose(0, 3, 1, 4, 2)
    crow = jnp.pad(crow, ((0, 0), (0, 0), (0, 0), (0, 6), (0, 0)))
    p3 = p16.reshape(nb, SEQ, PW)
    fox, fox_lse = _fox_fwd(p3, crow, name="fox_fwd")
    if late_weights is not None:
        w_kv, w_o = late_weights(fox_lse)

    dqkv3 = dqkv.reshape(nb, SEQ, 3 * DIL_W)
    dil, dil_lse = _dil_fwd(dqkv3, name="dil_fwd")

    mh = _rms_fwd(mem.reshape(nb * MEM_LEN, D_MODEL), mem_norm_g.reshape(1, D_MODEL), tm=nb * MEM_LEN, name="rms_mem")
    mkv = _matmul(mh, w_kv, out_dtype=BF16, tm=nb * MEM_LEN, tn=512, tk=D_MODEL, name="mem_kv")
    mkv3 = mkv.reshape(nb, MEM_LEN, 2 * MEM_W)
    mcfg = _mem_cfg()
    memo, mem_lse = _attn_fwd(mcfg, p3, mkv3, mkv3, out_cols=MEM_W, name="mem_fwd")

    fox2, dil2, memo2 = fox.reshape(t, FOX_W), dil.reshape(t, DIL_W), memo.reshape(t, MEM_W)
    y, dx2, dx2b, st = _out_loss(fox2, dil2, memo2, p16, w_o, x2d, tgt2d, final_norm_g.reshape(1, D_MODEL), tm=256,
                                 name="out_loss")

    g_wo = _matmul(y, dx2b, mode="tn", out_dtype=F32, tm=1024, tn=512, tk=1024, name="grad_w_out")
    datt, dgate = _dy_gate_bwd(dx2b, w_o, fox2, dil2, memo2, p16, tm=1024, tn=256, name="dy_gate_bwd")
    datt3 = datt.reshape(nb, SEQ, MIX_W)

    dfq, dfk, dfv, dcr = _fox_bwd(p3, crow, datt3, fox, fox_lse, do_off=0, name="fox_bwd")
    dcol = -dcr[:, :, :, :2, :].transpose(0, 2, 4, 1, 3).reshape(t, FOX_HEADS)
    dcol = jnp.pad(dcol, ((0, 0), (0, LANES - FOX_HEADS)))
    dflog, gb = _flog_bwd(dcol, flog, bpad, nb=nb, ts=256, name="flog_bwd")

    ddq, ddk, ddv = _dil_bwd(dqkv3, datt3, dil, dil_lse, tabs, do_off=_B1, name="dil_bwd")

    dmq, dmk, dmv = _attn_bwd(mcfg, p3, mkv3, mkv3, datt3, memo, mem_lse, out_cols=MEM_W, kv_cols=MEM_W, do_off=_B2,
                              name="mem_bwd")
    dmkv = jnp.concatenate([dmk, dmv], axis=-1).reshape(nb * MEM_LEN, 2 * MEM_W).astype(BF16)
    g_wkv = _matmul(mh, dmkv, mode="tn", out_dtype=F32, tm=512, tn=512, tk=nb * MEM_LEN, name="grad_w_kv")
    _, gmn = _dh_rms_bwd(dmkv, w_kv, mem.reshape(nb * MEM_LEN, D_MODEL), mem_norm_g.reshape(1, D_MODEL), None,
                         tm=nb * MEM_LEN, tk=2 * MEM_W, name="mem_rms_bwd")

    flat = lambda a: a.reshape(t, -1)
    dp = jnp.concatenate([flat(dfq), flat(dfk), flat(dfv), dgate[:, :FOX_W], flat(ddq), flat(ddk), flat(ddv),
                          dgate[:, FOX_W:FOX_W + DIL_W], flat(dmq).astype(BF16), dgate[:, FOX_W + DIL_W:], dflog,
                          jnp.zeros((t, PWF - PW - LANES), BF16)], axis=1)
    g_wr = _matmul(h, dp, mode="tn", out_dtype=F32, tm=D_MODEL, tn=512, tk=t, name="grad_w_in")
    gain = norm_g.reshape(1, D_MODEL)
    if start_reduce is not None:
        gain = gain + start_reduce(g_wr, g_wkv, g_wo)[0:1, 0:1]
    gx, gng = _dh_rms_bwd(dp, w_r, x2d, gain, dx2, tm=512, tk=PWF // 3, name="in_rms_bwd")

    gb_row = jnp.pad(gb[0:1, :], ((0, 0), (0, D_MODEL - LANES)))
    small = jnp.concatenate([gng[0:1], gmn[0:1], st[0:1], gb_row, st[1:2], jnp.zeros((3, D_MODEL), F32)], axis=0)
    return gx.reshape(nb, SEQ, D_MODEL), g_wr, g_wkv, g_wo, small


MESH = pl.DeviceIdType.MESH
ANY = pl.BlockSpec(memory_space=pl.ANY)


def _place():
    x, y, c = lax.axis_index("x"), lax.axis_index("y"), lax.axis_index("c")
    other_chips = [(1 - x, y), (x, 1 - y), (1 - x, 1 - y)]
    return x, y, c, other_chips


def _gather_weights(shards):
    n = len(shards)

    def body(*refs):
        in_refs, out_refs = refs[:n], refs[n:2 * n]
        send_sems, recv_sems = refs[2 * n:]
        x, y, c, chips = _place()
        me_chip = 2 * x + y
        sibling = (x, y, 1 - c)

        def half(ref, pc, rows):
            return ref.at[pl.ds(pc * (rows // 2), rows // 2), :]

        def rcopy(k, src, dst, to):
            return pltpu.make_async_remote_copy(src_ref=src, dst_ref=dst, send_sem=send_sems.at[k], recv_sem=recv_sems.at[k],
                                                device_id=to, device_id_type=MESH)

        sends = []
        for t in range(n):
            rows = shards[t].shape[0]
            for j, chip in enumerate(chips):
                cp = rcopy(6 * t + j, half(in_refs[t], c, rows), half(out_refs[t].at[me_chip], c, rows), (*chip, c))
                cp.start()
                sends.append(cp)
        for t in range(n):
            rows = shards[t].shape[0]
            for j, chip in enumerate(chips):
                slot = out_refs[t].at[2 * chip[0] + chip[1]]
                rcopy(6 * t + j, half(slot, c, rows), half(slot, c, rows), sibling).wait_recv()
                fw = rcopy(6 * t + 3 + j, half(slot, c, rows), half(slot, c, rows), sibling)
                fw.start()
                sends.append(fw)
        for t in range(n):
            rows = shards[t].shape[0]
            for j, chip in enumerate(chips):
                slot = out_refs[t].at[2 * chip[0] + chip[1]]
                rcopy(6 * t + 3 + j, half(slot, 1 - c, rows), half(slot, 1 - c, rows), sibling).wait_recv()
        for cp in sends:
            cp.wait_send()

    return pl.pallas_call(
        body,
        out_shape=tuple(jax.ShapeDtypeStruct((N_CHIPS,) + s.shape, s.dtype) for s in shards),
        in_specs=[ANY] * n,
        out_specs=tuple([ANY] * n),
        scratch_shapes=[pltpu.SemaphoreType.DMA((6 * n,)), pltpu.SemaphoreType.DMA((6 * n,))],
        name="gather_weights",
    )(*shards)


def _pair_exchange(gs):
    n = len(gs)

    def body(*refs):
        g_refs, r_refs = refs[:n], refs[n:2 * n]
        send_sems, recv_sems = refs[2 * n:]
        x, y, c, _ = _place()
        cps = []
        for t in range(n):
            hr = gs[t].shape[1] // 2
            cp = pltpu.make_async_remote_copy(src_ref=g_refs[t].at[:, pl.ds((1 - c) * hr, hr), :], dst_ref=r_refs[t],
                                              send_sem=send_sems.at[t], recv_sem=recv_sems.at[t],
                                              device_id=(x, y, 1 - c), device_id_type=MESH)
            cp.start()
            cps.append(cp)
        for cp in cps:
            cp.wait()

    return pl.pallas_call(
        body,
        out_shape=tuple(jax.ShapeDtypeStruct((N_CHIPS, g.shape[1] // 2, g.shape[2]), g.dtype) for g in gs),
        in_specs=[ANY] * n,
        out_specs=tuple([ANY] * n),
        scratch_shapes=[pltpu.SemaphoreType.DMA((n,)), pltpu.SemaphoreType.DMA((n,))],
        name="pair_exchange",
    )(*gs)


def _chip_exchange(ps):
    n = len(ps)

    def body(*refs):
        p_refs, o_refs = refs[:n], refs[n:2 * n]
        send_sems, recv_sems = refs[2 * n:]
        x, y, c, chips = _place()
        me_chip = 2 * x + y
        cps = []
        for t in range(n):
            for j, chip in enumerate(chips):
                cp = pltpu.make_async_remote_copy(src_ref=p_refs[t].at[2 * chip[0] + chip[1]], dst_ref=o_refs[t].at[me_chip],
                                                  send_sem=send_sems.at[3 * t + j], recv_sem=recv_sems.at[3 * t + j],
                                                  device_id=(*chip, c), device_id_type=MESH)
                cp.start()
                cps.append(cp)
        for cp in cps:
            cp.wait()

    return pl.pallas_call(
        body,
        out_shape=tuple(jax.ShapeDtypeStruct(p.shape, p.dtype) for p in ps),
        in_specs=[ANY] * n,
        out_specs=tuple([ANY] * n),
        scratch_shapes=[pltpu.SemaphoreType.DMA((3 * n,)), pltpu.SemaphoreType.DMA((3 * n,))],
        name="chip_exchange",
    )(*ps)


_HBM = pl.BlockSpec(memory_space=pltpu.HBM)
_SEM = pl.BlockSpec(memory_space=pltpu.SEMAPHORE)
_DATAFLOW = pltpu.SideEffectType.DATAFLOW_SIDE_EFFECTING


def _chip_copies(p_refs, land_refs, send_sems, recv_sems):
    x, y, c, chips = _place()
    me_chip = 2 * x + y
    return [pltpu.make_async_remote_copy(src_ref=p_refs[t].at[2 * chip[0] + chip[1]], dst_ref=land_refs[t].at[me_chip],
                                         send_sem=send_sems.at[3 * t + j], recv_sem=recv_sems.at[3 * t + j],
                                         device_id=(*chip, c), device_id_type=MESH)
            for t in range(len(p_refs)) for j, chip in enumerate(chips)]


def _chip_exchange_start(ps):
    n = len(ps)

    def body(*refs):
        p_refs, land_refs = refs[:n], refs[n:2 * n]
        send_sems, recv_sems = refs[2 * n:2 * n + 2]
        token = refs[-1]
        for cp in _chip_copies(p_refs, land_refs, send_sems, recv_sems):
            cp.start()
        token[...] = jnp.zeros_like(token)

    hbm = [pltpu.HBM(p.shape, p.dtype) for p in ps]
    args = [pltpu.with_memory_space_constraint(p, pltpu.HBM) for p in ps]
    args += [pltpu.with_memory_space_constraint(lax.empty(p.shape, p.dtype), pltpu.HBM) for p in ps]
    out = pl.pallas_call(
        body,
        name="chip_exchange_start",
        out_shape=(pltpu.SemaphoreType.DMA((3 * n,)), pltpu.SemaphoreType.DMA((3 * n,)), *hbm, *hbm,
                   jax.ShapeDtypeStruct((8, LANES), F32)),
        in_specs=[_HBM] * (2 * n),
        out_specs=(_SEM, _SEM, *([_HBM] * (2 * n)), pl.BlockSpec(memory_space=pltpu.VMEM)),
        input_output_aliases={i: 2 + i for i in range(2 * n)},
        compiler_params=pltpu.CompilerParams(has_side_effects=_DATAFLOW),
    )(*args)
    return out[0], out[1], out[2:2 + n], out[2 + n:2 + 2 * n], out[-1]


def _chip_exchange_wait(send_sems, recv_sems, p_thru, land_thru, after):
    n = len(p_thru)

    def body(*refs):
        p_refs, land_refs = refs[:n], refs[n:2 * n]
        ssem, rsem = refs[2 * n:2 * n + 2]
        for cp in _chip_copies(p_refs, land_refs, ssem, rsem):
            cp.wait_send()
            cp.wait_recv()

    hbm = [pltpu.HBM(p.shape, p.dtype) for p in p_thru]
    out = pl.pallas_call(
        body,
        name="chip_exchange_wait",
        out_shape=(*hbm, *hbm),
        in_specs=[_HBM] * (2 * n) + [_SEM, _SEM, ANY],
        out_specs=tuple([_HBM] * (2 * n)),
        input_output_aliases={i: i for i in range(2 * n)},
        compiler_params=pltpu.CompilerParams(has_side_effects=_DATAFLOW),
    )(*p_thru, *land_thru, send_sems, recv_sems, after)
    return out[:n], out[n:]


def _shard_copies(s_refs, land_refs, send_sems, recv_sems):
    x, y, c, chips = _place()
    me_chip = 2 * x + y
    return [pltpu.make_async_remote_copy(src_ref=s_refs[t], dst_ref=land_refs[t].at[me_chip],
                                         send_sem=send_sems.at[3 * t + j], recv_sem=recv_sems.at[3 * t + j],
                                         device_id=(*chip, c), device_id_type=MESH)
            for t in range(len(s_refs)) for j, chip in enumerate(chips)]


def _gather_late_start(shards):
    n = len(shards)

    def body(*refs):
        s_refs, land_refs = refs[:n], refs[n:2 * n]
        send_sems, recv_sems = refs[2 * n:2 * n + 2]
        token = refs[-1]
        for cp in _shard_copies(s_refs, land_refs, send_sems, recv_sems):
            cp.start()
        token[...] = jnp.zeros_like(token)

    lands = [(N_CHIPS,) + s.shape for s in shards]
    args = [pltpu.with_memory_space_constraint(s, pltpu.HBM) for s in shards]
    args += [pltpu.with_memory_space_constraint(lax.empty(shp, s.dtype), pltpu.HBM) for shp, s in zip(lands, shards)]
    out = pl.pallas_call(
        body,
        name="gather_late_start",
        out_shape=(pltpu.SemaphoreType.DMA((3 * n,)), pltpu.SemaphoreType.DMA((3 * n,)),
                   *[pltpu.HBM(s.shape, s.dtype) for s in shards], *[pltpu.HBM(shp, s.dtype) for shp, s in zip(lands, shards)],
                   jax.ShapeDtypeStruct((8, LANES), F32)),
        in_specs=[_HBM] * (2 * n),
        out_specs=(_SEM, _SEM, *([_HBM] * (2 * n)), pl.BlockSpec(memory_space=pltpu.VMEM)),
        input_output_aliases={i: 2 + i for i in range(2 * n)},
        compiler_params=pltpu.CompilerParams(has_side_effects=_DATAFLOW),
    )(*args)
    return out[0], out[1], out[2:2 + n], out[2 + n:2 + 2 * n], out[-1]


def _gather_late_wait(send_sems, recv_sems, s_thru, land_thru, after):
    n = len(s_thru)

    def body(*refs):
        s_refs, land_refs = refs[:n], refs[n:2 * n]
        ssem, rsem = refs[2 * n:2 * n + 2]
        for cp in _shard_copies(s_refs, land_refs, ssem, rsem):
            cp.wait_send()
            cp.wait_recv()

    out = pl.pallas_call(
        body,
        name="gather_late_wait",
        out_shape=(*[pltpu.HBM(s.shape, s.dtype) for s in s_thru], *[pltpu.HBM(l.shape, l.dtype) for l in land_thru]),
        in_specs=[_HBM] * (2 * n) + [_SEM, _SEM, ANY],
        out_specs=tuple([_HBM] * (2 * n)),
        input_output_aliases={i: i for i in range(2 * n)},
        compiler_params=pltpu.CompilerParams(has_side_effects=_DATAFLOW),
    )(*s_thru, *land_thru, send_sems, recv_sems, after)
    return out[:n], out[n:]


def _pair_swap(rs):
    n = len(rs)

    def body(*refs):
        r_refs, o_refs = refs[:n], refs[n:2 * n]
        send_sems, recv_sems = refs[2 * n:]
        x, y, c, _ = _place()
        cps = []
        for t in range(n):
            cp = pltpu.make_async_remote_copy(src_ref=r_refs[t], dst_ref=o_refs[t], send_sem=send_sems.at[t],
                                              recv_sem=recv_sems.at[t], device_id=(x, y, 1 - c), device_id_type=MESH)
            cp.start()
            cps.append(cp)
        for cp in cps:
            cp.wait()

    return pl.pallas_call(
        body,
        out_shape=tuple(jax.ShapeDtypeStruct(r.shape, r.dtype) for r in rs),
        in_specs=[ANY] * n,
        out_specs=tuple([ANY] * n),
        scratch_shapes=[pltpu.SemaphoreType.DMA((n,)), pltpu.SemaphoreType.DMA((n,))],
        name="pair_swap",
    )(*rs)


N_DEV = 8
LOSS_ROW = 4


def _small_allreduce(small):
    def body(s_ref, o_ref, all_ref, send_sems, recv_sems):
        x, y, c, _ = _place()
        me = 4 * x + 2 * y + c
        all_ref[me] = s_ref[...]
        cps = []
        for k in range(1, N_DEV):
            peer = tuple(1 - p if (k >> s) & 1 else p for p, s in ((x, 2), (y, 1), (c, 0)))
            cp = pltpu.make_async_remote_copy(src_ref=s_ref, dst_ref=all_ref.at[me], send_sem=send_sems.at[k - 1],
                                              recv_sem=recv_sems.at[k - 1], device_id=peer, device_id_type=MESH)
            cp.start()
            cps.append(cp)
        for cp in cps:
            cp.wait()
        tot = all_ref[0]
        for d in range(1, N_DEV):
            tot = tot + all_ref[d]
        o_ref[...] = tot
        o_ref[LOSS_ROW:LOSS_ROW + 1, :] = jnp.broadcast_to(jnp.sum(tot[LOSS_ROW:LOSS_ROW + 1, :], axis=1, keepdims=True),
                                                          (1, tot.shape[1]))

    vm = pl.BlockSpec(memory_space=pltpu.VMEM)
    return pl.pallas_call(
        body,
        out_shape=jax.ShapeDtypeStruct(small.shape, small.dtype),
        in_specs=[vm],
        out_specs=vm,
        scratch_shapes=[pltpu.VMEM((N_DEV,) + small.shape, small.dtype), pltpu.SemaphoreType.DMA((N_DEV - 1,)),
                        pltpu.SemaphoreType.DMA((N_DEV - 1,))],
        name="small_allreduce",
    )(small)


def _sum_pair(g, recv, cidx, *, tr, name):
    _, hr, cols = recv.shape
    nr = hr // tr

    def body(c_ref, g_ref, r_ref, o_ref):
        o_ref[...] = (g_ref[...] + r_ref[...]).astype(o_ref.dtype)

    grid_spec = pltpu.PrefetchScalarGridSpec(
        num_scalar_prefetch=1,
        grid=(N_CHIPS, nr),
        in_specs=[pl.BlockSpec((None, tr, cols), lambda k, i, c_ref: (k, c_ref[0] * nr + i, 0)),
                  pl.BlockSpec((None, tr, cols), lambda k, i, c_ref: (k, i, 0))],
        out_specs=pl.BlockSpec((None, tr, cols), lambda k, i, c_ref: (k, i, 0)),
    )
    return pl.pallas_call(body, out_shape=jax.ShapeDtypeStruct(recv.shape, BF16), grid_spec=grid_spec,
                          compiler_params=_cparams(), name=name)(cidx, g, recv)


def _sum_chips(p, *, tr, name):
    _, rows, cols = p.shape

    def body(p_ref, o_ref):
        tot = p_ref[0].astype(F32)
        for k in range(1, N_CHIPS):
            tot = tot + p_ref[k].astype(F32)
        o_ref[...] = tot

    return pl.pallas_call(
        body,
        out_shape=jax.ShapeDtypeStruct((rows, cols), F32),
        grid=(rows // tr,),
        in_specs=[pl.BlockSpec((N_CHIPS, tr, cols), lambda i: (0, i, 0))],
        out_specs=pl.BlockSpec((tr, cols), lambda i: (i, 0)),
        compiler_params=_cparams(),
        name=name,
    )(p)


def _adamw(w, g, m, v, *, tr, name):
    rows, cols = w.shape
    bc1 = 1.0 / (1.0 - ADAM_B1 ** ADAM_STEP)
    bc2 = 1.0 / (1.0 - ADAM_B2 ** ADAM_STEP)

    def body(w_ref, g_ref, m_ref, v_ref, d_ref, nm_ref, nv_ref):
        gv = g_ref[...]
        nm = ADAM_B1 * m_ref[...] + (1.0 - ADAM_B1) * gv
        nv = ADAM_B2 * v_ref[...] + (1.0 - ADAM_B2) * (gv * gv)
        d_ref[...] = -ADAM_LR * ((nm * bc1) / (jnp.sqrt(nv * bc2) + ADAM_EPS) + ADAM_WD * w_ref[...])
        nm_ref[...] = nm
        nv_ref[...] = nv

    spec = pl.BlockSpec((tr, cols), lambda i: (i, 0))
    sd = jax.ShapeDtypeStruct((rows, cols), F32)
    return pl.pallas_call(body, out_shape=(sd, sd, sd), grid=(rows // tr,), in_specs=[spec] * 4, out_specs=(spec,) * 3,
                          compiler_params=_cparams(), name=name)(w, g, m, v)


def _adamw_halves(w, own, sib, cidx, m, v, *, tr, name):
    rows, cols = w.shape
    hr = own.shape[0]
    nr = hr // tr
    assert rows == 2 * hr and hr % tr == 0
    bc1 = 1.0 / (1.0 - ADAM_B1 ** ADAM_STEP)
    bc2 = 1.0 / (1.0 - ADAM_B2 ** ADAM_STEP)

    def body(c_ref, w_ref, o_ref, s_ref, m_ref, v_ref, g_ref, d_ref, nm_ref, nv_ref):
        mine = (pl.program_id(0) // nr) == c_ref[0]
        gv = jnp.where(mine, o_ref[...], s_ref[...])
        nm = ADAM_B1 * m_ref[...] + (1.0 - ADAM_B1) * gv
        nv = ADAM_B2 * v_ref[...] + (1.0 - ADAM_B2) * (gv * gv)
        g_ref[...] = gv
        d_ref[...] = -ADAM_LR * ((nm * bc1) / (jnp.sqrt(nv * bc2) + ADAM_EPS) + ADAM_WD * w_ref[...])
        nm_ref[...] = nm
        nv_ref[...] = nv

    full = pl.BlockSpec((tr, cols), lambda i, c_ref: (i, 0))
    half = pl.BlockSpec((tr, cols), lambda i, c_ref: (i % nr, 0))
    sd = jax.ShapeDtypeStruct((rows, cols), F32)
    grid_spec = pltpu.PrefetchScalarGridSpec(num_scalar_prefetch=1, grid=(rows // tr,), in_specs=[full, half, half, full, full],
                                             out_specs=(full,) * 4)
    return pl.pallas_call(body, out_shape=(sd,) * 4, grid_spec=grid_spec, compiler_params=_cparams(), name=name)(
        cidx, w, own, sib, m, v)


def _pack_small(norm, mem_norm, final_norm, b_forget):
    rows = [norm.reshape(1, D_MODEL), mem_norm.reshape(1, D_MODEL), final_norm.reshape(1, D_MODEL),
            jnp.pad(b_forget.reshape(1, FOX_HEADS), ((0, 0), (0, D_MODEL - FOX_HEADS))), jnp.zeros((4, D_MODEL), F32)]
    return jnp.concatenate(rows, axis=0)


def _unpack_small(a):
    return a[0:1], a[3:4, :FOX_HEADS], a[1:2], a[2]


def kernel(x, mem, norm_g, w_in, b_forget, mem_norm_g, w_mem_kv, w_out, final_norm_g, loss_target, m_norm_g, m_w_in, m_b_forget, m_mem_norm_g, m_w_mem_kv, m_w_out, m_final_norm_g, v_norm_g, v_w_in, v_b_forget, v_mem_norm_g, v_w_mem_kv, v_w_out, v_final_norm_g):
    core = lax.axis_index("c").astype(jnp.int32)
    me_chip = (2 * lax.axis_index("x") + lax.axis_index("y")).astype(jnp.int32)
    cidx = core.reshape(1)

    def own_slot(arr, own):
        return lax.dynamic_update_slice(arr, own[None].astype(arr.dtype), (me_chip,) + (0,) * own.ndim)

    win_b, late = w_in[0].astype(BF16), [w_mem_kv[0].astype(BF16), w_out[0].astype(BF16)]
    g_in, = _gather_weights([win_b])
    g_in, late = lax.optimization_barrier((own_slot(g_in, win_b), late))
    w_r = _rearrange_w_in([g_in[k] for k in range(N_CHIPS)])
    *late_flight, early_token = _gather_late_start(late)

    def late_weights(after):
        shards, landed = _gather_late_wait(*late_flight, after)
        g_kv, g_out = (own_slot(g, s) for g, s in zip(landed, shards))
        return g_kv.reshape(D_MODEL, 2 * MEM_W), g_out.reshape(MIX_W, D_MODEL)

    trs = (128, 128, 256)
    names = ("w_in", "w_mem_kv", "w_out")
    flight = []

    def start_reduce(g_wr, g_wkv, g_wo):
        slabs = [_w_in_grad_slabs(g_wr),
                 g_wkv.reshape(N_CHIPS, D_MODEL // N_CHIPS, 2 * MEM_W),
                 g_wo.reshape(N_CHIPS, MIX_W // N_CHIPS, D_MODEL)]
        recv = _pair_exchange(slabs)
        pair = [_sum_pair(g, r, cidx, tr=tr, name=f"sum_pair_{nm}") for g, r, tr, nm in zip(slabs, recv, trs, names)]
        *handles, token = _chip_exchange_start(pair)
        flight.extend(handles)
        return token

    gx, g_wr, g_wkv, g_wo, small = _local_grads(x, mem, norm_g, w_r, b_forget, mem_norm_g, None, None, final_norm_g, loss_target,
                                                start_reduce=start_reduce, early_token=early_token, late_weights=late_weights)

    send_sems, recv_sems, pair, land = flight
    pair, landed = _chip_exchange_wait(send_sems, recv_sems, pair, land, small)
    got = [lax.dynamic_update_slice(g, lax.dynamic_slice(p, (me_chip, 0, 0), (1,) + p.shape[1:]), (me_chip, 0, 0))
           for g, p in zip(landed, pair)]
    red = [_sum_chips(p, tr=tr, name=f"sum_chips_{nm}") for p, tr, nm in zip(got, trs, names)]
    sib = _pair_swap(red)

    outs = {}
    for nm, r, s, w, m, v, tr in zip(names, red, sib, (w_in, w_mem_kv, w_out), (m_w_in, m_w_mem_kv, m_w_out),
                                     (v_w_in, v_w_mem_kv, v_w_out), trs):
        outs[nm] = tuple(a[None] for a in _adamw_halves(w[0], r, s, cidx, m[0], v[0], tr=tr, name=f"adamw_{nm}"))

    gsum = _small_allreduce(small)
    sd, sm, sv = _adamw(_pack_small(norm_g, mem_norm_g, final_norm_g, b_forget), gsum,
                        _pack_small(m_norm_g, m_mem_norm_g, m_final_norm_g, m_b_forget),
                        _pack_small(v_norm_g, v_mem_norm_g, v_final_norm_g, v_b_forget), tr=8, name="adamw_small")
    loss = gsum[LOSS_ROW, 0]

    def group(i, small_arr):
        ng, bf, mg, fg = _unpack_small(small_arr)
        return (ng, outs["w_in"][i], bf, mg, outs["w_mem_kv"][i], outs["w_out"][i], fg)

    return (loss, gx, *group(0, gsum), *group(1, sd), *group(2, sm), *group(3, sv))
```

```python
import functools
import math

import jax
import jax.numpy as jnp
from jax import lax
from jax.experimental import pallas as pl
from jax.experimental.pallas import tpu as pltpu

F32 = jnp.float32
BF16 = jnp.bfloat16

D_MODEL = 1024
SEQ = 2048
HEAD_DIM = 64
FOX_HEADS = 12
DIL_HEADS = 12
MEM_HEADS = 4
MEM_HEAD_DIM = 128
MEM_LEN = 256
FOX_W = FOX_HEADS * HEAD_DIM
DIL_W = DIL_HEADS * HEAD_DIM
MEM_W = MEM_HEADS * MEM_HEAD_DIM
MIX_W = FOX_W + DIL_W + MEM_W
DILATIONS = ((128, 1), (512, 4), (2048, 16))
ROPE_THETA = 500000.0
ROPE_DIM = HEAD_DIM // 4
RMS_EPS = 1e-6
NEG_INF = -1e30
IN_SIZES = [FOX_W] * 4 + [FOX_HEADS] + [DIL_W] * 4 + [MEM_W] * 2
IN_W = sum(IN_SIZES)

ADAM_LR = 0.001
ADAM_B1 = 0.9
ADAM_B2 = 0.999
ADAM_EPS = 1e-08
ADAM_WD = 0.01
ADAM_STEP = 10

LANES = 128
N_CHIPS = 4
PW = 7168
PWF = PW + 4 * LANES
C_FQ, C_FK, C_FV, C_FG = 0, 768, 1536, 2304
C_DQ, C_DK, C_DV, C_DG = 3072, 3840, 4608, 5376
C_MQ, C_MG = 6144, 6656
VMEM_LIMIT = 48 * 1024 * 1024


def _cparams(**kw):
    return pltpu.CompilerParams(vmem_limit_bytes=VMEM_LIMIT, **kw)


def _matmul(a, b, *, out_dtype, tm, tn, tk, name, mode="nn"):
    if mode == "tn":
        (kdim, m), n = a.shape, b.shape[1]
        a_spec = pl.BlockSpec((tk, tm), lambda i, j, k: (k, i))
        b_spec = pl.BlockSpec((tk, tn), lambda i, j, k: (k, j))
        dims = _T0
    elif mode == "nt":
        (m, kdim), n = a.shape, b.shape[0]
        a_spec = pl.BlockSpec((tm, tk), lambda i, j, k: (i, k))
        b_spec = pl.BlockSpec((tn, tk), lambda i, j, k: (j, k))
        dims = _NT
    else:
        (m, kdim), n = a.shape, b.shape[1]
        a_spec = pl.BlockSpec((tm, tk), lambda i, j, k: (i, k))
        b_spec = pl.BlockSpec((tk, tn), lambda i, j, k: (k, j))
        dims = (((1,), (0,)), ((), ()))
    nk = kdim // tk
    assert m % tm == 0 and n % tn == 0 and kdim % tk == 0

    def body(a_ref, b_ref, o_ref, *scratch):
        prod = lax.dot_general(a_ref[...], b_ref[...], dims, preferred_element_type=F32)
        if nk == 1:
            o_ref[...] = prod.astype(o_ref.dtype)
            return
        acc_ref, = scratch
        k = pl.program_id(2)

        @pl.when(k == 0)
        def _():
            acc_ref[...] = prod

        @pl.when(k > 0)
        def _():
            acc_ref[...] += prod

        @pl.when(k == nk - 1)
        def _():
            o_ref[...] = acc_ref[...].astype(o_ref.dtype)

    return pl.pallas_call(
        body,
        out_shape=jax.ShapeDtypeStruct((m, n), out_dtype),
        grid=(m // tm, n // tn, nk),
        in_specs=[a_spec, b_spec],
        out_specs=pl.BlockSpec((tm, tn), lambda i, j, k: (i, j)),
        scratch_shapes=[pltpu.VMEM((tm, tn), F32)] if nk > 1 else [],
        compiler_params=_cparams(dimension_semantics=("parallel", "parallel", "arbitrary")),
        name=name,
    )(a, b)


def _rms_fwd(x, g, *, tm, name):
    t, d = x.shape

    def body(x_ref, g_ref, h_ref):
        xv = x_ref[...]
        r = lax.rsqrt(jnp.mean(xv * xv, axis=-1, keepdims=True) + RMS_EPS)
        h_ref[...] = (xv * r * g_ref[...]).astype(h_ref.dtype)

    return pl.pallas_call(
        body,
        out_shape=jax.ShapeDtypeStruct((t, d), BF16),
        grid=(t // tm,),
        in_specs=[pl.BlockSpec((tm, d), lambda i: (i, 0)), pl.BlockSpec((1, d), lambda i: (0, 0))],
        out_specs=pl.BlockSpec((tm, d), lambda i: (i, 0)),
        compiler_params=_cparams(),
        name=name,
    )(x, g)


def _rope_tables():
    half = ROPE_DIM // 2
    pos = jnp.arange(SEQ, dtype=F32)
    inv_freq = 1.0 / (ROPE_THETA ** (jnp.arange(0, ROPE_DIM, 2, dtype=F32) / ROPE_DIM))
    ang = pos[:, None] * inv_freq[None, :]
    cos, sin = jnp.cos(ang), jnp.sin(ang)
    one = jnp.ones((SEQ, HEAD_DIM - ROPE_DIM), F32)
    zero = jnp.zeros((SEQ, HEAD_DIM - ROPE_DIM), F32)
    zh = jnp.zeros((SEQ, half), F32)
    c = jnp.concatenate([cos, cos, one], axis=1)
    s1 = jnp.concatenate([zh, sin, zero], axis=1)
    s2 = jnp.concatenate([-sin, zh, zero], axis=1)
    rep = LANES // HEAD_DIM
    return jnp.tile(c, (1, rep)), jnp.tile(s1, (1, rep)), jnp.tile(s2, (1, rep))


def _rope_apply(t, c, s1, s2, transpose=False):
    n = t.shape[-1]
    rep = n // LANES
    c, s1, s2 = (jnp.tile(u, (1, rep)) for u in (c, s1, s2))
    half = ROPE_DIM // 2
    if not transpose:
        return t * c + pltpu.roll(t, half, 1) * s1 + pltpu.roll(t, n - half, 1) * s2
    return t * c + pltpu.roll(t * s1, n - half, 1) + pltpu.roll(t * s2, half, 1)


def _proj(h, w, tabs, *, n, tm, tn, name):
    t, d = h.shape
    assert C_DQ % tn == 0 and (C_DV - C_DQ) % tn == 0 and (C_DG - C_DQ) % tn == 0
    rope_lo, rope_hi, dil_hi = C_DQ // tn, C_DV // tn, C_DG // tn
    s_blocks = SEQ // tm

    def body(h_ref, w_ref, c_ref, s1_ref, s2_ref, o_ref, f_ref):
        j = pl.program_id(1)
        acc = jnp.dot(h_ref[...], w_ref[...], preferred_element_type=F32)
        is_rope = jnp.logical_and(j >= rope_lo, j < rope_hi)

        @pl.when(is_rope)
        def _():
            r = _rope_apply(acc, c_ref[...], s1_ref[...], s2_ref[...])
            o_ref[...] = r.astype(o_ref.dtype)
            f_ref[...] = r

        @pl.when(jnp.logical_not(is_rope))
        def _():
            o_ref[...] = acc.astype(o_ref.dtype)

        @pl.when(jnp.logical_and(j >= rope_hi, j < dil_hi))
        def _():
            f_ref[...] = acc

    tab_spec = pl.BlockSpec((tm, LANES), lambda i, j: (i % s_blocks, 0))
    f_spec = pl.BlockSpec((tm, tn), lambda i, j: (i, jnp.clip(j - rope_lo, 0, dil_hi - rope_lo - 1)))
    return pl.pallas_call(
        body,
        out_shape=(jax.ShapeDtypeStruct((t, n), BF16), jax.ShapeDtypeStruct((t, 3 * DIL_W), F32)),
        grid=(t // tm, n // tn),
        in_specs=[pl.BlockSpec((tm, d), lambda i, j: (i, 0)), pl.BlockSpec((d, tn), lambda i, j: (0, j)),
                  tab_spec, tab_spec, tab_spec],
        out_specs=(pl.BlockSpec((tm, tn), lambda i, j: (i, j)), f_spec),
        compiler_params=_cparams(dimension_semantics=("parallel", "arbitrary")),
        name=name,
    )(h, w, *tabs)


def _split3(x):
    hi = x.astype(BF16)
    r1 = x - hi.astype(F32)
    mid = r1.astype(BF16)
    lo = (r1 - mid.astype(F32)).astype(BF16)
    return hi, mid, lo


def _dot3(sel, x, sel_is_lhs):
    out = None
    for piece in _split3(x):
        t = jnp.dot(sel, piece, preferred_element_type=F32) if sel_is_lhs else jnp.dot(piece, sel, preferred_element_type=F32)
        out = t if out is None else out + t
    return out


C_PIECES = 3


def _c_piece_matrix(piece):
    r = lax.broadcasted_iota(jnp.int32, (LANES, FOX_W), 0)
    c = lax.broadcasted_iota(jnp.int32, (LANES, FOX_W), 1)
    hit = jnp.logical_and(c // LANES == r // 2, c % LANES == C_PIECES * (r % 2) + piece)
    return jnp.where(jnp.logical_and(hit, r < FOX_HEADS), 1.0, 0.0).astype(BF16)


def _flog_fwd(flog, bpad, *, nb, ts, name):
    ns = SEQ // ts

    def body(f_ref, b_ref, ce_ref, carry_ref):
        s = pl.program_id(1)

        @pl.when(s == 0)
        def _():
            carry_ref[...] = jnp.zeros_like(carry_ref)

        z = f_ref[...] + b_ref[...]
        logf = jnp.minimum(z, 0.0) - jnp.log(1.0 + jnp.exp(-jnp.abs(z)))
        r = lax.broadcasted_iota(jnp.int32, (ts, ts), 0)
        c = lax.broadcasted_iota(jnp.int32, (ts, ts), 1)
        tri = jnp.where(r >= c, 1.0, 0.0).astype(BF16)
        cs = _dot3(tri, logf, True) + carry_ref[0:1, :]
        carry_ref[...] = jnp.broadcast_to(cs[ts - 1:ts, :], carry_ref.shape)
        ext = None
        for piece, val in enumerate(_split3(-cs)):
            t = jnp.dot(val, _c_piece_matrix(piece), preferred_element_type=F32)
            ext = t if ext is None else ext + t
        ce_ref[...] = ext.astype(ce_ref.dtype)

    return pl.pallas_call(
        body,
        out_shape=jax.ShapeDtypeStruct((nb * SEQ, FOX_W), BF16),
        grid=(nb, ns),
        in_specs=[pl.BlockSpec((ts, LANES), lambda b, s: (b * ns + s, 0)), pl.BlockSpec((1, LANES), lambda b, s: (0, 0))],
        out_specs=pl.BlockSpec((ts, FOX_W), lambda b, s: (b * ns + s, 0)),
        scratch_shapes=[pltpu.VMEM((8, LANES), F32)],
        compiler_params=_cparams(dimension_semantics=("parallel", "arbitrary")),
        name=name,
    )(flog, bpad)


def _flog_bwd(dcol, flog, bpad, *, nb, ts, name):
    ns = SEQ // ts

    def body(d_ref, f_ref, b_ref, o_ref, gb_ref, carry_ref):
        bi = pl.program_id(0)
        s = pl.program_id(1)

        @pl.when(s == 0)
        def _():
            carry_ref[...] = jnp.zeros_like(carry_ref)

        @pl.when(jnp.logical_and(bi == 0, s == 0))
        def _():
            gb_ref[...] = jnp.zeros_like(gb_ref)

        r = lax.broadcasted_iota(jnp.int32, (ts, ts), 0)
        c = lax.broadcasted_iota(jnp.int32, (ts, ts), 1)
        tri = jnp.where(r <= c, 1.0, 0.0).astype(BF16)
        rc = _dot3(tri, d_ref[...], True) + carry_ref[0:1, :]
        carry_ref[...] = jnp.broadcast_to(rc[0:1, :], carry_ref.shape)
        z = f_ref[...] + b_ref[...]
        dz = rc / (1.0 + jnp.exp(z))
        o_ref[...] = dz.astype(o_ref.dtype)
        gb_ref[...] += jnp.broadcast_to(jnp.sum(dz, axis=0, keepdims=True), gb_ref.shape)

    rev = lambda b, s: (b * ns + (ns - 1 - s), 0)
    return pl.pallas_call(
        body,
        out_shape=(jax.ShapeDtypeStruct((nb * SEQ, LANES), BF16), jax.ShapeDtypeStruct((8, LANES), F32)),
        grid=(nb, ns),
        in_specs=[pl.BlockSpec((ts, LANES), rev), pl.BlockSpec((ts, LANES), rev), pl.BlockSpec((1, LANES), lambda b, s: (0, 0))],
        out_specs=(pl.BlockSpec((ts, LANES), rev), pl.BlockSpec((8, LANES), lambda b, s: (0, 0))),
        scratch_shapes=[pltpu.VMEM((8, LANES), F32)],
        compiler_params=_cparams(dimension_semantics=("arbitrary", "arbitrary")),
        name=name,
    )(dcol, flog, bpad)


class _AttnCfg:
    def __init__(self, *, e, tq, tk, lq, lk, causal, window, ncol, qcol, kcol, vcol, split_p=False):
        self.e, self.tq, self.tk, self.lq, self.lk = e, tq, tk, lq, lk
        self.split_p = split_p
        self.causal, self.window = causal, window
        self.ncol, self.qcol, self.kcol, self.vcol = ncol, qcol, kcol, vcol
        self.nh = LANES // e
        self.scale = 1.0 / math.sqrt(e)
        self.nq, self.nk = lq // tq, lk // tk

    def k_range(self, i):
        if not self.causal:
            return 0, self.nk
        hi = ((i + 1) * self.tq - 1) // self.tk + 1
        if self.window is None:
            return 0, hi
        return jnp.maximum((i * self.tq - self.window) // self.tk, 0), hi


def _head_masks(nh):
    lane = lax.broadcasted_iota(jnp.int32, (1, LANES), 1)
    return [None] if nh == 1 else [lane < HEAD_DIM, lane >= HEAD_DIM]


def _sel(mask, a, b):
    return a if mask is None else jnp.where(mask, a, b)


def _scores(cfg, qh, kb, q0, k0, dlt0, bias):
    s = lax.dot_general(qh, kb, (((1,), (1,)), ((), ())), preferred_element_type=F32) * cfg.scale
    if bias is not None:
        s = s + bias
    if cfg.causal:
        d = dlt0 + (q0 - k0)
        if cfg.window is None:
            ok = d >= 0
        else:
            ok = d.astype(jnp.uint32) <= jnp.uint32(cfg.window)
        s = jnp.where(ok, s, NEG_INF)
    return s


def _attn_fwd(cfg, q, k, v, *, out_cols, bias=None, state=None, finalize=True, name):
    g = q.shape[0]
    tq, tk, e, nh = cfg.tq, cfg.tk, cfg.e, cfg.nh

    def body(*refs):
        refs = list(refs)
        q_ref, k_ref, v_ref = refs[:3]
        del refs[:3]
        if bias is not None:
            cb_ref, cr_ref = refs[:2]
            del refs[:2]
        if state is not None:
            ai_ref, mi_ref, li_ref = refs[:3]
            del refs[:3]
        out_refs = refs
        masks = _head_masks(nh)
        dlt0 = lax.broadcasted_iota(jnp.int32, (tq, tk), 0) - lax.broadcasted_iota(jnp.int32, (tq, tk), 1)

        def qbody(i, carry):
            q0 = pl.multiple_of(i * tq, tq)
            rows = pl.ds(q0, tq)
            qb = q_ref[rows, :]
            lo, hi = cfg.k_range(i)
            res = []
            for h in range(nh):
                qh = _sel(masks[h], qb, jnp.zeros_like(qb))
                if state is not None:
                    m0 = mi_ref[rows, h * e:h * e + 1]
                    l0 = li_ref[rows, h * e:h * e + 1]
                    a0 = ai_ref[rows, :]
                else:
                    m0 = jnp.full((tq, 1), NEG_INF, F32)
                    l0 = jnp.zeros((tq, 1), F32)
                    a0 = jnp.zeros((tq, LANES), F32)
                cq = cb_ref[rows, h * e:h * e + 1] if bias is not None else None

                def kbody(jk, c, qh=qh, cq=cq, h=h):
                    m, l, a = c
                    k0 = pl.multiple_of(jk * tk, tk)
                    kb = k_ref[pl.ds(k0, tk), :]
                    vb = v_ref[pl.ds(k0, tk), :]
                    b = (cq - cr_ref[jk, h:h + 1, :]) if bias is not None else None
                    s = _scores(cfg, qh, kb, q0, k0, dlt0, b)
                    m_new = jnp.maximum(m, jnp.max(s, axis=1, keepdims=True))
                    alpha = jnp.exp(m - m_new)
                    p = jnp.exp(s - m_new)
                    l = alpha * l + jnp.sum(p, axis=1, keepdims=True)
                    pb = p.astype(BF16)
                    pv = jnp.dot(pb, vb, preferred_element_type=F32)
                    if cfg.split_p:
                        pv = pv + jnp.dot((p - pb.astype(F32)).astype(BF16), vb, preferred_element_type=F32)
                    a = alpha * a + pv
                    return m_new, l, a

                res.append(lax.fori_loop(lo, hi, kbody, (m0, l0, a0)))
            if nh == 1:
                m, l, a = res[0]
                m, l = jnp.broadcast_to(m, (tq, LANES)), jnp.broadcast_to(l, (tq, LANES))
            else:
                m = jnp.where(masks[0], res[0][0], res[1][0])
                l = jnp.where(masks[0], res[0][1], res[1][1])
                a = jnp.where(masks[0], res[0][2], res[1][2])
            if finalize:
                out_refs[0][rows, :] = a / l
                out_refs[1][rows, :] = m + jnp.log(l)
            else:
                out_refs[0][rows, :] = a
                out_refs[1][rows, :] = m
                out_refs[2][rows, :] = l
            return carry

        lax.fori_loop(0, cfg.nq, qbody, 0)

    qspec = pl.BlockSpec((None, cfg.lq, LANES), lambda b, j: (b, 0, cfg.qcol(j)))
    kspec = pl.BlockSpec((None, cfg.lk, LANES), lambda b, j: (b, 0, cfg.kcol(j)))
    vspec = pl.BlockSpec((None, cfg.lk, LANES), lambda b, j: (b, 0, cfg.vcol(j)))
    ospec = pl.BlockSpec((None, cfg.lq, LANES), lambda b, j: (b, 0, j))
    args, in_specs = [q, k, v], [qspec, kspec, vspec]
    if bias is not None:
        args += list(bias)
        in_specs += [ospec, pl.BlockSpec((None, None, cfg.nk, 8, tk), lambda b, j: (b, j, 0, 0, 0))]
    aliases = {}
    if state is not None:
        aliases = {len(args) + t: t for t in range(3 if not finalize else 2)}
        args += list(state)
        in_specs += [ospec] * 3
    n_out = 2 if finalize else 3
    osd = jax.ShapeDtypeStruct((g, cfg.lq, out_cols), F32)
    return pl.pallas_call(
        body,
        out_shape=(osd,) * n_out,
        grid=(g, cfg.ncol),
        in_specs=in_specs,
        out_specs=(ospec,) * n_out,
        input_output_aliases=aliases,
        compiler_params=_cparams(dimension_semantics=("parallel", "parallel")),
        name=name,
    )(*args)


def _attn_bwd(cfg, q, k, v, do, o, lse, *, out_cols, kv_cols, bias=None, acc=None, do_off=0, name):
    g = q.shape[0]
    tq, tk, e, nh = cfg.tq, cfg.tk, cfg.e, cfg.nh
    t0 = (((0,), (0,)), ((), ()))

    def body(*refs):
        refs = list(refs)
        q_ref, k_ref, v_ref, do_ref, o_ref, lse_ref = refs[:6]
        del refs[:6]
        if bias is not None:
            cb_ref, cr_ref = refs[:2]
            del refs[:2]
        if acc is not None:
            dqi_ref, dki_ref, dvi_ref = refs[:3]
            del refs[:3]
        dq_ref, dk_ref, dv_ref = refs[:3]
        dcr_ref = refs[3] if bias is not None else None
        masks = _head_masks(nh)
        dlt0 = lax.broadcasted_iota(jnp.int32, (tq, tk), 0) - lax.broadcasted_iota(jnp.int32, (tq, tk), 1)
        if acc is not None:
            dq_ref[...] = dqi_ref[...]
            dk_ref[...] = dki_ref[...]
            dv_ref[...] = dvi_ref[...]
        else:
            dq_ref[...] = jnp.zeros_like(dq_ref)
            dk_ref[...] = jnp.zeros_like(dk_ref)
            dv_ref[...] = jnp.zeros_like(dv_ref)
        if dcr_ref is not None:
            dcr_ref[...] = jnp.zeros_like(dcr_ref)

        def qbody(i, carry):
            q0 = pl.multiple_of(i * tq, tq)
            rows = pl.ds(q0, tq)
            qb = q_ref[rows, :]
            dob = do_ref[rows, :].astype(BF16)
            prod = dob.astype(F32) * o_ref[rows, :]
            lo, hi = cfg.k_range(i)
            dqs = []
            for h in range(nh):
                qh = _sel(masks[h], qb, jnp.zeros_like(qb))
                doh = _sel(masks[h], dob, jnp.zeros_like(dob))
                lse_h = lse_ref[rows, h * e:h * e + 1]
                delta = jnp.sum(_sel(masks[h], prod, jnp.zeros_like(prod)), axis=1, keepdims=True)
                cq = cb_ref[rows, h * e:h * e + 1] if bias is not None else None

                def kbody(jk, dq_acc, qh=qh, doh=doh, lse_h=lse_h, delta=delta, cq=cq, h=h):
                    k0 = pl.multiple_of(jk * tk, tk)
                    krows = pl.ds(k0, tk)
                    kb = k_ref[krows, :]
                    vb = v_ref[krows, :]
                    b = (cq - cr_ref[jk, h:h + 1, :]) if bias is not None else None
                    s = _scores(cfg, qh, kb, q0, k0, dlt0, b)
                    p = jnp.exp(s - lse_h)
                    dp = lax.dot_general(doh, vb, (((1,), (1,)), ((), ())), preferred_element_type=F32)
                    ds = p * (dp - delta)
                    if dcr_ref is not None:
                        dcr_ref[jk, h:h + 1, :] += jnp.sum(ds, axis=0, keepdims=True)
                    dsb = (ds * cfg.scale).astype(BF16)
                    dv_ref[krows, :] += lax.dot_general(p.astype(BF16), doh, t0, preferred_element_type=F32)
                    dk_ref[krows, :] += lax.dot_general(dsb, qh, t0, preferred_element_type=F32)
                    return dq_acc + jnp.dot(dsb, kb, preferred_element_type=F32)

                dqs.append(lax.fori_loop(lo, hi, kbody, jnp.zeros((tq, LANES), F32)))
            dq = dqs[0] if nh == 1 else jnp.where(masks[0], dqs[0], dqs[1])
            dq_ref[rows, :] += dq
            return carry

        lax.fori_loop(0, cfg.nq, qbody, 0)

    qspec = pl.BlockSpec((None, cfg.lq, LANES), lambda b, j: (b, 0, cfg.qcol(j)))
    kspec = pl.BlockSpec((None, cfg.lk, LANES), lambda b, j: (b, 0, cfg.kcol(j)))
    vspec = pl.BlockSpec((None, cfg.lk, LANES), lambda b, j: (b, 0, cfg.vcol(j)))
    ospec = pl.BlockSpec((None, cfg.lq, LANES), lambda b, j: (b, 0, j))
    kvspec = pl.BlockSpec((None, cfg.lk, LANES), lambda b, j: (b, 0, j))
    dospec = pl.BlockSpec((None, cfg.lq, LANES), lambda b, j: (b, 0, do_off + j))
    args, in_specs = [q, k, v, do, o, lse], [qspec, kspec, vspec, dospec, ospec, ospec]
    out_shape = [jax.ShapeDtypeStruct((g, cfg.lq, out_cols), F32), jax.ShapeDtypeStruct((g, cfg.lk, kv_cols), F32),
                 jax.ShapeDtypeStruct((g, cfg.lk, kv_cols), F32)]
    out_specs = [ospec, kvspec, kvspec]
    if bias is not None:
        args += list(bias)
        crspec = pl.BlockSpec((None, None, cfg.nk, 8, tk), lambda b, j: (b, j, 0, 0, 0))
        in_specs += [ospec, crspec]
        out_shape.append(jax.ShapeDtypeStruct((g, cfg.ncol, cfg.nk, 8, tk), F32))
        out_specs.append(crspec)
    aliases = {}
    if acc is not None:
        aliases = {len(args) + t: t for t in range(3)}
        args += list(acc)
        in_specs += [ospec, kvspec, kvspec]
    return pl.pallas_call(
        body,
        out_shape=tuple(out_shape),
        grid=(g, cfg.ncol),
        in_specs=in_specs,
        out_specs=tuple(out_specs),
        input_output_aliases=aliases,
        compiler_params=_cparams(dimension_semantics=("parallel", "parallel")),
        name=name,
    )(*args)


BLK = 128
NBLK = SEQ // BLK
QK_SCALE = 1.0 / math.sqrt(HEAD_DIM)
DIL_STEPS = tuple(d for _, d in DILATIONS)
assert all(w // d == BLK for w, d in DILATIONS)
_T0 = (((0,), (0,)), ((), ()))
_NT = (((1,), (1,)), ((), ()))


def _stack_heads(a, masks):
    z = jnp.zeros_like(a)
    return jnp.concatenate([jnp.where(masks[0], a, z), jnp.where(masks[1], a, z)], axis=0)


def _tri_bias(lower):
    r = lax.broadcasted_iota(jnp.int32, (BLK, BLK), 0)
    c = lax.broadcasted_iota(jnp.int32, (BLK, BLK), 1)
    return jnp.where((c <= r) if lower else (c >= r), 0.0, NEG_INF).astype(F32)


def _dil_rows(r, i, d):
    start = r + i * (BLK * d)
    return pl.ds(start, BLK) if d == 1 else pl.ds(start, BLK, stride=d)


DIL_SET = 4


def _dil_sets(d, fn):
    nbk = SEQ // d // BLK
    if d == 1:
        def gbody(g, c):
            fn([(0, DIL_SET * g + a, None if a == 0 else True) for a in range(DIL_SET)])
            return c
        lax.fori_loop(0, nbk // DIL_SET, gbody, 0)
    elif nbk > 1:
        assert nbk == DIL_SET
        def rbody(r, c):
            fn([(r, i, i > 0) for i in range(nbk)])
            return c
        lax.fori_loop(0, d, rbody, 0)
    else:
        def rbody(rr, c):
            fn([(DIL_SET * rr + a, 0, False) for a in range(DIL_SET)])
            return c
        lax.fori_loop(0, d // DIL_SET, rbody, 0)


def _dil_key_tiles(r, i, d, has_prev, qrows, tri_cur, tri_prev):
    tiles = [(qrows, tri_cur)]
    if has_prev is None:
        tiles.append((_dil_rows(r, jnp.maximum(i - 1, 0), d), tri_prev + jnp.where(i > 0, 0.0, NEG_INF)))
    elif has_prev:
        tiles.append((_dil_rows(r, i - 1, d), tri_prev))
    return tiles


def _dil_fwd(qkv, *, name):
    nb = qkv.shape[0]
    ncol = DIL_W // LANES
    hd = HEAD_DIM

    def body(q_ref, k_ref, v_ref, o_ref, lse_ref, m_ref, l_ref, a_ref):
        masks = _head_masks(2)
        tri_cur, tri_prev = _tri_bias(True), _tri_bias(False)
        for pi, d in enumerate(DIL_STEPS):
            first, last = pi == 0, pi == len(DIL_STEPS) - 1

            def qset(blocks, d=d, first=first, last=last):
                work = []
                for r, i, has_prev in blocks:
                    qrows = _dil_rows(r, i, d)
                    qcat = _stack_heads((q_ref[qrows, :] * QK_SCALE).astype(BF16), masks)
                    ss, krs = [], []
                    for krows, bias in _dil_key_tiles(r, i, d, has_prev, qrows, tri_cur, tri_prev):
                        s = lax.dot_general(qcat, k_ref[krows, :].astype(BF16), _NT, preferred_element_type=F32)
                        ss.append((s[:BLK] + bias, s[BLK:] + bias))
                        krs.append(krows)
                    work.append((qrows, ss, krs))
                for qrows, ss, krs in work:
                    e0 = ss[0][0] if len(ss) == 1 else jnp.maximum(ss[0][0], ss[1][0])
                    e1 = ss[0][1] if len(ss) == 1 else jnp.maximum(ss[0][1], ss[1][1])
                    n0 = jnp.max(e0, axis=1, keepdims=True)
                    n1 = jnp.max(e1, axis=1, keepdims=True)
                    if not first:
                        mo, lo = m_ref[qrows, :], l_ref[qrows, :]
                        m0, m1 = mo[:, 0:1], mo[:, hd:hd + 1]
                        n0, n1 = jnp.maximum(n0, m0), jnp.maximum(n1, m1)
                        a0, a1 = jnp.exp(m0 - n0), jnp.exp(m1 - n1)
                    ps = [(jnp.exp(s0 - n0), jnp.exp(s1 - n1)) for s0, s1 in ss]
                    t0 = ps[0][0] if len(ps) == 1 else ps[0][0] + ps[1][0]
                    t1 = ps[0][1] if len(ps) == 1 else ps[0][1] + ps[1][1]
                    l0 = jnp.sum(t0, axis=1, keepdims=True)
                    l1 = jnp.sum(t1, axis=1, keepdims=True)
                    acc = None
                    for (p0, p1), krows in zip(ps, krs):
                        vcat = _stack_heads(v_ref[krows, :].astype(BF16), masks)
                        pv = jnp.dot(jnp.concatenate([p0, p1], axis=1).astype(BF16), vcat, preferred_element_type=F32)
                        acc = pv if acc is None else acc + pv
                    if not first:
                        l0 = l0 + a0 * lo[:, 0:1]
                        l1 = l1 + a1 * lo[:, hd:hd + 1]
                        acc = acc + a_ref[qrows, :] * jnp.where(masks[0], a0, a1)
                    if last:
                        o_ref[qrows, :] = acc / jnp.where(masks[0], l0, l1)
                        lse_ref[qrows, :] = jnp.where(masks[0], n0 + jnp.log(l0), n1 + jnp.log(l1))
                    else:
                        m_ref[qrows, :] = jnp.where(masks[0], n0, n1)
                        l_ref[qrows, :] = jnp.where(masks[0], l0, l1)
                        a_ref[qrows, :] = acc

            _dil_sets(d, qset)

    spec = lambda off: pl.BlockSpec((None, SEQ, LANES), lambda b, j: (b, 0, off + j))
    ospec = pl.BlockSpec((None, SEQ, LANES), lambda b, j: (b, 0, j))
    osd = jax.ShapeDtypeStruct((nb, SEQ, DIL_W), F32)
    return pl.pallas_call(
        body, out_shape=(osd, osd), grid=(nb, ncol),
        in_specs=[spec(0), spec(ncol), spec(2 * ncol)], out_specs=(ospec, ospec),
        scratch_shapes=[pltpu.VMEM((SEQ, LANES), F32)] * 3,
        compiler_params=_cparams(dimension_semantics=("parallel", "parallel")), name=name,
    )(qkv, qkv, qkv)


def _dil_bwd(qkv, do, o, lse, tabs, *, do_off, name):
    nb = qkv.shape[0]
    ncol = DIL_W // LANES
    hd = HEAD_DIM

    def body(q_ref, k_ref, v_ref, do_ref, o_ref, lse_ref, c_ref, s1_ref, s2_ref, dqo_ref, dko_ref, dvo_ref,
             dq_ref, dk_ref, dv_ref, dl_ref, dof_ref):
        masks = _head_masks(2)
        tri_cur, tri_prev = _tri_bias(True), _tri_bias(False)
        dq_ref[...] = jnp.zeros_like(dq_ref)
        dk_ref[...] = jnp.zeros_like(dk_ref)
        dv_ref[...] = jnp.zeros_like(dv_ref)

        def delta_body(i, c):
            rows = pl.ds(pl.multiple_of(i * BLK, BLK), BLK)
            dof = do_ref[rows, :].astype(F32)
            dof_ref[rows, :] = dof
            prod = dof * o_ref[rows, :]
            z = jnp.zeros_like(prod)
            dl_ref[rows, :] = jnp.where(masks[0], jnp.sum(jnp.where(masks[0], prod, z), axis=1, keepdims=True),
                                        jnp.sum(jnp.where(masks[1], prod, z), axis=1, keepdims=True))
            return c

        lax.fori_loop(0, NBLK, delta_body, 0)

        for d in DIL_STEPS:
            def qset(blocks, d=d):
                work = []
                for r, i, has_prev in blocks:
                    qrows = _dil_rows(r, i, d)
                    qcat = _stack_heads((q_ref[qrows, :] * QK_SCALE).astype(BF16), masks)
                    docat = _stack_heads(dof_ref[qrows, :].astype(BF16), masks)
                    tiles = []
                    for krows, bias in _dil_key_tiles(r, i, d, has_prev, qrows, tri_cur, tri_prev):
                        s = lax.dot_general(qcat, k_ref[krows, :].astype(BF16), _NT, preferred_element_type=F32)
                        dp = lax.dot_general(docat, v_ref[krows, :].astype(BF16), _NT, preferred_element_type=F32)
                        tiles.append((krows, s, dp, bias))
                    work.append((qrows, qcat, docat, tiles))
                for qrows, qcat, docat, tiles in work:
                    lseb, dlb = lse_ref[qrows, :], dl_ref[qrows, :]
                    lse0, lse1 = lseb[:, 0:1], lseb[:, hd:hd + 1]
                    dl0, dl1 = dlb[:, 0:1], dlb[:, hd:hd + 1]
                    dq = None
                    for krows, s, dp, bias in tiles:
                        p0 = jnp.exp(s[:BLK] + bias - lse0)
                        p1 = jnp.exp(s[BLK:] + bias - lse1)
                        ds0 = p0 * (dp[:BLK] - dl0)
                        ds1 = p1 * (dp[BLK:] - dl1)
                        ds0b, ds1b = ds0.astype(BF16), ds1.astype(BF16)
                        pcat = jnp.concatenate([p0.astype(BF16), p1.astype(BF16)], axis=0)
                        dscat = jnp.concatenate([ds0b, ds1b], axis=0)
                        dv_ref[krows, :] += lax.dot_general(pcat, docat, _T0, preferred_element_type=F32)
                        dk_ref[krows, :] += lax.dot_general(dscat, qcat, _T0, preferred_element_type=F32)
                        dsrow = jnp.concatenate([ds0b, ds1b], axis=1)
                        kcat = _stack_heads((k_ref[krows, :] * QK_SCALE).astype(BF16), masks)
                        t = jnp.dot(dsrow, kcat, preferred_element_type=F32)
                        dq = t if dq is None else dq + t
                    dq_ref[qrows, :] += dq

            _dil_sets(d, qset)

        def out_body(i, c):
            rows = pl.ds(pl.multiple_of(i * BLK, BLK), BLK)
            tab = (c_ref[rows, :], s1_ref[rows, :], s2_ref[rows, :])
            dqo_ref[rows, :] = _rope_apply(dq_ref[rows, :], *tab, transpose=True).astype(dqo_ref.dtype)
            dko_ref[rows, :] = _rope_apply(dk_ref[rows, :], *tab, transpose=True).astype(dko_ref.dtype)
            dvo_ref[rows, :] = dv_ref[rows, :].astype(dvo_ref.dtype)
            return c

        lax.fori_loop(0, NBLK, out_body, 0)

    spec = lambda off: pl.BlockSpec((None, SEQ, LANES), lambda b, j: (b, 0, off + j))
    ospec = pl.BlockSpec((None, SEQ, LANES), lambda b, j: (b, 0, j))
    tspec = pl.BlockSpec((SEQ, LANES), lambda b, j: (0, 0))
    osd = jax.ShapeDtypeStruct((nb, SEQ, DIL_W), BF16)
    return pl.pallas_call(
        body, out_shape=(osd, osd, osd), grid=(nb, ncol),
        in_specs=[spec(0), spec(ncol), spec(2 * ncol), spec(do_off), ospec, ospec, tspec, tspec, tspec],
        out_specs=(ospec, ospec, ospec),
        scratch_shapes=[pltpu.VMEM((SEQ, LANES), F32)] * 5,
        compiler_params=_cparams(dimension_semantics=("parallel", "parallel")), name=name,
    )(qkv, qkv, qkv, do, o, lse, *tabs)


FOX_GROUP = 4
assert NBLK % FOX_GROUP == 0
_FOX_COLS = tuple(c // LANES for c in (C_FQ, C_FK, C_FV))


def _fox_specs():
    cols = [pl.BlockSpec((None, SEQ, LANES), (lambda b, j, off=off: (b, 0, off + j))) for off in _FOX_COLS]
    ospec = pl.BlockSpec((None, SEQ, LANES), lambda b, j: (b, 0, j))
    crspec = pl.BlockSpec((None, None, NBLK, 8, BLK), lambda b, j: (b, j, 0, 0, 0))
    return cols, ospec, crspec


def _fox_q_ext(qcat):
    r = lax.broadcasted_iota(jnp.int32, (2 * BLK, LANES), 0)
    c = lax.broadcasted_iota(jnp.int32, (2 * BLK, LANES), 1)
    first = C_PIECES * (r // BLK)
    sel = jnp.logical_and(c >= first, c < first + C_PIECES)
    return jnp.concatenate([qcat, jnp.where(sel, 1.0, 0.0).astype(qcat.dtype)], axis=1)


def _fox_k_ext(k_ref, ce_ref, rows):
    return jnp.concatenate([k_ref[rows, :], ce_ref[rows, :]], axis=1)


def _fox_key_rows(t, e):
    return pl.ds(pl.multiple_of((FOX_GROUP * t + e) * BLK, BLK), BLK)


def _fox_fwd(p3, cext, *, name):
    nb = p3.shape[0]
    g = FOX_GROUP

    def body(q_ref, k_ref, v_ref, ce_ref, o_ref, lse_ref):
        masks = _head_masks(2)
        tri = _tri_bias(True)

        def qk(qext, t):
            return tuple(lax.dot_general(qext, _fox_k_ext(k_ref, ce_ref, _fox_key_rows(t, e)), _NT, preferred_element_type=F32)
                         for e in range(g))

        def consume(ss, t, state, nblk, diag):
            m0, m1, l0, l1, acc = state
            us = []
            for e in range(nblk):
                u0, u1 = ss[e][:BLK], ss[e][BLK:]
                if diag and e == nblk - 1:
                    u0, u1 = u0 + tri, u1 + tri
                us.append((u0, u1))
            x0 = functools.reduce(jnp.maximum, [u[0] for u in us])
            x1 = functools.reduce(jnp.maximum, [u[1] for u in us])
            n0 = jnp.maximum(m0, jnp.max(x0, axis=1, keepdims=True))
            n1 = jnp.maximum(m1, jnp.max(x1, axis=1, keepdims=True))
            a0, a1 = jnp.exp(m0 - n0), jnp.exp(m1 - n1)
            acc = acc * jnp.where(masks[0], a0, a1)
            t0 = t1 = None
            for e in range(nblk):
                p0, p1 = jnp.exp(us[e][0] - n0), jnp.exp(us[e][1] - n1)
                t0 = p0 if t0 is None else t0 + p0
                t1 = p1 if t1 is None else t1 + p1
                pcat = jnp.concatenate([p0, p1], axis=1)
                hi = pcat.astype(BF16)
                lo = (pcat - hi.astype(F32)).astype(BF16)
                vcat = _stack_heads(v_ref[_fox_key_rows(t, e), :], masks)
                acc = acc + jnp.dot(hi, vcat, preferred_element_type=F32) + jnp.dot(lo, vcat, preferred_element_type=F32)
            l0 = a0 * l0 + jnp.sum(t0, axis=1, keepdims=True)
            l1 = a1 * l1 + jnp.sum(t1, axis=1, keepdims=True)
            return n0, n1, l0, l1, acc

        def gbody(ng, c):
            neg = jnp.full((BLK, 1), NEG_INF, F32)
            z1 = jnp.zeros((BLK, 1), F32)
            rows = [pl.ds(pl.multiple_of((g * ng + a) * BLK, BLK), BLK) for a in range(g)]
            qcats = [_fox_q_ext(_stack_heads(q_ref[rows[a], :] * QK_SCALE, masks)) for a in range(g)]
            first = [qk(qcats[a], 0) for a in range(g)]
            done = []
            for a in range(g):
                def step(t, cc, qcat=qcats[a]):
                    ss, st = cc
                    nxt = qk(qcat, t + 1)
                    return nxt, consume(ss, t, st, g, False)

                done.append(lax.fori_loop(0, ng, step, (first[a], (neg, neg, z1, z1, jnp.zeros((BLK, LANES), F32)))))
            for a in range(g):
                ss, state = done[a]
                m0, m1, l0, l1, acc = consume(ss, ng, state, a + 1, True)
                o_ref[rows[a], :] = acc / jnp.where(masks[0], l0, l1)
                lse_ref[rows[a], :] = jnp.where(masks[0], m0 + jnp.log(l0), m1 + jnp.log(l1))
            return c

        lax.fori_loop(0, NBLK // g, gbody, 0)

    cols, ospec, _ = _fox_specs()
    osd = jax.ShapeDtypeStruct((nb, SEQ, FOX_W), F32)
    return pl.pallas_call(
        body, out_shape=(osd, osd), grid=(nb, FOX_W // LANES), in_specs=cols + [ospec], out_specs=(ospec, ospec),
        compiler_params=_cparams(dimension_semantics=("parallel", "parallel")), name=name,
    )(p3, p3, p3, cext)


def _fox_bwd(p3, cext, do, o, lse, *, do_off, name):
    nb = p3.shape[0]
    g = FOX_GROUP
    hd = HEAD_DIM

    def body(q_ref, k_ref, v_ref, ce_ref, do_ref, o_ref, lse_ref, dq_ref, dko_ref, dvo_ref, dcr_ref, dk_ref, dv_ref):
        masks = _head_masks(2)
        tri = _tri_bias(True)
        dk_ref[...] = jnp.zeros_like(dk_ref)
        dv_ref[...] = jnp.zeros_like(dv_ref)
        dcr_ref[...] = jnp.zeros_like(dcr_ref)

        def products(qcat, docat, t):
            out = []
            for e in range(g):
                krows = _fox_key_rows(t, e)
                out.append(lax.dot_general(_fox_q_ext(qcat), _fox_k_ext(k_ref, ce_ref, krows), _NT, preferred_element_type=F32))
                out.append(lax.dot_general(docat, v_ref[krows, :], _NT, preferred_element_type=F32))
            return tuple(out)

        def consume(prod, t, ctx, dq, nblk, diag):
            qcat, docat, lse0, lse1, dl0, dl1 = ctx
            for e in range(nblk):
                jb = g * t + e
                krows = _fox_key_rows(t, e)
                s, dp = prod[2 * e], prod[2 * e + 1]
                u0, u1 = s[:BLK], s[BLK:]
                if diag and e == nblk - 1:
                    u0, u1 = u0 + tri, u1 + tri
                p0 = jnp.exp(u0 - lse0)
                p1 = jnp.exp(u1 - lse1)
                ds0 = p0 * (dp[:BLK] - dl0)
                ds1 = p1 * (dp[BLK:] - dl1)
                dcr_ref[jb, 0:1, :] += jnp.sum(ds0, axis=0, keepdims=True)
                dcr_ref[jb, 1:2, :] += jnp.sum(ds1, axis=0, keepdims=True)
                ds0b, ds1b = ds0.astype(BF16), ds1.astype(BF16)
                pcat = jnp.concatenate([p0.astype(BF16), p1.astype(BF16)], axis=0)
                dscat = jnp.concatenate([ds0b, ds1b], axis=0)
                dv_ref[krows, :] += lax.dot_general(pcat, docat, _T0, preferred_element_type=F32)
                dk_ref[krows, :] += lax.dot_general(dscat, qcat, _T0, preferred_element_type=F32)
                dsrow = jnp.concatenate([ds0b, ds1b], axis=1)
                dq = dq + jnp.dot(dsrow, _stack_heads(k_ref[krows, :] * QK_SCALE, masks), preferred_element_type=F32)
            return dq

        def gbody(ng, c):
            ctxs, rows = [], []
            for a in range(g):
                r = pl.ds(pl.multiple_of((g * ng + a) * BLK, BLK), BLK)
                qcat = _stack_heads(q_ref[r, :] * QK_SCALE, masks)
                dob = do_ref[r, :].astype(BF16)
                prod = dob.astype(F32) * o_ref[r, :]
                z = jnp.zeros_like(prod)
                dl0 = jnp.sum(jnp.where(masks[0], prod, z), axis=1, keepdims=True)
                dl1 = jnp.sum(jnp.where(masks[1], prod, z), axis=1, keepdims=True)
                lseb = lse_ref[r, :]
                ctxs.append((qcat, _stack_heads(dob, masks), lseb[:, 0:1], lseb[:, hd:hd + 1], dl0, dl1))
                rows.append(r)
            first = [products(ctxs[a][0], ctxs[a][1], 0) for a in range(g)]
            done = []
            for a in range(g):
                def step(t, cc, ctx=ctxs[a]):
                    pr, dq = cc
                    nxt = products(ctx[0], ctx[1], t + 1)
                    return nxt, consume(pr, t, ctx, dq, g, False)

                done.append(lax.fori_loop(0, ng, step, (first[a], jnp.zeros((BLK, LANES), F32))))
            for a in range(g):
                pr, dq = done[a]
                dq_ref[rows[a], :] = consume(pr, ng, ctxs[a], dq, a + 1, True).astype(dq_ref.dtype)
            return c

        lax.fori_loop(0, NBLK // g, gbody, 0)
        dko_ref[...] = dk_ref[...].astype(dko_ref.dtype)
        dvo_ref[...] = dv_ref[...].astype(dvo_ref.dtype)

    cols, ospec, crspec = _fox_specs()
    dospec = pl.BlockSpec((None, SEQ, LANES), lambda b, j: (b, 0, do_off + j))
    osd = jax.ShapeDtypeStruct((nb, SEQ, FOX_W), BF16)
    return pl.pallas_call(
        body, out_shape=(osd, osd, osd, jax.ShapeDtypeStruct((nb, FOX_W // LANES, NBLK, 8, BLK), F32)),
        grid=(nb, FOX_W // LANES), in_specs=cols + [ospec, dospec, ospec, ospec], out_specs=(ospec, ospec, ospec, crspec),
        scratch_shapes=[pltpu.VMEM((SEQ, LANES), F32)] * 2,
        compiler_params=_cparams(dimension_semantics=("parallel", "parallel")), name=name,
    )(p3, p3, p3, cext, do, o, lse)


def _mem_cfg():
    return _AttnCfg(e=MEM_HEAD_DIM, tq=256, tk=MEM_LEN, lq=SEQ, lk=MEM_LEN, causal=False, window=None, ncol=MEM_HEADS,
                    qcol=lambda j: C_MQ // LANES + j, kcol=lambda j: j, vcol=lambda j: MEM_HEADS + j)


_B1, _B2 = FOX_W // LANES, (FOX_W + DIL_W) // LANES


def _dy_gate_bwd(dx2b, wo, fox, dil, memo, p16, *, tm, tn, name):
    t, d = dx2b.shape
    assert FOX_W % tn == 0 and DIL_W % tn == 0 and MEM_W % tn == 0 and all(c % tn == 0 for c in (C_FG, C_DG, C_MG))
    n1, n2, n3 = FOX_W // tn, (FOX_W + DIL_W) // tn, MIX_W // tn

    def body(dx_ref, w_ref, f_ref, d_ref, m_ref, g_ref, da_ref, dg_ref):
        j = pl.program_id(1)
        dyv = lax.dot_general(dx_ref[...], w_ref[...], _NT, preferred_element_type=F32)
        a = jnp.where(j < n1, f_ref[...], jnp.where(j < n2, d_ref[...], m_ref[...]))
        gt = g_ref[...].astype(F32)
        sg = 1.0 / (1.0 + jnp.exp(-gt))
        da_ref[...] = (dyv * gt * sg).astype(da_ref.dtype)
        dg_ref[...] = (dyv * a * sg * (1.0 + gt * (1.0 - sg))).astype(dg_ref.dtype)

    def gcol(j):
        return jnp.where(j < n1, C_FG // tn + j, jnp.where(j < n2, C_DG // tn + j - n1, C_MG // tn + j - n2))

    tile = pl.BlockSpec((tm, tn), lambda i, j: (i, j))
    return pl.pallas_call(
        body,
        out_shape=(jax.ShapeDtypeStruct((t, MIX_W), BF16), jax.ShapeDtypeStruct((t, MIX_W), BF16)),
        grid=(t // tm, n3),
        in_specs=[pl.BlockSpec((tm, d), lambda i, j: (i, 0)), pl.BlockSpec((tn, d), lambda i, j: (j, 0)),
                  pl.BlockSpec((tm, tn), lambda i, j: (i, jnp.minimum(j, n1 - 1))),
                  pl.BlockSpec((tm, tn), lambda i, j: (i, jnp.clip(j - n1, 0, n2 - n1 - 1))),
                  pl.BlockSpec((tm, tn), lambda i, j: (i, jnp.clip(j - n2, 0, n3 - n2 - 1))),
                  pl.BlockSpec((tm, tn), lambda i, j: (i, gcol(j)))],
        out_specs=(tile, tile),
        compiler_params=_cparams(dimension_semantics=("parallel", "parallel")),
        name=name,
    )(dx2b, wo, fox, dil, memo, p16)


def _silu(g):
    return g / (1.0 + jnp.exp(-g))


def _out_loss(fox, dil, memo, p16, wo, x, tgt, gfin, *, tm, name):
    t, d = x.shape
    n_feat = float(d)

    def body(f_ref, d_ref, m_ref, fg_ref, dg_ref, mg_ref, w_ref, x_ref, t_ref, g_ref, y_ref, dx_ref, dxb_ref, st_ref):
        i = pl.program_id(0)

        @pl.when(i == 0)
        def _():
            st_ref[...] = jnp.zeros_like(st_ref)

        y = jnp.concatenate([(a_ref[...] * _silu(gt_ref[...].astype(F32))).astype(BF16)
                             for a_ref, gt_ref in ((f_ref, fg_ref), (d_ref, dg_ref), (m_ref, mg_ref))], axis=1)
        y_ref[...] = y
        x2 = x_ref[...] + jnp.dot(y, w_ref[...], preferred_element_type=F32)
        r = lax.rsqrt(jnp.mean(x2 * x2, axis=-1, keepdims=True) + RMS_EPS)
        nrm = x2 * r
        gv = g_ref[...]
        err = nrm * gv - t_ref[...]
        dout = err * (1.0 / n_feat)
        dn = dout * gv
        dx2 = r * (dn - nrm * jnp.mean(dn * nrm, axis=-1, keepdims=True))
        dx_ref[...] = dx2
        dxb_ref[...] = dx2.astype(dxb_ref.dtype)
        st_ref[0:1, :] += jnp.sum(dout * nrm, axis=0, keepdims=True)
        st_ref[1:2, :] += (0.5 / n_feat) * jnp.sum(err * err, axis=0, keepdims=True)

    row = pl.BlockSpec((tm, d), lambda i: (i, 0))
    whole = lambda w: pl.BlockSpec((tm, w), lambda i: (i, 0))
    gate = lambda w, col: pl.BlockSpec((tm, w), lambda i: (i, col // w))
    return pl.pallas_call(
        body,
        out_shape=(jax.ShapeDtypeStruct((t, MIX_W), BF16), jax.ShapeDtypeStruct((t, d), F32), jax.ShapeDtypeStruct((t, d), BF16),
                   jax.ShapeDtypeStruct((8, d), F32)),
        grid=(t // tm,),
        in_specs=[whole(FOX_W), whole(DIL_W), whole(MEM_W), gate(FOX_W, C_FG), gate(DIL_W, C_DG), gate(MEM_W, C_MG),
                  pl.BlockSpec((MIX_W, d), lambda i: (0, 0)), row, row, pl.BlockSpec((1, d), lambda i: (0, 0))],
        out_specs=(pl.BlockSpec((tm, MIX_W), lambda i: (i, 0)), row, row, pl.BlockSpec((8, d), lambda i: (0, 0))),
        compiler_params=_cparams(dimension_semantics=("arbitrary",)),
        name=name,
    )(fox, dil, memo, p16, p16, p16, wo, x, tgt, gfin)


def _dh_rms_bwd(dp, w, x, g, resid, *, tm, tk, name):
    t, d = x.shape
    kdim = dp.shape[1]
    nk = kdim // tk

    def body(*refs):
        if resid is not None:
            dp_ref, w_ref, x_ref, g_ref, r_ref, dx_ref, gg_ref, acc_ref = refs
        else:
            dp_ref, w_ref, x_ref, g_ref, dx_ref, gg_ref, acc_ref = refs
        i = pl.program_id(0)
        k = pl.program_id(1)

        @pl.when(jnp.logical_and(i == 0, k == 0))
        def _():
            gg_ref[...] = jnp.zeros_like(gg_ref)

        prod = lax.dot_general(dp_ref[...], w_ref[...], _NT, preferred_element_type=F32)

        @pl.when(k == 0)
        def _():
            acc_ref[...] = prod

        @pl.when(k > 0)
        def _():
            acc_ref[...] += prod

        @pl.when(k == nk - 1)
        def _():
            dh = acc_ref[...]
            xv = x_ref[...]
            r = lax.rsqrt(jnp.mean(xv * xv, axis=-1, keepdims=True) + RMS_EPS)
            nrm = xv * r
            dn = dh * g_ref[...]
            dx = r * (dn - nrm * jnp.mean(dn * nrm, axis=-1, keepdims=True))
            if resid is not None:
                dx = dx + r_ref[...]
            dx_ref[...] = dx
            gg_ref[0:1, :] += jnp.sum(dh * nrm, axis=0, keepdims=True)

    row = pl.BlockSpec((tm, d), lambda i, k: (i, 0))
    in_specs = [pl.BlockSpec((tm, tk), lambda i, k: (i, k)), pl.BlockSpec((d, tk), lambda i, k: (0, k)), row,
                pl.BlockSpec((1, d), lambda i, k: (0, 0))]
    args = [dp, w, x, g]
    if resid is not None:
        in_specs.append(row)
        args.append(resid)
    return pl.pallas_call(
        body,
        out_shape=(jax.ShapeDtypeStruct((t, d), F32), jax.ShapeDtypeStruct((8, d), F32)),
        grid=(t // tm, nk),
        in_specs=in_specs,
        out_specs=(row, pl.BlockSpec((8, d), lambda i, k: (0, 0))),
        scratch_shapes=[pltpu.VMEM((tm, d), F32)],
        compiler_params=_cparams(dimension_semantics=("arbitrary", "arbitrary")),
        name=name,
    )(*args)


_FLOG0 = 4 * FOX_W
_W_IN_SEGMENTS = ((0, _FLOG0, 0), (_FLOG0, _FLOG0 + FOX_HEADS, PW), (_FLOG0 + FOX_HEADS, IN_W, C_DQ))
SHARD_W = IN_W // N_CHIPS


def _rearrange_w_in(shards):
    def cols(lo, hi):
        parts = []
        for k in range(N_CHIPS):
            a, b = max(lo, k * SHARD_W), min(hi, (k + 1) * SHARD_W)
            if a < b:
                parts.append(shards[k][:, a - k * SHARD_W:b - k * SHARD_W])
        return parts

    (a0, a1, _), (f0, f1, _), (b0, b1, _) = _W_IN_SEGMENTS
    pad = jnp.zeros((shards[0].shape[0], PWF - PW - FOX_HEADS), shards[0].dtype)
    return jnp.concatenate(cols(a0, a1) + cols(b0, b1) + cols(f0, f1) + [pad], axis=1)


def _w_in_grad_slabs(g):
    slabs = []
    for k in range(N_CHIPS):
        parts = []
        for lo, hi, at in _W_IN_SEGMENTS:
            a, b = max(lo, k * SHARD_W), min(hi, (k + 1) * SHARD_W)
            if a < b:
                parts.append(g[:, at + a - lo:at + b - lo])
        slabs.append(jnp.concatenate(parts, axis=1))
    return jnp.stack(slabs, axis=0)


def _local_grads(x, mem, norm_g, w_r, b_forget, mem_norm_g, w_kv, w_o, final_norm_g, tgt, start_reduce=None,
                 early_token=None, late_weights=None):
    nb = x.shape[0]
    t = nb * SEQ
    x2d = x.reshape(t, D_MODEL)
    tgt2d = tgt.reshape(t, D_MODEL)
    tabs = _rope_tables()
    bpad = jnp.pad(b_forget.reshape(1, FOX_HEADS), ((0, 0), (0, LANES - FOX_HEADS)))

    gain0 = norm_g.reshape(1, D_MODEL)
    if early_token is not None:
        gain0 = gain0 + early_token[0:1, 0:1]
    h = _rms_fwd(x2d, gain0, tm=512, name="rms_x")
    p16, dqkv = _proj(h, w_r, tabs, n=PW, tm=2048, tn=256, name="proj")
    flog = _matmul(h, w_r[:, PW:PW + LANES], out_dtype=F32, tm=1024, tn=LANES, tk=D_MODEL, name="proj_flog")
    cext = _flog_fwd(flog, bpad, nb=nb, ts=256, name="flog_fwd").reshape(nb, SEQ, FOX_W)
    p3 = p16.reshape(nb, SEQ, PW)
    fox, fox_lse = _fox_fwd(p3, cext, name="fox_fwd")
    if late_weights is not None:
        w_kv, w_o = late_weights(fox_lse)

    dqkv3 = dqkv.reshape(nb, SEQ, 3 * DIL_W)
    dil, dil_lse = _dil_fwd(dqkv3, name="dil_fwd")

    mh = _rms_fwd(mem.reshape(nb * MEM_LEN, D_MODEL), mem_norm_g.reshape(1, D_MODEL), tm=nb * MEM_LEN, name="rms_mem")
    mkv = _matmul(mh, w_kv, out_dtype=BF16, tm=nb * MEM_LEN, tn=512, tk=D_MODEL, name="mem_kv")
    mkv3 = mkv.reshape(nb, MEM_LEN, 2 * MEM_W)
    mcfg = _mem_cfg()
    memo, mem_lse = _attn_fwd(mcfg, p3, mkv3, mkv3, out_cols=MEM_W, name="mem_fwd")

    fox2, dil2, memo2 = fox.reshape(t, FOX_W), dil.reshape(t, DIL_W), memo.reshape(t, MEM_W)
    y, dx2, dx2b, st = _out_loss(fox2, dil2, memo2, p16, w_o, x2d, tgt2d, final_norm_g.reshape(1, D_MODEL), tm=256,
                                 name="out_loss")

    g_wo = _matmul(y, dx2b, mode="tn", out_dtype=F32, tm=1024, tn=512, tk=1024, name="grad_w_out")
    datt, dgate = _dy_gate_bwd(dx2b, w_o, fox2, dil2, memo2, p16, tm=1024, tn=256, name="dy_gate_bwd")
    datt3 = datt.reshape(nb, SEQ, MIX_W)

    dfq, dfk, dfv, dcr = _fox_bwd(p3, cext, datt3, fox, fox_lse, do_off=0, name="fox_bwd")
    dcol = -dcr[:, :, :, :2, :].transpose(0, 2, 4, 1, 3).reshape(t, FOX_HEADS)
    dcol = jnp.pad(dcol, ((0, 0), (0, LANES - FOX_HEADS)))
    dflog, gb = _flog_bwd(dcol, flog, bpad, nb=nb, ts=256, name="flog_bwd")

    ddq, ddk, ddv = _dil_bwd(dqkv3, datt3, dil, dil_lse, tabs, do_off=_B1, name="dil_bwd")

    dmq, dmk, dmv = _attn_bwd(mcfg, p3, mkv3, mkv3, datt3, memo, mem_lse, out_cols=MEM_W, kv_cols=MEM_W, do_off=_B2,
                              name="mem_bwd")
    dmkv = jnp.concatenate([dmk, dmv], axis=-1).reshape(nb * MEM_LEN, 2 * MEM_W).astype(BF16)
    g_wkv = _matmul(mh, dmkv, mode="tn", out_dtype=F32, tm=512, tn=512, tk=nb * MEM_LEN, name="grad_w_kv")
    _, gmn = _dh_rms_bwd(dmkv, w_kv, mem.reshape(nb * MEM_LEN, D_MODEL), mem_norm_g.reshape(1, D_MODEL), None,
                         tm=nb * MEM_LEN, tk=2 * MEM_W, name="mem_rms_bwd")

    flat = lambda a: a.reshape(t, -1)
    dp = jnp.concatenate([flat(dfq), flat(dfk), flat(dfv), dgate[:, :FOX_W], flat(ddq), flat(ddk), flat(ddv),
                          dgate[:, FOX_W:FOX_W + DIL_W], flat(dmq).astype(BF16), dgate[:, FOX_W + DIL_W:], dflog,
                          jnp.zeros((t, PWF - PW - LANES), BF16)], axis=1)
    g_wr = _matmul(h, dp, mode="tn", out_dtype=F32, tm=D_MODEL, tn=512, tk=t, name="grad_w_in")
    gain = norm_g.reshape(1, D_MODEL)
    if start_reduce is not None:
        gain = gain + start_reduce(g_wr, g_wkv, g_wo)[0:1, 0:1]
    gx, gng = _dh_rms_bwd(dp, w_r, x2d, gain, dx2, tm=512, tk=PWF // 3, name="in_rms_bwd")

    gb_row = jnp.pad(gb[0:1, :], ((0, 0), (0, D_MODEL - LANES)))
    small = jnp.concatenate([gng[0:1], gmn[0:1], st[0:1], gb_row, st[1:2], jnp.zeros((3, D_MODEL), F32)], axis=0)
    return gx.reshape(nb, SEQ, D_MODEL), g_wr, g_wkv, g_wo, small


MESH = pl.DeviceIdType.MESH
ANY = pl.BlockSpec(memory_space=pl.ANY)


def _place():
    x, y, c = lax.axis_index("x"), lax.axis_index("y"), lax.axis_index("c")
    other_chips = [(1 - x, y), (x, 1 - y), (1 - x, 1 - y)]
    return x, y, c, other_chips


def _gather_weights(shards):
    n = len(shards)

    def body(*refs):
        in_refs, out_refs = refs[:n], refs[n:2 * n]
        send_sems, recv_sems = refs[2 * n:]
        x, y, c, chips = _place()
        me_chip = 2 * x + y
        sibling = (x, y, 1 - c)

        def half(ref, pc, rows):
            return ref.at[pl.ds(pc * (rows // 2), rows // 2), :]

        def rcopy(k, src, dst, to):
            return pltpu.make_async_remote_copy(src_ref=src, dst_ref=dst, send_sem=send_sems.at[k], recv_sem=recv_sems.at[k],
                                                device_id=to, device_id_type=MESH)

        sends = []
        for t in range(n):
            rows = shards[t].shape[0]
            for j, chip in enumerate(chips):
                cp = rcopy(6 * t + j, half(in_refs[t], c, rows), half(out_refs[t].at[me_chip], c, rows), (*chip, c))
                cp.start()
                sends.append(cp)
        for t in range(n):
            rows = shards[t].shape[0]
            for j, chip in enumerate(chips):
                slot = out_refs[t].at[2 * chip[0] + chip[1]]
                rcopy(6 * t + j, half(slot, c, rows), half(slot, c, rows), sibling).wait_recv()
                fw = rcopy(6 * t + 3 + j, half(slot, c, rows), half(slot, c, rows), sibling)
                fw.start()
                sends.append(fw)
        for t in range(n):
            rows = shards[t].shape[0]
            for j, chip in enumerate(chips):
                slot = out_refs[t].at[2 * chip[0] + chip[1]]
                rcopy(6 * t + 3 + j, half(slot, 1 - c, rows), half(slot, 1 - c, rows), sibling).wait_recv()
        for cp in sends:
            cp.wait_send()

    return pl.pallas_call(
        body,
        out_shape=tuple(jax.ShapeDtypeStruct((N_CHIPS,) + s.shape, s.dtype) for s in shards),
        in_specs=[ANY] * n,
        out_specs=tuple([ANY] * n),
        scratch_shapes=[pltpu.SemaphoreType.DMA((6 * n,)), pltpu.SemaphoreType.DMA((6 * n,))],
        name="gather_weights",
    )(*shards)


def _pair_exchange(gs):
    n = len(gs)

    def body(*refs):
        g_refs, r_refs = refs[:n], refs[n:2 * n]
        send_sems, recv_sems = refs[2 * n:]
        x, y, c, _ = _place()
        cps = []
        for t in range(n):
            hr = gs[t].shape[1] // 2
            cp = pltpu.make_async_remote_copy(src_ref=g_refs[t].at[:, pl.ds((1 - c) * hr, hr), :], dst_ref=r_refs[t],
                                              send_sem=send_sems.at[t], recv_sem=recv_sems.at[t],
                                              device_id=(x, y, 1 - c), device_id_type=MESH)
            cp.start()
            cps.append(cp)
        for cp in cps:
            cp.wait()

    return pl.pallas_call(
        body,
        out_shape=tuple(jax.ShapeDtypeStruct((N_CHIPS, g.shape[1] // 2, g.shape[2]), g.dtype) for g in gs),
        in_specs=[ANY] * n,
        out_specs=tuple([ANY] * n),
        scratch_shapes=[pltpu.SemaphoreType.DMA((n,)), pltpu.SemaphoreType.DMA((n,))],
        name="pair_exchange",
    )(*gs)


def _chip_exchange(ps):
    n = len(ps)

    def body(*refs):
        p_refs, o_refs = refs[:n], refs[n:2 * n]
        send_sems, recv_sems = refs[2 * n:]
        x, y, c, chips = _place()
        me_chip = 2 * x + y
        cps = []
        for t in range(n):
            for j, chip in enumerate(chips):
                cp = pltpu.make_async_remote_copy(src_ref=p_refs[t].at[2 * chip[0] + chip[1]], dst_ref=o_refs[t].at[me_chip],
                                                  send_sem=send_sems.at[3 * t + j], recv_sem=recv_sems.at[3 * t + j],
                                                  device_id=(*chip, c), device_id_type=MESH)
                cp.start()
                cps.append(cp)
        for cp in cps:
            cp.wait()

    return pl.pallas_call(
        body,
        out_shape=tuple(jax.ShapeDtypeStruct(p.shape, p.dtype) for p in ps),
        in_specs=[ANY] * n,
        out_specs=tuple([ANY] * n),
        scratch_shapes=[pltpu.SemaphoreType.DMA((3 * n,)), pltpu.SemaphoreType.DMA((3 * n,))],
        name="chip_exchange",
    )(*ps)


_HBM = pl.BlockSpec(memory_space=pltpu.HBM)
_SEM = pl.BlockSpec(memory_space=pltpu.SEMAPHORE)
_DATAFLOW = pltpu.SideEffectType.DATAFLOW_SIDE_EFFECTING


def _chip_copies(p_refs, land_refs, send_sems, recv_sems):
    x, y, c, chips = _place()
    me_chip = 2 * x + y
    return [pltpu.make_async_remote_copy(src_ref=p_refs[t].at[2 * chip[0] + chip[1]], dst_ref=land_refs[t].at[me_chip],
                                         send_sem=send_sems.at[3 * t + j], recv_sem=recv_sems.at[3 * t + j],
                                         device_id=(*chip, c), device_id_type=MESH)
            for t in range(len(p_refs)) for j, chip in enumerate(chips)]


def _chip_exchange_start(ps):
    n = len(ps)

    def body(*refs):
        p_refs, land_refs = refs[:n], refs[n:2 * n]
        send_sems, recv_sems = refs[2 * n:2 * n + 2]
        token = refs[-1]
        for cp in _chip_copies(p_refs, land_refs, send_sems, recv_sems):
            cp.start()
        token[...] = jnp.zeros_like(token)

    hbm = [pltpu.HBM(p.shape, p.dtype) for p in ps]
    args = [pltpu.with_memory_space_constraint(p, pltpu.HBM) for p in ps]
    args += [pltpu.with_memory_space_constraint(lax.empty(p.shape, p.dtype), pltpu.HBM) for p in ps]
    out = pl.pallas_call(
        body,
        name="chip_exchange_start",
        out_shape=(pltpu.SemaphoreType.DMA((3 * n,)), pltpu.SemaphoreType.DMA((3 * n,)), *hbm, *hbm,
                   jax.ShapeDtypeStruct((8, LANES), F32)),
        in_specs=[_HBM] * (2 * n),
        out_specs=(_SEM, _SEM, *([_HBM] * (2 * n)), pl.BlockSpec(memory_space=pltpu.VMEM)),
        input_output_aliases={i: 2 + i for i in range(2 * n)},
        compiler_params=pltpu.CompilerParams(has_side_effects=_DATAFLOW),
    )(*args)
    return out[0], out[1], out[2:2 + n], out[2 + n:2 + 2 * n], out[-1]


def _chip_exchange_wait(send_sems, recv_sems, p_thru, land_thru, after):
    n = len(p_thru)

    def body(*refs):
        p_refs, land_refs = refs[:n], refs[n:2 * n]
        ssem, rsem = refs[2 * n:2 * n + 2]
        for cp in _chip_copies(p_refs, land_refs, ssem, rsem):
            cp.wait_send()
            cp.wait_recv()

    hbm = [pltpu.HBM(p.shape, p.dtype) for p in p_thru]
    out = pl.pallas_call(
        body,
        name="chip_exchange_wait",
        out_shape=(*hbm, *hbm),
        in_specs=[_HBM] * (2 * n) + [_SEM, _SEM, ANY],
        out_specs=tuple([_HBM] * (2 * n)),
        input_output_aliases={i: i for i in range(2 * n)},
        compiler_params=pltpu.CompilerParams(has_side_effects=_DATAFLOW),
    )(*p_thru, *land_thru, send_sems, recv_sems, after)
    return out[:n], out[n:]


def _shard_copies(s_refs, land_refs, send_sems, recv_sems):
    x, y, c, chips = _place()
    me_chip = 2 * x + y
    return [pltpu.make_async_remote_copy(src_ref=s_refs[t], dst_ref=land_refs[t].at[me_chip],
                                         send_sem=send_sems.at[3 * t + j], recv_sem=recv_sems.at[3 * t + j],
                                         device_id=(*chip, c), device_id_type=MESH)
            for t in range(len(s_refs)) for j, chip in enumerate(chips)]


def _gather_late_start(shards):
    n = len(shards)

    def body(*refs):
        s_refs, land_refs = refs[:n], refs[n:2 * n]
        send_sems, recv_sems = refs[2 * n:2 * n + 2]
        token = refs[-1]
        for cp in _shard_copies(s_refs, land_refs, send_sems, recv_sems):
            cp.start()
        token[...] = jnp.zeros_like(token)

    lands = [(N_CHIPS,) + s.shape for s in shards]
    args = [pltpu.with_memory_space_constraint(s, pltpu.HBM) for s in shards]
    args += [pltpu.with_memory_space_constraint(lax.empty(shp, s.dtype), pltpu.HBM) for shp, s in zip(lands, shards)]
    out = pl.pallas_call(
        body,
        name="gather_late_start",
        out_shape=(pltpu.SemaphoreType.DMA((3 * n,)), pltpu.SemaphoreType.DMA((3 * n,)),
                   *[pltpu.HBM(s.shape, s.dtype) for s in shards], *[pltpu.HBM(shp, s.dtype) for shp, s in zip(lands, shards)],
                   jax.ShapeDtypeStruct((8, LANES), F32)),
        in_specs=[_HBM] * (2 * n),
        out_specs=(_SEM, _SEM, *([_HBM] * (2 * n)), pl.BlockSpec(memory_space=pltpu.VMEM)),
        input_output_aliases={i: 2 + i for i in range(2 * n)},
        compiler_params=pltpu.CompilerParams(has_side_effects=_DATAFLOW),
    )(*args)
    return out[0], out[1], out[2:2 + n], out[2 + n:2 + 2 * n], out[-1]


def _gather_late_wait(send_sems, recv_sems, s_thru, land_thru, after):
    n = len(s_thru)

    def body(*refs):
        s_refs, land_refs = refs[:n], refs[n:2 * n]
        ssem, rsem = refs[2 * n:2 * n + 2]
        for cp in _shard_copies(s_refs, land_refs, ssem, rsem):
            cp.wait_send()
            cp.wait_recv()

    out = pl.pallas_call(
        body,
        name="gather_late_wait",
        out_shape=(*[pltpu.HBM(s.shape, s.dtype) for s in s_thru], *[pltpu.HBM(l.shape, l.dtype) for l in land_thru]),
        in_specs=[_HBM] * (2 * n) + [_SEM, _SEM, ANY],
        out_specs=tuple([_HBM] * (2 * n)),
        input_output_aliases={i: i for i in range(2 * n)},
        compiler_params=pltpu.CompilerParams(has_side_effects=_DATAFLOW),
    )(*s_thru, *land_thru, send_sems, recv_sems, after)
    return out[:n], out[n:]


def _pair_swap(rs):
    n = len(rs)

    def body(*refs):
        r_refs, o_refs = refs[:n], refs[n:2 * n]
        send_sems, recv_sems = refs[2 * n:]
        x, y, c, _ = _place()
        cps = []
        for t in range(n):
            cp = pltpu.make_async_remote_copy(src_ref=r_refs[t], dst_ref=o_refs[t], send_sem=send_sems.at[t],
                                              recv_sem=recv_sems.at[t], device_id=(x, y, 1 - c), device_id_type=MESH)
            cp.start()
            cps.append(cp)
        for cp in cps:
            cp.wait()

    return pl.pallas_call(
        body,
        out_shape=tuple(jax.ShapeDtypeStruct(r.shape, r.dtype) for r in rs),
        in_specs=[ANY] * n,
        out_specs=tuple([ANY] * n),
        scratch_shapes=[pltpu.SemaphoreType.DMA((n,)), pltpu.SemaphoreType.DMA((n,))],
        name="pair_swap",
    )(*rs)


N_DEV = 8
LOSS_ROW = 4


def _small_allreduce(small):
    def body(s_ref, o_ref, all_ref, send_sems, recv_sems):
        x, y, c, _ = _place()
        me = 4 * x + 2 * y + c
        all_ref[me] = s_ref[...]
        cps = []
        for k in range(1, N_DEV):
            peer = tuple(1 - p if (k >> s) & 1 else p for p, s in ((x, 2), (y, 1), (c, 0)))
            cp = pltpu.make_async_remote_copy(src_ref=s_ref, dst_ref=all_ref.at[me], send_sem=send_sems.at[k - 1],
                                              recv_sem=recv_sems.at[k - 1], device_id=peer, device_id_type=MESH)
            cp.start()
            cps.append(cp)
        for cp in cps:
            cp.wait()
        tot = all_ref[0]
        for d in range(1, N_DEV):
            tot = tot + all_ref[d]
        o_ref[...] = tot
        o_ref[LOSS_ROW:LOSS_ROW + 1, :] = jnp.broadcast_to(jnp.sum(tot[LOSS_ROW:LOSS_ROW + 1, :], axis=1, keepdims=True),
                                                          (1, tot.shape[1]))

    vm = pl.BlockSpec(memory_space=pltpu.VMEM)
    return pl.pallas_call(
        body,
        out_shape=jax.ShapeDtypeStruct(small.shape, small.dtype),
        in_specs=[vm],
        out_specs=vm,
        scratch_shapes=[pltpu.VMEM((N_DEV,) + small.shape, small.dtype), pltpu.SemaphoreType.DMA((N_DEV - 1,)),
                        pltpu.SemaphoreType.DMA((N_DEV - 1,))],
        name="small_allreduce",
    )(small)


def _sum_pair(g, recv, cidx, *, tr, name):
    _, hr, cols = recv.shape
    nr = hr // tr

    def body(c_ref, g_ref, r_ref, o_ref):
        o_ref[...] = (g_ref[...] + r_ref[...]).astype(o_ref.dtype)

    grid_spec = pltpu.PrefetchScalarGridSpec(
        num_scalar_prefetch=1,
        grid=(N_CHIPS, nr),
        in_specs=[pl.BlockSpec((None, tr, cols), lambda k, i, c_ref: (k, c_ref[0] * nr + i, 0)),
                  pl.BlockSpec((None, tr, cols), lambda k, i, c_ref: (k, i, 0))],
        out_specs=pl.BlockSpec((None, tr, cols), lambda k, i, c_ref: (k, i, 0)),
    )
    return pl.pallas_call(body, out_shape=jax.ShapeDtypeStruct(recv.shape, BF16), grid_spec=grid_spec,
                          compiler_params=_cparams(), name=name)(cidx, g, recv)


def _sum_chips(p, *, tr, name):
    _, rows, cols = p.shape

    def body(p_ref, o_ref):
        tot = p_ref[0].astype(F32)
        for k in range(1, N_CHIPS):
            tot = tot + p_ref[k].astype(F32)
        o_ref[...] = tot

    return pl.pallas_call(
        body,
        out_shape=jax.ShapeDtypeStruct((rows, cols), F32),
        grid=(rows // tr,),
        in_specs=[pl.BlockSpec((N_CHIPS, tr, cols), lambda i: (0, i, 0))],
        out_specs=pl.BlockSpec((tr, cols), lambda i: (i, 0)),
        compiler_params=_cparams(),
        name=name,
    )(p)


def _adamw(w, g, m, v, *, tr, name):
    rows, cols = w.shape
    bc1 = 1.0 / (1.0 - ADAM_B1 ** ADAM_STEP)
    bc2 = 1.0 / (1.0 - ADAM_B2 ** ADAM_STEP)

    def body(w_ref, g_ref, m_ref, v_ref, d_ref, nm_ref, nv_ref):
        gv = g_ref[...]
        nm = ADAM_B1 * m_ref[...] + (1.0 - ADAM_B1) * gv
        nv = ADAM_B2 * v_ref[...] + (1.0 - ADAM_B2) * (gv * gv)
        d_ref[...] = -ADAM_LR * ((nm * bc1) / (jnp.sqrt(nv * bc2) + ADAM_EPS) + ADAM_WD * w_ref[...])
        nm_ref[...] = nm
        nv_ref[...] = nv

    spec = pl.BlockSpec((tr, cols), lambda i: (i, 0))
    sd = jax.ShapeDtypeStruct((rows, cols), F32)
    return pl.pallas_call(body, out_shape=(sd, sd, sd), grid=(rows // tr,), in_specs=[spec] * 4, out_specs=(spec,) * 3,
                          compiler_params=_cparams(), name=name)(w, g, m, v)


def _adamw_halves(w, own, sib, cidx, m, v, *, tr, name):
    rows, cols = w.shape
    hr = own.shape[0]
    nr = hr // tr
    assert rows == 2 * hr and hr % tr == 0
    bc1 = 1.0 / (1.0 - ADAM_B1 ** ADAM_STEP)
    bc2 = 1.0 / (1.0 - ADAM_B2 ** ADAM_STEP)

    def body(c_ref, w_ref, o_ref, s_ref, m_ref, v_ref, g_ref, d_ref, nm_ref, nv_ref):
        mine = (pl.program_id(0) // nr) == c_ref[0]
        gv = jnp.where(mine, o_ref[...], s_ref[...])
        nm = ADAM_B1 * m_ref[...] + (1.0 - ADAM_B1) * gv
        nv = ADAM_B2 * v_ref[...] + (1.0 - ADAM_B2) * (gv * gv)
        g_ref[...] = gv
        d_ref[...] = -ADAM_LR * ((nm * bc1) / (jnp.sqrt(nv * bc2) + ADAM_EPS) + ADAM_WD * w_ref[...])
        nm_ref[...] = nm
        nv_ref[...] = nv

    full = pl.BlockSpec((tr, cols), lambda i, c_ref: (i, 0))
    half = pl.BlockSpec((tr, cols), lambda i, c_ref: (i % nr, 0))
    sd = jax.ShapeDtypeStruct((rows, cols), F32)
    grid_spec = pltpu.PrefetchScalarGridSpec(num_scalar_prefetch=1, grid=(rows // tr,), in_specs=[full, half, half, full, full],
                                             out_specs=(full,) * 4)
    return pl.pallas_call(body, out_shape=(sd,) * 4, grid_spec=grid_spec, compiler_params=_cparams(), name=name)(
        cidx, w, own, sib, m, v)


def _pack_small(norm, mem_norm, final_norm, b_forget):
    rows = [norm.reshape(1, D_MODEL), mem_norm.reshape(1, D_MODEL), final_norm.reshape(1, D_MODEL),
            jnp.pad(b_forget.reshape(1, FOX_HEADS), ((0, 0), (0, D_MODEL - FOX_HEADS))), jnp.zeros((4, D_MODEL), F32)]
    return jnp.concatenate(rows, axis=0)


def _unpack_small(a):
    return a[0:1], a[3:4, :FOX_HEADS], a[1:2], a[2]


def kernel(x, mem, norm_g, w_in, b_forget, mem_norm_g, w_mem_kv, w_out, final_norm_g, loss_target, m_norm_g, m_w_in, m_b_forget, m_mem_norm_g, m_w_mem_kv, m_w_out, m_final_norm_g, v_norm_g, v_w_in, v_b_forget, v_mem_norm_g, v_w_mem_kv, v_w_out, v_final_norm_g):
    core = lax.axis_index("c").astype(jnp.int32)
    me_chip = (2 * lax.axis_index("x") + lax.axis_index("y")).astype(jnp.int32)
    cidx = core.reshape(1)

    def own_slot(arr, own):
        return lax.dynamic_update_slice(arr, own[None].astype(arr.dtype), (me_chip,) + (0,) * own.ndim)

    win_b, late = w_in[0].astype(BF16), [w_mem_kv[0].astype(BF16), w_out[0].astype(BF16)]
    g_in, = _gather_weights([win_b])
    g_in, late = lax.optimization_barrier((own_slot(g_in, win_b), late))
    w_r = _rearrange_w_in([g_in[k] for k in range(N_CHIPS)])
    *late_flight, early_token = _gather_late_start(late)

    def late_weights(after):
        shards, landed = _gather_late_wait(*late_flight, after)
        g_kv, g_out = (own_slot(g, s) for g, s in zip(landed, shards))
        return g_kv.reshape(D_MODEL, 2 * MEM_W), g_out.reshape(MIX_W, D_MODEL)

    trs = (128, 128, 256)
    names = ("w_in", "w_mem_kv", "w_out")
    flight = []

    def start_reduce(g_wr, g_wkv, g_wo):
        slabs = [_w_in_grad_slabs(g_wr),
                 g_wkv.reshape(N_CHIPS, D_MODEL // N_CHIPS, 2 * MEM_W),
                 g_wo.reshape(N_CHIPS, MIX_W // N_CHIPS, D_MODEL)]
        recv = _pair_exchange(slabs)
        pair = [_sum_pair(g, r, cidx, tr=tr, name=f"sum_pair_{nm}") for g, r, tr, nm in zip(slabs, recv, trs, names)]
        *handles, token = _chip_exchange_start(pair)
        flight.extend(handles)
        return token

    gx, g_wr, g_wkv, g_wo, small = _local_grads(x, mem, norm_g, w_r, b_forget, mem_norm_g, None, None, final_norm_g, loss_target,
                                                start_reduce=start_reduce, early_token=early_token, late_weights=late_weights)

    send_sems, recv_sems, pair, land = flight
    pair, landed = _chip_exchange_wait(send_sems, recv_sems, pair, land, small)
    got = [lax.dynamic_update_slice(g, lax.dynamic_slice(p, (me_chip, 0, 0), (1,) + p.shape[1:]), (me_chip, 0, 0))
           for g, p in zip(landed, pair)]
    red = [_sum_chips(p, tr=tr, name=f"sum_chips_{nm}") for p, tr, nm in zip(got, trs, names)]
    sib = _pair_swap(red)

    outs = {}
    for nm, r, s, w, m, v, tr in zip(names, red, sib, (w_in, w_mem_kv, w_out), (m_w_in, m_w_mem_kv, m_w_out),
                                     (v_w_in, v_w_mem_kv, v_w_out), trs):
        outs[nm] = tuple(a[None] for a in _adamw_halves(w[0], r, s, cidx, m[0], v[0], tr=tr, name=f"adamw_{nm}"))

    gsum = _small_allreduce(small)
    sd, sm, sv = _adamw(_pack_small(norm_g, mem_norm_g, final_norm_g, b_forget), gsum,
                        _pack_small(m_norm_g, m_mem_norm_g, m_final_norm_g, m_b_forget),
                        _pack_small(v_norm_g, v_mem_norm_g, v_final_norm_g, v_b_forget), tr=8, name="adamw_small")
    loss = gsum[LOSS_ROW, 0]

    def group(i, small_arr):
        ng, bf, mg, fg = _unpack_small(small_arr)
        return (ng, outs["w_in"][i], bf, mg, outs["w_mem_kv"][i], outs["w_out"][i], fg)

    return (loss, gx, *group(0, gsum), *group(1, sd), *group(2, sm), *group(3, sv))
```

```python
import functools
import math

import jax
import jax.numpy as jnp
from jax import lax
from jax.experimental import pallas as pl
from jax.experimental.pallas import tpu as pltpu

F32 = jnp.float32
BF16 = jnp.bfloat16

D_MODEL = 1024
SEQ = 2048
HEAD_DIM = 64
FOX_HEADS = 12
DIL_HEADS = 12
MEM_HEADS = 4
MEM_HEAD_DIM = 128
MEM_LEN = 256
FOX_W = FOX_HEADS * HEAD_DIM
DIL_W = DIL_HEADS * HEAD_DIM
MEM_W = MEM_HEADS * MEM_HEAD_DIM
MIX_W = FOX_W + DIL_W + MEM_W
DILATIONS = ((128, 1), (512, 4), (2048, 16))
ROPE_THETA = 500000.0
ROPE_DIM = HEAD_DIM // 4
RMS_EPS = 1e-6
NEG_INF = -1e30
IN_SIZES = [FOX_W] * 4 + [FOX_HEADS] + [DIL_W] * 4 + [MEM_W] * 2
IN_W = sum(IN_SIZES)

ADAM_LR = 0.001
ADAM_B1 = 0.9
ADAM_B2 = 0.999
ADAM_EPS = 1e-08
ADAM_WD = 0.01
ADAM_STEP = 10

LANES = 128
N_CHIPS = 4
PW = 7168
PWF = PW + 4 * LANES
C_FQ, C_FK, C_FV, C_FG = 0, 768, 1536, 2304
C_DQ, C_DK, C_DV, C_DG = 3072, 3840, 4608, 5376
C_MQ, C_MG = 6144, 6656
VMEM_LIMIT = 48 * 1024 * 1024


def _cparams(**kw):
    return pltpu.CompilerParams(vmem_limit_bytes=VMEM_LIMIT, **kw)


def _matmul(a, b, *, out_dtype, tm, tn, tk, name, mode="nn"):
    if mode == "tn":
        (kdim, m), n = a.shape, b.shape[1]
        a_spec = pl.BlockSpec((tk, tm), lambda i, j, k: (k, i))
        b_spec = pl.BlockSpec((tk, tn), lambda i, j, k: (k, j))
        dims = _T0
    elif mode == "nt":
        (m, kdim), n = a.shape, b.shape[0]
        a_spec = pl.BlockSpec((tm, tk), lambda i, j, k: (i, k))
        b_spec = pl.BlockSpec((tn, tk), lambda i, j, k: (j, k))
        dims = _NT
    else:
        (m, kdim), n = a.shape, b.shape[1]
        a_spec = pl.BlockSpec((tm, tk), lambda i, j, k: (i, k))
        b_spec = pl.BlockSpec((tk, tn), lambda i, j, k: (k, j))
        dims = (((1,), (0,)), ((), ()))
    nk = kdim // tk
    assert m % tm == 0 and n % tn == 0 and kdim % tk == 0

    def body(a_ref, b_ref, o_ref, *scratch):
        prod = lax.dot_general(a_ref[...], b_ref[...], dims, preferred_element_type=F32)
        if nk == 1:
            o_ref[...] = prod.astype(o_ref.dtype)
            return
        acc_ref, = scratch
        k = pl.program_id(2)

        @pl.when(k == 0)
        def _():
            acc_ref[...] = prod

        @pl.when(k > 0)
        def _():
            acc_ref[...] += prod

        @pl.when(k == nk - 1)
        def _():
            o_ref[...] = acc_ref[...].astype(o_ref.dtype)

    return pl.pallas_call(
        body,
        out_shape=jax.ShapeDtypeStruct((m, n), out_dtype),
        grid=(m // tm, n // tn, nk),
        in_specs=[a_spec, b_spec],
        out_specs=pl.BlockSpec((tm, tn), lambda i, j, k: (i, j)),
        scratch_shapes=[pltpu.VMEM((tm, tn), F32)] if nk > 1 else [],
        compiler_params=_cparams(dimension_semantics=("parallel", "parallel", "arbitrary")),
        name=name,
    )(a, b)


def _rms_fwd(x, g, *, tm, name):
    t, d = x.shape

    def body(x_ref, g_ref, h_ref):
        xv = x_ref[...]
        r = lax.rsqrt(jnp.mean(xv * xv, axis=-1, keepdims=True) + RMS_EPS)
        h_ref[...] = (xv * r * g_ref[...]).astype(h_ref.dtype)

    return pl.pallas_call(
        body,
        out_shape=jax.ShapeDtypeStruct((t, d), BF16),
        grid=(t // tm,),
        in_specs=[pl.BlockSpec((tm, d), lambda i: (i, 0)), pl.BlockSpec((1, d), lambda i: (0, 0))],
        out_specs=pl.BlockSpec((tm, d), lambda i: (i, 0)),
        compiler_params=_cparams(),
        name=name,
    )(x, g)


def _rope_tables():
    half = ROPE_DIM // 2
    pos = jnp.arange(SEQ, dtype=F32)
    inv_freq = 1.0 / (ROPE_THETA ** (jnp.arange(0, ROPE_DIM, 2, dtype=F32) / ROPE_DIM))
    ang = pos[:, None] * inv_freq[None, :]
    cos, sin = jnp.cos(ang), jnp.sin(ang)
    one = jnp.ones((SEQ, HEAD_DIM - ROPE_DIM), F32)
    zero = jnp.zeros((SEQ, HEAD_DIM - ROPE_DIM), F32)
    zh = jnp.zeros((SEQ, half), F32)
    c = jnp.concatenate([cos, cos, one], axis=1)
    s1 = jnp.concatenate([zh, sin, zero], axis=1)
    s2 = jnp.concatenate([-sin, zh, zero], axis=1)
    rep = LANES // HEAD_DIM
    return jnp.tile(c, (1, rep)), jnp.tile(s1, (1, rep)), jnp.tile(s2, (1, rep))


def _rope_apply(t, c, s1, s2, transpose=False):
    n = t.shape[-1]
    rep = n // LANES
    c, s1, s2 = (jnp.tile(u, (1, rep)) for u in (c, s1, s2))
    half = ROPE_DIM // 2
    if not transpose:
        return t * c + pltpu.roll(t, half, 1) * s1 + pltpu.roll(t, n - half, 1) * s2
    return t * c + pltpu.roll(t * s1, n - half, 1) + pltpu.roll(t * s2, half, 1)


def _proj(h, w, tabs, *, n, tm, tn, name):
    t, d = h.shape
    assert C_DQ % tn == 0 and (C_DV - C_DQ) % tn == 0 and (C_DG - C_DQ) % tn == 0
    rope_lo, rope_hi, dil_hi = C_DQ // tn, C_DV // tn, C_DG // tn
    s_blocks = SEQ // tm

    def body(h_ref, w_ref, c_ref, s1_ref, s2_ref, o_ref, f_ref):
        j = pl.program_id(1)
        acc = jnp.dot(h_ref[...], w_ref[...], preferred_element_type=F32)
        is_rope = jnp.logical_and(j >= rope_lo, j < rope_hi)

        @pl.when(is_rope)
        def _():
            r = _rope_apply(acc, c_ref[...], s1_ref[...], s2_ref[...])
            o_ref[...] = r.astype(o_ref.dtype)
            f_ref[...] = r

        @pl.when(jnp.logical_not(is_rope))
        def _():
            o_ref[...] = acc.astype(o_ref.dtype)

        @pl.when(jnp.logical_and(j >= rope_hi, j < dil_hi))
        def _():
            f_ref[...] = acc

    tab_spec = pl.BlockSpec((tm, LANES), lambda i, j: (i % s_blocks, 0))
    f_spec = pl.BlockSpec((tm, tn), lambda i, j: (i, jnp.clip(j - rope_lo, 0, dil_hi - rope_lo - 1)))
    return pl.pallas_call(
        body,
        out_shape=(jax.ShapeDtypeStruct((t, n), BF16), jax.ShapeDtypeStruct((t, 3 * DIL_W), F32)),
        grid=(t // tm, n // tn),
        in_specs=[pl.BlockSpec((tm, d), lambda i, j: (i, 0)), pl.BlockSpec((d, tn), lambda i, j: (0, j)),
                  tab_spec, tab_spec, tab_spec],
        out_specs=(pl.BlockSpec((tm, tn), lambda i, j: (i, j)), f_spec),
        compiler_params=_cparams(dimension_semantics=("parallel", "arbitrary")),
        name=name,
    )(h, w, *tabs)


def _split3(x):
    hi = x.astype(BF16)
    r1 = x - hi.astype(F32)
    mid = r1.astype(BF16)
    lo = (r1 - mid.astype(F32)).astype(BF16)
    return hi, mid, lo


def _dot3(sel, x, sel_is_lhs):
    out = None
    for piece in _split3(x):
        t = jnp.dot(sel, piece, preferred_element_type=F32) if sel_is_lhs else jnp.dot(piece, sel, preferred_element_type=F32)
        out = t if out is None else out + t
    return out


def _flog_fwd(flog, bpad, *, nb, ts, name):
    ns = SEQ // ts

    def body(f_ref, b_ref, c_ref, carry_ref):
        s = pl.program_id(1)

        @pl.when(s == 0)
        def _():
            carry_ref[...] = jnp.zeros_like(carry_ref)

        z = f_ref[...] + b_ref[...]
        logf = jnp.minimum(z, 0.0) - jnp.log(1.0 + jnp.exp(-jnp.abs(z)))
        r = lax.broadcasted_iota(jnp.int32, (ts, ts), 0)
        c = lax.broadcasted_iota(jnp.int32, (ts, ts), 1)
        tri = jnp.where(r >= c, 1.0, 0.0).astype(BF16)
        cs = _dot3(tri, logf, True) + carry_ref[0:1, :]
        carry_ref[...] = jnp.broadcast_to(cs[ts - 1:ts, :], carry_ref.shape)
        c_ref[...] = cs

    return pl.pallas_call(
        body,
        out_shape=jax.ShapeDtypeStruct((nb * SEQ, LANES), F32),
        grid=(nb, ns),
        in_specs=[pl.BlockSpec((ts, LANES), lambda b, s: (b * ns + s, 0)), pl.BlockSpec((1, LANES), lambda b, s: (0, 0))],
        out_specs=pl.BlockSpec((ts, LANES), lambda b, s: (b * ns + s, 0)),
        scratch_shapes=[pltpu.VMEM((8, LANES), F32)],
        compiler_params=_cparams(dimension_semantics=("parallel", "arbitrary")),
        name=name,
    )(flog, bpad)


def _flog_bwd(dcol, flog, bpad, *, nb, ts, name):
    ns = SEQ // ts

    def body(d_ref, f_ref, b_ref, o_ref, gb_ref, carry_ref):
        bi = pl.program_id(0)
        s = pl.program_id(1)

        @pl.when(s == 0)
        def _():
            carry_ref[...] = jnp.zeros_like(carry_ref)

        @pl.when(jnp.logical_and(bi == 0, s == 0))
        def _():
            gb_ref[...] = jnp.zeros_like(gb_ref)

        r = lax.broadcasted_iota(jnp.int32, (ts, ts), 0)
        c = lax.broadcasted_iota(jnp.int32, (ts, ts), 1)
        tri = jnp.where(r <= c, 1.0, 0.0).astype(BF16)
        rc = _dot3(tri, d_ref[...], True) + carry_ref[0:1, :]
        carry_ref[...] = jnp.broadcast_to(rc[0:1, :], carry_ref.shape)
        z = f_ref[...] + b_ref[...]
        dz = rc / (1.0 + jnp.exp(z))
        o_ref[...] = dz.astype(o_ref.dtype)
        gb_ref[...] += jnp.broadcast_to(jnp.sum(dz, axis=0, keepdims=True), gb_ref.shape)

    rev = lambda b, s: (b * ns + (ns - 1 - s), 0)
    return pl.pallas_call(
        body,
        out_shape=(jax.ShapeDtypeStruct((nb * SEQ, LANES), BF16), jax.ShapeDtypeStruct((8, LANES), F32)),
        grid=(nb, ns),
        in_specs=[pl.BlockSpec((ts, LANES), rev), pl.BlockSpec((ts, LANES), rev), pl.BlockSpec((1, LANES), lambda b, s: (0, 0))],
        out_specs=(pl.BlockSpec((ts, LANES), rev), pl.BlockSpec((8, LANES), lambda b, s: (0, 0))),
        scratch_shapes=[pltpu.VMEM((8, LANES), F32)],
        compiler_params=_cparams(dimension_semantics=("arbitrary", "arbitrary")),
        name=name,
    )(dcol, flog, bpad)


class _AttnCfg:
    def __init__(self, *, e, tq, tk, lq, lk, causal, window, ncol, qcol, kcol, vcol, split_p=False):
        self.e, self.tq, self.tk, self.lq, self.lk = e, tq, tk, lq, lk
        self.split_p = split_p
        self.causal, self.window = causal, window
        self.ncol, self.qcol, self.kcol, self.vcol = ncol, qcol, kcol, vcol
        self.nh = LANES // e
        self.scale = 1.0 / math.sqrt(e)
        self.nq, self.nk = lq // tq, lk // tk

    def k_range(self, i):
        if not self.causal:
            return 0, self.nk
        hi = ((i + 1) * self.tq - 1) // self.tk + 1
        if self.window is None:
            return 0, hi
        return jnp.maximum((i * self.tq - self.window) // self.tk, 0), hi


def _head_masks(nh):
    lane = lax.broadcasted_iota(jnp.int32, (1, LANES), 1)
    return [None] if nh == 1 else [lane < HEAD_DIM, lane >= HEAD_DIM]


def _sel(mask, a, b):
    return a if mask is None else jnp.where(mask, a, b)


def _scores(cfg, qh, kb, q0, k0, dlt0, bias):
    s = lax.dot_general(qh, kb, (((1,), (1,)), ((), ())), preferred_element_type=F32) * cfg.scale
    if bias is not None:
        s = s + bias
    if cfg.causal:
        d = dlt0 + (q0 - k0)
        if cfg.window is None:
            ok = d >= 0
        else:
            ok = d.astype(jnp.uint32) <= jnp.uint32(cfg.window)
        s = jnp.where(ok, s, NEG_INF)
    return s


def _attn_fwd(cfg, q, k, v, *, out_cols, bias=None, state=None, finalize=True, name):
    g = q.shape[0]
    tq, tk, e, nh = cfg.tq, cfg.tk, cfg.e, cfg.nh

    def body(*refs):
        refs = list(refs)
        q_ref, k_ref, v_ref = refs[:3]
        del refs[:3]
        if bias is not None:
            cb_ref, cr_ref = refs[:2]
            del refs[:2]
        if state is not None:
            ai_ref, mi_ref, li_ref = refs[:3]
            del refs[:3]
        out_refs = refs
        masks = _head_masks(nh)
        dlt0 = lax.broadcasted_iota(jnp.int32, (tq, tk), 0) - lax.broadcasted_iota(jnp.int32, (tq, tk), 1)

        def qbody(i, carry):
            q0 = pl.multiple_of(i * tq, tq)
            rows = pl.ds(q0, tq)
            qb = q_ref[rows, :]
            lo, hi = cfg.k_range(i)
            res = []
            for h in range(nh):
                qh = _sel(masks[h], qb, jnp.zeros_like(qb))
                if state is not None:
                    m0 = mi_ref[rows, h * e:h * e + 1]
                    l0 = li_ref[rows, h * e:h * e + 1]
                    a0 = ai_ref[rows, :]
                else:
                    m0 = jnp.full((tq, 1), NEG_INF, F32)
                    l0 = jnp.zeros((tq, 1), F32)
                    a0 = jnp.zeros((tq, LANES), F32)
                cq = cb_ref[rows, h * e:h * e + 1] if bias is not None else None

                def kbody(jk, c, qh=qh, cq=cq, h=h):
                    m, l, a = c
                    k0 = pl.multiple_of(jk * tk, tk)
                    kb = k_ref[pl.ds(k0, tk), :]
                    vb = v_ref[pl.ds(k0, tk), :]
                    b = (cq - cr_ref[jk, h:h + 1, :]) if bias is not None else None
                    s = _scores(cfg, qh, kb, q0, k0, dlt0, b)
                    m_new = jnp.maximum(m, jnp.max(s, axis=1, keepdims=True))
                    alpha = jnp.exp(m - m_new)
                    p = jnp.exp(s - m_new)
                    l = alpha * l + jnp.sum(p, axis=1, keepdims=True)
                    pb = p.astype(BF16)
                    pv = jnp.dot(pb, vb, preferred_element_type=F32)
                    if cfg.split_p:
                        pv = pv + jnp.dot((p - pb.astype(F32)).astype(BF16), vb, preferred_element_type=F32)
                    a = alpha * a + pv
                    return m_new, l, a

                res.append(lax.fori_loop(lo, hi, kbody, (m0, l0, a0)))
            if nh == 1:
                m, l, a = res[0]
                m, l = jnp.broadcast_to(m, (tq, LANES)), jnp.broadcast_to(l, (tq, LANES))
            else:
                m = jnp.where(masks[0], res[0][0], res[1][0])
                l = jnp.where(masks[0], res[0][1], res[1][1])
                a = jnp.where(masks[0], res[0][2], res[1][2])
            if finalize:
                out_refs[0][rows, :] = a / l
                out_refs[1][rows, :] = m + jnp.log(l)
            else:
                out_refs[0][rows, :] = a
                out_refs[1][rows, :] = m
                out_refs[2][rows, :] = l
            return carry

        lax.fori_loop(0, cfg.nq, qbody, 0)

    qspec = pl.BlockSpec((None, cfg.lq, LANES), lambda b, j: (b, 0, cfg.qcol(j)))
    kspec = pl.BlockSpec((None, cfg.lk, LANES), lambda b, j: (b, 0, cfg.kcol(j)))
    vspec = pl.BlockSpec((None, cfg.lk, LANES), lambda b, j: (b, 0, cfg.vcol(j)))
    ospec = pl.BlockSpec((None, cfg.lq, LANES), lambda b, j: (b, 0, j))
    args, in_specs = [q, k, v], [qspec, kspec, vspec]
    if bias is not None:
        args += list(bias)
        in_specs += [ospec, pl.BlockSpec((None, None, cfg.nk, 8, tk), lambda b, j: (b, j, 0, 0, 0))]
    aliases = {}
    if state is not None:
        aliases = {len(args) + t: t for t in range(3 if not finalize else 2)}
        args += list(state)
        in_specs += [ospec] * 3
    n_out = 2 if finalize else 3
    osd = jax.ShapeDtypeStruct((g, cfg.lq, out_cols), F32)
    return pl.pallas_call(
        body,
        out_shape=(osd,) * n_out,
        grid=(g, cfg.ncol),
        in_specs=in_specs,
        out_specs=(ospec,) * n_out,
        input_output_aliases=aliases,
        compiler_params=_cparams(dimension_semantics=("parallel", "parallel")),
        name=name,
    )(*args)


def _attn_bwd(cfg, q, k, v, do, o, lse, *, out_cols, kv_cols, bias=None, acc=None, do_off=0, name):
    g = q.shape[0]
    tq, tk, e, nh = cfg.tq, cfg.tk, cfg.e, cfg.nh
    t0 = (((0,), (0,)), ((), ()))

    def body(*refs):
        refs = list(refs)
        q_ref, k_ref, v_ref, do_ref, o_ref, lse_ref = refs[:6]
        del refs[:6]
        if bias is not None:
            cb_ref, cr_ref = refs[:2]
            del refs[:2]
        if acc is not None:
            dqi_ref, dki_ref, dvi_ref = refs[:3]
            del refs[:3]
        dq_ref, dk_ref, dv_ref = refs[:3]
        dcr_ref = refs[3] if bias is not None else None
        masks = _head_masks(nh)
        dlt0 = lax.broadcasted_iota(jnp.int32, (tq, tk), 0) - lax.broadcasted_iota(jnp.int32, (tq, tk), 1)
        if acc is not None:
            dq_ref[...] = dqi_ref[...]
            dk_ref[...] = dki_ref[...]
            dv_ref[...] = dvi_ref[...]
        else:
            dq_ref[...] = jnp.zeros_like(dq_ref)
            dk_ref[...] = jnp.zeros_like(dk_ref)
            dv_ref[...] = jnp.zeros_like(dv_ref)
        if dcr_ref is not None:
            dcr_ref[...] = jnp.zeros_like(dcr_ref)

        def qbody(i, carry):
            q0 = pl.multiple_of(i * tq, tq)
            rows = pl.ds(q0, tq)
            qb = q_ref[rows, :]
            dob = do_ref[rows, :].astype(BF16)
            prod = dob.astype(F32) * o_ref[rows, :]
            lo, hi = cfg.k_range(i)
            dqs = []
            for h in range(nh):
                qh = _sel(masks[h], qb, jnp.zeros_like(qb))
                doh = _sel(masks[h], dob, jnp.zeros_like(dob))
                lse_h = lse_ref[rows, h * e:h * e + 1]
                delta = jnp.sum(_sel(masks[h], prod, jnp.zeros_like(prod)), axis=1, keepdims=True)
                cq = cb_ref[rows, h * e:h * e + 1] if bias is not None else None

                def kbody(jk, dq_acc, qh=qh, doh=doh, lse_h=lse_h, delta=delta, cq=cq, h=h):
                    k0 = pl.multiple_of(jk * tk, tk)
                    krows = pl.ds(k0, tk)
                    kb = k_ref[krows, :]
                    vb = v_ref[krows, :]
                    b = (cq - cr_ref[jk, h:h + 1, :]) if bias is not None else None
                    s = _scores(cfg, qh, kb, q0, k0, dlt0, b)
                    p = jnp.exp(s - lse_h)
                    dp = lax.dot_general(doh, vb, (((1,), (1,)), ((), ())), preferred_element_type=F32)
                    ds = p * (dp - delta)
                    if dcr_ref is not None:
                        dcr_ref[jk, h:h + 1, :] += jnp.sum(ds, axis=0, keepdims=True)
                    dsb = (ds * cfg.scale).astype(BF16)
                    dv_ref[krows, :] += lax.dot_general(p.astype(BF16), doh, t0, preferred_element_type=F32)
                    dk_ref[krows, :] += lax.dot_general(dsb, qh, t0, preferred_element_type=F32)
                    return dq_acc + jnp.dot(dsb, kb, preferred_element_type=F32)

                dqs.append(lax.fori_loop(lo, hi, kbody, jnp.zeros((tq, LANES), F32)))
            dq = dqs[0] if nh == 1 else jnp.where(masks[0], dqs[0], dqs[1])
            dq_ref[rows, :] += dq
            return carry

        lax.fori_loop(0, cfg.nq, qbody, 0)

    qspec = pl.BlockSpec((None, cfg.lq, LANES), lambda b, j: (b, 0, cfg.qcol(j)))
    kspec = pl.BlockSpec((None, cfg.lk, LANES), lambda b, j: (b, 0, cfg.kcol(j)))
    vspec = pl.BlockSpec((None, cfg.lk, LANES), lambda b, j: (b, 0, cfg.vcol(j)))
    ospec = pl.BlockSpec((None, cfg.lq, LANES), lambda b, j: (b, 0, j))
    kvspec = pl.BlockSpec((None, cfg.lk, LANES), lambda b, j: (b, 0, j))
    dospec = pl.BlockSpec((None, cfg.lq, LANES), lambda b, j: (b, 0, do_off + j))
    args, in_specs = [q, k, v, do, o, lse], [qspec, kspec, vspec, dospec, ospec, ospec]
    out_shape = [jax.ShapeDtypeStruct((g, cfg.lq, out_cols), F32), jax.ShapeDtypeStruct((g, cfg.lk, kv_cols), F32),
                 jax.ShapeDtypeStruct((g, cfg.lk, kv_cols), F32)]
    out_specs = [ospec, kvspec, kvspec]
    if bias is not None:
        args += list(bias)
        crspec = pl.BlockSpec((None, None, cfg.nk, 8, tk), lambda b, j: (b, j, 0, 0, 0))
        in_specs += [ospec, crspec]
        out_shape.append(jax.ShapeDtypeStruct((g, cfg.ncol, cfg.nk, 8, tk), F32))
        out_specs.append(crspec)
    aliases = {}
    if acc is not None:
        aliases = {len(args) + t: t for t in range(3)}
        args += list(acc)
        in_specs += [ospec, kvspec, kvspec]
    return pl.pallas_call(
        body,
        out_shape=tuple(out_shape),
        grid=(g, cfg.ncol),
        in_specs=in_specs,
        out_specs=tuple(out_specs),
        input_output_aliases=aliases,
        compiler_params=_cparams(dimension_semantics=("parallel", "parallel")),
        name=name,
    )(*args)


BLK = 128
NBLK = SEQ // BLK
QK_SCALE = 1.0 / math.sqrt(HEAD_DIM)
DIL_STEPS = tuple(d for _, d in DILATIONS)
assert all(w // d == BLK for w, d in DILATIONS)
_T0 = (((0,), (0,)), ((), ()))
_NT = (((1,), (1,)), ((), ()))


def _stack_heads(a, masks):
    z = jnp.zeros_like(a)
    return jnp.concatenate([jnp.where(masks[0], a, z), jnp.where(masks[1], a, z)], axis=0)


def _tri_bias(lower):
    r = lax.broadcasted_iota(jnp.int32, (BLK, BLK), 0)
    c = lax.broadcasted_iota(jnp.int32, (BLK, BLK), 1)
    return jnp.where((c <= r) if lower else (c >= r), 0.0, NEG_INF).astype(F32)


def _dil_rows(r, i, d):
    start = r + i * (BLK * d)
    return pl.ds(start, BLK) if d == 1 else pl.ds(start, BLK, stride=d)


DIL_SET = 4


def _dil_sets(d, fn):
    nbk = SEQ // d // BLK
    if d == 1:
        def gbody(g, c):
            fn([(0, DIL_SET * g + a, None if a == 0 else True) for a in range(DIL_SET)])
            return c
        lax.fori_loop(0, nbk // DIL_SET, gbody, 0)
    elif nbk > 1:
        assert nbk == DIL_SET
        def rbody(r, c):
            fn([(r, i, i > 0) for i in range(nbk)])
            return c
        lax.fori_loop(0, d, rbody, 0)
    else:
        def rbody(rr, c):
            fn([(DIL_SET * rr + a, 0, False) for a in range(DIL_SET)])
            return c
        lax.fori_loop(0, d // DIL_SET, rbody, 0)


def _dil_key_tiles(r, i, d, has_prev, qrows, tri_cur, tri_prev):
    tiles = [(qrows, tri_cur)]
    if has_prev is None:
        tiles.append((_dil_rows(r, jnp.maximum(i - 1, 0), d), tri_prev + jnp.where(i > 0, 0.0, NEG_INF)))
    elif has_prev:
        tiles.append((_dil_rows(r, i - 1, d), tri_prev))
    return tiles


def _dil_fwd(qkv, *, name):
    nb = qkv.shape[0]
    ncol = DIL_W // LANES
    hd = HEAD_DIM

    def body(q_ref, k_ref, v_ref, o_ref, lse_ref, m_ref, l_ref, a_ref):
        masks = _head_masks(2)
        tri_cur, tri_prev = _tri_bias(True), _tri_bias(False)
        for pi, d in enumerate(DIL_STEPS):
            first, last = pi == 0, pi == len(DIL_STEPS) - 1

            def qset(blocks, d=d, first=first, last=last):
                work = []
                for r, i, has_prev in blocks:
                    qrows = _dil_rows(r, i, d)
                    qcat = _stack_heads((q_ref[qrows, :] * QK_SCALE).astype(BF16), masks)
                    ss, krs = [], []
                    for krows, bias in _dil_key_tiles(r, i, d, has_prev, qrows, tri_cur, tri_prev):
                        s = lax.dot_general(qcat, k_ref[krows, :].astype(BF16), _NT, preferred_element_type=F32)
                        ss.append((s[:BLK] + bias, s[BLK:] + bias))
                        krs.append(krows)
                    work.append((qrows, ss, krs))
                for qrows, ss, krs in work:
                    e0 = ss[0][0] if len(ss) == 1 else jnp.maximum(ss[0][0], ss[1][0])
                    e1 = ss[0][1] if len(ss) == 1 else jnp.maximum(ss[0][1], ss[1][1])
                    n0 = jnp.max(e0, axis=1, keepdims=True)
                    n1 = jnp.max(e1, axis=1, keepdims=True)
                    if not first:
                        mo, lo = m_ref[qrows, :], l_ref[qrows, :]
                        m0, m1 = mo[:, 0:1], mo[:, hd:hd + 1]
                        n0, n1 = jnp.maximum(n0, m0), jnp.maximum(n1, m1)
                        a0, a1 = jnp.exp(m0 - n0), jnp.exp(m1 - n1)
                    ps = [(jnp.exp(s0 - n0), jnp.exp(s1 - n1)) for s0, s1 in ss]
                    t0 = ps[0][0] if len(ps) == 1 else ps[0][0] + ps[1][0]
                    t1 = ps[0][1] if len(ps) == 1 else ps[0][1] + ps[1][1]
                    l0 = jnp.sum(t0, axis=1, keepdims=True)
                    l1 = jnp.sum(t1, axis=1, keepdims=True)
                    acc = None
                    for (p0, p1), krows in zip(ps, krs):
                        vcat = _stack_heads(v_ref[krows, :].astype(BF16), masks)
                        pv = jnp.dot(jnp.concatenate([p0, p1], axis=1).astype(BF16), vcat, preferred_element_type=F32)
                        acc = pv if acc is None else acc + pv
                    if not first:
                        l0 = l0 + a0 * lo[:, 0:1]
                        l1 = l1 + a1 * lo[:, hd:hd + 1]
                        acc = acc + a_ref[qrows, :] * jnp.where(masks[0], a0, a1)
                    if last:
                        o_ref[qrows, :] = acc / jnp.where(masks[0], l0, l1)
                        lse_ref[qrows, :] = jnp.where(masks[0], n0 + jnp.log(l0), n1 + jnp.log(l1))
                    else:
                        m_ref[qrows, :] = jnp.where(masks[0], n0, n1)
                        l_ref[qrows, :] = jnp.where(masks[0], l0, l1)
                        a_ref[qrows, :] = acc

            _dil_sets(d, qset)

    spec = lambda off: pl.BlockSpec((None, SEQ, LANES), lambda b, j: (b, 0, off + j))
    ospec = pl.BlockSpec((None, SEQ, LANES), lambda b, j: (b, 0, j))
    osd = jax.ShapeDtypeStruct((nb, SEQ, DIL_W), F32)
    return pl.pallas_call(
        body, out_shape=(osd, osd), grid=(nb, ncol),
        in_specs=[spec(0), spec(ncol), spec(2 * ncol)], out_specs=(ospec, ospec),
        scratch_shapes=[pltpu.VMEM((SEQ, LANES), F32)] * 3,
        compiler_params=_cparams(dimension_semantics=("parallel", "parallel")), name=name,
    )(qkv, qkv, qkv)


def _dil_bwd(qkv, do, o, lse, tabs, *, do_off, name):
    nb = qkv.shape[0]
    ncol = DIL_W // LANES
    hd = HEAD_DIM

    def body(q_ref, k_ref, v_ref, do_ref, o_ref, lse_ref, c_ref, s1_ref, s2_ref, dqo_ref, dko_ref, dvo_ref,
             dq_ref, dk_ref, dv_ref, dl_ref, dof_ref):
        masks = _head_masks(2)
        tri_cur, tri_prev = _tri_bias(True), _tri_bias(False)
        dq_ref[...] = jnp.zeros_like(dq_ref)
        dk_ref[...] = jnp.zeros_like(dk_ref)
        dv_ref[...] = jnp.zeros_like(dv_ref)

        def delta_body(i, c):
            rows = pl.ds(pl.multiple_of(i * BLK, BLK), BLK)
            dof = do_ref[rows, :].astype(F32)
            dof_ref[rows, :] = dof
            prod = dof * o_ref[rows, :]
            z = jnp.zeros_like(prod)
            dl_ref[rows, :] = jnp.where(masks[0], jnp.sum(jnp.where(masks[0], prod, z), axis=1, keepdims=True),
                                        jnp.sum(jnp.where(masks[1], prod, z), axis=1, keepdims=True))
            return c

        lax.fori_loop(0, NBLK, delta_body, 0)

        for d in DIL_STEPS:
            def qset(blocks, d=d):
                work = []
                for r, i, has_prev in blocks:
                    qrows = _dil_rows(r, i, d)
                    qcat = _stack_heads((q_ref[qrows, :] * QK_SCALE).astype(BF16), masks)
                    docat = _stack_heads(dof_ref[qrows, :].astype(BF16), masks)
                    tiles = []
                    for krows, bias in _dil_key_tiles(r, i, d, has_prev, qrows, tri_cur, tri_prev):
                        s = lax.dot_general(qcat, k_ref[krows, :].astype(BF16), _NT, preferred_element_type=F32)
                        dp = lax.dot_general(docat, v_ref[krows, :].astype(BF16), _NT, preferred_element_type=F32)
                        tiles.append((krows, s, dp, bias))
                    work.append((qrows, qcat, docat, tiles))
                for qrows, qcat, docat, tiles in work:
                    lseb, dlb = lse_ref[qrows, :], dl_ref[qrows, :]
                    lse0, lse1 = lseb[:, 0:1], lseb[:, hd:hd + 1]
                    dl0, dl1 = dlb[:, 0:1], dlb[:, hd:hd + 1]
                    dq = None
                    for krows, s, dp, bias in tiles:
                        p0 = jnp.exp(s[:BLK] + bias - lse0)
                        p1 = jnp.exp(s[BLK:] + bias - lse1)
                        ds0 = p0 * (dp[:BLK] - dl0)
                        ds1 = p1 * (dp[BLK:] - dl1)
                        ds0b, ds1b = ds0.astype(BF16), ds1.astype(BF16)
                        pcat = jnp.concatenate([p0.astype(BF16), p1.astype(BF16)], axis=0)
                        dscat = jnp.concatenate([ds0b, ds1b], axis=0)
                        dv_ref[krows, :] += lax.dot_general(pcat, docat, _T0, preferred_element_type=F32)
                        dk_ref[krows, :] += lax.dot_general(dscat, qcat, _T0, preferred_element_type=F32)
                        dsrow = jnp.concatenate([ds0b, ds1b], axis=1)
                        kcat = _stack_heads((k_ref[krows, :] * QK_SCALE).astype(BF16), masks)
                        t = jnp.dot(dsrow, kcat, preferred_element_type=F32)
                        dq = t if dq is None else dq + t
                    dq_ref[qrows, :] += dq

            _dil_sets(d, qset)

        def out_body(i, c):
            rows = pl.ds(pl.multiple_of(i * BLK, BLK), BLK)
            tab = (c_ref[rows, :], s1_ref[rows, :], s2_ref[rows, :])
            dqo_ref[rows, :] = _rope_apply(dq_ref[rows, :], *tab, transpose=True).astype(dqo_ref.dtype)
            dko_ref[rows, :] = _rope_apply(dk_ref[rows, :], *tab, transpose=True).astype(dko_ref.dtype)
            dvo_ref[rows, :] = dv_ref[rows, :].astype(dvo_ref.dtype)
            return c

        lax.fori_loop(0, NBLK, out_body, 0)

    spec = lambda off: pl.BlockSpec((None, SEQ, LANES), lambda b, j: (b, 0, off + j))
    ospec = pl.BlockSpec((None, SEQ, LANES), lambda b, j: (b, 0, j))
    tspec = pl.BlockSpec((SEQ, LANES), lambda b, j: (0, 0))
    osd = jax.ShapeDtypeStruct((nb, SEQ, DIL_W), BF16)
    return pl.pallas_call(
        body, out_shape=(osd, osd, osd), grid=(nb, ncol),
        in_specs=[spec(0), spec(ncol), spec(2 * ncol), spec(do_off), ospec, ospec, tspec, tspec, tspec],
        out_specs=(ospec, ospec, ospec),
        scratch_shapes=[pltpu.VMEM((SEQ, LANES), F32)] * 5,
        compiler_params=_cparams(dimension_semantics=("parallel", "parallel")), name=name,
    )(qkv, qkv, qkv, do, o, lse, *tabs)


FOX_GROUP = 4
assert NBLK % FOX_GROUP == 0
_FOX_COLS = tuple(c // LANES for c in (C_FQ, C_FK, C_FV))


def _fox_specs():
    cols = [pl.BlockSpec((None, SEQ, LANES), (lambda b, j, off=off: (b, 0, off + j))) for off in _FOX_COLS]
    ospec = pl.BlockSpec((None, SEQ, LANES), lambda b, j: (b, 0, j))
    crspec = pl.BlockSpec((None, None, NBLK, 8, BLK), lambda b, j: (b, j, 0, 0, 0))
    return cols, ospec, crspec


def _fox_key_rows(t, e):
    return pl.ds(pl.multiple_of((FOX_GROUP * t + e) * BLK, BLK), BLK)


def _fox_fwd(p3, crow, *, name):
    nb = p3.shape[0]
    g = FOX_GROUP

    def body(q_ref, k_ref, v_ref, cr_ref, o_ref, lse_ref):
        masks = _head_masks(2)
        tri = _tri_bias(True)

        def qk(qcat, t):
            return tuple(lax.dot_general(qcat, k_ref[_fox_key_rows(t, e), :], _NT, preferred_element_type=F32) for e in range(g))

        def consume(ss, t, state, nblk, diag):
            m0, m1, l0, l1, acc = state
            us = []
            for e in range(nblk):
                cr = cr_ref[g * t + e]
                u0 = ss[e][:BLK] - cr[0:1, :]
                u1 = ss[e][BLK:] - cr[1:2, :]
                if diag and e == nblk - 1:
                    u0, u1 = u0 + tri, u1 + tri
                us.append((u0, u1))
            x0 = functools.reduce(jnp.maximum, [u[0] for u in us])
            x1 = functools.reduce(jnp.maximum, [u[1] for u in us])
            n0 = jnp.maximum(m0, jnp.max(x0, axis=1, keepdims=True))
            n1 = jnp.maximum(m1, jnp.max(x1, axis=1, keepdims=True))
            a0, a1 = jnp.exp(m0 - n0), jnp.exp(m1 - n1)
            acc = acc * jnp.where(masks[0], a0, a1)
            t0 = t1 = None
            for e in range(nblk):
                p0, p1 = jnp.exp(us[e][0] - n0), jnp.exp(us[e][1] - n1)
                t0 = p0 if t0 is None else t0 + p0
                t1 = p1 if t1 is None else t1 + p1
                pcat = jnp.concatenate([p0, p1], axis=1)
                hi = pcat.astype(BF16)
                lo = (pcat - hi.astype(F32)).astype(BF16)
                vcat = _stack_heads(v_ref[_fox_key_rows(t, e), :], masks)
                acc = acc + jnp.dot(hi, vcat, preferred_element_type=F32) + jnp.dot(lo, vcat, preferred_element_type=F32)
            l0 = a0 * l0 + jnp.sum(t0, axis=1, keepdims=True)
            l1 = a1 * l1 + jnp.sum(t1, axis=1, keepdims=True)
            return n0, n1, l0, l1, acc

        def gbody(ng, c):
            neg = jnp.full((BLK, 1), NEG_INF, F32)
            z1 = jnp.zeros((BLK, 1), F32)
            rows = [pl.ds(pl.multiple_of((g * ng + a) * BLK, BLK), BLK) for a in range(g)]
            qcats = [_stack_heads(q_ref[rows[a], :] * QK_SCALE, masks) for a in range(g)]
            first = [qk(qcats[a], 0) for a in range(g)]
            done = []
            for a in range(g):
                def step(t, cc, qcat=qcats[a]):
                    ss, st = cc
                    nxt = qk(qcat, t + 1)
                    return nxt, consume(ss, t, st, g, False)

                done.append(lax.fori_loop(0, ng, step, (first[a], (neg, neg, z1, z1, jnp.zeros((BLK, LANES), F32)))))
            for a in range(g):
                ss, state = done[a]
                m0, m1, l0, l1, acc = consume(ss, ng, state, a + 1, True)
                o_ref[rows[a], :] = acc / jnp.where(masks[0], l0, l1)
                lse_ref[rows[a], :] = jnp.where(masks[0], m0 + jnp.log(l0), m1 + jnp.log(l1))
            return c

        lax.fori_loop(0, NBLK // g, gbody, 0)

    cols, ospec, crspec = _fox_specs()
    osd = jax.ShapeDtypeStruct((nb, SEQ, FOX_W), F32)
    return pl.pallas_call(
        body, out_shape=(osd, osd), grid=(nb, FOX_W // LANES), in_specs=cols + [crspec], out_specs=(ospec, ospec),
        compiler_params=_cparams(dimension_semantics=("parallel", "parallel")), name=name,
    )(p3, p3, p3, crow)


def _fox_bwd(p3, crow, do, o, lse, *, do_off, name):
    nb = p3.shape[0]
    g = FOX_GROUP
    hd = HEAD_DIM

    def body(q_ref, k_ref, v_ref, cr_ref, do_ref, o_ref, lse_ref, dq_ref, dko_ref, dvo_ref, dcr_ref, dk_ref, dv_ref):
        masks = _head_masks(2)
        tri = _tri_bias(True)
        dk_ref[...] = jnp.zeros_like(dk_ref)
        dv_ref[...] = jnp.zeros_like(dv_ref)
        dcr_ref[...] = jnp.zeros_like(dcr_ref)

        def products(qcat, docat, t):
            out = []
            for e in range(g):
                krows = _fox_key_rows(t, e)
                out.append(lax.dot_general(qcat, k_ref[krows, :], _NT, preferred_element_type=F32))
                out.append(lax.dot_general(docat, v_ref[krows, :], _NT, preferred_element_type=F32))
            return tuple(out)

        def consume(prod, t, ctx, dq, nblk, diag):
            qcat, docat, lse0, lse1, dl0, dl1 = ctx
            for e in range(nblk):
                jb = g * t + e
                krows = _fox_key_rows(t, e)
                s, dp = prod[2 * e], prod[2 * e + 1]
                cr = cr_ref[jb]
                u0 = s[:BLK] - cr[0:1, :]
                u1 = s[BLK:] - cr[1:2, :]
                if diag and e == nblk - 1:
                    u0, u1 = u0 + tri, u1 + tri
                p0 = jnp.exp(u0 - lse0)
                p1 = jnp.exp(u1 - lse1)
                ds0 = p0 * (dp[:BLK] - dl0)
                ds1 = p1 * (dp[BLK:] - dl1)
                dcr_ref[jb, 0:1, :] += jnp.sum(ds0, axis=0, keepdims=True)
                dcr_ref[jb, 1:2, :] += jnp.sum(ds1, axis=0, keepdims=True)
                ds0b, ds1b = ds0.astype(BF16), ds1.astype(BF16)
                pcat = jnp.concatenate([p0.astype(BF16), p1.astype(BF16)], axis=0)
                dscat = jnp.concatenate([ds0b, ds1b], axis=0)
                dv_ref[krows, :] += lax.dot_general(pcat, docat, _T0, preferred_element_type=F32)
                dk_ref[krows, :] += lax.dot_general(dscat, qcat, _T0, preferred_element_type=F32)
                dsrow = jnp.concatenate([ds0b, ds1b], axis=1)
                dq = dq + jnp.dot(dsrow, _stack_heads(k_ref[krows, :] * QK_SCALE, masks), preferred_element_type=F32)
            return dq

        def gbody(ng, c):
            ctxs, rows = [], []
            for a in range(g):
                r = pl.ds(pl.multiple_of((g * ng + a) * BLK, BLK), BLK)
                qcat = _stack_heads(q_ref[r, :] * QK_SCALE, masks)
                dob = do_ref[r, :].astype(BF16)
                prod = dob.astype(F32) * o_ref[r, :]
                z = jnp.zeros_like(prod)
                dl0 = jnp.sum(jnp.where(masks[0], prod, z), axis=1, keepdims=True)
                dl1 = jnp.sum(jnp.where(masks[1], prod, z), axis=1, keepdims=True)
                lseb = lse_ref[r, :]
                ctxs.append((qcat, _stack_heads(dob, masks), lseb[:, 0:1], lseb[:, hd:hd + 1], dl0, dl1))
                rows.append(r)
            first = [products(ctxs[a][0], ctxs[a][1], 0) for a in range(g)]
            done = []
            for a in range(g):
                def step(t, cc, ctx=ctxs[a]):
                    pr, dq = cc
                    nxt = products(ctx[0], ctx[1], t + 1)
                    return nxt, consume(pr, t, ctx, dq, g, False)

                done.append(lax.fori_loop(0, ng, step, (first[a], jnp.zeros((BLK, LANES), F32))))
            for a in range(g):
                pr, dq = done[a]
                dq_ref[rows[a], :] = consume(pr, ng, ctxs[a], dq, a + 1, True).astype(dq_ref.dtype)
            return c

        lax.fori_loop(0, NBLK // g, gbody, 0)
        dko_ref[...] = dk_ref[...].astype(dko_ref.dtype)
        dvo_ref[...] = dv_ref[...].astype(dvo_ref.dtype)

    cols, ospec, crspec = _fox_specs()
    dospec = pl.BlockSpec((None, SEQ, LANES), lambda b, j: (b, 0, do_off + j))
    osd = jax.ShapeDtypeStruct((nb, SEQ, FOX_W), BF16)
    return pl.pallas_call(
        body, out_shape=(osd, osd, osd, jax.ShapeDtypeStruct((nb, FOX_W // LANES, NBLK, 8, BLK), F32)),
        grid=(nb, FOX_W // LANES), in_specs=cols + [crspec, dospec, ospec, ospec], out_specs=(ospec, ospec, ospec, crspec),
        scratch_shapes=[pltpu.VMEM((SEQ, LANES), F32)] * 2,
        compiler_params=_cparams(dimension_semantics=("parallel", "parallel")), name=name,
    )(p3, p3, p3, crow, do, o, lse)


def _mem_cfg():
    return _AttnCfg(e=MEM_HEAD_DIM, tq=256, tk=MEM_LEN, lq=SEQ, lk=MEM_LEN, causal=False, window=None, ncol=MEM_HEADS,
                    qcol=lambda j: C_MQ // LANES + j, kcol=lambda j: j, vcol=lambda j: MEM_HEADS + j)


_B1, _B2 = FOX_W // LANES, (FOX_W + DIL_W) // LANES


def _dy_gate_bwd(dx2b, wo, fox, dil, memo, p16, *, tm, tn, name):
    t, d = dx2b.shape
    assert FOX_W % tn == 0 and DIL_W % tn == 0 and MEM_W % tn == 0 and all(c % tn == 0 for c in (C_FG, C_DG, C_MG))
    n1, n2, n3 = FOX_W // tn, (FOX_W + DIL_W) // tn, MIX_W // tn

    def body(dx_ref, w_ref, f_ref, d_ref, m_ref, g_ref, da_ref, dg_ref):
        j = pl.program_id(1)
        dyv = lax.dot_general(dx_ref[...], w_ref[...], _NT, preferred_element_type=F32)
        a = jnp.where(j < n1, f_ref[...], jnp.where(j < n2, d_ref[...], m_ref[...]))
        gt = g_ref[...].astype(F32)
        sg = 1.0 / (1.0 + jnp.exp(-gt))
        da_ref[...] = (dyv * gt * sg).astype(da_ref.dtype)
        dg_ref[...] = (dyv * a * sg * (1.0 + gt * (1.0 - sg))).astype(dg_ref.dtype)

    def gcol(j):
        return jnp.where(j < n1, C_FG // tn + j, jnp.where(j < n2, C_DG // tn + j - n1, C_MG // tn + j - n2))

    tile = pl.BlockSpec((tm, tn), lambda i, j: (i, j))
    return pl.pallas_call(
        body,
        out_shape=(jax.ShapeDtypeStruct((t, MIX_W), BF16), jax.ShapeDtypeStruct((t, MIX_W), BF16)),
        grid=(t // tm, n3),
        in_specs=[pl.BlockSpec((tm, d), lambda i, j: (i, 0)), pl.BlockSpec((tn, d), lambda i, j: (j, 0)),
                  pl.BlockSpec((tm, tn), lambda i, j: (i, jnp.minimum(j, n1 - 1))),
                  pl.BlockSpec((tm, tn), lambda i, j: (i, jnp.clip(j - n1, 0, n2 - n1 - 1))),
                  pl.BlockSpec((tm, tn), lambda i, j: (i, jnp.clip(j - n2, 0, n3 - n2 - 1))),
                  pl.BlockSpec((tm, tn), lambda i, j: (i, gcol(j)))],
        out_specs=(tile, tile),
        compiler_params=_cparams(dimension_semantics=("parallel", "parallel")),
        name=name,
    )(dx2b, wo, fox, dil, memo, p16)


def _silu(g):
    return g / (1.0 + jnp.exp(-g))


def _out_loss(fox, dil, memo, p16, wo, x, tgt, gfin, *, tm, name):
    t, d = x.shape
    n_feat = float(d)

    def body(f_ref, d_ref, m_ref, fg_ref, dg_ref, mg_ref, w_ref, x_ref, t_ref, g_ref, y_ref, dx_ref, dxb_ref, st_ref):
        i = pl.program_id(0)

        @pl.when(i == 0)
        def _():
            st_ref[...] = jnp.zeros_like(st_ref)

        y = jnp.concatenate([(a_ref[...] * _silu(gt_ref[...].astype(F32))).astype(BF16)
                             for a_ref, gt_ref in ((f_ref, fg_ref), (d_ref, dg_ref), (m_ref, mg_ref))], axis=1)
        y_ref[...] = y
        x2 = x_ref[...] + jnp.dot(y, w_ref[...], preferred_element_type=F32)
        r = lax.rsqrt(jnp.mean(x2 * x2, axis=-1, keepdims=True) + RMS_EPS)
        nrm = x2 * r
        gv = g_ref[...]
        err = nrm * gv - t_ref[...]
        dout = err * (1.0 / n_feat)
        dn = dout * gv
        dx2 = r * (dn - nrm * jnp.mean(dn * nrm, axis=-1, keepdims=True))
        dx_ref[...] = dx2
        dxb_ref[...] = dx2.astype(dxb_ref.dtype)
        st_ref[0:1, :] += jnp.sum(dout * nrm, axis=0, keepdims=True)
        st_ref[1:2, :] += (0.5 / n_feat) * jnp.sum(err * err, axis=0, keepdims=True)

    row = pl.BlockSpec((tm, d), lambda i: (i, 0))
    whole = lambda w: pl.BlockSpec((tm, w), lambda i: (i, 0))
    gate = lambda w, col: pl.BlockSpec((tm, w), lambda i: (i, col // w))
    return pl.pallas_call(
        body,
        out_shape=(jax.ShapeDtypeStruct((t, MIX_W), BF16), jax.ShapeDtypeStruct((t, d), F32), jax.ShapeDtypeStruct((t, d), BF16),
                   jax.ShapeDtypeStruct((8, d), F32)),
        grid=(t // tm,),
        in_specs=[whole(FOX_W), whole(DIL_W), whole(MEM_W), gate(FOX_W, C_FG), gate(DIL_W, C_DG), gate(MEM_W, C_MG),
                  pl.BlockSpec((MIX_W, d), lambda i: (0, 0)), row, row, pl.BlockSpec((1, d), lambda i: (0, 0))],
        out_specs=(pl.BlockSpec((tm, MIX_W), lambda i: (i, 0)), row, row, pl.BlockSpec((8, d), lambda i: (0, 0))),
        compiler_params=_cparams(dimension_semantics=("arbitrary",)),
        name=name,
    )(fox, dil, memo, p16, p16, p16, wo, x, tgt, gfin)


def _dh_rms_bwd(dp, w, x, g, resid, *, tm, tk, name):
    t, d = x.shape
    kdim = dp.shape[1]
    nk = kdim // tk

    def body(*refs):
        if resid is not None:
            dp_ref, w_ref, x_ref, g_ref, r_ref, dx_ref, gg_ref, acc_ref = refs
        else:
            dp_ref, w_ref, x_ref, g_ref, dx_ref, gg_ref, acc_ref = refs
        i = pl.program_id(0)
        k = pl.program_id(1)

        @pl.when(jnp.logical_and(i == 0, k == 0))
        def _():
            gg_ref[...] = jnp.zeros_like(gg_ref)

        prod = lax.dot_general(dp_ref[...], w_ref[...], _NT, preferred_element_type=F32)

        @pl.when(k == 0)
        def _():
            acc_ref[...] = prod

        @pl.when(k > 0)
        def _():
            acc_ref[...] += prod

        @pl.when(k == nk - 1)
        def _():
            dh = acc_ref[...]
            xv = x_ref[...]
            r = lax.rsqrt(jnp.mean(xv * xv, axis=-1, keepdims=True) + RMS_EPS)
            nrm = xv * r
            dn = dh * g_ref[...]
            dx = r * (dn - nrm * jnp.mean(dn * nrm, axis=-1, keepdims=True))
            if resid is not None:
                dx = dx + r_ref[...]
            dx_ref[...] = dx
            gg_ref[0:1, :] += jnp.sum(dh * nrm, axis=0, keepdims=True)

    row = pl.BlockSpec((tm, d), lambda i, k: (i, 0))
    in_specs = [pl.BlockSpec((tm, tk), lambda i, k: (i, k)), pl.BlockSpec((d, tk), lambda i, k: (0, k)), row,
                pl.BlockSpec((1, d), lambda i, k: (0, 0))]
    args = [dp, w, x, g]
    if resid is not None:
        in_specs.append(row)
        args.append(resid)
    return pl.pallas_call(
        body,
        out_shape=(jax.ShapeDtypeStruct((t, d), F32), jax.ShapeDtypeStruct((8, d), F32)),
        grid=(t // tm, nk),
        in_specs=in_specs,
        out_specs=(row, pl.BlockSpec((8, d), lambda i, k: (0, 0))),
        scratch_shapes=[pltpu.VMEM((tm, d), F32)],
        compiler_params=_cparams(dimension_semantics=("arbitrary", "arbitrary")),
        name=name,
    )(*args)


_FLOG0 = 4 * FOX_W
_W_IN_SEGMENTS = ((0, _FLOG0, 0), (_FLOG0, _FLOG0 + FOX_HEADS, PW), (_FLOG0 + FOX_HEADS, IN_W, C_DQ))
SHARD_W = IN_W // N_CHIPS


def _rearrange_w_in(shards):
    def cols(lo, hi):
        parts = []
        for k in range(N_CHIPS):
            a, b = max(lo, k * SHARD_W), min(hi, (k + 1) * SHARD_W)
            if a < b:
                parts.append(shards[k][:, a - k * SHARD_W:b - k * SHARD_W])
        return parts

    (a0, a1, _), (f0, f1, _), (b0, b1, _) = _W_IN_SEGMENTS
    pad = jnp.zeros((shards[0].shape[0], PWF - PW - FOX_HEADS), shards[0].dtype)
    return jnp.concatenate(cols(a0, a1) + cols(b0, b1) + cols(f0, f1) + [pad], axis=1)


def _w_in_grad_slabs(g):
    slabs = []
    for k in range(N_CHIPS):
        parts = []
        for lo, hi, at in _W_IN_SEGMENTS:
            a, b = max(lo, k * SHARD_W), min(hi, (k + 1) * SHARD_W)
            if a < b:
                parts.append(g[:, at + a - lo:at + b - lo])
        slabs.append(jnp.concatenate(parts, axis=1))
    return jnp.stack(slabs, axis=0)


def _local_grads(x, mem, norm_g, w_r, b_forget, mem_norm_g, w_kv, w_o, final_norm_g, tgt, start_reduce=None,
                 early_token=None, late_weights=None):
    nb = x.shape[0]
    t = nb * SEQ
    x2d = x.reshape(t, D_MODEL)
    tgt2d = tgt.reshape(t, D_MODEL)
    tabs = _rope_tables()
    bpad = jnp.pad(b_forget.reshape(1, FOX_HEADS), ((0, 0), (0, LANES - FOX_HEADS)))

    gain0 = norm_g.reshape(1, D_MODEL)
    if early_token is not None:
        gain0 = gain0 + early_token[0:1, 0:1]
    h = _rms_fwd(x2d, gain0, tm=512, name="rms_x")
    p16, dqkv = _proj(h, w_r, tabs, n=PW, tm=2048, tn=256, name="proj")
    flog = _matmul(h, w_r[:, PW:PW + LANES], out_dtype=F32, tm=1024, tn=LANES, tk=D_MODEL, name="proj_flog")
    c12 = _flog_fwd(flog, bpad, nb=nb, ts=256, name="flog_fwd")

    crow = c12[:, :FOX_HEADS].reshape(nb, NBLK, BLK, FOX_HEADS // 2, 2).transpose(0, 3, 1, 4, 2)
    crow = jnp.pad(crow, ((0, 0), (0, 0), (0, 0), (0, 6), (0, 0)))
    p3 = p16.reshape(nb, SEQ, PW)
    fox, fox_lse = _fox_fwd(p3, crow, name="fox_fwd")
    if late_weights is not None:
        w_kv, w_o = late_weights(fox_lse)

    dqkv3 = dqkv.reshape(nb, SEQ, 3 * DIL_W)
    dil, dil_lse = _dil_fwd(dqkv3, name="dil_fwd")

    mh = _rms_fwd(mem.reshape(nb * MEM_LEN, D_MODEL), mem_norm_g.reshape(1, D_MODEL), tm=nb * MEM_LEN, name="rms_mem")
    mkv = _matmul(mh, w_kv, out_dtype=BF16, tm=nb * MEM_LEN, tn=512, tk=D_MODEL, name="mem_kv")
    mkv3 = mkv.reshape(nb, MEM_LEN, 2 * MEM_W)
    mcfg = _mem_cfg()
    memo, mem_lse = _attn_fwd(mcfg, p3, mkv3, mkv3, out_cols=MEM_W, name="mem_fwd")

    fox2, dil2, memo2 = fox.reshape(t, FOX_W), dil.reshape(t, DIL_W), memo.reshape(t, MEM_W)
    y, dx2, dx2b, st = _out_loss(fox2, dil2, memo2, p16, w_o, x2d, tgt2d, final_norm_g.reshape(1, D_MODEL), tm=256,
                                 name="out_loss")

    g_wo = _matmul(y, dx2b, mode="tn", out_dtype=BF16, tm=1024, tn=512, tk=t, name="grad_w_out")
    datt, dgate = _dy_gate_bwd(dx2b, w_o, fox2, dil2, memo2, p16, tm=1024, tn=256, name="dy_gate_bwd")
    datt3 = datt.reshape(nb, SEQ, MIX_W)

    dfq, dfk, dfv, dcr = _fox_bwd(p3, crow, datt3, fox, fox_lse, do_off=0, name="fox_bwd")
    dcol = -dcr[:, :, :, :2, :].transpose(0, 2, 4, 1, 3).reshape(t, FOX_HEADS)
    dcol = jnp.pad(dcol, ((0, 0), (0, LANES - FOX_HEADS)))
    dflog, gb = _flog_bwd(dcol, flog, bpad, nb=nb, ts=256, name="flog_bwd")

    ddq, ddk, ddv = _dil_bwd(dqkv3, datt3, dil, dil_lse, tabs, do_off=_B1, name="dil_bwd")

    dmq, dmk, dmv = _attn_bwd(mcfg, p3, mkv3, mkv3, datt3, memo, mem_lse, out_cols=MEM_W, kv_cols=MEM_W, do_off=_B2,
                              name="mem_bwd")
    dmkv = jnp.concatenate([dmk, dmv], axis=-1).reshape(nb * MEM_LEN, 2 * MEM_W).astype(BF16)
    g_wkv = _matmul(mh, dmkv, mode="tn", out_dtype=BF16, tm=512, tn=512, tk=nb * MEM_LEN, name="grad_w_kv")
    _, gmn = _dh_rms_bwd(dmkv, w_kv, mem.reshape(nb * MEM_LEN, D_MODEL), mem_norm_g.reshape(1, D_MODEL), None,
                         tm=nb * MEM_LEN, tk=2 * MEM_W, name="mem_rms_bwd")

    flat = lambda a: a.reshape(t, -1)
    dp = jnp.concatenate([flat(dfq), flat(dfk), flat(dfv), dgate[:, :FOX_W], flat(ddq), flat(ddk), flat(ddv),
                          dgate[:, FOX_W:FOX_W + DIL_W], flat(dmq).astype(BF16), dgate[:, FOX_W + DIL_W:], dflog,
                          jnp.zeros((t, PWF - PW - LANES), BF16)], axis=1)
    g_wr = _matmul(h, dp, mode="tn", out_dtype=BF16, tm=D_MODEL, tn=512, tk=t, name="grad_w_in")
    gain = norm_g.reshape(1, D_MODEL)
    if start_reduce is not None:
        gain = gain + start_reduce(g_wr, g_wkv, g_wo)[0:1, 0:1]
    gx, gng = _dh_rms_bwd(dp, w_r, x2d, gain, dx2, tm=512, tk=PWF // 3, name="in_rms_bwd")

    gb_row = jnp.pad(gb[0:1, :], ((0, 0), (0, D_MODEL - LANES)))
    small = jnp.concatenate([gng[0:1], gmn[0:1], st[0:1], gb_row, st[1:2], jnp.zeros((3, D_MODEL), F32)], axis=0)
    return gx.reshape(nb, SEQ, D_MODEL), g_wr, g_wkv, g_wo, small


MESH = pl.DeviceIdType.MESH
ANY = pl.BlockSpec(memory_space=pl.ANY)


def _place():
    x, y, c = lax.axis_index("x"), lax.axis_index("y"), lax.axis_index("c")
    other_chips = [(1 - x, y), (x, 1 - y), (1 - x, 1 - y)]
    return x, y, c, other_chips


def _gather_weights(shards):
    n = len(shards)

    def body(*refs):
        in_refs, out_refs = refs[:n], refs[n:2 * n]
        send_sems, recv_sems = refs[2 * n:]
        x, y, c, chips = _place()
        me_chip = 2 * x + y
        sibling = (x, y, 1 - c)

        def half(ref, pc, rows):
            return ref.at[pl.ds(pc * (rows // 2), rows // 2), :]

        def rcopy(k, src, dst, to):
            return pltpu.make_async_remote_copy(src_ref=src, dst_ref=dst, send_sem=send_sems.at[k], recv_sem=recv_sems.at[k],
                                                device_id=to, device_id_type=MESH)

        sends = []
        for t in range(n):
            rows = shards[t].shape[0]
            for j, chip in enumerate(chips):
                cp = rcopy(6 * t + j, half(in_refs[t], c, rows), half(out_refs[t].at[me_chip], c, rows), (*chip, c))
                cp.start()
                sends.append(cp)
        for t in range(n):
            rows = shards[t].shape[0]
            for j, chip in enumerate(chips):
                slot = out_refs[t].at[2 * chip[0] + chip[1]]
                rcopy(6 * t + j, half(slot, c, rows), half(slot, c, rows), sibling).wait_recv()
                fw = rcopy(6 * t + 3 + j, half(slot, c, rows), half(slot, c, rows), sibling)
                fw.start()
                sends.append(fw)
        for t in range(n):
            rows = shards[t].shape[0]
            for j, chip in enumerate(chips):
                slot = out_refs[t].at[2 * chip[0] + chip[1]]
                rcopy(6 * t + 3 + j, half(slot, 1 - c, rows), half(slot, 1 - c, rows), sibling).wait_recv()
        for cp in sends:
            cp.wait_send()

    return pl.pallas_call(
        body,
        out_shape=tuple(jax.ShapeDtypeStruct((N_CHIPS,) + s.shape, s.dtype) for s in shards),
        in_specs=[ANY] * n,
        out_specs=tuple([ANY] * n),
        scratch_shapes=[pltpu.SemaphoreType.DMA((6 * n,)), pltpu.SemaphoreType.DMA((6 * n,))],
        name="gather_weights",
    )(*shards)


def _pair_exchange(gs):
    n = len(gs)

    def body(*refs):
        g_refs, r_refs = refs[:n], refs[n:2 * n]
        send_sems, recv_sems = refs[2 * n:]
        x, y, c, _ = _place()
        cps = []
        for t in range(n):
            hr = gs[t].shape[1] // 2
            cp = pltpu.make_async_remote_copy(src_ref=g_refs[t].at[:, pl.ds((1 - c) * hr, hr), :], dst_ref=r_refs[t],
                                              send_sem=send_sems.at[t], recv_sem=recv_sems.at[t],
                                              device_id=(x, y, 1 - c), device_id_type=MESH)
            cp.start()
            cps.append(cp)
        for cp in cps:
            cp.wait()

    return pl.pallas_call(
        body,
        out_shape=tuple(jax.ShapeDtypeStruct((g.shape[0], g.shape[1] // 2, g.shape[2]), g.dtype) for g in gs),
        in_specs=[ANY] * n,
        out_specs=tuple([ANY] * n),
        scratch_shapes=[pltpu.SemaphoreType.DMA((n,)), pltpu.SemaphoreType.DMA((n,))],
        name="pair_exchange",
    )(*gs)


def _chip_exchange(ps):
    n = len(ps)

    def body(*refs):
        p_refs, o_refs = refs[:n], refs[n:2 * n]
        send_sems, recv_sems = refs[2 * n:]
        x, y, c, chips = _place()
        me_chip = 2 * x + y
        cps = []
        for t in range(n):
            for j, chip in enumerate(chips):
                cp = pltpu.make_async_remote_copy(src_ref=p_refs[t].at[2 * chip[0] + chip[1]], dst_ref=o_refs[t].at[me_chip],
                                                  send_sem=send_sems.at[3 * t + j], recv_sem=recv_sems.at[3 * t + j],
                                                  device_id=(*chip, c), device_id_type=MESH)
                cp.start()
                cps.append(cp)
        for cp in cps:
            cp.wait()

    return pl.pallas_call(
        body,
        out_shape=tuple(jax.ShapeDtypeStruct(p.shape, p.dtype) for p in ps),
        in_specs=[ANY] * n,
        out_specs=tuple([ANY] * n),
        scratch_shapes=[pltpu.SemaphoreType.DMA((3 * n,)), pltpu.SemaphoreType.DMA((3 * n,))],
        name="chip_exchange",
    )(*ps)


_HBM = pl.BlockSpec(memory_space=pltpu.HBM)
_SEM = pl.BlockSpec(memory_space=pltpu.SEMAPHORE)
_DATAFLOW = pltpu.SideEffectType.DATAFLOW_SIDE_EFFECTING


def _chip_copies(p_refs, land_refs, send_sems, recv_sems):
    x, y, c, chips = _place()
    me_chip = 2 * x + y
    return [pltpu.make_async_remote_copy(src_ref=p_refs[t].at[2 * chip[0] + chip[1]], dst_ref=land_refs[t].at[me_chip],
                                         send_sem=send_sems.at[3 * t + j], recv_sem=recv_sems.at[3 * t + j],
                                         device_id=(*chip, c), device_id_type=MESH)
            for t in range(len(p_refs)) for j, chip in enumerate(chips)]


def _chip_exchange_start(ps):
    n = len(ps)

    def body(*refs):
        p_refs, land_refs = refs[:n], refs[n:2 * n]
        send_sems, recv_sems = refs[2 * n:2 * n + 2]
        token = refs[-1]
        for cp in _chip_copies(p_refs, land_refs, send_sems, recv_sems):
            cp.start()
        token[...] = jnp.zeros_like(token)

    hbm = [pltpu.HBM(p.shape, p.dtype) for p in ps]
    args = [pltpu.with_memory_space_constraint(p, pltpu.HBM) for p in ps]
    args += [pltpu.with_memory_space_constraint(lax.empty(p.shape, p.dtype), pltpu.HBM) for p in ps]
    out = pl.pallas_call(
        body,
        name="chip_exchange_start",
        out_shape=(pltpu.SemaphoreType.DMA((3 * n,)), pltpu.SemaphoreType.DMA((3 * n,)), *hbm, *hbm,
                   jax.ShapeDtypeStruct((8, LANES), F32)),
        in_specs=[_HBM] * (2 * n),
        out_specs=(_SEM, _SEM, *([_HBM] * (2 * n)), pl.BlockSpec(memory_space=pltpu.VMEM)),
        input_output_aliases={i: 2 + i for i in range(2 * n)},
        compiler_params=pltpu.CompilerParams(has_side_effects=_DATAFLOW),
    )(*args)
    return out[0], out[1], out[2:2 + n], out[2 + n:2 + 2 * n], out[-1]


def _chip_exchange_wait(send_sems, recv_sems, p_thru, land_thru, after):
    n = len(p_thru)

    def body(*refs):
        p_refs, land_refs = refs[:n], refs[n:2 * n]
        ssem, rsem = refs[2 * n:2 * n + 2]
        for cp in _chip_copies(p_refs, land_refs, ssem, rsem):
            cp.wait_send()
            cp.wait_recv()

    hbm = [pltpu.HBM(p.shape, p.dtype) for p in p_thru]
    out = pl.pallas_call(
        body,
        name="chip_exchange_wait",
        out_shape=(*hbm, *hbm),
        in_specs=[_HBM] * (2 * n) + [_SEM, _SEM, ANY],
        out_specs=tuple([_HBM] * (2 * n)),
        input_output_aliases={i: i for i in range(2 * n)},
        compiler_params=pltpu.CompilerParams(has_side_effects=_DATAFLOW),
    )(*p_thru, *land_thru, send_sems, recv_sems, after)
    return out[:n], out[n:]


def _shard_copies(s_refs, land_refs, send_sems, recv_sems):
    x, y, c, chips = _place()
    me_chip = 2 * x + y
    return [pltpu.make_async_remote_copy(src_ref=s_refs[t], dst_ref=land_refs[t].at[me_chip],
                                         send_sem=send_sems.at[3 * t + j], recv_sem=recv_sems.at[3 * t + j],
                                         device_id=(*chip, c), device_id_type=MESH)
            for t in range(len(s_refs)) for j, chip in enumerate(chips)]


def _gather_late_start(shards):
    n = len(shards)

    def body(*refs):
        s_refs, land_refs = refs[:n], refs[n:2 * n]
        send_sems, recv_sems = refs[2 * n:2 * n + 2]
        token = refs[-1]
        for cp in _shard_copies(s_refs, land_refs, send_sems, recv_sems):
            cp.start()
        token[...] = jnp.zeros_like(token)

    lands = [(N_CHIPS,) + s.shape for s in shards]
    args = [pltpu.with_memory_space_constraint(s, pltpu.HBM) for s in shards]
    args += [pltpu.with_memory_space_constraint(lax.empty(shp, s.dtype), pltpu.HBM) for shp, s in zip(lands, shards)]
    out = pl.pallas_call(
        body,
        name="gather_late_start",
        out_shape=(pltpu.SemaphoreType.DMA((3 * n,)), pltpu.SemaphoreType.DMA((3 * n,)),
                   *[pltpu.HBM(s.shape, s.dtype) for s in shards], *[pltpu.HBM(shp, s.dtype) for shp, s in zip(lands, shards)],
                   jax.ShapeDtypeStruct((8, LANES), F32)),
        in_specs=[_HBM] * (2 * n),
        out_specs=(_SEM, _SEM, *([_HBM] * (2 * n)), pl.BlockSpec(memory_space=pltpu.VMEM)),
        input_output_aliases={i: 2 + i for i in range(2 * n)},
        compiler_params=pltpu.CompilerParams(has_side_effects=_DATAFLOW),
    )(*args)
    return out[0], out[1], out[2:2 + n], out[2 + n:2 + 2 * n], out[-1]


def _gather_late_wait(send_sems, recv_sems, s_thru, land_thru, after):
    n = len(s_thru)

    def body(*refs):
        s_refs, land_refs = refs[:n], refs[n:2 * n]
        ssem, rsem = refs[2 * n:2 * n + 2]
        for cp in _shard_copies(s_refs, land_refs, ssem, rsem):
            cp.wait_send()
            cp.wait_recv()

    out = pl.pallas_call(
        body,
        name="gather_late_wait",
        out_shape=(*[pltpu.HBM(s.shape, s.dtype) for s in s_thru], *[pltpu.HBM(l.shape, l.dtype) for l in land_thru]),
        in_specs=[_HBM] * (2 * n) + [_SEM, _SEM, ANY],
        out_specs=tuple([_HBM] * (2 * n)),
        input_output_aliases={i: i for i in range(2 * n)},
        compiler_params=pltpu.CompilerParams(has_side_effects=_DATAFLOW),
    )(*s_thru, *land_thru, send_sems, recv_sems, after)
    return out[:n], out[n:]


def _pair_swap(rs):
    n = len(rs)

    def body(*refs):
        r_refs, o_refs = refs[:n], refs[n:2 * n]
        send_sems, recv_sems = refs[2 * n:]
        x, y, c, _ = _place()
        cps = []
        for t in range(n):
            cp = pltpu.make_async_remote_copy(src_ref=r_refs[t], dst_ref=o_refs[t], send_sem=send_sems.at[t],
                                              recv_sem=recv_sems.at[t], device_id=(x, y, 1 - c), device_id_type=MESH)
            cp.start()
            cps.append(cp)
        for cp in cps:
            cp.wait()

    return pl.pallas_call(
        body,
        out_shape=tuple(jax.ShapeDtypeStruct(r.shape, r.dtype) for r in rs),
        in_specs=[ANY] * n,
        out_specs=tuple([ANY] * n),
        scratch_shapes=[pltpu.SemaphoreType.DMA((n,)), pltpu.SemaphoreType.DMA((n,))],
        name="pair_swap",
    )(*rs)


N_DEV = 8
LOSS_ROW = 4


def _small_allreduce(small):
    def body(s_ref, o_ref, all_ref, send_sems, recv_sems):
        x, y, c, _ = _place()
        me = 4 * x + 2 * y + c
        all_ref[me] = s_ref[...]
        cps = []
        for k in range(1, N_DEV):
            peer = tuple(1 - p if (k >> s) & 1 else p for p, s in ((x, 2), (y, 1), (c, 0)))
            cp = pltpu.make_async_remote_copy(src_ref=s_ref, dst_ref=all_ref.at[me], send_sem=send_sems.at[k - 1],
                                              recv_sem=recv_sems.at[k - 1], device_id=peer, device_id_type=MESH)
            cp.start()
            cps.append(cp)
        for cp in cps:
            cp.wait()
        tot = all_ref[0]
        for d in range(1, N_DEV):
            tot = tot + all_ref[d]
        o_ref[...] = tot
        o_ref[LOSS_ROW:LOSS_ROW + 1, :] = jnp.broadcast_to(jnp.sum(tot[LOSS_ROW:LOSS_ROW + 1, :], axis=1, keepdims=True),
                                                          (1, tot.shape[1]))

    vm = pl.BlockSpec(memory_space=pltpu.VMEM)
    return pl.pallas_call(
        body,
        out_shape=jax.ShapeDtypeStruct(small.shape, small.dtype),
        in_specs=[vm],
        out_specs=vm,
        scratch_shapes=[pltpu.VMEM((N_DEV,) + small.shape, small.dtype), pltpu.SemaphoreType.DMA((N_DEV - 1,)),
                        pltpu.SemaphoreType.DMA((N_DEV - 1,))],
        name="small_allreduce",
    )(small)


def _sum_pair(g, recv, cidx, *, tr, name):
    n, hr, cols = recv.shape
    nr = hr // tr

    def body(c_ref, g_ref, r_ref, o_ref):
        o_ref[...] = (g_ref[...].astype(F32) + r_ref[...].astype(F32)).astype(o_ref.dtype)

    grid_spec = pltpu.PrefetchScalarGridSpec(
        num_scalar_prefetch=1,
        grid=(n, nr),
        in_specs=[pl.BlockSpec((None, tr, cols), lambda k, i, c_ref: (k, c_ref[0] * nr + i, 0)),
                  pl.BlockSpec((None, tr, cols), lambda k, i, c_ref: (k, i, 0))],
        out_specs=pl.BlockSpec((None, tr, cols), lambda k, i, c_ref: (k, i, 0)),
    )
    return pl.pallas_call(body, out_shape=jax.ShapeDtypeStruct(recv.shape, BF16), grid_spec=grid_spec,
                          compiler_params=_cparams(), name=name)(cidx, g, recv)


def _sum_chips(p, *, tr, name):
    _, rows, cols = p.shape

    def body(p_ref, o_ref):
        tot = p_ref[0].astype(F32)
        for k in range(1, N_CHIPS):
            tot = tot + p_ref[k].astype(F32)
        o_ref[...] = tot

    return pl.pallas_call(
        body,
        out_shape=jax.ShapeDtypeStruct((rows, cols), F32),
        grid=(rows // tr,),
        in_specs=[pl.BlockSpec((N_CHIPS, tr, cols), lambda i: (0, i, 0))],
        out_specs=pl.BlockSpec((tr, cols), lambda i: (i, 0)),
        compiler_params=_cparams(),
        name=name,
    )(p)


def _adamw(w, g, m, v, *, tr, name):
    rows, cols = w.shape
    bc1 = 1.0 / (1.0 - ADAM_B1 ** ADAM_STEP)
    bc2 = 1.0 / (1.0 - ADAM_B2 ** ADAM_STEP)

    def body(w_ref, g_ref, m_ref, v_ref, d_ref, nm_ref, nv_ref):
        gv = g_ref[...]
        nm = ADAM_B1 * m_ref[...] + (1.0 - ADAM_B1) * gv
        nv = ADAM_B2 * v_ref[...] + (1.0 - ADAM_B2) * (gv * gv)
        d_ref[...] = -ADAM_LR * ((nm * bc1) / (jnp.sqrt(nv * bc2) + ADAM_EPS) + ADAM_WD * w_ref[...])
        nm_ref[...] = nm
        nv_ref[...] = nv

    spec = pl.BlockSpec((tr, cols), lambda i: (i, 0))
    sd = jax.ShapeDtypeStruct((rows, cols), F32)
    return pl.pallas_call(body, out_shape=(sd, sd, sd), grid=(rows // tr,), in_specs=[spec] * 4, out_specs=(spec,) * 3,
                          compiler_params=_cparams(), name=name)(w, g, m, v)


def _adamw_halves(w, own, sib, cidx, m, v, *, tr, name):
    rows, cols = w.shape
    hr = own.shape[0]
    nr = hr // tr
    assert rows == 2 * hr and hr % tr == 0
    bc1 = 1.0 / (1.0 - ADAM_B1 ** ADAM_STEP)
    bc2 = 1.0 / (1.0 - ADAM_B2 ** ADAM_STEP)

    def body(c_ref, w_ref, o_ref, s_ref, m_ref, v_ref, g_ref, d_ref, nm_ref, nv_ref):
        mine = (pl.program_id(0) // nr) == c_ref[0]
        gv = jnp.where(mine, o_ref[...], s_ref[...])
        nm = ADAM_B1 * m_ref[...] + (1.0 - ADAM_B1) * gv
        nv = ADAM_B2 * v_ref[...] + (1.0 - ADAM_B2) * (gv * gv)
        g_ref[...] = gv
        d_ref[...] = -ADAM_LR * ((nm * bc1) / (jnp.sqrt(nv * bc2) + ADAM_EPS) + ADAM_WD * w_ref[...])
        nm_ref[...] = nm
        nv_ref[...] = nv

    full = pl.BlockSpec((tr, cols), lambda i, c_ref: (i, 0))
    half = pl.BlockSpec((tr, cols), lambda i, c_ref: (i % nr, 0))
    sd = jax.ShapeDtypeStruct((rows, cols), F32)
    grid_spec = pltpu.PrefetchScalarGridSpec(num_scalar_prefetch=1, grid=(rows // tr,), in_specs=[full, half, half, full, full],
                                             out_specs=(full,) * 4)
    return pl.pallas_call(body, out_shape=(sd,) * 4, grid_spec=grid_spec, compiler_params=_cparams(), name=name)(
        cidx, w, own, sib, m, v)


def _pack_small(norm, mem_norm, final_norm, b_forget):
    rows = [norm.reshape(1, D_MODEL), mem_norm.reshape(1, D_MODEL), final_norm.reshape(1, D_MODEL),
            jnp.pad(b_forget.reshape(1, FOX_HEADS), ((0, 0), (0, D_MODEL - FOX_HEADS))), jnp.zeros((4, D_MODEL), F32)]
    return jnp.concatenate(rows, axis=0)


def _unpack_small(a):
    return a[0:1], a[3:4, :FOX_HEADS], a[1:2], a[2]


def kernel(x, mem, norm_g, w_in, b_forget, mem_norm_g, w_mem_kv, w_out, final_norm_g, loss_target, m_norm_g, m_w_in, m_b_forget, m_mem_norm_g, m_w_mem_kv, m_w_out, m_final_norm_g, v_norm_g, v_w_in, v_b_forget, v_mem_norm_g, v_w_mem_kv, v_w_out, v_final_norm_g):
    core = lax.axis_index("c").astype(jnp.int32)
    me_chip = (2 * lax.axis_index("x") + lax.axis_index("y")).astype(jnp.int32)
    cidx = core.reshape(1)

    def own_slot(arr, own):
        return lax.dynamic_update_slice(arr, own[None].astype(arr.dtype), (me_chip,) + (0,) * own.ndim)

    win_b, late = w_in[0].astype(BF16), [w_mem_kv[0].astype(BF16), w_out[0].astype(BF16)]
    g_in, = _gather_weights([win_b])
    g_in, late = lax.optimization_barrier((own_slot(g_in, win_b), late))
    w_r = _rearrange_w_in([g_in[k] for k in range(N_CHIPS)])
    *late_flight, early_token = _gather_late_start(late)

    def late_weights(after):
        shards, landed = _gather_late_wait(*late_flight, after)
        g_kv, g_out = (own_slot(g, s) for g, s in zip(landed, shards))
        return g_kv.reshape(D_MODEL, 2 * MEM_W), g_out.reshape(MIX_W, D_MODEL)

    trs = (128, 128, 256)
    names = ("w_in", "w_mem_kv", "w_out")
    flight = []

    def start_reduce(g_wr, g_wkv, g_wo):
        slabs = [g_wr[None],
                 g_wkv.reshape(N_CHIPS, D_MODEL // N_CHIPS, 2 * MEM_W),
                 g_wo.reshape(N_CHIPS, MIX_W // N_CHIPS, D_MODEL)]
        recv = _pair_exchange(slabs)
        pair = [_sum_pair(g, r, cidx, tr=tr, name=f"sum_pair_{nm}") for g, r, tr, nm in zip(slabs, recv, trs, names)]
        pair[0] = _w_in_grad_slabs(pair[0][0])
        *handles, token = _chip_exchange_start(pair)
        flight.extend(handles)
        return token

    gx, g_wr, g_wkv, g_wo, small = _local_grads(x, mem, norm_g, w_r, b_forget, mem_norm_g, None, None, final_norm_g, loss_target,
                                                start_reduce=start_reduce, early_token=early_token, late_weights=late_weights)

    send_sems, recv_sems, pair, land = flight
    pair, landed = _chip_exchange_wait(send_sems, recv_sems, pair, land, small)
    got = [lax.dynamic_update_slice(g, lax.dynamic_slice(p, (me_chip, 0, 0), (1,) + p.shape[1:]), (me_chip, 0, 0))
           for g, p in zip(landed, pair)]
    red = [_sum_chips(p, tr=tr, name=f"sum_chips_{nm}") for p, tr, nm in zip(got, trs, names)]
    sib = _pair_swap(red)

    outs = {}
    for nm, r, s, w, m, v, tr in zip(names, red, sib, (w_in, w_mem_kv, w_out), (m_w_in, m_w_mem_kv, m_w_out),
                                     (v_w_in, v_w_mem_kv, v_w_out), trs):
        outs[nm] = tuple(a[None] for a in _adamw_halves(w[0], r, s, cidx, m[0], v[0], tr=tr, name=f"adamw_{nm}"))

    gsum = _small_allreduce(small)
    sd, sm, sv = _adamw(_pack_small(norm_g, mem_norm_g, final_norm_g, b_forget), gsum,
                        _pack_small(m_norm_g, m_mem_norm_g, m_final_norm_g, m_b_forget),
                        _pack_small(v_norm_g, v_mem_norm_g, v_final_norm_g, v_b_forget), tr=8, name="adamw_small")
    loss = gsum[LOSS_ROW, 0]

    def group(i, small_arr):
        ng, bf, mg, fg = _unpack_small(small_arr)
        return (ng, outs["w_in"][i], bf, mg, outs["w_mem_kv"][i], outs["w_out"][i], fg)

    return (loss, gx, *group(0, gsum), *group(1, sd), *group(2, sm), *group(3, sv))
```

```python
import functools
import math

import jax
import jax.numpy as jnp
from jax import lax
from jax.experimental import pallas as pl
from jax.experimental.pallas import tpu as pltpu

F32 = jnp.float32
BF16 = jnp.bfloat16

D_MODEL = 1024
SEQ = 2048
HEAD_DIM = 64
FOX_HEADS = 12
DIL_HEADS = 12
MEM_HEADS = 4
MEM_HEAD_DIM = 128
MEM_LEN = 256
FOX_W = FOX_HEADS * HEAD_DIM
DIL_W = DIL_HEADS * HEAD_DIM
MEM_W = MEM_HEADS * MEM_HEAD_DIM
MIX_W = FOX_W + DIL_W + MEM_W
DILATIONS = ((128, 1), (512, 4), (2048, 16))
ROPE_THETA = 500000.0
ROPE_DIM = HEAD_DIM // 4
RMS_EPS = 1e-6
NEG_INF = -1e30
IN_SIZES = [FOX_W] * 4 + [FOX_HEADS] + [DIL_W] * 4 + [MEM_W] * 2
IN_W = sum(IN_SIZES)

ADAM_LR = 0.001
ADAM_B1 = 0.9
ADAM_B2 = 0.999
ADAM_EPS = 1e-08
ADAM_WD = 0.01
ADAM_STEP = 10

LANES = 128
N_CHIPS = 4
PW = 7168
PWF = PW + 4 * LANES
C_FQ, C_FK, C_FV, C_FG = 0, 768, 1536, 2304
C_DQ, C_DK, C_DV, C_DG = 3072, 3840, 4608, 5376
C_MQ, C_MG = 6144, 6656
VMEM_LIMIT = 48 * 1024 * 1024


def _cparams(**kw):
    return pltpu.CompilerParams(vmem_limit_bytes=VMEM_LIMIT, **kw)


def _matmul(a, b, *, out_dtype, tm, tn, tk, name, mode="nn"):
    if mode == "tn":
        (kdim, m), n = a.shape, b.shape[1]
        a_spec = pl.BlockSpec((tk, tm), lambda i, j, k: (k, i))
        b_spec = pl.BlockSpec((tk, tn), lambda i, j, k: (k, j))
        dims = _T0
    elif mode == "nt":
        (m, kdim), n = a.shape, b.shape[0]
        a_spec = pl.BlockSpec((tm, tk), lambda i, j, k: (i, k))
        b_spec = pl.BlockSpec((tn, tk), lambda i, j, k: (j, k))
        dims = _NT
    else:
        (m, kdim), n = a.shape, b.shape[1]
        a_spec = pl.BlockSpec((tm, tk), lambda i, j, k: (i, k))
        b_spec = pl.BlockSpec((tk, tn), lambda i, j, k: (k, j))
        dims = (((1,), (0,)), ((), ()))
    nk = kdim // tk
    assert m % tm == 0 and n % tn == 0 and kdim % tk == 0

    def body(a_ref, b_ref, o_ref, *scratch):
        prod = lax.dot_general(a_ref[...], b_ref[...], dims, preferred_element_type=F32)
        if nk == 1:
            o_ref[...] = prod.astype(o_ref.dtype)
            return
        acc_ref, = scratch
        k = pl.program_id(2)

        @pl.when(k == 0)
        def _():
            acc_ref[...] = prod

        @pl.when(k > 0)
        def _():
            acc_ref[...] += prod

        @pl.when(k == nk - 1)
        def _():
            o_ref[...] = acc_ref[...].astype(o_ref.dtype)

    return pl.pallas_call(
        body,
        out_shape=jax.ShapeDtypeStruct((m, n), out_dtype),
        grid=(m // tm, n // tn, nk),
        in_specs=[a_spec, b_spec],
        out_specs=pl.BlockSpec((tm, tn), lambda i, j, k: (i, j)),
        scratch_shapes=[pltpu.VMEM((tm, tn), F32)] if nk > 1 else [],
        compiler_params=_cparams(dimension_semantics=("parallel", "parallel", "arbitrary")),
        name=name,
    )(a, b)


def _rms_fwd(x, g, *, tm, name):
    t, d = x.shape

    def body(x_ref, g_ref, h_ref):
        xv = x_ref[...]
        r = lax.rsqrt(jnp.mean(xv * xv, axis=-1, keepdims=True) + RMS_EPS)
        h_ref[...] = (xv * r * g_ref[...]).astype(h_ref.dtype)

    return pl.pallas_call(
        body,
        out_shape=jax.ShapeDtypeStruct((t, d), BF16),
        grid=(t // tm,),
        in_specs=[pl.BlockSpec((tm, d), lambda i: (i, 0)), pl.BlockSpec((1, d), lambda i: (0, 0))],
        out_specs=pl.BlockSpec((tm, d), lambda i: (i, 0)),
        compiler_params=_cparams(),
        name=name,
    )(x, g)


def _rope_tables():
    half = ROPE_DIM // 2
    pos = jnp.arange(SEQ, dtype=F32)
    inv_freq = 1.0 / (ROPE_THETA ** (jnp.arange(0, ROPE_DIM, 2, dtype=F32) / ROPE_DIM))
    ang = pos[:, None] * inv_freq[None, :]
    cos, sin = jnp.cos(ang), jnp.sin(ang)
    one = jnp.ones((SEQ, HEAD_DIM - ROPE_DIM), F32)
    zero = jnp.zeros((SEQ, HEAD_DIM - ROPE_DIM), F32)
    zh = jnp.zeros((SEQ, half), F32)
    c = jnp.concatenate([cos, cos, one], axis=1)
    s1 = jnp.concatenate([zh, sin, zero], axis=1)
    s2 = jnp.concatenate([-sin, zh, zero], axis=1)
    rep = LANES // HEAD_DIM
    return jnp.tile(c, (1, rep)), jnp.tile(s1, (1, rep)), jnp.tile(s2, (1, rep))


def _rope_apply(t, c, s1, s2, transpose=False):
    n = t.shape[-1]
    rep = n // LANES
    c, s1, s2 = (jnp.tile(u, (1, rep)) for u in (c, s1, s2))
    half = ROPE_DIM // 2
    if not transpose:
        return t * c + pltpu.roll(t, half, 1) * s1 + pltpu.roll(t, n - half, 1) * s2
    return t * c + pltpu.roll(t * s1, n - half, 1) + pltpu.roll(t * s2, half, 1)


def _proj(h, w, tabs, *, n, tm, tn, name):
    t, d = h.shape
    assert C_DQ % tn == 0 and (C_DV - C_DQ) % tn == 0 and (C_DG - C_DQ) % tn == 0
    rope_lo, rope_hi, dil_hi = C_DQ // tn, C_DV // tn, C_DG // tn
    s_blocks = SEQ // tm

    def body(h_ref, w_ref, c_ref, s1_ref, s2_ref, o_ref, f_ref):
        j = pl.program_id(1)
        acc = jnp.dot(h_ref[...], w_ref[...], preferred_element_type=F32)
        is_rope = jnp.logical_and(j >= rope_lo, j < rope_hi)

        @pl.when(is_rope)
        def _():
            r = _rope_apply(acc, c_ref[...], s1_ref[...], s2_ref[...])
            o_ref[...] = r.astype(o_ref.dtype)
            f_ref[...] = r

        @pl.when(jnp.logical_not(is_rope))
        def _():
            o_ref[...] = acc.astype(o_ref.dtype)

        @pl.when(jnp.logical_and(j >= rope_hi, j < dil_hi))
        def _():
            f_ref[...] = acc

    tab_spec = pl.BlockSpec((tm, LANES), lambda i, j: (i % s_blocks, 0))
    f_spec = pl.BlockSpec((tm, tn), lambda i, j: (i, jnp.clip(j - rope_lo, 0, dil_hi - rope_lo - 1)))
    return pl.pallas_call(
        body,
        out_shape=(jax.ShapeDtypeStruct((t, n), BF16), jax.ShapeDtypeStruct((t, 3 * DIL_W), F32)),
        grid=(t // tm, n // tn),
        in_specs=[pl.BlockSpec((tm, d), lambda i, j: (i, 0)), pl.BlockSpec((d, tn), lambda i, j: (0, j)),
                  tab_spec, tab_spec, tab_spec],
        out_specs=(pl.BlockSpec((tm, tn), lambda i, j: (i, j)), f_spec),
        compiler_params=_cparams(dimension_semantics=("parallel", "arbitrary")),
        name=name,
    )(h, w, *tabs)


def _split3(x):
    hi = x.astype(BF16)
    r1 = x - hi.astype(F32)
    mid = r1.astype(BF16)
    lo = (r1 - mid.astype(F32)).astype(BF16)
    return hi, mid, lo


def _dot3(sel, x, sel_is_lhs):
    out = None
    for piece in _split3(x):
        t = jnp.dot(sel, piece, preferred_element_type=F32) if sel_is_lhs else jnp.dot(piece, sel, preferred_element_type=F32)
        out = t if out is None else out + t
    return out


def _flog_fwd(flog, bpad, *, nb, ts, name):
    ns = SEQ // ts

    def body(f_ref, b_ref, c_ref, carry_ref):
        s = pl.program_id(1)

        @pl.when(s == 0)
        def _():
            carry_ref[...] = jnp.zeros_like(carry_ref)

        z = f_ref[...] + b_ref[...]
        logf = jnp.minimum(z, 0.0) - jnp.log(1.0 + jnp.exp(-jnp.abs(z)))
        r = lax.broadcasted_iota(jnp.int32, (ts, ts), 0)
        c = lax.broadcasted_iota(jnp.int32, (ts, ts), 1)
        tri = jnp.where(r >= c, 1.0, 0.0).astype(BF16)
        cs = _dot3(tri, logf, True) + carry_ref[0:1, :]
        carry_ref[...] = jnp.broadcast_to(cs[ts - 1:ts, :], carry_ref.shape)
        c_ref[...] = cs

    return pl.pallas_call(
        body,
        out_shape=jax.ShapeDtypeStruct((nb * SEQ, LANES), F32),
        grid=(nb, ns),
        in_specs=[pl.BlockSpec((ts, LANES), lambda b, s: (b * ns + s, 0)), pl.BlockSpec((1, LANES), lambda b, s: (0, 0))],
        out_specs=pl.BlockSpec((ts, LANES), lambda b, s: (b * ns + s, 0)),
        scratch_shapes=[pltpu.VMEM((8, LANES), F32)],
        compiler_params=_cparams(dimension_semantics=("parallel", "arbitrary")),
        name=name,
    )(flog, bpad)


def _flog_bwd(dcol, flog, bpad, *, nb, ts, name):
    ns = SEQ // ts

    def body(d_ref, f_ref, b_ref, o_ref, gb_ref, carry_ref):
        bi = pl.program_id(0)
        s = pl.program_id(1)

        @pl.when(s == 0)
        def _():
            carry_ref[...] = jnp.zeros_like(carry_ref)

        @pl.when(jnp.logical_and(bi == 0, s == 0))
        def _():
            gb_ref[...] = jnp.zeros_like(gb_ref)

        r = lax.broadcasted_iota(jnp.int32, (ts, ts), 0)
        c = lax.broadcasted_iota(jnp.int32, (ts, ts), 1)
        tri = jnp.where(r <= c, 1.0, 0.0).astype(BF16)
        rc = _dot3(tri, d_ref[...], True) + carry_ref[0:1, :]
        carry_ref[...] = jnp.broadcast_to(rc[0:1, :], carry_ref.shape)
        z = f_ref[...] + b_ref[...]
        dz = rc / (1.0 + jnp.exp(z))
        o_ref[...] = dz.astype(o_ref.dtype)
        gb_ref[...] += jnp.broadcast_to(jnp.sum(dz, axis=0, keepdims=True), gb_ref.shape)

    rev = lambda b, s: (b * ns + (ns - 1 - s), 0)
    return pl.pallas_call(
        body,
        out_shape=(jax.ShapeDtypeStruct((nb * SEQ, LANES), BF16), jax.ShapeDtypeStruct((8, LANES), F32)),
        grid=(nb, ns),
        in_specs=[pl.BlockSpec((ts, LANES), rev), pl.BlockSpec((ts, LANES), rev), pl.BlockSpec((1, LANES), lambda b, s: (0, 0))],
        out_specs=(pl.BlockSpec((ts, LANES), rev), pl.BlockSpec((8, LANES), lambda b, s: (0, 0))),
        scratch_shapes=[pltpu.VMEM((8, LANES), F32)],
        compiler_params=_cparams(dimension_semantics=("arbitrary", "arbitrary")),
        name=name,
    )(dcol, flog, bpad)


MEM_TQ = 256
MEM_SET = 4
MEM_SCALE = 1.0 / math.sqrt(MEM_HEAD_DIM)
assert MEM_HEAD_DIM == LANES and SEQ % (MEM_TQ * MEM_SET) == 0


def _head_masks(nh):
    lane = lax.broadcasted_iota(jnp.int32, (1, LANES), 1)
    return [None] if nh == 1 else [lane < HEAD_DIM, lane >= HEAD_DIM]


def _mem_specs(qoff):
    qspec = pl.BlockSpec((None, SEQ, LANES), lambda b, j: (b, 0, qoff + j))
    kspec = pl.BlockSpec((None, MEM_LEN, LANES), lambda b, j: (b, 0, j))
    vspec = pl.BlockSpec((None, MEM_LEN, LANES), lambda b, j: (b, 0, MEM_HEADS + j))
    ospec = pl.BlockSpec((None, SEQ, LANES), lambda b, j: (b, 0, j))
    return qspec, kspec, vspec, ospec


def _mem_rows(g):
    return [pl.ds(pl.multiple_of((MEM_SET * g + a) * MEM_TQ, MEM_TQ), MEM_TQ) for a in range(MEM_SET)]


def _mem_fwd(p3, mkv3, *, qoff, name):
    nb = p3.shape[0]

    def body(q_ref, k_ref, v_ref, o_ref, lse_ref):
        kb, vb = k_ref[...], v_ref[...]

        def qset(g, c):
            rows = _mem_rows(g)
            ss = [lax.dot_general(q_ref[r, :] * MEM_SCALE, kb, _NT, preferred_element_type=F32) for r in rows]
            for r, s in zip(rows, ss):
                m = jnp.max(s, axis=1, keepdims=True)
                p = jnp.exp(s - m)
                l = jnp.sum(p, axis=1, keepdims=True)
                o_ref[r, :] = jnp.dot(p.astype(BF16), vb, preferred_element_type=F32) / l
                lse_ref[r, :] = jnp.broadcast_to(m + jnp.log(l), (MEM_TQ, LANES))
            return c

        lax.fori_loop(0, SEQ // MEM_TQ // MEM_SET, qset, 0)

    qspec, kspec, vspec, ospec = _mem_specs(qoff)
    osd = jax.ShapeDtypeStruct((nb, SEQ, MEM_W), F32)
    return pl.pallas_call(body, out_shape=(osd, osd), grid=(nb, MEM_HEADS), in_specs=[qspec, kspec, vspec],
                          out_specs=(ospec, ospec), compiler_params=_cparams(dimension_semantics=("parallel", "parallel")),
                          name=name)(p3, mkv3, mkv3)


def _mem_bwd(p3, mkv3, do, o, lse, *, qoff, do_off, name):
    nb = p3.shape[0]

    def body(q_ref, k_ref, v_ref, do_ref, o_ref, lse_ref, dq_ref, dk_ref, dv_ref):
        kb, vb = k_ref[...], v_ref[...]
        ks = kb * MEM_SCALE

        def qset(g, carry):
            dk, dv = carry
            work = []
            for r in _mem_rows(g):
                qs = q_ref[r, :] * MEM_SCALE
                dob = do_ref[r, :].astype(BF16)
                s = lax.dot_general(qs, kb, _NT, preferred_element_type=F32)
                dp = lax.dot_general(dob, vb, _NT, preferred_element_type=F32)
                work.append((r, qs, dob, s, dp))
            for r, qs, dob, s, dp in work:
                delta = jnp.sum(dob.astype(F32) * o_ref[r, :], axis=1, keepdims=True)
                p = jnp.exp(s - lse_ref[r, :][:, 0:1])
                ds = (p * (dp - delta)).astype(BF16)
                dq_ref[r, :] = jnp.dot(ds, ks, preferred_element_type=F32).astype(dq_ref.dtype)
                dk = dk + lax.dot_general(ds, qs, _T0, preferred_element_type=F32)
                dv = dv + lax.dot_general(p.astype(BF16), dob, _T0, preferred_element_type=F32)
            return dk, dv

        z = jnp.zeros((MEM_LEN, LANES), F32)
        dk, dv = lax.fori_loop(0, SEQ // MEM_TQ // MEM_SET, qset, (z, z))
        dk_ref[...] = dk
        dv_ref[...] = dv

    qspec, kspec, vspec, ospec = _mem_specs(qoff)
    dospec = pl.BlockSpec((None, SEQ, LANES), lambda b, j: (b, 0, do_off + j))
    kvo = pl.BlockSpec((None, MEM_LEN, LANES), lambda b, j: (b, 0, j))
    kvsd = jax.ShapeDtypeStruct((nb, MEM_LEN, MEM_W), F32)
    return pl.pallas_call(
        body, out_shape=(jax.ShapeDtypeStruct((nb, SEQ, MEM_W), BF16), kvsd, kvsd), grid=(nb, MEM_HEADS),
        in_specs=[qspec, kspec, vspec, dospec, ospec, ospec], out_specs=(ospec, kvo, kvo),
        compiler_params=_cparams(dimension_semantics=("parallel", "parallel")), name=name)(p3, mkv3, mkv3, do, o, lse)


BLK = 128
NBLK = SEQ // BLK
QK_SCALE = 1.0 / math.sqrt(HEAD_DIM)
DIL_STEPS = tuple(d for _, d in DILATIONS)
assert all(w // d == BLK for w, d in DILATIONS)
_T0 = (((0,), (0,)), ((), ()))
_NT = (((1,), (1,)), ((), ()))


def _stack_heads(a, masks):
    z = jnp.zeros_like(a)
    return jnp.concatenate([jnp.where(masks[0], a, z), jnp.where(masks[1], a, z)], axis=0)


def _tri_bias(lower):
    r = lax.broadcasted_iota(jnp.int32, (BLK, BLK), 0)
    c = lax.broadcasted_iota(jnp.int32, (BLK, BLK), 1)
    return jnp.where((c <= r) if lower else (c >= r), 0.0, NEG_INF).astype(F32)


def _dil_rows(r, i, d):
    start = r + i * (BLK * d)
    return pl.ds(start, BLK) if d == 1 else pl.ds(start, BLK, stride=d)


DIL_SET = 4


def _dil_sets(d, fn):
    nbk = SEQ // d // BLK
    if d == 1:
        def gbody(g, c):
            fn([(0, DIL_SET * g + a, None if a == 0 else True) for a in range(DIL_SET)])
            return c
        lax.fori_loop(0, nbk // DIL_SET, gbody, 0)
    elif nbk > 1:
        assert nbk == DIL_SET
        def rbody(r, c):
            fn([(r, i, i > 0) for i in range(nbk)])
            return c
        lax.fori_loop(0, d, rbody, 0)
    else:
        def rbody(rr, c):
            fn([(DIL_SET * rr + a, 0, False) for a in range(DIL_SET)])
            return c
        lax.fori_loop(0, d // DIL_SET, rbody, 0)


def _dil_key_tiles(r, i, d, has_prev, qrows, tri_cur, tri_prev):
    tiles = [(qrows, tri_cur)]
    if has_prev is None:
        tiles.append((_dil_rows(r, jnp.maximum(i - 1, 0), d), tri_prev + jnp.where(i > 0, 0.0, NEG_INF)))
    elif has_prev:
        tiles.append((_dil_rows(r, i - 1, d), tri_prev))
    return tiles


def _dil_fwd(qkv, *, name):
    nb = qkv.shape[0]
    ncol = DIL_W // LANES
    hd = HEAD_DIM

    def body(q_ref, k_ref, v_ref, o_ref, lse_ref, m_ref, l_ref, a_ref):
        masks = _head_masks(2)
        tri_cur, tri_prev = _tri_bias(True), _tri_bias(False)
        for pi, d in enumerate(DIL_STEPS):
            first, last = pi == 0, pi == len(DIL_STEPS) - 1

            def qset(blocks, d=d, first=first, last=last):
                work = []
                for r, i, has_prev in blocks:
                    qrows = _dil_rows(r, i, d)
                    qcat = _stack_heads((q_ref[qrows, :] * QK_SCALE).astype(BF16), masks)
                    ss, krs = [], []
                    for krows, bias in _dil_key_tiles(r, i, d, has_prev, qrows, tri_cur, tri_prev):
                        s = lax.dot_general(qcat, k_ref[krows, :].astype(BF16), _NT, preferred_element_type=F32)
                        ss.append((s[:BLK] + bias, s[BLK:] + bias))
                        krs.append(krows)
                    work.append((qrows, ss, krs))
                for qrows, ss, krs in work:
                    e0 = ss[0][0] if len(ss) == 1 else jnp.maximum(ss[0][0], ss[1][0])
                    e1 = ss[0][1] if len(ss) == 1 else jnp.maximum(ss[0][1], ss[1][1])
                    n0 = jnp.max(e0, axis=1, keepdims=True)
                    n1 = jnp.max(e1, axis=1, keepdims=True)
                    if not first:
                        mo, lo = m_ref[qrows, :], l_ref[qrows, :]
                        m0, m1 = mo[:, 0:1], mo[:, hd:hd + 1]
                        n0, n1 = jnp.maximum(n0, m0), jnp.maximum(n1, m1)
                        a0, a1 = jnp.exp(m0 - n0), jnp.exp(m1 - n1)
                    ps = [(jnp.exp(s0 - n0), jnp.exp(s1 - n1)) for s0, s1 in ss]
                    t0 = ps[0][0] if len(ps) == 1 else ps[0][0] + ps[1][0]
                    t1 = ps[0][1] if len(ps) == 1 else ps[0][1] + ps[1][1]
                    l0 = jnp.sum(t0, axis=1, keepdims=True)
                    l1 = jnp.sum(t1, axis=1, keepdims=True)
                    acc = None
                    for (p0, p1), krows in zip(ps, krs):
                        vcat = _stack_heads(v_ref[krows, :].astype(BF16), masks)
                        pv = jnp.dot(jnp.concatenate([p0, p1], axis=1).astype(BF16), vcat, preferred_element_type=F32)
                        acc = pv if acc is None else acc + pv
                    if not first:
                        l0 = l0 + a0 * lo[:, 0:1]
                        l1 = l1 + a1 * lo[:, hd:hd + 1]
                        acc = acc + a_ref[qrows, :] * jnp.where(masks[0], a0, a1)
                    if last:
                        o_ref[qrows, :] = acc / jnp.where(masks[0], l0, l1)
                        lse_ref[qrows, :] = jnp.where(masks[0], n0 + jnp.log(l0), n1 + jnp.log(l1))
                    else:
                        m_ref[qrows, :] = jnp.where(masks[0], n0, n1)
                        l_ref[qrows, :] = jnp.where(masks[0], l0, l1)
                        a_ref[qrows, :] = acc

            _dil_sets(d, qset)

    spec = lambda off: pl.BlockSpec((None, SEQ, LANES), lambda b, j: (b, 0, off + j))
    ospec = pl.BlockSpec((None, SEQ, LANES), lambda b, j: (b, 0, j))
    osd = jax.ShapeDtypeStruct((nb, SEQ, DIL_W), F32)
    return pl.pallas_call(
        body, out_shape=(osd, osd), grid=(nb, ncol),
        in_specs=[spec(0), spec(ncol), spec(2 * ncol)], out_specs=(ospec, ospec),
        scratch_shapes=[pltpu.VMEM((SEQ, LANES), F32)] * 3,
        compiler_params=_cparams(dimension_semantics=("parallel", "parallel")), name=name,
    )(qkv, qkv, qkv)


def _dil_bwd(qkv, do, o, lse, tabs, *, do_off, name):
    nb = qkv.shape[0]
    ncol = DIL_W // LANES
    hd = HEAD_DIM

    def body(q_ref, k_ref, v_ref, do_ref, o_ref, lse_ref, c_ref, s1_ref, s2_ref, dqo_ref, dko_ref, dvo_ref,
             dq_ref, dk_ref, dv_ref, dl_ref, dof_ref):
        masks = _head_masks(2)
        tri_cur, tri_prev = _tri_bias(True), _tri_bias(False)
        dq_ref[...] = jnp.zeros_like(dq_ref)
        dk_ref[...] = jnp.zeros_like(dk_ref)
        dv_ref[...] = jnp.zeros_like(dv_ref)

        def delta_body(i, c):
            rows = pl.ds(pl.multiple_of(i * BLK, BLK), BLK)
            dof = do_ref[rows, :].astype(F32)
            dof_ref[rows, :] = dof
            prod = dof * o_ref[rows, :]
            z = jnp.zeros_like(prod)
            dl_ref[rows, :] = jnp.where(masks[0], jnp.sum(jnp.where(masks[0], prod, z), axis=1, keepdims=True),
                                        jnp.sum(jnp.where(masks[1], prod, z), axis=1, keepdims=True))
            return c

        lax.fori_loop(0, NBLK, delta_body, 0)

        for d in DIL_STEPS:
            def qset(blocks, d=d):
                work = []
                for r, i, has_prev in blocks:
                    qrows = _dil_rows(r, i, d)
                    qcat = _stack_heads((q_ref[qrows, :] * QK_SCALE).astype(BF16), masks)
                    docat = _stack_heads(dof_ref[qrows, :].astype(BF16), masks)
                    tiles = []
                    for krows, bias in _dil_key_tiles(r, i, d, has_prev, qrows, tri_cur, tri_prev):
                        s = lax.dot_general(qcat, k_ref[krows, :].astype(BF16), _NT, preferred_element_type=F32)
                        dp = lax.dot_general(docat, v_ref[krows, :].astype(BF16), _NT, preferred_element_type=F32)
                        tiles.append((krows, s, dp, bias))
                    work.append((qrows, qcat, docat, tiles))
                for qrows, qcat, docat, tiles in work:
                    lseb, dlb = lse_ref[qrows, :], dl_ref[qrows, :]
                    lse0, lse1 = lseb[:, 0:1], lseb[:, hd:hd + 1]
                    dl0, dl1 = dlb[:, 0:1], dlb[:, hd:hd + 1]
                    dq = None
                    for krows, s, dp, bias in tiles:
                        p0 = jnp.exp(s[:BLK] + bias - lse0)
                        p1 = jnp.exp(s[BLK:] + bias - lse1)
                        ds0 = p0 * (dp[:BLK] - dl0)
                        ds1 = p1 * (dp[BLK:] - dl1)
                        ds0b, ds1b = ds0.astype(BF16), ds1.astype(BF16)
                        pcat = jnp.concatenate([p0.astype(BF16), p1.astype(BF16)], axis=0)
                        dscat = jnp.concatenate([ds0b, ds1b], axis=0)
                        dv_ref[krows, :] += lax.dot_general(pcat, docat, _T0, preferred_element_type=F32)
                        dk_ref[krows, :] += lax.dot_general(dscat, qcat, _T0, preferred_element_type=F32)
                        dsrow = jnp.concatenate([ds0b, ds1b], axis=1)
                        kcat = _stack_heads((k_ref[krows, :] * QK_SCALE).astype(BF16), masks)
                        t = jnp.dot(dsrow, kcat, preferred_element_type=F32)
                        dq = t if dq is None else dq + t
                    dq_ref[qrows, :] += dq

            _dil_sets(d, qset)

        def out_body(i, c):
            rows = pl.ds(pl.multiple_of(i * BLK, BLK), BLK)
            tab = (c_ref[rows, :], s1_ref[rows, :], s2_ref[rows, :])
            dqo_ref[rows, :] = _rope_apply(dq_ref[rows, :], *tab, transpose=True).astype(dqo_ref.dtype)
            dko_ref[rows, :] = _rope_apply(dk_ref[rows, :], *tab, transpose=True).astype(dko_ref.dtype)
            dvo_ref[rows, :] = dv_ref[rows, :].astype(dvo_ref.dtype)
            return c

        lax.fori_loop(0, NBLK, out_body, 0)

    spec = lambda off: pl.BlockSpec((None, SEQ, LANES), lambda b, j: (b, 0, off + j))
    ospec = pl.BlockSpec((None, SEQ, LANES), lambda b, j: (b, 0, j))
    tspec = pl.BlockSpec((SEQ, LANES), lambda b, j: (0, 0))
    osd = jax.ShapeDtypeStruct((nb, SEQ, DIL_W), BF16)
    return pl.pallas_call(
        body, out_shape=(osd, osd, osd), grid=(nb, ncol),
        in_specs=[spec(0), spec(ncol), spec(2 * ncol), spec(do_off), ospec, ospec, tspec, tspec, tspec],
        out_specs=(ospec, ospec, ospec),
        scratch_shapes=[pltpu.VMEM((SEQ, LANES), F32)] * 5,
        compiler_params=_cparams(dimension_semantics=("parallel", "parallel")), name=name,
    )(qkv, qkv, qkv, do, o, lse, *tabs)


FOX_GROUP = 4
assert NBLK % FOX_GROUP == 0
_FOX_COLS = tuple(c // LANES for c in (C_FQ, C_FK, C_FV))


def _fox_specs():
    cols = [pl.BlockSpec((None, SEQ, LANES), (lambda b, j, off=off: (b, 0, off + j))) for off in _FOX_COLS]
    ospec = pl.BlockSpec((None, SEQ, LANES), lambda b, j: (b, 0, j))
    crspec = pl.BlockSpec((None, None, NBLK, 8, BLK), lambda b, j: (b, j, 0, 0, 0))
    return cols, ospec, crspec


def _fox_key_rows(t, e):
    return pl.ds(pl.multiple_of((FOX_GROUP * t + e) * BLK, BLK), BLK)


def _fox_fwd(p3, crow, *, name):
    nb = p3.shape[0]
    g = FOX_GROUP

    def body(q_ref, k_ref, v_ref, cr_ref, o_ref, lse_ref):
        masks = _head_masks(2)
        tri = _tri_bias(True)

        def qk(qcat, t):
            return tuple(lax.dot_general(qcat, k_ref[_fox_key_rows(t, e), :], _NT, preferred_element_type=F32) for e in range(g))

        def consume(ss, t, state, nblk, diag):
            m0, m1, l0, l1, acc = state
            us = []
            for e in range(nblk):
                cr = cr_ref[g * t + e]
                u0 = ss[e][:BLK] - cr[0:1, :]
                u1 = ss[e][BLK:] - cr[1:2, :]
                if diag and e == nblk - 1:
                    u0, u1 = u0 + tri, u1 + tri
                us.append((u0, u1))
            x0 = functools.reduce(jnp.maximum, [u[0] for u in us])
            x1 = functools.reduce(jnp.maximum, [u[1] for u in us])
            n0 = jnp.maximum(m0, jnp.max(x0, axis=1, keepdims=True))
            n1 = jnp.maximum(m1, jnp.max(x1, axis=1, keepdims=True))
            a0, a1 = jnp.exp(m0 - n0), jnp.exp(m1 - n1)
            acc = acc * jnp.where(masks[0], a0, a1)
            t0 = t1 = None
            for e in range(nblk):
                p0, p1 = jnp.exp(us[e][0] - n0), jnp.exp(us[e][1] - n1)
                t0 = p0 if t0 is None else t0 + p0
                t1 = p1 if t1 is None else t1 + p1
                pcat = jnp.concatenate([p0, p1], axis=1)
                hi = pcat.astype(BF16)
                lo = (pcat - hi.astype(F32)).astype(BF16)
                vcat = _stack_heads(v_ref[_fox_key_rows(t, e), :], masks)
                acc = acc + jnp.dot(hi, vcat, preferred_element_type=F32) + jnp.dot(lo, vcat, preferred_element_type=F32)
            l0 = a0 * l0 + jnp.sum(t0, axis=1, keepdims=True)
            l1 = a1 * l1 + jnp.sum(t1, axis=1, keepdims=True)
            return n0, n1, l0, l1, acc

        def gbody(ng, c):
            neg = jnp.full((BLK, 1), NEG_INF, F32)
            z1 = jnp.zeros((BLK, 1), F32)
            rows = [pl.ds(pl.multiple_of((g * ng + a) * BLK, BLK), BLK) for a in range(g)]
            qcats = [_stack_heads(q_ref[rows[a], :] * QK_SCALE, masks) for a in range(g)]
            first = [qk(qcats[a], 0) for a in range(g)]
            done = []
            for a in range(g):
                def step(t, cc, qcat=qcats[a]):
                    ss, st = cc
                    nxt = qk(qcat, t + 1)
                    return nxt, consume(ss, t, st, g, False)

                done.append(lax.fori_loop(0, ng, step, (first[a], (neg, neg, z1, z1, jnp.zeros((BLK, LANES), F32)))))
            for a in range(g):
                ss, state = done[a]
                m0, m1, l0, l1, acc = consume(ss, ng, state, a + 1, True)
                o_ref[rows[a], :] = acc / jnp.where(masks[0], l0, l1)
                lse_ref[rows[a], :] = jnp.where(masks[0], m0 + jnp.log(l0), m1 + jnp.log(l1))
            return c

        lax.fori_loop(0, NBLK // g, gbody, 0)

    cols, ospec, crspec = _fox_specs()
    osd = jax.ShapeDtypeStruct((nb, SEQ, FOX_W), F32)
    return pl.pallas_call(
        body, out_shape=(osd, osd), grid=(nb, FOX_W // LANES), in_specs=cols + [crspec], out_specs=(ospec, ospec),
        compiler_params=_cparams(dimension_semantics=("parallel", "parallel")), name=name,
    )(p3, p3, p3, crow)


def _fox_bwd(p3, crow, do, o, lse, *, do_off, name):
    nb = p3.shape[0]
    g = FOX_GROUP
    hd = HEAD_DIM

    def body(q_ref, k_ref, v_ref, cr_ref, do_ref, o_ref, lse_ref, dq_ref, dko_ref, dvo_ref, dcr_ref, dk_ref, dv_ref):
        masks = _head_masks(2)
        tri = _tri_bias(True)
        dk_ref[...] = jnp.zeros_like(dk_ref)
        dv_ref[...] = jnp.zeros_like(dv_ref)
        dcr_ref[...] = jnp.zeros_like(dcr_ref)

        def products(qcat, docat, t):
            out = []
            for e in range(g):
                krows = _fox_key_rows(t, e)
                out.append(lax.dot_general(qcat, k_ref[krows, :], _NT, preferred_element_type=F32))
                out.append(lax.dot_general(docat, v_ref[krows, :], _NT, preferred_element_type=F32))
            return tuple(out)

        def consume(prod, t, ctx, dq, nblk, diag):
            qcat, docat, lse0, lse1, dl0, dl1 = ctx
            for e in range(nblk):
                jb = g * t + e
                krows = _fox_key_rows(t, e)
                s, dp = prod[2 * e], prod[2 * e + 1]
                cr = cr_ref[jb]
                u0 = s[:BLK] - cr[0:1, :]
                u1 = s[BLK:] - cr[1:2, :]
                if diag and e == nblk - 1:
                    u0, u1 = u0 + tri, u1 + tri
                p0 = jnp.exp(u0 - lse0)
                p1 = jnp.exp(u1 - lse1)
                ds0 = p0 * (dp[:BLK] - dl0)
                ds1 = p1 * (dp[BLK:] - dl1)
                dcr_ref[jb, 0:1, :] += jnp.sum(ds0, axis=0, keepdims=True)
                dcr_ref[jb, 1:2, :] += jnp.sum(ds1, axis=0, keepdims=True)
                ds0b, ds1b = ds0.astype(BF16), ds1.astype(BF16)
                pcat = jnp.concatenate([p0.astype(BF16), p1.astype(BF16)], axis=0)
                dscat = jnp.concatenate([ds0b, ds1b], axis=0)
                dv_ref[krows, :] += lax.dot_general(pcat, docat, _T0, preferred_element_type=F32)
                dk_ref[krows, :] += lax.dot_general(dscat, qcat, _T0, preferred_element_type=F32)
                dsrow = jnp.concatenate([ds0b, ds1b], axis=1)
                dq = dq + jnp.dot(dsrow, _stack_heads(k_ref[krows, :] * QK_SCALE, masks), preferred_element_type=F32)
            return dq

        def gbody(ng, c):
            ctxs, rows = [], []
            for a in range(g):
                r = pl.ds(pl.multiple_of((g * ng + a) * BLK, BLK), BLK)
                qcat = _stack_heads(q_ref[r, :] * QK_SCALE, masks)
                dob = do_ref[r, :].astype(BF16)
                prod = dob.astype(F32) * o_ref[r, :]
                z = jnp.zeros_like(prod)
                dl0 = jnp.sum(jnp.where(masks[0], prod, z), axis=1, keepdims=True)
                dl1 = jnp.sum(jnp.where(masks[1], prod, z), axis=1, keepdims=True)
                lseb = lse_ref[r, :]
                ctxs.append((qcat, _stack_heads(dob, masks), lseb[:, 0:1], lseb[:, hd:hd + 1], dl0, dl1))
                rows.append(r)
            first = [products(ctxs[a][0], ctxs[a][1], 0) for a in range(g)]
            done = []
            for a in range(g):
                def step(t, cc, ctx=ctxs[a]):
                    pr, dq = cc
                    nxt = products(ctx[0], ctx[1], t + 1)
                    return nxt, consume(pr, t, ctx, dq, g, False)

                done.append(lax.fori_loop(0, ng, step, (first[a], jnp.zeros((BLK, LANES), F32))))
            for a in range(g):
                pr, dq = done[a]
                dq_ref[rows[a], :] = consume(pr, ng, ctxs[a], dq, a + 1, True).astype(dq_ref.dtype)
            return c

        lax.fori_loop(0, NBLK // g, gbody, 0)
        dko_ref[...] = dk_ref[...].astype(dko_ref.dtype)
        dvo_ref[...] = dv_ref[...].astype(dvo_ref.dtype)

    cols, ospec, crspec = _fox_specs()
    dospec = pl.BlockSpec((None, SEQ, LANES), lambda b, j: (b, 0, do_off + j))
    osd = jax.ShapeDtypeStruct((nb, SEQ, FOX_W), BF16)
    return pl.pallas_call(
        body, out_shape=(osd, osd, osd, jax.ShapeDtypeStruct((nb, FOX_W // LANES, NBLK, 8, BLK), F32)),
        grid=(nb, FOX_W // LANES), in_specs=cols + [crspec, dospec, ospec, ospec], out_specs=(ospec, ospec, ospec, crspec),
        scratch_shapes=[pltpu.VMEM((SEQ, LANES), F32)] * 2,
        compiler_params=_cparams(dimension_semantics=("parallel", "parallel")), name=name,
    )(p3, p3, p3, crow, do, o, lse)


_B1, _B2 = FOX_W // LANES, (FOX_W + DIL_W) // LANES


def _dy_gate_bwd(dx2b, wo, fox, dil, memo, p16, *, tm, tn, name):
    t, d = dx2b.shape
    assert FOX_W % tn == 0 and DIL_W % tn == 0 and MEM_W % tn == 0 and all(c % tn == 0 for c in (C_FG, C_DG, C_MG))
    n1, n2, n3 = FOX_W // tn, (FOX_W + DIL_W) // tn, MIX_W // tn

    def body(dx_ref, w_ref, f_ref, d_ref, m_ref, g_ref, da_ref, dg_ref):
        j = pl.program_id(1)
        dyv = lax.dot_general(dx_ref[...], w_ref[...], _NT, preferred_element_type=F32)
        a = jnp.where(j < n1, f_ref[...], jnp.where(j < n2, d_ref[...], m_ref[...]))
        gt = g_ref[...].astype(F32)
        sg = 1.0 / (1.0 + jnp.exp(-gt))
        da_ref[...] = (dyv * gt * sg).astype(da_ref.dtype)
        dg_ref[...] = (dyv * a * sg * (1.0 + gt * (1.0 - sg))).astype(dg_ref.dtype)

    def gcol(j):
        return jnp.where(j < n1, C_FG // tn + j, jnp.where(j < n2, C_DG // tn + j - n1, C_MG // tn + j - n2))

    tile = pl.BlockSpec((tm, tn), lambda i, j: (i, j))
    return pl.pallas_call(
        body,
        out_shape=(jax.ShapeDtypeStruct((t, MIX_W), BF16), jax.ShapeDtypeStruct((t, MIX_W), BF16)),
        grid=(t // tm, n3),
        in_specs=[pl.BlockSpec((tm, d), lambda i, j: (i, 0)), pl.BlockSpec((tn, d), lambda i, j: (j, 0)),
                  pl.BlockSpec((tm, tn), lambda i, j: (i, jnp.minimum(j, n1 - 1))),
                  pl.BlockSpec((tm, tn), lambda i, j: (i, jnp.clip(j - n1, 0, n2 - n1 - 1))),
                  pl.BlockSpec((tm, tn), lambda i, j: (i, jnp.clip(j - n2, 0, n3 - n2 - 1))),
                  pl.BlockSpec((tm, tn), lambda i, j: (i, gcol(j)))],
        out_specs=(tile, tile),
        compiler_params=_cparams(dimension_semantics=("parallel", "parallel")),
        name=name,
    )(dx2b, wo, fox, dil, memo, p16)


def _silu(g):
    return g / (1.0 + jnp.exp(-g))


def _out_loss(fox, dil, memo, p16, wo, x, tgt, gfin, *, tm, name):
    t, d = x.shape
    n_feat = float(d)

    def body(f_ref, d_ref, m_ref, fg_ref, dg_ref, mg_ref, w_ref, x_ref, t_ref, g_ref, y_ref, dx_ref, dxb_ref, st_ref):
        i = pl.program_id(0)

        @pl.when(i == 0)
        def _():
            st_ref[...] = jnp.zeros_like(st_ref)

        y = jnp.concatenate([(a_ref[...] * _silu(gt_ref[...].astype(F32))).astype(BF16)
                             for a_ref, gt_ref in ((f_ref, fg_ref), (d_ref, dg_ref), (m_ref, mg_ref))], axis=1)
        y_ref[...] = y
        x2 = x_ref[...] + jnp.dot(y, w_ref[...], preferred_element_type=F32)
        r = lax.rsqrt(jnp.mean(x2 * x2, axis=-1, keepdims=True) + RMS_EPS)
        nrm = x2 * r
        gv = g_ref[...]
        err = nrm * gv - t_ref[...]
        dout = err * (1.0 / n_feat)
        dn = dout * gv
        dx2 = r * (dn - nrm * jnp.mean(dn * nrm, axis=-1, keepdims=True))
        dx_ref[...] = dx2
        dxb_ref[...] = dx2.astype(dxb_ref.dtype)
        st_ref[0:1, :] += jnp.sum(dout * nrm, axis=0, keepdims=True)
        st_ref[1:2, :] += (0.5 / n_feat) * jnp.sum(err * err, axis=0, keepdims=True)

    row = pl.BlockSpec((tm, d), lambda i: (i, 0))
    whole = lambda w: pl.BlockSpec((tm, w), lambda i: (i, 0))
    gate = lambda w, col: pl.BlockSpec((tm, w), lambda i: (i, col // w))
    return pl.pallas_call(
        body,
        out_shape=(jax.ShapeDtypeStruct((t, MIX_W), BF16), jax.ShapeDtypeStruct((t, d), F32), jax.ShapeDtypeStruct((t, d), BF16),
                   jax.ShapeDtypeStruct((8, d), F32)),
        grid=(t // tm,),
        in_specs=[whole(FOX_W), whole(DIL_W), whole(MEM_W), gate(FOX_W, C_FG), gate(DIL_W, C_DG), gate(MEM_W, C_MG),
                  pl.BlockSpec((MIX_W, d), lambda i: (0, 0)), row, row, pl.BlockSpec((1, d), lambda i: (0, 0))],
        out_specs=(pl.BlockSpec((tm, MIX_W), lambda i: (i, 0)), row, row, pl.BlockSpec((8, d), lambda i: (0, 0))),
        compiler_params=_cparams(dimension_semantics=("arbitrary",)),
        name=name,
    )(fox, dil, memo, p16, p16, p16, wo, x, tgt, gfin)


def _dh_rms_bwd(dp, w, x, g, resid, *, tm, tk, name):
    t, d = x.shape
    kdim = dp.shape[1]
    nk = kdim // tk

    def body(*refs):
        if resid is not None:
            dp_ref, w_ref, x_ref, g_ref, r_ref, dx_ref, gg_ref, acc_ref = refs
        else:
            dp_ref, w_ref, x_ref, g_ref, dx_ref, gg_ref, acc_ref = refs
        i = pl.program_id(0)
        k = pl.program_id(1)

        @pl.when(jnp.logical_and(i == 0, k == 0))
        def _():
            gg_ref[...] = jnp.zeros_like(gg_ref)

        prod = lax.dot_general(dp_ref[...], w_ref[...], _NT, preferred_element_type=F32)

        @pl.when(k == 0)
        def _():
            acc_ref[...] = prod

        @pl.when(k > 0)
        def _():
            acc_ref[...] += prod

        @pl.when(k == nk - 1)
        def _():
            dh = acc_ref[...]
            xv = x_ref[...]
            r = lax.rsqrt(jnp.mean(xv * xv, axis=-1, keepdims=True) + RMS_EPS)
            nrm = xv * r
            dn = dh * g_ref[...]
            dx = r * (dn - nrm * jnp.mean(dn * nrm, axis=-1, keepdims=True))
            if resid is not None:
                dx = dx + r_ref[...]
            dx_ref[...] = dx
            gg_ref[0:1, :] += jnp.sum(dh * nrm, axis=0, keepdims=True)

    row = pl.BlockSpec((tm, d), lambda i, k: (i, 0))
    in_specs = [pl.BlockSpec((tm, tk), lambda i, k: (i, k)), pl.BlockSpec((d, tk), lambda i, k: (0, k)), row,
                pl.BlockSpec((1, d), lambda i, k: (0, 0))]
    args = [dp, w, x, g]
    if resid is not None:
        in_specs.append(row)
        args.append(resid)
    return pl.pallas_call(
        body,
        out_shape=(jax.ShapeDtypeStruct((t, d), F32), jax.ShapeDtypeStruct((8, d), F32)),
        grid=(t // tm, nk),
        in_specs=in_specs,
        out_specs=(row, pl.BlockSpec((8, d), lambda i, k: (0, 0))),
        scratch_shapes=[pltpu.VMEM((tm, d), F32)],
        compiler_params=_cparams(dimension_semantics=("arbitrary", "arbitrary")),
        name=name,
    )(*args)


_FLOG0 = 4 * FOX_W
_W_IN_SEGMENTS = ((0, _FLOG0, 0), (_FLOG0, _FLOG0 + FOX_HEADS, PW), (_FLOG0 + FOX_HEADS, IN_W, C_DQ))
SHARD_W = IN_W // N_CHIPS


def _rearrange_w_in(shards):
    def cols(lo, hi):
        parts = []
        for k in range(N_CHIPS):
            a, b = max(lo, k * SHARD_W), min(hi, (k + 1) * SHARD_W)
            if a < b:
                parts.append(shards[k][:, a - k * SHARD_W:b - k * SHARD_W])
        return parts

    (a0, a1, _), (f0, f1, _), (b0, b1, _) = _W_IN_SEGMENTS
    pad = jnp.zeros((shards[0].shape[0], PWF - PW - FOX_HEADS), shards[0].dtype)
    return jnp.concatenate(cols(a0, a1) + cols(b0, b1) + cols(f0, f1) + [pad], axis=1)


def _w_in_grad_slabs(g):
    slabs = []
    for k in range(N_CHIPS):
        parts = []
        for lo, hi, at in _W_IN_SEGMENTS:
            a, b = max(lo, k * SHARD_W), min(hi, (k + 1) * SHARD_W)
            if a < b:
                parts.append(g[:, at + a - lo:at + b - lo])
        slabs.append(jnp.concatenate(parts, axis=1))
    return jnp.stack(slabs, axis=0)


def _local_grads(x, mem, norm_g, w_r, b_forget, mem_norm_g, w_kv, w_o, final_norm_g, tgt, start_reduce=None,
                 early_token=None, late_weights=None):
    nb = x.shape[0]
    t = nb * SEQ
    x2d = x.reshape(t, D_MODEL)
    tgt2d = tgt.reshape(t, D_MODEL)
    tabs = _rope_tables()
    bpad = jnp.pad(b_forget.reshape(1, FOX_HEADS), ((0, 0), (0, LANES - FOX_HEADS)))

    gain0 = norm_g.reshape(1, D_MODEL)
    if early_token is not None:
        gain0 = gain0 + early_token[0:1, 0:1]
    h = _rms_fwd(x2d, gain0, tm=512, name="rms_x")
    p16, dqkv = _proj(h, w_r, tabs, n=PWF, tm=1024, tn=768, name="proj")
    flog = _matmul(h, w_r[:, PW:PW + LANES], out_dtype=F32, tm=1024, tn=LANES, tk=D_MODEL, name="proj_flog")
    c12 = _flog_fwd(flog, bpad, nb=nb, ts=256, name="flog_fwd")

    crow = c12[:, :FOX_HEADS].reshape(nb, NBLK, BLK, FOX_HEADS // 2, 2).transpose(0, 3, 1, 4, 2)
    crow = jnp.pad(crow, ((0, 0), (0, 0), (0, 0), (0, 6), (0, 0)))
    p3 = p16.reshape(nb, SEQ, PWF)
    fox, fox_lse = _fox_fwd(p3, crow, name="fox_fwd")
    if late_weights is not None:
        w_kv, w_o = late_weights(fox_lse)

    dqkv3 = dqkv.reshape(nb, SEQ, 3 * DIL_W)
    dil, dil_lse = _dil_fwd(dqkv3, name="dil_fwd")

    mh = _rms_fwd(mem.reshape(nb * MEM_LEN, D_MODEL), mem_norm_g.reshape(1, D_MODEL), tm=nb * MEM_LEN, name="rms_mem")
    mkv = _matmul(mh, w_kv, out_dtype=BF16, tm=nb * MEM_LEN, tn=512, tk=D_MODEL, name="mem_kv")
    mkv3 = mkv.reshape(nb, MEM_LEN, 2 * MEM_W)
    memo, mem_lse = _mem_fwd(p3, mkv3, qoff=C_MQ // LANES, name="mem_fwd")

    fox2, dil2, memo2 = fox.reshape(t, FOX_W), dil.reshape(t, DIL_W), memo.reshape(t, MEM_W)
    y, dx2, dx2b, st = _out_loss(fox2, dil2, memo2, p16, w_o, x2d, tgt2d, final_norm_g.reshape(1, D_MODEL), tm=256,
                                 name="out_loss")

    g_wo = _matmul(y, dx2b, mode="tn", out_dtype=BF16, tm=1024, tn=512, tk=t, name="grad_w_out")
    datt, dgate = _dy_gate_bwd(dx2b, w_o, fox2, dil2, memo2, p16, tm=1024, tn=256, name="dy_gate_bwd")
    datt3 = datt.reshape(nb, SEQ, MIX_W)

    dfq, dfk, dfv, dcr = _fox_bwd(p3, crow, datt3, fox, fox_lse, do_off=0, name="fox_bwd")
    dcol = -dcr[:, :, :, :2, :].transpose(0, 2, 4, 1, 3).reshape(t, FOX_HEADS)
    dcol = jnp.pad(dcol, ((0, 0), (0, LANES - FOX_HEADS)))
    dflog, gb = _flog_bwd(dcol, flog, bpad, nb=nb, ts=256, name="flog_bwd")

    ddq, ddk, ddv = _dil_bwd(dqkv3, datt3, dil, dil_lse, tabs, do_off=_B1, name="dil_bwd")

    dmq, dmk, dmv = _mem_bwd(p3, mkv3, datt3, memo, mem_lse, qoff=C_MQ // LANES, do_off=_B2, name="mem_bwd")
    dmkv = jnp.concatenate([dmk, dmv], axis=-1).reshape(nb * MEM_LEN, 2 * MEM_W).astype(BF16)
    g_wkv = _matmul(mh, dmkv, mode="tn", out_dtype=BF16, tm=512, tn=512, tk=nb * MEM_LEN, name="grad_w_kv")
    _, gmn = _dh_rms_bwd(dmkv, w_kv, mem.reshape(nb * MEM_LEN, D_MODEL), mem_norm_g.reshape(1, D_MODEL), None,
                         tm=nb * MEM_LEN, tk=2 * MEM_W, name="mem_rms_bwd")

    flat = lambda a: a.reshape(t, -1)
    dp = jnp.concatenate([flat(dfq), flat(dfk), flat(dfv), dgate[:, :FOX_W], flat(ddq), flat(ddk), flat(ddv),
                          dgate[:, FOX_W:FOX_W + DIL_W], flat(dmq), dgate[:, FOX_W + DIL_W:], dflog,
                          jnp.zeros((t, PWF - PW - LANES), BF16)], axis=1)
    g_wr = _matmul(h, dp, mode="tn", out_dtype=BF16, tm=D_MODEL, tn=512, tk=t, name="grad_w_in")
    gain = norm_g.reshape(1, D_MODEL)
    if start_reduce is not None:
        gain = gain + start_reduce(g_wr, g_wkv, g_wo)[0:1, 0:1]
    gx, gng = _dh_rms_bwd(dp, w_r, x2d, gain, dx2, tm=512, tk=PWF // 3, name="in_rms_bwd")

    gb_row = jnp.pad(gb[0:1, :], ((0, 0), (0, D_MODEL - LANES)))
    small = jnp.concatenate([gng[0:1], gmn[0:1], st[0:1], gb_row, st[1:2], jnp.zeros((3, D_MODEL), F32)], axis=0)
    return gx.reshape(nb, SEQ, D_MODEL), g_wr, g_wkv, g_wo, small


MESH = pl.DeviceIdType.MESH
ANY = pl.BlockSpec(memory_space=pl.ANY)


def _place():
    x, y, c = lax.axis_index("x"), lax.axis_index("y"), lax.axis_index("c")
    other_chips = [(1 - x, y), (x, 1 - y), (1 - x, 1 - y)]
    return x, y, c, other_chips


def _gather_weights(shards):
    n = len(shards)

    def body(*refs):
        in_refs, out_refs = refs[:n], refs[n:2 * n]
        send_sems, recv_sems = refs[2 * n:]
        x, y, c, chips = _place()
        me_chip = 2 * x + y
        sibling = (x, y, 1 - c)

        def half(ref, pc, rows):
            return ref.at[pl.ds(pc * (rows // 2), rows // 2), :]

        def rcopy(k, src, dst, to):
            return pltpu.make_async_remote_copy(src_ref=src, dst_ref=dst, send_sem=send_sems.at[k], recv_sem=recv_sems.at[k],
                                                device_id=to, device_id_type=MESH)

        sends = []
        for t in range(n):
            rows = shards[t].shape[0]
            for j, chip in enumerate(chips):
                cp = rcopy(6 * t + j, half(in_refs[t], c, rows), half(out_refs[t].at[me_chip], c, rows), (*chip, c))
                cp.start()
                sends.append(cp)
        for t in range(n):
            rows = shards[t].shape[0]
            for j, chip in enumerate(chips):
                slot = out_refs[t].at[2 * chip[0] + chip[1]]
                rcopy(6 * t + j, half(slot, c, rows), half(slot, c, rows), sibling).wait_recv()
                fw = rcopy(6 * t + 3 + j, half(slot, c, rows), half(slot, c, rows), sibling)
                fw.start()
                sends.append(fw)
        for t in range(n):
            rows = shards[t].shape[0]
            for j, chip in enumerate(chips):
                slot = out_refs[t].at[2 * chip[0] + chip[1]]
                rcopy(6 * t + 3 + j, half(slot, 1 - c, rows), half(slot, 1 - c, rows), sibling).wait_recv()
        for cp in sends:
            cp.wait_send()

    return pl.pallas_call(
        body,
        out_shape=tuple(jax.ShapeDtypeStruct((N_CHIPS,) + s.shape, s.dtype) for s in shards),
        in_specs=[ANY] * n,
        out_specs=tuple([ANY] * n),
        scratch_shapes=[pltpu.SemaphoreType.DMA((6 * n,)), pltpu.SemaphoreType.DMA((6 * n,))],
        name="gather_weights",
    )(*shards)


def _pair_exchange(gs):
    n = len(gs)

    def body(*refs):
        g_refs, r_refs = refs[:n], refs[n:2 * n]
        send_sems, recv_sems = refs[2 * n:]
        x, y, c, _ = _place()
        cps = []
        for t in range(n):
            hr = gs[t].shape[1] // 2
            cp = pltpu.make_async_remote_copy(src_ref=g_refs[t].at[:, pl.ds((1 - c) * hr, hr), :], dst_ref=r_refs[t],
                                              send_sem=send_sems.at[t], recv_sem=recv_sems.at[t],
                                              device_id=(x, y, 1 - c), device_id_type=MESH)
            cp.start()
            cps.append(cp)
        for cp in cps:
            cp.wait()

    return pl.pallas_call(
        body,
        out_shape=tuple(jax.ShapeDtypeStruct((g.shape[0], g.shape[1] // 2, g.shape[2]), g.dtype) for g in gs),
        in_specs=[ANY] * n,
        out_specs=tuple([ANY] * n),
        scratch_shapes=[pltpu.SemaphoreType.DMA((n,)), pltpu.SemaphoreType.DMA((n,))],
        name="pair_exchange",
    )(*gs)


def _chip_exchange(ps):
    n = len(ps)

    def body(*refs):
        p_refs, o_refs = refs[:n], refs[n:2 * n]
        send_sems, recv_sems = refs[2 * n:]
        x, y, c, chips = _place()
        me_chip = 2 * x + y
        cps = []
        for t in range(n):
            for j, chip in enumerate(chips):
                cp = pltpu.make_async_remote_copy(src_ref=p_refs[t].at[2 * chip[0] + chip[1]], dst_ref=o_refs[t].at[me_chip],
                                                  send_sem=send_sems.at[3 * t + j], recv_sem=recv_sems.at[3 * t + j],
                                                  device_id=(*chip, c), device_id_type=MESH)
                cp.start()
                cps.append(cp)
        for cp in cps:
            cp.wait()

    return pl.pallas_call(
        body,
        out_shape=tuple(jax.ShapeDtypeStruct(p.shape, p.dtype) for p in ps),
        in_specs=[ANY] * n,
        out_specs=tuple([ANY] * n),
        scratch_shapes=[pltpu.SemaphoreType.DMA((3 * n,)), pltpu.SemaphoreType.DMA((3 * n,))],
        name="chip_exchange",
    )(*ps)


_HBM = pl.BlockSpec(memory_space=pltpu.HBM)
_SEM = pl.BlockSpec(memory_space=pltpu.SEMAPHORE)
_DATAFLOW = pltpu.SideEffectType.DATAFLOW_SIDE_EFFECTING


def _chip_copies(p_refs, land_refs, send_sems, recv_sems):
    x, y, c, chips = _place()
    me_chip = 2 * x + y
    return [pltpu.make_async_remote_copy(src_ref=p_refs[t].at[2 * chip[0] + chip[1]], dst_ref=land_refs[t].at[me_chip],
                                         send_sem=send_sems.at[3 * t + j], recv_sem=recv_sems.at[3 * t + j],
                                         device_id=(*chip, c), device_id_type=MESH)
            for t in range(len(p_refs)) for j, chip in enumerate(chips)]


def _chip_exchange_start(ps):
    n = len(ps)

    def body(*refs):
        p_refs, land_refs = refs[:n], refs[n:2 * n]
        send_sems, recv_sems = refs[2 * n:2 * n + 2]
        token = refs[-1]
        for cp in _chip_copies(p_refs, land_refs, send_sems, recv_sems):
            cp.start()
        token[...] = jnp.zeros_like(token)

    hbm = [pltpu.HBM(p.shape, p.dtype) for p in ps]
    args = [pltpu.with_memory_space_constraint(p, pltpu.HBM) for p in ps]
    args += [pltpu.with_memory_space_constraint(lax.empty(p.shape, p.dtype), pltpu.HBM) for p in ps]
    out = pl.pallas_call(
        body,
        name="chip_exchange_start",
        out_shape=(pltpu.SemaphoreType.DMA((3 * n,)), pltpu.SemaphoreType.DMA((3 * n,)), *hbm, *hbm,
                   jax.ShapeDtypeStruct((8, LANES), F32)),
        in_specs=[_HBM] * (2 * n),
        out_specs=(_SEM, _SEM, *([_HBM] * (2 * n)), pl.BlockSpec(memory_space=pltpu.VMEM)),
        input_output_aliases={i: 2 + i for i in range(2 * n)},
        compiler_params=pltpu.CompilerParams(has_side_effects=_DATAFLOW),
    )(*args)
    return out[0], out[1], out[2:2 + n], out[2 + n:2 + 2 * n], out[-1]


def _chip_exchange_wait(send_sems, recv_sems, p_thru, land_thru, after):
    n = len(p_thru)

    def body(*refs):
        p_refs, land_refs = refs[:n], refs[n:2 * n]
        ssem, rsem = refs[2 * n:2 * n + 2]
        for cp in _chip_copies(p_refs, land_refs, ssem, rsem):
            cp.wait_send()
            cp.wait_recv()

    hbm = [pltpu.HBM(p.shape, p.dtype) for p in p_thru]
    out = pl.pallas_call(
        body,
        name="chip_exchange_wait",
        out_shape=(*hbm, *hbm),
        in_specs=[_HBM] * (2 * n) + [_SEM, _SEM, ANY],
        out_specs=tuple([_HBM] * (2 * n)),
        input_output_aliases={i: i for i in range(2 * n)},
        compiler_params=pltpu.CompilerParams(has_side_effects=_DATAFLOW),
    )(*p_thru, *land_thru, send_sems, recv_sems, after)
    return out[:n], out[n:]


def _shard_copies(s_refs, land_refs, send_sems, recv_sems):
    x, y, c, chips = _place()
    me_chip = 2 * x + y
    return [pltpu.make_async_remote_copy(src_ref=s_refs[t], dst_ref=land_refs[t].at[me_chip],
                                         send_sem=send_sems.at[3 * t + j], recv_sem=recv_sems.at[3 * t + j],
                                         device_id=(*chip, c), device_id_type=MESH)
            for t in range(len(s_refs)) for j, chip in enumerate(chips)]


def _gather_late_start(shards):
    n = len(shards)

    def body(*refs):
        s_refs, land_refs = refs[:n], refs[n:2 * n]
        send_sems, recv_sems = refs[2 * n:2 * n + 2]
        token = refs[-1]
        for cp in _shard_copies(s_refs, land_refs, send_sems, recv_sems):
            cp.start()
        token[...] = jnp.zeros_like(token)

    lands = [(N_CHIPS,) + s.shape for s in shards]
    args = [pltpu.with_memory_space_constraint(s, pltpu.HBM) for s in shards]
    args += [pltpu.with_memory_space_constraint(lax.empty(shp, s.dtype), pltpu.HBM) for shp, s in zip(lands, shards)]
    out = pl.pallas_call(
        body,
        name="gather_late_start",
        out_shape=(pltpu.SemaphoreType.DMA((3 * n,)), pltpu.SemaphoreType.DMA((3 * n,)),
                   *[pltpu.HBM(s.shape, s.dtype) for s in shards], *[pltpu.HBM(shp, s.dtype) for shp, s in zip(lands, shards)],
                   jax.ShapeDtypeStruct((8, LANES), F32)),
        in_specs=[_HBM] * (2 * n),
        out_specs=(_SEM, _SEM, *([_HBM] * (2 * n)), pl.BlockSpec(memory_space=pltpu.VMEM)),
        input_output_aliases={i: 2 + i for i in range(2 * n)},
        compiler_params=pltpu.CompilerParams(has_side_effects=_DATAFLOW),
    )(*args)
    return out[0], out[1], out[2:2 + n], out[2 + n:2 + 2 * n], out[-1]


def _gather_late_wait(send_sems, recv_sems, s_thru, land_thru, after):
    n = len(s_thru)

    def body(*refs):
        s_refs, land_refs = refs[:n], refs[n:2 * n]
        ssem, rsem = refs[2 * n:2 * n + 2]
        for cp in _shard_copies(s_refs, land_refs, ssem, rsem):
            cp.wait_send()
            cp.wait_recv()

    out = pl.pallas_call(
        body,
        name="gather_late_wait",
        out_shape=(*[pltpu.HBM(s.shape, s.dtype) for s in s_thru], *[pltpu.HBM(l.shape, l.dtype) for l in land_thru]),
        in_specs=[_HBM] * (2 * n) + [_SEM, _SEM, ANY],
        out_specs=tuple([_HBM] * (2 * n)),
        input_output_aliases={i: i for i in range(2 * n)},
        compiler_params=pltpu.CompilerParams(has_side_effects=_DATAFLOW),
    )(*s_thru, *land_thru, send_sems, recv_sems, after)
    return out[:n], out[n:]


def _pair_swap(rs):
    n = len(rs)

    def body(*refs):
        r_refs, o_refs = refs[:n], refs[n:2 * n]
        send_sems, recv_sems = refs[2 * n:]
        x, y, c, _ = _place()
        cps = []
        for t in range(n):
            cp = pltpu.make_async_remote_copy(src_ref=r_refs[t], dst_ref=o_refs[t], send_sem=send_sems.at[t],
                                              recv_sem=recv_sems.at[t], device_id=(x, y, 1 - c), device_id_type=MESH)
            cp.start()
            cps.append(cp)
        for cp in cps:
            cp.wait()

    return pl.pallas_call(
        body,
        out_shape=tuple(jax.ShapeDtypeStruct(r.shape, r.dtype) for r in rs),
        in_specs=[ANY] * n,
        out_specs=tuple([ANY] * n),
        scratch_shapes=[pltpu.SemaphoreType.DMA((n,)), pltpu.SemaphoreType.DMA((n,))],
        name="pair_swap",
    )(*rs)


N_DEV = 8
LOSS_ROW = 4


def _small_allreduce(small):
    def body(s_ref, o_ref, all_ref, send_sems, recv_sems):
        x, y, c, _ = _place()
        me = 4 * x + 2 * y + c
        all_ref[me] = s_ref[...]
        cps = []
        for k in range(1, N_DEV):
            peer = tuple(1 - p if (k >> s) & 1 else p for p, s in ((x, 2), (y, 1), (c, 0)))
            cp = pltpu.make_async_remote_copy(src_ref=s_ref, dst_ref=all_ref.at[me], send_sem=send_sems.at[k - 1],
                                              recv_sem=recv_sems.at[k - 1], device_id=peer, device_id_type=MESH)
            cp.start()
            cps.append(cp)
        for cp in cps:
            cp.wait()
        tot = all_ref[0]
        for d in range(1, N_DEV):
            tot = tot + all_ref[d]
        o_ref[...] = tot
        o_ref[LOSS_ROW:LOSS_ROW + 1, :] = jnp.broadcast_to(jnp.sum(tot[LOSS_ROW:LOSS_ROW + 1, :], axis=1, keepdims=True),
                                                          (1, tot.shape[1]))

    vm = pl.BlockSpec(memory_space=pltpu.VMEM)
    return pl.pallas_call(
        body,
        out_shape=jax.ShapeDtypeStruct(small.shape, small.dtype),
        in_specs=[vm],
        out_specs=vm,
        scratch_shapes=[pltpu.VMEM((N_DEV,) + small.shape, small.dtype), pltpu.SemaphoreType.DMA((N_DEV - 1,)),
                        pltpu.SemaphoreType.DMA((N_DEV - 1,))],
        name="small_allreduce",
    )(small)


def _sum_pair(g, recv, cidx, *, tr, name):
    n, hr, cols = recv.shape
    nr = hr // tr

    def body(c_ref, g_ref, r_ref, o_ref):
        o_ref[...] = (g_ref[...].astype(F32) + r_ref[...].astype(F32)).astype(o_ref.dtype)

    grid_spec = pltpu.PrefetchScalarGridSpec(
        num_scalar_prefetch=1,
        grid=(n, nr),
        in_specs=[pl.BlockSpec((None, tr, cols), lambda k, i, c_ref: (k, c_ref[0] * nr + i, 0)),
                  pl.BlockSpec((None, tr, cols), lambda k, i, c_ref: (k, i, 0))],
        out_specs=pl.BlockSpec((None, tr, cols), lambda k, i, c_ref: (k, i, 0)),
    )
    return pl.pallas_call(body, out_shape=jax.ShapeDtypeStruct(recv.shape, BF16), grid_spec=grid_spec,
                          compiler_params=_cparams(), name=name)(cidx, g, recv)


def _sum_chips(p, *, tr, name):
    _, rows, cols = p.shape

    def body(p_ref, o_ref):
        tot = p_ref[0].astype(F32)
        for k in range(1, N_CHIPS):
            tot = tot + p_ref[k].astype(F32)
        o_ref[...] = tot

    return pl.pallas_call(
        body,
        out_shape=jax.ShapeDtypeStruct((rows, cols), F32),
        grid=(rows // tr,),
        in_specs=[pl.BlockSpec((N_CHIPS, tr, cols), lambda i: (0, i, 0))],
        out_specs=pl.BlockSpec((tr, cols), lambda i: (i, 0)),
        compiler_params=_cparams(),
        name=name,
    )(p)


def _adamw(w, g, m, v, *, tr, name):
    rows, cols = w.shape
    bc1 = 1.0 / (1.0 - ADAM_B1 ** ADAM_STEP)
    bc2 = 1.0 / (1.0 - ADAM_B2 ** ADAM_STEP)

    def body(w_ref, g_ref, m_ref, v_ref, d_ref, nm_ref, nv_ref):
        gv = g_ref[...]
        nm = ADAM_B1 * m_ref[...] + (1.0 - ADAM_B1) * gv
        nv = ADAM_B2 * v_ref[...] + (1.0 - ADAM_B2) * (gv * gv)
        d_ref[...] = -ADAM_LR * ((nm * bc1) / (jnp.sqrt(nv * bc2) + ADAM_EPS) + ADAM_WD * w_ref[...])
        nm_ref[...] = nm
        nv_ref[...] = nv

    spec = pl.BlockSpec((tr, cols), lambda i: (i, 0))
    sd = jax.ShapeDtypeStruct((rows, cols), F32)
    return pl.pallas_call(body, out_shape=(sd, sd, sd), grid=(rows // tr,), in_specs=[spec] * 4, out_specs=(spec,) * 3,
                          compiler_params=_cparams(), name=name)(w, g, m, v)


def _adamw_halves(w, own, sib, cidx, m, v, *, tr, name):
    rows, cols = w.shape
    hr = own.shape[0]
    nr = hr // tr
    assert rows == 2 * hr and hr % tr == 0
    bc1 = 1.0 / (1.0 - ADAM_B1 ** ADAM_STEP)
    bc2 = 1.0 / (1.0 - ADAM_B2 ** ADAM_STEP)

    def body(c_ref, w_ref, o_ref, s_ref, m_ref, v_ref, g_ref, d_ref, nm_ref, nv_ref):
        mine = (pl.program_id(0) // nr) == c_ref[0]
        gv = jnp.where(mine, o_ref[...], s_ref[...])
        nm = ADAM_B1 * m_ref[...] + (1.0 - ADAM_B1) * gv
        nv = ADAM_B2 * v_ref[...] + (1.0 - ADAM_B2) * (gv * gv)
        g_ref[...] = gv
        d_ref[...] = -ADAM_LR * ((nm * bc1) / (jnp.sqrt(nv * bc2) + ADAM_EPS) + ADAM_WD * w_ref[...])
        nm_ref[...] = nm
        nv_ref[...] = nv

    full = pl.BlockSpec((tr, cols), lambda i, c_ref: (i, 0))
    half = pl.BlockSpec((tr, cols), lambda i, c_ref: (i % nr, 0))
    sd = jax.ShapeDtypeStruct((rows, cols), F32)
    grid_spec = pltpu.PrefetchScalarGridSpec(num_scalar_prefetch=1, grid=(rows // tr,), in_specs=[full, half, half, full, full],
                                             out_specs=(full,) * 4)
    return pl.pallas_call(body, out_shape=(sd,) * 4, grid_spec=grid_spec, compiler_params=_cparams(), name=name)(
        cidx, w, own, sib, m, v)


def _pack_small(norm, mem_norm, final_norm, b_forget):
    rows = [norm.reshape(1, D_MODEL), mem_norm.reshape(1, D_MODEL), final_norm.reshape(1, D_MODEL),
            jnp.pad(b_forget.reshape(1, FOX_HEADS), ((0, 0), (0, D_MODEL - FOX_HEADS))), jnp.zeros((4, D_MODEL), F32)]
    return jnp.concatenate(rows, axis=0)


def _unpack_small(a):
    return a[0:1], a[3:4, :FOX_HEADS], a[1:2], a[2]


def kernel(x, mem, norm_g, w_in, b_forget, mem_norm_g, w_mem_kv, w_out, final_norm_g, loss_target, m_norm_g, m_w_in, m_b_forget, m_mem_norm_g, m_w_mem_kv, m_w_out, m_final_norm_g, v_norm_g, v_w_in, v_b_forget, v_mem_norm_g, v_w_mem_kv, v_w_out, v_final_norm_g):
    core = lax.axis_index("c").astype(jnp.int32)
    me_chip = (2 * lax.axis_index("x") + lax.axis_index("y")).astype(jnp.int32)
    cidx = core.reshape(1)

    def own_slot(arr, own):
        return lax.dynamic_update_slice(arr, own[None].astype(arr.dtype), (me_chip,) + (0,) * own.ndim)

    win_b, late = w_in[0].astype(BF16), [w_mem_kv[0].astype(BF16), w_out[0].astype(BF16)]
    g_in, = _gather_weights([win_b])
    g_in, late = lax.optimization_barrier((own_slot(g_in, win_b), late))
    w_r = _rearrange_w_in([g_in[k] for k in range(N_CHIPS)])
    *late_flight, early_token = _gather_late_start(late)

    def late_weights(after):
        shards, landed = _gather_late_wait(*late_flight, after)
        g_kv, g_out = (own_slot(g, s) for g, s in zip(landed, shards))
        return g_kv.reshape(D_MODEL, 2 * MEM_W), g_out.reshape(MIX_W, D_MODEL)

    trs = (128, 128, 256)
    names = ("w_in", "w_mem_kv", "w_out")
    flight = []

    def start_reduce(g_wr, g_wkv, g_wo):
        slabs = [g_wr[None],
                 g_wkv.reshape(N_CHIPS, D_MODEL // N_CHIPS, 2 * MEM_W),
                 g_wo.reshape(N_CHIPS, MIX_W // N_CHIPS, D_MODEL)]
        recv = _pair_exchange(slabs)
        pair = [_sum_pair(g, r, cidx, tr=tr, name=f"sum_pair_{nm}") for g, r, tr, nm in zip(slabs, recv, trs, names)]
        pair[0] = _w_in_grad_slabs(pair[0][0])
        *handles, token = _chip_exchange_start(pair)
        flight.extend(handles)
        return token

    gx, g_wr, g_wkv, g_wo, small = _local_grads(x, mem, norm_g, w_r, b_forget, mem_norm_g, None, None, final_norm_g, loss_target,
                                                start_reduce=start_reduce, early_token=early_token, late_weights=late_weights)

    send_sems, recv_sems, pair, land = flight
    pair, landed = _chip_exchange_wait(send_sems, recv_sems, pair, land, small)
    got = [lax.dynamic_update_slice(g, lax.dynamic_slice(p, (me_chip, 0, 0), (1,) + p.shape[1:]), (me_chip, 0, 0))
           for g, p in zip(landed, pair)]
    red = [_sum_chips(p, tr=tr, name=f"sum_chips_{nm}") for p, tr, nm in zip(got, trs, names)]
    sib = _pair_swap(red)

    outs = {}
    for nm, r, s, w, m, v, tr in zip(names, red, sib, (w_in, w_mem_kv, w_out), (m_w_in, m_w_mem_kv, m_w_out),
                                     (v_w_in, v_w_mem_kv, v_w_out), trs):
        outs[nm] = tuple(a[None] for a in _adamw_halves(w[0], r, s, cidx, m[0], v[0], tr=tr, name=f"adamw_{nm}"))

    gsum = _small_allreduce(small)
    sd, sm, sv = _adamw(_pack_small(norm_g, mem_norm_g, final_norm_g, b_forget), gsum,
                        _pack_small(m_norm_g, m_mem_norm_g, m_final_norm_g, m_b_forget),
                        _pack_small(v_norm_g, v_mem_norm_g, v_final_norm_g, v_b_forget), tr=8, name="adamw_small")
    loss = gsum[LOSS_ROW, 0]

    def group(i, small_arr):
        ng, bf, mg, fg = _unpack_small(small_arr)
        return (ng, outs["w_in"][i], bf, mg, outs["w_mem_kv"][i], outs["w_out"][i], fg)

    return (loss, gx, *group(0, gsum), *group(1, sd), *group(2, sm), *group(3, sv))
```

```python
import functools
import math

import jax
import jax.numpy as jnp
from jax import lax
from jax.experimental import pallas as pl
from jax.experimental.pallas import tpu as pltpu

F32 = jnp.float32
BF16 = jnp.bfloat16

D_MODEL = 1024
SEQ = 2048
HEAD_DIM = 64
FOX_HEADS = 12
DIL_HEADS = 12
MEM_HEADS = 4
MEM_HEAD_DIM = 128
MEM_LEN = 256
FOX_W = FOX_HEADS * HEAD_DIM
DIL_W = DIL_HEADS * HEAD_DIM
MEM_W = MEM_HEADS * MEM_HEAD_DIM
MIX_W = FOX_W + DIL_W + MEM_W
DILATIONS = ((128, 1), (512, 4), (2048, 16))
ROPE_THETA = 500000.0
ROPE_DIM = HEAD_DIM // 4
RMS_EPS = 1e-6
NEG_INF = -1e30
IN_SIZES = [FOX_W] * 4 + [FOX_HEADS] + [DIL_W] * 4 + [MEM_W] * 2
IN_W = sum(IN_SIZES)

ADAM_LR = 0.001
ADAM_B1 = 0.9
ADAM_B2 = 0.999
ADAM_EPS = 1e-08
ADAM_WD = 0.01
ADAM_STEP = 10

LANES = 128
N_CHIPS = 4
PW = 7168
PWF = PW + 4 * LANES
C_FQ, C_FK, C_FV, C_FG = 0, 768, 1536, 2304
C_DQ, C_DK, C_DV, C_DG = 3072, 3840, 4608, 5376
C_MQ, C_MG = 6144, 6656
VMEM_LIMIT = 48 * 1024 * 1024


def _cparams(**kw):
    return pltpu.CompilerParams(vmem_limit_bytes=VMEM_LIMIT, **kw)


def _matmul(a, b, *, out_dtype, tm, tn, tk, name, mode="nn"):
    if mode == "tn":
        (kdim, m), n = a.shape, b.shape[1]
        a_spec = pl.BlockSpec((tk, tm), lambda i, j, k: (k, i))
        b_spec = pl.BlockSpec((tk, tn), lambda i, j, k: (k, j))
        dims = _T0
    elif mode == "nt":
        (m, kdim), n = a.shape, b.shape[0]
        a_spec = pl.BlockSpec((tm, tk), lambda i, j, k: (i, k))
        b_spec = pl.BlockSpec((tn, tk), lambda i, j, k: (j, k))
        dims = _NT
    else:
        (m, kdim), n = a.shape, b.shape[1]
        a_spec = pl.BlockSpec((tm, tk), lambda i, j, k: (i, k))
        b_spec = pl.BlockSpec((tk, tn), lambda i, j, k: (k, j))
        dims = (((1,), (0,)), ((), ()))
    nk = kdim // tk
    assert m % tm == 0 and n % tn == 0 and kdim % tk == 0

    def body(a_ref, b_ref, o_ref, *scratch):
        prod = lax.dot_general(a_ref[...], b_ref[...], dims, preferred_element_type=F32)
        if nk == 1:
            o_ref[...] = prod.astype(o_ref.dtype)
            return
        acc_ref, = scratch
        k = pl.program_id(2)

        @pl.when(k == 0)
        def _():
            acc_ref[...] = prod

        @pl.when(k > 0)
        def _():
            acc_ref[...] += prod

        @pl.when(k == nk - 1)
        def _():
            o_ref[...] = acc_ref[...].astype(o_ref.dtype)

    return pl.pallas_call(
        body,
        out_shape=jax.ShapeDtypeStruct((m, n), out_dtype),
        grid=(m // tm, n // tn, nk),
        in_specs=[a_spec, b_spec],
        out_specs=pl.BlockSpec((tm, tn), lambda i, j, k: (i, j)),
        scratch_shapes=[pltpu.VMEM((tm, tn), F32)] if nk > 1 else [],
        compiler_params=_cparams(dimension_semantics=("parallel", "parallel", "arbitrary")),
        name=name,
    )(a, b)


def _rms_fwd(x, g, *, tm, name):
    t, d = x.shape

    def body(x_ref, g_ref, h_ref):
        xv = x_ref[...]
        r = lax.rsqrt(jnp.mean(xv * xv, axis=-1, keepdims=True) + RMS_EPS)
        h_ref[...] = (xv * r * g_ref[...]).astype(h_ref.dtype)

    return pl.pallas_call(
        body,
        out_shape=jax.ShapeDtypeStruct((t, d), BF16),
        grid=(t // tm,),
        in_specs=[pl.BlockSpec((tm, d), lambda i: (i, 0)), pl.BlockSpec((1, d), lambda i: (0, 0))],
        out_specs=pl.BlockSpec((tm, d), lambda i: (i, 0)),
        compiler_params=_cparams(),
        name=name,
    )(x, g)


def _rope_tables():
    half = ROPE_DIM // 2
    pos = jnp.arange(SEQ, dtype=F32)
    inv_freq = 1.0 / (ROPE_THETA ** (jnp.arange(0, ROPE_DIM, 2, dtype=F32) / ROPE_DIM))
    ang = pos[:, None] * inv_freq[None, :]
    cos, sin = jnp.cos(ang), jnp.sin(ang)
    one = jnp.ones((SEQ, HEAD_DIM - ROPE_DIM), F32)
    zero = jnp.zeros((SEQ, HEAD_DIM - ROPE_DIM), F32)
    zh = jnp.zeros((SEQ, half), F32)
    c = jnp.concatenate([cos, cos, one], axis=1)
    s1 = jnp.concatenate([zh, sin, zero], axis=1)
    s2 = jnp.concatenate([-sin, zh, zero], axis=1)
    rep = LANES // HEAD_DIM
    return jnp.tile(c, (1, rep)), jnp.tile(s1, (1, rep)), jnp.tile(s2, (1, rep))


def _rope_apply(t, c, s1, s2, transpose=False):
    n = t.shape[-1]
    rep = n // LANES
    c, s1, s2 = (jnp.tile(u, (1, rep)) for u in (c, s1, s2))
    half = ROPE_DIM // 2
    if not transpose:
        return t * c + pltpu.roll(t, half, 1) * s1 + pltpu.roll(t, n - half, 1) * s2
    return t * c + pltpu.roll(t * s1, n - half, 1) + pltpu.roll(t * s2, half, 1)


def _proj(h, w, tabs, *, n, tm, tn, name):
    t, d = h.shape
    assert C_DQ % tn == 0 and (C_DV - C_DQ) % tn == 0 and (C_DG - C_DQ) % tn == 0
    rope_lo, rope_hi, dil_hi = C_DQ // tn, C_DV // tn, C_DG // tn
    s_blocks = SEQ // tm

    def body(h_ref, w_ref, c_ref, s1_ref, s2_ref, o_ref, f_ref):
        j = pl.program_id(1)
        acc = jnp.dot(h_ref[...], w_ref[...], preferred_element_type=F32)
        is_rope = jnp.logical_and(j >= rope_lo, j < rope_hi)

        @pl.when(is_rope)
        def _():
            r = _rope_apply(acc, c_ref[...], s1_ref[...], s2_ref[...])
            o_ref[...] = r.astype(o_ref.dtype)
            f_ref[...] = r

        @pl.when(jnp.logical_not(is_rope))
        def _():
            o_ref[...] = acc.astype(o_ref.dtype)

        @pl.when(jnp.logical_and(j >= rope_hi, j < dil_hi))
        def _():
            f_ref[...] = acc

    tab_spec = pl.BlockSpec((tm, LANES), lambda i, j: (i % s_blocks, 0))
    f_spec = pl.BlockSpec((tm, tn), lambda i, j: (i, jnp.clip(j - rope_lo, 0, dil_hi - rope_lo - 1)))
    return pl.pallas_call(
        body,
        out_shape=(jax.ShapeDtypeStruct((t, n), BF16), jax.ShapeDtypeStruct((t, 3 * DIL_W), F32)),
        grid=(t // tm, n // tn),
        in_specs=[pl.BlockSpec((tm, d), lambda i, j: (i, 0)), pl.BlockSpec((d, tn), lambda i, j: (0, j)),
                  tab_spec, tab_spec, tab_spec],
        out_specs=(pl.BlockSpec((tm, tn), lambda i, j: (i, j)), f_spec),
        compiler_params=_cparams(dimension_semantics=("parallel", "arbitrary")),
        name=name,
    )(h, w, *tabs)


def _split3(x):
    hi = x.astype(BF16)
    r1 = x - hi.astype(F32)
    mid = r1.astype(BF16)
    lo = (r1 - mid.astype(F32)).astype(BF16)
    return hi, mid, lo


def _dot3(sel, x, sel_is_lhs):
    out = None
    for piece in _split3(x):
        t = jnp.dot(sel, piece, preferred_element_type=F32) if sel_is_lhs else jnp.dot(piece, sel, preferred_element_type=F32)
        out = t if out is None else out + t
    return out


def _flog_fwd(flog, bpad, *, nb, ts, name):
    ns = SEQ // ts

    def body(f_ref, b_ref, c_ref, carry_ref):
        s = pl.program_id(1)

        @pl.when(s == 0)
        def _():
            carry_ref[...] = jnp.zeros_like(carry_ref)

        z = f_ref[...] + b_ref[...]
        logf = jnp.minimum(z, 0.0) - jnp.log(1.0 + jnp.exp(-jnp.abs(z)))
        r = lax.broadcasted_iota(jnp.int32, (ts, ts), 0)
        c = lax.broadcasted_iota(jnp.int32, (ts, ts), 1)
        tri = jnp.where(r >= c, 1.0, 0.0).astype(BF16)
        cs = _dot3(tri, logf, True) + carry_ref[0:1, :]
        carry_ref[...] = jnp.broadcast_to(cs[ts - 1:ts, :], carry_ref.shape)
        c_ref[...] = cs

    return pl.pallas_call(
        body,
        out_shape=jax.ShapeDtypeStruct((nb * SEQ, LANES), F32),
        grid=(nb, ns),
        in_specs=[pl.BlockSpec((ts, LANES), lambda b, s: (b * ns + s, 0)), pl.BlockSpec((1, LANES), lambda b, s: (0, 0))],
        out_specs=pl.BlockSpec((ts, LANES), lambda b, s: (b * ns + s, 0)),
        scratch_shapes=[pltpu.VMEM((8, LANES), F32)],
        compiler_params=_cparams(dimension_semantics=("parallel", "arbitrary")),
        name=name,
    )(flog, bpad)


def _flog_bwd(dcol, flog, bpad, *, nb, ts, name):
    ns = SEQ // ts

    def body(d_ref, f_ref, b_ref, o_ref, gb_ref, carry_ref):
        bi = pl.program_id(0)
        s = pl.program_id(1)

        @pl.when(s == 0)
        def _():
            carry_ref[...] = jnp.zeros_like(carry_ref)

        @pl.when(jnp.logical_and(bi == 0, s == 0))
        def _():
            gb_ref[...] = jnp.zeros_like(gb_ref)

        r = lax.broadcasted_iota(jnp.int32, (ts, ts), 0)
        c = lax.broadcasted_iota(jnp.int32, (ts, ts), 1)
        tri = jnp.where(r <= c, 1.0, 0.0).astype(BF16)
        rc = _dot3(tri, d_ref[...], True) + carry_ref[0:1, :]
        carry_ref[...] = jnp.broadcast_to(rc[0:1, :], carry_ref.shape)
        z = f_ref[...] + b_ref[...]
        dz = rc / (1.0 + jnp.exp(z))
        o_ref[...] = dz.astype(o_ref.dtype)
        gb_ref[...] += jnp.broadcast_to(jnp.sum(dz, axis=0, keepdims=True), gb_ref.shape)

    rev = lambda b, s: (b * ns + (ns - 1 - s), 0)
    return pl.pallas_call(
        body,
        out_shape=(jax.ShapeDtypeStruct((nb * SEQ, LANES), BF16), jax.ShapeDtypeStruct((8, LANES), F32)),
        grid=(nb, ns),
        in_specs=[pl.BlockSpec((ts, LANES), rev), pl.BlockSpec((ts, LANES), rev), pl.BlockSpec((1, LANES), lambda b, s: (0, 0))],
        out_specs=(pl.BlockSpec((ts, LANES), rev), pl.BlockSpec((8, LANES), lambda b, s: (0, 0))),
        scratch_shapes=[pltpu.VMEM((8, LANES), F32)],
        compiler_params=_cparams(dimension_semantics=("arbitrary", "arbitrary")),
        name=name,
    )(dcol, flog, bpad)


MEM_TQ = 256
MEM_SET = 4
MEM_SCALE = 1.0 / math.sqrt(MEM_HEAD_DIM)
assert MEM_HEAD_DIM == LANES and SEQ % (MEM_TQ * MEM_SET) == 0


def _head_masks(nh):
    lane = lax.broadcasted_iota(jnp.int32, (1, LANES), 1)
    return [None] if nh == 1 else [lane < HEAD_DIM, lane >= HEAD_DIM]


def _mem_specs(qoff):
    qspec = pl.BlockSpec((None, SEQ, LANES), lambda b, j: (b, 0, qoff + j))
    kspec = pl.BlockSpec((None, MEM_LEN, LANES), lambda b, j: (b, 0, j))
    vspec = pl.BlockSpec((None, MEM_LEN, LANES), lambda b, j: (b, 0, MEM_HEADS + j))
    ospec = pl.BlockSpec((None, SEQ, LANES), lambda b, j: (b, 0, j))
    return qspec, kspec, vspec, ospec


def _mem_rows(g):
    return [pl.ds(pl.multiple_of((MEM_SET * g + a) * MEM_TQ, MEM_TQ), MEM_TQ) for a in range(MEM_SET)]


def _mem_fwd(p3, mkv3, *, qoff, name):
    nb = p3.shape[0]

    def body(q_ref, k_ref, v_ref, o_ref, lse_ref):
        kb, vb = k_ref[...], v_ref[...]

        def qset(g, c):
            rows = _mem_rows(g)
            ss = [lax.dot_general(q_ref[r, :] * MEM_SCALE, kb, _NT, preferred_element_type=F32) for r in rows]
            for r, s in zip(rows, ss):
                m = jnp.max(s, axis=1, keepdims=True)
                p = jnp.exp(s - m)
                l = jnp.sum(p, axis=1, keepdims=True)
                o_ref[r, :] = jnp.dot(p.astype(BF16), vb, preferred_element_type=F32) / l
                lse_ref[r, :] = jnp.broadcast_to(m + jnp.log(l), (MEM_TQ, LANES))
            return c

        lax.fori_loop(0, SEQ // MEM_TQ // MEM_SET, qset, 0)

    qspec, kspec, vspec, ospec = _mem_specs(qoff)
    osd = jax.ShapeDtypeStruct((nb, SEQ, MEM_W), F32)
    return pl.pallas_call(body, out_shape=(osd, osd), grid=(nb, MEM_HEADS), in_specs=[qspec, kspec, vspec],
                          out_specs=(ospec, ospec), compiler_params=_cparams(dimension_semantics=("parallel", "parallel")),
                          name=name)(p3, mkv3, mkv3)


def _mem_bwd(p3, mkv3, do, o, lse, *, qoff, do_off, name):
    nb = p3.shape[0]

    def body(q_ref, k_ref, v_ref, do_ref, o_ref, lse_ref, dq_ref, dk_ref, dv_ref):
        kb, vb = k_ref[...], v_ref[...]
        ks = kb * MEM_SCALE

        def qset(g, carry):
            dk, dv = carry
            work = []
            for r in _mem_rows(g):
                qs = q_ref[r, :] * MEM_SCALE
                dob = do_ref[r, :].astype(BF16)
                s = lax.dot_general(qs, kb, _NT, preferred_element_type=F32)
                dp = lax.dot_general(dob, vb, _NT, preferred_element_type=F32)
                work.append((r, qs, dob, s, dp))
            for r, qs, dob, s, dp in work:
                delta = jnp.sum(dob.astype(F32) * o_ref[r, :], axis=1, keepdims=True)
                p = jnp.exp(s - lse_ref[r, :][:, 0:1])
                ds = (p * (dp - delta)).astype(BF16)
                dq_ref[r, :] = jnp.dot(ds, ks, preferred_element_type=F32).astype(dq_ref.dtype)
                dk = dk + lax.dot_general(ds, qs, _T0, preferred_element_type=F32)
                dv = dv + lax.dot_general(p.astype(BF16), dob, _T0, preferred_element_type=F32)
            return dk, dv

        z = jnp.zeros((MEM_LEN, LANES), F32)
        dk, dv = lax.fori_loop(0, SEQ // MEM_TQ // MEM_SET, qset, (z, z))
        dk_ref[...] = dk
        dv_ref[...] = dv

    qspec, kspec, vspec, ospec = _mem_specs(qoff)
    dospec = pl.BlockSpec((None, SEQ, LANES), lambda b, j: (b, 0, do_off + j))
    kvo = pl.BlockSpec((None, MEM_LEN, LANES), lambda b, j: (b, 0, j))
    kvsd = jax.ShapeDtypeStruct((nb, MEM_LEN, MEM_W), F32)
    return pl.pallas_call(
        body, out_shape=(jax.ShapeDtypeStruct((nb, SEQ, MEM_W), BF16), kvsd, kvsd), grid=(nb, MEM_HEADS),
        in_specs=[qspec, kspec, vspec, dospec, ospec, ospec], out_specs=(ospec, kvo, kvo),
        compiler_params=_cparams(dimension_semantics=("parallel", "parallel")), name=name)(p3, mkv3, mkv3, do, o, lse)


BLK = 128
NBLK = SEQ // BLK
QK_SCALE = 1.0 / math.sqrt(HEAD_DIM)
DIL_STEPS = tuple(d for _, d in DILATIONS)
assert all(w // d == BLK for w, d in DILATIONS)
_T0 = (((0,), (0,)), ((), ()))
_NT = (((1,), (1,)), ((), ()))


def _stack_heads(a, masks):
    z = jnp.zeros_like(a)
    return jnp.concatenate([jnp.where(masks[0], a, z), jnp.where(masks[1], a, z)], axis=0)


def _tri_bias(lower):
    r = lax.broadcasted_iota(jnp.int32, (BLK, BLK), 0)
    c = lax.broadcasted_iota(jnp.int32, (BLK, BLK), 1)
    return jnp.where((c <= r) if lower else (c >= r), 0.0, NEG_INF).astype(F32)


def _dil_rows(r, i, d):
    start = r + i * (BLK * d)
    return pl.ds(start, BLK) if d == 1 else pl.ds(start, BLK, stride=d)


DIL_SET = 4


def _dil_sets(d, fn):
    nbk = SEQ // d // BLK
    if d == 1:
        def gbody(g, c):
            fn([(0, DIL_SET * g + a, None if a == 0 else True) for a in range(DIL_SET)])
            return c
        lax.fori_loop(0, nbk // DIL_SET, gbody, 0)
    elif nbk > 1:
        assert nbk == DIL_SET
        def rbody(r, c):
            fn([(r, i, i > 0) for i in range(nbk)])
            return c
        lax.fori_loop(0, d, rbody, 0)
    else:
        def rbody(rr, c):
            fn([(DIL_SET * rr + a, 0, False) for a in range(DIL_SET)])
            return c
        lax.fori_loop(0, d // DIL_SET, rbody, 0)


def _dil_key_tiles(r, i, d, has_prev, qrows, tri_cur, tri_prev):
    tiles = [(qrows, tri_cur)]
    if has_prev is None:
        tiles.append((_dil_rows(r, jnp.maximum(i - 1, 0), d), tri_prev + jnp.where(i > 0, 0.0, NEG_INF)))
    elif has_prev:
        tiles.append((_dil_rows(r, i - 1, d), tri_prev))
    return tiles


def _dil_fwd(qkv, *, name):
    nb = qkv.shape[0]
    ncol = DIL_W // LANES
    hd = HEAD_DIM

    def body(q_ref, k_ref, v_ref, o_ref, lse_ref, m_ref, l_ref, a_ref):
        masks = _head_masks(2)
        tri_cur, tri_prev = _tri_bias(True), _tri_bias(False)
        for pi, d in enumerate(DIL_STEPS):
            first, last = pi == 0, pi == len(DIL_STEPS) - 1

            def qset(blocks, d=d, first=first, last=last):
                work = []
                for r, i, has_prev in blocks:
                    qrows = _dil_rows(r, i, d)
                    qcat = _stack_heads((q_ref[qrows, :] * QK_SCALE).astype(BF16), masks)
                    ss, krs = [], []
                    for krows, bias in _dil_key_tiles(r, i, d, has_prev, qrows, tri_cur, tri_prev):
                        s = lax.dot_general(qcat, k_ref[krows, :].astype(BF16), _NT, preferred_element_type=F32)
                        ss.append((s[:BLK] + bias, s[BLK:] + bias))
                        krs.append(krows)
                    work.append((qrows, ss, krs))
                for qrows, ss, krs in work:
                    e0 = ss[0][0] if len(ss) == 1 else jnp.maximum(ss[0][0], ss[1][0])
                    e1 = ss[0][1] if len(ss) == 1 else jnp.maximum(ss[0][1], ss[1][1])
                    n0 = jnp.max(e0, axis=1, keepdims=True)
                    n1 = jnp.max(e1, axis=1, keepdims=True)
                    if not first:
                        mo, lo = m_ref[qrows, :], l_ref[qrows, :]
                        m0, m1 = mo[:, 0:1], mo[:, hd:hd + 1]
                        n0, n1 = jnp.maximum(n0, m0), jnp.maximum(n1, m1)
                        a0, a1 = jnp.exp(m0 - n0), jnp.exp(m1 - n1)
                    ps = [(jnp.exp(s0 - n0), jnp.exp(s1 - n1)) for s0, s1 in ss]
                    t0 = ps[0][0] if len(ps) == 1 else ps[0][0] + ps[1][0]
                    t1 = ps[0][1] if len(ps) == 1 else ps[0][1] + ps[1][1]
                    l0 = jnp.sum(t0, axis=1, keepdims=True)
                    l1 = jnp.sum(t1, axis=1, keepdims=True)
                    acc = None
                    for (p0, p1), krows in zip(ps, krs):
                        vcat = _stack_heads(v_ref[krows, :].astype(BF16), masks)
                        pv = jnp.dot(jnp.concatenate([p0, p1], axis=1).astype(BF16), vcat, preferred_element_type=F32)
                        acc = pv if acc is None else acc + pv
                    if not first:
                        l0 = l0 + a0 * lo[:, 0:1]
                        l1 = l1 + a1 * lo[:, hd:hd + 1]
                        acc = acc + a_ref[qrows, :] * jnp.where(masks[0], a0, a1)
                    if last:
                        o_ref[qrows, :] = acc / jnp.where(masks[0], l0, l1)
                        lse_ref[qrows, :] = jnp.where(masks[0], n0 + jnp.log(l0), n1 + jnp.log(l1))
                    else:
                        m_ref[qrows, :] = jnp.where(masks[0], n0, n1)
                        l_ref[qrows, :] = jnp.where(masks[0], l0, l1)
                        a_ref[qrows, :] = acc

            _dil_sets(d, qset)

    spec = lambda off: pl.BlockSpec((None, SEQ, LANES), lambda b, j: (b, 0, off + j))
    ospec = pl.BlockSpec((None, SEQ, LANES), lambda b, j: (b, 0, j))
    osd = jax.ShapeDtypeStruct((nb, SEQ, DIL_W), F32)
    return pl.pallas_call(
        body, out_shape=(osd, osd), grid=(nb, ncol),
        in_specs=[spec(0), spec(ncol), spec(2 * ncol)], out_specs=(ospec, ospec),
        scratch_shapes=[pltpu.VMEM((SEQ, LANES), F32)] * 3,
        compiler_params=_cparams(dimension_semantics=("parallel", "parallel")), name=name,
    )(qkv, qkv, qkv)


def _dil_bwd(qkv, do, o, lse, tabs, *, do_off, name):
    nb = qkv.shape[0]
    ncol = DIL_W // LANES
    hd = HEAD_DIM

    def body(q_ref, k_ref, v_ref, do_ref, o_ref, lse_ref, c_ref, s1_ref, s2_ref, dqo_ref, dko_ref, dvo_ref,
             dq_ref, dk_ref, dv_ref, dl_ref, dof_ref):
        masks = _head_masks(2)
        tri_cur, tri_prev = _tri_bias(True), _tri_bias(False)
        dq_ref[...] = jnp.zeros_like(dq_ref)
        dk_ref[...] = jnp.zeros_like(dk_ref)
        dv_ref[...] = jnp.zeros_like(dv_ref)

        def delta_body(i, c):
            rows = pl.ds(pl.multiple_of(i * BLK, BLK), BLK)
            dof = do_ref[rows, :].astype(F32)
            dof_ref[rows, :] = dof
            prod = dof * o_ref[rows, :]
            z = jnp.zeros_like(prod)
            dl_ref[rows, :] = jnp.where(masks[0], jnp.sum(jnp.where(masks[0], prod, z), axis=1, keepdims=True),
                                        jnp.sum(jnp.where(masks[1], prod, z), axis=1, keepdims=True))
            return c

        lax.fori_loop(0, NBLK, delta_body, 0)

        for d in DIL_STEPS:
            def qset(blocks, d=d):
                work = []
                for r, i, has_prev in blocks:
                    qrows = _dil_rows(r, i, d)
                    qcat = _stack_heads((q_ref[qrows, :] * QK_SCALE).astype(BF16), masks)
                    docat = _stack_heads(dof_ref[qrows, :].astype(BF16), masks)
                    tiles = []
                    for krows, bias in _dil_key_tiles(r, i, d, has_prev, qrows, tri_cur, tri_prev):
                        s = lax.dot_general(qcat, k_ref[krows, :].astype(BF16), _NT, preferred_element_type=F32)
                        dp = lax.dot_general(docat, v_ref[krows, :].astype(BF16), _NT, preferred_element_type=F32)
                        tiles.append((krows, s, dp, bias))
                    work.append((qrows, qcat, docat, tiles))
                for qrows, qcat, docat, tiles in work:
                    lseb, dlb = lse_ref[qrows, :], dl_ref[qrows, :]
                    lse0, lse1 = lseb[:, 0:1], lseb[:, hd:hd + 1]
                    dl0, dl1 = dlb[:, 0:1], dlb[:, hd:hd + 1]
                    dq = None
                    for krows, s, dp, bias in tiles:
                        p0 = jnp.exp(s[:BLK] + bias - lse0)
                        p1 = jnp.exp(s[BLK:] + bias - lse1)
                        ds0 = p0 * (dp[:BLK] - dl0)
                        ds1 = p1 * (dp[BLK:] - dl1)
                        ds0b, ds1b = ds0.astype(BF16), ds1.astype(BF16)
                        pcat = jnp.concatenate([p0.astype(BF16), p1.astype(BF16)], axis=0)
                        dscat = jnp.concatenate([ds0b, ds1b], axis=0)
                        dv_ref[krows, :] += lax.dot_general(pcat, docat, _T0, preferred_element_type=F32)
                        dk_ref[krows, :] += lax.dot_general(dscat, qcat, _T0, preferred_element_type=F32)
                        dsrow = jnp.concatenate([ds0b, ds1b], axis=1)
                        kcat = _stack_heads((k_ref[krows, :] * QK_SCALE).astype(BF16), masks)
                        t = jnp.dot(dsrow, kcat, preferred_element_type=F32)
                        dq = t if dq is None else dq + t
                    dq_ref[qrows, :] += dq

            _dil_sets(d, qset)

        def out_body(i, c):
            rows = pl.ds(pl.multiple_of(i * BLK, BLK), BLK)
            tab = (c_ref[rows, :], s1_ref[rows, :], s2_ref[rows, :])
            dqo_ref[rows, :] = _rope_apply(dq_ref[rows, :], *tab, transpose=True).astype(dqo_ref.dtype)
            dko_ref[rows, :] = _rope_apply(dk_ref[rows, :], *tab, transpose=True).astype(dko_ref.dtype)
            dvo_ref[rows, :] = dv_ref[rows, :].astype(dvo_ref.dtype)
            return c

        lax.fori_loop(0, NBLK, out_body, 0)

    spec = lambda off: pl.BlockSpec((None, SEQ, LANES), lambda b, j: (b, 0, off + j))
    ospec = pl.BlockSpec((None, SEQ, LANES), lambda b, j: (b, 0, j))
    tspec = pl.BlockSpec((SEQ, LANES), lambda b, j: (0, 0))
    osd = jax.ShapeDtypeStruct((nb, SEQ, DIL_W), BF16)
    return pl.pallas_call(
        body, out_shape=(osd, osd, osd), grid=(nb, ncol),
        in_specs=[spec(0), spec(ncol), spec(2 * ncol), spec(do_off), ospec, ospec, tspec, tspec, tspec],
        out_specs=(ospec, ospec, ospec),
        scratch_shapes=[pltpu.VMEM((SEQ, LANES), F32)] * 5,
        compiler_params=_cparams(dimension_semantics=("parallel", "parallel")), name=name,
    )(qkv, qkv, qkv, do, o, lse, *tabs)


FOX_GROUP = 4
assert NBLK % FOX_GROUP == 0
_FOX_COLS = tuple(c // LANES for c in (C_FQ, C_FK, C_FV))


def _fox_specs():
    cols = [pl.BlockSpec((None, SEQ, LANES), (lambda b, j, off=off: (b, 0, off + j))) for off in _FOX_COLS]
    ospec = pl.BlockSpec((None, SEQ, LANES), lambda b, j: (b, 0, j))
    crspec = pl.BlockSpec((None, None, NBLK, 8, BLK), lambda b, j: (b, j, 0, 0, 0))
    return cols, ospec, crspec


def _fox_key_rows(t, e):
    return pl.ds(pl.multiple_of((FOX_GROUP * t + e) * BLK, BLK), BLK)


def _fox_fwd(p3, crow, *, name):
    nb = p3.shape[0]
    g = FOX_GROUP

    def body(q_ref, k_ref, v_ref, cr_ref, o_ref, lse_ref):
        masks = _head_masks(2)
        tri = _tri_bias(True)

        def qk(qcat, t):
            return tuple(lax.dot_general(qcat, k_ref[_fox_key_rows(t, e), :], _NT, preferred_element_type=F32) for e in range(g))

        def consume(ss, t, state, nblk, diag):
            m0, m1, l0, l1, acc = state
            us = []
            for e in range(nblk):
                cr = cr_ref[g * t + e]
                u0 = ss[e][:BLK] - cr[0:1, :]
                u1 = ss[e][BLK:] - cr[1:2, :]
                if diag and e == nblk - 1:
                    u0, u1 = u0 + tri, u1 + tri
                us.append((u0, u1))
            x0 = functools.reduce(jnp.maximum, [u[0] for u in us])
            x1 = functools.reduce(jnp.maximum, [u[1] for u in us])
            n0 = jnp.maximum(m0, jnp.max(x0, axis=1, keepdims=True))
            n1 = jnp.maximum(m1, jnp.max(x1, axis=1, keepdims=True))
            a0, a1 = jnp.exp(m0 - n0), jnp.exp(m1 - n1)
            acc = acc * jnp.where(masks[0], a0, a1)
            t0 = t1 = None
            for e in range(nblk):
                p0, p1 = jnp.exp(us[e][0] - n0), jnp.exp(us[e][1] - n1)
                t0 = p0 if t0 is None else t0 + p0
                t1 = p1 if t1 is None else t1 + p1
                pcat = jnp.concatenate([p0, p1], axis=1)
                hi = pcat.astype(BF16)
                lo = (pcat - hi.astype(F32)).astype(BF16)
                vcat = _stack_heads(v_ref[_fox_key_rows(t, e), :], masks)
                acc = acc + jnp.dot(hi, vcat, preferred_element_type=F32) + jnp.dot(lo, vcat, preferred_element_type=F32)
            l0 = a0 * l0 + jnp.sum(t0, axis=1, keepdims=True)
            l1 = a1 * l1 + jnp.sum(t1, axis=1, keepdims=True)
            return n0, n1, l0, l1, acc

        def gbody(ng, c):
            neg = jnp.full((BLK, 1), NEG_INF, F32)
            z1 = jnp.zeros((BLK, 1), F32)
            rows = [pl.ds(pl.multiple_of((g * ng + a) * BLK, BLK), BLK) for a in range(g)]
            qcats = [_stack_heads(q_ref[rows[a], :] * QK_SCALE, masks) for a in range(g)]
            first = [qk(qcats[a], 0) for a in range(g)]
            done = []
            for a in range(g):
                def step(t, cc, qcat=qcats[a]):
                    ss, st = cc
                    nxt = qk(qcat, t + 1)
                    return nxt, consume(ss, t, st, g, False)

                done.append(lax.fori_loop(0, ng, step, (first[a], (neg, neg, z1, z1, jnp.zeros((BLK, LANES), F32)))))
            for a in range(g):
                ss, state = done[a]
                m0, m1, l0, l1, acc = consume(ss, ng, state, a + 1, True)
                o_ref[rows[a], :] = acc / jnp.where(masks[0], l0, l1)
                lse_ref[rows[a], :] = jnp.where(masks[0], m0 + jnp.log(l0), m1 + jnp.log(l1))
            return c

        lax.fori_loop(0, NBLK // g, gbody, 0)

    cols, ospec, crspec = _fox_specs()
    osd = jax.ShapeDtypeStruct((nb, SEQ, FOX_W), F32)
    return pl.pallas_call(
        body, out_shape=(osd, osd), grid=(nb, FOX_W // LANES), in_specs=cols + [crspec], out_specs=(ospec, ospec),
        compiler_params=_cparams(dimension_semantics=("parallel", "parallel")), name=name,
    )(p3, p3, p3, crow)


def _fox_bwd(p3, crow, do, o, lse, *, do_off, name):
    nb = p3.shape[0]
    g = FOX_GROUP
    hd = HEAD_DIM

    def body(q_ref, k_ref, v_ref, cr_ref, do_ref, o_ref, lse_ref, dq_ref, dko_ref, dvo_ref, dcr_ref, dk_ref, dv_ref):
        masks = _head_masks(2)
        tri = _tri_bias(True)
        dk_ref[...] = jnp.zeros_like(dk_ref)
        dv_ref[...] = jnp.zeros_like(dv_ref)
        dcr_ref[...] = jnp.zeros_like(dcr_ref)

        def products(qcat, docat, t):
            out = []
            for e in range(g):
                krows = _fox_key_rows(t, e)
                out.append(lax.dot_general(qcat, k_ref[krows, :], _NT, preferred_element_type=F32))
                out.append(lax.dot_general(docat, v_ref[krows, :], _NT, preferred_element_type=F32))
            return tuple(out)

        def consume(prod, t, ctx, dq, nblk, diag):
            qcat, docat, lse0, lse1, dl0, dl1 = ctx
            for e in range(nblk):
                jb = g * t + e
                krows = _fox_key_rows(t, e)
                s, dp = prod[2 * e], prod[2 * e + 1]
                cr = cr_ref[jb]
                u0 = s[:BLK] - cr[0:1, :]
                u1 = s[BLK:] - cr[1:2, :]
                if diag and e == nblk - 1:
                    u0, u1 = u0 + tri, u1 + tri
                p0 = jnp.exp(u0 - lse0)
                p1 = jnp.exp(u1 - lse1)
                ds0 = p0 * (dp[:BLK] - dl0)
                ds1 = p1 * (dp[BLK:] - dl1)
                dcr_ref[jb, 0:1, :] += jnp.sum(ds0, axis=0, keepdims=True)
                dcr_ref[jb, 1:2, :] += jnp.sum(ds1, axis=0, keepdims=True)
                ds0b, ds1b = ds0.astype(BF16), ds1.astype(BF16)
                pcat = jnp.concatenate([p0.astype(BF16), p1.astype(BF16)], axis=0)
                dscat = jnp.concatenate([ds0b, ds1b], axis=0)
                dv_ref[krows, :] += lax.dot_general(pcat, docat, _T0, preferred_element_type=F32)
                dk_ref[krows, :] += lax.dot_general(dscat, qcat, _T0, preferred_element_type=F32)
                dsrow = jnp.concatenate([ds0b, ds1b], axis=1)
                dq = dq + jnp.dot(dsrow, _stack_heads(k_ref[krows, :] * QK_SCALE, masks), preferred_element_type=F32)
            return dq

        def gbody(ng, c):
            ctxs, rows = [], []
            for a in range(g):
                r = pl.ds(pl.multiple_of((g * ng + a) * BLK, BLK), BLK)
                qcat = _stack_heads(q_ref[r, :] * QK_SCALE, masks)
                dob = do_ref[r, :].astype(BF16)
                prod = dob.astype(F32) * o_ref[r, :]
                z = jnp.zeros_like(prod)
                dl0 = jnp.sum(jnp.where(masks[0], prod, z), axis=1, keepdims=True)
                dl1 = jnp.sum(jnp.where(masks[1], prod, z), axis=1, keepdims=True)
                lseb = lse_ref[r, :]
                ctxs.append((qcat, _stack_heads(dob, masks), lseb[:, 0:1], lseb[:, hd:hd + 1], dl0, dl1))
                rows.append(r)
            first = [products(ctxs[a][0], ctxs[a][1], 0) for a in range(g)]
            done = []
            for a in range(g):
                def step(t, cc, ctx=ctxs[a]):
                    pr, dq = cc
                    nxt = products(ctx[0], ctx[1], t + 1)
                    return nxt, consume(pr, t, ctx, dq, g, False)

                done.append(lax.fori_loop(0, ng, step, (first[a], jnp.zeros((BLK, LANES), F32))))
            for a in range(g):
                pr, dq = done[a]
                dq_ref[rows[a], :] = consume(pr, ng, ctxs[a], dq, a + 1, True).astype(dq_ref.dtype)
            return c

        lax.fori_loop(0, NBLK // g, gbody, 0)
        dko_ref[...] = dk_ref[...].astype(dko_ref.dtype)
        dvo_ref[...] = dv_ref[...].astype(dvo_ref.dtype)

    cols, ospec, crspec = _fox_specs()
    dospec = pl.BlockSpec((None, SEQ, LANES), lambda b, j: (b, 0, do_off + j))
    osd = jax.ShapeDtypeStruct((nb, SEQ, FOX_W), BF16)
    return pl.pallas_call(
        body, out_shape=(osd, osd, osd, jax.ShapeDtypeStruct((nb, FOX_W // LANES, NBLK, 8, BLK), F32)),
        grid=(nb, FOX_W // LANES), in_specs=cols + [crspec, dospec, ospec, ospec], out_specs=(ospec, ospec, ospec, crspec),
        scratch_shapes=[pltpu.VMEM((SEQ, LANES), F32)] * 2,
        compiler_params=_cparams(dimension_semantics=("parallel", "parallel")), name=name,
    )(p3, p3, p3, crow, do, o, lse)


_B1, _B2 = FOX_W // LANES, (FOX_W + DIL_W) // LANES


def _dy_gate_bwd(dx2b, wo, fox, dil, memo, p16, *, tm, tn, name):
    t, d = dx2b.shape
    assert FOX_W % tn == 0 and DIL_W % tn == 0 and MEM_W % tn == 0 and all(c % tn == 0 for c in (C_FG, C_DG, C_MG))
    n1, n2, n3 = FOX_W // tn, (FOX_W + DIL_W) // tn, MIX_W // tn

    def body(dx_ref, w_ref, f_ref, d_ref, m_ref, g_ref, da_ref, dg_ref):
        j = pl.program_id(1)
        dyv = lax.dot_general(dx_ref[...], w_ref[...], _NT, preferred_element_type=F32)
        a = jnp.where(j < n1, f_ref[...], jnp.where(j < n2, d_ref[...], m_ref[...]))
        gt = g_ref[...].astype(F32)
        sg = 1.0 / (1.0 + jnp.exp(-gt))
        da_ref[...] = (dyv * gt * sg).astype(da_ref.dtype)
        dg_ref[...] = (dyv * a * sg * (1.0 + gt * (1.0 - sg))).astype(dg_ref.dtype)

    def gcol(j):
        return jnp.where(j < n1, C_FG // tn + j, jnp.where(j < n2, C_DG // tn + j - n1, C_MG // tn + j - n2))

    tile = pl.BlockSpec((tm, tn), lambda i, j: (i, j))
    return pl.pallas_call(
        body,
        out_shape=(jax.ShapeDtypeStruct((t, MIX_W), BF16), jax.ShapeDtypeStruct((t, MIX_W), BF16)),
        grid=(t // tm, n3),
        in_specs=[pl.BlockSpec((tm, d), lambda i, j: (i, 0)), pl.BlockSpec((tn, d), lambda i, j: (j, 0)),
                  pl.BlockSpec((tm, tn), lambda i, j: (i, jnp.minimum(j, n1 - 1))),
                  pl.BlockSpec((tm, tn), lambda i, j: (i, jnp.clip(j - n1, 0, n2 - n1 - 1))),
                  pl.BlockSpec((tm, tn), lambda i, j: (i, jnp.clip(j - n2, 0, n3 - n2 - 1))),
                  pl.BlockSpec((tm, tn), lambda i, j: (i, gcol(j)))],
        out_specs=(tile, tile),
        compiler_params=_cparams(dimension_semantics=("parallel", "parallel")),
        name=name,
    )(dx2b, wo, fox, dil, memo, p16)


def _silu(g):
    return g / (1.0 + jnp.exp(-g))


def _out_loss(fox, dil, memo, p16, wo, x, tgt, gfin, *, tm, name):
    t, d = x.shape
    n_feat = float(d)

    def body(f_ref, d_ref, m_ref, fg_ref, dg_ref, mg_ref, w_ref, x_ref, t_ref, g_ref, y_ref, dx_ref, dxb_ref, st_ref):
        i = pl.program_id(0)

        @pl.when(i == 0)
        def _():
            st_ref[...] = jnp.zeros_like(st_ref)

        y = jnp.concatenate([(a_ref[...] * _silu(gt_ref[...].astype(F32))).astype(BF16)
                             for a_ref, gt_ref in ((f_ref, fg_ref), (d_ref, dg_ref), (m_ref, mg_ref))], axis=1)
        y_ref[...] = y
        x2 = x_ref[...] + jnp.dot(y, w_ref[...], preferred_element_type=F32)
        r = lax.rsqrt(jnp.mean(x2 * x2, axis=-1, keepdims=True) + RMS_EPS)
        nrm = x2 * r
        gv = g_ref[...]
        err = nrm * gv - t_ref[...]
        dout = err * (1.0 / n_feat)
        dn = dout * gv
        dx2 = r * (dn - nrm * jnp.mean(dn * nrm, axis=-1, keepdims=True))
        dx_ref[...] = dx2
        dxb_ref[...] = dx2.astype(dxb_ref.dtype)
        st_ref[0:1, :] += jnp.sum(dout * nrm, axis=0, keepdims=True)
        st_ref[1:2, :] += (0.5 / n_feat) * jnp.sum(err * err, axis=0, keepdims=True)

    row = pl.BlockSpec((tm, d), lambda i: (i, 0))
    whole = lambda w: pl.BlockSpec((tm, w), lambda i: (i, 0))
    gate = lambda w, col: pl.BlockSpec((tm, w), lambda i: (i, col // w))
    return pl.pallas_call(
        body,
        out_shape=(jax.ShapeDtypeStruct((t, MIX_W), BF16), jax.ShapeDtypeStruct((t, d), F32), jax.ShapeDtypeStruct((t, d), BF16),
                   jax.ShapeDtypeStruct((8, d), F32)),
        grid=(t // tm,),
        in_specs=[whole(FOX_W), whole(DIL_W), whole(MEM_W), gate(FOX_W, C_FG), gate(DIL_W, C_DG), gate(MEM_W, C_MG),
                  pl.BlockSpec((MIX_W, d), lambda i: (0, 0)), row, row, pl.BlockSpec((1, d), lambda i: (0, 0))],
        out_specs=(pl.BlockSpec((tm, MIX_W), lambda i: (i, 0)), row, row, pl.BlockSpec((8, d), lambda i: (0, 0))),
        compiler_params=_cparams(dimension_semantics=("arbitrary",)),
        name=name,
    )(fox, dil, memo, p16, p16, p16, wo, x, tgt, gfin)


def _dh_rms_bwd(dp, w, x, g, resid, *, tm, name):
    t, d = x.shape
    kdim = dp.shape[1]

    def body(*refs):
        if resid is not None:
            dp_ref, w_ref, x_ref, g_ref, r_ref, dx_ref, gg_ref = refs
        else:
            dp_ref, w_ref, x_ref, g_ref, dx_ref, gg_ref = refs

        @pl.when(pl.program_id(0) == 0)
        def _():
            gg_ref[...] = jnp.zeros_like(gg_ref)

        dh = lax.dot_general(dp_ref[...], w_ref[...], _NT, preferred_element_type=F32)
        xv = x_ref[...]
        r = lax.rsqrt(jnp.mean(xv * xv, axis=-1, keepdims=True) + RMS_EPS)
        nrm = xv * r
        dn = dh * g_ref[...]
        dx = r * (dn - nrm * jnp.mean(dn * nrm, axis=-1, keepdims=True))
        if resid is not None:
            dx = dx + r_ref[...]
        dx_ref[...] = dx
        gg_ref[0:1, :] += jnp.sum(dh * nrm, axis=0, keepdims=True)

    row = pl.BlockSpec((tm, d), lambda i: (i, 0))
    in_specs = [pl.BlockSpec((tm, kdim), lambda i: (i, 0)),
                pl.BlockSpec((d, kdim), lambda i: (0, 0), pipeline_mode=pl.Buffered(1)), row,
                pl.BlockSpec((1, d), lambda i: (0, 0))]
    args = [dp, w, x, g]
    if resid is not None:
        in_specs.append(row)
        args.append(resid)
    return pl.pallas_call(
        body,
        out_shape=(jax.ShapeDtypeStruct((t, d), F32), jax.ShapeDtypeStruct((8, d), F32)),
        grid=(t // tm,),
        in_specs=in_specs,
        out_specs=(row, pl.BlockSpec((8, d), lambda i: (0, 0))),
        compiler_params=_cparams(dimension_semantics=("arbitrary",)),
        name=name,
    )(*args)


_FLOG0 = 4 * FOX_W
_W_IN_SEGMENTS = ((0, _FLOG0, 0), (_FLOG0, _FLOG0 + FOX_HEADS, PW), (_FLOG0 + FOX_HEADS, IN_W, C_DQ))
SHARD_W = IN_W // N_CHIPS


def _rearrange_w_in(shards):
    def cols(lo, hi):
        parts = []
        for k in range(N_CHIPS):
            a, b = max(lo, k * SHARD_W), min(hi, (k + 1) * SHARD_W)
            if a < b:
                parts.append(shards[k][:, a - k * SHARD_W:b - k * SHARD_W])
        return parts

    (a0, a1, _), (f0, f1, _), (b0, b1, _) = _W_IN_SEGMENTS
    pad = jnp.zeros((shards[0].shape[0], PWF - PW - FOX_HEADS), shards[0].dtype)
    return jnp.concatenate(cols(a0, a1) + cols(b0, b1) + cols(f0, f1) + [pad], axis=1)


def _w_in_grad_slabs(g):
    slabs = []
    for k in range(N_CHIPS):
        parts = []
        for lo, hi, at in _W_IN_SEGMENTS:
            a, b = max(lo, k * SHARD_W), min(hi, (k + 1) * SHARD_W)
            if a < b:
                parts.append(g[:, at + a - lo:at + b - lo])
        slabs.append(jnp.concatenate(parts, axis=1))
    return jnp.stack(slabs, axis=0)


def _local_grads(x, mem, norm_g, w_r, b_forget, mem_norm_g, w_kv, w_o, final_norm_g, tgt, start_reduce=None,
                 early_token=None, late_weights=None):
    nb = x.shape[0]
    t = nb * SEQ
    x2d = x.reshape(t, D_MODEL)
    tgt2d = tgt.reshape(t, D_MODEL)
    tabs = _rope_tables()
    bpad = jnp.pad(b_forget.reshape(1, FOX_HEADS), ((0, 0), (0, LANES - FOX_HEADS)))

    gain0 = norm_g.reshape(1, D_MODEL)
    if early_token is not None:
        gain0 = gain0 + early_token[0:1, 0:1]
    h = _rms_fwd(x2d, gain0, tm=512, name="rms_x")
    p16, dqkv = _proj(h, w_r, tabs, n=PWF, tm=1024, tn=768, name="proj")
    flog = _matmul(h, w_r[:, PW:PW + LANES], out_dtype=F32, tm=1024, tn=LANES, tk=D_MODEL, name="proj_flog")
    c12 = _flog_fwd(flog, bpad, nb=nb, ts=256, name="flog_fwd")

    crow = c12[:, :FOX_HEADS].reshape(nb, NBLK, BLK, FOX_HEADS // 2, 2).transpose(0, 3, 1, 4, 2)
    crow = jnp.pad(crow, ((0, 0), (0, 0), (0, 0), (0, 6), (0, 0)))
    p3 = p16.reshape(nb, SEQ, PWF)
    fox, fox_lse = _fox_fwd(p3, crow, name="fox_fwd")
    if late_weights is not None:
        w_kv, w_o = late_weights(fox_lse)

    dqkv3 = dqkv.reshape(nb, SEQ, 3 * DIL_W)
    dil, dil_lse = _dil_fwd(dqkv3, name="dil_fwd")

    mh = _rms_fwd(mem.reshape(nb * MEM_LEN, D_MODEL), mem_norm_g.reshape(1, D_MODEL), tm=nb * MEM_LEN, name="rms_mem")
    mkv = _matmul(mh, w_kv, out_dtype=BF16, tm=nb * MEM_LEN, tn=512, tk=D_MODEL, name="mem_kv")
    mkv3 = mkv.reshape(nb, MEM_LEN, 2 * MEM_W)
    memo, mem_lse = _mem_fwd(p3, mkv3, qoff=C_MQ // LANES, name="mem_fwd")

    fox2, dil2, memo2 = fox.reshape(t, FOX_W), dil.reshape(t, DIL_W), memo.reshape(t, MEM_W)
    y, dx2, dx2b, st = _out_loss(fox2, dil2, memo2, p16, w_o, x2d, tgt2d, final_norm_g.reshape(1, D_MODEL), tm=256,
                                 name="out_loss")

    g_wo = _matmul(y, dx2b, mode="tn", out_dtype=BF16, tm=1024, tn=512, tk=t, name="grad_w_out")
    datt, dgate = _dy_gate_bwd(dx2b, w_o, fox2, dil2, memo2, p16, tm=2048, tn=256, name="dy_gate_bwd")
    datt3 = datt.reshape(nb, SEQ, MIX_W)

    dfq, dfk, dfv, dcr = _fox_bwd(p3, crow, datt3, fox, fox_lse, do_off=0, name="fox_bwd")
    dcol = -dcr[:, :, :, :2, :].transpose(0, 2, 4, 1, 3).reshape(t, FOX_HEADS)
    dcol = jnp.pad(dcol, ((0, 0), (0, LANES - FOX_HEADS)))
    dflog, gb = _flog_bwd(dcol, flog, bpad, nb=nb, ts=256, name="flog_bwd")

    ddq, ddk, ddv = _dil_bwd(dqkv3, datt3, dil, dil_lse, tabs, do_off=_B1, name="dil_bwd")

    dmq, dmk, dmv = _mem_bwd(p3, mkv3, datt3, memo, mem_lse, qoff=C_MQ // LANES, do_off=_B2, name="mem_bwd")
    dmkv = jnp.concatenate([dmk, dmv], axis=-1).reshape(nb * MEM_LEN, 2 * MEM_W).astype(BF16)
    g_wkv = _matmul(mh, dmkv, mode="tn", out_dtype=BF16, tm=512, tn=512, tk=nb * MEM_LEN, name="grad_w_kv")
    _, gmn = _dh_rms_bwd(dmkv, w_kv, mem.reshape(nb * MEM_LEN, D_MODEL), mem_norm_g.reshape(1, D_MODEL), None,
                         tm=nb * MEM_LEN, name="mem_rms_bwd")

    flat = lambda a: a.reshape(t, -1)
    dp = jnp.concatenate([flat(dfq), flat(dfk), flat(dfv), dgate[:, :FOX_W], flat(ddq), flat(ddk), flat(ddv),
                          dgate[:, FOX_W:FOX_W + DIL_W], flat(dmq), dgate[:, FOX_W + DIL_W:], dflog,
                          jnp.zeros((t, PWF - PW - LANES), BF16)], axis=1)
    g_wr = _matmul(h, dp, mode="tn", out_dtype=BF16, tm=D_MODEL, tn=512, tk=t, name="grad_w_in")
    gain = norm_g.reshape(1, D_MODEL)
    if start_reduce is not None:
        gain = gain + start_reduce(g_wr, g_wkv, g_wo)[0:1, 0:1]
    gx, gng = _dh_rms_bwd(dp, w_r, x2d, gain, dx2, tm=256, name="in_rms_bwd")

    gb_row = jnp.pad(gb[0:1, :], ((0, 0), (0, D_MODEL - LANES)))
    small = jnp.concatenate([gng[0:1], gmn[0:1], st[0:1], gb_row, st[1:2], jnp.zeros((3, D_MODEL), F32)], axis=0)
    return gx.reshape(nb, SEQ, D_MODEL), g_wr, g_wkv, g_wo, small


MESH = pl.DeviceIdType.MESH
ANY = pl.BlockSpec(memory_space=pl.ANY)


def _place():
    x, y, c = lax.axis_index("x"), lax.axis_index("y"), lax.axis_index("c")
    other_chips = [(1 - x, y), (x, 1 - y), (1 - x, 1 - y)]
    return x, y, c, other_chips


def _gather_weights(shards):
    n = len(shards)

    def body(*refs):
        in_refs, out_refs = refs[:n], refs[n:2 * n]
        send_sems, recv_sems = refs[2 * n:]
        x, y, c, chips = _place()
        me_chip = 2 * x + y
        sibling = (x, y, 1 - c)

        def half(ref, pc, rows):
            return ref.at[pl.ds(pc * (rows // 2), rows // 2), :]

        def rcopy(k, src, dst, to):
            return pltpu.make_async_remote_copy(src_ref=src, dst_ref=dst, send_sem=send_sems.at[k], recv_sem=recv_sems.at[k],
                                                device_id=to, device_id_type=MESH)

        sends = []
        for t in range(n):
            rows = shards[t].shape[0]
            for j, chip in enumerate(chips):
                cp = rcopy(6 * t + j, half(in_refs[t], c, rows), half(out_refs[t].at[me_chip], c, rows), (*chip, c))
                cp.start()
                sends.append(cp)
        for t in range(n):
            rows = shards[t].shape[0]
            for j, chip in enumerate(chips):
                slot = out_refs[t].at[2 * chip[0] + chip[1]]
                rcopy(6 * t + j, half(slot, c, rows), half(slot, c, rows), sibling).wait_recv()
                fw = rcopy(6 * t + 3 + j, half(slot, c, rows), half(slot, c, rows), sibling)
                fw.start()
                sends.append(fw)
        for t in range(n):
            rows = shards[t].shape[0]
            for j, chip in enumerate(chips):
                slot = out_refs[t].at[2 * chip[0] + chip[1]]
                rcopy(6 * t + 3 + j, half(slot, 1 - c, rows), half(slot, 1 - c, rows), sibling).wait_recv()
        for cp in sends:
            cp.wait_send()

    return pl.pallas_call(
        body,
        out_shape=tuple(jax.ShapeDtypeStruct((N_CHIPS,) + s.shape, s.dtype) for s in shards),
        in_specs=[ANY] * n,
        out_specs=tuple([ANY] * n),
        scratch_shapes=[pltpu.SemaphoreType.DMA((6 * n,)), pltpu.SemaphoreType.DMA((6 * n,))],
        name="gather_weights",
    )(*shards)


def _pair_exchange(gs):
    n = len(gs)

    def body(*refs):
        g_refs, r_refs = refs[:n], refs[n:2 * n]
        send_sems, recv_sems = refs[2 * n:]
        x, y, c, _ = _place()
        cps = []
        for t in range(n):
            hr = gs[t].shape[1] // 2
            cp = pltpu.make_async_remote_copy(src_ref=g_refs[t].at[:, pl.ds((1 - c) * hr, hr), :], dst_ref=r_refs[t],
                                              send_sem=send_sems.at[t], recv_sem=recv_sems.at[t],
                                              device_id=(x, y, 1 - c), device_id_type=MESH)
            cp.start()
            cps.append(cp)
        for cp in cps:
            cp.wait()

    return pl.pallas_call(
        body,
        out_shape=tuple(jax.ShapeDtypeStruct((g.shape[0], g.shape[1] // 2, g.shape[2]), g.dtype) for g in gs),
        in_specs=[ANY] * n,
        out_specs=tuple([ANY] * n),
        scratch_shapes=[pltpu.SemaphoreType.DMA((n,)), pltpu.SemaphoreType.DMA((n,))],
        name="pair_exchange",
    )(*gs)


def _chip_exchange(ps):
    n = len(ps)

    def body(*refs):
        p_refs, o_refs = refs[:n], refs[n:2 * n]
        send_sems, recv_sems = refs[2 * n:]
        x, y, c, chips = _place()
        me_chip = 2 * x + y
        cps = []
        for t in range(n):
            for j, chip in enumerate(chips):
                cp = pltpu.make_async_remote_copy(src_ref=p_refs[t].at[2 * chip[0] + chip[1]], dst_ref=o_refs[t].at[me_chip],
                                                  send_sem=send_sems.at[3 * t + j], recv_sem=recv_sems.at[3 * t + j],
                                                  device_id=(*chip, c), device_id_type=MESH)
                cp.start()
                cps.append(cp)
        for cp in cps:
            cp.wait()

    return pl.pallas_call(
        body,
        out_shape=tuple(jax.ShapeDtypeStruct(p.shape, p.dtype) for p in ps),
        in_specs=[ANY] * n,
        out_specs=tuple([ANY] * n),
        scratch_shapes=[pltpu.SemaphoreType.DMA((3 * n,)), pltpu.SemaphoreType.DMA((3 * n,))],
        name="chip_exchange",
    )(*ps)


_HBM = pl.BlockSpec(memory_space=pltpu.HBM)
_SEM = pl.BlockSpec(memory_space=pltpu.SEMAPHORE)
_DATAFLOW = pltpu.SideEffectType.DATAFLOW_SIDE_EFFECTING


def _chip_copies(p_refs, land_refs, send_sems, recv_sems):
    x, y, c, chips = _place()
    me_chip = 2 * x + y
    return [pltpu.make_async_remote_copy(src_ref=p_refs[t].at[2 * chip[0] + chip[1]], dst_ref=land_refs[t].at[me_chip],
                                         send_sem=send_sems.at[3 * t + j], recv_sem=recv_sems.at[3 * t + j],
                                         device_id=(*chip, c), device_id_type=MESH)
            for t in range(len(p_refs)) for j, chip in enumerate(chips)]


def _chip_exchange_start(ps):
    n = len(ps)

    def body(*refs):
        p_refs, land_refs = refs[:n], refs[n:2 * n]
        send_sems, recv_sems = refs[2 * n:2 * n + 2]
        token = refs[-1]
        for cp in _chip_copies(p_refs, land_refs, send_sems, recv_sems):
            cp.start()
        token[...] = jnp.zeros_like(token)

    hbm = [pltpu.HBM(p.shape, p.dtype) for p in ps]
    args = [pltpu.with_memory_space_constraint(p, pltpu.HBM) for p in ps]
    args += [pltpu.with_memory_space_constraint(lax.empty(p.shape, p.dtype), pltpu.HBM) for p in ps]
    out = pl.pallas_call(
        body,
        name="chip_exchange_start",
        out_shape=(pltpu.SemaphoreType.DMA((3 * n,)), pltpu.SemaphoreType.DMA((3 * n,)), *hbm, *hbm,
                   jax.ShapeDtypeStruct((8, LANES), F32)),
        in_specs=[_HBM] * (2 * n),
        out_specs=(_SEM, _SEM, *([_HBM] * (2 * n)), pl.BlockSpec(memory_space=pltpu.VMEM)),
        input_output_aliases={i: 2 + i for i in range(2 * n)},
        compiler_params=pltpu.CompilerParams(has_side_effects=_DATAFLOW),
    )(*args)
    return out[0], out[1], out[2:2 + n], out[2 + n:2 + 2 * n], out[-1]


def _chip_exchange_wait(send_sems, recv_sems, p_thru, land_thru, after):
    n = len(p_thru)

    def body(*refs):
        p_refs, land_refs = refs[:n], refs[n:2 * n]
        ssem, rsem = refs[2 * n:2 * n + 2]
        for cp in _chip_copies(p_refs, land_refs, ssem, rsem):
            cp.wait_send()
            cp.wait_recv()

    hbm = [pltpu.HBM(p.shape, p.dtype) for p in p_thru]
    out = pl.pallas_call(
        body,
        name="chip_exchange_wait",
        out_shape=(*hbm, *hbm),
        in_specs=[_HBM] * (2 * n) + [_SEM, _SEM, ANY],
        out_specs=tuple([_HBM] * (2 * n)),
        input_output_aliases={i: i for i in range(2 * n)},
        compiler_params=pltpu.CompilerParams(has_side_effects=_DATAFLOW),
    )(*p_thru, *land_thru, send_sems, recv_sems, after)
    return out[:n], out[n:]


def _shard_copies(s_refs, land_refs, send_sems, recv_sems):
    x, y, c, chips = _place()
    me_chip = 2 * x + y
    return [pltpu.make_async_remote_copy(src_ref=s_refs[t], dst_ref=land_refs[t].at[me_chip],
                                         send_sem=send_sems.at[3 * t + j], recv_sem=recv_sems.at[3 * t + j],
                                         device_id=(*chip, c), device_id_type=MESH)
            for t in range(len(s_refs)) for j, chip in enumerate(chips)]


def _gather_late_start(shards):
    n = len(shards)

    def body(*refs):
        s_refs, land_refs = refs[:n], refs[n:2 * n]
        send_sems, recv_sems = refs[2 * n:2 * n + 2]
        token = refs[-1]
        for cp in _shard_copies(s_refs, land_refs, send_sems, recv_sems):
            cp.start()
        token[...] = jnp.zeros_like(token)

    lands = [(N_CHIPS,) + s.shape for s in shards]
    args = [pltpu.with_memory_space_constraint(s, pltpu.HBM) for s in shards]
    args += [pltpu.with_memory_space_constraint(lax.empty(shp, s.dtype), pltpu.HBM) for shp, s in zip(lands, shards)]
    out = pl.pallas_call(
        body,
        name="gather_late_start",
        out_shape=(pltpu.SemaphoreType.DMA((3 * n,)), pltpu.SemaphoreType.DMA((3 * n,)),
                   *[pltpu.HBM(s.shape, s.dtype) for s in shards], *[pltpu.HBM(shp, s.dtype) for shp, s in zip(lands, shards)],
                   jax.ShapeDtypeStruct((8, LANES), F32)),
        in_specs=[_HBM] * (2 * n),
        out_specs=(_SEM, _SEM, *([_HBM] * (2 * n)), pl.BlockSpec(memory_space=pltpu.VMEM)),
        input_output_aliases={i: 2 + i for i in range(2 * n)},
        compiler_params=pltpu.CompilerParams(has_side_effects=_DATAFLOW),
    )(*args)
    return out[0], out[1], out[2:2 + n], out[2 + n:2 + 2 * n], out[-1]


def _gather_late_wait(send_sems, recv_sems, s_thru, land_thru, after):
    n = len(s_thru)

    def body(*refs):
        s_refs, land_refs = refs[:n], refs[n:2 * n]
        ssem, rsem = refs[2 * n:2 * n + 2]
        for cp in _shard_copies(s_refs, land_refs, ssem, rsem):
            cp.wait_send()
            cp.wait_recv()

    out = pl.pallas_call(
        body,
        name="gather_late_wait",
        out_shape=(*[pltpu.HBM(s.shape, s.dtype) for s in s_thru], *[pltpu.HBM(l.shape, l.dtype) for l in land_thru]),
        in_specs=[_HBM] * (2 * n) + [_SEM, _SEM, ANY],
        out_specs=tuple([_HBM] * (2 * n)),
        input_output_aliases={i: i for i in range(2 * n)},
        compiler_params=pltpu.CompilerParams(has_side_effects=_DATAFLOW),
    )(*s_thru, *land_thru, send_sems, recv_sems, after)
    return out[:n], out[n:]


def _pair_swap(rs):
    n = len(rs)

    def body(*refs):
        r_refs, o_refs = refs[:n], refs[n:2 * n]
        send_sems, recv_sems = refs[2 * n:]
        x, y, c, _ = _place()
        cps = []
        for t in range(n):
            cp = pltpu.make_async_remote_copy(src_ref=r_refs[t], dst_ref=o_refs[t], send_sem=send_sems.at[t],
                                              recv_sem=recv_sems.at[t], device_id=(x, y, 1 - c), device_id_type=MESH)
            cp.start()
            cps.append(cp)
        for cp in cps:
            cp.wait()

    return pl.pallas_call(
        body,
        out_shape=tuple(jax.ShapeDtypeStruct(r.shape, r.dtype) for r in rs),
        in_specs=[ANY] * n,
        out_specs=tuple([ANY] * n),
        scratch_shapes=[pltpu.SemaphoreType.DMA((n,)), pltpu.SemaphoreType.DMA((n,))],
        name="pair_swap",
    )(*rs)


N_DEV = 8
LOSS_ROW = 4


def _small_allreduce(small):
    def body(s_ref, o_ref, all_ref, send_sems, recv_sems):
        x, y, c, _ = _place()
        me = 4 * x + 2 * y + c
        all_ref[me] = s_ref[...]
        cps = []
        for k in range(1, N_DEV):
            peer = tuple(1 - p if (k >> s) & 1 else p for p, s in ((x, 2), (y, 1), (c, 0)))
            cp = pltpu.make_async_remote_copy(src_ref=s_ref, dst_ref=all_ref.at[me], send_sem=send_sems.at[k - 1],
                                              recv_sem=recv_sems.at[k - 1], device_id=peer, device_id_type=MESH)
            cp.start()
            cps.append(cp)
        for cp in cps:
            cp.wait()
        tot = all_ref[0]
        for d in range(1, N_DEV):
            tot = tot + all_ref[d]
        o_ref[...] = tot
        o_ref[LOSS_ROW:LOSS_ROW + 1, :] = jnp.broadcast_to(jnp.sum(tot[LOSS_ROW:LOSS_ROW + 1, :], axis=1, keepdims=True),
                                                          (1, tot.shape[1]))

    vm = pl.BlockSpec(memory_space=pltpu.VMEM)
    return pl.pallas_call(
        body,
        out_shape=jax.ShapeDtypeStruct(small.shape, small.dtype),
        in_specs=[vm],
        out_specs=vm,
        scratch_shapes=[pltpu.VMEM((N_DEV,) + small.shape, small.dtype), pltpu.SemaphoreType.DMA((N_DEV - 1,)),
                        pltpu.SemaphoreType.DMA((N_DEV - 1,))],
        name="small_allreduce",
    )(small)


def _sum_pair(g, recv, cidx, *, tr, name):
    n, hr, cols = recv.shape
    nr = hr // tr

    def body(c_ref, g_ref, r_ref, o_ref):
        o_ref[...] = (g_ref[...].astype(F32) + r_ref[...].astype(F32)).astype(o_ref.dtype)

    grid_spec = pltpu.PrefetchScalarGridSpec(
        num_scalar_prefetch=1,
        grid=(n, nr),
        in_specs=[pl.BlockSpec((None, tr, cols), lambda k, i, c_ref: (k, c_ref[0] * nr + i, 0)),
                  pl.BlockSpec((None, tr, cols), lambda k, i, c_ref: (k, i, 0))],
        out_specs=pl.BlockSpec((None, tr, cols), lambda k, i, c_ref: (k, i, 0)),
    )
    return pl.pallas_call(body, out_shape=jax.ShapeDtypeStruct(recv.shape, BF16), grid_spec=grid_spec,
                          compiler_params=_cparams(), name=name)(cidx, g, recv)


def _sum_chips(p, *, tr, name):
    _, rows, cols = p.shape

    def body(p_ref, o_ref):
        tot = p_ref[0].astype(F32)
        for k in range(1, N_CHIPS):
            tot = tot + p_ref[k].astype(F32)
        o_ref[...] = tot

    return pl.pallas_call(
        body,
        out_shape=jax.ShapeDtypeStruct((rows, cols), F32),
        grid=(rows // tr,),
        in_specs=[pl.BlockSpec((N_CHIPS, tr, cols), lambda i: (0, i, 0))],
        out_specs=pl.BlockSpec((tr, cols), lambda i: (i, 0)),
        compiler_params=_cparams(),
        name=name,
    )(p)


def _adamw(w, g, m, v, *, tr, name):
    rows, cols = w.shape
    bc1 = 1.0 / (1.0 - ADAM_B1 ** ADAM_STEP)
    bc2 = 1.0 / (1.0 - ADAM_B2 ** ADAM_STEP)

    def body(w_ref, g_ref, m_ref, v_ref, d_ref, nm_ref, nv_ref):
        gv = g_ref[...]
        nm = ADAM_B1 * m_ref[...] + (1.0 - ADAM_B1) * gv
        nv = ADAM_B2 * v_ref[...] + (1.0 - ADAM_B2) * (gv * gv)
        d_ref[...] = -ADAM_LR * ((nm * bc1) / (jnp.sqrt(nv * bc2) + ADAM_EPS) + ADAM_WD * w_ref[...])
        nm_ref[...] = nm
        nv_ref[...] = nv

    spec = pl.BlockSpec((tr, cols), lambda i: (i, 0))
    sd = jax.ShapeDtypeStruct((rows, cols), F32)
    return pl.pallas_call(body, out_shape=(sd, sd, sd), grid=(rows // tr,), in_specs=[spec] * 4, out_specs=(spec,) * 3,
                          compiler_params=_cparams(), name=name)(w, g, m, v)


def _adamw_halves(w, own, sib, cidx, m, v, *, tr, name):
    rows, cols = w.shape
    hr = own.shape[0]
    nr = hr // tr
    assert rows == 2 * hr and hr % tr == 0
    bc1 = 1.0 / (1.0 - ADAM_B1 ** ADAM_STEP)
    bc2 = 1.0 / (1.0 - ADAM_B2 ** ADAM_STEP)

    def body(c_ref, w_ref, o_ref, s_ref, m_ref, v_ref, g_ref, d_ref, nm_ref, nv_ref):
        mine = (pl.program_id(0) // nr) == c_ref[0]
        gv = jnp.where(mine, o_ref[...], s_ref[...])
        nm = ADAM_B1 * m_ref[...] + (1.0 - ADAM_B1) * gv
        nv = ADAM_B2 * v_ref[...] + (1.0 - ADAM_B2) * (gv * gv)
        g_ref[...] = gv
        d_ref[...] = -ADAM_LR * ((nm * bc1) / (jnp.sqrt(nv * bc2) + ADAM_EPS) + ADAM_WD * w_ref[...])
        nm_ref[...] = nm
        nv_ref[...] = nv

    full = pl.BlockSpec((tr, cols), lambda i, c_ref: (i, 0))
    half = pl.BlockSpec((tr, cols), lambda i, c_ref: (i % nr, 0))
    sd = jax.ShapeDtypeStruct((rows, cols), F32)
    grid_spec = pltpu.PrefetchScalarGridSpec(num_scalar_prefetch=1, grid=(rows // tr,), in_specs=[full, half, half, full, full],
                                             out_specs=(full,) * 4)
    return pl.pallas_call(body, out_shape=(sd,) * 4, grid_spec=grid_spec, compiler_params=_cparams(), name=name)(
        cidx, w, own, sib, m, v)


def _pack_small(norm, mem_norm, final_norm, b_forget):
    rows = [norm.reshape(1, D_MODEL), mem_norm.reshape(1, D_MODEL), final_norm.reshape(1, D_MODEL),
            jnp.pad(b_forget.reshape(1, FOX_HEADS), ((0, 0), (0, D_MODEL - FOX_HEADS))), jnp.zeros((4, D_MODEL), F32)]
    return jnp.concatenate(rows, axis=0)


def _unpack_small(a):
    return a[0:1], a[3:4, :FOX_HEADS], a[1:2], a[2]


def kernel(x, mem, norm_g, w_in, b_forget, mem_norm_g, w_mem_kv, w_out, final_norm_g, loss_target, m_norm_g, m_w_in, m_b_forget, m_mem_norm_g, m_w_mem_kv, m_w_out, m_final_norm_g, v_norm_g, v_w_in, v_b_forget, v_mem_norm_g, v_w_mem_kv, v_w_out, v_final_norm_g):
    core = lax.axis_index("c").astype(jnp.int32)
    me_chip = (2 * lax.axis_index("x") + lax.axis_index("y")).astype(jnp.int32)
    cidx = core.reshape(1)

    def own_slot(arr, own):
        return lax.dynamic_update_slice(arr, own[None].astype(arr.dtype), (me_chip,) + (0,) * own.ndim)

    win_b, late = w_in[0].astype(BF16), [w_mem_kv[0].astype(BF16), w_out[0].astype(BF16)]
    g_in, = _gather_weights([win_b])
    g_in, late = lax.optimization_barrier((own_slot(g_in, win_b), late))
    w_r = _rearrange_w_in([g_in[k] for k in range(N_CHIPS)])
    *late_flight, early_token = _gather_late_start(late)

    def late_weights(after):
        shards, landed = _gather_late_wait(*late_flight, after)
        g_kv, g_out = (own_slot(g, s) for g, s in zip(landed, shards))
        return g_kv.reshape(D_MODEL, 2 * MEM_W), g_out.reshape(MIX_W, D_MODEL)

    trs = (128, 128, 256)
    names = ("w_in", "w_mem_kv", "w_out")
    flight = []

    def start_reduce(g_wr, g_wkv, g_wo):
        slabs = [g_wr[None],
                 g_wkv.reshape(N_CHIPS, D_MODEL // N_CHIPS, 2 * MEM_W),
                 g_wo.reshape(N_CHIPS, MIX_W // N_CHIPS, D_MODEL)]
        recv = _pair_exchange(slabs)
        pair = [_sum_pair(g, r, cidx, tr=tr, name=f"sum_pair_{nm}") for g, r, tr, nm in zip(slabs, recv, trs, names)]
        pair[0] = _w_in_grad_slabs(pair[0][0])
        *handles, token = _chip_exchange_start(pair)
        flight.extend(handles)
        return token

    gx, g_wr, g_wkv, g_wo, small = _local_grads(x, mem, norm_g, w_r, b_forget, mem_norm_g, None, None, final_norm_g, loss_target,
                                                start_reduce=start_reduce, early_token=early_token, late_weights=late_weights)

    send_sems, recv_sems, pair, land = flight
    pair, landed = _chip_exchange_wait(send_sems, recv_sems, pair, land, small)
    got = [lax.dynamic_update_slice(g, lax.dynamic_slice(p, (me_chip, 0, 0), (1,) + p.shape[1:]), (me_chip, 0, 0))
           for g, p in zip(landed, pair)]
    red = [_sum_chips(p, tr=tr, name=f"sum_chips_{nm}") for p, tr, nm in zip(got, trs, names)]
    sib = _pair_swap(red)

    outs = {}
    for nm, r, s, w, m, v, tr in zip(names, red, sib, (w_in, w_mem_kv, w_out), (m_w_in, m_w_mem_kv, m_w_out),
                                     (v_w_in, v_w_mem_kv, v_w_out), trs):
        outs[nm] = tuple(a[None] for a in _adamw_halves(w[0], r, s, cidx, m[0], v[0], tr=tr, name=f"adamw_{nm}"))

    gsum = _small_allreduce(small)
    sd, sm, sv = _adamw(_pack_small(norm_g, mem_norm_g, final_norm_g, b_forget), gsum,
                        _pack_small(m_norm_g, m_mem_norm_g, m_final_norm_g, m_b_forget),
                        _pack_small(v_norm_g, v_mem_norm_g, v_final_norm_g, v_b_forget), tr=8, name="adamw_small")
    loss = gsum[LOSS_ROW, 0]

    def group(i, small_arr):
        ng, bf, mg, fg = _unpack_small(small_arr)
        return (ng, outs["w_in"][i], bf, mg, outs["w_mem_kv"][i], outs["w_out"][i], fg)

    return (loss, gx, *group(0, gsum), *group(1, sd), *group(2, sm), *group(3, sv))
```

```python
import functools
import math

import jax
import jax.numpy as jnp
from jax import lax
from jax.experimental import pallas as pl
from jax.experimental.pallas import tpu as pltpu

F32 = jnp.float32
BF16 = jnp.bfloat16

D_MODEL = 1024
SEQ = 2048
HEAD_DIM = 64
FOX_HEADS = 12
DIL_HEADS = 12
MEM_HEADS = 4
MEM_HEAD_DIM = 128
MEM_LEN = 256
FOX_W = FOX_HEADS * HEAD_DIM
DIL_W = DIL_HEADS * HEAD_DIM
MEM_W = MEM_HEADS * MEM_HEAD_DIM
MIX_W = FOX_W + DIL_W + MEM_W
DILATIONS = ((128, 1), (512, 4), (2048, 16))
ROPE_THETA = 500000.0
ROPE_DIM = HEAD_DIM // 4
RMS_EPS = 1e-6
NEG_INF = -1e30
IN_SIZES = [FOX_W] * 4 + [FOX_HEADS] + [DIL_W] * 4 + [MEM_W] * 2
IN_W = sum(IN_SIZES)

ADAM_LR = 0.001
ADAM_B1 = 0.9
ADAM_B2 = 0.999
ADAM_EPS = 1e-08
ADAM_WD = 0.01
ADAM_STEP = 10

LANES = 128
N_CHIPS = 4
PW = 7168
PWF = PW + 4 * LANES
C_FQ, C_FK, C_FV, C_FG = 0, 768, 1536, 2304
C_DQ, C_DK, C_DV, C_DG = 3072, 3840, 4608, 5376
C_MQ, C_MG = 6144, 6656
VMEM_LIMIT = 48 * 1024 * 1024


def _cparams(**kw):
    return pltpu.CompilerParams(vmem_limit_bytes=VMEM_LIMIT, **kw)


def _matmul(a, b, *, out_dtype, tm, tn, tk, name, mode="nn"):
    if mode == "tn":
        (kdim, m), n = a.shape, b.shape[1]
        a_spec = pl.BlockSpec((tk, tm), lambda i, j, k: (k, i))
        b_spec = pl.BlockSpec((tk, tn), lambda i, j, k: (k, j))
        dims = _T0
    elif mode == "nt":
        (m, kdim), n = a.shape, b.shape[0]
        a_spec = pl.BlockSpec((tm, tk), lambda i, j, k: (i, k))
        b_spec = pl.BlockSpec((tn, tk), lambda i, j, k: (j, k))
        dims = _NT
    else:
        (m, kdim), n = a.shape, b.shape[1]
        a_spec = pl.BlockSpec((tm, tk), lambda i, j, k: (i, k))
        b_spec = pl.BlockSpec((tk, tn), lambda i, j, k: (k, j))
        dims = (((1,), (0,)), ((), ()))
    nk = kdim // tk
    assert m % tm == 0 and n % tn == 0 and kdim % tk == 0

    def body(a_ref, b_ref, o_ref, *scratch):
        prod = lax.dot_general(a_ref[...], b_ref[...], dims, preferred_element_type=F32)
        if nk == 1:
            o_ref[...] = prod.astype(o_ref.dtype)
            return
        acc_ref, = scratch
        k = pl.program_id(2)

        @pl.when(k == 0)
        def _():
            acc_ref[...] = prod

        @pl.when(k > 0)
        def _():
            acc_ref[...] += prod

        @pl.when(k == nk - 1)
        def _():
            o_ref[...] = acc_ref[...].astype(o_ref.dtype)

    return pl.pallas_call(
        body,
        out_shape=jax.ShapeDtypeStruct((m, n), out_dtype),
        grid=(m // tm, n // tn, nk),
        in_specs=[a_spec, b_spec],
        out_specs=pl.BlockSpec((tm, tn), lambda i, j, k: (i, j)),
        scratch_shapes=[pltpu.VMEM((tm, tn), F32)] if nk > 1 else [],
        compiler_params=_cparams(dimension_semantics=("parallel", "parallel", "arbitrary")),
        name=name,
    )(a, b)


def _rms_fwd(x, g, *, tm, name):
    t, d = x.shape

    def body(x_ref, g_ref, h_ref):
        xv = x_ref[...]
        r = lax.rsqrt(jnp.mean(xv * xv, axis=-1, keepdims=True) + RMS_EPS)
        h_ref[...] = (xv * r * g_ref[...]).astype(h_ref.dtype)

    return pl.pallas_call(
        body,
        out_shape=jax.ShapeDtypeStruct((t, d), BF16),
        grid=(t // tm,),
        in_specs=[pl.BlockSpec((tm, d), lambda i: (i, 0)), pl.BlockSpec((1, d), lambda i: (0, 0))],
        out_specs=pl.BlockSpec((tm, d), lambda i: (i, 0)),
        compiler_params=_cparams(),
        name=name,
    )(x, g)


def _rope_tables():
    half = ROPE_DIM // 2
    pos = jnp.arange(SEQ, dtype=F32)
    inv_freq = 1.0 / (ROPE_THETA ** (jnp.arange(0, ROPE_DIM, 2, dtype=F32) / ROPE_DIM))
    ang = pos[:, None] * inv_freq[None, :]
    cos, sin = jnp.cos(ang), jnp.sin(ang)
    one = jnp.ones((SEQ, HEAD_DIM - ROPE_DIM), F32)
    zero = jnp.zeros((SEQ, HEAD_DIM - ROPE_DIM), F32)
    zh = jnp.zeros((SEQ, half), F32)
    c = jnp.concatenate([cos, cos, one], axis=1)
    s1 = jnp.concatenate([zh, sin, zero], axis=1)
    s2 = jnp.concatenate([-sin, zh, zero], axis=1)
    rep = LANES // HEAD_DIM
    return jnp.tile(c, (1, rep)), jnp.tile(s1, (1, rep)), jnp.tile(s2, (1, rep))


def _rope_apply(t, c, s1, s2, transpose=False):
    n = t.shape[-1]
    rep = n // LANES
    c, s1, s2 = (jnp.tile(u, (1, rep)) for u in (c, s1, s2))
    half = ROPE_DIM // 2
    if not transpose:
        return t * c + pltpu.roll(t, half, 1) * s1 + pltpu.roll(t, n - half, 1) * s2
    return t * c + pltpu.roll(t * s1, n - half, 1) + pltpu.roll(t * s2, half, 1)


def _proj(h, w, tabs, *, n, tm, tn, name):
    t, d = h.shape
    assert C_DQ % tn == 0 and (C_DV - C_DQ) % tn == 0 and (C_DG - C_DQ) % tn == 0
    rope_lo, rope_hi, dil_hi = C_DQ // tn, C_DV // tn, C_DG // tn
    s_blocks = SEQ // tm

    def body(h_ref, w_ref, c_ref, s1_ref, s2_ref, o_ref, f_ref):
        j = pl.program_id(1)
        acc = jnp.dot(h_ref[...], w_ref[...], preferred_element_type=F32)
        is_rope = jnp.logical_and(j >= rope_lo, j < rope_hi)

        @pl.when(is_rope)
        def _():
            r = _rope_apply(acc, c_ref[...], s1_ref[...], s2_ref[...])
            o_ref[...] = r.astype(o_ref.dtype)
            f_ref[...] = r

        @pl.when(jnp.logical_not(is_rope))
        def _():
            o_ref[...] = acc.astype(o_ref.dtype)

        @pl.when(jnp.logical_and(j >= rope_hi, j < dil_hi))
        def _():
            f_ref[...] = acc

    tab_spec = pl.BlockSpec((tm, LANES), lambda i, j: (i % s_blocks, 0))
    f_spec = pl.BlockSpec((tm, tn), lambda i, j: (i, jnp.clip(j - rope_lo, 0, dil_hi - rope_lo - 1)))
    return pl.pallas_call(
        body,
        out_shape=(jax.ShapeDtypeStruct((t, n), BF16), jax.ShapeDtypeStruct((t, 3 * DIL_W), F32)),
        grid=(t // tm, n // tn),
        in_specs=[pl.BlockSpec((tm, d), lambda i, j: (i, 0)), pl.BlockSpec((d, tn), lambda i, j: (0, j)),
                  tab_spec, tab_spec, tab_spec],
        out_specs=(pl.BlockSpec((tm, tn), lambda i, j: (i, j)), f_spec),
        compiler_params=_cparams(dimension_semantics=("parallel", "arbitrary")),
        name=name,
    )(h, w, *tabs)


def _split3(x):
    hi = x.astype(BF16)
    r1 = x - hi.astype(F32)
    mid = r1.astype(BF16)
    lo = (r1 - mid.astype(F32)).astype(BF16)
    return hi, mid, lo


def _dot3(sel, x, sel_is_lhs):
    out = None
    for piece in _split3(x):
        t = jnp.dot(sel, piece, preferred_element_type=F32) if sel_is_lhs else jnp.dot(piece, sel, preferred_element_type=F32)
        out = t if out is None else out + t
    return out


def _flog_fwd(flog, bpad, *, nb, ts, name):
    ns = SEQ // ts

    def body(f_ref, b_ref, c_ref, carry_ref):
        s = pl.program_id(1)

        @pl.when(s == 0)
        def _():
            carry_ref[...] = jnp.zeros_like(carry_ref)

        z = f_ref[...] + b_ref[...]
        logf = jnp.minimum(z, 0.0) - jnp.log(1.0 + jnp.exp(-jnp.abs(z)))
        r = lax.broadcasted_iota(jnp.int32, (ts, ts), 0)
        c = lax.broadcasted_iota(jnp.int32, (ts, ts), 1)
        tri = jnp.where(r >= c, 1.0, 0.0).astype(BF16)
        cs = _dot3(tri, logf, True) + carry_ref[0:1, :]
        carry_ref[...] = jnp.broadcast_to(cs[ts - 1:ts, :], carry_ref.shape)
        c_ref[...] = cs

    return pl.pallas_call(
        body,
        out_shape=jax.ShapeDtypeStruct((nb * SEQ, LANES), F32),
        grid=(nb, ns),
        in_specs=[pl.BlockSpec((ts, LANES), lambda b, s: (b * ns + s, 0)), pl.BlockSpec((1, LANES), lambda b, s: (0, 0))],
        out_specs=pl.BlockSpec((ts, LANES), lambda b, s: (b * ns + s, 0)),
        scratch_shapes=[pltpu.VMEM((8, LANES), F32)],
        compiler_params=_cparams(dimension_semantics=("parallel", "arbitrary")),
        name=name,
    )(flog, bpad)


def _flog_bwd(dcol, flog, bpad, *, nb, ts, name):
    ns = SEQ // ts

    def body(d_ref, f_ref, b_ref, o_ref, gb_ref, carry_ref):
        bi = pl.program_id(0)
        s = pl.program_id(1)

        @pl.when(s == 0)
        def _():
            carry_ref[...] = jnp.zeros_like(carry_ref)

        @pl.when(jnp.logical_and(bi == 0, s == 0))
        def _():
            gb_ref[...] = jnp.zeros_like(gb_ref)

        r = lax.broadcasted_iota(jnp.int32, (ts, ts), 0)
        c = lax.broadcasted_iota(jnp.int32, (ts, ts), 1)
        tri = jnp.where(r <= c, 1.0, 0.0).astype(BF16)
        rc = _dot3(tri, d_ref[...], True) + carry_ref[0:1, :]
        carry_ref[...] = jnp.broadcast_to(rc[0:1, :], carry_ref.shape)
        z = f_ref[...] + b_ref[...]
        dz = rc / (1.0 + jnp.exp(z))
        o_ref[...] = dz.astype(o_ref.dtype)
        gb_ref[...] += jnp.broadcast_to(jnp.sum(dz, axis=0, keepdims=True), gb_ref.shape)

    rev = lambda b, s: (b * ns + (ns - 1 - s), 0)
    return pl.pallas_call(
        body,
        out_shape=(jax.ShapeDtypeStruct((nb * SEQ, LANES), BF16), jax.ShapeDtypeStruct((8, LANES), F32)),
        grid=(nb, ns),
        in_specs=[pl.BlockSpec((ts, LANES), rev), pl.BlockSpec((ts, LANES), rev), pl.BlockSpec((1, LANES), lambda b, s: (0, 0))],
        out_specs=(pl.BlockSpec((ts, LANES), rev), pl.BlockSpec((8, LANES), lambda b, s: (0, 0))),
        scratch_shapes=[pltpu.VMEM((8, LANES), F32)],
        compiler_params=_cparams(dimension_semantics=("arbitrary", "arbitrary")),
        name=name,
    )(dcol, flog, bpad)


MEM_TQ = 256
MEM_SET = 4
MEM_SCALE = 1.0 / math.sqrt(MEM_HEAD_DIM)
assert MEM_HEAD_DIM == LANES and SEQ % (MEM_TQ * MEM_SET) == 0


def _head_masks(nh):
    lane = lax.broadcasted_iota(jnp.int32, (1, LANES), 1)
    return [None] if nh == 1 else [lane < HEAD_DIM, lane >= HEAD_DIM]


def _mem_specs(qoff):
    qspec = pl.BlockSpec((None, SEQ, LANES), lambda b, j: (b, 0, qoff + j))
    kspec = pl.BlockSpec((None, MEM_LEN, LANES), lambda b, j: (b, 0, j))
    vspec = pl.BlockSpec((None, MEM_LEN, LANES), lambda b, j: (b, 0, MEM_HEADS + j))
    ospec = pl.BlockSpec((None, SEQ, LANES), lambda b, j: (b, 0, j))
    return qspec, kspec, vspec, ospec


def _mem_rows(g):
    return [pl.ds(pl.multiple_of((MEM_SET * g + a) * MEM_TQ, MEM_TQ), MEM_TQ) for a in range(MEM_SET)]


def _mem_fwd(p3, mkv3, *, qoff, name):
    nb = p3.shape[0]

    def body(q_ref, k_ref, v_ref, o_ref, lse_ref):
        kb, vb = k_ref[...], v_ref[...]

        def qset(g, c):
            rows = _mem_rows(g)
            ss = [lax.dot_general(q_ref[r, :] * MEM_SCALE, kb, _NT, preferred_element_type=F32) for r in rows]
            for r, s in zip(rows, ss):
                m = jnp.max(s, axis=1, keepdims=True)
                p = jnp.exp(s - m)
                l = jnp.sum(p, axis=1, keepdims=True)
                o_ref[r, :] = jnp.dot(p.astype(BF16), vb, preferred_element_type=F32) / l
                lse_ref[r, :] = jnp.broadcast_to(m + jnp.log(l), (MEM_TQ, LANES))
            return c

        lax.fori_loop(0, SEQ // MEM_TQ // MEM_SET, qset, 0)

    qspec, kspec, vspec, ospec = _mem_specs(qoff)
    osd = jax.ShapeDtypeStruct((nb, SEQ, MEM_W), F32)
    return pl.pallas_call(body, out_shape=(osd, osd), grid=(nb, MEM_HEADS), in_specs=[qspec, kspec, vspec],
                          out_specs=(ospec, ospec), compiler_params=_cparams(dimension_semantics=("parallel", "parallel")),
                          name=name)(p3, mkv3, mkv3)


def _mem_bwd(p3, mkv3, do, o, lse, *, qoff, do_off, name):
    nb = p3.shape[0]

    def body(q_ref, k_ref, v_ref, do_ref, o_ref, lse_ref, dq_ref, dk_ref, dv_ref):
        kb, vb = k_ref[...], v_ref[...]
        ks = kb * MEM_SCALE

        def qset(g, carry):
            dk, dv = carry
            work = []
            for r in _mem_rows(g):
                qs = q_ref[r, :] * MEM_SCALE
                dob = do_ref[r, :].astype(BF16)
                s = lax.dot_general(qs, kb, _NT, preferred_element_type=F32)
                dp = lax.dot_general(dob, vb, _NT, preferred_element_type=F32)
                work.append((r, qs, dob, s, dp))
            for r, qs, dob, s, dp in work:
                delta = jnp.sum(dob.astype(F32) * o_ref[r, :], axis=1, keepdims=True)
                p = jnp.exp(s - lse_ref[r, :][:, 0:1])
                ds = (p * (dp - delta)).astype(BF16)
                dq_ref[r, :] = jnp.dot(ds, ks, preferred_element_type=F32).astype(dq_ref.dtype)
                dk = dk + lax.dot_general(ds, qs, _T0, preferred_element_type=F32)
                dv = dv + lax.dot_general(p.astype(BF16), dob, _T0, preferred_element_type=F32)
            return dk, dv

        z = jnp.zeros((MEM_LEN, LANES), F32)
        dk, dv = lax.fori_loop(0, SEQ // MEM_TQ // MEM_SET, qset, (z, z))
        dk_ref[...] = dk
        dv_ref[...] = dv

    qspec, kspec, vspec, ospec = _mem_specs(qoff)
    dospec = pl.BlockSpec((None, SEQ, LANES), lambda b, j: (b, 0, do_off + j))
    kvo = pl.BlockSpec((None, MEM_LEN, LANES), lambda b, j: (b, 0, j))
    kvsd = jax.ShapeDtypeStruct((nb, MEM_LEN, MEM_W), F32)
    return pl.pallas_call(
        body, out_shape=(jax.ShapeDtypeStruct((nb, SEQ, MEM_W), BF16), kvsd, kvsd), grid=(nb, MEM_HEADS),
        in_specs=[qspec, kspec, vspec, dospec, ospec, ospec], out_specs=(ospec, kvo, kvo),
        compiler_params=_cparams(dimension_semantics=("parallel", "parallel")), name=name)(p3, mkv3, mkv3, do, o, lse)


BLK = 128
NBLK = SEQ // BLK
QK_SCALE = 1.0 / math.sqrt(HEAD_DIM)
DIL_STEPS = tuple(d for _, d in DILATIONS)
assert all(w // d == BLK for w, d in DILATIONS)
_T0 = (((0,), (0,)), ((), ()))
_NT = (((1,), (1,)), ((), ()))


def _stack_heads(a, masks):
    z = jnp.zeros_like(a)
    return jnp.concatenate([jnp.where(masks[0], a, z), jnp.where(masks[1], a, z)], axis=0)


def _tri_bias(lower):
    r = lax.broadcasted_iota(jnp.int32, (BLK, BLK), 0)
    c = lax.broadcasted_iota(jnp.int32, (BLK, BLK), 1)
    return jnp.where((c <= r) if lower else (c >= r), 0.0, NEG_INF).astype(F32)


def _dil_rows(r, i, d):
    start = r + i * (BLK * d)
    return pl.ds(start, BLK) if d == 1 else pl.ds(start, BLK, stride=d)


DIL_SET = 4


def _dil_sets(d, fn):
    nbk = SEQ // d // BLK
    if d == 1:
        n = 2 * DIL_SET
        def gbody(g, c):
            fn([(0, n * g + a, None if a == 0 else True) for a in range(n)])
            return c
        lax.fori_loop(0, nbk // n, gbody, 0)
    elif nbk > 1:
        assert nbk == DIL_SET
        def rbody(r, c):
            fn([(r, i, i > 0) for i in range(nbk)])
            return c
        lax.fori_loop(0, d, rbody, 0)
    else:
        def rbody(rr, c):
            fn([(DIL_SET * rr + a, 0, False) for a in range(DIL_SET)])
            return c
        lax.fori_loop(0, d // DIL_SET, rbody, 0)


def _dil_key_tiles(r, i, d, has_prev, qrows, tri_cur, tri_prev):
    tiles = [(qrows, tri_cur)]
    if has_prev is None:
        tiles.append((_dil_rows(r, jnp.maximum(i - 1, 0), d), tri_prev + jnp.where(i > 0, 0.0, NEG_INF)))
    elif has_prev:
        tiles.append((_dil_rows(r, i - 1, d), tri_prev))
    return tiles


def _dil_fwd(qkv, *, name):
    nb = qkv.shape[0]
    ncol = DIL_W // LANES
    hd = HEAD_DIM

    def body(q_ref, k_ref, v_ref, o_ref, lse_ref, m_ref, l_ref, a_ref):
        masks = _head_masks(2)
        tri_cur, tri_prev = _tri_bias(True), _tri_bias(False)
        for pi, d in enumerate(DIL_STEPS):
            first, last = pi == 0, pi == len(DIL_STEPS) - 1

            def qset(blocks, d=d, first=first, last=last):
                work = []
                for r, i, has_prev in blocks:
                    qrows = _dil_rows(r, i, d)
                    qcat = _stack_heads((q_ref[qrows, :] * QK_SCALE).astype(BF16), masks)
                    ss, krs = [], []
                    for krows, bias in _dil_key_tiles(r, i, d, has_prev, qrows, tri_cur, tri_prev):
                        s = lax.dot_general(qcat, k_ref[krows, :].astype(BF16), _NT, preferred_element_type=F32)
                        ss.append((s[:BLK] + bias, s[BLK:] + bias))
                        krs.append(krows)
                    work.append((qrows, ss, krs))
                for qrows, ss, krs in work:
                    e0 = ss[0][0] if len(ss) == 1 else jnp.maximum(ss[0][0], ss[1][0])
                    e1 = ss[0][1] if len(ss) == 1 else jnp.maximum(ss[0][1], ss[1][1])
                    n0 = jnp.max(e0, axis=1, keepdims=True)
                    n1 = jnp.max(e1, axis=1, keepdims=True)
                    if not first:
                        mo, lo = m_ref[qrows, :], l_ref[qrows, :]
                        m0, m1 = mo[:, 0:1], mo[:, hd:hd + 1]
                        n0, n1 = jnp.maximum(n0, m0), jnp.maximum(n1, m1)
                        a0, a1 = jnp.exp(m0 - n0), jnp.exp(m1 - n1)
                    ps = [(jnp.exp(s0 - n0), jnp.exp(s1 - n1)) for s0, s1 in ss]
                    t0 = ps[0][0] if len(ps) == 1 else ps[0][0] + ps[1][0]
                    t1 = ps[0][1] if len(ps) == 1 else ps[0][1] + ps[1][1]
                    l0 = jnp.sum(t0, axis=1, keepdims=True)
                    l1 = jnp.sum(t1, axis=1, keepdims=True)
                    acc = None
                    for (p0, p1), krows in zip(ps, krs):
                        vcat = _stack_heads(v_ref[krows, :].astype(BF16), masks)
                        pv = jnp.dot(jnp.concatenate([p0, p1], axis=1).astype(BF16), vcat, preferred_element_type=F32)
                        acc = pv if acc is None else acc + pv
                    if not first:
                        l0 = l0 + a0 * lo[:, 0:1]
                        l1 = l1 + a1 * lo[:, hd:hd + 1]
                        acc = acc + a_ref[qrows, :] * jnp.where(masks[0], a0, a1)
                    if last:
                        o_ref[qrows, :] = acc / jnp.where(masks[0], l0, l1)
                        lse_ref[qrows, :] = jnp.where(masks[0], n0 + jnp.log(l0), n1 + jnp.log(l1))
                    else:
                        m_ref[qrows, :] = jnp.where(masks[0], n0, n1)
                        l_ref[qrows, :] = jnp.where(masks[0], l0, l1)
                        a_ref[qrows, :] = acc

            _dil_sets(d, qset)

    spec = lambda off: pl.BlockSpec((None, SEQ, LANES), lambda b, j: (b, 0, off + j))
    ospec = pl.BlockSpec((None, SEQ, LANES), lambda b, j: (b, 0, j))
    osd = jax.ShapeDtypeStruct((nb, SEQ, DIL_W), F32)
    return pl.pallas_call(
        body, out_shape=(osd, osd), grid=(nb, ncol),
        in_specs=[spec(0), spec(ncol), spec(2 * ncol)], out_specs=(ospec, ospec),
        scratch_shapes=[pltpu.VMEM((SEQ, LANES), F32)] * 3,
        compiler_params=_cparams(dimension_semantics=("parallel", "parallel")), name=name,
    )(qkv, qkv, qkv)


def _dil_bwd(qkv, do, o, lse, tabs, *, do_off, name):
    nb = qkv.shape[0]
    ncol = DIL_W // LANES
    hd = HEAD_DIM

    def body(q_ref, k_ref, v_ref, do_ref, o_ref, lse_ref, c_ref, s1_ref, s2_ref, dqo_ref, dko_ref, dvo_ref,
             dq_ref, dk_ref, dv_ref, dl_ref, dof_ref):
        masks = _head_masks(2)
        tri_cur, tri_prev = _tri_bias(True), _tri_bias(False)
        dq_ref[...] = jnp.zeros_like(dq_ref)
        dk_ref[...] = jnp.zeros_like(dk_ref)
        dv_ref[...] = jnp.zeros_like(dv_ref)

        def delta_body(i, c):
            rows = pl.ds(pl.multiple_of(i * BLK, BLK), BLK)
            dof = do_ref[rows, :].astype(F32)
            dof_ref[rows, :] = dof
            prod = dof * o_ref[rows, :]
            z = jnp.zeros_like(prod)
            dl_ref[rows, :] = jnp.where(masks[0], jnp.sum(jnp.where(masks[0], prod, z), axis=1, keepdims=True),
                                        jnp.sum(jnp.where(masks[1], prod, z), axis=1, keepdims=True))
            return c

        lax.fori_loop(0, NBLK, delta_body, 0)

        for d in DIL_STEPS:
            def qset(blocks, d=d):
                work = []
                for r, i, has_prev in blocks:
                    qrows = _dil_rows(r, i, d)
                    qcat = _stack_heads((q_ref[qrows, :] * QK_SCALE).astype(BF16), masks)
                    docat = _stack_heads(dof_ref[qrows, :].astype(BF16), masks)
                    tiles = []
                    for krows, bias in _dil_key_tiles(r, i, d, has_prev, qrows, tri_cur, tri_prev):
                        s = lax.dot_general(qcat, k_ref[krows, :].astype(BF16), _NT, preferred_element_type=F32)
                        dp = lax.dot_general(docat, v_ref[krows, :].astype(BF16), _NT, preferred_element_type=F32)
                        tiles.append((krows, s, dp, bias))
                    work.append((qrows, qcat, docat, tiles))
                for qrows, qcat, docat, tiles in work:
                    lseb, dlb = lse_ref[qrows, :], dl_ref[qrows, :]
                    lse0, lse1 = lseb[:, 0:1], lseb[:, hd:hd + 1]
                    dl0, dl1 = dlb[:, 0:1], dlb[:, hd:hd + 1]
                    dq = None
                    for krows, s, dp, bias in tiles:
                        p0 = jnp.exp(s[:BLK] + bias - lse0)
                        p1 = jnp.exp(s[BLK:] + bias - lse1)
                        ds0 = p0 * (dp[:BLK] - dl0)
                        ds1 = p1 * (dp[BLK:] - dl1)
                        ds0b, ds1b = ds0.astype(BF16), ds1.astype(BF16)
                        pcat = jnp.concatenate([p0.astype(BF16), p1.astype(BF16)], axis=0)
                        dscat = jnp.concatenate([ds0b, ds1b], axis=0)
                        dv_ref[krows, :] += lax.dot_general(pcat, docat, _T0, preferred_element_type=F32)
                        dk_ref[krows, :] += lax.dot_general(dscat, qcat, _T0, preferred_element_type=F32)
                        dsrow = jnp.concatenate([ds0b, ds1b], axis=1)
                        kcat = _stack_heads((k_ref[krows, :] * QK_SCALE).astype(BF16), masks)
                        t = jnp.dot(dsrow, kcat, preferred_element_type=F32)
                        dq = t if dq is None else dq + t
                    dq_ref[qrows, :] += dq

            _dil_sets(d, qset)

        def out_body(i, c):
            rows = pl.ds(pl.multiple_of(i * BLK, BLK), BLK)
            tab = (c_ref[rows, :], s1_ref[rows, :], s2_ref[rows, :])
            dqo_ref[rows, :] = _rope_apply(dq_ref[rows, :], *tab, transpose=True).astype(dqo_ref.dtype)
            dko_ref[rows, :] = _rope_apply(dk_ref[rows, :], *tab, transpose=True).astype(dko_ref.dtype)
            dvo_ref[rows, :] = dv_ref[rows, :].astype(dvo_ref.dtype)
            return c

        lax.fori_loop(0, NBLK, out_body, 0)

    spec = lambda off: pl.BlockSpec((None, SEQ, LANES), lambda b, j: (b, 0, off + j))
    ospec = pl.BlockSpec((None, SEQ, LANES), lambda b, j: (b, 0, j))
    tspec = pl.BlockSpec((SEQ, LANES), lambda b, j: (0, 0))
    osd = jax.ShapeDtypeStruct((nb, SEQ, DIL_W), BF16)
    return pl.pallas_call(
        body, out_shape=(osd, osd, osd), grid=(nb, ncol),
        in_specs=[spec(0), spec(ncol), spec(2 * ncol), spec(do_off), ospec, ospec, tspec, tspec, tspec],
        out_specs=(ospec, ospec, ospec),
        scratch_shapes=[pltpu.VMEM((SEQ, LANES), F32)] * 5,
        compiler_params=_cparams(dimension_semantics=("parallel", "parallel")), name=name,
    )(qkv, qkv, qkv, do, o, lse, *tabs)


FOX_GROUP = 4
assert NBLK % FOX_GROUP == 0
_FOX_COLS = tuple(c // LANES for c in (C_FQ, C_FK, C_FV))


def _fox_specs():
    cols = [pl.BlockSpec((None, SEQ, LANES), (lambda b, j, off=off: (b, 0, off + j))) for off in _FOX_COLS]
    ospec = pl.BlockSpec((None, SEQ, LANES), lambda b, j: (b, 0, j))
    crspec = pl.BlockSpec((None, None, NBLK, 8, BLK), lambda b, j: (b, j, 0, 0, 0))
    return cols, ospec, crspec


def _fox_key_rows(t, e):
    return pl.ds(pl.multiple_of((FOX_GROUP * t + e) * BLK, BLK), BLK)


def _fox_fwd(p3, crow, *, name):
    nb = p3.shape[0]
    g = FOX_GROUP

    def body(q_ref, k_ref, v_ref, cr_ref, o_ref, lse_ref):
        masks = _head_masks(2)
        tri = _tri_bias(True)

        def qk(qcat, t):
            return tuple(lax.dot_general(qcat, k_ref[_fox_key_rows(t, e), :], _NT, preferred_element_type=F32) for e in range(g))

        def consume(ss, t, state, nblk, diag):
            m0, m1, l0, l1, acc = state
            us = []
            for e in range(nblk):
                cr = cr_ref[g * t + e]
                u0 = ss[e][:BLK] - cr[0:1, :]
                u1 = ss[e][BLK:] - cr[1:2, :]
                if diag and e == nblk - 1:
                    u0, u1 = u0 + tri, u1 + tri
                us.append((u0, u1))
            x0 = functools.reduce(jnp.maximum, [u[0] for u in us])
            x1 = functools.reduce(jnp.maximum, [u[1] for u in us])
            n0 = jnp.maximum(m0, jnp.max(x0, axis=1, keepdims=True))
            n1 = jnp.maximum(m1, jnp.max(x1, axis=1, keepdims=True))
            a0, a1 = jnp.exp(m0 - n0), jnp.exp(m1 - n1)
            acc = acc * jnp.where(masks[0], a0, a1)
            t0 = t1 = None
            for e in range(nblk):
                p0, p1 = jnp.exp(us[e][0] - n0), jnp.exp(us[e][1] - n1)
                t0 = p0 if t0 is None else t0 + p0
                t1 = p1 if t1 is None else t1 + p1
                pcat = jnp.concatenate([p0, p1], axis=1)
                hi = pcat.astype(BF16)
                lo = (pcat - hi.astype(F32)).astype(BF16)
                vcat = _stack_heads(v_ref[_fox_key_rows(t, e), :], masks)
                acc = acc + jnp.dot(hi, vcat, preferred_element_type=F32) + jnp.dot(lo, vcat, preferred_element_type=F32)
            l0 = a0 * l0 + jnp.sum(t0, axis=1, keepdims=True)
            l1 = a1 * l1 + jnp.sum(t1, axis=1, keepdims=True)
            return n0, n1, l0, l1, acc

        def gbody(ng, c):
            neg = jnp.full((BLK, 1), NEG_INF, F32)
            z1 = jnp.zeros((BLK, 1), F32)
            rows = [pl.ds(pl.multiple_of((g * ng + a) * BLK, BLK), BLK) for a in range(g)]
            qcats = [_stack_heads(q_ref[rows[a], :] * QK_SCALE, masks) for a in range(g)]
            first = [qk(qcats[a], 0) for a in range(g)]
            done = []
            for a in range(g):
                def step(t, cc, qcat=qcats[a]):
                    ss, st = cc
                    nxt = qk(qcat, t + 1)
                    return nxt, consume(ss, t, st, g, False)

                done.append(lax.fori_loop(0, ng, step, (first[a], (neg, neg, z1, z1, jnp.zeros((BLK, LANES), F32)))))
            for a in range(g):
                ss, state = done[a]
                m0, m1, l0, l1, acc = consume(ss, ng, state, a + 1, True)
                o_ref[rows[a], :] = acc / jnp.where(masks[0], l0, l1)
                lse_ref[rows[a], :] = jnp.where(masks[0], m0 + jnp.log(l0), m1 + jnp.log(l1))
            return c

        lax.fori_loop(0, NBLK // g, gbody, 0)

    cols, ospec, crspec = _fox_specs()
    osd = jax.ShapeDtypeStruct((nb, SEQ, FOX_W), F32)
    return pl.pallas_call(
        body, out_shape=(osd, osd), grid=(nb, FOX_W // LANES), in_specs=cols + [crspec], out_specs=(ospec, ospec),
        compiler_params=_cparams(dimension_semantics=("parallel", "parallel")), name=name,
    )(p3, p3, p3, crow)


def _fox_bwd(p3, crow, do, o, lse, *, do_off, name):
    nb = p3.shape[0]
    g = FOX_GROUP
    hd = HEAD_DIM

    def body(q_ref, k_ref, v_ref, cr_ref, do_ref, o_ref, lse_ref, dq_ref, dko_ref, dvo_ref, dcr_ref, dk_ref, dv_ref):
        masks = _head_masks(2)
        tri = _tri_bias(True)
        dk_ref[...] = jnp.zeros_like(dk_ref)
        dv_ref[...] = jnp.zeros_like(dv_ref)
        dcr_ref[...] = jnp.zeros_like(dcr_ref)

        def products(qcat, docat, t):
            out = []
            for e in range(g):
                krows = _fox_key_rows(t, e)
                out.append(lax.dot_general(qcat, k_ref[krows, :], _NT, preferred_element_type=F32))
                out.append(lax.dot_general(docat, v_ref[krows, :], _NT, preferred_element_type=F32))
            return tuple(out)

        def consume(prod, t, ctx, dq, nblk, diag):
            qcat, docat, lse0, lse1, dl0, dl1 = ctx
            for e in range(nblk):
                jb = g * t + e
                krows = _fox_key_rows(t, e)
                s, dp = prod[2 * e], prod[2 * e + 1]
                cr = cr_ref[jb]
                u0 = s[:BLK] - cr[0:1, :]
                u1 = s[BLK:] - cr[1:2, :]
                if diag and e == nblk - 1:
                    u0, u1 = u0 + tri, u1 + tri
                p0 = jnp.exp(u0 - lse0)
                p1 = jnp.exp(u1 - lse1)
                ds0 = p0 * (dp[:BLK] - dl0)
                ds1 = p1 * (dp[BLK:] - dl1)
                dcr_ref[jb, 0:1, :] += jnp.sum(ds0, axis=0, keepdims=True)
                dcr_ref[jb, 1:2, :] += jnp.sum(ds1, axis=0, keepdims=True)
                ds0b, ds1b = ds0.astype(BF16), ds1.astype(BF16)
                pcat = jnp.concatenate([p0.astype(BF16), p1.astype(BF16)], axis=0)
                dscat = jnp.concatenate([ds0b, ds1b], axis=0)
                dv_ref[krows, :] += lax.dot_general(pcat, docat, _T0, preferred_element_type=F32)
                dk_ref[krows, :] += lax.dot_general(dscat, qcat, _T0, preferred_element_type=F32)
                dsrow = jnp.concatenate([ds0b, ds1b], axis=1)
                dq = dq + jnp.dot(dsrow, _stack_heads(k_ref[krows, :] * QK_SCALE, masks), preferred_element_type=F32)
            return dq

        def gbody(ng, c):
            ctxs, rows = [], []
            for a in range(g):
                r = pl.ds(pl.multiple_of((g * ng + a) * BLK, BLK), BLK)
                qcat = _stack_heads(q_ref[r, :] * QK_SCALE, masks)
                dob = do_ref[r, :].astype(BF16)
                prod = dob.astype(F32) * o_ref[r, :]
                z = jnp.zeros_like(prod)
                dl0 = jnp.sum(jnp.where(masks[0], prod, z), axis=1, keepdims=True)
                dl1 = jnp.sum(jnp.where(masks[1], prod, z), axis=1, keepdims=True)
                lseb = lse_ref[r, :]
                ctxs.append((qcat, _stack_heads(dob, masks), lseb[:, 0:1], lseb[:, hd:hd + 1], dl0, dl1))
                rows.append(r)
            first = [products(ctxs[a][0], ctxs[a][1], 0) for a in range(g)]
            done = []
            for a in range(g):
                def step(t, cc, ctx=ctxs[a]):
                    pr, dq = cc
                    nxt = products(ctx[0], ctx[1], t + 1)
                    return nxt, consume(pr, t, ctx, dq, g, False)

                done.append(lax.fori_loop(0, ng, step, (first[a], jnp.zeros((BLK, LANES), F32))))
            for a in range(g):
                pr, dq = done[a]
                dq_ref[rows[a], :] = consume(pr, ng, ctxs[a], dq, a + 1, True).astype(dq_ref.dtype)
            return c

        lax.fori_loop(0, NBLK // g, gbody, 0)
        dko_ref[...] = dk_ref[...].astype(dko_ref.dtype)
        dvo_ref[...] = dv_ref[...].astype(dvo_ref.dtype)

    cols, ospec, crspec = _fox_specs()
    dospec = pl.BlockSpec((None, SEQ, LANES), lambda b, j: (b, 0, do_off + j))
    osd = jax.ShapeDtypeStruct((nb, SEQ, FOX_W), BF16)
    return pl.pallas_call(
        body, out_shape=(osd, osd, osd, jax.ShapeDtypeStruct((nb, FOX_W // LANES, NBLK, 8, BLK), F32)),
        grid=(nb, FOX_W // LANES), in_specs=cols + [crspec, dospec, ospec, ospec], out_specs=(ospec, ospec, ospec, crspec),
        scratch_shapes=[pltpu.VMEM((SEQ, LANES), F32)] * 2,
        compiler_params=_cparams(dimension_semantics=("parallel", "parallel")), name=name,
    )(p3, p3, p3, crow, do, o, lse)


_B1, _B2 = FOX_W // LANES, (FOX_W + DIL_W) // LANES


def _dy_gate_bwd(dx2b, wo, fox, dil, memo, p16, *, tm, tn, name):
    t, d = dx2b.shape
    assert FOX_W % tn == 0 and DIL_W % tn == 0 and MEM_W % tn == 0 and all(c % tn == 0 for c in (C_FG, C_DG, C_MG))
    n1, n2, n3 = FOX_W // tn, (FOX_W + DIL_W) // tn, MIX_W // tn

    def body(dx_ref, w_ref, f_ref, d_ref, m_ref, g_ref, da_ref, dg_ref):
        j = pl.program_id(1)
        dyv = lax.dot_general(dx_ref[...], w_ref[...], _NT, preferred_element_type=F32)
        a = jnp.where(j < n1, f_ref[...], jnp.where(j < n2, d_ref[...], m_ref[...]))
        gt = g_ref[...].astype(F32)
        sg = 1.0 / (1.0 + jnp.exp(-gt))
        da_ref[...] = (dyv * gt * sg).astype(da_ref.dtype)
        dg_ref[...] = (dyv * a * sg * (1.0 + gt * (1.0 - sg))).astype(dg_ref.dtype)

    def gcol(j):
        return jnp.where(j < n1, C_FG // tn + j, jnp.where(j < n2, C_DG // tn + j - n1, C_MG // tn + j - n2))

    tile = pl.BlockSpec((tm, tn), lambda i, j: (i, j))
    return pl.pallas_call(
        body,
        out_shape=(jax.ShapeDtypeStruct((t, MIX_W), BF16), jax.ShapeDtypeStruct((t, MIX_W), BF16)),
        grid=(t // tm, n3),
        in_specs=[pl.BlockSpec((tm, d), lambda i, j: (i, 0)), pl.BlockSpec((tn, d), lambda i, j: (j, 0)),
                  pl.BlockSpec((tm, tn), lambda i, j: (i, jnp.minimum(j, n1 - 1))),
                  pl.BlockSpec((tm, tn), lambda i, j: (i, jnp.clip(j - n1, 0, n2 - n1 - 1))),
                  pl.BlockSpec((tm, tn), lambda i, j: (i, jnp.clip(j - n2, 0, n3 - n2 - 1))),
                  pl.BlockSpec((tm, tn), lambda i, j: (i, gcol(j)))],
        out_specs=(tile, tile),
        compiler_params=_cparams(dimension_semantics=("parallel", "parallel")),
        name=name,
    )(dx2b, wo, fox, dil, memo, p16)


def _silu(g):
    return g / (1.0 + jnp.exp(-g))


def _out_loss(fox, dil, memo, p16, wo, x, tgt, gfin, *, tm, name):
    t, d = x.shape
    n_feat = float(d)

    def body(f_ref, d_ref, m_ref, fg_ref, dg_ref, mg_ref, w_ref, x_ref, t_ref, g_ref, y_ref, dx_ref, dxb_ref, st_ref):
        i = pl.program_id(0)

        @pl.when(i == 0)
        def _():
            st_ref[...] = jnp.zeros_like(st_ref)

        y = jnp.concatenate([(a_ref[...] * _silu(gt_ref[...].astype(F32))).astype(BF16)
                             for a_ref, gt_ref in ((f_ref, fg_ref), (d_ref, dg_ref), (m_ref, mg_ref))], axis=1)
        y_ref[...] = y
        x2 = x_ref[...] + jnp.dot(y, w_ref[...], preferred_element_type=F32)
        r = lax.rsqrt(jnp.mean(x2 * x2, axis=-1, keepdims=True) + RMS_EPS)
        nrm = x2 * r
        gv = g_ref[...]
        err = nrm * gv - t_ref[...]
        dout = err * (1.0 / n_feat)
        dn = dout * gv
        dx2 = r * (dn - nrm * jnp.mean(dn * nrm, axis=-1, keepdims=True))
        dx_ref[...] = dx2
        dxb_ref[...] = dx2.astype(dxb_ref.dtype)
        st_ref[0:1, :] += jnp.sum(dout * nrm, axis=0, keepdims=True)
        st_ref[1:2, :] += (0.5 / n_feat) * jnp.sum(err * err, axis=0, keepdims=True)

    row = pl.BlockSpec((tm, d), lambda i: (i, 0))
    whole = lambda w: pl.BlockSpec((tm, w), lambda i: (i, 0))
    gate = lambda w, col: pl.BlockSpec((tm, w), lambda i: (i, col // w))
    return pl.pallas_call(
        body,
        out_shape=(jax.ShapeDtypeStruct((t, MIX_W), BF16), jax.ShapeDtypeStruct((t, d), F32), jax.ShapeDtypeStruct((t, d), BF16),
                   jax.ShapeDtypeStruct((8, d), F32)),
        grid=(t // tm,),
        in_specs=[whole(FOX_W), whole(DIL_W), whole(MEM_W), gate(FOX_W, C_FG), gate(DIL_W, C_DG), gate(MEM_W, C_MG),
                  pl.BlockSpec((MIX_W, d), lambda i: (0, 0)), row, row, pl.BlockSpec((1, d), lambda i: (0, 0))],
        out_specs=(pl.BlockSpec((tm, MIX_W), lambda i: (i, 0)), row, row, pl.BlockSpec((8, d), lambda i: (0, 0))),
        compiler_params=_cparams(dimension_semantics=("arbitrary",)),
        name=name,
    )(fox, dil, memo, p16, p16, p16, wo, x, tgt, gfin)


def _dh_rms_bwd(dp, w, x, g, resid, *, tm, name):
    t, d = x.shape
    kdim = dp.shape[1]

    def body(*refs):
        if resid is not None:
            dp_ref, w_ref, x_ref, g_ref, r_ref, dx_ref, gg_ref = refs
        else:
            dp_ref, w_ref, x_ref, g_ref, dx_ref, gg_ref = refs

        @pl.when(pl.program_id(0) == 0)
        def _():
            gg_ref[...] = jnp.zeros_like(gg_ref)

        dh = lax.dot_general(dp_ref[...], w_ref[...], _NT, preferred_element_type=F32)
        xv = x_ref[...]
        r = lax.rsqrt(jnp.mean(xv * xv, axis=-1, keepdims=True) + RMS_EPS)
        nrm = xv * r
        dn = dh * g_ref[...]
        dx = r * (dn - nrm * jnp.mean(dn * nrm, axis=-1, keepdims=True))
        if resid is not None:
            dx = dx + r_ref[...]
        dx_ref[...] = dx
        gg_ref[0:1, :] += jnp.sum(dh * nrm, axis=0, keepdims=True)

    row = pl.BlockSpec((tm, d), lambda i: (i, 0))
    in_specs = [pl.BlockSpec((tm, kdim), lambda i: (i, 0)),
                pl.BlockSpec((d, kdim), lambda i: (0, 0), pipeline_mode=pl.Buffered(1)), row,
                pl.BlockSpec((1, d), lambda i: (0, 0))]
    args = [dp, w, x, g]
    if resid is not None:
        in_specs.append(row)
        args.append(resid)
    return pl.pallas_call(
        body,
        out_shape=(jax.ShapeDtypeStruct((t, d), F32), jax.ShapeDtypeStruct((8, d), F32)),
        grid=(t // tm,),
        in_specs=in_specs,
        out_specs=(row, pl.BlockSpec((8, d), lambda i: (0, 0))),
        compiler_params=_cparams(dimension_semantics=("arbitrary",)),
        name=name,
    )(*args)


_FLOG0 = 4 * FOX_W
_W_IN_SEGMENTS = ((0, _FLOG0, 0), (_FLOG0, _FLOG0 + FOX_HEADS, PW), (_FLOG0 + FOX_HEADS, IN_W, C_DQ))
SHARD_W = IN_W // N_CHIPS


def _rearrange_w_in(shards):
    def cols(lo, hi):
        parts = []
        for k in range(N_CHIPS):
            a, b = max(lo, k * SHARD_W), min(hi, (k + 1) * SHARD_W)
            if a < b:
                parts.append(shards[k][:, a - k * SHARD_W:b - k * SHARD_W])
        return parts

    (a0, a1, _), (f0, f1, _), (b0, b1, _) = _W_IN_SEGMENTS
    pad = jnp.zeros((shards[0].shape[0], PWF - PW - FOX_HEADS), shards[0].dtype)
    return jnp.concatenate(cols(a0, a1) + cols(b0, b1) + cols(f0, f1) + [pad], axis=1)


def _w_in_grad_slabs(g):
    slabs = []
    for k in range(N_CHIPS):
        parts = []
        for lo, hi, at in _W_IN_SEGMENTS:
            a, b = max(lo, k * SHARD_W), min(hi, (k + 1) * SHARD_W)
            if a < b:
                parts.append(g[:, at + a - lo:at + b - lo])
        slabs.append(jnp.concatenate(parts, axis=1))
    return jnp.stack(slabs, axis=0)


def _local_grads(x, mem, norm_g, w_r, b_forget, mem_norm_g, w_kv, w_o, final_norm_g, tgt, start_reduce=None,
                 early_token=None, late_weights=None):
    nb = x.shape[0]
    t = nb * SEQ
    x2d = x.reshape(t, D_MODEL)
    tgt2d = tgt.reshape(t, D_MODEL)
    tabs = _rope_tables()
    bpad = jnp.pad(b_forget.reshape(1, FOX_HEADS), ((0, 0), (0, LANES - FOX_HEADS)))

    gain0 = norm_g.reshape(1, D_MODEL)
    if early_token is not None:
        gain0 = gain0 + early_token[0:1, 0:1]
    h = _rms_fwd(x2d, gain0, tm=512, name="rms_x")
    p16, dqkv = _proj(h, w_r, tabs, n=PWF, tm=1024, tn=768, name="proj")
    flog = _matmul(h, w_r[:, PW:PW + LANES], out_dtype=F32, tm=1024, tn=LANES, tk=D_MODEL, name="proj_flog")
    c12 = _flog_fwd(flog, bpad, nb=nb, ts=256, name="flog_fwd")

    crow = c12[:, :FOX_HEADS].reshape(nb, NBLK, BLK, FOX_HEADS // 2, 2).transpose(0, 3, 1, 4, 2)
    crow = jnp.pad(crow, ((0, 0), (0, 0), (0, 0), (0, 6), (0, 0)))
    p3 = p16.reshape(nb, SEQ, PWF)
    fox, fox_lse = _fox_fwd(p3, crow, name="fox_fwd")
    if late_weights is not None:
        w_kv, w_o = late_weights(fox_lse)

    dqkv3 = dqkv.reshape(nb, SEQ, 3 * DIL_W)
    dil, dil_lse = _dil_fwd(dqkv3, name="dil_fwd")

    mh = _rms_fwd(mem.reshape(nb * MEM_LEN, D_MODEL), mem_norm_g.reshape(1, D_MODEL), tm=nb * MEM_LEN, name="rms_mem")
    mkv = _matmul(mh, w_kv, out_dtype=BF16, tm=nb * MEM_LEN, tn=512, tk=D_MODEL, name="mem_kv")
    mkv3 = mkv.reshape(nb, MEM_LEN, 2 * MEM_W)
    memo, mem_lse = _mem_fwd(p3, mkv3, qoff=C_MQ // LANES, name="mem_fwd")

    fox2, dil2, memo2 = fox.reshape(t, FOX_W), dil.reshape(t, DIL_W), memo.reshape(t, MEM_W)
    y, dx2, dx2b, st = _out_loss(fox2, dil2, memo2, p16, w_o, x2d, tgt2d, final_norm_g.reshape(1, D_MODEL), tm=256,
                                 name="out_loss")

    g_wo = _matmul(y, dx2b, mode="tn", out_dtype=BF16, tm=1024, tn=512, tk=t, name="grad_w_out")
    datt, dgate = _dy_gate_bwd(dx2b, w_o, fox2, dil2, memo2, p16, tm=2048, tn=256, name="dy_gate_bwd")
    datt3 = datt.reshape(nb, SEQ, MIX_W)

    dfq, dfk, dfv, dcr = _fox_bwd(p3, crow, datt3, fox, fox_lse, do_off=0, name="fox_bwd")
    dcol = -dcr[:, :, :, :2, :].transpose(0, 2, 4, 1, 3).reshape(t, FOX_HEADS)
    dcol = jnp.pad(dcol, ((0, 0), (0, LANES - FOX_HEADS)))
    dflog, gb = _flog_bwd(dcol, flog, bpad, nb=nb, ts=256, name="flog_bwd")

    ddq, ddk, ddv = _dil_bwd(dqkv3, datt3, dil, dil_lse, tabs, do_off=_B1, name="dil_bwd")

    dmq, dmk, dmv = _mem_bwd(p3, mkv3, datt3, memo, mem_lse, qoff=C_MQ // LANES, do_off=_B2, name="mem_bwd")
    dmkv = jnp.concatenate([dmk, dmv], axis=-1).reshape(nb * MEM_LEN, 2 * MEM_W).astype(BF16)
    g_wkv = _matmul(mh, dmkv, mode="tn", out_dtype=BF16, tm=512, tn=512, tk=nb * MEM_LEN, name="grad_w_kv")
    _, gmn = _dh_rms_bwd(dmkv, w_kv, mem.reshape(nb * MEM_LEN, D_MODEL), mem_norm_g.reshape(1, D_MODEL), None,
                         tm=nb * MEM_LEN, name="mem_rms_bwd")

    flat = lambda a: a.reshape(t, -1)
    dp = jnp.concatenate([flat(dfq), flat(dfk), flat(dfv), dgate[:, :FOX_W], flat(ddq), flat(ddk), flat(ddv),
                          dgate[:, FOX_W:FOX_W + DIL_W], flat(dmq), dgate[:, FOX_W + DIL_W:], dflog,
                          jnp.zeros((t, PWF - PW - LANES), BF16)], axis=1)
    g_wr = _matmul(h, dp, mode="tn", out_dtype=BF16, tm=D_MODEL, tn=512, tk=t, name="grad_w_in")
    gain = norm_g.reshape(1, D_MODEL)
    if start_reduce is not None:
        gain = gain + start_reduce(g_wr, g_wkv, g_wo)[0:1, 0:1]
    gx, gng = _dh_rms_bwd(dp, w_r, x2d, gain, dx2, tm=256, name="in_rms_bwd")

    gb_row = jnp.pad(gb[0:1, :], ((0, 0), (0, D_MODEL - LANES)))
    small = jnp.concatenate([gng[0:1], gmn[0:1], st[0:1], gb_row, st[1:2], jnp.zeros((3, D_MODEL), F32)], axis=0)
    return gx.reshape(nb, SEQ, D_MODEL), g_wr, g_wkv, g_wo, small


MESH = pl.DeviceIdType.MESH
ANY = pl.BlockSpec(memory_space=pl.ANY)


def _place():
    x, y, c = lax.axis_index("x"), lax.axis_index("y"), lax.axis_index("c")
    other_chips = [(1 - x, y), (x, 1 - y), (1 - x, 1 - y)]
    return x, y, c, other_chips


def _gather_weights(shards):
    n = len(shards)

    def body(*refs):
        in_refs, out_refs = refs[:n], refs[n:2 * n]
        send_sems, recv_sems = refs[2 * n:]
        x, y, c, chips = _place()
        me_chip = 2 * x + y
        sibling = (x, y, 1 - c)

        def half(ref, pc, rows):
            return ref.at[pl.ds(pc * (rows // 2), rows // 2), :]

        def rcopy(k, src, dst, to):
            return pltpu.make_async_remote_copy(src_ref=src, dst_ref=dst, send_sem=send_sems.at[k], recv_sem=recv_sems.at[k],
                                                device_id=to, device_id_type=MESH)

        sends = []
        for t in range(n):
            rows = shards[t].shape[0]
            for j, chip in enumerate(chips):
                cp = rcopy(6 * t + j, half(in_refs[t], c, rows), half(out_refs[t].at[me_chip], c, rows), (*chip, c))
                cp.start()
                sends.append(cp)
        for t in range(n):
            rows = shards[t].shape[0]
            for j, chip in enumerate(chips):
                slot = out_refs[t].at[2 * chip[0] + chip[1]]
                rcopy(6 * t + j, half(slot, c, rows), half(slot, c, rows), sibling).wait_recv()
                fw = rcopy(6 * t + 3 + j, half(slot, c, rows), half(slot, c, rows), sibling)
                fw.start()
                sends.append(fw)
        for t in range(n):
            rows = shards[t].shape[0]
            for j, chip in enumerate(chips):
                slot = out_refs[t].at[2 * chip[0] + chip[1]]
                rcopy(6 * t + 3 + j, half(slot, 1 - c, rows), half(slot, 1 - c, rows), sibling).wait_recv()
        for cp in sends:
            cp.wait_send()

    return pl.pallas_call(
        body,
        out_shape=tuple(jax.ShapeDtypeStruct((N_CHIPS,) + s.shape, s.dtype) for s in shards),
        in_specs=[ANY] * n,
        out_specs=tuple([ANY] * n),
        scratch_shapes=[pltpu.SemaphoreType.DMA((6 * n,)), pltpu.SemaphoreType.DMA((6 * n,))],
        name="gather_weights",
    )(*shards)


def _pair_exchange(gs):
    n = len(gs)

    def body(*refs):
        g_refs, r_refs = refs[:n], refs[n:2 * n]
        send_sems, recv_sems = refs[2 * n:]
        x, y, c, _ = _place()
        cps = []
        for t in range(n):
            hr = gs[t].shape[1] // 2
            cp = pltpu.make_async_remote_copy(src_ref=g_refs[t].at[:, pl.ds((1 - c) * hr, hr), :], dst_ref=r_refs[t],
                                              send_sem=send_sems.at[t], recv_sem=recv_sems.at[t],
                                              device_id=(x, y, 1 - c), device_id_type=MESH)
            cp.start()
            cps.append(cp)
        for cp in cps:
            cp.wait()

    return pl.pallas_call(
        body,
        out_shape=tuple(jax.ShapeDtypeStruct((g.shape[0], g.shape[1] // 2, g.shape[2]), g.dtype) for g in gs),
        in_specs=[ANY] * n,
        out_specs=tuple([ANY] * n),
        scratch_shapes=[pltpu.SemaphoreType.DMA((n,)), pltpu.SemaphoreType.DMA((n,))],
        name="pair_exchange",
    )(*gs)


_HBM = pl.BlockSpec(memory_space=pltpu.HBM)
_SEM = pl.BlockSpec(memory_space=pltpu.SEMAPHORE)
_DATAFLOW = pltpu.SideEffectType.DATAFLOW_SIDE_EFFECTING


def _chip_copies(p_refs, land_refs, send_sems, recv_sems):
    x, y, c, chips = _place()
    me_chip = 2 * x + y
    return [pltpu.make_async_remote_copy(src_ref=p_refs[t].at[2 * chip[0] + chip[1]], dst_ref=land_refs[t].at[me_chip],
                                         send_sem=send_sems.at[3 * t + j], recv_sem=recv_sems.at[3 * t + j],
                                         device_id=(*chip, c), device_id_type=MESH)
            for t in range(len(p_refs)) for j, chip in enumerate(chips)]


def _chip_exchange_start(ps):
    n = len(ps)

    def body(*refs):
        p_refs, land_refs = refs[:n], refs[n:2 * n]
        send_sems, recv_sems = refs[2 * n:2 * n + 2]
        token = refs[-1]
        for cp in _chip_copies(p_refs, land_refs, send_sems, recv_sems):
            cp.start()
        token[...] = jnp.zeros_like(token)

    hbm = [pltpu.HBM(p.shape, p.dtype) for p in ps]
    args = [pltpu.with_memory_space_constraint(p, pltpu.HBM) for p in ps]
    args += [pltpu.with_memory_space_constraint(lax.empty(p.shape, p.dtype), pltpu.HBM) for p in ps]
    out = pl.pallas_call(
        body,
        name="chip_exchange_start",
        out_shape=(pltpu.SemaphoreType.DMA((3 * n,)), pltpu.SemaphoreType.DMA((3 * n,)), *hbm, *hbm,
                   jax.ShapeDtypeStruct((8, LANES), F32)),
        in_specs=[_HBM] * (2 * n),
        out_specs=(_SEM, _SEM, *([_HBM] * (2 * n)), pl.BlockSpec(memory_space=pltpu.VMEM)),
        input_output_aliases={i: 2 + i for i in range(2 * n)},
        compiler_params=pltpu.CompilerParams(has_side_effects=_DATAFLOW),
    )(*args)
    return out[0], out[1], out[2:2 + n], out[2 + n:2 + 2 * n], out[-1]


def _chip_exchange_wait(send_sems, recv_sems, p_thru, land_thru, after):
    n = len(p_thru)

    def body(*refs):
        p_refs, land_refs = refs[:n], refs[n:2 * n]
        ssem, rsem = refs[2 * n:2 * n + 2]
        for cp in _chip_copies(p_refs, land_refs, ssem, rsem):
            cp.wait_send()
            cp.wait_recv()

    hbm = [pltpu.HBM(p.shape, p.dtype) for p in p_thru]
    out = pl.pallas_call(
        body,
        name="chip_exchange_wait",
        out_shape=(*hbm, *hbm),
        in_specs=[_HBM] * (2 * n) + [_SEM, _SEM, ANY],
        out_specs=tuple([_HBM] * (2 * n)),
        input_output_aliases={i: i for i in range(2 * n)},
        compiler_params=pltpu.CompilerParams(has_side_effects=_DATAFLOW),
    )(*p_thru, *land_thru, send_sems, recv_sems, after)
    return out[:n], out[n:]


def _shard_copies(s_refs, land_refs, send_sems, recv_sems):
    x, y, c, chips = _place()
    me_chip = 2 * x + y
    return [pltpu.make_async_remote_copy(src_ref=s_refs[t], dst_ref=land_refs[t].at[me_chip],
                                         send_sem=send_sems.at[3 * t + j], recv_sem=recv_sems.at[3 * t + j],
                                         device_id=(*chip, c), device_id_type=MESH)
            for t in range(len(s_refs)) for j, chip in enumerate(chips)]


def _gather_late_start(shards):
    n = len(shards)

    def body(*refs):
        s_refs, land_refs = refs[:n], refs[n:2 * n]
        send_sems, recv_sems = refs[2 * n:2 * n + 2]
        token = refs[-1]
        for cp in _shard_copies(s_refs, land_refs, send_sems, recv_sems):
            cp.start()
        token[...] = jnp.zeros_like(token)

    lands = [(N_CHIPS,) + s.shape for s in shards]
    args = [pltpu.with_memory_space_constraint(s, pltpu.HBM) for s in shards]
    args += [pltpu.with_memory_space_constraint(lax.empty(shp, s.dtype), pltpu.HBM) for shp, s in zip(lands, shards)]
    out = pl.pallas_call(
        body,
        name="gather_late_start",
        out_shape=(pltpu.SemaphoreType.DMA((3 * n,)), pltpu.SemaphoreType.DMA((3 * n,)),
                   *[pltpu.HBM(s.shape, s.dtype) for s in shards], *[pltpu.HBM(shp, s.dtype) for shp, s in zip(lands, shards)],
                   jax.ShapeDtypeStruct((8, LANES), F32)),
        in_specs=[_HBM] * (2 * n),
        out_specs=(_SEM, _SEM, *([_HBM] * (2 * n)), pl.BlockSpec(memory_space=pltpu.VMEM)),
        input_output_aliases={i: 2 + i for i in range(2 * n)},
        compiler_params=pltpu.CompilerParams(has_side_effects=_DATAFLOW),
    )(*args)
    return out[0], out[1], out[2:2 + n], out[2 + n:2 + 2 * n], out[-1]


def _gather_late_wait(send_sems, recv_sems, s_thru, land_thru, after):
    n = len(s_thru)

    def body(*refs):
        s_refs, land_refs = refs[:n], refs[n:2 * n]
        ssem, rsem = refs[2 * n:2 * n + 2]
        for cp in _shard_copies(s_refs, land_refs, ssem, rsem):
            cp.wait_send()
            cp.wait_recv()

    out = pl.pallas_call(
        body,
        name="gather_late_wait",
        out_shape=(*[pltpu.HBM(s.shape, s.dtype) for s in s_thru], *[pltpu.HBM(l.shape, l.dtype) for l in land_thru]),
        in_specs=[_HBM] * (2 * n) + [_SEM, _SEM, ANY],
        out_specs=tuple([_HBM] * (2 * n)),
        input_output_aliases={i: i for i in range(2 * n)},
        compiler_params=pltpu.CompilerParams(has_side_effects=_DATAFLOW),
    )(*s_thru, *land_thru, send_sems, recv_sems, after)
    return out[:n], out[n:]


def _pair_swap(rs):
    n = len(rs)

    def body(*refs):
        r_refs, o_refs = refs[:n], refs[n:2 * n]
        send_sems, recv_sems = refs[2 * n:]
        x, y, c, _ = _place()
        cps = []
        for t in range(n):
            cp = pltpu.make_async_remote_copy(src_ref=r_refs[t], dst_ref=o_refs[t], send_sem=send_sems.at[t],
                                              recv_sem=recv_sems.at[t], device_id=(x, y, 1 - c), device_id_type=MESH)
            cp.start()
            cps.append(cp)
        for cp in cps:
            cp.wait()

    return pl.pallas_call(
        body,
        out_shape=tuple(jax.ShapeDtypeStruct(r.shape, r.dtype) for r in rs),
        in_specs=[ANY] * n,
        out_specs=tuple([ANY] * n),
        scratch_shapes=[pltpu.SemaphoreType.DMA((n,)), pltpu.SemaphoreType.DMA((n,))],
        name="pair_swap",
    )(*rs)


N_DEV = 8
LOSS_ROW = 4


def _small_allreduce(small):
    def body(s_ref, o_ref, all_ref, send_sems, recv_sems):
        x, y, c, _ = _place()
        me = 4 * x + 2 * y + c
        all_ref[me] = s_ref[...]
        cps = []
        for k in range(1, N_DEV):
            peer = tuple(1 - p if (k >> s) & 1 else p for p, s in ((x, 2), (y, 1), (c, 0)))
            cp = pltpu.make_async_remote_copy(src_ref=s_ref, dst_ref=all_ref.at[me], send_sem=send_sems.at[k - 1],
                                              recv_sem=recv_sems.at[k - 1], device_id=peer, device_id_type=MESH)
            cp.start()
            cps.append(cp)
        for cp in cps:
            cp.wait()
        tot = all_ref[0]
        for d in range(1, N_DEV):
            tot = tot + all_ref[d]
        o_ref[...] = tot
        o_ref[LOSS_ROW:LOSS_ROW + 1, :] = jnp.broadcast_to(jnp.sum(tot[LOSS_ROW:LOSS_ROW + 1, :], axis=1, keepdims=True),
                                                          (1, tot.shape[1]))

    vm = pl.BlockSpec(memory_space=pltpu.VMEM)
    return pl.pallas_call(
        body,
        out_shape=jax.ShapeDtypeStruct(small.shape, small.dtype),
        in_specs=[vm],
        out_specs=vm,
        scratch_shapes=[pltpu.VMEM((N_DEV,) + small.shape, small.dtype), pltpu.SemaphoreType.DMA((N_DEV - 1,)),
                        pltpu.SemaphoreType.DMA((N_DEV - 1,))],
        name="small_allreduce",
    )(small)


def _sum_pair(g, recv, cidx, *, tr, name):
    n, hr, cols = recv.shape
    nr = hr // tr

    def body(c_ref, g_ref, r_ref, o_ref):
        o_ref[...] = (g_ref[...].astype(F32) + r_ref[...].astype(F32)).astype(o_ref.dtype)

    grid_spec = pltpu.PrefetchScalarGridSpec(
        num_scalar_prefetch=1,
        grid=(n, nr),
        in_specs=[pl.BlockSpec((None, tr, cols), lambda k, i, c_ref: (k, c_ref[0] * nr + i, 0)),
                  pl.BlockSpec((None, tr, cols), lambda k, i, c_ref: (k, i, 0))],
        out_specs=pl.BlockSpec((None, tr, cols), lambda k, i, c_ref: (k, i, 0)),
    )
    return pl.pallas_call(body, out_shape=jax.ShapeDtypeStruct(recv.shape, BF16), grid_spec=grid_spec,
                          compiler_params=_cparams(), name=name)(cidx, g, recv)


def _sum_chips(p, *, tr, name):
    _, rows, cols = p.shape

    def body(p_ref, o_ref):
        tot = p_ref[0].astype(F32)
        for k in range(1, N_CHIPS):
            tot = tot + p_ref[k].astype(F32)
        o_ref[...] = tot

    return pl.pallas_call(
        body,
        out_shape=jax.ShapeDtypeStruct((rows, cols), F32),
        grid=(rows // tr,),
        in_specs=[pl.BlockSpec((N_CHIPS, tr, cols), lambda i: (0, i, 0))],
        out_specs=pl.BlockSpec((tr, cols), lambda i: (i, 0)),
        compiler_params=_cparams(),
        name=name,
    )(p)


def _adamw(w, g, m, v, *, tr, name):
    rows, cols = w.shape
    bc1 = 1.0 / (1.0 - ADAM_B1 ** ADAM_STEP)
    bc2 = 1.0 / (1.0 - ADAM_B2 ** ADAM_STEP)

    def body(w_ref, g_ref, m_ref, v_ref, d_ref, nm_ref, nv_ref):
        gv = g_ref[...]
        nm = ADAM_B1 * m_ref[...] + (1.0 - ADAM_B1) * gv
        nv = ADAM_B2 * v_ref[...] + (1.0 - ADAM_B2) * (gv * gv)
        d_ref[...] = -ADAM_LR * ((nm * bc1) / (jnp.sqrt(nv * bc2) + ADAM_EPS) + ADAM_WD * w_ref[...])
        nm_ref[...] = nm
        nv_ref[...] = nv

    spec = pl.BlockSpec((tr, cols), lambda i: (i, 0))
    sd = jax.ShapeDtypeStruct((rows, cols), F32)
    return pl.pallas_call(body, out_shape=(sd, sd, sd), grid=(rows // tr,), in_specs=[spec] * 4, out_specs=(spec,) * 3,
                          compiler_params=_cparams(), name=name)(w, g, m, v)


def _adamw_halves(w, own, sib, cidx, m, v, *, tr, name):
    rows, cols = w.shape
    hr = own.shape[0]
    nr = hr // tr
    assert rows == 2 * hr and hr % tr == 0
    bc1 = 1.0 / (1.0 - ADAM_B1 ** ADAM_STEP)
    bc2 = 1.0 / (1.0 - ADAM_B2 ** ADAM_STEP)

    def body(c_ref, w_ref, o_ref, s_ref, m_ref, v_ref, g_ref, d_ref, nm_ref, nv_ref):
        mine = (pl.program_id(0) // nr) == c_ref[0]
        gv = jnp.where(mine, o_ref[...], s_ref[...])
        nm = ADAM_B1 * m_ref[...] + (1.0 - ADAM_B1) * gv
        nv = ADAM_B2 * v_ref[...] + (1.0 - ADAM_B2) * (gv * gv)
        g_ref[...] = gv
        d_ref[...] = -ADAM_LR * ((nm * bc1) / (jnp.sqrt(nv * bc2) + ADAM_EPS) + ADAM_WD * w_ref[...])
        nm_ref[...] = nm
        nv_ref[...] = nv

    full = pl.BlockSpec((tr, cols), lambda i, c_ref: (i, 0))
    half = pl.BlockSpec((tr, cols), lambda i, c_ref: (i % nr, 0))
    sd = jax.ShapeDtypeStruct((rows, cols), F32)
    grid_spec = pltpu.PrefetchScalarGridSpec(num_scalar_prefetch=1, grid=(rows // tr,), in_specs=[full, half, half, full, full],
                                             out_specs=(full,) * 4)
    return pl.pallas_call(body, out_shape=(sd,) * 4, grid_spec=grid_spec, compiler_params=_cparams(), name=name)(
        cidx, w, own, sib, m, v)


def _pack_small(norm, mem_norm, final_norm, b_forget):
    rows = [norm.reshape(1, D_MODEL), mem_norm.reshape(1, D_MODEL), final_norm.reshape(1, D_MODEL),
            jnp.pad(b_forget.reshape(1, FOX_HEADS), ((0, 0), (0, D_MODEL - FOX_HEADS))), jnp.zeros((4, D_MODEL), F32)]
    return jnp.concatenate(rows, axis=0)


def _unpack_small(a):
    return a[0:1], a[3:4, :FOX_HEADS], a[1:2], a[2]


def kernel(x, mem, norm_g, w_in, b_forget, mem_norm_g, w_mem_kv, w_out, final_norm_g, loss_target, m_norm_g, m_w_in, m_b_forget, m_mem_norm_g, m_w_mem_kv, m_w_out, m_final_norm_g, v_norm_g, v_w_in, v_b_forget, v_mem_norm_g, v_w_mem_kv, v_w_out, v_final_norm_g):
    core = lax.axis_index("c").astype(jnp.int32)
    me_chip = (2 * lax.axis_index("x") + lax.axis_index("y")).astype(jnp.int32)
    cidx = core.reshape(1)

    def own_slot(arr, own):
        return lax.dynamic_update_slice(arr, own[None].astype(arr.dtype), (me_chip,) + (0,) * own.ndim)

    win_b, late = w_in[0].astype(BF16), [w_mem_kv[0].astype(BF16), w_out[0].astype(BF16)]
    g_in, = _gather_weights([win_b])
    g_in, late = lax.optimization_barrier((own_slot(g_in, win_b), late))
    w_r = _rearrange_w_in([g_in[k] for k in range(N_CHIPS)])
    *late_flight, early_token = _gather_late_start(late)

    def late_weights(after):
        shards, landed = _gather_late_wait(*late_flight, after)
        g_kv, g_out = (own_slot(g, s) for g, s in zip(landed, shards))
        return g_kv.reshape(D_MODEL, 2 * MEM_W), g_out.reshape(MIX_W, D_MODEL)

    trs = (128, 128, 256)
    names = ("w_in", "w_mem_kv", "w_out")
    flight = []

    def start_reduce(g_wr, g_wkv, g_wo):
        slabs = [g_wr[None],
                 g_wkv.reshape(N_CHIPS, D_MODEL // N_CHIPS, 2 * MEM_W),
                 g_wo.reshape(N_CHIPS, MIX_W // N_CHIPS, D_MODEL)]
        recv = _pair_exchange(slabs)
        pair = [_sum_pair(g, r, cidx, tr=tr, name=f"sum_pair_{nm}") for g, r, tr, nm in zip(slabs, recv, trs, names)]
        pair[0] = _w_in_grad_slabs(pair[0][0])
        *handles, token = _chip_exchange_start(pair)
        flight.extend(handles)
        return token

    gx, g_wr, g_wkv, g_wo, small = _local_grads(x, mem, norm_g, w_r, b_forget, mem_norm_g, None, None, final_norm_g, loss_target,
                                                start_reduce=start_reduce, early_token=early_token, late_weights=late_weights)

    send_sems, recv_sems, pair, land = flight
    pair, landed = _chip_exchange_wait(send_sems, recv_sems, pair, land, small)
    got = [lax.dynamic_update_slice(g, lax.dynamic_slice(p, (me_chip, 0, 0), (1,) + p.shape[1:]), (me_chip, 0, 0))
           for g, p in zip(landed, pair)]
    red = [_sum_chips(p, tr=tr, name=f"sum_chips_{nm}") for p, tr, nm in zip(got, trs, names)]
    sib = _pair_swap(red)

    outs = {}
    for nm, r, s, w, m, v, tr in zip(names, red, sib, (w_in, w_mem_kv, w_out), (m_w_in, m_w_mem_kv, m_w_out),
                                     (v_w_in, v_w_mem_kv, v_w_out), trs):
        outs[nm] = tuple(a[None] for a in _adamw_halves(w[0], r, s, cidx, m[0], v[0], tr=tr, name=f"adamw_{nm}"))

    gsum = _small_allreduce(small)
    sd, sm, sv = _adamw(_pack_small(norm_g, mem_norm_g, final_norm_g, b_forget), gsum,
                        _pack_small(m_norm_g, m_mem_norm_g, m_final_norm_g, m_b_forget),
                        _pack_small(v_norm_g, v_mem_norm_g, v_final_norm_g, v_b_forget), tr=8, name="adamw_small")
    loss = gsum[LOSS_ROW, 0]

    def group(i, small_arr):
        ng, bf, mg, fg = _unpack_small(small_arr)
        return (ng, outs["w_in"][i], bf, mg, outs["w_mem_kv"][i], outs["w_out"][i], fg)

    return (loss, gx, *group(0, gsum), *group(1, sd), *group(2, sm), *group(3, sv))
```

```python
import functools
import math

import jax
import jax.numpy as jnp
from jax import lax
from jax.experimental import pallas as pl
from jax.experimental.pallas import tpu as pltpu

F32 = jnp.float32
BF16 = jnp.bfloat16

D_MODEL = 1024
SEQ = 2048
HEAD_DIM = 64
FOX_HEADS = 12
DIL_HEADS = 12
MEM_HEADS = 4
MEM_HEAD_DIM = 128
MEM_LEN = 256
FOX_W = FOX_HEADS * HEAD_DIM
DIL_W = DIL_HEADS * HEAD_DIM
MEM_W = MEM_HEADS * MEM_HEAD_DIM
MIX_W = FOX_W + DIL_W + MEM_W
DILATIONS = ((128, 1), (512, 4), (2048, 16))
ROPE_THETA = 500000.0
ROPE_DIM = HEAD_DIM // 4
RMS_EPS = 1e-6
NEG_INF = -1e30
IN_SIZES = [FOX_W] * 4 + [FOX_HEADS] + [DIL_W] * 4 + [MEM_W] * 2
IN_W = sum(IN_SIZES)

ADAM_LR = 0.001
ADAM_B1 = 0.9
ADAM_B2 = 0.999
ADAM_EPS = 1e-08
ADAM_WD = 0.01
ADAM_STEP = 10

LANES = 128
N_CHIPS = 4
PW = 7168
PWF = PW + 4 * LANES
C_FQ, C_FK, C_FV, C_FG = 0, 768, 1536, 2304
C_DQ, C_DK, C_DV, C_DG = 3072, 3840, 4608, 5376
C_MQ, C_MG = 6144, 6656
VMEM_LIMIT = 48 * 1024 * 1024


def _cparams(**kw):
    return pltpu.CompilerParams(vmem_limit_bytes=VMEM_LIMIT, **kw)


def _matmul(a, b, *, out_dtype, tm, tn, tk, name, mode="nn"):
    if mode == "tn":
        (kdim, m), n = a.shape, b.shape[1]
        a_spec = pl.BlockSpec((tk, tm), lambda i, j, k: (k, i))
        b_spec = pl.BlockSpec((tk, tn), lambda i, j, k: (k, j))
        dims = _T0
    elif mode == "nt":
        (m, kdim), n = a.shape, b.shape[0]
        a_spec = pl.BlockSpec((tm, tk), lambda i, j, k: (i, k))
        b_spec = pl.BlockSpec((tn, tk), lambda i, j, k: (j, k))
        dims = _NT
    else:
        (m, kdim), n = a.shape, b.shape[1]
        a_spec = pl.BlockSpec((tm, tk), lambda i, j, k: (i, k))
        b_spec = pl.BlockSpec((tk, tn), lambda i, j, k: (k, j))
        dims = (((1,), (0,)), ((), ()))
    nk = kdim // tk
    assert m % tm == 0 and n % tn == 0 and kdim % tk == 0

    def body(a_ref, b_ref, o_ref, *scratch):
        prod = lax.dot_general(a_ref[...], b_ref[...], dims, preferred_element_type=F32)
        if nk == 1:
            o_ref[...] = prod.astype(o_ref.dtype)
            return
        acc_ref, = scratch
        k = pl.program_id(2)

        @pl.when(k == 0)
        def _():
            acc_ref[...] = prod

        @pl.when(k > 0)
        def _():
            acc_ref[...] += prod

        @pl.when(k == nk - 1)
        def _():
            o_ref[...] = acc_ref[...].astype(o_ref.dtype)

    return pl.pallas_call(
        body,
        out_shape=jax.ShapeDtypeStruct((m, n), out_dtype),
        grid=(m // tm, n // tn, nk),
        in_specs=[a_spec, b_spec],
        out_specs=pl.BlockSpec((tm, tn), lambda i, j, k: (i, j)),
        scratch_shapes=[pltpu.VMEM((tm, tn), F32)] if nk > 1 else [],
        compiler_params=_cparams(dimension_semantics=("parallel", "parallel", "arbitrary")),
        name=name,
    )(a, b)


def _rms_fwd(x, g, *, tm, name):
    t, d = x.shape

    def body(x_ref, g_ref, h_ref):
        xv = x_ref[...]
        r = lax.rsqrt(jnp.mean(xv * xv, axis=-1, keepdims=True) + RMS_EPS)
        h_ref[...] = (xv * r * g_ref[...]).astype(h_ref.dtype)

    return pl.pallas_call(
        body,
        out_shape=jax.ShapeDtypeStruct((t, d), BF16),
        grid=(t // tm,),
        in_specs=[pl.BlockSpec((tm, d), lambda i: (i, 0)), pl.BlockSpec((1, d), lambda i: (0, 0))],
        out_specs=pl.BlockSpec((tm, d), lambda i: (i, 0)),
        compiler_params=_cparams(),
        name=name,
    )(x, g)


def _rope_tables():
    half = ROPE_DIM // 2
    pos = jnp.arange(SEQ, dtype=F32)
    inv_freq = 1.0 / (ROPE_THETA ** (jnp.arange(0, ROPE_DIM, 2, dtype=F32) / ROPE_DIM))
    ang = pos[:, None] * inv_freq[None, :]
    cos, sin = jnp.cos(ang), jnp.sin(ang)
    one = jnp.ones((SEQ, HEAD_DIM - ROPE_DIM), F32)
    zero = jnp.zeros((SEQ, HEAD_DIM - ROPE_DIM), F32)
    zh = jnp.zeros((SEQ, half), F32)
    c = jnp.concatenate([cos, cos, one], axis=1)
    s1 = jnp.concatenate([zh, sin, zero], axis=1)
    s2 = jnp.concatenate([-sin, zh, zero], axis=1)
    rep = LANES // HEAD_DIM
    return jnp.tile(c, (1, rep)), jnp.tile(s1, (1, rep)), jnp.tile(s2, (1, rep))


def _rope_apply(t, c, s1, s2, transpose=False):
    n = t.shape[-1]
    rep = n // LANES
    c, s1, s2 = (jnp.tile(u, (1, rep)) for u in (c, s1, s2))
    half = ROPE_DIM // 2
    if not transpose:
        return t * c + pltpu.roll(t, half, 1) * s1 + pltpu.roll(t, n - half, 1) * s2
    return t * c + pltpu.roll(t * s1, n - half, 1) + pltpu.roll(t * s2, half, 1)


def _proj(h, w, tabs, *, n, tm, tn, name):
    t, d = h.shape
    assert C_DQ % tn == 0 and (C_DV - C_DQ) % tn == 0 and (C_DG - C_DQ) % tn == 0
    rope_lo, rope_hi, dil_hi = C_DQ // tn, C_DV // tn, C_DG // tn
    s_blocks = SEQ // tm

    def body(h_ref, w_ref, c_ref, s1_ref, s2_ref, o_ref, f_ref):
        j = pl.program_id(1)
        acc = jnp.dot(h_ref[...], w_ref[...], preferred_element_type=F32)
        is_rope = jnp.logical_and(j >= rope_lo, j < rope_hi)

        @pl.when(is_rope)
        def _():
            r = _rope_apply(acc, c_ref[...], s1_ref[...], s2_ref[...])
            o_ref[...] = r.astype(o_ref.dtype)
            f_ref[...] = r

        @pl.when(jnp.logical_not(is_rope))
        def _():
            o_ref[...] = acc.astype(o_ref.dtype)

        @pl.when(jnp.logical_and(j >= rope_hi, j < dil_hi))
        def _():
            f_ref[...] = acc

    tab_spec = pl.BlockSpec((tm, LANES), lambda i, j: (i % s_blocks, 0))
    f_spec = pl.BlockSpec((tm, tn), lambda i, j: (i, jnp.clip(j - rope_lo, 0, dil_hi - rope_lo - 1)))
    return pl.pallas_call(
        body,
        out_shape=(jax.ShapeDtypeStruct((t, n), BF16), jax.ShapeDtypeStruct((t, 3 * DIL_W), F32)),
        grid=(t // tm, n // tn),
        in_specs=[pl.BlockSpec((tm, d), lambda i, j: (i, 0)), pl.BlockSpec((d, tn), lambda i, j: (0, j)),
                  tab_spec, tab_spec, tab_spec],
        out_specs=(pl.BlockSpec((tm, tn), lambda i, j: (i, j)), f_spec),
        compiler_params=_cparams(dimension_semantics=("parallel", "arbitrary")),
        name=name,
    )(h, w, *tabs)


def _split3(x):
    hi = x.astype(BF16)
    r1 = x - hi.astype(F32)
    mid = r1.astype(BF16)
    lo = (r1 - mid.astype(F32)).astype(BF16)
    return hi, mid, lo


def _dot3(sel, x, sel_is_lhs):
    out = None
    for piece in _split3(x):
        t = jnp.dot(sel, piece, preferred_element_type=F32) if sel_is_lhs else jnp.dot(piece, sel, preferred_element_type=F32)
        out = t if out is None else out + t
    return out


def _flog_fwd(flog, bpad, *, nb, ts, name):
    ns = SEQ // ts

    def body(f_ref, b_ref, c_ref, carry_ref):
        s = pl.program_id(1)

        @pl.when(s == 0)
        def _():
            carry_ref[...] = jnp.zeros_like(carry_ref)

        z = f_ref[...] + b_ref[...]
        logf = jnp.minimum(z, 0.0) - jnp.log(1.0 + jnp.exp(-jnp.abs(z)))
        r = lax.broadcasted_iota(jnp.int32, (ts, ts), 0)
        c = lax.broadcasted_iota(jnp.int32, (ts, ts), 1)
        tri = jnp.where(r >= c, 1.0, 0.0).astype(BF16)
        cs = _dot3(tri, logf, True) + carry_ref[0:1, :]
        carry_ref[...] = jnp.broadcast_to(cs[ts - 1:ts, :], carry_ref.shape)
        c_ref[...] = cs

    return pl.pallas_call(
        body,
        out_shape=jax.ShapeDtypeStruct((nb * SEQ, LANES), F32),
        grid=(nb, ns),
        in_specs=[pl.BlockSpec((ts, LANES), lambda b, s: (b * ns + s, 0)), pl.BlockSpec((1, LANES), lambda b, s: (0, 0))],
        out_specs=pl.BlockSpec((ts, LANES), lambda b, s: (b * ns + s, 0)),
        scratch_shapes=[pltpu.VMEM((8, LANES), F32)],
        compiler_params=_cparams(dimension_semantics=("parallel", "arbitrary")),
        name=name,
    )(flog, bpad)


def _flog_bwd(dcol, flog, bpad, *, nb, ts, name):
    ns = SEQ // ts

    def body(d_ref, f_ref, b_ref, o_ref, gb_ref, carry_ref):
        bi = pl.program_id(0)
        s = pl.program_id(1)

        @pl.when(s == 0)
        def _():
            carry_ref[...] = jnp.zeros_like(carry_ref)

        @pl.when(jnp.logical_and(bi == 0, s == 0))
        def _():
            gb_ref[...] = jnp.zeros_like(gb_ref)

        r = lax.broadcasted_iota(jnp.int32, (ts, ts), 0)
        c = lax.broadcasted_iota(jnp.int32, (ts, ts), 1)
        tri = jnp.where(r <= c, 1.0, 0.0).astype(BF16)
        rc = _dot3(tri, d_ref[...], True) + carry_ref[0:1, :]
        carry_ref[...] = jnp.broadcast_to(rc[0:1, :], carry_ref.shape)
        z = f_ref[...] + b_ref[...]
        dz = rc / (1.0 + jnp.exp(z))
        o_ref[...] = dz.astype(o_ref.dtype)
        gb_ref[...] += jnp.broadcast_to(jnp.sum(dz, axis=0, keepdims=True), gb_ref.shape)

    rev = lambda b, s: (b * ns + (ns - 1 - s), 0)
    return pl.pallas_call(
        body,
        out_shape=(jax.ShapeDtypeStruct((nb * SEQ, LANES), BF16), jax.ShapeDtypeStruct((8, LANES), F32)),
        grid=(nb, ns),
        in_specs=[pl.BlockSpec((ts, LANES), rev), pl.BlockSpec((ts, LANES), rev), pl.BlockSpec((1, LANES), lambda b, s: (0, 0))],
        out_specs=(pl.BlockSpec((ts, LANES), rev), pl.BlockSpec((8, LANES), lambda b, s: (0, 0))),
        scratch_shapes=[pltpu.VMEM((8, LANES), F32)],
        compiler_params=_cparams(dimension_semantics=("arbitrary", "arbitrary")),
        name=name,
    )(dcol, flog, bpad)


MEM_TQ = 256
MEM_SET = 4
MEM_SCALE = 1.0 / math.sqrt(MEM_HEAD_DIM)
assert MEM_HEAD_DIM == LANES and SEQ % (MEM_TQ * MEM_SET) == 0


def _head_masks(nh):
    lane = lax.broadcasted_iota(jnp.int32, (1, LANES), 1)
    return [None] if nh == 1 else [lane < HEAD_DIM, lane >= HEAD_DIM]


def _mem_specs(qoff):
    qspec = pl.BlockSpec((None, SEQ, LANES), lambda b, j: (b, 0, qoff + j))
    kspec = pl.BlockSpec((None, MEM_LEN, LANES), lambda b, j: (b, 0, j))
    vspec = pl.BlockSpec((None, MEM_LEN, LANES), lambda b, j: (b, 0, MEM_HEADS + j))
    ospec = pl.BlockSpec((None, SEQ, LANES), lambda b, j: (b, 0, j))
    return qspec, kspec, vspec, ospec


def _mem_rows(g):
    return [pl.ds(pl.multiple_of((MEM_SET * g + a) * MEM_TQ, MEM_TQ), MEM_TQ) for a in range(MEM_SET)]


def _mem_fwd(p3, mkv3, *, qoff, name):
    nb = p3.shape[0]

    def body(q_ref, k_ref, v_ref, o_ref, lse_ref):
        kb, vb = k_ref[...], v_ref[...]

        def qset(g, c):
            rows = _mem_rows(g)
            ss = [lax.dot_general(q_ref[r, :] * MEM_SCALE, kb, _NT, preferred_element_type=F32) for r in rows]
            for r, s in zip(rows, ss):
                m = jnp.max(s, axis=1, keepdims=True)
                p = jnp.exp(s - m)
                l = jnp.sum(p, axis=1, keepdims=True)
                o_ref[r, :] = jnp.dot(p.astype(BF16), vb, preferred_element_type=F32) / l
                lse_ref[r, :] = jnp.broadcast_to(m + jnp.log(l), (MEM_TQ, LANES))
            return c

        lax.fori_loop(0, SEQ // MEM_TQ // MEM_SET, qset, 0)

    qspec, kspec, vspec, ospec = _mem_specs(qoff)
    osd = jax.ShapeDtypeStruct((nb, SEQ, MEM_W), F32)
    return pl.pallas_call(body, out_shape=(osd, osd), grid=(nb, MEM_HEADS), in_specs=[qspec, kspec, vspec],
                          out_specs=(ospec, ospec), compiler_params=_cparams(dimension_semantics=("parallel", "parallel")),
                          name=name)(p3, mkv3, mkv3)


def _mem_bwd(p3, mkv3, do, o, lse, *, qoff, do_off, name):
    nb = p3.shape[0]

    def body(q_ref, k_ref, v_ref, do_ref, o_ref, lse_ref, dq_ref, dk_ref, dv_ref):
        kb, vb = k_ref[...], v_ref[...]
        ks = kb * MEM_SCALE

        def qset(g, carry):
            dk, dv = carry
            work = []
            for r in _mem_rows(g):
                qs = q_ref[r, :] * MEM_SCALE
                dob = do_ref[r, :].astype(BF16)
                s = lax.dot_general(qs, kb, _NT, preferred_element_type=F32)
                dp = lax.dot_general(dob, vb, _NT, preferred_element_type=F32)
                work.append((r, qs, dob, s, dp))
            for r, qs, dob, s, dp in work:
                delta = jnp.sum(dob.astype(F32) * o_ref[r, :], axis=1, keepdims=True)
                p = jnp.exp(s - lse_ref[r, :][:, 0:1])
                ds = (p * (dp - delta)).astype(BF16)
                dq_ref[r, :] = jnp.dot(ds, ks, preferred_element_type=F32).astype(dq_ref.dtype)
                dk = dk + lax.dot_general(ds, qs, _T0, preferred_element_type=F32)
                dv = dv + lax.dot_general(p.astype(BF16), dob, _T0, preferred_element_type=F32)
            return dk, dv

        z = jnp.zeros((MEM_LEN, LANES), F32)
        dk, dv = lax.fori_loop(0, SEQ // MEM_TQ // MEM_SET, qset, (z, z))
        dk_ref[...] = dk
        dv_ref[...] = dv

    qspec, kspec, vspec, ospec = _mem_specs(qoff)
    dospec = pl.BlockSpec((None, SEQ, LANES), lambda b, j: (b, 0, do_off + j))
    kvo = pl.BlockSpec((None, MEM_LEN, LANES), lambda b, j: (b, 0, j))
    kvsd = jax.ShapeDtypeStruct((nb, MEM_LEN, MEM_W), F32)
    return pl.pallas_call(
        body, out_shape=(jax.ShapeDtypeStruct((nb, SEQ, MEM_W), BF16), kvsd, kvsd), grid=(nb, MEM_HEADS),
        in_specs=[qspec, kspec, vspec, dospec, ospec, ospec], out_specs=(ospec, kvo, kvo),
        compiler_params=_cparams(dimension_semantics=("parallel", "parallel")), name=name)(p3, mkv3, mkv3, do, o, lse)


BLK = 128
NBLK = SEQ // BLK
QK_SCALE = 1.0 / math.sqrt(HEAD_DIM)
DIL_STEPS = tuple(d for _, d in DILATIONS)
assert all(w // d == BLK for w, d in DILATIONS)
_T0 = (((0,), (0,)), ((), ()))
_NT = (((1,), (1,)), ((), ()))


def _stack_heads(a, masks):
    z = jnp.zeros_like(a)
    return jnp.concatenate([jnp.where(masks[0], a, z), jnp.where(masks[1], a, z)], axis=0)


def _tri_bias(lower):
    r = lax.broadcasted_iota(jnp.int32, (BLK, BLK), 0)
    c = lax.broadcasted_iota(jnp.int32, (BLK, BLK), 1)
    return jnp.where((c <= r) if lower else (c >= r), 0.0, NEG_INF).astype(F32)


def _dil_rows(r, i, d):
    start = r + i * (BLK * d)
    return pl.ds(start, BLK) if d == 1 else pl.ds(start, BLK, stride=d)


DIL_SET = 4


def _dil_sets(d, fn):
    nbk = SEQ // d // BLK
    if d == 1:
        n = 2 * DIL_SET
        def gbody(g, c):
            fn([(0, n * g + a, None if a == 0 else True) for a in range(n)])
            return c
        lax.fori_loop(0, nbk // n, gbody, 0)
    elif nbk > 1:
        assert nbk == DIL_SET
        def rbody(r, c):
            fn([(r, i, i > 0) for i in range(nbk)])
            return c
        lax.fori_loop(0, d, rbody, 0)
    else:
        def rbody(rr, c):
            fn([(DIL_SET * rr + a, 0, False) for a in range(DIL_SET)])
            return c
        lax.fori_loop(0, d // DIL_SET, rbody, 0)


def _dil_key_tiles(r, i, d, has_prev, qrows, tri_cur, tri_prev):
    tiles = [(qrows, tri_cur)]
    if has_prev is None:
        tiles.append((_dil_rows(r, jnp.maximum(i - 1, 0), d), tri_prev + jnp.where(i > 0, 0.0, NEG_INF)))
    elif has_prev:
        tiles.append((_dil_rows(r, i - 1, d), tri_prev))
    return tiles


def _dil_fwd(qkv, *, name):
    nb = qkv.shape[0]
    ncol = DIL_W // LANES
    hd = HEAD_DIM

    def body(q_ref, k_ref, v_ref, o_ref, lse_ref, m_ref, l_ref, a_ref):
        masks = _head_masks(2)
        tri_cur, tri_prev = _tri_bias(True), _tri_bias(False)
        for pi, d in enumerate(DIL_STEPS):
            first, last = pi == 0, pi == len(DIL_STEPS) - 1

            def qset(blocks, d=d, first=first, last=last):
                work = []
                for r, i, has_prev in blocks:
                    qrows = _dil_rows(r, i, d)
                    qcat = _stack_heads((q_ref[qrows, :] * QK_SCALE).astype(BF16), masks)
                    ss, krs = [], []
                    for krows, bias in _dil_key_tiles(r, i, d, has_prev, qrows, tri_cur, tri_prev):
                        s = lax.dot_general(qcat, k_ref[krows, :].astype(BF16), _NT, preferred_element_type=F32)
                        ss.append((s[:BLK] + bias, s[BLK:] + bias))
                        krs.append(krows)
                    work.append((qrows, ss, krs))
                for qrows, ss, krs in work:
                    e0 = ss[0][0] if len(ss) == 1 else jnp.maximum(ss[0][0], ss[1][0])
                    e1 = ss[0][1] if len(ss) == 1 else jnp.maximum(ss[0][1], ss[1][1])
                    n0 = jnp.max(e0, axis=1, keepdims=True)
                    n1 = jnp.max(e1, axis=1, keepdims=True)
                    if not first:
                        mo, lo = m_ref[qrows, :], l_ref[qrows, :]
                        m0, m1 = mo[:, 0:1], mo[:, hd:hd + 1]
                        n0, n1 = jnp.maximum(n0, m0), jnp.maximum(n1, m1)
                        a0, a1 = jnp.exp(m0 - n0), jnp.exp(m1 - n1)
                    ps = [(jnp.exp(s0 - n0), jnp.exp(s1 - n1)) for s0, s1 in ss]
                    t0 = ps[0][0] if len(ps) == 1 else ps[0][0] + ps[1][0]
                    t1 = ps[0][1] if len(ps) == 1 else ps[0][1] + ps[1][1]
                    l0 = jnp.sum(t0, axis=1, keepdims=True)
                    l1 = jnp.sum(t1, axis=1, keepdims=True)
                    acc = None
                    for (p0, p1), krows in zip(ps, krs):
                        vcat = _stack_heads(v_ref[krows, :].astype(BF16), masks)
                        pv = jnp.dot(jnp.concatenate([p0, p1], axis=1).astype(BF16), vcat, preferred_element_type=F32)
                        acc = pv if acc is None else acc + pv
                    if not first:
                        l0 = l0 + a0 * lo[:, 0:1]
                        l1 = l1 + a1 * lo[:, hd:hd + 1]
                        acc = acc + a_ref[qrows, :] * jnp.where(masks[0], a0, a1)
                    if last:
                        o_ref[qrows, :] = acc / jnp.where(masks[0], l0, l1)
                        lse_ref[qrows, :] = jnp.where(masks[0], n0 + jnp.log(l0), n1 + jnp.log(l1))
                    else:
                        m_ref[qrows, :] = jnp.where(masks[0], n0, n1)
                        l_ref[qrows, :] = jnp.where(masks[0], l0, l1)
                        a_ref[qrows, :] = acc

            _dil_sets(d, qset)

    spec = lambda off: pl.BlockSpec((None, SEQ, LANES), lambda b, j: (b, 0, off + j))
    ospec = pl.BlockSpec((None, SEQ, LANES), lambda b, j: (b, 0, j))
    osd = jax.ShapeDtypeStruct((nb, SEQ, DIL_W), F32)
    return pl.pallas_call(
        body, out_shape=(osd, osd), grid=(nb, ncol),
        in_specs=[spec(0), spec(ncol), spec(2 * ncol)], out_specs=(ospec, ospec),
        scratch_shapes=[pltpu.VMEM((SEQ, LANES), F32)] * 3,
        compiler_params=_cparams(dimension_semantics=("parallel", "parallel")), name=name,
    )(qkv, qkv, qkv)


def _dil_bwd(qkv, do, o, lse, tabs, *, do_off, name):
    nb = qkv.shape[0]
    ncol = DIL_W // LANES
    hd = HEAD_DIM

    def body(q_ref, k_ref, v_ref, do_ref, o_ref, lse_ref, c_ref, s1_ref, s2_ref, dqo_ref, dko_ref, dvo_ref,
             dq_ref, dk_ref, dv_ref, dl_ref, dof_ref):
        masks = _head_masks(2)
        tri_cur, tri_prev = _tri_bias(True), _tri_bias(False)
        dq_ref[...] = jnp.zeros_like(dq_ref)
        dk_ref[...] = jnp.zeros_like(dk_ref)
        dv_ref[...] = jnp.zeros_like(dv_ref)

        def delta_body(i, c):
            rows = pl.ds(pl.multiple_of(i * BLK, BLK), BLK)
            dof = do_ref[rows, :].astype(F32)
            dof_ref[rows, :] = dof
            prod = dof * o_ref[rows, :]
            z = jnp.zeros_like(prod)
            dl_ref[rows, :] = jnp.where(masks[0], jnp.sum(jnp.where(masks[0], prod, z), axis=1, keepdims=True),
                                        jnp.sum(jnp.where(masks[1], prod, z), axis=1, keepdims=True))
            return c

        lax.fori_loop(0, NBLK, delta_body, 0)

        for d in DIL_STEPS:
            def qset(blocks, d=d):
                work = []
                for r, i, has_prev in blocks:
                    qrows = _dil_rows(r, i, d)
                    qcat = _stack_heads((q_ref[qrows, :] * QK_SCALE).astype(BF16), masks)
                    docat = _stack_heads(dof_ref[qrows, :].astype(BF16), masks)
                    tiles = []
                    for krows, bias in _dil_key_tiles(r, i, d, has_prev, qrows, tri_cur, tri_prev):
                        s = lax.dot_general(qcat, k_ref[krows, :].astype(BF16), _NT, preferred_element_type=F32)
                        dp = lax.dot_general(docat, v_ref[krows, :].astype(BF16), _NT, preferred_element_type=F32)
                        tiles.append((krows, s, dp, bias))
                    work.append((qrows, qcat, docat, tiles))
                for qrows, qcat, docat, tiles in work:
                    lseb, dlb = lse_ref[qrows, :], dl_ref[qrows, :]
                    lse0, lse1 = lseb[:, 0:1], lseb[:, hd:hd + 1]
                    dl0, dl1 = dlb[:, 0:1], dlb[:, hd:hd + 1]
                    dq = None
                    for krows, s, dp, bias in tiles:
                        p0 = jnp.exp(s[:BLK] + bias - lse0)
                        p1 = jnp.exp(s[BLK:] + bias - lse1)
                        ds0 = p0 * (dp[:BLK] - dl0)
                        ds1 = p1 * (dp[BLK:] - dl1)
                        ds0b, ds1b = ds0.astype(BF16), ds1.astype(BF16)
                        pcat = jnp.concatenate([p0.astype(BF16), p1.astype(BF16)], axis=0)
                        dscat = jnp.concatenate([ds0b, ds1b], axis=0)
                        dv_ref[krows, :] += lax.dot_general(pcat, docat, _T0, preferred_element_type=F32)
                        dk_ref[krows, :] += lax.dot_general(dscat, qcat, _T0, preferred_element_type=F32)
                        dsrow = jnp.concatenate([ds0b, ds1b], axis=1)
                        kcat = _stack_heads((k_ref[krows, :] * QK_SCALE).astype(BF16), masks)
                        t = jnp.dot(dsrow, kcat, preferred_element_type=F32)
                        dq = t if dq is None else dq + t
                    dq_ref[qrows, :] += dq

            _dil_sets(d, qset)

        def out_body(i, c):
            rows = pl.ds(pl.multiple_of(i * BLK, BLK), BLK)
            tab = (c_ref[rows, :], s1_ref[rows, :], s2_ref[rows, :])
            dqo_ref[rows, :] = _rope_apply(dq_ref[rows, :], *tab, transpose=True).astype(dqo_ref.dtype)
            dko_ref[rows, :] = _rope_apply(dk_ref[rows, :], *tab, transpose=True).astype(dko_ref.dtype)
            dvo_ref[rows, :] = dv_ref[rows, :].astype(dvo_ref.dtype)
            return c

        lax.fori_loop(0, NBLK, out_body, 0)

    spec = lambda off: pl.BlockSpec((None, SEQ, LANES), lambda b, j: (b, 0, off + j))
    ospec = pl.BlockSpec((None, SEQ, LANES), lambda b, j: (b, 0, j))
    tspec = pl.BlockSpec((SEQ, LANES), lambda b, j: (0, 0))
    osd = jax.ShapeDtypeStruct((nb, SEQ, DIL_W), BF16)
    return pl.pallas_call(
        body, out_shape=(osd, osd, osd), grid=(nb, ncol),
        in_specs=[spec(0), spec(ncol), spec(2 * ncol), spec(do_off), ospec, ospec, tspec, tspec, tspec],
        out_specs=(ospec, ospec, ospec),
        scratch_shapes=[pltpu.VMEM((SEQ, LANES), F32)] * 5,
        compiler_params=_cparams(dimension_semantics=("parallel", "parallel")), name=name,
    )(qkv, qkv, qkv, do, o, lse, *tabs)


FOX_GROUP = 4
assert NBLK % FOX_GROUP == 0
_FOX_COLS = tuple(c // LANES for c in (C_FQ, C_FK, C_FV))


def _fox_specs():
    cols = [pl.BlockSpec((None, SEQ, LANES), (lambda b, j, off=off: (b, 0, off + j))) for off in _FOX_COLS]
    ospec = pl.BlockSpec((None, SEQ, LANES), lambda b, j: (b, 0, j))
    crspec = pl.BlockSpec((None, None, NBLK, 8, BLK), lambda b, j: (b, j, 0, 0, 0))
    return cols, ospec, crspec


def _fox_key_rows(t, e):
    return pl.ds(pl.multiple_of((FOX_GROUP * t + e) * BLK, BLK), BLK)


def _fox_fwd(p3, crow, *, name):
    nb = p3.shape[0]
    g = FOX_GROUP

    def body(q_ref, k_ref, v_ref, cr_ref, o_ref, lse_ref):
        masks = _head_masks(2)
        tri = _tri_bias(True)

        def qk(qcat, t):
            return tuple(lax.dot_general(qcat, k_ref[_fox_key_rows(t, e), :], _NT, preferred_element_type=F32) for e in range(g))

        def consume(ss, t, state, nblk, diag):
            m0, m1, l0, l1, acc = state
            us = []
            for e in range(nblk):
                cr = cr_ref[g * t + e]
                u0 = ss[e][:BLK] - cr[0:1, :]
                u1 = ss[e][BLK:] - cr[1:2, :]
                if diag and e == nblk - 1:
                    u0, u1 = u0 + tri, u1 + tri
                us.append((u0, u1))
            x0 = functools.reduce(jnp.maximum, [u[0] for u in us])
            x1 = functools.reduce(jnp.maximum, [u[1] for u in us])
            n0 = jnp.maximum(m0, jnp.max(x0, axis=1, keepdims=True))
            n1 = jnp.maximum(m1, jnp.max(x1, axis=1, keepdims=True))
            a0, a1 = jnp.exp(m0 - n0), jnp.exp(m1 - n1)
            acc = acc * jnp.where(masks[0], a0, a1)
            t0 = t1 = None
            for e in range(nblk):
                p0, p1 = jnp.exp(us[e][0] - n0), jnp.exp(us[e][1] - n1)
                t0 = p0 if t0 is None else t0 + p0
                t1 = p1 if t1 is None else t1 + p1
                pcat = jnp.concatenate([p0, p1], axis=1)
                hi = pcat.astype(BF16)
                lo = (pcat - hi.astype(F32)).astype(BF16)
                vcat = _stack_heads(v_ref[_fox_key_rows(t, e), :], masks)
                acc = acc + jnp.dot(hi, vcat, preferred_element_type=F32) + jnp.dot(lo, vcat, preferred_element_type=F32)
            l0 = a0 * l0 + jnp.sum(t0, axis=1, keepdims=True)
            l1 = a1 * l1 + jnp.sum(t1, axis=1, keepdims=True)
            return n0, n1, l0, l1, acc

        def gbody(ng, c):
            neg = jnp.full((BLK, 1), NEG_INF, F32)
            z1 = jnp.zeros((BLK, 1), F32)
            rows = [pl.ds(pl.multiple_of((g * ng + a) * BLK, BLK), BLK) for a in range(g)]
            qcats = [_stack_heads(q_ref[rows[a], :] * QK_SCALE, masks) for a in range(g)]
            first = [qk(qcats[a], 0) for a in range(g)]
            done = []
            for a in range(g):
                def step(t, cc, qcat=qcats[a]):
                    ss, st = cc
                    nxt = qk(qcat, t + 1)
                    return nxt, consume(ss, t, st, g, False)

                done.append(lax.fori_loop(0, ng, step, (first[a], (neg, neg, z1, z1, jnp.zeros((BLK, LANES), F32)))))
            for a in range(g):
                ss, state = done[a]
                m0, m1, l0, l1, acc = consume(ss, ng, state, a + 1, True)
                o_ref[rows[a], :] = acc / jnp.where(masks[0], l0, l1)
                lse_ref[rows[a], :] = jnp.where(masks[0], m0 + jnp.log(l0), m1 + jnp.log(l1))
            return c

        lax.fori_loop(0, NBLK // g, gbody, 0)

    cols, ospec, crspec = _fox_specs()
    osd = jax.ShapeDtypeStruct((nb, SEQ, FOX_W), F32)
    return pl.pallas_call(
        body, out_shape=(osd, osd), grid=(nb, FOX_W // LANES), in_specs=cols + [crspec], out_specs=(ospec, ospec),
        compiler_params=_cparams(dimension_semantics=("parallel", "parallel")), name=name,
    )(p3, p3, p3, crow)


def _fox_bwd(p3, crow, do, o, lse, *, do_off, name):
    nb = p3.shape[0]
    g = FOX_GROUP
    hd = HEAD_DIM

    def body(q_ref, k_ref, v_ref, cr_ref, do_ref, o_ref, lse_ref, dq_ref, dko_ref, dvo_ref, dcr_ref, dk_ref, dv_ref):
        masks = _head_masks(2)
        tri = _tri_bias(True)
        dk_ref[...] = jnp.zeros_like(dk_ref)
        dv_ref[...] = jnp.zeros_like(dv_ref)
        dcr_ref[...] = jnp.zeros_like(dcr_ref)

        def products(qcat, docat, t):
            out = []
            for e in range(g):
                krows = _fox_key_rows(t, e)
                out.append(lax.dot_general(qcat, k_ref[krows, :], _NT, preferred_element_type=F32))
                out.append(lax.dot_general(docat, v_ref[krows, :], _NT, preferred_element_type=F32))
            return tuple(out)

        def consume(prod, t, ctx, dq, nblk, diag):
            qcat, docat, lse0, lse1, dl0, dl1 = ctx
            for e in range(nblk):
                jb = g * t + e
                krows = _fox_key_rows(t, e)
                s, dp = prod[2 * e], prod[2 * e + 1]
                cr = cr_ref[jb]
                u0 = s[:BLK] - cr[0:1, :]
                u1 = s[BLK:] - cr[1:2, :]
                if diag and e == nblk - 1:
                    u0, u1 = u0 + tri, u1 + tri
                p0 = jnp.exp(u0 - lse0)
                p1 = jnp.exp(u1 - lse1)
                ds0 = p0 * (dp[:BLK] - dl0)
                ds1 = p1 * (dp[BLK:] - dl1)
                dcr_ref[jb, 0:1, :] += jnp.sum(ds0, axis=0, keepdims=True)
                dcr_ref[jb, 1:2, :] += jnp.sum(ds1, axis=0, keepdims=True)
                ds0b, ds1b = ds0.astype(BF16), ds1.astype(BF16)
                pcat = jnp.concatenate([p0.astype(BF16), p1.astype(BF16)], axis=0)
                dscat = jnp.concatenate([ds0b, ds1b], axis=0)
                dv_ref[krows, :] += lax.dot_general(pcat, docat, _T0, preferred_element_type=F32)
                dk_ref[krows, :] += lax.dot_general(dscat, qcat, _T0, preferred_element_type=F32)
                dsrow = jnp.concatenate([ds0b, ds1b], axis=1)
                dq = dq + jnp.dot(dsrow, _stack_heads(k_ref[krows, :] * QK_SCALE, masks), preferred_element_type=F32)
            return dq

        def gbody(ng, c):
            ctxs, rows = [], []
            for a in range(g):
                r = pl.ds(pl.multiple_of((g * ng + a) * BLK, BLK), BLK)
                qcat = _stack_heads(q_ref[r, :] * QK_SCALE, masks)
                dob = do_ref[r, :].astype(BF16)
                prod = dob.astype(F32) * o_ref[r, :]
                z = jnp.zeros_like(prod)
                dl0 = jnp.sum(jnp.where(masks[0], prod, z), axis=1, keepdims=True)
                dl1 = jnp.sum(jnp.where(masks[1], prod, z), axis=1, keepdims=True)
                lseb = lse_ref[r, :]
                ctxs.append((qcat, _stack_heads(dob, masks), lseb[:, 0:1], lseb[:, hd:hd + 1], dl0, dl1))
                rows.append(r)
            first = [products(ctxs[a][0], ctxs[a][1], 0) for a in range(g)]
            done = []
            for a in range(g):
                def step(t, cc, ctx=ctxs[a]):
                    pr, dq = cc
                    nxt = products(ctx[0], ctx[1], t + 1)
                    return nxt, consume(pr, t, ctx, dq, g, False)

                done.append(lax.fori_loop(0, ng, step, (first[a], jnp.zeros((BLK, LANES), F32))))
            for a in range(g):
                pr, dq = done[a]
                dq_ref[rows[a], :] = consume(pr, ng, ctxs[a], dq, a + 1, True).astype(dq_ref.dtype)
            return c

        lax.fori_loop(0, NBLK // g, gbody, 0)
        dko_ref[...] = dk_ref[...].astype(dko_ref.dtype)
        dvo_ref[...] = dv_ref[...].astype(dvo_ref.dtype)

    cols, ospec, crspec = _fox_specs()
    dospec = pl.BlockSpec((None, SEQ, LANES), lambda b, j: (b, 0, do_off + j))
    osd = jax.ShapeDtypeStruct((nb, SEQ, FOX_W), BF16)
    return pl.pallas_call(
        body, out_shape=(osd, osd, osd, jax.ShapeDtypeStruct((nb, FOX_W // LANES, NBLK, 8, BLK), F32)),
        grid=(nb, FOX_W // LANES), in_specs=cols + [crspec, dospec, ospec, ospec], out_specs=(ospec, ospec, ospec, crspec),
        scratch_shapes=[pltpu.VMEM((SEQ, LANES), F32)] * 2,
        compiler_params=_cparams(dimension_semantics=("parallel", "parallel")), name=name,
    )(p3, p3, p3, crow, do, o, lse)


_B1, _B2 = FOX_W // LANES, (FOX_W + DIL_W) // LANES


def _dy_gate_bwd(dx2b, wo, fox, dil, memo, p16, *, tm, tn, name):
    t, d = dx2b.shape
    assert FOX_W % tn == 0 and DIL_W % tn == 0 and MEM_W % tn == 0 and all(c % tn == 0 for c in (C_FG, C_DG, C_MG))
    n1, n2, n3 = FOX_W // tn, (FOX_W + DIL_W) // tn, MIX_W // tn

    def body(dx_ref, w_ref, f_ref, d_ref, m_ref, g_ref, da_ref, dg_ref):
        j = pl.program_id(1)
        dyv = lax.dot_general(dx_ref[...], w_ref[...], _NT, preferred_element_type=F32)
        a = jnp.where(j < n1, f_ref[...], jnp.where(j < n2, d_ref[...], m_ref[...]))
        gt = g_ref[...].astype(F32)
        sg = 1.0 / (1.0 + jnp.exp(-gt))
        da_ref[...] = (dyv * gt * sg).astype(da_ref.dtype)
        dg_ref[...] = (dyv * a * sg * (1.0 + gt * (1.0 - sg))).astype(dg_ref.dtype)

    def gcol(j):
        return jnp.where(j < n1, C_FG // tn + j, jnp.where(j < n2, C_DG // tn + j - n1, C_MG // tn + j - n2))

    tile = pl.BlockSpec((tm, tn), lambda i, j: (i, j))
    return pl.pallas_call(
        body,
        out_shape=(jax.ShapeDtypeStruct((t, MIX_W), BF16), jax.ShapeDtypeStruct((t, MIX_W), BF16)),
        grid=(t // tm, n3),
        in_specs=[pl.BlockSpec((tm, d), lambda i, j: (i, 0)), pl.BlockSpec((tn, d), lambda i, j: (j, 0)),
                  pl.BlockSpec((tm, tn), lambda i, j: (i, jnp.minimum(j, n1 - 1))),
                  pl.BlockSpec((tm, tn), lambda i, j: (i, jnp.clip(j - n1, 0, n2 - n1 - 1))),
                  pl.BlockSpec((tm, tn), lambda i, j: (i, jnp.clip(j - n2, 0, n3 - n2 - 1))),
                  pl.BlockSpec((tm, tn), lambda i, j: (i, gcol(j)))],
        out_specs=(tile, tile),
        compiler_params=_cparams(dimension_semantics=("parallel", "parallel")),
        name=name,
    )(dx2b, wo, fox, dil, memo, p16)


def _silu(g):
    return g / (1.0 + jnp.exp(-g))


def _out_loss(fox, dil, memo, p16, wo, x, tgt, gfin, *, tm, name):
    t, d = x.shape
    n_feat = float(d)

    def body(f_ref, d_ref, m_ref, fg_ref, dg_ref, mg_ref, w_ref, x_ref, t_ref, g_ref, y_ref, dx_ref, dxb_ref, st_ref):
        i = pl.program_id(0)

        @pl.when(i == 0)
        def _():
            st_ref[...] = jnp.zeros_like(st_ref)

        y = jnp.concatenate([(a_ref[...] * _silu(gt_ref[...].astype(F32))).astype(BF16)
                             for a_ref, gt_ref in ((f_ref, fg_ref), (d_ref, dg_ref), (m_ref, mg_ref))], axis=1)
        y_ref[...] = y
        x2 = x_ref[...] + jnp.dot(y, w_ref[...], preferred_element_type=F32)
        r = lax.rsqrt(jnp.mean(x2 * x2, axis=-1, keepdims=True) + RMS_EPS)
        nrm = x2 * r
        gv = g_ref[...]
        err = nrm * gv - t_ref[...]
        dout = err * (1.0 / n_feat)
        dn = dout * gv
        dx2 = r * (dn - nrm * jnp.mean(dn * nrm, axis=-1, keepdims=True))
        dx_ref[...] = dx2
        dxb_ref[...] = dx2.astype(dxb_ref.dtype)
        st_ref[0:1, :] += jnp.sum(dout * nrm, axis=0, keepdims=True)
        st_ref[1:2, :] += (0.5 / n_feat) * jnp.sum(err * err, axis=0, keepdims=True)

    row = pl.BlockSpec((tm, d), lambda i: (i, 0))
    whole = lambda w: pl.BlockSpec((tm, w), lambda i: (i, 0))
    gate = lambda w, col: pl.BlockSpec((tm, w), lambda i: (i, col // w))
    return pl.pallas_call(
        body,
        out_shape=(jax.ShapeDtypeStruct((t, MIX_W), BF16), jax.ShapeDtypeStruct((t, d), F32), jax.ShapeDtypeStruct((t, d), BF16),
                   jax.ShapeDtypeStruct((8, d), F32)),
        grid=(t // tm,),
        in_specs=[whole(FOX_W), whole(DIL_W), whole(MEM_W), gate(FOX_W, C_FG), gate(DIL_W, C_DG), gate(MEM_W, C_MG),
                  pl.BlockSpec((MIX_W, d), lambda i: (0, 0)), row, row, pl.BlockSpec((1, d), lambda i: (0, 0))],
        out_specs=(pl.BlockSpec((tm, MIX_W), lambda i: (i, 0)), row, row, pl.BlockSpec((8, d), lambda i: (0, 0))),
        compiler_params=_cparams(dimension_semantics=("arbitrary",)),
        name=name,
    )(fox, dil, memo, p16, p16, p16, wo, x, tgt, gfin)


def _dh_rms_bwd(dp, w, x, g, resid, *, tm, name):
    t, d = x.shape
    kdim = dp.shape[1]

    def body(*refs):
        if resid is not None:
            dp_ref, w_ref, x_ref, g_ref, r_ref, dx_ref, gg_ref = refs
        else:
            dp_ref, w_ref, x_ref, g_ref, dx_ref, gg_ref = refs

        @pl.when(pl.program_id(0) == 0)
        def _():
            gg_ref[...] = jnp.zeros_like(gg_ref)

        dh = lax.dot_general(dp_ref[...], w_ref[...], _NT, preferred_element_type=F32)
        xv = x_ref[...]
        r = lax.rsqrt(jnp.mean(xv * xv, axis=-1, keepdims=True) + RMS_EPS)
        nrm = xv * r
        dn = dh * g_ref[...]
        dx = r * (dn - nrm * jnp.mean(dn * nrm, axis=-1, keepdims=True))
        if resid is not None:
            dx = dx + r_ref[...]
        dx_ref[...] = dx
        gg_ref[0:1, :] += jnp.sum(dh * nrm, axis=0, keepdims=True)

    row = pl.BlockSpec((tm, d), lambda i: (i, 0))
    in_specs = [pl.BlockSpec((tm, kdim), lambda i: (i, 0)),
                pl.BlockSpec((d, kdim), lambda i: (0, 0), pipeline_mode=pl.Buffered(1)), row,
                pl.BlockSpec((1, d), lambda i: (0, 0))]
    args = [dp, w, x, g]
    if resid is not None:
        in_specs.append(row)
        args.append(resid)
    return pl.pallas_call(
        body,
        out_shape=(jax.ShapeDtypeStruct((t, d), F32), jax.ShapeDtypeStruct((8, d), F32)),
        grid=(t // tm,),
        in_specs=in_specs,
        out_specs=(row, pl.BlockSpec((8, d), lambda i: (0, 0))),
        compiler_params=_cparams(dimension_semantics=("arbitrary",)),
        name=name,
    )(*args)


_FLOG0 = 4 * FOX_W
_W_IN_SEGMENTS = ((0, _FLOG0, 0), (_FLOG0, _FLOG0 + FOX_HEADS, PW), (_FLOG0 + FOX_HEADS, IN_W, C_DQ))
SHARD_W = IN_W // N_CHIPS


def _rearrange_w_in(shards):
    def cols(lo, hi):
        parts = []
        for k in range(N_CHIPS):
            a, b = max(lo, k * SHARD_W), min(hi, (k + 1) * SHARD_W)
            if a < b:
                parts.append(shards[k][:, a - k * SHARD_W:b - k * SHARD_W])
        return parts

    (a0, a1, _), (f0, f1, _), (b0, b1, _) = _W_IN_SEGMENTS
    pad = jnp.zeros((shards[0].shape[0], PWF - PW - FOX_HEADS), shards[0].dtype)
    return jnp.concatenate(cols(a0, a1) + cols(b0, b1) + cols(f0, f1) + [pad], axis=1)


def _w_in_grad_slabs(g):
    slabs = []
    for k in range(N_CHIPS):
        parts = []
        for lo, hi, at in _W_IN_SEGMENTS:
            a, b = max(lo, k * SHARD_W), min(hi, (k + 1) * SHARD_W)
            if a < b:
                parts.append(g[:, at + a - lo:at + b - lo])
        slabs.append(jnp.concatenate(parts, axis=1))
    return jnp.stack(slabs, axis=0)


def _local_grads(x, mem, norm_g, w_r, b_forget, mem_norm_g, w_kv, w_o, final_norm_g, tgt, start_reduce=None,
                 start_reduce_small=None, early_token=None, late_weights=None):
    nb = x.shape[0]
    t = nb * SEQ
    x2d = x.reshape(t, D_MODEL)
    tgt2d = tgt.reshape(t, D_MODEL)
    tabs = _rope_tables()
    bpad = jnp.pad(b_forget.reshape(1, FOX_HEADS), ((0, 0), (0, LANES - FOX_HEADS)))

    gain0 = norm_g.reshape(1, D_MODEL)
    if early_token is not None:
        gain0 = gain0 + early_token[0:1, 0:1]
    h = _rms_fwd(x2d, gain0, tm=512, name="rms_x")
    p16, dqkv = _proj(h, w_r, tabs, n=PWF, tm=1024, tn=768, name="proj")
    flog = _matmul(h, w_r[:, PW:PW + LANES], out_dtype=F32, tm=1024, tn=LANES, tk=D_MODEL, name="proj_flog")
    c12 = _flog_fwd(flog, bpad, nb=nb, ts=256, name="flog_fwd")

    crow = c12[:, :FOX_HEADS].reshape(nb, NBLK, BLK, FOX_HEADS // 2, 2).transpose(0, 3, 1, 4, 2)
    crow = jnp.pad(crow, ((0, 0), (0, 0), (0, 0), (0, 6), (0, 0)))
    p3 = p16.reshape(nb, SEQ, PWF)
    fox, fox_lse = _fox_fwd(p3, crow, name="fox_fwd")
    if late_weights is not None:
        w_kv, w_o = late_weights(fox_lse)

    dqkv3 = dqkv.reshape(nb, SEQ, 3 * DIL_W)
    dil, dil_lse = _dil_fwd(dqkv3, name="dil_fwd")

    mh = _rms_fwd(mem.reshape(nb * MEM_LEN, D_MODEL), mem_norm_g.reshape(1, D_MODEL), tm=nb * MEM_LEN, name="rms_mem")
    mkv = _matmul(mh, w_kv, out_dtype=BF16, tm=nb * MEM_LEN, tn=512, tk=D_MODEL, name="mem_kv")
    mkv3 = mkv.reshape(nb, MEM_LEN, 2 * MEM_W)
    memo, mem_lse = _mem_fwd(p3, mkv3, qoff=C_MQ // LANES, name="mem_fwd")

    fox2, dil2, memo2 = fox.reshape(t, FOX_W), dil.reshape(t, DIL_W), memo.reshape(t, MEM_W)
    y, dx2, dx2b, st = _out_loss(fox2, dil2, memo2, p16, w_o, x2d, tgt2d, final_norm_g.reshape(1, D_MODEL), tm=256,
                                 name="out_loss")

    g_wo = _matmul(y, dx2b, mode="tn", out_dtype=BF16, tm=1024, tn=512, tk=t, name="grad_w_out")
    datt, dgate = _dy_gate_bwd(dx2b, w_o, fox2, dil2, memo2, p16, tm=2048, tn=256, name="dy_gate_bwd")
    datt3 = datt.reshape(nb, SEQ, MIX_W)

    dmq, dmk, dmv = _mem_bwd(p3, mkv3, datt3, memo, mem_lse, qoff=C_MQ // LANES, do_off=_B2, name="mem_bwd")
    dmkv = jnp.concatenate([dmk, dmv], axis=-1).reshape(nb * MEM_LEN, 2 * MEM_W).astype(BF16)
    g_wkv = _matmul(mh, dmkv, mode="tn", out_dtype=BF16, tm=512, tn=512, tk=nb * MEM_LEN, name="grad_w_kv")
    mem_gain = mem_norm_g.reshape(1, D_MODEL)
    if start_reduce_small is not None:
        tok = start_reduce_small(g_wkv, g_wo)[0:1, 0:1]
        mem_gain, crow = mem_gain + tok, crow + tok
    _, gmn = _dh_rms_bwd(dmkv, w_kv, mem.reshape(nb * MEM_LEN, D_MODEL), mem_gain, None, tm=nb * MEM_LEN, name="mem_rms_bwd")

    dfq, dfk, dfv, dcr = _fox_bwd(p3, crow, datt3, fox, fox_lse, do_off=0, name="fox_bwd")
    dcol = -dcr[:, :, :, :2, :].transpose(0, 2, 4, 1, 3).reshape(t, FOX_HEADS)
    dcol = jnp.pad(dcol, ((0, 0), (0, LANES - FOX_HEADS)))
    dflog, gb = _flog_bwd(dcol, flog, bpad, nb=nb, ts=256, name="flog_bwd")

    ddq, ddk, ddv = _dil_bwd(dqkv3, datt3, dil, dil_lse, tabs, do_off=_B1, name="dil_bwd")

    flat = lambda a: a.reshape(t, -1)
    dp = jnp.concatenate([flat(dfq), flat(dfk), flat(dfv), dgate[:, :FOX_W], flat(ddq), flat(ddk), flat(ddv),
                          dgate[:, FOX_W:FOX_W + DIL_W], flat(dmq), dgate[:, FOX_W + DIL_W:], dflog,
                          jnp.zeros((t, PWF - PW - LANES), BF16)], axis=1)
    g_wr = _matmul(h, dp, mode="tn", out_dtype=BF16, tm=D_MODEL, tn=512, tk=t, name="grad_w_in")
    gain = norm_g.reshape(1, D_MODEL)
    if start_reduce is not None:
        gain = gain + start_reduce(g_wr)[0:1, 0:1]
    gx, gng = _dh_rms_bwd(dp, w_r, x2d, gain, dx2, tm=256, name="in_rms_bwd")

    gb_row = jnp.pad(gb[0:1, :], ((0, 0), (0, D_MODEL - LANES)))
    small = jnp.concatenate([gng[0:1], gmn[0:1], st[0:1], gb_row, st[1:2], jnp.zeros((3, D_MODEL), F32)], axis=0)
    return gx.reshape(nb, SEQ, D_MODEL), g_wr, g_wkv, g_wo, small


MESH = pl.DeviceIdType.MESH
ANY = pl.BlockSpec(memory_space=pl.ANY)


def _place():
    x, y, c = lax.axis_index("x"), lax.axis_index("y"), lax.axis_index("c")
    other_chips = [(1 - x, y), (x, 1 - y), (1 - x, 1 - y)]
    return x, y, c, other_chips


def _gather_weights(shards):
    n = len(shards)

    def body(*refs):
        in_refs, out_refs = refs[:n], refs[n:2 * n]
        send_sems, recv_sems = refs[2 * n:]
        x, y, c, chips = _place()
        me_chip = 2 * x + y
        sibling = (x, y, 1 - c)

        def half(ref, pc, rows):
            return ref.at[pl.ds(pc * (rows // 2), rows // 2), :]

        def rcopy(k, src, dst, to):
            return pltpu.make_async_remote_copy(src_ref=src, dst_ref=dst, send_sem=send_sems.at[k], recv_sem=recv_sems.at[k],
                                                device_id=to, device_id_type=MESH)

        sends = []
        for t in range(n):
            rows = shards[t].shape[0]
            for j, chip in enumerate(chips):
                cp = rcopy(6 * t + j, half(in_refs[t], c, rows), half(out_refs[t].at[me_chip], c, rows), (*chip, c))
                cp.start()
                sends.append(cp)
        for t in range(n):
            rows = shards[t].shape[0]
            for j, chip in enumerate(chips):
                slot = out_refs[t].at[2 * chip[0] + chip[1]]
                rcopy(6 * t + j, half(slot, c, rows), half(slot, c, rows), sibling).wait_recv()
                fw = rcopy(6 * t + 3 + j, half(slot, c, rows), half(slot, c, rows), sibling)
                fw.start()
                sends.append(fw)
        for t in range(n):
            rows = shards[t].shape[0]
            for j, chip in enumerate(chips):
                slot = out_refs[t].at[2 * chip[0] + chip[1]]
                rcopy(6 * t + 3 + j, half(slot, 1 - c, rows), half(slot, 1 - c, rows), sibling).wait_recv()
        for cp in sends:
            cp.wait_send()

    return pl.pallas_call(
        body,
        out_shape=tuple(jax.ShapeDtypeStruct((N_CHIPS,) + s.shape, s.dtype) for s in shards),
        in_specs=[ANY] * n,
        out_specs=tuple([ANY] * n),
        scratch_shapes=[pltpu.SemaphoreType.DMA((6 * n,)), pltpu.SemaphoreType.DMA((6 * n,))],
        name="gather_weights",
    )(*shards)


def _pair_exchange(gs, *, name):
    n = len(gs)

    def body(*refs):
        g_refs, r_refs = refs[:n], refs[n:2 * n]
        send_sems, recv_sems = refs[2 * n:]
        x, y, c, _ = _place()
        cps = []
        for t in range(n):
            hr = gs[t].shape[1] // 2
            cp = pltpu.make_async_remote_copy(src_ref=g_refs[t].at[:, pl.ds((1 - c) * hr, hr), :], dst_ref=r_refs[t],
                                              send_sem=send_sems.at[t], recv_sem=recv_sems.at[t],
                                              device_id=(x, y, 1 - c), device_id_type=MESH)
            cp.start()
            cps.append(cp)
        for cp in cps:
            cp.wait()

    return pl.pallas_call(
        body,
        out_shape=tuple(jax.ShapeDtypeStruct((g.shape[0], g.shape[1] // 2, g.shape[2]), g.dtype) for g in gs),
        in_specs=[ANY] * n,
        out_specs=tuple([ANY] * n),
        scratch_shapes=[pltpu.SemaphoreType.DMA((n,)), pltpu.SemaphoreType.DMA((n,))],
        name=name,
    )(*gs)


_HBM = pl.BlockSpec(memory_space=pltpu.HBM)
_SEM = pl.BlockSpec(memory_space=pltpu.SEMAPHORE)
_DATAFLOW = pltpu.SideEffectType.DATAFLOW_SIDE_EFFECTING


def _chip_copies(p_refs, land_refs, send_sems, recv_sems):
    x, y, c, chips = _place()
    me_chip = 2 * x + y
    return [pltpu.make_async_remote_copy(src_ref=p_refs[t].at[2 * chip[0] + chip[1]], dst_ref=land_refs[t].at[me_chip],
                                         send_sem=send_sems.at[3 * t + j], recv_sem=recv_sems.at[3 * t + j],
                                         device_id=(*chip, c), device_id_type=MESH)
            for t in range(len(p_refs)) for j, chip in enumerate(chips)]


def _chip_exchange_start(ps, *, tag):
    n = len(ps)

    def body(*refs):
        p_refs, land_refs = refs[:n], refs[n:2 * n]
        send_sems, recv_sems = refs[2 * n:2 * n + 2]
        token = refs[-1]
        for cp in _chip_copies(p_refs, land_refs, send_sems, recv_sems):
            cp.start()
        token[...] = jnp.zeros_like(token)

    hbm = [pltpu.HBM(p.shape, p.dtype) for p in ps]
    args = [pltpu.with_memory_space_constraint(p, pltpu.HBM) for p in ps]
    args += [pltpu.with_memory_space_constraint(lax.empty(p.shape, p.dtype), pltpu.HBM) for p in ps]
    out = pl.pallas_call(
        body,
        name=f"chip_exchange_start_{tag}",
        out_shape=(pltpu.SemaphoreType.DMA((3 * n,)), pltpu.SemaphoreType.DMA((3 * n,)), *hbm, *hbm,
                   jax.ShapeDtypeStruct((8, LANES), F32)),
        in_specs=[_HBM] * (2 * n),
        out_specs=(_SEM, _SEM, *([_HBM] * (2 * n)), pl.BlockSpec(memory_space=pltpu.VMEM)),
        input_output_aliases={i: 2 + i for i in range(2 * n)},
        compiler_params=pltpu.CompilerParams(has_side_effects=_DATAFLOW),
    )(*args)
    return out[0], out[1], out[2:2 + n], out[2 + n:2 + 2 * n], out[-1]


def _chip_exchange_wait(send_sems, recv_sems, p_thru, land_thru, after, *, tag):
    n = len(p_thru)

    def body(*refs):
        p_refs, land_refs = refs[:n], refs[n:2 * n]
        ssem, rsem = refs[2 * n:2 * n + 2]
        for cp in _chip_copies(p_refs, land_refs, ssem, rsem):
            cp.wait_send()
            cp.wait_recv()

    hbm = [pltpu.HBM(p.shape, p.dtype) for p in p_thru]
    out = pl.pallas_call(
        body,
        name=f"chip_exchange_wait_{tag}",
        out_shape=(*hbm, *hbm),
        in_specs=[_HBM] * (2 * n) + [_SEM, _SEM, ANY],
        out_specs=tuple([_HBM] * (2 * n)),
        input_output_aliases={i: i for i in range(2 * n)},
        compiler_params=pltpu.CompilerParams(has_side_effects=_DATAFLOW),
    )(*p_thru, *land_thru, send_sems, recv_sems, after)
    return out[:n], out[n:]


def _shard_copies(s_refs, land_refs, send_sems, recv_sems):
    x, y, c, chips = _place()
    me_chip = 2 * x + y
    return [pltpu.make_async_remote_copy(src_ref=s_refs[t], dst_ref=land_refs[t].at[me_chip],
                                         send_sem=send_sems.at[3 * t + j], recv_sem=recv_sems.at[3 * t + j],
                                         device_id=(*chip, c), device_id_type=MESH)
            for t in range(len(s_refs)) for j, chip in enumerate(chips)]


def _gather_late_start(shards):
    n = len(shards)

    def body(*refs):
        s_refs, land_refs = refs[:n], refs[n:2 * n]
        send_sems, recv_sems = refs[2 * n:2 * n + 2]
        token = refs[-1]
        for cp in _shard_copies(s_refs, land_refs, send_sems, recv_sems):
            cp.start()
        token[...] = jnp.zeros_like(token)

    lands = [(N_CHIPS,) + s.shape for s in shards]
    args = [pltpu.with_memory_space_constraint(s, pltpu.HBM) for s in shards]
    args += [pltpu.with_memory_space_constraint(lax.empty(shp, s.dtype), pltpu.HBM) for shp, s in zip(lands, shards)]
    out = pl.pallas_call(
        body,
        name="gather_late_start",
        out_shape=(pltpu.SemaphoreType.DMA((3 * n,)), pltpu.SemaphoreType.DMA((3 * n,)),
                   *[pltpu.HBM(s.shape, s.dtype) for s in shards], *[pltpu.HBM(shp, s.dtype) for shp, s in zip(lands, shards)],
                   jax.ShapeDtypeStruct((8, LANES), F32)),
        in_specs=[_HBM] * (2 * n),
        out_specs=(_SEM, _SEM, *([_HBM] * (2 * n)), pl.BlockSpec(memory_space=pltpu.VMEM)),
        input_output_aliases={i: 2 + i for i in range(2 * n)},
        compiler_params=pltpu.CompilerParams(has_side_effects=_DATAFLOW),
    )(*args)
    return out[0], out[1], out[2:2 + n], out[2 + n:2 + 2 * n], out[-1]


def _gather_late_wait(send_sems, recv_sems, s_thru, land_thru, after):
    n = len(s_thru)

    def body(*refs):
        s_refs, land_refs = refs[:n], refs[n:2 * n]
        ssem, rsem = refs[2 * n:2 * n + 2]
        for cp in _shard_copies(s_refs, land_refs, ssem, rsem):
            cp.wait_send()
            cp.wait_recv()

    out = pl.pallas_call(
        body,
        name="gather_late_wait",
        out_shape=(*[pltpu.HBM(s.shape, s.dtype) for s in s_thru], *[pltpu.HBM(l.shape, l.dtype) for l in land_thru]),
        in_specs=[_HBM] * (2 * n) + [_SEM, _SEM, ANY],
        out_specs=tuple([_HBM] * (2 * n)),
        input_output_aliases={i: i for i in range(2 * n)},
        compiler_params=pltpu.CompilerParams(has_side_effects=_DATAFLOW),
    )(*s_thru, *land_thru, send_sems, recv_sems, after)
    return out[:n], out[n:]


def _pair_swap(rs):
    n = len(rs)

    def body(*refs):
        r_refs, o_refs = refs[:n], refs[n:2 * n]
        send_sems, recv_sems = refs[2 * n:]
        x, y, c, _ = _place()
        cps = []
        for t in range(n):
            cp = pltpu.make_async_remote_copy(src_ref=r_refs[t], dst_ref=o_refs[t], send_sem=send_sems.at[t],
                                              recv_sem=recv_sems.at[t], device_id=(x, y, 1 - c), device_id_type=MESH)
            cp.start()
            cps.append(cp)
        for cp in cps:
            cp.wait()

    return pl.pallas_call(
        body,
        out_shape=tuple(jax.ShapeDtypeStruct(r.shape, r.dtype) for r in rs),
        in_specs=[ANY] * n,
        out_specs=tuple([ANY] * n),
        scratch_shapes=[pltpu.SemaphoreType.DMA((n,)), pltpu.SemaphoreType.DMA((n,))],
        name="pair_swap",
    )(*rs)


N_DEV = 8
LOSS_ROW = 4


def _small_allreduce(small):
    def body(s_ref, o_ref, all_ref, send_sems, recv_sems):
        x, y, c, _ = _place()
        me = 4 * x + 2 * y + c
        all_ref[me] = s_ref[...]
        cps = []
        for k in range(1, N_DEV):
            peer = tuple(1 - p if (k >> s) & 1 else p for p, s in ((x, 2), (y, 1), (c, 0)))
            cp = pltpu.make_async_remote_copy(src_ref=s_ref, dst_ref=all_ref.at[me], send_sem=send_sems.at[k - 1],
                                              recv_sem=recv_sems.at[k - 1], device_id=peer, device_id_type=MESH)
            cp.start()
            cps.append(cp)
        for cp in cps:
            cp.wait()
        tot = all_ref[0]
        for d in range(1, N_DEV):
            tot = tot + all_ref[d]
        o_ref[...] = tot
        o_ref[LOSS_ROW:LOSS_ROW + 1, :] = jnp.broadcast_to(jnp.sum(tot[LOSS_ROW:LOSS_ROW + 1, :], axis=1, keepdims=True),
                                                          (1, tot.shape[1]))

    vm = pl.BlockSpec(memory_space=pltpu.VMEM)
    return pl.pallas_call(
        body,
        out_shape=jax.ShapeDtypeStruct(small.shape, small.dtype),
        in_specs=[vm],
        out_specs=vm,
        scratch_shapes=[pltpu.VMEM((N_DEV,) + small.shape, small.dtype), pltpu.SemaphoreType.DMA((N_DEV - 1,)),
                        pltpu.SemaphoreType.DMA((N_DEV - 1,))],
        name="small_allreduce",
    )(small)


def _sum_pair(g, recv, cidx, *, tr, name):
    n, hr, cols = recv.shape
    nr = hr // tr

    def body(c_ref, g_ref, r_ref, o_ref):
        o_ref[...] = (g_ref[...].astype(F32) + r_ref[...].astype(F32)).astype(o_ref.dtype)

    grid_spec = pltpu.PrefetchScalarGridSpec(
        num_scalar_prefetch=1,
        grid=(n, nr),
        in_specs=[pl.BlockSpec((None, tr, cols), lambda k, i, c_ref: (k, c_ref[0] * nr + i, 0)),
                  pl.BlockSpec((None, tr, cols), lambda k, i, c_ref: (k, i, 0))],
        out_specs=pl.BlockSpec((None, tr, cols), lambda k, i, c_ref: (k, i, 0)),
    )
    return pl.pallas_call(body, out_shape=jax.ShapeDtypeStruct(recv.shape, BF16), grid_spec=grid_spec,
                          compiler_params=_cparams(), name=name)(cidx, g, recv)


def _sum_chips(p, *, tr, name):
    _, rows, cols = p.shape

    def body(p_ref, o_ref):
        tot = p_ref[0].astype(F32)
        for k in range(1, N_CHIPS):
            tot = tot + p_ref[k].astype(F32)
        o_ref[...] = tot

    return pl.pallas_call(
        body,
        out_shape=jax.ShapeDtypeStruct((rows, cols), F32),
        grid=(rows // tr,),
        in_specs=[pl.BlockSpec((N_CHIPS, tr, cols), lambda i: (0, i, 0))],
        out_specs=pl.BlockSpec((tr, cols), lambda i: (i, 0)),
        compiler_params=_cparams(),
        name=name,
    )(p)


def _adamw(w, g, m, v, *, tr, name):
    rows, cols = w.shape
    bc1 = 1.0 / (1.0 - ADAM_B1 ** ADAM_STEP)
    bc2 = 1.0 / (1.0 - ADAM_B2 ** ADAM_STEP)

    def body(w_ref, g_ref, m_ref, v_ref, d_ref, nm_ref, nv_ref):
        gv = g_ref[...]
        nm = ADAM_B1 * m_ref[...] + (1.0 - ADAM_B1) * gv
        nv = ADAM_B2 * v_ref[...] + (1.0 - ADAM_B2) * (gv * gv)
        d_ref[...] = -ADAM_LR * ((nm * bc1) / (jnp.sqrt(nv * bc2) + ADAM_EPS) + ADAM_WD * w_ref[...])
        nm_ref[...] = nm
        nv_ref[...] = nv

    spec = pl.BlockSpec((tr, cols), lambda i: (i, 0))
    sd = jax.ShapeDtypeStruct((rows, cols), F32)
    return pl.pallas_call(body, out_shape=(sd, sd, sd), grid=(rows // tr,), in_specs=[spec] * 4, out_specs=(spec,) * 3,
                          compiler_params=_cparams(), name=name)(w, g, m, v)


def _adamw_halves(w, own, sib, cidx, m, v, *, tr, name):
    rows, cols = w.shape
    hr = own.shape[0]
    nr = hr // tr
    assert rows == 2 * hr and hr % tr == 0
    bc1 = 1.0 / (1.0 - ADAM_B1 ** ADAM_STEP)
    bc2 = 1.0 / (1.0 - ADAM_B2 ** ADAM_STEP)

    def body(c_ref, w_ref, o_ref, s_ref, m_ref, v_ref, g_ref, d_ref, nm_ref, nv_ref):
        mine = (pl.program_id(0) // nr) == c_ref[0]
        gv = jnp.where(mine, o_ref[...], s_ref[...])
        nm = ADAM_B1 * m_ref[...] + (1.0 - ADAM_B1) * gv
        nv = ADAM_B2 * v_ref[...] + (1.0 - ADAM_B2) * (gv * gv)
        g_ref[...] = gv
        d_ref[...] = -ADAM_LR * ((nm * bc1) / (jnp.sqrt(nv * bc2) + ADAM_EPS) + ADAM_WD * w_ref[...])
        nm_ref[...] = nm
        nv_ref[...] = nv

    full = pl.BlockSpec((tr, cols), lambda i, c_ref: (i, 0))
    half = pl.BlockSpec((tr, cols), lambda i, c_ref: (i % nr, 0))
    sd = jax.ShapeDtypeStruct((rows, cols), F32)
    grid_spec = pltpu.PrefetchScalarGridSpec(num_scalar_prefetch=1, grid=(rows // tr,), in_specs=[full, half, half, full, full],
                                             out_specs=(full,) * 4)
    return pl.pallas_call(body, out_shape=(sd,) * 4, grid_spec=grid_spec, compiler_params=_cparams(), name=name)(
        cidx, w, own, sib, m, v)


def _pack_small(norm, mem_norm, final_norm, b_forget):
    rows = [norm.reshape(1, D_MODEL), mem_norm.reshape(1, D_MODEL), final_norm.reshape(1, D_MODEL),
            jnp.pad(b_forget.reshape(1, FOX_HEADS), ((0, 0), (0, D_MODEL - FOX_HEADS))), jnp.zeros((4, D_MODEL), F32)]
    return jnp.concatenate(rows, axis=0)


def _unpack_small(a):
    return a[0:1], a[3:4, :FOX_HEADS], a[1:2], a[2]


def kernel(x, mem, norm_g, w_in, b_forget, mem_norm_g, w_mem_kv, w_out, final_norm_g, loss_target, m_norm_g, m_w_in, m_b_forget, m_mem_norm_g, m_w_mem_kv, m_w_out, m_final_norm_g, v_norm_g, v_w_in, v_b_forget, v_mem_norm_g, v_w_mem_kv, v_w_out, v_final_norm_g):
    core = lax.axis_index("c").astype(jnp.int32)
    me_chip = (2 * lax.axis_index("x") + lax.axis_index("y")).astype(jnp.int32)
    cidx = core.reshape(1)

    def own_slot(arr, own):
        return lax.dynamic_update_slice(arr, own[None].astype(arr.dtype), (me_chip,) + (0,) * own.ndim)

    win_b, late = w_in[0].astype(BF16), [w_mem_kv[0].astype(BF16), w_out[0].astype(BF16)]
    g_in, = _gather_weights([win_b])
    g_in, late = lax.optimization_barrier((own_slot(g_in, win_b), late))
    w_r = _rearrange_w_in([g_in[k] for k in range(N_CHIPS)])
    *late_flight, early_token = _gather_late_start(late)

    def late_weights(after):
        shards, landed = _gather_late_wait(*late_flight, after)
        g_kv, g_out = (own_slot(g, s) for g, s in zip(landed, shards))
        return g_kv.reshape(D_MODEL, 2 * MEM_W), g_out.reshape(MIX_W, D_MODEL)

    trs = (128, 128, 256)
    names = ("w_in", "w_mem_kv", "w_out")
    flights = {}

    def exchange(slabs, nms, ts, tag):
        recv = _pair_exchange(slabs, name=f"pair_exchange_{tag}")
        pair = [_sum_pair(g, r, cidx, tr=tr, name=f"sum_pair_{nm}") for g, r, tr, nm in zip(slabs, recv, ts, nms)]
        if tag == "w_in":
            pair[0] = _w_in_grad_slabs(pair[0][0])
        *flights[tag], token = _chip_exchange_start(pair, tag=tag)
        return token

    def start_reduce_small(g_wkv, g_wo):
        slabs = [g_wkv.reshape(N_CHIPS, D_MODEL // N_CHIPS, 2 * MEM_W), g_wo.reshape(N_CHIPS, MIX_W // N_CHIPS, D_MODEL)]
        return exchange(slabs, names[1:], trs[1:], "small")

    def start_reduce(g_wr):
        return exchange([g_wr[None]], names[:1], trs[:1], "w_in")

    gx, g_wr, g_wkv, g_wo, small = _local_grads(x, mem, norm_g, w_r, b_forget, mem_norm_g, None, None, final_norm_g, loss_target,
                                                start_reduce=start_reduce, start_reduce_small=start_reduce_small,
                                                early_token=early_token, late_weights=late_weights)

    pair, landed = [], []
    for tag in ("w_in", "small"):
        p, l = _chip_exchange_wait(*flights[tag], small, tag=tag)
        pair += list(p)
        landed += list(l)
    got = [lax.dynamic_update_slice(g, lax.dynamic_slice(p, (me_chip, 0, 0), (1,) + p.shape[1:]), (me_chip, 0, 0))
           for g, p in zip(landed, pair)]
    red = [_sum_chips(p, tr=tr, name=f"sum_chips_{nm}") for p, tr, nm in zip(got, trs, names)]
    sib = _pair_swap(red)

    outs = {}
    for nm, r, s, w, m, v, tr in zip(names, red, sib, (w_in, w_mem_kv, w_out), (m_w_in, m_w_mem_kv, m_w_out),
                                     (v_w_in, v_w_mem_kv, v_w_out), trs):
        outs[nm] = tuple(a[None] for a in _adamw_halves(w[0], r, s, cidx, m[0], v[0], tr=tr, name=f"adamw_{nm}"))

    gsum = _small_allreduce(small)
    sd, sm, sv = _adamw(_pack_small(norm_g, mem_norm_g, final_norm_g, b_forget), gsum,
                        _pack_small(m_norm_g, m_mem_norm_g, m_final_norm_g, m_b_forget),
                        _pack_small(v_norm_g, v_mem_norm_g, v_final_norm_g, v_b_forget), tr=8, name="adamw_small")
    loss = gsum[LOSS_ROW, 0]

    def group(i, small_arr):
        ng, bf, mg, fg = _unpack_small(small_arr)
        return (ng, outs["w_in"][i], bf, mg, outs["w_mem_kv"][i], outs["w_out"][i], fg)

    return (loss, gx, *group(0, gsum), *group(1, sd), *group(2, sm), *group(3, sv))
```

```python
import functools
import math

import jax
import jax.numpy as jnp
from jax import lax
from jax.experimental import pallas as pl
from jax.experimental.pallas import tpu as pltpu

F32 = jnp.float32
BF16 = jnp.bfloat16

D_MODEL = 1024
SEQ = 2048
HEAD_DIM = 64
FOX_HEADS = 12
DIL_HEADS = 12
MEM_HEADS = 4
MEM_HEAD_DIM = 128
MEM_LEN = 256
FOX_W = FOX_HEADS * HEAD_DIM
DIL_W = DIL_HEADS * HEAD_DIM
MEM_W = MEM_HEADS * MEM_HEAD_DIM
MIX_W = FOX_W + DIL_W + MEM_W
DILATIONS = ((128, 1), (512, 4), (2048, 16))
ROPE_THETA = 500000.0
ROPE_DIM = HEAD_DIM // 4
RMS_EPS = 1e-6
NEG_INF = -1e30
IN_SIZES = [FOX_W] * 4 + [FOX_HEADS] + [DIL_W] * 4 + [MEM_W] * 2
IN_W = sum(IN_SIZES)

ADAM_LR = 0.001
ADAM_B1 = 0.9
ADAM_B2 = 0.999
ADAM_EPS = 1e-08
ADAM_WD = 0.01
ADAM_STEP = 10

LANES = 128
N_CHIPS = 4
PW = 7168
PWF = PW + 4 * LANES
C_FQ, C_FK, C_FV, C_FG = 0, 768, 1536, 2304
C_DQ, C_DK, C_DV, C_DG = 3072, 3840, 4608, 5376
C_MQ, C_MG = 6144, 6656
VMEM_LIMIT = 48 * 1024 * 1024


def _cparams(**kw):
    return pltpu.CompilerParams(vmem_limit_bytes=VMEM_LIMIT, **kw)


def _matmul(a, b, *, out_dtype, tm, tn, tk, name, mode="nn"):
    if mode == "tn":
        (kdim, m), n = a.shape, b.shape[1]
        a_spec = pl.BlockSpec((tk, tm), lambda i, j, k: (k, i))
        b_spec = pl.BlockSpec((tk, tn), lambda i, j, k: (k, j))
        dims = _T0
    elif mode == "nt":
        (m, kdim), n = a.shape, b.shape[0]
        a_spec = pl.BlockSpec((tm, tk), lambda i, j, k: (i, k))
        b_spec = pl.BlockSpec((tn, tk), lambda i, j, k: (j, k))
        dims = _NT
    else:
        (m, kdim), n = a.shape, b.shape[1]
        a_spec = pl.BlockSpec((tm, tk), lambda i, j, k: (i, k))
        b_spec = pl.BlockSpec((tk, tn), lambda i, j, k: (k, j))
        dims = (((1,), (0,)), ((), ()))
    nk = kdim // tk
    assert m % tm == 0 and n % tn == 0 and kdim % tk == 0

    def body(a_ref, b_ref, o_ref, *scratch):
        prod = lax.dot_general(a_ref[...], b_ref[...], dims, preferred_element_type=F32)
        if nk == 1:
            o_ref[...] = prod.astype(o_ref.dtype)
            return
        acc_ref, = scratch
        k = pl.program_id(2)

        @pl.when(k == 0)
        def _():
            acc_ref[...] = prod

        @pl.when(k > 0)
        def _():
            acc_ref[...] += prod

        @pl.when(k == nk - 1)
        def _():
            o_ref[...] = acc_ref[...].astype(o_ref.dtype)

    return pl.pallas_call(
        body,
        out_shape=jax.ShapeDtypeStruct((m, n), out_dtype),
        grid=(m // tm, n // tn, nk),
        in_specs=[a_spec, b_spec],
        out_specs=pl.BlockSpec((tm, tn), lambda i, j, k: (i, j)),
        scratch_shapes=[pltpu.VMEM((tm, tn), F32)] if nk > 1 else [],
        compiler_params=_cparams(dimension_semantics=("parallel", "parallel", "arbitrary")),
        name=name,
    )(a, b)


def _rms_fwd(x, g, *, tm, name):
    t, d = x.shape

    def body(x_ref, g_ref, h_ref):
        xv = x_ref[...]
        r = lax.rsqrt(jnp.mean(xv * xv, axis=-1, keepdims=True) + RMS_EPS)
        h_ref[...] = (xv * r * g_ref[...]).astype(h_ref.dtype)

    return pl.pallas_call(
        body,
        out_shape=jax.ShapeDtypeStruct((t, d), BF16),
        grid=(t // tm,),
        in_specs=[pl.BlockSpec((tm, d), lambda i: (i, 0)), pl.BlockSpec((1, d), lambda i: (0, 0))],
        out_specs=pl.BlockSpec((tm, d), lambda i: (i, 0)),
        compiler_params=_cparams(),
        name=name,
    )(x, g)


def _rope_tables():
    half = ROPE_DIM // 2
    pos = jnp.arange(SEQ, dtype=F32)
    inv_freq = 1.0 / (ROPE_THETA ** (jnp.arange(0, ROPE_DIM, 2, dtype=F32) / ROPE_DIM))
    ang = pos[:, None] * inv_freq[None, :]
    cos, sin = jnp.cos(ang), jnp.sin(ang)
    one = jnp.ones((SEQ, HEAD_DIM - ROPE_DIM), F32)
    zero = jnp.zeros((SEQ, HEAD_DIM - ROPE_DIM), F32)
    zh = jnp.zeros((SEQ, half), F32)
    c = jnp.concatenate([cos, cos, one], axis=1)
    s1 = jnp.concatenate([zh, sin, zero], axis=1)
    s2 = jnp.concatenate([-sin, zh, zero], axis=1)
    rep = LANES // HEAD_DIM
    return jnp.tile(c, (1, rep)), jnp.tile(s1, (1, rep)), jnp.tile(s2, (1, rep))


def _rope_apply(t, c, s1, s2, transpose=False):
    n = t.shape[-1]
    rep = n // LANES
    c, s1, s2 = (jnp.tile(u, (1, rep)) for u in (c, s1, s2))
    half = ROPE_DIM // 2
    if not transpose:
        return t * c + pltpu.roll(t, half, 1) * s1 + pltpu.roll(t, n - half, 1) * s2
    return t * c + pltpu.roll(t * s1, n - half, 1) + pltpu.roll(t * s2, half, 1)


def _proj(x, g, w, tabs, *, n, tm, tn, name):
    t, d = x.shape
    assert C_DQ % tn == 0 and (C_DV - C_DQ) % tn == 0 and (C_DG - C_DQ) % tn == 0
    rope_lo, rope_hi, dil_hi = C_DQ // tn, C_DV // tn, C_DG // tn
    flog_blk, flog_at = PW // tn, PW % tn
    assert flog_at % LANES == 0 and flog_at + LANES <= tn
    s_blocks = SEQ // tm

    def body(x_ref, g_ref, w_ref, c_ref, s1_ref, s2_ref, h_ref, o_ref, f_ref, fl_ref, h_scr):
        j = pl.program_id(1)

        @pl.when(j == 0)
        def _():
            xv = x_ref[...]
            r = lax.rsqrt(jnp.mean(xv * xv, axis=-1, keepdims=True) + RMS_EPS)
            hv = (xv * r * g_ref[...]).astype(BF16)
            h_scr[...] = hv
            h_ref[...] = hv

        acc = jnp.dot(h_scr[...], w_ref[...], preferred_element_type=F32)
        is_rope = jnp.logical_and(j >= rope_lo, j < rope_hi)

        @pl.when(j == flog_blk)
        def _():
            fl_ref[...] = acc[:, flog_at:flog_at + LANES]

        @pl.when(is_rope)
        def _():
            r = _rope_apply(acc, c_ref[...], s1_ref[...], s2_ref[...])
            o_ref[...] = r.astype(o_ref.dtype)
            f_ref[...] = r

        @pl.when(jnp.logical_not(is_rope))
        def _():
            o_ref[...] = acc.astype(o_ref.dtype)

        @pl.when(jnp.logical_and(j >= rope_hi, j < dil_hi))
        def _():
            f_ref[...] = acc

    tab_spec = pl.BlockSpec((tm, LANES), lambda i, j: (i % s_blocks, 0))
    f_spec = pl.BlockSpec((tm, tn), lambda i, j: (i, jnp.clip(j - rope_lo, 0, dil_hi - rope_lo - 1)))
    row = pl.BlockSpec((tm, d), lambda i, j: (i, 0))
    return pl.pallas_call(
        body,
        out_shape=(jax.ShapeDtypeStruct((t, d), BF16), jax.ShapeDtypeStruct((t, n), BF16),
                   jax.ShapeDtypeStruct((t, 3 * DIL_W), F32), jax.ShapeDtypeStruct((t, LANES), F32)),
        grid=(t // tm, n // tn),
        in_specs=[row, pl.BlockSpec((1, d), lambda i, j: (0, 0)), pl.BlockSpec((d, tn), lambda i, j: (0, j)),
                  tab_spec, tab_spec, tab_spec],
        out_specs=(row, pl.BlockSpec((tm, tn), lambda i, j: (i, j)), f_spec, pl.BlockSpec((tm, LANES), lambda i, j: (i, 0))),
        scratch_shapes=[pltpu.VMEM((tm, d), BF16)],
        compiler_params=_cparams(dimension_semantics=("parallel", "arbitrary")),
        name=name,
    )(x, g, w, *tabs)


def _split3(x):
    hi = x.astype(BF16)
    r1 = x - hi.astype(F32)
    mid = r1.astype(BF16)
    lo = (r1 - mid.astype(F32)).astype(BF16)
    return hi, mid, lo


def _dot3(sel, x, sel_is_lhs):
    out = None
    for piece in _split3(x):
        t = jnp.dot(sel, piece, preferred_element_type=F32) if sel_is_lhs else jnp.dot(piece, sel, preferred_element_type=F32)
        out = t if out is None else out + t
    return out


def _flog_fwd(flog, bpad, *, nb, ts, name):
    ns = SEQ // ts

    def body(f_ref, b_ref, c_ref, carry_ref):
        s = pl.program_id(1)

        @pl.when(s == 0)
        def _():
            carry_ref[...] = jnp.zeros_like(carry_ref)

        z = f_ref[...] + b_ref[...]
        logf = jnp.minimum(z, 0.0) - jnp.log(1.0 + jnp.exp(-jnp.abs(z)))
        r = lax.broadcasted_iota(jnp.int32, (ts, ts), 0)
        c = lax.broadcasted_iota(jnp.int32, (ts, ts), 1)
        tri = jnp.where(r >= c, 1.0, 0.0).astype(BF16)
        cs = _dot3(tri, logf, True) + carry_ref[0:1, :]
        carry_ref[...] = jnp.broadcast_to(cs[ts - 1:ts, :], carry_ref.shape)
        c_ref[...] = cs

    return pl.pallas_call(
        body,
        out_shape=jax.ShapeDtypeStruct((nb * SEQ, LANES), F32),
        grid=(nb, ns),
        in_specs=[pl.BlockSpec((ts, LANES), lambda b, s: (b * ns + s, 0)), pl.BlockSpec((1, LANES), lambda b, s: (0, 0))],
        out_specs=pl.BlockSpec((ts, LANES), lambda b, s: (b * ns + s, 0)),
        scratch_shapes=[pltpu.VMEM((8, LANES), F32)],
        compiler_params=_cparams(dimension_semantics=("parallel", "arbitrary")),
        name=name,
    )(flog, bpad)


def _flog_bwd(dcol, flog, bpad, *, nb, ts, name):
    ns = SEQ // ts

    def body(d_ref, f_ref, b_ref, o_ref, gb_ref, carry_ref):
        bi = pl.program_id(0)
        s = pl.program_id(1)

        @pl.when(s == 0)
        def _():
            carry_ref[...] = jnp.zeros_like(carry_ref)

        @pl.when(jnp.logical_and(bi == 0, s == 0))
        def _():
            gb_ref[...] = jnp.zeros_like(gb_ref)

        r = lax.broadcasted_iota(jnp.int32, (ts, ts), 0)
        c = lax.broadcasted_iota(jnp.int32, (ts, ts), 1)
        tri = jnp.where(r <= c, 1.0, 0.0).astype(BF16)
        rc = _dot3(tri, d_ref[...], True) + carry_ref[0:1, :]
        carry_ref[...] = jnp.broadcast_to(rc[0:1, :], carry_ref.shape)
        z = f_ref[...] + b_ref[...]
        dz = rc / (1.0 + jnp.exp(z))
        o_ref[...] = dz.astype(o_ref.dtype)
        gb_ref[...] += jnp.broadcast_to(jnp.sum(dz, axis=0, keepdims=True), gb_ref.shape)

    rev = lambda b, s: (b * ns + (ns - 1 - s), 0)
    return pl.pallas_call(
        body,
        out_shape=(jax.ShapeDtypeStruct((nb * SEQ, LANES), BF16), jax.ShapeDtypeStruct((8, LANES), F32)),
        grid=(nb, ns),
        in_specs=[pl.BlockSpec((ts, LANES), rev), pl.BlockSpec((ts, LANES), rev), pl.BlockSpec((1, LANES), lambda b, s: (0, 0))],
        out_specs=(pl.BlockSpec((ts, LANES), rev), pl.BlockSpec((8, LANES), lambda b, s: (0, 0))),
        scratch_shapes=[pltpu.VMEM((8, LANES), F32)],
        compiler_params=_cparams(dimension_semantics=("arbitrary", "arbitrary")),
        name=name,
    )(dcol, flog, bpad)


MEM_TQ = 256
MEM_SET = 4
MEM_SCALE = 1.0 / math.sqrt(MEM_HEAD_DIM)
assert MEM_HEAD_DIM == LANES and SEQ % (MEM_TQ * MEM_SET) == 0


def _head_masks(nh):
    lane = lax.broadcasted_iota(jnp.int32, (1, LANES), 1)
    return [None] if nh == 1 else [lane < HEAD_DIM, lane >= HEAD_DIM]


def _mem_specs(qoff):
    qspec = pl.BlockSpec((None, SEQ, LANES), lambda b, j: (b, 0, qoff + j))
    kspec = pl.BlockSpec((None, MEM_LEN, LANES), lambda b, j: (b, 0, j))
    vspec = pl.BlockSpec((None, MEM_LEN, LANES), lambda b, j: (b, 0, MEM_HEADS + j))
    ospec = pl.BlockSpec((None, SEQ, LANES), lambda b, j: (b, 0, j))
    return qspec, kspec, vspec, ospec


def _mem_rows(g):
    return [pl.ds(pl.multiple_of((MEM_SET * g + a) * MEM_TQ, MEM_TQ), MEM_TQ) for a in range(MEM_SET)]


def _mem_fwd(p3, mkv3, *, qoff, name):
    nb = p3.shape[0]

    def body(q_ref, k_ref, v_ref, o_ref, lse_ref):
        kb, vb = k_ref[...], v_ref[...]

        def qset(g, c):
            rows = _mem_rows(g)
            ss = [lax.dot_general(q_ref[r, :] * MEM_SCALE, kb, _NT, preferred_element_type=F32) for r in rows]
            for r, s in zip(rows, ss):
                m = jnp.max(s, axis=1, keepdims=True)
                p = jnp.exp(s - m)
                l = jnp.sum(p, axis=1, keepdims=True)
                o_ref[r, :] = jnp.dot(p.astype(BF16), vb, preferred_element_type=F32) / l
                lse_ref[r, :] = jnp.broadcast_to(m + jnp.log(l), (MEM_TQ, LANES))
            return c

        lax.fori_loop(0, SEQ // MEM_TQ // MEM_SET, qset, 0)

    qspec, kspec, vspec, ospec = _mem_specs(qoff)
    osd = jax.ShapeDtypeStruct((nb, SEQ, MEM_W), F32)
    return pl.pallas_call(body, out_shape=(osd, osd), grid=(nb, MEM_HEADS), in_specs=[qspec, kspec, vspec],
                          out_specs=(ospec, ospec), compiler_params=_cparams(dimension_semantics=("parallel", "parallel")),
                          name=name)(p3, mkv3, mkv3)


def _mem_bwd(p3, mkv3, do, o, lse, *, qoff, do_off, name):
    nb = p3.shape[0]

    def body(q_ref, k_ref, v_ref, do_ref, o_ref, lse_ref, dq_ref, dk_ref, dv_ref):
        kb, vb = k_ref[...], v_ref[...]
        ks = kb * MEM_SCALE

        def qset(g, carry):
            dk, dv = carry
            work = []
            for r in _mem_rows(g):
                qs = q_ref[r, :] * MEM_SCALE
                dob = do_ref[r, :].astype(BF16)
                s = lax.dot_general(qs, kb, _NT, preferred_element_type=F32)
                dp = lax.dot_general(dob, vb, _NT, preferred_element_type=F32)
                work.append((r, qs, dob, s, dp))
            for r, qs, dob, s, dp in work:
                delta = jnp.sum(dob.astype(F32) * o_ref[r, :], axis=1, keepdims=True)
                p = jnp.exp(s - lse_ref[r, :][:, 0:1])
                ds = (p * (dp - delta)).astype(BF16)
                dq_ref[r, :] = jnp.dot(ds, ks, preferred_element_type=F32).astype(dq_ref.dtype)
                dk = dk + lax.dot_general(ds, qs, _T0, preferred_element_type=F32)
                dv = dv + lax.dot_general(p.astype(BF16), dob, _T0, preferred_element_type=F32)
            return dk, dv

        z = jnp.zeros((MEM_LEN, LANES), F32)
        dk, dv = lax.fori_loop(0, SEQ // MEM_TQ // MEM_SET, qset, (z, z))
        dk_ref[...] = dk
        dv_ref[...] = dv

    qspec, kspec, vspec, ospec = _mem_specs(qoff)
    dospec = pl.BlockSpec((None, SEQ, LANES), lambda b, j: (b, 0, do_off + j))
    kvo = pl.BlockSpec((None, MEM_LEN, LANES), lambda b, j: (b, 0, j))
    kvsd = jax.ShapeDtypeStruct((nb, MEM_LEN, MEM_W), F32)
    return pl.pallas_call(
        body, out_shape=(jax.ShapeDtypeStruct((nb, SEQ, MEM_W), BF16), kvsd, kvsd), grid=(nb, MEM_HEADS),
        in_specs=[qspec, kspec, vspec, dospec, ospec, ospec], out_specs=(ospec, kvo, kvo),
        compiler_params=_cparams(dimension_semantics=("parallel", "parallel")), name=name)(p3, mkv3, mkv3, do, o, lse)


BLK = 128
NBLK = SEQ // BLK
QK_SCALE = 1.0 / math.sqrt(HEAD_DIM)
DIL_STEPS = tuple(d for _, d in DILATIONS)
assert all(w // d == BLK for w, d in DILATIONS)
_T0 = (((0,), (0,)), ((), ()))
_NT = (((1,), (1,)), ((), ()))


def _stack_heads(a, masks):
    z = jnp.zeros_like(a)
    return jnp.concatenate([jnp.where(masks[0], a, z), jnp.where(masks[1], a, z)], axis=0)


def _tri_bias(lower):
    r = lax.broadcasted_iota(jnp.int32, (BLK, BLK), 0)
    c = lax.broadcasted_iota(jnp.int32, (BLK, BLK), 1)
    return jnp.where((c <= r) if lower else (c >= r), 0.0, NEG_INF).astype(F32)


def _dil_rows(r, i, d):
    start = r + i * (BLK * d)
    return pl.ds(start, BLK) if d == 1 else pl.ds(start, BLK, stride=d)


DIL_SET = 4


def _dil_sets(d, fn):
    nbk = SEQ // d // BLK
    if d == 1:
        n = 2 * DIL_SET
        def gbody(g, c):
            fn([(0, n * g + a, None if a == 0 else True) for a in range(n)])
            return c
        lax.fori_loop(0, nbk // n, gbody, 0)
    elif nbk > 1:
        assert nbk == DIL_SET
        def rbody(r, c):
            fn([(r, i, i > 0) for i in range(nbk)])
            return c
        lax.fori_loop(0, d, rbody, 0)
    else:
        def rbody(rr, c):
            fn([(DIL_SET * rr + a, 0, False) for a in range(DIL_SET)])
            return c
        lax.fori_loop(0, d // DIL_SET, rbody, 0)


def _dil_key_tiles(r, i, d, has_prev, qrows, tri_cur, tri_prev):
    tiles = [(qrows, tri_cur)]
    if has_prev is None:
        tiles.append((_dil_rows(r, jnp.maximum(i - 1, 0), d), tri_prev + jnp.where(i > 0, 0.0, NEG_INF)))
    elif has_prev:
        tiles.append((_dil_rows(r, i - 1, d), tri_prev))
    return tiles


def _dil_fwd(qkv, *, name):
    nb = qkv.shape[0]
    ncol = DIL_W // LANES
    hd = HEAD_DIM

    def body(q_ref, k_ref, v_ref, o_ref, lse_ref, m_ref, l_ref, a_ref):
        masks = _head_masks(2)
        tri_cur, tri_prev = _tri_bias(True), _tri_bias(False)
        for pi, d in enumerate(reversed(DIL_STEPS)):
            first, last = pi == 0, pi == len(DIL_STEPS) - 1

            def qset(blocks, d=d, first=first, last=last):
                work = []
                for r, i, has_prev in blocks:
                    qrows = _dil_rows(r, i, d)
                    qcat = _stack_heads((q_ref[qrows, :] * QK_SCALE).astype(BF16), masks)
                    ss, krs = [], []
                    for krows, bias in _dil_key_tiles(r, i, d, has_prev, qrows, tri_cur, tri_prev):
                        s = lax.dot_general(qcat, k_ref[krows, :].astype(BF16), _NT, preferred_element_type=F32)
                        ss.append((s[:BLK] + bias, s[BLK:] + bias))
                        krs.append(krows)
                    work.append((qrows, ss, krs))
                for qrows, ss, krs in work:
                    e0 = ss[0][0] if len(ss) == 1 else jnp.maximum(ss[0][0], ss[1][0])
                    e1 = ss[0][1] if len(ss) == 1 else jnp.maximum(ss[0][1], ss[1][1])
                    n0 = jnp.max(e0, axis=1, keepdims=True)
                    n1 = jnp.max(e1, axis=1, keepdims=True)
                    if not first:
                        mo, lo = m_ref[qrows, :], l_ref[qrows, :]
                        m0, m1 = mo[:, 0:1], mo[:, hd:hd + 1]
                        n0, n1 = jnp.maximum(n0, m0), jnp.maximum(n1, m1)
                        a0, a1 = jnp.exp(m0 - n0), jnp.exp(m1 - n1)
                    ps = [(jnp.exp(s0 - n0), jnp.exp(s1 - n1)) for s0, s1 in ss]
                    t0 = ps[0][0] if len(ps) == 1 else ps[0][0] + ps[1][0]
                    t1 = ps[0][1] if len(ps) == 1 else ps[0][1] + ps[1][1]
                    l0 = jnp.sum(t0, axis=1, keepdims=True)
                    l1 = jnp.sum(t1, axis=1, keepdims=True)
                    acc = None
                    for (p0, p1), krows in zip(ps, krs):
                        vcat = _stack_heads(v_ref[krows, :].astype(BF16), masks)
                        pv = jnp.dot(jnp.concatenate([p0, p1], axis=1).astype(BF16), vcat, preferred_element_type=F32)
                        acc = pv if acc is None else acc + pv
                    if not first:
                        l0 = l0 + a0 * lo[:, 0:1]
                        l1 = l1 + a1 * lo[:, hd:hd + 1]
                        acc = acc + a_ref[qrows, :] * jnp.where(masks[0], a0, a1)
                    if last:
                        o_ref[qrows, :] = acc / jnp.where(masks[0], l0, l1)
                        lse_ref[qrows, :] = jnp.where(masks[0], n0 + jnp.log(l0), n1 + jnp.log(l1))
                    else:
                        m_ref[qrows, :] = jnp.where(masks[0], n0, n1)
                        l_ref[qrows, :] = jnp.where(masks[0], l0, l1)
                        a_ref[qrows, :] = acc

            _dil_sets(d, qset)

    spec = lambda off: pl.BlockSpec((None, SEQ, LANES), lambda b, j: (b, 0, off + j))
    ospec = pl.BlockSpec((None, SEQ, LANES), lambda b, j: (b, 0, j))
    osd = jax.ShapeDtypeStruct((nb, SEQ, DIL_W), F32)
    return pl.pallas_call(
        body, out_shape=(osd, osd), grid=(nb, ncol),
        in_specs=[spec(0), spec(ncol), spec(2 * ncol)], out_specs=(ospec, ospec),
        scratch_shapes=[pltpu.VMEM((SEQ, LANES), F32)] * 3,
        compiler_params=_cparams(dimension_semantics=("parallel", "parallel")), name=name,
    )(qkv, qkv, qkv)


def _dil_bwd(qkv, do, o, lse, tabs, *, do_off, name):
    nb = qkv.shape[0]
    ncol = DIL_W // LANES
    hd = HEAD_DIM

    def body(q_ref, k_ref, v_ref, do_ref, o_ref, lse_ref, c_ref, s1_ref, s2_ref, dqo_ref, dko_ref, dvo_ref,
             dq_ref, dk_ref, dv_ref, dl_ref, dof_ref):
        masks = _head_masks(2)
        tri_cur, tri_prev = _tri_bias(True), _tri_bias(False)
        dq_ref[...] = jnp.zeros_like(dq_ref)
        dk_ref[...] = jnp.zeros_like(dk_ref)
        dv_ref[...] = jnp.zeros_like(dv_ref)

        def delta_body(i, c):
            rows = pl.ds(pl.multiple_of(i * BLK, BLK), BLK)
            dof = do_ref[rows, :].astype(F32)
            dof_ref[rows, :] = dof
            prod = dof * o_ref[rows, :]
            z = jnp.zeros_like(prod)
            dl_ref[rows, :] = jnp.where(masks[0], jnp.sum(jnp.where(masks[0], prod, z), axis=1, keepdims=True),
                                        jnp.sum(jnp.where(masks[1], prod, z), axis=1, keepdims=True))
            return c

        lax.fori_loop(0, NBLK, delta_body, 0)

        for d in DIL_STEPS:
            def qset(blocks, d=d):
                work = []
                for r, i, has_prev in blocks:
                    qrows = _dil_rows(r, i, d)
                    qcat = _stack_heads((q_ref[qrows, :] * QK_SCALE).astype(BF16), masks)
                    docat = _stack_heads(dof_ref[qrows, :].astype(BF16), masks)
                    tiles = []
                    for krows, bias in _dil_key_tiles(r, i, d, has_prev, qrows, tri_cur, tri_prev):
                        s = lax.dot_general(qcat, k_ref[krows, :].astype(BF16), _NT, preferred_element_type=F32)
                        dp = lax.dot_general(docat, v_ref[krows, :].astype(BF16), _NT, preferred_element_type=F32)
                        tiles.append((krows, s, dp, bias))
                    work.append((qrows, qcat, docat, tiles))
                for qrows, qcat, docat, tiles in work:
                    lseb, dlb = lse_ref[qrows, :], dl_ref[qrows, :]
                    lse0, lse1 = lseb[:, 0:1], lseb[:, hd:hd + 1]
                    dl0, dl1 = dlb[:, 0:1], dlb[:, hd:hd + 1]
                    dq = None
                    for krows, s, dp, bias in tiles:
                        p0 = jnp.exp(s[:BLK] + bias - lse0)
                        p1 = jnp.exp(s[BLK:] + bias - lse1)
                        ds0 = p0 * (dp[:BLK] - dl0)
                        ds1 = p1 * (dp[BLK:] - dl1)
                        ds0b, ds1b = ds0.astype(BF16), ds1.astype(BF16)
                        pcat = jnp.concatenate([p0.astype(BF16), p1.astype(BF16)], axis=0)
                        dscat = jnp.concatenate([ds0b, ds1b], axis=0)
                        dv_ref[krows, :] += lax.dot_general(pcat, docat, _T0, preferred_element_type=F32)
                        dk_ref[krows, :] += lax.dot_general(dscat, qcat, _T0, preferred_element_type=F32)
                        dsrow = jnp.concatenate([ds0b, ds1b], axis=1)
                        kcat = _stack_heads((k_ref[krows, :] * QK_SCALE).astype(BF16), masks)
                        t = jnp.dot(dsrow, kcat, preferred_element_type=F32)
                        dq = t if dq is None else dq + t
                    dq_ref[qrows, :] += dq

            _dil_sets(d, qset)

        def out_body(i, c):
            rows = pl.ds(pl.multiple_of(i * BLK, BLK), BLK)
            tab = (c_ref[rows, :], s1_ref[rows, :], s2_ref[rows, :])
            dqo_ref[rows, :] = _rope_apply(dq_ref[rows, :], *tab, transpose=True).astype(dqo_ref.dtype)
            dko_ref[rows, :] = _rope_apply(dk_ref[rows, :], *tab, transpose=True).astype(dko_ref.dtype)
            dvo_ref[rows, :] = dv_ref[rows, :].astype(dvo_ref.dtype)
            return c

        lax.fori_loop(0, NBLK, out_body, 0)

    spec = lambda off: pl.BlockSpec((None, SEQ, LANES), lambda b, j: (b, 0, off + j))
    ospec = pl.BlockSpec((None, SEQ, LANES), lambda b, j: (b, 0, j))
    tspec = pl.BlockSpec((SEQ, LANES), lambda b, j: (0, 0))
    osd = jax.ShapeDtypeStruct((nb, SEQ, DIL_W), BF16)
    return pl.pallas_call(
        body, out_shape=(osd, osd, osd), grid=(nb, ncol),
        in_specs=[spec(0), spec(ncol), spec(2 * ncol), spec(do_off), ospec, ospec, tspec, tspec, tspec],
        out_specs=(ospec, ospec, ospec),
        scratch_shapes=[pltpu.VMEM((SEQ, LANES), F32)] * 5,
        compiler_params=_cparams(dimension_semantics=("parallel", "parallel")), name=name,
    )(qkv, qkv, qkv, do, o, lse, *tabs)


FOX_GROUP = 4
assert NBLK % FOX_GROUP == 0
_FOX_COLS = tuple(c // LANES for c in (C_FQ, C_FK, C_FV))


def _fox_specs():
    cols = [pl.BlockSpec((None, SEQ, LANES), (lambda b, j, off=off: (b, 0, off + j))) for off in _FOX_COLS]
    ospec = pl.BlockSpec((None, SEQ, LANES), lambda b, j: (b, 0, j))
    crspec = pl.BlockSpec((None, None, NBLK, 8, BLK), lambda b, j: (b, j, 0, 0, 0))
    return cols, ospec, crspec


def _fox_key_rows(t, e):
    return pl.ds(pl.multiple_of((FOX_GROUP * t + e) * BLK, BLK), BLK)


def _fox_fwd(p3, crow, *, name):
    nb = p3.shape[0]
    g = FOX_GROUP

    def body(q_ref, k_ref, v_ref, cr_ref, o_ref, lse_ref):
        masks = _head_masks(2)
        tri = _tri_bias(True)

        def qk(qcat, t):
            return tuple(lax.dot_general(qcat, k_ref[_fox_key_rows(t, e), :], _NT, preferred_element_type=F32) for e in range(g))

        def consume(ss, t, state, nblk, diag):
            m0, m1, l0, l1, acc = state
            us = []
            for e in range(nblk):
                cr = cr_ref[g * t + e]
                u0 = ss[e][:BLK] - cr[0:1, :]
                u1 = ss[e][BLK:] - cr[1:2, :]
                if diag and e == nblk - 1:
                    u0, u1 = u0 + tri, u1 + tri
                us.append((u0, u1))
            x0 = functools.reduce(jnp.maximum, [u[0] for u in us])
            x1 = functools.reduce(jnp.maximum, [u[1] for u in us])
            n0 = jnp.maximum(m0, jnp.max(x0, axis=1, keepdims=True))
            n1 = jnp.maximum(m1, jnp.max(x1, axis=1, keepdims=True))
            a0, a1 = jnp.exp(m0 - n0), jnp.exp(m1 - n1)
            acc = acc * jnp.where(masks[0], a0, a1)
            t0 = t1 = None
            for e in range(nblk):
                p0, p1 = jnp.exp(us[e][0] - n0), jnp.exp(us[e][1] - n1)
                t0 = p0 if t0 is None else t0 + p0
                t1 = p1 if t1 is None else t1 + p1
                pcat = jnp.concatenate([p0, p1], axis=1)
                hi = pcat.astype(BF16)
                lo = (pcat - hi.astype(F32)).astype(BF16)
                vcat = _stack_heads(v_ref[_fox_key_rows(t, e), :], masks)
                acc = acc + jnp.dot(hi, vcat, preferred_element_type=F32) + jnp.dot(lo, vcat, preferred_element_type=F32)
            l0 = a0 * l0 + jnp.sum(t0, axis=1, keepdims=True)
            l1 = a1 * l1 + jnp.sum(t1, axis=1, keepdims=True)
            return n0, n1, l0, l1, acc

        def gbody(ng, c):
            neg = jnp.full((BLK, 1), NEG_INF, F32)
            z1 = jnp.zeros((BLK, 1), F32)
            rows = [pl.ds(pl.multiple_of((g * ng + a) * BLK, BLK), BLK) for a in range(g)]
            qcats = [_stack_heads(q_ref[rows[a], :] * QK_SCALE, masks) for a in range(g)]
            first = [qk(qcats[a], 0) for a in range(g)]
            done = []
            for a in range(g):
                def step(t, cc, qcat=qcats[a]):
                    ss, st = cc
                    nxt = qk(qcat, t + 1)
                    return nxt, consume(ss, t, st, g, False)

                done.append(lax.fori_loop(0, ng, step, (first[a], (neg, neg, z1, z1, jnp.zeros((BLK, LANES), F32)))))
            for a in range(g):
                ss, state = done[a]
                m0, m1, l0, l1, acc = consume(ss, ng, state, a + 1, True)
                o_ref[rows[a], :] = acc / jnp.where(masks[0], l0, l1)
                lse_ref[rows[a], :] = jnp.where(masks[0], m0 + jnp.log(l0), m1 + jnp.log(l1))
            return c

        lax.fori_loop(0, NBLK // g, gbody, 0)

    cols, ospec, crspec = _fox_specs()
    osd = jax.ShapeDtypeStruct((nb, SEQ, FOX_W), F32)
    return pl.pallas_call(
        body, out_shape=(osd, osd), grid=(nb, FOX_W // LANES), in_specs=cols + [crspec], out_specs=(ospec, ospec),
        compiler_params=_cparams(dimension_semantics=("parallel", "parallel")), name=name,
    )(p3, p3, p3, crow)


def _fox_bwd(p3, crow, do, o, lse, *, do_off, name):
    nb = p3.shape[0]
    g = FOX_GROUP
    hd = HEAD_DIM

    def body(q_ref, k_ref, v_ref, cr_ref, do_ref, o_ref, lse_ref, dq_ref, dko_ref, dvo_ref, dcr_ref, dk_ref, dv_ref):
        masks = _head_masks(2)
        tri = _tri_bias(True)
        dk_ref[...] = jnp.zeros_like(dk_ref)
        dv_ref[...] = jnp.zeros_like(dv_ref)
        dcr_ref[...] = jnp.zeros_like(dcr_ref)

        def products(qcat, docat, t):
            out = []
            for e in range(g):
                krows = _fox_key_rows(t, e)
                out.append(lax.dot_general(qcat, k_ref[krows, :], _NT, preferred_element_type=F32))
                out.append(lax.dot_general(docat, v_ref[krows, :], _NT, preferred_element_type=F32))
            return tuple(out)

        def consume(prod, t, ctx, dq, nblk, diag):
            qcat, docat, lse0, lse1, dl0, dl1 = ctx
            for e in range(nblk):
                jb = g * t + e
                krows = _fox_key_rows(t, e)
                s, dp = prod[2 * e], prod[2 * e + 1]
                cr = cr_ref[jb]
                u0 = s[:BLK] - cr[0:1, :]
                u1 = s[BLK:] - cr[1:2, :]
                if diag and e == nblk - 1:
                    u0, u1 = u0 + tri, u1 + tri
                p0 = jnp.exp(u0 - lse0)
                p1 = jnp.exp(u1 - lse1)
                ds0 = p0 * (dp[:BLK] - dl0)
                ds1 = p1 * (dp[BLK:] - dl1)
                dcr_ref[jb, 0:1, :] += jnp.sum(ds0, axis=0, keepdims=True)
                dcr_ref[jb, 1:2, :] += jnp.sum(ds1, axis=0, keepdims=True)
                ds0b, ds1b = ds0.astype(BF16), ds1.astype(BF16)
                pcat = jnp.concatenate([p0.astype(BF16), p1.astype(BF16)], axis=0)
                dscat = jnp.concatenate([ds0b, ds1b], axis=0)
                dv_ref[krows, :] += lax.dot_general(pcat, docat, _T0, preferred_element_type=F32)
                dk_ref[krows, :] += lax.dot_general(dscat, qcat, _T0, preferred_element_type=F32)
                dsrow = jnp.concatenate([ds0b, ds1b], axis=1)
                dq = dq + jnp.dot(dsrow, _stack_heads(k_ref[krows, :] * QK_SCALE, masks), preferred_element_type=F32)
            return dq

        def gbody(ng, c):
            ctxs, rows = [], []
            for a in range(g):
                r = pl.ds(pl.multiple_of((g * ng + a) * BLK, BLK), BLK)
                qcat = _stack_heads(q_ref[r, :] * QK_SCALE, masks)
                dob = do_ref[r, :].astype(BF16)
                prod = dob.astype(F32) * o_ref[r, :]
                z = jnp.zeros_like(prod)
                dl0 = jnp.sum(jnp.where(masks[0], prod, z), axis=1, keepdims=True)
                dl1 = jnp.sum(jnp.where(masks[1], prod, z), axis=1, keepdims=True)
                lseb = lse_ref[r, :]
                ctxs.append((qcat, _stack_heads(dob, masks), lseb[:, 0:1], lseb[:, hd:hd + 1], dl0, dl1))
                rows.append(r)
            first = [products(ctxs[a][0], ctxs[a][1], 0) for a in range(g)]
            done = []
            for a in range(g):
                def step(t, cc, ctx=ctxs[a]):
                    pr, dq = cc
                    nxt = products(ctx[0], ctx[1], t + 1)
                    return nxt, consume(pr, t, ctx, dq, g, False)

                done.append(lax.fori_loop(0, ng, step, (first[a], jnp.zeros((BLK, LANES), F32))))
            for a in range(g):
                pr, dq = done[a]
                dq_ref[rows[a], :] = consume(pr, ng, ctxs[a], dq, a + 1, True).astype(dq_ref.dtype)
            return c

        lax.fori_loop(0, NBLK // g, gbody, 0)
        dko_ref[...] = dk_ref[...].astype(dko_ref.dtype)
        dvo_ref[...] = dv_ref[...].astype(dvo_ref.dtype)

    cols, ospec, crspec = _fox_specs()
    dospec = pl.BlockSpec((None, SEQ, LANES), lambda b, j: (b, 0, do_off + j))
    osd = jax.ShapeDtypeStruct((nb, SEQ, FOX_W), BF16)
    return pl.pallas_call(
        body, out_shape=(osd, osd, osd, jax.ShapeDtypeStruct((nb, FOX_W // LANES, NBLK, 8, BLK), F32)),
        grid=(nb, FOX_W // LANES), in_specs=cols + [crspec, dospec, ospec, ospec], out_specs=(ospec, ospec, ospec, crspec),
        scratch_shapes=[pltpu.VMEM((SEQ, LANES), F32)] * 2,
        compiler_params=_cparams(dimension_semantics=("parallel", "parallel")), name=name,
    )(p3, p3, p3, crow, do, o, lse)


_B1, _B2 = FOX_W // LANES, (FOX_W + DIL_W) // LANES


def _dy_gate_bwd(dx2b, wo, fox, dil, memo, p16, *, tm, tn, name):
    t, d = dx2b.shape
    assert FOX_W % tn == 0 and DIL_W % tn == 0 and MEM_W % tn == 0 and all(c % tn == 0 for c in (C_FG, C_DG, C_MG))
    n1, n2, n3 = FOX_W // tn, (FOX_W + DIL_W) // tn, MIX_W // tn

    def body(dx_ref, w_ref, f_ref, d_ref, m_ref, g_ref, da_ref, dg_ref):
        j = pl.program_id(1)
        dyv = lax.dot_general(dx_ref[...], w_ref[...], _NT, preferred_element_type=F32)
        a = jnp.where(j < n1, f_ref[...], jnp.where(j < n2, d_ref[...], m_ref[...]))
        gt = g_ref[...].astype(F32)
        sg = 1.0 / (1.0 + jnp.exp(-gt))
        da_ref[...] = (dyv * gt * sg).astype(da_ref.dtype)
        dg_ref[...] = (dyv * a * sg * (1.0 + gt * (1.0 - sg))).astype(dg_ref.dtype)

    def gcol(j):
        return jnp.where(j < n1, C_FG // tn + j, jnp.where(j < n2, C_DG // tn + j - n1, C_MG // tn + j - n2))

    tile = pl.BlockSpec((tm, tn), lambda i, j: (i, j))
    return pl.pallas_call(
        body,
        out_shape=(jax.ShapeDtypeStruct((t, MIX_W), BF16), jax.ShapeDtypeStruct((t, MIX_W), BF16)),
        grid=(t // tm, n3),
        in_specs=[pl.BlockSpec((tm, d), lambda i, j: (i, 0)), pl.BlockSpec((tn, d), lambda i, j: (j, 0)),
                  pl.BlockSpec((tm, tn), lambda i, j: (i, jnp.minimum(j, n1 - 1))),
                  pl.BlockSpec((tm, tn), lambda i, j: (i, jnp.clip(j - n1, 0, n2 - n1 - 1))),
                  pl.BlockSpec((tm, tn), lambda i, j: (i, jnp.clip(j - n2, 0, n3 - n2 - 1))),
                  pl.BlockSpec((tm, tn), lambda i, j: (i, gcol(j)))],
        out_specs=(tile, tile),
        compiler_params=_cparams(dimension_semantics=("parallel", "parallel")),
        name=name,
    )(dx2b, wo, fox, dil, memo, p16)


def _silu(g):
    return g / (1.0 + jnp.exp(-g))


def _out_loss(fox, dil, memo, p16, wo, x, tgt, gfin, *, tm, name):
    t, d = x.shape
    n_feat = float(d)

    def body(f_ref, d_ref, m_ref, fg_ref, dg_ref, mg_ref, w_ref, x_ref, t_ref, g_ref, y_ref, dx_ref, dxb_ref, st_ref):
        i = pl.program_id(0)

        @pl.when(i == 0)
        def _():
            st_ref[...] = jnp.zeros_like(st_ref)

        y = jnp.concatenate([(a_ref[...] * _silu(gt_ref[...].astype(F32))).astype(BF16)
                             for a_ref, gt_ref in ((f_ref, fg_ref), (d_ref, dg_ref), (m_ref, mg_ref))], axis=1)
        y_ref[...] = y
        x2 = x_ref[...] + jnp.dot(y, w_ref[...], preferred_element_type=F32)
        r = lax.rsqrt(jnp.mean(x2 * x2, axis=-1, keepdims=True) + RMS_EPS)
        nrm = x2 * r
        gv = g_ref[...]
        err = nrm * gv - t_ref[...]
        dout = err * (1.0 / n_feat)
        dn = dout * gv
        dx2 = r * (dn - nrm * jnp.mean(dn * nrm, axis=-1, keepdims=True))
        dx_ref[...] = dx2
        dxb_ref[...] = dx2.astype(dxb_ref.dtype)
        st_ref[0:1, :] += jnp.sum(dout * nrm, axis=0, keepdims=True)
        st_ref[1:2, :] += (0.5 / n_feat) * jnp.sum(err * err, axis=0, keepdims=True)

    row = pl.BlockSpec((tm, d), lambda i: (i, 0))
    whole = lambda w: pl.BlockSpec((tm, w), lambda i: (i, 0))
    gate = lambda w, col: pl.BlockSpec((tm, w), lambda i: (i, col // w))
    return pl.pallas_call(
        body,
        out_shape=(jax.ShapeDtypeStruct((t, MIX_W), BF16), jax.ShapeDtypeStruct((t, d), F32), jax.ShapeDtypeStruct((t, d), BF16),
                   jax.ShapeDtypeStruct((8, d), F32)),
        grid=(t // tm,),
        in_specs=[whole(FOX_W), whole(DIL_W), whole(MEM_W), gate(FOX_W, C_FG), gate(DIL_W, C_DG), gate(MEM_W, C_MG),
                  pl.BlockSpec((MIX_W, d), lambda i: (0, 0)), row, row, pl.BlockSpec((1, d), lambda i: (0, 0))],
        out_specs=(pl.BlockSpec((tm, MIX_W), lambda i: (i, 0)), row, row, pl.BlockSpec((8, d), lambda i: (0, 0))),
        compiler_params=_cparams(dimension_semantics=("arbitrary",)),
        name=name,
    )(fox, dil, memo, p16, p16, p16, wo, x, tgt, gfin)


def _dh_rms_bwd(dp, w, x, g, resid, *, tm, name):
    t, d = x.shape
    kdim = dp.shape[1]

    def body(*refs):
        if resid is not None:
            dp_ref, w_ref, x_ref, g_ref, r_ref, dx_ref, gg_ref = refs
        else:
            dp_ref, w_ref, x_ref, g_ref, dx_ref, gg_ref = refs

        @pl.when(pl.program_id(0) == 0)
        def _():
            gg_ref[...] = jnp.zeros_like(gg_ref)

        dh = lax.dot_general(dp_ref[...], w_ref[...], _NT, preferred_element_type=F32)
        xv = x_ref[...]
        r = lax.rsqrt(jnp.mean(xv * xv, axis=-1, keepdims=True) + RMS_EPS)
        nrm = xv * r
        dn = dh * g_ref[...]
        dx = r * (dn - nrm * jnp.mean(dn * nrm, axis=-1, keepdims=True))
        if resid is not None:
            dx = dx + r_ref[...]
        dx_ref[...] = dx
        gg_ref[0:1, :] += jnp.sum(dh * nrm, axis=0, keepdims=True)

    row = pl.BlockSpec((tm, d), lambda i: (i, 0))
    in_specs = [pl.BlockSpec((tm, kdim), lambda i: (i, 0)),
                pl.BlockSpec((d, kdim), lambda i: (0, 0), pipeline_mode=pl.Buffered(1)), row,
                pl.BlockSpec((1, d), lambda i: (0, 0))]
    args = [dp, w, x, g]
    if resid is not None:
        in_specs.append(row)
        args.append(resid)
    return pl.pallas_call(
        body,
        out_shape=(jax.ShapeDtypeStruct((t, d), F32), jax.ShapeDtypeStruct((8, d), F32)),
        grid=(t // tm,),
        in_specs=in_specs,
        out_specs=(row, pl.BlockSpec((8, d), lambda i: (0, 0))),
        compiler_params=_cparams(dimension_semantics=("arbitrary",)),
        name=name,
    )(*args)


_FLOG0 = 4 * FOX_W
_W_IN_SEGMENTS = ((0, _FLOG0, 0), (_FLOG0, _FLOG0 + FOX_HEADS, PW), (_FLOG0 + FOX_HEADS, IN_W, C_DQ))
SHARD_W = IN_W // N_CHIPS


def _rearrange_w_in(shards):
    def cols(lo, hi):
        parts = []
        for k in range(N_CHIPS):
            a, b = max(lo, k * SHARD_W), min(hi, (k + 1) * SHARD_W)
            if a < b:
                parts.append(shards[k][:, a - k * SHARD_W:b - k * SHARD_W])
        return parts

    (a0, a1, _), (f0, f1, _), (b0, b1, _) = _W_IN_SEGMENTS
    pad = jnp.zeros((shards[0].shape[0], PWF - PW - FOX_HEADS), shards[0].dtype)
    return jnp.concatenate(cols(a0, a1) + cols(b0, b1) + cols(f0, f1) + [pad], axis=1)


def _w_in_grad_slabs(g):
    slabs = []
    for k in range(N_CHIPS):
        parts = []
        for lo, hi, at in _W_IN_SEGMENTS:
            a, b = max(lo, k * SHARD_W), min(hi, (k + 1) * SHARD_W)
            if a < b:
                parts.append(g[:, at + a - lo:at + b - lo])
        slabs.append(jnp.concatenate(parts, axis=1))
    return jnp.stack(slabs, axis=0)


def _local_grads(x, mem, norm_g, w_r, b_forget, mem_norm_g, w_kv, w_o, final_norm_g, tgt, start_reduce=None,
                 start_reduce_small=None, early_token=None, late_weights=None):
    nb = x.shape[0]
    t = nb * SEQ
    x2d = x.reshape(t, D_MODEL)
    tgt2d = tgt.reshape(t, D_MODEL)
    tabs = _rope_tables()
    bpad = jnp.pad(b_forget.reshape(1, FOX_HEADS), ((0, 0), (0, LANES - FOX_HEADS)))

    gain0 = norm_g.reshape(1, D_MODEL)
    if early_token is not None:
        gain0 = gain0 + early_token[0:1, 0:1]
    h, p16, dqkv, flog = _proj(x2d, gain0, w_r, tabs, n=PWF, tm=1024, tn=768, name="proj")
    c12 = _flog_fwd(flog, bpad, nb=nb, ts=256, name="flog_fwd")

    crow = c12[:, :FOX_HEADS].reshape(nb, NBLK, BLK, FOX_HEADS // 2, 2).transpose(0, 3, 1, 4, 2)
    crow = jnp.pad(crow, ((0, 0), (0, 0), (0, 0), (0, 6), (0, 0)))
    p3 = p16.reshape(nb, SEQ, PWF)
    fox, fox_lse = _fox_fwd(p3, crow, name="fox_fwd")
    if late_weights is not None:
        w_kv, w_o = late_weights(fox_lse)

    dqkv3 = dqkv.reshape(nb, SEQ, 3 * DIL_W)
    dil, dil_lse = _dil_fwd(dqkv3, name="dil_fwd")

    mh = _rms_fwd(mem.reshape(nb * MEM_LEN, D_MODEL), mem_norm_g.reshape(1, D_MODEL), tm=nb * MEM_LEN, name="rms_mem")
    mkv = _matmul(mh, w_kv, out_dtype=BF16, tm=nb * MEM_LEN, tn=512, tk=D_MODEL, name="mem_kv")
    mkv3 = mkv.reshape(nb, MEM_LEN, 2 * MEM_W)
    memo, mem_lse = _mem_fwd(p3, mkv3, qoff=C_MQ // LANES, name="mem_fwd")

    fox2, dil2, memo2 = fox.reshape(t, FOX_W), dil.reshape(t, DIL_W), memo.reshape(t, MEM_W)
    y, dx2, dx2b, st = _out_loss(fox2, dil2, memo2, p16, w_o, x2d, tgt2d, final_norm_g.reshape(1, D_MODEL), tm=256,
                                 name="out_loss")

    g_wo = _matmul(y, dx2b, mode="tn", out_dtype=BF16, tm=1024, tn=512, tk=t, name="grad_w_out")
    datt, dgate = _dy_gate_bwd(dx2b, w_o, fox2, dil2, memo2, p16, tm=2048, tn=256, name="dy_gate_bwd")
    datt3 = datt.reshape(nb, SEQ, MIX_W)

    dmq, dmk, dmv = _mem_bwd(p3, mkv3, datt3, memo, mem_lse, qoff=C_MQ // LANES, do_off=_B2, name="mem_bwd")
    dmkv = jnp.concatenate([dmk, dmv], axis=-1).reshape(nb * MEM_LEN, 2 * MEM_W).astype(BF16)
    g_wkv = _matmul(mh, dmkv, mode="tn", out_dtype=BF16, tm=512, tn=512, tk=nb * MEM_LEN, name="grad_w_kv")
    mem_gain = mem_norm_g.reshape(1, D_MODEL)
    if start_reduce_small is not None:
        tok = start_reduce_small(g_wkv, g_wo)[0:1, 0:1]
        mem_gain, crow = mem_gain + tok, crow + tok
    _, gmn = _dh_rms_bwd(dmkv, w_kv, mem.reshape(nb * MEM_LEN, D_MODEL), mem_gain, None, tm=nb * MEM_LEN, name="mem_rms_bwd")

    dfq, dfk, dfv, dcr = _fox_bwd(p3, crow, datt3, fox, fox_lse, do_off=0, name="fox_bwd")
    dcol = -dcr[:, :, :, :2, :].transpose(0, 2, 4, 1, 3).reshape(t, FOX_HEADS)
    dcol = jnp.pad(dcol, ((0, 0), (0, LANES - FOX_HEADS)))
    dflog, gb = _flog_bwd(dcol, flog, bpad, nb=nb, ts=256, name="flog_bwd")

    ddq, ddk, ddv = _dil_bwd(dqkv3, datt3, dil, dil_lse, tabs, do_off=_B1, name="dil_bwd")

    flat = lambda a: a.reshape(t, -1)
    dp = jnp.concatenate([flat(dfq), flat(dfk), flat(dfv), dgate[:, :FOX_W], flat(ddq), flat(ddk), flat(ddv),
                          dgate[:, FOX_W:FOX_W + DIL_W], flat(dmq), dgate[:, FOX_W + DIL_W:], dflog,
                          jnp.zeros((t, PWF - PW - LANES), BF16)], axis=1)
    g_wr = _matmul(h, dp, mode="tn", out_dtype=BF16, tm=D_MODEL, tn=512, tk=t, name="grad_w_in")
    gain = norm_g.reshape(1, D_MODEL)
    if start_reduce is not None:
        gain = gain + start_reduce(g_wr)[0:1, 0:1]
    gx, gng = _dh_rms_bwd(dp, w_r, x2d, gain, dx2, tm=256, name="in_rms_bwd")

    gb_row = jnp.pad(gb[0:1, :], ((0, 0), (0, D_MODEL - LANES)))
    small = jnp.concatenate([gng[0:1], gmn[0:1], st[0:1], gb_row, st[1:2], jnp.zeros((3, D_MODEL), F32)], axis=0)
    return gx.reshape(nb, SEQ, D_MODEL), g_wr, g_wkv, g_wo, small


MESH = pl.DeviceIdType.MESH
ANY = pl.BlockSpec(memory_space=pl.ANY)


def _place():
    x, y, c = lax.axis_index("x"), lax.axis_index("y"), lax.axis_index("c")
    other_chips = [(1 - x, y), (x, 1 - y), (1 - x, 1 - y)]
    return x, y, c, other_chips


def _gather_weights(shards):
    n = len(shards)

    def body(*refs):
        in_refs, out_refs = refs[:n], refs[n:2 * n]
        send_sems, recv_sems = refs[2 * n:]
        x, y, c, chips = _place()
        me_chip = 2 * x + y
        sibling = (x, y, 1 - c)

        def half(ref, pc, rows):
            return ref.at[pl.ds(pc * (rows // 2), rows // 2), :]

        def rcopy(k, src, dst, to):
            return pltpu.make_async_remote_copy(src_ref=src, dst_ref=dst, send_sem=send_sems.at[k], recv_sem=recv_sems.at[k],
                                                device_id=to, device_id_type=MESH)

        sends = []
        for t in range(n):
            rows = shards[t].shape[0]
            for j, chip in enumerate(chips):
                cp = rcopy(6 * t + j, half(in_refs[t], c, rows), half(out_refs[t].at[me_chip], c, rows), (*chip, c))
                cp.start()
                sends.append(cp)
        for t in range(n):
            rows = shards[t].shape[0]
            for j, chip in enumerate(chips):
                slot = out_refs[t].at[2 * chip[0] + chip[1]]
                rcopy(6 * t + j, half(slot, c, rows), half(slot, c, rows), sibling).wait_recv()
                fw = rcopy(6 * t + 3 + j, half(slot, c, rows), half(slot, c, rows), sibling)
                fw.start()
                sends.append(fw)
        for t in range(n):
            rows = shards[t].shape[0]
            for j, chip in enumerate(chips):
                slot = out_refs[t].at[2 * chip[0] + chip[1]]
                rcopy(6 * t + 3 + j, half(slot, 1 - c, rows), half(slot, 1 - c, rows), sibling).wait_recv()
        for cp in sends:
            cp.wait_send()

    return pl.pallas_call(
        body,
        out_shape=tuple(jax.ShapeDtypeStruct((N_CHIPS,) + s.shape, s.dtype) for s in shards),
        in_specs=[ANY] * n,
        out_specs=tuple([ANY] * n),
        scratch_shapes=[pltpu.SemaphoreType.DMA((6 * n,)), pltpu.SemaphoreType.DMA((6 * n,))],
        name="gather_weights",
    )(*shards)


def _pair_exchange(gs, *, name):
    n = len(gs)

    def body(*refs):
        g_refs, r_refs = refs[:n], refs[n:2 * n]
        send_sems, recv_sems = refs[2 * n:]
        x, y, c, _ = _place()
        cps = []
        for t in range(n):
            hr = gs[t].shape[1] // 2
            cp = pltpu.make_async_remote_copy(src_ref=g_refs[t].at[:, pl.ds((1 - c) * hr, hr), :], dst_ref=r_refs[t],
                                              send_sem=send_sems.at[t], recv_sem=recv_sems.at[t],
                                              device_id=(x, y, 1 - c), device_id_type=MESH)
            cp.start()
            cps.append(cp)
        for cp in cps:
            cp.wait()

    return pl.pallas_call(
        body,
        out_shape=tuple(jax.ShapeDtypeStruct((g.shape[0], g.shape[1] // 2, g.shape[2]), g.dtype) for g in gs),
        in_specs=[ANY] * n,
        out_specs=tuple([ANY] * n),
        scratch_shapes=[pltpu.SemaphoreType.DMA((n,)), pltpu.SemaphoreType.DMA((n,))],
        name=name,
    )(*gs)


_HBM = pl.BlockSpec(memory_space=pltpu.HBM)
_SEM = pl.BlockSpec(memory_space=pltpu.SEMAPHORE)
_DATAFLOW = pltpu.SideEffectType.DATAFLOW_SIDE_EFFECTING


def _chip_copies(p_refs, land_refs, send_sems, recv_sems):
    x, y, c, chips = _place()
    me_chip = 2 * x + y
    return [pltpu.make_async_remote_copy(src_ref=p_refs[t].at[2 * chip[0] + chip[1]], dst_ref=land_refs[t].at[me_chip],
                                         send_sem=send_sems.at[3 * t + j], recv_sem=recv_sems.at[3 * t + j],
                                         device_id=(*chip, c), device_id_type=MESH)
            for t in range(len(p_refs)) for j, chip in enumerate(chips)]


def _chip_exchange_start(ps, *, tag):
    n = len(ps)

    def body(*refs):
        p_refs, land_refs = refs[:n], refs[n:2 * n]
        send_sems, recv_sems = refs[2 * n:2 * n + 2]
        token = refs[-1]
        for cp in _chip_copies(p_refs, land_refs, send_sems, recv_sems):
            cp.start()
        token[...] = jnp.zeros_like(token)

    hbm = [pltpu.HBM(p.shape, p.dtype) for p in ps]
    args = [pltpu.with_memory_space_constraint(p, pltpu.HBM) for p in ps]
    args += [pltpu.with_memory_space_constraint(lax.empty(p.shape, p.dtype), pltpu.HBM) for p in ps]
    out = pl.pallas_call(
        body,
        name=f"chip_exchange_start_{tag}",
        out_shape=(pltpu.SemaphoreType.DMA((3 * n,)), pltpu.SemaphoreType.DMA((3 * n,)), *hbm, *hbm,
                   jax.ShapeDtypeStruct((8, LANES), F32)),
        in_specs=[_HBM] * (2 * n),
        out_specs=(_SEM, _SEM, *([_HBM] * (2 * n)), pl.BlockSpec(memory_space=pltpu.VMEM)),
        input_output_aliases={i: 2 + i for i in range(2 * n)},
        compiler_params=pltpu.CompilerParams(has_side_effects=_DATAFLOW),
    )(*args)
    return out[0], out[1], out[2:2 + n], out[2 + n:2 + 2 * n], out[-1]


def _chip_exchange_wait(send_sems, recv_sems, p_thru, land_thru, after, *, tag):
    n = len(p_thru)

    def body(*refs):
        p_refs, land_refs = refs[:n], refs[n:2 * n]
        ssem, rsem = refs[2 * n:2 * n + 2]
        for cp in _chip_copies(p_refs, land_refs, ssem, rsem):
            cp.wait_send()
            cp.wait_recv()

    hbm = [pltpu.HBM(p.shape, p.dtype) for p in p_thru]
    out = pl.pallas_call(
        body,
        name=f"chip_exchange_wait_{tag}",
        out_shape=(*hbm, *hbm),
        in_specs=[_HBM] * (2 * n) + [_SEM, _SEM, ANY],
        out_specs=tuple([_HBM] * (2 * n)),
        input_output_aliases={i: i for i in range(2 * n)},
        compiler_params=pltpu.CompilerParams(has_side_effects=_DATAFLOW),
    )(*p_thru, *land_thru, send_sems, recv_sems, after)
    return out[:n], out[n:]


def _shard_copies(s_refs, land_refs, send_sems, recv_sems):
    x, y, c, chips = _place()
    me_chip = 2 * x + y
    return [pltpu.make_async_remote_copy(src_ref=s_refs[t], dst_ref=land_refs[t].at[me_chip],
                                         send_sem=send_sems.at[3 * t + j], recv_sem=recv_sems.at[3 * t + j],
                                         device_id=(*chip, c), device_id_type=MESH)
            for t in range(len(s_refs)) for j, chip in enumerate(chips)]


def _gather_late_start(shards):
    n = len(shards)

    def body(*refs):
        s_refs, land_refs = refs[:n], refs[n:2 * n]
        send_sems, recv_sems = refs[2 * n:2 * n + 2]
        token = refs[-1]
        for cp in _shard_copies(s_refs, land_refs, send_sems, recv_sems):
            cp.start()
        token[...] = jnp.zeros_like(token)

    lands = [(N_CHIPS,) + s.shape for s in shards]
    args = [pltpu.with_memory_space_constraint(s, pltpu.HBM) for s in shards]
    args += [pltpu.with_memory_space_constraint(lax.empty(shp, s.dtype), pltpu.HBM) for shp, s in zip(lands, shards)]
    out = pl.pallas_call(
        body,
        name="gather_late_start",
        out_shape=(pltpu.SemaphoreType.DMA((3 * n,)), pltpu.SemaphoreType.DMA((3 * n,)),
                   *[pltpu.HBM(s.shape, s.dtype) for s in shards], *[pltpu.HBM(shp, s.dtype) for shp, s in zip(lands, shards)],
                   jax.ShapeDtypeStruct((8, LANES), F32)),
        in_specs=[_HBM] * (2 * n),
        out_specs=(_SEM, _SEM, *([_HBM] * (2 * n)), pl.BlockSpec(memory_space=pltpu.VMEM)),
        input_output_aliases={i: 2 + i for i in range(2 * n)},
        compiler_params=pltpu.CompilerParams(has_side_effects=_DATAFLOW),
    )(*args)
    return out[0], out[1], out[2:2 + n], out[2 + n:2 + 2 * n], out[-1]


def _gather_late_wait(send_sems, recv_sems, s_thru, land_thru, after):
    n = len(s_thru)

    def body(*refs):
        s_refs, land_refs = refs[:n], refs[n:2 * n]
        ssem, rsem = refs[2 * n:2 * n + 2]
        for cp in _shard_copies(s_refs, land_refs, ssem, rsem):
            cp.wait_send()
            cp.wait_recv()

    out = pl.pallas_call(
        body,
        name="gather_late_wait",
        out_shape=(*[pltpu.HBM(s.shape, s.dtype) for s in s_thru], *[pltpu.HBM(l.shape, l.dtype) for l in land_thru]),
        in_specs=[_HBM] * (2 * n) + [_SEM, _SEM, ANY],
        out_specs=tuple([_HBM] * (2 * n)),
        input_output_aliases={i: i for i in range(2 * n)},
        compiler_params=pltpu.CompilerParams(has_side_effects=_DATAFLOW),
    )(*s_thru, *land_thru, send_sems, recv_sems, after)
    return out[:n], out[n:]


def _pair_swap(rs):
    n = len(rs)

    def body(*refs):
        r_refs, o_refs = refs[:n], refs[n:2 * n]
        send_sems, recv_sems = refs[2 * n:]
        x, y, c, _ = _place()
        cps = []
        for t in range(n):
            cp = pltpu.make_async_remote_copy(src_ref=r_refs[t], dst_ref=o_refs[t], send_sem=send_sems.at[t],
                                              recv_sem=recv_sems.at[t], device_id=(x, y, 1 - c), device_id_type=MESH)
            cp.start()
            cps.append(cp)
        for cp in cps:
            cp.wait()

    return pl.pallas_call(
        body,
        out_shape=tuple(jax.ShapeDtypeStruct(r.shape, r.dtype) for r in rs),
        in_specs=[ANY] * n,
        out_specs=tuple([ANY] * n),
        scratch_shapes=[pltpu.SemaphoreType.DMA((n,)), pltpu.SemaphoreType.DMA((n,))],
        name="pair_swap",
    )(*rs)


N_DEV = 8
LOSS_ROW = 4


def _small_allreduce(small):
    def body(s_ref, o_ref, all_ref, send_sems, recv_sems):
        x, y, c, _ = _place()
        me = 4 * x + 2 * y + c
        all_ref[me] = s_ref[...]
        cps = []
        for k in range(1, N_DEV):
            peer = tuple(1 - p if (k >> s) & 1 else p for p, s in ((x, 2), (y, 1), (c, 0)))
            cp = pltpu.make_async_remote_copy(src_ref=s_ref, dst_ref=all_ref.at[me], send_sem=send_sems.at[k - 1],
                                              recv_sem=recv_sems.at[k - 1], device_id=peer, device_id_type=MESH)
            cp.start()
            cps.append(cp)
        for cp in cps:
            cp.wait()
        tot = all_ref[0]
        for d in range(1, N_DEV):
            tot = tot + all_ref[d]
        o_ref[...] = tot
        o_ref[LOSS_ROW:LOSS_ROW + 1, :] = jnp.broadcast_to(jnp.sum(tot[LOSS_ROW:LOSS_ROW + 1, :], axis=1, keepdims=True),
                                                          (1, tot.shape[1]))

    vm = pl.BlockSpec(memory_space=pltpu.VMEM)
    return pl.pallas_call(
        body,
        out_shape=jax.ShapeDtypeStruct(small.shape, small.dtype),
        in_specs=[vm],
        out_specs=vm,
        scratch_shapes=[pltpu.VMEM((N_DEV,) + small.shape, small.dtype), pltpu.SemaphoreType.DMA((N_DEV - 1,)),
                        pltpu.SemaphoreType.DMA((N_DEV - 1,))],
        name="small_allreduce",
    )(small)


def _sum_pair(g, recv, cidx, *, tr, name):
    n, hr, cols = recv.shape
    nr = hr // tr

    def body(c_ref, g_ref, r_ref, o_ref):
        o_ref[...] = (g_ref[...].astype(F32) + r_ref[...].astype(F32)).astype(o_ref.dtype)

    grid_spec = pltpu.PrefetchScalarGridSpec(
        num_scalar_prefetch=1,
        grid=(n, nr),
        in_specs=[pl.BlockSpec((None, tr, cols), lambda k, i, c_ref: (k, c_ref[0] * nr + i, 0)),
                  pl.BlockSpec((None, tr, cols), lambda k, i, c_ref: (k, i, 0))],
        out_specs=pl.BlockSpec((None, tr, cols), lambda k, i, c_ref: (k, i, 0)),
    )
    return pl.pallas_call(body, out_shape=jax.ShapeDtypeStruct(recv.shape, BF16), grid_spec=grid_spec,
                          compiler_params=_cparams(), name=name)(cidx, g, recv)


def _sum_chips(p, *, tr, name):
    _, rows, cols = p.shape

    def body(p_ref, o_ref):
        tot = p_ref[0].astype(F32)
        for k in range(1, N_CHIPS):
            tot = tot + p_ref[k].astype(F32)
        o_ref[...] = tot

    return pl.pallas_call(
        body,
        out_shape=jax.ShapeDtypeStruct((rows, cols), F32),
        grid=(rows // tr,),
        in_specs=[pl.BlockSpec((N_CHIPS, tr, cols), lambda i: (0, i, 0))],
        out_specs=pl.BlockSpec((tr, cols), lambda i: (i, 0)),
        compiler_params=_cparams(),
        name=name,
    )(p)


def _adamw(w, g, m, v, *, tr, name):
    rows, cols = w.shape
    bc1 = 1.0 / (1.0 - ADAM_B1 ** ADAM_STEP)
    bc2 = 1.0 / (1.0 - ADAM_B2 ** ADAM_STEP)

    def body(w_ref, g_ref, m_ref, v_ref, d_ref, nm_ref, nv_ref):
        gv = g_ref[...]
        nm = ADAM_B1 * m_ref[...] + (1.0 - ADAM_B1) * gv
        nv = ADAM_B2 * v_ref[...] + (1.0 - ADAM_B2) * (gv * gv)
        d_ref[...] = -ADAM_LR * ((nm * bc1) / (jnp.sqrt(nv * bc2) + ADAM_EPS) + ADAM_WD * w_ref[...])
        nm_ref[...] = nm
        nv_ref[...] = nv

    spec = pl.BlockSpec((tr, cols), lambda i: (i, 0))
    sd = jax.ShapeDtypeStruct((rows, cols), F32)
    return pl.pallas_call(body, out_shape=(sd, sd, sd), grid=(rows // tr,), in_specs=[spec] * 4, out_specs=(spec,) * 3,
                          compiler_params=_cparams(), name=name)(w, g, m, v)


def _adamw_halves(w, own, sib, cidx, m, v, *, tr, name):
    rows, cols = w.shape
    hr = own.shape[0]
    nr = hr // tr
    assert rows == 2 * hr and hr % tr == 0
    bc1 = 1.0 / (1.0 - ADAM_B1 ** ADAM_STEP)
    bc2 = 1.0 / (1.0 - ADAM_B2 ** ADAM_STEP)

    def body(c_ref, w_ref, o_ref, s_ref, m_ref, v_ref, g_ref, d_ref, nm_ref, nv_ref):
        mine = (pl.program_id(0) // nr) == c_ref[0]
        gv = jnp.where(mine, o_ref[...], s_ref[...])
        nm = ADAM_B1 * m_ref[...] + (1.0 - ADAM_B1) * gv
        nv = ADAM_B2 * v_ref[...] + (1.0 - ADAM_B2) * (gv * gv)
        g_ref[...] = gv
        d_ref[...] = -ADAM_LR * ((nm * bc1) / (jnp.sqrt(nv * bc2) + ADAM_EPS) + ADAM_WD * w_ref[...])
        nm_ref[...] = nm
        nv_ref[...] = nv

    full = pl.BlockSpec((tr, cols), lambda i, c_ref: (i, 0))
    half = pl.BlockSpec((tr, cols), lambda i, c_ref: (i % nr, 0))
    sd = jax.ShapeDtypeStruct((rows, cols), F32)
    grid_spec = pltpu.PrefetchScalarGridSpec(num_scalar_prefetch=1, grid=(rows // tr,), in_specs=[full, half, half, full, full],
                                             out_specs=(full,) * 4)
    return pl.pallas_call(body, out_shape=(sd,) * 4, grid_spec=grid_spec, compiler_params=_cparams(), name=name)(
        cidx, w, own, sib, m, v)


def _pack_small(norm, mem_norm, final_norm, b_forget):
    rows = [norm.reshape(1, D_MODEL), mem_norm.reshape(1, D_MODEL), final_norm.reshape(1, D_MODEL),
            jnp.pad(b_forget.reshape(1, FOX_HEADS), ((0, 0), (0, D_MODEL - FOX_HEADS))), jnp.zeros((4, D_MODEL), F32)]
    return jnp.concatenate(rows, axis=0)


def _unpack_small(a):
    return a[0:1], a[3:4, :FOX_HEADS], a[1:2], a[2]


def kernel(x, mem, norm_g, w_in, b_forget, mem_norm_g, w_mem_kv, w_out, final_norm_g, loss_target, m_norm_g, m_w_in, m_b_forget, m_mem_norm_g, m_w_mem_kv, m_w_out, m_final_norm_g, v_norm_g, v_w_in, v_b_forget, v_mem_norm_g, v_w_mem_kv, v_w_out, v_final_norm_g):
    core = lax.axis_index("c").astype(jnp.int32)
    me_chip = (2 * lax.axis_index("x") + lax.axis_index("y")).astype(jnp.int32)
    cidx = core.reshape(1)

    def own_slot(arr, own):
        return lax.dynamic_update_slice(arr, own[None].astype(arr.dtype), (me_chip,) + (0,) * own.ndim)

    win_b, late = w_in[0].astype(BF16), [w_mem_kv[0].astype(BF16), w_out[0].astype(BF16)]
    g_in, = _gather_weights([win_b])
    g_in, late = lax.optimization_barrier((own_slot(g_in, win_b), late))
    w_r = _rearrange_w_in([g_in[k] for k in range(N_CHIPS)])
    *late_flight, early_token = _gather_late_start(late)

    def late_weights(after):
        shards, landed = _gather_late_wait(*late_flight, after)
        g_kv, g_out = (own_slot(g, s) for g, s in zip(landed, shards))
        return g_kv.reshape(D_MODEL, 2 * MEM_W), g_out.reshape(MIX_W, D_MODEL)

    trs = (128, 128, 256)
    names = ("w_in", "w_mem_kv", "w_out")
    flights = {}

    def exchange(slabs, nms, ts, tag):
        recv = _pair_exchange(slabs, name=f"pair_exchange_{tag}")
        pair = [_sum_pair(g, r, cidx, tr=tr, name=f"sum_pair_{nm}") for g, r, tr, nm in zip(slabs, recv, ts, nms)]
        if tag == "w_in":
            pair[0] = _w_in_grad_slabs(pair[0][0])
        *flights[tag], token = _chip_exchange_start(pair, tag=tag)
        return token

    def start_reduce_small(g_wkv, g_wo):
        slabs = [g_wkv.reshape(N_CHIPS, D_MODEL // N_CHIPS, 2 * MEM_W), g_wo.reshape(N_CHIPS, MIX_W // N_CHIPS, D_MODEL)]
        return exchange(slabs, names[1:], trs[1:], "small")

    def start_reduce(g_wr):
        return exchange([g_wr[None]], names[:1], trs[:1], "w_in")

    gx, g_wr, g_wkv, g_wo, small = _local_grads(x, mem, norm_g, w_r, b_forget, mem_norm_g, None, None, final_norm_g, loss_target,
                                                start_reduce=start_reduce, start_reduce_small=start_reduce_small,
                                                early_token=early_token, late_weights=late_weights)

    pair, landed = [], []
    for tag in ("w_in", "small"):
        p, l = _chip_exchange_wait(*flights[tag], small, tag=tag)
        pair += list(p)
        landed += list(l)
    got = [lax.dynamic_update_slice(g, lax.dynamic_slice(p, (me_chip, 0, 0), (1,) + p.shape[1:]), (me_chip, 0, 0))
           for g, p in zip(landed, pair)]
    red = [_sum_chips(p, tr=tr, name=f"sum_chips_{nm}") for p, tr, nm in zip(got, trs, names)]
    sib = _pair_swap(red)

    outs = {}
    for nm, r, s, w, m, v, tr in zip(names, red, sib, (w_in, w_mem_kv, w_out), (m_w_in, m_w_mem_kv, m_w_out),
                                     (v_w_in, v_w_mem_kv, v_w_out), trs):
        outs[nm] = tuple(a[None] for a in _adamw_halves(w[0], r, s, cidx, m[0], v[0], tr=tr, name=f"adamw_{nm}"))

    gsum = _small_allreduce(small)
    sd, sm, sv = _adamw(_pack_small(norm_g, mem_norm_g, final_norm_g, b_forget), gsum,
                        _pack_small(m_norm_g, m_mem_norm_g, m_final_norm_g, m_b_forget),
                        _pack_small(v_norm_g, v_mem_norm_g, v_final_norm_g, v_b_forget), tr=8, name="adamw_small")
    loss = gsum[LOSS_ROW, 0]

    def group(i, small_arr):
        ng, bf, mg, fg = _unpack_small(small_arr)
        return (ng, outs["w_in"][i], bf, mg, outs["w_mem_kv"][i], outs["w_out"][i], fg)

    return (loss, gx, *group(0, gsum), *group(1, sd), *group(2, sm), *group(3, sv))
```

```python
import functools
import math

import jax
import jax.numpy as jnp
from jax import lax
from jax.experimental import pallas as pl
from jax.experimental.pallas import tpu as pltpu

F32 = jnp.float32
BF16 = jnp.bfloat16

D_MODEL = 1024
SEQ = 2048
HEAD_DIM = 64
FOX_HEADS = 12
DIL_HEADS = 12
MEM_HEADS = 4
MEM_HEAD_DIM = 128
MEM_LEN = 256
FOX_W = FOX_HEADS * HEAD_DIM
DIL_W = DIL_HEADS * HEAD_DIM
MEM_W = MEM_HEADS * MEM_HEAD_DIM
MIX_W = FOX_W + DIL_W + MEM_W
DILATIONS = ((128, 1), (512, 4), (2048, 16))
ROPE_THETA = 500000.0
ROPE_DIM = HEAD_DIM // 4
RMS_EPS = 1e-6
NEG_INF = -1e30
IN_SIZES = [FOX_W] * 4 + [FOX_HEADS] + [DIL_W] * 4 + [MEM_W] * 2
IN_W = sum(IN_SIZES)

ADAM_LR = 0.001
ADAM_B1 = 0.9
ADAM_B2 = 0.999
ADAM_EPS = 1e-08
ADAM_WD = 0.01
ADAM_STEP = 10

LANES = 128
N_CHIPS = 4
PW = 7168
PWF = PW + 4 * LANES
C_FQ, C_FK, C_FV, C_FG = 0, 768, 1536, 2304
C_DQ, C_DK, C_DV, C_DG = 3072, 3840, 4608, 5376
C_MQ, C_MG = 6144, 6656
VMEM_LIMIT = 48 * 1024 * 1024


def _cparams(**kw):
    return pltpu.CompilerParams(vmem_limit_bytes=VMEM_LIMIT, **kw)


def _matmul(a, b, *, out_dtype, tm, tn, tk, name, mode="nn"):
    if mode == "tn":
        (kdim, m), n = a.shape, b.shape[1]
        a_spec = pl.BlockSpec((tk, tm), lambda i, j, k: (k, i))
        b_spec = pl.BlockSpec((tk, tn), lambda i, j, k: (k, j))
        dims = _T0
    elif mode == "nt":
        (m, kdim), n = a.shape, b.shape[0]
        a_spec = pl.BlockSpec((tm, tk), lambda i, j, k: (i, k))
        b_spec = pl.BlockSpec((tn, tk), lambda i, j, k: (j, k))
        dims = _NT
    else:
        (m, kdim), n = a.shape, b.shape[1]
        a_spec = pl.BlockSpec((tm, tk), lambda i, j, k: (i, k))
        b_spec = pl.BlockSpec((tk, tn), lambda i, j, k: (k, j))
        dims = (((1,), (0,)), ((), ()))
    nk = kdim // tk
    assert m % tm == 0 and n % tn == 0 and kdim % tk == 0

    def body(a_ref, b_ref, o_ref, *scratch):
        prod = lax.dot_general(a_ref[...], b_ref[...], dims, preferred_element_type=F32)
        if nk == 1:
            o_ref[...] = prod.astype(o_ref.dtype)
            return
        acc_ref, = scratch
        k = pl.program_id(2)

        @pl.when(k == 0)
        def _():
            acc_ref[...] = prod

        @pl.when(k > 0)
        def _():
            acc_ref[...] += prod

        @pl.when(k == nk - 1)
        def _():
            o_ref[...] = acc_ref[...].astype(o_ref.dtype)

    return pl.pallas_call(
        body,
        out_shape=jax.ShapeDtypeStruct((m, n), out_dtype),
        grid=(m // tm, n // tn, nk),
        in_specs=[a_spec, b_spec],
        out_specs=pl.BlockSpec((tm, tn), lambda i, j, k: (i, j)),
        scratch_shapes=[pltpu.VMEM((tm, tn), F32)] if nk > 1 else [],
        compiler_params=_cparams(dimension_semantics=("parallel", "parallel", "arbitrary")),
        name=name,
    )(a, b)


def _rms_fwd(x, g, *, tm, name):
    t, d = x.shape

    def body(x_ref, g_ref, h_ref):
        xv = x_ref[...]
        r = lax.rsqrt(jnp.mean(xv * xv, axis=-1, keepdims=True) + RMS_EPS)
        h_ref[...] = (xv * r * g_ref[...]).astype(h_ref.dtype)

    return pl.pallas_call(
        body,
        out_shape=jax.ShapeDtypeStruct((t, d), BF16),
        grid=(t // tm,),
        in_specs=[pl.BlockSpec((tm, d), lambda i: (i, 0)), pl.BlockSpec((1, d), lambda i: (0, 0))],
        out_specs=pl.BlockSpec((tm, d), lambda i: (i, 0)),
        compiler_params=_cparams(),
        name=name,
    )(x, g)


def _rope_tables():
    half = ROPE_DIM // 2
    pos = jnp.arange(SEQ, dtype=F32)
    inv_freq = 1.0 / (ROPE_THETA ** (jnp.arange(0, ROPE_DIM, 2, dtype=F32) / ROPE_DIM))
    ang = pos[:, None] * inv_freq[None, :]
    cos, sin = jnp.cos(ang), jnp.sin(ang)
    one = jnp.ones((SEQ, HEAD_DIM - ROPE_DIM), F32)
    zero = jnp.zeros((SEQ, HEAD_DIM - ROPE_DIM), F32)
    zh = jnp.zeros((SEQ, half), F32)
    c = jnp.concatenate([cos, cos, one], axis=1)
    s1 = jnp.concatenate([zh, sin, zero], axis=1)
    s2 = jnp.concatenate([-sin, zh, zero], axis=1)
    rep = LANES // HEAD_DIM
    return jnp.tile(c, (1, rep)), jnp.tile(s1, (1, rep)), jnp.tile(s2, (1, rep))


def _rope_apply(t, c, s1, s2, transpose=False):
    n = t.shape[-1]
    rep = n // LANES
    c, s1, s2 = (jnp.tile(u, (1, rep)) for u in (c, s1, s2))
    half = ROPE_DIM // 2
    if not transpose:
        return t * c + pltpu.roll(t, half, 1) * s1 + pltpu.roll(t, n - half, 1) * s2
    return t * c + pltpu.roll(t * s1, n - half, 1) + pltpu.roll(t * s2, half, 1)


def _proj(x, g, w, tabs, *, n, tm, tn, name):
    t, d = x.shape
    assert C_DQ % tn == 0 and (C_DV - C_DQ) % tn == 0 and (C_DG - C_DQ) % tn == 0
    rope_lo, rope_hi, dil_hi = C_DQ // tn, C_DV // tn, C_DG // tn
    flog_blk, flog_at = PW // tn, PW % tn
    assert flog_at % LANES == 0 and flog_at + LANES <= tn
    s_blocks = SEQ // tm

    def body(x_ref, g_ref, w_ref, c_ref, s1_ref, s2_ref, h_ref, o_ref, f_ref, fl_ref, h_scr):
        j = pl.program_id(1)

        @pl.when(j == 0)
        def _():
            xv = x_ref[...]
            r = lax.rsqrt(jnp.mean(xv * xv, axis=-1, keepdims=True) + RMS_EPS)
            hv = (xv * r * g_ref[...]).astype(BF16)
            h_scr[...] = hv
            h_ref[...] = hv

        acc = jnp.dot(h_scr[...], w_ref[...], preferred_element_type=F32)
        is_rope = jnp.logical_and(j >= rope_lo, j < rope_hi)

        @pl.when(j == flog_blk)
        def _():
            fl_ref[...] = acc[:, flog_at:flog_at + LANES]

        @pl.when(is_rope)
        def _():
            r = _rope_apply(acc, c_ref[...], s1_ref[...], s2_ref[...])
            o_ref[...] = r.astype(o_ref.dtype)
            f_ref[...] = r

        @pl.when(jnp.logical_not(is_rope))
        def _():
            o_ref[...] = acc.astype(o_ref.dtype)

        @pl.when(jnp.logical_and(j >= rope_hi, j < dil_hi))
        def _():
            f_ref[...] = acc

    tab_spec = pl.BlockSpec((tm, LANES), lambda i, j: (i % s_blocks, 0))
    f_spec = pl.BlockSpec((tm, tn), lambda i, j: (i, jnp.clip(j - rope_lo, 0, dil_hi - rope_lo - 1)))
    row = pl.BlockSpec((tm, d), lambda i, j: (i, 0))
    return pl.pallas_call(
        body,
        out_shape=(jax.ShapeDtypeStruct((t, d), BF16), jax.ShapeDtypeStruct((t, n), BF16),
                   jax.ShapeDtypeStruct((t, 3 * DIL_W), F32), jax.ShapeDtypeStruct((t, LANES), F32)),
        grid=(t // tm, n // tn),
        in_specs=[row, pl.BlockSpec((1, d), lambda i, j: (0, 0)), pl.BlockSpec((d, tn), lambda i, j: (0, j)),
                  tab_spec, tab_spec, tab_spec],
        out_specs=(row, pl.BlockSpec((tm, tn), lambda i, j: (i, j)), f_spec, pl.BlockSpec((tm, LANES), lambda i, j: (i, 0))),
        scratch_shapes=[pltpu.VMEM((tm, d), BF16)],
        compiler_params=_cparams(dimension_semantics=("parallel", "arbitrary")),
        name=name,
    )(x, g, w, *tabs)


def _split3(x):
    hi = x.astype(BF16)
    r1 = x - hi.astype(F32)
    mid = r1.astype(BF16)
    lo = (r1 - mid.astype(F32)).astype(BF16)
    return hi, mid, lo


def _dot3(sel, x, sel_is_lhs):
    out = None
    for piece in _split3(x):
        t = jnp.dot(sel, piece, preferred_element_type=F32) if sel_is_lhs else jnp.dot(piece, sel, preferred_element_type=F32)
        out = t if out is None else out + t
    return out


def _flog_fwd(flog, bpad, *, nb, ts, name):
    ns = SEQ // ts

    def body(f_ref, b_ref, c_ref, carry_ref):
        s = pl.program_id(1)

        @pl.when(s == 0)
        def _():
            carry_ref[...] = jnp.zeros_like(carry_ref)

        z = f_ref[...] + b_ref[...]
        logf = jnp.minimum(z, 0.0) - jnp.log(1.0 + jnp.exp(-jnp.abs(z)))
        r = lax.broadcasted_iota(jnp.int32, (ts, ts), 0)
        c = lax.broadcasted_iota(jnp.int32, (ts, ts), 1)
        tri = jnp.where(r >= c, 1.0, 0.0).astype(BF16)
        cs = _dot3(tri, logf, True) + carry_ref[0:1, :]
        carry_ref[...] = jnp.broadcast_to(cs[ts - 1:ts, :], carry_ref.shape)
        c_ref[...] = cs

    return pl.pallas_call(
        body,
        out_shape=jax.ShapeDtypeStruct((nb * SEQ, LANES), F32),
        grid=(nb, ns),
        in_specs=[pl.BlockSpec((ts, LANES), lambda b, s: (b * ns + s, 0)), pl.BlockSpec((1, LANES), lambda b, s: (0, 0))],
        out_specs=pl.BlockSpec((ts, LANES), lambda b, s: (b * ns + s, 0)),
        scratch_shapes=[pltpu.VMEM((8, LANES), F32)],
        compiler_params=_cparams(dimension_semantics=("parallel", "arbitrary")),
        name=name,
    )(flog, bpad)


def _flog_bwd(dcol, flog, bpad, *, nb, ts, name):
    ns = SEQ // ts

    def body(d_ref, f_ref, b_ref, o_ref, gb_ref, carry_ref):
        bi = pl.program_id(0)
        s = pl.program_id(1)

        @pl.when(s == 0)
        def _():
            carry_ref[...] = jnp.zeros_like(carry_ref)

        @pl.when(jnp.logical_and(bi == 0, s == 0))
        def _():
            gb_ref[...] = jnp.zeros_like(gb_ref)

        r = lax.broadcasted_iota(jnp.int32, (ts, ts), 0)
        c = lax.broadcasted_iota(jnp.int32, (ts, ts), 1)
        tri = jnp.where(r <= c, 1.0, 0.0).astype(BF16)
        rc = _dot3(tri, d_ref[...], True) + carry_ref[0:1, :]
        carry_ref[...] = jnp.broadcast_to(rc[0:1, :], carry_ref.shape)
        z = f_ref[...] + b_ref[...]
        dz = rc / (1.0 + jnp.exp(z))
        o_ref[...] = dz.astype(o_ref.dtype)
        gb_ref[...] += jnp.broadcast_to(jnp.sum(dz, axis=0, keepdims=True), gb_ref.shape)

    rev = lambda b, s: (b * ns + (ns - 1 - s), 0)
    return pl.pallas_call(
        body,
        out_shape=(jax.ShapeDtypeStruct((nb * SEQ, LANES), BF16), jax.ShapeDtypeStruct((8, LANES), F32)),
        grid=(nb, ns),
        in_specs=[pl.BlockSpec((ts, LANES), rev), pl.BlockSpec((ts, LANES), rev), pl.BlockSpec((1, LANES), lambda b, s: (0, 0))],
        out_specs=(pl.BlockSpec((ts, LANES), rev), pl.BlockSpec((8, LANES), lambda b, s: (0, 0))),
        scratch_shapes=[pltpu.VMEM((8, LANES), F32)],
        compiler_params=_cparams(dimension_semantics=("arbitrary", "arbitrary")),
        name=name,
    )(dcol, flog, bpad)


MEM_TQ = 256
MEM_SET = 4
MEM_SCALE = 1.0 / math.sqrt(MEM_HEAD_DIM)
assert MEM_HEAD_DIM == LANES and SEQ % (MEM_TQ * MEM_SET) == 0


def _head_masks(nh):
    lane = lax.broadcasted_iota(jnp.int32, (1, LANES), 1)
    return [None] if nh == 1 else [lane < HEAD_DIM, lane >= HEAD_DIM]


def _mem_specs(qoff):
    qspec = pl.BlockSpec((None, SEQ, LANES), lambda b, j: (b, 0, qoff + j))
    kspec = pl.BlockSpec((None, MEM_LEN, LANES), lambda b, j: (b, 0, j))
    vspec = pl.BlockSpec((None, MEM_LEN, LANES), lambda b, j: (b, 0, MEM_HEADS + j))
    ospec = pl.BlockSpec((None, SEQ, LANES), lambda b, j: (b, 0, j))
    return qspec, kspec, vspec, ospec


def _mem_rows(g):
    return [pl.ds(pl.multiple_of((MEM_SET * g + a) * MEM_TQ, MEM_TQ), MEM_TQ) for a in range(MEM_SET)]


def _mem_fwd(p3, mkv3, *, qoff, name):
    nb = p3.shape[0]

    def body(q_ref, k_ref, v_ref, o_ref, lse_ref):
        kb, vb = k_ref[...], v_ref[...]

        def qset(g, c):
            rows = _mem_rows(g)
            ss = [lax.dot_general(q_ref[r, :] * MEM_SCALE, kb, _NT, preferred_element_type=F32) for r in rows]
            for r, s in zip(rows, ss):
                m = jnp.max(s, axis=1, keepdims=True)
                p = jnp.exp(s - m)
                l = jnp.sum(p, axis=1, keepdims=True)
                o_ref[r, :] = jnp.dot(p.astype(BF16), vb, preferred_element_type=F32) / l
                lse_ref[r, :] = jnp.broadcast_to(m + jnp.log(l), (MEM_TQ, LANES))
            return c

        lax.fori_loop(0, SEQ // MEM_TQ // MEM_SET, qset, 0)

    qspec, kspec, vspec, ospec = _mem_specs(qoff)
    osd = jax.ShapeDtypeStruct((nb, SEQ, MEM_W), F32)
    return pl.pallas_call(body, out_shape=(osd, osd), grid=(nb, MEM_HEADS), in_specs=[qspec, kspec, vspec],
                          out_specs=(ospec, ospec), compiler_params=_cparams(dimension_semantics=("parallel", "parallel")),
                          name=name)(p3, mkv3, mkv3)


def _mem_bwd(p3, mkv3, do, o, lse, *, qoff, do_off, name):
    nb = p3.shape[0]

    def body(q_ref, k_ref, v_ref, do_ref, o_ref, lse_ref, dq_ref, dk_ref, dv_ref):
        kb, vb = k_ref[...], v_ref[...]
        ks = kb * MEM_SCALE

        def qset(g, carry):
            dk, dv = carry
            work = []
            for r in _mem_rows(g):
                qs = q_ref[r, :] * MEM_SCALE
                dob = do_ref[r, :].astype(BF16)
                s = lax.dot_general(qs, kb, _NT, preferred_element_type=F32)
                dp = lax.dot_general(dob, vb, _NT, preferred_element_type=F32)
                work.append((r, qs, dob, s, dp))
            for r, qs, dob, s, dp in work:
                delta = jnp.sum(dob.astype(F32) * o_ref[r, :], axis=1, keepdims=True)
                p = jnp.exp(s - lse_ref[r, :][:, 0:1])
                ds = (p * (dp - delta)).astype(BF16)
                dq_ref[r, :] = jnp.dot(ds, ks, preferred_element_type=F32).astype(dq_ref.dtype)
                dk = dk + lax.dot_general(ds, qs, _T0, preferred_element_type=F32)
                dv = dv + lax.dot_general(p.astype(BF16), dob, _T0, preferred_element_type=F32)
            return dk, dv

        z = jnp.zeros((MEM_LEN, LANES), F32)
        dk, dv = lax.fori_loop(0, SEQ // MEM_TQ // MEM_SET, qset, (z, z))
        dk_ref[...] = dk
        dv_ref[...] = dv

    qspec, kspec, vspec, ospec = _mem_specs(qoff)
    dospec = pl.BlockSpec((None, SEQ, LANES), lambda b, j: (b, 0, do_off + j))
    kvo = pl.BlockSpec((None, MEM_LEN, LANES), lambda b, j: (b, 0, j))
    kvsd = jax.ShapeDtypeStruct((nb, MEM_LEN, MEM_W), F32)
    return pl.pallas_call(
        body, out_shape=(jax.ShapeDtypeStruct((nb, SEQ, MEM_W), BF16), kvsd, kvsd), grid=(nb, MEM_HEADS),
        in_specs=[qspec, kspec, vspec, dospec, ospec, ospec], out_specs=(ospec, kvo, kvo),
        compiler_params=_cparams(dimension_semantics=("parallel", "parallel")), name=name)(p3, mkv3, mkv3, do, o, lse)


BLK = 128
NBLK = SEQ // BLK
QK_SCALE = 1.0 / math.sqrt(HEAD_DIM)
DIL_STEPS = tuple(d for _, d in DILATIONS)
assert all(w // d == BLK for w, d in DILATIONS)
_T0 = (((0,), (0,)), ((), ()))
_NT = (((1,), (1,)), ((), ()))


def _stack_heads(a, masks):
    z = jnp.zeros_like(a)
    return jnp.concatenate([jnp.where(masks[0], a, z), jnp.where(masks[1], a, z)], axis=0)


def _tri_bias(lower):
    r = lax.broadcasted_iota(jnp.int32, (BLK, BLK), 0)
    c = lax.broadcasted_iota(jnp.int32, (BLK, BLK), 1)
    return jnp.where((c <= r) if lower else (c >= r), 0.0, NEG_INF).astype(F32)


def _dil_rows(r, i, d):
    start = r + i * (BLK * d)
    return pl.ds(start, BLK) if d == 1 else pl.ds(start, BLK, stride=d)


DIL_SET = 4


def _dil_sets(d, fn):
    nbk = SEQ // d // BLK
    if d == 1:
        n = 2 * DIL_SET
        def gbody(g, c):
            fn([(0, n * g + a, None if a == 0 else True) for a in range(n)])
            return c
        lax.fori_loop(0, nbk // n, gbody, 0)
    elif nbk > 1:
        assert nbk == DIL_SET
        def rbody(r, c):
            fn([(r, i, i > 0) for i in range(nbk)])
            return c
        lax.fori_loop(0, d, rbody, 0)
    else:
        def rbody(rr, c):
            fn([(DIL_SET * rr + a, 0, False) for a in range(DIL_SET)])
            return c
        lax.fori_loop(0, d // DIL_SET, rbody, 0)


def _dil_key_tiles(r, i, d, has_prev, qrows, tri_cur, tri_prev):
    tiles = [(qrows, tri_cur)]
    if has_prev is None:
        tiles.append((_dil_rows(r, jnp.maximum(i - 1, 0), d), tri_prev + jnp.where(i > 0, 0.0, NEG_INF)))
    elif has_prev:
        tiles.append((_dil_rows(r, i - 1, d), tri_prev))
    return tiles


def _dil_fwd(qkv, *, name):
    nb = qkv.shape[0]
    ncol = DIL_W // LANES
    hd = HEAD_DIM

    def body(q_ref, k_ref, v_ref, o_ref, lse_ref, m_ref, l_ref, a_ref):
        masks = _head_masks(2)
        tri_cur, tri_prev = _tri_bias(True), _tri_bias(False)
        for pi, d in enumerate(DIL_STEPS):
            first, last = pi == 0, pi == len(DIL_STEPS) - 1

            def qset(blocks, d=d, first=first, last=last):
                work = []
                for r, i, has_prev in blocks:
                    qrows = _dil_rows(r, i, d)
                    qcat = _stack_heads((q_ref[qrows, :] * QK_SCALE).astype(BF16), masks)
                    ss, krs = [], []
                    for krows, bias in _dil_key_tiles(r, i, d, has_prev, qrows, tri_cur, tri_prev):
                        s = lax.dot_general(qcat, k_ref[krows, :].astype(BF16), _NT, preferred_element_type=F32)
                        ss.append((s[:BLK] + bias, s[BLK:] + bias))
                        krs.append(krows)
                    work.append((qrows, ss, krs))
                for qrows, ss, krs in work:
                    e0 = ss[0][0] if len(ss) == 1 else jnp.maximum(ss[0][0], ss[1][0])
                    e1 = ss[0][1] if len(ss) == 1 else jnp.maximum(ss[0][1], ss[1][1])
                    n0 = jnp.max(e0, axis=1, keepdims=True)
                    n1 = jnp.max(e1, axis=1, keepdims=True)
                    if not first:
                        mo, lo = m_ref[qrows, :], l_ref[qrows, :]
                        m0, m1 = mo[:, 0:1], mo[:, hd:hd + 1]
                        n0, n1 = jnp.maximum(n0, m0), jnp.maximum(n1, m1)
                        a0, a1 = jnp.exp(m0 - n0), jnp.exp(m1 - n1)
                    ps = [(jnp.exp(s0 - n0), jnp.exp(s1 - n1)) for s0, s1 in ss]
                    t0 = ps[0][0] if len(ps) == 1 else ps[0][0] + ps[1][0]
                    t1 = ps[0][1] if len(ps) == 1 else ps[0][1] + ps[1][1]
                    l0 = jnp.sum(t0, axis=1, keepdims=True)
                    l1 = jnp.sum(t1, axis=1, keepdims=True)
                    acc = None
                    for (p0, p1), krows in zip(ps, krs):
                        vcat = _stack_heads(v_ref[krows, :].astype(BF16), masks)
                        pv = jnp.dot(jnp.concatenate([p0, p1], axis=1).astype(BF16), vcat, preferred_element_type=F32)
                        acc = pv if acc is None else acc + pv
                    if not first:
                        l0 = l0 + a0 * lo[:, 0:1]
                        l1 = l1 + a1 * lo[:, hd:hd + 1]
                        acc = acc + a_ref[qrows, :] * jnp.where(masks[0], a0, a1)
                    if last:
                        o_ref[qrows, :] = acc / jnp.where(masks[0], l0, l1)
                        lse_ref[qrows, :] = jnp.where(masks[0], n0 + jnp.log(l0), n1 + jnp.log(l1))
                    else:
                        m_ref[qrows, :] = jnp.where(masks[0], n0, n1)
                        l_ref[qrows, :] = jnp.where(masks[0], l0, l1)
                        a_ref[qrows, :] = acc

            _dil_sets(d, qset)

    spec = lambda off: pl.BlockSpec((None, SEQ, LANES), lambda b, j: (b, 0, off + j))
    ospec = pl.BlockSpec((None, SEQ, LANES), lambda b, j: (b, 0, j))
    osd = jax.ShapeDtypeStruct((nb, SEQ, DIL_W), F32)
    return pl.pallas_call(
        body, out_shape=(osd, osd), grid=(nb, ncol),
        in_specs=[spec(0), spec(ncol), spec(2 * ncol)], out_specs=(ospec, ospec),
        scratch_shapes=[pltpu.VMEM((SEQ, LANES), F32)] * 3,
        compiler_params=_cparams(dimension_semantics=("parallel", "parallel")), name=name,
    )(qkv, qkv, qkv)


def _dil_bwd(qkv, do, o, lse, tabs, *, do_off, name):
    nb = qkv.shape[0]
    ncol = DIL_W // LANES
    hd = HEAD_DIM

    def body(q_ref, k_ref, v_ref, do_ref, o_ref, lse_ref, c_ref, s1_ref, s2_ref, dqo_ref, dko_ref, dvo_ref,
             dq_ref, dk_ref, dv_ref, dl_ref, dof_ref):
        masks = _head_masks(2)
        tri_cur, tri_prev = _tri_bias(True), _tri_bias(False)
        dq_ref[...] = jnp.zeros_like(dq_ref)
        dk_ref[...] = jnp.zeros_like(dk_ref)
        dv_ref[...] = jnp.zeros_like(dv_ref)

        def delta_body(i, c):
            rows = pl.ds(pl.multiple_of(i * BLK, BLK), BLK)
            dof = do_ref[rows, :].astype(F32)
            dof_ref[rows, :] = dof
            prod = dof * o_ref[rows, :]
            z = jnp.zeros_like(prod)
            dl_ref[rows, :] = jnp.where(masks[0], jnp.sum(jnp.where(masks[0], prod, z), axis=1, keepdims=True),
                                        jnp.sum(jnp.where(masks[1], prod, z), axis=1, keepdims=True))
            return c

        lax.fori_loop(0, NBLK, delta_body, 0)

        for d in DIL_STEPS:
            def qset(blocks, d=d):
                work = []
                for r, i, has_prev in blocks:
                    qrows = _dil_rows(r, i, d)
                    qcat = _stack_heads((q_ref[qrows, :] * QK_SCALE).astype(BF16), masks)
                    docat = _stack_heads(dof_ref[qrows, :].astype(BF16), masks)
                    tiles = []
                    for krows, bias in _dil_key_tiles(r, i, d, has_prev, qrows, tri_cur, tri_prev):
                        s = lax.dot_general(qcat, k_ref[krows, :].astype(BF16), _NT, preferred_element_type=F32)
                        dp = lax.dot_general(docat, v_ref[krows, :].astype(BF16), _NT, preferred_element_type=F32)
                        tiles.append((krows, s, dp, bias))
                    work.append((qrows, qcat, docat, tiles))
                for qrows, qcat, docat, tiles in work:
                    lseb, dlb = lse_ref[qrows, :], dl_ref[qrows, :]
                    lse0, lse1 = lseb[:, 0:1], lseb[:, hd:hd + 1]
                    dl0, dl1 = dlb[:, 0:1], dlb[:, hd:hd + 1]
                    dq = None
                    for krows, s, dp, bias in tiles:
                        p0 = jnp.exp(s[:BLK] + bias - lse0)
                        p1 = jnp.exp(s[BLK:] + bias - lse1)
                        ds0 = p0 * (dp[:BLK] - dl0)
                        ds1 = p1 * (dp[BLK:] - dl1)
                        ds0b, ds1b = ds0.astype(BF16), ds1.astype(BF16)
                        pcat = jnp.concatenate([p0.astype(BF16), p1.astype(BF16)], axis=0)
                        dscat = jnp.concatenate([ds0b, ds1b], axis=0)
                        dv_ref[krows, :] += lax.dot_general(pcat, docat, _T0, preferred_element_type=F32)
                        dk_ref[krows, :] += lax.dot_general(dscat, qcat, _T0, preferred_element_type=F32)
                        dsrow = jnp.concatenate([ds0b, ds1b], axis=1)
                        kcat = _stack_heads((k_ref[krows, :] * QK_SCALE).astype(BF16), masks)
                        t = jnp.dot(dsrow, kcat, preferred_element_type=F32)
                        dq = t if dq is None else dq + t
                    dq_ref[qrows, :] += dq

            _dil_sets(d, qset)

        def out_body(i, c):
            rows = pl.ds(pl.multiple_of(i * BLK, BLK), BLK)
            tab = (c_ref[rows, :], s1_ref[rows, :], s2_ref[rows, :])
            dqo_ref[rows, :] = _rope_apply(dq_ref[rows, :], *tab, transpose=True).astype(dqo_ref.dtype)
            dko_ref[rows, :] = _rope_apply(dk_ref[rows, :], *tab, transpose=True).astype(dko_ref.dtype)
            dvo_ref[rows, :] = dv_ref[rows, :].astype(dvo_ref.dtype)
            return c

        lax.fori_loop(0, NBLK, out_body, 0)

    spec = lambda off: pl.BlockSpec((None, SEQ, LANES), lambda b, j: (b, 0, off + j))
    ospec = pl.BlockSpec((None, SEQ, LANES), lambda b, j: (b, 0, j))
    tspec = pl.BlockSpec((SEQ, LANES), lambda b, j: (0, 0))
    osd = jax.ShapeDtypeStruct((nb, SEQ, DIL_W), BF16)
    return pl.pallas_call(
        body, out_shape=(osd, osd, osd), grid=(nb, ncol),
        in_specs=[spec(0), spec(ncol), spec(2 * ncol), spec(do_off), ospec, ospec, tspec, tspec, tspec],
        out_specs=(ospec, ospec, ospec),
        scratch_shapes=[pltpu.VMEM((SEQ, LANES), F32)] * 5,
        compiler_params=_cparams(dimension_semantics=("parallel", "parallel")), name=name,
    )(qkv, qkv, qkv, do, o, lse, *tabs)


FOX_GROUP = 4
assert NBLK % FOX_GROUP == 0
_FOX_COLS = tuple(c // LANES for c in (C_FQ, C_FK, C_FV))


def _fox_specs():
    cols = [pl.BlockSpec((None, SEQ, LANES), (lambda b, j, off=off: (b, 0, off + j))) for off in _FOX_COLS]
    ospec = pl.BlockSpec((None, SEQ, LANES), lambda b, j: (b, 0, j))
    crspec = pl.BlockSpec((None, None, NBLK, 8, BLK), lambda b, j: (b, j, 0, 0, 0))
    return cols, ospec, crspec


def _fox_key_rows(t, e):
    return pl.ds(pl.multiple_of((FOX_GROUP * t + e) * BLK, BLK), BLK)


def _fox_fwd(p3, crow, *, name):
    nb = p3.shape[0]
    g = FOX_GROUP

    def body(q_ref, k_ref, v_ref, cr_ref, o_ref, lse_ref):
        masks = _head_masks(2)
        tri = _tri_bias(True)

        def qk(qcat, t):
            return tuple(lax.dot_general(qcat, k_ref[_fox_key_rows(t, e), :], _NT, preferred_element_type=F32) for e in range(g))

        def consume(ss, t, state, nblk, diag):
            m0, m1, l0, l1, acc = state
            us = []
            for e in range(nblk):
                cr = cr_ref[g * t + e]
                u0 = ss[e][:BLK] - cr[0:1, :]
                u1 = ss[e][BLK:] - cr[1:2, :]
                if diag and e == nblk - 1:
                    u0, u1 = u0 + tri, u1 + tri
                us.append((u0, u1))
            x0 = functools.reduce(jnp.maximum, [u[0] for u in us])
            x1 = functools.reduce(jnp.maximum, [u[1] for u in us])
            n0 = jnp.maximum(m0, jnp.max(x0, axis=1, keepdims=True))
            n1 = jnp.maximum(m1, jnp.max(x1, axis=1, keepdims=True))
            a0, a1 = jnp.exp(m0 - n0), jnp.exp(m1 - n1)
            acc = acc * jnp.where(masks[0], a0, a1)
            t0 = t1 = None
            for e in range(nblk):
                p0, p1 = jnp.exp(us[e][0] - n0), jnp.exp(us[e][1] - n1)
                t0 = p0 if t0 is None else t0 + p0
                t1 = p1 if t1 is None else t1 + p1
                pcat = jnp.concatenate([p0, p1], axis=1)
                hi = pcat.astype(BF16)
                lo = (pcat - hi.astype(F32)).astype(BF16)
                vcat = _stack_heads(v_ref[_fox_key_rows(t, e), :], masks)
                acc = acc + jnp.dot(hi, vcat, preferred_element_type=F32) + jnp.dot(lo, vcat, preferred_element_type=F32)
            l0 = a0 * l0 + jnp.sum(t0, axis=1, keepdims=True)
            l1 = a1 * l1 + jnp.sum(t1, axis=1, keepdims=True)
            return n0, n1, l0, l1, acc

        def gbody(ng, c):
            neg = jnp.full((BLK, 1), NEG_INF, F32)
            z1 = jnp.zeros((BLK, 1), F32)
            rows = [pl.ds(pl.multiple_of((g * ng + a) * BLK, BLK), BLK) for a in range(g)]
            qcats = [_stack_heads(q_ref[rows[a], :] * QK_SCALE, masks) for a in range(g)]
            first = [qk(qcats[a], 0) for a in range(g)]
            done = []
            for a in range(g):
                def step(t, cc, qcat=qcats[a]):
                    ss, st = cc
                    nxt = qk(qcat, t + 1)
                    return nxt, consume(ss, t, st, g, False)

                done.append(lax.fori_loop(0, ng, step, (first[a], (neg, neg, z1, z1, jnp.zeros((BLK, LANES), F32)))))
            for a in range(g):
                ss, state = done[a]
                m0, m1, l0, l1, acc = consume(ss, ng, state, a + 1, True)
                o_ref[rows[a], :] = acc / jnp.where(masks[0], l0, l1)
                lse_ref[rows[a], :] = jnp.where(masks[0], m0 + jnp.log(l0), m1 + jnp.log(l1))
            return c

        lax.fori_loop(0, NBLK // g, gbody, 0)

    cols, ospec, crspec = _fox_specs()
    osd = jax.ShapeDtypeStruct((nb, SEQ, FOX_W), F32)
    return pl.pallas_call(
        body, out_shape=(osd, osd), grid=(nb, FOX_W // LANES), in_specs=cols + [crspec], out_specs=(ospec, ospec),
        compiler_params=_cparams(dimension_semantics=("parallel", "parallel")), name=name,
    )(p3, p3, p3, crow)


def _fox_bwd(p3, crow, do, o, lse, *, do_off, name):
    nb = p3.shape[0]
    g = FOX_GROUP
    hd = HEAD_DIM

    def body(q_ref, k_ref, v_ref, cr_ref, do_ref, o_ref, lse_ref, dq_ref, dko_ref, dvo_ref, dcr_ref, dk_ref, dv_ref):
        masks = _head_masks(2)
        tri = _tri_bias(True)
        dk_ref[...] = jnp.zeros_like(dk_ref)
        dv_ref[...] = jnp.zeros_like(dv_ref)
        dcr_ref[...] = jnp.zeros_like(dcr_ref)

        def products(qcat, docat, t):
            out = []
            for e in range(g):
                krows = _fox_key_rows(t, e)
                out.append(lax.dot_general(qcat, k_ref[krows, :], _NT, preferred_element_type=F32))
                out.append(lax.dot_general(docat, v_ref[krows, :], _NT, preferred_element_type=F32))
            return tuple(out)

        def consume(prod, t, ctx, dq, nblk, diag):
            qcat, docat, lse0, lse1, dl0, dl1 = ctx
            for e in range(nblk):
                jb = g * t + e
                krows = _fox_key_rows(t, e)
                s, dp = prod[2 * e], prod[2 * e + 1]
                cr = cr_ref[jb]
                u0 = s[:BLK] - cr[0:1, :]
                u1 = s[BLK:] - cr[1:2, :]
                if diag and e == nblk - 1:
                    u0, u1 = u0 + tri, u1 + tri
                p0 = jnp.exp(u0 - lse0)
                p1 = jnp.exp(u1 - lse1)
                ds0 = p0 * (dp[:BLK] - dl0)
                ds1 = p1 * (dp[BLK:] - dl1)
                dcr_ref[jb, 0:1, :] += jnp.sum(ds0, axis=0, keepdims=True)
                dcr_ref[jb, 1:2, :] += jnp.sum(ds1, axis=0, keepdims=True)
                ds0b, ds1b = ds0.astype(BF16), ds1.astype(BF16)
                pcat = jnp.concatenate([p0.astype(BF16), p1.astype(BF16)], axis=0)
                dscat = jnp.concatenate([ds0b, ds1b], axis=0)
                dv_ref[krows, :] += lax.dot_general(pcat, docat, _T0, preferred_element_type=F32)
                dk_ref[krows, :] += lax.dot_general(dscat, qcat, _T0, preferred_element_type=F32)
                dsrow = jnp.concatenate([ds0b, ds1b], axis=1)
                dq = dq + jnp.dot(dsrow, _stack_heads(k_ref[krows, :] * QK_SCALE, masks), preferred_element_type=F32)
            return dq

        def gbody(ng, c):
            ctxs, rows = [], []
            for a in range(g):
                r = pl.ds(pl.multiple_of((g * ng + a) * BLK, BLK), BLK)
                qcat = _stack_heads(q_ref[r, :] * QK_SCALE, masks)
                dob = do_ref[r, :].astype(BF16)
                prod = dob.astype(F32) * o_ref[r, :]
                z = jnp.zeros_like(prod)
                dl0 = jnp.sum(jnp.where(masks[0], prod, z), axis=1, keepdims=True)
                dl1 = jnp.sum(jnp.where(masks[1], prod, z), axis=1, keepdims=True)
                lseb = lse_ref[r, :]
                ctxs.append((qcat, _stack_heads(dob, masks), lseb[:, 0:1], lseb[:, hd:hd + 1], dl0, dl1))
                rows.append(r)
            first = [products(ctxs[a][0], ctxs[a][1], 0) for a in range(g)]
            done = []
            for a in range(g):
                def step(t, cc, ctx=ctxs[a]):
                    pr, dq = cc
                    nxt = products(ctx[0], ctx[1], t + 1)
                    return nxt, consume(pr, t, ctx, dq, g, False)

                done.append(lax.fori_loop(0, ng, step, (first[a], jnp.zeros((BLK, LANES), F32))))
            for a in range(g):
                pr, dq = done[a]
                dq_ref[rows[a], :] = consume(pr, ng, ctxs[a], dq, a + 1, True).astype(dq_ref.dtype)
            return c

        lax.fori_loop(0, NBLK // g, gbody, 0)
        dko_ref[...] = dk_ref[...].astype(dko_ref.dtype)
        dvo_ref[...] = dv_ref[...].astype(dvo_ref.dtype)

    cols, ospec, crspec = _fox_specs()
    dospec = pl.BlockSpec((None, SEQ, LANES), lambda b, j: (b, 0, do_off + j))
    osd = jax.ShapeDtypeStruct((nb, SEQ, FOX_W), BF16)
    return pl.pallas_call(
        body, out_shape=(osd, osd, osd, jax.ShapeDtypeStruct((nb, FOX_W // LANES, NBLK, 8, BLK), F32)),
        grid=(nb, FOX_W // LANES), in_specs=cols + [crspec, dospec, ospec, ospec], out_specs=(ospec, ospec, ospec, crspec),
        scratch_shapes=[pltpu.VMEM((SEQ, LANES), F32)] * 2,
        compiler_params=_cparams(dimension_semantics=("parallel", "parallel")), name=name,
    )(p3, p3, p3, crow, do, o, lse)


_B1, _B2 = FOX_W // LANES, (FOX_W + DIL_W) // LANES


def _dy_gate_bwd(dx2b, wo, fox, dil, memo, p16, *, tm, tn, name):
    t, d = dx2b.shape
    assert FOX_W % tn == 0 and DIL_W % tn == 0 and MEM_W % tn == 0 and all(c % tn == 0 for c in (C_FG, C_DG, C_MG))
    n1, n2, n3 = FOX_W // tn, (FOX_W + DIL_W) // tn, MIX_W // tn

    def body(dx_ref, w_ref, f_ref, d_ref, m_ref, g_ref, da_ref, dg_ref):
        j = pl.program_id(1)
        dyv = lax.dot_general(dx_ref[...], w_ref[...], _NT, preferred_element_type=F32)
        a = jnp.where(j < n1, f_ref[...], jnp.where(j < n2, d_ref[...], m_ref[...]))
        gt = g_ref[...].astype(F32)
        sg = 1.0 / (1.0 + jnp.exp(-gt))
        da_ref[...] = (dyv * gt * sg).astype(da_ref.dtype)
        dg_ref[...] = (dyv * a * sg * (1.0 + gt * (1.0 - sg))).astype(dg_ref.dtype)

    def gcol(j):
        return jnp.where(j < n1, C_FG // tn + j, jnp.where(j < n2, C_DG // tn + j - n1, C_MG // tn + j - n2))

    tile = pl.BlockSpec((tm, tn), lambda i, j: (i, j))
    return pl.pallas_call(
        body,
        out_shape=(jax.ShapeDtypeStruct((t, MIX_W), BF16), jax.ShapeDtypeStruct((t, MIX_W), BF16)),
        grid=(t // tm, n3),
        in_specs=[pl.BlockSpec((tm, d), lambda i, j: (i, 0)), pl.BlockSpec((tn, d), lambda i, j: (j, 0)),
                  pl.BlockSpec((tm, tn), lambda i, j: (i, jnp.minimum(j, n1 - 1))),
                  pl.BlockSpec((tm, tn), lambda i, j: (i, jnp.clip(j - n1, 0, n2 - n1 - 1))),
                  pl.BlockSpec((tm, tn), lambda i, j: (i, jnp.clip(j - n2, 0, n3 - n2 - 1))),
                  pl.BlockSpec((tm, tn), lambda i, j: (i, gcol(j)))],
        out_specs=(tile, tile),
        compiler_params=_cparams(dimension_semantics=("parallel", "parallel")),
        name=name,
    )(dx2b, wo, fox, dil, memo, p16)


def _silu(g):
    return g / (1.0 + jnp.exp(-g))


def _out_loss(fox, dil, memo, p16, wo, x, tgt, gfin, *, tm, name):
    t, d = x.shape
    n_feat = float(d)

    def body(f_ref, d_ref, m_ref, fg_ref, dg_ref, mg_ref, w_ref, x_ref, t_ref, g_ref, y_ref, dx_ref, dxb_ref, st_ref):
        i = pl.program_id(0)

        @pl.when(i == 0)
        def _():
            st_ref[...] = jnp.zeros_like(st_ref)

        y = jnp.concatenate([(a_ref[...] * _silu(gt_ref[...].astype(F32))).astype(BF16)
                             for a_ref, gt_ref in ((f_ref, fg_ref), (d_ref, dg_ref), (m_ref, mg_ref))], axis=1)
        y_ref[...] = y
        x2 = x_ref[...] + jnp.dot(y, w_ref[...], preferred_element_type=F32)
        r = lax.rsqrt(jnp.mean(x2 * x2, axis=-1, keepdims=True) + RMS_EPS)
        nrm = x2 * r
        gv = g_ref[...]
        err = nrm * gv - t_ref[...]
        dout = err * (1.0 / n_feat)
        dn = dout * gv
        dx2 = r * (dn - nrm * jnp.mean(dn * nrm, axis=-1, keepdims=True))
        dx_ref[...] = dx2
        dxb_ref[...] = dx2.astype(dxb_ref.dtype)
        st_ref[0:1, :] += jnp.sum(dout * nrm, axis=0, keepdims=True)
        st_ref[1:2, :] += (0.5 / n_feat) * jnp.sum(err * err, axis=0, keepdims=True)

    row = pl.BlockSpec((tm, d), lambda i: (i, 0))
    whole = lambda w: pl.BlockSpec((tm, w), lambda i: (i, 0))
    gate = lambda w, col: pl.BlockSpec((tm, w), lambda i: (i, col // w))
    return pl.pallas_call(
        body,
        out_shape=(jax.ShapeDtypeStruct((t, MIX_W), BF16), jax.ShapeDtypeStruct((t, d), F32), jax.ShapeDtypeStruct((t, d), BF16),
                   jax.ShapeDtypeStruct((8, d), F32)),
        grid=(t // tm,),
        in_specs=[whole(FOX_W), whole(DIL_W), whole(MEM_W), gate(FOX_W, C_FG), gate(DIL_W, C_DG), gate(MEM_W, C_MG),
                  pl.BlockSpec((MIX_W, d), lambda i: (0, 0)), row, row, pl.BlockSpec((1, d), lambda i: (0, 0))],
        out_specs=(pl.BlockSpec((tm, MIX_W), lambda i: (i, 0)), row, row, pl.BlockSpec((8, d), lambda i: (0, 0))),
        compiler_params=_cparams(dimension_semantics=("arbitrary",)),
        name=name,
    )(fox, dil, memo, p16, p16, p16, wo, x, tgt, gfin)


def _dh_rms_bwd(dp, w, x, g, resid, *, tm, name):
    t, d = x.shape
    kdim = dp.shape[1]

    def body(*refs):
        if resid is not None:
            dp_ref, w_ref, x_ref, g_ref, r_ref, dx_ref, gg_ref = refs
        else:
            dp_ref, w_ref, x_ref, g_ref, dx_ref, gg_ref = refs

        @pl.when(pl.program_id(0) == 0)
        def _():
            gg_ref[...] = jnp.zeros_like(gg_ref)

        dh = lax.dot_general(dp_ref[...], w_ref[...], _NT, preferred_element_type=F32)
        xv = x_ref[...]
        r = lax.rsqrt(jnp.mean(xv * xv, axis=-1, keepdims=True) + RMS_EPS)
        nrm = xv * r
        dn = dh * g_ref[...]
        dx = r * (dn - nrm * jnp.mean(dn * nrm, axis=-1, keepdims=True))
        if resid is not None:
            dx = dx + r_ref[...]
        dx_ref[...] = dx
        gg_ref[0:1, :] += jnp.sum(dh * nrm, axis=0, keepdims=True)

    row = pl.BlockSpec((tm, d), lambda i: (i, 0))
    in_specs = [pl.BlockSpec((tm, kdim), lambda i: (i, 0)),
                pl.BlockSpec((d, kdim), lambda i: (0, 0), pipeline_mode=pl.Buffered(1)), row,
                pl.BlockSpec((1, d), lambda i: (0, 0))]
    args = [dp, w, x, g]
    if resid is not None:
        in_specs.append(row)
        args.append(resid)
    return pl.pallas_call(
        body,
        out_shape=(jax.ShapeDtypeStruct((t, d), F32), jax.ShapeDtypeStruct((8, d), F32)),
        grid=(t // tm,),
        in_specs=in_specs,
        out_specs=(row, pl.BlockSpec((8, d), lambda i: (0, 0))),
        compiler_params=_cparams(dimension_semantics=("arbitrary",)),
        name=name,
    )(*args)


_FLOG0 = 4 * FOX_W
_W_IN_SEGMENTS = ((0, _FLOG0, 0), (_FLOG0, _FLOG0 + FOX_HEADS, PW), (_FLOG0 + FOX_HEADS, IN_W, C_DQ))
SHARD_W = IN_W // N_CHIPS


def _rearrange_w_in(shards):
    def cols(lo, hi):
        parts = []
        for k in range(N_CHIPS):
            a, b = max(lo, k * SHARD_W), min(hi, (k + 1) * SHARD_W)
            if a < b:
                parts.append(shards[k][:, a - k * SHARD_W:b - k * SHARD_W])
        return parts

    (a0, a1, _), (f0, f1, _), (b0, b1, _) = _W_IN_SEGMENTS
    pad = jnp.zeros((shards[0].shape[0], PWF - PW - FOX_HEADS), shards[0].dtype)
    return jnp.concatenate(cols(a0, a1) + cols(b0, b1) + cols(f0, f1) + [pad], axis=1)


def _w_in_grad_slabs(g):
    slabs = []
    for k in range(N_CHIPS):
        parts = []
        for lo, hi, at in _W_IN_SEGMENTS:
            a, b = max(lo, k * SHARD_W), min(hi, (k + 1) * SHARD_W)
            if a < b:
                parts.append(g[:, at + a - lo:at + b - lo])
        slabs.append(jnp.concatenate(parts, axis=1))
    return jnp.stack(slabs, axis=0)


def _local_grads(x, mem, norm_g, w_r, b_forget, mem_norm_g, w_kv, w_o, final_norm_g, tgt, start_reduce=None,
                 start_reduce_small=None, early_token=None, late_weights=None):
    nb = x.shape[0]
    t = nb * SEQ
    x2d = x.reshape(t, D_MODEL)
    tgt2d = tgt.reshape(t, D_MODEL)
    tabs = _rope_tables()
    bpad = jnp.pad(b_forget.reshape(1, FOX_HEADS), ((0, 0), (0, LANES - FOX_HEADS)))

    gain0 = norm_g.reshape(1, D_MODEL)
    if early_token is not None:
        gain0 = gain0 + early_token[0:1, 0:1]
    h, p16, dqkv, flog = _proj(x2d, gain0, w_r, tabs, n=PWF, tm=1024, tn=768, name="proj")
    c12 = _flog_fwd(flog, bpad, nb=nb, ts=256, name="flog_fwd")

    crow = c12[:, :FOX_HEADS].reshape(nb, NBLK, BLK, FOX_HEADS // 2, 2).transpose(0, 3, 1, 4, 2)
    crow = jnp.pad(crow, ((0, 0), (0, 0), (0, 0), (0, 6), (0, 0)))
    p3 = p16.reshape(nb, SEQ, PWF)
    fox, fox_lse = _fox_fwd(p3, crow, name="fox_fwd")
    if late_weights is not None:
        w_kv, w_o = late_weights(fox_lse)

    dqkv3 = dqkv.reshape(nb, SEQ, 3 * DIL_W)
    dil, dil_lse = _dil_fwd(dqkv3, name="dil_fwd")

    mh = _rms_fwd(mem.reshape(nb * MEM_LEN, D_MODEL), mem_norm_g.reshape(1, D_MODEL), tm=nb * MEM_LEN, name="rms_mem")
    mkv = _matmul(mh, w_kv, out_dtype=BF16, tm=nb * MEM_LEN, tn=512, tk=D_MODEL, name="mem_kv")
    mkv3 = mkv.reshape(nb, MEM_LEN, 2 * MEM_W)
    memo, mem_lse = _mem_fwd(p3, mkv3, qoff=C_MQ // LANES, name="mem_fwd")

    fox2, dil2, memo2 = fox.reshape(t, FOX_W), dil.reshape(t, DIL_W), memo.reshape(t, MEM_W)
    y, dx2, dx2b, st = _out_loss(fox2, dil2, memo2, p16, w_o, x2d, tgt2d, final_norm_g.reshape(1, D_MODEL), tm=256,
                                 name="out_loss")

    g_wo = _matmul(y, dx2b, mode="tn", out_dtype=BF16, tm=1024, tn=512, tk=t, name="grad_w_out")
    datt, dgate = _dy_gate_bwd(dx2b, w_o, fox2, dil2, memo2, p16, tm=2048, tn=256, name="dy_gate_bwd")
    datt3 = datt.reshape(nb, SEQ, MIX_W)

    dmq, dmk, dmv = _mem_bwd(p3, mkv3, datt3, memo, mem_lse, qoff=C_MQ // LANES, do_off=_B2, name="mem_bwd")
    dmkv = jnp.concatenate([dmk, dmv], axis=-1).reshape(nb * MEM_LEN, 2 * MEM_W).astype(BF16)
    g_wkv = _matmul(mh, dmkv, mode="tn", out_dtype=BF16, tm=512, tn=512, tk=nb * MEM_LEN, name="grad_w_kv")
    mem_gain = mem_norm_g.reshape(1, D_MODEL)
    if start_reduce_small is not None:
        tok = start_reduce_small(g_wkv, g_wo)[0:1, 0:1]
        mem_gain, crow = mem_gain + tok, crow + tok
    _, gmn = _dh_rms_bwd(dmkv, w_kv, mem.reshape(nb * MEM_LEN, D_MODEL), mem_gain, None, tm=nb * MEM_LEN, name="mem_rms_bwd")

    dfq, dfk, dfv, dcr = _fox_bwd(p3, crow, datt3, fox, fox_lse, do_off=0, name="fox_bwd")
    dcol = -dcr[:, :, :, :2, :].transpose(0, 2, 4, 1, 3).reshape(t, FOX_HEADS)
    dcol = jnp.pad(dcol, ((0, 0), (0, LANES - FOX_HEADS)))
    dflog, gb = _flog_bwd(dcol, flog, bpad, nb=nb, ts=256, name="flog_bwd")

    ddq, ddk, ddv = _dil_bwd(dqkv3, datt3, dil, dil_lse, tabs, do_off=_B1, name="dil_bwd")

    flat = lambda a: a.reshape(t, -1)
    dp = jnp.concatenate([flat(dfq), flat(dfk), flat(dfv), dgate[:, :FOX_W], flat(ddq), flat(ddk), flat(ddv),
                          dgate[:, FOX_W:FOX_W + DIL_W], flat(dmq), dgate[:, FOX_W + DIL_W:], dflog,
                          jnp.zeros((t, PWF - PW - LANES), BF16)], axis=1)
    g_wr = _matmul(h, dp, mode="tn", out_dtype=BF16, tm=D_MODEL, tn=768, tk=t, name="grad_w_in")
    gain = norm_g.reshape(1, D_MODEL)
    if start_reduce is not None:
        gain = gain + start_reduce(g_wr)[0:1, 0:1]
    gx, gng = _dh_rms_bwd(dp, w_r, x2d, gain, dx2, tm=256, name="in_rms_bwd")

    gb_row = jnp.pad(gb[0:1, :], ((0, 0), (0, D_MODEL - LANES)))
    small = jnp.concatenate([gng[0:1], gmn[0:1], st[0:1], gb_row, st[1:2], jnp.zeros((3, D_MODEL), F32)], axis=0)
    return gx.reshape(nb, SEQ, D_MODEL), g_wr, g_wkv, g_wo, small


MESH = pl.DeviceIdType.MESH
ANY = pl.BlockSpec(memory_space=pl.ANY)


def _place():
    x, y, c = lax.axis_index("x"), lax.axis_index("y"), lax.axis_index("c")
    other_chips = [(1 - x, y), (x, 1 - y), (1 - x, 1 - y)]
    return x, y, c, other_chips


def _gather_weights(shards):
    n = len(shards)

    def body(*refs):
        in_refs, out_refs = refs[:n], refs[n:2 * n]
        send_sems, recv_sems = refs[2 * n:]
        x, y, c, chips = _place()
        me_chip = 2 * x + y
        sibling = (x, y, 1 - c)

        def half(ref, pc, rows):
            return ref.at[pl.ds(pc * (rows // 2), rows // 2), :]

        def rcopy(k, src, dst, to):
            return pltpu.make_async_remote_copy(src_ref=src, dst_ref=dst, send_sem=send_sems.at[k], recv_sem=recv_sems.at[k],
                                                device_id=to, device_id_type=MESH)

        sends = []
        for t in range(n):
            rows = shards[t].shape[0]
            for j, chip in enumerate(chips):
                cp = rcopy(6 * t + j, half(in_refs[t], c, rows), half(out_refs[t].at[me_chip], c, rows), (*chip, c))
                cp.start()
                sends.append(cp)
        for t in range(n):
            rows = shards[t].shape[0]
            for j, chip in enumerate(chips):
                slot = out_refs[t].at[2 * chip[0] + chip[1]]
                rcopy(6 * t + j, half(slot, c, rows), half(slot, c, rows), sibling).wait_recv()
                fw = rcopy(6 * t + 3 + j, half(slot, c, rows), half(slot, c, rows), sibling)
                fw.start()
                sends.append(fw)
        for t in range(n):
            rows = shards[t].shape[0]
            for j, chip in enumerate(chips):
                slot = out_refs[t].at[2 * chip[0] + chip[1]]
                rcopy(6 * t + 3 + j, half(slot, 1 - c, rows), half(slot, 1 - c, rows), sibling).wait_recv()
        for cp in sends:
            cp.wait_send()

    return pl.pallas_call(
        body,
        out_shape=tuple(jax.ShapeDtypeStruct((N_CHIPS,) + s.shape, s.dtype) for s in shards),
        in_specs=[ANY] * n,
        out_specs=tuple([ANY] * n),
        scratch_shapes=[pltpu.SemaphoreType.DMA((6 * n,)), pltpu.SemaphoreType.DMA((6 * n,))],
        name="gather_weights",
    )(*shards)


def _pair_exchange(gs, *, name):
    n = len(gs)

    def body(*refs):
        g_refs, r_refs = refs[:n], refs[n:2 * n]
        send_sems, recv_sems = refs[2 * n:]
        x, y, c, _ = _place()
        cps = []
        for t in range(n):
            hr = gs[t].shape[1] // 2
            cp = pltpu.make_async_remote_copy(src_ref=g_refs[t].at[:, pl.ds((1 - c) * hr, hr), :], dst_ref=r_refs[t],
                                              send_sem=send_sems.at[t], recv_sem=recv_sems.at[t],
                                              device_id=(x, y, 1 - c), device_id_type=MESH)
            cp.start()
            cps.append(cp)
        for cp in cps:
            cp.wait()

    return pl.pallas_call(
        body,
        out_shape=tuple(jax.ShapeDtypeStruct((g.shape[0], g.shape[1] // 2, g.shape[2]), g.dtype) for g in gs),
        in_specs=[ANY] * n,
        out_specs=tuple([ANY] * n),
        scratch_shapes=[pltpu.SemaphoreType.DMA((n,)), pltpu.SemaphoreType.DMA((n,))],
        name=name,
    )(*gs)


_HBM = pl.BlockSpec(memory_space=pltpu.HBM)
_SEM = pl.BlockSpec(memory_space=pltpu.SEMAPHORE)
_DATAFLOW = pltpu.SideEffectType.DATAFLOW_SIDE_EFFECTING


def _chip_copies(p_refs, land_refs, send_sems, recv_sems):
    x, y, c, chips = _place()
    me_chip = 2 * x + y
    return [pltpu.make_async_remote_copy(src_ref=p_refs[t].at[2 * chip[0] + chip[1]], dst_ref=land_refs[t].at[me_chip],
                                         send_sem=send_sems.at[3 * t + j], recv_sem=recv_sems.at[3 * t + j],
                                         device_id=(*chip, c), device_id_type=MESH)
            for t in range(len(p_refs)) for j, chip in enumerate(chips)]


def _chip_exchange_start(ps, *, tag):
    n = len(ps)

    def body(*refs):
        p_refs, land_refs = refs[:n], refs[n:2 * n]
        send_sems, recv_sems = refs[2 * n:2 * n + 2]
        token = refs[-1]
        for cp in _chip_copies(p_refs, land_refs, send_sems, recv_sems):
            cp.start()
        token[...] = jnp.zeros_like(token)

    hbm = [pltpu.HBM(p.shape, p.dtype) for p in ps]
    args = [pltpu.with_memory_space_constraint(p, pltpu.HBM) for p in ps]
    args += [pltpu.with_memory_space_constraint(lax.empty(p.shape, p.dtype), pltpu.HBM) for p in ps]
    out = pl.pallas_call(
        body,
        name=f"chip_exchange_start_{tag}",
        out_shape=(pltpu.SemaphoreType.DMA((3 * n,)), pltpu.SemaphoreType.DMA((3 * n,)), *hbm, *hbm,
                   jax.ShapeDtypeStruct((8, LANES), F32)),
        in_specs=[_HBM] * (2 * n),
        out_specs=(_SEM, _SEM, *([_HBM] * (2 * n)), pl.BlockSpec(memory_space=pltpu.VMEM)),
        input_output_aliases={i: 2 + i for i in range(2 * n)},
        compiler_params=pltpu.CompilerParams(has_side_effects=_DATAFLOW),
    )(*args)
    return out[0], out[1], out[2:2 + n], out[2 + n:2 + 2 * n], out[-1]


def _chip_exchange_wait(send_sems, recv_sems, p_thru, land_thru, after, *, tag):
    n = len(p_thru)

    def body(*refs):
        p_refs, land_refs = refs[:n], refs[n:2 * n]
        ssem, rsem = refs[2 * n:2 * n + 2]
        for cp in _chip_copies(p_refs, land_refs, ssem, rsem):
            cp.wait_send()
            cp.wait_recv()

    hbm = [pltpu.HBM(p.shape, p.dtype) for p in p_thru]
    out = pl.pallas_call(
        body,
        name=f"chip_exchange_wait_{tag}",
        out_shape=(*hbm, *hbm),
        in_specs=[_HBM] * (2 * n) + [_SEM, _SEM, ANY],
        out_specs=tuple([_HBM] * (2 * n)),
        input_output_aliases={i: i for i in range(2 * n)},
        compiler_params=pltpu.CompilerParams(has_side_effects=_DATAFLOW),
    )(*p_thru, *land_thru, send_sems, recv_sems, after)
    return out[:n], out[n:]


def _shard_copies(s_refs, land_refs, send_sems, recv_sems):
    x, y, c, chips = _place()
    me_chip = 2 * x + y
    return [pltpu.make_async_remote_copy(src_ref=s_refs[t], dst_ref=land_refs[t].at[me_chip],
                                         send_sem=send_sems.at[3 * t + j], recv_sem=recv_sems.at[3 * t + j],
                                         device_id=(*chip, c), device_id_type=MESH)
            for t in range(len(s_refs)) for j, chip in enumerate(chips)]


def _gather_late_start(shards):
    n = len(shards)

    def body(*refs):
        s_refs, land_refs = refs[:n], refs[n:2 * n]
        send_sems, recv_sems = refs[2 * n:2 * n + 2]
        token = refs[-1]
        for cp in _shard_copies(s_refs, land_refs, send_sems, recv_sems):
            cp.start()
        token[...] = jnp.zeros_like(token)

    lands = [(N_CHIPS,) + s.shape for s in shards]
    args = [pltpu.with_memory_space_constraint(s, pltpu.HBM) for s in shards]
    args += [pltpu.with_memory_space_constraint(lax.empty(shp, s.dtype), pltpu.HBM) for shp, s in zip(lands, shards)]
    out = pl.pallas_call(
        body,
        name="gather_late_start",
        out_shape=(pltpu.SemaphoreType.DMA((3 * n,)), pltpu.SemaphoreType.DMA((3 * n,)),
                   *[pltpu.HBM(s.shape, s.dtype) for s in shards], *[pltpu.HBM(shp, s.dtype) for shp, s in zip(lands, shards)],
                   jax.ShapeDtypeStruct((8, LANES), F32)),
        in_specs=[_HBM] * (2 * n),
        out_specs=(_SEM, _SEM, *([_HBM] * (2 * n)), pl.BlockSpec(memory_space=pltpu.VMEM)),
        input_output_aliases={i: 2 + i for i in range(2 * n)},
        compiler_params=pltpu.CompilerParams(has_side_effects=_DATAFLOW),
    )(*args)
    return out[0], out[1], out[2:2 + n], out[2 + n:2 + 2 * n], out[-1]


def _gather_late_wait(send_sems, recv_sems, s_thru, land_thru, after):
    n = len(s_thru)

    def body(*refs):
        s_refs, land_refs = refs[:n], refs[n:2 * n]
        ssem, rsem = refs[2 * n:2 * n + 2]
        for cp in _shard_copies(s_refs, land_refs, ssem, rsem):
            cp.wait_send()
            cp.wait_recv()

    out = pl.pallas_call(
        body,
        name="gather_late_wait",
        out_shape=(*[pltpu.HBM(s.shape, s.dtype) for s in s_thru], *[pltpu.HBM(l.shape, l.dtype) for l in land_thru]),
        in_specs=[_HBM] * (2 * n) + [_SEM, _SEM, ANY],
        out_specs=tuple([_HBM] * (2 * n)),
        input_output_aliases={i: i for i in range(2 * n)},
        compiler_params=pltpu.CompilerParams(has_side_effects=_DATAFLOW),
    )(*s_thru, *land_thru, send_sems, recv_sems, after)
    return out[:n], out[n:]


def _pair_swap(rs):
    n = len(rs)

    def body(*refs):
        r_refs, o_refs = refs[:n], refs[n:2 * n]
        send_sems, recv_sems = refs[2 * n:]
        x, y, c, _ = _place()
        cps = []
        for t in range(n):
            cp = pltpu.make_async_remote_copy(src_ref=r_refs[t], dst_ref=o_refs[t], send_sem=send_sems.at[t],
                                              recv_sem=recv_sems.at[t], device_id=(x, y, 1 - c), device_id_type=MESH)
            cp.start()
            cps.append(cp)
        for cp in cps:
            cp.wait()

    return pl.pallas_call(
        body,
        out_shape=tuple(jax.ShapeDtypeStruct(r.shape, r.dtype) for r in rs),
        in_specs=[ANY] * n,
        out_specs=tuple([ANY] * n),
        scratch_shapes=[pltpu.SemaphoreType.DMA((n,)), pltpu.SemaphoreType.DMA((n,))],
        name="pair_swap",
    )(*rs)


N_DEV = 8
LOSS_ROW = 4


def _small_allreduce(small):
    def body(s_ref, o_ref, all_ref, send_sems, recv_sems):
        x, y, c, _ = _place()
        me = 4 * x + 2 * y + c
        all_ref[me] = s_ref[...]
        cps = []
        for k in range(1, N_DEV):
            peer = tuple(1 - p if (k >> s) & 1 else p for p, s in ((x, 2), (y, 1), (c, 0)))
            cp = pltpu.make_async_remote_copy(src_ref=s_ref, dst_ref=all_ref.at[me], send_sem=send_sems.at[k - 1],
                                              recv_sem=recv_sems.at[k - 1], device_id=peer, device_id_type=MESH)
            cp.start()
            cps.append(cp)
        for cp in cps:
            cp.wait()
        tot = all_ref[0]
        for d in range(1, N_DEV):
            tot = tot + all_ref[d]
        o_ref[...] = tot
        o_ref[LOSS_ROW:LOSS_ROW + 1, :] = jnp.broadcast_to(jnp.sum(tot[LOSS_ROW:LOSS_ROW + 1, :], axis=1, keepdims=True),
                                                          (1, tot.shape[1]))

    vm = pl.BlockSpec(memory_space=pltpu.VMEM)
    return pl.pallas_call(
        body,
        out_shape=jax.ShapeDtypeStruct(small.shape, small.dtype),
        in_specs=[vm],
        out_specs=vm,
        scratch_shapes=[pltpu.VMEM((N_DEV,) + small.shape, small.dtype), pltpu.SemaphoreType.DMA((N_DEV - 1,)),
                        pltpu.SemaphoreType.DMA((N_DEV - 1,))],
        name="small_allreduce",
    )(small)


def _sum_pair(g, recv, cidx, *, tr, name):
    n, hr, cols = recv.shape
    nr = hr // tr

    def body(c_ref, g_ref, r_ref, o_ref):
        o_ref[...] = (g_ref[...].astype(F32) + r_ref[...].astype(F32)).astype(o_ref.dtype)

    grid_spec = pltpu.PrefetchScalarGridSpec(
        num_scalar_prefetch=1,
        grid=(n, nr),
        in_specs=[pl.BlockSpec((None, tr, cols), lambda k, i, c_ref: (k, c_ref[0] * nr + i, 0)),
                  pl.BlockSpec((None, tr, cols), lambda k, i, c_ref: (k, i, 0))],
        out_specs=pl.BlockSpec((None, tr, cols), lambda k, i, c_ref: (k, i, 0)),
    )
    return pl.pallas_call(body, out_shape=jax.ShapeDtypeStruct(recv.shape, BF16), grid_spec=grid_spec,
                          compiler_params=_cparams(), name=name)(cidx, g, recv)


def _sum_chips(p, *, tr, name):
    _, rows, cols = p.shape

    def body(p_ref, o_ref):
        tot = p_ref[0].astype(F32)
        for k in range(1, N_CHIPS):
            tot = tot + p_ref[k].astype(F32)
        o_ref[...] = tot

    return pl.pallas_call(
        body,
        out_shape=jax.ShapeDtypeStruct((rows, cols), F32),
        grid=(rows // tr,),
        in_specs=[pl.BlockSpec((N_CHIPS, tr, cols), lambda i: (0, i, 0))],
        out_specs=pl.BlockSpec((tr, cols), lambda i: (i, 0)),
        compiler_params=_cparams(),
        name=name,
    )(p)


def _adamw(w, g, m, v, *, tr, name):
    rows, cols = w.shape
    bc1 = 1.0 / (1.0 - ADAM_B1 ** ADAM_STEP)
    bc2 = 1.0 / (1.0 - ADAM_B2 ** ADAM_STEP)

    def body(w_ref, g_ref, m_ref, v_ref, d_ref, nm_ref, nv_ref):
        gv = g_ref[...]
        nm = ADAM_B1 * m_ref[...] + (1.0 - ADAM_B1) * gv
        nv = ADAM_B2 * v_ref[...] + (1.0 - ADAM_B2) * (gv * gv)
        d_ref[...] = -ADAM_LR * ((nm * bc1) / (jnp.sqrt(nv * bc2) + ADAM_EPS) + ADAM_WD * w_ref[...])
        nm_ref[...] = nm
        nv_ref[...] = nv

    spec = pl.BlockSpec((tr, cols), lambda i: (i, 0))
    sd = jax.ShapeDtypeStruct((rows, cols), F32)
    return pl.pallas_call(body, out_shape=(sd, sd, sd), grid=(rows // tr,), in_specs=[spec] * 4, out_specs=(spec,) * 3,
                          compiler_params=_cparams(), name=name)(w, g, m, v)


def _adamw_halves(w, own, sib, cidx, m, v, *, tr, name):
    rows, cols = w.shape
    hr = own.shape[0]
    nr = hr // tr
    assert rows == 2 * hr and hr % tr == 0
    bc1 = 1.0 / (1.0 - ADAM_B1 ** ADAM_STEP)
    bc2 = 1.0 / (1.0 - ADAM_B2 ** ADAM_STEP)

    def body(c_ref, w_ref, o_ref, s_ref, m_ref, v_ref, g_ref, d_ref, nm_ref, nv_ref):
        mine = (pl.program_id(0) // nr) == c_ref[0]
        gv = jnp.where(mine, o_ref[...], s_ref[...])
        nm = ADAM_B1 * m_ref[...] + (1.0 - ADAM_B1) * gv
        nv = ADAM_B2 * v_ref[...] + (1.0 - ADAM_B2) * (gv * gv)
        g_ref[...] = gv
        d_ref[...] = -ADAM_LR * ((nm * bc1) / (jnp.sqrt(nv * bc2) + ADAM_EPS) + ADAM_WD * w_ref[...])
        nm_ref[...] = nm
        nv_ref[...] = nv

    full = pl.BlockSpec((tr, cols), lambda i, c_ref: (i, 0))
    half = pl.BlockSpec((tr, cols), lambda i, c_ref: (i % nr, 0))
    sd = jax.ShapeDtypeStruct((rows, cols), F32)
    grid_spec = pltpu.PrefetchScalarGridSpec(num_scalar_prefetch=1, grid=(rows // tr,), in_specs=[full, half, half, full, full],
                                             out_specs=(full,) * 4)
    return pl.pallas_call(body, out_shape=(sd,) * 4, grid_spec=grid_spec, compiler_params=_cparams(), name=name)(
        cidx, w, own, sib, m, v)


def _pack_small(norm, mem_norm, final_norm, b_forget):
    rows = [norm.reshape(1, D_MODEL), mem_norm.reshape(1, D_MODEL), final_norm.reshape(1, D_MODEL),
            jnp.pad(b_forget.reshape(1, FOX_HEADS), ((0, 0), (0, D_MODEL - FOX_HEADS))), jnp.zeros((4, D_MODEL), F32)]
    return jnp.concatenate(rows, axis=0)


def _unpack_small(a):
    return a[0:1], a[3:4, :FOX_HEADS], a[1:2], a[2]


def kernel(x, mem, norm_g, w_in, b_forget, mem_norm_g, w_mem_kv, w_out, final_norm_g, loss_target, m_norm_g, m_w_in, m_b_forget, m_mem_norm_g, m_w_mem_kv, m_w_out, m_final_norm_g, v_norm_g, v_w_in, v_b_forget, v_mem_norm_g, v_w_mem_kv, v_w_out, v_final_norm_g):
    core = lax.axis_index("c").astype(jnp.int32)
    me_chip = (2 * lax.axis_index("x") + lax.axis_index("y")).astype(jnp.int32)
    cidx = core.reshape(1)

    def own_slot(arr, own):
        return lax.dynamic_update_slice(arr, own[None].astype(arr.dtype), (me_chip,) + (0,) * own.ndim)

    win_b, late = w_in[0].astype(BF16), [w_mem_kv[0].astype(BF16), w_out[0].astype(BF16)]
    g_in, = _gather_weights([win_b])
    g_in, late = lax.optimization_barrier((own_slot(g_in, win_b), late))
    w_r = _rearrange_w_in([g_in[k] for k in range(N_CHIPS)])
    *late_flight, early_token = _gather_late_start(late)

    def late_weights(after):
        shards, landed = _gather_late_wait(*late_flight, after)
        g_kv, g_out = (own_slot(g, s) for g, s in zip(landed, shards))
        return g_kv.reshape(D_MODEL, 2 * MEM_W), g_out.reshape(MIX_W, D_MODEL)

    trs = (128, 128, 256)
    names = ("w_in", "w_mem_kv", "w_out")
    flights = {}

    def exchange(slabs, nms, ts, tag):
        recv = _pair_exchange(slabs, name=f"pair_exchange_{tag}")
        pair = [_sum_pair(g, r, cidx, tr=tr, name=f"sum_pair_{nm}") for g, r, tr, nm in zip(slabs, recv, ts, nms)]
        if tag == "w_in":
            pair[0] = _w_in_grad_slabs(pair[0][0])
        *flights[tag], token = _chip_exchange_start(pair, tag=tag)
        return token

    def start_reduce_small(g_wkv, g_wo):
        slabs = [g_wkv.reshape(N_CHIPS, D_MODEL // N_CHIPS, 2 * MEM_W), g_wo.reshape(N_CHIPS, MIX_W // N_CHIPS, D_MODEL)]
        return exchange(slabs, names[1:], trs[1:], "small")

    def start_reduce(g_wr):
        return exchange([g_wr[None]], names[:1], trs[:1], "w_in")

    gx, g_wr, g_wkv, g_wo, small = _local_grads(x, mem, norm_g, w_r, b_forget, mem_norm_g, None, None, final_norm_g, loss_target,
                                                start_reduce=start_reduce, start_reduce_small=start_reduce_small,
                                                early_token=early_token, late_weights=late_weights)

    pair, landed = [], []
    for tag in ("w_in", "small"):
        p, l = _chip_exchange_wait(*flights[tag], small, tag=tag)
        pair += list(p)
        landed += list(l)
    got = [lax.dynamic_update_slice(g, lax.dynamic_slice(p, (me_chip, 0, 0), (1,) + p.shape[1:]), (me_chip, 0, 0))
           for g, p in zip(landed, pair)]
    red = [_sum_chips(p, tr=tr, name=f"sum_chips_{nm}") for p, tr, nm in zip(got, trs, names)]
    sib = _pair_swap(red)

    outs = {}
    for nm, r, s, w, m, v, tr in zip(names, red, sib, (w_in, w_mem_kv, w_out), (m_w_in, m_w_mem_kv, m_w_out),
                                     (v_w_in, v_w_mem_kv, v_w_out), trs):
        outs[nm] = tuple(a[None] for a in _adamw_halves(w[0], r, s, cidx, m[0], v[0], tr=tr, name=f"adamw_{nm}"))

    gsum = _small_allreduce(small)
    sd, sm, sv = _adamw(_pack_small(norm_g, mem_norm_g, final_norm_g, b_forget), gsum,
                        _pack_small(m_norm_g, m_mem_norm_g, m_final_norm_g, m_b_forget),
                        _pack_small(v_norm_g, v_mem_norm_g, v_final_norm_g, v_b_forget), tr=8, name="adamw_small")
    loss = gsum[LOSS_ROW, 0]

    def group(i, small_arr):
        ng, bf, mg, fg = _unpack_small(small_arr)
        return (ng, outs["w_in"][i], bf, mg, outs["w_mem_kv"][i], outs["w_out"][i], fg)

    return (loss, gx, *group(0, gsum), *group(1, sd), *group(2, sm), *group(3, sv))
```

```python
import functools
import math

import jax
import jax.numpy as jnp
from jax import lax
from jax.experimental import pallas as pl
from jax.experimental.pallas import tpu as pltpu

F32 = jnp.float32
BF16 = jnp.bfloat16

D_MODEL = 1024
SEQ = 2048
HEAD_DIM = 64
FOX_HEADS = 12
DIL_HEADS = 12
MEM_HEADS = 4
MEM_HEAD_DIM = 128
MEM_LEN = 256
FOX_W = FOX_HEADS * HEAD_DIM
DIL_W = DIL_HEADS * HEAD_DIM
MEM_W = MEM_HEADS * MEM_HEAD_DIM
MIX_W = FOX_W + DIL_W + MEM_W
DILATIONS = ((128, 1), (512, 4), (2048, 16))
ROPE_THETA = 500000.0
ROPE_DIM = HEAD_DIM // 4
RMS_EPS = 1e-6
NEG_INF = -1e30
IN_SIZES = [FOX_W] * 4 + [FOX_HEADS] + [DIL_W] * 4 + [MEM_W] * 2
IN_W = sum(IN_SIZES)

ADAM_LR = 0.001
ADAM_B1 = 0.9
ADAM_B2 = 0.999
ADAM_EPS = 1e-08
ADAM_WD = 0.01
ADAM_STEP = 10

LANES = 128
N_CHIPS = 4
PW = 7168
PWF = PW + 4 * LANES
C_FQ, C_FK, C_FV, C_FG = 0, 768, 1536, 2304
C_DQ, C_DK, C_DV, C_DG = 3072, 3840, 4608, 5376
C_MQ, C_MG = 6144, 6656
VMEM_LIMIT = 48 * 1024 * 1024


def _cparams(**kw):
    return pltpu.CompilerParams(vmem_limit_bytes=VMEM_LIMIT, **kw)


MM_CHUNK = 256


def _matmul(a, b, *, out_dtype, tm, tn, tk, name, mode="nn"):
    if mode == "tn":
        (kdim, m), n = a.shape, b.shape[1]
        a_spec = pl.BlockSpec((tk, tm), lambda i, j, k: (k, i))
        b_spec = pl.BlockSpec((tk, tn), lambda i, j, k: (k, j))
        dims = _T0
    elif mode == "nt":
        (m, kdim), n = a.shape, b.shape[0]
        a_spec = pl.BlockSpec((tm, tk), lambda i, j, k: (i, k))
        b_spec = pl.BlockSpec((tn, tk), lambda i, j, k: (j, k))
        dims = _NT
    else:
        (m, kdim), n = a.shape, b.shape[1]
        a_spec = pl.BlockSpec((tm, tk), lambda i, j, k: (i, k))
        b_spec = pl.BlockSpec((tk, tn), lambda i, j, k: (k, j))
        dims = (((1,), (0,)), ((), ()))
    nk = kdim // tk
    assert m % tm == 0 and n % tn == 0 and kdim % tk == 0

    def body(a_ref, b_ref, o_ref, *scratch):
        if nk == 1:
            bv = b_ref[...]
            for c0 in range(0, tm, min(tm, MM_CHUNK)):
                rows = pl.ds(c0, min(tm, MM_CHUNK))
                av = a_ref[:, rows] if mode == "tn" else a_ref[rows, :]
                o_ref[rows, :] = lax.dot_general(av, bv, dims, preferred_element_type=F32).astype(o_ref.dtype)
            return
        prod = lax.dot_general(a_ref[...], b_ref[...], dims, preferred_element_type=F32)
        acc_ref, = scratch
        k = pl.program_id(2)

        @pl.when(k == 0)
        def _():
            acc_ref[...] = prod

        @pl.when(k > 0)
        def _():
            acc_ref[...] += prod

        @pl.when(k == nk - 1)
        def _():
            o_ref[...] = acc_ref[...].astype(o_ref.dtype)

    return pl.pallas_call(
        body,
        out_shape=jax.ShapeDtypeStruct((m, n), out_dtype),
        grid=(m // tm, n // tn, nk),
        in_specs=[a_spec, b_spec],
        out_specs=pl.BlockSpec((tm, tn), lambda i, j, k: (i, j)),
        scratch_shapes=[pltpu.VMEM((tm, tn), F32)] if nk > 1 else [],
        compiler_params=_cparams(dimension_semantics=("parallel", "parallel", "arbitrary")),
        name=name,
    )(a, b)


def _rms_fwd(x, g, *, tm, name):
    t, d = x.shape

    def body(x_ref, g_ref, h_ref):
        xv = x_ref[...]
        r = lax.rsqrt(jnp.mean(xv * xv, axis=-1, keepdims=True) + RMS_EPS)
        h_ref[...] = (xv * r * g_ref[...]).astype(h_ref.dtype)

    return pl.pallas_call(
        body,
        out_shape=jax.ShapeDtypeStruct((t, d), BF16),
        grid=(t // tm,),
        in_specs=[pl.BlockSpec((tm, d), lambda i: (i, 0)), pl.BlockSpec((1, d), lambda i: (0, 0))],
        out_specs=pl.BlockSpec((tm, d), lambda i: (i, 0)),
        compiler_params=_cparams(),
        name=name,
    )(x, g)


def _rope_tables():
    half = ROPE_DIM // 2
    pos = jnp.arange(SEQ, dtype=F32)
    inv_freq = 1.0 / (ROPE_THETA ** (jnp.arange(0, ROPE_DIM, 2, dtype=F32) / ROPE_DIM))
    ang = pos[:, None] * inv_freq[None, :]
    cos, sin = jnp.cos(ang), jnp.sin(ang)
    one = jnp.ones((SEQ, HEAD_DIM - ROPE_DIM), F32)
    zero = jnp.zeros((SEQ, HEAD_DIM - ROPE_DIM), F32)
    zh = jnp.zeros((SEQ, half), F32)
    c = jnp.concatenate([cos, cos, one], axis=1)
    s1 = jnp.concatenate([zh, sin, zero], axis=1)
    s2 = jnp.concatenate([-sin, zh, zero], axis=1)
    rep = LANES // HEAD_DIM
    return jnp.tile(c, (1, rep)), jnp.tile(s1, (1, rep)), jnp.tile(s2, (1, rep))


def _rope_apply(t, c, s1, s2, transpose=False):
    n = t.shape[-1]
    rep = n // LANES
    c, s1, s2 = (jnp.tile(u, (1, rep)) for u in (c, s1, s2))
    half = ROPE_DIM // 2
    if not transpose:
        return t * c + pltpu.roll(t, half, 1) * s1 + pltpu.roll(t, n - half, 1) * s2
    return t * c + pltpu.roll(t * s1, n - half, 1) + pltpu.roll(t * s2, half, 1)


PROJ_CHUNK = 256


def _proj(x, g, w, tabs, *, n, tm, tn, name):
    t, d = x.shape
    assert C_DQ % tn == 0 and (C_DV - C_DQ) % tn == 0 and (C_DG - C_DQ) % tn == 0
    rope_lo, rope_hi, dil_hi = C_DQ // tn, C_DV // tn, C_DG // tn
    flog_blk, flog_at = PW // tn, PW % tn
    assert flog_at % LANES == 0 and flog_at + LANES <= tn
    s_blocks = SEQ // tm

    def body(x_ref, g_ref, w_ref, c_ref, s1_ref, s2_ref, h_ref, o_ref, f_ref, fl_ref, h_scr):
        j = pl.program_id(1)

        @pl.when(j == 0)
        def _():
            xv = x_ref[...]
            r = lax.rsqrt(jnp.mean(xv * xv, axis=-1, keepdims=True) + RMS_EPS)
            hv = (xv * r * g_ref[...]).astype(BF16)
            h_scr[...] = hv
            h_ref[...] = hv

        def tile(kind):
            wv = w_ref[...]
            for c0 in range(0, tm, PROJ_CHUNK):
                rows = pl.ds(c0, PROJ_CHUNK)
                acc = jnp.dot(h_scr[rows, :], wv, preferred_element_type=F32)
                if kind == "rope":
                    acc = _rope_apply(acc, c_ref[rows, :], s1_ref[rows, :], s2_ref[rows, :])
                o_ref[rows, :] = acc.astype(o_ref.dtype)
                if kind in ("rope", "dv"):
                    f_ref[rows, :] = acc
                if kind == "flog":
                    fl_ref[rows, :] = acc[:, flog_at:flog_at + LANES]

        is_rope = jnp.logical_and(j >= rope_lo, j < rope_hi)
        is_dv = jnp.logical_and(j >= rope_hi, j < dil_hi)
        is_flog = j == flog_blk
        pl.when(is_rope)(functools.partial(tile, "rope"))
        pl.when(is_dv)(functools.partial(tile, "dv"))
        pl.when(is_flog)(functools.partial(tile, "flog"))
        pl.when(jnp.logical_not(jnp.logical_or(jnp.logical_or(is_rope, is_dv), is_flog)))(functools.partial(tile, "plain"))

    tab_spec = pl.BlockSpec((tm, LANES), lambda i, j: (i % s_blocks, 0))
    f_spec = pl.BlockSpec((tm, tn), lambda i, j: (i, jnp.clip(j - rope_lo, 0, dil_hi - rope_lo - 1)))
    row = pl.BlockSpec((tm, d), lambda i, j: (i, 0))
    return pl.pallas_call(
        body,
        out_shape=(jax.ShapeDtypeStruct((t, d), BF16), jax.ShapeDtypeStruct((t, n), BF16),
                   jax.ShapeDtypeStruct((t, 3 * DIL_W), F32), jax.ShapeDtypeStruct((t, LANES), F32)),
        grid=(t // tm, n // tn),
        in_specs=[row, pl.BlockSpec((1, d), lambda i, j: (0, 0)), pl.BlockSpec((d, tn), lambda i, j: (0, j)),
                  tab_spec, tab_spec, tab_spec],
        out_specs=(row, pl.BlockSpec((tm, tn), lambda i, j: (i, j)), f_spec, pl.BlockSpec((tm, LANES), lambda i, j: (i, 0))),
        scratch_shapes=[pltpu.VMEM((tm, d), BF16)],
        compiler_params=_cparams(dimension_semantics=("parallel", "arbitrary")),
        name=name,
    )(x, g, w, *tabs)


def _split3(x):
    hi = x.astype(BF16)
    r1 = x - hi.astype(F32)
    mid = r1.astype(BF16)
    lo = (r1 - mid.astype(F32)).astype(BF16)
    return hi, mid, lo


def _dot3(sel, x, sel_is_lhs):
    out = None
    for piece in _split3(x):
        t = jnp.dot(sel, piece, preferred_element_type=F32) if sel_is_lhs else jnp.dot(piece, sel, preferred_element_type=F32)
        out = t if out is None else out + t
    return out


def _flog_fwd(flog, bpad, *, nb, ts, name):
    ns = SEQ // ts

    def body(f_ref, b_ref, c_ref, carry_ref):
        s = pl.program_id(1)

        @pl.when(s == 0)
        def _():
            carry_ref[...] = jnp.zeros_like(carry_ref)

        z = f_ref[...] + b_ref[...]
        logf = jnp.minimum(z, 0.0) - jnp.log(1.0 + jnp.exp(-jnp.abs(z)))
        r = lax.broadcasted_iota(jnp.int32, (ts, ts), 0)
        c = lax.broadcasted_iota(jnp.int32, (ts, ts), 1)
        tri = jnp.where(r >= c, 1.0, 0.0).astype(BF16)
        cs = _dot3(tri, logf, True) + carry_ref[0:1, :]
        carry_ref[...] = jnp.broadcast_to(cs[ts - 1:ts, :], carry_ref.shape)
        c_ref[...] = cs

    return pl.pallas_call(
        body,
        out_shape=jax.ShapeDtypeStruct((nb * SEQ, LANES), F32),
        grid=(nb, ns),
        in_specs=[pl.BlockSpec((ts, LANES), lambda b, s: (b * ns + s, 0)), pl.BlockSpec((1, LANES), lambda b, s: (0, 0))],
        out_specs=pl.BlockSpec((ts, LANES), lambda b, s: (b * ns + s, 0)),
        scratch_shapes=[pltpu.VMEM((8, LANES), F32)],
        compiler_params=_cparams(dimension_semantics=("parallel", "arbitrary")),
        name=name,
    )(flog, bpad)


def _flog_bwd(dcol, flog, bpad, *, nb, ts, name):
    ns = SEQ // ts

    def body(d_ref, f_ref, b_ref, o_ref, gb_ref, carry_ref):
        bi = pl.program_id(0)
        s = pl.program_id(1)

        @pl.when(s == 0)
        def _():
            carry_ref[...] = jnp.zeros_like(carry_ref)

        @pl.when(jnp.logical_and(bi == 0, s == 0))
        def _():
            gb_ref[...] = jnp.zeros_like(gb_ref)

        r = lax.broadcasted_iota(jnp.int32, (ts, ts), 0)
        c = lax.broadcasted_iota(jnp.int32, (ts, ts), 1)
        tri = jnp.where(r <= c, 1.0, 0.0).astype(BF16)
        rc = _dot3(tri, d_ref[...], True) + carry_ref[0:1, :]
        carry_ref[...] = jnp.broadcast_to(rc[0:1, :], carry_ref.shape)
        z = f_ref[...] + b_ref[...]
        dz = rc / (1.0 + jnp.exp(z))
        o_ref[...] = dz.astype(o_ref.dtype)
        gb_ref[...] += jnp.broadcast_to(jnp.sum(dz, axis=0, keepdims=True), gb_ref.shape)

    rev = lambda b, s: (b * ns + (ns - 1 - s), 0)
    return pl.pallas_call(
        body,
        out_shape=(jax.ShapeDtypeStruct((nb * SEQ, LANES), BF16), jax.ShapeDtypeStruct((8, LANES), F32)),
        grid=(nb, ns),
        in_specs=[pl.BlockSpec((ts, LANES), rev), pl.BlockSpec((ts, LANES), rev), pl.BlockSpec((1, LANES), lambda b, s: (0, 0))],
        out_specs=(pl.BlockSpec((ts, LANES), rev), pl.BlockSpec((8, LANES), lambda b, s: (0, 0))),
        scratch_shapes=[pltpu.VMEM((8, LANES), F32)],
        compiler_params=_cparams(dimension_semantics=("arbitrary", "arbitrary")),
        name=name,
    )(dcol, flog, bpad)


MEM_TQ = 256
MEM_SET = 4
MEM_SCALE = 1.0 / math.sqrt(MEM_HEAD_DIM)
assert MEM_HEAD_DIM == LANES and SEQ % (MEM_TQ * MEM_SET) == 0


def _head_masks(nh):
    lane = lax.broadcasted_iota(jnp.int32, (1, LANES), 1)
    return [None] if nh == 1 else [lane < HEAD_DIM, lane >= HEAD_DIM]


def _mem_specs(qoff):
    qspec = pl.BlockSpec((None, SEQ, LANES), lambda b, j: (b, 0, qoff + j))
    kspec = pl.BlockSpec((None, MEM_LEN, LANES), lambda b, j: (b, 0, j))
    vspec = pl.BlockSpec((None, MEM_LEN, LANES), lambda b, j: (b, 0, MEM_HEADS + j))
    ospec = pl.BlockSpec((None, SEQ, LANES), lambda b, j: (b, 0, j))
    return qspec, kspec, vspec, ospec


def _mem_rows(g):
    return [pl.ds(pl.multiple_of((MEM_SET * g + a) * MEM_TQ, MEM_TQ), MEM_TQ) for a in range(MEM_SET)]


def _mem_fwd(p3, mkv3, *, qoff, name):
    nb = p3.shape[0]

    def body(q_ref, k_ref, v_ref, o_ref, lse_ref):
        kb, vb = k_ref[...], v_ref[...]

        def qset(g, c):
            rows = _mem_rows(g)
            ss = [lax.dot_general(q_ref[r, :] * MEM_SCALE, kb, _NT, preferred_element_type=F32) for r in rows]
            for r, s in zip(rows, ss):
                m = jnp.max(s, axis=1, keepdims=True)
                p = jnp.exp(s - m)
                l = jnp.sum(p, axis=1, keepdims=True)
                o_ref[r, :] = jnp.dot(p.astype(BF16), vb, preferred_element_type=F32) / l
                lse_ref[r, :] = jnp.broadcast_to(m + jnp.log(l), (MEM_TQ, LANES))
            return c

        lax.fori_loop(0, SEQ // MEM_TQ // MEM_SET, qset, 0)

    qspec, kspec, vspec, ospec = _mem_specs(qoff)
    osd = jax.ShapeDtypeStruct((nb, SEQ, MEM_W), F32)
    return pl.pallas_call(body, out_shape=(osd, osd), grid=(nb, MEM_HEADS), in_specs=[qspec, kspec, vspec],
                          out_specs=(ospec, ospec), compiler_params=_cparams(dimension_semantics=("parallel", "parallel")),
                          name=name)(p3, mkv3, mkv3)


def _mem_bwd(p3, mkv3, do, o, lse, *, qoff, do_off, name):
    nb = p3.shape[0]

    def body(q_ref, k_ref, v_ref, do_ref, o_ref, lse_ref, dq_ref, dk_ref, dv_ref):
        kb, vb = k_ref[...], v_ref[...]
        ks = kb * MEM_SCALE

        def qset(g, carry):
            dk, dv = carry
            work = []
            for r in _mem_rows(g):
                qs = q_ref[r, :] * MEM_SCALE
                dob = do_ref[r, :].astype(BF16)
                s = lax.dot_general(qs, kb, _NT, preferred_element_type=F32)
                dp = lax.dot_general(dob, vb, _NT, preferred_element_type=F32)
                work.append((r, qs, dob, s, dp))
            for r, qs, dob, s, dp in work:
                delta = jnp.sum(dob.astype(F32) * o_ref[r, :], axis=1, keepdims=True)
                p = jnp.exp(s - lse_ref[r, :][:, 0:1])
                ds = (p * (dp - delta)).astype(BF16)
                dq_ref[r, :] = jnp.dot(ds, ks, preferred_element_type=F32).astype(dq_ref.dtype)
                dk = dk + lax.dot_general(ds, qs, _T0, preferred_element_type=F32)
                dv = dv + lax.dot_general(p.astype(BF16), dob, _T0, preferred_element_type=F32)
            return dk, dv

        z = jnp.zeros((MEM_LEN, LANES), F32)
        dk, dv = lax.fori_loop(0, SEQ // MEM_TQ // MEM_SET, qset, (z, z))
        dk_ref[...] = dk
        dv_ref[...] = dv

    qspec, kspec, vspec, ospec = _mem_specs(qoff)
    dospec = pl.BlockSpec((None, SEQ, LANES), lambda b, j: (b, 0, do_off + j))
    kvo = pl.BlockSpec((None, MEM_LEN, LANES), lambda b, j: (b, 0, j))
    kvsd = jax.ShapeDtypeStruct((nb, MEM_LEN, MEM_W), F32)
    return pl.pallas_call(
        body, out_shape=(jax.ShapeDtypeStruct((nb, SEQ, MEM_W), BF16), kvsd, kvsd), grid=(nb, MEM_HEADS),
        in_specs=[qspec, kspec, vspec, dospec, ospec, ospec], out_specs=(ospec, kvo, kvo),
        compiler_params=_cparams(dimension_semantics=("parallel", "parallel")), name=name)(p3, mkv3, mkv3, do, o, lse)


BLK = 128
NBLK = SEQ // BLK
QK_SCALE = 1.0 / math.sqrt(HEAD_DIM)
DIL_STEPS = tuple(d for _, d in DILATIONS)
assert all(w // d == BLK for w, d in DILATIONS)
_T0 = (((0,), (0,)), ((), ()))
_NT = (((1,), (1,)), ((), ()))


def _stack_heads(a, masks):
    z = jnp.zeros_like(a)
    return jnp.concatenate([jnp.where(masks[0], a, z), jnp.where(masks[1], a, z)], axis=0)


def _tri_bias(lower):
    r = lax.broadcasted_iota(jnp.int32, (BLK, BLK), 0)
    c = lax.broadcasted_iota(jnp.int32, (BLK, BLK), 1)
    return jnp.where((c <= r) if lower else (c >= r), 0.0, NEG_INF).astype(F32)


def _dil_rows(r, i, d):
    start = r + i * (BLK * d)
    return pl.ds(start, BLK) if d == 1 else pl.ds(start, BLK, stride=d)


DIL_SET = 4


def _dil_sets(d, fn):
    nbk = SEQ // d // BLK
    if d == 1:
        n = 2 * DIL_SET
        def gbody(g, c):
            fn([(0, n * g + a, None if a == 0 else True) for a in range(n)])
            return c
        lax.fori_loop(0, nbk // n, gbody, 0)
    elif nbk > 1:
        assert nbk == DIL_SET
        def rbody(r, c):
            fn([(r, i, i > 0) for i in range(nbk)])
            return c
        lax.fori_loop(0, d, rbody, 0)
    else:
        def rbody(rr, c):
            fn([(DIL_SET * rr + a, 0, False) for a in range(DIL_SET)])
            return c
        lax.fori_loop(0, d // DIL_SET, rbody, 0)


def _dil_key_tiles(r, i, d, has_prev, qrows, tri_cur, tri_prev):
    tiles = [(qrows, tri_cur)]
    if has_prev is None:
        tiles.append((_dil_rows(r, jnp.maximum(i - 1, 0), d), tri_prev + jnp.where(i > 0, 0.0, NEG_INF)))
    elif has_prev:
        tiles.append((_dil_rows(r, i - 1, d), tri_prev))
    return tiles


def _dil_fwd(qkv, *, name):
    nb = qkv.shape[0]
    ncol = DIL_W // LANES
    hd = HEAD_DIM

    def body(q_ref, k_ref, v_ref, o_ref, lse_ref, m_ref, l_ref, a_ref):
        masks = _head_masks(2)
        tri_cur, tri_prev = _tri_bias(True), _tri_bias(False)
        for pi, d in enumerate(DIL_STEPS):
            first, last = pi == 0, pi == len(DIL_STEPS) - 1

            def qset(blocks, d=d, first=first, last=last):
                work = []
                for r, i, has_prev in blocks:
                    qrows = _dil_rows(r, i, d)
                    qcat = _stack_heads((q_ref[qrows, :] * QK_SCALE).astype(BF16), masks)
                    ss, krs = [], []
                    for krows, bias in _dil_key_tiles(r, i, d, has_prev, qrows, tri_cur, tri_prev):
                        s = lax.dot_general(qcat, k_ref[krows, :].astype(BF16), _NT, preferred_element_type=F32)
                        ss.append((s[:BLK] + bias, s[BLK:] + bias))
                        krs.append(krows)
                    work.append((qrows, ss, krs))
                for qrows, ss, krs in work:
                    e0 = ss[0][0] if len(ss) == 1 else jnp.maximum(ss[0][0], ss[1][0])
                    e1 = ss[0][1] if len(ss) == 1 else jnp.maximum(ss[0][1], ss[1][1])
                    n0 = jnp.max(e0, axis=1, keepdims=True)
                    n1 = jnp.max(e1, axis=1, keepdims=True)
                    if not first:
                        mo, lo = m_ref[qrows, :], l_ref[qrows, :]
                        m0, m1 = mo[:, 0:1], mo[:, hd:hd + 1]
                        n0, n1 = jnp.maximum(n0, m0), jnp.maximum(n1, m1)
                        a0, a1 = jnp.exp(m0 - n0), jnp.exp(m1 - n1)
                    ps = [(jnp.exp(s0 - n0), jnp.exp(s1 - n1)) for s0, s1 in ss]
                    t0 = ps[0][0] if len(ps) == 1 else ps[0][0] + ps[1][0]
                    t1 = ps[0][1] if len(ps) == 1 else ps[0][1] + ps[1][1]
                    l0 = jnp.sum(t0, axis=1, keepdims=True)
                    l1 = jnp.sum(t1, axis=1, keepdims=True)
                    acc = None
                    for (p0, p1), krows in zip(ps, krs):
                        vcat = _stack_heads(v_ref[krows, :].astype(BF16), masks)
                        pv = jnp.dot(jnp.concatenate([p0, p1], axis=1).astype(BF16), vcat, preferred_element_type=F32)
                        acc = pv if acc is None else acc + pv
                    if not first:
                        l0 = l0 + a0 * lo[:, 0:1]
                        l1 = l1 + a1 * lo[:, hd:hd + 1]
                        acc = acc + a_ref[qrows, :] * jnp.where(masks[0], a0, a1)
                    if last:
                        o_ref[qrows, :] = acc / jnp.where(masks[0], l0, l1)
                        lse_ref[qrows, :] = jnp.where(masks[0], n0 + jnp.log(l0), n1 + jnp.log(l1))
                    else:
                        m_ref[qrows, :] = jnp.where(masks[0], n0, n1)
                        l_ref[qrows, :] = jnp.where(masks[0], l0, l1)
                        a_ref[qrows, :] = acc

            _dil_sets(d, qset)

    spec = lambda off: pl.BlockSpec((None, SEQ, LANES), lambda b, j: (b, 0, off + j))
    ospec = pl.BlockSpec((None, SEQ, LANES), lambda b, j: (b, 0, j))
    osd = jax.ShapeDtypeStruct((nb, SEQ, DIL_W), F32)
    return pl.pallas_call(
        body, out_shape=(osd, osd), grid=(nb, ncol),
        in_specs=[spec(0), spec(ncol), spec(2 * ncol)], out_specs=(ospec, ospec),
        scratch_shapes=[pltpu.VMEM((SEQ, LANES), F32)] * 3,
        compiler_params=_cparams(dimension_semantics=("parallel", "parallel")), name=name,
    )(qkv, qkv, qkv)


def _dil_bwd(qkv, do, o, lse, tabs, *, do_off, name):
    nb = qkv.shape[0]
    ncol = DIL_W // LANES
    hd = HEAD_DIM

    def body(q_ref, k_ref, v_ref, do_ref, o_ref, lse_ref, c_ref, s1_ref, s2_ref, dqo_ref, dko_ref, dvo_ref,
             dq_ref, dk_ref, dv_ref, dl_ref, dof_ref):
        masks = _head_masks(2)
        tri_cur, tri_prev = _tri_bias(True), _tri_bias(False)
        dq_ref[...] = jnp.zeros_like(dq_ref)
        dk_ref[...] = jnp.zeros_like(dk_ref)
        dv_ref[...] = jnp.zeros_like(dv_ref)

        def delta_body(i, c):
            rows = pl.ds(pl.multiple_of(i * BLK, BLK), BLK)
            dof = do_ref[rows, :].astype(F32)
            dof_ref[rows, :] = dof
            prod = dof * o_ref[rows, :]
            z = jnp.zeros_like(prod)
            dl_ref[rows, :] = jnp.where(masks[0], jnp.sum(jnp.where(masks[0], prod, z), axis=1, keepdims=True),
                                        jnp.sum(jnp.where(masks[1], prod, z), axis=1, keepdims=True))
            return c

        lax.fori_loop(0, NBLK, delta_body, 0)

        for d in DIL_STEPS:
            def qset(blocks, d=d):
                work = []
                for r, i, has_prev in blocks:
                    qrows = _dil_rows(r, i, d)
                    qcat = _stack_heads((q_ref[qrows, :] * QK_SCALE).astype(BF16), masks)
                    docat = _stack_heads(dof_ref[qrows, :].astype(BF16), masks)
                    tiles = []
                    for krows, bias in _dil_key_tiles(r, i, d, has_prev, qrows, tri_cur, tri_prev):
                        s = lax.dot_general(qcat, k_ref[krows, :].astype(BF16), _NT, preferred_element_type=F32)
                        dp = lax.dot_general(docat, v_ref[krows, :].astype(BF16), _NT, preferred_element_type=F32)
                        tiles.append((krows, s, dp, bias))
                    work.append((qrows, qcat, docat, tiles))
                for qrows, qcat, docat, tiles in work:
                    lseb, dlb = lse_ref[qrows, :], dl_ref[qrows, :]
                    lse0, lse1 = lseb[:, 0:1], lseb[:, hd:hd + 1]
                    dl0, dl1 = dlb[:, 0:1], dlb[:, hd:hd + 1]
                    dq = None
                    for krows, s, dp, bias in tiles:
                        p0 = jnp.exp(s[:BLK] + bias - lse0)
                        p1 = jnp.exp(s[BLK:] + bias - lse1)
                        ds0 = p0 * (dp[:BLK] - dl0)
                        ds1 = p1 * (dp[BLK:] - dl1)
                        ds0b, ds1b = ds0.astype(BF16), ds1.astype(BF16)
                        pcat = jnp.concatenate([p0.astype(BF16), p1.astype(BF16)], axis=0)
                        dscat = jnp.concatenate([ds0b, ds1b], axis=0)
                        dv_ref[krows, :] += lax.dot_general(pcat, docat, _T0, preferred_element_type=F32)
                        dk_ref[krows, :] += lax.dot_general(dscat, qcat, _T0, preferred_element_type=F32)
                        dsrow = jnp.concatenate([ds0b, ds1b], axis=1)
                        kcat = _stack_heads((k_ref[krows, :] * QK_SCALE).astype(BF16), masks)
                        t = jnp.dot(dsrow, kcat, preferred_element_type=F32)
                        dq = t if dq is None else dq + t
                    dq_ref[qrows, :] += dq

            _dil_sets(d, qset)

        def out_body(i, c):
            rows = pl.ds(pl.multiple_of(i * BLK, BLK), BLK)
            tab = (c_ref[rows, :], s1_ref[rows, :], s2_ref[rows, :])
            dqo_ref[rows, :] = _rope_apply(dq_ref[rows, :], *tab, transpose=True).astype(dqo_ref.dtype)
            dko_ref[rows, :] = _rope_apply(dk_ref[rows, :], *tab, transpose=True).astype(dko_ref.dtype)
            dvo_ref[rows, :] = dv_ref[rows, :].astype(dvo_ref.dtype)
            return c

        lax.fori_loop(0, NBLK, out_body, 0)

    spec = lambda off: pl.BlockSpec((None, SEQ, LANES), lambda b, j: (b, 0, off + j))
    ospec = pl.BlockSpec((None, SEQ, LANES), lambda b, j: (b, 0, j))
    tspec = pl.BlockSpec((SEQ, LANES), lambda b, j: (0, 0))
    osd = jax.ShapeDtypeStruct((nb, SEQ, DIL_W), BF16)
    return pl.pallas_call(
        body, out_shape=(osd, osd, osd), grid=(nb, ncol),
        in_specs=[spec(0), spec(ncol), spec(2 * ncol), spec(do_off), ospec, ospec, tspec, tspec, tspec],
        out_specs=(ospec, ospec, ospec),
        scratch_shapes=[pltpu.VMEM((SEQ, LANES), F32)] * 5,
        compiler_params=_cparams(dimension_semantics=("parallel", "parallel")), name=name,
    )(qkv, qkv, qkv, do, o, lse, *tabs)


FOX_GROUP = 4
assert NBLK % FOX_GROUP == 0
_FOX_COLS = tuple(c // LANES for c in (C_FQ, C_FK, C_FV))


def _fox_specs():
    cols = [pl.BlockSpec((None, SEQ, LANES), (lambda b, j, off=off: (b, 0, off + j))) for off in _FOX_COLS]
    ospec = pl.BlockSpec((None, SEQ, LANES), lambda b, j: (b, 0, j))
    crspec = pl.BlockSpec((None, None, NBLK, 8, BLK), lambda b, j: (b, j, 0, 0, 0))
    return cols, ospec, crspec


def _fox_key_rows(t, e):
    return pl.ds(pl.multiple_of((FOX_GROUP * t + e) * BLK, BLK), BLK)


def _fox_fwd(p3, crow, *, name):
    nb = p3.shape[0]
    g = FOX_GROUP

    def body(q_ref, k_ref, v_ref, cr_ref, o_ref, lse_ref):
        masks = _head_masks(2)
        tri = _tri_bias(True)

        def qk(qcat, t):
            return tuple(lax.dot_general(qcat, k_ref[_fox_key_rows(t, e), :], _NT, preferred_element_type=F32) for e in range(g))

        def consume(ss, t, state, nblk, diag):
            m0, m1, l0, l1, acc = state
            us = []
            for e in range(nblk):
                cr = cr_ref[g * t + e]
                u0 = ss[e][:BLK] - cr[0:1, :]
                u1 = ss[e][BLK:] - cr[1:2, :]
                if diag and e == nblk - 1:
                    u0, u1 = u0 + tri, u1 + tri
                us.append((u0, u1))
            x0 = functools.reduce(jnp.maximum, [u[0] for u in us])
            x1 = functools.reduce(jnp.maximum, [u[1] for u in us])
            n0 = jnp.maximum(m0, jnp.max(x0, axis=1, keepdims=True))
            n1 = jnp.maximum(m1, jnp.max(x1, axis=1, keepdims=True))
            a0, a1 = jnp.exp(m0 - n0), jnp.exp(m1 - n1)
            acc = acc * jnp.where(masks[0], a0, a1)
            t0 = t1 = None
            for e in range(nblk):
                p0, p1 = jnp.exp(us[e][0] - n0), jnp.exp(us[e][1] - n1)
                t0 = p0 if t0 is None else t0 + p0
                t1 = p1 if t1 is None else t1 + p1
                pcat = jnp.concatenate([p0, p1], axis=1)
                hi = pcat.astype(BF16)
                lo = (pcat - hi.astype(F32)).astype(BF16)
                vcat = _stack_heads(v_ref[_fox_key_rows(t, e), :], masks)
                acc = acc + jnp.dot(hi, vcat, preferred_element_type=F32) + jnp.dot(lo, vcat, preferred_element_type=F32)
            l0 = a0 * l0 + jnp.sum(t0, axis=1, keepdims=True)
            l1 = a1 * l1 + jnp.sum(t1, axis=1, keepdims=True)
            return n0, n1, l0, l1, acc

        def gbody(ng, c):
            neg = jnp.full((BLK, 1), NEG_INF, F32)
            z1 = jnp.zeros((BLK, 1), F32)
            rows = [pl.ds(pl.multiple_of((g * ng + a) * BLK, BLK), BLK) for a in range(g)]
            qcats = [_stack_heads(q_ref[rows[a], :] * QK_SCALE, masks) for a in range(g)]
            first = [qk(qcats[a], 0) for a in range(g)]
            done = []
            for a in range(g):
                def step(t, cc, qcat=qcats[a]):
                    ss, st = cc
                    nxt = qk(qcat, t + 1)
                    return nxt, consume(ss, t, st, g, False)

                done.append(lax.fori_loop(0, ng, step, (first[a], (neg, neg, z1, z1, jnp.zeros((BLK, LANES), F32)))))
            for a in range(g):
                ss, state = done[a]
                m0, m1, l0, l1, acc = consume(ss, ng, state, a + 1, True)
                o_ref[rows[a], :] = acc / jnp.where(masks[0], l0, l1)
                lse_ref[rows[a], :] = jnp.where(masks[0], m0 + jnp.log(l0), m1 + jnp.log(l1))
            return c

        lax.fori_loop(0, NBLK // g, gbody, 0)

    cols, ospec, crspec = _fox_specs()
    osd = jax.ShapeDtypeStruct((nb, SEQ, FOX_W), F32)
    return pl.pallas_call(
        body, out_shape=(osd, osd), grid=(nb, FOX_W // LANES), in_specs=cols + [crspec], out_specs=(ospec, ospec),
        compiler_params=_cparams(dimension_semantics=("parallel", "parallel")), name=name,
    )(p3, p3, p3, crow)


def _fox_bwd(p3, crow, do, o, lse, *, do_off, name):
    nb = p3.shape[0]
    g = FOX_GROUP
    hd = HEAD_DIM

    def body(q_ref, k_ref, v_ref, cr_ref, do_ref, o_ref, lse_ref, dq_ref, dko_ref, dvo_ref, dcr_ref, dk_ref, dv_ref):
        masks = _head_masks(2)
        tri = _tri_bias(True)
        dk_ref[...] = jnp.zeros_like(dk_ref)
        dv_ref[...] = jnp.zeros_like(dv_ref)
        dcr_ref[...] = jnp.zeros_like(dcr_ref)

        def products(qcat, docat, t):
            out = []
            for e in range(g):
                krows = _fox_key_rows(t, e)
                out.append(lax.dot_general(qcat, k_ref[krows, :], _NT, preferred_element_type=F32))
                out.append(lax.dot_general(docat, v_ref[krows, :], _NT, preferred_element_type=F32))
            return tuple(out)

        def consume(prod, t, ctx, dq, nblk, diag):
            qcat, docat, lse0, lse1, dl0, dl1 = ctx
            for e in range(nblk):
                jb = g * t + e
                krows = _fox_key_rows(t, e)
                s, dp = prod[2 * e], prod[2 * e + 1]
                cr = cr_ref[jb]
                u0 = s[:BLK] - cr[0:1, :]
                u1 = s[BLK:] - cr[1:2, :]
                if diag and e == nblk - 1:
                    u0, u1 = u0 + tri, u1 + tri
                p0 = jnp.exp(u0 - lse0)
                p1 = jnp.exp(u1 - lse1)
                ds0 = p0 * (dp[:BLK] - dl0)
                ds1 = p1 * (dp[BLK:] - dl1)
                dcr_ref[jb, 0:1, :] += jnp.sum(ds0, axis=0, keepdims=True)
                dcr_ref[jb, 1:2, :] += jnp.sum(ds1, axis=0, keepdims=True)
                ds0b, ds1b = ds0.astype(BF16), ds1.astype(BF16)
                pcat = jnp.concatenate([p0.astype(BF16), p1.astype(BF16)], axis=0)
                dscat = jnp.concatenate([ds0b, ds1b], axis=0)
                dv_ref[krows, :] += lax.dot_general(pcat, docat, _T0, preferred_element_type=F32)
                dk_ref[krows, :] += lax.dot_general(dscat, qcat, _T0, preferred_element_type=F32)
                dsrow = jnp.concatenate([ds0b, ds1b], axis=1)
                dq = dq + jnp.dot(dsrow, _stack_heads(k_ref[krows, :] * QK_SCALE, masks), preferred_element_type=F32)
            return dq

        def gbody(ng, c):
            ctxs, rows = [], []
            for a in range(g):
                r = pl.ds(pl.multiple_of((g * ng + a) * BLK, BLK), BLK)
                qcat = _stack_heads(q_ref[r, :] * QK_SCALE, masks)
                dob = do_ref[r, :].astype(BF16)
                prod = dob.astype(F32) * o_ref[r, :]
                z = jnp.zeros_like(prod)
                dl0 = jnp.sum(jnp.where(masks[0], prod, z), axis=1, keepdims=True)
                dl1 = jnp.sum(jnp.where(masks[1], prod, z), axis=1, keepdims=True)
                lseb = lse_ref[r, :]
                ctxs.append((qcat, _stack_heads(dob, masks), lseb[:, 0:1], lseb[:, hd:hd + 1], dl0, dl1))
                rows.append(r)
            first = [products(ctxs[a][0], ctxs[a][1], 0) for a in range(g)]
            done = []
            for a in range(g):
                def step(t, cc, ctx=ctxs[a]):
                    pr, dq = cc
                    nxt = products(ctx[0], ctx[1], t + 1)
                    return nxt, consume(pr, t, ctx, dq, g, False)

                done.append(lax.fori_loop(0, ng, step, (first[a], jnp.zeros((BLK, LANES), F32))))
            for a in range(g):
                pr, dq = done[a]
                dq_ref[rows[a], :] = consume(pr, ng, ctxs[a], dq, a + 1, True).astype(dq_ref.dtype)
            return c

        lax.fori_loop(0, NBLK // g, gbody, 0)
        dko_ref[...] = dk_ref[...].astype(dko_ref.dtype)
        dvo_ref[...] = dv_ref[...].astype(dvo_ref.dtype)

    cols, ospec, crspec = _fox_specs()
    dospec = pl.BlockSpec((None, SEQ, LANES), lambda b, j: (b, 0, do_off + j))
    osd = jax.ShapeDtypeStruct((nb, SEQ, FOX_W), BF16)
    return pl.pallas_call(
        body, out_shape=(osd, osd, osd, jax.ShapeDtypeStruct((nb, FOX_W // LANES, NBLK, 8, BLK), F32)),
        grid=(nb, FOX_W // LANES), in_specs=cols + [crspec, dospec, ospec, ospec], out_specs=(ospec, ospec, ospec, crspec),
        scratch_shapes=[pltpu.VMEM((SEQ, LANES), F32)] * 2,
        compiler_params=_cparams(dimension_semantics=("parallel", "parallel")), name=name,
    )(p3, p3, p3, crow, do, o, lse)


_B1, _B2 = FOX_W // LANES, (FOX_W + DIL_W) // LANES


def _dy_gate_bwd(dx2b, wo, fox, dil, memo, p16, *, tm, tn, name):
    t, d = dx2b.shape
    assert FOX_W % tn == 0 and DIL_W % tn == 0 and MEM_W % tn == 0 and all(c % tn == 0 for c in (C_FG, C_DG, C_MG))
    n1, n2, n3 = FOX_W // tn, (FOX_W + DIL_W) // tn, MIX_W // tn

    def body(dx_ref, w_ref, f_ref, d_ref, m_ref, g_ref, da_ref, dg_ref):
        j = pl.program_id(1)
        wv = w_ref[...]
        for c0 in range(0, tm, min(tm, 2 * MM_CHUNK)):
            rows = pl.ds(c0, min(tm, 2 * MM_CHUNK))
            dyv = lax.dot_general(dx_ref[rows, :], wv, _NT, preferred_element_type=F32)
            a = jnp.where(j < n1, f_ref[rows, :], jnp.where(j < n2, d_ref[rows, :], m_ref[rows, :]))
            gt = g_ref[rows, :].astype(F32)
            sg = 1.0 / (1.0 + jnp.exp(-gt))
            da_ref[rows, :] = (dyv * gt * sg).astype(da_ref.dtype)
            dg_ref[rows, :] = (dyv * a * sg * (1.0 + gt * (1.0 - sg))).astype(dg_ref.dtype)

    def gcol(j):
        return jnp.where(j < n1, C_FG // tn + j, jnp.where(j < n2, C_DG // tn + j - n1, C_MG // tn + j - n2))

    tile = pl.BlockSpec((tm, tn), lambda i, j: (i, j))
    return pl.pallas_call(
        body,
        out_shape=(jax.ShapeDtypeStruct((t, MIX_W), BF16), jax.ShapeDtypeStruct((t, MIX_W), BF16)),
        grid=(t // tm, n3),
        in_specs=[pl.BlockSpec((tm, d), lambda i, j: (i, 0)), pl.BlockSpec((tn, d), lambda i, j: (j, 0)),
                  pl.BlockSpec((tm, tn), lambda i, j: (i, jnp.minimum(j, n1 - 1))),
                  pl.BlockSpec((tm, tn), lambda i, j: (i, jnp.clip(j - n1, 0, n2 - n1 - 1))),
                  pl.BlockSpec((tm, tn), lambda i, j: (i, jnp.clip(j - n2, 0, n3 - n2 - 1))),
                  pl.BlockSpec((tm, tn), lambda i, j: (i, gcol(j)))],
        out_specs=(tile, tile),
        compiler_params=_cparams(dimension_semantics=("parallel", "parallel")),
        name=name,
    )(dx2b, wo, fox, dil, memo, p16)


def _silu(g):
    return g / (1.0 + jnp.exp(-g))


def _out_loss(fox, dil, memo, p16, wo, x, tgt, gfin, *, tm, name):
    t, d = x.shape
    n_feat = float(d)

    def body(f_ref, d_ref, m_ref, fg_ref, dg_ref, mg_ref, w_ref, x_ref, t_ref, g_ref, y_ref, dx_ref, dxb_ref, st_ref):
        i = pl.program_id(0)

        @pl.when(i == 0)
        def _():
            st_ref[...] = jnp.zeros_like(st_ref)

        wv, gv = w_ref[...], g_ref[...]
        half = tm // 2
        for c0 in (0, half):
            rows = pl.ds(c0, half)
            y = jnp.concatenate([(a_ref[rows, :] * _silu(gt_ref[rows, :].astype(F32))).astype(BF16)
                                 for a_ref, gt_ref in ((f_ref, fg_ref), (d_ref, dg_ref), (m_ref, mg_ref))], axis=1)
            y_ref[rows, :] = y
            x2 = x_ref[rows, :] + jnp.dot(y, wv, preferred_element_type=F32)
            r = lax.rsqrt(jnp.mean(x2 * x2, axis=-1, keepdims=True) + RMS_EPS)
            nrm = x2 * r
            err = nrm * gv - t_ref[rows, :]
            dout = err * (1.0 / n_feat)
            dn = dout * gv
            dx2 = r * (dn - nrm * jnp.mean(dn * nrm, axis=-1, keepdims=True))
            dx_ref[rows, :] = dx2
            dxb_ref[rows, :] = dx2.astype(dxb_ref.dtype)
            st_ref[0:1, :] += jnp.sum(dout * nrm, axis=0, keepdims=True)
            st_ref[1:2, :] += (0.5 / n_feat) * jnp.sum(err * err, axis=0, keepdims=True)

    row = pl.BlockSpec((tm, d), lambda i: (i, 0))
    whole = lambda w: pl.BlockSpec((tm, w), lambda i: (i, 0))
    gate = lambda w, col: pl.BlockSpec((tm, w), lambda i: (i, col // w))
    return pl.pallas_call(
        body,
        out_shape=(jax.ShapeDtypeStruct((t, MIX_W), BF16), jax.ShapeDtypeStruct((t, d), F32), jax.ShapeDtypeStruct((t, d), BF16),
                   jax.ShapeDtypeStruct((8, d), F32)),
        grid=(t // tm,),
        in_specs=[whole(FOX_W), whole(DIL_W), whole(MEM_W), gate(FOX_W, C_FG), gate(DIL_W, C_DG), gate(MEM_W, C_MG),
                  pl.BlockSpec((MIX_W, d), lambda i: (0, 0)), row, row, pl.BlockSpec((1, d), lambda i: (0, 0))],
        out_specs=(pl.BlockSpec((tm, MIX_W), lambda i: (i, 0)), row, row, pl.BlockSpec((8, d), lambda i: (0, 0))),
        compiler_params=_cparams(dimension_semantics=("arbitrary",)),
        name=name,
    )(fox, dil, memo, p16, p16, p16, wo, x, tgt, gfin)


def _dh_rms_bwd(dp, w, x, g, resid, *, tm, name):
    t, d = x.shape
    kdim = dp.shape[1]

    def body(*refs):
        if resid is not None:
            dp_ref, w_ref, x_ref, g_ref, r_ref, dx_ref, gg_ref = refs
        else:
            dp_ref, w_ref, x_ref, g_ref, dx_ref, gg_ref = refs

        @pl.when(pl.program_id(0) == 0)
        def _():
            gg_ref[...] = jnp.zeros_like(gg_ref)

        dh = lax.dot_general(dp_ref[...], w_ref[...], _NT, preferred_element_type=F32)
        xv = x_ref[...]
        r = lax.rsqrt(jnp.mean(xv * xv, axis=-1, keepdims=True) + RMS_EPS)
        nrm = xv * r
        dn = dh * g_ref[...]
        dx = r * (dn - nrm * jnp.mean(dn * nrm, axis=-1, keepdims=True))
        if resid is not None:
            dx = dx + r_ref[...]
        dx_ref[...] = dx
        gg_ref[0:1, :] += jnp.sum(dh * nrm, axis=0, keepdims=True)

    row = pl.BlockSpec((tm, d), lambda i: (i, 0))
    in_specs = [pl.BlockSpec((tm, kdim), lambda i: (i, 0)),
                pl.BlockSpec((d, kdim), lambda i: (0, 0), pipeline_mode=pl.Buffered(1)), row,
                pl.BlockSpec((1, d), lambda i: (0, 0))]
    args = [dp, w, x, g]
    if resid is not None:
        in_specs.append(row)
        args.append(resid)
    return pl.pallas_call(
        body,
        out_shape=(jax.ShapeDtypeStruct((t, d), F32), jax.ShapeDtypeStruct((8, d), F32)),
        grid=(t // tm,),
        in_specs=in_specs,
        out_specs=(row, pl.BlockSpec((8, d), lambda i: (0, 0))),
        compiler_params=_cparams(dimension_semantics=("arbitrary",)),
        name=name,
    )(*args)


_FLOG0 = 4 * FOX_W
_W_IN_SEGMENTS = ((0, _FLOG0, 0), (_FLOG0, _FLOG0 + FOX_HEADS, PW), (_FLOG0 + FOX_HEADS, IN_W, C_DQ))
SHARD_W = IN_W // N_CHIPS


def _rearrange_w_in(shards):
    def cols(lo, hi):
        parts = []
        for k in range(N_CHIPS):
            a, b = max(lo, k * SHARD_W), min(hi, (k + 1) * SHARD_W)
            if a < b:
                parts.append(shards[k][:, a - k * SHARD_W:b - k * SHARD_W])
        return parts

    (a0, a1, _), (f0, f1, _), (b0, b1, _) = _W_IN_SEGMENTS
    pad = jnp.zeros((shards[0].shape[0], PWF - PW - FOX_HEADS), shards[0].dtype)
    return jnp.concatenate(cols(a0, a1) + cols(b0, b1) + cols(f0, f1) + [pad], axis=1)


def _w_in_grad_slabs(g):
    slabs = []
    for k in range(N_CHIPS):
        parts = []
        for lo, hi, at in _W_IN_SEGMENTS:
            a, b = max(lo, k * SHARD_W), min(hi, (k + 1) * SHARD_W)
            if a < b:
                parts.append(g[:, at + a - lo:at + b - lo])
        slabs.append(jnp.concatenate(parts, axis=1))
    return jnp.stack(slabs, axis=0)


def _local_grads(x, mem, norm_g, w_r, b_forget, mem_norm_g, w_kv, w_o, final_norm_g, tgt, start_reduce=None,
                 start_reduce_small=None, early_token=None, late_weights=None):
    nb = x.shape[0]
    t = nb * SEQ
    x2d = x.reshape(t, D_MODEL)
    tgt2d = tgt.reshape(t, D_MODEL)
    tabs = _rope_tables()
    bpad = jnp.pad(b_forget.reshape(1, FOX_HEADS), ((0, 0), (0, LANES - FOX_HEADS)))

    gain0 = norm_g.reshape(1, D_MODEL)
    if early_token is not None:
        gain0 = gain0 + early_token[0:1, 0:1]
    h, p16, dqkv, flog = _proj(x2d, gain0, w_r, tabs, n=PWF, tm=1024, tn=768, name="proj")
    c12 = _flog_fwd(flog, bpad, nb=nb, ts=256, name="flog_fwd")

    crow = c12[:, :FOX_HEADS].reshape(nb, NBLK, BLK, FOX_HEADS // 2, 2).transpose(0, 3, 1, 4, 2)
    crow = jnp.pad(crow, ((0, 0), (0, 0), (0, 0), (0, 6), (0, 0)))
    p3 = p16.reshape(nb, SEQ, PWF)
    fox, fox_lse = _fox_fwd(p3, crow, name="fox_fwd")
    if late_weights is not None:
        w_kv, w_o = late_weights(fox_lse)

    dqkv3 = dqkv.reshape(nb, SEQ, 3 * DIL_W)
    dil, dil_lse = _dil_fwd(dqkv3, name="dil_fwd")

    mh = _rms_fwd(mem.reshape(nb * MEM_LEN, D_MODEL), mem_norm_g.reshape(1, D_MODEL), tm=nb * MEM_LEN, name="rms_mem")
    mkv = _matmul(mh, w_kv, out_dtype=BF16, tm=nb * MEM_LEN, tn=512, tk=D_MODEL, name="mem_kv")
    mkv3 = mkv.reshape(nb, MEM_LEN, 2 * MEM_W)
    memo, mem_lse = _mem_fwd(p3, mkv3, qoff=C_MQ // LANES, name="mem_fwd")

    fox2, dil2, memo2 = fox.reshape(t, FOX_W), dil.reshape(t, DIL_W), memo.reshape(t, MEM_W)
    y, dx2, dx2b, st = _out_loss(fox2, dil2, memo2, p16, w_o, x2d, tgt2d, final_norm_g.reshape(1, D_MODEL), tm=256,
                                 name="out_loss")

    g_wo = _matmul(y, dx2b, mode="tn", out_dtype=BF16, tm=1024, tn=512, tk=t, name="grad_w_out")
    datt, dgate = _dy_gate_bwd(dx2b, w_o, fox2, dil2, memo2, p16, tm=2048, tn=256, name="dy_gate_bwd")
    datt3 = datt.reshape(nb, SEQ, MIX_W)

    dmq, dmk, dmv = _mem_bwd(p3, mkv3, datt3, memo, mem_lse, qoff=C_MQ // LANES, do_off=_B2, name="mem_bwd")
    dmkv = jnp.concatenate([dmk, dmv], axis=-1).reshape(nb * MEM_LEN, 2 * MEM_W).astype(BF16)
    g_wkv = _matmul(mh, dmkv, mode="tn", out_dtype=BF16, tm=512, tn=512, tk=nb * MEM_LEN, name="grad_w_kv")
    mem_gain = mem_norm_g.reshape(1, D_MODEL)
    if start_reduce_small is not None:
        tok = start_reduce_small(g_wkv, g_wo)[0:1, 0:1]
        mem_gain, crow = mem_gain + tok, crow + tok
    _, gmn = _dh_rms_bwd(dmkv, w_kv, mem.reshape(nb * MEM_LEN, D_MODEL), mem_gain, None, tm=nb * MEM_LEN, name="mem_rms_bwd")

    dfq, dfk, dfv, dcr = _fox_bwd(p3, crow, datt3, fox, fox_lse, do_off=0, name="fox_bwd")
    dcol = -dcr[:, :, :, :2, :].transpose(0, 2, 4, 1, 3).reshape(t, FOX_HEADS)
    dcol = jnp.pad(dcol, ((0, 0), (0, LANES - FOX_HEADS)))
    dflog, gb = _flog_bwd(dcol, flog, bpad, nb=nb, ts=256, name="flog_bwd")

    ddq, ddk, ddv = _dil_bwd(dqkv3, datt3, dil, dil_lse, tabs, do_off=_B1, name="dil_bwd")

    flat = lambda a: a.reshape(t, -1)
    dp = jnp.concatenate([flat(dfq), flat(dfk), flat(dfv), dgate[:, :FOX_W], flat(ddq), flat(ddk), flat(ddv),
                          dgate[:, FOX_W:FOX_W + DIL_W], flat(dmq), dgate[:, FOX_W + DIL_W:], dflog,
                          jnp.zeros((t, PWF - PW - LANES), BF16)], axis=1)
    g_wr = _matmul(h, dp, mode="tn", out_dtype=BF16, tm=D_MODEL, tn=768, tk=t, name="grad_w_in")
    gain = norm_g.reshape(1, D_MODEL)
    if start_reduce is not None:
        gain = gain + start_reduce(g_wr)[0:1, 0:1]
    gx, gng = _dh_rms_bwd(dp, w_r, x2d, gain, dx2, tm=256, name="in_rms_bwd")

    gb_row = jnp.pad(gb[0:1, :], ((0, 0), (0, D_MODEL - LANES)))
    small = jnp.concatenate([gng[0:1], gmn[0:1], st[0:1], gb_row, st[1:2], jnp.zeros((3, D_MODEL), F32)], axis=0)
    return gx.reshape(nb, SEQ, D_MODEL), g_wr, g_wkv, g_wo, small


MESH = pl.DeviceIdType.MESH
ANY = pl.BlockSpec(memory_space=pl.ANY)


def _place():
    x, y, c = lax.axis_index("x"), lax.axis_index("y"), lax.axis_index("c")
    other_chips = [(1 - x, y), (x, 1 - y), (1 - x, 1 - y)]
    return x, y, c, other_chips


def _gather_weights(shards):
    n = len(shards)

    def body(*refs):
        in_refs, out_refs = refs[:n], refs[n:2 * n]
        send_sems, recv_sems = refs[2 * n:]
        x, y, c, chips = _place()
        me_chip = 2 * x + y
        sibling = (x, y, 1 - c)

        def half(ref, pc, rows):
            return ref.at[pl.ds(pc * (rows // 2), rows // 2), :]

        def rcopy(k, src, dst, to):
            return pltpu.make_async_remote_copy(src_ref=src, dst_ref=dst, send_sem=send_sems.at[k], recv_sem=recv_sems.at[k],
                                                device_id=to, device_id_type=MESH)

        sends = []
        for t in range(n):
            rows = shards[t].shape[0]
            for j, chip in enumerate(chips):
                cp = rcopy(6 * t + j, half(in_refs[t], c, rows), half(out_refs[t].at[me_chip], c, rows), (*chip, c))
                cp.start()
                sends.append(cp)
        for t in range(n):
            rows = shards[t].shape[0]
            for j, chip in enumerate(chips):
                slot = out_refs[t].at[2 * chip[0] + chip[1]]
                rcopy(6 * t + j, half(slot, c, rows), half(slot, c, rows), sibling).wait_recv()
                fw = rcopy(6 * t + 3 + j, half(slot, c, rows), half(slot, c, rows), sibling)
                fw.start()
                sends.append(fw)
        for t in range(n):
            rows = shards[t].shape[0]
            for j, chip in enumerate(chips):
                slot = out_refs[t].at[2 * chip[0] + chip[1]]
                rcopy(6 * t + 3 + j, half(slot, 1 - c, rows), half(slot, 1 - c, rows), sibling).wait_recv()
        for cp in sends:
            cp.wait_send()

    return pl.pallas_call(
        body,
        out_shape=tuple(jax.ShapeDtypeStruct((N_CHIPS,) + s.shape, s.dtype) for s in shards),
        in_specs=[ANY] * n,
        out_specs=tuple([ANY] * n),
        scratch_shapes=[pltpu.SemaphoreType.DMA((6 * n,)), pltpu.SemaphoreType.DMA((6 * n,))],
        name="gather_weights",
    )(*shards)


def _pair_exchange(gs, *, name):
    n = len(gs)

    def body(*refs):
        g_refs, r_refs = refs[:n], refs[n:2 * n]
        send_sems, recv_sems = refs[2 * n:]
        x, y, c, _ = _place()
        cps = []
        for t in range(n):
            hr = gs[t].shape[1] // 2
            cp = pltpu.make_async_remote_copy(src_ref=g_refs[t].at[:, pl.ds((1 - c) * hr, hr), :], dst_ref=r_refs[t],
                                              send_sem=send_sems.at[t], recv_sem=recv_sems.at[t],
                                              device_id=(x, y, 1 - c), device_id_type=MESH)
            cp.start()
            cps.append(cp)
        for cp in cps:
            cp.wait()

    return pl.pallas_call(
        body,
        out_shape=tuple(jax.ShapeDtypeStruct((g.shape[0], g.shape[1] // 2, g.shape[2]), g.dtype) for g in gs),
        in_specs=[ANY] * n,
        out_specs=tuple([ANY] * n),
        scratch_shapes=[pltpu.SemaphoreType.DMA((n,)), pltpu.SemaphoreType.DMA((n,))],
        name=name,
    )(*gs)


_HBM = pl.BlockSpec(memory_space=pltpu.HBM)
_SEM = pl.BlockSpec(memory_space=pltpu.SEMAPHORE)
_DATAFLOW = pltpu.SideEffectType.DATAFLOW_SIDE_EFFECTING


def _chip_copies(p_refs, land_refs, send_sems, recv_sems):
    x, y, c, chips = _place()
    me_chip = 2 * x + y
    return [pltpu.make_async_remote_copy(src_ref=p_refs[t].at[2 * chip[0] + chip[1]], dst_ref=land_refs[t].at[me_chip],
                                         send_sem=send_sems.at[3 * t + j], recv_sem=recv_sems.at[3 * t + j],
                                         device_id=(*chip, c), device_id_type=MESH)
            for t in range(len(p_refs)) for j, chip in enumerate(chips)]


def _chip_exchange_start(ps, *, tag):
    n = len(ps)

    def body(*refs):
        p_refs, land_refs = refs[:n], refs[n:2 * n]
        send_sems, recv_sems = refs[2 * n:2 * n + 2]
        token = refs[-1]
        for cp in _chip_copies(p_refs, land_refs, send_sems, recv_sems):
            cp.start()
        token[...] = jnp.zeros_like(token)

    hbm = [pltpu.HBM(p.shape, p.dtype) for p in ps]
    args = [pltpu.with_memory_space_constraint(p, pltpu.HBM) for p in ps]
    args += [pltpu.with_memory_space_constraint(lax.empty(p.shape, p.dtype), pltpu.HBM) for p in ps]
    out = pl.pallas_call(
        body,
        name=f"chip_exchange_start_{tag}",
        out_shape=(pltpu.SemaphoreType.DMA((3 * n,)), pltpu.SemaphoreType.DMA((3 * n,)), *hbm, *hbm,
                   jax.ShapeDtypeStruct((8, LANES), F32)),
        in_specs=[_HBM] * (2 * n),
        out_specs=(_SEM, _SEM, *([_HBM] * (2 * n)), pl.BlockSpec(memory_space=pltpu.VMEM)),
        input_output_aliases={i: 2 + i for i in range(2 * n)},
        compiler_params=pltpu.CompilerParams(has_side_effects=_DATAFLOW),
    )(*args)
    return out[0], out[1], out[2:2 + n], out[2 + n:2 + 2 * n], out[-1]


def _chip_exchange_wait(send_sems, recv_sems, p_thru, land_thru, after, *, tag):
    n = len(p_thru)

    def body(*refs):
        p_refs, land_refs = refs[:n], refs[n:2 * n]
        ssem, rsem = refs[2 * n:2 * n + 2]
        for cp in _chip_copies(p_refs, land_refs, ssem, rsem):
            cp.wait_send()
            cp.wait_recv()

    hbm = [pltpu.HBM(p.shape, p.dtype) for p in p_thru]
    out = pl.pallas_call(
        body,
        name=f"chip_exchange_wait_{tag}",
        out_shape=(*hbm, *hbm),
        in_specs=[_HBM] * (2 * n) + [_SEM, _SEM, ANY],
        out_specs=tuple([_HBM] * (2 * n)),
        input_output_aliases={i: i for i in range(2 * n)},
        compiler_params=pltpu.CompilerParams(has_side_effects=_DATAFLOW),
    )(*p_thru, *land_thru, send_sems, recv_sems, after)
    return out[:n], out[n:]


def _shard_copies(s_refs, land_refs, send_sems, recv_sems):
    x, y, c, chips = _place()
    me_chip = 2 * x + y
    return [pltpu.make_async_remote_copy(src_ref=s_refs[t], dst_ref=land_refs[t].at[me_chip],
                                         send_sem=send_sems.at[3 * t + j], recv_sem=recv_sems.at[3 * t + j],
                                         device_id=(*chip, c), device_id_type=MESH)
            for t in range(len(s_refs)) for j, chip in enumerate(chips)]


def _gather_late_start(shards):
    n = len(shards)

    def body(*refs):
        s_refs, land_refs = refs[:n], refs[n:2 * n]
        send_sems, recv_sems = refs[2 * n:2 * n + 2]
        token = refs[-1]
        for cp in _shard_copies(s_refs, land_refs, send_sems, recv_sems):
            cp.start()
        token[...] = jnp.zeros_like(token)

    lands = [(N_CHIPS,) + s.shape for s in shards]
    args = [pltpu.with_memory_space_constraint(s, pltpu.HBM) for s in shards]
    args += [pltpu.with_memory_space_constraint(lax.empty(shp, s.dtype), pltpu.HBM) for shp, s in zip(lands, shards)]
    out = pl.pallas_call(
        body,
        name="gather_late_start",
        out_shape=(pltpu.SemaphoreType.DMA((3 * n,)), pltpu.SemaphoreType.DMA((3 * n,)),
                   *[pltpu.HBM(s.shape, s.dtype) for s in shards], *[pltpu.HBM(shp, s.dtype) for shp, s in zip(lands, shards)],
                   jax.ShapeDtypeStruct((8, LANES), F32)),
        in_specs=[_HBM] * (2 * n),
        out_specs=(_SEM, _SEM, *([_HBM] * (2 * n)), pl.BlockSpec(memory_space=pltpu.VMEM)),
        input_output_aliases={i: 2 + i for i in range(2 * n)},
        compiler_params=pltpu.CompilerParams(has_side_effects=_DATAFLOW),
    )(*args)
    return out[0], out[1], out[2:2 + n], out[2 + n:2 + 2 * n], out[-1]


def _gather_late_wait(send_sems, recv_sems, s_thru, land_thru, after):
    n = len(s_thru)

    def body(*refs):
        s_refs, land_refs = refs[:n], refs[n:2 * n]
        ssem, rsem = refs[2 * n:2 * n + 2]
        for cp in _shard_copies(s_refs, land_refs, ssem, rsem):
            cp.wait_send()
            cp.wait_recv()

    out = pl.pallas_call(
        body,
        name="gather_late_wait",
        out_shape=(*[pltpu.HBM(s.shape, s.dtype) for s in s_thru], *[pltpu.HBM(l.shape, l.dtype) for l in land_thru]),
        in_specs=[_HBM] * (2 * n) + [_SEM, _SEM, ANY],
        out_specs=tuple([_HBM] * (2 * n)),
        input_output_aliases={i: i for i in range(2 * n)},
        compiler_params=pltpu.CompilerParams(has_side_effects=_DATAFLOW),
    )(*s_thru, *land_thru, send_sems, recv_sems, after)
    return out[:n], out[n:]


def _pair_swap(rs):
    n = len(rs)

    def body(*refs):
        r_refs, o_refs = refs[:n], refs[n:2 * n]
        send_sems, recv_sems = refs[2 * n:]
        x, y, c, _ = _place()
        cps = []
        for t in range(n):
            cp = pltpu.make_async_remote_copy(src_ref=r_refs[t], dst_ref=o_refs[t], send_sem=send_sems.at[t],
                                              recv_sem=recv_sems.at[t], device_id=(x, y, 1 - c), device_id_type=MESH)
            cp.start()
            cps.append(cp)
        for cp in cps:
            cp.wait()

    return pl.pallas_call(
        body,
        out_shape=tuple(jax.ShapeDtypeStruct(r.shape, r.dtype) for r in rs),
        in_specs=[ANY] * n,
        out_specs=tuple([ANY] * n),
        scratch_shapes=[pltpu.SemaphoreType.DMA((n,)), pltpu.SemaphoreType.DMA((n,))],
        name="pair_swap",
    )(*rs)


N_DEV = 8
LOSS_ROW = 4


def _small_allreduce(small):
    def body(s_ref, o_ref, all_ref, send_sems, recv_sems):
        x, y, c, _ = _place()
        me = 4 * x + 2 * y + c
        all_ref[me] = s_ref[...]
        cps = []
        for k in range(1, N_DEV):
            peer = tuple(1 - p if (k >> s) & 1 else p for p, s in ((x, 2), (y, 1), (c, 0)))
            cp = pltpu.make_async_remote_copy(src_ref=s_ref, dst_ref=all_ref.at[me], send_sem=send_sems.at[k - 1],
                                              recv_sem=recv_sems.at[k - 1], device_id=peer, device_id_type=MESH)
            cp.start()
            cps.append(cp)
        for cp in cps:
            cp.wait()
        tot = all_ref[0]
        for d in range(1, N_DEV):
            tot = tot + all_ref[d]
        o_ref[...] = tot
        o_ref[LOSS_ROW:LOSS_ROW + 1, :] = jnp.broadcast_to(jnp.sum(tot[LOSS_ROW:LOSS_ROW + 1, :], axis=1, keepdims=True),
                                                          (1, tot.shape[1]))

    vm = pl.BlockSpec(memory_space=pltpu.VMEM)
    return pl.pallas_call(
        body,
        out_shape=jax.ShapeDtypeStruct(small.shape, small.dtype),
        in_specs=[vm],
        out_specs=vm,
        scratch_shapes=[pltpu.VMEM((N_DEV,) + small.shape, small.dtype), pltpu.SemaphoreType.DMA((N_DEV - 1,)),
                        pltpu.SemaphoreType.DMA((N_DEV - 1,))],
        name="small_allreduce",
    )(small)


def _sum_pair(g, recv, cidx, *, tr, name):
    n, hr, cols = recv.shape
    nr = hr // tr

    def body(c_ref, g_ref, r_ref, o_ref):
        o_ref[...] = (g_ref[...].astype(F32) + r_ref[...].astype(F32)).astype(o_ref.dtype)

    grid_spec = pltpu.PrefetchScalarGridSpec(
        num_scalar_prefetch=1,
        grid=(n, nr),
        in_specs=[pl.BlockSpec((None, tr, cols), lambda k, i, c_ref: (k, c_ref[0] * nr + i, 0)),
                  pl.BlockSpec((None, tr, cols), lambda k, i, c_ref: (k, i, 0))],
        out_specs=pl.BlockSpec((None, tr, cols), lambda k, i, c_ref: (k, i, 0)),
    )
    return pl.pallas_call(body, out_shape=jax.ShapeDtypeStruct(recv.shape, BF16), grid_spec=grid_spec,
                          compiler_params=_cparams(), name=name)(cidx, g, recv)


def _sum_chips(p, *, tr, name):
    _, rows, cols = p.shape

    def body(p_ref, o_ref):
        tot = p_ref[0].astype(F32)
        for k in range(1, N_CHIPS):
            tot = tot + p_ref[k].astype(F32)
        o_ref[...] = tot

    return pl.pallas_call(
        body,
        out_shape=jax.ShapeDtypeStruct((rows, cols), F32),
        grid=(rows // tr,),
        in_specs=[pl.BlockSpec((N_CHIPS, tr, cols), lambda i: (0, i, 0))],
        out_specs=pl.BlockSpec((tr, cols), lambda i: (i, 0)),
        compiler_params=_cparams(),
        name=name,
    )(p)


def _adamw(w, g, m, v, *, tr, name):
    rows, cols = w.shape
    bc1 = 1.0 / (1.0 - ADAM_B1 ** ADAM_STEP)
    bc2 = 1.0 / (1.0 - ADAM_B2 ** ADAM_STEP)

    def body(w_ref, g_ref, m_ref, v_ref, d_ref, nm_ref, nv_ref):
        gv = g_ref[...]
        nm = ADAM_B1 * m_ref[...] + (1.0 - ADAM_B1) * gv
        nv = ADAM_B2 * v_ref[...] + (1.0 - ADAM_B2) * (gv * gv)
        d_ref[...] = -ADAM_LR * ((nm * bc1) / (jnp.sqrt(nv * bc2) + ADAM_EPS) + ADAM_WD * w_ref[...])
        nm_ref[...] = nm
        nv_ref[...] = nv

    spec = pl.BlockSpec((tr, cols), lambda i: (i, 0))
    sd = jax.ShapeDtypeStruct((rows, cols), F32)
    return pl.pallas_call(body, out_shape=(sd, sd, sd), grid=(rows // tr,), in_specs=[spec] * 4, out_specs=(spec,) * 3,
                          compiler_params=_cparams(), name=name)(w, g, m, v)


def _adamw_halves(w, own, sib, cidx, m, v, *, tr, name):
    rows, cols = w.shape
    hr = own.shape[0]
    nr = hr // tr
    assert rows == 2 * hr and hr % tr == 0
    bc1 = 1.0 / (1.0 - ADAM_B1 ** ADAM_STEP)
    bc2 = 1.0 / (1.0 - ADAM_B2 ** ADAM_STEP)

    def body(c_ref, w_ref, o_ref, s_ref, m_ref, v_ref, g_ref, d_ref, nm_ref, nv_ref):
        mine = (pl.program_id(0) // nr) == c_ref[0]
        gv = jnp.where(mine, o_ref[...], s_ref[...])
        nm = ADAM_B1 * m_ref[...] + (1.0 - ADAM_B1) * gv
        nv = ADAM_B2 * v_ref[...] + (1.0 - ADAM_B2) * (gv * gv)
        g_ref[...] = gv
        d_ref[...] = -ADAM_LR * ((nm * bc1) / (jnp.sqrt(nv * bc2) + ADAM_EPS) + ADAM_WD * w_ref[...])
        nm_ref[...] = nm
        nv_ref[...] = nv

    full = pl.BlockSpec((tr, cols), lambda i, c_ref: (i, 0))
    half = pl.BlockSpec((tr, cols), lambda i, c_ref: (i % nr, 0))
    sd = jax.ShapeDtypeStruct((rows, cols), F32)
    grid_spec = pltpu.PrefetchScalarGridSpec(num_scalar_prefetch=1, grid=(rows // tr,), in_specs=[full, half, half, full, full],
                                             out_specs=(full,) * 4)
    return pl.pallas_call(body, out_shape=(sd,) * 4, grid_spec=grid_spec, compiler_params=_cparams(), name=name)(
        cidx, w, own, sib, m, v)


def _pack_small(norm, mem_norm, final_norm, b_forget):
    rows = [norm.reshape(1, D_MODEL), mem_norm.reshape(1, D_MODEL), final_norm.reshape(1, D_MODEL),
            jnp.pad(b_forget.reshape(1, FOX_HEADS), ((0, 0), (0, D_MODEL - FOX_HEADS))), jnp.zeros((4, D_MODEL), F32)]
    return jnp.concatenate(rows, axis=0)


def _unpack_small(a):
    return a[0:1], a[3:4, :FOX_HEADS], a[1:2], a[2]


def kernel(x, mem, norm_g, w_in, b_forget, mem_norm_g, w_mem_kv, w_out, final_norm_g, loss_target, m_norm_g, m_w_in, m_b_forget, m_mem_norm_g, m_w_mem_kv, m_w_out, m_final_norm_g, v_norm_g, v_w_in, v_b_forget, v_mem_norm_g, v_w_mem_kv, v_w_out, v_final_norm_g):
    core = lax.axis_index("c").astype(jnp.int32)
    me_chip = (2 * lax.axis_index("x") + lax.axis_index("y")).astype(jnp.int32)
    cidx = core.reshape(1)

    def own_slot(arr, own):
        return lax.dynamic_update_slice(arr, own[None].astype(arr.dtype), (me_chip,) + (0,) * own.ndim)

    win_b, late = w_in[0].astype(BF16), [w_mem_kv[0].astype(BF16), w_out[0].astype(BF16)]
    g_in, = _gather_weights([win_b])
    g_in, late = lax.optimization_barrier((own_slot(g_in, win_b), late))
    w_r = _rearrange_w_in([g_in[k] for k in range(N_CHIPS)])
    *late_flight, early_token = _gather_late_start(late)

    def late_weights(after):
        shards, landed = _gather_late_wait(*late_flight, after)
        g_kv, g_out = (own_slot(g, s) for g, s in zip(landed, shards))
        return g_kv.reshape(D_MODEL, 2 * MEM_W), g_out.reshape(MIX_W, D_MODEL)

    trs = (128, 128, 256)
    names = ("w_in", "w_mem_kv", "w_out")
    flights = {}

    def exchange(slabs, nms, ts, tag):
        recv = _pair_exchange(slabs, name=f"pair_exchange_{tag}")
        pair = [_sum_pair(g, r, cidx, tr=tr, name=f"sum_pair_{nm}") for g, r, tr, nm in zip(slabs, recv, ts, nms)]
        if tag == "w_in":
            pair[0] = _w_in_grad_slabs(pair[0][0])
        *flights[tag], token = _chip_exchange_start(pair, tag=tag)
        return token

    def start_reduce_small(g_wkv, g_wo):
        slabs = [g_wkv.reshape(N_CHIPS, D_MODEL // N_CHIPS, 2 * MEM_W), g_wo.reshape(N_CHIPS, MIX_W // N_CHIPS, D_MODEL)]
        return exchange(slabs, names[1:], trs[1:], "small")

    def start_reduce(g_wr):
        return exchange([g_wr[None]], names[:1], trs[:1], "w_in")

    gx, g_wr, g_wkv, g_wo, small = _local_grads(x, mem, norm_g, w_r, b_forget, mem_norm_g, None, None, final_norm_g, loss_target,
                                                start_reduce=start_reduce, start_reduce_small=start_reduce_small,
                                                early_token=early_token, late_weights=late_weights)

    pair, landed = [], []
    for tag in ("w_in", "small"):
        p, l = _chip_exchange_wait(*flights[tag], small, tag=tag)
        pair += list(p)
        landed += list(l)
    got = [lax.dynamic_update_slice(g, lax.dynamic_slice(p, (me_chip, 0, 0), (1,) + p.shape[1:]), (me_chip, 0, 0))
           for g, p in zip(landed, pair)]
    red = [_sum_chips(p, tr=tr, name=f"sum_chips_{nm}") for p, tr, nm in zip(got, trs, names)]
    sib = _pair_swap(red)

    outs = {}
    for nm, r, s, w, m, v, tr in zip(names, red, sib, (w_in, w_mem_kv, w_out), (m_w_in, m_w_mem_kv, m_w_out),
                                     (v_w_in, v_w_mem_kv, v_w_out), trs):
        outs[nm] = tuple(a[None] for a in _adamw_halves(w[0], r, s, cidx, m[0], v[0], tr=tr, name=f"adamw_{nm}"))

    gsum = _small_allreduce(small)
    sd, sm, sv = _adamw(_pack_small(norm_g, mem_norm_g, final_norm_g, b_forget), gsum,
                        _pack_small(m_norm_g, m_mem_norm_g, m_final_norm_g, m_b_forget),
                        _pack_small(v_norm_g, v_mem_norm_g, v_final_norm_g, v_b_forget), tr=8, name="adamw_small")
    loss = gsum[LOSS_ROW, 0]

    def group(i, small_arr):
        ng, bf, mg, fg = _unpack_small(small_arr)
        return (ng, outs["w_in"][i], bf, mg, outs["w_mem_kv"][i], outs["w_out"][i], fg)

    return (loss, gx, *group(0, gsum), *group(1, sd), *group(2, sm), *group(3, sv))
```

```python
import functools
import math

import jax
import jax.numpy as jnp
from jax import lax
from jax.experimental import pallas as pl
from jax.experimental.pallas import tpu as pltpu

F32 = jnp.float32
BF16 = jnp.bfloat16

D_MODEL = 1024
SEQ = 2048
HEAD_DIM = 64
FOX_HEADS = 12
DIL_HEADS = 12
MEM_HEADS = 4
MEM_HEAD_DIM = 128
MEM_LEN = 256
FOX_W = FOX_HEADS * HEAD_DIM
DIL_W = DIL_HEADS * HEAD_DIM
MEM_W = MEM_HEADS * MEM_HEAD_DIM
MIX_W = FOX_W + DIL_W + MEM_W
DILATIONS = ((128, 1), (512, 4), (2048, 16))
ROPE_THETA = 500000.0
ROPE_DIM = HEAD_DIM // 4
RMS_EPS = 1e-6
NEG_INF = -1e30
IN_SIZES = [FOX_W] * 4 + [FOX_HEADS] + [DIL_W] * 4 + [MEM_W] * 2
IN_W = sum(IN_SIZES)

ADAM_LR = 0.001
ADAM_B1 = 0.9
ADAM_B2 = 0.999
ADAM_EPS = 1e-08
ADAM_WD = 0.01
ADAM_STEP = 10

LANES = 128
N_CHIPS = 4
PW = 7168
PWF = PW + 4 * LANES
C_FQ, C_FK, C_FV, C_FG = 0, 768, 1536, 2304
C_DQ, C_DK, C_DV, C_DG = 3072, 3840, 4608, 5376
C_MQ, C_MG = 6144, 6656
VMEM_LIMIT = 48 * 1024 * 1024


def _cparams(**kw):
    return pltpu.CompilerParams(vmem_limit_bytes=VMEM_LIMIT, **kw)


MM_CHUNK = 256


def _matmul(a, b, *, out_dtype, tm, tn, tk, name, mode="nn"):
    if mode == "tn":
        (kdim, m), n = a.shape, b.shape[1]
        a_spec = pl.BlockSpec((tk, tm), lambda i, j, k: (k, i))
        b_spec = pl.BlockSpec((tk, tn), lambda i, j, k: (k, j))
        dims = _T0
    elif mode == "nt":
        (m, kdim), n = a.shape, b.shape[0]
        a_spec = pl.BlockSpec((tm, tk), lambda i, j, k: (i, k))
        b_spec = pl.BlockSpec((tn, tk), lambda i, j, k: (j, k))
        dims = _NT
    else:
        (m, kdim), n = a.shape, b.shape[1]
        a_spec = pl.BlockSpec((tm, tk), lambda i, j, k: (i, k))
        b_spec = pl.BlockSpec((tk, tn), lambda i, j, k: (k, j))
        dims = (((1,), (0,)), ((), ()))
    nk = kdim // tk
    assert m % tm == 0 and n % tn == 0 and kdim % tk == 0

    def body(a_ref, b_ref, o_ref, *scratch):
        if nk == 1:
            bv = b_ref[...]
            for c0 in range(0, tm, min(tm, MM_CHUNK)):
                rows = pl.ds(c0, min(tm, MM_CHUNK))
                av = a_ref[:, rows] if mode == "tn" else a_ref[rows, :]
                o_ref[rows, :] = lax.dot_general(av, bv, dims, preferred_element_type=F32).astype(o_ref.dtype)
            return
        prod = lax.dot_general(a_ref[...], b_ref[...], dims, preferred_element_type=F32)
        acc_ref, = scratch
        k = pl.program_id(2)

        @pl.when(k == 0)
        def _():
            acc_ref[...] = prod

        @pl.when(k > 0)
        def _():
            acc_ref[...] += prod

        @pl.when(k == nk - 1)
        def _():
            o_ref[...] = acc_ref[...].astype(o_ref.dtype)

    return pl.pallas_call(
        body,
        out_shape=jax.ShapeDtypeStruct((m, n), out_dtype),
        grid=(m // tm, n // tn, nk),
        in_specs=[a_spec, b_spec],
        out_specs=pl.BlockSpec((tm, tn), lambda i, j, k: (i, j)),
        scratch_shapes=[pltpu.VMEM((tm, tn), F32)] if nk > 1 else [],
        compiler_params=_cparams(dimension_semantics=("parallel", "parallel", "arbitrary")),
        name=name,
    )(a, b)


def _rms_fwd(x, g, *, tm, name):
    t, d = x.shape

    def body(x_ref, g_ref, h_ref):
        xv = x_ref[...]
        r = lax.rsqrt(jnp.mean(xv * xv, axis=-1, keepdims=True) + RMS_EPS)
        h_ref[...] = (xv * r * g_ref[...]).astype(h_ref.dtype)

    return pl.pallas_call(
        body,
        out_shape=jax.ShapeDtypeStruct((t, d), BF16),
        grid=(t // tm,),
        in_specs=[pl.BlockSpec((tm, d), lambda i: (i, 0)), pl.BlockSpec((1, d), lambda i: (0, 0))],
        out_specs=pl.BlockSpec((tm, d), lambda i: (i, 0)),
        compiler_params=_cparams(),
        name=name,
    )(x, g)


def _rope_tables():
    half = ROPE_DIM // 2
    pos = jnp.arange(SEQ, dtype=F32)
    inv_freq = 1.0 / (ROPE_THETA ** (jnp.arange(0, ROPE_DIM, 2, dtype=F32) / ROPE_DIM))
    ang = pos[:, None] * inv_freq[None, :]
    cos, sin = jnp.cos(ang), jnp.sin(ang)
    one = jnp.ones((SEQ, HEAD_DIM - ROPE_DIM), F32)
    zero = jnp.zeros((SEQ, HEAD_DIM - ROPE_DIM), F32)
    zh = jnp.zeros((SEQ, half), F32)
    c = jnp.concatenate([cos, cos, one], axis=1)
    s1 = jnp.concatenate([zh, sin, zero], axis=1)
    s2 = jnp.concatenate([-sin, zh, zero], axis=1)
    rep = LANES // HEAD_DIM
    return jnp.tile(c, (1, rep)), jnp.tile(s1, (1, rep)), jnp.tile(s2, (1, rep))


def _rope_apply(t, c, s1, s2, transpose=False):
    n = t.shape[-1]
    rep = n // LANES
    c, s1, s2 = (jnp.tile(u, (1, rep)) for u in (c, s1, s2))
    half = ROPE_DIM // 2
    if not transpose:
        return t * c + pltpu.roll(t, half, 1) * s1 + pltpu.roll(t, n - half, 1) * s2
    return t * c + pltpu.roll(t * s1, n - half, 1) + pltpu.roll(t * s2, half, 1)


PROJ_CHUNK = 256


def _proj(x, g, w, tabs, *, n, tm, tn, name):
    t, d = x.shape
    assert C_DQ % tn == 0 and (C_DV - C_DQ) % tn == 0 and (C_DG - C_DQ) % tn == 0
    rope_lo, rope_hi, dil_hi = C_DQ // tn, C_DV // tn, C_DG // tn
    flog_blk, flog_at = PW // tn, PW % tn
    assert flog_at % LANES == 0 and flog_at + LANES <= tn
    s_blocks = SEQ // tm

    def body(x_ref, g_ref, w_ref, c_ref, s1_ref, s2_ref, h_ref, o_ref, f_ref, fl_ref, h_scr):
        j = pl.program_id(1)

        @pl.when(j == 0)
        def _():
            xv = x_ref[...]
            r = lax.rsqrt(jnp.mean(xv * xv, axis=-1, keepdims=True) + RMS_EPS)
            hv = (xv * r * g_ref[...]).astype(BF16)
            h_scr[...] = hv
            h_ref[...] = hv

        def tile(kind):
            wv = w_ref[...]
            for c0 in range(0, tm, PROJ_CHUNK):
                rows = pl.ds(c0, PROJ_CHUNK)
                acc = jnp.dot(h_scr[rows, :], wv, preferred_element_type=F32)
                if kind == "rope":
                    acc = _rope_apply(acc, c_ref[rows, :], s1_ref[rows, :], s2_ref[rows, :])
                o_ref[rows, :] = acc.astype(o_ref.dtype)
                if kind in ("rope", "dv"):
                    f_ref[rows, :] = acc
                if kind == "flog":
                    fl_ref[rows, :] = acc[:, flog_at:flog_at + LANES]

        is_rope = jnp.logical_and(j >= rope_lo, j < rope_hi)
        is_dv = jnp.logical_and(j >= rope_hi, j < dil_hi)
        is_flog = j == flog_blk
        pl.when(is_rope)(functools.partial(tile, "rope"))
        pl.when(is_dv)(functools.partial(tile, "dv"))
        pl.when(is_flog)(functools.partial(tile, "flog"))
        pl.when(jnp.logical_not(jnp.logical_or(jnp.logical_or(is_rope, is_dv), is_flog)))(functools.partial(tile, "plain"))

    tab_spec = pl.BlockSpec((tm, LANES), lambda i, j: (i % s_blocks, 0))
    f_spec = pl.BlockSpec((tm, tn), lambda i, j: (i, jnp.clip(j - rope_lo, 0, dil_hi - rope_lo - 1)))
    row = pl.BlockSpec((tm, d), lambda i, j: (i, 0))
    return pl.pallas_call(
        body,
        out_shape=(jax.ShapeDtypeStruct((t, d), BF16), jax.ShapeDtypeStruct((t, n), BF16),
                   jax.ShapeDtypeStruct((t, 3 * DIL_W), F32), jax.ShapeDtypeStruct((t, LANES), F32)),
        grid=(t // tm, n // tn),
        in_specs=[row, pl.BlockSpec((1, d), lambda i, j: (0, 0)), pl.BlockSpec((d, tn), lambda i, j: (0, j)),
                  tab_spec, tab_spec, tab_spec],
        out_specs=(row, pl.BlockSpec((tm, tn), lambda i, j: (i, j)), f_spec, pl.BlockSpec((tm, LANES), lambda i, j: (i, 0))),
        scratch_shapes=[pltpu.VMEM((tm, d), BF16)],
        compiler_params=_cparams(dimension_semantics=("parallel", "arbitrary")),
        name=name,
    )(x, g, w, *tabs)


def _split3(x):
    hi = x.astype(BF16)
    r1 = x - hi.astype(F32)
    mid = r1.astype(BF16)
    lo = (r1 - mid.astype(F32)).astype(BF16)
    return hi, mid, lo


def _dot3(sel, x, sel_is_lhs):
    out = None
    for piece in _split3(x):
        t = jnp.dot(sel, piece, preferred_element_type=F32) if sel_is_lhs else jnp.dot(piece, sel, preferred_element_type=F32)
        out = t if out is None else out + t
    return out


def _flog_fwd(flog, bpad, *, nb, ts, name):
    ns = SEQ // ts

    def body(f_ref, b_ref, c_ref, carry_ref):
        s = pl.program_id(1)

        @pl.when(s == 0)
        def _():
            carry_ref[...] = jnp.zeros_like(carry_ref)

        z = f_ref[...] + b_ref[...]
        logf = jnp.minimum(z, 0.0) - jnp.log(1.0 + jnp.exp(-jnp.abs(z)))
        r = lax.broadcasted_iota(jnp.int32, (ts, ts), 0)
        c = lax.broadcasted_iota(jnp.int32, (ts, ts), 1)
        tri = jnp.where(r >= c, 1.0, 0.0).astype(BF16)
        cs = _dot3(tri, logf, True) + carry_ref[0:1, :]
        carry_ref[...] = jnp.broadcast_to(cs[ts - 1:ts, :], carry_ref.shape)
        c_ref[...] = cs

    return pl.pallas_call(
        body,
        out_shape=jax.ShapeDtypeStruct((nb * SEQ, LANES), F32),
        grid=(nb, ns),
        in_specs=[pl.BlockSpec((ts, LANES), lambda b, s: (b * ns + s, 0)), pl.BlockSpec((1, LANES), lambda b, s: (0, 0))],
        out_specs=pl.BlockSpec((ts, LANES), lambda b, s: (b * ns + s, 0)),
        scratch_shapes=[pltpu.VMEM((8, LANES), F32)],
        compiler_params=_cparams(dimension_semantics=("parallel", "arbitrary")),
        name=name,
    )(flog, bpad)


def _flog_bwd(dcol, flog, bpad, *, nb, ts, name):
    ns = SEQ // ts

    def body(d_ref, f_ref, b_ref, o_ref, gb_ref, carry_ref):
        bi = pl.program_id(0)
        s = pl.program_id(1)

        @pl.when(s == 0)
        def _():
            carry_ref[...] = jnp.zeros_like(carry_ref)

        @pl.when(jnp.logical_and(bi == 0, s == 0))
        def _():
            gb_ref[...] = jnp.zeros_like(gb_ref)

        r = lax.broadcasted_iota(jnp.int32, (ts, ts), 0)
        c = lax.broadcasted_iota(jnp.int32, (ts, ts), 1)
        tri = jnp.where(r <= c, 1.0, 0.0).astype(BF16)
        rc = _dot3(tri, d_ref[...], True) + carry_ref[0:1, :]
        carry_ref[...] = jnp.broadcast_to(rc[0:1, :], carry_ref.shape)
        z = f_ref[...] + b_ref[...]
        dz = rc / (1.0 + jnp.exp(z))
        o_ref[...] = dz.astype(o_ref.dtype)
        gb_ref[...] += jnp.broadcast_to(jnp.sum(dz, axis=0, keepdims=True), gb_ref.shape)

    rev = lambda b, s: (b * ns + (ns - 1 - s), 0)
    return pl.pallas_call(
        body,
        out_shape=(jax.ShapeDtypeStruct((nb * SEQ, LANES), BF16), jax.ShapeDtypeStruct((8, LANES), F32)),
        grid=(nb, ns),
        in_specs=[pl.BlockSpec((ts, LANES), rev), pl.BlockSpec((ts, LANES), rev), pl.BlockSpec((1, LANES), lambda b, s: (0, 0))],
        out_specs=(pl.BlockSpec((ts, LANES), rev), pl.BlockSpec((8, LANES), lambda b, s: (0, 0))),
        scratch_shapes=[pltpu.VMEM((8, LANES), F32)],
        compiler_params=_cparams(dimension_semantics=("arbitrary", "arbitrary")),
        name=name,
    )(dcol, flog, bpad)


MEM_TQ = 256
MEM_SET = 4
MEM_SCALE = 1.0 / math.sqrt(MEM_HEAD_DIM)
assert MEM_HEAD_DIM == LANES and SEQ % (MEM_TQ * MEM_SET) == 0


def _head_masks(nh):
    lane = lax.broadcasted_iota(jnp.int32, (1, LANES), 1)
    return [None] if nh == 1 else [lane < HEAD_DIM, lane >= HEAD_DIM]


def _mem_specs(qoff):
    qspec = pl.BlockSpec((None, SEQ, LANES), lambda b, j: (b, 0, qoff + j))
    kspec = pl.BlockSpec((None, MEM_LEN, LANES), lambda b, j: (b, 0, j))
    vspec = pl.BlockSpec((None, MEM_LEN, LANES), lambda b, j: (b, 0, MEM_HEADS + j))
    ospec = pl.BlockSpec((None, SEQ, LANES), lambda b, j: (b, 0, j))
    return qspec, kspec, vspec, ospec


def _mem_rows(g):
    return [pl.ds(pl.multiple_of((MEM_SET * g + a) * MEM_TQ, MEM_TQ), MEM_TQ) for a in range(MEM_SET)]


def _mem_fwd(p3, mkv3, *, qoff, name):
    nb = p3.shape[0]

    def body(q_ref, k_ref, v_ref, o_ref, lse_ref):
        kb, vb = k_ref[...], v_ref[...]

        def qset(g, c):
            rows = _mem_rows(g)
            ss = [lax.dot_general(q_ref[r, :] * MEM_SCALE, kb, _NT, preferred_element_type=F32) for r in rows]
            for r, s in zip(rows, ss):
                m = jnp.max(s, axis=1, keepdims=True)
                p = jnp.exp(s - m)
                l = jnp.sum(p, axis=1, keepdims=True)
                o_ref[r, :] = jnp.dot(p.astype(BF16), vb, preferred_element_type=F32) / l
                lse_ref[r, :] = jnp.broadcast_to(m + jnp.log(l), (MEM_TQ, LANES))
            return c

        lax.fori_loop(0, SEQ // MEM_TQ // MEM_SET, qset, 0)

    qspec, kspec, vspec, ospec = _mem_specs(qoff)
    osd = jax.ShapeDtypeStruct((nb, SEQ, MEM_W), F32)
    return pl.pallas_call(body, out_shape=(osd, osd), grid=(nb, MEM_HEADS), in_specs=[qspec, kspec, vspec],
                          out_specs=(ospec, ospec), compiler_params=_cparams(dimension_semantics=("parallel", "parallel")),
                          name=name)(p3, mkv3, mkv3)


def _mem_bwd(p3, mkv3, do, o, lse, *, qoff, do_off, name):
    nb = p3.shape[0]

    def body(q_ref, k_ref, v_ref, do_ref, o_ref, lse_ref, dq_ref, dk_ref, dv_ref):
        kb, vb = k_ref[...], v_ref[...]
        ks = kb * MEM_SCALE

        def qset(g, carry):
            dk, dv = carry
            work = []
            for r in _mem_rows(g):
                qs = q_ref[r, :] * MEM_SCALE
                dob = do_ref[r, :].astype(BF16)
                s = lax.dot_general(qs, kb, _NT, preferred_element_type=F32)
                dp = lax.dot_general(dob, vb, _NT, preferred_element_type=F32)
                work.append((r, qs, dob, s, dp))
            for r, qs, dob, s, dp in work:
                delta = jnp.sum(dob.astype(F32) * o_ref[r, :], axis=1, keepdims=True)
                p = jnp.exp(s - lse_ref[r, :][:, 0:1])
                ds = (p * (dp - delta)).astype(BF16)
                dq_ref[r, :] = jnp.dot(ds, ks, preferred_element_type=F32).astype(dq_ref.dtype)
                dk = dk + lax.dot_general(ds, qs, _T0, preferred_element_type=F32)
                dv = dv + lax.dot_general(p.astype(BF16), dob, _T0, preferred_element_type=F32)
            return dk, dv

        z = jnp.zeros((MEM_LEN, LANES), F32)
        dk, dv = lax.fori_loop(0, SEQ // MEM_TQ // MEM_SET, qset, (z, z))
        dk_ref[...] = dk
        dv_ref[...] = dv

    qspec, kspec, vspec, ospec = _mem_specs(qoff)
    dospec = pl.BlockSpec((None, SEQ, LANES), lambda b, j: (b, 0, do_off + j))
    kvo = pl.BlockSpec((None, MEM_LEN, LANES), lambda b, j: (b, 0, j))
    kvsd = jax.ShapeDtypeStruct((nb, MEM_LEN, MEM_W), F32)
    return pl.pallas_call(
        body, out_shape=(jax.ShapeDtypeStruct((nb, SEQ, MEM_W), BF16), kvsd, kvsd), grid=(nb, MEM_HEADS),
        in_specs=[qspec, kspec, vspec, dospec, ospec, ospec], out_specs=(ospec, kvo, kvo),
        compiler_params=_cparams(dimension_semantics=("parallel", "parallel")), name=name)(p3, mkv3, mkv3, do, o, lse)


BLK = 128
NBLK = SEQ // BLK
QK_SCALE = 1.0 / math.sqrt(HEAD_DIM)
DIL_STEPS = tuple(d for _, d in DILATIONS)
assert all(w // d == BLK for w, d in DILATIONS)
_T0 = (((0,), (0,)), ((), ()))
_NT = (((1,), (1,)), ((), ()))


def _stack_heads(a, masks):
    z = jnp.zeros_like(a)
    return jnp.concatenate([jnp.where(masks[0], a, z), jnp.where(masks[1], a, z)], axis=0)


def _tri_bias(lower):
    r = lax.broadcasted_iota(jnp.int32, (BLK, BLK), 0)
    c = lax.broadcasted_iota(jnp.int32, (BLK, BLK), 1)
    return jnp.where((c <= r) if lower else (c >= r), 0.0, NEG_INF).astype(F32)


def _dil_rows(r, i, d):
    start = r + i * (BLK * d)
    return pl.ds(start, BLK) if d == 1 else pl.ds(start, BLK, stride=d)


DIL_SET = 4


def _dil_sets(d, fn):
    nbk = SEQ // d // BLK
    if d == 1:
        n = 2 * DIL_SET
        def gbody(g, c):
            fn([(0, n * g + a, None if a == 0 else True) for a in range(n)])
            return c
        lax.fori_loop(0, nbk // n, gbody, 0)
    elif nbk > 1:
        assert nbk == DIL_SET
        def rbody(r, c):
            fn([(r, i, i > 0) for i in range(nbk)])
            return c
        lax.fori_loop(0, d, rbody, 0)
    else:
        def rbody(rr, c):
            fn([(DIL_SET * rr + a, 0, False) for a in range(DIL_SET)])
            return c
        lax.fori_loop(0, d // DIL_SET, rbody, 0)


def _dil_key_tiles(r, i, d, has_prev, qrows, tri_cur, tri_prev):
    tiles = [(qrows, tri_cur)]
    if has_prev is None:
        tiles.append((_dil_rows(r, jnp.maximum(i - 1, 0), d), tri_prev + jnp.where(i > 0, 0.0, NEG_INF)))
    elif has_prev:
        tiles.append((_dil_rows(r, i - 1, d), tri_prev))
    return tiles


def _dil_fwd(qkv, *, name):
    nb = qkv.shape[0]
    ncol = DIL_W // LANES
    hd = HEAD_DIM

    def body(q_ref, k_ref, v_ref, o_ref, lse_ref, ml_ref, a_ref):
        masks = _head_masks(2)
        quarter = (lax.broadcasted_iota(jnp.int32, (1, LANES), 1) % hd) < hd // 2
        tri_cur, tri_prev = _tri_bias(True), _tri_bias(False)
        for pi, d in enumerate(DIL_STEPS):
            first, last = pi == 0, pi == len(DIL_STEPS) - 1

            def qset(blocks, d=d, first=first, last=last):
                work = []
                for r, i, has_prev in blocks:
                    qrows = _dil_rows(r, i, d)
                    qcat = _stack_heads((q_ref[qrows, :] * QK_SCALE).astype(BF16), masks)
                    ss, krs = [], []
                    for krows, bias in _dil_key_tiles(r, i, d, has_prev, qrows, tri_cur, tri_prev):
                        s = lax.dot_general(qcat, k_ref[krows, :].astype(BF16), _NT, preferred_element_type=F32)
                        ss.append((s[:BLK] + bias, s[BLK:] + bias))
                        krs.append(krows)
                    work.append((qrows, ss, krs))
                for qrows, ss, krs in work:
                    e0 = ss[0][0] if len(ss) == 1 else jnp.maximum(ss[0][0], ss[1][0])
                    e1 = ss[0][1] if len(ss) == 1 else jnp.maximum(ss[0][1], ss[1][1])
                    n0 = jnp.max(e0, axis=1, keepdims=True)
                    n1 = jnp.max(e1, axis=1, keepdims=True)
                    if not first:
                        ml = ml_ref[qrows, :]
                        m0, m1 = ml[:, 0:1], ml[:, hd:hd + 1]
                        lo0, lo1 = ml[:, hd // 2:hd // 2 + 1], ml[:, hd + hd // 2:hd + hd // 2 + 1]
                        n0, n1 = jnp.maximum(n0, m0), jnp.maximum(n1, m1)
                        a0, a1 = jnp.exp(m0 - n0), jnp.exp(m1 - n1)
                    ps = [(jnp.exp(s0 - n0), jnp.exp(s1 - n1)) for s0, s1 in ss]
                    t0 = ps[0][0] if len(ps) == 1 else ps[0][0] + ps[1][0]
                    t1 = ps[0][1] if len(ps) == 1 else ps[0][1] + ps[1][1]
                    l0 = jnp.sum(t0, axis=1, keepdims=True)
                    l1 = jnp.sum(t1, axis=1, keepdims=True)
                    acc = None
                    for (p0, p1), krows in zip(ps, krs):
                        vcat = _stack_heads(v_ref[krows, :].astype(BF16), masks)
                        pv = jnp.dot(jnp.concatenate([p0, p1], axis=1).astype(BF16), vcat, preferred_element_type=F32)
                        acc = pv if acc is None else acc + pv
                    if not first:
                        l0 = l0 + a0 * lo0
                        l1 = l1 + a1 * lo1
                        acc = acc + a_ref[qrows, :] * jnp.where(masks[0], a0, a1)
                    if last:
                        o_ref[qrows, :] = acc / jnp.where(masks[0], l0, l1)
                        lse_ref[qrows, :] = jnp.where(masks[0], n0 + jnp.log(l0), n1 + jnp.log(l1))
                    else:
                        ml_ref[qrows, :] = jnp.where(masks[0], jnp.where(quarter, n0, l0), jnp.where(quarter, n1, l1))
                        a_ref[qrows, :] = acc

            _dil_sets(d, qset)

    spec = lambda off: pl.BlockSpec((None, SEQ, LANES), lambda b, j: (b, 0, off + j))
    ospec = pl.BlockSpec((None, SEQ, LANES), lambda b, j: (b, 0, j))
    osd = jax.ShapeDtypeStruct((nb, SEQ, DIL_W), F32)
    return pl.pallas_call(
        body, out_shape=(osd, osd), grid=(nb, ncol),
        in_specs=[spec(0), spec(ncol), spec(2 * ncol)], out_specs=(ospec, ospec),
        scratch_shapes=[pltpu.VMEM((SEQ, LANES), F32)] * 2,
        compiler_params=_cparams(dimension_semantics=("parallel", "parallel")), name=name,
    )(qkv, qkv, qkv)


def _dil_bwd(qkv, do, o, lse, tabs, *, do_off, name):
    nb = qkv.shape[0]
    ncol = DIL_W // LANES
    hd = HEAD_DIM

    def body(q_ref, k_ref, v_ref, do_ref, o_ref, lse_ref, c_ref, s1_ref, s2_ref, dqo_ref, dko_ref, dvo_ref,
             dq_ref, dk_ref, dv_ref, ld_ref, dof_ref):
        masks = _head_masks(2)
        quarter = (lax.broadcasted_iota(jnp.int32, (1, LANES), 1) % hd) < hd // 2
        tri_cur, tri_prev = _tri_bias(True), _tri_bias(False)
        dq_ref[...] = jnp.zeros_like(dq_ref)
        dk_ref[...] = jnp.zeros_like(dk_ref)
        dv_ref[...] = jnp.zeros_like(dv_ref)

        def delta_body(i, c):
            rows = pl.ds(pl.multiple_of(i * BLK, BLK), BLK)
            dof = do_ref[rows, :].astype(F32)
            dof_ref[rows, :] = dof
            prod = dof * o_ref[rows, :]
            z = jnp.zeros_like(prod)
            delta = jnp.where(masks[0], jnp.sum(jnp.where(masks[0], prod, z), axis=1, keepdims=True),
                              jnp.sum(jnp.where(masks[1], prod, z), axis=1, keepdims=True))
            ld_ref[rows, :] = jnp.where(quarter, lse_ref[rows, :], delta)
            return c

        lax.fori_loop(0, NBLK, delta_body, 0)

        for d in DIL_STEPS:
            def qset(blocks, d=d):
                work = []
                for r, i, has_prev in blocks:
                    qrows = _dil_rows(r, i, d)
                    qcat = _stack_heads((q_ref[qrows, :] * QK_SCALE).astype(BF16), masks)
                    docat = _stack_heads(dof_ref[qrows, :].astype(BF16), masks)
                    tiles = []
                    for krows, bias in _dil_key_tiles(r, i, d, has_prev, qrows, tri_cur, tri_prev):
                        s = lax.dot_general(qcat, k_ref[krows, :].astype(BF16), _NT, preferred_element_type=F32)
                        dp = lax.dot_general(docat, v_ref[krows, :].astype(BF16), _NT, preferred_element_type=F32)
                        tiles.append((krows, s, dp, bias))
                    work.append((qrows, qcat, docat, tiles))
                for qrows, qcat, docat, tiles in work:
                    ld = ld_ref[qrows, :]
                    lse0, lse1 = ld[:, 0:1], ld[:, hd:hd + 1]
                    dl0, dl1 = ld[:, hd // 2:hd // 2 + 1], ld[:, hd + hd // 2:hd + hd // 2 + 1]
                    dq = None
                    for krows, s, dp, bias in tiles:
                        p0 = jnp.exp(s[:BLK] + bias - lse0)
                        p1 = jnp.exp(s[BLK:] + bias - lse1)
                        ds0 = p0 * (dp[:BLK] - dl0)
                        ds1 = p1 * (dp[BLK:] - dl1)
                        ds0b, ds1b = ds0.astype(BF16), ds1.astype(BF16)
                        pcat = jnp.concatenate([p0.astype(BF16), p1.astype(BF16)], axis=0)
                        dscat = jnp.concatenate([ds0b, ds1b], axis=0)
                        dv_ref[krows, :] += lax.dot_general(pcat, docat, _T0, preferred_element_type=F32)
                        dk_ref[krows, :] += lax.dot_general(dscat, qcat, _T0, preferred_element_type=F32)
                        dsrow = jnp.concatenate([ds0b, ds1b], axis=1)
                        kcat = _stack_heads((k_ref[krows, :] * QK_SCALE).astype(BF16), masks)
                        t = jnp.dot(dsrow, kcat, preferred_element_type=F32)
                        dq = t if dq is None else dq + t
                    dq_ref[qrows, :] += dq

            _dil_sets(d, qset)

        def out_body(i, c):
            rows = pl.ds(pl.multiple_of(i * BLK, BLK), BLK)
            tab = (c_ref[rows, :], s1_ref[rows, :], s2_ref[rows, :])
            dqo_ref[rows, :] = _rope_apply(dq_ref[rows, :], *tab, transpose=True).astype(dqo_ref.dtype)
            dko_ref[rows, :] = _rope_apply(dk_ref[rows, :], *tab, transpose=True).astype(dko_ref.dtype)
            dvo_ref[rows, :] = dv_ref[rows, :].astype(dvo_ref.dtype)
            return c

        lax.fori_loop(0, NBLK, out_body, 0)

    spec = lambda off: pl.BlockSpec((None, SEQ, LANES), lambda b, j: (b, 0, off + j))
    ospec = pl.BlockSpec((None, SEQ, LANES), lambda b, j: (b, 0, j))
    tspec = pl.BlockSpec((SEQ, LANES), lambda b, j: (0, 0))
    osd = jax.ShapeDtypeStruct((nb, SEQ, DIL_W), BF16)
    return pl.pallas_call(
        body, out_shape=(osd, osd, osd), grid=(nb, ncol),
        in_specs=[spec(0), spec(ncol), spec(2 * ncol), spec(do_off), ospec, ospec, tspec, tspec, tspec],
        out_specs=(ospec, ospec, ospec),
        scratch_shapes=[pltpu.VMEM((SEQ, LANES), F32)] * 5,
        compiler_params=_cparams(dimension_semantics=("parallel", "parallel")), name=name,
    )(qkv, qkv, qkv, do, o, lse, *tabs)


FOX_GROUP = 4
assert NBLK % FOX_GROUP == 0
_FOX_COLS = tuple(c // LANES for c in (C_FQ, C_FK, C_FV))


def _fox_specs():
    cols = [pl.BlockSpec((None, SEQ, LANES), (lambda b, j, off=off: (b, 0, off + j))) for off in _FOX_COLS]
    ospec = pl.BlockSpec((None, SEQ, LANES), lambda b, j: (b, 0, j))
    crspec = pl.BlockSpec((None, None, NBLK, 8, BLK), lambda b, j: (b, j, 0, 0, 0))
    return cols, ospec, crspec


def _fox_key_rows(t, e):
    return pl.ds(pl.multiple_of((FOX_GROUP * t + e) * BLK, BLK), BLK)


def _fox_fwd(p3, crow, *, name):
    nb = p3.shape[0]
    g = FOX_GROUP

    def body(q_ref, k_ref, v_ref, cr_ref, o_ref, lse_ref):
        masks = _head_masks(2)
        tri = _tri_bias(True)

        def qk(qcat, t):
            return tuple(lax.dot_general(qcat, k_ref[_fox_key_rows(t, e), :], _NT, preferred_element_type=F32) for e in range(g))

        def consume(ss, t, state, nblk, diag):
            m0, m1, l0, l1, acc = state
            us = []
            for e in range(nblk):
                cr = cr_ref[g * t + e]
                u0 = ss[e][:BLK] - cr[0:1, :]
                u1 = ss[e][BLK:] - cr[1:2, :]
                if diag and e == nblk - 1:
                    u0, u1 = u0 + tri, u1 + tri
                us.append((u0, u1))
            x0 = functools.reduce(jnp.maximum, [u[0] for u in us])
            x1 = functools.reduce(jnp.maximum, [u[1] for u in us])
            n0 = jnp.maximum(m0, jnp.max(x0, axis=1, keepdims=True))
            n1 = jnp.maximum(m1, jnp.max(x1, axis=1, keepdims=True))
            a0, a1 = jnp.exp(m0 - n0), jnp.exp(m1 - n1)
            acc = acc * jnp.where(masks[0], a0, a1)
            t0 = t1 = None
            for e in range(nblk):
                p0, p1 = jnp.exp(us[e][0] - n0), jnp.exp(us[e][1] - n1)
                t0 = p0 if t0 is None else t0 + p0
                t1 = p1 if t1 is None else t1 + p1
                pcat = jnp.concatenate([p0, p1], axis=1)
                hi = pcat.astype(BF16)
                lo = (pcat - hi.astype(F32)).astype(BF16)
                vcat = _stack_heads(v_ref[_fox_key_rows(t, e), :], masks)
                acc = acc + jnp.dot(hi, vcat, preferred_element_type=F32) + jnp.dot(lo, vcat, preferred_element_type=F32)
            l0 = a0 * l0 + jnp.sum(t0, axis=1, keepdims=True)
            l1 = a1 * l1 + jnp.sum(t1, axis=1, keepdims=True)
            return n0, n1, l0, l1, acc

        def gbody(ng, c):
            neg = jnp.full((BLK, 1), NEG_INF, F32)
            z1 = jnp.zeros((BLK, 1), F32)
            rows = [pl.ds(pl.multiple_of((g * ng + a) * BLK, BLK), BLK) for a in range(g)]
            qcats = [_stack_heads(q_ref[rows[a], :] * QK_SCALE, masks) for a in range(g)]
            first = [qk(qcats[a], 0) for a in range(g)]
            done = []
            for a in range(g):
                def step(t, cc, qcat=qcats[a]):
                    ss, st = cc
                    nxt = qk(qcat, t + 1)
                    return nxt, consume(ss, t, st, g, False)

                done.append(lax.fori_loop(0, ng, step, (first[a], (neg, neg, z1, z1, jnp.zeros((BLK, LANES), F32)))))
            for a in range(g):
                ss, state = done[a]
                m0, m1, l0, l1, acc = consume(ss, ng, state, a + 1, True)
                o_ref[rows[a], :] = acc / jnp.where(masks[0], l0, l1)
                lse_ref[rows[a], :] = jnp.where(masks[0], m0 + jnp.log(l0), m1 + jnp.log(l1))
            return c

        lax.fori_loop(0, NBLK // g, gbody, 0)

    cols, ospec, crspec = _fox_specs()
    osd = jax.ShapeDtypeStruct((nb, SEQ, FOX_W), F32)
    return pl.pallas_call(
        body, out_shape=(osd, osd), grid=(nb, FOX_W // LANES), in_specs=cols + [crspec], out_specs=(ospec, ospec),
        compiler_params=_cparams(dimension_semantics=("parallel", "parallel")), name=name,
    )(p3, p3, p3, crow)


def _fox_bwd(p3, crow, do, o, lse, *, do_off, name):
    nb = p3.shape[0]
    g = FOX_GROUP
    hd = HEAD_DIM

    def body(q_ref, k_ref, v_ref, cr_ref, do_ref, o_ref, lse_ref, dq_ref, dko_ref, dvo_ref, dcr_ref, dk_ref, dv_ref):
        masks = _head_masks(2)
        tri = _tri_bias(True)
        dk_ref[...] = jnp.zeros_like(dk_ref)
        dv_ref[...] = jnp.zeros_like(dv_ref)
        dcr_ref[...] = jnp.zeros_like(dcr_ref)

        def products(qcat, docat, t):
            out = []
            for e in range(g):
                krows = _fox_key_rows(t, e)
                out.append(lax.dot_general(qcat, k_ref[krows, :], _NT, preferred_element_type=F32))
                out.append(lax.dot_general(docat, v_ref[krows, :], _NT, preferred_element_type=F32))
            return tuple(out)

        def consume(prod, t, ctx, dq, nblk, diag):
            qcat, docat, lse0, lse1, dl0, dl1 = ctx
            for e in range(nblk):
                jb = g * t + e
                krows = _fox_key_rows(t, e)
                s, dp = prod[2 * e], prod[2 * e + 1]
                cr = cr_ref[jb]
                u0 = s[:BLK] - cr[0:1, :]
                u1 = s[BLK:] - cr[1:2, :]
                if diag and e == nblk - 1:
                    u0, u1 = u0 + tri, u1 + tri
                p0 = jnp.exp(u0 - lse0)
                p1 = jnp.exp(u1 - lse1)
                ds0 = p0 * (dp[:BLK] - dl0)
                ds1 = p1 * (dp[BLK:] - dl1)
                dcr_ref[jb, 0:1, :] += jnp.sum(ds0, axis=0, keepdims=True)
                dcr_ref[jb, 1:2, :] += jnp.sum(ds1, axis=0, keepdims=True)
                ds0b, ds1b = ds0.astype(BF16), ds1.astype(BF16)
                pcat = jnp.concatenate([p0.astype(BF16), p1.astype(BF16)], axis=0)
                dscat = jnp.concatenate([ds0b, ds1b], axis=0)
                dv_ref[krows, :] += lax.dot_general(pcat, docat, _T0, preferred_element_type=F32)
                dk_ref[krows, :] += lax.dot_general(dscat, qcat, _T0, preferred_element_type=F32)
                dsrow = jnp.concatenate([ds0b, ds1b], axis=1)
                dq = dq + jnp.dot(dsrow, _stack_heads(k_ref[krows, :] * QK_SCALE, masks), preferred_element_type=F32)
            return dq

        def gbody(ng, c):
            ctxs, rows = [], []
            for a in range(g):
                r = pl.ds(pl.multiple_of((g * ng + a) * BLK, BLK), BLK)
                qcat = _stack_heads(q_ref[r, :] * QK_SCALE, masks)
                dob = do_ref[r, :].astype(BF16)
                prod = dob.astype(F32) * o_ref[r, :]
                z = jnp.zeros_like(prod)
                dl0 = jnp.sum(jnp.where(masks[0], prod, z), axis=1, keepdims=True)
                dl1 = jnp.sum(jnp.where(masks[1], prod, z), axis=1, keepdims=True)
                lseb = lse_ref[r, :]
                ctxs.append((qcat, _stack_heads(dob, masks), lseb[:, 0:1], lseb[:, hd:hd + 1], dl0, dl1))
                rows.append(r)
            first = [products(ctxs[a][0], ctxs[a][1], 0) for a in range(g)]
            done = []
            for a in range(g):
                def step(t, cc, ctx=ctxs[a]):
                    pr, dq = cc
                    nxt = products(ctx[0], ctx[1], t + 1)
                    return nxt, consume(pr, t, ctx, dq, g, False)

                done.append(lax.fori_loop(0, ng, step, (first[a], jnp.zeros((BLK, LANES), F32))))
            for a in range(g):
                pr, dq = done[a]
                dq_ref[rows[a], :] = consume(pr, ng, ctxs[a], dq, a + 1, True).astype(dq_ref.dtype)
            return c

        lax.fori_loop(0, NBLK // g, gbody, 0)
        dko_ref[...] = dk_ref[...].astype(dko_ref.dtype)
        dvo_ref[...] = dv_ref[...].astype(dvo_ref.dtype)

    cols, ospec, crspec = _fox_specs()
    dospec = pl.BlockSpec((None, SEQ, LANES), lambda b, j: (b, 0, do_off + j))
    osd = jax.ShapeDtypeStruct((nb, SEQ, FOX_W), BF16)
    return pl.pallas_call(
        body, out_shape=(osd, osd, osd, jax.ShapeDtypeStruct((nb, FOX_W // LANES, NBLK, 8, BLK), F32)),
        grid=(nb, FOX_W // LANES), in_specs=cols + [crspec, dospec, ospec, ospec], out_specs=(ospec, ospec, ospec, crspec),
        scratch_shapes=[pltpu.VMEM((SEQ, LANES), F32)] * 2,
        compiler_params=_cparams(dimension_semantics=("parallel", "parallel")), name=name,
    )(p3, p3, p3, crow, do, o, lse)


_B1, _B2 = FOX_W // LANES, (FOX_W + DIL_W) // LANES


def _dy_gate_bwd(dx2b, wo, fox, dil, memo, p16, *, tm, tn, name):
    t, d = dx2b.shape
    assert FOX_W % tn == 0 and DIL_W % tn == 0 and MEM_W % tn == 0 and all(c % tn == 0 for c in (C_FG, C_DG, C_MG))
    n1, n2, n3 = FOX_W // tn, (FOX_W + DIL_W) // tn, MIX_W // tn

    def body(dx_ref, w_ref, f_ref, d_ref, m_ref, g_ref, da_ref, dg_ref):
        j = pl.program_id(1)
        wv = w_ref[...]
        for c0 in range(0, tm, min(tm, 2 * MM_CHUNK)):
            rows = pl.ds(c0, min(tm, 2 * MM_CHUNK))
            dyv = lax.dot_general(dx_ref[rows, :], wv, _NT, preferred_element_type=F32)
            a = jnp.where(j < n1, f_ref[rows, :], jnp.where(j < n2, d_ref[rows, :], m_ref[rows, :]))
            gt = g_ref[rows, :].astype(F32)
            sg = 1.0 / (1.0 + jnp.exp(-gt))
            da_ref[rows, :] = (dyv * gt * sg).astype(da_ref.dtype)
            dg_ref[rows, :] = (dyv * a * sg * (1.0 + gt * (1.0 - sg))).astype(dg_ref.dtype)

    def gcol(j):
        return jnp.where(j < n1, C_FG // tn + j, jnp.where(j < n2, C_DG // tn + j - n1, C_MG // tn + j - n2))

    tile = pl.BlockSpec((tm, tn), lambda i, j: (i, j))
    return pl.pallas_call(
        body,
        out_shape=(jax.ShapeDtypeStruct((t, MIX_W), BF16), jax.ShapeDtypeStruct((t, MIX_W), BF16)),
        grid=(t // tm, n3),
        in_specs=[pl.BlockSpec((tm, d), lambda i, j: (i, 0)), pl.BlockSpec((tn, d), lambda i, j: (j, 0)),
                  pl.BlockSpec((tm, tn), lambda i, j: (i, jnp.minimum(j, n1 - 1))),
                  pl.BlockSpec((tm, tn), lambda i, j: (i, jnp.clip(j - n1, 0, n2 - n1 - 1))),
                  pl.BlockSpec((tm, tn), lambda i, j: (i, jnp.clip(j - n2, 0, n3 - n2 - 1))),
                  pl.BlockSpec((tm, tn), lambda i, j: (i, gcol(j)))],
        out_specs=(tile, tile),
        compiler_params=_cparams(dimension_semantics=("parallel", "parallel")),
        name=name,
    )(dx2b, wo, fox, dil, memo, p16)


def _silu(g):
    return g / (1.0 + jnp.exp(-g))


def _out_loss(fox, dil, memo, p16, wo, x, tgt, gfin, *, tm, name):
    t, d = x.shape
    n_feat = float(d)

    def body(f_ref, d_ref, m_ref, fg_ref, dg_ref, mg_ref, w_ref, x_ref, t_ref, g_ref, y_ref, dx_ref, dxb_ref, st_ref):
        i = pl.program_id(0)

        @pl.when(i == 0)
        def _():
            st_ref[...] = jnp.zeros_like(st_ref)

        wv, gv = w_ref[...], g_ref[...]
        half = tm // 2
        for c0 in (0, half):
            rows = pl.ds(c0, half)
            y = jnp.concatenate([(a_ref[rows, :] * _silu(gt_ref[rows, :].astype(F32))).astype(BF16)
                                 for a_ref, gt_ref in ((f_ref, fg_ref), (d_ref, dg_ref), (m_ref, mg_ref))], axis=1)
            y_ref[rows, :] = y
            x2 = x_ref[rows, :] + jnp.dot(y, wv, preferred_element_type=F32)
            r = lax.rsqrt(jnp.mean(x2 * x2, axis=-1, keepdims=True) + RMS_EPS)
            nrm = x2 * r
            err = nrm * gv - t_ref[rows, :]
            dout = err * (1.0 / n_feat)
            dn = dout * gv
            dx2 = r * (dn - nrm * jnp.mean(dn * nrm, axis=-1, keepdims=True))
            dx_ref[rows, :] = dx2
            dxb_ref[rows, :] = dx2.astype(dxb_ref.dtype)
            st_ref[0:1, :] += jnp.sum(dout * nrm, axis=0, keepdims=True)
            st_ref[1:2, :] += (0.5 / n_feat) * jnp.sum(err * err, axis=0, keepdims=True)

    row = pl.BlockSpec((tm, d), lambda i: (i, 0))
    whole = lambda w: pl.BlockSpec((tm, w), lambda i: (i, 0))
    gate = lambda w, col: pl.BlockSpec((tm, w), lambda i: (i, col // w))
    return pl.pallas_call(
        body,
        out_shape=(jax.ShapeDtypeStruct((t, MIX_W), BF16), jax.ShapeDtypeStruct((t, d), F32), jax.ShapeDtypeStruct((t, d), BF16),
                   jax.ShapeDtypeStruct((8, d), F32)),
        grid=(t // tm,),
        in_specs=[whole(FOX_W), whole(DIL_W), whole(MEM_W), gate(FOX_W, C_FG), gate(DIL_W, C_DG), gate(MEM_W, C_MG),
                  pl.BlockSpec((MIX_W, d), lambda i: (0, 0)), row, row, pl.BlockSpec((1, d), lambda i: (0, 0))],
        out_specs=(pl.BlockSpec((tm, MIX_W), lambda i: (i, 0)), row, row, pl.BlockSpec((8, d), lambda i: (0, 0))),
        compiler_params=_cparams(dimension_semantics=("arbitrary",)),
        name=name,
    )(fox, dil, memo, p16, p16, p16, wo, x, tgt, gfin)


def _dh_rms_bwd(dp, w, x, g, resid, *, tm, name):
    t, d = x.shape
    kdim = dp.shape[1]

    def body(*refs):
        if resid is not None:
            dp_ref, w_ref, x_ref, g_ref, r_ref, dx_ref, gg_ref = refs
        else:
            dp_ref, w_ref, x_ref, g_ref, dx_ref, gg_ref = refs

        @pl.when(pl.program_id(0) == 0)
        def _():
            gg_ref[...] = jnp.zeros_like(gg_ref)

        dh = lax.dot_general(dp_ref[...], w_ref[...], _NT, preferred_element_type=F32)
        xv = x_ref[...]
        r = lax.rsqrt(jnp.mean(xv * xv, axis=-1, keepdims=True) + RMS_EPS)
        nrm = xv * r
        dn = dh * g_ref[...]
        dx = r * (dn - nrm * jnp.mean(dn * nrm, axis=-1, keepdims=True))
        if resid is not None:
            dx = dx + r_ref[...]
        dx_ref[...] = dx
        gg_ref[0:1, :] += jnp.sum(dh * nrm, axis=0, keepdims=True)

    row = pl.BlockSpec((tm, d), lambda i: (i, 0))
    in_specs = [pl.BlockSpec((tm, kdim), lambda i: (i, 0)),
                pl.BlockSpec((d, kdim), lambda i: (0, 0), pipeline_mode=pl.Buffered(1)), row,
                pl.BlockSpec((1, d), lambda i: (0, 0))]
    args = [dp, w, x, g]
    if resid is not None:
        in_specs.append(row)
        args.append(resid)
    return pl.pallas_call(
        body,
        out_shape=(jax.ShapeDtypeStruct((t, d), F32), jax.ShapeDtypeStruct((8, d), F32)),
        grid=(t // tm,),
        in_specs=in_specs,
        out_specs=(row, pl.BlockSpec((8, d), lambda i: (0, 0))),
        compiler_params=_cparams(dimension_semantics=("arbitrary",)),
        name=name,
    )(*args)


_FLOG0 = 4 * FOX_W
_W_IN_SEGMENTS = ((0, _FLOG0, 0), (_FLOG0, _FLOG0 + FOX_HEADS, PW), (_FLOG0 + FOX_HEADS, IN_W, C_DQ))
SHARD_W = IN_W // N_CHIPS


def _rearrange_w_in(shards):
    def cols(lo, hi):
        parts = []
        for k in range(N_CHIPS):
            a, b = max(lo, k * SHARD_W), min(hi, (k + 1) * SHARD_W)
            if a < b:
                parts.append(shards[k][:, a - k * SHARD_W:b - k * SHARD_W])
        return parts

    (a0, a1, _), (f0, f1, _), (b0, b1, _) = _W_IN_SEGMENTS
    pad = jnp.zeros((shards[0].shape[0], PWF - PW - FOX_HEADS), shards[0].dtype)
    return jnp.concatenate(cols(a0, a1) + cols(b0, b1) + cols(f0, f1) + [pad], axis=1)


def _w_in_grad_slabs(g):
    slabs = []
    for k in range(N_CHIPS):
        parts = []
        for lo, hi, at in _W_IN_SEGMENTS:
            a, b = max(lo, k * SHARD_W), min(hi, (k + 1) * SHARD_W)
            if a < b:
                parts.append(g[:, at + a - lo:at + b - lo])
        slabs.append(jnp.concatenate(parts, axis=1))
    return jnp.stack(slabs, axis=0)


def _local_grads(x, mem, norm_g, w_r, b_forget, mem_norm_g, w_kv, w_o, final_norm_g, tgt, start_reduce=None,
                 start_reduce_small=None, early_token=None, late_weights=None):
    nb = x.shape[0]
    t = nb * SEQ
    x2d = x.reshape(t, D_MODEL)
    tgt2d = tgt.reshape(t, D_MODEL)
    tabs = _rope_tables()
    bpad = jnp.pad(b_forget.reshape(1, FOX_HEADS), ((0, 0), (0, LANES - FOX_HEADS)))

    gain0 = norm_g.reshape(1, D_MODEL)
    if early_token is not None:
        gain0 = gain0 + early_token[0:1, 0:1]
    h, p16, dqkv, flog = _proj(x2d, gain0, w_r, tabs, n=PWF, tm=1024, tn=768, name="proj")
    c12 = _flog_fwd(flog, bpad, nb=nb, ts=256, name="flog_fwd")

    crow = c12[:, :FOX_HEADS].reshape(nb, NBLK, BLK, FOX_HEADS // 2, 2).transpose(0, 3, 1, 4, 2)
    crow = jnp.pad(crow, ((0, 0), (0, 0), (0, 0), (0, 6), (0, 0)))
    p3 = p16.reshape(nb, SEQ, PWF)
    fox, fox_lse = _fox_fwd(p3, crow, name="fox_fwd")
    if late_weights is not None:
        w_kv, w_o = late_weights(fox_lse)

    dqkv3 = dqkv.reshape(nb, SEQ, 3 * DIL_W)
    dil, dil_lse = _dil_fwd(dqkv3, name="dil_fwd")

    mh = _rms_fwd(mem.reshape(nb * MEM_LEN, D_MODEL), mem_norm_g.reshape(1, D_MODEL), tm=nb * MEM_LEN, name="rms_mem")
    mkv = _matmul(mh, w_kv, out_dtype=BF16, tm=nb * MEM_LEN, tn=512, tk=D_MODEL, name="mem_kv")
    mkv3 = mkv.reshape(nb, MEM_LEN, 2 * MEM_W)
    memo, mem_lse = _mem_fwd(p3, mkv3, qoff=C_MQ // LANES, name="mem_fwd")

    fox2, dil2, memo2 = fox.reshape(t, FOX_W), dil.reshape(t, DIL_W), memo.reshape(t, MEM_W)
    y, dx2, dx2b, st = _out_loss(fox2, dil2, memo2, p16, w_o, x2d, tgt2d, final_norm_g.reshape(1, D_MODEL), tm=256,
                                 name="out_loss")

    g_wo = _matmul(y, dx2b, mode="tn", out_dtype=BF16, tm=1024, tn=512, tk=t, name="grad_w_out")
    datt, dgate = _dy_gate_bwd(dx2b, w_o, fox2, dil2, memo2, p16, tm=2048, tn=256, name="dy_gate_bwd")
    datt3 = datt.reshape(nb, SEQ, MIX_W)

    dmq, dmk, dmv = _mem_bwd(p3, mkv3, datt3, memo, mem_lse, qoff=C_MQ // LANES, do_off=_B2, name="mem_bwd")
    dmkv = jnp.concatenate([dmk, dmv], axis=-1).reshape(nb * MEM_LEN, 2 * MEM_W).astype(BF16)
    g_wkv = _matmul(mh, dmkv, mode="tn", out_dtype=BF16, tm=512, tn=512, tk=nb * MEM_LEN, name="grad_w_kv")
    mem_gain = mem_norm_g.reshape(1, D_MODEL)
    if start_reduce_small is not None:
        tok = start_reduce_small(g_wkv, g_wo)[0:1, 0:1]
        mem_gain, crow = mem_gain + tok, crow + tok
    _, gmn = _dh_rms_bwd(dmkv, w_kv, mem.reshape(nb * MEM_LEN, D_MODEL), mem_gain, None, tm=nb * MEM_LEN, name="mem_rms_bwd")

    dfq, dfk, dfv, dcr = _fox_bwd(p3, crow, datt3, fox, fox_lse, do_off=0, name="fox_bwd")
    dcol = -dcr[:, :, :, :2, :].transpose(0, 2, 4, 1, 3).reshape(t, FOX_HEADS)
    dcol = jnp.pad(dcol, ((0, 0), (0, LANES - FOX_HEADS)))
    dflog, gb = _flog_bwd(dcol, flog, bpad, nb=nb, ts=256, name="flog_bwd")

    ddq, ddk, ddv = _dil_bwd(dqkv3, datt3, dil, dil_lse, tabs, do_off=_B1, name="dil_bwd")

    flat = lambda a: a.reshape(t, -1)
    dp = jnp.concatenate([flat(dfq), flat(dfk), flat(dfv), dgate[:, :FOX_W], flat(ddq), flat(ddk), flat(ddv),
                          dgate[:, FOX_W:FOX_W + DIL_W], flat(dmq), dgate[:, FOX_W + DIL_W:], dflog,
                          jnp.zeros((t, PWF - PW - LANES), BF16)], axis=1)
    g_wr = _matmul(h, dp, mode="tn", out_dtype=BF16, tm=D_MODEL, tn=768, tk=t, name="grad_w_in")
    gain = norm_g.reshape(1, D_MODEL)
    if start_reduce is not None:
        gain = gain + start_reduce(g_wr)[0:1, 0:1]
    gx, gng = _dh_rms_bwd(dp, w_r, x2d, gain, dx2, tm=256, name="in_rms_bwd")

    gb_row = jnp.pad(gb[0:1, :], ((0, 0), (0, D_MODEL - LANES)))
    small = jnp.concatenate([gng[0:1], gmn[0:1], st[0:1], gb_row, st[1:2], jnp.zeros((3, D_MODEL), F32)], axis=0)
    return gx.reshape(nb, SEQ, D_MODEL), g_wr, g_wkv, g_wo, small


MESH = pl.DeviceIdType.MESH
ANY = pl.BlockSpec(memory_space=pl.ANY)


def _place():
    x, y, c = lax.axis_index("x"), lax.axis_index("y"), lax.axis_index("c")
    other_chips = [(1 - x, y), (x, 1 - y), (1 - x, 1 - y)]
    return x, y, c, other_chips


def _gather_weights(shards):
    n = len(shards)

    def body(*refs):
        in_refs, out_refs = refs[:n], refs[n:2 * n]
        send_sems, recv_sems = refs[2 * n:]
        x, y, c, chips = _place()
        me_chip = 2 * x + y
        sibling = (x, y, 1 - c)

        def half(ref, pc, rows):
            return ref.at[pl.ds(pc * (rows // 2), rows // 2), :]

        def rcopy(k, src, dst, to):
            return pltpu.make_async_remote_copy(src_ref=src, dst_ref=dst, send_sem=send_sems.at[k], recv_sem=recv_sems.at[k],
                                                device_id=to, device_id_type=MESH)

        sends = []
        for t in range(n):
            rows = shards[t].shape[0]
            for j, chip in enumerate(chips):
                cp = rcopy(6 * t + j, half(in_refs[t], c, rows), half(out_refs[t].at[me_chip], c, rows), (*chip, c))
                cp.start()
                sends.append(cp)
        for t in range(n):
            rows = shards[t].shape[0]
            for j, chip in enumerate(chips):
                slot = out_refs[t].at[2 * chip[0] + chip[1]]
                rcopy(6 * t + j, half(slot, c, rows), half(slot, c, rows), sibling).wait_recv()
                fw = rcopy(6 * t + 3 + j, half(slot, c, rows), half(slot, c, rows), sibling)
                fw.start()
                sends.append(fw)
        for t in range(n):
            rows = shards[t].shape[0]
            for j, chip in enumerate(chips):
                slot = out_refs[t].at[2 * chip[0] + chip[1]]
                rcopy(6 * t + 3 + j, half(slot, 1 - c, rows), half(slot, 1 - c, rows), sibling).wait_recv()
        for cp in sends:
            cp.wait_send()

    return pl.pallas_call(
        body,
        out_shape=tuple(jax.ShapeDtypeStruct((N_CHIPS,) + s.shape, s.dtype) for s in shards),
        in_specs=[ANY] * n,
        out_specs=tuple([ANY] * n),
        scratch_shapes=[pltpu.SemaphoreType.DMA((6 * n,)), pltpu.SemaphoreType.DMA((6 * n,))],
        name="gather_weights",
    )(*shards)


def _pair_exchange(gs, *, name):
    n = len(gs)

    def body(*refs):
        g_refs, r_refs = refs[:n], refs[n:2 * n]
        send_sems, recv_sems = refs[2 * n:]
        x, y, c, _ = _place()
        cps = []
        for t in range(n):
            hr = gs[t].shape[1] // 2
            cp = pltpu.make_async_remote_copy(src_ref=g_refs[t].at[:, pl.ds((1 - c) * hr, hr), :], dst_ref=r_refs[t],
                                              send_sem=send_sems.at[t], recv_sem=recv_sems.at[t],
                                              device_id=(x, y, 1 - c), device_id_type=MESH)
            cp.start()
            cps.append(cp)
        for cp in cps:
            cp.wait()

    return pl.pallas_call(
        body,
        out_shape=tuple(jax.ShapeDtypeStruct((g.shape[0], g.shape[1] // 2, g.shape[2]), g.dtype) for g in gs),
        in_specs=[ANY] * n,
        out_specs=tuple([ANY] * n),
        scratch_shapes=[pltpu.SemaphoreType.DMA((n,)), pltpu.SemaphoreType.DMA((n,))],
        name=name,
    )(*gs)


_HBM = pl.BlockSpec(memory_space=pltpu.HBM)
_SEM = pl.BlockSpec(memory_space=pltpu.SEMAPHORE)
_DATAFLOW = pltpu.SideEffectType.DATAFLOW_SIDE_EFFECTING


def _chip_copies(p_refs, land_refs, send_sems, recv_sems):
    x, y, c, chips = _place()
    me_chip = 2 * x + y
    return [pltpu.make_async_remote_copy(src_ref=p_refs[t].at[2 * chip[0] + chip[1]], dst_ref=land_refs[t].at[me_chip],
                                         send_sem=send_sems.at[3 * t + j], recv_sem=recv_sems.at[3 * t + j],
                                         device_id=(*chip, c), device_id_type=MESH)
            for t in range(len(p_refs)) for j, chip in enumerate(chips)]


def _chip_exchange_start(ps, *, tag):
    n = len(ps)

    def body(*refs):
        p_refs, land_refs = refs[:n], refs[n:2 * n]
        send_sems, recv_sems = refs[2 * n:2 * n + 2]
        token = refs[-1]
        for cp in _chip_copies(p_refs, land_refs, send_sems, recv_sems):
            cp.start()
        token[...] = jnp.zeros_like(token)

    hbm = [pltpu.HBM(p.shape, p.dtype) for p in ps]
    args = [pltpu.with_memory_space_constraint(p, pltpu.HBM) for p in ps]
    args += [pltpu.with_memory_space_constraint(lax.empty(p.shape, p.dtype), pltpu.HBM) for p in ps]
    out = pl.pallas_call(
        body,
        name=f"chip_exchange_start_{tag}",
        out_shape=(pltpu.SemaphoreType.DMA((3 * n,)), pltpu.SemaphoreType.DMA((3 * n,)), *hbm, *hbm,
                   jax.ShapeDtypeStruct((8, LANES), F32)),
        in_specs=[_HBM] * (2 * n),
        out_specs=(_SEM, _SEM, *([_HBM] * (2 * n)), pl.BlockSpec(memory_space=pltpu.VMEM)),
        input_output_aliases={i: 2 + i for i in range(2 * n)},
        compiler_params=pltpu.CompilerParams(has_side_effects=_DATAFLOW),
    )(*args)
    return out[0], out[1], out[2:2 + n], out[2 + n:2 + 2 * n], out[-1]


def _chip_exchange_wait(send_sems, recv_sems, p_thru, land_thru, after, *, tag):
    n = len(p_thru)

    def body(*refs):
        p_refs, land_refs = refs[:n], refs[n:2 * n]
        ssem, rsem = refs[2 * n:2 * n + 2]
        for cp in _chip_copies(p_refs, land_refs, ssem, rsem):
            cp.wait_send()
            cp.wait_recv()

    hbm = [pltpu.HBM(p.shape, p.dtype) for p in p_thru]
    out = pl.pallas_call(
        body,
        name=f"chip_exchange_wait_{tag}",
        out_shape=(*hbm, *hbm),
        in_specs=[_HBM] * (2 * n) + [_SEM, _SEM, ANY],
        out_specs=tuple([_HBM] * (2 * n)),
        input_output_aliases={i: i for i in range(2 * n)},
        compiler_params=pltpu.CompilerParams(has_side_effects=_DATAFLOW),
    )(*p_thru, *land_thru, send_sems, recv_sems, after)
    return out[:n], out[n:]


def _shard_copies(s_refs, land_refs, send_sems, recv_sems):
    x, y, c, chips = _place()
    me_chip = 2 * x + y
    return [pltpu.make_async_remote_copy(src_ref=s_refs[t], dst_ref=land_refs[t].at[me_chip],
                                         send_sem=send_sems.at[3 * t + j], recv_sem=recv_sems.at[3 * t + j],
                                         device_id=(*chip, c), device_id_type=MESH)
            for t in range(len(s_refs)) for j, chip in enumerate(chips)]


def _gather_late_start(shards):
    n = len(shards)

    def body(*refs):
        s_refs, land_refs = refs[:n], refs[n:2 * n]
        send_sems, recv_sems = refs[2 * n:2 * n + 2]
        token = refs[-1]
        for cp in _shard_copies(s_refs, land_refs, send_sems, recv_sems):
            cp.start()
        token[...] = jnp.zeros_like(token)

    lands = [(N_CHIPS,) + s.shape for s in shards]
    args = [pltpu.with_memory_space_constraint(s, pltpu.HBM) for s in shards]
    args += [pltpu.with_memory_space_constraint(lax.empty(shp, s.dtype), pltpu.HBM) for shp, s in zip(lands, shards)]
    out = pl.pallas_call(
        body,
        name="gather_late_start",
        out_shape=(pltpu.SemaphoreType.DMA((3 * n,)), pltpu.SemaphoreType.DMA((3 * n,)),
                   *[pltpu.HBM(s.shape, s.dtype) for s in shards], *[pltpu.HBM(shp, s.dtype) for shp, s in zip(lands, shards)],
                   jax.ShapeDtypeStruct((8, LANES), F32)),
        in_specs=[_HBM] * (2 * n),
        out_specs=(_SEM, _SEM, *([_HBM] * (2 * n)), pl.BlockSpec(memory_space=pltpu.VMEM)),
        input_output_aliases={i: 2 + i for i in range(2 * n)},
        compiler_params=pltpu.CompilerParams(has_side_effects=_DATAFLOW),
    )(*args)
    return out[0], out[1], out[2:2 + n], out[2 + n:2 + 2 * n], out[-1]


def _gather_late_wait(send_sems, recv_sems, s_thru, land_thru, after):
    n = len(s_thru)

    def body(*refs):
        s_refs, land_refs = refs[:n], refs[n:2 * n]
        ssem, rsem = refs[2 * n:2 * n + 2]
        for cp in _shard_copies(s_refs, land_refs, ssem, rsem):
            cp.wait_send()
            cp.wait_recv()

    out = pl.pallas_call(
        body,
        name="gather_late_wait",
        out_shape=(*[pltpu.HBM(s.shape, s.dtype) for s in s_thru], *[pltpu.HBM(l.shape, l.dtype) for l in land_thru]),
        in_specs=[_HBM] * (2 * n) + [_SEM, _SEM, ANY],
        out_specs=tuple([_HBM] * (2 * n)),
        input_output_aliases={i: i for i in range(2 * n)},
        compiler_params=pltpu.CompilerParams(has_side_effects=_DATAFLOW),
    )(*s_thru, *land_thru, send_sems, recv_sems, after)
    return out[:n], out[n:]


def _pair_swap(rs):
    n = len(rs)

    def body(*refs):
        r_refs, o_refs = refs[:n], refs[n:2 * n]
        send_sems, recv_sems = refs[2 * n:]
        x, y, c, _ = _place()
        cps = []
        for t in range(n):
            cp = pltpu.make_async_remote_copy(src_ref=r_refs[t], dst_ref=o_refs[t], send_sem=send_sems.at[t],
                                              recv_sem=recv_sems.at[t], device_id=(x, y, 1 - c), device_id_type=MESH)
            cp.start()
            cps.append(cp)
        for cp in cps:
            cp.wait()

    return pl.pallas_call(
        body,
        out_shape=tuple(jax.ShapeDtypeStruct(r.shape, r.dtype) for r in rs),
        in_specs=[ANY] * n,
        out_specs=tuple([ANY] * n),
        scratch_shapes=[pltpu.SemaphoreType.DMA((n,)), pltpu.SemaphoreType.DMA((n,))],
        name="pair_swap",
    )(*rs)


N_DEV = 8
LOSS_ROW = 4


def _small_allreduce(small):
    def body(s_ref, o_ref, all_ref, send_sems, recv_sems):
        x, y, c, _ = _place()
        me = 4 * x + 2 * y + c
        all_ref[me] = s_ref[...]
        cps = []
        for k in range(1, N_DEV):
            peer = tuple(1 - p if (k >> s) & 1 else p for p, s in ((x, 2), (y, 1), (c, 0)))
            cp = pltpu.make_async_remote_copy(src_ref=s_ref, dst_ref=all_ref.at[me], send_sem=send_sems.at[k - 1],
                                              recv_sem=recv_sems.at[k - 1], device_id=peer, device_id_type=MESH)
            cp.start()
            cps.append(cp)
        for cp in cps:
            cp.wait()
        tot = all_ref[0]
        for d in range(1, N_DEV):
            tot = tot + all_ref[d]
        o_ref[...] = tot
        o_ref[LOSS_ROW:LOSS_ROW + 1, :] = jnp.broadcast_to(jnp.sum(tot[LOSS_ROW:LOSS_ROW + 1, :], axis=1, keepdims=True),
                                                          (1, tot.shape[1]))

    vm = pl.BlockSpec(memory_space=pltpu.VMEM)
    return pl.pallas_call(
        body,
        out_shape=jax.ShapeDtypeStruct(small.shape, small.dtype),
        in_specs=[vm],
        out_specs=vm,
        scratch_shapes=[pltpu.VMEM((N_DEV,) + small.shape, small.dtype), pltpu.SemaphoreType.DMA((N_DEV - 1,)),
                        pltpu.SemaphoreType.DMA((N_DEV - 1,))],
        name="small_allreduce",
    )(small)


def _sum_pair(g, recv, cidx, *, tr, name):
    n, hr, cols = recv.shape
    nr = hr // tr

    def body(c_ref, g_ref, r_ref, o_ref):
        o_ref[...] = (g_ref[...].astype(F32) + r_ref[...].astype(F32)).astype(o_ref.dtype)

    grid_spec = pltpu.PrefetchScalarGridSpec(
        num_scalar_prefetch=1,
        grid=(n, nr),
        in_specs=[pl.BlockSpec((None, tr, cols), lambda k, i, c_ref: (k, c_ref[0] * nr + i, 0)),
                  pl.BlockSpec((None, tr, cols), lambda k, i, c_ref: (k, i, 0))],
        out_specs=pl.BlockSpec((None, tr, cols), lambda k, i, c_ref: (k, i, 0)),
    )
    return pl.pallas_call(body, out_shape=jax.ShapeDtypeStruct(recv.shape, BF16), grid_spec=grid_spec,
                          compiler_params=_cparams(), name=name)(cidx, g, recv)


def _sum_chips(p, *, tr, name):
    _, rows, cols = p.shape

    def body(p_ref, o_ref):
        tot = p_ref[0].astype(F32)
        for k in range(1, N_CHIPS):
            tot = tot + p_ref[k].astype(F32)
        o_ref[...] = tot

    return pl.pallas_call(
        body,
        out_shape=jax.ShapeDtypeStruct((rows, cols), F32),
        grid=(rows // tr,),
        in_specs=[pl.BlockSpec((N_CHIPS, tr, cols), lambda i: (0, i, 0))],
        out_specs=pl.BlockSpec((tr, cols), lambda i: (i, 0)),
        compiler_params=_cparams(),
        name=name,
    )(p)


def _adamw(w, g, m, v, *, tr, name):
    rows, cols = w.shape
    bc1 = 1.0 / (1.0 - ADAM_B1 ** ADAM_STEP)
    bc2 = 1.0 / (1.0 - ADAM_B2 ** ADAM_STEP)

    def body(w_ref, g_ref, m_ref, v_ref, d_ref, nm_ref, nv_ref):
        gv = g_ref[...]
        nm = ADAM_B1 * m_ref[...] + (1.0 - ADAM_B1) * gv
        nv = ADAM_B2 * v_ref[...] + (1.0 - ADAM_B2) * (gv * gv)
        d_ref[...] = -ADAM_LR * ((nm * bc1) / (jnp.sqrt(nv * bc2) + ADAM_EPS) + ADAM_WD * w_ref[...])
        nm_ref[...] = nm
        nv_ref[...] = nv

    spec = pl.BlockSpec((tr, cols), lambda i: (i, 0))
    sd = jax.ShapeDtypeStruct((rows, cols), F32)
    return pl.pallas_call(body, out_shape=(sd, sd, sd), grid=(rows // tr,), in_specs=[spec] * 4, out_specs=(spec,) * 3,
                          compiler_params=_cparams(), name=name)(w, g, m, v)


def _adamw_halves(w, own, sib, cidx, m, v, *, tr, name):
    rows, cols = w.shape
    hr = own.shape[0]
    nr = hr // tr
    assert rows == 2 * hr and hr % tr == 0
    bc1 = 1.0 / (1.0 - ADAM_B1 ** ADAM_STEP)
    bc2 = 1.0 / (1.0 - ADAM_B2 ** ADAM_STEP)

    def body(c_ref, w_ref, o_ref, s_ref, m_ref, v_ref, g_ref, d_ref, nm_ref, nv_ref):
        mine = (pl.program_id(0) // nr) == c_ref[0]
        gv = jnp.where(mine, o_ref[...], s_ref[...])
        nm = ADAM_B1 * m_ref[...] + (1.0 - ADAM_B1) * gv
        nv = ADAM_B2 * v_ref[...] + (1.0 - ADAM_B2) * (gv * gv)
        g_ref[...] = gv
        d_ref[...] = -ADAM_LR * ((nm * bc1) / (jnp.sqrt(nv * bc2) + ADAM_EPS) + ADAM_WD * w_ref[...])
        nm_ref[...] = nm
        nv_ref[...] = nv

    full = pl.BlockSpec((tr, cols), lambda i, c_ref: (i, 0))
    half = pl.BlockSpec((tr, cols), lambda i, c_ref: (i % nr, 0))
    sd = jax.ShapeDtypeStruct((rows, cols), F32)
    grid_spec = pltpu.PrefetchScalarGridSpec(num_scalar_prefetch=1, grid=(rows // tr,), in_specs=[full, half, half, full, full],
                                             out_specs=(full,) * 4)
    return pl.pallas_call(body, out_shape=(sd,) * 4, grid_spec=grid_spec, compiler_params=_cparams(), name=name)(
        cidx, w, own, sib, m, v)


def _pack_small(norm, mem_norm, final_norm, b_forget):
    rows = [norm.reshape(1, D_MODEL), mem_norm.reshape(1, D_MODEL), final_norm.reshape(1, D_MODEL),
            jnp.pad(b_forget.reshape(1, FOX_HEADS), ((0, 0), (0, D_MODEL - FOX_HEADS))), jnp.zeros((4, D_MODEL), F32)]
    return jnp.concatenate(rows, axis=0)


def _unpack_small(a):
    return a[0:1], a[3:4, :FOX_HEADS], a[1:2], a[2]


def kernel(x, mem, norm_g, w_in, b_forget, mem_norm_g, w_mem_kv, w_out, final_norm_g, loss_target, m_norm_g, m_w_in, m_b_forget, m_mem_norm_g, m_w_mem_kv, m_w_out, m_final_norm_g, v_norm_g, v_w_in, v_b_forget, v_mem_norm_g, v_w_mem_kv, v_w_out, v_final_norm_g):
    core = lax.axis_index("c").astype(jnp.int32)
    me_chip = (2 * lax.axis_index("x") + lax.axis_index("y")).astype(jnp.int32)
    cidx = core.reshape(1)

    def own_slot(arr, own):
        return lax.dynamic_update_slice(arr, own[None].astype(arr.dtype), (me_chip,) + (0,) * own.ndim)

    win_b, late = w_in[0].astype(BF16), [w_mem_kv[0].astype(BF16), w_out[0].astype(BF16)]
    g_in, = _gather_weights([win_b])
    g_in, late = lax.optimization_barrier((own_slot(g_in, win_b), late))
    w_r = _rearrange_w_in([g_in[k] for k in range(N_CHIPS)])
    *late_flight, early_token = _gather_late_start(late)

    def late_weights(after):
        shards, landed = _gather_late_wait(*late_flight, after)
        g_kv, g_out = (own_slot(g, s) for g, s in zip(landed, shards))
        return g_kv.reshape(D_MODEL, 2 * MEM_W), g_out.reshape(MIX_W, D_MODEL)

    trs = (128, 128, 256)
    names = ("w_in", "w_mem_kv", "w_out")
    flights = {}

    def exchange(slabs, nms, ts, tag):
        recv = _pair_exchange(slabs, name=f"pair_exchange_{tag}")
        pair = [_sum_pair(g, r, cidx, tr=tr, name=f"sum_pair_{nm}") for g, r, tr, nm in zip(slabs, recv, ts, nms)]
        if tag == "w_in":
            pair[0] = _w_in_grad_slabs(pair[0][0])
        *flights[tag], token = _chip_exchange_start(pair, tag=tag)
        return token

    def start_reduce_small(g_wkv, g_wo):
        slabs = [g_wkv.reshape(N_CHIPS, D_MODEL // N_CHIPS, 2 * MEM_W), g_wo.reshape(N_CHIPS, MIX_W // N_CHIPS, D_MODEL)]
        return exchange(slabs, names[1:], trs[1:], "small")

    def start_reduce(g_wr):
        return exchange([g_wr[None]], names[:1], trs[:1], "w_in")

    gx, g_wr, g_wkv, g_wo, small = _local_grads(x, mem, norm_g, w_r, b_forget, mem_norm_g, None, None, final_norm_g, loss_target,
                                                start_reduce=start_reduce, start_reduce_small=start_reduce_small,
                                                early_token=early_token, late_weights=late_weights)

    pair, landed = [], []
    for tag in ("w_in", "small"):
        p, l = _chip_exchange_wait(*flights[tag], small, tag=tag)
        pair += list(p)
        landed += list(l)
    got = [lax.dynamic_update_slice(g, lax.dynamic_slice(p, (me_chip, 0, 0), (1,) + p.shape[1:]), (me_chip, 0, 0))
           for g, p in zip(landed, pair)]
    red = [_sum_chips(p, tr=tr, name=f"sum_chips_{nm}") for p, tr, nm in zip(got, trs, names)]
    sib = _pair_swap(red)

    outs = {}
    for nm, r, s, w, m, v, tr in zip(names, red, sib, (w_in, w_mem_kv, w_out), (m_w_in, m_w_mem_kv, m_w_out),
                                     (v_w_in, v_w_mem_kv, v_w_out), trs):
        outs[nm] = tuple(a[None] for a in _adamw_halves(w[0], r, s, cidx, m[0], v[0], tr=tr, name=f"adamw_{nm}"))

    gsum = _small_allreduce(small)
    sd, sm, sv = _adamw(_pack_small(norm_g, mem_norm_g, final_norm_g, b_forget), gsum,
                        _pack_small(m_norm_g, m_mem_norm_g, m_final_norm_g, m_b_forget),
                        _pack_small(v_norm_g, v_mem_norm_g, v_final_norm_g, v_b_forget), tr=8, name="adamw_small")
    loss = gsum[LOSS_ROW, 0]

    def group(i, small_arr):
        ng, bf, mg, fg = _unpack_small(small_arr)
        return (ng, outs["w_in"][i], bf, mg, outs["w_mem_kv"][i], outs["w_out"][i], fg)

    return (loss, gx, *group(0, gsum), *group(1, sd), *group(2, sm), *group(3, sv))
```

```python
import functools
import math

import jax
import jax.numpy as jnp
from jax import lax
from jax.experimental import pallas as pl
from jax.experimental.pallas import tpu as pltpu

F32 = jnp.float32
BF16 = jnp.bfloat16

D_MODEL = 1024
SEQ = 2048
HEAD_DIM = 64
FOX_HEADS = 12
DIL_HEADS = 12
MEM_HEADS = 4
MEM_HEAD_DIM = 128
MEM_LEN = 256
FOX_W = FOX_HEADS * HEAD_DIM
DIL_W = DIL_HEADS * HEAD_DIM
MEM_W = MEM_HEADS * MEM_HEAD_DIM
MIX_W = FOX_W + DIL_W + MEM_W
DILATIONS = ((128, 1), (512, 4), (2048, 16))
ROPE_THETA = 500000.0
ROPE_DIM = HEAD_DIM // 4
RMS_EPS = 1e-6
NEG_INF = -1e30
IN_SIZES = [FOX_W] * 4 + [FOX_HEADS] + [DIL_W] * 4 + [MEM_W] * 2
IN_W = sum(IN_SIZES)

ADAM_LR = 0.001
ADAM_B1 = 0.9
ADAM_B2 = 0.999
ADAM_EPS = 1e-08
ADAM_WD = 0.01
ADAM_STEP = 10

LANES = 128
N_CHIPS = 4
PW = 7168
PWF = PW + 4 * LANES
C_FQ, C_FK, C_FV, C_FG = 0, 768, 1536, 2304
C_DQ, C_DK, C_DV, C_DG = 3072, 3840, 4608, 5376
C_MQ, C_MG = 6144, 6656
VMEM_LIMIT = 48 * 1024 * 1024


def _cparams(**kw):
    return pltpu.CompilerParams(vmem_limit_bytes=VMEM_LIMIT, **kw)


MM_CHUNK = 256


def _matmul(a, b, *, out_dtype, tm, tn, tk, name, mode="nn"):
    if mode == "tn":
        (kdim, m), n = a.shape, b.shape[1]
        a_spec = pl.BlockSpec((tk, tm), lambda i, j, k: (k, i))
        b_spec = pl.BlockSpec((tk, tn), lambda i, j, k: (k, j))
        dims = _T0
    elif mode == "nt":
        (m, kdim), n = a.shape, b.shape[0]
        a_spec = pl.BlockSpec((tm, tk), lambda i, j, k: (i, k))
        b_spec = pl.BlockSpec((tn, tk), lambda i, j, k: (j, k))
        dims = _NT
    else:
        (m, kdim), n = a.shape, b.shape[1]
        a_spec = pl.BlockSpec((tm, tk), lambda i, j, k: (i, k))
        b_spec = pl.BlockSpec((tk, tn), lambda i, j, k: (k, j))
        dims = (((1,), (0,)), ((), ()))
    nk = kdim // tk
    assert m % tm == 0 and n % tn == 0 and kdim % tk == 0

    def body(a_ref, b_ref, o_ref, *scratch):
        if nk == 1:
            bv = b_ref[...]
            for c0 in range(0, tm, min(tm, MM_CHUNK)):
                rows = pl.ds(c0, min(tm, MM_CHUNK))
                av = a_ref[:, rows] if mode == "tn" else a_ref[rows, :]
                o_ref[rows, :] = lax.dot_general(av, bv, dims, preferred_element_type=F32).astype(o_ref.dtype)
            return
        prod = lax.dot_general(a_ref[...], b_ref[...], dims, preferred_element_type=F32)
        acc_ref, = scratch
        k = pl.program_id(2)

        @pl.when(k == 0)
        def _():
            acc_ref[...] = prod

        @pl.when(k > 0)
        def _():
            acc_ref[...] += prod

        @pl.when(k == nk - 1)
        def _():
            o_ref[...] = acc_ref[...].astype(o_ref.dtype)

    return pl.pallas_call(
        body,
        out_shape=jax.ShapeDtypeStruct((m, n), out_dtype),
        grid=(m // tm, n // tn, nk),
        in_specs=[a_spec, b_spec],
        out_specs=pl.BlockSpec((tm, tn), lambda i, j, k: (i, j)),
        scratch_shapes=[pltpu.VMEM((tm, tn), F32)] if nk > 1 else [],
        compiler_params=_cparams(dimension_semantics=("parallel", "parallel", "arbitrary")),
        name=name,
    )(a, b)


def _rms_fwd(x, g, *, tm, name):
    t, d = x.shape

    def body(x_ref, g_ref, h_ref):
        xv = x_ref[...]
        r = lax.rsqrt(jnp.mean(xv * xv, axis=-1, keepdims=True) + RMS_EPS)
        h_ref[...] = (xv * r * g_ref[...]).astype(h_ref.dtype)

    return pl.pallas_call(
        body,
        out_shape=jax.ShapeDtypeStruct((t, d), BF16),
        grid=(t // tm,),
        in_specs=[pl.BlockSpec((tm, d), lambda i: (i, 0)), pl.BlockSpec((1, d), lambda i: (0, 0))],
        out_specs=pl.BlockSpec((tm, d), lambda i: (i, 0)),
        compiler_params=_cparams(),
        name=name,
    )(x, g)


def _rope_tables():
    half = ROPE_DIM // 2
    pos = jnp.arange(SEQ, dtype=F32)
    inv_freq = 1.0 / (ROPE_THETA ** (jnp.arange(0, ROPE_DIM, 2, dtype=F32) / ROPE_DIM))
    ang = pos[:, None] * inv_freq[None, :]
    cos, sin = jnp.cos(ang), jnp.sin(ang)
    one = jnp.ones((SEQ, HEAD_DIM - ROPE_DIM), F32)
    zero = jnp.zeros((SEQ, HEAD_DIM - ROPE_DIM), F32)
    zh = jnp.zeros((SEQ, half), F32)
    c = jnp.concatenate([cos, cos, one], axis=1)
    s1 = jnp.concatenate([zh, sin, zero], axis=1)
    s2 = jnp.concatenate([-sin, zh, zero], axis=1)
    rep = LANES // HEAD_DIM
    return jnp.tile(c, (1, rep)), jnp.tile(s1, (1, rep)), jnp.tile(s2, (1, rep))


def _rope_apply(t, c, s1, s2, transpose=False):
    n = t.shape[-1]
    rep = n // LANES
    c, s1, s2 = (jnp.tile(u, (1, rep)) for u in (c, s1, s2))
    half = ROPE_DIM // 2
    if not transpose:
        return t * c + pltpu.roll(t, half, 1) * s1 + pltpu.roll(t, n - half, 1) * s2
    return t * c + pltpu.roll(t * s1, n - half, 1) + pltpu.roll(t * s2, half, 1)


PROJ_CHUNK = 256


def _proj(x, g, w, tabs, *, n, tm, tn, name):
    t, d = x.shape
    assert C_DQ % tn == 0 and (C_DV - C_DQ) % tn == 0 and (C_DG - C_DQ) % tn == 0
    rope_lo, rope_hi, dil_hi = C_DQ // tn, C_DV // tn, C_DG // tn
    flog_blk, flog_at = PW // tn, PW % tn
    assert flog_at % LANES == 0 and flog_at + LANES <= tn
    s_blocks = SEQ // tm

    def body(x_ref, g_ref, w_ref, c_ref, s1_ref, s2_ref, h_ref, o_ref, f_ref, fl_ref, h_scr):
        j = pl.program_id(1)

        @pl.when(j == 0)
        def _():
            xv = x_ref[...]
            r = lax.rsqrt(jnp.mean(xv * xv, axis=-1, keepdims=True) + RMS_EPS)
            hv = (xv * r * g_ref[...]).astype(BF16)
            h_scr[...] = hv
            h_ref[...] = hv

        def tile(kind):
            wv = w_ref[...]
            for c0 in range(0, tm, PROJ_CHUNK):
                rows = pl.ds(c0, PROJ_CHUNK)
                acc = jnp.dot(h_scr[rows, :], wv, preferred_element_type=F32)
                if kind == "rope":
                    acc = _rope_apply(acc, c_ref[rows, :], s1_ref[rows, :], s2_ref[rows, :])
                o_ref[rows, :] = acc.astype(o_ref.dtype)
                if kind in ("rope", "dv"):
                    f_ref[rows, :] = acc
                if kind == "flog":
                    fl_ref[rows, :] = acc[:, flog_at:flog_at + LANES]

        is_rope = jnp.logical_and(j >= rope_lo, j < rope_hi)
        is_dv = jnp.logical_and(j >= rope_hi, j < dil_hi)
        is_flog = j == flog_blk
        pl.when(is_rope)(functools.partial(tile, "rope"))
        pl.when(is_dv)(functools.partial(tile, "dv"))
        pl.when(is_flog)(functools.partial(tile, "flog"))
        pl.when(jnp.logical_not(jnp.logical_or(jnp.logical_or(is_rope, is_dv), is_flog)))(functools.partial(tile, "plain"))

    tab_spec = pl.BlockSpec((tm, LANES), lambda i, j: (i % s_blocks, 0))
    f_spec = pl.BlockSpec((tm, tn), lambda i, j: (i, jnp.clip(j - rope_lo, 0, dil_hi - rope_lo - 1)))
    row = pl.BlockSpec((tm, d), lambda i, j: (i, 0))
    return pl.pallas_call(
        body,
        out_shape=(jax.ShapeDtypeStruct((t, d), BF16), jax.ShapeDtypeStruct((t, n), BF16),
                   jax.ShapeDtypeStruct((t, 3 * DIL_W), F32), jax.ShapeDtypeStruct((t, LANES), F32)),
        grid=(t // tm, n // tn),
        in_specs=[row, pl.BlockSpec((1, d), lambda i, j: (0, 0)), pl.BlockSpec((d, tn), lambda i, j: (0, j)),
                  tab_spec, tab_spec, tab_spec],
        out_specs=(row, pl.BlockSpec((tm, tn), lambda i, j: (i, j)), f_spec, pl.BlockSpec((tm, LANES), lambda i, j: (i, 0))),
        scratch_shapes=[pltpu.VMEM((tm, d), BF16)],
        compiler_params=_cparams(dimension_semantics=("parallel", "arbitrary")),
        name=name,
    )(x, g, w, *tabs)


def _split3(x):
    hi = x.astype(BF16)
    r1 = x - hi.astype(F32)
    mid = r1.astype(BF16)
    lo = (r1 - mid.astype(F32)).astype(BF16)
    return hi, mid, lo


def _dot3(sel, x, sel_is_lhs):
    out = None
    for piece in _split3(x):
        t = jnp.dot(sel, piece, preferred_element_type=F32) if sel_is_lhs else jnp.dot(piece, sel, preferred_element_type=F32)
        out = t if out is None else out + t
    return out


def _flog_fwd(flog, bpad, *, nb, ts, name):
    ns = SEQ // ts

    def body(f_ref, b_ref, c_ref, carry_ref):
        s = pl.program_id(1)

        @pl.when(s == 0)
        def _():
            carry_ref[...] = jnp.zeros_like(carry_ref)

        z = f_ref[...] + b_ref[...]
        logf = jnp.minimum(z, 0.0) - jnp.log(1.0 + jnp.exp(-jnp.abs(z)))
        r = lax.broadcasted_iota(jnp.int32, (ts, ts), 0)
        c = lax.broadcasted_iota(jnp.int32, (ts, ts), 1)
        tri = jnp.where(r >= c, 1.0, 0.0).astype(BF16)
        cs = _dot3(tri, logf, True) + carry_ref[0:1, :]
        carry_ref[...] = jnp.broadcast_to(cs[ts - 1:ts, :], carry_ref.shape)
        c_ref[...] = cs

    return pl.pallas_call(
        body,
        out_shape=jax.ShapeDtypeStruct((nb * SEQ, LANES), F32),
        grid=(nb, ns),
        in_specs=[pl.BlockSpec((ts, LANES), lambda b, s: (b * ns + s, 0)), pl.BlockSpec((1, LANES), lambda b, s: (0, 0))],
        out_specs=pl.BlockSpec((ts, LANES), lambda b, s: (b * ns + s, 0)),
        scratch_shapes=[pltpu.VMEM((8, LANES), F32)],
        compiler_params=_cparams(dimension_semantics=("parallel", "arbitrary")),
        name=name,
    )(flog, bpad)


def _flog_bwd(dcol, flog, bpad, *, nb, ts, name):
    ns = SEQ // ts

    def body(d_ref, f_ref, b_ref, o_ref, gb_ref, carry_ref):
        bi = pl.program_id(0)
        s = pl.program_id(1)

        @pl.when(s == 0)
        def _():
            carry_ref[...] = jnp.zeros_like(carry_ref)

        @pl.when(jnp.logical_and(bi == 0, s == 0))
        def _():
            gb_ref[...] = jnp.zeros_like(gb_ref)

        r = lax.broadcasted_iota(jnp.int32, (ts, ts), 0)
        c = lax.broadcasted_iota(jnp.int32, (ts, ts), 1)
        tri = jnp.where(r <= c, 1.0, 0.0).astype(BF16)
        rc = _dot3(tri, d_ref[...], True) + carry_ref[0:1, :]
        carry_ref[...] = jnp.broadcast_to(rc[0:1, :], carry_ref.shape)
        z = f_ref[...] + b_ref[...]
        dz = rc / (1.0 + jnp.exp(z))
        o_ref[...] = dz.astype(o_ref.dtype)
        gb_ref[...] += jnp.broadcast_to(jnp.sum(dz, axis=0, keepdims=True), gb_ref.shape)

    rev = lambda b, s: (b * ns + (ns - 1 - s), 0)
    return pl.pallas_call(
        body,
        out_shape=(jax.ShapeDtypeStruct((nb * SEQ, LANES), BF16), jax.ShapeDtypeStruct((8, LANES), F32)),
        grid=(nb, ns),
        in_specs=[pl.BlockSpec((ts, LANES), rev), pl.BlockSpec((ts, LANES), rev), pl.BlockSpec((1, LANES), lambda b, s: (0, 0))],
        out_specs=(pl.BlockSpec((ts, LANES), rev), pl.BlockSpec((8, LANES), lambda b, s: (0, 0))),
        scratch_shapes=[pltpu.VMEM((8, LANES), F32)],
        compiler_params=_cparams(dimension_semantics=("arbitrary", "arbitrary")),
        name=name,
    )(dcol, flog, bpad)


MEM_TQ = 256
MEM_SET = 4
MEM_SCALE = 1.0 / math.sqrt(MEM_HEAD_DIM)
assert MEM_HEAD_DIM == LANES and SEQ % (MEM_TQ * MEM_SET) == 0


def _head_masks(nh):
    lane = lax.broadcasted_iota(jnp.int32, (1, LANES), 1)
    return [None] if nh == 1 else [lane < HEAD_DIM, lane >= HEAD_DIM]


def _mem_specs(qoff):
    qspec = pl.BlockSpec((None, SEQ, LANES), lambda b, j: (b, 0, qoff + j))
    kspec = pl.BlockSpec((None, MEM_LEN, LANES), lambda b, j: (b, 0, j))
    vspec = pl.BlockSpec((None, MEM_LEN, LANES), lambda b, j: (b, 0, MEM_HEADS + j))
    ospec = pl.BlockSpec((None, SEQ, LANES), lambda b, j: (b, 0, j))
    return qspec, kspec, vspec, ospec


def _mem_rows(g):
    return [pl.ds(pl.multiple_of((MEM_SET * g + a) * MEM_TQ, MEM_TQ), MEM_TQ) for a in range(MEM_SET)]


def _mem_fwd(p3, mkv3, *, qoff, name):
    nb = p3.shape[0]

    def body(q_ref, k_ref, v_ref, o_ref, lse_ref):
        kb, vb = k_ref[...], v_ref[...]

        def qset(g, c):
            rows = _mem_rows(g)
            ss = [lax.dot_general(q_ref[r, :] * MEM_SCALE, kb, _NT, preferred_element_type=F32) for r in rows]
            for r, s in zip(rows, ss):
                m = jnp.max(s, axis=1, keepdims=True)
                p = jnp.exp(s - m)
                l = jnp.sum(p, axis=1, keepdims=True)
                o_ref[r, :] = jnp.dot(p.astype(BF16), vb, preferred_element_type=F32) / l
                lse_ref[r, :] = jnp.broadcast_to(m + jnp.log(l), (MEM_TQ, LANES))
            return c

        lax.fori_loop(0, SEQ // MEM_TQ // MEM_SET, qset, 0)

    qspec, kspec, vspec, ospec = _mem_specs(qoff)
    osd = jax.ShapeDtypeStruct((nb, SEQ, MEM_W), F32)
    return pl.pallas_call(body, out_shape=(osd, osd), grid=(nb, MEM_HEADS), in_specs=[qspec, kspec, vspec],
                          out_specs=(ospec, ospec), compiler_params=_cparams(dimension_semantics=("parallel", "parallel")),
                          name=name)(p3, mkv3, mkv3)


def _mem_bwd(p3, mkv3, do, o, lse, *, qoff, do_off, name):
    nb = p3.shape[0]

    def body(q_ref, k_ref, v_ref, do_ref, o_ref, lse_ref, dq_ref, dk_ref, dv_ref):
        kb, vb = k_ref[...], v_ref[...]
        ks = kb * MEM_SCALE

        def qset(g, carry):
            dk, dv = carry
            work = []
            for r in _mem_rows(g):
                qs = q_ref[r, :] * MEM_SCALE
                dob = do_ref[r, :].astype(BF16)
                s = lax.dot_general(qs, kb, _NT, preferred_element_type=F32)
                dp = lax.dot_general(dob, vb, _NT, preferred_element_type=F32)
                work.append((r, qs, dob, s, dp))
            for r, qs, dob, s, dp in work:
                delta = jnp.sum(dob.astype(F32) * o_ref[r, :], axis=1, keepdims=True)
                p = jnp.exp(s - lse_ref[r, :][:, 0:1])
                ds = (p * (dp - delta)).astype(BF16)
                dq_ref[r, :] = jnp.dot(ds, ks, preferred_element_type=F32).astype(dq_ref.dtype)
                dk = dk + lax.dot_general(ds, qs, _T0, preferred_element_type=F32)
                dv = dv + lax.dot_general(p.astype(BF16), dob, _T0, preferred_element_type=F32)
            return dk, dv

        z = jnp.zeros((MEM_LEN, LANES), F32)
        dk, dv = lax.fori_loop(0, SEQ // MEM_TQ // MEM_SET, qset, (z, z))
        dk_ref[...] = dk
        dv_ref[...] = dv

    qspec, kspec, vspec, ospec = _mem_specs(qoff)
    dospec = pl.BlockSpec((None, SEQ, LANES), lambda b, j: (b, 0, do_off + j))
    kvo = pl.BlockSpec((None, MEM_LEN, LANES), lambda b, j: (b, 0, j))
    kvsd = jax.ShapeDtypeStruct((nb, MEM_LEN, MEM_W), F32)
    return pl.pallas_call(
        body, out_shape=(jax.ShapeDtypeStruct((nb, SEQ, MEM_W), BF16), kvsd, kvsd), grid=(nb, MEM_HEADS),
        in_specs=[qspec, kspec, vspec, dospec, ospec, ospec], out_specs=(ospec, kvo, kvo),
        compiler_params=_cparams(dimension_semantics=("parallel", "parallel")), name=name)(p3, mkv3, mkv3, do, o, lse)


BLK = 128
NBLK = SEQ // BLK
QK_SCALE = 1.0 / math.sqrt(HEAD_DIM)
DIL_STEPS = tuple(d for _, d in DILATIONS)
assert all(w // d == BLK for w, d in DILATIONS)
_T0 = (((0,), (0,)), ((), ()))
_NT = (((1,), (1,)), ((), ()))


def _stack_heads(a, masks):
    z = jnp.zeros_like(a)
    return jnp.concatenate([jnp.where(masks[0], a, z), jnp.where(masks[1], a, z)], axis=0)


def _tri_bias(lower):
    r = lax.broadcasted_iota(jnp.int32, (BLK, BLK), 0)
    c = lax.broadcasted_iota(jnp.int32, (BLK, BLK), 1)
    return jnp.where((c <= r) if lower else (c >= r), 0.0, NEG_INF).astype(F32)


def _dil_rows(r, i, d):
    start = r + i * (BLK * d)
    return pl.ds(start, BLK) if d == 1 else pl.ds(start, BLK, stride=d)


DIL_SET = 4


def _dil_sets(d, fn):
    nbk = SEQ // d // BLK
    if d == 1:
        n = 2 * DIL_SET
        def gbody(g, c):
            fn([(0, n * g + a, None if a == 0 else True) for a in range(n)])
            return c
        lax.fori_loop(0, nbk // n, gbody, 0)
    elif nbk > 1:
        assert nbk == DIL_SET
        def rbody(r, c):
            fn([(r, i, i > 0) for i in range(nbk)])
            return c
        lax.fori_loop(0, d, rbody, 0)
    else:
        def rbody(rr, c):
            fn([(DIL_SET * rr + a, 0, False) for a in range(DIL_SET)])
            return c
        lax.fori_loop(0, d // DIL_SET, rbody, 0)


def _dil_key_tiles(r, i, d, has_prev, qrows, tri_cur, tri_prev):
    tiles = [(qrows, tri_cur)]
    if has_prev is None:
        tiles.append((_dil_rows(r, jnp.maximum(i - 1, 0), d), tri_prev + jnp.where(i > 0, 0.0, NEG_INF)))
    elif has_prev:
        tiles.append((_dil_rows(r, i - 1, d), tri_prev))
    return tiles


def _dil_fwd(qkv, *, name):
    nb = qkv.shape[0]
    ncol = DIL_W // LANES
    hd = HEAD_DIM

    def body(q_ref, k_ref, v_ref, o_ref, lse_ref, m_ref, l_ref, a_ref):
        masks = _head_masks(2)
        tri_cur, tri_prev = _tri_bias(True), _tri_bias(False)
        for pi, d in enumerate(DIL_STEPS):
            first, last = pi == 0, pi == len(DIL_STEPS) - 1

            def qset(blocks, d=d, first=first, last=last):
                work = []
                for r, i, has_prev in blocks:
                    qrows = _dil_rows(r, i, d)
                    qcat = _stack_heads((q_ref[qrows, :] * QK_SCALE).astype(BF16), masks)
                    ss, krs = [], []
                    for krows, bias in _dil_key_tiles(r, i, d, has_prev, qrows, tri_cur, tri_prev):
                        s = lax.dot_general(qcat, k_ref[krows, :].astype(BF16), _NT, preferred_element_type=F32)
                        ss.append((s[:BLK] + bias, s[BLK:] + bias))
                        krs.append(krows)
                    work.append((qrows, ss, krs))
                for qrows, ss, krs in work:
                    e0 = ss[0][0] if len(ss) == 1 else jnp.maximum(ss[0][0], ss[1][0])
                    e1 = ss[0][1] if len(ss) == 1 else jnp.maximum(ss[0][1], ss[1][1])
                    n0 = jnp.max(e0, axis=1, keepdims=True)
                    n1 = jnp.max(e1, axis=1, keepdims=True)
                    if not first:
                        mo, lo = m_ref[qrows, :], l_ref[qrows, :]
                        m0, m1 = mo[:, 0:1], mo[:, hd:hd + 1]
                        n0, n1 = jnp.maximum(n0, m0), jnp.maximum(n1, m1)
                        a0, a1 = jnp.exp(m0 - n0), jnp.exp(m1 - n1)
                    ps = [(jnp.exp(s0 - n0), jnp.exp(s1 - n1)) for s0, s1 in ss]
                    t0 = ps[0][0] if len(ps) == 1 else ps[0][0] + ps[1][0]
                    t1 = ps[0][1] if len(ps) == 1 else ps[0][1] + ps[1][1]
                    l0 = jnp.sum(t0, axis=1, keepdims=True)
                    l1 = jnp.sum(t1, axis=1, keepdims=True)
                    acc = None
                    for (p0, p1), krows in zip(ps, krs):
                        vcat = _stack_heads(v_ref[krows, :].astype(BF16), masks)
                        pv = jnp.dot(jnp.concatenate([p0, p1], axis=1).astype(BF16), vcat, preferred_element_type=F32)
                        acc = pv if acc is None else acc + pv
                    if not first:
                        l0 = l0 + a0 * lo[:, 0:1]
                        l1 = l1 + a1 * lo[:, hd:hd + 1]
                        acc = acc + a_ref[qrows, :] * jnp.where(masks[0], a0, a1)
                    if last:
                        o_ref[qrows, :] = acc / jnp.where(masks[0], l0, l1)
                        lse_ref[qrows, :] = jnp.where(masks[0], n0 + jnp.log(l0), n1 + jnp.log(l1))
                    else:
                        m_ref[qrows, :] = jnp.where(masks[0], n0, n1)
                        l_ref[qrows, :] = jnp.where(masks[0], l0, l1)
                        a_ref[qrows, :] = acc

            _dil_sets(d, qset)

    spec = lambda off: pl.BlockSpec((None, SEQ, LANES), lambda b, j: (b, 0, off + j))
    ospec = pl.BlockSpec((None, SEQ, LANES), lambda b, j: (b, 0, j))
    osd = jax.ShapeDtypeStruct((nb, SEQ, DIL_W), F32)
    return pl.pallas_call(
        body, out_shape=(osd, osd), grid=(nb, ncol),
        in_specs=[spec(0), spec(ncol), spec(2 * ncol)], out_specs=(ospec, ospec),
        scratch_shapes=[pltpu.VMEM((SEQ, LANES), F32)] * 3,
        compiler_params=_cparams(dimension_semantics=("parallel", "parallel")), name=name,
    )(qkv, qkv, qkv)


def _dil_bwd(qkv, do, o, lse, tabs, *, do_off, name):
    nb = qkv.shape[0]
    ncol = DIL_W // LANES
    hd = HEAD_DIM

    def body(q_ref, k_ref, v_ref, do_ref, o_ref, lse_ref, c_ref, s1_ref, s2_ref, dqo_ref, dko_ref, dvo_ref,
             dq_ref, dk_ref, dv_ref, dl_ref, dof_ref):
        masks = _head_masks(2)
        tri_cur, tri_prev = _tri_bias(True), _tri_bias(False)
        dq_ref[...] = jnp.zeros_like(dq_ref)
        dk_ref[...] = jnp.zeros_like(dk_ref)
        dv_ref[...] = jnp.zeros_like(dv_ref)

        def delta_body(i, c):
            rows = pl.ds(pl.multiple_of(i * BLK, BLK), BLK)
            dof = do_ref[rows, :].astype(F32)
            dof_ref[rows, :] = dof
            prod = dof * o_ref[rows, :]
            z = jnp.zeros_like(prod)
            dl_ref[rows, :] = jnp.where(masks[0], jnp.sum(jnp.where(masks[0], prod, z), axis=1, keepdims=True),
                                        jnp.sum(jnp.where(masks[1], prod, z), axis=1, keepdims=True))
            return c

        lax.fori_loop(0, NBLK, delta_body, 0)

        for d in DIL_STEPS:
            def qset(blocks, d=d):
                work = []
                for r, i, has_prev in blocks:
                    qrows = _dil_rows(r, i, d)
                    qcat = _stack_heads((q_ref[qrows, :] * QK_SCALE).astype(BF16), masks)
                    docat = _stack_heads(dof_ref[qrows, :].astype(BF16), masks)
                    tiles = []
                    for krows, bias in _dil_key_tiles(r, i, d, has_prev, qrows, tri_cur, tri_prev):
                        s = lax.dot_general(qcat, k_ref[krows, :].astype(BF16), _NT, preferred_element_type=F32)
                        dp = lax.dot_general(docat, v_ref[krows, :].astype(BF16), _NT, preferred_element_type=F32)
                        tiles.append((krows, s, dp, bias))
                    work.append((qrows, qcat, docat, tiles))
                for qrows, qcat, docat, tiles in work:
                    lseb, dlb = lse_ref[qrows, :], dl_ref[qrows, :]
                    lse0, lse1 = lseb[:, 0:1], lseb[:, hd:hd + 1]
                    dl0, dl1 = dlb[:, 0:1], dlb[:, hd:hd + 1]
                    dq = None
                    for krows, s, dp, bias in tiles:
                        p0 = jnp.exp(s[:BLK] + bias - lse0)
                        p1 = jnp.exp(s[BLK:] + bias - lse1)
                        ds0 = p0 * (dp[:BLK] - dl0)
                        ds1 = p1 * (dp[BLK:] - dl1)
                        ds0b, ds1b = ds0.astype(BF16), ds1.astype(BF16)
                        pcat = jnp.concatenate([p0.astype(BF16), p1.astype(BF16)], axis=0)
                        dscat = jnp.concatenate([ds0b, ds1b], axis=0)
                        dv_ref[krows, :] += lax.dot_general(pcat, docat, _T0, preferred_element_type=F32)
                        dk_ref[krows, :] += lax.dot_general(dscat, qcat, _T0, preferred_element_type=F32)
                        dsrow = jnp.concatenate([ds0b, ds1b], axis=1)
                        kcat = _stack_heads((k_ref[krows, :] * QK_SCALE).astype(BF16), masks)
                        t = jnp.dot(dsrow, kcat, preferred_element_type=F32)
                        dq = t if dq is None else dq + t
                    dq_ref[qrows, :] += dq

            _dil_sets(d, qset)

        def out_body(i, c):
            rows = pl.ds(pl.multiple_of(i * BLK, BLK), BLK)
            tab = (c_ref[rows, :], s1_ref[rows, :], s2_ref[rows, :])
            dqo_ref[rows, :] = _rope_apply(dq_ref[rows, :], *tab, transpose=True).astype(dqo_ref.dtype)
            dko_ref[rows, :] = _rope_apply(dk_ref[rows, :], *tab, transpose=True).astype(dko_ref.dtype)
            dvo_ref[rows, :] = dv_ref[rows, :].astype(dvo_ref.dtype)
            return c

        lax.fori_loop(0, NBLK, out_body, 0)

    spec = lambda off: pl.BlockSpec((None, SEQ, LANES), lambda b, j: (b, 0, off + j))
    ospec = pl.BlockSpec((None, SEQ, LANES), lambda b, j: (b, 0, j))
    tspec = pl.BlockSpec((SEQ, LANES), lambda b, j: (0, 0))
    osd = jax.ShapeDtypeStruct((nb, SEQ, DIL_W), BF16)
    return pl.pallas_call(
        body, out_shape=(osd, osd, osd), grid=(nb, ncol),
        in_specs=[spec(0), spec(ncol), spec(2 * ncol), spec(do_off), ospec, ospec, tspec, tspec, tspec],
        out_specs=(ospec, ospec, ospec),
        scratch_shapes=[pltpu.VMEM((SEQ, LANES), F32)] * 5,
        compiler_params=_cparams(dimension_semantics=("parallel", "parallel")), name=name,
    )(qkv, qkv, qkv, do, o, lse, *tabs)


FOX_GROUP = 4
assert NBLK % FOX_GROUP == 0
_FOX_COLS = tuple(c // LANES for c in (C_FQ, C_FK, C_FV))


def _fox_specs():
    cols = [pl.BlockSpec((None, SEQ, LANES), (lambda b, j, off=off: (b, 0, off + j))) for off in _FOX_COLS]
    ospec = pl.BlockSpec((None, SEQ, LANES), lambda b, j: (b, 0, j))
    crspec = pl.BlockSpec((None, None, NBLK, 8, BLK), lambda b, j: (b, j, 0, 0, 0))
    return cols, ospec, crspec


def _fox_key_rows(t, e):
    return pl.ds(pl.multiple_of((FOX_GROUP * t + e) * BLK, BLK), BLK)


def _fox_fwd(p3, crow, *, name):
    nb = p3.shape[0]
    g = FOX_GROUP

    def body(q_ref, k_ref, v_ref, cr_ref, o_ref, lse_ref):
        masks = _head_masks(2)
        tri = _tri_bias(True)

        def qk(qcat, t):
            return tuple(lax.dot_general(qcat, k_ref[_fox_key_rows(t, e), :], _NT, preferred_element_type=F32) for e in range(g))

        def consume(ss, t, state, nblk, diag):
            m0, m1, l0, l1, acc = state
            us = []
            for e in range(nblk):
                cr = cr_ref[g * t + e]
                u0 = ss[e][:BLK] - cr[0:1, :]
                u1 = ss[e][BLK:] - cr[1:2, :]
                if diag and e == nblk - 1:
                    u0, u1 = u0 + tri, u1 + tri
                us.append((u0, u1))
            x0 = functools.reduce(jnp.maximum, [u[0] for u in us])
            x1 = functools.reduce(jnp.maximum, [u[1] for u in us])
            n0 = jnp.maximum(m0, jnp.max(x0, axis=1, keepdims=True))
            n1 = jnp.maximum(m1, jnp.max(x1, axis=1, keepdims=True))
            a0, a1 = jnp.exp(m0 - n0), jnp.exp(m1 - n1)
            acc = acc * jnp.where(masks[0], a0, a1)
            t0 = t1 = None
            for e in range(nblk):
                p0, p1 = jnp.exp(us[e][0] - n0), jnp.exp(us[e][1] - n1)
                t0 = p0 if t0 is None else t0 + p0
                t1 = p1 if t1 is None else t1 + p1
                pcat = jnp.concatenate([p0, p1], axis=1)
                hi = pcat.astype(BF16)
                lo = (pcat - hi.astype(F32)).astype(BF16)
                vcat = _stack_heads(v_ref[_fox_key_rows(t, e), :], masks)
                acc = acc + jnp.dot(hi, vcat, preferred_element_type=F32) + jnp.dot(lo, vcat, preferred_element_type=F32)
            l0 = a0 * l0 + jnp.sum(t0, axis=1, keepdims=True)
            l1 = a1 * l1 + jnp.sum(t1, axis=1, keepdims=True)
            return n0, n1, l0, l1, acc

        def gbody(ng, c):
            neg = jnp.full((BLK, 1), NEG_INF, F32)
            z1 = jnp.zeros((BLK, 1), F32)
            rows = [pl.ds(pl.multiple_of((g * ng + a) * BLK, BLK), BLK) for a in range(g)]
            qcats = [_stack_heads(q_ref[rows[a], :] * QK_SCALE, masks) for a in range(g)]
            first = [qk(qcats[a], 0) for a in range(g)]
            def step(t, cc):
                nxt = [qk(qcats[a], t + 1) for a in range(g)]
                return tuple((nxt[a], consume(cc[a][0], t, cc[a][1], g, False)) for a in range(g))

            init = (neg, neg, z1, z1, jnp.zeros((BLK, LANES), F32))
            done = lax.fori_loop(0, ng, step, tuple((first[a], init) for a in range(g)))
            for a in range(g):
                ss, state = done[a]
                m0, m1, l0, l1, acc = consume(ss, ng, state, a + 1, True)
                o_ref[rows[a], :] = acc / jnp.where(masks[0], l0, l1)
                lse_ref[rows[a], :] = jnp.where(masks[0], m0 + jnp.log(l0), m1 + jnp.log(l1))
            return c

        lax.fori_loop(0, NBLK // g, gbody, 0)

    cols, ospec, crspec = _fox_specs()
    osd = jax.ShapeDtypeStruct((nb, SEQ, FOX_W), F32)
    return pl.pallas_call(
        body, out_shape=(osd, osd), grid=(nb, FOX_W // LANES), in_specs=cols + [crspec], out_specs=(ospec, ospec),
        compiler_params=_cparams(dimension_semantics=("parallel", "parallel")), name=name,
    )(p3, p3, p3, crow)


def _fox_bwd(p3, crow, do, o, lse, *, do_off, name):
    nb = p3.shape[0]
    g = FOX_GROUP
    hd = HEAD_DIM

    def body(q_ref, k_ref, v_ref, cr_ref, do_ref, o_ref, lse_ref, dq_ref, dko_ref, dvo_ref, dcr_ref, dk_ref, dv_ref):
        masks = _head_masks(2)
        tri = _tri_bias(True)
        dk_ref[...] = jnp.zeros_like(dk_ref)
        dv_ref[...] = jnp.zeros_like(dv_ref)
        dcr_ref[...] = jnp.zeros_like(dcr_ref)

        def products(qcat, docat, t):
            out = []
            for e in range(g):
                krows = _fox_key_rows(t, e)
                out.append(lax.dot_general(qcat, k_ref[krows, :], _NT, preferred_element_type=F32))
                out.append(lax.dot_general(docat, v_ref[krows, :], _NT, preferred_element_type=F32))
            return tuple(out)

        def consume(prod, t, ctx, dq, nblk, diag):
            qcat, docat, lse0, lse1, dl0, dl1 = ctx
            for e in range(nblk):
                jb = g * t + e
                krows = _fox_key_rows(t, e)
                s, dp = prod[2 * e], prod[2 * e + 1]
                cr = cr_ref[jb]
                u0 = s[:BLK] - cr[0:1, :]
                u1 = s[BLK:] - cr[1:2, :]
                if diag and e == nblk - 1:
                    u0, u1 = u0 + tri, u1 + tri
                p0 = jnp.exp(u0 - lse0)
                p1 = jnp.exp(u1 - lse1)
                ds0 = p0 * (dp[:BLK] - dl0)
                ds1 = p1 * (dp[BLK:] - dl1)
                dcr_ref[jb, 0:1, :] += jnp.sum(ds0, axis=0, keepdims=True)
                dcr_ref[jb, 1:2, :] += jnp.sum(ds1, axis=0, keepdims=True)
                ds0b, ds1b = ds0.astype(BF16), ds1.astype(BF16)
                pcat = jnp.concatenate([p0.astype(BF16), p1.astype(BF16)], axis=0)
                dscat = jnp.concatenate([ds0b, ds1b], axis=0)
                dv_ref[krows, :] += lax.dot_general(pcat, docat, _T0, preferred_element_type=F32)
                dk_ref[krows, :] += lax.dot_general(dscat, qcat, _T0, preferred_element_type=F32)
                dsrow = jnp.concatenate([ds0b, ds1b], axis=1)
                dq = dq + jnp.dot(dsrow, _stack_heads(k_ref[krows, :] * QK_SCALE, masks), preferred_element_type=F32)
            return dq

        def gbody(ng, c):
            ctxs, rows = [], []
            for a in range(g):
                r = pl.ds(pl.multiple_of((g * ng + a) * BLK, BLK), BLK)
                qcat = _stack_heads(q_ref[r, :] * QK_SCALE, masks)
                dob = do_ref[r, :].astype(BF16)
                prod = dob.astype(F32) * o_ref[r, :]
                z = jnp.zeros_like(prod)
                dl0 = jnp.sum(jnp.where(masks[0], prod, z), axis=1, keepdims=True)
                dl1 = jnp.sum(jnp.where(masks[1], prod, z), axis=1, keepdims=True)
                lseb = lse_ref[r, :]
                ctxs.append((qcat, _stack_heads(dob, masks), lseb[:, 0:1], lseb[:, hd:hd + 1], dl0, dl1))
                rows.append(r)
            first = [products(ctxs[a][0], ctxs[a][1], 0) for a in range(g)]
            def step(t, cc):
                nxt = [products(ctxs[a][0], ctxs[a][1], t + 1) for a in range(g)]
                return tuple((nxt[a], consume(cc[a][0], t, ctxs[a], cc[a][1], g, False)) for a in range(g))

            done = lax.fori_loop(0, ng, step, tuple((first[a], jnp.zeros((BLK, LANES), F32)) for a in range(g)))
            for a in range(g):
                pr, dq = done[a]
                dq_ref[rows[a], :] = consume(pr, ng, ctxs[a], dq, a + 1, True).astype(dq_ref.dtype)
            return c

        lax.fori_loop(0, NBLK // g, gbody, 0)
        dko_ref[...] = dk_ref[...].astype(dko_ref.dtype)
        dvo_ref[...] = dv_ref[...].astype(dvo_ref.dtype)

    cols, ospec, crspec = _fox_specs()
    dospec = pl.BlockSpec((None, SEQ, LANES), lambda b, j: (b, 0, do_off + j))
    osd = jax.ShapeDtypeStruct((nb, SEQ, FOX_W), BF16)
    return pl.pallas_call(
        body, out_shape=(osd, osd, osd, jax.ShapeDtypeStruct((nb, FOX_W // LANES, NBLK, 8, BLK), F32)),
        grid=(nb, FOX_W // LANES), in_specs=cols + [crspec, dospec, ospec, ospec], out_specs=(ospec, ospec, ospec, crspec),
        scratch_shapes=[pltpu.VMEM((SEQ, LANES), F32)] * 2,
        compiler_params=_cparams(dimension_semantics=("parallel", "parallel")), name=name,
    )(p3, p3, p3, crow, do, o, lse)


_B1, _B2 = FOX_W // LANES, (FOX_W + DIL_W) // LANES


def _dy_gate_bwd(dx2b, wo, fox, dil, memo, p16, *, tm, tn, name):
    t, d = dx2b.shape
    assert FOX_W % tn == 0 and DIL_W % tn == 0 and MEM_W % tn == 0 and all(c % tn == 0 for c in (C_FG, C_DG, C_MG))
    n1, n2, n3 = FOX_W // tn, (FOX_W + DIL_W) // tn, MIX_W // tn

    def body(dx_ref, w_ref, f_ref, d_ref, m_ref, g_ref, da_ref, dg_ref):
        j = pl.program_id(1)
        wv = w_ref[...]
        for c0 in range(0, tm, min(tm, 2 * MM_CHUNK)):
            rows = pl.ds(c0, min(tm, 2 * MM_CHUNK))
            dyv = lax.dot_general(dx_ref[rows, :], wv, _NT, preferred_element_type=F32)
            a = jnp.where(j < n1, f_ref[rows, :], jnp.where(j < n2, d_ref[rows, :], m_ref[rows, :]))
            gt = g_ref[rows, :].astype(F32)
            sg = 1.0 / (1.0 + jnp.exp(-gt))
            da_ref[rows, :] = (dyv * gt * sg).astype(da_ref.dtype)
            dg_ref[rows, :] = (dyv * a * sg * (1.0 + gt * (1.0 - sg))).astype(dg_ref.dtype)

    def gcol(j):
        return jnp.where(j < n1, C_FG // tn + j, jnp.where(j < n2, C_DG // tn + j - n1, C_MG // tn + j - n2))

    tile = pl.BlockSpec((tm, tn), lambda i, j: (i, j))
    return pl.pallas_call(
        body,
        out_shape=(jax.ShapeDtypeStruct((t, MIX_W), BF16), jax.ShapeDtypeStruct((t, MIX_W), BF16)),
        grid=(t // tm, n3),
        in_specs=[pl.BlockSpec((tm, d), lambda i, j: (i, 0)), pl.BlockSpec((tn, d), lambda i, j: (j, 0)),
                  pl.BlockSpec((tm, tn), lambda i, j: (i, jnp.minimum(j, n1 - 1))),
                  pl.BlockSpec((tm, tn), lambda i, j: (i, jnp.clip(j - n1, 0, n2 - n1 - 1))),
                  pl.BlockSpec((tm, tn), lambda i, j: (i, jnp.clip(j - n2, 0, n3 - n2 - 1))),
                  pl.BlockSpec((tm, tn), lambda i, j: (i, gcol(j)))],
        out_specs=(tile, tile),
        compiler_params=_cparams(dimension_semantics=("parallel", "parallel")),
        name=name,
    )(dx2b, wo, fox, dil, memo, p16)


def _silu(g):
    return g / (1.0 + jnp.exp(-g))


def _out_loss(fox, dil, memo, p16, wo, x, tgt, gfin, *, tm, name):
    t, d = x.shape
    n_feat = float(d)

    def body(f_ref, d_ref, m_ref, fg_ref, dg_ref, mg_ref, w_ref, x_ref, t_ref, g_ref, y_ref, dx_ref, dxb_ref, st_ref):
        i = pl.program_id(0)

        @pl.when(i == 0)
        def _():
            st_ref[...] = jnp.zeros_like(st_ref)

        wv, gv = w_ref[...], g_ref[...]
        half = tm // 2
        for c0 in (0, half):
            rows = pl.ds(c0, half)
            y = jnp.concatenate([(a_ref[rows, :] * _silu(gt_ref[rows, :].astype(F32))).astype(BF16)
                                 for a_ref, gt_ref in ((f_ref, fg_ref), (d_ref, dg_ref), (m_ref, mg_ref))], axis=1)
            y_ref[rows, :] = y
            x2 = x_ref[rows, :] + jnp.dot(y, wv, preferred_element_type=F32)
            r = lax.rsqrt(jnp.mean(x2 * x2, axis=-1, keepdims=True) + RMS_EPS)
            nrm = x2 * r
            err = nrm * gv - t_ref[rows, :]
            dout = err * (1.0 / n_feat)
            dn = dout * gv
            dx2 = r * (dn - nrm * jnp.mean(dn * nrm, axis=-1, keepdims=True))
            dx_ref[rows, :] = dx2
            dxb_ref[rows, :] = dx2.astype(dxb_ref.dtype)
            st_ref[0:1, :] += jnp.sum(dout * nrm, axis=0, keepdims=True)
            st_ref[1:2, :] += (0.5 / n_feat) * jnp.sum(err * err, axis=0, keepdims=True)

    row = pl.BlockSpec((tm, d), lambda i: (i, 0))
    whole = lambda w: pl.BlockSpec((tm, w), lambda i: (i, 0))
    gate = lambda w, col: pl.BlockSpec((tm, w), lambda i: (i, col // w))
    return pl.pallas_call(
        body,
        out_shape=(jax.ShapeDtypeStruct((t, MIX_W), BF16), jax.ShapeDtypeStruct((t, d), F32), jax.ShapeDtypeStruct((t, d), BF16),
                   jax.ShapeDtypeStruct((8, d), F32)),
        grid=(t // tm,),
        in_specs=[whole(FOX_W), whole(DIL_W), whole(MEM_W), gate(FOX_W, C_FG), gate(DIL_W, C_DG), gate(MEM_W, C_MG),
                  pl.BlockSpec((MIX_W, d), lambda i: (0, 0)), row, row, pl.BlockSpec((1, d), lambda i: (0, 0))],
        out_specs=(pl.BlockSpec((tm, MIX_W), lambda i: (i, 0)), row, row, pl.BlockSpec((8, d), lambda i: (0, 0))),
        compiler_params=_cparams(dimension_semantics=("arbitrary",)),
        name=name,
    )(fox, dil, memo, p16, p16, p16, wo, x, tgt, gfin)


def _dh_rms_bwd(dp, w, x, g, resid, *, tm, name):
    t, d = x.shape
    kdim = dp.shape[1]

    def body(*refs):
        if resid is not None:
            dp_ref, w_ref, x_ref, g_ref, r_ref, dx_ref, gg_ref = refs
        else:
            dp_ref, w_ref, x_ref, g_ref, dx_ref, gg_ref = refs

        @pl.when(pl.program_id(0) == 0)
        def _():
            gg_ref[...] = jnp.zeros_like(gg_ref)

        dh = lax.dot_general(dp_ref[...], w_ref[...], _NT, preferred_element_type=F32)
        xv = x_ref[...]
        r = lax.rsqrt(jnp.mean(xv * xv, axis=-1, keepdims=True) + RMS_EPS)
        nrm = xv * r
        dn = dh * g_ref[...]
        dx = r * (dn - nrm * jnp.mean(dn * nrm, axis=-1, keepdims=True))
        if resid is not None:
            dx = dx + r_ref[...]
        dx_ref[...] = dx
        gg_ref[0:1, :] += jnp.sum(dh * nrm, axis=0, keepdims=True)

    row = pl.BlockSpec((tm, d), lambda i: (i, 0))
    in_specs = [pl.BlockSpec((tm, kdim), lambda i: (i, 0)),
                pl.BlockSpec((d, kdim), lambda i: (0, 0), pipeline_mode=pl.Buffered(1)), row,
                pl.BlockSpec((1, d), lambda i: (0, 0))]
    args = [dp, w, x, g]
    if resid is not None:
        in_specs.append(row)
        args.append(resid)
    return pl.pallas_call(
        body,
        out_shape=(jax.ShapeDtypeStruct((t, d), F32), jax.ShapeDtypeStruct((8, d), F32)),
        grid=(t // tm,),
        in_specs=in_specs,
        out_specs=(row, pl.BlockSpec((8, d), lambda i: (0, 0))),
        compiler_params=_cparams(dimension_semantics=("arbitrary",)),
        name=name,
    )(*args)


_FLOG0 = 4 * FOX_W
_W_IN_SEGMENTS = ((0, _FLOG0, 0), (_FLOG0, _FLOG0 + FOX_HEADS, PW), (_FLOG0 + FOX_HEADS, IN_W, C_DQ))
SHARD_W = IN_W // N_CHIPS


def _rearrange_w_in(shards):
    def cols(lo, hi):
        parts = []
        for k in range(N_CHIPS):
            a, b = max(lo, k * SHARD_W), min(hi, (k + 1) * SHARD_W)
            if a < b:
                parts.append(shards[k][:, a - k * SHARD_W:b - k * SHARD_W])
        return parts

    (a0, a1, _), (f0, f1, _), (b0, b1, _) = _W_IN_SEGMENTS
    pad = jnp.zeros((shards[0].shape[0], PWF - PW - FOX_HEADS), shards[0].dtype)
    return jnp.concatenate(cols(a0, a1) + cols(b0, b1) + cols(f0, f1) + [pad], axis=1)


def _w_in_grad_slabs(g):
    slabs = []
    for k in range(N_CHIPS):
        parts = []
        for lo, hi, at in _W_IN_SEGMENTS:
            a, b = max(lo, k * SHARD_W), min(hi, (k + 1) * SHARD_W)
            if a < b:
                parts.append(g[:, at + a - lo:at + b - lo])
        slabs.append(jnp.concatenate(parts, axis=1))
    return jnp.stack(slabs, axis=0)


def _local_grads(x, mem, norm_g, w_r, b_forget, mem_norm_g, w_kv, w_o, final_norm_g, tgt, start_reduce=None,
                 start_reduce_small=None, early_token=None, late_weights=None):
    nb = x.shape[0]
    t = nb * SEQ
    x2d = x.reshape(t, D_MODEL)
    tgt2d = tgt.reshape(t, D_MODEL)
    tabs = _rope_tables()
    bpad = jnp.pad(b_forget.reshape(1, FOX_HEADS), ((0, 0), (0, LANES - FOX_HEADS)))

    gain0 = norm_g.reshape(1, D_MODEL)
    if early_token is not None:
        gain0 = gain0 + early_token[0:1, 0:1]
    h, p16, dqkv, flog = _proj(x2d, gain0, w_r, tabs, n=PWF, tm=1024, tn=768, name="proj")
    c12 = _flog_fwd(flog, bpad, nb=nb, ts=256, name="flog_fwd")

    crow = c12[:, :FOX_HEADS].reshape(nb, NBLK, BLK, FOX_HEADS // 2, 2).transpose(0, 3, 1, 4, 2)
    crow = jnp.pad(crow, ((0, 0), (0, 0), (0, 0), (0, 6), (0, 0)))
    p3 = p16.reshape(nb, SEQ, PWF)
    fox, fox_lse = _fox_fwd(p3, crow, name="fox_fwd")
    if late_weights is not None:
        w_kv, w_o = late_weights(fox_lse)

    dqkv3 = dqkv.reshape(nb, SEQ, 3 * DIL_W)
    dil, dil_lse = _dil_fwd(dqkv3, name="dil_fwd")

    mh = _rms_fwd(mem.reshape(nb * MEM_LEN, D_MODEL), mem_norm_g.reshape(1, D_MODEL), tm=nb * MEM_LEN, name="rms_mem")
    mkv = _matmul(mh, w_kv, out_dtype=BF16, tm=nb * MEM_LEN, tn=512, tk=D_MODEL, name="mem_kv")
    mkv3 = mkv.reshape(nb, MEM_LEN, 2 * MEM_W)
    memo, mem_lse = _mem_fwd(p3, mkv3, qoff=C_MQ // LANES, name="mem_fwd")

    fox2, dil2, memo2 = fox.reshape(t, FOX_W), dil.reshape(t, DIL_W), memo.reshape(t, MEM_W)
    y, dx2, dx2b, st = _out_loss(fox2, dil2, memo2, p16, w_o, x2d, tgt2d, final_norm_g.reshape(1, D_MODEL), tm=256,
                                 name="out_loss")

    g_wo = _matmul(y, dx2b, mode="tn", out_dtype=BF16, tm=1024, tn=512, tk=t, name="grad_w_out")
    datt, dgate = _dy_gate_bwd(dx2b, w_o, fox2, dil2, memo2, p16, tm=2048, tn=256, name="dy_gate_bwd")
    datt3 = datt.reshape(nb, SEQ, MIX_W)

    dmq, dmk, dmv = _mem_bwd(p3, mkv3, datt3, memo, mem_lse, qoff=C_MQ // LANES, do_off=_B2, name="mem_bwd")
    dmkv = jnp.concatenate([dmk, dmv], axis=-1).reshape(nb * MEM_LEN, 2 * MEM_W).astype(BF16)
    g_wkv = _matmul(mh, dmkv, mode="tn", out_dtype=BF16, tm=512, tn=512, tk=nb * MEM_LEN, name="grad_w_kv")
    mem_gain = mem_norm_g.reshape(1, D_MODEL)
    if start_reduce_small is not None:
        tok = start_reduce_small(g_wkv, g_wo)[0:1, 0:1]
        mem_gain, crow = mem_gain + tok, crow + tok
    _, gmn = _dh_rms_bwd(dmkv, w_kv, mem.reshape(nb * MEM_LEN, D_MODEL), mem_gain, None, tm=nb * MEM_LEN, name="mem_rms_bwd")

    dfq, dfk, dfv, dcr = _fox_bwd(p3, crow, datt3, fox, fox_lse, do_off=0, name="fox_bwd")
    dcol = -dcr[:, :, :, :2, :].transpose(0, 2, 4, 1, 3).reshape(t, FOX_HEADS)
    dcol = jnp.pad(dcol, ((0, 0), (0, LANES - FOX_HEADS)))
    dflog, gb = _flog_bwd(dcol, flog, bpad, nb=nb, ts=256, name="flog_bwd")

    ddq, ddk, ddv = _dil_bwd(dqkv3, datt3, dil, dil_lse, tabs, do_off=_B1, name="dil_bwd")

    flat = lambda a: a.reshape(t, -1)
    dp = jnp.concatenate([flat(dfq), flat(dfk), flat(dfv), dgate[:, :FOX_W], flat(ddq), flat(ddk), flat(ddv),
                          dgate[:, FOX_W:FOX_W + DIL_W], flat(dmq), dgate[:, FOX_W + DIL_W:], dflog,
                          jnp.zeros((t, PWF - PW - LANES), BF16)], axis=1)
    g_wr = _matmul(h, dp, mode="tn", out_dtype=BF16, tm=D_MODEL, tn=768, tk=t, name="grad_w_in")
    gain = norm_g.reshape(1, D_MODEL)
    if start_reduce is not None:
        gain = gain + start_reduce(g_wr)[0:1, 0:1]
    gx, gng = _dh_rms_bwd(dp, w_r, x2d, gain, dx2, tm=256, name="in_rms_bwd")

    gb_row = jnp.pad(gb[0:1, :], ((0, 0), (0, D_MODEL - LANES)))
    small = jnp.concatenate([gng[0:1], gmn[0:1], st[0:1], gb_row, st[1:2], jnp.zeros((3, D_MODEL), F32)], axis=0)
    return gx.reshape(nb, SEQ, D_MODEL), g_wr, g_wkv, g_wo, small


MESH = pl.DeviceIdType.MESH
ANY = pl.BlockSpec(memory_space=pl.ANY)


def _place():
    x, y, c = lax.axis_index("x"), lax.axis_index("y"), lax.axis_index("c")
    other_chips = [(1 - x, y), (x, 1 - y), (1 - x, 1 - y)]
    return x, y, c, other_chips


def _gather_weights(shards):
    n = len(shards)

    def body(*refs):
        in_refs, out_refs = refs[:n], refs[n:2 * n]
        send_sems, recv_sems = refs[2 * n:]
        x, y, c, chips = _place()
        me_chip = 2 * x + y
        sibling = (x, y, 1 - c)

        def half(ref, pc, rows):
            return ref.at[pl.ds(pc * (rows // 2), rows // 2), :]

        def rcopy(k, src, dst, to):
            return pltpu.make_async_remote_copy(src_ref=src, dst_ref=dst, send_sem=send_sems.at[k], recv_sem=recv_sems.at[k],
                                                device_id=to, device_id_type=MESH)

        sends = []
        for t in range(n):
            rows = shards[t].shape[0]
            for j, chip in enumerate(chips):
                cp = rcopy(6 * t + j, half(in_refs[t], c, rows), half(out_refs[t].at[me_chip], c, rows), (*chip, c))
                cp.start()
                sends.append(cp)
        for t in range(n):
            rows = shards[t].shape[0]
            for j, chip in enumerate(chips):
                slot = out_refs[t].at[2 * chip[0] + chip[1]]
                rcopy(6 * t + j, half(slot, c, rows), half(slot, c, rows), sibling).wait_recv()
                fw = rcopy(6 * t + 3 + j, half(slot, c, rows), half(slot, c, rows), sibling)
                fw.start()
                sends.append(fw)
        for t in range(n):
            rows = shards[t].shape[0]
            for j, chip in enumerate(chips):
                slot = out_refs[t].at[2 * chip[0] + chip[1]]
                rcopy(6 * t + 3 + j, half(slot, 1 - c, rows), half(slot, 1 - c, rows), sibling).wait_recv()
        for cp in sends:
            cp.wait_send()

    return pl.pallas_call(
        body,
        out_shape=tuple(jax.ShapeDtypeStruct((N_CHIPS,) + s.shape, s.dtype) for s in shards),
        in_specs=[ANY] * n,
        out_specs=tuple([ANY] * n),
        scratch_shapes=[pltpu.SemaphoreType.DMA((6 * n,)), pltpu.SemaphoreType.DMA((6 * n,))],
        name="gather_weights",
    )(*shards)


def _pair_exchange(gs, *, name):
    n = len(gs)

    def body(*refs):
        g_refs, r_refs = refs[:n], refs[n:2 * n]
        send_sems, recv_sems = refs[2 * n:]
        x, y, c, _ = _place()
        cps = []
        for t in range(n):
            hr = gs[t].shape[1] // 2
            cp = pltpu.make_async_remote_copy(src_ref=g_refs[t].at[:, pl.ds((1 - c) * hr, hr), :], dst_ref=r_refs[t],
                                              send_sem=send_sems.at[t], recv_sem=recv_sems.at[t],
                                              device_id=(x, y, 1 - c), device_id_type=MESH)
            cp.start()
            cps.append(cp)
        for cp in cps:
            cp.wait()

    return pl.pallas_call(
        body,
        out_shape=tuple(jax.ShapeDtypeStruct((g.shape[0], g.shape[1] // 2, g.shape[2]), g.dtype) for g in gs),
        in_specs=[ANY] * n,
        out_specs=tuple([ANY] * n),
        scratch_shapes=[pltpu.SemaphoreType.DMA((n,)), pltpu.SemaphoreType.DMA((n,))],
        name=name,
    )(*gs)


_HBM = pl.BlockSpec(memory_space=pltpu.HBM)
_SEM = pl.BlockSpec(memory_space=pltpu.SEMAPHORE)
_DATAFLOW = pltpu.SideEffectType.DATAFLOW_SIDE_EFFECTING


def _chip_copies(p_refs, land_refs, send_sems, recv_sems):
    x, y, c, chips = _place()
    me_chip = 2 * x + y
    return [pltpu.make_async_remote_copy(src_ref=p_refs[t].at[2 * chip[0] + chip[1]], dst_ref=land_refs[t].at[me_chip],
                                         send_sem=send_sems.at[3 * t + j], recv_sem=recv_sems.at[3 * t + j],
                                         device_id=(*chip, c), device_id_type=MESH)
            for t in range(len(p_refs)) for j, chip in enumerate(chips)]


def _chip_exchange_start(ps, *, tag):
    n = len(ps)

    def body(*refs):
        p_refs, land_refs = refs[:n], refs[n:2 * n]
        send_sems, recv_sems = refs[2 * n:2 * n + 2]
        token = refs[-1]
        for cp in _chip_copies(p_refs, land_refs, send_sems, recv_sems):
            cp.start()
        token[...] = jnp.zeros_like(token)

    hbm = [pltpu.HBM(p.shape, p.dtype) for p in ps]
    args = [pltpu.with_memory_space_constraint(p, pltpu.HBM) for p in ps]
    args += [pltpu.with_memory_space_constraint(lax.empty(p.shape, p.dtype), pltpu.HBM) for p in ps]
    out = pl.pallas_call(
        body,
        name=f"chip_exchange_start_{tag}",
        out_shape=(pltpu.SemaphoreType.DMA((3 * n,)), pltpu.SemaphoreType.DMA((3 * n,)), *hbm, *hbm,
                   jax.ShapeDtypeStruct((8, LANES), F32)),
        in_specs=[_HBM] * (2 * n),
        out_specs=(_SEM, _SEM, *([_HBM] * (2 * n)), pl.BlockSpec(memory_space=pltpu.VMEM)),
        input_output_aliases={i: 2 + i for i in range(2 * n)},
        compiler_params=pltpu.CompilerParams(has_side_effects=_DATAFLOW),
    )(*args)
    return out[0], out[1], out[2:2 + n], out[2 + n:2 + 2 * n], out[-1]


def _chip_exchange_wait(send_sems, recv_sems, p_thru, land_thru, after, *, tag):
    n = len(p_thru)

    def body(*refs):
        p_refs, land_refs = refs[:n], refs[n:2 * n]
        ssem, rsem = refs[2 * n:2 * n + 2]
        for cp in _chip_copies(p_refs, land_refs, ssem, rsem):
            cp.wait_send()
            cp.wait_recv()

    hbm = [pltpu.HBM(p.shape, p.dtype) for p in p_thru]
    out = pl.pallas_call(
        body,
        name=f"chip_exchange_wait_{tag}",
        out_shape=(*hbm, *hbm),
        in_specs=[_HBM] * (2 * n) + [_SEM, _SEM, ANY],
        out_specs=tuple([_HBM] * (2 * n)),
        input_output_aliases={i: i for i in range(2 * n)},
        compiler_params=pltpu.CompilerParams(has_side_effects=_DATAFLOW),
    )(*p_thru, *land_thru, send_sems, recv_sems, after)
    return out[:n], out[n:]


def _shard_copies(s_refs, land_refs, send_sems, recv_sems):
    x, y, c, chips = _place()
    me_chip = 2 * x + y
    return [pltpu.make_async_remote_copy(src_ref=s_refs[t], dst_ref=land_refs[t].at[me_chip],
                                         send_sem=send_sems.at[3 * t + j], recv_sem=recv_sems.at[3 * t + j],
                                         device_id=(*chip, c), device_id_type=MESH)
            for t in range(len(s_refs)) for j, chip in enumerate(chips)]


def _gather_late_start(shards):
    n = len(shards)

    def body(*refs):
        s_refs, land_refs = refs[:n], refs[n:2 * n]
        send_sems, recv_sems = refs[2 * n:2 * n + 2]
        token = refs[-1]
        for cp in _shard_copies(s_refs, land_refs, send_sems, recv_sems):
            cp.start()
        token[...] = jnp.zeros_like(token)

    lands = [(N_CHIPS,) + s.shape for s in shards]
    args = [pltpu.with_memory_space_constraint(s, pltpu.HBM) for s in shards]
    args += [pltpu.with_memory_space_constraint(lax.empty(shp, s.dtype), pltpu.HBM) for shp, s in zip(lands, shards)]
    out = pl.pallas_call(
        body,
        name="gather_late_start",
        out_shape=(pltpu.SemaphoreType.DMA((3 * n,)), pltpu.SemaphoreType.DMA((3 * n,)),
                   *[pltpu.HBM(s.shape, s.dtype) for s in shards], *[pltpu.HBM(shp, s.dtype) for shp, s in zip(lands, shards)],
                   jax.ShapeDtypeStruct((8, LANES), F32)),
        in_specs=[_HBM] * (2 * n),
        out_specs=(_SEM, _SEM, *([_HBM] * (2 * n)), pl.BlockSpec(memory_space=pltpu.VMEM)),
        input_output_aliases={i: 2 + i for i in range(2 * n)},
        compiler_params=pltpu.CompilerParams(has_side_effects=_DATAFLOW),
    )(*args)
    return out[0], out[1], out[2:2 + n], out[2 + n:2 + 2 * n], out[-1]


def _gather_late_wait(send_sems, recv_sems, s_thru, land_thru, after):
    n = len(s_thru)

    def body(*refs):
        s_refs, land_refs = refs[:n], refs[n:2 * n]
        ssem, rsem = refs[2 * n:2 * n + 2]
        for cp in _shard_copies(s_refs, land_refs, ssem, rsem):
            cp.wait_send()
            cp.wait_recv()

    out = pl.pallas_call(
        body,
        name="gather_late_wait",
        out_shape=(*[pltpu.HBM(s.shape, s.dtype) for s in s_thru], *[pltpu.HBM(l.shape, l.dtype) for l in land_thru]),
        in_specs=[_HBM] * (2 * n) + [_SEM, _SEM, ANY],
        out_specs=tuple([_HBM] * (2 * n)),
        input_output_aliases={i: i for i in range(2 * n)},
        compiler_params=pltpu.CompilerParams(has_side_effects=_DATAFLOW),
    )(*s_thru, *land_thru, send_sems, recv_sems, after)
    return out[:n], out[n:]


def _pair_swap(rs):
    n = len(rs)

    def body(*refs):
        r_refs, o_refs = refs[:n], refs[n:2 * n]
        send_sems, recv_sems = refs[2 * n:]
        x, y, c, _ = _place()
        cps = []
        for t in range(n):
            cp = pltpu.make_async_remote_copy(src_ref=r_refs[t], dst_ref=o_refs[t], send_sem=send_sems.at[t],
                                              recv_sem=recv_sems.at[t], device_id=(x, y, 1 - c), device_id_type=MESH)
            cp.start()
            cps.append(cp)
        for cp in cps:
            cp.wait()

    return pl.pallas_call(
        body,
        out_shape=tuple(jax.ShapeDtypeStruct(r.shape, r.dtype) for r in rs),
        in_specs=[ANY] * n,
        out_specs=tuple([ANY] * n),
        scratch_shapes=[pltpu.SemaphoreType.DMA((n,)), pltpu.SemaphoreType.DMA((n,))],
        name="pair_swap",
    )(*rs)


N_DEV = 8
LOSS_ROW = 4


def _small_allreduce(small):
    def body(s_ref, o_ref, all_ref, send_sems, recv_sems):
        x, y, c, _ = _place()
        me = 4 * x + 2 * y + c
        all_ref[me] = s_ref[...]
        cps = []
        for k in range(1, N_DEV):
            peer = tuple(1 - p if (k >> s) & 1 else p for p, s in ((x, 2), (y, 1), (c, 0)))
            cp = pltpu.make_async_remote_copy(src_ref=s_ref, dst_ref=all_ref.at[me], send_sem=send_sems.at[k - 1],
                                              recv_sem=recv_sems.at[k - 1], device_id=peer, device_id_type=MESH)
            cp.start()
            cps.append(cp)
        for cp in cps:
            cp.wait()
        tot = all_ref[0]
        for d in range(1, N_DEV):
            tot = tot + all_ref[d]
        o_ref[...] = tot
        o_ref[LOSS_ROW:LOSS_ROW + 1, :] = jnp.broadcast_to(jnp.sum(tot[LOSS_ROW:LOSS_ROW + 1, :], axis=1, keepdims=True),
                                                          (1, tot.shape[1]))

    vm = pl.BlockSpec(memory_space=pltpu.VMEM)
    return pl.pallas_call(
        body,
        out_shape=jax.ShapeDtypeStruct(small.shape, small.dtype),
        in_specs=[vm],
        out_specs=vm,
        scratch_shapes=[pltpu.VMEM((N_DEV,) + small.shape, small.dtype), pltpu.SemaphoreType.DMA((N_DEV - 1,)),
                        pltpu.SemaphoreType.DMA((N_DEV - 1,))],
        name="small_allreduce",
    )(small)


def _sum_pair(g, recv, cidx, *, tr, name):
    n, hr, cols = recv.shape
    nr = hr // tr

    def body(c_ref, g_ref, r_ref, o_ref):
        o_ref[...] = (g_ref[...].astype(F32) + r_ref[...].astype(F32)).astype(o_ref.dtype)

    grid_spec = pltpu.PrefetchScalarGridSpec(
        num_scalar_prefetch=1,
        grid=(n, nr),
        in_specs=[pl.BlockSpec((None, tr, cols), lambda k, i, c_ref: (k, c_ref[0] * nr + i, 0)),
                  pl.BlockSpec((None, tr, cols), lambda k, i, c_ref: (k, i, 0))],
        out_specs=pl.BlockSpec((None, tr, cols), lambda k, i, c_ref: (k, i, 0)),
    )
    return pl.pallas_call(body, out_shape=jax.ShapeDtypeStruct(recv.shape, BF16), grid_spec=grid_spec,
                          compiler_params=_cparams(), name=name)(cidx, g, recv)


def _sum_chips(p, *, tr, name):
    _, rows, cols = p.shape

    def body(p_ref, o_ref):
        tot = p_ref[0].astype(F32)
        for k in range(1, N_CHIPS):
            tot = tot + p_ref[k].astype(F32)
        o_ref[...] = tot

    return pl.pallas_call(
        body,
        out_shape=jax.ShapeDtypeStruct((rows, cols), F32),
        grid=(rows // tr,),
        in_specs=[pl.BlockSpec((N_CHIPS, tr, cols), lambda i: (0, i, 0))],
        out_specs=pl.BlockSpec((tr, cols), lambda i: (i, 0)),
        compiler_params=_cparams(),
        name=name,
    )(p)


def _adamw(w, g, m, v, *, tr, name):
    rows, cols = w.shape
    bc1 = 1.0 / (1.0 - ADAM_B1 ** ADAM_STEP)
    bc2 = 1.0 / (1.0 - ADAM_B2 ** ADAM_STEP)

    def body(w_ref, g_ref, m_ref, v_ref, d_ref, nm_ref, nv_ref):
        gv = g_ref[...]
        nm = ADAM_B1 * m_ref[...] + (1.0 - ADAM_B1) * gv
        nv = ADAM_B2 * v_ref[...] + (1.0 - ADAM_B2) * (gv * gv)
        d_ref[...] = -ADAM_LR * ((nm * bc1) / (jnp.sqrt(nv * bc2) + ADAM_EPS) + ADAM_WD * w_ref[...])
        nm_ref[...] = nm
        nv_ref[...] = nv

    spec = pl.BlockSpec((tr, cols), lambda i: (i, 0))
    sd = jax.ShapeDtypeStruct((rows, cols), F32)
    return pl.pallas_call(body, out_shape=(sd, sd, sd), grid=(rows // tr,), in_specs=[spec] * 4, out_specs=(spec,) * 3,
                          compiler_params=_cparams(), name=name)(w, g, m, v)


def _adamw_halves(w, own, sib, cidx, m, v, *, tr, name):
    rows, cols = w.shape
    hr = own.shape[0]
    nr = hr // tr
    assert rows == 2 * hr and hr % tr == 0
    bc1 = 1.0 / (1.0 - ADAM_B1 ** ADAM_STEP)
    bc2 = 1.0 / (1.0 - ADAM_B2 ** ADAM_STEP)

    def body(c_ref, w_ref, o_ref, s_ref, m_ref, v_ref, g_ref, d_ref, nm_ref, nv_ref):
        mine = (pl.program_id(0) // nr) == c_ref[0]
        gv = jnp.where(mine, o_ref[...], s_ref[...])
        nm = ADAM_B1 * m_ref[...] + (1.0 - ADAM_B1) * gv
        nv = ADAM_B2 * v_ref[...] + (1.0 - ADAM_B2) * (gv * gv)
        g_ref[...] = gv
        d_ref[...] = -ADAM_LR * ((nm * bc1) / (jnp.sqrt(nv * bc2) + ADAM_EPS) + ADAM_WD * w_ref[...])
        nm_ref[...] = nm
        nv_ref[...] = nv

    full = pl.BlockSpec((tr, cols), lambda i, c_ref: (i, 0))
    half = pl.BlockSpec((tr, cols), lambda i, c_ref: (i % nr, 0))
    sd = jax.ShapeDtypeStruct((rows, cols), F32)
    grid_spec = pltpu.PrefetchScalarGridSpec(num_scalar_prefetch=1, grid=(rows // tr,), in_specs=[full, half, half, full, full],
                                             out_specs=(full,) * 4)
    return pl.pallas_call(body, out_shape=(sd,) * 4, grid_spec=grid_spec, compiler_params=_cparams(), name=name)(
        cidx, w, own, sib, m, v)


def _pack_small(norm, mem_norm, final_norm, b_forget):
    rows = [norm.reshape(1, D_MODEL), mem_norm.reshape(1, D_MODEL), final_norm.reshape(1, D_MODEL),
            jnp.pad(b_forget.reshape(1, FOX_HEADS), ((0, 0), (0, D_MODEL - FOX_HEADS))), jnp.zeros((4, D_MODEL), F32)]
    return jnp.concatenate(rows, axis=0)


def _unpack_small(a):
    return a[0:1], a[3:4, :FOX_HEADS], a[1:2], a[2]


def kernel(x, mem, norm_g, w_in, b_forget, mem_norm_g, w_mem_kv, w_out, final_norm_g, loss_target, m_norm_g, m_w_in, m_b_forget, m_mem_norm_g, m_w_mem_kv, m_w_out, m_final_norm_g, v_norm_g, v_w_in, v_b_forget, v_mem_norm_g, v_w_mem_kv, v_w_out, v_final_norm_g):
    core = lax.axis_index("c").astype(jnp.int32)
    me_chip = (2 * lax.axis_index("x") + lax.axis_index("y")).astype(jnp.int32)
    cidx = core.reshape(1)

    def own_slot(arr, own):
        return lax.dynamic_update_slice(arr, own[None].astype(arr.dtype), (me_chip,) + (0,) * own.ndim)

    win_b, late = w_in[0].astype(BF16), [w_mem_kv[0].astype(BF16), w_out[0].astype(BF16)]
    g_in, = _gather_weights([win_b])
    g_in, late = lax.optimization_barrier((own_slot(g_in, win_b), late))
    w_r = _rearrange_w_in([g_in[k] for k in range(N_CHIPS)])
    *late_flight, early_token = _gather_late_start(late)

    def late_weights(after):
        shards, landed = _gather_late_wait(*late_flight, after)
        g_kv, g_out = (own_slot(g, s) for g, s in zip(landed, shards))
        return g_kv.reshape(D_MODEL, 2 * MEM_W), g_out.reshape(MIX_W, D_MODEL)

    trs = (128, 128, 256)
    names = ("w_in", "w_mem_kv", "w_out")
    flights = {}

    def exchange(slabs, nms, ts, tag):
        recv = _pair_exchange(slabs, name=f"pair_exchange_{tag}")
        pair = [_sum_pair(g, r, cidx, tr=tr, name=f"sum_pair_{nm}") for g, r, tr, nm in zip(slabs, recv, ts, nms)]
        if tag == "w_in":
            pair[0] = _w_in_grad_slabs(pair[0][0])
        *flights[tag], token = _chip_exchange_start(pair, tag=tag)
        return token

    def start_reduce_small(g_wkv, g_wo):
        slabs = [g_wkv.reshape(N_CHIPS, D_MODEL // N_CHIPS, 2 * MEM_W), g_wo.reshape(N_CHIPS, MIX_W // N_CHIPS, D_MODEL)]
        return exchange(slabs, names[1:], trs[1:], "small")

    def start_reduce(g_wr):
        return exchange([g_wr[None]], names[:1], trs[:1], "w_in")

    gx, g_wr, g_wkv, g_wo, small = _local_grads(x, mem, norm_g, w_r, b_forget, mem_norm_g, None, None, final_norm_g, loss_target,
                                                start_reduce=start_reduce, start_reduce_small=start_reduce_small,
                                                early_token=early_token, late_weights=late_weights)

    pair, landed = [], []
    for tag in ("w_in", "small"):
        p, l = _chip_exchange_wait(*flights[tag], small, tag=tag)
        pair += list(p)
        landed += list(l)
    got = [lax.dynamic_update_slice(g, lax.dynamic_slice(p, (me_chip, 0, 0), (1,) + p.shape[1:]), (me_chip, 0, 0))
           for g, p in zip(landed, pair)]
    red = [_sum_chips(p, tr=tr, name=f"sum_chips_{nm}") for p, tr, nm in zip(got, trs, names)]
    sib = _pair_swap(red)

    outs = {}
    for nm, r, s, w, m, v, tr in zip(names, red, sib, (w_in, w_mem_kv, w_out), (m_w_in, m_w_mem_kv, m_w_out),
                                     (v_w_in, v_w_mem_kv, v_w_out), trs):
        outs[nm] = tuple(a[None] for a in _adamw_halves(w[0], r, s, cidx, m[0], v[0], tr=tr, name=f"adamw_{nm}"))

    gsum = _small_allreduce(small)
    sd, sm, sv = _adamw(_pack_small(norm_g, mem_norm_g, final_norm_g, b_forget), gsum,
                        _pack_small(m_norm_g, m_mem_norm_g, m_final_norm_g, m_b_forget),
                        _pack_small(v_norm_g, v_mem_norm_g, v_final_norm_g, v_b_forget), tr=8, name="adamw_small")
    loss = gsum[LOSS_ROW, 0]

    def group(i, small_arr):
        ng, bf, mg, fg = _unpack_small(small_arr)
        return (ng, outs["w_in"][i], bf, mg, outs["w_mem_kv"][i], outs["w_out"][i], fg)

    return (loss, gx, *group(0, gsum), *group(1, sd), *group(2, sm), *group(3, sv))
```

```python
import functools
import math

import jax
import jax.numpy as jnp
from jax import lax
from jax.experimental import pallas as pl
from jax.experimental.pallas import tpu as pltpu

F32 = jnp.float32
BF16 = jnp.bfloat16

D_MODEL = 1024
SEQ = 2048
HEAD_DIM = 64
FOX_HEADS = 12
DIL_HEADS = 12
MEM_HEADS = 4
MEM_HEAD_DIM = 128
MEM_LEN = 256
FOX_W = FOX_HEADS * HEAD_DIM
DIL_W = DIL_HEADS * HEAD_DIM
MEM_W = MEM_HEADS * MEM_HEAD_DIM
MIX_W = FOX_W + DIL_W + MEM_W
DILATIONS = ((128, 1), (512, 4), (2048, 16))
ROPE_THETA = 500000.0
ROPE_DIM = HEAD_DIM // 4
RMS_EPS = 1e-6
NEG_INF = -1e30
IN_SIZES = [FOX_W] * 4 + [FOX_HEADS] + [DIL_W] * 4 + [MEM_W] * 2
IN_W = sum(IN_SIZES)

ADAM_LR = 0.001
ADAM_B1 = 0.9
ADAM_B2 = 0.999
ADAM_EPS = 1e-08
ADAM_WD = 0.01
ADAM_STEP = 10

LANES = 128
N_CHIPS = 4
PW = 7168
PWF = PW + 4 * LANES
C_FQ, C_FK, C_FV, C_FG = 0, 768, 1536, 2304
C_DQ, C_DK, C_DV, C_DG = 3072, 3840, 4608, 5376
C_MQ, C_MG = 6144, 6656
VMEM_LIMIT = 48 * 1024 * 1024


def _cparams(**kw):
    return pltpu.CompilerParams(vmem_limit_bytes=VMEM_LIMIT, **kw)


MM_CHUNK = 256


def _matmul(a, b, *, out_dtype, tm, tn, tk, name, mode="nn"):
    if mode == "tn":
        (kdim, m), n = a.shape, b.shape[1]
        a_spec = pl.BlockSpec((tk, tm), lambda i, j, k: (k, i))
        b_spec = pl.BlockSpec((tk, tn), lambda i, j, k: (k, j))
        dims = _T0
    elif mode == "nt":
        (m, kdim), n = a.shape, b.shape[0]
        a_spec = pl.BlockSpec((tm, tk), lambda i, j, k: (i, k))
        b_spec = pl.BlockSpec((tn, tk), lambda i, j, k: (j, k))
        dims = _NT
    else:
        (m, kdim), n = a.shape, b.shape[1]
        a_spec = pl.BlockSpec((tm, tk), lambda i, j, k: (i, k))
        b_spec = pl.BlockSpec((tk, tn), lambda i, j, k: (k, j))
        dims = (((1,), (0,)), ((), ()))
    nk = kdim // tk
    assert m % tm == 0 and n % tn == 0 and kdim % tk == 0

    def body(a_ref, b_ref, o_ref, *scratch):
        if nk == 1:
            bv = b_ref[...]
            for c0 in range(0, tm, min(tm, MM_CHUNK)):
                rows = pl.ds(c0, min(tm, MM_CHUNK))
                av = a_ref[:, rows] if mode == "tn" else a_ref[rows, :]
                o_ref[rows, :] = lax.dot_general(av, bv, dims, preferred_element_type=F32).astype(o_ref.dtype)
            return
        prod = lax.dot_general(a_ref[...], b_ref[...], dims, preferred_element_type=F32)
        acc_ref, = scratch
        k = pl.program_id(2)

        @pl.when(k == 0)
        def _():
            acc_ref[...] = prod

        @pl.when(k > 0)
        def _():
            acc_ref[...] += prod

        @pl.when(k == nk - 1)
        def _():
            o_ref[...] = acc_ref[...].astype(o_ref.dtype)

    return pl.pallas_call(
        body,
        out_shape=jax.ShapeDtypeStruct((m, n), out_dtype),
        grid=(m // tm, n // tn, nk),
        in_specs=[a_spec, b_spec],
        out_specs=pl.BlockSpec((tm, tn), lambda i, j, k: (i, j)),
        scratch_shapes=[pltpu.VMEM((tm, tn), F32)] if nk > 1 else [],
        compiler_params=_cparams(dimension_semantics=("parallel", "parallel", "arbitrary")),
        name=name,
    )(a, b)


def _rms_fwd(x, g, *, tm, name):
    t, d = x.shape

    def body(x_ref, g_ref, h_ref):
        xv = x_ref[...]
        r = lax.rsqrt(jnp.mean(xv * xv, axis=-1, keepdims=True) + RMS_EPS)
        h_ref[...] = (xv * r * g_ref[...]).astype(h_ref.dtype)

    return pl.pallas_call(
        body,
        out_shape=jax.ShapeDtypeStruct((t, d), BF16),
        grid=(t // tm,),
        in_specs=[pl.BlockSpec((tm, d), lambda i: (i, 0)), pl.BlockSpec((1, d), lambda i: (0, 0))],
        out_specs=pl.BlockSpec((tm, d), lambda i: (i, 0)),
        compiler_params=_cparams(),
        name=name,
    )(x, g)


def _rope_tables():
    half = ROPE_DIM // 2
    pos = jnp.arange(SEQ, dtype=F32)
    inv_freq = 1.0 / (ROPE_THETA ** (jnp.arange(0, ROPE_DIM, 2, dtype=F32) / ROPE_DIM))
    ang = pos[:, None] * inv_freq[None, :]
    cos, sin = jnp.cos(ang), jnp.sin(ang)
    one = jnp.ones((SEQ, HEAD_DIM - ROPE_DIM), F32)
    zero = jnp.zeros((SEQ, HEAD_DIM - ROPE_DIM), F32)
    zh = jnp.zeros((SEQ, half), F32)
    c = jnp.concatenate([cos, cos, one], axis=1)
    s1 = jnp.concatenate([zh, sin, zero], axis=1)
    s2 = jnp.concatenate([-sin, zh, zero], axis=1)
    rep = LANES // HEAD_DIM
    return jnp.tile(c, (1, rep)), jnp.tile(s1, (1, rep)), jnp.tile(s2, (1, rep))


def _rope_apply(t, c, s1, s2, transpose=False):
    n = t.shape[-1]
    rep = n // LANES
    c, s1, s2 = (jnp.tile(u, (1, rep)) for u in (c, s1, s2))
    half = ROPE_DIM // 2
    if not transpose:
        return t * c + pltpu.roll(t, half, 1) * s1 + pltpu.roll(t, n - half, 1) * s2
    return t * c + pltpu.roll(t * s1, n - half, 1) + pltpu.roll(t * s2, half, 1)


PROJ_CHUNK = 256


def _proj(x, g, w, tabs, *, n, tm, tn, name):
    t, d = x.shape
    assert C_DQ % tn == 0 and (C_DV - C_DQ) % tn == 0 and (C_DG - C_DQ) % tn == 0
    rope_lo, rope_hi, dil_hi = C_DQ // tn, C_DV // tn, C_DG // tn
    flog_blk, flog_at = PW // tn, PW % tn
    assert flog_at % LANES == 0 and flog_at + LANES <= tn
    s_blocks = SEQ // tm

    def body(x_ref, g_ref, w_ref, c_ref, s1_ref, s2_ref, h_ref, o_ref, f_ref, fl_ref, h_scr):
        j = pl.program_id(1)

        @pl.when(j == 0)
        def _():
            xv = x_ref[...]
            r = lax.rsqrt(jnp.mean(xv * xv, axis=-1, keepdims=True) + RMS_EPS)
            hv = (xv * r * g_ref[...]).astype(BF16)
            h_scr[...] = hv
            h_ref[...] = hv

        def tile(kind):
            wv = w_ref[...]
            for c0 in range(0, tm, PROJ_CHUNK):
                rows = pl.ds(c0, PROJ_CHUNK)
                acc = jnp.dot(h_scr[rows, :], wv, preferred_element_type=F32)
                if kind == "rope":
                    acc = _rope_apply(acc, c_ref[rows, :], s1_ref[rows, :], s2_ref[rows, :])
                o_ref[rows, :] = acc.astype(o_ref.dtype)
                if kind in ("rope", "dv"):
                    f_ref[rows, :] = acc
                if kind == "flog":
                    fl_ref[rows, :] = acc[:, flog_at:flog_at + LANES]

        is_rope = jnp.logical_and(j >= rope_lo, j < rope_hi)
        is_dv = jnp.logical_and(j >= rope_hi, j < dil_hi)
        is_flog = j == flog_blk
        pl.when(is_rope)(functools.partial(tile, "rope"))
        pl.when(is_dv)(functools.partial(tile, "dv"))
        pl.when(is_flog)(functools.partial(tile, "flog"))
        pl.when(jnp.logical_not(jnp.logical_or(jnp.logical_or(is_rope, is_dv), is_flog)))(functools.partial(tile, "plain"))

    tab_spec = pl.BlockSpec((tm, LANES), lambda i, j: (i % s_blocks, 0))
    f_spec = pl.BlockSpec((tm, tn), lambda i, j: (i, jnp.clip(j - rope_lo, 0, dil_hi - rope_lo - 1)))
    row = pl.BlockSpec((tm, d), lambda i, j: (i, 0))
    return pl.pallas_call(
        body,
        out_shape=(jax.ShapeDtypeStruct((t, d), BF16), jax.ShapeDtypeStruct((t, n), BF16),
                   jax.ShapeDtypeStruct((t, 3 * DIL_W), F32), jax.ShapeDtypeStruct((t, LANES), F32)),
        grid=(t // tm, n // tn),
        in_specs=[row, pl.BlockSpec((1, d), lambda i, j: (0, 0)), pl.BlockSpec((d, tn), lambda i, j: (0, j)),
                  tab_spec, tab_spec, tab_spec],
        out_specs=(row, pl.BlockSpec((tm, tn), lambda i, j: (i, j)), f_spec, pl.BlockSpec((tm, LANES), lambda i, j: (i, 0))),
        scratch_shapes=[pltpu.VMEM((tm, d), BF16)],
        compiler_params=_cparams(dimension_semantics=("parallel", "arbitrary")),
        name=name,
    )(x, g, w, *tabs)


def _split3(x):
    hi = x.astype(BF16)
    r1 = x - hi.astype(F32)
    mid = r1.astype(BF16)
    lo = (r1 - mid.astype(F32)).astype(BF16)
    return hi, mid, lo


def _dot3(sel, x, sel_is_lhs):
    out = None
    for piece in _split3(x):
        t = jnp.dot(sel, piece, preferred_element_type=F32) if sel_is_lhs else jnp.dot(piece, sel, preferred_element_type=F32)
        out = t if out is None else out + t
    return out


def _flog_fwd(flog, bpad, *, nb, ts, name):
    ns = SEQ // ts

    def body(f_ref, b_ref, c_ref, carry_ref):
        s = pl.program_id(1)

        @pl.when(s == 0)
        def _():
            carry_ref[...] = jnp.zeros_like(carry_ref)

        z = f_ref[...] + b_ref[...]
        logf = jnp.minimum(z, 0.0) - jnp.log(1.0 + jnp.exp(-jnp.abs(z)))
        r = lax.broadcasted_iota(jnp.int32, (ts, ts), 0)
        c = lax.broadcasted_iota(jnp.int32, (ts, ts), 1)
        tri = jnp.where(r >= c, 1.0, 0.0).astype(BF16)
        cs = _dot3(tri, logf, True) + carry_ref[0:1, :]
        carry_ref[...] = jnp.broadcast_to(cs[ts - 1:ts, :], carry_ref.shape)
        c_ref[...] = cs

    return pl.pallas_call(
        body,
        out_shape=jax.ShapeDtypeStruct((nb * SEQ, LANES), F32),
        grid=(nb, ns),
        in_specs=[pl.BlockSpec((ts, LANES), lambda b, s: (b * ns + s, 0)), pl.BlockSpec((1, LANES), lambda b, s: (0, 0))],
        out_specs=pl.BlockSpec((ts, LANES), lambda b, s: (b * ns + s, 0)),
        scratch_shapes=[pltpu.VMEM((8, LANES), F32)],
        compiler_params=_cparams(dimension_semantics=("parallel", "arbitrary")),
        name=name,
    )(flog, bpad)


def _flog_bwd(dcol, flog, bpad, *, nb, ts, name):
    ns = SEQ // ts

    def body(d_ref, f_ref, b_ref, o_ref, gb_ref, carry_ref):
        bi = pl.program_id(0)
        s = pl.program_id(1)

        @pl.when(s == 0)
        def _():
            carry_ref[...] = jnp.zeros_like(carry_ref)

        @pl.when(jnp.logical_and(bi == 0, s == 0))
        def _():
            gb_ref[...] = jnp.zeros_like(gb_ref)

        r = lax.broadcasted_iota(jnp.int32, (ts, ts), 0)
        c = lax.broadcasted_iota(jnp.int32, (ts, ts), 1)
        tri = jnp.where(r <= c, 1.0, 0.0).astype(BF16)
        rc = _dot3(tri, d_ref[...], True) + carry_ref[0:1, :]
        carry_ref[...] = jnp.broadcast_to(rc[0:1, :], carry_ref.shape)
        z = f_ref[...] + b_ref[...]
        dz = rc / (1.0 + jnp.exp(z))
        o_ref[...] = dz.astype(o_ref.dtype)
        gb_ref[...] += jnp.broadcast_to(jnp.sum(dz, axis=0, keepdims=True), gb_ref.shape)

    rev = lambda b, s: (b * ns + (ns - 1 - s), 0)
    return pl.pallas_call(
        body,
        out_shape=(jax.ShapeDtypeStruct((nb * SEQ, LANES), BF16), jax.ShapeDtypeStruct((8, LANES), F32)),
        grid=(nb, ns),
        in_specs=[pl.BlockSpec((ts, LANES), rev), pl.BlockSpec((ts, LANES), rev), pl.BlockSpec((1, LANES), lambda b, s: (0, 0))],
        out_specs=(pl.BlockSpec((ts, LANES), rev), pl.BlockSpec((8, LANES), lambda b, s: (0, 0))),
        scratch_shapes=[pltpu.VMEM((8, LANES), F32)],
        compiler_params=_cparams(dimension_semantics=("arbitrary", "arbitrary")),
        name=name,
    )(dcol, flog, bpad)


MEM_TQ = 256
MEM_SET = 4
MEM_SCALE = 1.0 / math.sqrt(MEM_HEAD_DIM)
assert MEM_HEAD_DIM == LANES and SEQ % (MEM_TQ * MEM_SET) == 0


def _head_masks(nh):
    lane = lax.broadcasted_iota(jnp.int32, (1, LANES), 1)
    return [None] if nh == 1 else [lane < HEAD_DIM, lane >= HEAD_DIM]


def _mem_specs(qoff):
    qspec = pl.BlockSpec((None, SEQ, LANES), lambda b, j: (b, 0, qoff + j))
    kspec = pl.BlockSpec((None, MEM_LEN, LANES), lambda b, j: (b, 0, j))
    vspec = pl.BlockSpec((None, MEM_LEN, LANES), lambda b, j: (b, 0, MEM_HEADS + j))
    ospec = pl.BlockSpec((None, SEQ, LANES), lambda b, j: (b, 0, j))
    return qspec, kspec, vspec, ospec


def _mem_rows(g):
    return [pl.ds(pl.multiple_of((MEM_SET * g + a) * MEM_TQ, MEM_TQ), MEM_TQ) for a in range(MEM_SET)]


def _mem_fwd(p3, mkv3, *, qoff, name):
    nb = p3.shape[0]

    def body(q_ref, k_ref, v_ref, o_ref, lse_ref):
        kb, vb = k_ref[...], v_ref[...]

        def qset(g, c):
            rows = _mem_rows(g)
            ss = [lax.dot_general(q_ref[r, :] * MEM_SCALE, kb, _NT, preferred_element_type=F32) for r in rows]
            for r, s in zip(rows, ss):
                m = jnp.max(s, axis=1, keepdims=True)
                p = jnp.exp(s - m)
                l = jnp.sum(p, axis=1, keepdims=True)
                o_ref[r, :] = jnp.dot(p.astype(BF16), vb, preferred_element_type=F32) / l
                lse_ref[r, :] = jnp.broadcast_to(m + jnp.log(l), (MEM_TQ, LANES))
            return c

        lax.fori_loop(0, SEQ // MEM_TQ // MEM_SET, qset, 0)

    qspec, kspec, vspec, ospec = _mem_specs(qoff)
    osd = jax.ShapeDtypeStruct((nb, SEQ, MEM_W), F32)
    return pl.pallas_call(body, out_shape=(osd, osd), grid=(nb, MEM_HEADS), in_specs=[qspec, kspec, vspec],
                          out_specs=(ospec, ospec), compiler_params=_cparams(dimension_semantics=("parallel", "parallel")),
                          name=name)(p3, mkv3, mkv3)


def _mem_bwd(p3, mkv3, do, o, lse, *, qoff, do_off, name):
    nb = p3.shape[0]

    def body(q_ref, k_ref, v_ref, do_ref, o_ref, lse_ref, dq_ref, dk_ref, dv_ref):
        kb, vb = k_ref[...], v_ref[...]
        ks = kb * MEM_SCALE

        def qset(g, carry):
            dk, dv = carry
            work = []
            for r in _mem_rows(g):
                qs = q_ref[r, :] * MEM_SCALE
                dob = do_ref[r, :].astype(BF16)
                s = lax.dot_general(qs, kb, _NT, preferred_element_type=F32)
                dp = lax.dot_general(dob, vb, _NT, preferred_element_type=F32)
                work.append((r, qs, dob, s, dp))
            for r, qs, dob, s, dp in work:
                delta = jnp.sum(dob.astype(F32) * o_ref[r, :], axis=1, keepdims=True)
                p = jnp.exp(s - lse_ref[r, :][:, 0:1])
                ds = (p * (dp - delta)).astype(BF16)
                dq_ref[r, :] = jnp.dot(ds, ks, preferred_element_type=F32).astype(dq_ref.dtype)
                dk = dk + lax.dot_general(ds, qs, _T0, preferred_element_type=F32)
                dv = dv + lax.dot_general(p.astype(BF16), dob, _T0, preferred_element_type=F32)
            return dk, dv

        z = jnp.zeros((MEM_LEN, LANES), F32)
        dk, dv = lax.fori_loop(0, SEQ // MEM_TQ // MEM_SET, qset, (z, z))
        dk_ref[...] = dk
        dv_ref[...] = dv

    qspec, kspec, vspec, ospec = _mem_specs(qoff)
    dospec = pl.BlockSpec((None, SEQ, LANES), lambda b, j: (b, 0, do_off + j))
    kvo = pl.BlockSpec((None, MEM_LEN, LANES), lambda b, j: (b, 0, j))
    kvsd = jax.ShapeDtypeStruct((nb, MEM_LEN, MEM_W), F32)
    return pl.pallas_call(
        body, out_shape=(jax.ShapeDtypeStruct((nb, SEQ, MEM_W), BF16), kvsd, kvsd), grid=(nb, MEM_HEADS),
        in_specs=[qspec, kspec, vspec, dospec, ospec, ospec], out_specs=(ospec, kvo, kvo),
        compiler_params=_cparams(dimension_semantics=("parallel", "parallel")), name=name)(p3, mkv3, mkv3, do, o, lse)


BLK = 128
NBLK = SEQ // BLK
QK_SCALE = 1.0 / math.sqrt(HEAD_DIM)
DIL_STEPS = tuple(d for _, d in DILATIONS)
assert all(w // d == BLK for w, d in DILATIONS)
_T0 = (((0,), (0,)), ((), ()))
_NT = (((1,), (1,)), ((), ()))


def _stack_heads(a, masks):
    z = jnp.zeros_like(a)
    return jnp.concatenate([jnp.where(masks[0], a, z), jnp.where(masks[1], a, z)], axis=0)


def _tri_bias(lower):
    r = lax.broadcasted_iota(jnp.int32, (BLK, BLK), 0)
    c = lax.broadcasted_iota(jnp.int32, (BLK, BLK), 1)
    return jnp.where((c <= r) if lower else (c >= r), 0.0, NEG_INF).astype(F32)


def _dil_rows(r, i, d):
    start = r + i * (BLK * d)
    return pl.ds(start, BLK) if d == 1 else pl.ds(start, BLK, stride=d)


DIL_SET = 4


def _dil_sets(d, fn):
    nbk = SEQ // d // BLK
    if d == 1:
        n = 2 * DIL_SET
        def gbody(g, c):
            fn([(0, n * g + a, None if a == 0 else True) for a in range(n)])
            return c
        lax.fori_loop(0, nbk // n, gbody, 0)
    elif nbk > 1:
        assert nbk == DIL_SET
        def rbody(r, c):
            fn([(r, i, i > 0) for i in range(nbk)])
            return c
        lax.fori_loop(0, d, rbody, 0)
    else:
        def rbody(rr, c):
            fn([(DIL_SET * rr + a, 0, False) for a in range(DIL_SET)])
            return c
        lax.fori_loop(0, d // DIL_SET, rbody, 0)


def _dil_key_tiles(r, i, d, has_prev, qrows, tri_cur, tri_prev):
    tiles = [(qrows, tri_cur)]
    if has_prev is None:
        tiles.append((_dil_rows(r, jnp.maximum(i - 1, 0), d), tri_prev + jnp.where(i > 0, 0.0, NEG_INF)))
    elif has_prev:
        tiles.append((_dil_rows(r, i - 1, d), tri_prev))
    return tiles


def _dil_fwd(qkv, *, name):
    nb = qkv.shape[0]
    ncol = DIL_W // LANES
    hd = HEAD_DIM

    def body(q_ref, k_ref, v_ref, o_ref, lse_ref, m_ref, l_ref, a_ref):
        masks = _head_masks(2)
        tri_cur, tri_prev = _tri_bias(True), _tri_bias(False)
        for pi, d in enumerate(DIL_STEPS):
            first, last = pi == 0, pi == len(DIL_STEPS) - 1

            def qset(blocks, d=d, first=first, last=last):
                work = []
                for r, i, has_prev in blocks:
                    qrows = _dil_rows(r, i, d)
                    qcat = _stack_heads((q_ref[qrows, :] * QK_SCALE).astype(BF16), masks)
                    ss, krs = [], []
                    for krows, bias in _dil_key_tiles(r, i, d, has_prev, qrows, tri_cur, tri_prev):
                        s = lax.dot_general(qcat, k_ref[krows, :].astype(BF16), _NT, preferred_element_type=F32)
                        ss.append((s[:BLK] + bias, s[BLK:] + bias))
                        krs.append(krows)
                    work.append((qrows, ss, krs))
                for qrows, ss, krs in work:
                    e0 = ss[0][0] if len(ss) == 1 else jnp.maximum(ss[0][0], ss[1][0])
                    e1 = ss[0][1] if len(ss) == 1 else jnp.maximum(ss[0][1], ss[1][1])
                    n0 = jnp.max(e0, axis=1, keepdims=True)
                    n1 = jnp.max(e1, axis=1, keepdims=True)
                    if not first:
                        mo, lo = m_ref[qrows, :], l_ref[qrows, :]
                        m0, m1 = mo[:, 0:1], mo[:, hd:hd + 1]
                        n0, n1 = jnp.maximum(n0, m0), jnp.maximum(n1, m1)
                        a0, a1 = jnp.exp(m0 - n0), jnp.exp(m1 - n1)
                    ps = [(jnp.exp(s0 - n0), jnp.exp(s1 - n1)) for s0, s1 in ss]
                    t0 = ps[0][0] if len(ps) == 1 else ps[0][0] + ps[1][0]
                    t1 = ps[0][1] if len(ps) == 1 else ps[0][1] + ps[1][1]
                    l0 = jnp.sum(t0, axis=1, keepdims=True)
                    l1 = jnp.sum(t1, axis=1, keepdims=True)
                    acc = None
                    for (p0, p1), krows in zip(ps, krs):
                        vcat = _stack_heads(v_ref[krows, :].astype(BF16), masks)
                        pv = jnp.dot(jnp.concatenate([p0, p1], axis=1).astype(BF16), vcat, preferred_element_type=F32)
                        acc = pv if acc is None else acc + pv
                    if not first:
                        l0 = l0 + a0 * lo[:, 0:1]
                        l1 = l1 + a1 * lo[:, hd:hd + 1]
                        acc = acc + a_ref[qrows, :] * jnp.where(masks[0], a0, a1)
                    if last:
                        o_ref[qrows, :] = acc / jnp.where(masks[0], l0, l1)
                        lse_ref[qrows, :] = jnp.where(masks[0], n0 + jnp.log(l0), n1 + jnp.log(l1))
                    else:
                        m_ref[qrows, :] = jnp.where(masks[0], n0, n1)
                        l_ref[qrows, :] = jnp.where(masks[0], l0, l1)
                        a_ref[qrows, :] = acc

            _dil_sets(d, qset)

    spec = lambda off: pl.BlockSpec((None, SEQ, LANES), lambda b, j: (b, 0, off + j))
    ospec = pl.BlockSpec((None, SEQ, LANES), lambda b, j: (b, 0, j))
    osd = jax.ShapeDtypeStruct((nb, SEQ, DIL_W), F32)
    return pl.pallas_call(
        body, out_shape=(osd, osd), grid=(nb, ncol),
        in_specs=[spec(0), spec(ncol), spec(2 * ncol)], out_specs=(ospec, ospec),
        scratch_shapes=[pltpu.VMEM((SEQ, LANES), F32)] * 3,
        compiler_params=_cparams(dimension_semantics=("parallel", "parallel")), name=name,
    )(qkv, qkv, qkv)


def _dil_bwd(qkv, do, o, lse, tabs, *, do_off, name):
    nb = qkv.shape[0]
    ncol = DIL_W // LANES
    hd = HEAD_DIM

    def body(q_ref, k_ref, v_ref, do_ref, o_ref, lse_ref, c_ref, s1_ref, s2_ref, dqo_ref, dko_ref, dvo_ref,
             dq_ref, dk_ref, dv_ref, dl_ref, dof_ref):
        masks = _head_masks(2)
        tri_cur, tri_prev = _tri_bias(True), _tri_bias(False)
        dq_ref[...] = jnp.zeros_like(dq_ref)
        dk_ref[...] = jnp.zeros_like(dk_ref)
        dv_ref[...] = jnp.zeros_like(dv_ref)

        def delta_body(i, c):
            rows = pl.ds(pl.multiple_of(i * BLK, BLK), BLK)
            dof = do_ref[rows, :].astype(F32)
            dof_ref[rows, :] = dof
            prod = dof * o_ref[rows, :]
            z = jnp.zeros_like(prod)
            dl_ref[rows, :] = jnp.where(masks[0], jnp.sum(jnp.where(masks[0], prod, z), axis=1, keepdims=True),
                                        jnp.sum(jnp.where(masks[1], prod, z), axis=1, keepdims=True))
            return c

        lax.fori_loop(0, NBLK, delta_body, 0)

        for d in DIL_STEPS:
            def qset(blocks, d=d):
                work = []
                for r, i, has_prev in blocks:
                    qrows = _dil_rows(r, i, d)
                    qcat = _stack_heads((q_ref[qrows, :] * QK_SCALE).astype(BF16), masks)
                    docat = _stack_heads(dof_ref[qrows, :].astype(BF16), masks)
                    tiles = []
                    for krows, bias in _dil_key_tiles(r, i, d, has_prev, qrows, tri_cur, tri_prev):
                        s = lax.dot_general(qcat, k_ref[krows, :].astype(BF16), _NT, preferred_element_type=F32)
                        dp = lax.dot_general(docat, v_ref[krows, :].astype(BF16), _NT, preferred_element_type=F32)
                        tiles.append((krows, s, dp, bias))
                    work.append((qrows, qcat, docat, tiles))
                for qrows, qcat, docat, tiles in work:
                    lseb, dlb = lse_ref[qrows, :], dl_ref[qrows, :]
                    lse0, lse1 = lseb[:, 0:1], lseb[:, hd:hd + 1]
                    dl0, dl1 = dlb[:, 0:1], dlb[:, hd:hd + 1]
                    dq = None
                    for krows, s, dp, bias in tiles:
                        p0 = jnp.exp(s[:BLK] + bias - lse0)
                        p1 = jnp.exp(s[BLK:] + bias - lse1)
                        ds0 = p0 * (dp[:BLK] - dl0)
                        ds1 = p1 * (dp[BLK:] - dl1)
                        ds0b, ds1b = ds0.astype(BF16), ds1.astype(BF16)
                        pcat = jnp.concatenate([p0.astype(BF16), p1.astype(BF16)], axis=0)
                        dscat = jnp.concatenate([ds0b, ds1b], axis=0)
                        dv_ref[krows, :] += lax.dot_general(pcat, docat, _T0, preferred_element_type=F32)
                        dk_ref[krows, :] += lax.dot_general(dscat, qcat, _T0, preferred_element_type=F32)
                        dsrow = jnp.concatenate([ds0b, ds1b], axis=1)
                        kcat = _stack_heads((k_ref[krows, :] * QK_SCALE).astype(BF16), masks)
                        t = jnp.dot(dsrow, kcat, preferred_element_type=F32)
                        dq = t if dq is None else dq + t
                    dq_ref[qrows, :] += dq

            _dil_sets(d, qset)

        def out_body(i, c):
            rows = pl.ds(pl.multiple_of(i * BLK, BLK), BLK)
            tab = (c_ref[rows, :], s1_ref[rows, :], s2_ref[rows, :])
            dqo_ref[rows, :] = _rope_apply(dq_ref[rows, :], *tab, transpose=True).astype(dqo_ref.dtype)
            dko_ref[rows, :] = _rope_apply(dk_ref[rows, :], *tab, transpose=True).astype(dko_ref.dtype)
            dvo_ref[rows, :] = dv_ref[rows, :].astype(dvo_ref.dtype)
            return c

        lax.fori_loop(0, NBLK, out_body, 0)

    spec = lambda off: pl.BlockSpec((None, SEQ, LANES), lambda b, j: (b, 0, off + j))
    ospec = pl.BlockSpec((None, SEQ, LANES), lambda b, j: (b, 0, j))
    tspec = pl.BlockSpec((SEQ, LANES), lambda b, j: (0, 0))
    osd = jax.ShapeDtypeStruct((nb, SEQ, DIL_W), BF16)
    return pl.pallas_call(
        body, out_shape=(osd, osd, osd), grid=(nb, ncol),
        in_specs=[spec(0), spec(ncol), spec(2 * ncol), spec(do_off), ospec, ospec, tspec, tspec, tspec],
        out_specs=(ospec, ospec, ospec),
        scratch_shapes=[pltpu.VMEM((SEQ, LANES), F32)] * 5,
        compiler_params=_cparams(dimension_semantics=("parallel", "parallel")), name=name,
    )(qkv, qkv, qkv, do, o, lse, *tabs)


FOX_GROUP = 4
assert NBLK % FOX_GROUP == 0
_FOX_COLS = tuple(c // LANES for c in (C_FQ, C_FK, C_FV))


def _fox_specs():
    cols = [pl.BlockSpec((None, SEQ, LANES), (lambda b, j, off=off: (b, 0, off + j))) for off in _FOX_COLS]
    ospec = pl.BlockSpec((None, SEQ, LANES), lambda b, j: (b, 0, j))
    crspec = pl.BlockSpec((None, None, NBLK, 8, BLK), lambda b, j: (b, j, 0, 0, 0))
    return cols, ospec, crspec


def _fox_key_rows(t, e):
    return pl.ds(pl.multiple_of((FOX_GROUP * t + e) * BLK, BLK), BLK)


def _fox_fwd(p3, crow, *, name):
    nb = p3.shape[0]
    g = FOX_GROUP

    def body(q_ref, k_ref, v_ref, cr_ref, o_ref, lse_ref):
        masks = _head_masks(2)
        tri = _tri_bias(True)

        def qk(qcat, t, nblk=g):
            return tuple(lax.dot_general(qcat, k_ref[_fox_key_rows(t, e), :], _NT, preferred_element_type=F32) for e in range(nblk))

        def consume(ss, t, state, nblk, diag):
            m0, m1, l0, l1, acc = state
            us = []
            for e in range(nblk):
                cr = cr_ref[g * t + e]
                u0 = ss[e][:BLK] - cr[0:1, :]
                u1 = ss[e][BLK:] - cr[1:2, :]
                if diag and e == nblk - 1:
                    u0, u1 = u0 + tri, u1 + tri
                us.append((u0, u1))
            x0 = functools.reduce(jnp.maximum, [u[0] for u in us])
            x1 = functools.reduce(jnp.maximum, [u[1] for u in us])
            n0 = jnp.maximum(m0, jnp.max(x0, axis=1, keepdims=True))
            n1 = jnp.maximum(m1, jnp.max(x1, axis=1, keepdims=True))
            a0, a1 = jnp.exp(m0 - n0), jnp.exp(m1 - n1)
            acc = acc * jnp.where(masks[0], a0, a1)
            t0 = t1 = None
            for e in range(nblk):
                p0, p1 = jnp.exp(us[e][0] - n0), jnp.exp(us[e][1] - n1)
                t0 = p0 if t0 is None else t0 + p0
                t1 = p1 if t1 is None else t1 + p1
                pcat = jnp.concatenate([p0, p1], axis=1)
                hi = pcat.astype(BF16)
                lo = (pcat - hi.astype(F32)).astype(BF16)
                vcat = _stack_heads(v_ref[_fox_key_rows(t, e), :], masks)
                acc = acc + jnp.dot(hi, vcat, preferred_element_type=F32) + jnp.dot(lo, vcat, preferred_element_type=F32)
            l0 = a0 * l0 + jnp.sum(t0, axis=1, keepdims=True)
            l1 = a1 * l1 + jnp.sum(t1, axis=1, keepdims=True)
            return n0, n1, l0, l1, acc

        def gbody(ng, c):
            neg = jnp.full((BLK, 1), NEG_INF, F32)
            z1 = jnp.zeros((BLK, 1), F32)
            rows = [pl.ds(pl.multiple_of((g * ng + a) * BLK, BLK), BLK) for a in range(g)]
            qcats = [_stack_heads(q_ref[rows[a], :] * QK_SCALE, masks) for a in range(g)]
            def step(t, cc):
                cur = [qk(qcats[a], t) for a in range(g)]
                return tuple(consume(cur[a], t, cc[a], g, False) for a in range(g))

            init = (neg, neg, z1, z1, jnp.zeros((BLK, LANES), F32))
            done = lax.fori_loop(0, ng, step, tuple(init for a in range(g)))
            last = [qk(qcats[a], ng, a + 1) for a in range(g)]
            for a in range(g):
                ss, state = last[a], done[a]
                m0, m1, l0, l1, acc = consume(ss, ng, state, a + 1, True)
                o_ref[rows[a], :] = acc / jnp.where(masks[0], l0, l1)
                lse_ref[rows[a], :] = jnp.where(masks[0], m0 + jnp.log(l0), m1 + jnp.log(l1))
            return c

        lax.fori_loop(0, NBLK // g, gbody, 0)

    cols, ospec, crspec = _fox_specs()
    osd = jax.ShapeDtypeStruct((nb, SEQ, FOX_W), F32)
    return pl.pallas_call(
        body, out_shape=(osd, osd), grid=(nb, FOX_W // LANES), in_specs=cols + [crspec], out_specs=(ospec, ospec),
        compiler_params=_cparams(dimension_semantics=("parallel", "parallel")), name=name,
    )(p3, p3, p3, crow)


def _fox_bwd(p3, crow, do, o, lse, *, do_off, name):
    nb = p3.shape[0]
    g = FOX_GROUP
    hd = HEAD_DIM

    def body(q_ref, k_ref, v_ref, cr_ref, do_ref, o_ref, lse_ref, dq_ref, dko_ref, dvo_ref, dcr_ref, dk_ref, dv_ref):
        masks = _head_masks(2)
        tri = _tri_bias(True)
        dk_ref[...] = jnp.zeros_like(dk_ref)
        dv_ref[...] = jnp.zeros_like(dv_ref)
        dcr_ref[...] = jnp.zeros_like(dcr_ref)

        def products(qcat, docat, t, nblk=g):
            out = []
            for e in range(nblk):
                krows = _fox_key_rows(t, e)
                out.append(lax.dot_general(qcat, k_ref[krows, :], _NT, preferred_element_type=F32))
                out.append(lax.dot_general(docat, v_ref[krows, :], _NT, preferred_element_type=F32))
            return tuple(out)

        def consume(prod, t, ctx, dq, nblk, diag):
            qcat, docat, lse0, lse1, dl0, dl1 = ctx
            for e in range(nblk):
                jb = g * t + e
                krows = _fox_key_rows(t, e)
                s, dp = prod[2 * e], prod[2 * e + 1]
                cr = cr_ref[jb]
                u0 = s[:BLK] - cr[0:1, :]
                u1 = s[BLK:] - cr[1:2, :]
                if diag and e == nblk - 1:
                    u0, u1 = u0 + tri, u1 + tri
                p0 = jnp.exp(u0 - lse0)
                p1 = jnp.exp(u1 - lse1)
                ds0 = p0 * (dp[:BLK] - dl0)
                ds1 = p1 * (dp[BLK:] - dl1)
                dcr_ref[jb, 0:1, :] += jnp.sum(ds0, axis=0, keepdims=True)
                dcr_ref[jb, 1:2, :] += jnp.sum(ds1, axis=0, keepdims=True)
                ds0b, ds1b = ds0.astype(BF16), ds1.astype(BF16)
                pcat = jnp.concatenate([p0.astype(BF16), p1.astype(BF16)], axis=0)
                dscat = jnp.concatenate([ds0b, ds1b], axis=0)
                dv_ref[krows, :] += lax.dot_general(pcat, docat, _T0, preferred_element_type=F32)
                dk_ref[krows, :] += lax.dot_general(dscat, qcat, _T0, preferred_element_type=F32)
                dsrow = jnp.concatenate([ds0b, ds1b], axis=1)
                dq = dq + jnp.dot(dsrow, _stack_heads(k_ref[krows, :] * QK_SCALE, masks), preferred_element_type=F32)
            return dq

        def gbody(ng, c):
            ctxs, rows = [], []
            for a in range(g):
                r = pl.ds(pl.multiple_of((g * ng + a) * BLK, BLK), BLK)
                qcat = _stack_heads(q_ref[r, :] * QK_SCALE, masks)
                dob = do_ref[r, :].astype(BF16)
                prod = dob.astype(F32) * o_ref[r, :]
                z = jnp.zeros_like(prod)
                dl0 = jnp.sum(jnp.where(masks[0], prod, z), axis=1, keepdims=True)
                dl1 = jnp.sum(jnp.where(masks[1], prod, z), axis=1, keepdims=True)
                lseb = lse_ref[r, :]
                ctxs.append((qcat, _stack_heads(dob, masks), lseb[:, 0:1], lseb[:, hd:hd + 1], dl0, dl1))
                rows.append(r)
            def step(t, cc):
                cur = [products(ctxs[a][0], ctxs[a][1], t) for a in range(g)]
                return tuple(consume(cur[a], t, ctxs[a], cc[a], g, False) for a in range(g))

            done = lax.fori_loop(0, ng, step, tuple(jnp.zeros((BLK, LANES), F32) for a in range(g)))
            last = [products(ctxs[a][0], ctxs[a][1], ng, a + 1) for a in range(g)]
            for a in range(g):
                dq_ref[rows[a], :] = consume(last[a], ng, ctxs[a], done[a], a + 1, True).astype(dq_ref.dtype)
            return c

        lax.fori_loop(0, NBLK // g, gbody, 0)
        dko_ref[...] = dk_ref[...].astype(dko_ref.dtype)
        dvo_ref[...] = dv_ref[...].astype(dvo_ref.dtype)

    cols, ospec, crspec = _fox_specs()
    dospec = pl.BlockSpec((None, SEQ, LANES), lambda b, j: (b, 0, do_off + j))
    osd = jax.ShapeDtypeStruct((nb, SEQ, FOX_W), BF16)
    return pl.pallas_call(
        body, out_shape=(osd, osd, osd, jax.ShapeDtypeStruct((nb, FOX_W // LANES, NBLK, 8, BLK), F32)),
        grid=(nb, FOX_W // LANES), in_specs=cols + [crspec, dospec, ospec, ospec], out_specs=(ospec, ospec, ospec, crspec),
        scratch_shapes=[pltpu.VMEM((SEQ, LANES), F32)] * 2,
        compiler_params=_cparams(dimension_semantics=("parallel", "parallel")), name=name,
    )(p3, p3, p3, crow, do, o, lse)


_B1, _B2 = FOX_W // LANES, (FOX_W + DIL_W) // LANES


def _dy_gate_bwd(dx2b, wo, fox, dil, memo, p16, *, tm, tn, name):
    t, d = dx2b.shape
    assert FOX_W % tn == 0 and DIL_W % tn == 0 and MEM_W % tn == 0 and all(c % tn == 0 for c in (C_FG, C_DG, C_MG))
    n1, n2, n3 = FOX_W // tn, (FOX_W + DIL_W) // tn, MIX_W // tn

    def body(dx_ref, w_ref, f_ref, d_ref, m_ref, g_ref, da_ref, dg_ref):
        j = pl.program_id(1)
        wv = w_ref[...]
        for c0 in range(0, tm, min(tm, 2 * MM_CHUNK)):
            rows = pl.ds(c0, min(tm, 2 * MM_CHUNK))
            dyv = lax.dot_general(dx_ref[rows, :], wv, _NT, preferred_element_type=F32)
            a = jnp.where(j < n1, f_ref[rows, :], jnp.where(j < n2, d_ref[rows, :], m_ref[rows, :]))
            gt = g_ref[rows, :].astype(F32)
            sg = 1.0 / (1.0 + jnp.exp(-gt))
            da_ref[rows, :] = (dyv * gt * sg).astype(da_ref.dtype)
            dg_ref[rows, :] = (dyv * a * sg * (1.0 + gt * (1.0 - sg))).astype(dg_ref.dtype)

    def gcol(j):
        return jnp.where(j < n1, C_FG // tn + j, jnp.where(j < n2, C_DG // tn + j - n1, C_MG // tn + j - n2))

    tile = pl.BlockSpec((tm, tn), lambda i, j: (i, j))
    return pl.pallas_call(
        body,
        out_shape=(jax.ShapeDtypeStruct((t, MIX_W), BF16), jax.ShapeDtypeStruct((t, MIX_W), BF16)),
        grid=(t // tm, n3),
        in_specs=[pl.BlockSpec((tm, d), lambda i, j: (i, 0)), pl.BlockSpec((tn, d), lambda i, j: (j, 0)),
                  pl.BlockSpec((tm, tn), lambda i, j: (i, jnp.minimum(j, n1 - 1))),
                  pl.BlockSpec((tm, tn), lambda i, j: (i, jnp.clip(j - n1, 0, n2 - n1 - 1))),
                  pl.BlockSpec((tm, tn), lambda i, j: (i, jnp.clip(j - n2, 0, n3 - n2 - 1))),
                  pl.BlockSpec((tm, tn), lambda i, j: (i, gcol(j)))],
        out_specs=(tile, tile),
        compiler_params=_cparams(dimension_semantics=("parallel", "parallel")),
        name=name,
    )(dx2b, wo, fox, dil, memo, p16)


def _silu(g):
    return g / (1.0 + jnp.exp(-g))


def _out_loss(fox, dil, memo, p16, wo, x, tgt, gfin, *, tm, name):
    t, d = x.shape
    n_feat = float(d)

    def body(f_ref, d_ref, m_ref, fg_ref, dg_ref, mg_ref, w_ref, x_ref, t_ref, g_ref, y_ref, dx_ref, dxb_ref, st_ref):
        i = pl.program_id(0)

        @pl.when(i == 0)
        def _():
            st_ref[...] = jnp.zeros_like(st_ref)

        wv, gv = w_ref[...], g_ref[...]
        half = tm // 2
        for c0 in (0, half):
            rows = pl.ds(c0, half)
            y = jnp.concatenate([(a_ref[rows, :] * _silu(gt_ref[rows, :].astype(F32))).astype(BF16)
                                 for a_ref, gt_ref in ((f_ref, fg_ref), (d_ref, dg_ref), (m_ref, mg_ref))], axis=1)
            y_ref[rows, :] = y
            x2 = x_ref[rows, :] + jnp.dot(y, wv, preferred_element_type=F32)
            r = lax.rsqrt(jnp.mean(x2 * x2, axis=-1, keepdims=True) + RMS_EPS)
            nrm = x2 * r
            err = nrm * gv - t_ref[rows, :]
            dout = err * (1.0 / n_feat)
            dn = dout * gv
            dx2 = r * (dn - nrm * jnp.mean(dn * nrm, axis=-1, keepdims=True))
            dx_ref[rows, :] = dx2
            dxb_ref[rows, :] = dx2.astype(dxb_ref.dtype)
            st_ref[0:1, :] += jnp.sum(dout * nrm, axis=0, keepdims=True)
            st_ref[1:2, :] += (0.5 / n_feat) * jnp.sum(err * err, axis=0, keepdims=True)

    row = pl.BlockSpec((tm, d), lambda i: (i, 0))
    whole = lambda w: pl.BlockSpec((tm, w), lambda i: (i, 0))
    gate = lambda w, col: pl.BlockSpec((tm, w), lambda i: (i, col // w))
    return pl.pallas_call(
        body,
        out_shape=(jax.ShapeDtypeStruct((t, MIX_W), BF16), jax.ShapeDtypeStruct((t, d), F32), jax.ShapeDtypeStruct((t, d), BF16),
                   jax.ShapeDtypeStruct((8, d), F32)),
        grid=(t // tm,),
        in_specs=[whole(FOX_W), whole(DIL_W), whole(MEM_W), gate(FOX_W, C_FG), gate(DIL_W, C_DG), gate(MEM_W, C_MG),
                  pl.BlockSpec((MIX_W, d), lambda i: (0, 0)), row, row, pl.BlockSpec((1, d), lambda i: (0, 0))],
        out_specs=(pl.BlockSpec((tm, MIX_W), lambda i: (i, 0)), row, row, pl.BlockSpec((8, d), lambda i: (0, 0))),
        compiler_params=_cparams(dimension_semantics=("arbitrary",)),
        name=name,
    )(fox, dil, memo, p16, p16, p16, wo, x, tgt, gfin)


def _dh_rms_bwd(dp, w, x, g, resid, *, tm, name):
    t, d = x.shape
    kdim = dp.shape[1]

    def body(*refs):
        if resid is not None:
            dp_ref, w_ref, x_ref, g_ref, r_ref, dx_ref, gg_ref = refs
        else:
            dp_ref, w_ref, x_ref, g_ref, dx_ref, gg_ref = refs

        @pl.when(pl.program_id(0) == 0)
        def _():
            gg_ref[...] = jnp.zeros_like(gg_ref)

        dh = lax.dot_general(dp_ref[...], w_ref[...], _NT, preferred_element_type=F32)
        xv = x_ref[...]
        r = lax.rsqrt(jnp.mean(xv * xv, axis=-1, keepdims=True) + RMS_EPS)
        nrm = xv * r
        dn = dh * g_ref[...]
        dx = r * (dn - nrm * jnp.mean(dn * nrm, axis=-1, keepdims=True))
        if resid is not None:
            dx = dx + r_ref[...]
        dx_ref[...] = dx
        gg_ref[0:1, :] += jnp.sum(dh * nrm, axis=0, keepdims=True)

    row = pl.BlockSpec((tm, d), lambda i: (i, 0))
    in_specs = [pl.BlockSpec((tm, kdim), lambda i: (i, 0)),
                pl.BlockSpec((d, kdim), lambda i: (0, 0), pipeline_mode=pl.Buffered(1)), row,
                pl.BlockSpec((1, d), lambda i: (0, 0))]
    args = [dp, w, x, g]
    if resid is not None:
        in_specs.append(row)
        args.append(resid)
    return pl.pallas_call(
        body,
        out_shape=(jax.ShapeDtypeStruct((t, d), F32), jax.ShapeDtypeStruct((8, d), F32)),
        grid=(t // tm,),
        in_specs=in_specs,
        out_specs=(row, pl.BlockSpec((8, d), lambda i: (0, 0))),
        compiler_params=_cparams(dimension_semantics=("arbitrary",)),
        name=name,
    )(*args)


_FLOG0 = 4 * FOX_W
_W_IN_SEGMENTS = ((0, _FLOG0, 0), (_FLOG0, _FLOG0 + FOX_HEADS, PW), (_FLOG0 + FOX_HEADS, IN_W, C_DQ))
SHARD_W = IN_W // N_CHIPS


def _rearrange_w_in(shards):
    def cols(lo, hi):
        parts = []
        for k in range(N_CHIPS):
            a, b = max(lo, k * SHARD_W), min(hi, (k + 1) * SHARD_W)
            if a < b:
                parts.append(shards[k][:, a - k * SHARD_W:b - k * SHARD_W])
        return parts

    (a0, a1, _), (f0, f1, _), (b0, b1, _) = _W_IN_SEGMENTS
    pad = jnp.zeros((shards[0].shape[0], PWF - PW - FOX_HEADS), shards[0].dtype)
    return jnp.concatenate(cols(a0, a1) + cols(b0, b1) + cols(f0, f1) + [pad], axis=1)


def _w_in_grad_slabs(g):
    slabs = []
    for k in range(N_CHIPS):
        parts = []
        for lo, hi, at in _W_IN_SEGMENTS:
            a, b = max(lo, k * SHARD_W), min(hi, (k + 1) * SHARD_W)
            if a < b:
                parts.append(g[:, at + a - lo:at + b - lo])
        slabs.append(jnp.concatenate(parts, axis=1))
    return jnp.stack(slabs, axis=0)


def _local_grads(x, mem, norm_g, w_r, b_forget, mem_norm_g, w_kv, w_o, final_norm_g, tgt, start_reduce=None,
                 start_reduce_small=None, early_token=None, late_weights=None):
    nb = x.shape[0]
    t = nb * SEQ
    x2d = x.reshape(t, D_MODEL)
    tgt2d = tgt.reshape(t, D_MODEL)
    tabs = _rope_tables()
    bpad = jnp.pad(b_forget.reshape(1, FOX_HEADS), ((0, 0), (0, LANES - FOX_HEADS)))

    gain0 = norm_g.reshape(1, D_MODEL)
    if early_token is not None:
        gain0 = gain0 + early_token[0:1, 0:1]
    h, p16, dqkv, flog = _proj(x2d, gain0, w_r, tabs, n=PWF, tm=1024, tn=768, name="proj")
    c12 = _flog_fwd(flog, bpad, nb=nb, ts=256, name="flog_fwd")

    crow = c12[:, :FOX_HEADS].reshape(nb, NBLK, BLK, FOX_HEADS // 2, 2).transpose(0, 3, 1, 4, 2)
    crow = jnp.pad(crow, ((0, 0), (0, 0), (0, 0), (0, 6), (0, 0)))
    p3 = p16.reshape(nb, SEQ, PWF)
    fox, fox_lse = _fox_fwd(p3, crow, name="fox_fwd")
    if late_weights is not None:
        w_kv, w_o = late_weights(fox_lse)

    dqkv3 = dqkv.reshape(nb, SEQ, 3 * DIL_W)
    dil, dil_lse = _dil_fwd(dqkv3, name="dil_fwd")

    mh = _rms_fwd(mem.reshape(nb * MEM_LEN, D_MODEL), mem_norm_g.reshape(1, D_MODEL), tm=nb * MEM_LEN, name="rms_mem")
    mkv = _matmul(mh, w_kv, out_dtype=BF16, tm=nb * MEM_LEN, tn=512, tk=D_MODEL, name="mem_kv")
    mkv3 = mkv.reshape(nb, MEM_LEN, 2 * MEM_W)
    memo, mem_lse = _mem_fwd(p3, mkv3, qoff=C_MQ // LANES, name="mem_fwd")

    fox2, dil2, memo2 = fox.reshape(t, FOX_W), dil.reshape(t, DIL_W), memo.reshape(t, MEM_W)
    y, dx2, dx2b, st = _out_loss(fox2, dil2, memo2, p16, w_o, x2d, tgt2d, final_norm_g.reshape(1, D_MODEL), tm=256,
                                 name="out_loss")

    g_wo = _matmul(y, dx2b, mode="tn", out_dtype=BF16, tm=1024, tn=512, tk=t, name="grad_w_out")
    datt, dgate = _dy_gate_bwd(dx2b, w_o, fox2, dil2, memo2, p16, tm=2048, tn=256, name="dy_gate_bwd")
    datt3 = datt.reshape(nb, SEQ, MIX_W)

    dmq, dmk, dmv = _mem_bwd(p3, mkv3, datt3, memo, mem_lse, qoff=C_MQ // LANES, do_off=_B2, name="mem_bwd")
    dmkv = jnp.concatenate([dmk, dmv], axis=-1).reshape(nb * MEM_LEN, 2 * MEM_W).astype(BF16)
    g_wkv = _matmul(mh, dmkv, mode="tn", out_dtype=BF16, tm=512, tn=512, tk=nb * MEM_LEN, name="grad_w_kv")
    mem_gain = mem_norm_g.reshape(1, D_MODEL)
    if start_reduce_small is not None:
        tok = start_reduce_small(g_wkv, g_wo)[0:1, 0:1]
        mem_gain, crow = mem_gain + tok, crow + tok
    _, gmn = _dh_rms_bwd(dmkv, w_kv, mem.reshape(nb * MEM_LEN, D_MODEL), mem_gain, None, tm=nb * MEM_LEN, name="mem_rms_bwd")

    dfq, dfk, dfv, dcr = _fox_bwd(p3, crow, datt3, fox, fox_lse, do_off=0, name="fox_bwd")
    dcol = -dcr[:, :, :, :2, :].transpose(0, 2, 4, 1, 3).reshape(t, FOX_HEADS)
    dcol = jnp.pad(dcol, ((0, 0), (0, LANES - FOX_HEADS)))
    dflog, gb = _flog_bwd(dcol, flog, bpad, nb=nb, ts=256, name="flog_bwd")

    ddq, ddk, ddv = _dil_bwd(dqkv3, datt3, dil, dil_lse, tabs, do_off=_B1, name="dil_bwd")

    flat = lambda a: a.reshape(t, -1)
    dp = jnp.concatenate([flat(dfq), flat(dfk), flat(dfv), dgate[:, :FOX_W], flat(ddq), flat(ddk), flat(ddv),
                          dgate[:, FOX_W:FOX_W + DIL_W], flat(dmq), dgate[:, FOX_W + DIL_W:], dflog,
                          jnp.zeros((t, PWF - PW - LANES), BF16)], axis=1)
    g_wr = _matmul(h, dp, mode="tn", out_dtype=BF16, tm=D_MODEL, tn=768, tk=t, name="grad_w_in")
    gain = norm_g.reshape(1, D_MODEL)
    if start_reduce is not None:
        gain = gain + start_reduce(g_wr)[0:1, 0:1]
    gx, gng = _dh_rms_bwd(dp, w_r, x2d, gain, dx2, tm=256, name="in_rms_bwd")

    gb_row = jnp.pad(gb[0:1, :], ((0, 0), (0, D_MODEL - LANES)))
    small = jnp.concatenate([gng[0:1], gmn[0:1], st[0:1], gb_row, st[1:2], jnp.zeros((3, D_MODEL), F32)], axis=0)
    return gx.reshape(nb, SEQ, D_MODEL), g_wr, g_wkv, g_wo, small


MESH = pl.DeviceIdType.MESH
ANY = pl.BlockSpec(memory_space=pl.ANY)


def _place():
    x, y, c = lax.axis_index("x"), lax.axis_index("y"), lax.axis_index("c")
    other_chips = [(1 - x, y), (x, 1 - y), (1 - x, 1 - y)]
    return x, y, c, other_chips


def _gather_weights(shards):
    n = len(shards)

    def body(*refs):
        in_refs, out_refs = refs[:n], refs[n:2 * n]
        send_sems, recv_sems = refs[2 * n:]
        x, y, c, chips = _place()
        me_chip = 2 * x + y
        sibling = (x, y, 1 - c)

        def half(ref, pc, rows):
            return ref.at[pl.ds(pc * (rows // 2), rows // 2), :]

        def rcopy(k, src, dst, to):
            return pltpu.make_async_remote_copy(src_ref=src, dst_ref=dst, send_sem=send_sems.at[k], recv_sem=recv_sems.at[k],
                                                device_id=to, device_id_type=MESH)

        sends = []
        for t in range(n):
            rows = shards[t].shape[0]
            for j, chip in enumerate(chips):
                cp = rcopy(6 * t + j, half(in_refs[t], c, rows), half(out_refs[t].at[me_chip], c, rows), (*chip, c))
                cp.start()
                sends.append(cp)
        for t in range(n):
            rows = shards[t].shape[0]
            for j, chip in enumerate(chips):
                slot = out_refs[t].at[2 * chip[0] + chip[1]]
                rcopy(6 * t + j, half(slot, c, rows), half(slot, c, rows), sibling).wait_recv()
                fw = rcopy(6 * t + 3 + j, half(slot, c, rows), half(slot, c, rows), sibling)
                fw.start()
                sends.append(fw)
        for t in range(n):
            rows = shards[t].shape[0]
            for j, chip in enumerate(chips):
                slot = out_refs[t].at[2 * chip[0] + chip[1]]
                rcopy(6 * t + 3 + j, half(slot, 1 - c, rows), half(slot, 1 - c, rows), sibling).wait_recv()
        for cp in sends:
            cp.wait_send()

    return pl.pallas_call(
        body,
        out_shape=tuple(jax.ShapeDtypeStruct((N_CHIPS,) + s.shape, s.dtype) for s in shards),
        in_specs=[ANY] * n,
        out_specs=tuple([ANY] * n),
        scratch_shapes=[pltpu.SemaphoreType.DMA((6 * n,)), pltpu.SemaphoreType.DMA((6 * n,))],
        name="gather_weights",
    )(*shards)


def _pair_exchange(gs, *, name):
    n = len(gs)

    def body(*refs):
        g_refs, r_refs = refs[:n], refs[n:2 * n]
        send_sems, recv_sems = refs[2 * n:]
        x, y, c, _ = _place()
        cps = []
        for t in range(n):
            hr = gs[t].shape[1] // 2
            cp = pltpu.make_async_remote_copy(src_ref=g_refs[t].at[:, pl.ds((1 - c) * hr, hr), :], dst_ref=r_refs[t],
                                              send_sem=send_sems.at[t], recv_sem=recv_sems.at[t],
                                              device_id=(x, y, 1 - c), device_id_type=MESH)
            cp.start()
            cps.append(cp)
        for cp in cps:
            cp.wait()

    return pl.pallas_call(
        body,
        out_shape=tuple(jax.ShapeDtypeStruct((g.shape[0], g.shape[1] // 2, g.shape[2]), g.dtype) for g in gs),
        in_specs=[ANY] * n,
        out_specs=tuple([ANY] * n),
        scratch_shapes=[pltpu.SemaphoreType.DMA((n,)), pltpu.SemaphoreType.DMA((n,))],
        name=name,
    )(*gs)


_HBM = pl.BlockSpec(memory_space=pltpu.HBM)
_SEM = pl.BlockSpec(memory_space=pltpu.SEMAPHORE)
_DATAFLOW = pltpu.SideEffectType.DATAFLOW_SIDE_EFFECTING


def _chip_copies(p_refs, land_refs, send_sems, recv_sems):
    x, y, c, chips = _place()
    me_chip = 2 * x + y
    return [pltpu.make_async_remote_copy(src_ref=p_refs[t].at[2 * chip[0] + chip[1]], dst_ref=land_refs[t].at[me_chip],
                                         send_sem=send_sems.at[3 * t + j], recv_sem=recv_sems.at[3 * t + j],
                                         device_id=(*chip, c), device_id_type=MESH)
            for t in range(len(p_refs)) for j, chip in enumerate(chips)]


def _chip_exchange_start(ps, *, tag):
    n = len(ps)

    def body(*refs):
        p_refs, land_refs = refs[:n], refs[n:2 * n]
        send_sems, recv_sems = refs[2 * n:2 * n + 2]
        token = refs[-1]
        for cp in _chip_copies(p_refs, land_refs, send_sems, recv_sems):
            cp.start()
        token[...] = jnp.zeros_like(token)

    hbm = [pltpu.HBM(p.shape, p.dtype) for p in ps]
    args = [pltpu.with_memory_space_constraint(p, pltpu.HBM) for p in ps]
    args += [pltpu.with_memory_space_constraint(lax.empty(p.shape, p.dtype), pltpu.HBM) for p in ps]
    out = pl.pallas_call(
        body,
        name=f"chip_exchange_start_{tag}",
        out_shape=(pltpu.SemaphoreType.DMA((3 * n,)), pltpu.SemaphoreType.DMA((3 * n,)), *hbm, *hbm,
                   jax.ShapeDtypeStruct((8, LANES), F32)),
        in_specs=[_HBM] * (2 * n),
        out_specs=(_SEM, _SEM, *([_HBM] * (2 * n)), pl.BlockSpec(memory_space=pltpu.VMEM)),
        input_output_aliases={i: 2 + i for i in range(2 * n)},
        compiler_params=pltpu.CompilerParams(has_side_effects=_DATAFLOW),
    )(*args)
    return out[0], out[1], out[2:2 + n], out[2 + n:2 + 2 * n], out[-1]


def _chip_exchange_wait(send_sems, recv_sems, p_thru, land_thru, after, *, tag):
    n = len(p_thru)

    def body(*refs):
        p_refs, land_refs = refs[:n], refs[n:2 * n]
        ssem, rsem = refs[2 * n:2 * n + 2]
        for cp in _chip_copies(p_refs, land_refs, ssem, rsem):
            cp.wait_send()
            cp.wait_recv()

    hbm = [pltpu.HBM(p.shape, p.dtype) for p in p_thru]
    out = pl.pallas_call(
        body,
        name=f"chip_exchange_wait_{tag}",
        out_shape=(*hbm, *hbm),
        in_specs=[_HBM] * (2 * n) + [_SEM, _SEM, ANY],
        out_specs=tuple([_HBM] * (2 * n)),
        input_output_aliases={i: i for i in range(2 * n)},
        compiler_params=pltpu.CompilerParams(has_side_effects=_DATAFLOW),
    )(*p_thru, *land_thru, send_sems, recv_sems, after)
    return out[:n], out[n:]


def _shard_copies(s_refs, land_refs, send_sems, recv_sems):
    x, y, c, chips = _place()
    me_chip = 2 * x + y
    return [pltpu.make_async_remote_copy(src_ref=s_refs[t], dst_ref=land_refs[t].at[me_chip],
                                         send_sem=send_sems.at[3 * t + j], recv_sem=recv_sems.at[3 * t + j],
                                         device_id=(*chip, c), device_id_type=MESH)
            for t in range(len(s_refs)) for j, chip in enumerate(chips)]


def _gather_late_start(shards):
    n = len(shards)

    def body(*refs):
        s_refs, land_refs = refs[:n], refs[n:2 * n]
        send_sems, recv_sems = refs[2 * n:2 * n + 2]
        token = refs[-1]
        for cp in _shard_copies(s_refs, land_refs, send_sems, recv_sems):
            cp.start()
        token[...] = jnp.zeros_like(token)

    lands = [(N_CHIPS,) + s.shape for s in shards]
    args = [pltpu.with_memory_space_constraint(s, pltpu.HBM) for s in shards]
    args += [pltpu.with_memory_space_constraint(lax.empty(shp, s.dtype), pltpu.HBM) for shp, s in zip(lands, shards)]
    out = pl.pallas_call(
        body,
        name="gather_late_start",
        out_shape=(pltpu.SemaphoreType.DMA((3 * n,)), pltpu.SemaphoreType.DMA((3 * n,)),
                   *[pltpu.HBM(s.shape, s.dtype) for s in shards], *[pltpu.HBM(shp, s.dtype) for shp, s in zip(lands, shards)],
                   jax.ShapeDtypeStruct((8, LANES), F32)),
        in_specs=[_HBM] * (2 * n),
        out_specs=(_SEM, _SEM, *([_HBM] * (2 * n)), pl.BlockSpec(memory_space=pltpu.VMEM)),
        input_output_aliases={i: 2 + i for i in range(2 * n)},
        compiler_params=pltpu.CompilerParams(has_side_effects=_DATAFLOW),
    )(*args)
    return out[0], out[1], out[2:2 + n], out[2 + n:2 + 2 * n], out[-1]


def _gather_late_wait(send_sems, recv_sems, s_thru, land_thru, after):
    n = len(s_thru)

    def body(*refs):
        s_refs, land_refs = refs[:n], refs[n:2 * n]
        ssem, rsem = refs[2 * n:2 * n + 2]
        for cp in _shard_copies(s_refs, land_refs, ssem, rsem):
            cp.wait_send()
            cp.wait_recv()

    out = pl.pallas_call(
        body,
        name="gather_late_wait",
        out_shape=(*[pltpu.HBM(s.shape, s.dtype) for s in s_thru], *[pltpu.HBM(l.shape, l.dtype) for l in land_thru]),
        in_specs=[_HBM] * (2 * n) + [_SEM, _SEM, ANY],
        out_specs=tuple([_HBM] * (2 * n)),
        input_output_aliases={i: i for i in range(2 * n)},
        compiler_params=pltpu.CompilerParams(has_side_effects=_DATAFLOW),
    )(*s_thru, *land_thru, send_sems, recv_sems, after)
    return out[:n], out[n:]


def _pair_swap(rs):
    n = len(rs)

    def body(*refs):
        r_refs, o_refs = refs[:n], refs[n:2 * n]
        send_sems, recv_sems = refs[2 * n:]
        x, y, c, _ = _place()
        cps = []
        for t in range(n):
            cp = pltpu.make_async_remote_copy(src_ref=r_refs[t], dst_ref=o_refs[t], send_sem=send_sems.at[t],
                                              recv_sem=recv_sems.at[t], device_id=(x, y, 1 - c), device_id_type=MESH)
            cp.start()
            cps.append(cp)
        for cp in cps:
            cp.wait()

    return pl.pallas_call(
        body,
        out_shape=tuple(jax.ShapeDtypeStruct(r.shape, r.dtype) for r in rs),
        in_specs=[ANY] * n,
        out_specs=tuple([ANY] * n),
        scratch_shapes=[pltpu.SemaphoreType.DMA((n,)), pltpu.SemaphoreType.DMA((n,))],
        name="pair_swap",
    )(*rs)


N_DEV = 8
LOSS_ROW = 4


def _small_allreduce(small):
    def body(s_ref, o_ref, all_ref, send_sems, recv_sems):
        x, y, c, _ = _place()
        me = 4 * x + 2 * y + c
        all_ref[me] = s_ref[...]
        cps = []
        for k in range(1, N_DEV):
            peer = tuple(1 - p if (k >> s) & 1 else p for p, s in ((x, 2), (y, 1), (c, 0)))
            cp = pltpu.make_async_remote_copy(src_ref=s_ref, dst_ref=all_ref.at[me], send_sem=send_sems.at[k - 1],
                                              recv_sem=recv_sems.at[k - 1], device_id=peer, device_id_type=MESH)
            cp.start()
            cps.append(cp)
        for cp in cps:
            cp.wait()
        tot = all_ref[0]
        for d in range(1, N_DEV):
            tot = tot + all_ref[d]
        o_ref[...] = tot
        o_ref[LOSS_ROW:LOSS_ROW + 1, :] = jnp.broadcast_to(jnp.sum(tot[LOSS_ROW:LOSS_ROW + 1, :], axis=1, keepdims=True),
                                                          (1, tot.shape[1]))

    vm = pl.BlockSpec(memory_space=pltpu.VMEM)
    return pl.pallas_call(
        body,
        out_shape=jax.ShapeDtypeStruct(small.shape, small.dtype),
        in_specs=[vm],
        out_specs=vm,
        scratch_shapes=[pltpu.VMEM((N_DEV,) + small.shape, small.dtype), pltpu.SemaphoreType.DMA((N_DEV - 1,)),
                        pltpu.SemaphoreType.DMA((N_DEV - 1,))],
        name="small_allreduce",
    )(small)


def _sum_pair(g, recv, cidx, *, tr, name):
    n, hr, cols = recv.shape
    nr = hr // tr

    def body(c_ref, g_ref, r_ref, o_ref):
        o_ref[...] = (g_ref[...].astype(F32) + r_ref[...].astype(F32)).astype(o_ref.dtype)

    grid_spec = pltpu.PrefetchScalarGridSpec(
        num_scalar_prefetch=1,
        grid=(n, nr),
        in_specs=[pl.BlockSpec((None, tr, cols), lambda k, i, c_ref: (k, c_ref[0] * nr + i, 0)),
                  pl.BlockSpec((None, tr, cols), lambda k, i, c_ref: (k, i, 0))],
        out_specs=pl.BlockSpec((None, tr, cols), lambda k, i, c_ref: (k, i, 0)),
    )
    return pl.pallas_call(body, out_shape=jax.ShapeDtypeStruct(recv.shape, BF16), grid_spec=grid_spec,
                          compiler_params=_cparams(), name=name)(cidx, g, recv)


def _sum_chips(p, *, tr, name):
    _, rows, cols = p.shape

    def body(p_ref, o_ref):
        tot = p_ref[0].astype(F32)
        for k in range(1, N_CHIPS):
            tot = tot + p_ref[k].astype(F32)
        o_ref[...] = tot

    return pl.pallas_call(
        body,
        out_shape=jax.ShapeDtypeStruct((rows, cols), F32),
        grid=(rows // tr,),
        in_specs=[pl.BlockSpec((N_CHIPS, tr, cols), lambda i: (0, i, 0))],
        out_specs=pl.BlockSpec((tr, cols), lambda i: (i, 0)),
        compiler_params=_cparams(),
        name=name,
    )(p)


def _adamw(w, g, m, v, *, tr, name):
    rows, cols = w.shape
    bc1 = 1.0 / (1.0 - ADAM_B1 ** ADAM_STEP)
    bc2 = 1.0 / (1.0 - ADAM_B2 ** ADAM_STEP)

    def body(w_ref, g_ref, m_ref, v_ref, d_ref, nm_ref, nv_ref):
        gv = g_ref[...]
        nm = ADAM_B1 * m_ref[...] + (1.0 - ADAM_B1) * gv
        nv = ADAM_B2 * v_ref[...] + (1.0 - ADAM_B2) * (gv * gv)
        d_ref[...] = -ADAM_LR * ((nm * bc1) / (jnp.sqrt(nv * bc2) + ADAM_EPS) + ADAM_WD * w_ref[...])
        nm_ref[...] = nm
        nv_ref[...] = nv

    spec = pl.BlockSpec((tr, cols), lambda i: (i, 0))
    sd = jax.ShapeDtypeStruct((rows, cols), F32)
    return pl.pallas_call(body, out_shape=(sd, sd, sd), grid=(rows // tr,), in_specs=[spec] * 4, out_specs=(spec,) * 3,
                          compiler_params=_cparams(), name=name)(w, g, m, v)


def _adamw_halves(w, own, sib, cidx, m, v, *, tr, name):
    rows, cols = w.shape
    hr = own.shape[0]
    nr = hr // tr
    assert rows == 2 * hr and hr % tr == 0
    bc1 = 1.0 / (1.0 - ADAM_B1 ** ADAM_STEP)
    bc2 = 1.0 / (1.0 - ADAM_B2 ** ADAM_STEP)

    def body(c_ref, w_ref, o_ref, s_ref, m_ref, v_ref, g_ref, d_ref, nm_ref, nv_ref):
        mine = (pl.program_id(0) // nr) == c_ref[0]
        gv = jnp.where(mine, o_ref[...], s_ref[...])
        nm = ADAM_B1 * m_ref[...] + (1.0 - ADAM_B1) * gv
        nv = ADAM_B2 * v_ref[...] + (1.0 - ADAM_B2) * (gv * gv)
        g_ref[...] = gv
        d_ref[...] = -ADAM_LR * ((nm * bc1) / (jnp.sqrt(nv * bc2) + ADAM_EPS) + ADAM_WD * w_ref[...])
        nm_ref[...] = nm
        nv_ref[...] = nv

    full = pl.BlockSpec((tr, cols), lambda i, c_ref: (i, 0))
    half = pl.BlockSpec((tr, cols), lambda i, c_ref: (i % nr, 0))
    sd = jax.ShapeDtypeStruct((rows, cols), F32)
    grid_spec = pltpu.PrefetchScalarGridSpec(num_scalar_prefetch=1, grid=(rows // tr,), in_specs=[full, half, half, full, full],
                                             out_specs=(full,) * 4)
    return pl.pallas_call(body, out_shape=(sd,) * 4, grid_spec=grid_spec, compiler_params=_cparams(), name=name)(
        cidx, w, own, sib, m, v)


def _pack_small(norm, mem_norm, final_norm, b_forget):
    rows = [norm.reshape(1, D_MODEL), mem_norm.reshape(1, D_MODEL), final_norm.reshape(1, D_MODEL),
            jnp.pad(b_forget.reshape(1, FOX_HEADS), ((0, 0), (0, D_MODEL - FOX_HEADS))), jnp.zeros((4, D_MODEL), F32)]
    return jnp.concatenate(rows, axis=0)


def _unpack_small(a):
    return a[0:1], a[3:4, :FOX_HEADS], a[1:2], a[2]


def kernel(x, mem, norm_g, w_in, b_forget, mem_norm_g, w_mem_kv, w_out, final_norm_g, loss_target, m_norm_g, m_w_in, m_b_forget, m_mem_norm_g, m_w_mem_kv, m_w_out, m_final_norm_g, v_norm_g, v_w_in, v_b_forget, v_mem_norm_g, v_w_mem_kv, v_w_out, v_final_norm_g):
    core = lax.axis_index("c").astype(jnp.int32)
    me_chip = (2 * lax.axis_index("x") + lax.axis_index("y")).astype(jnp.int32)
    cidx = core.reshape(1)

    def own_slot(arr, own):
        return lax.dynamic_update_slice(arr, own[None].astype(arr.dtype), (me_chip,) + (0,) * own.ndim)

    win_b, late = w_in[0].astype(BF16), [w_mem_kv[0].astype(BF16), w_out[0].astype(BF16)]
    g_in, = _gather_weights([win_b])
    g_in, late = lax.optimization_barrier((own_slot(g_in, win_b), late))
    w_r = _rearrange_w_in([g_in[k] for k in range(N_CHIPS)])
    *late_flight, early_token = _gather_late_start(late)

    def late_weights(after):
        shards, landed = _gather_late_wait(*late_flight, after)
        g_kv, g_out = (own_slot(g, s) for g, s in zip(landed, shards))
        return g_kv.reshape(D_MODEL, 2 * MEM_W), g_out.reshape(MIX_W, D_MODEL)

    trs = (128, 128, 256)
    names = ("w_in", "w_mem_kv", "w_out")
    flights = {}

    def exchange(slabs, nms, ts, tag):
        recv = _pair_exchange(slabs, name=f"pair_exchange_{tag}")
        pair = [_sum_pair(g, r, cidx, tr=tr, name=f"sum_pair_{nm}") for g, r, tr, nm in zip(slabs, recv, ts, nms)]
        if tag == "w_in":
            pair[0] = _w_in_grad_slabs(pair[0][0])
        *flights[tag], token = _chip_exchange_start(pair, tag=tag)
        return token

    def start_reduce_small(g_wkv, g_wo):
        slabs = [g_wkv.reshape(N_CHIPS, D_MODEL // N_CHIPS, 2 * MEM_W), g_wo.reshape(N_CHIPS, MIX_W // N_CHIPS, D_MODEL)]
        return exchange(slabs, names[1:], trs[1:], "small")

    def start_reduce(g_wr):
        return exchange([g_wr[None]], names[:1], trs[:1], "w_in")

    gx, g_wr, g_wkv, g_wo, small = _local_grads(x, mem, norm_g, w_r, b_forget, mem_norm_g, None, None, final_norm_g, loss_target,
                                                start_reduce=start_reduce, start_reduce_small=start_reduce_small,
                                                early_token=early_token, late_weights=late_weights)

    pair, landed = [], []
    for tag in ("w_in", "small"):
        p, l = _chip_exchange_wait(*flights[tag], small, tag=tag)
        pair += list(p)
        landed += list(l)
    got = [lax.dynamic_update_slice(g, lax.dynamic_slice(p, (me_chip, 0, 0), (1,) + p.shape[1:]), (me_chip, 0, 0))
           for g, p in zip(landed, pair)]
    red = [_sum_chips(p, tr=tr, name=f"sum_chips_{nm}") for p, tr, nm in zip(got, trs, names)]
    sib = _pair_swap(red)

    outs = {}
    for nm, r, s, w, m, v, tr in zip(names, red, sib, (w_in, w_mem_kv, w_out), (m_w_in, m_w_mem_kv, m_w_out),
                                     (v_w_in, v_w_mem_kv, v_w_out), trs):
        outs[nm] = tuple(a[None] for a in _adamw_halves(w[0], r, s, cidx, m[0], v[0], tr=tr, name=f"adamw_{nm}"))

    gsum = _small_allreduce(small)
    sd, sm, sv = _adamw(_pack_small(norm_g, mem_norm_g, final_norm_g, b_forget), gsum,
                        _pack_small(m_norm_g, m_mem_norm_g, m_final_norm_g, m_b_forget),
                        _pack_small(v_norm_g, v_mem_norm_g, v_final_norm_g, v_b_forget), tr=8, name="adamw_small")
    loss = gsum[LOSS_ROW, 0]

    def group(i, small_arr):
        ng, bf, mg, fg = _unpack_small(small_arr)
        return (ng, outs["w_in"][i], bf, mg, outs["w_mem_kv"][i], outs["w_out"][i], fg)

    return (loss, gx, *group(0, gsum), *group(1, sd), *group(2, sm), *group(3, sv))
```

```python
import functools
import math

import jax
import jax.numpy as jnp
from jax import lax
from jax.experimental import pallas as pl
from jax.experimental.pallas import tpu as pltpu

F32 = jnp.float32
BF16 = jnp.bfloat16

D_MODEL = 1024
SEQ = 2048
HEAD_DIM = 64
FOX_HEADS = 12
DIL_HEADS = 12
MEM_HEADS = 4
MEM_HEAD_DIM = 128
MEM_LEN = 256
FOX_W = FOX_HEADS * HEAD_DIM
DIL_W = DIL_HEADS * HEAD_DIM
MEM_W = MEM_HEADS * MEM_HEAD_DIM
MIX_W = FOX_W + DIL_W + MEM_W
DILATIONS = ((128, 1), (512, 4), (2048, 16))
ROPE_THETA = 500000.0
ROPE_DIM = HEAD_DIM // 4
RMS_EPS = 1e-6
NEG_INF = -1e30
IN_SIZES = [FOX_W] * 4 + [FOX_HEADS] + [DIL_W] * 4 + [MEM_W] * 2
IN_W = sum(IN_SIZES)

ADAM_LR = 0.001
ADAM_B1 = 0.9
ADAM_B2 = 0.999
ADAM_EPS = 1e-08
ADAM_WD = 0.01
ADAM_STEP = 10

LANES = 128
N_CHIPS = 4
PW = 7168
PWF = PW + 4 * LANES
C_FQ, C_FK, C_FV, C_FG = 0, 768, 1536, 2304
C_DQ, C_DK, C_DV, C_DG = 3072, 3840, 4608, 5376
C_MQ, C_MG = 6144, 6656
VMEM_LIMIT = 48 * 1024 * 1024


def _cparams(**kw):
    return pltpu.CompilerParams(vmem_limit_bytes=VMEM_LIMIT, **kw)


MM_CHUNK = 256


def _matmul(a, b, *, out_dtype, tm, tn, tk, name, mode="nn"):
    if mode == "tn":
        (kdim, m), n = a.shape, b.shape[1]
        a_spec = pl.BlockSpec((tk, tm), lambda i, j, k: (k, i))
        b_spec = pl.BlockSpec((tk, tn), lambda i, j, k: (k, j))
        dims = _T0
    elif mode == "nt":
        (m, kdim), n = a.shape, b.shape[0]
        a_spec = pl.BlockSpec((tm, tk), lambda i, j, k: (i, k))
        b_spec = pl.BlockSpec((tn, tk), lambda i, j, k: (j, k))
        dims = _NT
    else:
        (m, kdim), n = a.shape, b.shape[1]
        a_spec = pl.BlockSpec((tm, tk), lambda i, j, k: (i, k))
        b_spec = pl.BlockSpec((tk, tn), lambda i, j, k: (k, j))
        dims = (((1,), (0,)), ((), ()))
    nk = kdim // tk
    assert m % tm == 0 and n % tn == 0 and kdim % tk == 0

    def body(a_ref, b_ref, o_ref, *scratch):
        if nk == 1:
            bv = b_ref[...]
            for c0 in range(0, tm, min(tm, MM_CHUNK)):
                rows = pl.ds(c0, min(tm, MM_CHUNK))
                av = a_ref[:, rows] if mode == "tn" else a_ref[rows, :]
                o_ref[rows, :] = lax.dot_general(av, bv, dims, preferred_element_type=F32).astype(o_ref.dtype)
            return
        prod = lax.dot_general(a_ref[...], b_ref[...], dims, preferred_element_type=F32)
        acc_ref, = scratch
        k = pl.program_id(2)

        @pl.when(k == 0)
        def _():
            acc_ref[...] = prod

        @pl.when(k > 0)
        def _():
            acc_ref[...] += prod

        @pl.when(k == nk - 1)
        def _():
            o_ref[...] = acc_ref[...].astype(o_ref.dtype)

    return pl.pallas_call(
        body,
        out_shape=jax.ShapeDtypeStruct((m, n), out_dtype),
        grid=(m // tm, n // tn, nk),
        in_specs=[a_spec, b_spec],
        out_specs=pl.BlockSpec((tm, tn), lambda i, j, k: (i, j)),
        scratch_shapes=[pltpu.VMEM((tm, tn), F32)] if nk > 1 else [],
        compiler_params=_cparams(dimension_semantics=("parallel", "parallel", "arbitrary")),
        name=name,
    )(a, b)


def _rms_fwd(x, g, *, tm, name):
    t, d = x.shape

    def body(x_ref, g_ref, h_ref):
        xv = x_ref[...]
        r = lax.rsqrt(jnp.mean(xv * xv, axis=-1, keepdims=True) + RMS_EPS)
        h_ref[...] = (xv * r * g_ref[...]).astype(h_ref.dtype)

    return pl.pallas_call(
        body,
        out_shape=jax.ShapeDtypeStruct((t, d), BF16),
        grid=(t // tm,),
        in_specs=[pl.BlockSpec((tm, d), lambda i: (i, 0)), pl.BlockSpec((1, d), lambda i: (0, 0))],
        out_specs=pl.BlockSpec((tm, d), lambda i: (i, 0)),
        compiler_params=_cparams(),
        name=name,
    )(x, g)


def _rope_tables():
    half = ROPE_DIM // 2
    pos = jnp.arange(SEQ, dtype=F32)
    inv_freq = 1.0 / (ROPE_THETA ** (jnp.arange(0, ROPE_DIM, 2, dtype=F32) / ROPE_DIM))
    ang = pos[:, None] * inv_freq[None, :]
    cos, sin = jnp.cos(ang), jnp.sin(ang)
    one = jnp.ones((SEQ, HEAD_DIM - ROPE_DIM), F32)
    zero = jnp.zeros((SEQ, HEAD_DIM - ROPE_DIM), F32)
    zh = jnp.zeros((SEQ, half), F32)
    c = jnp.concatenate([cos, cos, one], axis=1)
    s1 = jnp.concatenate([zh, sin, zero], axis=1)
    s2 = jnp.concatenate([-sin, zh, zero], axis=1)
    rep = LANES // HEAD_DIM
    return jnp.tile(c, (1, rep)), jnp.tile(s1, (1, rep)), jnp.tile(s2, (1, rep))


def _rope_apply(t, c, s1, s2, transpose=False):
    n = t.shape[-1]
    rep = n // LANES
    c, s1, s2 = (jnp.tile(u, (1, rep)) for u in (c, s1, s2))
    half = ROPE_DIM // 2
    if not transpose:
        return t * c + pltpu.roll(t, half, 1) * s1 + pltpu.roll(t, n - half, 1) * s2
    return t * c + pltpu.roll(t * s1, n - half, 1) + pltpu.roll(t * s2, half, 1)


PROJ_CHUNK = 256


def _proj(x, g, w, tabs, *, n, tm, tn, name):
    t, d = x.shape
    assert C_DQ % tn == 0 and (C_DV - C_DQ) % tn == 0 and (C_DG - C_DQ) % tn == 0
    rope_lo, rope_hi, dil_hi = C_DQ // tn, C_DV // tn, C_DG // tn
    flog_blk, flog_at = PW // tn, PW % tn
    assert flog_at % LANES == 0 and flog_at + LANES <= tn
    s_blocks = SEQ // tm

    def body(x_ref, g_ref, w_ref, c_ref, s1_ref, s2_ref, h_ref, o_ref, f_ref, fl_ref, h_scr):
        j = pl.program_id(1)

        @pl.when(j == 0)
        def _():
            xv = x_ref[...]
            r = lax.rsqrt(jnp.mean(xv * xv, axis=-1, keepdims=True) + RMS_EPS)
            hv = (xv * r * g_ref[...]).astype(BF16)
            h_scr[...] = hv
            h_ref[...] = hv

        def tile(kind):
            wv = w_ref[...]
            for c0 in range(0, tm, PROJ_CHUNK):
                rows = pl.ds(c0, PROJ_CHUNK)
                acc = jnp.dot(h_scr[rows, :], wv, preferred_element_type=F32)
                if kind == "rope":
                    acc = _rope_apply(acc, c_ref[rows, :], s1_ref[rows, :], s2_ref[rows, :])
                o_ref[rows, :] = acc.astype(o_ref.dtype)
                if kind in ("rope", "dv"):
                    f_ref[rows, :] = acc
                if kind == "flog":
                    fl_ref[rows, :] = acc[:, flog_at:flog_at + LANES]

        is_rope = jnp.logical_and(j >= rope_lo, j < rope_hi)
        is_dv = jnp.logical_and(j >= rope_hi, j < dil_hi)
        is_flog = j == flog_blk
        pl.when(is_rope)(functools.partial(tile, "rope"))
        pl.when(is_dv)(functools.partial(tile, "dv"))
        pl.when(is_flog)(functools.partial(tile, "flog"))
        pl.when(jnp.logical_not(jnp.logical_or(jnp.logical_or(is_rope, is_dv), is_flog)))(functools.partial(tile, "plain"))

    tab_spec = pl.BlockSpec((tm, LANES), lambda i, j: (i % s_blocks, 0))
    f_spec = pl.BlockSpec((tm, tn), lambda i, j: (i, jnp.clip(j - rope_lo, 0, dil_hi - rope_lo - 1)))
    row = pl.BlockSpec((tm, d), lambda i, j: (i, 0))
    return pl.pallas_call(
        body,
        out_shape=(jax.ShapeDtypeStruct((t, d), BF16), jax.ShapeDtypeStruct((t, n), BF16),
                   jax.ShapeDtypeStruct((t, 3 * DIL_W), F32), jax.ShapeDtypeStruct((t, LANES), F32)),
        grid=(t // tm, n // tn),
        in_specs=[row, pl.BlockSpec((1, d), lambda i, j: (0, 0)), pl.BlockSpec((d, tn), lambda i, j: (0, j)),
                  tab_spec, tab_spec, tab_spec],
        out_specs=(row, pl.BlockSpec((tm, tn), lambda i, j: (i, j)), f_spec, pl.BlockSpec((tm, LANES), lambda i, j: (i, 0))),
        scratch_shapes=[pltpu.VMEM((tm, d), BF16)],
        compiler_params=_cparams(dimension_semantics=("parallel", "arbitrary")),
        name=name,
    )(x, g, w, *tabs)


def _split3(x):
    hi = x.astype(BF16)
    r1 = x - hi.astype(F32)
    mid = r1.astype(BF16)
    lo = (r1 - mid.astype(F32)).astype(BF16)
    return hi, mid, lo


def _dot3(sel, x, sel_is_lhs):
    out = None
    for piece in _split3(x):
        t = jnp.dot(sel, piece, preferred_element_type=F32) if sel_is_lhs else jnp.dot(piece, sel, preferred_element_type=F32)
        out = t if out is None else out + t
    return out


def _flog_fwd(flog, bpad, *, nb, ts, name):
    ns = SEQ // ts

    def body(f_ref, b_ref, c_ref, carry_ref):
        s = pl.program_id(1)

        @pl.when(s == 0)
        def _():
            carry_ref[...] = jnp.zeros_like(carry_ref)

        z = f_ref[...] + b_ref[...]
        logf = jnp.minimum(z, 0.0) - jnp.log(1.0 + jnp.exp(-jnp.abs(z)))
        r = lax.broadcasted_iota(jnp.int32, (ts, ts), 0)
        c = lax.broadcasted_iota(jnp.int32, (ts, ts), 1)
        tri = jnp.where(r >= c, 1.0, 0.0).astype(BF16)
        cs = _dot3(tri, logf, True) + carry_ref[0:1, :]
        carry_ref[...] = jnp.broadcast_to(cs[ts - 1:ts, :], carry_ref.shape)
        c_ref[...] = cs

    return pl.pallas_call(
        body,
        out_shape=jax.ShapeDtypeStruct((nb * SEQ, LANES), F32),
        grid=(nb, ns),
        in_specs=[pl.BlockSpec((ts, LANES), lambda b, s: (b * ns + s, 0)), pl.BlockSpec((1, LANES), lambda b, s: (0, 0))],
        out_specs=pl.BlockSpec((ts, LANES), lambda b, s: (b * ns + s, 0)),
        scratch_shapes=[pltpu.VMEM((8, LANES), F32)],
        compiler_params=_cparams(dimension_semantics=("parallel", "arbitrary")),
        name=name,
    )(flog, bpad)


def _flog_bwd(dcol, flog, bpad, *, nb, ts, name):
    ns = SEQ // ts

    def body(d_ref, f_ref, b_ref, o_ref, gb_ref, carry_ref):
        bi = pl.program_id(0)
        s = pl.program_id(1)

        @pl.when(s == 0)
        def _():
            carry_ref[...] = jnp.zeros_like(carry_ref)

        @pl.when(jnp.logical_and(bi == 0, s == 0))
        def _():
            gb_ref[...] = jnp.zeros_like(gb_ref)

        r = lax.broadcasted_iota(jnp.int32, (ts, ts), 0)
        c = lax.broadcasted_iota(jnp.int32, (ts, ts), 1)
        tri = jnp.where(r <= c, 1.0, 0.0).astype(BF16)
        rc = _dot3(tri, d_ref[...], True) + carry_ref[0:1, :]
        carry_ref[...] = jnp.broadcast_to(rc[0:1, :], carry_ref.shape)
        z = f_ref[...] + b_ref[...]
        dz = rc / (1.0 + jnp.exp(z))
        o_ref[...] = dz.astype(o_ref.dtype)
        gb_ref[...] += jnp.broadcast_to(jnp.sum(dz, axis=0, keepdims=True), gb_ref.shape)

    rev = lambda b, s: (b * ns + (ns - 1 - s), 0)
    return pl.pallas_call(
        body,
        out_shape=(jax.ShapeDtypeStruct((nb * SEQ, LANES), BF16), jax.ShapeDtypeStruct((8, LANES), F32)),
        grid=(nb, ns),
        in_specs=[pl.BlockSpec((ts, LANES), rev), pl.BlockSpec((ts, LANES), rev), pl.BlockSpec((1, LANES), lambda b, s: (0, 0))],
        out_specs=(pl.BlockSpec((ts, LANES), rev), pl.BlockSpec((8, LANES), lambda b, s: (0, 0))),
        scratch_shapes=[pltpu.VMEM((8, LANES), F32)],
        compiler_params=_cparams(dimension_semantics=("arbitrary", "arbitrary")),
        name=name,
    )(dcol, flog, bpad)


MEM_TQ = 256
MEM_SET = 4
MEM_SCALE = 1.0 / math.sqrt(MEM_HEAD_DIM)
assert MEM_HEAD_DIM == LANES and SEQ % (MEM_TQ * MEM_SET) == 0


def _head_masks(nh):
    lane = lax.broadcasted_iota(jnp.int32, (1, LANES), 1)
    return [None] if nh == 1 else [lane < HEAD_DIM, lane >= HEAD_DIM]


def _mem_specs(qoff):
    qspec = pl.BlockSpec((None, SEQ, LANES), lambda b, j: (b, 0, qoff + j))
    kspec = pl.BlockSpec((None, MEM_LEN, LANES), lambda b, j: (b, 0, j))
    vspec = pl.BlockSpec((None, MEM_LEN, LANES), lambda b, j: (b, 0, MEM_HEADS + j))
    ospec = pl.BlockSpec((None, SEQ, LANES), lambda b, j: (b, 0, j))
    return qspec, kspec, vspec, ospec


def _mem_rows(g):
    return [pl.ds(pl.multiple_of((MEM_SET * g + a) * MEM_TQ, MEM_TQ), MEM_TQ) for a in range(MEM_SET)]


def _mem_fwd(p3, mkv3, *, qoff, name):
    nb = p3.shape[0]

    def body(q_ref, k_ref, v_ref, o_ref, lse_ref):
        kb, vb = k_ref[...], v_ref[...]

        def qset(g, c):
            rows = _mem_rows(g)
            ss = [lax.dot_general(q_ref[r, :] * MEM_SCALE, kb, _NT, preferred_element_type=F32) for r in rows]
            for r, s in zip(rows, ss):
                m = jnp.max(s, axis=1, keepdims=True)
                p = jnp.exp(s - m)
                l = jnp.sum(p, axis=1, keepdims=True)
                o_ref[r, :] = jnp.dot(p.astype(BF16), vb, preferred_element_type=F32) / l
                lse_ref[r, :] = jnp.broadcast_to(m + jnp.log(l), (MEM_TQ, LANES))
            return c

        lax.fori_loop(0, SEQ // MEM_TQ // MEM_SET, qset, 0)

    qspec, kspec, vspec, ospec = _mem_specs(qoff)
    osd = jax.ShapeDtypeStruct((nb, SEQ, MEM_W), F32)
    return pl.pallas_call(body, out_shape=(osd, osd), grid=(nb, MEM_HEADS), in_specs=[qspec, kspec, vspec],
                          out_specs=(ospec, ospec), compiler_params=_cparams(dimension_semantics=("parallel", "parallel")),
                          name=name)(p3, mkv3, mkv3)


def _mem_bwd(p3, mkv3, do, o, lse, *, qoff, do_off, name):
    nb = p3.shape[0]

    def body(q_ref, k_ref, v_ref, do_ref, o_ref, lse_ref, dq_ref, dk_ref, dv_ref):
        kb, vb = k_ref[...], v_ref[...]
        ks = kb * MEM_SCALE

        def qset(g, carry):
            dk, dv = carry
            work = []
            for r in _mem_rows(g):
                qs = q_ref[r, :] * MEM_SCALE
                dob = do_ref[r, :].astype(BF16)
                s = lax.dot_general(qs, kb, _NT, preferred_element_type=F32)
                dp = lax.dot_general(dob, vb, _NT, preferred_element_type=F32)
                work.append((r, qs, dob, s, dp))
            for r, qs, dob, s, dp in work:
                delta = jnp.sum(dob.astype(F32) * o_ref[r, :], axis=1, keepdims=True)
                p = jnp.exp(s - lse_ref[r, :][:, 0:1])
                ds = (p * (dp - delta)).astype(BF16)
                dq_ref[r, :] = jnp.dot(ds, ks, preferred_element_type=F32).astype(dq_ref.dtype)
                dk = dk + lax.dot_general(ds, qs, _T0, preferred_element_type=F32)
                dv = dv + lax.dot_general(p.astype(BF16), dob, _T0, preferred_element_type=F32)
            return dk, dv

        z = jnp.zeros((MEM_LEN, LANES), F32)
        dk, dv = lax.fori_loop(0, SEQ // MEM_TQ // MEM_SET, qset, (z, z))
        dk_ref[...] = dk
        dv_ref[...] = dv

    qspec, kspec, vspec, ospec = _mem_specs(qoff)
    dospec = pl.BlockSpec((None, SEQ, LANES), lambda b, j: (b, 0, do_off + j))
    kvo = pl.BlockSpec((None, MEM_LEN, LANES), lambda b, j: (b, 0, j))
    kvsd = jax.ShapeDtypeStruct((nb, MEM_LEN, MEM_W), F32)
    return pl.pallas_call(
        body, out_shape=(jax.ShapeDtypeStruct((nb, SEQ, MEM_W), BF16), kvsd, kvsd), grid=(nb, MEM_HEADS),
        in_specs=[qspec, kspec, vspec, dospec, ospec, ospec], out_specs=(ospec, kvo, kvo),
        compiler_params=_cparams(dimension_semantics=("parallel", "parallel")), name=name)(p3, mkv3, mkv3, do, o, lse)


BLK = 128
NBLK = SEQ // BLK
QK_SCALE = 1.0 / math.sqrt(HEAD_DIM)
DIL_STEPS = tuple(d for _, d in DILATIONS)
assert all(w // d == BLK for w, d in DILATIONS)
_T0 = (((0,), (0,)), ((), ()))
_NT = (((1,), (1,)), ((), ()))


def _stack_heads(a, masks):
    z = jnp.zeros_like(a)
    return jnp.concatenate([jnp.where(masks[0], a, z), jnp.where(masks[1], a, z)], axis=0)


def _tri_bias(lower):
    r = lax.broadcasted_iota(jnp.int32, (BLK, BLK), 0)
    c = lax.broadcasted_iota(jnp.int32, (BLK, BLK), 1)
    return jnp.where((c <= r) if lower else (c >= r), 0.0, NEG_INF).astype(F32)


def _dil_rows(r, i, d):
    start = r + i * (BLK * d)
    return pl.ds(start, BLK) if d == 1 else pl.ds(start, BLK, stride=d)


DIL_SET = 4


def _dil_sets(d, fn):
    nbk = SEQ // d // BLK
    if d == 1:
        n = 2 * DIL_SET
        def gbody(g, c):
            fn([(0, n * g + a, None if a == 0 else True) for a in range(n)])
            return c
        lax.fori_loop(0, nbk // n, gbody, 0)
    elif nbk > 1:
        assert nbk == DIL_SET
        def rbody(r, c):
            fn([(r, i, i > 0) for i in range(nbk)])
            return c
        lax.fori_loop(0, d, rbody, 0)
    else:
        def rbody(rr, c):
            fn([(DIL_SET * rr + a, 0, False) for a in range(DIL_SET)])
            return c
        lax.fori_loop(0, d // DIL_SET, rbody, 0)


def _dil_key_tiles(r, i, d, has_prev, qrows, tri_cur, tri_prev):
    tiles = [(qrows, tri_cur)]
    if has_prev is None:
        tiles.append((_dil_rows(r, jnp.maximum(i - 1, 0), d), tri_prev + jnp.where(i > 0, 0.0, NEG_INF)))
    elif has_prev:
        tiles.append((_dil_rows(r, i - 1, d), tri_prev))
    return tiles


def _dil_fwd(qkv, *, name):
    nb = qkv.shape[0]
    ncol = DIL_W // LANES
    hd = HEAD_DIM

    def body(q_ref, k_ref, v_ref, o_ref, lse_ref, m_ref, l_ref, a_ref):
        masks = _head_masks(2)
        tri_cur, tri_prev = _tri_bias(True), _tri_bias(False)
        for pi, d in enumerate(DIL_STEPS):
            first, last = pi == 0, pi == len(DIL_STEPS) - 1

            def qset(blocks, d=d, first=first, last=last):
                work = []
                for r, i, has_prev in blocks:
                    qrows = _dil_rows(r, i, d)
                    qcat = _stack_heads((q_ref[qrows, :] * QK_SCALE).astype(BF16), masks)
                    ss, krs = [], []
                    for krows, bias in _dil_key_tiles(r, i, d, has_prev, qrows, tri_cur, tri_prev):
                        s = lax.dot_general(qcat, k_ref[krows, :].astype(BF16), _NT, preferred_element_type=F32)
                        ss.append((s[:BLK] + bias, s[BLK:] + bias))
                        krs.append(krows)
                    work.append((qrows, ss, krs))
                for qrows, ss, krs in work:
                    e0 = ss[0][0] if len(ss) == 1 else jnp.maximum(ss[0][0], ss[1][0])
                    e1 = ss[0][1] if len(ss) == 1 else jnp.maximum(ss[0][1], ss[1][1])
                    n0 = jnp.max(e0, axis=1, keepdims=True)
                    n1 = jnp.max(e1, axis=1, keepdims=True)
                    if not first:
                        mo, lo = m_ref[qrows, :], l_ref[qrows, :]
                        m0, m1 = mo[:, 0:1], mo[:, hd:hd + 1]
                        n0, n1 = jnp.maximum(n0, m0), jnp.maximum(n1, m1)
                        a0, a1 = jnp.exp(m0 - n0), jnp.exp(m1 - n1)
                    ps = [(jnp.exp(s0 - n0), jnp.exp(s1 - n1)) for s0, s1 in ss]
                    t0 = ps[0][0] if len(ps) == 1 else ps[0][0] + ps[1][0]
                    t1 = ps[0][1] if len(ps) == 1 else ps[0][1] + ps[1][1]
                    l0 = jnp.sum(t0, axis=1, keepdims=True)
                    l1 = jnp.sum(t1, axis=1, keepdims=True)
                    acc = None
                    for (p0, p1), krows in zip(ps, krs):
                        vcat = _stack_heads(v_ref[krows, :].astype(BF16), masks)
                        pv = jnp.dot(jnp.concatenate([p0, p1], axis=1).astype(BF16), vcat, preferred_element_type=F32)
                        acc = pv if acc is None else acc + pv
                    if not first:
                        l0 = l0 + a0 * lo[:, 0:1]
                        l1 = l1 + a1 * lo[:, hd:hd + 1]
                        acc = acc + a_ref[qrows, :] * jnp.where(masks[0], a0, a1)
                    if last:
                        o_ref[qrows, :] = acc / jnp.where(masks[0], l0, l1)
                        lse_ref[qrows, :] = jnp.where(masks[0], n0 + jnp.log(l0), n1 + jnp.log(l1))
                    else:
                        m_ref[qrows, :] = jnp.where(masks[0], n0, n1)
                        l_ref[qrows, :] = jnp.where(masks[0], l0, l1)
                        a_ref[qrows, :] = acc

            _dil_sets(d, qset)

    spec = lambda off: pl.BlockSpec((None, SEQ, LANES), lambda b, j: (b, 0, off + j))
    ospec = pl.BlockSpec((None, SEQ, LANES), lambda b, j: (b, 0, j))
    osd = jax.ShapeDtypeStruct((nb, SEQ, DIL_W), F32)
    return pl.pallas_call(
        body, out_shape=(osd, osd), grid=(nb, ncol),
        in_specs=[spec(0), spec(ncol), spec(2 * ncol)], out_specs=(ospec, ospec),
        scratch_shapes=[pltpu.VMEM((SEQ, LANES), F32)] * 3,
        compiler_params=_cparams(dimension_semantics=("parallel", "parallel")), name=name,
    )(qkv, qkv, qkv)


def _dil_bwd(qkv, do, o, lse, tabs, *, do_off, name):
    nb = qkv.shape[0]
    ncol = DIL_W // LANES
    hd = HEAD_DIM

    def body(q_ref, k_ref, v_ref, do_ref, o_ref, lse_ref, c_ref, s1_ref, s2_ref, dqo_ref, dko_ref, dvo_ref,
             dq_ref, dk_ref, dv_ref, dl_ref, dof_ref):
        masks = _head_masks(2)
        tri_cur, tri_prev = _tri_bias(True), _tri_bias(False)
        dq_ref[...] = jnp.zeros_like(dq_ref)
        dk_ref[...] = jnp.zeros_like(dk_ref)
        dv_ref[...] = jnp.zeros_like(dv_ref)

        def delta_body(i, c):
            rows = pl.ds(pl.multiple_of(i * BLK, BLK), BLK)
            dof = do_ref[rows, :].astype(F32)
            dof_ref[rows, :] = dof
            prod = dof * o_ref[rows, :]
            z = jnp.zeros_like(prod)
            dl_ref[rows, :] = jnp.where(masks[0], jnp.sum(jnp.where(masks[0], prod, z), axis=1, keepdims=True),
                                        jnp.sum(jnp.where(masks[1], prod, z), axis=1, keepdims=True))
            return c

        lax.fori_loop(0, NBLK, delta_body, 0)

        for d in DIL_STEPS:
            def qset(blocks, d=d):
                work = []
                for r, i, has_prev in blocks:
                    qrows = _dil_rows(r, i, d)
                    qcat = _stack_heads((q_ref[qrows, :] * QK_SCALE).astype(BF16), masks)
                    docat = _stack_heads(dof_ref[qrows, :].astype(BF16), masks)
                    tiles = []
                    for krows, bias in _dil_key_tiles(r, i, d, has_prev, qrows, tri_cur, tri_prev):
                        s = lax.dot_general(qcat, k_ref[krows, :].astype(BF16), _NT, preferred_element_type=F32)
                        dp = lax.dot_general(docat, v_ref[krows, :].astype(BF16), _NT, preferred_element_type=F32)
                        tiles.append((krows, s, dp, bias))
                    work.append((qrows, qcat, docat, tiles))
                for qrows, qcat, docat, tiles in work:
                    lseb, dlb = lse_ref[qrows, :], dl_ref[qrows, :]
                    lse0, lse1 = lseb[:, 0:1], lseb[:, hd:hd + 1]
                    dl0, dl1 = dlb[:, 0:1], dlb[:, hd:hd + 1]
                    dq = None
                    for krows, s, dp, bias in tiles:
                        p0 = jnp.exp(s[:BLK] + bias - lse0)
                        p1 = jnp.exp(s[BLK:] + bias - lse1)
                        ds0 = p0 * (dp[:BLK] - dl0)
                        ds1 = p1 * (dp[BLK:] - dl1)
                        ds0b, ds1b = ds0.astype(BF16), ds1.astype(BF16)
                        pcat = jnp.concatenate([p0.astype(BF16), p1.astype(BF16)], axis=0)
                        dscat = jnp.concatenate([ds0b, ds1b], axis=0)
                        dv_ref[krows, :] += lax.dot_general(pcat, docat, _T0, preferred_element_type=F32)
                        dk_ref[krows, :] += lax.dot_general(dscat, qcat, _T0, preferred_element_type=F32)
                        dsrow = jnp.concatenate([ds0b, ds1b], axis=1)
                        kcat = _stack_heads((k_ref[krows, :] * QK_SCALE).astype(BF16), masks)
                        t = jnp.dot(dsrow, kcat, preferred_element_type=F32)
                        dq = t if dq is None else dq + t
                    dq_ref[qrows, :] += dq

            _dil_sets(d, qset)

        def out_body(i, c):
            rows = pl.ds(pl.multiple_of(i * BLK, BLK), BLK)
            tab = (c_ref[rows, :], s1_ref[rows, :], s2_ref[rows, :])
            dqo_ref[rows, :] = _rope_apply(dq_ref[rows, :], *tab, transpose=True).astype(dqo_ref.dtype)
            dko_ref[rows, :] = _rope_apply(dk_ref[rows, :], *tab, transpose=True).astype(dko_ref.dtype)
            dvo_ref[rows, :] = dv_ref[rows, :].astype(dvo_ref.dtype)
            return c

        lax.fori_loop(0, NBLK, out_body, 0)

    spec = lambda off: pl.BlockSpec((None, SEQ, LANES), lambda b, j: (b, 0, off + j))
    ospec = pl.BlockSpec((None, SEQ, LANES), lambda b, j: (b, 0, j))
    tspec = pl.BlockSpec((SEQ, LANES), lambda b, j: (0, 0))
    osd = jax.ShapeDtypeStruct((nb, SEQ, DIL_W), BF16)
    return pl.pallas_call(
        body, out_shape=(osd, osd, osd), grid=(nb, ncol),
        in_specs=[spec(0), spec(ncol), spec(2 * ncol), spec(do_off), ospec, ospec, tspec, tspec, tspec],
        out_specs=(ospec, ospec, ospec),
        scratch_shapes=[pltpu.VMEM((SEQ, LANES), F32)] * 5,
        compiler_params=_cparams(dimension_semantics=("parallel", "parallel")), name=name,
    )(qkv, qkv, qkv, do, o, lse, *tabs)


FOX_FWD_GROUP = 8
FOX_BWD_GROUP = 4
assert NBLK % FOX_FWD_GROUP == 0 and NBLK % FOX_BWD_GROUP == 0
_FOX_COLS = tuple(c // LANES for c in (C_FQ, C_FK, C_FV))


def _fox_specs():
    cols = [pl.BlockSpec((None, SEQ, LANES), (lambda b, j, off=off: (b, 0, off + j))) for off in _FOX_COLS]
    ospec = pl.BlockSpec((None, SEQ, LANES), lambda b, j: (b, 0, j))
    crspec = pl.BlockSpec((None, None, NBLK, 8, BLK), lambda b, j: (b, j, 0, 0, 0))
    return cols, ospec, crspec


def _fox_key_rows(t, e, g):
    return pl.ds(pl.multiple_of((g * t + e) * BLK, BLK), BLK)


def _fox_fwd(p3, crow, *, name):
    nb = p3.shape[0]
    g = FOX_FWD_GROUP

    def body(q_ref, k_ref, v_ref, cr_ref, o_ref, lse_ref):
        masks = _head_masks(2)
        tri = _tri_bias(True)

        def qk(qcat, t, nblk=g):
            return tuple(lax.dot_general(qcat, k_ref[_fox_key_rows(t, e, g), :], _NT, preferred_element_type=F32) for e in range(nblk))

        def consume(ss, t, state, nblk, diag):
            m0, m1, l0, l1, acc = state
            us = []
            for e in range(nblk):
                cr = cr_ref[g * t + e]
                u0 = ss[e][:BLK] - cr[0:1, :]
                u1 = ss[e][BLK:] - cr[1:2, :]
                if diag and e == nblk - 1:
                    u0, u1 = u0 + tri, u1 + tri
                us.append((u0, u1))
            x0 = functools.reduce(jnp.maximum, [u[0] for u in us])
            x1 = functools.reduce(jnp.maximum, [u[1] for u in us])
            n0 = jnp.maximum(m0, jnp.max(x0, axis=1, keepdims=True))
            n1 = jnp.maximum(m1, jnp.max(x1, axis=1, keepdims=True))
            a0, a1 = jnp.exp(m0 - n0), jnp.exp(m1 - n1)
            acc = acc * jnp.where(masks[0], a0, a1)
            t0 = t1 = None
            for e in range(nblk):
                p0, p1 = jnp.exp(us[e][0] - n0), jnp.exp(us[e][1] - n1)
                t0 = p0 if t0 is None else t0 + p0
                t1 = p1 if t1 is None else t1 + p1
                pcat = jnp.concatenate([p0, p1], axis=1)
                hi = pcat.astype(BF16)
                lo = (pcat - hi.astype(F32)).astype(BF16)
                vcat = _stack_heads(v_ref[_fox_key_rows(t, e, g), :], masks)
                acc = acc + jnp.dot(hi, vcat, preferred_element_type=F32) + jnp.dot(lo, vcat, preferred_element_type=F32)
            l0 = a0 * l0 + jnp.sum(t0, axis=1, keepdims=True)
            l1 = a1 * l1 + jnp.sum(t1, axis=1, keepdims=True)
            return n0, n1, l0, l1, acc

        def gbody(ng, c):
            neg = jnp.full((BLK, 1), NEG_INF, F32)
            z1 = jnp.zeros((BLK, 1), F32)
            rows = [pl.ds(pl.multiple_of((g * ng + a) * BLK, BLK), BLK) for a in range(g)]
            qcats = [_stack_heads(q_ref[rows[a], :] * QK_SCALE, masks) for a in range(g)]
            def step(t, cc):
                cur = [qk(qcats[a], t) for a in range(g)]
                return tuple(consume(cur[a], t, cc[a], g, False) for a in range(g))

            init = (neg, neg, z1, z1, jnp.zeros((BLK, LANES), F32))
            done = lax.fori_loop(0, ng, step, tuple(init for a in range(g)))
            last = [qk(qcats[a], ng, a + 1) for a in range(g)]
            for a in range(g):
                ss, state = last[a], done[a]
                m0, m1, l0, l1, acc = consume(ss, ng, state, a + 1, True)
                o_ref[rows[a], :] = acc / jnp.where(masks[0], l0, l1)
                lse_ref[rows[a], :] = jnp.where(masks[0], m0 + jnp.log(l0), m1 + jnp.log(l1))
            return c

        lax.fori_loop(0, NBLK // g, gbody, 0)

    cols, ospec, crspec = _fox_specs()
    osd = jax.ShapeDtypeStruct((nb, SEQ, FOX_W), F32)
    return pl.pallas_call(
        body, out_shape=(osd, osd), grid=(nb, FOX_W // LANES), in_specs=cols + [crspec], out_specs=(ospec, ospec),
        compiler_params=_cparams(dimension_semantics=("parallel", "parallel")), name=name,
    )(p3, p3, p3, crow)


def _fox_bwd(p3, crow, do, o, lse, *, do_off, name):
    nb = p3.shape[0]
    g = FOX_BWD_GROUP
    hd = HEAD_DIM

    def body(q_ref, k_ref, v_ref, cr_ref, do_ref, o_ref, lse_ref, dq_ref, dko_ref, dvo_ref, dcr_ref, dk_ref, dv_ref):
        masks = _head_masks(2)
        tri = _tri_bias(True)
        dk_ref[...] = jnp.zeros_like(dk_ref)
        dv_ref[...] = jnp.zeros_like(dv_ref)
        dcr_ref[...] = jnp.zeros_like(dcr_ref)

        def products(qcat, docat, t, nblk=g):
            out = []
            for e in range(nblk):
                krows = _fox_key_rows(t, e, g)
                out.append(lax.dot_general(qcat, k_ref[krows, :], _NT, preferred_element_type=F32))
                out.append(lax.dot_general(docat, v_ref[krows, :], _NT, preferred_element_type=F32))
            return tuple(out)

        def consume(prod, t, ctx, dq, nblk, diag):
            qcat, docat, lse0, lse1, dl0, dl1 = ctx
            for e in range(nblk):
                jb = g * t + e
                krows = _fox_key_rows(t, e, g)
                s, dp = prod[2 * e], prod[2 * e + 1]
                cr = cr_ref[jb]
                u0 = s[:BLK] - cr[0:1, :]
                u1 = s[BLK:] - cr[1:2, :]
                if diag and e == nblk - 1:
                    u0, u1 = u0 + tri, u1 + tri
                p0 = jnp.exp(u0 - lse0)
                p1 = jnp.exp(u1 - lse1)
                ds0 = p0 * (dp[:BLK] - dl0)
                ds1 = p1 * (dp[BLK:] - dl1)
                dcr_ref[jb, 0:1, :] += jnp.sum(ds0, axis=0, keepdims=True)
                dcr_ref[jb, 1:2, :] += jnp.sum(ds1, axis=0, keepdims=True)
                ds0b, ds1b = ds0.astype(BF16), ds1.astype(BF16)
                pcat = jnp.concatenate([p0.astype(BF16), p1.astype(BF16)], axis=0)
                dscat = jnp.concatenate([ds0b, ds1b], axis=0)
                dv_ref[krows, :] += lax.dot_general(pcat, docat, _T0, preferred_element_type=F32)
                dk_ref[krows, :] += lax.dot_general(dscat, qcat, _T0, preferred_element_type=F32)
                dsrow = jnp.concatenate([ds0b, ds1b], axis=1)
                dq = dq + jnp.dot(dsrow, _stack_heads(k_ref[krows, :] * QK_SCALE, masks), preferred_element_type=F32)
            return dq

        def gbody(ng, c):
            ctxs, rows = [], []
            for a in range(g):
                r = pl.ds(pl.multiple_of((g * ng + a) * BLK, BLK), BLK)
                qcat = _stack_heads(q_ref[r, :] * QK_SCALE, masks)
                dob = do_ref[r, :].astype(BF16)
                prod = dob.astype(F32) * o_ref[r, :]
                z = jnp.zeros_like(prod)
                dl0 = jnp.sum(jnp.where(masks[0], prod, z), axis=1, keepdims=True)
                dl1 = jnp.sum(jnp.where(masks[1], prod, z), axis=1, keepdims=True)
                lseb = lse_ref[r, :]
                ctxs.append((qcat, _stack_heads(dob, masks), lseb[:, 0:1], lseb[:, hd:hd + 1], dl0, dl1))
                rows.append(r)
            def step(t, cc):
                cur = [products(ctxs[a][0], ctxs[a][1], t) for a in range(g)]
                return tuple(consume(cur[a], t, ctxs[a], cc[a], g, False) for a in range(g))

            done = lax.fori_loop(0, ng, step, tuple(jnp.zeros((BLK, LANES), F32) for a in range(g)))
            last = [products(ctxs[a][0], ctxs[a][1], ng, a + 1) for a in range(g)]
            for a in range(g):
                dq_ref[rows[a], :] = consume(last[a], ng, ctxs[a], done[a], a + 1, True).astype(dq_ref.dtype)
            return c

        lax.fori_loop(0, NBLK // g, gbody, 0)
        dko_ref[...] = dk_ref[...].astype(dko_ref.dtype)
        dvo_ref[...] = dv_ref[...].astype(dvo_ref.dtype)

    cols, ospec, crspec = _fox_specs()
    dospec = pl.BlockSpec((None, SEQ, LANES), lambda b, j: (b, 0, do_off + j))
    osd = jax.ShapeDtypeStruct((nb, SEQ, FOX_W), BF16)
    return pl.pallas_call(
        body, out_shape=(osd, osd, osd, jax.ShapeDtypeStruct((nb, FOX_W // LANES, NBLK, 8, BLK), F32)),
        grid=(nb, FOX_W // LANES), in_specs=cols + [crspec, dospec, ospec, ospec], out_specs=(ospec, ospec, ospec, crspec),
        scratch_shapes=[pltpu.VMEM((SEQ, LANES), F32)] * 2,
        compiler_params=_cparams(dimension_semantics=("parallel", "parallel")), name=name,
    )(p3, p3, p3, crow, do, o, lse)


_B1, _B2 = FOX_W // LANES, (FOX_W + DIL_W) // LANES


def _dy_gate_bwd(dx2b, wo, fox, dil, memo, p16, *, tm, tn, name):
    t, d = dx2b.shape
    assert FOX_W % tn == 0 and DIL_W % tn == 0 and MEM_W % tn == 0 and all(c % tn == 0 for c in (C_FG, C_DG, C_MG))
    n1, n2, n3 = FOX_W // tn, (FOX_W + DIL_W) // tn, MIX_W // tn

    def body(dx_ref, w_ref, f_ref, d_ref, m_ref, g_ref, da_ref, dg_ref):
        j = pl.program_id(1)
        wv = w_ref[...]
        for c0 in range(0, tm, min(tm, 2 * MM_CHUNK)):
            rows = pl.ds(c0, min(tm, 2 * MM_CHUNK))
            dyv = lax.dot_general(dx_ref[rows, :], wv, _NT, preferred_element_type=F32)
            a = jnp.where(j < n1, f_ref[rows, :], jnp.where(j < n2, d_ref[rows, :], m_ref[rows, :]))
            gt = g_ref[rows, :].astype(F32)
            sg = 1.0 / (1.0 + jnp.exp(-gt))
            da_ref[rows, :] = (dyv * gt * sg).astype(da_ref.dtype)
            dg_ref[rows, :] = (dyv * a * sg * (1.0 + gt * (1.0 - sg))).astype(dg_ref.dtype)

    def gcol(j):
        return jnp.where(j < n1, C_FG // tn + j, jnp.where(j < n2, C_DG // tn + j - n1, C_MG // tn + j - n2))

    tile = pl.BlockSpec((tm, tn), lambda i, j: (i, j))
    return pl.pallas_call(
        body,
        out_shape=(jax.ShapeDtypeStruct((t, MIX_W), BF16), jax.ShapeDtypeStruct((t, MIX_W), BF16)),
        grid=(t // tm, n3),
        in_specs=[pl.BlockSpec((tm, d), lambda i, j: (i, 0)), pl.BlockSpec((tn, d), lambda i, j: (j, 0)),
                  pl.BlockSpec((tm, tn), lambda i, j: (i, jnp.minimum(j, n1 - 1))),
                  pl.BlockSpec((tm, tn), lambda i, j: (i, jnp.clip(j - n1, 0, n2 - n1 - 1))),
                  pl.BlockSpec((tm, tn), lambda i, j: (i, jnp.clip(j - n2, 0, n3 - n2 - 1))),
                  pl.BlockSpec((tm, tn), lambda i, j: (i, gcol(j)))],
        out_specs=(tile, tile),
        compiler_params=_cparams(dimension_semantics=("parallel", "parallel")),
        name=name,
    )(dx2b, wo, fox, dil, memo, p16)


def _silu(g):
    return g / (1.0 + jnp.exp(-g))


def _out_loss(fox, dil, memo, p16, wo, x, tgt, gfin, *, tm, name):
    t, d = x.shape
    n_feat = float(d)

    def body(f_ref, d_ref, m_ref, fg_ref, dg_ref, mg_ref, w_ref, x_ref, t_ref, g_ref, y_ref, dx_ref, dxb_ref, st_ref):
        i = pl.program_id(0)

        @pl.when(i == 0)
        def _():
            st_ref[...] = jnp.zeros_like(st_ref)

        wv, gv = w_ref[...], g_ref[...]
        half = tm // 2
        for c0 in (0, half):
            rows = pl.ds(c0, half)
            y = jnp.concatenate([(a_ref[rows, :] * _silu(gt_ref[rows, :].astype(F32))).astype(BF16)
                                 for a_ref, gt_ref in ((f_ref, fg_ref), (d_ref, dg_ref), (m_ref, mg_ref))], axis=1)
            y_ref[rows, :] = y
            x2 = x_ref[rows, :] + jnp.dot(y, wv, preferred_element_type=F32)
            r = lax.rsqrt(jnp.mean(x2 * x2, axis=-1, keepdims=True) + RMS_EPS)
            nrm = x2 * r
            err = nrm * gv - t_ref[rows, :]
            dout = err * (1.0 / n_feat)
            dn = dout * gv
            dx2 = r * (dn - nrm * jnp.mean(dn * nrm, axis=-1, keepdims=True))
            dx_ref[rows, :] = dx2
            dxb_ref[rows, :] = dx2.astype(dxb_ref.dtype)
            st_ref[0:1, :] += jnp.sum(dout * nrm, axis=0, keepdims=True)
            st_ref[1:2, :] += (0.5 / n_feat) * jnp.sum(err * err, axis=0, keepdims=True)

    row = pl.BlockSpec((tm, d), lambda i: (i, 0))
    whole = lambda w: pl.BlockSpec((tm, w), lambda i: (i, 0))
    gate = lambda w, col: pl.BlockSpec((tm, w), lambda i: (i, col // w))
    return pl.pallas_call(
        body,
        out_shape=(jax.ShapeDtypeStruct((t, MIX_W), BF16), jax.ShapeDtypeStruct((t, d), F32), jax.ShapeDtypeStruct((t, d), BF16),
                   jax.ShapeDtypeStruct((8, d), F32)),
        grid=(t // tm,),
        in_specs=[whole(FOX_W), whole(DIL_W), whole(MEM_W), gate(FOX_W, C_FG), gate(DIL_W, C_DG), gate(MEM_W, C_MG),
                  pl.BlockSpec((MIX_W, d), lambda i: (0, 0)), row, row, pl.BlockSpec((1, d), lambda i: (0, 0))],
        out_specs=(pl.BlockSpec((tm, MIX_W), lambda i: (i, 0)), row, row, pl.BlockSpec((8, d), lambda i: (0, 0))),
        compiler_params=_cparams(dimension_semantics=("arbitrary",)),
        name=name,
    )(fox, dil, memo, p16, p16, p16, wo, x, tgt, gfin)


def _dh_rms_bwd(dp, w, x, g, resid, *, tm, name):
    t, d = x.shape
    kdim = dp.shape[1]

    def body(*refs):
        if resid is not None:
            dp_ref, w_ref, x_ref, g_ref, r_ref, dx_ref, gg_ref = refs
        else:
            dp_ref, w_ref, x_ref, g_ref, dx_ref, gg_ref = refs

        @pl.when(pl.program_id(0) == 0)
        def _():
            gg_ref[...] = jnp.zeros_like(gg_ref)

        dh = lax.dot_general(dp_ref[...], w_ref[...], _NT, preferred_element_type=F32)
        xv = x_ref[...]
        r = lax.rsqrt(jnp.mean(xv * xv, axis=-1, keepdims=True) + RMS_EPS)
        nrm = xv * r
        dn = dh * g_ref[...]
        dx = r * (dn - nrm * jnp.mean(dn * nrm, axis=-1, keepdims=True))
        if resid is not None:
            dx = dx + r_ref[...]
        dx_ref[...] = dx
        gg_ref[0:1, :] += jnp.sum(dh * nrm, axis=0, keepdims=True)

    row = pl.BlockSpec((tm, d), lambda i: (i, 0))
    in_specs = [pl.BlockSpec((tm, kdim), lambda i: (i, 0)),
                pl.BlockSpec((d, kdim), lambda i: (0, 0), pipeline_mode=pl.Buffered(1)), row,
                pl.BlockSpec((1, d), lambda i: (0, 0))]
    args = [dp, w, x, g]
    if resid is not None:
        in_specs.append(row)
        args.append(resid)
    return pl.pallas_call(
        body,
        out_shape=(jax.ShapeDtypeStruct((t, d), F32), jax.ShapeDtypeStruct((8, d), F32)),
        grid=(t // tm,),
        in_specs=in_specs,
        out_specs=(row, pl.BlockSpec((8, d), lambda i: (0, 0))),
        compiler_params=_cparams(dimension_semantics=("arbitrary",)),
        name=name,
    )(*args)


_FLOG0 = 4 * FOX_W
_W_IN_SEGMENTS = ((0, _FLOG0, 0), (_FLOG0, _FLOG0 + FOX_HEADS, PW), (_FLOG0 + FOX_HEADS, IN_W, C_DQ))
SHARD_W = IN_W // N_CHIPS


def _rearrange_w_in(shards):
    def cols(lo, hi):
        parts = []
        for k in range(N_CHIPS):
            a, b = max(lo, k * SHARD_W), min(hi, (k + 1) * SHARD_W)
            if a < b:
                parts.append(shards[k][:, a - k * SHARD_W:b - k * SHARD_W])
        return parts

    (a0, a1, _), (f0, f1, _), (b0, b1, _) = _W_IN_SEGMENTS
    pad = jnp.zeros((shards[0].shape[0], PWF - PW - FOX_HEADS), shards[0].dtype)
    return jnp.concatenate(cols(a0, a1) + cols(b0, b1) + cols(f0, f1) + [pad], axis=1)


def _w_in_grad_slabs(g):
    slabs = []
    for k in range(N_CHIPS):
        parts = []
        for lo, hi, at in _W_IN_SEGMENTS:
            a, b = max(lo, k * SHARD_W), min(hi, (k + 1) * SHARD_W)
            if a < b:
                parts.append(g[:, at + a - lo:at + b - lo])
        slabs.append(jnp.concatenate(parts, axis=1))
    return jnp.stack(slabs, axis=0)


def _local_grads(x, mem, norm_g, w_r, b_forget, mem_norm_g, w_kv, w_o, final_norm_g, tgt, start_reduce=None,
                 start_reduce_small=None, early_token=None, late_weights=None):
    nb = x.shape[0]
    t = nb * SEQ
    x2d = x.reshape(t, D_MODEL)
    tgt2d = tgt.reshape(t, D_MODEL)
    tabs = _rope_tables()
    bpad = jnp.pad(b_forget.reshape(1, FOX_HEADS), ((0, 0), (0, LANES - FOX_HEADS)))

    gain0 = norm_g.reshape(1, D_MODEL)
    if early_token is not None:
        gain0 = gain0 + early_token[0:1, 0:1]
    h, p16, dqkv, flog = _proj(x2d, gain0, w_r, tabs, n=PWF, tm=1024, tn=768, name="proj")
    c12 = _flog_fwd(flog, bpad, nb=nb, ts=256, name="flog_fwd")

    crow = c12[:, :FOX_HEADS].reshape(nb, NBLK, BLK, FOX_HEADS // 2, 2).transpose(0, 3, 1, 4, 2)
    crow = jnp.pad(crow, ((0, 0), (0, 0), (0, 0), (0, 6), (0, 0)))
    p3 = p16.reshape(nb, SEQ, PWF)
    fox, fox_lse = _fox_fwd(p3, crow, name="fox_fwd")
    if late_weights is not None:
        w_kv, w_o = late_weights(fox_lse)

    dqkv3 = dqkv.reshape(nb, SEQ, 3 * DIL_W)
    dil, dil_lse = _dil_fwd(dqkv3, name="dil_fwd")

    mh = _rms_fwd(mem.reshape(nb * MEM_LEN, D_MODEL), mem_norm_g.reshape(1, D_MODEL), tm=nb * MEM_LEN, name="rms_mem")
    mkv = _matmul(mh, w_kv, out_dtype=BF16, tm=nb * MEM_LEN, tn=512, tk=D_MODEL, name="mem_kv")
    mkv3 = mkv.reshape(nb, MEM_LEN, 2 * MEM_W)
    memo, mem_lse = _mem_fwd(p3, mkv3, qoff=C_MQ // LANES, name="mem_fwd")

    fox2, dil2, memo2 = fox.reshape(t, FOX_W), dil.reshape(t, DIL_W), memo.reshape(t, MEM_W)
    y, dx2, dx2b, st = _out_loss(fox2, dil2, memo2, p16, w_o, x2d, tgt2d, final_norm_g.reshape(1, D_MODEL), tm=256,
                                 name="out_loss")

    g_wo = _matmul(y, dx2b, mode="tn", out_dtype=BF16, tm=1024, tn=512, tk=t, name="grad_w_out")
    datt, dgate = _dy_gate_bwd(dx2b, w_o, fox2, dil2, memo2, p16, tm=2048, tn=256, name="dy_gate_bwd")
    datt3 = datt.reshape(nb, SEQ, MIX_W)

    dmq, dmk, dmv = _mem_bwd(p3, mkv3, datt3, memo, mem_lse, qoff=C_MQ // LANES, do_off=_B2, name="mem_bwd")
    dmkv = jnp.concatenate([dmk, dmv], axis=-1).reshape(nb * MEM_LEN, 2 * MEM_W).astype(BF16)
    g_wkv = _matmul(mh, dmkv, mode="tn", out_dtype=BF16, tm=512, tn=512, tk=nb * MEM_LEN, name="grad_w_kv")
    mem_gain = mem_norm_g.reshape(1, D_MODEL)
    if start_reduce_small is not None:
        tok = start_reduce_small(g_wkv, g_wo)[0:1, 0:1]
        mem_gain, crow = mem_gain + tok, crow + tok
    _, gmn = _dh_rms_bwd(dmkv, w_kv, mem.reshape(nb * MEM_LEN, D_MODEL), mem_gain, None, tm=nb * MEM_LEN, name="mem_rms_bwd")

    dfq, dfk, dfv, dcr = _fox_bwd(p3, crow, datt3, fox, fox_lse, do_off=0, name="fox_bwd")
    dcol = -dcr[:, :, :, :2, :].transpose(0, 2, 4, 1, 3).reshape(t, FOX_HEADS)
    dcol = jnp.pad(dcol, ((0, 0), (0, LANES - FOX_HEADS)))
    dflog, gb = _flog_bwd(dcol, flog, bpad, nb=nb, ts=256, name="flog_bwd")

    ddq, ddk, ddv = _dil_bwd(dqkv3, datt3, dil, dil_lse, tabs, do_off=_B1, name="dil_bwd")

    flat = lambda a: a.reshape(t, -1)
    dp = jnp.concatenate([flat(dfq), flat(dfk), flat(dfv), dgate[:, :FOX_W], flat(ddq), flat(ddk), flat(ddv),
                          dgate[:, FOX_W:FOX_W + DIL_W], flat(dmq), dgate[:, FOX_W + DIL_W:], dflog,
                          jnp.zeros((t, PWF - PW - LANES), BF16)], axis=1)
    g_wr = _matmul(h, dp, mode="tn", out_dtype=BF16, tm=D_MODEL, tn=768, tk=t, name="grad_w_in")
    gain = norm_g.reshape(1, D_MODEL)
    if start_reduce is not None:
        gain = gain + start_reduce(g_wr)[0:1, 0:1]
    gx, gng = _dh_rms_bwd(dp, w_r, x2d, gain, dx2, tm=256, name="in_rms_bwd")

    gb_row = jnp.pad(gb[0:1, :], ((0, 0), (0, D_MODEL - LANES)))
    small = jnp.concatenate([gng[0:1], gmn[0:1], st[0:1], gb_row, st[1:2], jnp.zeros((3, D_MODEL), F32)], axis=0)
    return gx.reshape(nb, SEQ, D_MODEL), g_wr, g_wkv, g_wo, small


MESH = pl.DeviceIdType.MESH
ANY = pl.BlockSpec(memory_space=pl.ANY)


def _place():
    x, y, c = lax.axis_index("x"), lax.axis_index("y"), lax.axis_index("c")
    other_chips = [(1 - x, y), (x, 1 - y), (1 - x, 1 - y)]
    return x, y, c, other_chips


def _gather_weights(shards):
    n = len(shards)

    def body(*refs):
        in_refs, out_refs = refs[:n], refs[n:2 * n]
        send_sems, recv_sems = refs[2 * n:]
        x, y, c, chips = _place()
        me_chip = 2 * x + y
        sibling = (x, y, 1 - c)

        def half(ref, pc, rows):
            return ref.at[pl.ds(pc * (rows // 2), rows // 2), :]

        def rcopy(k, src, dst, to):
            return pltpu.make_async_remote_copy(src_ref=src, dst_ref=dst, send_sem=send_sems.at[k], recv_sem=recv_sems.at[k],
                                                device_id=to, device_id_type=MESH)

        sends = []
        for t in range(n):
            rows = shards[t].shape[0]
            for j, chip in enumerate(chips):
                cp = rcopy(6 * t + j, half(in_refs[t], c, rows), half(out_refs[t].at[me_chip], c, rows), (*chip, c))
                cp.start()
                sends.append(cp)
        for t in range(n):
            rows = shards[t].shape[0]
            for j, chip in enumerate(chips):
                slot = out_refs[t].at[2 * chip[0] + chip[1]]
                rcopy(6 * t + j, half(slot, c, rows), half(slot, c, rows), sibling).wait_recv()
                fw = rcopy(6 * t + 3 + j, half(slot, c, rows), half(slot, c, rows), sibling)
                fw.start()
                sends.append(fw)
        for t in range(n):
            rows = shards[t].shape[0]
            for j, chip in enumerate(chips):
                slot = out_refs[t].at[2 * chip[0] + chip[1]]
                rcopy(6 * t + 3 + j, half(slot, 1 - c, rows), half(slot, 1 - c, rows), sibling).wait_recv()
        for cp in sends:
            cp.wait_send()

    return pl.pallas_call(
        body,
        out_shape=tuple(jax.ShapeDtypeStruct((N_CHIPS,) + s.shape, s.dtype) for s in shards),
        in_specs=[ANY] * n,
        out_specs=tuple([ANY] * n),
        scratch_shapes=[pltpu.SemaphoreType.DMA((6 * n,)), pltpu.SemaphoreType.DMA((6 * n,))],
        name="gather_weights",
    )(*shards)


def _pair_exchange(gs, *, name):
    n = len(gs)

    def body(*refs):
        g_refs, r_refs = refs[:n], refs[n:2 * n]
        send_sems, recv_sems = refs[2 * n:]
        x, y, c, _ = _place()
        cps = []
        for t in range(n):
            hr = gs[t].shape[1] // 2
            cp = pltpu.make_async_remote_copy(src_ref=g_refs[t].at[:, pl.ds((1 - c) * hr, hr), :], dst_ref=r_refs[t],
                                              send_sem=send_sems.at[t], recv_sem=recv_sems.at[t],
                                              device_id=(x, y, 1 - c), device_id_type=MESH)
            cp.start()
            cps.append(cp)
        for cp in cps:
            cp.wait()

    return pl.pallas_call(
        body,
        out_shape=tuple(jax.ShapeDtypeStruct((g.shape[0], g.shape[1] // 2, g.shape[2]), g.dtype) for g in gs),
        in_specs=[ANY] * n,
        out_specs=tuple([ANY] * n),
        scratch_shapes=[pltpu.SemaphoreType.DMA((n,)), pltpu.SemaphoreType.DMA((n,))],
        name=name,
    )(*gs)


_HBM = pl.BlockSpec(memory_space=pltpu.HBM)
_SEM = pl.BlockSpec(memory_space=pltpu.SEMAPHORE)
_DATAFLOW = pltpu.SideEffectType.DATAFLOW_SIDE_EFFECTING


def _chip_copies(p_refs, land_refs, send_sems, recv_sems):
    x, y, c, chips = _place()
    me_chip = 2 * x + y
    return [pltpu.make_async_remote_copy(src_ref=p_refs[t].at[2 * chip[0] + chip[1]], dst_ref=land_refs[t].at[me_chip],
                                         send_sem=send_sems.at[3 * t + j], recv_sem=recv_sems.at[3 * t + j],
                                         device_id=(*chip, c), device_id_type=MESH)
            for t in range(len(p_refs)) for j, chip in enumerate(chips)]


def _chip_exchange_start(ps, *, tag):
    n = len(ps)

    def body(*refs):
        p_refs, land_refs = refs[:n], refs[n:2 * n]
        send_sems, recv_sems = refs[2 * n:2 * n + 2]
        token = refs[-1]
        for cp in _chip_copies(p_refs, land_refs, send_sems, recv_sems):
            cp.start()
        token[...] = jnp.zeros_like(token)

    hbm = [pltpu.HBM(p.shape, p.dtype) for p in ps]
    args = [pltpu.with_memory_space_constraint(p, pltpu.HBM) for p in ps]
    args += [pltpu.with_memory_space_constraint(lax.empty(p.shape, p.dtype), pltpu.HBM) for p in ps]
    out = pl.pallas_call(
        body,
        name=f"chip_exchange_start_{tag}",
        out_shape=(pltpu.SemaphoreType.DMA((3 * n,)), pltpu.SemaphoreType.DMA((3 * n,)), *hbm, *hbm,
                   jax.ShapeDtypeStruct((8, LANES), F32)),
        in_specs=[_HBM] * (2 * n),
        out_specs=(_SEM, _SEM, *([_HBM] * (2 * n)), pl.BlockSpec(memory_space=pltpu.VMEM)),
        input_output_aliases={i: 2 + i for i in range(2 * n)},
        compiler_params=pltpu.CompilerParams(has_side_effects=_DATAFLOW),
    )(*args)
    return out[0], out[1], out[2:2 + n], out[2 + n:2 + 2 * n], out[-1]


def _chip_exchange_wait(send_sems, recv_sems, p_thru, land_thru, after, *, tag):
    n = len(p_thru)

    def body(*refs):
        p_refs, land_refs = refs[:n], refs[n:2 * n]
        ssem, rsem = refs[2 * n:2 * n + 2]
        for cp in _chip_copies(p_refs, land_refs, ssem, rsem):
            cp.wait_send()
            cp.wait_recv()

    hbm = [pltpu.HBM(p.shape, p.dtype) for p in p_thru]
    out = pl.pallas_call(
        body,
        name=f"chip_exchange_wait_{tag}",
        out_shape=(*hbm, *hbm),
        in_specs=[_HBM] * (2 * n) + [_SEM, _SEM, ANY],
        out_specs=tuple([_HBM] * (2 * n)),
        input_output_aliases={i: i for i in range(2 * n)},
        compiler_params=pltpu.CompilerParams(has_side_effects=_DATAFLOW),
    )(*p_thru, *land_thru, send_sems, recv_sems, after)
    return out[:n], out[n:]


def _shard_copies(s_refs, land_refs, send_sems, recv_sems):
    x, y, c, chips = _place()
    me_chip = 2 * x + y
    return [pltpu.make_async_remote_copy(src_ref=s_refs[t], dst_ref=land_refs[t].at[me_chip],
                                         send_sem=send_sems.at[3 * t + j], recv_sem=recv_sems.at[3 * t + j],
                                         device_id=(*chip, c), device_id_type=MESH)
            for t in range(len(s_refs)) for j, chip in enumerate(chips)]


def _gather_late_start(shards):
    n = len(shards)

    def body(*refs):
        s_refs, land_refs = refs[:n], refs[n:2 * n]
        send_sems, recv_sems = refs[2 * n:2 * n + 2]
        token = refs[-1]
        for cp in _shard_copies(s_refs, land_refs, send_sems, recv_sems):
            cp.start()
        token[...] = jnp.zeros_like(token)

    lands = [(N_CHIPS,) + s.shape for s in shards]
    args = [pltpu.with_memory_space_constraint(s, pltpu.HBM) for s in shards]
    args += [pltpu.with_memory_space_constraint(lax.empty(shp, s.dtype), pltpu.HBM) for shp, s in zip(lands, shards)]
    out = pl.pallas_call(
        body,
        name="gather_late_start",
        out_shape=(pltpu.SemaphoreType.DMA((3 * n,)), pltpu.SemaphoreType.DMA((3 * n,)),
                   *[pltpu.HBM(s.shape, s.dtype) for s in shards], *[pltpu.HBM(shp, s.dtype) for shp, s in zip(lands, shards)],
                   jax.ShapeDtypeStruct((8, LANES), F32)),
        in_specs=[_HBM] * (2 * n),
        out_specs=(_SEM, _SEM, *([_HBM] * (2 * n)), pl.BlockSpec(memory_space=pltpu.VMEM)),
        input_output_aliases={i: 2 + i for i in range(2 * n)},
        compiler_params=pltpu.CompilerParams(has_side_effects=_DATAFLOW),
    )(*args)
    return out[0], out[1], out[2:2 + n], out[2 + n:2 + 2 * n], out[-1]


def _gather_late_wait(send_sems, recv_sems, s_thru, land_thru, after):
    n = len(s_thru)

    def body(*refs):
        s_refs, land_refs = refs[:n], refs[n:2 * n]
        ssem, rsem = refs[2 * n:2 * n + 2]
        for cp in _shard_copies(s_refs, land_refs, ssem, rsem):
            cp.wait_send()
            cp.wait_recv()

    out = pl.pallas_call(
        body,
        name="gather_late_wait",
        out_shape=(*[pltpu.HBM(s.shape, s.dtype) for s in s_thru], *[pltpu.HBM(l.shape, l.dtype) for l in land_thru]),
        in_specs=[_HBM] * (2 * n) + [_SEM, _SEM, ANY],
        out_specs=tuple([_HBM] * (2 * n)),
        input_output_aliases={i: i for i in range(2 * n)},
        compiler_params=pltpu.CompilerParams(has_side_effects=_DATAFLOW),
    )(*s_thru, *land_thru, send_sems, recv_sems, after)
    return out[:n], out[n:]


def _pair_swap(rs):
    n = len(rs)

    def body(*refs):
        r_refs, o_refs = refs[:n], refs[n:2 * n]
        send_sems, recv_sems = refs[2 * n:]
        x, y, c, _ = _place()
        cps = []
        for t in range(n):
            cp = pltpu.make_async_remote_copy(src_ref=r_refs[t], dst_ref=o_refs[t], send_sem=send_sems.at[t],
                                              recv_sem=recv_sems.at[t], device_id=(x, y, 1 - c), device_id_type=MESH)
            cp.start()
            cps.append(cp)
        for cp in cps:
            cp.wait()

    return pl.pallas_call(
        body,
        out_shape=tuple(jax.ShapeDtypeStruct(r.shape, r.dtype) for r in rs),
        in_specs=[ANY] * n,
        out_specs=tuple([ANY] * n),
        scratch_shapes=[pltpu.SemaphoreType.DMA((n,)), pltpu.SemaphoreType.DMA((n,))],
        name="pair_swap",
    )(*rs)


N_DEV = 8
LOSS_ROW = 4


def _small_allreduce(small):
    def body(s_ref, o_ref, all_ref, send_sems, recv_sems):
        x, y, c, _ = _place()
        me = 4 * x + 2 * y + c
        all_ref[me] = s_ref[...]
        cps = []
        for k in range(1, N_DEV):
            peer = tuple(1 - p if (k >> s) & 1 else p for p, s in ((x, 2), (y, 1), (c, 0)))
            cp = pltpu.make_async_remote_copy(src_ref=s_ref, dst_ref=all_ref.at[me], send_sem=send_sems.at[k - 1],
                                              recv_sem=recv_sems.at[k - 1], device_id=peer, device_id_type=MESH)
            cp.start()
            cps.append(cp)
        for cp in cps:
            cp.wait()
        tot = all_ref[0]
        for d in range(1, N_DEV):
            tot = tot + all_ref[d]
        o_ref[...] = tot
        o_ref[LOSS_ROW:LOSS_ROW + 1, :] = jnp.broadcast_to(jnp.sum(tot[LOSS_ROW:LOSS_ROW + 1, :], axis=1, keepdims=True),
                                                          (1, tot.shape[1]))

    vm = pl.BlockSpec(memory_space=pltpu.VMEM)
    return pl.pallas_call(
        body,
        out_shape=jax.ShapeDtypeStruct(small.shape, small.dtype),
        in_specs=[vm],
        out_specs=vm,
        scratch_shapes=[pltpu.VMEM((N_DEV,) + small.shape, small.dtype), pltpu.SemaphoreType.DMA((N_DEV - 1,)),
                        pltpu.SemaphoreType.DMA((N_DEV - 1,))],
        name="small_allreduce",
    )(small)


def _sum_pair(g, recv, cidx, *, tr, name):
    n, hr, cols = recv.shape
    nr = hr // tr

    def body(c_ref, g_ref, r_ref, o_ref):
        o_ref[...] = (g_ref[...].astype(F32) + r_ref[...].astype(F32)).astype(o_ref.dtype)

    grid_spec = pltpu.PrefetchScalarGridSpec(
        num_scalar_prefetch=1,
        grid=(n, nr),
        in_specs=[pl.BlockSpec((None, tr, cols), lambda k, i, c_ref: (k, c_ref[0] * nr + i, 0)),
                  pl.BlockSpec((None, tr, cols), lambda k, i, c_ref: (k, i, 0))],
        out_specs=pl.BlockSpec((None, tr, cols), lambda k, i, c_ref: (k, i, 0)),
    )
    return pl.pallas_call(body, out_shape=jax.ShapeDtypeStruct(recv.shape, BF16), grid_spec=grid_spec,
                          compiler_params=_cparams(), name=name)(cidx, g, recv)


def _sum_chips(p, *, tr, name):
    _, rows, cols = p.shape

    def body(p_ref, o_ref):
        tot = p_ref[0].astype(F32)
        for k in range(1, N_CHIPS):
            tot = tot + p_ref[k].astype(F32)
        o_ref[...] = tot

    return pl.pallas_call(
        body,
        out_shape=jax.ShapeDtypeStruct((rows, cols), F32),
        grid=(rows // tr,),
        in_specs=[pl.BlockSpec((N_CHIPS, tr, cols), lambda i: (0, i, 0))],
        out_specs=pl.BlockSpec((tr, cols), lambda i: (i, 0)),
        compiler_params=_cparams(),
        name=name,
    )(p)


def _adamw(w, g, m, v, *, tr, name):
    rows, cols = w.shape
    bc1 = 1.0 / (1.0 - ADAM_B1 ** ADAM_STEP)
    bc2 = 1.0 / (1.0 - ADAM_B2 ** ADAM_STEP)

    def body(w_ref, g_ref, m_ref, v_ref, d_ref, nm_ref, nv_ref):
        gv = g_ref[...]
        nm = ADAM_B1 * m_ref[...] + (1.0 - ADAM_B1) * gv
        nv = ADAM_B2 * v_ref[...] + (1.0 - ADAM_B2) * (gv * gv)
        d_ref[...] = -ADAM_LR * ((nm * bc1) / (jnp.sqrt(nv * bc2) + ADAM_EPS) + ADAM_WD * w_ref[...])
        nm_ref[...] = nm
        nv_ref[...] = nv

    spec = pl.BlockSpec((tr, cols), lambda i: (i, 0))
    sd = jax.ShapeDtypeStruct((rows, cols), F32)
    return pl.pallas_call(body, out_shape=(sd, sd, sd), grid=(rows // tr,), in_specs=[spec] * 4, out_specs=(spec,) * 3,
                          compiler_params=_cparams(), name=name)(w, g, m, v)


def _adamw_halves(w, own, sib, cidx, m, v, *, tr, name):
    rows, cols = w.shape
    hr = own.shape[0]
    nr = hr // tr
    assert rows == 2 * hr and hr % tr == 0
    bc1 = 1.0 / (1.0 - ADAM_B1 ** ADAM_STEP)
    bc2 = 1.0 / (1.0 - ADAM_B2 ** ADAM_STEP)

    def body(c_ref, w_ref, o_ref, s_ref, m_ref, v_ref, g_ref, d_ref, nm_ref, nv_ref):
        mine = (pl.program_id(0) // nr) == c_ref[0]
        gv = jnp.where(mine, o_ref[...], s_ref[...])
        nm = ADAM_B1 * m_ref[...] + (1.0 - ADAM_B1) * gv
        nv = ADAM_B2 * v_ref[...] + (1.0 - ADAM_B2) * (gv * gv)
        g_ref[...] = gv
        d_ref[...] = -ADAM_LR * ((nm * bc1) / (jnp.sqrt(nv * bc2) + ADAM_EPS) + ADAM_WD * w_ref[...])
        nm_ref[...] = nm
        nv_ref[...] = nv

    full = pl.BlockSpec((tr, cols), lambda i, c_ref: (i, 0))
    half = pl.BlockSpec((tr, cols), lambda i, c_ref: (i % nr, 0))
    sd = jax.ShapeDtypeStruct((rows, cols), F32)
    grid_spec = pltpu.PrefetchScalarGridSpec(num_scalar_prefetch=1, grid=(rows // tr,), in_specs=[full, half, half, full, full],
                                             out_specs=(full,) * 4)
    return pl.pallas_call(body, out_shape=(sd,) * 4, grid_spec=grid_spec, compiler_params=_cparams(), name=name)(
        cidx, w, own, sib, m, v)


def _pack_small(norm, mem_norm, final_norm, b_forget):
    rows = [norm.reshape(1, D_MODEL), mem_norm.reshape(1, D_MODEL), final_norm.reshape(1, D_MODEL),
            jnp.pad(b_forget.reshape(1, FOX_HEADS), ((0, 0), (0, D_MODEL - FOX_HEADS))), jnp.zeros((4, D_MODEL), F32)]
    return jnp.concatenate(rows, axis=0)


def _unpack_small(a):
    return a[0:1], a[3:4, :FOX_HEADS], a[1:2], a[2]


def kernel(x, mem, norm_g, w_in, b_forget, mem_norm_g, w_mem_kv, w_out, final_norm_g, loss_target, m_norm_g, m_w_in, m_b_forget, m_mem_norm_g, m_w_mem_kv, m_w_out, m_final_norm_g, v_norm_g, v_w_in, v_b_forget, v_mem_norm_g, v_w_mem_kv, v_w_out, v_final_norm_g):
    core = lax.axis_index("c").astype(jnp.int32)
    me_chip = (2 * lax.axis_index("x") + lax.axis_index("y")).astype(jnp.int32)
    cidx = core.reshape(1)

    def own_slot(arr, own):
        return lax.dynamic_update_slice(arr, own[None].astype(arr.dtype), (me_chip,) + (0,) * own.ndim)

    win_b, late = w_in[0].astype(BF16), [w_mem_kv[0].astype(BF16), w_out[0].astype(BF16)]
    g_in, = _gather_weights([win_b])
    g_in, late = lax.optimization_barrier((own_slot(g_in, win_b), late))
    w_r = _rearrange_w_in([g_in[k] for k in range(N_CHIPS)])
    *late_flight, early_token = _gather_late_start(late)

    def late_weights(after):
        shards, landed = _gather_late_wait(*late_flight, after)
        g_kv, g_out = (own_slot(g, s) for g, s in zip(landed, shards))
        return g_kv.reshape(D_MODEL, 2 * MEM_W), g_out.reshape(MIX_W, D_MODEL)

    trs = (128, 128, 256)
    names = ("w_in", "w_mem_kv", "w_out")
    flights = {}

    def exchange(slabs, nms, ts, tag):
        recv = _pair_exchange(slabs, name=f"pair_exchange_{tag}")
        pair = [_sum_pair(g, r, cidx, tr=tr, name=f"sum_pair_{nm}") for g, r, tr, nm in zip(slabs, recv, ts, nms)]
        if tag == "w_in":
            pair[0] = _w_in_grad_slabs(pair[0][0])
        *flights[tag], token = _chip_exchange_start(pair, tag=tag)
        return token

    def start_reduce_small(g_wkv, g_wo):
        slabs = [g_wkv.reshape(N_CHIPS, D_MODEL // N_CHIPS, 2 * MEM_W), g_wo.reshape(N_CHIPS, MIX_W // N_CHIPS, D_MODEL)]
        return exchange(slabs, names[1:], trs[1:], "small")

    def start_reduce(g_wr):
        return exchange([g_wr[None]], names[:1], trs[:1], "w_in")

    gx, g_wr, g_wkv, g_wo, small = _local_grads(x, mem, norm_g, w_r, b_forget, mem_norm_g, None, None, final_norm_g, loss_target,
                                                start_reduce=start_reduce, start_reduce_small=start_reduce_small,
                                                early_token=early_token, late_weights=late_weights)

    pair, landed = [], []
    for tag in ("w_in", "small"):
        p, l = _chip_exchange_wait(*flights[tag], small, tag=tag)
        pair += list(p)
        landed += list(l)
    got = [lax.dynamic_update_slice(g, lax.dynamic_slice(p, (me_chip, 0, 0), (1,) + p.shape[1:]), (me_chip, 0, 0))
           for g, p in zip(landed, pair)]
    red = [_sum_chips(p, tr=tr, name=f"sum_chips_{nm}") for p, tr, nm in zip(got, trs, names)]
    sib = _pair_swap(red)

    outs = {}
    for nm, r, s, w, m, v, tr in zip(names, red, sib, (w_in, w_mem_kv, w_out), (m_w_in, m_w_mem_kv, m_w_out),
                                     (v_w_in, v_w_mem_kv, v_w_out), trs):
        outs[nm] = tuple(a[None] for a in _adamw_halves(w[0], r, s, cidx, m[0], v[0], tr=tr, name=f"adamw_{nm}"))

    gsum = _small_allreduce(small)
    sd, sm, sv = _adamw(_pack_small(norm_g, mem_norm_g, final_norm_g, b_forget), gsum,
                        _pack_small(m_norm_g, m_mem_norm_g, m_final_norm_g, m_b_forget),
                        _pack_small(v_norm_g, v_mem_norm_g, v_final_norm_g, v_b_forget), tr=8, name="adamw_small")
    loss = gsum[LOSS_ROW, 0]

    def group(i, small_arr):
        ng, bf, mg, fg = _unpack_small(small_arr)
        return (ng, outs["w_in"][i], bf, mg, outs["w_mem_kv"][i], outs["w_out"][i], fg)

    return (loss, gx, *group(0, gsum), *group(1, sd), *group(2, sm), *group(3, sv))
```

```python
import functools
import math

import jax
import jax.numpy as jnp
from jax import lax
from jax.experimental import pallas as pl
from jax.experimental.pallas import tpu as pltpu

F32 = jnp.float32
BF16 = jnp.bfloat16

D_MODEL = 1024
SEQ = 2048
HEAD_DIM = 64
FOX_HEADS = 12
DIL_HEADS = 12
MEM_HEADS = 4
MEM_HEAD_DIM = 128
MEM_LEN = 256
FOX_W = FOX_HEADS * HEAD_DIM
DIL_W = DIL_HEADS * HEAD_DIM
MEM_W = MEM_HEADS * MEM_HEAD_DIM
MIX_W = FOX_W + DIL_W + MEM_W
DILATIONS = ((128, 1), (512, 4), (2048, 16))
ROPE_THETA = 500000.0
ROPE_DIM = HEAD_DIM // 4
RMS_EPS = 1e-6
NEG_INF = -1e30
IN_SIZES = [FOX_W] * 4 + [FOX_HEADS] + [DIL_W] * 4 + [MEM_W] * 2
IN_W = sum(IN_SIZES)

ADAM_LR = 0.001
ADAM_B1 = 0.9
ADAM_B2 = 0.999
ADAM_EPS = 1e-08
ADAM_WD = 0.01
ADAM_STEP = 10

LANES = 128
N_CHIPS = 4
PW = 7168
PWF = PW + 4 * LANES
C_FQ, C_FK, C_FV, C_FG = 0, 768, 1536, 2304
C_DQ, C_DK, C_DV, C_DG = 3072, 3840, 4608, 5376
C_MQ, C_MG = 6144, 6656
VMEM_LIMIT = 48 * 1024 * 1024


def _cparams(**kw):
    return pltpu.CompilerParams(vmem_limit_bytes=VMEM_LIMIT, **kw)


MM_CHUNK = 256


def _matmul(a, b, *, out_dtype, tm, tn, tk, name, mode="nn"):
    if mode == "tn":
        (kdim, m), n = a.shape, b.shape[1]
        a_spec = pl.BlockSpec((tk, tm), lambda i, j, k: (k, i))
        b_spec = pl.BlockSpec((tk, tn), lambda i, j, k: (k, j))
        dims = _T0
    elif mode == "nt":
        (m, kdim), n = a.shape, b.shape[0]
        a_spec = pl.BlockSpec((tm, tk), lambda i, j, k: (i, k))
        b_spec = pl.BlockSpec((tn, tk), lambda i, j, k: (j, k))
        dims = _NT
    else:
        (m, kdim), n = a.shape, b.shape[1]
        a_spec = pl.BlockSpec((tm, tk), lambda i, j, k: (i, k))
        b_spec = pl.BlockSpec((tk, tn), lambda i, j, k: (k, j))
        dims = (((1,), (0,)), ((), ()))
    nk = kdim // tk
    assert m % tm == 0 and n % tn == 0 and kdim % tk == 0

    def body(a_ref, b_ref, o_ref, *scratch):
        if nk == 1:
            bv = b_ref[...]
            for c0 in range(0, tm, min(tm, MM_CHUNK)):
                rows = pl.ds(c0, min(tm, MM_CHUNK))
                av = a_ref[:, rows] if mode == "tn" else a_ref[rows, :]
                o_ref[rows, :] = lax.dot_general(av, bv, dims, preferred_element_type=F32).astype(o_ref.dtype)
            return
        prod = lax.dot_general(a_ref[...], b_ref[...], dims, preferred_element_type=F32)
        acc_ref, = scratch
        k = pl.program_id(2)

        @pl.when(k == 0)
        def _():
            acc_ref[...] = prod

        @pl.when(k > 0)
        def _():
            acc_ref[...] += prod

        @pl.when(k == nk - 1)
        def _():
            o_ref[...] = acc_ref[...].astype(o_ref.dtype)

    return pl.pallas_call(
        body,
        out_shape=jax.ShapeDtypeStruct((m, n), out_dtype),
        grid=(m // tm, n // tn, nk),
        in_specs=[a_spec, b_spec],
        out_specs=pl.BlockSpec((tm, tn), lambda i, j, k: (i, j)),
        scratch_shapes=[pltpu.VMEM((tm, tn), F32)] if nk > 1 else [],
        compiler_params=_cparams(dimension_semantics=("parallel", "parallel", "arbitrary")),
        name=name,
    )(a, b)


def _rms_fwd(x, g, *, tm, name):
    t, d = x.shape

    def body(x_ref, g_ref, h_ref):
        xv = x_ref[...]
        r = lax.rsqrt(jnp.mean(xv * xv, axis=-1, keepdims=True) + RMS_EPS)
        h_ref[...] = (xv * r * g_ref[...]).astype(h_ref.dtype)

    return pl.pallas_call(
        body,
        out_shape=jax.ShapeDtypeStruct((t, d), BF16),
        grid=(t // tm,),
        in_specs=[pl.BlockSpec((tm, d), lambda i: (i, 0)), pl.BlockSpec((1, d), lambda i: (0, 0))],
        out_specs=pl.BlockSpec((tm, d), lambda i: (i, 0)),
        compiler_params=_cparams(),
        name=name,
    )(x, g)


def _rope_tables():
    half = ROPE_DIM // 2
    pos = jnp.arange(SEQ, dtype=F32)
    inv_freq = 1.0 / (ROPE_THETA ** (jnp.arange(0, ROPE_DIM, 2, dtype=F32) / ROPE_DIM))
    ang = pos[:, None] * inv_freq[None, :]
    cos, sin = jnp.cos(ang), jnp.sin(ang)
    one = jnp.ones((SEQ, HEAD_DIM - ROPE_DIM), F32)
    zero = jnp.zeros((SEQ, HEAD_DIM - ROPE_DIM), F32)
    zh = jnp.zeros((SEQ, half), F32)
    c = jnp.concatenate([cos, cos, one], axis=1)
    s1 = jnp.concatenate([zh, sin, zero], axis=1)
    s2 = jnp.concatenate([-sin, zh, zero], axis=1)
    rep = LANES // HEAD_DIM
    return jnp.tile(c, (1, rep)), jnp.tile(s1, (1, rep)), jnp.tile(s2, (1, rep))


def _rope_apply(t, c, s1, s2, transpose=False):
    n = t.shape[-1]
    rep = n // LANES
    c, s1, s2 = (jnp.tile(u, (1, rep)) for u in (c, s1, s2))
    half = ROPE_DIM // 2
    if not transpose:
        return t * c + pltpu.roll(t, half, 1) * s1 + pltpu.roll(t, n - half, 1) * s2
    return t * c + pltpu.roll(t * s1, n - half, 1) + pltpu.roll(t * s2, half, 1)


PROJ_CHUNK = 256


def _proj(x, g, w, tabs, *, n, tm, tn, name):
    t, d = x.shape
    assert C_DQ % tn == 0 and (C_DV - C_DQ) % tn == 0 and (C_DG - C_DQ) % tn == 0
    rope_lo, rope_hi, dil_hi = C_DQ // tn, C_DV // tn, C_DG // tn
    flog_blk, flog_at = PW // tn, PW % tn
    assert flog_at % LANES == 0 and flog_at + LANES <= tn
    s_blocks = SEQ // tm

    def body(x_ref, g_ref, w_ref, c_ref, s1_ref, s2_ref, h_ref, o_ref, f_ref, fl_ref, h_scr):
        j = pl.program_id(1)

        @pl.when(j == 0)
        def _():
            xv = x_ref[...]
            r = lax.rsqrt(jnp.mean(xv * xv, axis=-1, keepdims=True) + RMS_EPS)
            hv = (xv * r * g_ref[...]).astype(BF16)
            h_scr[...] = hv
            h_ref[...] = hv

        def tile(kind):
            wv = w_ref[...]
            for c0 in range(0, tm, PROJ_CHUNK):
                rows = pl.ds(c0, PROJ_CHUNK)
                acc = jnp.dot(h_scr[rows, :], wv, preferred_element_type=F32)
                if kind == "rope":
                    acc = _rope_apply(acc, c_ref[rows, :], s1_ref[rows, :], s2_ref[rows, :])
                o_ref[rows, :] = acc.astype(o_ref.dtype)
                if kind in ("rope", "dv"):
                    f_ref[rows, :] = acc
                if kind == "flog":
                    fl_ref[rows, :] = acc[:, flog_at:flog_at + LANES]

        is_rope = jnp.logical_and(j >= rope_lo, j < rope_hi)
        is_dv = jnp.logical_and(j >= rope_hi, j < dil_hi)
        is_flog = j == flog_blk
        pl.when(is_rope)(functools.partial(tile, "rope"))
        pl.when(is_dv)(functools.partial(tile, "dv"))
        pl.when(is_flog)(functools.partial(tile, "flog"))
        pl.when(jnp.logical_not(jnp.logical_or(jnp.logical_or(is_rope, is_dv), is_flog)))(functools.partial(tile, "plain"))

    tab_spec = pl.BlockSpec((tm, LANES), lambda i, j: (i % s_blocks, 0))
    f_spec = pl.BlockSpec((tm, tn), lambda i, j: (i, jnp.clip(j - rope_lo, 0, dil_hi - rope_lo - 1)))
    row = pl.BlockSpec((tm, d), lambda i, j: (i, 0))
    return pl.pallas_call(
        body,
        out_shape=(jax.ShapeDtypeStruct((t, d), BF16), jax.ShapeDtypeStruct((t, n), BF16),
                   jax.ShapeDtypeStruct((t, 3 * DIL_W), F32), jax.ShapeDtypeStruct((t, LANES), F32)),
        grid=(t // tm, n // tn),
        in_specs=[row, pl.BlockSpec((1, d), lambda i, j: (0, 0)), pl.BlockSpec((d, tn), lambda i, j: (0, j)),
                  tab_spec, tab_spec, tab_spec],
        out_specs=(row, pl.BlockSpec((tm, tn), lambda i, j: (i, j)), f_spec, pl.BlockSpec((tm, LANES), lambda i, j: (i, 0))),
        scratch_shapes=[pltpu.VMEM((tm, d), BF16)],
        compiler_params=_cparams(dimension_semantics=("parallel", "arbitrary")),
        name=name,
    )(x, g, w, *tabs)


def _split3(x):
    hi = x.astype(BF16)
    r1 = x - hi.astype(F32)
    mid = r1.astype(BF16)
    lo = (r1 - mid.astype(F32)).astype(BF16)
    return hi, mid, lo


def _dot3(sel, x, sel_is_lhs):
    out = None
    for piece in _split3(x):
        t = jnp.dot(sel, piece, preferred_element_type=F32) if sel_is_lhs else jnp.dot(piece, sel, preferred_element_type=F32)
        out = t if out is None else out + t
    return out


def _flog_fwd(flog, bpad, *, nb, ts, name):
    ns = SEQ // ts

    def body(f_ref, b_ref, c_ref, carry_ref):
        s = pl.program_id(1)

        @pl.when(s == 0)
        def _():
            carry_ref[...] = jnp.zeros_like(carry_ref)

        z = f_ref[...] + b_ref[...]
        logf = jnp.minimum(z, 0.0) - jnp.log(1.0 + jnp.exp(-jnp.abs(z)))
        r = lax.broadcasted_iota(jnp.int32, (ts, ts), 0)
        c = lax.broadcasted_iota(jnp.int32, (ts, ts), 1)
        tri = jnp.where(r >= c, 1.0, 0.0).astype(BF16)
        cs = _dot3(tri, logf, True) + carry_ref[0:1, :]
        carry_ref[...] = jnp.broadcast_to(cs[ts - 1:ts, :], carry_ref.shape)
        c_ref[...] = cs

    return pl.pallas_call(
        body,
        out_shape=jax.ShapeDtypeStruct((nb * SEQ, LANES), F32),
        grid=(nb, ns),
        in_specs=[pl.BlockSpec((ts, LANES), lambda b, s: (b * ns + s, 0)), pl.BlockSpec((1, LANES), lambda b, s: (0, 0))],
        out_specs=pl.BlockSpec((ts, LANES), lambda b, s: (b * ns + s, 0)),
        scratch_shapes=[pltpu.VMEM((8, LANES), F32)],
        compiler_params=_cparams(dimension_semantics=("parallel", "arbitrary")),
        name=name,
    )(flog, bpad)


def _flog_bwd(dcol, flog, bpad, *, nb, ts, name):
    ns = SEQ // ts

    def body(d_ref, f_ref, b_ref, o_ref, gb_ref, carry_ref):
        bi = pl.program_id(0)
        s = pl.program_id(1)

        @pl.when(s == 0)
        def _():
            carry_ref[...] = jnp.zeros_like(carry_ref)

        @pl.when(jnp.logical_and(bi == 0, s == 0))
        def _():
            gb_ref[...] = jnp.zeros_like(gb_ref)

        r = lax.broadcasted_iota(jnp.int32, (ts, ts), 0)
        c = lax.broadcasted_iota(jnp.int32, (ts, ts), 1)
        tri = jnp.where(r <= c, 1.0, 0.0).astype(BF16)
        rc = _dot3(tri, d_ref[...], True) + carry_ref[0:1, :]
        carry_ref[...] = jnp.broadcast_to(rc[0:1, :], carry_ref.shape)
        z = f_ref[...] + b_ref[...]
        dz = rc / (1.0 + jnp.exp(z))
        o_ref[...] = dz.astype(o_ref.dtype)
        gb_ref[...] += jnp.broadcast_to(jnp.sum(dz, axis=0, keepdims=True), gb_ref.shape)

    rev = lambda b, s: (b * ns + (ns - 1 - s), 0)
    return pl.pallas_call(
        body,
        out_shape=(jax.ShapeDtypeStruct((nb * SEQ, LANES), BF16), jax.ShapeDtypeStruct((8, LANES), F32)),
        grid=(nb, ns),
        in_specs=[pl.BlockSpec((ts, LANES), rev), pl.BlockSpec((ts, LANES), rev), pl.BlockSpec((1, LANES), lambda b, s: (0, 0))],
        out_specs=(pl.BlockSpec((ts, LANES), rev), pl.BlockSpec((8, LANES), lambda b, s: (0, 0))),
        scratch_shapes=[pltpu.VMEM((8, LANES), F32)],
        compiler_params=_cparams(dimension_semantics=("arbitrary", "arbitrary")),
        name=name,
    )(dcol, flog, bpad)


MEM_TQ = 256
MEM_SET = 4
MEM_SCALE = 1.0 / math.sqrt(MEM_HEAD_DIM)
assert MEM_HEAD_DIM == LANES and SEQ % (MEM_TQ * MEM_SET) == 0


def _head_masks(nh):
    lane = lax.broadcasted_iota(jnp.int32, (1, LANES), 1)
    return [None] if nh == 1 else [lane < HEAD_DIM, lane >= HEAD_DIM]


def _mem_specs(qoff):
    qspec = pl.BlockSpec((None, SEQ, LANES), lambda b, j: (b, 0, qoff + j))
    kspec = pl.BlockSpec((None, MEM_LEN, LANES), lambda b, j: (b, 0, j))
    vspec = pl.BlockSpec((None, MEM_LEN, LANES), lambda b, j: (b, 0, MEM_HEADS + j))
    ospec = pl.BlockSpec((None, SEQ, LANES), lambda b, j: (b, 0, j))
    return qspec, kspec, vspec, ospec


def _mem_rows(g):
    return [pl.ds(pl.multiple_of((MEM_SET * g + a) * MEM_TQ, MEM_TQ), MEM_TQ) for a in range(MEM_SET)]


def _mem_fwd(p3, mkv3, *, qoff, name):
    nb = p3.shape[0]

    def body(q_ref, k_ref, v_ref, o_ref, lse_ref):
        kb, vb = k_ref[...], v_ref[...]

        def qset(g, c):
            rows = _mem_rows(g)
            ss = [lax.dot_general(q_ref[r, :] * MEM_SCALE, kb, _NT, preferred_element_type=F32) for r in rows]
            for r, s in zip(rows, ss):
                m = jnp.max(s, axis=1, keepdims=True)
                p = jnp.exp(s - m)
                l = jnp.sum(p, axis=1, keepdims=True)
                o_ref[r, :] = jnp.dot(p.astype(BF16), vb, preferred_element_type=F32) / l
                lse_ref[r, :] = jnp.broadcast_to(m + jnp.log(l), (MEM_TQ, LANES))
            return c

        lax.fori_loop(0, SEQ // MEM_TQ // MEM_SET, qset, 0)

    qspec, kspec, vspec, ospec = _mem_specs(qoff)
    osd = jax.ShapeDtypeStruct((nb, SEQ, MEM_W), F32)
    return pl.pallas_call(body, out_shape=(osd, osd), grid=(nb, MEM_HEADS), in_specs=[qspec, kspec, vspec],
                          out_specs=(ospec, ospec), compiler_params=_cparams(dimension_semantics=("parallel", "parallel")),
                          name=name)(p3, mkv3, mkv3)


def _mem_bwd(p3, mkv3, do, o, lse, *, qoff, do_off, name):
    nb = p3.shape[0]

    def body(q_ref, k_ref, v_ref, do_ref, o_ref, lse_ref, dq_ref, dk_ref, dv_ref):
        kb, vb = k_ref[...], v_ref[...]
        ks = kb * MEM_SCALE

        def qset(g, carry):
            dk, dv = carry
            work = []
            for r in _mem_rows(g):
                qs = q_ref[r, :] * MEM_SCALE
                dob = do_ref[r, :].astype(BF16)
                s = lax.dot_general(qs, kb, _NT, preferred_element_type=F32)
                dp = lax.dot_general(dob, vb, _NT, preferred_element_type=F32)
                work.append((r, qs, dob, s, dp))
            for r, qs, dob, s, dp in work:
                delta = jnp.sum(dob.astype(F32) * o_ref[r, :], axis=1, keepdims=True)
                p = jnp.exp(s - lse_ref[r, :][:, 0:1])
                ds = (p * (dp - delta)).astype(BF16)
                dq_ref[r, :] = jnp.dot(ds, ks, preferred_element_type=F32).astype(dq_ref.dtype)
                dk = dk + lax.dot_general(ds, qs, _T0, preferred_element_type=F32)
                dv = dv + lax.dot_general(p.astype(BF16), dob, _T0, preferred_element_type=F32)
            return dk, dv

        z = jnp.zeros((MEM_LEN, LANES), F32)
        dk, dv = lax.fori_loop(0, SEQ // MEM_TQ // MEM_SET, qset, (z, z))
        dk_ref[...] = dk
        dv_ref[...] = dv

    qspec, kspec, vspec, ospec = _mem_specs(qoff)
    dospec = pl.BlockSpec((None, SEQ, LANES), lambda b, j: (b, 0, do_off + j))
    kvo = pl.BlockSpec((None, MEM_LEN, LANES), lambda b, j: (b, 0, j))
    kvsd = jax.ShapeDtypeStruct((nb, MEM_LEN, MEM_W), F32)
    return pl.pallas_call(
        body, out_shape=(jax.ShapeDtypeStruct((nb, SEQ, MEM_W), BF16), kvsd, kvsd), grid=(nb, MEM_HEADS),
        in_specs=[qspec, kspec, vspec, dospec, ospec, ospec], out_specs=(ospec, kvo, kvo),
        compiler_params=_cparams(dimension_semantics=("parallel", "parallel")), name=name)(p3, mkv3, mkv3, do, o, lse)


BLK = 128
NBLK = SEQ // BLK
QK_SCALE = 1.0 / math.sqrt(HEAD_DIM)
DIL_STEPS = tuple(d for _, d in DILATIONS)
assert all(w // d == BLK for w, d in DILATIONS)
_T0 = (((0,), (0,)), ((), ()))
_NT = (((1,), (1,)), ((), ()))


def _stack_heads(a, masks):
    z = jnp.zeros_like(a)
    return jnp.concatenate([jnp.where(masks[0], a, z), jnp.where(masks[1], a, z)], axis=0)


def _tri_bias(lower):
    r = lax.broadcasted_iota(jnp.int32, (BLK, BLK), 0)
    c = lax.broadcasted_iota(jnp.int32, (BLK, BLK), 1)
    return jnp.where((c <= r) if lower else (c >= r), 0.0, NEG_INF).astype(F32)


def _dil_rows(r, i, d):
    start = r + i * (BLK * d)
    return pl.ds(start, BLK) if d == 1 else pl.ds(start, BLK, stride=d)


DIL_SET = 4


def _dil_sets(d, fn):
    nbk = SEQ // d // BLK
    if d == 1:
        n = 2 * DIL_SET
        def gbody(g, c):
            fn([(0, n * g + a, None if a == 0 else True) for a in range(n)])
            return c
        lax.fori_loop(0, nbk // n, gbody, 0)
    elif nbk > 1:
        assert nbk == DIL_SET
        def rbody(r, c):
            fn([(r, i, i > 0) for i in range(nbk)])
            return c
        lax.fori_loop(0, d, rbody, 0)
    else:
        def rbody(rr, c):
            fn([(DIL_SET * rr + a, 0, False) for a in range(DIL_SET)])
            return c
        lax.fori_loop(0, d // DIL_SET, rbody, 0)


def _dil_key_tiles(r, i, d, has_prev, qrows, tri_cur, tri_prev):
    tiles = [(qrows, tri_cur)]
    if has_prev is None:
        tiles.append((_dil_rows(r, jnp.maximum(i - 1, 0), d), tri_prev + jnp.where(i > 0, 0.0, NEG_INF)))
    elif has_prev:
        tiles.append((_dil_rows(r, i - 1, d), tri_prev))
    return tiles


def _dil_fwd(qkv, *, name):
    nb = qkv.shape[0]
    ncol = DIL_W // LANES
    hd = HEAD_DIM

    def body(q_ref, k_ref, v_ref, o_ref, lse_ref, m_ref, l_ref, a_ref):
        masks = _head_masks(2)
        tri_cur, tri_prev = _tri_bias(True), _tri_bias(False)
        for pi, d in enumerate(DIL_STEPS):
            first, last = pi == 0, pi == len(DIL_STEPS) - 1

            def qset(blocks, d=d, first=first, last=last):
                work = []
                for r, i, has_prev in blocks:
                    qrows = _dil_rows(r, i, d)
                    qcat = _stack_heads((q_ref[qrows, :] * QK_SCALE).astype(BF16), masks)
                    ss, krs = [], []
                    for krows, bias in _dil_key_tiles(r, i, d, has_prev, qrows, tri_cur, tri_prev):
                        s = lax.dot_general(qcat, k_ref[krows, :].astype(BF16), _NT, preferred_element_type=F32)
                        ss.append((s[:BLK] + bias, s[BLK:] + bias))
                        krs.append(krows)
                    work.append((qrows, ss, krs))
                for qrows, ss, krs in work:
                    e0 = ss[0][0] if len(ss) == 1 else jnp.maximum(ss[0][0], ss[1][0])
                    e1 = ss[0][1] if len(ss) == 1 else jnp.maximum(ss[0][1], ss[1][1])
                    n0 = jnp.max(e0, axis=1, keepdims=True)
                    n1 = jnp.max(e1, axis=1, keepdims=True)
                    if not first:
                        mo, lo = m_ref[qrows, :], l_ref[qrows, :]
                        m0, m1 = mo[:, 0:1], mo[:, hd:hd + 1]
                        n0, n1 = jnp.maximum(n0, m0), jnp.maximum(n1, m1)
                        a0, a1 = jnp.exp(m0 - n0), jnp.exp(m1 - n1)
                    ps = [(jnp.exp(s0 - n0), jnp.exp(s1 - n1)) for s0, s1 in ss]
                    t0 = ps[0][0] if len(ps) == 1 else ps[0][0] + ps[1][0]
                    t1 = ps[0][1] if len(ps) == 1 else ps[0][1] + ps[1][1]
                    l0 = jnp.sum(t0, axis=1, keepdims=True)
                    l1 = jnp.sum(t1, axis=1, keepdims=True)
                    acc = None
                    for (p0, p1), krows in zip(ps, krs):
                        vcat = _stack_heads(v_ref[krows, :].astype(BF16), masks)
                        pv = jnp.dot(jnp.concatenate([p0, p1], axis=1).astype(BF16), vcat, preferred_element_type=F32)
                        acc = pv if acc is None else acc + pv
                    if not first:
                        l0 = l0 + a0 * lo[:, 0:1]
                        l1 = l1 + a1 * lo[:, hd:hd + 1]
                        acc = acc + a_ref[qrows, :] * jnp.where(masks[0], a0, a1)
                    if last:
                        o_ref[qrows, :] = acc / jnp.where(masks[0], l0, l1)
                        lse_ref[qrows, :] = jnp.where(masks[0], n0 + jnp.log(l0), n1 + jnp.log(l1))
                    else:
                        m_ref[qrows, :] = jnp.where(masks[0], n0, n1)
                        l_ref[qrows, :] = jnp.where(masks[0], l0, l1)
                        a_ref[qrows, :] = acc

            _dil_sets(d, qset)

    spec = lambda off: pl.BlockSpec((None, SEQ, LANES), lambda b, j: (b, 0, off + j))
    ospec = pl.BlockSpec((None, SEQ, LANES), lambda b, j: (b, 0, j))
    osd = jax.ShapeDtypeStruct((nb, SEQ, DIL_W), F32)
    return pl.pallas_call(
        body, out_shape=(osd, osd), grid=(nb, ncol),
        in_specs=[spec(0), spec(ncol), spec(2 * ncol)], out_specs=(ospec, ospec),
        scratch_shapes=[pltpu.VMEM((SEQ, LANES), F32)] * 3,
        compiler_params=_cparams(dimension_semantics=("parallel", "parallel")), name=name,
    )(qkv, qkv, qkv)


def _dil_bwd(qkv, do, o, lse, tabs, *, do_off, name):
    nb = qkv.shape[0]
    ncol = DIL_W // LANES
    hd = HEAD_DIM

    def body(q_ref, k_ref, v_ref, do_ref, o_ref, lse_ref, c_ref, s1_ref, s2_ref, dqo_ref, dko_ref, dvo_ref,
             dq_ref, dk_ref, dv_ref, dl_ref, dof_ref):
        masks = _head_masks(2)
        tri_cur, tri_prev = _tri_bias(True), _tri_bias(False)
        dq_ref[...] = jnp.zeros_like(dq_ref)
        dk_ref[...] = jnp.zeros_like(dk_ref)
        dv_ref[...] = jnp.zeros_like(dv_ref)

        def delta_body(i, c):
            rows = pl.ds(pl.multiple_of(i * BLK, BLK), BLK)
            dof = do_ref[rows, :].astype(F32)
            dof_ref[rows, :] = dof
            prod = dof * o_ref[rows, :]
            z = jnp.zeros_like(prod)
            dl_ref[rows, :] = jnp.where(masks[0], jnp.sum(jnp.where(masks[0], prod, z), axis=1, keepdims=True),
                                        jnp.sum(jnp.where(masks[1], prod, z), axis=1, keepdims=True))
            return c

        lax.fori_loop(0, NBLK, delta_body, 0)

        for d in DIL_STEPS:
            def qset(blocks, d=d):
                work = []
                for r, i, has_prev in blocks:
                    qrows = _dil_rows(r, i, d)
                    qcat = _stack_heads((q_ref[qrows, :] * QK_SCALE).astype(BF16), masks)
                    docat = _stack_heads(dof_ref[qrows, :].astype(BF16), masks)
                    tiles = []
                    for krows, bias in _dil_key_tiles(r, i, d, has_prev, qrows, tri_cur, tri_prev):
                        s = lax.dot_general(qcat, k_ref[krows, :].astype(BF16), _NT, preferred_element_type=F32)
                        dp = lax.dot_general(docat, v_ref[krows, :].astype(BF16), _NT, preferred_element_type=F32)
                        tiles.append((krows, s, dp, bias))
                    work.append((qrows, qcat, docat, tiles))
                for qrows, qcat, docat, tiles in work:
                    lseb, dlb = lse_ref[qrows, :], dl_ref[qrows, :]
                    lse0, lse1 = lseb[:, 0:1], lseb[:, hd:hd + 1]
                    dl0, dl1 = dlb[:, 0:1], dlb[:, hd:hd + 1]
                    dq = None
                    for krows, s, dp, bias in tiles:
                        p0 = jnp.exp(s[:BLK] + bias - lse0)
                        p1 = jnp.exp(s[BLK:] + bias - lse1)
                        ds0 = p0 * (dp[:BLK] - dl0)
                        ds1 = p1 * (dp[BLK:] - dl1)
                        ds0b, ds1b = ds0.astype(BF16), ds1.astype(BF16)
                        pcat = jnp.concatenate([p0.astype(BF16), p1.astype(BF16)], axis=0)
                        dscat = jnp.concatenate([ds0b, ds1b], axis=0)
                        dv_ref[krows, :] += lax.dot_general(pcat, docat, _T0, preferred_element_type=F32)
                        dk_ref[krows, :] += lax.dot_general(dscat, qcat, _T0, preferred_element_type=F32)
                        dsrow = jnp.concatenate([ds0b, ds1b], axis=1)
                        kcat = _stack_heads((k_ref[krows, :] * QK_SCALE).astype(BF16), masks)
                        t = jnp.dot(dsrow, kcat, preferred_element_type=F32)
                        dq = t if dq is None else dq + t
                    dq_ref[qrows, :] += dq

            _dil_sets(d, qset)

        def out_body(i, c):
            rows = pl.ds(pl.multiple_of(i * BLK, BLK), BLK)
            tab = (c_ref[rows, :], s1_ref[rows, :], s2_ref[rows, :])
            dqo_ref[rows, :] = _rope_apply(dq_ref[rows, :], *tab, transpose=True).astype(dqo_ref.dtype)
            dko_ref[rows, :] = _rope_apply(dk_ref[rows, :], *tab, transpose=True).astype(dko_ref.dtype)
            dvo_ref[rows, :] = dv_ref[rows, :].astype(dvo_ref.dtype)
            return c

        lax.fori_loop(0, NBLK, out_body, 0)

    spec = lambda off: pl.BlockSpec((None, SEQ, LANES), lambda b, j: (b, 0, off + j))
    ospec = pl.BlockSpec((None, SEQ, LANES), lambda b, j: (b, 0, j))
    tspec = pl.BlockSpec((SEQ, LANES), lambda b, j: (0, 0))
    osd = jax.ShapeDtypeStruct((nb, SEQ, DIL_W), BF16)
    return pl.pallas_call(
        body, out_shape=(osd, osd, osd), grid=(nb, ncol),
        in_specs=[spec(0), spec(ncol), spec(2 * ncol), spec(do_off), ospec, ospec, tspec, tspec, tspec],
        out_specs=(ospec, ospec, ospec),
        scratch_shapes=[pltpu.VMEM((SEQ, LANES), F32)] * 5,
        compiler_params=_cparams(dimension_semantics=("parallel", "parallel")), name=name,
    )(qkv, qkv, qkv, do, o, lse, *tabs)


FOX_FWD_GROUP = 16
FOX_BWD_GROUP = 4
assert NBLK % FOX_FWD_GROUP == 0 and NBLK % FOX_BWD_GROUP == 0
_FOX_COLS = tuple(c // LANES for c in (C_FQ, C_FK, C_FV))


def _fox_specs():
    cols = [pl.BlockSpec((None, SEQ, LANES), (lambda b, j, off=off: (b, 0, off + j))) for off in _FOX_COLS]
    ospec = pl.BlockSpec((None, SEQ, LANES), lambda b, j: (b, 0, j))
    crspec = pl.BlockSpec((None, None, NBLK, 8, BLK), lambda b, j: (b, j, 0, 0, 0))
    return cols, ospec, crspec


def _fox_key_rows(t, e, g):
    return pl.ds(pl.multiple_of((g * t + e) * BLK, BLK), BLK)


def _fox_fwd(p3, crow, *, name):
    nb = p3.shape[0]
    g = FOX_FWD_GROUP

    def body(q_ref, k_ref, v_ref, cr_ref, o_ref, lse_ref):
        masks = _head_masks(2)
        tri = _tri_bias(True)

        def qk(qcat, t, nblk=g):
            return tuple(lax.dot_general(qcat, k_ref[_fox_key_rows(t, e, g), :], _NT, preferred_element_type=F32) for e in range(nblk))

        def consume(ss, t, state, nblk, diag):
            m0, m1, l0, l1, acc = state
            us = []
            for e in range(nblk):
                cr = cr_ref[g * t + e]
                u0 = ss[e][:BLK] - cr[0:1, :]
                u1 = ss[e][BLK:] - cr[1:2, :]
                if diag and e == nblk - 1:
                    u0, u1 = u0 + tri, u1 + tri
                us.append((u0, u1))
            x0 = functools.reduce(jnp.maximum, [u[0] for u in us])
            x1 = functools.reduce(jnp.maximum, [u[1] for u in us])
            n0 = jnp.maximum(m0, jnp.max(x0, axis=1, keepdims=True))
            n1 = jnp.maximum(m1, jnp.max(x1, axis=1, keepdims=True))
            a0, a1 = jnp.exp(m0 - n0), jnp.exp(m1 - n1)
            acc = acc * jnp.where(masks[0], a0, a1)
            t0 = t1 = None
            for e in range(nblk):
                p0, p1 = jnp.exp(us[e][0] - n0), jnp.exp(us[e][1] - n1)
                t0 = p0 if t0 is None else t0 + p0
                t1 = p1 if t1 is None else t1 + p1
                pcat = jnp.concatenate([p0, p1], axis=1)
                hi = pcat.astype(BF16)
                lo = (pcat - hi.astype(F32)).astype(BF16)
                vcat = _stack_heads(v_ref[_fox_key_rows(t, e, g), :], masks)
                acc = acc + jnp.dot(hi, vcat, preferred_element_type=F32) + jnp.dot(lo, vcat, preferred_element_type=F32)
            l0 = a0 * l0 + jnp.sum(t0, axis=1, keepdims=True)
            l1 = a1 * l1 + jnp.sum(t1, axis=1, keepdims=True)
            return n0, n1, l0, l1, acc

        def gbody(ng, c):
            neg = jnp.full((BLK, 1), NEG_INF, F32)
            z1 = jnp.zeros((BLK, 1), F32)
            rows = [pl.ds(pl.multiple_of((g * ng + a) * BLK, BLK), BLK) for a in range(g)]
            qcats = [_stack_heads(q_ref[rows[a], :] * QK_SCALE, masks) for a in range(g)]
            def step(t, cc):
                cur = [qk(qcats[a], t) for a in range(g)]
                return tuple(consume(cur[a], t, cc[a], g, False) for a in range(g))

            init = (neg, neg, z1, z1, jnp.zeros((BLK, LANES), F32))
            done = lax.fori_loop(0, ng, step, tuple(init for a in range(g)))
            last = [qk(qcats[a], ng, a + 1) for a in range(g)]
            for a in range(g):
                ss, state = last[a], done[a]
                m0, m1, l0, l1, acc = consume(ss, ng, state, a + 1, True)
                o_ref[rows[a], :] = acc / jnp.where(masks[0], l0, l1)
                lse_ref[rows[a], :] = jnp.where(masks[0], m0 + jnp.log(l0), m1 + jnp.log(l1))
            return c

        lax.fori_loop(0, NBLK // g, gbody, 0)

    cols, ospec, crspec = _fox_specs()
    osd = jax.ShapeDtypeStruct((nb, SEQ, FOX_W), F32)
    return pl.pallas_call(
        body, out_shape=(osd, osd), grid=(nb, FOX_W // LANES), in_specs=cols + [crspec], out_specs=(ospec, ospec),
        compiler_params=_cparams(dimension_semantics=("parallel", "parallel")), name=name,
    )(p3, p3, p3, crow)


def _fox_bwd(p3, crow, do, o, lse, *, do_off, name):
    nb = p3.shape[0]
    g = FOX_BWD_GROUP
    hd = HEAD_DIM

    def body(q_ref, k_ref, v_ref, cr_ref, do_ref, o_ref, lse_ref, dq_ref, dko_ref, dvo_ref, dcr_ref, dk_ref, dv_ref):
        masks = _head_masks(2)
        tri = _tri_bias(True)
        dk_ref[...] = jnp.zeros_like(dk_ref)
        dv_ref[...] = jnp.zeros_like(dv_ref)
        dcr_ref[...] = jnp.zeros_like(dcr_ref)

        def products(qcat, docat, t, nblk=g):
            out = []
            for e in range(nblk):
                krows = _fox_key_rows(t, e, g)
                out.append(lax.dot_general(qcat, k_ref[krows, :], _NT, preferred_element_type=F32))
                out.append(lax.dot_general(docat, v_ref[krows, :], _NT, preferred_element_type=F32))
            return tuple(out)

        def consume(prod, t, ctx, dq, nblk, diag):
            qcat, docat, lse0, lse1, dl0, dl1 = ctx
            for e in range(nblk):
                jb = g * t + e
                krows = _fox_key_rows(t, e, g)
                s, dp = prod[2 * e], prod[2 * e + 1]
                cr = cr_ref[jb]
                u0 = s[:BLK] - cr[0:1, :]
                u1 = s[BLK:] - cr[1:2, :]
                if diag and e == nblk - 1:
                    u0, u1 = u0 + tri, u1 + tri
                p0 = jnp.exp(u0 - lse0)
                p1 = jnp.exp(u1 - lse1)
                ds0 = p0 * (dp[:BLK] - dl0)
                ds1 = p1 * (dp[BLK:] - dl1)
                dcr_ref[jb, 0:1, :] += jnp.sum(ds0, axis=0, keepdims=True)
                dcr_ref[jb, 1:2, :] += jnp.sum(ds1, axis=0, keepdims=True)
                ds0b, ds1b = ds0.astype(BF16), ds1.astype(BF16)
                pcat = jnp.concatenate([p0.astype(BF16), p1.astype(BF16)], axis=0)
                dscat = jnp.concatenate([ds0b, ds1b], axis=0)
                dv_ref[krows, :] += lax.dot_general(pcat, docat, _T0, preferred_element_type=F32)
                dk_ref[krows, :] += lax.dot_general(dscat, qcat, _T0, preferred_element_type=F32)
                dsrow = jnp.concatenate([ds0b, ds1b], axis=1)
                dq = dq + jnp.dot(dsrow, _stack_heads(k_ref[krows, :] * QK_SCALE, masks), preferred_element_type=F32)
            return dq

        def gbody(ng, c):
            ctxs, rows = [], []
            for a in range(g):
                r = pl.ds(pl.multiple_of((g * ng + a) * BLK, BLK), BLK)
                qcat = _stack_heads(q_ref[r, :] * QK_SCALE, masks)
                dob = do_ref[r, :].astype(BF16)
                prod = dob.astype(F32) * o_ref[r, :]
                z = jnp.zeros_like(prod)
                dl0 = jnp.sum(jnp.where(masks[0], prod, z), axis=1, keepdims=True)
                dl1 = jnp.sum(jnp.where(masks[1], prod, z), axis=1, keepdims=True)
                lseb = lse_ref[r, :]
                ctxs.append((qcat, _stack_heads(dob, masks), lseb[:, 0:1], lseb[:, hd:hd + 1], dl0, dl1))
                rows.append(r)
            def step(t, cc):
                cur = [products(ctxs[a][0], ctxs[a][1], t) for a in range(g)]
                return tuple(consume(cur[a], t, ctxs[a], cc[a], g, False) for a in range(g))

            done = lax.fori_loop(0, ng, step, tuple(jnp.zeros((BLK, LANES), F32) for a in range(g)))
            last = [products(ctxs[a][0], ctxs[a][1], ng, a + 1) for a in range(g)]
            for a in range(g):
                dq_ref[rows[a], :] = consume(last[a], ng, ctxs[a], done[a], a + 1, True).astype(dq_ref.dtype)
            return c

        lax.fori_loop(0, NBLK // g, gbody, 0)
        dko_ref[...] = dk_ref[...].astype(dko_ref.dtype)
        dvo_ref[...] = dv_ref[...].astype(dvo_ref.dtype)

    cols, ospec, crspec = _fox_specs()
    dospec = pl.BlockSpec((None, SEQ, LANES), lambda b, j: (b, 0, do_off + j))
    osd = jax.ShapeDtypeStruct((nb, SEQ, FOX_W), BF16)
    return pl.pallas_call(
        body, out_shape=(osd, osd, osd, jax.ShapeDtypeStruct((nb, FOX_W // LANES, NBLK, 8, BLK), F32)),
        grid=(nb, FOX_W // LANES), in_specs=cols + [crspec, dospec, ospec, ospec], out_specs=(ospec, ospec, ospec, crspec),
        scratch_shapes=[pltpu.VMEM((SEQ, LANES), F32)] * 2,
        compiler_params=_cparams(dimension_semantics=("parallel", "parallel")), name=name,
    )(p3, p3, p3, crow, do, o, lse)


_B1, _B2 = FOX_W // LANES, (FOX_W + DIL_W) // LANES


def _dy_gate_bwd(dx2b, wo, fox, dil, memo, p16, *, tm, tn, name):
    t, d = dx2b.shape
    assert FOX_W % tn == 0 and DIL_W % tn == 0 and MEM_W % tn == 0 and all(c % tn == 0 for c in (C_FG, C_DG, C_MG))
    n1, n2, n3 = FOX_W // tn, (FOX_W + DIL_W) // tn, MIX_W // tn

    def body(dx_ref, w_ref, f_ref, d_ref, m_ref, g_ref, da_ref, dg_ref):
        j = pl.program_id(1)
        wv = w_ref[...]
        for c0 in range(0, tm, min(tm, 2 * MM_CHUNK)):
            rows = pl.ds(c0, min(tm, 2 * MM_CHUNK))
            dyv = lax.dot_general(dx_ref[rows, :], wv, _NT, preferred_element_type=F32)
            a = jnp.where(j < n1, f_ref[rows, :], jnp.where(j < n2, d_ref[rows, :], m_ref[rows, :]))
            gt = g_ref[rows, :].astype(F32)
            sg = 1.0 / (1.0 + jnp.exp(-gt))
            da_ref[rows, :] = (dyv * gt * sg).astype(da_ref.dtype)
            dg_ref[rows, :] = (dyv * a * sg * (1.0 + gt * (1.0 - sg))).astype(dg_ref.dtype)

    def gcol(j):
        return jnp.where(j < n1, C_FG // tn + j, jnp.where(j < n2, C_DG // tn + j - n1, C_MG // tn + j - n2))

    tile = pl.BlockSpec((tm, tn), lambda i, j: (i, j))
    return pl.pallas_call(
        body,
        out_shape=(jax.ShapeDtypeStruct((t, MIX_W), BF16), jax.ShapeDtypeStruct((t, MIX_W), BF16)),
        grid=(t // tm, n3),
        in_specs=[pl.BlockSpec((tm, d), lambda i, j: (i, 0)), pl.BlockSpec((tn, d), lambda i, j: (j, 0)),
                  pl.BlockSpec((tm, tn), lambda i, j: (i, jnp.minimum(j, n1 - 1))),
                  pl.BlockSpec((tm, tn), lambda i, j: (i, jnp.clip(j - n1, 0, n2 - n1 - 1))),
                  pl.BlockSpec((tm, tn), lambda i, j: (i, jnp.clip(j - n2, 0, n3 - n2 - 1))),
                  pl.BlockSpec((tm, tn), lambda i, j: (i, gcol(j)))],
        out_specs=(tile, tile),
        compiler_params=_cparams(dimension_semantics=("parallel", "parallel")),
        name=name,
    )(dx2b, wo, fox, dil, memo, p16)


def _silu(g):
    return g / (1.0 + jnp.exp(-g))


def _out_loss(fox, dil, memo, p16, wo, x, tgt, gfin, *, tm, name):
    t, d = x.shape
    n_feat = float(d)

    def body(f_ref, d_ref, m_ref, fg_ref, dg_ref, mg_ref, w_ref, x_ref, t_ref, g_ref, y_ref, dx_ref, dxb_ref, st_ref):
        i = pl.program_id(0)

        @pl.when(i == 0)
        def _():
            st_ref[...] = jnp.zeros_like(st_ref)

        wv, gv = w_ref[...], g_ref[...]
        half = tm // 2
        for c0 in (0, half):
            rows = pl.ds(c0, half)
            y = jnp.concatenate([(a_ref[rows, :] * _silu(gt_ref[rows, :].astype(F32))).astype(BF16)
                                 for a_ref, gt_ref in ((f_ref, fg_ref), (d_ref, dg_ref), (m_ref, mg_ref))], axis=1)
            y_ref[rows, :] = y
            x2 = x_ref[rows, :] + jnp.dot(y, wv, preferred_element_type=F32)
            r = lax.rsqrt(jnp.mean(x2 * x2, axis=-1, keepdims=True) + RMS_EPS)
            nrm = x2 * r
            err = nrm * gv - t_ref[rows, :]
            dout = err * (1.0 / n_feat)
            dn = dout * gv
            dx2 = r * (dn - nrm * jnp.mean(dn * nrm, axis=-1, keepdims=True))
            dx_ref[rows, :] = dx2
            dxb_ref[rows, :] = dx2.astype(dxb_ref.dtype)
            st_ref[0:1, :] += jnp.sum(dout * nrm, axis=0, keepdims=True)
            st_ref[1:2, :] += (0.5 / n_feat) * jnp.sum(err * err, axis=0, keepdims=True)

    row = pl.BlockSpec((tm, d), lambda i: (i, 0))
    whole = lambda w: pl.BlockSpec((tm, w), lambda i: (i, 0))
    gate = lambda w, col: pl.BlockSpec((tm, w), lambda i: (i, col // w))
    return pl.pallas_call(
        body,
        out_shape=(jax.ShapeDtypeStruct((t, MIX_W), BF16), jax.ShapeDtypeStruct((t, d), F32), jax.ShapeDtypeStruct((t, d), BF16),
                   jax.ShapeDtypeStruct((8, d), F32)),
        grid=(t // tm,),
        in_specs=[whole(FOX_W), whole(DIL_W), whole(MEM_W), gate(FOX_W, C_FG), gate(DIL_W, C_DG), gate(MEM_W, C_MG),
                  pl.BlockSpec((MIX_W, d), lambda i: (0, 0)), row, row, pl.BlockSpec((1, d), lambda i: (0, 0))],
        out_specs=(pl.BlockSpec((tm, MIX_W), lambda i: (i, 0)), row, row, pl.BlockSpec((8, d), lambda i: (0, 0))),
        compiler_params=_cparams(dimension_semantics=("arbitrary",)),
        name=name,
    )(fox, dil, memo, p16, p16, p16, wo, x, tgt, gfin)


def _dh_rms_bwd(dp, w, x, g, resid, *, tm, name):
    t, d = x.shape
    kdim = dp.shape[1]

    def body(*refs):
        if resid is not None:
            dp_ref, w_ref, x_ref, g_ref, r_ref, dx_ref, gg_ref = refs
        else:
            dp_ref, w_ref, x_ref, g_ref, dx_ref, gg_ref = refs

        @pl.when(pl.program_id(0) == 0)
        def _():
            gg_ref[...] = jnp.zeros_like(gg_ref)

        dh = lax.dot_general(dp_ref[...], w_ref[...], _NT, preferred_element_type=F32)
        xv = x_ref[...]
        r = lax.rsqrt(jnp.mean(xv * xv, axis=-1, keepdims=True) + RMS_EPS)
        nrm = xv * r
        dn = dh * g_ref[...]
        dx = r * (dn - nrm * jnp.mean(dn * nrm, axis=-1, keepdims=True))
        if resid is not None:
            dx = dx + r_ref[...]
        dx_ref[...] = dx
        gg_ref[0:1, :] += jnp.sum(dh * nrm, axis=0, keepdims=True)

    row = pl.BlockSpec((tm, d), lambda i: (i, 0))
    in_specs = [pl.BlockSpec((tm, kdim), lambda i: (i, 0)),
                pl.BlockSpec((d, kdim), lambda i: (0, 0), pipeline_mode=pl.Buffered(1)), row,
                pl.BlockSpec((1, d), lambda i: (0, 0))]
    args = [dp, w, x, g]
    if resid is not None:
        in_specs.append(row)
        args.append(resid)
    return pl.pallas_call(
        body,
        out_shape=(jax.ShapeDtypeStruct((t, d), F32), jax.ShapeDtypeStruct((8, d), F32)),
        grid=(t // tm,),
        in_specs=in_specs,
        out_specs=(row, pl.BlockSpec((8, d), lambda i: (0, 0))),
        compiler_params=_cparams(dimension_semantics=("arbitrary",)),
        name=name,
    )(*args)


_FLOG0 = 4 * FOX_W
_W_IN_SEGMENTS = ((0, _FLOG0, 0), (_FLOG0, _FLOG0 + FOX_HEADS, PW), (_FLOG0 + FOX_HEADS, IN_W, C_DQ))
SHARD_W = IN_W // N_CHIPS


def _rearrange_w_in(shards):
    def cols(lo, hi):
        parts = []
        for k in range(N_CHIPS):
            a, b = max(lo, k * SHARD_W), min(hi, (k + 1) * SHARD_W)
            if a < b:
                parts.append(shards[k][:, a - k * SHARD_W:b - k * SHARD_W])
        return parts

    (a0, a1, _), (f0, f1, _), (b0, b1, _) = _W_IN_SEGMENTS
    pad = jnp.zeros((shards[0].shape[0], PWF - PW - FOX_HEADS), shards[0].dtype)
    return jnp.concatenate(cols(a0, a1) + cols(b0, b1) + cols(f0, f1) + [pad], axis=1)


def _w_in_grad_slabs(g):
    slabs = []
    for k in range(N_CHIPS):
        parts = []
        for lo, hi, at in _W_IN_SEGMENTS:
            a, b = max(lo, k * SHARD_W), min(hi, (k + 1) * SHARD_W)
            if a < b:
                parts.append(g[:, at + a - lo:at + b - lo])
        slabs.append(jnp.concatenate(parts, axis=1))
    return jnp.stack(slabs, axis=0)


def _local_grads(x, mem, norm_g, w_r, b_forget, mem_norm_g, w_kv, w_o, final_norm_g, tgt, start_reduce=None,
                 start_reduce_small=None, early_token=None, late_weights=None):
    nb = x.shape[0]
    t = nb * SEQ
    x2d = x.reshape(t, D_MODEL)
    tgt2d = tgt.reshape(t, D_MODEL)
    tabs = _rope_tables()
    bpad = jnp.pad(b_forget.reshape(1, FOX_HEADS), ((0, 0), (0, LANES - FOX_HEADS)))

    gain0 = norm_g.reshape(1, D_MODEL)
    if early_token is not None:
        gain0 = gain0 + early_token[0:1, 0:1]
    h, p16, dqkv, flog = _proj(x2d, gain0, w_r, tabs, n=PWF, tm=1024, tn=768, name="proj")
    c12 = _flog_fwd(flog, bpad, nb=nb, ts=256, name="flog_fwd")

    crow = c12[:, :FOX_HEADS].reshape(nb, NBLK, BLK, FOX_HEADS // 2, 2).transpose(0, 3, 1, 4, 2)
    crow = jnp.pad(crow, ((0, 0), (0, 0), (0, 0), (0, 6), (0, 0)))
    p3 = p16.reshape(nb, SEQ, PWF)
    fox, fox_lse = _fox_fwd(p3, crow, name="fox_fwd")
    if late_weights is not None:
        w_kv, w_o = late_weights(fox_lse)

    dqkv3 = dqkv.reshape(nb, SEQ, 3 * DIL_W)
    dil, dil_lse = _dil_fwd(dqkv3, name="dil_fwd")

    mh = _rms_fwd(mem.reshape(nb * MEM_LEN, D_MODEL), mem_norm_g.reshape(1, D_MODEL), tm=nb * MEM_LEN, name="rms_mem")
    mkv = _matmul(mh, w_kv, out_dtype=BF16, tm=nb * MEM_LEN, tn=512, tk=D_MODEL, name="mem_kv")
    mkv3 = mkv.reshape(nb, MEM_LEN, 2 * MEM_W)
    memo, mem_lse = _mem_fwd(p3, mkv3, qoff=C_MQ // LANES, name="mem_fwd")

    fox2, dil2, memo2 = fox.reshape(t, FOX_W), dil.reshape(t, DIL_W), memo.reshape(t, MEM_W)
    y, dx2, dx2b, st = _out_loss(fox2, dil2, memo2, p16, w_o, x2d, tgt2d, final_norm_g.reshape(1, D_MODEL), tm=256,
                                 name="out_loss")

    g_wo = _matmul(y, dx2b, mode="tn", out_dtype=BF16, tm=1024, tn=512, tk=t, name="grad_w_out")
    datt, dgate = _dy_gate_bwd(dx2b, w_o, fox2, dil2, memo2, p16, tm=2048, tn=256, name="dy_gate_bwd")
    datt3 = datt.reshape(nb, SEQ, MIX_W)

    dmq, dmk, dmv = _mem_bwd(p3, mkv3, datt3, memo, mem_lse, qoff=C_MQ // LANES, do_off=_B2, name="mem_bwd")
    dmkv = jnp.concatenate([dmk, dmv], axis=-1).reshape(nb * MEM_LEN, 2 * MEM_W).astype(BF16)
    g_wkv = _matmul(mh, dmkv, mode="tn", out_dtype=BF16, tm=512, tn=512, tk=nb * MEM_LEN, name="grad_w_kv")
    mem_gain = mem_norm_g.reshape(1, D_MODEL)
    if start_reduce_small is not None:
        tok = start_reduce_small(g_wkv, g_wo)[0:1, 0:1]
        mem_gain, crow = mem_gain + tok, crow + tok
    _, gmn = _dh_rms_bwd(dmkv, w_kv, mem.reshape(nb * MEM_LEN, D_MODEL), mem_gain, None, tm=nb * MEM_LEN, name="mem_rms_bwd")

    dfq, dfk, dfv, dcr = _fox_bwd(p3, crow, datt3, fox, fox_lse, do_off=0, name="fox_bwd")
    dcol = -dcr[:, :, :, :2, :].transpose(0, 2, 4, 1, 3).reshape(t, FOX_HEADS)
    dcol = jnp.pad(dcol, ((0, 0), (0, LANES - FOX_HEADS)))
    dflog, gb = _flog_bwd(dcol, flog, bpad, nb=nb, ts=256, name="flog_bwd")

    ddq, ddk, ddv = _dil_bwd(dqkv3, datt3, dil, dil_lse, tabs, do_off=_B1, name="dil_bwd")

    flat = lambda a: a.reshape(t, -1)
    dp = jnp.concatenate([flat(dfq), flat(dfk), flat(dfv), dgate[:, :FOX_W], flat(ddq), flat(ddk), flat(ddv),
                          dgate[:, FOX_W:FOX_W + DIL_W], flat(dmq), dgate[:, FOX_W + DIL_W:], dflog,
                          jnp.zeros((t, PWF - PW - LANES), BF16)], axis=1)
    g_wr = _matmul(h, dp, mode="tn", out_dtype=BF16, tm=D_MODEL, tn=768, tk=t, name="grad_w_in")
    gain = norm_g.reshape(1, D_MODEL)
    if start_reduce is not None:
        gain = gain + start_reduce(g_wr)[0:1, 0:1]
    gx, gng = _dh_rms_bwd(dp, w_r, x2d, gain, dx2, tm=256, name="in_rms_bwd")

    gb_row = jnp.pad(gb[0:1, :], ((0, 0), (0, D_MODEL - LANES)))
    small = jnp.concatenate([gng[0:1], gmn[0:1], st[0:1], gb_row, st[1:2], jnp.zeros((3, D_MODEL), F32)], axis=0)
    return gx.reshape(nb, SEQ, D_MODEL), g_wr, g_wkv, g_wo, small


MESH = pl.DeviceIdType.MESH
ANY = pl.BlockSpec(memory_space=pl.ANY)


def _place():
    x, y, c = lax.axis_index("x"), lax.axis_index("y"), lax.axis_index("c")
    other_chips = [(1 - x, y), (x, 1 - y), (1 - x, 1 - y)]
    return x, y, c, other_chips


def _gather_weights(shards):
    n = len(shards)

    def body(*refs):
        in_refs, out_refs = refs[:n], refs[n:2 * n]
        send_sems, recv_sems = refs[2 * n:]
        x, y, c, chips = _place()
        me_chip = 2 * x + y
        sibling = (x, y, 1 - c)

        def half(ref, pc, rows):
            return ref.at[pl.ds(pc * (rows // 2), rows // 2), :]

        def rcopy(k, src, dst, to):
            return pltpu.make_async_remote_copy(src_ref=src, dst_ref=dst, send_sem=send_sems.at[k], recv_sem=recv_sems.at[k],
                                                device_id=to, device_id_type=MESH)

        sends = []
        for t in range(n):
            rows = shards[t].shape[0]
            for j, chip in enumerate(chips):
                cp = rcopy(6 * t + j, half(in_refs[t], c, rows), half(out_refs[t].at[me_chip], c, rows), (*chip, c))
                cp.start()
                sends.append(cp)
        for t in range(n):
            rows = shards[t].shape[0]
            for j, chip in enumerate(chips):
                slot = out_refs[t].at[2 * chip[0] + chip[1]]
                rcopy(6 * t + j, half(slot, c, rows), half(slot, c, rows), sibling).wait_recv()
                fw = rcopy(6 * t + 3 + j, half(slot, c, rows), half(slot, c, rows), sibling)
                fw.start()
                sends.append(fw)
        for t in range(n):
            rows = shards[t].shape[0]
            for j, chip in enumerate(chips):
                slot = out_refs[t].at[2 * chip[0] + chip[1]]
                rcopy(6 * t + 3 + j, half(slot, 1 - c, rows), half(slot, 1 - c, rows), sibling).wait_recv()
        for cp in sends:
            cp.wait_send()

    return pl.pallas_call(
        body,
        out_shape=tuple(jax.ShapeDtypeStruct((N_CHIPS,) + s.shape, s.dtype) for s in shards),
        in_specs=[ANY] * n,
        out_specs=tuple([ANY] * n),
        scratch_shapes=[pltpu.SemaphoreType.DMA((6 * n,)), pltpu.SemaphoreType.DMA((6 * n,))],
        name="gather_weights",
    )(*shards)


def _pair_exchange(gs, *, name):
    n = len(gs)

    def body(*refs):
        g_refs, r_refs = refs[:n], refs[n:2 * n]
        send_sems, recv_sems = refs[2 * n:]
        x, y, c, _ = _place()
        cps = []
        for t in range(n):
            hr = gs[t].shape[1] // 2
            cp = pltpu.make_async_remote_copy(src_ref=g_refs[t].at[:, pl.ds((1 - c) * hr, hr), :], dst_ref=r_refs[t],
                                              send_sem=send_sems.at[t], recv_sem=recv_sems.at[t],
                                              device_id=(x, y, 1 - c), device_id_type=MESH)
            cp.start()
            cps.append(cp)
        for cp in cps:
            cp.wait()

    return pl.pallas_call(
        body,
        out_shape=tuple(jax.ShapeDtypeStruct((g.shape[0], g.shape[1] // 2, g.shape[2]), g.dtype) for g in gs),
        in_specs=[ANY] * n,
        out_specs=tuple([ANY] * n),
        scratch_shapes=[pltpu.SemaphoreType.DMA((n,)), pltpu.SemaphoreType.DMA((n,))],
        name=name,
    )(*gs)


_HBM = pl.BlockSpec(memory_space=pltpu.HBM)
_SEM = pl.BlockSpec(memory_space=pltpu.SEMAPHORE)
_DATAFLOW = pltpu.SideEffectType.DATAFLOW_SIDE_EFFECTING


def _chip_copies(p_refs, land_refs, send_sems, recv_sems):
    x, y, c, chips = _place()
    me_chip = 2 * x + y
    return [pltpu.make_async_remote_copy(src_ref=p_refs[t].at[2 * chip[0] + chip[1]], dst_ref=land_refs[t].at[me_chip],
                                         send_sem=send_sems.at[3 * t + j], recv_sem=recv_sems.at[3 * t + j],
                                         device_id=(*chip, c), device_id_type=MESH)
            for t in range(len(p_refs)) for j, chip in enumerate(chips)]


def _chip_exchange_start(ps, *, tag):
    n = len(ps)

    def body(*refs):
        p_refs, land_refs = refs[:n], refs[n:2 * n]
        send_sems, recv_sems = refs[2 * n:2 * n + 2]
        token = refs[-1]
        for cp in _chip_copies(p_refs, land_refs, send_sems, recv_sems):
            cp.start()
        token[...] = jnp.zeros_like(token)

    hbm = [pltpu.HBM(p.shape, p.dtype) for p in ps]
    args = [pltpu.with_memory_space_constraint(p, pltpu.HBM) for p in ps]
    args += [pltpu.with_memory_space_constraint(lax.empty(p.shape, p.dtype), pltpu.HBM) for p in ps]
    out = pl.pallas_call(
        body,
        name=f"chip_exchange_start_{tag}",
        out_shape=(pltpu.SemaphoreType.DMA((3 * n,)), pltpu.SemaphoreType.DMA((3 * n,)), *hbm, *hbm,
                   jax.ShapeDtypeStruct((8, LANES), F32)),
        in_specs=[_HBM] * (2 * n),
        out_specs=(_SEM, _SEM, *([_HBM] * (2 * n)), pl.BlockSpec(memory_space=pltpu.VMEM)),
        input_output_aliases={i: 2 + i for i in range(2 * n)},
        compiler_params=pltpu.CompilerParams(has_side_effects=_DATAFLOW),
    )(*args)
    return out[0], out[1], out[2:2 + n], out[2 + n:2 + 2 * n], out[-1]


def _chip_exchange_wait(send_sems, recv_sems, p_thru, land_thru, after, *, tag):
    n = len(p_thru)

    def body(*refs):
        p_refs, land_refs = refs[:n], refs[n:2 * n]
        ssem, rsem = refs[2 * n:2 * n + 2]
        for cp in _chip_copies(p_refs, land_refs, ssem, rsem):
            cp.wait_send()
            cp.wait_recv()

    hbm = [pltpu.HBM(p.shape, p.dtype) for p in p_thru]
    out = pl.pallas_call(
        body,
        name=f"chip_exchange_wait_{tag}",
        out_shape=(*hbm, *hbm),
        in_specs=[_HBM] * (2 * n) + [_SEM, _SEM, ANY],
        out_specs=tuple([_HBM] * (2 * n)),
        input_output_aliases={i: i for i in range(2 * n)},
        compiler_params=pltpu.CompilerParams(has_side_effects=_DATAFLOW),
    )(*p_thru, *land_thru, send_sems, recv_sems, after)
    return out[:n], out[n:]


def _shard_copies(s_refs, land_refs, send_sems, recv_sems):
    x, y, c, chips = _place()
    me_chip = 2 * x + y
    return [pltpu.make_async_remote_copy(src_ref=s_refs[t], dst_ref=land_refs[t].at[me_chip],
                                         send_sem=send_sems.at[3 * t + j], recv_sem=recv_sems.at[3 * t + j],
                                         device_id=(*chip, c), device_id_type=MESH)
            for t in range(len(s_refs)) for j, chip in enumerate(chips)]


def _gather_late_start(shards):
    n = len(shards)

    def body(*refs):
        s_refs, land_refs = refs[:n], refs[n:2 * n]
        send_sems, recv_sems = refs[2 * n:2 * n + 2]
        token = refs[-1]
        for cp in _shard_copies(s_refs, land_refs, send_sems, recv_sems):
            cp.start()
        token[...] = jnp.zeros_like(token)

    lands = [(N_CHIPS,) + s.shape for s in shards]
    args = [pltpu.with_memory_space_constraint(s, pltpu.HBM) for s in shards]
    args += [pltpu.with_memory_space_constraint(lax.empty(shp, s.dtype), pltpu.HBM) for shp, s in zip(lands, shards)]
    out = pl.pallas_call(
        body,
        name="gather_late_start",
        out_shape=(pltpu.SemaphoreType.DMA((3 * n,)), pltpu.SemaphoreType.DMA((3 * n,)),
                   *[pltpu.HBM(s.shape, s.dtype) for s in shards], *[pltpu.HBM(shp, s.dtype) for shp, s in zip(lands, shards)],
                   jax.ShapeDtypeStruct((8, LANES), F32)),
        in_specs=[_HBM] * (2 * n),
        out_specs=(_SEM, _SEM, *([_HBM] * (2 * n)), pl.BlockSpec(memory_space=pltpu.VMEM)),
        input_output_aliases={i: 2 + i for i in range(2 * n)},
        compiler_params=pltpu.CompilerParams(has_side_effects=_DATAFLOW),
    )(*args)
    return out[0], out[1], out[2:2 + n], out[2 + n:2 + 2 * n], out[-1]


def _gather_late_wait(send_sems, recv_sems, s_thru, land_thru, after):
    n = len(s_thru)

    def body(*refs):
        s_refs, land_refs = refs[:n], refs[n:2 * n]
        ssem, rsem = refs[2 * n:2 * n + 2]
        for cp in _shard_copies(s_refs, land_refs, ssem, rsem):
            cp.wait_send()
            cp.wait_recv()

    out = pl.pallas_call(
        body,
        name="gather_late_wait",
        out_shape=(*[pltpu.HBM(s.shape, s.dtype) for s in s_thru], *[pltpu.HBM(l.shape, l.dtype) for l in land_thru]),
        in_specs=[_HBM] * (2 * n) + [_SEM, _SEM, ANY],
        out_specs=tuple([_HBM] * (2 * n)),
        input_output_aliases={i: i for i in range(2 * n)},
        compiler_params=pltpu.CompilerParams(has_side_effects=_DATAFLOW),
    )(*s_thru, *land_thru, send_sems, recv_sems, after)
    return out[:n], out[n:]


def _pair_swap(rs):
    n = len(rs)

    def body(*refs):
        r_refs, o_refs = refs[:n], refs[n:2 * n]
        send_sems, recv_sems = refs[2 * n:]
        x, y, c, _ = _place()
        cps = []
        for t in range(n):
            cp = pltpu.make_async_remote_copy(src_ref=r_refs[t], dst_ref=o_refs[t], send_sem=send_sems.at[t],
                                              recv_sem=recv_sems.at[t], device_id=(x, y, 1 - c), device_id_type=MESH)
            cp.start()
            cps.append(cp)
        for cp in cps:
            cp.wait()

    return pl.pallas_call(
        body,
        out_shape=tuple(jax.ShapeDtypeStruct(r.shape, r.dtype) for r in rs),
        in_specs=[ANY] * n,
        out_specs=tuple([ANY] * n),
        scratch_shapes=[pltpu.SemaphoreType.DMA((n,)), pltpu.SemaphoreType.DMA((n,))],
        name="pair_swap",
    )(*rs)


N_DEV = 8
LOSS_ROW = 4


def _small_allreduce(small):
    def body(s_ref, o_ref, all_ref, send_sems, recv_sems):
        x, y, c, _ = _place()
        me = 4 * x + 2 * y + c
        all_ref[me] = s_ref[...]
        cps = []
        for k in range(1, N_DEV):
            peer = tuple(1 - p if (k >> s) & 1 else p for p, s in ((x, 2), (y, 1), (c, 0)))
            cp = pltpu.make_async_remote_copy(src_ref=s_ref, dst_ref=all_ref.at[me], send_sem=send_sems.at[k - 1],
                                              recv_sem=recv_sems.at[k - 1], device_id=peer, device_id_type=MESH)
            cp.start()
            cps.append(cp)
        for cp in cps:
            cp.wait()
        tot = all_ref[0]
        for d in range(1, N_DEV):
            tot = tot + all_ref[d]
        o_ref[...] = tot
        o_ref[LOSS_ROW:LOSS_ROW + 1, :] = jnp.broadcast_to(jnp.sum(tot[LOSS_ROW:LOSS_ROW + 1, :], axis=1, keepdims=True),
                                                          (1, tot.shape[1]))

    vm = pl.BlockSpec(memory_space=pltpu.VMEM)
    return pl.pallas_call(
        body,
        out_shape=jax.ShapeDtypeStruct(small.shape, small.dtype),
        in_specs=[vm],
        out_specs=vm,
        scratch_shapes=[pltpu.VMEM((N_DEV,) + small.shape, small.dtype), pltpu.SemaphoreType.DMA((N_DEV - 1,)),
                        pltpu.SemaphoreType.DMA((N_DEV - 1,))],
        name="small_allreduce",
    )(small)


def _sum_pair(g, recv, cidx, *, tr, name):
    n, hr, cols = recv.shape
    nr = hr // tr

    def body(c_ref, g_ref, r_ref, o_ref):
        o_ref[...] = (g_ref[...].astype(F32) + r_ref[...].astype(F32)).astype(o_ref.dtype)

    grid_spec = pltpu.PrefetchScalarGridSpec(
        num_scalar_prefetch=1,
        grid=(n, nr),
        in_specs=[pl.BlockSpec((None, tr, cols), lambda k, i, c_ref: (k, c_ref[0] * nr + i, 0)),
                  pl.BlockSpec((None, tr, cols), lambda k, i, c_ref: (k, i, 0))],
        out_specs=pl.BlockSpec((None, tr, cols), lambda k, i, c_ref: (k, i, 0)),
    )
    return pl.pallas_call(body, out_shape=jax.ShapeDtypeStruct(recv.shape, BF16), grid_spec=grid_spec,
                          compiler_params=_cparams(), name=name)(cidx, g, recv)


def _sum_chips(p, *, tr, name):
    _, rows, cols = p.shape

    def body(p_ref, o_ref):
        tot = p_ref[0].astype(F32)
        for k in range(1, N_CHIPS):
            tot = tot + p_ref[k].astype(F32)
        o_ref[...] = tot

    return pl.pallas_call(
        body,
        out_shape=jax.ShapeDtypeStruct((rows, cols), F32),
        grid=(rows // tr,),
        in_specs=[pl.BlockSpec((N_CHIPS, tr, cols), lambda i: (0, i, 0))],
        out_specs=pl.BlockSpec((tr, cols), lambda i: (i, 0)),
        compiler_params=_cparams(),
        name=name,
    )(p)


def _adamw(w, g, m, v, *, tr, name):
    rows, cols = w.shape
    bc1 = 1.0 / (1.0 - ADAM_B1 ** ADAM_STEP)
    bc2 = 1.0 / (1.0 - ADAM_B2 ** ADAM_STEP)

    def body(w_ref, g_ref, m_ref, v_ref, d_ref, nm_ref, nv_ref):
        gv = g_ref[...]
        nm = ADAM_B1 * m_ref[...] + (1.0 - ADAM_B1) * gv
        nv = ADAM_B2 * v_ref[...] + (1.0 - ADAM_B2) * (gv * gv)
        d_ref[...] = -ADAM_LR * ((nm * bc1) / (jnp.sqrt(nv * bc2) + ADAM_EPS) + ADAM_WD * w_ref[...])
        nm_ref[...] = nm
        nv_ref[...] = nv

    spec = pl.BlockSpec((tr, cols), lambda i: (i, 0))
    sd = jax.ShapeDtypeStruct((rows, cols), F32)
    return pl.pallas_call(body, out_shape=(sd, sd, sd), grid=(rows // tr,), in_specs=[spec] * 4, out_specs=(spec,) * 3,
                          compiler_params=_cparams(), name=name)(w, g, m, v)


def _adamw_halves(w, own, sib, cidx, m, v, *, tr, name):
    rows, cols = w.shape
    hr = own.shape[0]
    nr = hr // tr
    assert rows == 2 * hr and hr % tr == 0
    bc1 = 1.0 / (1.0 - ADAM_B1 ** ADAM_STEP)
    bc2 = 1.0 / (1.0 - ADAM_B2 ** ADAM_STEP)

    def body(c_ref, w_ref, o_ref, s_ref, m_ref, v_ref, g_ref, d_ref, nm_ref, nv_ref):
        mine = (pl.program_id(0) // nr) == c_ref[0]
        gv = jnp.where(mine, o_ref[...], s_ref[...])
        nm = ADAM_B1 * m_ref[...] + (1.0 - ADAM_B1) * gv
        nv = ADAM_B2 * v_ref[...] + (1.0 - ADAM_B2) * (gv * gv)
        g_ref[...] = gv
        d_ref[...] = -ADAM_LR * ((nm * bc1) / (jnp.sqrt(nv * bc2) + ADAM_EPS) + ADAM_WD * w_ref[...])
        nm_ref[...] = nm
        nv_ref[...] = nv

    full = pl.BlockSpec((tr, cols), lambda i, c_ref: (i, 0))
    half = pl.BlockSpec((tr, cols), lambda i, c_ref: (i % nr, 0))
    sd = jax.ShapeDtypeStruct((rows, cols), F32)
    grid_spec = pltpu.PrefetchScalarGridSpec(num_scalar_prefetch=1, grid=(rows // tr,), in_specs=[full, half, half, full, full],
                                             out_specs=(full,) * 4)
    return pl.pallas_call(body, out_shape=(sd,) * 4, grid_spec=grid_spec, compiler_params=_cparams(), name=name)(
        cidx, w, own, sib, m, v)


def _pack_small(norm, mem_norm, final_norm, b_forget):
    rows = [norm.reshape(1, D_MODEL), mem_norm.reshape(1, D_MODEL), final_norm.reshape(1, D_MODEL),
            jnp.pad(b_forget.reshape(1, FOX_HEADS), ((0, 0), (0, D_MODEL - FOX_HEADS))), jnp.zeros((4, D_MODEL), F32)]
    return jnp.concatenate(rows, axis=0)


def _unpack_small(a):
    return a[0:1], a[3:4, :FOX_HEADS], a[1:2], a[2]


def kernel(x, mem, norm_g, w_in, b_forget, mem_norm_g, w_mem_kv, w_out, final_norm_g, loss_target, m_norm_g, m_w_in, m_b_forget, m_mem_norm_g, m_w_mem_kv, m_w_out, m_final_norm_g, v_norm_g, v_w_in, v_b_forget, v_mem_norm_g, v_w_mem_kv, v_w_out, v_final_norm_g):
    core = lax.axis_index("c").astype(jnp.int32)
    me_chip = (2 * lax.axis_index("x") + lax.axis_index("y")).astype(jnp.int32)
    cidx = core.reshape(1)

    def own_slot(arr, own):
        return lax.dynamic_update_slice(arr, own[None].astype(arr.dtype), (me_chip,) + (0,) * own.ndim)

    win_b, late = w_in[0].astype(BF16), [w_mem_kv[0].astype(BF16), w_out[0].astype(BF16)]
    g_in, = _gather_weights([win_b])
    g_in, late = lax.optimization_barrier((own_slot(g_in, win_b), late))
    w_r = _rearrange_w_in([g_in[k] for k in range(N_CHIPS)])
    *late_flight, early_token = _gather_late_start(late)

    def late_weights(after):
        shards, landed = _gather_late_wait(*late_flight, after)
        g_kv, g_out = (own_slot(g, s) for g, s in zip(landed, shards))
        return g_kv.reshape(D_MODEL, 2 * MEM_W), g_out.reshape(MIX_W, D_MODEL)

    trs = (128, 128, 256)
    names = ("w_in", "w_mem_kv", "w_out")
    flights = {}

    def exchange(slabs, nms, ts, tag):
        recv = _pair_exchange(slabs, name=f"pair_exchange_{tag}")
        pair = [_sum_pair(g, r, cidx, tr=tr, name=f"sum_pair_{nm}") for g, r, tr, nm in zip(slabs, recv, ts, nms)]
        if tag == "w_in":
            pair[0] = _w_in_grad_slabs(pair[0][0])
        *flights[tag], token = _chip_exchange_start(pair, tag=tag)
        return token

    def start_reduce_small(g_wkv, g_wo):
        slabs = [g_wkv.reshape(N_CHIPS, D_MODEL // N_CHIPS, 2 * MEM_W), g_wo.reshape(N_CHIPS, MIX_W // N_CHIPS, D_MODEL)]
        return exchange(slabs, names[1:], trs[1:], "small")

    def start_reduce(g_wr):
        return exchange([g_wr[None]], names[:1], trs[:1], "w_in")

    gx, g_wr, g_wkv, g_wo, small = _local_grads(x, mem, norm_g, w_r, b_forget, mem_norm_g, None, None, final_norm_g, loss_target,
                                                start_reduce=start_reduce, start_reduce_small=start_reduce_small,
                                                early_token=early_token, late_weights=late_weights)

    pair, landed = [], []
    for tag in ("w_in", "small"):
        p, l = _chip_exchange_wait(*flights[tag], small, tag=tag)
        pair += list(p)
        landed += list(l)
    got = [lax.dynamic_update_slice(g, lax.dynamic_slice(p, (me_chip, 0, 0), (1,) + p.shape[1:]), (me_chip, 0, 0))
           for g, p in zip(landed, pair)]
    red = [_sum_chips(p, tr=tr, name=f"sum_chips_{nm}") for p, tr, nm in zip(got, trs, names)]
    sib = _pair_swap(red)

    outs = {}
    for nm, r, s, w, m, v, tr in zip(names, red, sib, (w_in, w_mem_kv, w_out), (m_w_in, m_w_mem_kv, m_w_out),
                                     (v_w_in, v_w_mem_kv, v_w_out), trs):
        outs[nm] = tuple(a[None] for a in _adamw_halves(w[0], r, s, cidx, m[0], v[0], tr=tr, name=f"adamw_{nm}"))

    gsum = _small_allreduce(small)
    sd, sm, sv = _adamw(_pack_small(norm_g, mem_norm_g, final_norm_g, b_forget), gsum,
                        _pack_small(m_norm_g, m_mem_norm_g, m_final_norm_g, m_b_forget),
                        _pack_small(v_norm_g, v_mem_norm_g, v_final_norm_g, v_b_forget), tr=8, name="adamw_small")
    loss = gsum[LOSS_ROW, 0]

    def group(i, small_arr):
        ng, bf, mg, fg = _unpack_small(small_arr)
        return (ng, outs["w_in"][i], bf, mg, outs["w_mem_kv"][i], outs["w_out"][i], fg)

    return (loss, gx, *group(0, gsum), *group(1, sd), *group(2, sm), *group(3, sv))
```

```python
import functools
import math

import jax
import jax.numpy as jnp
from jax import lax
from jax.experimental import pallas as pl
from jax.experimental.pallas import tpu as pltpu

F32 = jnp.float32
BF16 = jnp.bfloat16

D_MODEL = 1024
SEQ = 2048
HEAD_DIM = 64
FOX_HEADS = 12
DIL_HEADS = 12
MEM_HEADS = 4
MEM_HEAD_DIM = 128
MEM_LEN = 256
FOX_W = FOX_HEADS * HEAD_DIM
DIL_W = DIL_HEADS * HEAD_DIM
MEM_W = MEM_HEADS * MEM_HEAD_DIM
MIX_W = FOX_W + DIL_W + MEM_W
DILATIONS = ((128, 1), (512, 4), (2048, 16))
ROPE_THETA = 500000.0
ROPE_DIM = HEAD_DIM // 4
RMS_EPS = 1e-6
NEG_INF = -1e30
IN_SIZES = [FOX_W] * 4 + [FOX_HEADS] + [DIL_W] * 4 + [MEM_W] * 2
IN_W = sum(IN_SIZES)

ADAM_LR = 0.001
ADAM_B1 = 0.9
ADAM_B2 = 0.999
ADAM_EPS = 1e-08
ADAM_WD = 0.01
ADAM_STEP = 10

LANES = 128
N_CHIPS = 4
PW = 7168
PWF = PW + 4 * LANES
C_FQ, C_FK, C_FV, C_FG = 0, 768, 1536, 2304
C_DQ, C_DK, C_DV, C_DG = 3072, 3840, 4608, 5376
C_MQ, C_MG = 6144, 6656
VMEM_LIMIT = 48 * 1024 * 1024


def _cparams(**kw):
    return pltpu.CompilerParams(vmem_limit_bytes=VMEM_LIMIT, **kw)


MM_CHUNK = 256


def _matmul(a, b, *, out_dtype, tm, tn, tk, name, mode="nn"):
    if mode == "tn":
        (kdim, m), n = a.shape, b.shape[1]
        a_spec = pl.BlockSpec((tk, tm), lambda i, j, k: (k, i))
        b_spec = pl.BlockSpec((tk, tn), lambda i, j, k: (k, j))
        dims = _T0
    elif mode == "nt":
        (m, kdim), n = a.shape, b.shape[0]
        a_spec = pl.BlockSpec((tm, tk), lambda i, j, k: (i, k))
        b_spec = pl.BlockSpec((tn, tk), lambda i, j, k: (j, k))
        dims = _NT
    else:
        (m, kdim), n = a.shape, b.shape[1]
        a_spec = pl.BlockSpec((tm, tk), lambda i, j, k: (i, k))
        b_spec = pl.BlockSpec((tk, tn), lambda i, j, k: (k, j))
        dims = (((1,), (0,)), ((), ()))
    nk = kdim // tk
    assert m % tm == 0 and n % tn == 0 and kdim % tk == 0

    def body(a_ref, b_ref, o_ref, *scratch):
        if nk == 1:
            bv = b_ref[...]
            for c0 in range(0, tm, min(tm, MM_CHUNK)):
                rows = pl.ds(c0, min(tm, MM_CHUNK))
                av = a_ref[:, rows] if mode == "tn" else a_ref[rows, :]
                o_ref[rows, :] = lax.dot_general(av, bv, dims, preferred_element_type=F32).astype(o_ref.dtype)
            return
        prod = lax.dot_general(a_ref[...], b_ref[...], dims, preferred_element_type=F32)
        acc_ref, = scratch
        k = pl.program_id(2)

        @pl.when(k == 0)
        def _():
            acc_ref[...] = prod

        @pl.when(k > 0)
        def _():
            acc_ref[...] += prod

        @pl.when(k == nk - 1)
        def _():
            o_ref[...] = acc_ref[...].astype(o_ref.dtype)

    return pl.pallas_call(
        body,
        out_shape=jax.ShapeDtypeStruct((m, n), out_dtype),
        grid=(m // tm, n // tn, nk),
        in_specs=[a_spec, b_spec],
        out_specs=pl.BlockSpec((tm, tn), lambda i, j, k: (i, j)),
        scratch_shapes=[pltpu.VMEM((tm, tn), F32)] if nk > 1 else [],
        compiler_params=_cparams(dimension_semantics=("parallel", "parallel", "arbitrary")),
        name=name,
    )(a, b)


def _rms_fwd(x, g, *, tm, name):
    t, d = x.shape

    def body(x_ref, g_ref, h_ref):
        xv = x_ref[...]
        r = lax.rsqrt(jnp.mean(xv * xv, axis=-1, keepdims=True) + RMS_EPS)
        h_ref[...] = (xv * r * g_ref[...]).astype(h_ref.dtype)

    return pl.pallas_call(
        body,
        out_shape=jax.ShapeDtypeStruct((t, d), BF16),
        grid=(t // tm,),
        in_specs=[pl.BlockSpec((tm, d), lambda i: (i, 0)), pl.BlockSpec((1, d), lambda i: (0, 0))],
        out_specs=pl.BlockSpec((tm, d), lambda i: (i, 0)),
        compiler_params=_cparams(),
        name=name,
    )(x, g)


def _rope_tables():
    half = ROPE_DIM // 2
    pos = jnp.arange(SEQ, dtype=F32)
    inv_freq = 1.0 / (ROPE_THETA ** (jnp.arange(0, ROPE_DIM, 2, dtype=F32) / ROPE_DIM))
    ang = pos[:, None] * inv_freq[None, :]
    cos, sin = jnp.cos(ang), jnp.sin(ang)
    one = jnp.ones((SEQ, HEAD_DIM - ROPE_DIM), F32)
    zero = jnp.zeros((SEQ, HEAD_DIM - ROPE_DIM), F32)
    zh = jnp.zeros((SEQ, half), F32)
    c = jnp.concatenate([cos, cos, one], axis=1)
    s1 = jnp.concatenate([zh, sin, zero], axis=1)
    s2 = jnp.concatenate([-sin, zh, zero], axis=1)
    rep = LANES // HEAD_DIM
    return jnp.tile(c, (1, rep)), jnp.tile(s1, (1, rep)), jnp.tile(s2, (1, rep))


def _rope_apply(t, c, s1, s2, transpose=False):
    n = t.shape[-1]
    rep = n // LANES
    c, s1, s2 = (jnp.tile(u, (1, rep)) for u in (c, s1, s2))
    half = ROPE_DIM // 2
    if not transpose:
        return t * c + pltpu.roll(t, half, 1) * s1 + pltpu.roll(t, n - half, 1) * s2
    return t * c + pltpu.roll(t * s1, n - half, 1) + pltpu.roll(t * s2, half, 1)


PROJ_CHUNK = 256


def _proj(x, g, w, tabs, *, n, tm, tn, name):
    t, d = x.shape
    assert C_DQ % tn == 0 and (C_DV - C_DQ) % tn == 0 and (C_DG - C_DQ) % tn == 0
    rope_lo, rope_hi, dil_hi = C_DQ // tn, C_DV // tn, C_DG // tn
    flog_blk, flog_at = PW // tn, PW % tn
    assert flog_at % LANES == 0 and flog_at + LANES <= tn
    s_blocks = SEQ // tm

    def body(x_ref, g_ref, w_ref, c_ref, s1_ref, s2_ref, h_ref, o_ref, f_ref, fl_ref, h_scr):
        j = pl.program_id(1)

        @pl.when(j == 0)
        def _():
            xv = x_ref[...]
            r = lax.rsqrt(jnp.mean(xv * xv, axis=-1, keepdims=True) + RMS_EPS)
            hv = (xv * r * g_ref[...]).astype(BF16)
            h_scr[...] = hv
            h_ref[...] = hv

        def tile(kind):
            wv = w_ref[...]
            for c0 in range(0, tm, PROJ_CHUNK):
                rows = pl.ds(c0, PROJ_CHUNK)
                acc = jnp.dot(h_scr[rows, :], wv, preferred_element_type=F32)
                if kind == "rope":
                    acc = _rope_apply(acc, c_ref[rows, :], s1_ref[rows, :], s2_ref[rows, :])
                o_ref[rows, :] = acc.astype(o_ref.dtype)
                if kind in ("rope", "dv"):
                    f_ref[rows, :] = acc
                if kind == "flog":
                    fl_ref[rows, :] = acc[:, flog_at:flog_at + LANES]

        is_rope = jnp.logical_and(j >= rope_lo, j < rope_hi)
        is_dv = jnp.logical_and(j >= rope_hi, j < dil_hi)
        is_flog = j == flog_blk
        pl.when(is_rope)(functools.partial(tile, "rope"))
        pl.when(is_dv)(functools.partial(tile, "dv"))
        pl.when(is_flog)(functools.partial(tile, "flog"))
        pl.when(jnp.logical_not(jnp.logical_or(jnp.logical_or(is_rope, is_dv), is_flog)))(functools.partial(tile, "plain"))

    tab_spec = pl.BlockSpec((tm, LANES), lambda i, j: (i % s_blocks, 0))
    f_spec = pl.BlockSpec((tm, tn), lambda i, j: (i, jnp.clip(j - rope_lo, 0, dil_hi - rope_lo - 1)))
    row = pl.BlockSpec((tm, d), lambda i, j: (i, 0))
    return pl.pallas_call(
        body,
        out_shape=(jax.ShapeDtypeStruct((t, d), BF16), jax.ShapeDtypeStruct((t, n), BF16),
                   jax.ShapeDtypeStruct((t, 3 * DIL_W), F32), jax.ShapeDtypeStruct((t, LANES), F32)),
        grid=(t // tm, n // tn),
        in_specs=[row, pl.BlockSpec((1, d), lambda i, j: (0, 0)), pl.BlockSpec((d, tn), lambda i, j: (0, j)),
                  tab_spec, tab_spec, tab_spec],
        out_specs=(row, pl.BlockSpec((tm, tn), lambda i, j: (i, j)), f_spec, pl.BlockSpec((tm, LANES), lambda i, j: (i, 0))),
        scratch_shapes=[pltpu.VMEM((tm, d), BF16)],
        compiler_params=_cparams(dimension_semantics=("parallel", "arbitrary")),
        name=name,
    )(x, g, w, *tabs)


def _split3(x):
    hi = x.astype(BF16)
    r1 = x - hi.astype(F32)
    mid = r1.astype(BF16)
    lo = (r1 - mid.astype(F32)).astype(BF16)
    return hi, mid, lo


def _dot3(sel, x, sel_is_lhs):
    out = None
    for piece in _split3(x):
        t = jnp.dot(sel, piece, preferred_element_type=F32) if sel_is_lhs else jnp.dot(piece, sel, preferred_element_type=F32)
        out = t if out is None else out + t
    return out


def _flog_fwd(flog, bpad, *, nb, ts, name):
    ns = SEQ // ts

    def body(f_ref, b_ref, c_ref, carry_ref):
        s = pl.program_id(1)

        @pl.when(s == 0)
        def _():
            carry_ref[...] = jnp.zeros_like(carry_ref)

        z = f_ref[...] + b_ref[...]
        logf = jnp.minimum(z, 0.0) - jnp.log(1.0 + jnp.exp(-jnp.abs(z)))
        r = lax.broadcasted_iota(jnp.int32, (ts, ts), 0)
        c = lax.broadcasted_iota(jnp.int32, (ts, ts), 1)
        tri = jnp.where(r >= c, 1.0, 0.0).astype(BF16)
        cs = _dot3(tri, logf, True) + carry_ref[0:1, :]
        carry_ref[...] = jnp.broadcast_to(cs[ts - 1:ts, :], carry_ref.shape)
        c_ref[...] = cs

    return pl.pallas_call(
        body,
        out_shape=jax.ShapeDtypeStruct((nb * SEQ, LANES), F32),
        grid=(nb, ns),
        in_specs=[pl.BlockSpec((ts, LANES), lambda b, s: (b * ns + s, 0)), pl.BlockSpec((1, LANES), lambda b, s: (0, 0))],
        out_specs=pl.BlockSpec((ts, LANES), lambda b, s: (b * ns + s, 0)),
        scratch_shapes=[pltpu.VMEM((8, LANES), F32)],
        compiler_params=_cparams(dimension_semantics=("parallel", "arbitrary")),
        name=name,
    )(flog, bpad)


def _flog_bwd(dcol, flog, bpad, *, nb, ts, name):
    ns = SEQ // ts

    def body(d_ref, f_ref, b_ref, o_ref, gb_ref, carry_ref):
        bi = pl.program_id(0)
        s = pl.program_id(1)

        @pl.when(s == 0)
        def _():
            carry_ref[...] = jnp.zeros_like(carry_ref)

        @pl.when(jnp.logical_and(bi == 0, s == 0))
        def _():
            gb_ref[...] = jnp.zeros_like(gb_ref)

        r = lax.broadcasted_iota(jnp.int32, (ts, ts), 0)
        c = lax.broadcasted_iota(jnp.int32, (ts, ts), 1)
        tri = jnp.where(r <= c, 1.0, 0.0).astype(BF16)
        rc = _dot3(tri, d_ref[...], True) + carry_ref[0:1, :]
        carry_ref[...] = jnp.broadcast_to(rc[0:1, :], carry_ref.shape)
        z = f_ref[...] + b_ref[...]
        dz = rc / (1.0 + jnp.exp(z))
        o_ref[...] = dz.astype(o_ref.dtype)
        gb_ref[...] += jnp.broadcast_to(jnp.sum(dz, axis=0, keepdims=True), gb_ref.shape)

    rev = lambda b, s: (b * ns + (ns - 1 - s), 0)
    return pl.pallas_call(
        body,
        out_shape=(jax.ShapeDtypeStruct((nb * SEQ, LANES), BF16), jax.ShapeDtypeStruct((8, LANES), F32)),
        grid=(nb, ns),
        in_specs=[pl.BlockSpec((ts, LANES), rev), pl.BlockSpec((ts, LANES), rev), pl.BlockSpec((1, LANES), lambda b, s: (0, 0))],
        out_specs=(pl.BlockSpec((ts, LANES), rev), pl.BlockSpec((8, LANES), lambda b, s: (0, 0))),
        scratch_shapes=[pltpu.VMEM((8, LANES), F32)],
        compiler_params=_cparams(dimension_semantics=("arbitrary", "arbitrary")),
        name=name,
    )(dcol, flog, bpad)


MEM_TQ = 256
MEM_SET = 4
MEM_SCALE = 1.0 / math.sqrt(MEM_HEAD_DIM)
assert MEM_HEAD_DIM == LANES and SEQ % (MEM_TQ * MEM_SET) == 0


def _head_masks(nh):
    lane = lax.broadcasted_iota(jnp.int32, (1, LANES), 1)
    return [None] if nh == 1 else [lane < HEAD_DIM, lane >= HEAD_DIM]


def _mem_specs(qoff):
    qspec = pl.BlockSpec((None, SEQ, LANES), lambda b, j: (b, 0, qoff + j))
    kspec = pl.BlockSpec((None, MEM_LEN, LANES), lambda b, j: (b, 0, j))
    vspec = pl.BlockSpec((None, MEM_LEN, LANES), lambda b, j: (b, 0, MEM_HEADS + j))
    ospec = pl.BlockSpec((None, SEQ, LANES), lambda b, j: (b, 0, j))
    return qspec, kspec, vspec, ospec


def _mem_rows(g):
    return [pl.ds(pl.multiple_of((MEM_SET * g + a) * MEM_TQ, MEM_TQ), MEM_TQ) for a in range(MEM_SET)]


def _mem_fwd(p3, mkv3, *, qoff, name):
    nb = p3.shape[0]

    def body(q_ref, k_ref, v_ref, o_ref, lse_ref):
        kb, vb = k_ref[...], v_ref[...]

        def qset(g, c):
            rows = _mem_rows(g)
            ss = [lax.dot_general(q_ref[r, :] * MEM_SCALE, kb, _NT, preferred_element_type=F32) for r in rows]
            for r, s in zip(rows, ss):
                m = jnp.max(s, axis=1, keepdims=True)
                p = jnp.exp(s - m)
                l = jnp.sum(p, axis=1, keepdims=True)
                o_ref[r, :] = jnp.dot(p.astype(BF16), vb, preferred_element_type=F32) / l
                lse_ref[r, :] = jnp.broadcast_to(m + jnp.log(l), (MEM_TQ, LANES))
            return c

        lax.fori_loop(0, SEQ // MEM_TQ // MEM_SET, qset, 0)

    qspec, kspec, vspec, ospec = _mem_specs(qoff)
    osd = jax.ShapeDtypeStruct((nb, SEQ, MEM_W), F32)
    return pl.pallas_call(body, out_shape=(osd, osd), grid=(nb, MEM_HEADS), in_specs=[qspec, kspec, vspec],
                          out_specs=(ospec, ospec), compiler_params=_cparams(dimension_semantics=("parallel", "parallel")),
                          name=name)(p3, mkv3, mkv3)


def _mem_bwd(p3, mkv3, do, o, lse, *, qoff, do_off, name):
    nb = p3.shape[0]

    def body(q_ref, k_ref, v_ref, do_ref, o_ref, lse_ref, dq_ref, dk_ref, dv_ref):
        kb, vb = k_ref[...], v_ref[...]
        ks = kb * MEM_SCALE

        def qset(g, carry):
            dk, dv = carry
            work = []
            for r in _mem_rows(g):
                qs = q_ref[r, :] * MEM_SCALE
                dob = do_ref[r, :].astype(BF16)
                s = lax.dot_general(qs, kb, _NT, preferred_element_type=F32)
                dp = lax.dot_general(dob, vb, _NT, preferred_element_type=F32)
                work.append((r, qs, dob, s, dp))
            for r, qs, dob, s, dp in work:
                delta = jnp.sum(dob.astype(F32) * o_ref[r, :], axis=1, keepdims=True)
                p = jnp.exp(s - lse_ref[r, :][:, 0:1])
                ds = (p * (dp - delta)).astype(BF16)
                dq_ref[r, :] = jnp.dot(ds, ks, preferred_element_type=F32).astype(dq_ref.dtype)
                dk = dk + lax.dot_general(ds, qs, _T0, preferred_element_type=F32)
                dv = dv + lax.dot_general(p.astype(BF16), dob, _T0, preferred_element_type=F32)
            return dk, dv

        z = jnp.zeros((MEM_LEN, LANES), F32)
        dk, dv = lax.fori_loop(0, SEQ // MEM_TQ // MEM_SET, qset, (z, z))
        dk_ref[...] = dk
        dv_ref[...] = dv

    qspec, kspec, vspec, ospec = _mem_specs(qoff)
    dospec = pl.BlockSpec((None, SEQ, LANES), lambda b, j: (b, 0, do_off + j))
    kvo = pl.BlockSpec((None, MEM_LEN, LANES), lambda b, j: (b, 0, j))
    kvsd = jax.ShapeDtypeStruct((nb, MEM_LEN, MEM_W), F32)
    return pl.pallas_call(
        body, out_shape=(jax.ShapeDtypeStruct((nb, SEQ, MEM_W), BF16), kvsd, kvsd), grid=(nb, MEM_HEADS),
        in_specs=[qspec, kspec, vspec, dospec, ospec, ospec], out_specs=(ospec, kvo, kvo),
        compiler_params=_cparams(dimension_semantics=("parallel", "parallel")), name=name)(p3, mkv3, mkv3, do, o, lse)


BLK = 128
NBLK = SEQ // BLK
QK_SCALE = 1.0 / math.sqrt(HEAD_DIM)
DIL_STEPS = tuple(d for _, d in DILATIONS)
assert all(w // d == BLK for w, d in DILATIONS)
_T0 = (((0,), (0,)), ((), ()))
_NT = (((1,), (1,)), ((), ()))


def _stack_heads(a, masks):
    z = jnp.zeros_like(a)
    return jnp.concatenate([jnp.where(masks[0], a, z), jnp.where(masks[1], a, z)], axis=0)


def _tri_bias(lower):
    r = lax.broadcasted_iota(jnp.int32, (BLK, BLK), 0)
    c = lax.broadcasted_iota(jnp.int32, (BLK, BLK), 1)
    return jnp.where((c <= r) if lower else (c >= r), 0.0, NEG_INF).astype(F32)


def _dil_rows(r, i, d):
    start = r + i * (BLK * d)
    return pl.ds(start, BLK) if d == 1 else pl.ds(start, BLK, stride=d)


DIL_SET = 4


def _dil_sets(d, fn):
    nbk = SEQ // d // BLK
    if d == 1:
        n = 2 * DIL_SET
        def gbody(g, c):
            fn([(0, n * g + a, None if a == 0 else True) for a in range(n)])
            return c
        lax.fori_loop(0, nbk // n, gbody, 0)
    elif nbk > 1:
        assert nbk == DIL_SET
        def rbody(r, c):
            fn([(r, i, i > 0) for i in range(nbk)])
            return c
        lax.fori_loop(0, d, rbody, 0)
    else:
        def rbody(rr, c):
            fn([(DIL_SET * rr + a, 0, False) for a in range(DIL_SET)])
            return c
        lax.fori_loop(0, d // DIL_SET, rbody, 0)


def _dil_key_tiles(r, i, d, has_prev, qrows, tri_cur, tri_prev):
    tiles = [(qrows, tri_cur)]
    if has_prev is None:
        tiles.append((_dil_rows(r, jnp.maximum(i - 1, 0), d), tri_prev + jnp.where(i > 0, 0.0, NEG_INF)))
    elif has_prev:
        tiles.append((_dil_rows(r, i - 1, d), tri_prev))
    return tiles


def _dil_fwd(qkv, *, name):
    nb = qkv.shape[0]
    ncol = DIL_W // LANES
    hd = HEAD_DIM

    def body(q_ref, k_ref, v_ref, o_ref, lse_ref, m_ref, l_ref, a_ref):
        masks = _head_masks(2)
        tri_cur, tri_prev = _tri_bias(True), _tri_bias(False)
        for pi, d in enumerate(DIL_STEPS):
            first, last = pi == 0, pi == len(DIL_STEPS) - 1

            def qset(blocks, d=d, first=first, last=last):
                work = []
                for r, i, has_prev in blocks:
                    qrows = _dil_rows(r, i, d)
                    qcat = _stack_heads((q_ref[qrows, :] * QK_SCALE).astype(BF16), masks)
                    ss, krs = [], []
                    for krows, bias in _dil_key_tiles(r, i, d, has_prev, qrows, tri_cur, tri_prev):
                        s = lax.dot_general(qcat, k_ref[krows, :].astype(BF16), _NT, preferred_element_type=F32)
                        ss.append((s[:BLK] + bias, s[BLK:] + bias))
                        krs.append(krows)
                    work.append((qrows, ss, krs))
                for qrows, ss, krs in work:
                    e0 = ss[0][0] if len(ss) == 1 else jnp.maximum(ss[0][0], ss[1][0])
                    e1 = ss[0][1] if len(ss) == 1 else jnp.maximum(ss[0][1], ss[1][1])
                    n0 = jnp.max(e0, axis=1, keepdims=True)
                    n1 = jnp.max(e1, axis=1, keepdims=True)
                    if not first:
                        mo, lo = m_ref[qrows, :], l_ref[qrows, :]
                        m0, m1 = mo[:, 0:1], mo[:, hd:hd + 1]
                        n0, n1 = jnp.maximum(n0, m0), jnp.maximum(n1, m1)
                        a0, a1 = jnp.exp(m0 - n0), jnp.exp(m1 - n1)
                    ps = [(jnp.exp(s0 - n0), jnp.exp(s1 - n1)) for s0, s1 in ss]
                    t0 = ps[0][0] if len(ps) == 1 else ps[0][0] + ps[1][0]
                    t1 = ps[0][1] if len(ps) == 1 else ps[0][1] + ps[1][1]
                    l0 = jnp.sum(t0, axis=1, keepdims=True)
                    l1 = jnp.sum(t1, axis=1, keepdims=True)
                    acc = None
                    for (p0, p1), krows in zip(ps, krs):
                        vcat = _stack_heads(v_ref[krows, :].astype(BF16), masks)
                        pv = jnp.dot(jnp.concatenate([p0, p1], axis=1).astype(BF16), vcat, preferred_element_type=F32)
                        acc = pv if acc is None else acc + pv
                    if not first:
                        l0 = l0 + a0 * lo[:, 0:1]
                        l1 = l1 + a1 * lo[:, hd:hd + 1]
                        acc = acc + a_ref[qrows, :] * jnp.where(masks[0], a0, a1)
                    if last:
                        o_ref[qrows, :] = acc / jnp.where(masks[0], l0, l1)
                        lse_ref[qrows, :] = jnp.where(masks[0], n0 + jnp.log(l0), n1 + jnp.log(l1))
                    else:
                        m_ref[qrows, :] = jnp.where(masks[0], n0, n1)
                        l_ref[qrows, :] = jnp.where(masks[0], l0, l1)
                        a_ref[qrows, :] = acc

            _dil_sets(d, qset)

    spec = lambda off: pl.BlockSpec((None, SEQ, LANES), lambda b, j: (b, 0, off + j))
    ospec = pl.BlockSpec((None, SEQ, LANES), lambda b, j: (b, 0, j))
    osd = jax.ShapeDtypeStruct((nb, SEQ, DIL_W), F32)
    return pl.pallas_call(
        body, out_shape=(osd, osd), grid=(nb, ncol),
        in_specs=[spec(0), spec(ncol), spec(2 * ncol)], out_specs=(ospec, ospec),
        scratch_shapes=[pltpu.VMEM((SEQ, LANES), F32)] * 3,
        compiler_params=_cparams(dimension_semantics=("parallel", "parallel")), name=name,
    )(qkv, qkv, qkv)


def _dil_bwd(qkv, do, o, lse, tabs, *, do_off, name):
    nb = qkv.shape[0]
    ncol = DIL_W // LANES
    hd = HEAD_DIM

    def body(q_ref, k_ref, v_ref, do_ref, o_ref, lse_ref, c_ref, s1_ref, s2_ref, dqo_ref, dko_ref, dvo_ref,
             dq_ref, dk_ref, dv_ref, dl_ref, dof_ref):
        masks = _head_masks(2)
        tri_cur, tri_prev = _tri_bias(True), _tri_bias(False)
        dq_ref[...] = jnp.zeros_like(dq_ref)
        dk_ref[...] = jnp.zeros_like(dk_ref)
        dv_ref[...] = jnp.zeros_like(dv_ref)

        def delta_body(i, c):
            rows = pl.ds(pl.multiple_of(i * BLK, BLK), BLK)
            dof = do_ref[rows, :].astype(F32)
            dof_ref[rows, :] = dof
            prod = dof * o_ref[rows, :]
            z = jnp.zeros_like(prod)
            dl_ref[rows, :] = jnp.where(masks[0], jnp.sum(jnp.where(masks[0], prod, z), axis=1, keepdims=True),
                                        jnp.sum(jnp.where(masks[1], prod, z), axis=1, keepdims=True))
            return c

        lax.fori_loop(0, NBLK, delta_body, 0)

        for d in DIL_STEPS:
            def qset(blocks, d=d):
                work = []
                for r, i, has_prev in blocks:
                    qrows = _dil_rows(r, i, d)
                    qcat = _stack_heads((q_ref[qrows, :] * QK_SCALE).astype(BF16), masks)
                    docat = _stack_heads(dof_ref[qrows, :].astype(BF16), masks)
                    tiles = []
                    for krows, bias in _dil_key_tiles(r, i, d, has_prev, qrows, tri_cur, tri_prev):
                        s = lax.dot_general(qcat, k_ref[krows, :].astype(BF16), _NT, preferred_element_type=F32)
                        dp = lax.dot_general(docat, v_ref[krows, :].astype(BF16), _NT, preferred_element_type=F32)
                        tiles.append((krows, s, dp, bias))
                    work.append((qrows, qcat, docat, tiles))
                for qrows, qcat, docat, tiles in work:
                    lseb, dlb = lse_ref[qrows, :], dl_ref[qrows, :]
                    lse0, lse1 = lseb[:, 0:1], lseb[:, hd:hd + 1]
                    dl0, dl1 = dlb[:, 0:1], dlb[:, hd:hd + 1]
                    dq = None
                    for krows, s, dp, bias in tiles:
                        p0 = jnp.exp(s[:BLK] + bias - lse0)
                        p1 = jnp.exp(s[BLK:] + bias - lse1)
                        ds0 = p0 * (dp[:BLK] - dl0)
                        ds1 = p1 * (dp[BLK:] - dl1)
                        ds0b, ds1b = ds0.astype(BF16), ds1.astype(BF16)
                        pcat = jnp.concatenate([p0.astype(BF16), p1.astype(BF16)], axis=0)
                        dscat = jnp.concatenate([ds0b, ds1b], axis=0)
                        dv_ref[krows, :] += lax.dot_general(pcat, docat, _T0, preferred_element_type=F32)
                        dk_ref[krows, :] += lax.dot_general(dscat, qcat, _T0, preferred_element_type=F32)
                        dsrow = jnp.concatenate([ds0b, ds1b], axis=1)
                        kcat = _stack_heads((k_ref[krows, :] * QK_SCALE).astype(BF16), masks)
                        t = jnp.dot(dsrow, kcat, preferred_element_type=F32)
                        dq = t if dq is None else dq + t
                    dq_ref[qrows, :] += dq

            _dil_sets(d, qset)

        def out_body(i, c):
            rows = pl.ds(pl.multiple_of(i * BLK, BLK), BLK)
            tab = (c_ref[rows, :], s1_ref[rows, :], s2_ref[rows, :])
            dqo_ref[rows, :] = _rope_apply(dq_ref[rows, :], *tab, transpose=True).astype(dqo_ref.dtype)
            dko_ref[rows, :] = _rope_apply(dk_ref[rows, :], *tab, transpose=True).astype(dko_ref.dtype)
            dvo_ref[rows, :] = dv_ref[rows, :].astype(dvo_ref.dtype)
            return c

        lax.fori_loop(0, NBLK, out_body, 0)

    spec = lambda off: pl.BlockSpec((None, SEQ, LANES), lambda b, j: (b, 0, off + j))
    ospec = pl.BlockSpec((None, SEQ, LANES), lambda b, j: (b, 0, j))
    tspec = pl.BlockSpec((SEQ, LANES), lambda b, j: (0, 0))
    osd = jax.ShapeDtypeStruct((nb, SEQ, DIL_W), BF16)
    return pl.pallas_call(
        body, out_shape=(osd, osd, osd), grid=(nb, ncol),
        in_specs=[spec(0), spec(ncol), spec(2 * ncol), spec(do_off), ospec, ospec, tspec, tspec, tspec],
        out_specs=(ospec, ospec, ospec),
        scratch_shapes=[pltpu.VMEM((SEQ, LANES), F32)] * 5,
        compiler_params=_cparams(dimension_semantics=("parallel", "parallel")), name=name,
    )(qkv, qkv, qkv, do, o, lse, *tabs)


FOX_FWD_GROUP = 16
FOX_BWD_GROUP = 16
assert NBLK % FOX_FWD_GROUP == 0 and NBLK % FOX_BWD_GROUP == 0
_FOX_COLS = tuple(c // LANES for c in (C_FQ, C_FK, C_FV))


def _fox_specs():
    cols = [pl.BlockSpec((None, SEQ, LANES), (lambda b, j, off=off: (b, 0, off + j))) for off in _FOX_COLS]
    ospec = pl.BlockSpec((None, SEQ, LANES), lambda b, j: (b, 0, j))
    crspec = pl.BlockSpec((None, None, NBLK, 8, BLK), lambda b, j: (b, j, 0, 0, 0))
    return cols, ospec, crspec


def _fox_key_rows(t, e, g):
    return pl.ds(pl.multiple_of((g * t + e) * BLK, BLK), BLK)


def _fox_fwd(p3, crow, *, name):
    nb = p3.shape[0]
    g = FOX_FWD_GROUP

    def body(q_ref, k_ref, v_ref, cr_ref, o_ref, lse_ref):
        masks = _head_masks(2)
        tri = _tri_bias(True)

        def qk(qcat, t, nblk=g):
            return tuple(lax.dot_general(qcat, k_ref[_fox_key_rows(t, e, g), :], _NT, preferred_element_type=F32) for e in range(nblk))

        def consume(ss, t, state, nblk, diag):
            m0, m1, l0, l1, acc = state
            us = []
            for e in range(nblk):
                cr = cr_ref[g * t + e]
                u0 = ss[e][:BLK] - cr[0:1, :]
                u1 = ss[e][BLK:] - cr[1:2, :]
                if diag and e == nblk - 1:
                    u0, u1 = u0 + tri, u1 + tri
                us.append((u0, u1))
            x0 = functools.reduce(jnp.maximum, [u[0] for u in us])
            x1 = functools.reduce(jnp.maximum, [u[1] for u in us])
            n0 = jnp.maximum(m0, jnp.max(x0, axis=1, keepdims=True))
            n1 = jnp.maximum(m1, jnp.max(x1, axis=1, keepdims=True))
            a0, a1 = jnp.exp(m0 - n0), jnp.exp(m1 - n1)
            acc = acc * jnp.where(masks[0], a0, a1)
            t0 = t1 = None
            for e in range(nblk):
                p0, p1 = jnp.exp(us[e][0] - n0), jnp.exp(us[e][1] - n1)
                t0 = p0 if t0 is None else t0 + p0
                t1 = p1 if t1 is None else t1 + p1
                pcat = jnp.concatenate([p0, p1], axis=1)
                hi = pcat.astype(BF16)
                lo = (pcat - hi.astype(F32)).astype(BF16)
                vcat = _stack_heads(v_ref[_fox_key_rows(t, e, g), :], masks)
                acc = acc + jnp.dot(hi, vcat, preferred_element_type=F32) + jnp.dot(lo, vcat, preferred_element_type=F32)
            l0 = a0 * l0 + jnp.sum(t0, axis=1, keepdims=True)
            l1 = a1 * l1 + jnp.sum(t1, axis=1, keepdims=True)
            return n0, n1, l0, l1, acc

        def gbody(ng, c):
            neg = jnp.full((BLK, 1), NEG_INF, F32)
            z1 = jnp.zeros((BLK, 1), F32)
            rows = [pl.ds(pl.multiple_of((g * ng + a) * BLK, BLK), BLK) for a in range(g)]
            qcats = [_stack_heads(q_ref[rows[a], :] * QK_SCALE, masks) for a in range(g)]
            def step(t, cc):
                cur = [qk(qcats[a], t) for a in range(g)]
                return tuple(consume(cur[a], t, cc[a], g, False) for a in range(g))

            init = (neg, neg, z1, z1, jnp.zeros((BLK, LANES), F32))
            done = lax.fori_loop(0, ng, step, tuple(init for a in range(g)))
            last = [qk(qcats[a], ng, a + 1) for a in range(g)]
            for a in range(g):
                ss, state = last[a], done[a]
                m0, m1, l0, l1, acc = consume(ss, ng, state, a + 1, True)
                o_ref[rows[a], :] = acc / jnp.where(masks[0], l0, l1)
                lse_ref[rows[a], :] = jnp.where(masks[0], m0 + jnp.log(l0), m1 + jnp.log(l1))
            return c

        lax.fori_loop(0, NBLK // g, gbody, 0)

    cols, ospec, crspec = _fox_specs()
    osd = jax.ShapeDtypeStruct((nb, SEQ, FOX_W), F32)
    return pl.pallas_call(
        body, out_shape=(osd, osd), grid=(nb, FOX_W // LANES), in_specs=cols + [crspec], out_specs=(ospec, ospec),
        compiler_params=_cparams(dimension_semantics=("parallel", "parallel")), name=name,
    )(p3, p3, p3, crow)


def _fox_bwd(p3, crow, do, o, lse, *, do_off, name):
    nb = p3.shape[0]
    g = FOX_BWD_GROUP
    hd = HEAD_DIM

    def body(q_ref, k_ref, v_ref, cr_ref, do_ref, o_ref, lse_ref, dq_ref, dko_ref, dvo_ref, dcr_ref, dk_ref, dv_ref):
        masks = _head_masks(2)
        tri = _tri_bias(True)
        dk_ref[...] = jnp.zeros_like(dk_ref)
        dv_ref[...] = jnp.zeros_like(dv_ref)
        dcr_ref[...] = jnp.zeros_like(dcr_ref)

        def products(qcat, docat, t, nblk=g):
            out = []
            for e in range(nblk):
                krows = _fox_key_rows(t, e, g)
                out.append(lax.dot_general(qcat, k_ref[krows, :], _NT, preferred_element_type=F32))
                out.append(lax.dot_general(docat, v_ref[krows, :], _NT, preferred_element_type=F32))
            return tuple(out)

        def consume(prod, t, ctx, dq, nblk, diag):
            qcat, docat, lse0, lse1, dl0, dl1 = ctx
            for e in range(nblk):
                jb = g * t + e
                krows = _fox_key_rows(t, e, g)
                s, dp = prod[2 * e], prod[2 * e + 1]
                cr = cr_ref[jb]
                u0 = s[:BLK] - cr[0:1, :]
                u1 = s[BLK:] - cr[1:2, :]
                if diag and e == nblk - 1:
                    u0, u1 = u0 + tri, u1 + tri
                p0 = jnp.exp(u0 - lse0)
                p1 = jnp.exp(u1 - lse1)
                ds0 = p0 * (dp[:BLK] - dl0)
                ds1 = p1 * (dp[BLK:] - dl1)
                dcr_ref[jb, 0:1, :] += jnp.sum(ds0, axis=0, keepdims=True)
                dcr_ref[jb, 1:2, :] += jnp.sum(ds1, axis=0, keepdims=True)
                ds0b, ds1b = ds0.astype(BF16), ds1.astype(BF16)
                pcat = jnp.concatenate([p0.astype(BF16), p1.astype(BF16)], axis=0)
                dscat = jnp.concatenate([ds0b, ds1b], axis=0)
                dv_ref[krows, :] += lax.dot_general(pcat, docat, _T0, preferred_element_type=F32)
                dk_ref[krows, :] += lax.dot_general(dscat, qcat, _T0, preferred_element_type=F32)
                dsrow = jnp.concatenate([ds0b, ds1b], axis=1)
                dq = dq + jnp.dot(dsrow, _stack_heads(k_ref[krows, :] * QK_SCALE, masks), preferred_element_type=F32)
            return dq

        def gbody(ng, c):
            ctxs, rows = [], []
            for a in range(g):
                r = pl.ds(pl.multiple_of((g * ng + a) * BLK, BLK), BLK)
                qcat = _stack_heads(q_ref[r, :] * QK_SCALE, masks)
                dob = do_ref[r, :].astype(BF16)
                prod = dob.astype(F32) * o_ref[r, :]
                z = jnp.zeros_like(prod)
                dl0 = jnp.sum(jnp.where(masks[0], prod, z), axis=1, keepdims=True)
                dl1 = jnp.sum(jnp.where(masks[1], prod, z), axis=1, keepdims=True)
                lseb = lse_ref[r, :]
                ctxs.append((qcat, _stack_heads(dob, masks), lseb[:, 0:1], lseb[:, hd:hd + 1], dl0, dl1))
                rows.append(r)
            def step(t, cc):
                cur = [products(ctxs[a][0], ctxs[a][1], t) for a in range(g)]
                return tuple(consume(cur[a], t, ctxs[a], cc[a], g, False) for a in range(g))

            done = lax.fori_loop(0, ng, step, tuple(jnp.zeros((BLK, LANES), F32) for a in range(g)))
            last = [products(ctxs[a][0], ctxs[a][1], ng, a + 1) for a in range(g)]
            for a in range(g):
                dq_ref[rows[a], :] = consume(last[a], ng, ctxs[a], done[a], a + 1, True).astype(dq_ref.dtype)
            return c

        lax.fori_loop(0, NBLK // g, gbody, 0)
        dko_ref[...] = dk_ref[...].astype(dko_ref.dtype)
        dvo_ref[...] = dv_ref[...].astype(dvo_ref.dtype)

    cols, ospec, crspec = _fox_specs()
    dospec = pl.BlockSpec((None, SEQ, LANES), lambda b, j: (b, 0, do_off + j))
    osd = jax.ShapeDtypeStruct((nb, SEQ, FOX_W), BF16)
    return pl.pallas_call(
        body, out_shape=(osd, osd, osd, jax.ShapeDtypeStruct((nb, FOX_W // LANES, NBLK, 8, BLK), F32)),
        grid=(nb, FOX_W // LANES), in_specs=cols + [crspec, dospec, ospec, ospec], out_specs=(ospec, ospec, ospec, crspec),
        scratch_shapes=[pltpu.VMEM((SEQ, LANES), F32)] * 2,
        compiler_params=_cparams(dimension_semantics=("parallel", "parallel")), name=name,
    )(p3, p3, p3, crow, do, o, lse)


_B1, _B2 = FOX_W // LANES, (FOX_W + DIL_W) // LANES


def _dy_gate_bwd(dx2b, wo, fox, dil, memo, p16, *, tm, tn, name):
    t, d = dx2b.shape
    assert FOX_W % tn == 0 and DIL_W % tn == 0 and MEM_W % tn == 0 and all(c % tn == 0 for c in (C_FG, C_DG, C_MG))
    n1, n2, n3 = FOX_W // tn, (FOX_W + DIL_W) // tn, MIX_W // tn

    def body(dx_ref, w_ref, f_ref, d_ref, m_ref, g_ref, da_ref, dg_ref):
        j = pl.program_id(1)
        wv = w_ref[...]
        for c0 in range(0, tm, min(tm, 2 * MM_CHUNK)):
            rows = pl.ds(c0, min(tm, 2 * MM_CHUNK))
            dyv = lax.dot_general(dx_ref[rows, :], wv, _NT, preferred_element_type=F32)
            a = jnp.where(j < n1, f_ref[rows, :], jnp.where(j < n2, d_ref[rows, :], m_ref[rows, :]))
            gt = g_ref[rows, :].astype(F32)
            sg = 1.0 / (1.0 + jnp.exp(-gt))
            da_ref[rows, :] = (dyv * gt * sg).astype(da_ref.dtype)
            dg_ref[rows, :] = (dyv * a * sg * (1.0 + gt * (1.0 - sg))).astype(dg_ref.dtype)

    def gcol(j):
        return jnp.where(j < n1, C_FG // tn + j, jnp.where(j < n2, C_DG // tn + j - n1, C_MG // tn + j - n2))

    tile = pl.BlockSpec((tm, tn), lambda i, j: (i, j))
    return pl.pallas_call(
        body,
        out_shape=(jax.ShapeDtypeStruct((t, MIX_W), BF16), jax.ShapeDtypeStruct((t, MIX_W), BF16)),
        grid=(t // tm, n3),
        in_specs=[pl.BlockSpec((tm, d), lambda i, j: (i, 0)), pl.BlockSpec((tn, d), lambda i, j: (j, 0)),
                  pl.BlockSpec((tm, tn), lambda i, j: (i, jnp.minimum(j, n1 - 1))),
                  pl.BlockSpec((tm, tn), lambda i, j: (i, jnp.clip(j - n1, 0, n2 - n1 - 1))),
                  pl.BlockSpec((tm, tn), lambda i, j: (i, jnp.clip(j - n2, 0, n3 - n2 - 1))),
                  pl.BlockSpec((tm, tn), lambda i, j: (i, gcol(j)))],
        out_specs=(tile, tile),
        compiler_params=_cparams(dimension_semantics=("parallel", "parallel")),
        name=name,
    )(dx2b, wo, fox, dil, memo, p16)


def _silu(g):
    return g / (1.0 + jnp.exp(-g))


def _out_loss(fox, dil, memo, p16, wo, x, tgt, gfin, *, tm, name):
    t, d = x.shape
    n_feat = float(d)

    def body(f_ref, d_ref, m_ref, fg_ref, dg_ref, mg_ref, w_ref, x_ref, t_ref, g_ref, y_ref, dx_ref, dxb_ref, st_ref):
        i = pl.program_id(0)

        @pl.when(i == 0)
        def _():
            st_ref[...] = jnp.zeros_like(st_ref)

        wv, gv = w_ref[...], g_ref[...]
        half = tm // 2
        for c0 in (0, half):
            rows = pl.ds(c0, half)
            y = jnp.concatenate([(a_ref[rows, :] * _silu(gt_ref[rows, :].astype(F32))).astype(BF16)
                                 for a_ref, gt_ref in ((f_ref, fg_ref), (d_ref, dg_ref), (m_ref, mg_ref))], axis=1)
            y_ref[rows, :] = y
            x2 = x_ref[rows, :] + jnp.dot(y, wv, preferred_element_type=F32)
            r = lax.rsqrt(jnp.mean(x2 * x2, axis=-1, keepdims=True) + RMS_EPS)
            nrm = x2 * r
            err = nrm * gv - t_ref[rows, :]
            dout = err * (1.0 / n_feat)
            dn = dout * gv
            dx2 = r * (dn - nrm * jnp.mean(dn * nrm, axis=-1, keepdims=True))
            dx_ref[rows, :] = dx2
            dxb_ref[rows, :] = dx2.astype(dxb_ref.dtype)
            st_ref[0:1, :] += jnp.sum(dout * nrm, axis=0, keepdims=True)
            st_ref[1:2, :] += (0.5 / n_feat) * jnp.sum(err * err, axis=0, keepdims=True)

    row = pl.BlockSpec((tm, d), lambda i: (i, 0))
    whole = lambda w: pl.BlockSpec((tm, w), lambda i: (i, 0))
    gate = lambda w, col: pl.BlockSpec((tm, w), lambda i: (i, col // w))
    return pl.pallas_call(
        body,
        out_shape=(jax.ShapeDtypeStruct((t, MIX_W), BF16), jax.ShapeDtypeStruct((t, d), F32), jax.ShapeDtypeStruct((t, d), BF16),
                   jax.ShapeDtypeStruct((8, d), F32)),
        grid=(t // tm,),
        in_specs=[whole(FOX_W), whole(DIL_W), whole(MEM_W), gate(FOX_W, C_FG), gate(DIL_W, C_DG), gate(MEM_W, C_MG),
                  pl.BlockSpec((MIX_W, d), lambda i: (0, 0)), row, row, pl.BlockSpec((1, d), lambda i: (0, 0))],
        out_specs=(pl.BlockSpec((tm, MIX_W), lambda i: (i, 0)), row, row, pl.BlockSpec((8, d), lambda i: (0, 0))),
        compiler_params=_cparams(dimension_semantics=("arbitrary",)),
        name=name,
    )(fox, dil, memo, p16, p16, p16, wo, x, tgt, gfin)


def _dh_rms_bwd(dp, w, x, g, resid, *, tm, name):
    t, d = x.shape
    kdim = dp.shape[1]

    def body(*refs):
        if resid is not None:
            dp_ref, w_ref, x_ref, g_ref, r_ref, dx_ref, gg_ref = refs
        else:
            dp_ref, w_ref, x_ref, g_ref, dx_ref, gg_ref = refs

        @pl.when(pl.program_id(0) == 0)
        def _():
            gg_ref[...] = jnp.zeros_like(gg_ref)

        dh = lax.dot_general(dp_ref[...], w_ref[...], _NT, preferred_element_type=F32)
        xv = x_ref[...]
        r = lax.rsqrt(jnp.mean(xv * xv, axis=-1, keepdims=True) + RMS_EPS)
        nrm = xv * r
        dn = dh * g_ref[...]
        dx = r * (dn - nrm * jnp.mean(dn * nrm, axis=-1, keepdims=True))
        if resid is not None:
            dx = dx + r_ref[...]
        dx_ref[...] = dx
        gg_ref[0:1, :] += jnp.sum(dh * nrm, axis=0, keepdims=True)

    row = pl.BlockSpec((tm, d), lambda i: (i, 0))
    in_specs = [pl.BlockSpec((tm, kdim), lambda i: (i, 0)),
                pl.BlockSpec((d, kdim), lambda i: (0, 0), pipeline_mode=pl.Buffered(1)), row,
                pl.BlockSpec((1, d), lambda i: (0, 0))]
    args = [dp, w, x, g]
    if resid is not None:
        in_specs.append(row)
        args.append(resid)
    return pl.pallas_call(
        body,
        out_shape=(jax.ShapeDtypeStruct((t, d), F32), jax.ShapeDtypeStruct((8, d), F32)),
        grid=(t // tm,),
        in_specs=in_specs,
        out_specs=(row, pl.BlockSpec((8, d), lambda i: (0, 0))),
        compiler_params=_cparams(dimension_semantics=("arbitrary",)),
        name=name,
    )(*args)


_FLOG0 = 4 * FOX_W
_W_IN_SEGMENTS = ((0, _FLOG0, 0), (_FLOG0, _FLOG0 + FOX_HEADS, PW), (_FLOG0 + FOX_HEADS, IN_W, C_DQ))
SHARD_W = IN_W // N_CHIPS


def _rearrange_w_in(shards):
    def cols(lo, hi):
        parts = []
        for k in range(N_CHIPS):
            a, b = max(lo, k * SHARD_W), min(hi, (k + 1) * SHARD_W)
            if a < b:
                parts.append(shards[k][:, a - k * SHARD_W:b - k * SHARD_W])
        return parts

    (a0, a1, _), (f0, f1, _), (b0, b1, _) = _W_IN_SEGMENTS
    pad = jnp.zeros((shards[0].shape[0], PWF - PW - FOX_HEADS), shards[0].dtype)
    return jnp.concatenate(cols(a0, a1) + cols(b0, b1) + cols(f0, f1) + [pad], axis=1)


def _w_in_grad_slabs(g):
    slabs = []
    for k in range(N_CHIPS):
        parts = []
        for lo, hi, at in _W_IN_SEGMENTS:
            a, b = max(lo, k * SHARD_W), min(hi, (k + 1) * SHARD_W)
            if a < b:
                parts.append(g[:, at + a - lo:at + b - lo])
        slabs.append(jnp.concatenate(parts, axis=1))
    return jnp.stack(slabs, axis=0)


def _local_grads(x, mem, norm_g, w_r, b_forget, mem_norm_g, w_kv, w_o, final_norm_g, tgt, start_reduce=None,
                 start_reduce_small=None, early_token=None, late_weights=None):
    nb = x.shape[0]
    t = nb * SEQ
    x2d = x.reshape(t, D_MODEL)
    tgt2d = tgt.reshape(t, D_MODEL)
    tabs = _rope_tables()
    bpad = jnp.pad(b_forget.reshape(1, FOX_HEADS), ((0, 0), (0, LANES - FOX_HEADS)))

    gain0 = norm_g.reshape(1, D_MODEL)
    if early_token is not None:
        gain0 = gain0 + early_token[0:1, 0:1]
    h, p16, dqkv, flog = _proj(x2d, gain0, w_r, tabs, n=PWF, tm=1024, tn=768, name="proj")
    c12 = _flog_fwd(flog, bpad, nb=nb, ts=256, name="flog_fwd")

    crow = c12[:, :FOX_HEADS].reshape(nb, NBLK, BLK, FOX_HEADS // 2, 2).transpose(0, 3, 1, 4, 2)
    crow = jnp.pad(crow, ((0, 0), (0, 0), (0, 0), (0, 6), (0, 0)))
    p3 = p16.reshape(nb, SEQ, PWF)
    fox, fox_lse = _fox_fwd(p3, crow, name="fox_fwd")
    if late_weights is not None:
        w_kv, w_o = late_weights(fox_lse)

    dqkv3 = dqkv.reshape(nb, SEQ, 3 * DIL_W)
    dil, dil_lse = _dil_fwd(dqkv3, name="dil_fwd")

    mh = _rms_fwd(mem.reshape(nb * MEM_LEN, D_MODEL), mem_norm_g.reshape(1, D_MODEL), tm=nb * MEM_LEN, name="rms_mem")
    mkv = _matmul(mh, w_kv, out_dtype=BF16, tm=nb * MEM_LEN, tn=512, tk=D_MODEL, name="mem_kv")
    mkv3 = mkv.reshape(nb, MEM_LEN, 2 * MEM_W)
    memo, mem_lse = _mem_fwd(p3, mkv3, qoff=C_MQ // LANES, name="mem_fwd")

    fox2, dil2, memo2 = fox.reshape(t, FOX_W), dil.reshape(t, DIL_W), memo.reshape(t, MEM_W)
    y, dx2, dx2b, st = _out_loss(fox2, dil2, memo2, p16, w_o, x2d, tgt2d, final_norm_g.reshape(1, D_MODEL), tm=256,
                                 name="out_loss")

    g_wo = _matmul(y, dx2b, mode="tn", out_dtype=BF16, tm=1024, tn=512, tk=t, name="grad_w_out")
    datt, dgate = _dy_gate_bwd(dx2b, w_o, fox2, dil2, memo2, p16, tm=2048, tn=256, name="dy_gate_bwd")
    datt3 = datt.reshape(nb, SEQ, MIX_W)

    dmq, dmk, dmv = _mem_bwd(p3, mkv3, datt3, memo, mem_lse, qoff=C_MQ // LANES, do_off=_B2, name="mem_bwd")
    dmkv = jnp.concatenate([dmk, dmv], axis=-1).reshape(nb * MEM_LEN, 2 * MEM_W).astype(BF16)
    g_wkv = _matmul(mh, dmkv, mode="tn", out_dtype=BF16, tm=512, tn=512, tk=nb * MEM_LEN, name="grad_w_kv")
    mem_gain = mem_norm_g.reshape(1, D_MODEL)
    if start_reduce_small is not None:
        tok = start_reduce_small(g_wkv, g_wo)[0:1, 0:1]
        mem_gain, crow = mem_gain + tok, crow + tok
    _, gmn = _dh_rms_bwd(dmkv, w_kv, mem.reshape(nb * MEM_LEN, D_MODEL), mem_gain, None, tm=nb * MEM_LEN, name="mem_rms_bwd")

    dfq, dfk, dfv, dcr = _fox_bwd(p3, crow, datt3, fox, fox_lse, do_off=0, name="fox_bwd")
    dcol = -dcr[:, :, :, :2, :].transpose(0, 2, 4, 1, 3).reshape(t, FOX_HEADS)
    dcol = jnp.pad(dcol, ((0, 0), (0, LANES - FOX_HEADS)))
    dflog, gb = _flog_bwd(dcol, flog, bpad, nb=nb, ts=256, name="flog_bwd")

    ddq, ddk, ddv = _dil_bwd(dqkv3, datt3, dil, dil_lse, tabs, do_off=_B1, name="dil_bwd")

    flat = lambda a: a.reshape(t, -1)
    dp = jnp.concatenate([flat(dfq), flat(dfk), flat(dfv), dgate[:, :FOX_W], flat(ddq), flat(ddk), flat(ddv),
                          dgate[:, FOX_W:FOX_W + DIL_W], flat(dmq), dgate[:, FOX_W + DIL_W:], dflog,
                          jnp.zeros((t, PWF - PW - LANES), BF16)], axis=1)
    g_wr = _matmul(h, dp, mode="tn", out_dtype=BF16, tm=D_MODEL, tn=768, tk=t, name="grad_w_in")
    gain = norm_g.reshape(1, D_MODEL)
    if start_reduce is not None:
        gain = gain + start_reduce(g_wr)[0:1, 0:1]
    gx, gng = _dh_rms_bwd(dp, w_r, x2d, gain, dx2, tm=256, name="in_rms_bwd")

    gb_row = jnp.pad(gb[0:1, :], ((0, 0), (0, D_MODEL - LANES)))
    small = jnp.concatenate([gng[0:1], gmn[0:1], st[0:1], gb_row, st[1:2], jnp.zeros((3, D_MODEL), F32)], axis=0)
    return gx.reshape(nb, SEQ, D_MODEL), g_wr, g_wkv, g_wo, small


MESH = pl.DeviceIdType.MESH
ANY = pl.BlockSpec(memory_space=pl.ANY)


def _place():
    x, y, c = lax.axis_index("x"), lax.axis_index("y"), lax.axis_index("c")
    other_chips = [(1 - x, y), (x, 1 - y), (1 - x, 1 - y)]
    return x, y, c, other_chips


def _gather_weights(shards):
    n = len(shards)

    def body(*refs):
        in_refs, out_refs = refs[:n], refs[n:2 * n]
        send_sems, recv_sems = refs[2 * n:]
        x, y, c, chips = _place()
        me_chip = 2 * x + y
        sibling = (x, y, 1 - c)

        def half(ref, pc, rows):
            return ref.at[pl.ds(pc * (rows // 2), rows // 2), :]

        def rcopy(k, src, dst, to):
            return pltpu.make_async_remote_copy(src_ref=src, dst_ref=dst, send_sem=send_sems.at[k], recv_sem=recv_sems.at[k],
                                                device_id=to, device_id_type=MESH)

        sends = []
        for t in range(n):
            rows = shards[t].shape[0]
            for j, chip in enumerate(chips):
                cp = rcopy(6 * t + j, half(in_refs[t], c, rows), half(out_refs[t].at[me_chip], c, rows), (*chip, c))
                cp.start()
                sends.append(cp)
        for t in range(n):
            rows = shards[t].shape[0]
            for j, chip in enumerate(chips):
                slot = out_refs[t].at[2 * chip[0] + chip[1]]
                rcopy(6 * t + j, half(slot, c, rows), half(slot, c, rows), sibling).wait_recv()
                fw = rcopy(6 * t + 3 + j, half(slot, c, rows), half(slot, c, rows), sibling)
                fw.start()
                sends.append(fw)
        for t in range(n):
            rows = shards[t].shape[0]
            for j, chip in enumerate(chips):
                slot = out_refs[t].at[2 * chip[0] + chip[1]]
                rcopy(6 * t + 3 + j, half(slot, 1 - c, rows), half(slot, 1 - c, rows), sibling).wait_recv()
        for cp in sends:
            cp.wait_send()

    return pl.pallas_call(
        body,
        out_shape=tuple(jax.ShapeDtypeStruct((N_CHIPS,) + s.shape, s.dtype) for s in shards),
        in_specs=[ANY] * n,
        out_specs=tuple([ANY] * n),
        scratch_shapes=[pltpu.SemaphoreType.DMA((6 * n,)), pltpu.SemaphoreType.DMA((6 * n,))],
        name="gather_weights",
    )(*shards)


def _pair_exchange(gs, *, name):
    n = len(gs)

    def body(*refs):
        g_refs, r_refs = refs[:n], refs[n:2 * n]
        send_sems, recv_sems = refs[2 * n:]
        x, y, c, _ = _place()
        cps = []
        for t in range(n):
            hr = gs[t].shape[1] // 2
            cp = pltpu.make_async_remote_copy(src_ref=g_refs[t].at[:, pl.ds((1 - c) * hr, hr), :], dst_ref=r_refs[t],
                                              send_sem=send_sems.at[t], recv_sem=recv_sems.at[t],
                                              device_id=(x, y, 1 - c), device_id_type=MESH)
            cp.start()
            cps.append(cp)
        for cp in cps:
            cp.wait()

    return pl.pallas_call(
        body,
        out_shape=tuple(jax.ShapeDtypeStruct((g.shape[0], g.shape[1] // 2, g.shape[2]), g.dtype) for g in gs),
        in_specs=[ANY] * n,
        out_specs=tuple([ANY] * n),
        scratch_shapes=[pltpu.SemaphoreType.DMA((n,)), pltpu.SemaphoreType.DMA((n,))],
        name=name,
    )(*gs)


_HBM = pl.BlockSpec(memory_space=pltpu.HBM)
_SEM = pl.BlockSpec(memory_space=pltpu.SEMAPHORE)
_DATAFLOW = pltpu.SideEffectType.DATAFLOW_SIDE_EFFECTING


def _chip_copies(p_refs, land_refs, send_sems, recv_sems):
    x, y, c, chips = _place()
    me_chip = 2 * x + y
    return [pltpu.make_async_remote_copy(src_ref=p_refs[t].at[2 * chip[0] + chip[1]], dst_ref=land_refs[t].at[me_chip],
                                         send_sem=send_sems.at[3 * t + j], recv_sem=recv_sems.at[3 * t + j],
                                         device_id=(*chip, c), device_id_type=MESH)
            for t in range(len(p_refs)) for j, chip in enumerate(chips)]


def _chip_exchange_start(ps, *, tag):
    n = len(ps)

    def body(*refs):
        p_refs, land_refs = refs[:n], refs[n:2 * n]
        send_sems, recv_sems = refs[2 * n:2 * n + 2]
        token = refs[-1]
        for cp in _chip_copies(p_refs, land_refs, send_sems, recv_sems):
            cp.start()
        token[...] = jnp.zeros_like(token)

    hbm = [pltpu.HBM(p.shape, p.dtype) for p in ps]
    args = [pltpu.with_memory_space_constraint(p, pltpu.HBM) for p in ps]
    args += [pltpu.with_memory_space_constraint(lax.empty(p.shape, p.dtype), pltpu.HBM) for p in ps]
    out = pl.pallas_call(
        body,
        name=f"chip_exchange_start_{tag}",
        out_shape=(pltpu.SemaphoreType.DMA((3 * n,)), pltpu.SemaphoreType.DMA((3 * n,)), *hbm, *hbm,
                   jax.ShapeDtypeStruct((8, LANES), F32)),
        in_specs=[_HBM] * (2 * n),
        out_specs=(_SEM, _SEM, *([_HBM] * (2 * n)), pl.BlockSpec(memory_space=pltpu.VMEM)),
        input_output_aliases={i: 2 + i for i in range(2 * n)},
        compiler_params=pltpu.CompilerParams(has_side_effects=_DATAFLOW),
    )(*args)
    return out[0], out[1], out[2:2 + n], out[2 + n:2 + 2 * n], out[-1]


def _chip_exchange_wait(send_sems, recv_sems, p_thru, land_thru, after, *, tag):
    n = len(p_thru)

    def body(*refs):
        p_refs, land_refs = refs[:n], refs[n:2 * n]
        ssem, rsem = refs[2 * n:2 * n + 2]
        for cp in _chip_copies(p_refs, land_refs, ssem, rsem):
            cp.wait_send()
            cp.wait_recv()

    hbm = [pltpu.HBM(p.shape, p.dtype) for p in p_thru]
    out = pl.pallas_call(
        body,
        name=f"chip_exchange_wait_{tag}",
        out_shape=(*hbm, *hbm),
        in_specs=[_HBM] * (2 * n) + [_SEM, _SEM, ANY],
        out_specs=tuple([_HBM] * (2 * n)),
        input_output_aliases={i: i for i in range(2 * n)},
        compiler_params=pltpu.CompilerParams(has_side_effects=_DATAFLOW),
    )(*p_thru, *land_thru, send_sems, recv_sems, after)
    return out[:n], out[n:]


def _shard_copies(s_refs, land_refs, send_sems, recv_sems):
    x, y, c, chips = _place()
    me_chip = 2 * x + y
    return [pltpu.make_async_remote_copy(src_ref=s_refs[t], dst_ref=land_refs[t].at[me_chip],
                                         send_sem=send_sems.at[3 * t + j], recv_sem=recv_sems.at[3 * t + j],
                                         device_id=(*chip, c), device_id_type=MESH)
            for t in range(len(s_refs)) for j, chip in enumerate(chips)]


def _gather_late_start(shards):
    n = len(shards)

    def body(*refs):
        s_refs, land_refs = refs[:n], refs[n:2 * n]
        send_sems, recv_sems = refs[2 * n:2 * n + 2]
        token = refs[-1]
        for cp in _shard_copies(s_refs, land_refs, send_sems, recv_sems):
            cp.start()
        token[...] = jnp.zeros_like(token)

    lands = [(N_CHIPS,) + s.shape for s in shards]
    args = [pltpu.with_memory_space_constraint(s, pltpu.HBM) for s in shards]
    args += [pltpu.with_memory_space_constraint(lax.empty(shp, s.dtype), pltpu.HBM) for shp, s in zip(lands, shards)]
    out = pl.pallas_call(
        body,
        name="gather_late_start",
        out_shape=(pltpu.SemaphoreType.DMA((3 * n,)), pltpu.SemaphoreType.DMA((3 * n,)),
                   *[pltpu.HBM(s.shape, s.dtype) for s in shards], *[pltpu.HBM(shp, s.dtype) for shp, s in zip(lands, shards)],
                   jax.ShapeDtypeStruct((8, LANES), F32)),
        in_specs=[_HBM] * (2 * n),
        out_specs=(_SEM, _SEM, *([_HBM] * (2 * n)), pl.BlockSpec(memory_space=pltpu.VMEM)),
        input_output_aliases={i: 2 + i for i in range(2 * n)},
        compiler_params=pltpu.CompilerParams(has_side_effects=_DATAFLOW),
    )(*args)
    return out[0], out[1], out[2:2 + n], out[2 + n:2 + 2 * n], out[-1]


def _gather_late_wait(send_sems, recv_sems, s_thru, land_thru, after):
    n = len(s_thru)

    def body(*refs):
        s_refs, land_refs = refs[:n], refs[n:2 * n]
        ssem, rsem = refs[2 * n:2 * n + 2]
        for cp in _shard_copies(s_refs, land_refs, ssem, rsem):
            cp.wait_send()
            cp.wait_recv()

    out = pl.pallas_call(
        body,
        name="gather_late_wait",
        out_shape=(*[pltpu.HBM(s.shape, s.dtype) for s in s_thru], *[pltpu.HBM(l.shape, l.dtype) for l in land_thru]),
        in_specs=[_HBM] * (2 * n) + [_SEM, _SEM, ANY],
        out_specs=tuple([_HBM] * (2 * n)),
        input_output_aliases={i: i for i in range(2 * n)},
        compiler_params=pltpu.CompilerParams(has_side_effects=_DATAFLOW),
    )(*s_thru, *land_thru, send_sems, recv_sems, after)
    return out[:n], out[n:]


def _pair_swap(rs):
    n = len(rs)

    def body(*refs):
        r_refs, o_refs = refs[:n], refs[n:2 * n]
        send_sems, recv_sems = refs[2 * n:]
        x, y, c, _ = _place()
        cps = []
        for t in range(n):
            cp = pltpu.make_async_remote_copy(src_ref=r_refs[t], dst_ref=o_refs[t], send_sem=send_sems.at[t],
                                              recv_sem=recv_sems.at[t], device_id=(x, y, 1 - c), device_id_type=MESH)
            cp.start()
            cps.append(cp)
        for cp in cps:
            cp.wait()

    return pl.pallas_call(
        body,
        out_shape=tuple(jax.ShapeDtypeStruct(r.shape, r.dtype) for r in rs),
        in_specs=[ANY] * n,
        out_specs=tuple([ANY] * n),
        scratch_shapes=[pltpu.SemaphoreType.DMA((n,)), pltpu.SemaphoreType.DMA((n,))],
        name="pair_swap",
    )(*rs)


N_DEV = 8
LOSS_ROW = 4


def _small_allreduce(small):
    def body(s_ref, o_ref, all_ref, send_sems, recv_sems):
        x, y, c, _ = _place()
        me = 4 * x + 2 * y + c
        all_ref[me] = s_ref[...]
        cps = []
        for k in range(1, N_DEV):
            peer = tuple(1 - p if (k >> s) & 1 else p for p, s in ((x, 2), (y, 1), (c, 0)))
            cp = pltpu.make_async_remote_copy(src_ref=s_ref, dst_ref=all_ref.at[me], send_sem=send_sems.at[k - 1],
                                              recv_sem=recv_sems.at[k - 1], device_id=peer, device_id_type=MESH)
            cp.start()
            cps.append(cp)
        for cp in cps:
            cp.wait()
        tot = all_ref[0]
        for d in range(1, N_DEV):
            tot = tot + all_ref[d]
        o_ref[...] = tot
        o_ref[LOSS_ROW:LOSS_ROW + 1, :] = jnp.broadcast_to(jnp.sum(tot[LOSS_ROW:LOSS_ROW + 1, :], axis=1, keepdims=True),
                                                          (1, tot.shape[1]))

    vm = pl.BlockSpec(memory_space=pltpu.VMEM)
    return pl.pallas_call(
        body,
        out_shape=jax.ShapeDtypeStruct(small.shape, small.dtype),
        in_specs=[vm],
        out_specs=vm,
        scratch_shapes=[pltpu.VMEM((N_DEV,) + small.shape, small.dtype), pltpu.SemaphoreType.DMA((N_DEV - 1,)),
                        pltpu.SemaphoreType.DMA((N_DEV - 1,))],
        name="small_allreduce",
    )(small)


def _sum_pair(g, recv, cidx, *, tr, name):
    n, hr, cols = recv.shape
    nr = hr // tr

    def body(c_ref, g_ref, r_ref, o_ref):
        o_ref[...] = (g_ref[...].astype(F32) + r_ref[...].astype(F32)).astype(o_ref.dtype)

    grid_spec = pltpu.PrefetchScalarGridSpec(
        num_scalar_prefetch=1,
        grid=(n, nr),
        in_specs=[pl.BlockSpec((None, tr, cols), lambda k, i, c_ref: (k, c_ref[0] * nr + i, 0)),
                  pl.BlockSpec((None, tr, cols), lambda k, i, c_ref: (k, i, 0))],
        out_specs=pl.BlockSpec((None, tr, cols), lambda k, i, c_ref: (k, i, 0)),
    )
    return pl.pallas_call(body, out_shape=jax.ShapeDtypeStruct(recv.shape, BF16), grid_spec=grid_spec,
                          compiler_params=_cparams(), name=name)(cidx, g, recv)


def _sum_chips(p, *, tr, name):
    _, rows, cols = p.shape

    def body(p_ref, o_ref):
        tot = p_ref[0].astype(F32)
        for k in range(1, N_CHIPS):
            tot = tot + p_ref[k].astype(F32)
        o_ref[...] = tot

    return pl.pallas_call(
        body,
        out_shape=jax.ShapeDtypeStruct((rows, cols), F32),
        grid=(rows // tr,),
        in_specs=[pl.BlockSpec((N_CHIPS, tr, cols), lambda i: (0, i, 0))],
        out_specs=pl.BlockSpec((tr, cols), lambda i: (i, 0)),
        compiler_params=_cparams(),
        name=name,
    )(p)


def _adamw(w, g, m, v, *, tr, name):
    rows, cols = w.shape
    bc1 = 1.0 / (1.0 - ADAM_B1 ** ADAM_STEP)
    bc2 = 1.0 / (1.0 - ADAM_B2 ** ADAM_STEP)

    def body(w_ref, g_ref, m_ref, v_ref, d_ref, nm_ref, nv_ref):
        gv = g_ref[...]
        nm = ADAM_B1 * m_ref[...] + (1.0 - ADAM_B1) * gv
        nv = ADAM_B2 * v_ref[...] + (1.0 - ADAM_B2) * (gv * gv)
        d_ref[...] = -ADAM_LR * ((nm * bc1) / (jnp.sqrt(nv * bc2) + ADAM_EPS) + ADAM_WD * w_ref[...])
        nm_ref[...] = nm
        nv_ref[...] = nv

    spec = pl.BlockSpec((tr, cols), lambda i: (i, 0))
    sd = jax.ShapeDtypeStruct((rows, cols), F32)
    return pl.pallas_call(body, out_shape=(sd, sd, sd), grid=(rows // tr,), in_specs=[spec] * 4, out_specs=(spec,) * 3,
                          compiler_params=_cparams(), name=name)(w, g, m, v)


def _adamw_halves(w, own, sib, cidx, m, v, *, tr, name):
    rows, cols = w.shape
    hr = own.shape[0]
    nr = hr // tr
    assert rows == 2 * hr and hr % tr == 0
    bc1 = 1.0 / (1.0 - ADAM_B1 ** ADAM_STEP)
    bc2 = 1.0 / (1.0 - ADAM_B2 ** ADAM_STEP)

    def body(c_ref, w_ref, o_ref, s_ref, m_ref, v_ref, g_ref, d_ref, nm_ref, nv_ref):
        mine = (pl.program_id(0) // nr) == c_ref[0]
        gv = jnp.where(mine, o_ref[...], s_ref[...])
        nm = ADAM_B1 * m_ref[...] + (1.0 - ADAM_B1) * gv
        nv = ADAM_B2 * v_ref[...] + (1.0 - ADAM_B2) * (gv * gv)
        g_ref[...] = gv
        d_ref[...] = -ADAM_LR * ((nm * bc1) / (jnp.sqrt(nv * bc2) + ADAM_EPS) + ADAM_WD * w_ref[...])
        nm_ref[...] = nm
        nv_ref[...] = nv

    full = pl.BlockSpec((tr, cols), lambda i, c_ref: (i, 0))
    half = pl.BlockSpec((tr, cols), lambda i, c_ref: (i % nr, 0))
    sd = jax.ShapeDtypeStruct((rows, cols), F32)
    grid_spec = pltpu.PrefetchScalarGridSpec(num_scalar_prefetch=1, grid=(rows // tr,), in_specs=[full, half, half, full, full],
                                             out_specs=(full,) * 4)
    return pl.pallas_call(body, out_shape=(sd,) * 4, grid_spec=grid_spec, compiler_params=_cparams(), name=name)(
        cidx, w, own, sib, m, v)


def _pack_small(norm, mem_norm, final_norm, b_forget):
    rows = [norm.reshape(1, D_MODEL), mem_norm.reshape(1, D_MODEL), final_norm.reshape(1, D_MODEL),
            jnp.pad(b_forget.reshape(1, FOX_HEADS), ((0, 0), (0, D_MODEL - FOX_HEADS))), jnp.zeros((4, D_MODEL), F32)]
    return jnp.concatenate(rows, axis=0)


def _unpack_small(a):
    return a[0:1], a[3:4, :FOX_HEADS], a[1:2], a[2]


def kernel(x, mem, norm_g, w_in, b_forget, mem_norm_g, w_mem_kv, w_out, final_norm_g, loss_target, m_norm_g, m_w_in, m_b_forget, m_mem_norm_g, m_w_mem_kv, m_w_out, m_final_norm_g, v_norm_g, v_w_in, v_b_forget, v_mem_norm_g, v_w_mem_kv, v_w_out, v_final_norm_g):
    core = lax.axis_index("c").astype(jnp.int32)
    me_chip = (2 * lax.axis_index("x") + lax.axis_index("y")).astype(jnp.int32)
    cidx = core.reshape(1)

    def own_slot(arr, own):
        return lax.dynamic_update_slice(arr, own[None].astype(arr.dtype), (me_chip,) + (0,) * own.ndim)

    win_b, late = w_in[0].astype(BF16), [w_mem_kv[0].astype(BF16), w_out[0].astype(BF16)]
    g_in, = _gather_weights([win_b])
    g_in, late = lax.optimization_barrier((own_slot(g_in, win_b), late))
    w_r = _rearrange_w_in([g_in[k] for k in range(N_CHIPS)])
    *late_flight, early_token = _gather_late_start(late)

    def late_weights(after):
        shards, landed = _gather_late_wait(*late_flight, after)
        g_kv, g_out = (own_slot(g, s) for g, s in zip(landed, shards))
        return g_kv.reshape(D_MODEL, 2 * MEM_W), g_out.reshape(MIX_W, D_MODEL)

    trs = (128, 128, 256)
    names = ("w_in", "w_mem_kv", "w_out")
    flights = {}

    def exchange(slabs, nms, ts, tag):
        recv = _pair_exchange(slabs, name=f"pair_exchange_{tag}")
        pair = [_sum_pair(g, r, cidx, tr=tr, name=f"sum_pair_{nm}") for g, r, tr, nm in zip(slabs, recv, ts, nms)]
        if tag == "w_in":
            pair[0] = _w_in_grad_slabs(pair[0][0])
        *flights[tag], token = _chip_exchange_start(pair, tag=tag)
        return token

    def start_reduce_small(g_wkv, g_wo):
        slabs = [g_wkv.reshape(N_CHIPS, D_MODEL // N_CHIPS, 2 * MEM_W), g_wo.reshape(N_CHIPS, MIX_W // N_CHIPS, D_MODEL)]
        return exchange(slabs, names[1:], trs[1:], "small")

    def start_reduce(g_wr):
        return exchange([g_wr[None]], names[:1], trs[:1], "w_in")

    gx, g_wr, g_wkv, g_wo, small = _local_grads(x, mem, norm_g, w_r, b_forget, mem_norm_g, None, None, final_norm_g, loss_target,
                                                start_reduce=start_reduce, start_reduce_small=start_reduce_small,
                                                early_token=early_token, late_weights=late_weights)

    pair, landed = [], []
    for tag in ("w_in", "small"):
        p, l = _chip_exchange_wait(*flights[tag], small, tag=tag)
        pair += list(p)
        landed += list(l)
    got = [lax.dynamic_update_slice(g, lax.dynamic_slice(p, (me_chip, 0, 0), (1,) + p.shape[1:]), (me_chip, 0, 0))
           for g, p in zip(landed, pair)]
    red = [_sum_chips(p, tr=tr, name=f"sum_chips_{nm}") for p, tr, nm in zip(got, trs, names)]
    sib = _pair_swap(red)

    outs = {}
    for nm, r, s, w, m, v, tr in zip(names, red, sib, (w_in, w_mem_kv, w_out), (m_w_in, m_w_mem_kv, m_w_out),
                                     (v_w_in, v_w_mem_kv, v_w_out), trs):
        outs[nm] = tuple(a[None] for a in _adamw_halves(w[0], r, s, cidx, m[0], v[0], tr=tr, name=f"adamw_{nm}"))

    gsum = _small_allreduce(small)
    sd, sm, sv = _adamw(_pack_small(norm_g, mem_norm_g, final_norm_g, b_forget), gsum,
                        _pack_small(m_norm_g, m_mem_norm_g, m_final_norm_g, m_b_forget),
                        _pack_small(v_norm_g, v_mem_norm_g, v_final_norm_g, v_b_forget), tr=8, name="adamw_small")
    loss = gsum[LOSS_ROW, 0]

    def group(i, small_arr):
        ng, bf, mg, fg = _unpack_small(small_arr)
        return (ng, outs["w_in"][i], bf, mg, outs["w_mem_kv"][i], outs["w_out"][i], fg)

    return (loss, gx, *group(0, gsum), *group(1, sd), *group(2, sm), *group(3, sv))
```

```python
import functools
import math

import jax
import jax.numpy as jnp
from jax import lax
from jax.experimental import pallas as pl
from jax.experimental.pallas import tpu as pltpu

F32 = jnp.float32
BF16 = jnp.bfloat16

D_MODEL = 1024
SEQ = 2048
HEAD_DIM = 64
FOX_HEADS = 12
DIL_HEADS = 12
MEM_HEADS = 4
MEM_HEAD_DIM = 128
MEM_LEN = 256
FOX_W = FOX_HEADS * HEAD_DIM
DIL_W = DIL_HEADS * HEAD_DIM
MEM_W = MEM_HEADS * MEM_HEAD_DIM
MIX_W = FOX_W + DIL_W + MEM_W
DILATIONS = ((128, 1), (512, 4), (2048, 16))
ROPE_THETA = 500000.0
ROPE_DIM = HEAD_DIM // 4
RMS_EPS = 1e-6
NEG_INF = -1e30
IN_SIZES = [FOX_W] * 4 + [FOX_HEADS] + [DIL_W] * 4 + [MEM_W] * 2
IN_W = sum(IN_SIZES)

ADAM_LR = 0.001
ADAM_B1 = 0.9
ADAM_B2 = 0.999
ADAM_EPS = 1e-08
ADAM_WD = 0.01
ADAM_STEP = 10

LANES = 128
N_CHIPS = 4
PW = 7168
PWF = PW + 4 * LANES
C_FQ, C_FK, C_FV, C_FG = 0, 768, 1536, 2304
C_DQ, C_DK, C_DV, C_DG = 3072, 3840, 4608, 5376
C_MQ, C_MG = 6144, 6656
VMEM_LIMIT = 48 * 1024 * 1024


def _cparams(**kw):
    return pltpu.CompilerParams(vmem_limit_bytes=VMEM_LIMIT, **kw)


MM_CHUNK = 256


def _matmul(a, b, *, out_dtype, tm, tn, tk, name, mode="nn"):
    if mode == "tn":
        (kdim, m), n = a.shape, b.shape[1]
        a_spec = pl.BlockSpec((tk, tm), lambda i, j, k: (k, i))
        b_spec = pl.BlockSpec((tk, tn), lambda i, j, k: (k, j))
        dims = _T0
    elif mode == "nt":
        (m, kdim), n = a.shape, b.shape[0]
        a_spec = pl.BlockSpec((tm, tk), lambda i, j, k: (i, k))
        b_spec = pl.BlockSpec((tn, tk), lambda i, j, k: (j, k))
        dims = _NT
    else:
        (m, kdim), n = a.shape, b.shape[1]
        a_spec = pl.BlockSpec((tm, tk), lambda i, j, k: (i, k))
        b_spec = pl.BlockSpec((tk, tn), lambda i, j, k: (k, j))
        dims = (((1,), (0,)), ((), ()))
    nk = kdim // tk
    assert m % tm == 0 and n % tn == 0 and kdim % tk == 0

    def body(a_ref, b_ref, o_ref, *scratch):
        if nk == 1:
            bv = b_ref[...]
            for c0 in range(0, tm, min(tm, MM_CHUNK)):
                rows = pl.ds(c0, min(tm, MM_CHUNK))
                av = a_ref[:, rows] if mode == "tn" else a_ref[rows, :]
                o_ref[rows, :] = lax.dot_general(av, bv, dims, preferred_element_type=F32).astype(o_ref.dtype)
            return
        prod = lax.dot_general(a_ref[...], b_ref[...], dims, preferred_element_type=F32)
        acc_ref, = scratch
        k = pl.program_id(2)

        @pl.when(k == 0)
        def _():
            acc_ref[...] = prod

        @pl.when(k > 0)
        def _():
            acc_ref[...] += prod

        @pl.when(k == nk - 1)
        def _():
            o_ref[...] = acc_ref[...].astype(o_ref.dtype)

    return pl.pallas_call(
        body,
        out_shape=jax.ShapeDtypeStruct((m, n), out_dtype),
        grid=(m // tm, n // tn, nk),
        in_specs=[a_spec, b_spec],
        out_specs=pl.BlockSpec((tm, tn), lambda i, j, k: (i, j)),
        scratch_shapes=[pltpu.VMEM((tm, tn), F32)] if nk > 1 else [],
        compiler_params=_cparams(dimension_semantics=("parallel", "parallel", "arbitrary")),
        name=name,
    )(a, b)


def _rms_fwd(x, g, *, tm, name):
    t, d = x.shape

    def body(x_ref, g_ref, h_ref):
        xv = x_ref[...]
        r = lax.rsqrt(jnp.mean(xv * xv, axis=-1, keepdims=True) + RMS_EPS)
        h_ref[...] = (xv * r * g_ref[...]).astype(h_ref.dtype)

    return pl.pallas_call(
        body,
        out_shape=jax.ShapeDtypeStruct((t, d), BF16),
        grid=(t // tm,),
        in_specs=[pl.BlockSpec((tm, d), lambda i: (i, 0)), pl.BlockSpec((1, d), lambda i: (0, 0))],
        out_specs=pl.BlockSpec((tm, d), lambda i: (i, 0)),
        compiler_params=_cparams(),
        name=name,
    )(x, g)


def _rope_tables():
    half = ROPE_DIM // 2
    pos = jnp.arange(SEQ, dtype=F32)
    inv_freq = 1.0 / (ROPE_THETA ** (jnp.arange(0, ROPE_DIM, 2, dtype=F32) / ROPE_DIM))
    ang = pos[:, None] * inv_freq[None, :]
    cos, sin = jnp.cos(ang), jnp.sin(ang)
    one = jnp.ones((SEQ, HEAD_DIM - ROPE_DIM), F32)
    zero = jnp.zeros((SEQ, HEAD_DIM - ROPE_DIM), F32)
    zh = jnp.zeros((SEQ, half), F32)
    c = jnp.concatenate([cos, cos, one], axis=1)
    s1 = jnp.concatenate([zh, sin, zero], axis=1)
    s2 = jnp.concatenate([-sin, zh, zero], axis=1)
    rep = LANES // HEAD_DIM
    return jnp.tile(c, (1, rep)), jnp.tile(s1, (1, rep)), jnp.tile(s2, (1, rep))


def _rope_apply(t, c, s1, s2, transpose=False):
    n = t.shape[-1]
    rep = n // LANES
    c, s1, s2 = (jnp.tile(u, (1, rep)) for u in (c, s1, s2))
    half = ROPE_DIM // 2
    if not transpose:
        return t * c + pltpu.roll(t, half, 1) * s1 + pltpu.roll(t, n - half, 1) * s2
    return t * c + pltpu.roll(t * s1, n - half, 1) + pltpu.roll(t * s2, half, 1)


PROJ_CHUNK = 256


def _proj(x, g, w, tabs, *, n, tm, tn, name):
    t, d = x.shape
    assert C_DQ % tn == 0 and (C_DV - C_DQ) % tn == 0 and (C_DG - C_DQ) % tn == 0
    rope_lo, rope_hi, dil_hi = C_DQ // tn, C_DV // tn, C_DG // tn
    flog_blk, flog_at = PW // tn, PW % tn
    assert flog_at % LANES == 0 and flog_at + LANES <= tn
    s_blocks = SEQ // tm

    def body(x_ref, g_ref, w_ref, c_ref, s1_ref, s2_ref, h_ref, o_ref, f_ref, fl_ref, h_scr):
        j = pl.program_id(1)

        @pl.when(j == 0)
        def _():
            xv = x_ref[...]
            r = lax.rsqrt(jnp.mean(xv * xv, axis=-1, keepdims=True) + RMS_EPS)
            hv = (xv * r * g_ref[...]).astype(BF16)
            h_scr[...] = hv
            h_ref[...] = hv

        def tile(kind):
            wv = w_ref[...]
            for c0 in range(0, tm, PROJ_CHUNK):
                rows = pl.ds(c0, PROJ_CHUNK)
                acc = jnp.dot(h_scr[rows, :], wv, preferred_element_type=F32)
                if kind == "rope":
                    acc = _rope_apply(acc, c_ref[rows, :], s1_ref[rows, :], s2_ref[rows, :])
                o_ref[rows, :] = acc.astype(o_ref.dtype)
                if kind in ("rope", "dv"):
                    f_ref[rows, :] = acc
                if kind == "flog":
                    fl_ref[rows, :] = acc[:, flog_at:flog_at + LANES]

        is_rope = jnp.logical_and(j >= rope_lo, j < rope_hi)
        is_dv = jnp.logical_and(j >= rope_hi, j < dil_hi)
        is_flog = j == flog_blk
        pl.when(is_rope)(functools.partial(tile, "rope"))
        pl.when(is_dv)(functools.partial(tile, "dv"))
        pl.when(is_flog)(functools.partial(tile, "flog"))
        pl.when(jnp.logical_not(jnp.logical_or(jnp.logical_or(is_rope, is_dv), is_flog)))(functools.partial(tile, "plain"))

    tab_spec = pl.BlockSpec((tm, LANES), lambda i, j: (i % s_blocks, 0))
    f_spec = pl.BlockSpec((tm, tn), lambda i, j: (i, jnp.clip(j - rope_lo, 0, dil_hi - rope_lo - 1)))
    row = pl.BlockSpec((tm, d), lambda i, j: (i, 0))
    return pl.pallas_call(
        body,
        out_shape=(jax.ShapeDtypeStruct((t, d), BF16), jax.ShapeDtypeStruct((t, n), BF16),
                   jax.ShapeDtypeStruct((t, 3 * DIL_W), F32), jax.ShapeDtypeStruct((t, LANES), F32)),
        grid=(t // tm, n // tn),
        in_specs=[row, pl.BlockSpec((1, d), lambda i, j: (0, 0)), pl.BlockSpec((d, tn), lambda i, j: (0, j)),
                  tab_spec, tab_spec, tab_spec],
        out_specs=(row, pl.BlockSpec((tm, tn), lambda i, j: (i, j)), f_spec, pl.BlockSpec((tm, LANES), lambda i, j: (i, 0))),
        scratch_shapes=[pltpu.VMEM((tm, d), BF16)],
        compiler_params=_cparams(dimension_semantics=("parallel", "arbitrary")),
        name=name,
    )(x, g, w, *tabs)


def _split3(x):
    hi = x.astype(BF16)
    r1 = x - hi.astype(F32)
    mid = r1.astype(BF16)
    lo = (r1 - mid.astype(F32)).astype(BF16)
    return hi, mid, lo


def _dot3(sel, x, sel_is_lhs):
    out = None
    for piece in _split3(x):
        t = jnp.dot(sel, piece, preferred_element_type=F32) if sel_is_lhs else jnp.dot(piece, sel, preferred_element_type=F32)
        out = t if out is None else out + t
    return out


def _flog_fwd(flog, bpad, *, nb, ts, name):
    ns = SEQ // ts

    def body(f_ref, b_ref, c_ref, carry_ref):
        s = pl.program_id(1)

        @pl.when(s == 0)
        def _():
            carry_ref[...] = jnp.zeros_like(carry_ref)

        z = f_ref[...] + b_ref[...]
        logf = jnp.minimum(z, 0.0) - jnp.log(1.0 + jnp.exp(-jnp.abs(z)))
        r = lax.broadcasted_iota(jnp.int32, (ts, ts), 0)
        c = lax.broadcasted_iota(jnp.int32, (ts, ts), 1)
        tri = jnp.where(r >= c, 1.0, 0.0).astype(BF16)
        cs = _dot3(tri, logf, True) + carry_ref[0:1, :]
        carry_ref[...] = jnp.broadcast_to(cs[ts - 1:ts, :], carry_ref.shape)
        c_ref[...] = cs

    return pl.pallas_call(
        body,
        out_shape=jax.ShapeDtypeStruct((nb * SEQ, LANES), F32),
        grid=(nb, ns),
        in_specs=[pl.BlockSpec((ts, LANES), lambda b, s: (b * ns + s, 0)), pl.BlockSpec((1, LANES), lambda b, s: (0, 0))],
        out_specs=pl.BlockSpec((ts, LANES), lambda b, s: (b * ns + s, 0)),
        scratch_shapes=[pltpu.VMEM((8, LANES), F32)],
        compiler_params=_cparams(dimension_semantics=("parallel", "arbitrary")),
        name=name,
    )(flog, bpad)


def _flog_bwd(dcol, flog, bpad, *, nb, ts, name):
    ns = SEQ // ts

    def body(d_ref, f_ref, b_ref, o_ref, gb_ref, carry_ref):
        bi = pl.program_id(0)
        s = pl.program_id(1)

        @pl.when(s == 0)
        def _():
            carry_ref[...] = jnp.zeros_like(carry_ref)

        @pl.when(jnp.logical_and(bi == 0, s == 0))
        def _():
            gb_ref[...] = jnp.zeros_like(gb_ref)

        r = lax.broadcasted_iota(jnp.int32, (ts, ts), 0)
        c = lax.broadcasted_iota(jnp.int32, (ts, ts), 1)
        tri = jnp.where(r <= c, 1.0, 0.0).astype(BF16)
        rc = _dot3(tri, d_ref[...], True) + carry_ref[0:1, :]
        carry_ref[...] = jnp.broadcast_to(rc[0:1, :], carry_ref.shape)
        z = f_ref[...] + b_ref[...]
        dz = rc / (1.0 + jnp.exp(z))
        o_ref[...] = dz.astype(o_ref.dtype)
        gb_ref[...] += jnp.broadcast_to(jnp.sum(dz, axis=0, keepdims=True), gb_ref.shape)

    rev = lambda b, s: (b * ns + (ns - 1 - s), 0)
    return pl.pallas_call(
        body,
        out_shape=(jax.ShapeDtypeStruct((nb * SEQ, LANES), BF16), jax.ShapeDtypeStruct((8, LANES), F32)),
        grid=(nb, ns),
        in_specs=[pl.BlockSpec((ts, LANES), rev), pl.BlockSpec((ts, LANES), rev), pl.BlockSpec((1, LANES), lambda b, s: (0, 0))],
        out_specs=(pl.BlockSpec((ts, LANES), rev), pl.BlockSpec((8, LANES), lambda b, s: (0, 0))),
        scratch_shapes=[pltpu.VMEM((8, LANES), F32)],
        compiler_params=_cparams(dimension_semantics=("arbitrary", "arbitrary")),
        name=name,
    )(dcol, flog, bpad)


MEM_TQ = 256
MEM_SET = 4
MEM_SCALE = 1.0 / math.sqrt(MEM_HEAD_DIM)
assert MEM_HEAD_DIM == LANES and SEQ % (MEM_TQ * MEM_SET) == 0


def _head_masks(nh):
    lane = lax.broadcasted_iota(jnp.int32, (1, LANES), 1)
    return [None] if nh == 1 else [lane < HEAD_DIM, lane >= HEAD_DIM]


def _mem_specs(qoff):
    qspec = pl.BlockSpec((None, SEQ, LANES), lambda b, j: (b, 0, qoff + j))
    kspec = pl.BlockSpec((None, MEM_LEN, LANES), lambda b, j: (b, 0, j))
    vspec = pl.BlockSpec((None, MEM_LEN, LANES), lambda b, j: (b, 0, MEM_HEADS + j))
    ospec = pl.BlockSpec((None, SEQ, LANES), lambda b, j: (b, 0, j))
    return qspec, kspec, vspec, ospec


def _mem_rows(g):
    return [pl.ds(pl.multiple_of((MEM_SET * g + a) * MEM_TQ, MEM_TQ), MEM_TQ) for a in range(MEM_SET)]


def _mem_fwd(p3, mkv3, *, qoff, name):
    nb = p3.shape[0]

    def body(q_ref, k_ref, v_ref, o_ref, lse_ref):
        kb, vb = k_ref[...], v_ref[...]

        def qset(g, c):
            rows = _mem_rows(g)
            ss = [lax.dot_general(q_ref[r, :] * MEM_SCALE, kb, _NT, preferred_element_type=F32) for r in rows]
            for r, s in zip(rows, ss):
                m = jnp.max(s, axis=1, keepdims=True)
                p = jnp.exp(s - m)
                l = jnp.sum(p, axis=1, keepdims=True)
                o_ref[r, :] = jnp.dot(p.astype(BF16), vb, preferred_element_type=F32) / l
                lse_ref[r, :] = jnp.broadcast_to(m + jnp.log(l), (MEM_TQ, LANES))
            return c

        lax.fori_loop(0, SEQ // MEM_TQ // MEM_SET, qset, 0)

    qspec, kspec, vspec, ospec = _mem_specs(qoff)
    osd = jax.ShapeDtypeStruct((nb, SEQ, MEM_W), F32)
    return pl.pallas_call(body, out_shape=(osd, osd), grid=(nb, MEM_HEADS), in_specs=[qspec, kspec, vspec],
                          out_specs=(ospec, ospec), compiler_params=_cparams(dimension_semantics=("parallel", "parallel")),
                          name=name)(p3, mkv3, mkv3)


def _mem_bwd(p3, mkv3, do, o, lse, *, qoff, do_off, name):
    nb = p3.shape[0]

    def body(q_ref, k_ref, v_ref, do_ref, o_ref, lse_ref, dq_ref, dk_ref, dv_ref):
        kb, vb = k_ref[...], v_ref[...]
        ks = kb * MEM_SCALE

        def qset(g, carry):
            dk, dv = carry
            work = []
            for r in _mem_rows(g):
                qs = q_ref[r, :] * MEM_SCALE
                dob = do_ref[r, :].astype(BF16)
                s = lax.dot_general(qs, kb, _NT, preferred_element_type=F32)
                dp = lax.dot_general(dob, vb, _NT, preferred_element_type=F32)
                work.append((r, qs, dob, s, dp))
            for r, qs, dob, s, dp in work:
                delta = jnp.sum(dob.astype(F32) * o_ref[r, :], axis=1, keepdims=True)
                p = jnp.exp(s - lse_ref[r, :][:, 0:1])
                ds = (p * (dp - delta)).astype(BF16)
                dq_ref[r, :] = jnp.dot(ds, ks, preferred_element_type=F32).astype(dq_ref.dtype)
                dk = dk + lax.dot_general(ds, qs, _T0, preferred_element_type=F32)
                dv = dv + lax.dot_general(p.astype(BF16), dob, _T0, preferred_element_type=F32)
            return dk, dv

        z = jnp.zeros((MEM_LEN, LANES), F32)
        dk, dv = lax.fori_loop(0, SEQ // MEM_TQ // MEM_SET, qset, (z, z))
        dk_ref[...] = dk
        dv_ref[...] = dv

    qspec, kspec, vspec, ospec = _mem_specs(qoff)
    dospec = pl.BlockSpec((None, SEQ, LANES), lambda b, j: (b, 0, do_off + j))
    kvo = pl.BlockSpec((None, MEM_LEN, LANES), lambda b, j: (b, 0, j))
    kvsd = jax.ShapeDtypeStruct((nb, MEM_LEN, MEM_W), F32)
    return pl.pallas_call(
        body, out_shape=(jax.ShapeDtypeStruct((nb, SEQ, MEM_W), BF16), kvsd, kvsd), grid=(nb, MEM_HEADS),
        in_specs=[qspec, kspec, vspec, dospec, ospec, ospec], out_specs=(ospec, kvo, kvo),
        compiler_params=_cparams(dimension_semantics=("parallel", "parallel")), name=name)(p3, mkv3, mkv3, do, o, lse)


BLK = 128
NBLK = SEQ // BLK
QK_SCALE = 1.0 / math.sqrt(HEAD_DIM)
DIL_STEPS = tuple(d for _, d in DILATIONS)
assert all(w // d == BLK for w, d in DILATIONS)
_T0 = (((0,), (0,)), ((), ()))
_NT = (((1,), (1,)), ((), ()))


def _stack_heads(a, masks):
    z = jnp.zeros_like(a)
    return jnp.concatenate([jnp.where(masks[0], a, z), jnp.where(masks[1], a, z)], axis=0)


def _tri_bias(lower):
    r = lax.broadcasted_iota(jnp.int32, (BLK, BLK), 0)
    c = lax.broadcasted_iota(jnp.int32, (BLK, BLK), 1)
    return jnp.where((c <= r) if lower else (c >= r), 0.0, NEG_INF).astype(F32)


def _dil_rows(r, i, d):
    start = r + i * (BLK * d)
    return pl.ds(start, BLK) if d == 1 else pl.ds(start, BLK, stride=d)


DIL_SET = 4


def _dil_sets(d, fn):
    nbk = SEQ // d // BLK
    if d == 1:
        n = 2 * DIL_SET
        def gbody(g, c):
            fn([(0, n * g + a, None if a == 0 else True) for a in range(n)])
            return c
        lax.fori_loop(0, nbk // n, gbody, 0)
    elif nbk > 1:
        assert nbk == DIL_SET
        def rbody(r, c):
            fn([(r, i, i > 0) for i in range(nbk)])
            return c
        lax.fori_loop(0, d, rbody, 0)
    else:
        def rbody(rr, c):
            fn([(DIL_SET * rr + a, 0, False) for a in range(DIL_SET)])
            return c
        lax.fori_loop(0, d // DIL_SET, rbody, 0)


def _dil_key_tiles(r, i, d, has_prev, qrows, tri_cur, tri_prev):
    tiles = [(qrows, tri_cur)]
    if has_prev is None:
        tiles.append((_dil_rows(r, jnp.maximum(i - 1, 0), d), tri_prev + jnp.where(i > 0, 0.0, NEG_INF)))
    elif has_prev:
        tiles.append((_dil_rows(r, i - 1, d), tri_prev))
    return tiles


def _dil_fwd(qkv, *, name):
    nb = qkv.shape[0]
    ncol = DIL_W // LANES
    hd = HEAD_DIM

    def body(q_ref, k_ref, v_ref, o_ref, lse_ref, m_ref, l_ref, a_ref):
        masks = _head_masks(2)
        tri_cur, tri_prev = _tri_bias(True), _tri_bias(False)
        for pi, d in enumerate(DIL_STEPS):
            first, last = pi == 0, pi == len(DIL_STEPS) - 1

            def qset(blocks, d=d, first=first, last=last):
                work = []
                for r, i, has_prev in blocks:
                    qrows = _dil_rows(r, i, d)
                    qcat = _stack_heads((q_ref[qrows, :] * QK_SCALE).astype(BF16), masks)
                    ss, krs = [], []
                    for krows, bias in _dil_key_tiles(r, i, d, has_prev, qrows, tri_cur, tri_prev):
                        s = lax.dot_general(qcat, k_ref[krows, :].astype(BF16), _NT, preferred_element_type=F32)
                        ss.append((s[:BLK] + bias, s[BLK:] + bias))
                        krs.append(krows)
                    work.append((qrows, ss, krs))
                for qrows, ss, krs in work:
                    e0 = ss[0][0] if len(ss) == 1 else jnp.maximum(ss[0][0], ss[1][0])
                    e1 = ss[0][1] if len(ss) == 1 else jnp.maximum(ss[0][1], ss[1][1])
                    n0 = jnp.max(e0, axis=1, keepdims=True)
                    n1 = jnp.max(e1, axis=1, keepdims=True)
                    if not first:
                        mo, lo = m_ref[qrows, :], l_ref[qrows, :]
                        m0, m1 = mo[:, 0:1], mo[:, hd:hd + 1]
                        n0, n1 = jnp.maximum(n0, m0), jnp.maximum(n1, m1)
                        a0, a1 = jnp.exp(m0 - n0), jnp.exp(m1 - n1)
                    ps = [(jnp.exp(s0 - n0), jnp.exp(s1 - n1)) for s0, s1 in ss]
                    t0 = ps[0][0] if len(ps) == 1 else ps[0][0] + ps[1][0]
                    t1 = ps[0][1] if len(ps) == 1 else ps[0][1] + ps[1][1]
                    l0 = jnp.sum(t0, axis=1, keepdims=True)
                    l1 = jnp.sum(t1, axis=1, keepdims=True)
                    acc = None
                    for (p0, p1), krows in zip(ps, krs):
                        vcat = _stack_heads(v_ref[krows, :].astype(BF16), masks)
                        pv = jnp.dot(jnp.concatenate([p0, p1], axis=1).astype(BF16), vcat, preferred_element_type=F32)
                        acc = pv if acc is None else acc + pv
                    if not first:
                        l0 = l0 + a0 * lo[:, 0:1]
                        l1 = l1 + a1 * lo[:, hd:hd + 1]
                        acc = acc + a_ref[qrows, :] * jnp.where(masks[0], a0, a1)
                    if last:
                        o_ref[qrows, :] = acc / jnp.where(masks[0], l0, l1)
                        lse_ref[qrows, :] = jnp.where(masks[0], n0 + jnp.log(l0), n1 + jnp.log(l1))
                    else:
                        m_ref[qrows, :] = jnp.where(masks[0], n0, n1)
                        l_ref[qrows, :] = jnp.where(masks[0], l0, l1)
                        a_ref[qrows, :] = acc

            _dil_sets(d, qset)

    spec = lambda off: pl.BlockSpec((None, SEQ, LANES), lambda b, j: (b, 0, off + j))
    ospec = pl.BlockSpec((None, SEQ, LANES), lambda b, j: (b, 0, j))
    osd = jax.ShapeDtypeStruct((nb, SEQ, DIL_W), F32)
    return pl.pallas_call(
        body, out_shape=(osd, osd), grid=(nb, ncol),
        in_specs=[spec(0), spec(ncol), spec(2 * ncol)], out_specs=(ospec, ospec),
        scratch_shapes=[pltpu.VMEM((SEQ, LANES), F32)] * 3,
        compiler_params=_cparams(dimension_semantics=("parallel", "parallel")), name=name,
    )(qkv, qkv, qkv)


def _dil_bwd(qkv, do, o, lse, tabs, *, do_off, name):
    nb = qkv.shape[0]
    ncol = DIL_W // LANES
    hd = HEAD_DIM

    def body(q_ref, k_ref, v_ref, do_ref, o_ref, lse_ref, c_ref, s1_ref, s2_ref, dqo_ref, dko_ref, dvo_ref,
             dq_ref, dk_ref, dv_ref, dl_ref, dof_ref):
        masks = _head_masks(2)
        tri_cur, tri_prev = _tri_bias(True), _tri_bias(False)
        dq_ref[...] = jnp.zeros_like(dq_ref)
        dk_ref[...] = jnp.zeros_like(dk_ref)
        dv_ref[...] = jnp.zeros_like(dv_ref)

        def delta_body(i, c):
            rows = pl.ds(pl.multiple_of(i * BLK, BLK), BLK)
            dof = do_ref[rows, :].astype(F32)
            dof_ref[rows, :] = dof
            prod = dof * o_ref[rows, :]
            z = jnp.zeros_like(prod)
            dl_ref[rows, :] = jnp.where(masks[0], jnp.sum(jnp.where(masks[0], prod, z), axis=1, keepdims=True),
                                        jnp.sum(jnp.where(masks[1], prod, z), axis=1, keepdims=True))
            return c

        lax.fori_loop(0, NBLK, delta_body, 0)

        for d in DIL_STEPS:
            def qset(blocks, d=d):
                work = []
                for r, i, has_prev in blocks:
                    qrows = _dil_rows(r, i, d)
                    qcat = _stack_heads((q_ref[qrows, :] * QK_SCALE).astype(BF16), masks)
                    docat = _stack_heads(dof_ref[qrows, :].astype(BF16), masks)
                    tiles = []
                    for krows, bias in _dil_key_tiles(r, i, d, has_prev, qrows, tri_cur, tri_prev):
                        s = lax.dot_general(qcat, k_ref[krows, :].astype(BF16), _NT, preferred_element_type=F32)
                        dp = lax.dot_general(docat, v_ref[krows, :].astype(BF16), _NT, preferred_element_type=F32)
                        tiles.append((krows, s, dp, bias))
                    work.append((qrows, qcat, docat, tiles))
                for qrows, qcat, docat, tiles in work:
                    lseb, dlb = lse_ref[qrows, :], dl_ref[qrows, :]
                    lse0, lse1 = lseb[:, 0:1], lseb[:, hd:hd + 1]
                    dl0, dl1 = dlb[:, 0:1], dlb[:, hd:hd + 1]
                    dq = None
                    for krows, s, dp, bias in tiles:
                        p0 = jnp.exp(s[:BLK] + bias - lse0)
                        p1 = jnp.exp(s[BLK:] + bias - lse1)
                        ds0 = p0 * (dp[:BLK] - dl0)
                        ds1 = p1 * (dp[BLK:] - dl1)
                        ds0b, ds1b = ds0.astype(BF16), ds1.astype(BF16)
                        pcat = jnp.concatenate([p0.astype(BF16), p1.astype(BF16)], axis=0)
                        dscat = jnp.concatenate([ds0b, ds1b], axis=0)
                        dv_ref[krows, :] += lax.dot_general(pcat, docat, _T0, preferred_element_type=F32)
                        dk_ref[krows, :] += lax.dot_general(dscat, qcat, _T0, preferred_element_type=F32)
                        dsrow = jnp.concatenate([ds0b, ds1b], axis=1)
                        kcat = _stack_heads((k_ref[krows, :] * QK_SCALE).astype(BF16), masks)
                        t = jnp.dot(dsrow, kcat, preferred_element_type=F32)
                        dq = t if dq is None else dq + t
                    dq_ref[qrows, :] += dq

            _dil_sets(d, qset)

        def out_body(i, c):
            rows = pl.ds(pl.multiple_of(i * BLK, BLK), BLK)
            tab = (c_ref[rows, :], s1_ref[rows, :], s2_ref[rows, :])
            dqo_ref[rows, :] = _rope_apply(dq_ref[rows, :], *tab, transpose=True).astype(dqo_ref.dtype)
            dko_ref[rows, :] = _rope_apply(dk_ref[rows, :], *tab, transpose=True).astype(dko_ref.dtype)
            dvo_ref[rows, :] = dv_ref[rows, :].astype(dvo_ref.dtype)
            return c

        lax.fori_loop(0, NBLK, out_body, 0)

    spec = lambda off: pl.BlockSpec((None, SEQ, LANES), lambda b, j: (b, 0, off + j))
    ospec = pl.BlockSpec((None, SEQ, LANES), lambda b, j: (b, 0, j))
    tspec = pl.BlockSpec((SEQ, LANES), lambda b, j: (0, 0))
    osd = jax.ShapeDtypeStruct((nb, SEQ, DIL_W), BF16)
    return pl.pallas_call(
        body, out_shape=(osd, osd, osd), grid=(nb, ncol),
        in_specs=[spec(0), spec(ncol), spec(2 * ncol), spec(do_off), ospec, ospec, tspec, tspec, tspec],
        out_specs=(ospec, ospec, ospec),
        scratch_shapes=[pltpu.VMEM((SEQ, LANES), F32)] * 5,
        compiler_params=_cparams(dimension_semantics=("parallel", "parallel")), name=name,
    )(qkv, qkv, qkv, do, o, lse, *tabs)


FOX_FWD_GROUP = 16
FOX_BWD_GROUP = 16
assert NBLK % FOX_FWD_GROUP == 0 and NBLK % FOX_BWD_GROUP == 0
_FOX_COLS = tuple(c // LANES for c in (C_FQ, C_FK, C_FV))


def _fox_specs():
    cols = [pl.BlockSpec((None, SEQ, LANES), (lambda b, j, off=off: (b, 0, off + j))) for off in _FOX_COLS]
    ospec = pl.BlockSpec((None, SEQ, LANES), lambda b, j: (b, 0, j))
    crspec = pl.BlockSpec((None, None, NBLK, 8, BLK), lambda b, j: (b, j, 0, 0, 0))
    return cols, ospec, crspec


def _fox_key_rows(t, e, g):
    return pl.ds(pl.multiple_of((g * t + e) * BLK, BLK), BLK)


def _fox_fwd(p3, crow, *, name):
    nb = p3.shape[0]
    g = FOX_FWD_GROUP

    def body(q_ref, k_ref, v_ref, cr_ref, o_ref, lse_ref):
        masks = _head_masks(2)
        tri = _tri_bias(True)

        def qk(qcat, t, nblk=g):
            out = []
            for e in range(0, nblk, 2):
                n = min(2, nblk - e)
                krows = pl.ds(pl.multiple_of((g * t + e) * BLK, BLK), n * BLK)
                s = lax.dot_general(qcat, k_ref[krows, :], _NT, preferred_element_type=F32)
                out += [s[:, h * BLK:(h + 1) * BLK] for h in range(n)]
            return tuple(out)

        def consume(ss, t, state, nblk, diag):
            m0, m1, l0, l1, acc = state
            us = []
            for e in range(nblk):
                cr = cr_ref[g * t + e]
                u0 = ss[e][:BLK] - cr[0:1, :]
                u1 = ss[e][BLK:] - cr[1:2, :]
                if diag and e == nblk - 1:
                    u0, u1 = u0 + tri, u1 + tri
                us.append((u0, u1))
            x0 = functools.reduce(jnp.maximum, [u[0] for u in us])
            x1 = functools.reduce(jnp.maximum, [u[1] for u in us])
            n0 = jnp.maximum(m0, jnp.max(x0, axis=1, keepdims=True))
            n1 = jnp.maximum(m1, jnp.max(x1, axis=1, keepdims=True))
            a0, a1 = jnp.exp(m0 - n0), jnp.exp(m1 - n1)
            acc = acc * jnp.where(masks[0], a0, a1)
            t0 = t1 = None
            for e in range(nblk):
                p0, p1 = jnp.exp(us[e][0] - n0), jnp.exp(us[e][1] - n1)
                t0 = p0 if t0 is None else t0 + p0
                t1 = p1 if t1 is None else t1 + p1
                pcat = jnp.concatenate([p0, p1], axis=1)
                hi = pcat.astype(BF16)
                lo = (pcat - hi.astype(F32)).astype(BF16)
                vcat = _stack_heads(v_ref[_fox_key_rows(t, e, g), :], masks)
                acc = acc + jnp.dot(hi, vcat, preferred_element_type=F32) + jnp.dot(lo, vcat, preferred_element_type=F32)
            l0 = a0 * l0 + jnp.sum(t0, axis=1, keepdims=True)
            l1 = a1 * l1 + jnp.sum(t1, axis=1, keepdims=True)
            return n0, n1, l0, l1, acc

        def gbody(ng, c):
            neg = jnp.full((BLK, 1), NEG_INF, F32)
            z1 = jnp.zeros((BLK, 1), F32)
            rows = [pl.ds(pl.multiple_of((g * ng + a) * BLK, BLK), BLK) for a in range(g)]
            qcats = [_stack_heads(q_ref[rows[a], :] * QK_SCALE, masks) for a in range(g)]
            def step(t, cc):
                cur = [qk(qcats[a], t) for a in range(g)]
                return tuple(consume(cur[a], t, cc[a], g, False) for a in range(g))

            init = (neg, neg, z1, z1, jnp.zeros((BLK, LANES), F32))
            done = lax.fori_loop(0, ng, step, tuple(init for a in range(g)))
            last = [qk(qcats[a], ng, a + 1) for a in range(g)]
            for a in range(g):
                ss, state = last[a], done[a]
                m0, m1, l0, l1, acc = consume(ss, ng, state, a + 1, True)
                o_ref[rows[a], :] = acc / jnp.where(masks[0], l0, l1)
                lse_ref[rows[a], :] = jnp.where(masks[0], m0 + jnp.log(l0), m1 + jnp.log(l1))
            return c

        lax.fori_loop(0, NBLK // g, gbody, 0)

    cols, ospec, crspec = _fox_specs()
    osd = jax.ShapeDtypeStruct((nb, SEQ, FOX_W), F32)
    return pl.pallas_call(
        body, out_shape=(osd, osd), grid=(nb, FOX_W // LANES), in_specs=cols + [crspec], out_specs=(ospec, ospec),
        compiler_params=_cparams(dimension_semantics=("parallel", "parallel")), name=name,
    )(p3, p3, p3, crow)


def _fox_bwd(p3, crow, do, o, lse, *, do_off, name):
    nb = p3.shape[0]
    g = FOX_BWD_GROUP
    hd = HEAD_DIM

    def body(q_ref, k_ref, v_ref, cr_ref, do_ref, o_ref, lse_ref, dq_ref, dko_ref, dvo_ref, dcr_ref, dk_ref, dv_ref):
        masks = _head_masks(2)
        tri = _tri_bias(True)
        dk_ref[...] = jnp.zeros_like(dk_ref)
        dv_ref[...] = jnp.zeros_like(dv_ref)
        dcr_ref[...] = jnp.zeros_like(dcr_ref)

        def products(qcat, docat, t, nblk=g):
            out = []
            for e in range(0, nblk, 2):
                n = min(2, nblk - e)
                krows = pl.ds(pl.multiple_of((g * t + e) * BLK, BLK), n * BLK)
                s = lax.dot_general(qcat, k_ref[krows, :], _NT, preferred_element_type=F32)
                dp = lax.dot_general(docat, v_ref[krows, :], _NT, preferred_element_type=F32)
                for h in range(n):
                    out += [s[:, h * BLK:(h + 1) * BLK], dp[:, h * BLK:(h + 1) * BLK]]
            return tuple(out)

        def consume(prod, t, ctx, dq, nblk, diag):
            qcat, docat, lse0, lse1, dl0, dl1 = ctx
            for e in range(nblk):
                jb = g * t + e
                krows = _fox_key_rows(t, e, g)
                s, dp = prod[2 * e], prod[2 * e + 1]
                cr = cr_ref[jb]
                u0 = s[:BLK] - cr[0:1, :]
                u1 = s[BLK:] - cr[1:2, :]
                if diag and e == nblk - 1:
                    u0, u1 = u0 + tri, u1 + tri
                p0 = jnp.exp(u0 - lse0)
                p1 = jnp.exp(u1 - lse1)
                ds0 = p0 * (dp[:BLK] - dl0)
                ds1 = p1 * (dp[BLK:] - dl1)
                dcr_ref[jb, 0:1, :] += jnp.sum(ds0, axis=0, keepdims=True)
                dcr_ref[jb, 1:2, :] += jnp.sum(ds1, axis=0, keepdims=True)
                ds0b, ds1b = ds0.astype(BF16), ds1.astype(BF16)
                pcat = jnp.concatenate([p0.astype(BF16), p1.astype(BF16)], axis=0)
                dscat = jnp.concatenate([ds0b, ds1b], axis=0)
                dv_ref[krows, :] += lax.dot_general(pcat, docat, _T0, preferred_element_type=F32)
                dk_ref[krows, :] += lax.dot_general(dscat, qcat, _T0, preferred_element_type=F32)
                dsrow = jnp.concatenate([ds0b, ds1b], axis=1)
                dq = dq + jnp.dot(dsrow, _stack_heads(k_ref[krows, :] * QK_SCALE, masks), preferred_element_type=F32)
            return dq

        def gbody(ng, c):
            ctxs, rows = [], []
            for a in range(g):
                r = pl.ds(pl.multiple_of((g * ng + a) * BLK, BLK), BLK)
                qcat = _stack_heads(q_ref[r, :] * QK_SCALE, masks)
                dob = do_ref[r, :].astype(BF16)
                prod = dob.astype(F32) * o_ref[r, :]
                z = jnp.zeros_like(prod)
                dl0 = jnp.sum(jnp.where(masks[0], prod, z), axis=1, keepdims=True)
                dl1 = jnp.sum(jnp.where(masks[1], prod, z), axis=1, keepdims=True)
                lseb = lse_ref[r, :]
                ctxs.append((qcat, _stack_heads(dob, masks), lseb[:, 0:1], lseb[:, hd:hd + 1], dl0, dl1))
                rows.append(r)
            def step(t, cc):
                cur = [products(ctxs[a][0], ctxs[a][1], t) for a in range(g)]
                return tuple(consume(cur[a], t, ctxs[a], cc[a], g, False) for a in range(g))

            done = lax.fori_loop(0, ng, step, tuple(jnp.zeros((BLK, LANES), F32) for a in range(g)))
            last = [products(ctxs[a][0], ctxs[a][1], ng, a + 1) for a in range(g)]
            for a in range(g):
                dq_ref[rows[a], :] = consume(last[a], ng, ctxs[a], done[a], a + 1, True).astype(dq_ref.dtype)
            return c

        lax.fori_loop(0, NBLK // g, gbody, 0)
        dko_ref[...] = dk_ref[...].astype(dko_ref.dtype)
        dvo_ref[...] = dv_ref[...].astype(dvo_ref.dtype)

    cols, ospec, crspec = _fox_specs()
    dospec = pl.BlockSpec((None, SEQ, LANES), lambda b, j: (b, 0, do_off + j))
    osd = jax.ShapeDtypeStruct((nb, SEQ, FOX_W), BF16)
    return pl.pallas_call(
        body, out_shape=(osd, osd, osd, jax.ShapeDtypeStruct((nb, FOX_W // LANES, NBLK, 8, BLK), F32)),
        grid=(nb, FOX_W // LANES), in_specs=cols + [crspec, dospec, ospec, ospec], out_specs=(ospec, ospec, ospec, crspec),
        scratch_shapes=[pltpu.VMEM((SEQ, LANES), F32)] * 2,
        compiler_params=_cparams(dimension_semantics=("parallel", "parallel")), name=name,
    )(p3, p3, p3, crow, do, o, lse)


_B1, _B2 = FOX_W // LANES, (FOX_W + DIL_W) // LANES


def _dy_gate_bwd(dx2b, wo, fox, dil, memo, p16, *, tm, tn, name):
    t, d = dx2b.shape
    assert FOX_W % tn == 0 and DIL_W % tn == 0 and MEM_W % tn == 0 and all(c % tn == 0 for c in (C_FG, C_DG, C_MG))
    n1, n2, n3 = FOX_W // tn, (FOX_W + DIL_W) // tn, MIX_W // tn

    def body(dx_ref, w_ref, f_ref, d_ref, m_ref, g_ref, da_ref, dg_ref):
        j = pl.program_id(1)
        wv = w_ref[...]
        for c0 in range(0, tm, min(tm, 2 * MM_CHUNK)):
            rows = pl.ds(c0, min(tm, 2 * MM_CHUNK))
            dyv = lax.dot_general(dx_ref[rows, :], wv, _NT, preferred_element_type=F32)
            a = jnp.where(j < n1, f_ref[rows, :], jnp.where(j < n2, d_ref[rows, :], m_ref[rows, :]))
            gt = g_ref[rows, :].astype(F32)
            sg = 1.0 / (1.0 + jnp.exp(-gt))
            da_ref[rows, :] = (dyv * gt * sg).astype(da_ref.dtype)
            dg_ref[rows, :] = (dyv * a * sg * (1.0 + gt * (1.0 - sg))).astype(dg_ref.dtype)

    def gcol(j):
        return jnp.where(j < n1, C_FG // tn + j, jnp.where(j < n2, C_DG // tn + j - n1, C_MG // tn + j - n2))

    tile = pl.BlockSpec((tm, tn), lambda i, j: (i, j))
    return pl.pallas_call(
        body,
        out_shape=(jax.ShapeDtypeStruct((t, MIX_W), BF16), jax.ShapeDtypeStruct((t, MIX_W), BF16)),
        grid=(t // tm, n3),
        in_specs=[pl.BlockSpec((tm, d), lambda i, j: (i, 0)), pl.BlockSpec((tn, d), lambda i, j: (j, 0)),
                  pl.BlockSpec((tm, tn), lambda i, j: (i, jnp.minimum(j, n1 - 1))),
                  pl.BlockSpec((tm, tn), lambda i, j: (i, jnp.clip(j - n1, 0, n2 - n1 - 1))),
                  pl.BlockSpec((tm, tn), lambda i, j: (i, jnp.clip(j - n2, 0, n3 - n2 - 1))),
                  pl.BlockSpec((tm, tn), lambda i, j: (i, gcol(j)))],
        out_specs=(tile, tile),
        compiler_params=_cparams(dimension_semantics=("parallel", "parallel")),
        name=name,
    )(dx2b, wo, fox, dil, memo, p16)


def _silu(g):
    return g / (1.0 + jnp.exp(-g))


def _out_loss(fox, dil, memo, p16, wo, x, tgt, gfin, *, tm, name):
    t, d = x.shape
    n_feat = float(d)

    def body(f_ref, d_ref, m_ref, fg_ref, dg_ref, mg_ref, w_ref, x_ref, t_ref, g_ref, y_ref, dx_ref, dxb_ref, st_ref):
        i = pl.program_id(0)

        @pl.when(i == 0)
        def _():
            st_ref[...] = jnp.zeros_like(st_ref)

        wv, gv = w_ref[...], g_ref[...]
        half = tm // 2
        for c0 in (0, half):
            rows = pl.ds(c0, half)
            y = jnp.concatenate([(a_ref[rows, :] * _silu(gt_ref[rows, :].astype(F32))).astype(BF16)
                                 for a_ref, gt_ref in ((f_ref, fg_ref), (d_ref, dg_ref), (m_ref, mg_ref))], axis=1)
            y_ref[rows, :] = y
            x2 = x_ref[rows, :] + jnp.dot(y, wv, preferred_element_type=F32)
            r = lax.rsqrt(jnp.mean(x2 * x2, axis=-1, keepdims=True) + RMS_EPS)
            nrm = x2 * r
            err = nrm * gv - t_ref[rows, :]
            dout = err * (1.0 / n_feat)
            dn = dout * gv
            dx2 = r * (dn - nrm * jnp.mean(dn * nrm, axis=-1, keepdims=True))
            dx_ref[rows, :] = dx2
            dxb_ref[rows, :] = dx2.astype(dxb_ref.dtype)
            st_ref[0:1, :] += jnp.sum(dout * nrm, axis=0, keepdims=True)
            st_ref[1:2, :] += (0.5 / n_feat) * jnp.sum(err * err, axis=0, keepdims=True)

    row = pl.BlockSpec((tm, d), lambda i: (i, 0))
    whole = lambda w: pl.BlockSpec((tm, w), lambda i: (i, 0))
    gate = lambda w, col: pl.BlockSpec((tm, w), lambda i: (i, col // w))
    return pl.pallas_call(
        body,
        out_shape=(jax.ShapeDtypeStruct((t, MIX_W), BF16), jax.ShapeDtypeStruct((t, d), F32), jax.ShapeDtypeStruct((t, d), BF16),
                   jax.ShapeDtypeStruct((8, d), F32)),
        grid=(t // tm,),
        in_specs=[whole(FOX_W), whole(DIL_W), whole(MEM_W), gate(FOX_W, C_FG), gate(DIL_W, C_DG), gate(MEM_W, C_MG),
                  pl.BlockSpec((MIX_W, d), lambda i: (0, 0)), row, row, pl.BlockSpec((1, d), lambda i: (0, 0))],
        out_specs=(pl.BlockSpec((tm, MIX_W), lambda i: (i, 0)), row, row, pl.BlockSpec((8, d), lambda i: (0, 0))),
        compiler_params=_cparams(dimension_semantics=("arbitrary",)),
        name=name,
    )(fox, dil, memo, p16, p16, p16, wo, x, tgt, gfin)


def _dh_rms_bwd(dp, w, x, g, resid, *, tm, name):
    t, d = x.shape
    kdim = dp.shape[1]

    def body(*refs):
        if resid is not None:
            dp_ref, w_ref, x_ref, g_ref, r_ref, dx_ref, gg_ref = refs
        else:
            dp_ref, w_ref, x_ref, g_ref, dx_ref, gg_ref = refs

        @pl.when(pl.program_id(0) == 0)
        def _():
            gg_ref[...] = jnp.zeros_like(gg_ref)

        dh = lax.dot_general(dp_ref[...], w_ref[...], _NT, preferred_element_type=F32)
        xv = x_ref[...]
        r = lax.rsqrt(jnp.mean(xv * xv, axis=-1, keepdims=True) + RMS_EPS)
        nrm = xv * r
        dn = dh * g_ref[...]
        dx = r * (dn - nrm * jnp.mean(dn * nrm, axis=-1, keepdims=True))
        if resid is not None:
            dx = dx + r_ref[...]
        dx_ref[...] = dx
        gg_ref[0:1, :] += jnp.sum(dh * nrm, axis=0, keepdims=True)

    row = pl.BlockSpec((tm, d), lambda i: (i, 0))
    in_specs = [pl.BlockSpec((tm, kdim), lambda i: (i, 0)),
                pl.BlockSpec((d, kdim), lambda i: (0, 0), pipeline_mode=pl.Buffered(1)), row,
                pl.BlockSpec((1, d), lambda i: (0, 0))]
    args = [dp, w, x, g]
    if resid is not None:
        in_specs.append(row)
        args.append(resid)
    return pl.pallas_call(
        body,
        out_shape=(jax.ShapeDtypeStruct((t, d), F32), jax.ShapeDtypeStruct((8, d), F32)),
        grid=(t // tm,),
        in_specs=in_specs,
        out_specs=(row, pl.BlockSpec((8, d), lambda i: (0, 0))),
        compiler_params=_cparams(dimension_semantics=("arbitrary",)),
        name=name,
    )(*args)


_FLOG0 = 4 * FOX_W
_W_IN_SEGMENTS = ((0, _FLOG0, 0), (_FLOG0, _FLOG0 + FOX_HEADS, PW), (_FLOG0 + FOX_HEADS, IN_W, C_DQ))
SHARD_W = IN_W // N_CHIPS


def _rearrange_w_in(shards):
    def cols(lo, hi):
        parts = []
        for k in range(N_CHIPS):
            a, b = max(lo, k * SHARD_W), min(hi, (k + 1) * SHARD_W)
            if a < b:
                parts.append(shards[k][:, a - k * SHARD_W:b - k * SHARD_W])
        return parts

    (a0, a1, _), (f0, f1, _), (b0, b1, _) = _W_IN_SEGMENTS
    pad = jnp.zeros((shards[0].shape[0], PWF - PW - FOX_HEADS), shards[0].dtype)
    return jnp.concatenate(cols(a0, a1) + cols(b0, b1) + cols(f0, f1) + [pad], axis=1)


def _w_in_grad_slabs(g):
    slabs = []
    for k in range(N_CHIPS):
        parts = []
        for lo, hi, at in _W_IN_SEGMENTS:
            a, b = max(lo, k * SHARD_W), min(hi, (k + 1) * SHARD_W)
            if a < b:
                parts.append(g[:, at + a - lo:at + b - lo])
        slabs.append(jnp.concatenate(parts, axis=1))
    return jnp.stack(slabs, axis=0)


def _local_grads(x, mem, norm_g, w_r, b_forget, mem_norm_g, w_kv, w_o, final_norm_g, tgt, start_reduce=None,
                 start_reduce_small=None, early_token=None, late_weights=None):
    nb = x.shape[0]
    t = nb * SEQ
    x2d = x.reshape(t, D_MODEL)
    tgt2d = tgt.reshape(t, D_MODEL)
    tabs = _rope_tables()
    bpad = jnp.pad(b_forget.reshape(1, FOX_HEADS), ((0, 0), (0, LANES - FOX_HEADS)))

    gain0 = norm_g.reshape(1, D_MODEL)
    if early_token is not None:
        gain0 = gain0 + early_token[0:1, 0:1]
    h, p16, dqkv, flog = _proj(x2d, gain0, w_r, tabs, n=PWF, tm=1024, tn=768, name="proj")
    c12 = _flog_fwd(flog, bpad, nb=nb, ts=256, name="flog_fwd")

    crow = c12[:, :FOX_HEADS].reshape(nb, NBLK, BLK, FOX_HEADS // 2, 2).transpose(0, 3, 1, 4, 2)
    crow = jnp.pad(crow, ((0, 0), (0, 0), (0, 0), (0, 6), (0, 0)))
    p3 = p16.reshape(nb, SEQ, PWF)
    fox, fox_lse = _fox_fwd(p3, crow, name="fox_fwd")
    if late_weights is not None:
        w_kv, w_o = late_weights(fox_lse)

    dqkv3 = dqkv.reshape(nb, SEQ, 3 * DIL_W)
    dil, dil_lse = _dil_fwd(dqkv3, name="dil_fwd")

    mh = _rms_fwd(mem.reshape(nb * MEM_LEN, D_MODEL), mem_norm_g.reshape(1, D_MODEL), tm=nb * MEM_LEN, name="rms_mem")
    mkv = _matmul(mh, w_kv, out_dtype=BF16, tm=nb * MEM_LEN, tn=512, tk=D_MODEL, name="mem_kv")
    mkv3 = mkv.reshape(nb, MEM_LEN, 2 * MEM_W)
    memo, mem_lse = _mem_fwd(p3, mkv3, qoff=C_MQ // LANES, name="mem_fwd")

    fox2, dil2, memo2 = fox.reshape(t, FOX_W), dil.reshape(t, DIL_W), memo.reshape(t, MEM_W)
    y, dx2, dx2b, st = _out_loss(fox2, dil2, memo2, p16, w_o, x2d, tgt2d, final_norm_g.reshape(1, D_MODEL), tm=256,
                                 name="out_loss")

    g_wo = _matmul(y, dx2b, mode="tn", out_dtype=BF16, tm=1024, tn=512, tk=t, name="grad_w_out")
    datt, dgate = _dy_gate_bwd(dx2b, w_o, fox2, dil2, memo2, p16, tm=2048, tn=256, name="dy_gate_bwd")
    datt3 = datt.reshape(nb, SEQ, MIX_W)

    dmq, dmk, dmv = _mem_bwd(p3, mkv3, datt3, memo, mem_lse, qoff=C_MQ // LANES, do_off=_B2, name="mem_bwd")
    dmkv = jnp.concatenate([dmk, dmv], axis=-1).reshape(nb * MEM_LEN, 2 * MEM_W).astype(BF16)
    g_wkv = _matmul(mh, dmkv, mode="tn", out_dtype=BF16, tm=512, tn=512, tk=nb * MEM_LEN, name="grad_w_kv")
    mem_gain = mem_norm_g.reshape(1, D_MODEL)
    if start_reduce_small is not None:
        tok = start_reduce_small(g_wkv, g_wo)[0:1, 0:1]
        mem_gain, crow = mem_gain + tok, crow + tok
    _, gmn = _dh_rms_bwd(dmkv, w_kv, mem.reshape(nb * MEM_LEN, D_MODEL), mem_gain, None, tm=nb * MEM_LEN, name="mem_rms_bwd")

    dfq, dfk, dfv, dcr = _fox_bwd(p3, crow, datt3, fox, fox_lse, do_off=0, name="fox_bwd")
    dcol = -dcr[:, :, :, :2, :].transpose(0, 2, 4, 1, 3).reshape(t, FOX_HEADS)
    dcol = jnp.pad(dcol, ((0, 0), (0, LANES - FOX_HEADS)))
    dflog, gb = _flog_bwd(dcol, flog, bpad, nb=nb, ts=256, name="flog_bwd")

    ddq, ddk, ddv = _dil_bwd(dqkv3, datt3, dil, dil_lse, tabs, do_off=_B1, name="dil_bwd")

    flat = lambda a: a.reshape(t, -1)
    dp = jnp.concatenate([flat(dfq), flat(dfk), flat(dfv), dgate[:, :FOX_W], flat(ddq), flat(ddk), flat(ddv),
                          dgate[:, FOX_W:FOX_W + DIL_W], flat(dmq), dgate[:, FOX_W + DIL_W:], dflog,
                          jnp.zeros((t, PWF - PW - LANES), BF16)], axis=1)
    g_wr = _matmul(h, dp, mode="tn", out_dtype=BF16, tm=D_MODEL, tn=768, tk=t, name="grad_w_in")
    gain = norm_g.reshape(1, D_MODEL)
    if start_reduce is not None:
        gain = gain + start_reduce(g_wr)[0:1, 0:1]
    gx, gng = _dh_rms_bwd(dp, w_r, x2d, gain, dx2, tm=256, name="in_rms_bwd")

    gb_row = jnp.pad(gb[0:1, :], ((0, 0), (0, D_MODEL - LANES)))
    small = jnp.concatenate([gng[0:1], gmn[0:1], st[0:1], gb_row, st[1:2], jnp.zeros((3, D_MODEL), F32)], axis=0)
    return gx.reshape(nb, SEQ, D_MODEL), g_wr, g_wkv, g_wo, small


MESH = pl.DeviceIdType.MESH
ANY = pl.BlockSpec(memory_space=pl.ANY)


def _place():
    x, y, c = lax.axis_index("x"), lax.axis_index("y"), lax.axis_index("c")
    other_chips = [(1 - x, y), (x, 1 - y), (1 - x, 1 - y)]
    return x, y, c, other_chips


def _gather_weights(shards):
    n = len(shards)

    def body(*refs):
        in_refs, out_refs = refs[:n], refs[n:2 * n]
        send_sems, recv_sems = refs[2 * n:]
        x, y, c, chips = _place()
        me_chip = 2 * x + y
        sibling = (x, y, 1 - c)

        def half(ref, pc, rows):
            return ref.at[pl.ds(pc * (rows // 2), rows // 2), :]

        def rcopy(k, src, dst, to):
            return pltpu.make_async_remote_copy(src_ref=src, dst_ref=dst, send_sem=send_sems.at[k], recv_sem=recv_sems.at[k],
                                                device_id=to, device_id_type=MESH)

        sends = []
        for t in range(n):
            rows = shards[t].shape[0]
            for j, chip in enumerate(chips):
                cp = rcopy(6 * t + j, half(in_refs[t], c, rows), half(out_refs[t].at[me_chip], c, rows), (*chip, c))
                cp.start()
                sends.append(cp)
        for t in range(n):
            rows = shards[t].shape[0]
            for j, chip in enumerate(chips):
                slot = out_refs[t].at[2 * chip[0] + chip[1]]
                rcopy(6 * t + j, half(slot, c, rows), half(slot, c, rows), sibling).wait_recv()
                fw = rcopy(6 * t + 3 + j, half(slot, c, rows), half(slot, c, rows), sibling)
                fw.start()
                sends.append(fw)
        for t in range(n):
            rows = shards[t].shape[0]
            for j, chip in enumerate(chips):
                slot = out_refs[t].at[2 * chip[0] + chip[1]]
                rcopy(6 * t + 3 + j, half(slot, 1 - c, rows), half(slot, 1 - c, rows), sibling).wait_recv()
        for cp in sends:
            cp.wait_send()

    return pl.pallas_call(
        body,
        out_shape=tuple(jax.ShapeDtypeStruct((N_CHIPS,) + s.shape, s.dtype) for s in shards),
        in_specs=[ANY] * n,
        out_specs=tuple([ANY] * n),
        scratch_shapes=[pltpu.SemaphoreType.DMA((6 * n,)), pltpu.SemaphoreType.DMA((6 * n,))],
        name="gather_weights",
    )(*shards)


def _pair_exchange(gs, *, name):
    n = len(gs)

    def body(*refs):
        g_refs, r_refs = refs[:n], refs[n:2 * n]
        send_sems, recv_sems = refs[2 * n:]
        x, y, c, _ = _place()
        cps = []
        for t in range(n):
            hr = gs[t].shape[1] // 2
            cp = pltpu.make_async_remote_copy(src_ref=g_refs[t].at[:, pl.ds((1 - c) * hr, hr), :], dst_ref=r_refs[t],
                                              send_sem=send_sems.at[t], recv_sem=recv_sems.at[t],
                                              device_id=(x, y, 1 - c), device_id_type=MESH)
            cp.start()
            cps.append(cp)
        for cp in cps:
            cp.wait()

    return pl.pallas_call(
        body,
        out_shape=tuple(jax.ShapeDtypeStruct((g.shape[0], g.shape[1] // 2, g.shape[2]), g.dtype) for g in gs),
        in_specs=[ANY] * n,
        out_specs=tuple([ANY] * n),
        scratch_shapes=[pltpu.SemaphoreType.DMA((n,)), pltpu.SemaphoreType.DMA((n,))],
        name=name,
    )(*gs)


_HBM = pl.BlockSpec(memory_space=pltpu.HBM)
_SEM = pl.BlockSpec(memory_space=pltpu.SEMAPHORE)
_DATAFLOW = pltpu.SideEffectType.DATAFLOW_SIDE_EFFECTING


def _chip_copies(p_refs, land_refs, send_sems, recv_sems):
    x, y, c, chips = _place()
    me_chip = 2 * x + y
    return [pltpu.make_async_remote_copy(src_ref=p_refs[t].at[2 * chip[0] + chip[1]], dst_ref=land_refs[t].at[me_chip],
                                         send_sem=send_sems.at[3 * t + j], recv_sem=recv_sems.at[3 * t + j],
                                         device_id=(*chip, c), device_id_type=MESH)
            for t in range(len(p_refs)) for j, chip in enumerate(chips)]


def _chip_exchange_start(ps, *, tag):
    n = len(ps)

    def body(*refs):
        p_refs, land_refs = refs[:n], refs[n:2 * n]
        send_sems, recv_sems = refs[2 * n:2 * n + 2]
        token = refs[-1]
        for cp in _chip_copies(p_refs, land_refs, send_sems, recv_sems):
            cp.start()
        token[...] = jnp.zeros_like(token)

    hbm = [pltpu.HBM(p.shape, p.dtype) for p in ps]
    args = [pltpu.with_memory_space_constraint(p, pltpu.HBM) for p in ps]
    args += [pltpu.with_memory_space_constraint(lax.empty(p.shape, p.dtype), pltpu.HBM) for p in ps]
    out = pl.pallas_call(
        body,
        name=f"chip_exchange_start_{tag}",
        out_shape=(pltpu.SemaphoreType.DMA((3 * n,)), pltpu.SemaphoreType.DMA((3 * n,)), *hbm, *hbm,
                   jax.ShapeDtypeStruct((8, LANES), F32)),
        in_specs=[_HBM] * (2 * n),
        out_specs=(_SEM, _SEM, *([_HBM] * (2 * n)), pl.BlockSpec(memory_space=pltpu.VMEM)),
        input_output_aliases={i: 2 + i for i in range(2 * n)},
        compiler_params=pltpu.CompilerParams(has_side_effects=_DATAFLOW),
    )(*args)
    return out[0], out[1], out[2:2 + n], out[2 + n:2 + 2 * n], out[-1]


def _chip_exchange_wait(send_sems, recv_sems, p_thru, land_thru, after, *, tag):
    n = len(p_thru)

    def body(*refs):
        p_refs, land_refs = refs[:n], refs[n:2 * n]
        ssem, rsem = refs[2 * n:2 * n + 2]
        for cp in _chip_copies(p_refs, land_refs, ssem, rsem):
            cp.wait_send()
            cp.wait_recv()

    hbm = [pltpu.HBM(p.shape, p.dtype) for p in p_thru]
    out = pl.pallas_call(
        body,
        name=f"chip_exchange_wait_{tag}",
        out_shape=(*hbm, *hbm),
        in_specs=[_HBM] * (2 * n) + [_SEM, _SEM, ANY],
        out_specs=tuple([_HBM] * (2 * n)),
        input_output_aliases={i: i for i in range(2 * n)},
        compiler_params=pltpu.CompilerParams(has_side_effects=_DATAFLOW),
    )(*p_thru, *land_thru, send_sems, recv_sems, after)
    return out[:n], out[n:]


def _shard_copies(s_refs, land_refs, send_sems, recv_sems):
    x, y, c, chips = _place()
    me_chip = 2 * x + y
    return [pltpu.make_async_remote_copy(src_ref=s_refs[t], dst_ref=land_refs[t].at[me_chip],
                                         send_sem=send_sems.at[3 * t + j], recv_sem=recv_sems.at[3 * t + j],
                                         device_id=(*chip, c), device_id_type=MESH)
            for t in range(len(s_refs)) for j, chip in enumerate(chips)]


def _gather_late_start(shards):
    n = len(shards)

    def body(*refs):
        s_refs, land_refs = refs[:n], refs[n:2 * n]
        send_sems, recv_sems = refs[2 * n:2 * n + 2]
        token = refs[-1]
        for cp in _shard_copies(s_refs, land_refs, send_sems, recv_sems):
            cp.start()
        token[...] = jnp.zeros_like(token)

    lands = [(N_CHIPS,) + s.shape for s in shards]
    args = [pltpu.with_memory_space_constraint(s, pltpu.HBM) for s in shards]
    args += [pltpu.with_memory_space_constraint(lax.empty(shp, s.dtype), pltpu.HBM) for shp, s in zip(lands, shards)]
    out = pl.pallas_call(
        body,
        name="gather_late_start",
        out_shape=(pltpu.SemaphoreType.DMA((3 * n,)), pltpu.SemaphoreType.DMA((3 * n,)),
                   *[pltpu.HBM(s.shape, s.dtype) for s in shards], *[pltpu.HBM(shp, s.dtype) for shp, s in zip(lands, shards)],
                   jax.ShapeDtypeStruct((8, LANES), F32)),
        in_specs=[_HBM] * (2 * n),
        out_specs=(_SEM, _SEM, *([_HBM] * (2 * n)), pl.BlockSpec(memory_space=pltpu.VMEM)),
        input_output_aliases={i: 2 + i for i in range(2 * n)},
        compiler_params=pltpu.CompilerParams(has_side_effects=_DATAFLOW),
    )(*args)
    return out[0], out[1], out[2:2 + n], out[2 + n:2 + 2 * n], out[-1]


def _gather_late_wait(send_sems, recv_sems, s_thru, land_thru, after):
    n = len(s_thru)

    def body(*refs):
        s_refs, land_refs = refs[:n], refs[n:2 * n]
        ssem, rsem = refs[2 * n:2 * n + 2]
        for cp in _shard_copies(s_refs, land_refs, ssem, rsem):
            cp.wait_send()
            cp.wait_recv()

    out = pl.pallas_call(
        body,
        name="gather_late_wait",
        out_shape=(*[pltpu.HBM(s.shape, s.dtype) for s in s_thru], *[pltpu.HBM(l.shape, l.dtype) for l in land_thru]),
        in_specs=[_HBM] * (2 * n) + [_SEM, _SEM, ANY],
        out_specs=tuple([_HBM] * (2 * n)),
        input_output_aliases={i: i for i in range(2 * n)},
        compiler_params=pltpu.CompilerParams(has_side_effects=_DATAFLOW),
    )(*s_thru, *land_thru, send_sems, recv_sems, after)
    return out[:n], out[n:]


def _pair_swap(rs):
    n = len(rs)

    def body(*refs):
        r_refs, o_refs = refs[:n], refs[n:2 * n]
        send_sems, recv_sems = refs[2 * n:]
        x, y, c, _ = _place()
        cps = []
        for t in range(n):
            cp = pltpu.make_async_remote_copy(src_ref=r_refs[t], dst_ref=o_refs[t], send_sem=send_sems.at[t],
                                              recv_sem=recv_sems.at[t], device_id=(x, y, 1 - c), device_id_type=MESH)
            cp.start()
            cps.append(cp)
        for cp in cps:
            cp.wait()

    return pl.pallas_call(
        body,
        out_shape=tuple(jax.ShapeDtypeStruct(r.shape, r.dtype) for r in rs),
        in_specs=[ANY] * n,
        out_specs=tuple([ANY] * n),
        scratch_shapes=[pltpu.SemaphoreType.DMA((n,)), pltpu.SemaphoreType.DMA((n,))],
        name="pair_swap",
    )(*rs)


N_DEV = 8
LOSS_ROW = 4


def _small_allreduce(small):
    def body(s_ref, o_ref, all_ref, send_sems, recv_sems):
        x, y, c, _ = _place()
        me = 4 * x + 2 * y + c
        all_ref[me] = s_ref[...]
        cps = []
        for k in range(1, N_DEV):
            peer = tuple(1 - p if (k >> s) & 1 else p for p, s in ((x, 2), (y, 1), (c, 0)))
            cp = pltpu.make_async_remote_copy(src_ref=s_ref, dst_ref=all_ref.at[me], send_sem=send_sems.at[k - 1],
                                              recv_sem=recv_sems.at[k - 1], device_id=peer, device_id_type=MESH)
            cp.start()
            cps.append(cp)
        for cp in cps:
            cp.wait()
        tot = all_ref[0]
        for d in range(1, N_DEV):
            tot = tot + all_ref[d]
        o_ref[...] = tot
        o_ref[LOSS_ROW:LOSS_ROW + 1, :] = jnp.broadcast_to(jnp.sum(tot[LOSS_ROW:LOSS_ROW + 1, :], axis=1, keepdims=True),
                                                          (1, tot.shape[1]))

    vm = pl.BlockSpec(memory_space=pltpu.VMEM)
    return pl.pallas_call(
        body,
        out_shape=jax.ShapeDtypeStruct(small.shape, small.dtype),
        in_specs=[vm],
        out_specs=vm,
        scratch_shapes=[pltpu.VMEM((N_DEV,) + small.shape, small.dtype), pltpu.SemaphoreType.DMA((N_DEV - 1,)),
                        pltpu.SemaphoreType.DMA((N_DEV - 1,))],
        name="small_allreduce",
    )(small)


def _sum_pair(g, recv, cidx, *, tr, name):
    n, hr, cols = recv.shape
    nr = hr // tr

    def body(c_ref, g_ref, r_ref, o_ref):
        o_ref[...] = (g_ref[...].astype(F32) + r_ref[...].astype(F32)).astype(o_ref.dtype)

    grid_spec = pltpu.PrefetchScalarGridSpec(
        num_scalar_prefetch=1,
        grid=(n, nr),
        in_specs=[pl.BlockSpec((None, tr, cols), lambda k, i, c_ref: (k, c_ref[0] * nr + i, 0)),
                  pl.BlockSpec((None, tr, cols), lambda k, i, c_ref: (k, i, 0))],
        out_specs=pl.BlockSpec((None, tr, cols), lambda k, i, c_ref: (k, i, 0)),
    )
    return pl.pallas_call(body, out_shape=jax.ShapeDtypeStruct(recv.shape, BF16), grid_spec=grid_spec,
                          compiler_params=_cparams(), name=name)(cidx, g, recv)


def _sum_chips(p, *, tr, name):
    _, rows, cols = p.shape

    def body(p_ref, o_ref):
        tot = p_ref[0].astype(F32)
        for k in range(1, N_CHIPS):
            tot = tot + p_ref[k].astype(F32)
        o_ref[...] = tot

    return pl.pallas_call(
        body,
        out_shape=jax.ShapeDtypeStruct((rows, cols), F32),
        grid=(rows // tr,),
        in_specs=[pl.BlockSpec((N_CHIPS, tr, cols), lambda i: (0, i, 0))],
        out_specs=pl.BlockSpec((tr, cols), lambda i: (i, 0)),
        compiler_params=_cparams(),
        name=name,
    )(p)


def _adamw(w, g, m, v, *, tr, name):
    rows, cols = w.shape
    bc1 = 1.0 / (1.0 - ADAM_B1 ** ADAM_STEP)
    bc2 = 1.0 / (1.0 - ADAM_B2 ** ADAM_STEP)

    def body(w_ref, g_ref, m_ref, v_ref, d_ref, nm_ref, nv_ref):
        gv = g_ref[...]
        nm = ADAM_B1 * m_ref[...] + (1.0 - ADAM_B1) * gv
        nv = ADAM_B2 * v_ref[...] + (1.0 - ADAM_B2) * (gv * gv)
        d_ref[...] = -ADAM_LR * ((nm * bc1) / (jnp.sqrt(nv * bc2) + ADAM_EPS) + ADAM_WD * w_ref[...])
        nm_ref[...] = nm
        nv_ref[...] = nv

    spec = pl.BlockSpec((tr, cols), lambda i: (i, 0))
    sd = jax.ShapeDtypeStruct((rows, cols), F32)
    return pl.pallas_call(body, out_shape=(sd, sd, sd), grid=(rows // tr,), in_specs=[spec] * 4, out_specs=(spec,) * 3,
                          compiler_params=_cparams(), name=name)(w, g, m, v)


def _adamw_halves(w, own, sib, cidx, m, v, *, tr, name):
    rows, cols = w.shape
    hr = own.shape[0]
    nr = hr // tr
    assert rows == 2 * hr and hr % tr == 0
    bc1 = 1.0 / (1.0 - ADAM_B1 ** ADAM_STEP)
    bc2 = 1.0 / (1.0 - ADAM_B2 ** ADAM_STEP)

    def body(c_ref, w_ref, o_ref, s_ref, m_ref, v_ref, g_ref, d_ref, nm_ref, nv_ref):
        mine = (pl.program_id(0) // nr) == c_ref[0]
        gv = jnp.where(mine, o_ref[...], s_ref[...])
        nm = ADAM_B1 * m_ref[...] + (1.0 - ADAM_B1) * gv
        nv = ADAM_B2 * v_ref[...] + (1.0 - ADAM_B2) * (gv * gv)
        g_ref[...] = gv
        d_ref[...] = -ADAM_LR * ((nm * bc1) / (jnp.sqrt(nv * bc2) + ADAM_EPS) + ADAM_WD * w_ref[...])
        nm_ref[...] = nm
        nv_ref[...] = nv

    full = pl.BlockSpec((tr, cols), lambda i, c_ref: (i, 0))
    half = pl.BlockSpec((tr, cols), lambda i, c_ref: (i % nr, 0))
    sd = jax.ShapeDtypeStruct((rows, cols), F32)
    grid_spec = pltpu.PrefetchScalarGridSpec(num_scalar_prefetch=1, grid=(rows // tr,), in_specs=[full, half, half, full, full],
                                             out_specs=(full,) * 4)
    return pl.pallas_call(body, out_shape=(sd,) * 4, grid_spec=grid_spec, compiler_params=_cparams(), name=name)(
        cidx, w, own, sib, m, v)


def _pack_small(norm, mem_norm, final_norm, b_forget):
    rows = [norm.reshape(1, D_MODEL), mem_norm.reshape(1, D_MODEL), final_norm.reshape(1, D_MODEL),
            jnp.pad(b_forget.reshape(1, FOX_HEADS), ((0, 0), (0, D_MODEL - FOX_HEADS))), jnp.zeros((4, D_MODEL), F32)]
    return jnp.concatenate(rows, axis=0)


def _unpack_small(a):
    return a[0:1], a[3:4, :FOX_HEADS], a[1:2], a[2]


def kernel(x, mem, norm_g, w_in, b_forget, mem_norm_g, w_mem_kv, w_out, final_norm_g, loss_target, m_norm_g, m_w_in, m_b_forget, m_mem_norm_g, m_w_mem_kv, m_w_out, m_final_norm_g, v_norm_g, v_w_in, v_b_forget, v_mem_norm_g, v_w_mem_kv, v_w_out, v_final_norm_g):
    core = lax.axis_index("c").astype(jnp.int32)
    me_chip = (2 * lax.axis_index("x") + lax.axis_index("y")).astype(jnp.int32)
    cidx = core.reshape(1)

    def own_slot(arr, own):
        return lax.dynamic_update_slice(arr, own[None].astype(arr.dtype), (me_chip,) + (0,) * own.ndim)

    win_b, late = w_in[0].astype(BF16), [w_mem_kv[0].astype(BF16), w_out[0].astype(BF16)]
    g_in, = _gather_weights([win_b])
    g_in, late = lax.optimization_barrier((own_slot(g_in, win_b), late))
    w_r = _rearrange_w_in([g_in[k] for k in range(N_CHIPS)])
    *late_flight, early_token = _gather_late_start(late)

    def late_weights(after):
        shards, landed = _gather_late_wait(*late_flight, after)
        g_kv, g_out = (own_slot(g, s) for g, s in zip(landed, shards))
        return g_kv.reshape(D_MODEL, 2 * MEM_W), g_out.reshape(MIX_W, D_MODEL)

    trs = (128, 128, 256)
    names = ("w_in", "w_mem_kv", "w_out")
    flights = {}

    def exchange(slabs, nms, ts, tag):
        recv = _pair_exchange(slabs, name=f"pair_exchange_{tag}")
        pair = [_sum_pair(g, r, cidx, tr=tr, name=f"sum_pair_{nm}") for g, r, tr, nm in zip(slabs, recv, ts, nms)]
        if tag == "w_in":
            pair[0] = _w_in_grad_slabs(pair[0][0])
        *flights[tag], token = _chip_exchange_start(pair, tag=tag)
        return token

    def start_reduce_small(g_wkv, g_wo):
        slabs = [g_wkv.reshape(N_CHIPS, D_MODEL // N_CHIPS, 2 * MEM_W), g_wo.reshape(N_CHIPS, MIX_W // N_CHIPS, D_MODEL)]
        return exchange(slabs, names[1:], trs[1:], "small")

    def start_reduce(g_wr):
        return exchange([g_wr[None]], names[:1], trs[:1], "w_in")

    gx, g_wr, g_wkv, g_wo, small = _local_grads(x, mem, norm_g, w_r, b_forget, mem_norm_g, None, None, final_norm_g, loss_target,
                                                start_reduce=start_reduce, start_reduce_small=start_reduce_small,
                                                early_token=early_token, late_weights=late_weights)

    pair, landed = [], []
    for tag in ("w_in", "small"):
        p, l = _chip_exchange_wait(*flights[tag], small, tag=tag)
        pair += list(p)
        landed += list(l)
    got = [lax.dynamic_update_slice(g, lax.dynamic_slice(p, (me_chip, 0, 0), (1,) + p.shape[1:]), (me_chip, 0, 0))
           for g, p in zip(landed, pair)]
    red = [_sum_chips(p, tr=tr, name=f"sum_chips_{nm}") for p, tr, nm in zip(got, trs, names)]
    sib = _pair_swap(red)

    outs = {}
    for nm, r, s, w, m, v, tr in zip(names, red, sib, (w_in, w_mem_kv, w_out), (m_w_in, m_w_mem_kv, m_w_out),
                                     (v_w_in, v_w_mem_kv, v_w_out), trs):
        outs[nm] = tuple(a[None] for a in _adamw_halves(w[0], r, s, cidx, m[0], v[0], tr=tr, name=f"adamw_{nm}"))

    gsum = _small_allreduce(small)
    sd, sm, sv = _adamw(_pack_small(norm_g, mem_norm_g, final_norm_g, b_forget), gsum,
                        _pack_small(m_norm_g, m_mem_norm_g, m_final_norm_g, m_b_forget),
                        _pack_small(v_norm_g, v_mem_norm_g, v_final_norm_g, v_b_forget), tr=8, name="adamw_small")
    loss = gsum[LOSS_ROW, 0]

    def group(i, small_arr):
        ng, bf, mg, fg = _unpack_small(small_arr)
        return (ng, outs["w_in"][i], bf, mg, outs["w_mem_kv"][i], outs["w_out"][i], fg)

    return (loss, gx, *group(0, gsum), *group(1, sd), *group(2, sm), *group(3, sv))
```

```python
import functools
import math

import jax
import jax.numpy as jnp
from jax import lax
from jax.experimental import pallas as pl
from jax.experimental.pallas import tpu as pltpu

F32 = jnp.float32
BF16 = jnp.bfloat16

D_MODEL = 1024
SEQ = 2048
HEAD_DIM = 64
FOX_HEADS = 12
DIL_HEADS = 12
MEM_HEADS = 4
MEM_HEAD_DIM = 128
MEM_LEN = 256
FOX_W = FOX_HEADS * HEAD_DIM
DIL_W = DIL_HEADS * HEAD_DIM
MEM_W = MEM_HEADS * MEM_HEAD_DIM
MIX_W = FOX_W + DIL_W + MEM_W
DILATIONS = ((128, 1), (512, 4), (2048, 16))
ROPE_THETA = 500000.0
ROPE_DIM = HEAD_DIM // 4
RMS_EPS = 1e-6
NEG_INF = -1e30
IN_SIZES = [FOX_W] * 4 + [FOX_HEADS] + [DIL_W] * 4 + [MEM_W] * 2
IN_W = sum(IN_SIZES)

ADAM_LR = 0.001
ADAM_B1 = 0.9
ADAM_B2 = 0.999
ADAM_EPS = 1e-08
ADAM_WD = 0.01
ADAM_STEP = 10

LANES = 128
N_CHIPS = 4
PW = 7168
PWF = PW + 4 * LANES
C_FQ, C_FK, C_FV, C_FG = 0, 768, 1536, 2304
C_DQ, C_DK, C_DV, C_DG = 3072, 3840, 4608, 5376
C_MQ, C_MG = 6144, 6656
VMEM_LIMIT = 48 * 1024 * 1024


def _cparams(**kw):
    return pltpu.CompilerParams(vmem_limit_bytes=VMEM_LIMIT, **kw)


MM_CHUNK = 256


def _matmul(a, b, *, out_dtype, tm, tn, tk, name, mode="nn"):
    if mode == "tn":
        (kdim, m), n = a.shape, b.shape[1]
        a_spec = pl.BlockSpec((tk, tm), lambda i, j, k: (k, i))
        b_spec = pl.BlockSpec((tk, tn), lambda i, j, k: (k, j))
        dims = _T0
    elif mode == "nt":
        (m, kdim), n = a.shape, b.shape[0]
        a_spec = pl.BlockSpec((tm, tk), lambda i, j, k: (i, k))
        b_spec = pl.BlockSpec((tn, tk), lambda i, j, k: (j, k))
        dims = _NT
    else:
        (m, kdim), n = a.shape, b.shape[1]
        a_spec = pl.BlockSpec((tm, tk), lambda i, j, k: (i, k))
        b_spec = pl.BlockSpec((tk, tn), lambda i, j, k: (k, j))
        dims = (((1,), (0,)), ((), ()))
    nk = kdim // tk
    assert m % tm == 0 and n % tn == 0 and kdim % tk == 0

    def body(a_ref, b_ref, o_ref, *scratch):
        if nk == 1:
            bv = b_ref[...]
            for c0 in range(0, tm, min(tm, MM_CHUNK)):
                rows = pl.ds(c0, min(tm, MM_CHUNK))
                av = a_ref[:, rows] if mode == "tn" else a_ref[rows, :]
                o_ref[rows, :] = lax.dot_general(av, bv, dims, preferred_element_type=F32).astype(o_ref.dtype)
            return
        prod = lax.dot_general(a_ref[...], b_ref[...], dims, preferred_element_type=F32)
        acc_ref, = scratch
        k = pl.program_id(2)

        @pl.when(k == 0)
        def _():
            acc_ref[...] = prod

        @pl.when(k > 0)
        def _():
            acc_ref[...] += prod

        @pl.when(k == nk - 1)
        def _():
            o_ref[...] = acc_ref[...].astype(o_ref.dtype)

    return pl.pallas_call(
        body,
        out_shape=jax.ShapeDtypeStruct((m, n), out_dtype),
        grid=(m // tm, n // tn, nk),
        in_specs=[a_spec, b_spec],
        out_specs=pl.BlockSpec((tm, tn), lambda i, j, k: (i, j)),
        scratch_shapes=[pltpu.VMEM((tm, tn), F32)] if nk > 1 else [],
        compiler_params=_cparams(dimension_semantics=("parallel", "parallel", "arbitrary")),
        name=name,
    )(a, b)


def _rms_fwd(x, g, *, tm, name):
    t, d = x.shape

    def body(x_ref, g_ref, h_ref):
        xv = x_ref[...]
        r = lax.rsqrt(jnp.mean(xv * xv, axis=-1, keepdims=True) + RMS_EPS)
        h_ref[...] = (xv * r * g_ref[...]).astype(h_ref.dtype)

    return pl.pallas_call(
        body,
        out_shape=jax.ShapeDtypeStruct((t, d), BF16),
        grid=(t // tm,),
        in_specs=[pl.BlockSpec((tm, d), lambda i: (i, 0)), pl.BlockSpec((1, d), lambda i: (0, 0))],
        out_specs=pl.BlockSpec((tm, d), lambda i: (i, 0)),
        compiler_params=_cparams(),
        name=name,
    )(x, g)


def _rope_tables():
    half = ROPE_DIM // 2
    pos = jnp.arange(SEQ, dtype=F32)
    inv_freq = 1.0 / (ROPE_THETA ** (jnp.arange(0, ROPE_DIM, 2, dtype=F32) / ROPE_DIM))
    ang = pos[:, None] * inv_freq[None, :]
    cos, sin = jnp.cos(ang), jnp.sin(ang)
    one = jnp.ones((SEQ, HEAD_DIM - ROPE_DIM), F32)
    zero = jnp.zeros((SEQ, HEAD_DIM - ROPE_DIM), F32)
    zh = jnp.zeros((SEQ, half), F32)
    c = jnp.concatenate([cos, cos, one], axis=1)
    s1 = jnp.concatenate([zh, sin, zero], axis=1)
    s2 = jnp.concatenate([-sin, zh, zero], axis=1)
    rep = LANES // HEAD_DIM
    return jnp.tile(c, (1, rep)), jnp.tile(s1, (1, rep)), jnp.tile(s2, (1, rep))


def _rope_apply(t, c, s1, s2, transpose=False):
    n = t.shape[-1]
    rep = n // LANES
    c, s1, s2 = (jnp.tile(u, (1, rep)) for u in (c, s1, s2))
    half = ROPE_DIM // 2
    if not transpose:
        return t * c + pltpu.roll(t, half, 1) * s1 + pltpu.roll(t, n - half, 1) * s2
    return t * c + pltpu.roll(t * s1, n - half, 1) + pltpu.roll(t * s2, half, 1)


PROJ_CHUNK = 256


def _proj(x, g, w, tabs, *, n, tm, tn, name):
    t, d = x.shape
    assert C_DQ % tn == 0 and (C_DV - C_DQ) % tn == 0 and (C_DG - C_DQ) % tn == 0
    rope_lo, rope_hi, dil_hi = C_DQ // tn, C_DV // tn, C_DG // tn
    flog_blk, flog_at = PW // tn, PW % tn
    assert flog_at % LANES == 0 and flog_at + LANES <= tn
    s_blocks = SEQ // tm

    def body(x_ref, g_ref, w_ref, c_ref, s1_ref, s2_ref, h_ref, o_ref, f_ref, fl_ref, h_scr):
        j = pl.program_id(1)

        @pl.when(j == 0)
        def _():
            xv = x_ref[...]
            r = lax.rsqrt(jnp.mean(xv * xv, axis=-1, keepdims=True) + RMS_EPS)
            hv = (xv * r * g_ref[...]).astype(BF16)
            h_scr[...] = hv
            h_ref[...] = hv

        def tile(kind):
            wv = w_ref[...]
            for c0 in range(0, tm, PROJ_CHUNK):
                rows = pl.ds(c0, PROJ_CHUNK)
                acc = jnp.dot(h_scr[rows, :], wv, preferred_element_type=F32)
                if kind == "rope":
                    acc = _rope_apply(acc, c_ref[rows, :], s1_ref[rows, :], s2_ref[rows, :])
                o_ref[rows, :] = acc.astype(o_ref.dtype)
                if kind in ("rope", "dv"):
                    f_ref[rows, :] = acc
                if kind == "flog":
                    fl_ref[rows, :] = acc[:, flog_at:flog_at + LANES]

        is_rope = jnp.logical_and(j >= rope_lo, j < rope_hi)
        is_dv = jnp.logical_and(j >= rope_hi, j < dil_hi)
        is_flog = j == flog_blk
        pl.when(is_rope)(functools.partial(tile, "rope"))
        pl.when(is_dv)(functools.partial(tile, "dv"))
        pl.when(is_flog)(functools.partial(tile, "flog"))
        pl.when(jnp.logical_not(jnp.logical_or(jnp.logical_or(is_rope, is_dv), is_flog)))(functools.partial(tile, "plain"))

    tab_spec = pl.BlockSpec((tm, LANES), lambda i, j: (i % s_blocks, 0))
    f_spec = pl.BlockSpec((tm, tn), lambda i, j: (i, jnp.clip(j - rope_lo, 0, dil_hi - rope_lo - 1)))
    row = pl.BlockSpec((tm, d), lambda i, j: (i, 0))
    return pl.pallas_call(
        body,
        out_shape=(jax.ShapeDtypeStruct((t, d), BF16), jax.ShapeDtypeStruct((t, n), BF16),
                   jax.ShapeDtypeStruct((t, 3 * DIL_W), F32), jax.ShapeDtypeStruct((t, LANES), F32)),
        grid=(t // tm, n // tn),
        in_specs=[row, pl.BlockSpec((1, d), lambda i, j: (0, 0)), pl.BlockSpec((d, tn), lambda i, j: (0, j)),
                  tab_spec, tab_spec, tab_spec],
        out_specs=(row, pl.BlockSpec((tm, tn), lambda i, j: (i, j)), f_spec, pl.BlockSpec((tm, LANES), lambda i, j: (i, 0))),
        scratch_shapes=[pltpu.VMEM((tm, d), BF16)],
        compiler_params=_cparams(dimension_semantics=("parallel", "arbitrary")),
        name=name,
    )(x, g, w, *tabs)


def _split3(x):
    hi = x.astype(BF16)
    r1 = x - hi.astype(F32)
    mid = r1.astype(BF16)
    lo = (r1 - mid.astype(F32)).astype(BF16)
    return hi, mid, lo


def _dot3(sel, x, sel_is_lhs):
    out = None
    for piece in _split3(x):
        t = jnp.dot(sel, piece, preferred_element_type=F32) if sel_is_lhs else jnp.dot(piece, sel, preferred_element_type=F32)
        out = t if out is None else out + t
    return out


def _flog_fwd(flog, bpad, *, nb, ts, name):
    ns = SEQ // ts

    def body(f_ref, b_ref, c_ref, carry_ref):
        s = pl.program_id(1)

        @pl.when(s == 0)
        def _():
            carry_ref[...] = jnp.zeros_like(carry_ref)

        z = f_ref[...] + b_ref[...]
        logf = jnp.minimum(z, 0.0) - jnp.log(1.0 + jnp.exp(-jnp.abs(z)))
        r = lax.broadcasted_iota(jnp.int32, (ts, ts), 0)
        c = lax.broadcasted_iota(jnp.int32, (ts, ts), 1)
        tri = jnp.where(r >= c, 1.0, 0.0).astype(BF16)
        cs = _dot3(tri, logf, True) + carry_ref[0:1, :]
        carry_ref[...] = jnp.broadcast_to(cs[ts - 1:ts, :], carry_ref.shape)
        c_ref[...] = cs

    return pl.pallas_call(
        body,
        out_shape=jax.ShapeDtypeStruct((nb * SEQ, LANES), F32),
        grid=(nb, ns),
        in_specs=[pl.BlockSpec((ts, LANES), lambda b, s: (b * ns + s, 0)), pl.BlockSpec((1, LANES), lambda b, s: (0, 0))],
        out_specs=pl.BlockSpec((ts, LANES), lambda b, s: (b * ns + s, 0)),
        scratch_shapes=[pltpu.VMEM((8, LANES), F32)],
        compiler_params=_cparams(dimension_semantics=("parallel", "arbitrary")),
        name=name,
    )(flog, bpad)


def _flog_bwd(dcol, flog, bpad, *, nb, ts, name):
    ns = SEQ // ts

    def body(d_ref, f_ref, b_ref, o_ref, gb_ref, carry_ref):
        bi = pl.program_id(0)
        s = pl.program_id(1)

        @pl.when(s == 0)
        def _():
            carry_ref[...] = jnp.zeros_like(carry_ref)

        @pl.when(jnp.logical_and(bi == 0, s == 0))
        def _():
            gb_ref[...] = jnp.zeros_like(gb_ref)

        r = lax.broadcasted_iota(jnp.int32, (ts, ts), 0)
        c = lax.broadcasted_iota(jnp.int32, (ts, ts), 1)
        tri = jnp.where(r <= c, 1.0, 0.0).astype(BF16)
        rc = _dot3(tri, d_ref[...], True) + carry_ref[0:1, :]
        carry_ref[...] = jnp.broadcast_to(rc[0:1, :], carry_ref.shape)
        z = f_ref[...] + b_ref[...]
        dz = rc / (1.0 + jnp.exp(z))
        o_ref[...] = dz.astype(o_ref.dtype)
        gb_ref[...] += jnp.broadcast_to(jnp.sum(dz, axis=0, keepdims=True), gb_ref.shape)

    rev = lambda b, s: (b * ns + (ns - 1 - s), 0)
    return pl.pallas_call(
        body,
        out_shape=(jax.ShapeDtypeStruct((nb * SEQ, LANES), BF16), jax.ShapeDtypeStruct((8, LANES), F32)),
        grid=(nb, ns),
        in_specs=[pl.BlockSpec((ts, LANES), rev), pl.BlockSpec((ts, LANES), rev), pl.BlockSpec((1, LANES), lambda b, s: (0, 0))],
        out_specs=(pl.BlockSpec((ts, LANES), rev), pl.BlockSpec((8, LANES), lambda b, s: (0, 0))),
        scratch_shapes=[pltpu.VMEM((8, LANES), F32)],
        compiler_params=_cparams(dimension_semantics=("arbitrary", "arbitrary")),
        name=name,
    )(dcol, flog, bpad)


MEM_TQ = 256
MEM_SET = 4
MEM_SCALE = 1.0 / math.sqrt(MEM_HEAD_DIM)
assert MEM_HEAD_DIM == LANES and SEQ % (MEM_TQ * MEM_SET) == 0


def _head_masks(nh):
    lane = lax.broadcasted_iota(jnp.int32, (1, LANES), 1)
    return [None] if nh == 1 else [lane < HEAD_DIM, lane >= HEAD_DIM]


def _mem_specs(qoff):
    qspec = pl.BlockSpec((None, SEQ, LANES), lambda b, j: (b, 0, qoff + j))
    kspec = pl.BlockSpec((None, MEM_LEN, LANES), lambda b, j: (b, 0, j))
    vspec = pl.BlockSpec((None, MEM_LEN, LANES), lambda b, j: (b, 0, MEM_HEADS + j))
    ospec = pl.BlockSpec((None, SEQ, LANES), lambda b, j: (b, 0, j))
    return qspec, kspec, vspec, ospec


def _mem_rows(g):
    return [pl.ds(pl.multiple_of((MEM_SET * g + a) * MEM_TQ, MEM_TQ), MEM_TQ) for a in range(MEM_SET)]


def _mem_fwd(p3, mkv3, *, qoff, name):
    nb = p3.shape[0]

    def body(q_ref, k_ref, v_ref, o_ref, lse_ref):
        kb, vb = k_ref[...], v_ref[...]

        def qset(g, c):
            rows = _mem_rows(g)
            ss = [lax.dot_general(q_ref[r, :] * MEM_SCALE, kb, _NT, preferred_element_type=F32) for r in rows]
            for r, s in zip(rows, ss):
                m = jnp.max(s, axis=1, keepdims=True)
                p = jnp.exp(s - m)
                l = jnp.sum(p, axis=1, keepdims=True)
                o_ref[r, :] = jnp.dot(p.astype(BF16), vb, preferred_element_type=F32) / l
                lse_ref[r, :] = jnp.broadcast_to(m + jnp.log(l), (MEM_TQ, LANES))
            return c

        lax.fori_loop(0, SEQ // MEM_TQ // MEM_SET, qset, 0)

    qspec, kspec, vspec, ospec = _mem_specs(qoff)
    osd = jax.ShapeDtypeStruct((nb, SEQ, MEM_W), F32)
    return pl.pallas_call(body, out_shape=(osd, osd), grid=(nb, MEM_HEADS), in_specs=[qspec, kspec, vspec],
                          out_specs=(ospec, ospec), compiler_params=_cparams(dimension_semantics=("parallel", "parallel")),
                          name=name)(p3, mkv3, mkv3)


def _mem_bwd(p3, mkv3, do, o, lse, *, qoff, do_off, name):
    nb = p3.shape[0]

    def body(q_ref, k_ref, v_ref, do_ref, o_ref, lse_ref, dq_ref, dk_ref, dv_ref):
        kb, vb = k_ref[...], v_ref[...]
        ks = kb * MEM_SCALE

        def qset(g, carry):
            dk, dv = carry
            work = []
            for r in _mem_rows(g):
                qs = q_ref[r, :] * MEM_SCALE
                dob = do_ref[r, :].astype(BF16)
                s = lax.dot_general(qs, kb, _NT, preferred_element_type=F32)
                dp = lax.dot_general(dob, vb, _NT, preferred_element_type=F32)
                work.append((r, qs, dob, s, dp))
            for r, qs, dob, s, dp in work:
                delta = jnp.sum(dob.astype(F32) * o_ref[r, :], axis=1, keepdims=True)
                p = jnp.exp(s - lse_ref[r, :][:, 0:1])
                ds = (p * (dp - delta)).astype(BF16)
                dq_ref[r, :] = jnp.dot(ds, ks, preferred_element_type=F32).astype(dq_ref.dtype)
                dk = dk + lax.dot_general(ds, qs, _T0, preferred_element_type=F32)
                dv = dv + lax.dot_general(p.astype(BF16), dob, _T0, preferred_element_type=F32)
            return dk, dv

        z = jnp.zeros((MEM_LEN, LANES), F32)
        dk, dv = lax.fori_loop(0, SEQ // MEM_TQ // MEM_SET, qset, (z, z))
        dk_ref[...] = dk
        dv_ref[...] = dv

    qspec, kspec, vspec, ospec = _mem_specs(qoff)
    dospec = pl.BlockSpec((None, SEQ, LANES), lambda b, j: (b, 0, do_off + j))
    kvo = pl.BlockSpec((None, MEM_LEN, LANES), lambda b, j: (b, 0, j))
    kvsd = jax.ShapeDtypeStruct((nb, MEM_LEN, MEM_W), F32)
    return pl.pallas_call(
        body, out_shape=(jax.ShapeDtypeStruct((nb, SEQ, MEM_W), BF16), kvsd, kvsd), grid=(nb, MEM_HEADS),
        in_specs=[qspec, kspec, vspec, dospec, ospec, ospec], out_specs=(ospec, kvo, kvo),
        compiler_params=_cparams(dimension_semantics=("parallel", "parallel")), name=name)(p3, mkv3, mkv3, do, o, lse)


BLK = 128
NBLK = SEQ // BLK
QK_SCALE = 1.0 / math.sqrt(HEAD_DIM)
DIL_STEPS = tuple(d for _, d in DILATIONS)
assert all(w // d == BLK for w, d in DILATIONS)
_T0 = (((0,), (0,)), ((), ()))
_NT = (((1,), (1,)), ((), ()))


def _stack_heads(a, masks):
    z = jnp.zeros_like(a)
    return jnp.concatenate([jnp.where(masks[0], a, z), jnp.where(masks[1], a, z)], axis=0)


def _tri_bias(lower):
    r = lax.broadcasted_iota(jnp.int32, (BLK, BLK), 0)
    c = lax.broadcasted_iota(jnp.int32, (BLK, BLK), 1)
    return jnp.where((c <= r) if lower else (c >= r), 0.0, NEG_INF).astype(F32)


def _dil_rows(r, i, d):
    start = r + i * (BLK * d)
    return pl.ds(start, BLK) if d == 1 else pl.ds(start, BLK, stride=d)


DIL_SET = 4


def _dil_sets(d, fn):
    nbk = SEQ // d // BLK
    if d == 1:
        n = 2 * DIL_SET
        def gbody(g, c):
            fn([(0, n * g + a, None if a == 0 else True) for a in range(n)])
            return c
        lax.fori_loop(0, nbk // n, gbody, 0)
    elif nbk > 1:
        assert nbk == DIL_SET
        def rbody(r, c):
            fn([(r, i, i > 0) for i in range(nbk)])
            return c
        lax.fori_loop(0, d, rbody, 0)
    else:
        def rbody(rr, c):
            fn([(DIL_SET * rr + a, 0, False) for a in range(DIL_SET)])
            return c
        lax.fori_loop(0, d // DIL_SET, rbody, 0)


def _dil_key_tiles(r, i, d, has_prev, qrows, tri_cur, tri_prev):
    tiles = [(qrows, tri_cur)]
    if has_prev is None:
        tiles.append((_dil_rows(r, jnp.maximum(i - 1, 0), d), tri_prev + jnp.where(i > 0, 0.0, NEG_INF)))
    elif has_prev:
        tiles.append((_dil_rows(r, i - 1, d), tri_prev))
    return tiles


def _dil_fwd(qkv, *, name):
    nb = qkv.shape[0]
    ncol = DIL_W // LANES
    hd = HEAD_DIM

    def body(q_ref, k_ref, v_ref, o_ref, lse_ref, m_ref, l_ref, a_ref):
        masks = _head_masks(2)
        tri_cur, tri_prev = _tri_bias(True), _tri_bias(False)
        for pi, d in enumerate(DIL_STEPS):
            first, last = pi == 0, pi == len(DIL_STEPS) - 1

            def qset(blocks, d=d, first=first, last=last):
                work = []
                for r, i, has_prev in blocks:
                    qrows = _dil_rows(r, i, d)
                    qcat = _stack_heads((q_ref[qrows, :] * QK_SCALE).astype(BF16), masks)
                    ss, krs = [], []
                    if has_prev is True:
                        start = r + (i - 1) * (BLK * d)
                        both = pl.ds(start, 2 * BLK) if d == 1 else pl.ds(start, 2 * BLK, stride=d)
                        s = lax.dot_general(qcat, k_ref[both, :].astype(BF16), _NT, preferred_element_type=F32)
                        ss = [(s[:BLK, BLK:] + tri_cur, s[BLK:, BLK:] + tri_cur), (s[:BLK, :BLK] + tri_prev, s[BLK:, :BLK] + tri_prev)]
                        krs = [qrows, _dil_rows(r, i - 1, d)]
                    else:
                        for krows, bias in _dil_key_tiles(r, i, d, has_prev, qrows, tri_cur, tri_prev):
                            s = lax.dot_general(qcat, k_ref[krows, :].astype(BF16), _NT, preferred_element_type=F32)
                            ss.append((s[:BLK] + bias, s[BLK:] + bias))
                            krs.append(krows)
                    work.append((qrows, ss, krs))
                for qrows, ss, krs in work:
                    e0 = ss[0][0] if len(ss) == 1 else jnp.maximum(ss[0][0], ss[1][0])
                    e1 = ss[0][1] if len(ss) == 1 else jnp.maximum(ss[0][1], ss[1][1])
                    n0 = jnp.max(e0, axis=1, keepdims=True)
                    n1 = jnp.max(e1, axis=1, keepdims=True)
                    if not first:
                        mo, lo = m_ref[qrows, :], l_ref[qrows, :]
                        m0, m1 = mo[:, 0:1], mo[:, hd:hd + 1]
                        n0, n1 = jnp.maximum(n0, m0), jnp.maximum(n1, m1)
                        a0, a1 = jnp.exp(m0 - n0), jnp.exp(m1 - n1)
                    ps = [(jnp.exp(s0 - n0), jnp.exp(s1 - n1)) for s0, s1 in ss]
                    t0 = ps[0][0] if len(ps) == 1 else ps[0][0] + ps[1][0]
                    t1 = ps[0][1] if len(ps) == 1 else ps[0][1] + ps[1][1]
                    l0 = jnp.sum(t0, axis=1, keepdims=True)
                    l1 = jnp.sum(t1, axis=1, keepdims=True)
                    acc = None
                    for (p0, p1), krows in zip(ps, krs):
                        vcat = _stack_heads(v_ref[krows, :].astype(BF16), masks)
                        pv = jnp.dot(jnp.concatenate([p0, p1], axis=1).astype(BF16), vcat, preferred_element_type=F32)
                        acc = pv if acc is None else acc + pv
                    if not first:
                        l0 = l0 + a0 * lo[:, 0:1]
                        l1 = l1 + a1 * lo[:, hd:hd + 1]
                        acc = acc + a_ref[qrows, :] * jnp.where(masks[0], a0, a1)
                    if last:
                        o_ref[qrows, :] = acc / jnp.where(masks[0], l0, l1)
                        lse_ref[qrows, :] = jnp.where(masks[0], n0 + jnp.log(l0), n1 + jnp.log(l1))
                    else:
                        m_ref[qrows, :] = jnp.where(masks[0], n0, n1)
                        l_ref[qrows, :] = jnp.where(masks[0], l0, l1)
                        a_ref[qrows, :] = acc

            _dil_sets(d, qset)

    spec = lambda off: pl.BlockSpec((None, SEQ, LANES), lambda b, j: (b, 0, off + j))
    ospec = pl.BlockSpec((None, SEQ, LANES), lambda b, j: (b, 0, j))
    osd = jax.ShapeDtypeStruct((nb, SEQ, DIL_W), F32)
    return pl.pallas_call(
        body, out_shape=(osd, osd), grid=(nb, ncol),
        in_specs=[spec(0), spec(ncol), spec(2 * ncol)], out_specs=(ospec, ospec),
        scratch_shapes=[pltpu.VMEM((SEQ, LANES), F32)] * 3,
        compiler_params=_cparams(dimension_semantics=("parallel", "parallel")), name=name,
    )(qkv, qkv, qkv)


def _dil_bwd(qkv, do, o, lse, tabs, *, do_off, name):
    nb = qkv.shape[0]
    ncol = DIL_W // LANES
    hd = HEAD_DIM

    def body(q_ref, k_ref, v_ref, do_ref, o_ref, lse_ref, c_ref, s1_ref, s2_ref, dqo_ref, dko_ref, dvo_ref,
             dq_ref, dk_ref, dv_ref, dl_ref, dof_ref):
        masks = _head_masks(2)
        tri_cur, tri_prev = _tri_bias(True), _tri_bias(False)
        dq_ref[...] = jnp.zeros_like(dq_ref)
        dk_ref[...] = jnp.zeros_like(dk_ref)
        dv_ref[...] = jnp.zeros_like(dv_ref)

        def delta_body(i, c):
            rows = pl.ds(pl.multiple_of(i * BLK, BLK), BLK)
            dof = do_ref[rows, :].astype(F32)
            dof_ref[rows, :] = dof
            prod = dof * o_ref[rows, :]
            z = jnp.zeros_like(prod)
            dl_ref[rows, :] = jnp.where(masks[0], jnp.sum(jnp.where(masks[0], prod, z), axis=1, keepdims=True),
                                        jnp.sum(jnp.where(masks[1], prod, z), axis=1, keepdims=True))
            return c

        lax.fori_loop(0, NBLK, delta_body, 0)

        for d in DIL_STEPS:
            def qset(blocks, d=d):
                work = []
                for r, i, has_prev in blocks:
                    qrows = _dil_rows(r, i, d)
                    qcat = _stack_heads((q_ref[qrows, :] * QK_SCALE).astype(BF16), masks)
                    docat = _stack_heads(dof_ref[qrows, :].astype(BF16), masks)
                    tiles = []
                    for krows, bias in _dil_key_tiles(r, i, d, has_prev, qrows, tri_cur, tri_prev):
                        s = lax.dot_general(qcat, k_ref[krows, :].astype(BF16), _NT, preferred_element_type=F32)
                        dp = lax.dot_general(docat, v_ref[krows, :].astype(BF16), _NT, preferred_element_type=F32)
                        tiles.append((krows, s, dp, bias))
                    work.append((qrows, qcat, docat, tiles))
                for qrows, qcat, docat, tiles in work:
                    lseb, dlb = lse_ref[qrows, :], dl_ref[qrows, :]
                    lse0, lse1 = lseb[:, 0:1], lseb[:, hd:hd + 1]
                    dl0, dl1 = dlb[:, 0:1], dlb[:, hd:hd + 1]
                    dq = None
                    for krows, s, dp, bias in tiles:
                        p0 = jnp.exp(s[:BLK] + bias - lse0)
                        p1 = jnp.exp(s[BLK:] + bias - lse1)
                        ds0 = p0 * (dp[:BLK] - dl0)
                        ds1 = p1 * (dp[BLK:] - dl1)
                        ds0b, ds1b = ds0.astype(BF16), ds1.astype(BF16)
                        pcat = jnp.concatenate([p0.astype(BF16), p1.astype(BF16)], axis=0)
                        dscat = jnp.concatenate([ds0b, ds1b], axis=0)
                        dv_ref[krows, :] += lax.dot_general(pcat, docat, _T0, preferred_element_type=F32)
                        dk_ref[krows, :] += lax.dot_general(dscat, qcat, _T0, preferred_element_type=F32)
                        dsrow = jnp.concatenate([ds0b, ds1b], axis=1)
                        kcat = _stack_heads((k_ref[krows, :] * QK_SCALE).astype(BF16), masks)
                        t = jnp.dot(dsrow, kcat, preferred_element_type=F32)
                        dq = t if dq is None else dq + t
                    dq_ref[qrows, :] += dq

            _dil_sets(d, qset)

        def out_body(i, c):
            rows = pl.ds(pl.multiple_of(i * BLK, BLK), BLK)
            tab = (c_ref[rows, :], s1_ref[rows, :], s2_ref[rows, :])
            dqo_ref[rows, :] = _rope_apply(dq_ref[rows, :], *tab, transpose=True).astype(dqo_ref.dtype)
            dko_ref[rows, :] = _rope_apply(dk_ref[rows, :], *tab, transpose=True).astype(dko_ref.dtype)
            dvo_ref[rows, :] = dv_ref[rows, :].astype(dvo_ref.dtype)
            return c

        lax.fori_loop(0, NBLK, out_body, 0)

    spec = lambda off: pl.BlockSpec((None, SEQ, LANES), lambda b, j: (b, 0, off + j))
    ospec = pl.BlockSpec((None, SEQ, LANES), lambda b, j: (b, 0, j))
    tspec = pl.BlockSpec((SEQ, LANES), lambda b, j: (0, 0))
    osd = jax.ShapeDtypeStruct((nb, SEQ, DIL_W), BF16)
    return pl.pallas_call(
        body, out_shape=(osd, osd, osd), grid=(nb, ncol),
        in_specs=[spec(0), spec(ncol), spec(2 * ncol), spec(do_off), ospec, ospec, tspec, tspec, tspec],
        out_specs=(ospec, ospec, ospec),
        scratch_shapes=[pltpu.VMEM((SEQ, LANES), F32)] * 5,
        compiler_params=_cparams(dimension_semantics=("parallel", "parallel")), name=name,
    )(qkv, qkv, qkv, do, o, lse, *tabs)


FOX_FWD_GROUP = 16
FOX_BWD_GROUP = 16
assert NBLK % FOX_FWD_GROUP == 0 and NBLK % FOX_BWD_GROUP == 0
_FOX_COLS = tuple(c // LANES for c in (C_FQ, C_FK, C_FV))


def _fox_specs():
    cols = [pl.BlockSpec((None, SEQ, LANES), (lambda b, j, off=off: (b, 0, off + j))) for off in _FOX_COLS]
    ospec = pl.BlockSpec((None, SEQ, LANES), lambda b, j: (b, 0, j))
    crspec = pl.BlockSpec((None, None, NBLK, 8, BLK), lambda b, j: (b, j, 0, 0, 0))
    return cols, ospec, crspec


def _fox_key_rows(t, e, g):
    return pl.ds(pl.multiple_of((g * t + e) * BLK, BLK), BLK)


def _fox_fwd(p3, crow, *, name):
    nb = p3.shape[0]
    g = FOX_FWD_GROUP

    def body(q_ref, k_ref, v_ref, cr_ref, o_ref, lse_ref):
        masks = _head_masks(2)
        tri = _tri_bias(True)

        def qk(qcat, t, nblk=g):
            out = []
            for e in range(0, nblk, 2):
                n = min(2, nblk - e)
                krows = pl.ds(pl.multiple_of((g * t + e) * BLK, BLK), n * BLK)
                s = lax.dot_general(qcat, k_ref[krows, :], _NT, preferred_element_type=F32)
                out += [s[:, h * BLK:(h + 1) * BLK] for h in range(n)]
            return tuple(out)

        def consume(ss, t, state, nblk, diag):
            m0, m1, l0, l1, acc = state
            us = []
            for e in range(nblk):
                cr = cr_ref[g * t + e]
                u0 = ss[e][:BLK] - cr[0:1, :]
                u1 = ss[e][BLK:] - cr[1:2, :]
                if diag and e == nblk - 1:
                    u0, u1 = u0 + tri, u1 + tri
                us.append((u0, u1))
            x0 = functools.reduce(jnp.maximum, [u[0] for u in us])
            x1 = functools.reduce(jnp.maximum, [u[1] for u in us])
            n0 = jnp.maximum(m0, jnp.max(x0, axis=1, keepdims=True))
            n1 = jnp.maximum(m1, jnp.max(x1, axis=1, keepdims=True))
            a0, a1 = jnp.exp(m0 - n0), jnp.exp(m1 - n1)
            acc = acc * jnp.where(masks[0], a0, a1)
            t0 = t1 = None
            for e in range(nblk):
                p0, p1 = jnp.exp(us[e][0] - n0), jnp.exp(us[e][1] - n1)
                t0 = p0 if t0 is None else t0 + p0
                t1 = p1 if t1 is None else t1 + p1
                pcat = jnp.concatenate([p0, p1], axis=1)
                hi = pcat.astype(BF16)
                lo = (pcat - hi.astype(F32)).astype(BF16)
                vcat = _stack_heads(v_ref[_fox_key_rows(t, e, g), :], masks)
                acc = acc + jnp.dot(hi, vcat, preferred_element_type=F32) + jnp.dot(lo, vcat, preferred_element_type=F32)
            l0 = a0 * l0 + jnp.sum(t0, axis=1, keepdims=True)
            l1 = a1 * l1 + jnp.sum(t1, axis=1, keepdims=True)
            return n0, n1, l0, l1, acc

        def gbody(ng, c):
            neg = jnp.full((BLK, 1), NEG_INF, F32)
            z1 = jnp.zeros((BLK, 1), F32)
            rows = [pl.ds(pl.multiple_of((g * ng + a) * BLK, BLK), BLK) for a in range(g)]
            qcats = [_stack_heads(q_ref[rows[a], :] * QK_SCALE, masks) for a in range(g)]
            def step(t, cc):
                cur = [qk(qcats[a], t) for a in range(g)]
                return tuple(consume(cur[a], t, cc[a], g, False) for a in range(g))

            init = (neg, neg, z1, z1, jnp.zeros((BLK, LANES), F32))
            done = lax.fori_loop(0, ng, step, tuple(init for a in range(g)))
            last = [qk(qcats[a], ng, a + 1) for a in range(g)]
            for a in range(g):
                ss, state = last[a], done[a]
                m0, m1, l0, l1, acc = consume(ss, ng, state, a + 1, True)
                o_ref[rows[a], :] = acc / jnp.where(masks[0], l0, l1)
                lse_ref[rows[a], :] = jnp.where(masks[0], m0 + jnp.log(l0), m1 + jnp.log(l1))
            return c

        lax.fori_loop(0, NBLK // g, gbody, 0)

    cols, ospec, crspec = _fox_specs()
    osd = jax.ShapeDtypeStruct((nb, SEQ, FOX_W), F32)
    return pl.pallas_call(
        body, out_shape=(osd, osd), grid=(nb, FOX_W // LANES), in_specs=cols + [crspec], out_specs=(ospec, ospec),
        compiler_params=_cparams(dimension_semantics=("parallel", "parallel")), name=name,
    )(p3, p3, p3, crow)


def _fox_bwd(p3, crow, do, o, lse, *, do_off, name):
    nb = p3.shape[0]
    g = FOX_BWD_GROUP
    hd = HEAD_DIM

    def body(q_ref, k_ref, v_ref, cr_ref, do_ref, o_ref, lse_ref, dq_ref, dko_ref, dvo_ref, dcr_ref, dk_ref, dv_ref):
        masks = _head_masks(2)
        tri = _tri_bias(True)
        dk_ref[...] = jnp.zeros_like(dk_ref)
        dv_ref[...] = jnp.zeros_like(dv_ref)
        dcr_ref[...] = jnp.zeros_like(dcr_ref)

        def products(qcat, docat, t, nblk=g):
            out = []
            for e in range(0, nblk, 2):
                n = min(2, nblk - e)
                krows = pl.ds(pl.multiple_of((g * t + e) * BLK, BLK), n * BLK)
                s = lax.dot_general(qcat, k_ref[krows, :], _NT, preferred_element_type=F32)
                dp = lax.dot_general(docat, v_ref[krows, :], _NT, preferred_element_type=F32)
                for h in range(n):
                    out += [s[:, h * BLK:(h + 1) * BLK], dp[:, h * BLK:(h + 1) * BLK]]
            return tuple(out)

        def consume(prod, t, ctx, dq, nblk, diag):
            qcat, docat, lse0, lse1, dl0, dl1 = ctx
            for e in range(nblk):
                jb = g * t + e
                krows = _fox_key_rows(t, e, g)
                s, dp = prod[2 * e], prod[2 * e + 1]
                cr = cr_ref[jb]
                u0 = s[:BLK] - cr[0:1, :]
                u1 = s[BLK:] - cr[1:2, :]
                if diag and e == nblk - 1:
                    u0, u1 = u0 + tri, u1 + tri
                p0 = jnp.exp(u0 - lse0)
                p1 = jnp.exp(u1 - lse1)
                ds0 = p0 * (dp[:BLK] - dl0)
                ds1 = p1 * (dp[BLK:] - dl1)
                dcr_ref[jb, 0:1, :] += jnp.sum(ds0, axis=0, keepdims=True)
                dcr_ref[jb, 1:2, :] += jnp.sum(ds1, axis=0, keepdims=True)
                ds0b, ds1b = ds0.astype(BF16), ds1.astype(BF16)
                pcat = jnp.concatenate([p0.astype(BF16), p1.astype(BF16)], axis=0)
                dscat = jnp.concatenate([ds0b, ds1b], axis=0)
                dv_ref[krows, :] += lax.dot_general(pcat, docat, _T0, preferred_element_type=F32)
                dk_ref[krows, :] += lax.dot_general(dscat, qcat, _T0, preferred_element_type=F32)
                dsrow = jnp.concatenate([ds0b, ds1b], axis=1)
                dq = dq + jnp.dot(dsrow, _stack_heads(k_ref[krows, :] * QK_SCALE, masks), preferred_element_type=F32)
            return dq

        def gbody(ng, c):
            ctxs, rows = [], []
            for a in range(g):
                r = pl.ds(pl.multiple_of((g * ng + a) * BLK, BLK), BLK)
                qcat = _stack_heads(q_ref[r, :] * QK_SCALE, masks)
                dob = do_ref[r, :].astype(BF16)
                prod = dob.astype(F32) * o_ref[r, :]
                z = jnp.zeros_like(prod)
                dl0 = jnp.sum(jnp.where(masks[0], prod, z), axis=1, keepdims=True)
                dl1 = jnp.sum(jnp.where(masks[1], prod, z), axis=1, keepdims=True)
                lseb = lse_ref[r, :]
                ctxs.append((qcat, _stack_heads(dob, masks), lseb[:, 0:1], lseb[:, hd:hd + 1], dl0, dl1))
                rows.append(r)
            def step(t, cc):
                cur = [products(ctxs[a][0], ctxs[a][1], t) for a in range(g)]
                return tuple(consume(cur[a], t, ctxs[a], cc[a], g, False) for a in range(g))

            done = lax.fori_loop(0, ng, step, tuple(jnp.zeros((BLK, LANES), F32) for a in range(g)))
            last = [products(ctxs[a][0], ctxs[a][1], ng, a + 1) for a in range(g)]
            for a in range(g):
                dq_ref[rows[a], :] = consume(last[a], ng, ctxs[a], done[a], a + 1, True).astype(dq_ref.dtype)
            return c

        lax.fori_loop(0, NBLK // g, gbody, 0)
        dko_ref[...] = dk_ref[...].astype(dko_ref.dtype)
        dvo_ref[...] = dv_ref[...].astype(dvo_ref.dtype)

    cols, ospec, crspec = _fox_specs()
    dospec = pl.BlockSpec((None, SEQ, LANES), lambda b, j: (b, 0, do_off + j))
    osd = jax.ShapeDtypeStruct((nb, SEQ, FOX_W), BF16)
    return pl.pallas_call(
        body, out_shape=(osd, osd, osd, jax.ShapeDtypeStruct((nb, FOX_W // LANES, NBLK, 8, BLK), F32)),
        grid=(nb, FOX_W // LANES), in_specs=cols + [crspec, dospec, ospec, ospec], out_specs=(ospec, ospec, ospec, crspec),
        scratch_shapes=[pltpu.VMEM((SEQ, LANES), F32)] * 2,
        compiler_params=_cparams(dimension_semantics=("parallel", "parallel")), name=name,
    )(p3, p3, p3, crow, do, o, lse)


_B1, _B2 = FOX_W // LANES, (FOX_W + DIL_W) // LANES


def _dy_gate_bwd(dx2b, wo, fox, dil, memo, p16, *, tm, tn, name):
    t, d = dx2b.shape
    assert FOX_W % tn == 0 and DIL_W % tn == 0 and MEM_W % tn == 0 and all(c % tn == 0 for c in (C_FG, C_DG, C_MG))
    n1, n2, n3 = FOX_W // tn, (FOX_W + DIL_W) // tn, MIX_W // tn

    def body(dx_ref, w_ref, f_ref, d_ref, m_ref, g_ref, da_ref, dg_ref):
        j = pl.program_id(1)
        wv = w_ref[...]
        for c0 in range(0, tm, min(tm, 2 * MM_CHUNK)):
            rows = pl.ds(c0, min(tm, 2 * MM_CHUNK))
            dyv = lax.dot_general(dx_ref[rows, :], wv, _NT, preferred_element_type=F32)
            a = jnp.where(j < n1, f_ref[rows, :], jnp.where(j < n2, d_ref[rows, :], m_ref[rows, :]))
            gt = g_ref[rows, :].astype(F32)
            sg = 1.0 / (1.0 + jnp.exp(-gt))
            da_ref[rows, :] = (dyv * gt * sg).astype(da_ref.dtype)
            dg_ref[rows, :] = (dyv * a * sg * (1.0 + gt * (1.0 - sg))).astype(dg_ref.dtype)

    def gcol(j):
        return jnp.where(j < n1, C_FG // tn + j, jnp.where(j < n2, C_DG // tn + j - n1, C_MG // tn + j - n2))

    tile = pl.BlockSpec((tm, tn), lambda i, j: (i, j))
    return pl.pallas_call(
        body,
        out_shape=(jax.ShapeDtypeStruct((t, MIX_W), BF16), jax.ShapeDtypeStruct((t, MIX_W), BF16)),
        grid=(t // tm, n3),
        in_specs=[pl.BlockSpec((tm, d), lambda i, j: (i, 0)), pl.BlockSpec((tn, d), lambda i, j: (j, 0)),
                  pl.BlockSpec((tm, tn), lambda i, j: (i, jnp.minimum(j, n1 - 1))),
                  pl.BlockSpec((tm, tn), lambda i, j: (i, jnp.clip(j - n1, 0, n2 - n1 - 1))),
                  pl.BlockSpec((tm, tn), lambda i, j: (i, jnp.clip(j - n2, 0, n3 - n2 - 1))),
                  pl.BlockSpec((tm, tn), lambda i, j: (i, gcol(j)))],
        out_specs=(tile, tile),
        compiler_params=_cparams(dimension_semantics=("parallel", "parallel")),
        name=name,
    )(dx2b, wo, fox, dil, memo, p16)


def _silu(g):
    return g / (1.0 + jnp.exp(-g))


def _out_loss(fox, dil, memo, p16, wo, x, tgt, gfin, *, tm, name):
    t, d = x.shape
    n_feat = float(d)

    def body(f_ref, d_ref, m_ref, fg_ref, dg_ref, mg_ref, w_ref, x_ref, t_ref, g_ref, y_ref, dx_ref, dxb_ref, st_ref):
        i = pl.program_id(0)

        @pl.when(i == 0)
        def _():
            st_ref[...] = jnp.zeros_like(st_ref)

        wv, gv = w_ref[...], g_ref[...]
        half = tm // 2
        for c0 in (0, half):
            rows = pl.ds(c0, half)
            y = jnp.concatenate([(a_ref[rows, :] * _silu(gt_ref[rows, :].astype(F32))).astype(BF16)
                                 for a_ref, gt_ref in ((f_ref, fg_ref), (d_ref, dg_ref), (m_ref, mg_ref))], axis=1)
            y_ref[rows, :] = y
            x2 = x_ref[rows, :] + jnp.dot(y, wv, preferred_element_type=F32)
            r = lax.rsqrt(jnp.mean(x2 * x2, axis=-1, keepdims=True) + RMS_EPS)
            nrm = x2 * r
            err = nrm * gv - t_ref[rows, :]
            dout = err * (1.0 / n_feat)
            dn = dout * gv
            dx2 = r * (dn - nrm * jnp.mean(dn * nrm, axis=-1, keepdims=True))
            dx_ref[rows, :] = dx2
            dxb_ref[rows, :] = dx2.astype(dxb_ref.dtype)
            st_ref[0:1, :] += jnp.sum(dout * nrm, axis=0, keepdims=True)
            st_ref[1:2, :] += (0.5 / n_feat) * jnp.sum(err * err, axis=0, keepdims=True)

    row = pl.BlockSpec((tm, d), lambda i: (i, 0))
    whole = lambda w: pl.BlockSpec((tm, w), lambda i: (i, 0))
    gate = lambda w, col: pl.BlockSpec((tm, w), lambda i: (i, col // w))
    return pl.pallas_call(
        body,
        out_shape=(jax.ShapeDtypeStruct((t, MIX_W), BF16), jax.ShapeDtypeStruct((t, d), F32), jax.ShapeDtypeStruct((t, d), BF16),
                   jax.ShapeDtypeStruct((8, d), F32)),
        grid=(t // tm,),
        in_specs=[whole(FOX_W), whole(DIL_W), whole(MEM_W), gate(FOX_W, C_FG), gate(DIL_W, C_DG), gate(MEM_W, C_MG),
                  pl.BlockSpec((MIX_W, d), lambda i: (0, 0)), row, row, pl.BlockSpec((1, d), lambda i: (0, 0))],
        out_specs=(pl.BlockSpec((tm, MIX_W), lambda i: (i, 0)), row, row, pl.BlockSpec((8, d), lambda i: (0, 0))),
        compiler_params=_cparams(dimension_semantics=("arbitrary",)),
        name=name,
    )(fox, dil, memo, p16, p16, p16, wo, x, tgt, gfin)


def _dh_rms_bwd(dp, w, x, g, resid, *, tm, name):
    t, d = x.shape
    kdim = dp.shape[1]

    def body(*refs):
        if resid is not None:
            dp_ref, w_ref, x_ref, g_ref, r_ref, dx_ref, gg_ref = refs
        else:
            dp_ref, w_ref, x_ref, g_ref, dx_ref, gg_ref = refs

        @pl.when(pl.program_id(0) == 0)
        def _():
            gg_ref[...] = jnp.zeros_like(gg_ref)

        dh = lax.dot_general(dp_ref[...], w_ref[...], _NT, preferred_element_type=F32)
        xv = x_ref[...]
        r = lax.rsqrt(jnp.mean(xv * xv, axis=-1, keepdims=True) + RMS_EPS)
        nrm = xv * r
        dn = dh * g_ref[...]
        dx = r * (dn - nrm * jnp.mean(dn * nrm, axis=-1, keepdims=True))
        if resid is not None:
            dx = dx + r_ref[...]
        dx_ref[...] = dx
        gg_ref[0:1, :] += jnp.sum(dh * nrm, axis=0, keepdims=True)

    row = pl.BlockSpec((tm, d), lambda i: (i, 0))
    in_specs = [pl.BlockSpec((tm, kdim), lambda i: (i, 0)),
                pl.BlockSpec((d, kdim), lambda i: (0, 0), pipeline_mode=pl.Buffered(1)), row,
                pl.BlockSpec((1, d), lambda i: (0, 0))]
    args = [dp, w, x, g]
    if resid is not None:
        in_specs.append(row)
        args.append(resid)
    return pl.pallas_call(
        body,
        out_shape=(jax.ShapeDtypeStruct((t, d), F32), jax.ShapeDtypeStruct((8, d), F32)),
        grid=(t // tm,),
        in_specs=in_specs,
        out_specs=(row, pl.BlockSpec((8, d), lambda i: (0, 0))),
        compiler_params=_cparams(dimension_semantics=("arbitrary",)),
        name=name,
    )(*args)


_FLOG0 = 4 * FOX_W
_W_IN_SEGMENTS = ((0, _FLOG0, 0), (_FLOG0, _FLOG0 + FOX_HEADS, PW), (_FLOG0 + FOX_HEADS, IN_W, C_DQ))
SHARD_W = IN_W // N_CHIPS


def _rearrange_w_in(shards):
    def cols(lo, hi):
        parts = []
        for k in range(N_CHIPS):
            a, b = max(lo, k * SHARD_W), min(hi, (k + 1) * SHARD_W)
            if a < b:
                parts.append(shards[k][:, a - k * SHARD_W:b - k * SHARD_W])
        return parts

    (a0, a1, _), (f0, f1, _), (b0, b1, _) = _W_IN_SEGMENTS
    pad = jnp.zeros((shards[0].shape[0], PWF - PW - FOX_HEADS), shards[0].dtype)
    return jnp.concatenate(cols(a0, a1) + cols(b0, b1) + cols(f0, f1) + [pad], axis=1)


def _w_in_grad_slabs(g):
    slabs = []
    for k in range(N_CHIPS):
        parts = []
        for lo, hi, at in _W_IN_SEGMENTS:
            a, b = max(lo, k * SHARD_W), min(hi, (k + 1) * SHARD_W)
            if a < b:
                parts.append(g[:, at + a - lo:at + b - lo])
        slabs.append(jnp.concatenate(parts, axis=1))
    return jnp.stack(slabs, axis=0)


def _local_grads(x, mem, norm_g, w_r, b_forget, mem_norm_g, w_kv, w_o, final_norm_g, tgt, start_reduce=None,
                 start_reduce_small=None, early_token=None, late_weights=None):
    nb = x.shape[0]
    t = nb * SEQ
    x2d = x.reshape(t, D_MODEL)
    tgt2d = tgt.reshape(t, D_MODEL)
    tabs = _rope_tables()
    bpad = jnp.pad(b_forget.reshape(1, FOX_HEADS), ((0, 0), (0, LANES - FOX_HEADS)))

    gain0 = norm_g.reshape(1, D_MODEL)
    if early_token is not None:
        gain0 = gain0 + early_token[0:1, 0:1]
    h, p16, dqkv, flog = _proj(x2d, gain0, w_r, tabs, n=PWF, tm=1024, tn=768, name="proj")
    c12 = _flog_fwd(flog, bpad, nb=nb, ts=256, name="flog_fwd")

    crow = c12[:, :FOX_HEADS].reshape(nb, NBLK, BLK, FOX_HEADS // 2, 2).transpose(0, 3, 1, 4, 2)
    crow = jnp.pad(crow, ((0, 0), (0, 0), (0, 0), (0, 6), (0, 0)))
    p3 = p16.reshape(nb, SEQ, PWF)
    fox, fox_lse = _fox_fwd(p3, crow, name="fox_fwd")
    if late_weights is not None:
        w_kv, w_o = late_weights(fox_lse)

    dqkv3 = dqkv.reshape(nb, SEQ, 3 * DIL_W)
    dil, dil_lse = _dil_fwd(dqkv3, name="dil_fwd")

    mh = _rms_fwd(mem.reshape(nb * MEM_LEN, D_MODEL), mem_norm_g.reshape(1, D_MODEL), tm=nb * MEM_LEN, name="rms_mem")
    mkv = _matmul(mh, w_kv, out_dtype=BF16, tm=nb * MEM_LEN, tn=512, tk=D_MODEL, name="mem_kv")
    mkv3 = mkv.reshape(nb, MEM_LEN, 2 * MEM_W)
    memo, mem_lse = _mem_fwd(p3, mkv3, qoff=C_MQ // LANES, name="mem_fwd")

    fox2, dil2, memo2 = fox.reshape(t, FOX_W), dil.reshape(t, DIL_W), memo.reshape(t, MEM_W)
    y, dx2, dx2b, st = _out_loss(fox2, dil2, memo2, p16, w_o, x2d, tgt2d, final_norm_g.reshape(1, D_MODEL), tm=256,
                                 name="out_loss")

    g_wo = _matmul(y, dx2b, mode="tn", out_dtype=BF16, tm=1024, tn=512, tk=t, name="grad_w_out")
    datt, dgate = _dy_gate_bwd(dx2b, w_o, fox2, dil2, memo2, p16, tm=2048, tn=256, name="dy_gate_bwd")
    datt3 = datt.reshape(nb, SEQ, MIX_W)

    dmq, dmk, dmv = _mem_bwd(p3, mkv3, datt3, memo, mem_lse, qoff=C_MQ // LANES, do_off=_B2, name="mem_bwd")
    dmkv = jnp.concatenate([dmk, dmv], axis=-1).reshape(nb * MEM_LEN, 2 * MEM_W).astype(BF16)
    g_wkv = _matmul(mh, dmkv, mode="tn", out_dtype=BF16, tm=512, tn=512, tk=nb * MEM_LEN, name="grad_w_kv")
    mem_gain = mem_norm_g.reshape(1, D_MODEL)
    if start_reduce_small is not None:
        tok = start_reduce_small(g_wkv, g_wo)[0:1, 0:1]
        mem_gain, crow = mem_gain + tok, crow + tok
    _, gmn = _dh_rms_bwd(dmkv, w_kv, mem.reshape(nb * MEM_LEN, D_MODEL), mem_gain, None, tm=nb * MEM_LEN, name="mem_rms_bwd")

    dfq, dfk, dfv, dcr = _fox_bwd(p3, crow, datt3, fox, fox_lse, do_off=0, name="fox_bwd")
    dcol = -dcr[:, :, :, :2, :].transpose(0, 2, 4, 1, 3).reshape(t, FOX_HEADS)
    dcol = jnp.pad(dcol, ((0, 0), (0, LANES - FOX_HEADS)))
    dflog, gb = _flog_bwd(dcol, flog, bpad, nb=nb, ts=256, name="flog_bwd")

    ddq, ddk, ddv = _dil_bwd(dqkv3, datt3, dil, dil_lse, tabs, do_off=_B1, name="dil_bwd")

    flat = lambda a: a.reshape(t, -1)
    dp = jnp.concatenate([flat(dfq), flat(dfk), flat(dfv), dgate[:, :FOX_W], flat(ddq), flat(ddk), flat(ddv),
                          dgate[:, FOX_W:FOX_W + DIL_W], flat(dmq), dgate[:, FOX_W + DIL_W:], dflog,
                          jnp.zeros((t, PWF - PW - LANES), BF16)], axis=1)
    g_wr = _matmul(h, dp, mode="tn", out_dtype=BF16, tm=D_MODEL, tn=768, tk=t, name="grad_w_in")
    gain = norm_g.reshape(1, D_MODEL)
    if start_reduce is not None:
        gain = gain + start_reduce(g_wr)[0:1, 0:1]
    gx, gng = _dh_rms_bwd(dp, w_r, x2d, gain, dx2, tm=256, name="in_rms_bwd")

    gb_row = jnp.pad(gb[0:1, :], ((0, 0), (0, D_MODEL - LANES)))
    small = jnp.concatenate([gng[0:1], gmn[0:1], st[0:1], gb_row, st[1:2], jnp.zeros((3, D_MODEL), F32)], axis=0)
    return gx.reshape(nb, SEQ, D_MODEL), g_wr, g_wkv, g_wo, small


MESH = pl.DeviceIdType.MESH
ANY = pl.BlockSpec(memory_space=pl.ANY)


def _place():
    x, y, c = lax.axis_index("x"), lax.axis_index("y"), lax.axis_index("c")
    other_chips = [(1 - x, y), (x, 1 - y), (1 - x, 1 - y)]
    return x, y, c, other_chips


def _gather_weights(shards):
    n = len(shards)

    def body(*refs):
        in_refs, out_refs = refs[:n], refs[n:2 * n]
        send_sems, recv_sems = refs[2 * n:]
        x, y, c, chips = _place()
        me_chip = 2 * x + y
        sibling = (x, y, 1 - c)

        def half(ref, pc, rows):
            return ref.at[pl.ds(pc * (rows // 2), rows // 2), :]

        def rcopy(k, src, dst, to):
            return pltpu.make_async_remote_copy(src_ref=src, dst_ref=dst, send_sem=send_sems.at[k], recv_sem=recv_sems.at[k],
                                                device_id=to, device_id_type=MESH)

        sends = []
        for t in range(n):
            rows = shards[t].shape[0]
            for j, chip in enumerate(chips):
                cp = rcopy(6 * t + j, half(in_refs[t], c, rows), half(out_refs[t].at[me_chip], c, rows), (*chip, c))
                cp.start()
                sends.append(cp)
        for t in range(n):
            rows = shards[t].shape[0]
            for j, chip in enumerate(chips):
                slot = out_refs[t].at[2 * chip[0] + chip[1]]
                rcopy(6 * t + j, half(slot, c, rows), half(slot, c, rows), sibling).wait_recv()
                fw = rcopy(6 * t + 3 + j, half(slot, c, rows), half(slot, c, rows), sibling)
                fw.start()
                sends.append(fw)
        for t in range(n):
            rows = shards[t].shape[0]
            for j, chip in enumerate(chips):
                slot = out_refs[t].at[2 * chip[0] + chip[1]]
                rcopy(6 * t + 3 + j, half(slot, 1 - c, rows), half(slot, 1 - c, rows), sibling).wait_recv()
        for cp in sends:
            cp.wait_send()

    return pl.pallas_call(
        body,
        out_shape=tuple(jax.ShapeDtypeStruct((N_CHIPS,) + s.shape, s.dtype) for s in shards),
        in_specs=[ANY] * n,
        out_specs=tuple([ANY] * n),
        scratch_shapes=[pltpu.SemaphoreType.DMA((6 * n,)), pltpu.SemaphoreType.DMA((6 * n,))],
        name="gather_weights",
    )(*shards)


def _pair_exchange(gs, *, name):
    n = len(gs)

    def body(*refs):
        g_refs, r_refs = refs[:n], refs[n:2 * n]
        send_sems, recv_sems = refs[2 * n:]
        x, y, c, _ = _place()
        cps = []
        for t in range(n):
            hr = gs[t].shape[1] // 2
            cp = pltpu.make_async_remote_copy(src_ref=g_refs[t].at[:, pl.ds((1 - c) * hr, hr), :], dst_ref=r_refs[t],
                                              send_sem=send_sems.at[t], recv_sem=recv_sems.at[t],
                                              device_id=(x, y, 1 - c), device_id_type=MESH)
            cp.start()
            cps.append(cp)
        for cp in cps:
            cp.wait()

    return pl.pallas_call(
        body,
        out_shape=tuple(jax.ShapeDtypeStruct((g.shape[0], g.shape[1] // 2, g.shape[2]), g.dtype) for g in gs),
        in_specs=[ANY] * n,
        out_specs=tuple([ANY] * n),
        scratch_shapes=[pltpu.SemaphoreType.DMA((n,)), pltpu.SemaphoreType.DMA((n,))],
        name=name,
    )(*gs)


_HBM = pl.BlockSpec(memory_space=pltpu.HBM)
_SEM = pl.BlockSpec(memory_space=pltpu.SEMAPHORE)
_DATAFLOW = pltpu.SideEffectType.DATAFLOW_SIDE_EFFECTING


def _chip_copies(p_refs, land_refs, send_sems, recv_sems):
    x, y, c, chips = _place()
    me_chip = 2 * x + y
    return [pltpu.make_async_remote_copy(src_ref=p_refs[t].at[2 * chip[0] + chip[1]], dst_ref=land_refs[t].at[me_chip],
                                         send_sem=send_sems.at[3 * t + j], recv_sem=recv_sems.at[3 * t + j],
                                         device_id=(*chip, c), device_id_type=MESH)
            for t in range(len(p_refs)) for j, chip in enumerate(chips)]


def _chip_exchange_start(ps, *, tag):
    n = len(ps)

    def body(*refs):
        p_refs, land_refs = refs[:n], refs[n:2 * n]
        send_sems, recv_sems = refs[2 * n:2 * n + 2]
        token = refs[-1]
        for cp in _chip_copies(p_refs, land_refs, send_sems, recv_sems):
            cp.start()
        token[...] = jnp.zeros_like(token)

    hbm = [pltpu.HBM(p.shape, p.dtype) for p in ps]
    args = [pltpu.with_memory_space_constraint(p, pltpu.HBM) for p in ps]
    args += [pltpu.with_memory_space_constraint(lax.empty(p.shape, p.dtype), pltpu.HBM) for p in ps]
    out = pl.pallas_call(
        body,
        name=f"chip_exchange_start_{tag}",
        out_shape=(pltpu.SemaphoreType.DMA((3 * n,)), pltpu.SemaphoreType.DMA((3 * n,)), *hbm, *hbm,
                   jax.ShapeDtypeStruct((8, LANES), F32)),
        in_specs=[_HBM] * (2 * n),
        out_specs=(_SEM, _SEM, *([_HBM] * (2 * n)), pl.BlockSpec(memory_space=pltpu.VMEM)),
        input_output_aliases={i: 2 + i for i in range(2 * n)},
        compiler_params=pltpu.CompilerParams(has_side_effects=_DATAFLOW),
    )(*args)
    return out[0], out[1], out[2:2 + n], out[2 + n:2 + 2 * n], out[-1]


def _chip_exchange_wait(send_sems, recv_sems, p_thru, land_thru, after, *, tag):
    n = len(p_thru)

    def body(*refs):
        p_refs, land_refs = refs[:n], refs[n:2 * n]
        ssem, rsem = refs[2 * n:2 * n + 2]
        for cp in _chip_copies(p_refs, land_refs, ssem, rsem):
            cp.wait_send()
            cp.wait_recv()

    hbm = [pltpu.HBM(p.shape, p.dtype) for p in p_thru]
    out = pl.pallas_call(
        body,
        name=f"chip_exchange_wait_{tag}",
        out_shape=(*hbm, *hbm),
        in_specs=[_HBM] * (2 * n) + [_SEM, _SEM, ANY],
        out_specs=tuple([_HBM] * (2 * n)),
        input_output_aliases={i: i for i in range(2 * n)},
        compiler_params=pltpu.CompilerParams(has_side_effects=_DATAFLOW),
    )(*p_thru, *land_thru, send_sems, recv_sems, after)
    return out[:n], out[n:]


def _shard_copies(s_refs, land_refs, send_sems, recv_sems):
    x, y, c, chips = _place()
    me_chip = 2 * x + y
    return [pltpu.make_async_remote_copy(src_ref=s_refs[t], dst_ref=land_refs[t].at[me_chip],
                                         send_sem=send_sems.at[3 * t + j], recv_sem=recv_sems.at[3 * t + j],
                                         device_id=(*chip, c), device_id_type=MESH)
            for t in range(len(s_refs)) for j, chip in enumerate(chips)]


def _gather_late_start(shards):
    n = len(shards)

    def body(*refs):
        s_refs, land_refs = refs[:n], refs[n:2 * n]
        send_sems, recv_sems = refs[2 * n:2 * n + 2]
        token = refs[-1]
        for cp in _shard_copies(s_refs, land_refs, send_sems, recv_sems):
            cp.start()
        token[...] = jnp.zeros_like(token)

    lands = [(N_CHIPS,) + s.shape for s in shards]
    args = [pltpu.with_memory_space_constraint(s, pltpu.HBM) for s in shards]
    args += [pltpu.with_memory_space_constraint(lax.empty(shp, s.dtype), pltpu.HBM) for shp, s in zip(lands, shards)]
    out = pl.pallas_call(
        body,
        name="gather_late_start",
        out_shape=(pltpu.SemaphoreType.DMA((3 * n,)), pltpu.SemaphoreType.DMA((3 * n,)),
                   *[pltpu.HBM(s.shape, s.dtype) for s in shards], *[pltpu.HBM(shp, s.dtype) for shp, s in zip(lands, shards)],
                   jax.ShapeDtypeStruct((8, LANES), F32)),
        in_specs=[_HBM] * (2 * n),
        out_specs=(_SEM, _SEM, *([_HBM] * (2 * n)), pl.BlockSpec(memory_space=pltpu.VMEM)),
        input_output_aliases={i: 2 + i for i in range(2 * n)},
        compiler_params=pltpu.CompilerParams(has_side_effects=_DATAFLOW),
    )(*args)
    return out[0], out[1], out[2:2 + n], out[2 + n:2 + 2 * n], out[-1]


def _gather_late_wait(send_sems, recv_sems, s_thru, land_thru, after):
    n = len(s_thru)

    def body(*refs):
        s_refs, land_refs = refs[:n], refs[n:2 * n]
        ssem, rsem = refs[2 * n:2 * n + 2]
        for cp in _shard_copies(s_refs, land_refs, ssem, rsem):
            cp.wait_send()
            cp.wait_recv()

    out = pl.pallas_call(
        body,
        name="gather_late_wait",
        out_shape=(*[pltpu.HBM(s.shape, s.dtype) for s in s_thru], *[pltpu.HBM(l.shape, l.dtype) for l in land_thru]),
        in_specs=[_HBM] * (2 * n) + [_SEM, _SEM, ANY],
        out_specs=tuple([_HBM] * (2 * n)),
        input_output_aliases={i: i for i in range(2 * n)},
        compiler_params=pltpu.CompilerParams(has_side_effects=_DATAFLOW),
    )(*s_thru, *land_thru, send_sems, recv_sems, after)
    return out[:n], out[n:]


def _pair_swap(rs):
    n = len(rs)

    def body(*refs):
        r_refs, o_refs = refs[:n], refs[n:2 * n]
        send_sems, recv_sems = refs[2 * n:]
        x, y, c, _ = _place()
        cps = []
        for t in range(n):
            cp = pltpu.make_async_remote_copy(src_ref=r_refs[t], dst_ref=o_refs[t], send_sem=send_sems.at[t],
                                              recv_sem=recv_sems.at[t], device_id=(x, y, 1 - c), device_id_type=MESH)
            cp.start()
            cps.append(cp)
        for cp in cps:
            cp.wait()

    return pl.pallas_call(
        body,
        out_shape=tuple(jax.ShapeDtypeStruct(r.shape, r.dtype) for r in rs),
        in_specs=[ANY] * n,
        out_specs=tuple([ANY] * n),
        scratch_shapes=[pltpu.SemaphoreType.DMA((n,)), pltpu.SemaphoreType.DMA((n,))],
        name="pair_swap",
    )(*rs)


N_DEV = 8
LOSS_ROW = 4


def _small_allreduce(small):
    def body(s_ref, o_ref, all_ref, send_sems, recv_sems):
        x, y, c, _ = _place()
        me = 4 * x + 2 * y + c
        all_ref[me] = s_ref[...]
        cps = []
        for k in range(1, N_DEV):
            peer = tuple(1 - p if (k >> s) & 1 else p for p, s in ((x, 2), (y, 1), (c, 0)))
            cp = pltpu.make_async_remote_copy(src_ref=s_ref, dst_ref=all_ref.at[me], send_sem=send_sems.at[k - 1],
                                              recv_sem=recv_sems.at[k - 1], device_id=peer, device_id_type=MESH)
            cp.start()
            cps.append(cp)
        for cp in cps:
            cp.wait()
        tot = all_ref[0]
        for d in range(1, N_DEV):
            tot = tot + all_ref[d]
        o_ref[...] = tot
        o_ref[LOSS_ROW:LOSS_ROW + 1, :] = jnp.broadcast_to(jnp.sum(tot[LOSS_ROW:LOSS_ROW + 1, :], axis=1, keepdims=True),
                                                          (1, tot.shape[1]))

    vm = pl.BlockSpec(memory_space=pltpu.VMEM)
    return pl.pallas_call(
        body,
        out_shape=jax.ShapeDtypeStruct(small.shape, small.dtype),
        in_specs=[vm],
        out_specs=vm,
        scratch_shapes=[pltpu.VMEM((N_DEV,) + small.shape, small.dtype), pltpu.SemaphoreType.DMA((N_DEV - 1,)),
                        pltpu.SemaphoreType.DMA((N_DEV - 1,))],
        name="small_allreduce",
    )(small)


def _sum_pair(g, recv, cidx, *, tr, name):
    n, hr, cols = recv.shape
    nr = hr // tr

    def body(c_ref, g_ref, r_ref, o_ref):
        o_ref[...] = (g_ref[...].astype(F32) + r_ref[...].astype(F32)).astype(o_ref.dtype)

    grid_spec = pltpu.PrefetchScalarGridSpec(
        num_scalar_prefetch=1,
        grid=(n, nr),
        in_specs=[pl.BlockSpec((None, tr, cols), lambda k, i, c_ref: (k, c_ref[0] * nr + i, 0)),
                  pl.BlockSpec((None, tr, cols), lambda k, i, c_ref: (k, i, 0))],
        out_specs=pl.BlockSpec((None, tr, cols), lambda k, i, c_ref: (k, i, 0)),
    )
    return pl.pallas_call(body, out_shape=jax.ShapeDtypeStruct(recv.shape, BF16), grid_spec=grid_spec,
                          compiler_params=_cparams(), name=name)(cidx, g, recv)


def _sum_chips(p, *, tr, name):
    _, rows, cols = p.shape

    def body(p_ref, o_ref):
        tot = p_ref[0].astype(F32)
        for k in range(1, N_CHIPS):
            tot = tot + p_ref[k].astype(F32)
        o_ref[...] = tot

    return pl.pallas_call(
        body,
        out_shape=jax.ShapeDtypeStruct((rows, cols), F32),
        grid=(rows // tr,),
        in_specs=[pl.BlockSpec((N_CHIPS, tr, cols), lambda i: (0, i, 0))],
        out_specs=pl.BlockSpec((tr, cols), lambda i: (i, 0)),
        compiler_params=_cparams(),
        name=name,
    )(p)


def _adamw(w, g, m, v, *, tr, name):
    rows, cols = w.shape
    bc1 = 1.0 / (1.0 - ADAM_B1 ** ADAM_STEP)
    bc2 = 1.0 / (1.0 - ADAM_B2 ** ADAM_STEP)

    def body(w_ref, g_ref, m_ref, v_ref, d_ref, nm_ref, nv_ref):
        gv = g_ref[...]
        nm = ADAM_B1 * m_ref[...] + (1.0 - ADAM_B1) * gv
        nv = ADAM_B2 * v_ref[...] + (1.0 - ADAM_B2) * (gv * gv)
        d_ref[...] = -ADAM_LR * ((nm * bc1) / (jnp.sqrt(nv * bc2) + ADAM_EPS) + ADAM_WD * w_ref[...])
        nm_ref[...] = nm
        nv_ref[...] = nv

    spec = pl.BlockSpec((tr, cols), lambda i: (i, 0))
    sd = jax.ShapeDtypeStruct((rows, cols), F32)
    return pl.pallas_call(body, out_shape=(sd, sd, sd), grid=(rows // tr,), in_specs=[spec] * 4, out_specs=(spec,) * 3,
                          compiler_params=_cparams(), name=name)(w, g, m, v)


def _adamw_halves(w, own, sib, cidx, m, v, *, tr, name):
    rows, cols = w.shape
    hr = own.shape[0]
    nr = hr // tr
    assert rows == 2 * hr and hr % tr == 0
    bc1 = 1.0 / (1.0 - ADAM_B1 ** ADAM_STEP)
    bc2 = 1.0 / (1.0 - ADAM_B2 ** ADAM_STEP)

    def body(c_ref, w_ref, o_ref, s_ref, m_ref, v_ref, g_ref, d_ref, nm_ref, nv_ref):
        mine = (pl.program_id(0) // nr) == c_ref[0]
        gv = jnp.where(mine, o_ref[...], s_ref[...])
        nm = ADAM_B1 * m_ref[...] + (1.0 - ADAM_B1) * gv
        nv = ADAM_B2 * v_ref[...] + (1.0 - ADAM_B2) * (gv * gv)
        g_ref[...] = gv
        d_ref[...] = -ADAM_LR * ((nm * bc1) / (jnp.sqrt(nv * bc2) + ADAM_EPS) + ADAM_WD * w_ref[...])
        nm_ref[...] = nm
        nv_ref[...] = nv

    full = pl.BlockSpec((tr, cols), lambda i, c_ref: (i, 0))
    half = pl.BlockSpec((tr, cols), lambda i, c_ref: (i % nr, 0))
    sd = jax.ShapeDtypeStruct((rows, cols), F32)
    grid_spec = pltpu.PrefetchScalarGridSpec(num_scalar_prefetch=1, grid=(rows // tr,), in_specs=[full, half, half, full, full],
                                             out_specs=(full,) * 4)
    return pl.pallas_call(body, out_shape=(sd,) * 4, grid_spec=grid_spec, compiler_params=_cparams(), name=name)(
        cidx, w, own, sib, m, v)


def _pack_small(norm, mem_norm, final_norm, b_forget):
    rows = [norm.reshape(1, D_MODEL), mem_norm.reshape(1, D_MODEL), final_norm.reshape(1, D_MODEL),
            jnp.pad(b_forget.reshape(1, FOX_HEADS), ((0, 0), (0, D_MODEL - FOX_HEADS))), jnp.zeros((4, D_MODEL), F32)]
    return jnp.concatenate(rows, axis=0)


def _unpack_small(a):
    return a[0:1], a[3:4, :FOX_HEADS], a[1:2], a[2]


def kernel(x, mem, norm_g, w_in, b_forget, mem_norm_g, w_mem_kv, w_out, final_norm_g, loss_target, m_norm_g, m_w_in, m_b_forget, m_mem_norm_g, m_w_mem_kv, m_w_out, m_final_norm_g, v_norm_g, v_w_in, v_b_forget, v_mem_norm_g, v_w_mem_kv, v_w_out, v_final_norm_g):
    core = lax.axis_index("c").astype(jnp.int32)
    me_chip = (2 * lax.axis_index("x") + lax.axis_index("y")).astype(jnp.int32)
    cidx = core.reshape(1)

    def own_slot(arr, own):
        return lax.dynamic_update_slice(arr, own[None].astype(arr.dtype), (me_chip,) + (0,) * own.ndim)

    win_b, late = w_in[0].astype(BF16), [w_mem_kv[0].astype(BF16), w_out[0].astype(BF16)]
    g_in, = _gather_weights([win_b])
    g_in, late = lax.optimization_barrier((own_slot(g_in, win_b), late))
    w_r = _rearrange_w_in([g_in[k] for k in range(N_CHIPS)])
    *late_flight, early_token = _gather_late_start(late)

    def late_weights(after):
        shards, landed = _gather_late_wait(*late_flight, after)
        g_kv, g_out = (own_slot(g, s) for g, s in zip(landed, shards))
        return g_kv.reshape(D_MODEL, 2 * MEM_W), g_out.reshape(MIX_W, D_MODEL)

    trs = (128, 128, 256)
    names = ("w_in", "w_mem_kv", "w_out")
    flights = {}

    def exchange(slabs, nms, ts, tag):
        recv = _pair_exchange(slabs, name=f"pair_exchange_{tag}")
        pair = [_sum_pair(g, r, cidx, tr=tr, name=f"sum_pair_{nm}") for g, r, tr, nm in zip(slabs, recv, ts, nms)]
        if tag == "w_in":
            pair[0] = _w_in_grad_slabs(pair[0][0])
        *flights[tag], token = _chip_exchange_start(pair, tag=tag)
        return token

    def start_reduce_small(g_wkv, g_wo):
        slabs = [g_wkv.reshape(N_CHIPS, D_MODEL // N_CHIPS, 2 * MEM_W), g_wo.reshape(N_CHIPS, MIX_W // N_CHIPS, D_MODEL)]
        return exchange(slabs, names[1:], trs[1:], "small")

    def start_reduce(g_wr):
        return exchange([g_wr[None]], names[:1], trs[:1], "w_in")

    gx, g_wr, g_wkv, g_wo, small = _local_grads(x, mem, norm_g, w_r, b_forget, mem_norm_g, None, None, final_norm_g, loss_target,
                                                start_reduce=start_reduce, start_reduce_small=start_reduce_small,
                                                early_token=early_token, late_weights=late_weights)

    pair, landed = [], []
    for tag in ("w_in", "small"):
        p, l = _chip_exchange_wait(*flights[tag], small, tag=tag)
        pair += list(p)
        landed += list(l)
    got = [lax.dynamic_update_slice(g, lax.dynamic_slice(p, (me_chip, 0, 0), (1,) + p.shape[1:]), (me_chip, 0, 0))
           for g, p in zip(landed, pair)]
    red = [_sum_chips(p, tr=tr, name=f"sum_chips_{nm}") for p, tr, nm in zip(got, trs, names)]
    sib = _pair_swap(red)

    outs = {}
    for nm, r, s, w, m, v, tr in zip(names, red, sib, (w_in, w_mem_kv, w_out), (m_w_in, m_w_mem_kv, m_w_out),
                                     (v_w_in, v_w_mem_kv, v_w_out), trs):
        outs[nm] = tuple(a[None] for a in _adamw_halves(w[0], r, s, cidx, m[0], v[0], tr=tr, name=f"adamw_{nm}"))

    gsum = _small_allreduce(small)
    sd, sm, sv = _adamw(_pack_small(norm_g, mem_norm_g, final_norm_g, b_forget), gsum,
                        _pack_small(m_norm_g, m_mem_norm_g, m_final_norm_g, m_b_forget),
                        _pack_small(v_norm_g, v_mem_norm_g, v_final_norm_g, v_b_forget), tr=8, name="adamw_small")
    loss = gsum[LOSS_ROW, 0]

    def group(i, small_arr):
        ng, bf, mg, fg = _unpack_small(small_arr)
        return (ng, outs["w_in"][i], bf, mg, outs["w_mem_kv"][i], outs["w_out"][i], fg)

    return (loss, gx, *group(0, gsum), *group(1, sd), *group(2, sm), *group(3, sv))
```
